```python
import jax, jax.numpy as jnp
from jax import lax
import numpy as np

D_MODEL = 1024
BATCH = 8
SEQ = 4096
DEPTH = 2

N_EVEN = (DEPTH + 1) // 2
N_ODD = DEPTH // 2
NORM_EPS = 1e-6

RG_WIDTH = D_MODEL
RG_HEADS = 8
RG_HEAD_DIM = RG_WIDTH // RG_HEADS
RG_CONV = 4
RG_C = 8.0
RG_CONV_LEFT = 2
SC_WIDTH = D_MODEL
SC_CONV = 3
SC_CONV_LEFT = 1
EVEN_IN = 2 * RG_WIDTH + 4 * SC_WIDTH
EVEN_MIX = RG_WIDTH + SC_WIDTH
GLA_HEADS = 4
GLA_KEY = D_MODEL // 2
GLA_VAL = D_MODEL
GLA_DK = GLA_KEY // GLA_HEADS
GLA_DV = GLA_VAL // GLA_HEADS
GLA_RANK = 16
GLA_NORMALIZER = 16.0
GLA_CHUNK = 64
ODD_IN = 2 * GLA_KEY + 2 * GLA_VAL + 2 * GLA_RANK

kernel_name = 'hybrid_rglru_shortconv_gla_encoder'


def rmsnorm(x, g):
    xf = x.astype(jnp.float32)
    y = xf * lax.rsqrt(jnp.mean(xf * xf, axis=-1, keepdims=True) + NORM_EPS)
    return (y * g.astype(jnp.float32)).astype(x.dtype)


def centred_dwconv(u, w, left):
    seq = u.shape[1]
    width = w.shape[0]
    up = jnp.pad(u, ((0, 0), (left, width - 1 - left), (0, 0)))
    out = up[:, 0:seq] * w[0]
    for k in range(1, width):
        out = out + up[:, k:k + seq] * w[k]
    return out


def rg_lru(u, gate_w, gate_b, lam, reverse):
    bsz, seq, width = u.shape
    uf = u.astype(jnp.float32)
    uh = uf.reshape(bsz, seq, RG_HEADS, RG_HEAD_DIM)
    gates = jax.nn.sigmoid(
        jnp.einsum('bshi,ghij->gbshj', uh, gate_w.astype(jnp.float32))
        + gate_b[:, None, None].astype(jnp.float32))
    r = gates[0].reshape(bsz, seq, width)
    i = gates[1].reshape(bsz, seq, width)
    log_a = -RG_C * r * jax.nn.softplus(-lam.astype(jnp.float32))
    a = jnp.exp(log_a)
    b = jnp.sqrt(-jnp.expm1(2.0 * log_a)) * (i * uf)

    def combine(left, right):
        a_l, b_l = left
        a_r, b_r = right
        return (a_l * a_r, a_r * b_l + b_r)

    _, h = lax.associative_scan(combine, (a, b), reverse=reverse, axis=1)
    return h


def gla_chunked(q, k, v, g):
    bsz, nh, seq, dk = q.shape
    dv = v.shape[-1]
    nc = seq // GLA_CHUNK
    qc = q.astype(jnp.float32).reshape(bsz, nh, nc, GLA_CHUNK, dk) * (dk ** -0.5)
    kc = k.astype(jnp.float32).reshape(bsz, nh, nc, GLA_CHUNK, dk)
    vc = v.astype(jnp.float32).reshape(bsz, nh, nc, GLA_CHUNK, dv)
    gc = g.reshape(bsz, nh, nc, GLA_CHUNK, dk)
    bcum = jnp.cumsum(gc, axis=3)
    btot = bcum[..., -1:, :]
    q_in = qc * jnp.exp(bcum)
    k_in = kc * jnp.exp(-bcum)
    k_st = kc * jnp.exp(btot - bcum)
    mask = jnp.tril(jnp.ones((GLA_CHUNK, GLA_CHUNK), dtype=bool))
    scores = jnp.einsum('bhnid,bhnjd->bhnij', q_in, k_in)
    scores = jnp.where(mask, scores, 0.0)
    o_intra = jnp.einsum('bhnij,bhnje->bhnie', scores, vc)
    decay = jnp.exp(btot[..., 0, :])

    def step(state, xs):
        q_n, k_n, v_n, dec_n = xs
        o_n = jnp.einsum('bhcd,bhde->bhce', q_n, state)
        state = dec_n[..., None] * state + jnp.einsum('bhcd,bhce->bhde', k_n, v_n)
        return state, o_n

    xs = (jnp.moveaxis(q_in, 2, 0), jnp.moveaxis(k_st, 2, 0),
          jnp.moveaxis(vc, 2, 0), jnp.moveaxis(decay, 2, 0))
    state0 = jnp.zeros((bsz, nh, dk, dv), jnp.float32)
    _, o_inter = lax.scan(step, state0, xs)
    o_inter = jnp.moveaxis(o_inter, 0, 2)
    return (o_intra + o_inter).reshape(bsz, nh, seq, dv)


def even_layer(x, norm_pre, norm_post, w_in, conv_w, conv_b, gate_w, gate_b, lam, sc_w, w_out):
    h = rmsnorm(x, norm_pre)
    proj = h @ w_in
    xa, za, xb, gb, gc, zb = jnp.split(
        proj, [RG_WIDTH, 2 * RG_WIDTH, 2 * RG_WIDTH + SC_WIDTH,
               2 * RG_WIDTH + 2 * SC_WIDTH, 2 * RG_WIDTH + 3 * SC_WIDTH], axis=-1)
    ua = centred_dwconv(xa, conv_w, RG_CONV_LEFT) + conv_b
    ya = rg_lru(ua, gate_w[0], gate_b[0], lam[0], False) + rg_lru(ua, gate_w[1], gate_b[1], lam[1], True)
    ya = ya * jax.nn.silu(za.astype(jnp.float32))
    yb = gb * centred_dwconv(gc * xb, sc_w, SC_CONV_LEFT)
    yb = yb * jax.nn.silu(zb)
    y = jnp.concatenate([ya, yb.astype(jnp.float32)], axis=-1) @ w_out
    return (x + rmsnorm(y, norm_post)).astype(x.dtype)


def odd_layer(x, norm_pre, norm_post, w_in, w_gate_lr, b_gate, head_norm_g, w_out):
    bsz, seq, _ = x.shape
    h = rmsnorm(x, norm_pre)
    proj = h @ w_in
    q, k, v, r, lr = jnp.split(
        proj, [GLA_KEY, 2 * GLA_KEY, 2 * GLA_KEY + GLA_VAL, 2 * GLA_KEY + 2 * GLA_VAL], axis=-1)
    lr = lr.reshape(bsz, seq, 2, GLA_RANK).astype(jnp.float32)
    z = jnp.einsum('bsdr,drk->dbsk', lr, w_gate_lr.astype(jnp.float32)) + b_gate[:, None, None].astype(jnp.float32)
    log_alpha = jax.nn.log_sigmoid(z) / GLA_NORMALIZER

    def heads(t, dh):
        return jnp.transpose(t.reshape(bsz, seq, GLA_HEADS, dh), (0, 2, 1, 3))

    qh, kh, vh = heads(q, GLA_DK), heads(k, GLA_DK), heads(v, GLA_DV)
    g_f, g_b = heads(log_alpha[0], GLA_DK), heads(log_alpha[1], GLA_DK)
    o_f = gla_chunked(qh, kh, vh, g_f)
    o_b = jnp.flip(gla_chunked(jnp.flip(qh, 2), jnp.flip(kh, 2), jnp.flip(vh, 2), jnp.flip(g_b, 2)), 2)
    o = rmsnorm(o_f + o_b, head_norm_g)
    o = jnp.transpose(o, (0, 2, 1, 3)).reshape(bsz, seq, GLA_VAL)
    y = (o * jax.nn.silu(r.astype(jnp.float32))) @ w_out
    return (x + rmsnorm(y, norm_post)).astype(x.dtype)


def _fwd_setup_inputs(seed: int = 0) -> dict:
    key = jax.random.key(seed)
    ks = jax.random.split(key, 20)
    nrm = jax.random.normal
    f32 = jnp.float32
    lam_a = jax.random.uniform(ks[8], (N_EVEN, 2, RG_WIDTH), f32, minval=0.9, maxval=0.999)
    lam_s = lam_a ** (1.0 / RG_C)
    rg_lambda = jnp.log(lam_s) - jnp.log1p(-lam_s)
    return {
        'x': nrm(ks[0], (BATCH, SEQ, D_MODEL), f32),
        'even_norm_pre': 1.0 + 0.05 * nrm(ks[1], (N_EVEN, D_MODEL), f32),
        'even_norm_post': 1.0 + 0.05 * nrm(ks[2], (N_EVEN, D_MODEL), f32),
        'even_w_in': nrm(ks[3], (N_EVEN, D_MODEL, EVEN_IN), f32) * D_MODEL ** -0.5,
        'rg_conv_w': nrm(ks[4], (N_EVEN, RG_CONV, RG_WIDTH), f32) * RG_CONV ** -0.5,
        'rg_conv_b': 0.01 * nrm(ks[5], (N_EVEN, RG_WIDTH), f32),
        'rg_gate_w': nrm(ks[6], (N_EVEN, 2, 2, RG_HEADS, RG_HEAD_DIM, RG_HEAD_DIM), f32) * RG_HEAD_DIM ** -0.5,
        'rg_gate_b': 0.01 * nrm(ks[7], (N_EVEN, 2, 2, RG_HEADS, RG_HEAD_DIM), f32),
        'rg_lambda': rg_lambda,
        'sc_conv_w': nrm(ks[9], (N_EVEN, SC_CONV, SC_WIDTH), f32) * SC_CONV ** -0.5,
        'even_w_out': nrm(ks[10], (N_EVEN, EVEN_MIX, D_MODEL), f32) * EVEN_MIX ** -0.5,
        'odd_norm_pre': 1.0 + 0.05 * nrm(ks[11], (N_ODD, D_MODEL), f32),
        'odd_norm_post': 1.0 + 0.05 * nrm(ks[12], (N_ODD, D_MODEL), f32),
        'odd_w_in': nrm(ks[13], (N_ODD, D_MODEL, ODD_IN), f32) * D_MODEL ** -0.5,
        'gla_w_gate_lr': nrm(ks[14], (N_ODD, 2, GLA_RANK, GLA_KEY), f32) * GLA_RANK ** -0.5,
        'gla_b_gate': 1.0 + 0.3 * nrm(ks[15], (N_ODD, 2, GLA_KEY), f32),
        'gla_norm_g': 1.0 + 0.05 * nrm(ks[16], (N_ODD, GLA_DV), f32),
        'odd_w_out': nrm(ks[17], (N_ODD, GLA_VAL, D_MODEL), f32) * GLA_VAL ** -0.5,
    }


def _fwd_reference(x, even_norm_pre, even_norm_post, even_w_in, rg_conv_w, rg_conv_b, rg_gate_w,
              rg_gate_b, rg_lambda, sc_conv_w, even_w_out, odd_norm_pre, odd_norm_post,
              odd_w_in, gla_w_gate_lr, gla_b_gate, gla_norm_g, odd_w_out):
    for layer in range(DEPTH):
        j = layer // 2
        if layer % 2 == 0:
            x = even_layer(x, even_norm_pre[j], even_norm_post[j], even_w_in[j], rg_conv_w[j],
                           rg_conv_b[j], rg_gate_w[j], rg_gate_b[j], rg_lambda[j],
                           sc_conv_w[j], even_w_out[j])
        else:
            x = odd_layer(x, odd_norm_pre[j], odd_norm_post[j], odd_w_in[j], gla_w_gate_lr[j],
                          gla_b_gate[j], gla_norm_g[j], odd_w_out[j])
    return x


import jax as _jax
import jax.numpy as _jnp

TWIN_FORMAT = 'train_step'
FWD_PARAMS = ['x', 'even_norm_pre', 'even_norm_post', 'even_w_in', 'rg_conv_w', 'rg_conv_b', 'rg_gate_w', 'rg_gate_b', 'rg_lambda', 'sc_conv_w', 'even_w_out', 'odd_norm_pre', 'odd_norm_post', 'odd_w_in', 'gla_w_gate_lr', 'gla_b_gate', 'gla_norm_g', 'odd_w_out']
TWIN_WEIGHTS = ['even_norm_pre', 'even_norm_post', 'even_w_in', 'rg_conv_w', 'rg_conv_b', 'rg_gate_w', 'rg_gate_b', 'rg_lambda', 'sc_conv_w', 'even_w_out', 'odd_norm_pre', 'odd_norm_post', 'odd_w_in', 'gla_w_gate_lr', 'gla_b_gate', 'gla_norm_g', 'odd_w_out']
TWIN_DIFF_INPUT = 'x'
TWIN_INPUTS = ['x', 'even_norm_pre', 'even_norm_post', 'even_w_in', 'rg_conv_w', 'rg_conv_b', 'rg_gate_w', 'rg_gate_b', 'rg_lambda', 'sc_conv_w', 'even_w_out', 'odd_norm_pre', 'odd_norm_post', 'odd_w_in', 'gla_w_gate_lr', 'gla_b_gate', 'gla_norm_g', 'odd_w_out', 'loss_target', 'm_even_norm_pre', 'm_even_norm_post', 'm_even_w_in', 'm_rg_conv_w', 'm_rg_conv_b', 'm_rg_gate_w', 'm_rg_gate_b', 'm_rg_lambda', 'm_sc_conv_w', 'm_even_w_out', 'm_odd_norm_pre', 'm_odd_norm_post', 'm_odd_w_in', 'm_gla_w_gate_lr', 'm_gla_b_gate', 'm_gla_norm_g', 'm_odd_w_out', 'v_even_norm_pre', 'v_even_norm_post', 'v_even_w_in', 'v_rg_conv_w', 'v_rg_conv_b', 'v_rg_gate_w', 'v_rg_gate_b', 'v_rg_lambda', 'v_sc_conv_w', 'v_even_w_out', 'v_odd_norm_pre', 'v_odd_norm_post', 'v_odd_w_in', 'v_gla_w_gate_lr', 'v_gla_b_gate', 'v_gla_norm_g', 'v_odd_w_out']
TWIN_OUTPUTS = ['loss', 'grad_x', 'grad_even_norm_pre', 'grad_even_norm_post', 'grad_even_w_in', 'grad_rg_conv_w', 'grad_rg_conv_b', 'grad_rg_gate_w', 'grad_rg_gate_b', 'grad_rg_lambda', 'grad_sc_conv_w', 'grad_even_w_out', 'grad_odd_norm_pre', 'grad_odd_norm_post', 'grad_odd_w_in', 'grad_gla_w_gate_lr', 'grad_gla_b_gate', 'grad_gla_norm_g', 'grad_odd_w_out', 'delta_even_norm_pre', 'delta_even_norm_post', 'delta_even_w_in', 'delta_rg_conv_w', 'delta_rg_conv_b', 'delta_rg_gate_w', 'delta_rg_gate_b', 'delta_rg_lambda', 'delta_sc_conv_w', 'delta_even_w_out', 'delta_odd_norm_pre', 'delta_odd_norm_post', 'delta_odd_w_in', 'delta_gla_w_gate_lr', 'delta_gla_b_gate', 'delta_gla_norm_g', 'delta_odd_w_out', 'new_m_even_norm_pre', 'new_m_even_norm_post', 'new_m_even_w_in', 'new_m_rg_conv_w', 'new_m_rg_conv_b', 'new_m_rg_gate_w', 'new_m_rg_gate_b', 'new_m_rg_lambda', 'new_m_sc_conv_w', 'new_m_even_w_out', 'new_m_odd_norm_pre', 'new_m_odd_norm_post', 'new_m_odd_w_in', 'new_m_gla_w_gate_lr', 'new_m_gla_b_gate', 'new_m_gla_norm_g', 'new_m_odd_w_out', 'new_v_even_norm_pre', 'new_v_even_norm_post', 'new_v_even_w_in', 'new_v_rg_conv_w', 'new_v_rg_conv_b', 'new_v_rg_gate_w', 'new_v_rg_gate_b', 'new_v_rg_lambda', 'new_v_sc_conv_w', 'new_v_even_w_out', 'new_v_odd_norm_pre', 'new_v_odd_norm_post', 'new_v_odd_w_in', 'new_v_gla_w_gate_lr', 'new_v_gla_b_gate', 'new_v_gla_norm_g', 'new_v_odd_w_out']
TWIN_LEAF_KINDS = {'loss': 'loss', 'grad_x': 'grad_x', 'grad_even_norm_pre': 'grad_w', 'grad_even_norm_post': 'grad_w', 'grad_even_w_in': 'grad_w', 'grad_rg_conv_w': 'grad_w', 'grad_rg_conv_b': 'grad_w', 'grad_rg_gate_w': 'grad_w', 'grad_rg_gate_b': 'grad_w', 'grad_rg_lambda': 'grad_w', 'grad_sc_conv_w': 'grad_w', 'grad_even_w_out': 'grad_w', 'grad_odd_norm_pre': 'grad_w', 'grad_odd_norm_post': 'grad_w', 'grad_odd_w_in': 'grad_w', 'grad_gla_w_gate_lr': 'grad_w', 'grad_gla_b_gate': 'grad_w', 'grad_gla_norm_g': 'grad_w', 'grad_odd_w_out': 'grad_w', 'delta_even_norm_pre': 'delta_w', 'delta_even_norm_post': 'delta_w', 'delta_even_w_in': 'delta_w', 'delta_rg_conv_w': 'delta_w', 'delta_rg_conv_b': 'delta_w', 'delta_rg_gate_w': 'delta_w', 'delta_rg_gate_b': 'delta_w', 'delta_rg_lambda': 'delta_w', 'delta_sc_conv_w': 'delta_w', 'delta_even_w_out': 'delta_w', 'delta_odd_norm_pre': 'delta_w', 'delta_odd_norm_post': 'delta_w', 'delta_odd_w_in': 'delta_w', 'delta_gla_w_gate_lr': 'delta_w', 'delta_gla_b_gate': 'delta_w', 'delta_gla_norm_g': 'delta_w', 'delta_odd_w_out': 'delta_w', 'new_m_even_norm_pre': 'new_m', 'new_m_even_norm_post': 'new_m', 'new_m_even_w_in': 'new_m', 'new_m_rg_conv_w': 'new_m', 'new_m_rg_conv_b': 'new_m', 'new_m_rg_gate_w': 'new_m', 'new_m_rg_gate_b': 'new_m', 'new_m_rg_lambda': 'new_m', 'new_m_sc_conv_w': 'new_m', 'new_m_even_w_out': 'new_m', 'new_m_odd_norm_pre': 'new_m', 'new_m_odd_norm_post': 'new_m', 'new_m_odd_w_in': 'new_m', 'new_m_gla_w_gate_lr': 'new_m', 'new_m_gla_b_gate': 'new_m', 'new_m_gla_norm_g': 'new_m', 'new_m_odd_w_out': 'new_m', 'new_v_even_norm_pre': 'new_v', 'new_v_even_norm_post': 'new_v', 'new_v_even_w_in': 'new_v', 'new_v_rg_conv_w': 'new_v', 'new_v_rg_conv_b': 'new_v', 'new_v_rg_gate_w': 'new_v', 'new_v_rg_gate_b': 'new_v', 'new_v_rg_lambda': 'new_v', 'new_v_sc_conv_w': 'new_v', 'new_v_even_w_out': 'new_v', 'new_v_odd_norm_pre': 'new_v', 'new_v_odd_norm_post': 'new_v', 'new_v_odd_w_in': 'new_v', 'new_v_gla_w_gate_lr': 'new_v', 'new_v_gla_b_gate': 'new_v', 'new_v_gla_norm_g': 'new_v', 'new_v_odd_w_out': 'new_v'}


def _forward(args):
    return _fwd_reference(*[args[k] for k in FWD_PARAMS])


def _output_shape():
    out = _jax.eval_shape(lambda: _forward(_fwd_setup_inputs(0)))
    return out.shape, out.dtype

N_MICROBATCH = 1
ADAM_LR = 0.001
ADAM_B1 = 0.9
ADAM_B2 = 0.999
ADAM_EPS = 1e-08
ADAM_WD = 0.01
ADAM_STEP = 10
PER_EXAMPLE_BATCH_AXIS = {'x': 0, 'loss_target': 0}
SHARED_INPUTS = []
_WEIGHT_DTYPES = {'even_norm_pre': _jnp.float32, 'even_norm_post': _jnp.float32, 'even_w_in': _jnp.float32, 'rg_conv_w': _jnp.float32, 'rg_conv_b': _jnp.float32, 'rg_gate_w': _jnp.float32, 'rg_gate_b': _jnp.float32, 'rg_lambda': _jnp.float32, 'sc_conv_w': _jnp.float32, 'even_w_out': _jnp.float32, 'odd_norm_pre': _jnp.float32, 'odd_norm_post': _jnp.float32, 'odd_w_in': _jnp.float32, 'gla_w_gate_lr': _jnp.float32, 'gla_b_gate': _jnp.float32, 'gla_norm_g': _jnp.float32, 'odd_w_out': _jnp.float32}
MOMENT_SCALE = {'even_norm_pre': 9.756579e-01, 'even_norm_post': 3.166879e+01, 'even_w_in': 3.923205e-01, 'rg_conv_w': 4.592120e-01, 'rg_conv_b': 8.539120e+00, 'rg_gate_w': 1.589748e-01, 'rg_gate_b': 9.137341e-02, 'rg_lambda': 1.518936e-01, 'sc_conv_w': 4.148638e-01, 'even_w_out': 5.827053e-01, 'odd_norm_pre': 6.797080e-01, 'odd_norm_post': 3.212477e+01, 'odd_w_in': 3.895759e-01, 'gla_w_gate_lr': 4.527046e-02, 'gla_b_gate': 1.875707e-01, 'gla_norm_g': 6.553772e-01, 'odd_w_out': 3.423848e-01}


def _to_microbatches(a, axis):
    t = _jnp.moveaxis(a, axis, 0)
    t = t.reshape((N_MICROBATCH, t.shape[0] // N_MICROBATCH) + t.shape[1:])
    return _jnp.moveaxis(t, 1, axis + 1)


def setup_inputs(seed: int = 0) -> dict:
    inp = _fwd_setup_inputs(seed)
    key = _jax.random.fold_in(_jax.random.key(seed), 7919)
    shape, _ = _output_shape()
    out = dict(inp)
    out["loss_target"] = _jax.random.normal(_jax.random.fold_in(key, 0), shape, _jnp.float32)
    for i, name in enumerate(TWIN_WEIGHTS):
        w = inp[name].astype(_jnp.float32)
        if MOMENT_SCALE is None:
            s = _jnp.sqrt(_jnp.mean(_jnp.square(w)) + 1e-30)
        else:
            s = MOMENT_SCALE[name]
        km, kv = _jax.random.split(_jax.random.fold_in(key, i + 1))
        out[name] = w
        out["m_" + name] = s * _jax.random.normal(km, w.shape, _jnp.float32)
        out["v_" + name] = (s * s) * _jax.random.uniform(kv, w.shape, _jnp.float32, 0.5, 1.5)
    if N_MICROBATCH > 1:
        for name, axis in PER_EXAMPLE_BATCH_AXIS.items():
            out[name] = _to_microbatches(out[name], axis)
    return {'x': out['x'], 'even_norm_pre': out['even_norm_pre'], 'even_norm_post': out['even_norm_post'], 'even_w_in': out['even_w_in'], 'rg_conv_w': out['rg_conv_w'], 'rg_conv_b': out['rg_conv_b'], 'rg_gate_w': out['rg_gate_w'], 'rg_gate_b': out['rg_gate_b'], 'rg_lambda': out['rg_lambda'], 'sc_conv_w': out['sc_conv_w'], 'even_w_out': out['even_w_out'], 'odd_norm_pre': out['odd_norm_pre'], 'odd_norm_post': out['odd_norm_post'], 'odd_w_in': out['odd_w_in'], 'gla_w_gate_lr': out['gla_w_gate_lr'], 'gla_b_gate': out['gla_b_gate'], 'gla_norm_g': out['gla_norm_g'], 'odd_w_out': out['odd_w_out'], 'loss_target': out['loss_target'], 'm_even_norm_pre': out['m_even_norm_pre'], 'm_even_norm_post': out['m_even_norm_post'], 'm_even_w_in': out['m_even_w_in'], 'm_rg_conv_w': out['m_rg_conv_w'], 'm_rg_conv_b': out['m_rg_conv_b'], 'm_rg_gate_w': out['m_rg_gate_w'], 'm_rg_gate_b': out['m_rg_gate_b'], 'm_rg_lambda': out['m_rg_lambda'], 'm_sc_conv_w': out['m_sc_conv_w'], 'm_even_w_out': out['m_even_w_out'], 'm_odd_norm_pre': out['m_odd_norm_pre'], 'm_odd_norm_post': out['m_odd_norm_post'], 'm_odd_w_in': out['m_odd_w_in'], 'm_gla_w_gate_lr': out['m_gla_w_gate_lr'], 'm_gla_b_gate': out['m_gla_b_gate'], 'm_gla_norm_g': out['m_gla_norm_g'], 'm_odd_w_out': out['m_odd_w_out'], 'v_even_norm_pre': out['v_even_norm_pre'], 'v_even_norm_post': out['v_even_norm_post'], 'v_even_w_in': out['v_even_w_in'], 'v_rg_conv_w': out['v_rg_conv_w'], 'v_rg_conv_b': out['v_rg_conv_b'], 'v_rg_gate_w': out['v_rg_gate_w'], 'v_rg_gate_b': out['v_rg_gate_b'], 'v_rg_lambda': out['v_rg_lambda'], 'v_sc_conv_w': out['v_sc_conv_w'], 'v_even_w_out': out['v_even_w_out'], 'v_odd_norm_pre': out['v_odd_norm_pre'], 'v_odd_norm_post': out['v_odd_norm_post'], 'v_odd_w_in': out['v_odd_w_in'], 'v_gla_w_gate_lr': out['v_gla_w_gate_lr'], 'v_gla_b_gate': out['v_gla_b_gate'], 'v_gla_norm_g': out['v_gla_norm_g'], 'v_odd_w_out': out['v_odd_w_out']}


def _loss(weights, diff, rest, loss_target):
    with _jax.named_scope("forward"):
        args = {**rest, TWIN_DIFF_INPUT: diff, **{k: w.astype(_WEIGHT_DTYPES[k]) for k, w in weights.items()}}
        y = _forward(args)
    with _jax.named_scope("loss_head"):
        err = _jnp.square(y.astype(_jnp.float32) - loss_target)
        return 0.5 * _jnp.sum(_jnp.mean(err, axis=-1)) if err.ndim else 0.5 * err


def _adamw(w, g, m, v):
    m = ADAM_B1 * m + (1.0 - ADAM_B1) * g
    v = ADAM_B2 * v + (1.0 - ADAM_B2) * _jnp.square(g)
    m_hat = m / (1.0 - ADAM_B1 ** ADAM_STEP)
    v_hat = v / (1.0 - ADAM_B2 ** ADAM_STEP)
    delta = -ADAM_LR * (m_hat / (_jnp.sqrt(v_hat) + ADAM_EPS) + ADAM_WD * w)
    return delta, m, v


def reference(x, even_norm_pre, even_norm_post, even_w_in, rg_conv_w, rg_conv_b, rg_gate_w, rg_gate_b, rg_lambda, sc_conv_w, even_w_out, odd_norm_pre, odd_norm_post, odd_w_in, gla_w_gate_lr, gla_b_gate, gla_norm_g, odd_w_out, loss_target, m_even_norm_pre, m_even_norm_post, m_even_w_in, m_rg_conv_w, m_rg_conv_b, m_rg_gate_w, m_rg_gate_b, m_rg_lambda, m_sc_conv_w, m_even_w_out, m_odd_norm_pre, m_odd_norm_post, m_odd_w_in, m_gla_w_gate_lr, m_gla_b_gate, m_gla_norm_g, m_odd_w_out, v_even_norm_pre, v_even_norm_post, v_even_w_in, v_rg_conv_w, v_rg_conv_b, v_rg_gate_w, v_rg_gate_b, v_rg_lambda, v_sc_conv_w, v_even_w_out, v_odd_norm_pre, v_odd_norm_post, v_odd_w_in, v_gla_w_gate_lr, v_gla_b_gate, v_gla_norm_g, v_odd_w_out):
    given = dict(x=x, even_norm_pre=even_norm_pre, even_norm_post=even_norm_post, even_w_in=even_w_in, rg_conv_w=rg_conv_w, rg_conv_b=rg_conv_b, rg_gate_w=rg_gate_w, rg_gate_b=rg_gate_b, rg_lambda=rg_lambda, sc_conv_w=sc_conv_w, even_w_out=even_w_out, odd_norm_pre=odd_norm_pre, odd_norm_post=odd_norm_post, odd_w_in=odd_w_in, gla_w_gate_lr=gla_w_gate_lr, gla_b_gate=gla_b_gate, gla_norm_g=gla_norm_g, odd_w_out=odd_w_out, loss_target=loss_target, m_even_norm_pre=m_even_norm_pre, m_even_norm_post=m_even_norm_post, m_even_w_in=m_even_w_in, m_rg_conv_w=m_rg_conv_w, m_rg_conv_b=m_rg_conv_b, m_rg_gate_w=m_rg_gate_w, m_rg_gate_b=m_rg_gate_b, m_rg_lambda=m_rg_lambda, m_sc_conv_w=m_sc_conv_w, m_even_w_out=m_even_w_out, m_odd_norm_pre=m_odd_norm_pre, m_odd_norm_post=m_odd_norm_post, m_odd_w_in=m_odd_w_in, m_gla_w_gate_lr=m_gla_w_gate_lr, m_gla_b_gate=m_gla_b_gate, m_gla_norm_g=m_gla_norm_g, m_odd_w_out=m_odd_w_out, v_even_norm_pre=v_even_norm_pre, v_even_norm_post=v_even_norm_post, v_even_w_in=v_even_w_in, v_rg_conv_w=v_rg_conv_w, v_rg_conv_b=v_rg_conv_b, v_rg_gate_w=v_rg_gate_w, v_rg_gate_b=v_rg_gate_b, v_rg_lambda=v_rg_lambda, v_sc_conv_w=v_sc_conv_w, v_even_w_out=v_even_w_out, v_odd_norm_pre=v_odd_norm_pre, v_odd_norm_post=v_odd_norm_post, v_odd_w_in=v_odd_w_in, v_gla_w_gate_lr=v_gla_w_gate_lr, v_gla_b_gate=v_gla_b_gate, v_gla_norm_g=v_gla_norm_g, v_odd_w_out=v_odd_w_out)
    weights = {n: given[n] for n in TWIN_WEIGHTS}
    shared = {n: given[n] for n in SHARED_INPUTS}
    per_example = {n: given[n] for n in ['x']}
    grad_fn = _jax.value_and_grad(_loss, argnums=(0, 1))

    def one_microbatch(ex, loss_target):
        ex = dict(ex)
        diff = ex.pop(TWIN_DIFF_INPUT)
        return grad_fn(weights, diff, {**shared, **ex}, loss_target)

    if N_MICROBATCH == 1:
        loss, (grad_w, grad_x) = one_microbatch(per_example, given["loss_target"])
    else:
        def body(carry, xs):
            loss_sum, grad_sum = carry
            l_k, (gw_k, gx_k) = one_microbatch(xs[0], xs[1])
            with _jax.named_scope("update"):
                return (loss_sum + l_k, _jax.tree.map(_jnp.add, grad_sum, gw_k)), gx_k

        init = (_jnp.zeros((), _jnp.float32), _jax.tree.map(_jnp.zeros_like, weights))
        (loss, grad_w), grad_x = _jax.lax.scan(body, init, (per_example, given["loss_target"]))
    with _jax.named_scope("update"):
        delta_w, new_m, new_v = {}, {}, {}
        for n in TWIN_WEIGHTS:
            delta_w[n], new_m[n], new_v[n] = _adamw(weights[n], grad_w[n], given["m_" + n], given["v_" + n])
    return (loss, grad_x, *[grad_w[n] for n in TWIN_WEIGHTS], *[delta_w[n] for n in TWIN_WEIGHTS],
            *[new_m[n] for n in TWIN_WEIGHTS], *[new_v[n] for n in TWIN_WEIGHTS])
```

```python
import functools

import jax
import jax.numpy as jnp
from jax import lax
from jax.experimental import pallas as pl
from jax.experimental.pallas import tpu as pltpu

F32 = jnp.float32
BF16 = jnp.bfloat16

N_DEV = 8
D_MODEL = 1024
NORM_EPS = 1e-6
RG_HEADS = 8
RG_HEAD_DIM = 128
RG_C = 8.0
GLA_HEADS = 4
GLA_DK = 128
GLA_DV = 256
GLA_KEY = 512
GLA_RANK = 16
GLA_NORMALIZER = 16.0
GLA_CHUNK = 64
EVEN_IN = 6144
ODD_IN = 3104
ODD_IN_PAD = 3200
ODD_SHARD = ODD_IN // N_DEV
EVEN_SHARD = EVEN_IN // N_DEV
ADAM_LR = 0.001
ADAM_B1 = 0.9
ADAM_B2 = 0.999
ADAM_EPS = 1e-08
ADAM_WD = 0.01
ADAM_STEP = 10

SUBLANES = 8
LANES = 128
VMEM_LIMIT_BYTES = 48 * 2 ** 20
ROW_TILE = 256
MM_TILE = 512
PACK_ROWS = 48
MESH_ID = pl.DeviceIdType.MESH


def _params(n_grid):
    return pltpu.CompilerParams(dimension_semantics=("arbitrary",) * n_grid, vmem_limit_bytes=VMEM_LIMIT_BYTES)


def _bdot(a, b):
    return jnp.dot(a.astype(BF16), b.astype(BF16), preferred_element_type=F32)


def _bdot_nt(a, b):
    return lax.dot_general(a.astype(BF16), b.astype(BF16), (((1,), (1,)), ((), ())), preferred_element_type=F32)


def _bdot_tn(a, b):
    return lax.dot_general(a.astype(BF16), b.astype(BF16), (((0,), (0,)), ((), ())), preferred_element_type=F32)


def _rstd(x):
    return lax.rsqrt(jnp.mean(x * x, axis=-1, keepdims=True) + NORM_EPS)


def _rms(x, g):
    return x * _rstd(x) * g


def _rms_bwd(x, g, dy):
    xh = x * _rstd(x)
    dyg = dy * g
    dx = _rstd(x) * (dyg - xh * jnp.mean(dyg * xh, axis=-1, keepdims=True))
    return dx, jnp.sum(dy * xh, axis=0, keepdims=True)


def _sigmoid(z):
    return jax.nn.sigmoid(z)


def _silu_and_grad(z):
    s = _sigmoid(z)
    return z * s, s * (1.0 + z * (1.0 - s))


def _softplus(z):
    return jnp.maximum(z, 0.0) + jnp.log(1.0 + jnp.exp(-jnp.abs(z)))


def _shift_rows(cur, before, after, d):
    ts = cur.shape[0]
    row = lax.broadcasted_iota(jnp.int32, cur.shape, 0)
    out = pltpu.roll(cur, (-d) % ts, 0)
    if d < 0:
        for j in range(-d):
            out = jnp.where(row == j, before[SUBLANES + j + d:SUBLANES + j + d + 1, :], out)
    else:
        for j in range(d):
            out = jnp.where(row == ts - d + j, after[j:j + 1, :], out)
    return out


def _halo_specs(ts, s, width, col):
    per = ts // SUBLANES
    last = s // SUBLANES - 1
    return [
        pl.BlockSpec((ts, width), lambda i: (i, col)),
        pl.BlockSpec((SUBLANES, width), lambda i: (jnp.maximum(i * per - 1, 0), col)),
        pl.BlockSpec((SUBLANES, width), lambda i: (jnp.minimum((i + 1) * per, last), col)),
    ]


def _halo_load(cur_ref, before_ref, after_ref, n_tiles):
    i = pl.program_id(0)
    before = jnp.where(i > 0, before_ref[...], 0.0)
    after = jnp.where(i < n_tiles - 1, after_ref[...], 0.0)
    return cur_ref[...], before, after


def _full(shape):
    return pl.BlockSpec(shape, lambda *_: (0,) * len(shape))


def rms_matmul(x, g, w, tn, name):
    s, d = x.shape
    n = w.shape[1]
    tm = min(MM_TILE, s)

    def body(x_ref, g_ref, w_ref, o_ref, h_ref):
        @pl.when(pl.program_id(1) == 0)
        def _():
            h_ref[...] = _rms(x_ref[...], g_ref[...]).astype(BF16)

        o_ref[...] = jnp.dot(h_ref[...], w_ref[...], preferred_element_type=F32)

    return pl.pallas_call(
        body, name=name, grid=(s // tm, n // tn),
        in_specs=[pl.BlockSpec((tm, d), lambda i, j: (i, 0)), _full((1, d)), pl.BlockSpec((d, tn), lambda i, j: (0, j))],
        out_specs=[pl.BlockSpec((tm, tn), lambda i, j: (i, j)), pl.BlockSpec((tm, d), lambda i, j: (i, 0))],
        out_shape=[jax.ShapeDtypeStruct((s, n), F32), jax.ShapeDtypeStruct((s, d), BF16)],
        compiler_params=_params(2),
    )(x, g, w)


def matmul_post(u, w, xres, g, name, target=None):
    s, k = u.shape
    d = w.shape[1]
    tm = min(MM_TILE, s)
    with_loss = target is not None

    def body(*refs):
        if with_loss:
            u_ref, w_ref, x_ref, g_ref, t_ref, y_ref, dout_ref, loss_ref = refs
        else:
            u_ref, w_ref, x_ref, g_ref, y_ref, out_ref = refs
        y = jnp.dot(u_ref[...], w_ref[...], preferred_element_type=F32)
        y_ref[...] = y
        out = x_ref[...] + _rms(y, g_ref[...])
        if with_loss:
            @pl.when(pl.program_id(0) == 0)
            def _():
                loss_ref[...] = jnp.zeros_like(loss_ref)

            diff = out - t_ref[...]
            dout_ref[...] = diff * (1.0 / d)
            loss_ref[...] += 0.5 * jnp.sum(jnp.mean(diff * diff, axis=-1, keepdims=True))
        else:
            out_ref[...] = out

    row = pl.BlockSpec((tm, d), lambda i: (i, 0))
    in_specs = [pl.BlockSpec((tm, k), lambda i: (i, 0)), _full((k, d)), row, _full((1, d))]
    args = [u, w, xres, g]
    out_specs = [row, row]
    out_shape = [jax.ShapeDtypeStruct((s, d), F32), jax.ShapeDtypeStruct((s, d), F32)]
    if with_loss:
        in_specs.append(row)
        args.append(target)
        out_specs.append(_full((SUBLANES, LANES)))
        out_shape.append(jax.ShapeDtypeStruct((SUBLANES, LANES), F32))
    return pl.pallas_call(body, name=name, grid=(s // tm,), in_specs=in_specs, out_specs=out_specs,
                          out_shape=out_shape, compiler_params=_params(1))(*args)


def normbwd_matmul_nt(y, g, dout, w, tn, name):
    s, d = y.shape
    n = w.shape[0]
    tm = min(MM_TILE, s)

    def body(y_ref, g_ref, dout_ref, w_ref, du_ref, dy_ref, dg_ref):
        i, j = pl.program_id(0), pl.program_id(1)

        @pl.when(j == 0)
        def _():
            dy, dg = _rms_bwd(y_ref[...], g_ref[...], dout_ref[...])
            dy_ref[...] = dy.astype(BF16)

            @pl.when(i == 0)
            def _():
                dg_ref[...] = jnp.zeros_like(dg_ref)

            dg_ref[...] += dg

        du_ref[...] = lax.dot_general(dy_ref[...], w_ref[...], (((1,), (1,)), ((), ())), preferred_element_type=F32)

    row = pl.BlockSpec((tm, d), lambda i, j: (i, 0))
    return pl.pallas_call(
        body, name=name, grid=(s // tm, n // tn),
        in_specs=[row, _full((1, d)), row, pl.BlockSpec((tn, d), lambda i, j: (j, 0))],
        out_specs=[pl.BlockSpec((tm, tn), lambda i, j: (i, j)), row, _full((1, d))],
        out_shape=[jax.ShapeDtypeStruct((s, n), F32), jax.ShapeDtypeStruct((s, d), BF16), jax.ShapeDtypeStruct((1, d), F32)],
        compiler_params=_params(2),
    )(y, g, dout, w)


def matmul_tn(a, b, tm, tn, out_dtype, name):
    s, m = a.shape
    n = b.shape[1]
    ts = min(MM_TILE, s)
    n_k = s // ts

    def body(a_ref, b_ref, o_ref, acc):
        k = pl.program_id(2)

        @pl.when(k == 0)
        def _():
            acc[...] = jnp.zeros_like(acc)

        acc[...] += lax.dot_general(a_ref[...], b_ref[...], (((0,), (0,)), ((), ())), preferred_element_type=F32)

        @pl.when(k == n_k - 1)
        def _():
            o_ref[...] = acc[...].astype(out_dtype)

    return pl.pallas_call(
        body, name=name, grid=(m // tm, n // tn, n_k),
        in_specs=[pl.BlockSpec((ts, tm), lambda i, j, k: (k, i)), pl.BlockSpec((ts, tn), lambda i, j, k: (k, j))],
        out_specs=pl.BlockSpec((tm, tn), lambda i, j, k: (i, j)),
        out_shape=jax.ShapeDtypeStruct((m, n), out_dtype),
        scratch_shapes=[pltpu.VMEM((tm, tn), F32)],
        compiler_params=_params(3),
    )(a, b)


def matmul_nt_normbwd(dproj, w, x, g, dres, tk, name):
    s, kt = dproj.shape
    d = w.shape[0]
    tm = min(MM_TILE, s)
    n_k = kt // tk

    def body(a_ref, w_ref, x_ref, g_ref, r_ref, dx_ref, dg_ref, acc):
        i, k = pl.program_id(0), pl.program_id(1)

        @pl.when(k == 0)
        def _():
            acc[...] = jnp.zeros_like(acc)

        acc[...] += lax.dot_general(a_ref[...], w_ref[...], (((1,), (1,)), ((), ())), preferred_element_type=F32)

        @pl.when(k == n_k - 1)
        def _():
            dx, dg = _rms_bwd(x_ref[...], g_ref[...], acc[...])
            dx_ref[...] = r_ref[...] + dx

            @pl.when(i == 0)
            def _():
                dg_ref[...] = jnp.zeros_like(dg_ref)

            dg_ref[...] += dg

    row = pl.BlockSpec((tm, d), lambda i, k: (i, 0))
    return pl.pallas_call(
        body, name=name, grid=(s // tm, n_k),
        in_specs=[pl.BlockSpec((tm, tk), lambda i, k: (i, k)), pl.BlockSpec((d, tk), lambda i, k: (0, k)), row, _full((1, d)), row],
        out_specs=[row, _full((1, d))],
        out_shape=[jax.ShapeDtypeStruct((s, d), F32), jax.ShapeDtypeStruct((1, d), F32)],
        scratch_shapes=[pltpu.VMEM((tm, d), F32)],
        compiler_params=_params(2),
    )(dproj, w, x, g, dres)


def _rg_conv(xa, before, after, cw, cb):
    return (cw[0:1, :] * _shift_rows(xa, before, after, -2) + cw[1:2, :] * _shift_rows(xa, before, after, -1)
            + cw[2:3, :] * xa + cw[3:4, :] * _shift_rows(xa, before, after, 1) + cb)


def _rg_gates(ua_h, gw_ref, gb, c_h, direction, head):
    lanes = slice(head * RG_HEAD_DIM, (head + 1) * RG_HEAD_DIM)
    r = _sigmoid(_bdot(ua_h, gw_ref[2 * direction, head]) + gb[2 * direction:2 * direction + 1, lanes])
    i = _sigmoid(_bdot(ua_h, gw_ref[2 * direction + 1, head]) + gb[2 * direction + 1:2 * direction + 2, lanes])
    log_a = -c_h * r
    a = jnp.exp(log_a)
    beta = jnp.sqrt(-jnp.tanh(log_a) * (1.0 + a * a))
    return r, i, a, beta


def even_gates_fwd(proj, conv_w, conv_b, gate_w, gate_b, lam):
    s = proj.shape[0]
    ts = min(ROW_TILE, s)
    n_tiles = s // ts

    def body(xa_ref, xb_ref, xn_ref, cw_ref, cb_ref, gw_ref, gb_ref, lam_ref, o_ref):
        xa, before, after = _halo_load(xa_ref, xb_ref, xn_ref, n_tiles)
        ua = _rg_conv(xa, before, after, cw_ref[...], cb_ref[...])
        c = RG_C * _softplus(-lam_ref[...])
        gb = gb_ref[...]
        for direction in range(2):
            for head in range(RG_HEADS):
                lanes = slice(head * RG_HEAD_DIM, (head + 1) * RG_HEAD_DIM)
                ua_h = ua[:, lanes]
                _, i, a, beta = _rg_gates(ua_h, gw_ref, gb, c[direction:direction + 1, lanes], direction, head)
                o_ref[2 * direction, :, lanes] = a
                o_ref[2 * direction + 1, :, lanes] = beta * (i * ua_h)

    return pl.pallas_call(
        body, name="even_gates_fwd", grid=(n_tiles,),
        in_specs=_halo_specs(ts, s, D_MODEL, 0) + [_full(conv_w.shape), _full(conv_b.shape), _full(gate_w.shape),
                                                   _full(gate_b.shape), _full(lam.shape)],
        out_specs=pl.BlockSpec((4, ts, D_MODEL), lambda i: (0, i, 0)),
        out_shape=jax.ShapeDtypeStruct((4, s, D_MODEL), F32),
        compiler_params=_params(1),
    )(proj, proj, proj, conv_w, conv_b, gate_w, gate_b, lam)


def linear_scan(a_arr, a_idx, b_arr, b_idx, reverse, b_times_a, name):
    _, s, c = a_arr.shape
    ts = min(MM_TILE, s)
    n_tiles = s // ts
    n_blocks = ts // SUBLANES

    def tile_of(i):
        return n_tiles - 1 - i if reverse else i

    def body(a_ref, b_ref, h_ref, carry):
        @pl.when(pl.program_id(0) == 0)
        def _():
            carry[...] = jnp.zeros_like(carry)

        row = lax.broadcasted_iota(jnp.int32, (SUBLANES, c), 0)

        def block(j, h_in):
            r0 = pl.multiple_of((n_blocks - 1 - j if reverse else j) * SUBLANES, SUBLANES)
            a = a_ref[pl.ds(r0, SUBLANES), :]
            b = b_ref[pl.ds(r0, SUBLANES), :]
            if b_times_a:
                b = a * b
            for step in (1, 2, 4):
                shift = SUBLANES - step if reverse else step
                valid = row < SUBLANES - step if reverse else row >= step
                b = jnp.where(valid, a * pltpu.roll(b, shift, 0) + b, b)
                a = jnp.where(valid, a * pltpu.roll(a, shift, 0), a)
            h = a * h_in + b
            h_ref[pl.ds(r0, SUBLANES), :] = h
            return h[0:1, :] if reverse else h[SUBLANES - 1:SUBLANES, :]

        carry[0:1, :] = lax.fori_loop(0, n_blocks, block, carry[0:1, :])

    return pl.pallas_call(
        body, name=name, grid=(n_tiles,),
        in_specs=[pl.BlockSpec((None, ts, c), lambda i: (a_idx, tile_of(i), 0)),
                  pl.BlockSpec((None, ts, c), lambda i: (b_idx, tile_of(i), 0))],
        out_specs=pl.BlockSpec((ts, c), lambda i: (tile_of(i), 0)),
        out_shape=jax.ShapeDtypeStruct((s, c), F32),
        scratch_shapes=[pltpu.VMEM((SUBLANES, c), F32)],
        compiler_params=_params(1),
    )(a_arr, b_arr)


def _sc_conv(p, before, after, w):
    return w[0:1, :] * _shift_rows(p, before, after, -1) + w[1:2, :] * p + w[2:3, :] * _shift_rows(p, before, after, 1)


def even_mix_fwd(hf, hb, proj, sc_w):
    s = proj.shape[0]
    ts = min(ROW_TILE, s)
    n_tiles = s // ts
    row = pl.BlockSpec((ts, D_MODEL), lambda i: (i, 0))

    def col(c):
        return pl.BlockSpec((ts, D_MODEL), lambda i: (i, c))

    def body(hf_ref, hb_ref, za_ref, xb_ref, xbb_ref, xbn_ref, gb_ref, gc_ref, gcb_ref, gcn_ref, zb_ref, w_ref, u_ref):
        xb, xb_before, xb_after = _halo_load(xb_ref, xbb_ref, xbn_ref, n_tiles)
        gc, gc_before, gc_after = _halo_load(gc_ref, gcb_ref, gcn_ref, n_tiles)
        silu_za, _ = _silu_and_grad(za_ref[...])
        silu_zb, _ = _silu_and_grad(zb_ref[...])
        u_ref[:, :D_MODEL] = ((hf_ref[...] + hb_ref[...]) * silu_za).astype(BF16)
        cv = _sc_conv(gc * xb, gc_before * xb_before, gc_after * xb_after, w_ref[...])
        u_ref[:, D_MODEL:] = (gb_ref[...] * cv * silu_zb).astype(BF16)

    return pl.pallas_call(
        body, name="even_mix_fwd", grid=(n_tiles,),
        in_specs=[row, row, col(1)] + _halo_specs(ts, s, D_MODEL, 2) + [col(3)] + _halo_specs(ts, s, D_MODEL, 4)
        + [col(5), _full(sc_w.shape)],
        out_specs=pl.BlockSpec((ts, 2 * D_MODEL), lambda i: (i, 0)),
        out_shape=jax.ShapeDtypeStruct((s, 2 * D_MODEL), BF16),
        compiler_params=_params(1),
    )(hf, hb, proj, proj, proj, proj, proj, proj, proj, proj, proj, sc_w)


def even_mix_bwd(du, hf, hb, proj, sc_w):
    s = proj.shape[0]
    ts = min(ROW_TILE, s)
    n_tiles = s // ts
    row = pl.BlockSpec((ts, D_MODEL), lambda i: (i, 0))

    def body(dya_ref, dyb_ref, dybb_ref, dybn_ref, hf_ref, hb_ref, za_ref, xb_ref, xbb_ref, xbn_ref,
             gb_ref, gbb_ref, gbn_ref, gc_ref, gcb_ref, gcn_ref, zb_ref, zbb_ref, zbn_ref, w_ref,
             dh_ref, dp_ref, dw_ref):
        dyb, dyb_before, dyb_after = _halo_load(dyb_ref, dybb_ref, dybn_ref, n_tiles)
        xb, xb_before, xb_after = _halo_load(xb_ref, xbb_ref, xbn_ref, n_tiles)
        gb, gb_before, gb_after = _halo_load(gb_ref, gbb_ref, gbn_ref, n_tiles)
        gc, gc_before, gc_after = _halo_load(gc_ref, gcb_ref, gcn_ref, n_tiles)
        zb, zb_before, zb_after = _halo_load(zb_ref, zbb_ref, zbn_ref, n_tiles)
        w = w_ref[...]
        dya, za = dya_ref[...], za_ref[...]
        silu_za, dsilu_za = _silu_and_grad(za)
        dh_ref[...] = dya * silu_za
        dp_ref[:, 0:D_MODEL] = (dya * (hf_ref[...] + hb_ref[...]) * dsilu_za).astype(BF16)

        silu_zb, dsilu_zb = _silu_and_grad(zb)
        p, p_before, p_after = gc * xb, gc_before * xb_before, gc_after * xb_after
        cv = _sc_conv(p, p_before, p_after, w)
        dcv = dyb * gb * silu_zb
        dcv_before = dyb_before * gb_before * _silu_and_grad(zb_before)[0]
        dcv_after = dyb_after * gb_after * _silu_and_grad(zb_after)[0]
        dpp = (w[0:1, :] * _shift_rows(dcv, dcv_before, dcv_after, 1) + w[1:2, :] * dcv
               + w[2:3, :] * _shift_rows(dcv, dcv_before, dcv_after, -1))
        dp_ref[:, D_MODEL:2 * D_MODEL] = (dpp * gc).astype(BF16)
        dp_ref[:, 2 * D_MODEL:3 * D_MODEL] = (dyb * cv * silu_zb).astype(BF16)
        dp_ref[:, 3 * D_MODEL:4 * D_MODEL] = (dpp * xb).astype(BF16)
        dp_ref[:, 4 * D_MODEL:5 * D_MODEL] = (dyb * gb * cv * dsilu_zb).astype(BF16)

        @pl.when(pl.program_id(0) == 0)
        def _():
            dw_ref[...] = jnp.zeros_like(dw_ref)

        dw_ref[0:1, :] += jnp.sum(dcv * _shift_rows(p, p_before, p_after, -1), axis=0, keepdims=True)
        dw_ref[1:2, :] += jnp.sum(dcv * p, axis=0, keepdims=True)
        dw_ref[2:3, :] += jnp.sum(dcv * _shift_rows(p, p_before, p_after, 1), axis=0, keepdims=True)

    return pl.pallas_call(
        body, name="even_mix_bwd", grid=(n_tiles,),
        in_specs=[row] + _halo_specs(ts, s, D_MODEL, 1) + [row, row, pl.BlockSpec((ts, D_MODEL), lambda i: (i, 1))]
        + _halo_specs(ts, s, D_MODEL, 2) + _halo_specs(ts, s, D_MODEL, 3) + _halo_specs(ts, s, D_MODEL, 4)
        + _halo_specs(ts, s, D_MODEL, 5) + [_full(sc_w.shape)],
        out_specs=[row, pl.BlockSpec((ts, 5 * D_MODEL), lambda i: (i, 0)), _full(sc_w.shape)],
        out_shape=[jax.ShapeDtypeStruct((s, D_MODEL), F32), jax.ShapeDtypeStruct((s, 5 * D_MODEL), BF16),
                   jax.ShapeDtypeStruct(sc_w.shape, F32)],
        compiler_params=_params(1),
    )(du, du, du, du, hf, hb, proj, *([proj] * 12), sc_w)


def even_gates_bwd(proj, adj_f, adj_b, hf, hb, dh, conv_w, conv_b, gate_w, gate_b, lam):
    s = proj.shape[0]
    ts = min(ROW_TILE, s)
    n_tiles = s // ts
    row = pl.BlockSpec((ts, D_MODEL), lambda i: (i, 0))

    def body(xa_ref, xab_ref, xan_ref, af_ref, afb_ref, afn_ref, ab_ref, abb_ref, abn_ref,
             hf_ref, hfb_ref, hfn_ref, hb_ref, hbb_ref, hbn_ref, dh_ref,
             cw_ref, cb_ref, gw_ref, gb_ref, lam_ref, dua_ref, dgw_ref, dgb_ref, dlam_ref):
        @pl.when(pl.program_id(0) == 0)
        def _():
            dgw_ref[...] = jnp.zeros_like(dgw_ref)
            dgb_ref[...] = jnp.zeros_like(dgb_ref)
            dlam_ref[...] = jnp.zeros_like(dlam_ref)

        xa, before, after = _halo_load(xa_ref, xab_ref, xan_ref, n_tiles)
        ua = _rg_conv(xa, before, after, cw_ref[...], cb_ref[...])
        lam_v = lam_ref[...]
        c = RG_C * _softplus(-lam_v)
        dc_dlam = -RG_C * _sigmoid(-lam_v)
        gb = gb_ref[...]
        dh = dh_ref[...]
        adj = (_halo_load(af_ref, afb_ref, afn_ref, n_tiles), _halo_load(ab_ref, abb_ref, abn_ref, n_tiles))
        hs = (_halo_load(hf_ref, hfb_ref, hfn_ref, n_tiles), _halo_load(hb_ref, hbb_ref, hbn_ref, n_tiles))
        dua = jnp.zeros_like(ua)
        for direction in range(2):
            step = 1 if direction == 0 else -1
            g = dh + _shift_rows(*adj[direction], step)
            da_all = g * _shift_rows(*hs[direction], -step)
            dua_parts = []
            for head in range(RG_HEADS):
                lanes = slice(head * RG_HEAD_DIM, (head + 1) * RG_HEAD_DIM)
                ua_h = ua[:, lanes]
                c_h = c[direction:direction + 1, lanes]
                r, i, a, beta = _rg_gates(ua_h, gw_ref, gb, c_h, direction, head)
                db = g[:, lanes]
                d_i = db * beta * ua_h
                dbeta = db * (i * ua_h)
                dlog_a = (da_all[:, lanes] - dbeta * a / beta) * a
                dpr = -c_h * dlog_a * r * (1.0 - r)
                dpi = d_i * i * (1.0 - i)
                dua_parts.append(db * beta * i + _bdot_nt(dpr, gw_ref[2 * direction, head])
                                 + _bdot_nt(dpi, gw_ref[2 * direction + 1, head]))
                dgw_ref[2 * direction, head] += _bdot_tn(ua_h, dpr)
                dgw_ref[2 * direction + 1, head] += _bdot_tn(ua_h, dpi)
                dgb_ref[2 * direction:2 * direction + 1, lanes] += jnp.sum(dpr, axis=0, keepdims=True)
                dgb_ref[2 * direction + 1:2 * direction + 2, lanes] += jnp.sum(dpi, axis=0, keepdims=True)
                dlam_ref[direction:direction + 1, lanes] += (
                    jnp.sum(-r * dlog_a, axis=0, keepdims=True) * dc_dlam[direction:direction + 1, lanes])
            dua = dua + jnp.concatenate(dua_parts, axis=1)
        dua_ref[...] = dua

    return pl.pallas_call(
        body, name="even_gates_bwd", grid=(n_tiles,),
        in_specs=_halo_specs(ts, s, D_MODEL, 0) * 5 + [row] + [_full(conv_w.shape), _full(conv_b.shape), _full(gate_w.shape),
                                                             _full(gate_b.shape), _full(lam.shape)],
        out_specs=[row, _full(gate_w.shape), _full(gate_b.shape), _full(lam.shape)],
        out_shape=[jax.ShapeDtypeStruct((s, D_MODEL), F32), jax.ShapeDtypeStruct(gate_w.shape, F32),
                   jax.ShapeDtypeStruct(gate_b.shape, F32), jax.ShapeDtypeStruct(lam.shape, F32)],
        compiler_params=_params(1),
    )(proj, proj, proj, adj_f, adj_f, adj_f, adj_b, adj_b, adj_b, hf, hf, hf, hb, hb, hb, dh, conv_w, conv_b, gate_w, gate_b, lam)


def rg_conv_bwd(dua, proj, drest, conv_w):
    s = proj.shape[0]
    ts = min(ROW_TILE, s)
    n_tiles = s // ts

    def body(du_ref, dub_ref, dun_ref, xa_ref, xab_ref, xan_ref, dr_ref, cw_ref, dp_ref, dw_ref, db_ref):
        @pl.when(pl.program_id(0) == 0)
        def _():
            dw_ref[...] = jnp.zeros_like(dw_ref)
            db_ref[...] = jnp.zeros_like(db_ref)

        dua, dua_before, dua_after = _halo_load(du_ref, dub_ref, dun_ref, n_tiles)
        xa, xa_before, xa_after = _halo_load(xa_ref, xab_ref, xan_ref, n_tiles)
        cw = cw_ref[...]
        dxa = (cw[0:1, :] * _shift_rows(dua, dua_before, dua_after, 2) + cw[1:2, :] * _shift_rows(dua, dua_before, dua_after, 1)
               + cw[2:3, :] * dua + cw[3:4, :] * _shift_rows(dua, dua_before, dua_after, -1))
        dp_ref[:, :D_MODEL] = dxa.astype(BF16)
        dp_ref[:, D_MODEL:] = dr_ref[...]
        for tap, offset in enumerate((-2, -1, 0, 1)):
            shifted = xa if offset == 0 else _shift_rows(xa, xa_before, xa_after, offset)
            dw_ref[tap:tap + 1, :] += jnp.sum(dua * shifted, axis=0, keepdims=True)
        db_ref[...] += jnp.sum(dua, axis=0, keepdims=True)

    return pl.pallas_call(
        body, name="rg_conv_bwd", grid=(n_tiles,),
        in_specs=_halo_specs(ts, s, D_MODEL, 0) * 2 + [pl.BlockSpec((ts, 5 * D_MODEL), lambda i: (i, 0)), _full(conv_w.shape)],
        out_specs=[pl.BlockSpec((ts, EVEN_IN), lambda i: (i, 0)), _full(conv_w.shape), _full((1, D_MODEL))],
        out_shape=[jax.ShapeDtypeStruct((s, EVEN_IN), BF16), jax.ShapeDtypeStruct(conv_w.shape, F32),
                   jax.ShapeDtypeStruct((1, D_MODEL), F32)],
        compiler_params=_params(1),
    )(dua, dua, dua, proj, proj, proj, drest, conv_w)


def _split3(x):
    x1 = x.astype(BF16)
    rest = x - x1.astype(F32)
    x2 = rest.astype(BF16)
    return x1, x2, (rest - x2.astype(F32)).astype(BF16)


def _chunk_sum_matrix(t, reverse, transpose):
    i = lax.broadcasted_iota(jnp.int32, (t, t), 0)
    j = lax.broadcasted_iota(jnp.int32, (t, t), 1)
    if transpose:
        i, j = j, i
    same = (i // GLA_CHUNK) == (j // GLA_CHUNK)
    return jnp.where(same & ((j >= i) if reverse else (j <= i)), 1.0, 0.0).astype(BF16)


def _exact_dot(m, x):
    return sum(jnp.dot(m, part, preferred_element_type=F32) for part in _split3(x))


def _causal_mask(reverse):
    i = lax.broadcasted_iota(jnp.int32, (GLA_CHUNK, GLA_CHUNK), 0)
    j = lax.broadcasted_iota(jnp.int32, (GLA_CHUNK, GLA_CHUNK), 1)
    return (j >= i) if reverse else (j <= i)


def _gla_gate(lr, wg, bg):
    z = _bdot(lr, wg) + bg
    log_alpha = (jnp.minimum(z, 0.0) - jnp.log(1.0 + jnp.exp(-jnp.abs(z)))) * (1.0 / GLA_NORMALIZER)
    return z, log_alpha


def _gla_chunk_terms(q, k, bcum, rows, reverse):
    bc = bcum[rows, :]
    edge = rows.start if reverse else rows.stop - 1
    btot = bcum[edge:edge + 1, :]
    e_pos, e_neg, e_st = jnp.exp(bc), jnp.exp(-bc), jnp.exp(btot - bc)
    qc, kc = q[rows, :], k[rows, :]
    return qc * (GLA_DK ** -0.5) * e_pos, kc * e_neg, kc * e_st, e_pos, e_neg, e_st, jnp.exp(btot)


def _gla_specs(t, n_tiles, reverse_order):
    def tile(i):
        return n_tiles - 1 - i if reverse_order else i

    return tile, [
        pl.BlockSpec((t, GLA_KEY), lambda i: (tile(i), 0)),
        pl.BlockSpec((t, GLA_KEY), lambda i: (tile(i), 1)),
        pl.BlockSpec((t, D_MODEL), lambda i: (tile(i), 1)),
        pl.BlockSpec((t, LANES), lambda i: (tile(i), (ODD_IN_PAD - LANES) // LANES)),
    ]


def gla_fwd(proj, wg, bg, reverse, o_other=None, gnorm=None):
    s = proj.shape[0]
    t = min(ROW_TILE, s)
    n_tiles = s // t
    n_chunks = t // GLA_CHUNK
    final = o_other is not None
    tile, specs = _gla_specs(t, n_tiles, reverse)

    def body(*refs):
        if final:
            q_ref, k_ref, v_ref, lr_ref, wg_ref, bg_ref, oo_ref, r_ref, gn_ref, osum_ref, u_ref, st_ref, state = refs
        else:
            q_ref, k_ref, v_ref, lr_ref, wg_ref, bg_ref, o_ref, st_ref, state = refs
            osum_ref = o_ref

        @pl.when(pl.program_id(0) == 0)
        def _():
            state[...] = jnp.zeros_like(state)

        _, log_alpha = _gla_gate(lr_ref[...], wg_ref[...], bg_ref[...])
        bcum = _exact_dot(_chunk_sum_matrix(t, reverse, False), log_alpha)
        q, k, v = q_ref[...], k_ref[...], v_ref[...]
        mask = _causal_mask(reverse)
        for c in (range(n_chunks - 1, -1, -1) if reverse else range(n_chunks)):
            rows = slice(c * GLA_CHUNK, (c + 1) * GLA_CHUNK)
            q_in, k_in, k_st, _, _, _, decay = _gla_chunk_terms(q, k, bcum, rows, reverse)
            for head in range(GLA_HEADS):
                kl = slice(head * GLA_DK, (head + 1) * GLA_DK)
                vl = slice(head * GLA_DV, (head + 1) * GLA_DV)
                s_prev = state[head]
                st_ref[c, head] = s_prev
                scores = jnp.where(mask, _bdot_nt(q_in[:, kl], k_in[:, kl]), 0.0)
                osum_ref[rows, vl] = _bdot(scores, v[rows, vl]) + _bdot_nt(q_in[:, kl], s_prev)
                state[head] = s_prev * decay[:, kl] + _bdot_tn(v[rows, vl], k_st[:, kl])
        if final:
            osum = osum_ref[...] + oo_ref[...]
            osum_ref[...] = osum
            silu_r, _ = _silu_and_grad(r_ref[...])
            gn = gn_ref[...]
            for head in range(GLA_HEADS):
                vl = slice(head * GLA_DV, (head + 1) * GLA_DV)
                u_ref[:, vl] = (_rms(osum[:, vl], gn[:, vl]) * silu_r[:, vl]).astype(BF16)

    row = pl.BlockSpec((t, D_MODEL), lambda i: (tile(i), 0))
    st_spec = pl.BlockSpec((n_chunks, GLA_HEADS, GLA_DV, GLA_DK), lambda i: (tile(i), 0, 0, 0))
    st_shape = jax.ShapeDtypeStruct((s // GLA_CHUNK, GLA_HEADS, GLA_DV, GLA_DK), F32)
    in_specs = specs + [_full(wg.shape), _full(bg.shape)]
    args = [proj, proj, proj, proj, wg, bg]
    if final:
        in_specs += [row, pl.BlockSpec((t, D_MODEL), lambda i: (tile(i), 2)), _full(gnorm.shape)]
        args += [o_other, proj, gnorm]
        out_specs = [row, row, st_spec]
        out_shape = [jax.ShapeDtypeStruct((s, D_MODEL), F32), jax.ShapeDtypeStruct((s, D_MODEL), BF16), st_shape]
    else:
        out_specs = [row, st_spec]
        out_shape = [jax.ShapeDtypeStruct((s, D_MODEL), F32), st_shape]
    return pl.pallas_call(
        body, name="gla_fwd_rev" if reverse else "gla_fwd", grid=(n_tiles,), in_specs=in_specs, out_specs=out_specs,
        out_shape=out_shape, scratch_shapes=[pltpu.VMEM((GLA_HEADS, GLA_DV, GLA_DK), F32)], compiler_params=_params(1),
    )(*args)


def gla_out_bwd(du, proj, osum, gnorm):
    s = proj.shape[0]
    ts = min(ROW_TILE, s)
    row = pl.BlockSpec((ts, D_MODEL), lambda i: (i, 0))

    def body(du_ref, r_ref, o_ref, gn_ref, do_ref, dr_ref, dgn_ref):
        @pl.when(pl.program_id(0) == 0)
        def _():
            dgn_ref[...] = jnp.zeros_like(dgn_ref)

        du, osum, gn = du_ref[...], o_ref[...], gn_ref[...]
        silu_r, dsilu_r = _silu_and_grad(r_ref[...])
        for head in range(GLA_HEADS):
            vl = slice(head * GLA_DV, (head + 1) * GLA_DV)
            o_h, g_h, du_h = osum[:, vl], gn[:, vl], du[:, vl]
            dr_ref[:, vl] = (du_h * _rms(o_h, g_h) * dsilu_r[:, vl]).astype(BF16)
            do_h, dg_h = _rms_bwd(o_h, g_h, du_h * silu_r[:, vl])
            do_ref[:, vl] = do_h
            dgn_ref[...] += dg_h

    return pl.pallas_call(
        body, name="gla_out_bwd", grid=(s // ts,),
        in_specs=[row, pl.BlockSpec((ts, D_MODEL), lambda i: (i, 2)), row, _full(gnorm.shape)],
        out_specs=[row, row, _full((1, GLA_DV))],
        out_shape=[jax.ShapeDtypeStruct((s, D_MODEL), F32), jax.ShapeDtypeStruct((s, D_MODEL), BF16),
                   jax.ShapeDtypeStruct((1, GLA_DV), F32)],
        compiler_params=_params(1),
    )(du, proj, osum, gnorm)


def gla_bwd(proj, wg, bg, do, states, reverse, first=None):
    s = proj.shape[0]
    t = min(ROW_TILE, s)
    n_tiles = s // t
    n_chunks = t // GLA_CHUNK
    final = first is not None
    tile, specs = _gla_specs(t, n_tiles, not reverse)

    def body(*refs):
        if final:
            (q_ref, k_ref, v_ref, lr_ref, wg_ref, bg_ref, do_ref, st_ref, dqkv1_ref, dlr1_ref, dr_ref,
             dp_ref, dwg_ref, dbg_ref, dstate, dqkv, dbc, dbt) = refs
        else:
            (q_ref, k_ref, v_ref, lr_ref, wg_ref, bg_ref, do_ref, st_ref,
             dqkv, dlr_ref, dwg_ref, dbg_ref, dstate, dbc, dbt) = refs

        @pl.when(pl.program_id(0) == 0)
        def _():
            dstate[...] = jnp.zeros_like(dstate)
            dwg_ref[...] = jnp.zeros_like(dwg_ref)
            dbg_ref[...] = jnp.zeros_like(dbg_ref)

        lr, wg_v = lr_ref[...], wg_ref[...]
        z, log_alpha = _gla_gate(lr, wg_v, bg_ref[...])
        bcum = _exact_dot(_chunk_sum_matrix(t, reverse, False), log_alpha)
        q, k, v, do_v = q_ref[...], k_ref[...], v_ref[...], do_ref[...]
        mask = _causal_mask(reverse)
        for c in (range(n_chunks) if reverse else range(n_chunks - 1, -1, -1)):
            rows = slice(c * GLA_CHUNK, (c + 1) * GLA_CHUNK)
            q_in, k_in, k_st, e_pos, e_neg, e_st, decay = _gla_chunk_terms(q, k, bcum, rows, reverse)
            dbtot_parts = []
            for head in range(GLA_HEADS):
                kl = slice(head * GLA_DK, (head + 1) * GLA_DK)
                vl = slice(head * GLA_DV, (head + 1) * GLA_DV)
                q_h, k_h, ks_h, v_h, do_h = q_in[:, kl], k_in[:, kl], k_st[:, kl], v[rows, vl], do_v[rows, vl]
                s_prev = st_ref[c, head]
                ds_next = dstate[head]
                scores = jnp.where(mask, _bdot_nt(q_h, k_h), 0.0)
                dscores = jnp.where(mask, _bdot_nt(do_h, v_h), 0.0)
                dq_in = _bdot(dscores, k_h) + _bdot(do_h, s_prev)
                dk_in = _bdot_tn(dscores, q_h)
                dk_st = _bdot(v_h, ds_next)
                dqkv[rows, 2 * GLA_KEY + head * GLA_DV:2 * GLA_KEY + (head + 1) * GLA_DV] = (
                    _bdot_tn(scores, do_h) + _bdot_nt(ks_h, ds_next))
                ddecay = jnp.sum(ds_next * s_prev, axis=0, keepdims=True)
                dstate[head] = ds_next * decay[:, kl] + _bdot_tn(do_h, q_h)
                dqkv[rows, kl] = dq_in * (GLA_DK ** -0.5) * e_pos[:, kl]
                dqkv[rows, GLA_KEY + head * GLA_DK:GLA_KEY + (head + 1) * GLA_DK] = dk_in * e_neg[:, kl] + dk_st * e_st[:, kl]
                dbc[rows, kl] = dq_in * q_h - dk_in * k_h - dk_st * ks_h
                dbtot_parts.append(jnp.sum(dk_st * ks_h, axis=0, keepdims=True) + ddecay * decay[:, kl])
            dbt[rows, :] = jnp.broadcast_to(jnp.concatenate(dbtot_parts, axis=1), (GLA_CHUNK, GLA_KEY))
        dlog_alpha = _exact_dot(_chunk_sum_matrix(t, reverse, True), dbc[...]) + dbt[...]
        dz = dlog_alpha * _sigmoid(-z) * (1.0 / GLA_NORMALIZER)
        dlr = _bdot_nt(dz, wg_v)
        dwg_ref[...] += _bdot_tn(lr, dz)
        dbg_ref[...] += jnp.sum(dz, axis=0, keepdims=True)
        if final:
            dp_ref[:, :2 * D_MODEL] = (dqkv[...] + dqkv1_ref[...]).astype(BF16)
            dp_ref[:, 2 * D_MODEL:3 * D_MODEL] = dr_ref[...]
            dp_ref[:, 3 * D_MODEL:] = (dlr + dlr1_ref[...]).astype(BF16)
        else:
            dlr_ref[...] = dlr

    row = pl.BlockSpec((t, D_MODEL), lambda i: (tile(i), 0))
    wide = pl.BlockSpec((t, 2 * D_MODEL), lambda i: (tile(i), 0))
    narrow = pl.BlockSpec((t, LANES), lambda i: (tile(i), 0))
    st_spec = pl.BlockSpec((n_chunks, GLA_HEADS, GLA_DV, GLA_DK), lambda i: (tile(i), 0, 0, 0))
    in_specs = specs + [_full(wg.shape), _full(bg.shape), row, st_spec]
    args = [proj, proj, proj, proj, wg, bg, do, states]
    acc_specs = [_full(wg.shape), _full(bg.shape)]
    acc_shapes = [jax.ShapeDtypeStruct(wg.shape, F32), jax.ShapeDtypeStruct(bg.shape, F32)]
    scratch = [pltpu.VMEM((GLA_HEADS, GLA_DV, GLA_DK), F32)]
    work = [pltpu.VMEM((t, GLA_KEY), F32), pltpu.VMEM((t, GLA_KEY), F32)]
    if final:
        in_specs += [wide, narrow, row]
        args += list(first)
        out_specs = [pl.BlockSpec((t, ODD_IN_PAD), lambda i: (tile(i), 0))] + acc_specs
        out_shape = [jax.ShapeDtypeStruct((s, ODD_IN_PAD), BF16)] + acc_shapes
        scratch += [pltpu.VMEM((t, 2 * D_MODEL), F32)] + work
    else:
        out_specs = [wide, narrow] + acc_specs
        out_shape = [jax.ShapeDtypeStruct((s, 2 * D_MODEL), F32), jax.ShapeDtypeStruct((s, LANES), F32)] + acc_shapes
        scratch += work
    return pl.pallas_call(
        body, name="gla_bwd_rev" if reverse else "gla_bwd", grid=(n_tiles,), in_specs=in_specs, out_specs=out_specs,
        out_shape=out_shape, scratch_shapes=scratch, compiler_params=_params(1),
    )(*args)


def _mesh_position():
    x, y, c = lax.axis_index("x"), lax.axis_index("y"), lax.axis_index("c")
    return x, y, c, 4 * x + 2 * y + c


def _peer(x, y, c, mask):
    px, py, pc = x ^ (mask >> 2), y ^ ((mask >> 1) & 1), c ^ (mask & 1)
    return (px, py, pc), 4 * px + 2 * py + pc


def _exchange(body_copies, n_arrays, name, args, out_shape):
    n_in = len(args)

    def body(*refs):
        in_refs, out_refs = refs[:n_in], refs[n_in:n_in + len(out_shape)]
        send_sems, recv_sems, local_sems = refs[n_in + len(out_shape):]
        x, y, c, me = _mesh_position()
        started, arriving = [], []
        for arr, mask, src, dst, landing in body_copies((x, y, c, me), in_refs, out_refs):
            if mask == 0:
                cp = pltpu.make_async_copy(src, dst, local_sems.at[arr])
            else:
                peer, _ = _peer(x, y, c, mask)
                cp = pltpu.make_async_remote_copy(src_ref=src, dst_ref=dst, send_sem=send_sems.at[arr, mask - 1],
                                                  recv_sem=recv_sems.at[arr, mask - 1], device_id=peer, device_id_type=MESH_ID)
                arriving.append(pltpu.make_async_remote_copy(
                    src_ref=src, dst_ref=landing, send_sem=send_sems.at[arr, mask - 1], recv_sem=recv_sems.at[arr, mask - 1],
                    device_id=peer, device_id_type=MESH_ID))
            cp.start()
            started.append((mask, cp))
        for cp in arriving:
            cp.wait_recv()
        for mask, cp in started:
            if mask == 0:
                cp.wait()
            else:
                cp.wait_send()

    hbm = pl.BlockSpec(memory_space=pl.ANY)
    return pl.pallas_call(
        body, name=name, in_specs=[hbm] * n_in, out_specs=[hbm] * len(out_shape), out_shape=out_shape,
        scratch_shapes=[pltpu.SemaphoreType.DMA((n_arrays, N_DEV - 1)), pltpu.SemaphoreType.DMA((n_arrays, N_DEV - 1)),
                        pltpu.SemaphoreType.DMA((n_arrays,))],
    )(*args)


def gather_blocks(blocks, name, column_blocks=()):
    def copies(position, in_refs, out_refs):
        x, y, c, me = position
        for arr, (src, out) in enumerate(zip(in_refs, out_refs)):
            width = src.shape[-1]
            for mask in range(N_DEV):
                _, peer_id = _peer(x, y, c, mask)
                if arr in column_blocks:
                    mine = out.at[:, pl.ds(pl.multiple_of(me * width, LANES), width)]
                    theirs = out.at[:, pl.ds(pl.multiple_of(peer_id * width, LANES), width)]
                else:
                    mine, theirs = out.at[me], out.at[peer_id]
                yield arr, mask, src, mine, theirs

    out_shape = []
    for arr, b in enumerate(blocks):
        if arr in column_blocks:
            out_shape.append(jax.ShapeDtypeStruct((b.shape[0], N_DEV * b.shape[1]), b.dtype))
        else:
            out_shape.append(jax.ShapeDtypeStruct((N_DEV,) + b.shape, b.dtype))
    return _exchange(copies, len(blocks), name, list(blocks), out_shape)


def scatter_blocks(stacks, name, column_blocks=()):
    def copies(position, in_refs, out_refs):
        x, y, c, me = position
        for arr, (src, out) in enumerate(zip(in_refs, out_refs)):
            for mask in range(N_DEV):
                _, peer_id = _peer(x, y, c, mask)
                if arr in column_blocks:
                    width = src.shape[-1] // N_DEV
                    block = src.at[:, pl.ds(pl.multiple_of(peer_id * width, LANES), width)]
                else:
                    block = src.at[peer_id]
                yield arr, mask, block, out.at[me], out.at[peer_id]

    out_shape = []
    for arr, st in enumerate(stacks):
        if arr in column_blocks:
            out_shape.append(jax.ShapeDtypeStruct((N_DEV, st.shape[0], st.shape[1] // N_DEV), st.dtype))
        else:
            out_shape.append(jax.ShapeDtypeStruct(st.shape, st.dtype))
    return _exchange(copies, len(stacks), name, list(stacks), out_shape)


def sum_parts(parts, name):
    _, r, c = parts.shape

    def body(p_ref, o_ref):
        total = p_ref[0].astype(F32)
        for j in range(1, N_DEV):
            total = total + p_ref[j].astype(F32)
        o_ref[...] = total

    return pl.pallas_call(body, name=name, in_specs=[_full(parts.shape)], out_specs=_full((r, c)), grid=(1,),
                          out_shape=jax.ShapeDtypeStruct((r, c), F32), compiler_params=_params(1))(parts)


def adamw(parts, w, m, v, name):
    n, r, c = parts.shape
    tr = r if r <= MM_TILE else ROW_TILE

    def body(p_ref, w_ref, m_ref, v_ref, g_ref, d_ref, nm_ref, nv_ref):
        g = p_ref[0].astype(F32)
        for j in range(1, n):
            g = g + p_ref[j].astype(F32)
        new_m = ADAM_B1 * m_ref[...] + (1.0 - ADAM_B1) * g
        new_v = ADAM_B2 * v_ref[...] + (1.0 - ADAM_B2) * (g * g)
        m_hat = new_m / (1.0 - ADAM_B1 ** ADAM_STEP)
        v_hat = new_v / (1.0 - ADAM_B2 ** ADAM_STEP)
        g_ref[...] = g
        d_ref[...] = -ADAM_LR * (m_hat / (jnp.sqrt(v_hat) + ADAM_EPS) + ADAM_WD * w_ref[...])
        nm_ref[...] = new_m
        nv_ref[...] = new_v

    row = pl.BlockSpec((tr, c), lambda i: (i, 0))
    return pl.pallas_call(
        body, name=name, grid=(r // tr,),
        in_specs=[pl.BlockSpec((n, tr, c), lambda i: (0, i, 0)), row, row, row], out_specs=[row] * 4,
        out_shape=[jax.ShapeDtypeStruct((r, c), F32)] * 4, compiler_params=_params(1),
    )(parts, w, m, v)


SMALL_SHARDED = ("rg_conv_w", "rg_lambda", "sc_conv_w", "odd_norm_pre", "odd_norm_post", "gla_b_gate", "gla_norm_g", "gla_w_gate_lr")
SMALL_ROWS = {"rg_conv_w": (0, 4), "rg_lambda": (4, 2), "sc_conv_w": (6, 3), "odd_norm_pre": (9, 1), "odd_norm_post": (10, 1),
              "gla_b_gate": (11, 2), "gla_norm_g": (13, 1), "gla_w_gate_lr": (16, 32)}
REPLICATED_ROWS = {"even_norm_pre": (0, 1), "even_norm_post": (1, 1), "rg_conv_b": (2, 1), "rg_gate_b": (3, 4)}


def _pack_small(shards):
    out = jnp.zeros((PACK_ROWS, LANES), F32)
    for name in SMALL_SHARDED:
        start, rows = SMALL_ROWS[name]
        a = shards[name].reshape(rows, -1)
        out = out.at[start:start + rows, :a.shape[1]].set(a)
    return out


def _unpack_small(pack, shapes):
    out = {}
    for name in SMALL_SHARDED:
        start, rows = SMALL_ROWS[name]
        width = 1
        for n in shapes[name]:
            width *= n
        width //= rows
        out[name] = pack[start:start + rows, :width].reshape(shapes[name])
    return out


def _unpack_gathered(g):
    def cols(name, width):
        start, rows = SMALL_ROWS[name]
        return jnp.transpose(g[:, start:start + rows, :width], (1, 0, 2)).reshape(rows, N_DEV * width)

    w_lr = cols("gla_w_gate_lr", GLA_KEY // N_DEV).reshape(2, GLA_RANK, GLA_KEY)
    return dict(rg_conv_w=cols("rg_conv_w", LANES), rg_lambda=cols("rg_lambda", LANES), sc_conv_w=cols("sc_conv_w", LANES),
                odd_norm_pre=cols("odd_norm_pre", LANES), odd_norm_post=cols("odd_norm_post", LANES),
                gla_b_gate=cols("gla_b_gate", GLA_KEY // N_DEV), gla_norm_g=cols("gla_norm_g", GLA_DV // N_DEV), gla_w_gate_lr=w_lr)


def _pack_small_grads(grads):
    out = jnp.zeros((N_DEV, PACK_ROWS, LANES), F32)
    for name in SMALL_SHARDED:
        start, rows = SMALL_ROWS[name]
        a = grads[name].reshape(rows, N_DEV, -1)
        out = out.at[:, start:start + rows, :a.shape[2]].set(jnp.transpose(a, (1, 0, 2)))
    return out


def _pack_replicated(vals):
    out = jnp.zeros((N_DEV, SUBLANES, LANES), F32)
    for name, (start, rows) in REPLICATED_ROWS.items():
        out = out.at[:, start:start + rows, :].set(vals[name].reshape(N_DEV, rows, LANES))
    return out


def _unpack_replicated(pack, shapes):
    return {name: pack[:, start:start + rows, :].reshape(shapes[name]) for name, (start, rows) in REPLICATED_ROWS.items()}


def kernel(x, even_norm_pre, even_norm_post, even_w_in, rg_conv_w, rg_conv_b, rg_gate_w, rg_gate_b, rg_lambda, sc_conv_w, even_w_out, odd_norm_pre, odd_norm_post, odd_w_in, gla_w_gate_lr, gla_b_gate, gla_norm_g, odd_w_out, loss_target, m_even_norm_pre, m_even_norm_post, m_even_w_in, m_rg_conv_w, m_rg_conv_b, m_rg_gate_w, m_rg_gate_b, m_rg_lambda, m_sc_conv_w, m_even_w_out, m_odd_norm_pre, m_odd_norm_post, m_odd_w_in, m_gla_w_gate_lr, m_gla_b_gate, m_gla_norm_g, m_odd_w_out, v_even_norm_pre, v_even_norm_post, v_even_w_in, v_rg_conv_w, v_rg_conv_b, v_rg_gate_w, v_rg_gate_b, v_rg_lambda, v_sc_conv_w, v_even_w_out, v_odd_norm_pre, v_odd_norm_post, v_odd_w_in, v_gla_w_gate_lr, v_gla_b_gate, v_gla_norm_g, v_odd_w_out):
    weights = dict(even_norm_pre=even_norm_pre, even_norm_post=even_norm_post, even_w_in=even_w_in, rg_conv_w=rg_conv_w,
                   rg_conv_b=rg_conv_b, rg_gate_w=rg_gate_w, rg_gate_b=rg_gate_b, rg_lambda=rg_lambda, sc_conv_w=sc_conv_w,
                   even_w_out=even_w_out, odd_norm_pre=odd_norm_pre, odd_norm_post=odd_norm_post, odd_w_in=odd_w_in,
                   gla_w_gate_lr=gla_w_gate_lr, gla_b_gate=gla_b_gate, gla_norm_g=gla_norm_g, odd_w_out=odd_w_out)
    m_in = dict(even_norm_pre=m_even_norm_pre, even_norm_post=m_even_norm_post, even_w_in=m_even_w_in, rg_conv_w=m_rg_conv_w,
                rg_conv_b=m_rg_conv_b, rg_gate_w=m_rg_gate_w, rg_gate_b=m_rg_gate_b, rg_lambda=m_rg_lambda, sc_conv_w=m_sc_conv_w,
                even_w_out=m_even_w_out, odd_norm_pre=m_odd_norm_pre, odd_norm_post=m_odd_norm_post, odd_w_in=m_odd_w_in,
                gla_w_gate_lr=m_gla_w_gate_lr, gla_b_gate=m_gla_b_gate, gla_norm_g=m_gla_norm_g, odd_w_out=m_odd_w_out)
    v_in = dict(even_norm_pre=v_even_norm_pre, even_norm_post=v_even_norm_post, even_w_in=v_even_w_in, rg_conv_w=v_rg_conv_w,
                rg_conv_b=v_rg_conv_b, rg_gate_w=v_rg_gate_w, rg_gate_b=v_rg_gate_b, rg_lambda=v_rg_lambda, sc_conv_w=v_sc_conv_w,
                even_w_out=v_even_w_out, odd_norm_pre=v_odd_norm_pre, odd_norm_post=v_odd_norm_post, odd_w_in=v_odd_w_in,
                gla_w_gate_lr=v_gla_w_gate_lr, gla_b_gate=v_gla_b_gate, gla_norm_g=v_gla_norm_g, odd_w_out=v_odd_w_out)
    names = list(weights)
    shapes = {n: weights[n].shape for n in names}
    xs = x[0]
    tgt = loss_target[0]

    small_pack = _pack_small({n: weights[n][0] for n in SMALL_SHARDED})
    w_in_e, w_out_e, w_in_o_blocks, w_out_o, small_all = gather_blocks(
        [even_w_in[0].astype(BF16), even_w_out[0].astype(BF16), odd_w_in[0].astype(BF16), odd_w_out[0].astype(BF16), small_pack],
        "gather_weights", column_blocks=(0,))
    w_out_e = w_out_e.reshape(2 * D_MODEL, D_MODEL)
    w_out_o = w_out_o.reshape(D_MODEL, D_MODEL)
    w_in_o = jnp.transpose(w_in_o_blocks, (1, 0, 2)).reshape(D_MODEL, ODD_IN)
    w_in_o = jnp.pad(w_in_o, ((0, 0), (0, ODD_IN_PAD - ODD_IN)))
    small = _unpack_gathered(small_all)
    gate_w = rg_gate_w[0].reshape(4, RG_HEADS, RG_HEAD_DIM, RG_HEAD_DIM).astype(BF16)
    gate_b = rg_gate_b[0].reshape(4, D_MODEL)
    conv_b = rg_conv_b
    wg_pad = [jnp.zeros((LANES, GLA_KEY), F32).at[GLA_RANK * d:GLA_RANK * (d + 1)].set(small["gla_w_gate_lr"][d]).astype(BF16)
              for d in range(2)]
    bg = [small["gla_b_gate"][d:d + 1] for d in range(2)]
    gnorm = jnp.tile(small["gla_norm_g"], (1, GLA_HEADS))

    proj_e, h_e = rms_matmul(xs, even_norm_pre, w_in_e, EVEN_SHARD, "even_in")
    ab = even_gates_fwd(proj_e, small["rg_conv_w"], conv_b, gate_w, gate_b, small["rg_lambda"])
    hf = linear_scan(ab, 0, ab, 1, False, False, "scan_fwd")
    hb = linear_scan(ab, 2, ab, 3, True, False, "scan_rev")
    u_e = even_mix_fwd(hf, hb, proj_e, small["sc_conv_w"])
    y_e, x1 = matmul_post(u_e, w_out_e, xs, even_norm_post, "even_out")

    proj_o, h_o = rms_matmul(x1, small["odd_norm_pre"], w_in_o, ODD_IN_PAD // 5, "odd_in")
    o_f, st_f = gla_fwd(proj_o, wg_pad[0], bg[0], False)
    osum, u_o, st_b = gla_fwd(proj_o, wg_pad[1], bg[1], True, o_other=o_f, gnorm=gnorm)
    y_o, dout, loss_part = matmul_post(u_o, w_out_o, x1, small["odd_norm_post"], "odd_out", target=tgt)
    loss = lax.psum(loss_part[0, 0], ("x", "y", "c"))

    du_o, dy_o, d_odd_norm_post = normbwd_matmul_nt(y_o, small["odd_norm_post"], dout, w_out_o, D_MODEL, "odd_out_bwd")
    d_w_out_o = matmul_tn(u_o, dy_o, D_MODEL, D_MODEL, BF16, "odd_w_out_grad")
    do, dr, d_gnorm = gla_out_bwd(du_o, proj_o, osum, gnorm)
    dqkv_f, dlr_f, dwg_f, dbg_f = gla_bwd(proj_o, wg_pad[0], bg[0], do, st_f, False)
    dproj_o, dwg_b, dbg_b = gla_bwd(proj_o, wg_pad[1], bg[1], do, st_b, True, first=(dqkv_f, dlr_f, dr))
    dx1, d_odd_norm_pre = matmul_nt_normbwd(dproj_o, w_in_o, x1, small["odd_norm_pre"], dout, ODD_IN_PAD // 5, "odd_in_bwd")
    d_w_in_o = matmul_tn(h_o, dproj_o, D_MODEL, ODD_IN_PAD // 5, BF16, "odd_w_in_grad")

    du_e, dy_e, d_even_norm_post = normbwd_matmul_nt(y_e, even_norm_post, dx1, w_out_e, D_MODEL, "even_out_bwd")
    d_w_out_e = matmul_tn(u_e, dy_e, D_MODEL, D_MODEL, BF16, "even_w_out_grad")
    dh, drest, d_sc_w = even_mix_bwd(du_e, hf, hb, proj_e, small["sc_conv_w"])
    dh3 = dh.reshape(1, *dh.shape)
    adj_f = linear_scan(ab, 0, dh3, 0, True, True, "scan_fwd_adjoint")
    adj_b = linear_scan(ab, 2, dh3, 0, False, True, "scan_rev_adjoint")
    dua, d_gate_w, d_gate_b, d_lam = even_gates_bwd(proj_e, adj_f, adj_b, hf, hb, dh, small["rg_conv_w"], conv_b, gate_w, gate_b,
                                                    small["rg_lambda"])
    dproj_e, d_conv_w, d_conv_b = rg_conv_bwd(dua, proj_e, drest, small["rg_conv_w"])
    grad_x, d_even_norm_pre = matmul_nt_normbwd(dproj_e, w_in_e, xs, even_norm_pre, dx1, D_MODEL, "even_in_bwd")
    d_w_in_e = matmul_tn(h_e, dproj_e, D_MODEL, D_MODEL, BF16, "even_w_in_grad")

    d_w_in_o_blocks = jnp.transpose(d_w_in_o[:, :ODD_IN].reshape(D_MODEL, N_DEV, ODD_SHARD), (1, 0, 2))
    small_grads = _pack_small_grads(dict(
        rg_conv_w=d_conv_w, rg_lambda=d_lam, sc_conv_w=d_sc_w, odd_norm_pre=d_odd_norm_pre, odd_norm_post=d_odd_norm_post,
        gla_b_gate=jnp.concatenate([dbg_f, dbg_b], axis=0), gla_norm_g=d_gnorm,
        gla_w_gate_lr=jnp.concatenate([dwg_f[:GLA_RANK], dwg_b[GLA_RANK:2 * GLA_RANK]], axis=0)))
    rep_grads = _pack_replicated(dict(even_norm_pre=d_even_norm_pre, even_norm_post=d_even_norm_post, rg_conv_b=d_conv_b,
                                      rg_gate_b=d_gate_b))
    gate_w_rows = 4 * RG_HEADS * RG_HEAD_DIM
    parts = scatter_blocks(
        [d_w_in_e, d_w_out_e.reshape(N_DEV, 2 * D_MODEL // N_DEV, D_MODEL), d_w_in_o_blocks,
         d_w_out_o.reshape(N_DEV, D_MODEL // N_DEV, D_MODEL), small_grads,
         d_gate_w.reshape(N_DEV, gate_w_rows // N_DEV, RG_HEAD_DIM), rep_grads],
        "scatter_grads", column_blocks=(0,))
    p_w_in_e, p_w_out_e, p_w_in_o, p_w_out_o, p_small, p_gate_w, p_rep = parts
    g_gate_w_all, g_rep_all = gather_blocks([sum_parts(p_gate_w, "sum_gate_w"), sum_parts(p_rep, "sum_replicated")], "gather_reduced")

    results = {}

    def update(name, parts_, shape2d):
        outs = adamw(parts_, weights[name][0].reshape(shape2d), m_in[name][0].reshape(shape2d), v_in[name][0].reshape(shape2d),
                     "adamw_" + name)
        results[name] = [o.reshape(shapes[name]) for o in outs]

    update("even_w_in", p_w_in_e, (D_MODEL, EVEN_SHARD))
    update("even_w_out", p_w_out_e, (2 * D_MODEL // N_DEV, D_MODEL))
    update("odd_w_in", p_w_in_o, (D_MODEL, ODD_SHARD))
    update("odd_w_out", p_w_out_o, (D_MODEL // N_DEV, D_MODEL))
    update("rg_gate_w", g_gate_w_all.reshape(1, gate_w_rows, RG_HEAD_DIM), (gate_w_rows, RG_HEAD_DIM))

    small_out = adamw(p_small, *[_pack_small({n: src[n][0] for n in SMALL_SHARDED}) for src in (weights, m_in, v_in)], "adamw_small")
    small_out = [_unpack_small(o, shapes) for o in small_out]
    for n in SMALL_SHARDED:
        results[n] = [o[n] for o in small_out]
    rep_flat = g_rep_all.reshape(1, N_DEV * SUBLANES, LANES)
    rep_out = adamw(rep_flat, *[_pack_replicated({n: src[n] for n in REPLICATED_ROWS}).reshape(N_DEV * SUBLANES, LANES)
                                for src in (weights, m_in, v_in)], "adamw_replicated")
    rep_out = [_unpack_replicated(o.reshape(N_DEV, SUBLANES, LANES), shapes) for o in rep_out]
    for n in REPLICATED_ROWS:
        results[n] = [o[n] for o in rep_out]

    return (loss, grad_x.reshape(x.shape), *[results[n][0] for n in names], *[results[n][1] for n in names],
            *[results[n][2] for n in names], *[results[n][3] for n in names])
```

```python
import functools

import jax
import jax.numpy as jnp
from jax import lax
from jax.experimental import pallas as pl
from jax.experimental.pallas import tpu as pltpu

F32 = jnp.float32
BF16 = jnp.bfloat16

N_DEV = 8
D_MODEL = 1024
NORM_EPS = 1e-6
RG_HEADS = 8
RG_HEAD_DIM = 128
RG_C = 8.0
GLA_HEADS = 4
GLA_DK = 128
GLA_DV = 256
GLA_KEY = 512
GLA_RANK = 16
GLA_NORMALIZER = 16.0
GLA_CHUNK = 64
EVEN_IN = 6144
ODD_IN = 3104
ODD_IN_PAD = 3200
ODD_SHARD = ODD_IN // N_DEV
EVEN_SHARD = EVEN_IN // N_DEV
ADAM_LR = 0.001
ADAM_B1 = 0.9
ADAM_B2 = 0.999
ADAM_EPS = 1e-08
ADAM_WD = 0.01
ADAM_STEP = 10

SUBLANES = 8
LANES = 128
VMEM_LIMIT_BYTES = 48 * 2 ** 20
ROW_TILE = 256
MM_TILE = 512
PACK_ROWS = 48
MESH_ID = pl.DeviceIdType.MESH


def _params(n_grid):
    return pltpu.CompilerParams(dimension_semantics=("arbitrary",) * n_grid, vmem_limit_bytes=VMEM_LIMIT_BYTES)


def _bdot(a, b):
    return jnp.dot(a.astype(BF16), b.astype(BF16), preferred_element_type=F32)


def _bdot_nt(a, b):
    return lax.dot_general(a.astype(BF16), b.astype(BF16), (((1,), (1,)), ((), ())), preferred_element_type=F32)


def _bdot_tn(a, b):
    return lax.dot_general(a.astype(BF16), b.astype(BF16), (((0,), (0,)), ((), ())), preferred_element_type=F32)


def _rstd(x):
    return lax.rsqrt(jnp.mean(x * x, axis=-1, keepdims=True) + NORM_EPS)


def _rms(x, g):
    return x * _rstd(x) * g


def _rms_bwd(x, g, dy):
    xh = x * _rstd(x)
    dyg = dy * g
    dx = _rstd(x) * (dyg - xh * jnp.mean(dyg * xh, axis=-1, keepdims=True))
    return dx, jnp.sum(dy * xh, axis=0, keepdims=True)


def _sigmoid(z):
    return jax.nn.sigmoid(z)


def _silu_and_grad(z):
    s = _sigmoid(z)
    return z * s, s * (1.0 + z * (1.0 - s))


def _softplus(z):
    return jnp.maximum(z, 0.0) + jnp.log(1.0 + jnp.exp(-jnp.abs(z)))


def _shift_rows(cur, before, after, d):
    ts = cur.shape[0]
    row = lax.broadcasted_iota(jnp.int32, cur.shape, 0)
    out = pltpu.roll(cur, (-d) % ts, 0)
    if d < 0:
        for j in range(-d):
            out = jnp.where(row == j, before[SUBLANES + j + d:SUBLANES + j + d + 1, :], out)
    else:
        for j in range(d):
            out = jnp.where(row == ts - d + j, after[j:j + 1, :], out)
    return out


def _halo_specs(ts, s, width, col):
    per = ts // SUBLANES
    last = s // SUBLANES - 1
    return [
        pl.BlockSpec((ts, width), lambda i: (i, col)),
        pl.BlockSpec((SUBLANES, width), lambda i: (jnp.maximum(i * per - 1, 0), col)),
        pl.BlockSpec((SUBLANES, width), lambda i: (jnp.minimum((i + 1) * per, last), col)),
    ]


def _halo_load(cur_ref, before_ref, after_ref, n_tiles):
    i = pl.program_id(0)
    before = jnp.where(i > 0, before_ref[...], 0.0)
    after = jnp.where(i < n_tiles - 1, after_ref[...], 0.0)
    return cur_ref[...], before, after


def _full(shape):
    return pl.BlockSpec(shape, lambda *_: (0,) * len(shape))


def _peer(x, y, c, mask):
    px, py, pc = x ^ (mask >> 2), y ^ ((mask >> 1) & 1), c ^ (mask & 1)
    return (px, py, pc), 4 * px + 2 * py + pc


class Exchange:
    def __init__(self):
        self.args, self.out_shape, self._kinds = [], [], []

    def gather(self, block, columns=False):
        shape = (block.shape[0], N_DEV * block.shape[1]) if columns else (N_DEV,) + block.shape
        return self._add(block, shape, ("gather", columns))

    def scatter(self, stack, columns=False):
        shape = (N_DEV, stack.shape[0], stack.shape[1] // N_DEV) if columns else stack.shape
        return self._add(stack, shape, ("scatter", columns))

    def _add(self, arg, shape, kind):
        self.args.append(arg)
        self.out_shape.append(jax.ShapeDtypeStruct(shape, arg.dtype))
        self._kinds.append(kind)
        return len(self.args) - 1

    def semaphores(self):
        n = len(self.args)
        return [pltpu.SemaphoreType.DMA((n, N_DEV - 1)), pltpu.SemaphoreType.DMA((n, N_DEV - 1)), pltpu.SemaphoreType.DMA((n,))]

    def _copies(self, position, in_refs, out_refs):
        x, y, c, me = position
        for arr, ((kind, columns), src, out) in enumerate(zip(self._kinds, in_refs, out_refs)):
            for mask in range(N_DEV):
                _, peer_id = _peer(x, y, c, mask)
                if kind == "gather":
                    if columns:
                        width = src.shape[-1]
                        yield (arr, mask, src, out.at[:, pl.ds(pl.multiple_of(me * width, LANES), width)],
                               out.at[:, pl.ds(pl.multiple_of(peer_id * width, LANES), width)])
                    else:
                        yield arr, mask, src, out.at[me], out.at[peer_id]
                else:
                    if columns:
                        width = src.shape[-1] // N_DEV
                        block = src.at[:, pl.ds(pl.multiple_of(peer_id * width, LANES), width)]
                    else:
                        block = src.at[peer_id]
                    yield arr, mask, block, out.at[me], out.at[peer_id]

    def _descriptors(self, position, in_refs, out_refs, sems):
        send_sems, recv_sems, local_sems = sems
        x, y, c, _ = position
        for arr, mask, src, dst, landing in self._copies(position, in_refs, out_refs):
            if mask == 0:
                yield mask, pltpu.make_async_copy(src, dst, local_sems.at[arr]), None
            else:
                peer, _ = _peer(x, y, c, mask)
                pair = dict(send_sem=send_sems.at[arr, mask - 1], recv_sem=recv_sems.at[arr, mask - 1], device_id=peer,
                            device_id_type=MESH_ID)
                yield (mask, pltpu.make_async_remote_copy(src_ref=src, dst_ref=dst, **pair),
                       pltpu.make_async_remote_copy(src_ref=src, dst_ref=landing, **pair))

    def start(self, position, in_refs, out_refs, sems):
        for _, outgoing, _ in self._descriptors(position, in_refs, out_refs, sems):
            outgoing.start()

    def wait(self, position, in_refs, out_refs, sems):
        for mask, outgoing, incoming in self._descriptors(position, in_refs, out_refs, sems):
            if mask == 0:
                outgoing.wait()
            else:
                incoming.wait_recv()
                outgoing.wait_send()


def _call(body, *, name, grid, in_specs, out_specs, out_shape, args, scratch_shapes=(), exchange=None):
    single = not isinstance(out_shape, (list, tuple))
    if single:
        out_specs, out_shape = [out_specs], [out_shape]
    params = _params(len(grid))
    if exchange is None:
        outs = pl.pallas_call(body, name=name, grid=grid, in_specs=in_specs, out_specs=out_specs, out_shape=out_shape,
                              scratch_shapes=list(scratch_shapes), compiler_params=params)(*args)
        return outs[0] if single else outs
    counts = (len(args), len(exchange.args), len(out_shape), len(exchange.out_shape), len(scratch_shapes), 3)

    def wrapped(*refs):
        groups, at = [], 0
        for n in counts:
            groups.append(refs[at:at + n])
            at += n
        main_in, ex_in, main_out, ex_out, main_scratch, sems = groups
        x, y, c = lax.axis_index("x"), lax.axis_index("y"), lax.axis_index("c")
        position = (x, y, c, 4 * x + 2 * y + c)
        ids = [pl.program_id(a) for a in range(len(grid))]
        first = functools.reduce(jnp.logical_and, [i == 0 for i in ids])
        last = functools.reduce(jnp.logical_and, [i == g - 1 for i, g in zip(ids, grid)])

        @pl.when(first)
        def _():
            exchange.start(position, ex_in, ex_out, sems)

        body(*main_in, *main_out, *main_scratch)

        @pl.when(last)
        def _():
            exchange.wait(position, ex_in, ex_out, sems)

    hbm = pl.BlockSpec(memory_space=pl.ANY)
    outs = pl.pallas_call(
        wrapped, name=name, grid=grid, in_specs=list(in_specs) + [hbm] * counts[1], out_specs=list(out_specs) + [hbm] * counts[3],
        out_shape=list(out_shape) + exchange.out_shape, scratch_shapes=list(scratch_shapes) + exchange.semaphores(),
        compiler_params=params)(*args, *exchange.args)
    main = outs[:counts[2]]
    return (main[0] if single else main), outs[counts[2]:]


def run_exchange(exchange, name):
    return _call(lambda: None, name=name, grid=(1,), in_specs=[], out_specs=[], out_shape=[], args=[], exchange=exchange)[1]


def rms_matmul(x, g, w, tn, name, exchange=None):
    s, d = x.shape
    n = w.shape[1]
    tm = min(MM_TILE, s)

    def body(x_ref, g_ref, w_ref, o_ref, h_ref):
        @pl.when(pl.program_id(1) == 0)
        def _():
            h_ref[...] = _rms(x_ref[...], g_ref[...]).astype(BF16)

        o_ref[...] = jnp.dot(h_ref[...], w_ref[...], preferred_element_type=F32)

    return _call(
        body, name=name, grid=(s // tm, n // tn),
        in_specs=[pl.BlockSpec((tm, d), lambda i, j: (i, 0)), _full((1, d)), pl.BlockSpec((d, tn), lambda i, j: (0, j))],
        out_specs=[pl.BlockSpec((tm, tn), lambda i, j: (i, j)), pl.BlockSpec((tm, d), lambda i, j: (i, 0))],
        out_shape=[jax.ShapeDtypeStruct((s, n), F32), jax.ShapeDtypeStruct((s, d), BF16)],
        args=[x, g, w], exchange=exchange)


def matmul_post(u, w, xres, g, name, target=None):
    s, k = u.shape
    d = w.shape[1]
    tm = min(MM_TILE, s)
    with_loss = target is not None

    def body(*refs):
        if with_loss:
            u_ref, w_ref, x_ref, g_ref, t_ref, y_ref, dout_ref, loss_ref = refs
        else:
            u_ref, w_ref, x_ref, g_ref, y_ref, out_ref = refs
        y = jnp.dot(u_ref[...], w_ref[...], preferred_element_type=F32)
        y_ref[...] = y
        out = x_ref[...] + _rms(y, g_ref[...])
        if with_loss:
            @pl.when(pl.program_id(0) == 0)
            def _():
                loss_ref[...] = jnp.zeros_like(loss_ref)

            diff = out - t_ref[...]
            dout_ref[...] = diff * (1.0 / d)
            loss_ref[...] += 0.5 * jnp.sum(jnp.mean(diff * diff, axis=-1, keepdims=True))
        else:
            out_ref[...] = out

    row = pl.BlockSpec((tm, d), lambda i: (i, 0))
    in_specs = [pl.BlockSpec((tm, k), lambda i: (i, 0)), _full((k, d)), row, _full((1, d))]
    args = [u, w, xres, g]
    out_specs = [row, row]
    out_shape = [jax.ShapeDtypeStruct((s, d), F32), jax.ShapeDtypeStruct((s, d), F32)]
    if with_loss:
        in_specs.append(row)
        args.append(target)
        out_specs.append(_full((SUBLANES, LANES)))
        out_shape.append(jax.ShapeDtypeStruct((SUBLANES, LANES), F32))
    return pl.pallas_call(body, name=name, grid=(s // tm,), in_specs=in_specs, out_specs=out_specs,
                          out_shape=out_shape, compiler_params=_params(1))(*args)


def normbwd_matmul_nt(y, g, dout, w, tn, name, exchange=None):
    s, d = y.shape
    n = w.shape[0]
    tm = min(MM_TILE, s)

    def body(y_ref, g_ref, dout_ref, w_ref, du_ref, dy_ref, dg_ref):
        i, j = pl.program_id(0), pl.program_id(1)

        @pl.when(j == 0)
        def _():
            dy, dg = _rms_bwd(y_ref[...], g_ref[...], dout_ref[...])
            dy_ref[...] = dy.astype(BF16)

            @pl.when(i == 0)
            def _():
                dg_ref[...] = jnp.zeros_like(dg_ref)

            dg_ref[...] += dg

        du_ref[...] = lax.dot_general(dy_ref[...], w_ref[...], (((1,), (1,)), ((), ())), preferred_element_type=F32)

    row = pl.BlockSpec((tm, d), lambda i, j: (i, 0))
    return _call(
        body, name=name, grid=(s // tm, n // tn),
        in_specs=[row, _full((1, d)), row, pl.BlockSpec((tn, d), lambda i, j: (j, 0))],
        out_specs=[pl.BlockSpec((tm, tn), lambda i, j: (i, j)), row, _full((1, d))],
        out_shape=[jax.ShapeDtypeStruct((s, n), F32), jax.ShapeDtypeStruct((s, d), BF16), jax.ShapeDtypeStruct((1, d), F32)],
        args=[y, g, dout, w], exchange=exchange)


def matmul_tn(a, b, tm, tn, out_dtype, name, exchange=None):
    s, m = a.shape
    n = b.shape[1]
    ts = min(MM_TILE, s)
    n_k = s // ts

    def body(a_ref, b_ref, o_ref, acc):
        k = pl.program_id(2)

        @pl.when(k == 0)
        def _():
            acc[...] = jnp.zeros_like(acc)

        acc[...] += lax.dot_general(a_ref[...], b_ref[...], (((0,), (0,)), ((), ())), preferred_element_type=F32)

        @pl.when(k == n_k - 1)
        def _():
            o_ref[...] = acc[...].astype(out_dtype)

    return _call(
        body, name=name, grid=(m // tm, n // tn, n_k),
        in_specs=[pl.BlockSpec((ts, tm), lambda i, j, k: (k, i)), pl.BlockSpec((ts, tn), lambda i, j, k: (k, j))],
        out_specs=pl.BlockSpec((tm, tn), lambda i, j, k: (i, j)),
        out_shape=jax.ShapeDtypeStruct((m, n), out_dtype),
        scratch_shapes=[pltpu.VMEM((tm, tn), F32)], args=[a, b], exchange=exchange)


def matmul_nt_normbwd(dproj, w, x, g, dres, tk, name, exchange=None):
    s, kt = dproj.shape
    d = w.shape[0]
    tm = min(MM_TILE, s)
    n_k = kt // tk

    def body(a_ref, w_ref, x_ref, g_ref, r_ref, dx_ref, dg_ref, acc):
        i, k = pl.program_id(0), pl.program_id(1)

        @pl.when(k == 0)
        def _():
            acc[...] = jnp.zeros_like(acc)

        acc[...] += lax.dot_general(a_ref[...], w_ref[...], (((1,), (1,)), ((), ())), preferred_element_type=F32)

        @pl.when(k == n_k - 1)
        def _():
            dx, dg = _rms_bwd(x_ref[...], g_ref[...], acc[...])
            dx_ref[...] = r_ref[...] + dx

            @pl.when(i == 0)
            def _():
                dg_ref[...] = jnp.zeros_like(dg_ref)

            dg_ref[...] += dg

    row = pl.BlockSpec((tm, d), lambda i, k: (i, 0))
    return _call(
        body, name=name, grid=(s // tm, n_k),
        in_specs=[pl.BlockSpec((tm, tk), lambda i, k: (i, k)), pl.BlockSpec((d, tk), lambda i, k: (0, k)), row, _full((1, d)), row],
        out_specs=[row, _full((1, d))],
        out_shape=[jax.ShapeDtypeStruct((s, d), F32), jax.ShapeDtypeStruct((1, d), F32)],
        scratch_shapes=[pltpu.VMEM((tm, d), F32)], args=[dproj, w, x, g, dres], exchange=exchange)


def _rg_conv(xa, before, after, cw, cb):
    return (cw[0:1, :] * _shift_rows(xa, before, after, -2) + cw[1:2, :] * _shift_rows(xa, before, after, -1)
            + cw[2:3, :] * xa + cw[3:4, :] * _shift_rows(xa, before, after, 1) + cb)


def _rg_gates(ua_h, gw_ref, gb_ref, c_h, direction, head):
    r = _sigmoid(_bdot(ua_h, gw_ref[2 * direction, head]) + gb_ref[2 * direction, head:head + 1, :])
    i = _sigmoid(_bdot(ua_h, gw_ref[2 * direction + 1, head]) + gb_ref[2 * direction + 1, head:head + 1, :])
    log_a = -c_h * r
    a = jnp.exp(log_a)
    beta = jnp.sqrt(-jnp.tanh(log_a) * (1.0 + a * a))
    return r, i, a, beta


def even_gates_fwd(proj, conv_w, conv_b, gate_w, gate_b, lam, exchange=None):
    s = proj.shape[0]
    ts = min(ROW_TILE, s)
    n_tiles = s // ts

    def body(xa_ref, xb_ref, xn_ref, cw_ref, cb_ref, gw_ref, gb_ref, lam_ref, o_ref):
        xa, before, after = _halo_load(xa_ref, xb_ref, xn_ref, n_tiles)
        ua = _rg_conv(xa, before, after, cw_ref[...], cb_ref[...])
        c = RG_C * _softplus(-lam_ref[...])
        for direction in range(2):
            for head in range(RG_HEADS):
                lanes = slice(head * RG_HEAD_DIM, (head + 1) * RG_HEAD_DIM)
                ua_h = ua[:, lanes]
                _, i, a, beta = _rg_gates(ua_h, gw_ref, gb_ref, c[direction:direction + 1, lanes], direction, head)
                o_ref[2 * direction, :, lanes] = a
                o_ref[2 * direction + 1, :, lanes] = beta * (i * ua_h)

    return _call(
        body, name="even_gates_fwd", grid=(n_tiles,),
        in_specs=_halo_specs(ts, s, D_MODEL, 0) + [_full(conv_w.shape), _full(conv_b.shape), _full(gate_w.shape),
                                                   _full(gate_b.shape), _full(lam.shape)],
        out_specs=pl.BlockSpec((4, ts, D_MODEL), lambda i: (0, i, 0)),
        out_shape=jax.ShapeDtypeStruct((4, s, D_MODEL), F32),
        args=[proj, proj, proj, conv_w, conv_b, gate_w, gate_b, lam], exchange=exchange)


def linear_scan(a_arr, a_idx, b_arr, b_idx, reverse, b_times_a, name):
    _, s, c = a_arr.shape
    ts = min(MM_TILE, s)
    n_tiles = s // ts
    n_blocks = ts // SUBLANES

    def tile_of(i):
        return n_tiles - 1 - i if reverse else i

    def body(a_ref, b_ref, h_ref, carry):
        @pl.when(pl.program_id(0) == 0)
        def _():
            carry[...] = jnp.zeros_like(carry)

        row = lax.broadcasted_iota(jnp.int32, (SUBLANES, c), 0)

        def block(j, h_in):
            r0 = pl.multiple_of((n_blocks - 1 - j if reverse else j) * SUBLANES, SUBLANES)
            a = a_ref[pl.ds(r0, SUBLANES), :]
            b = b_ref[pl.ds(r0, SUBLANES), :]
            if b_times_a:
                b = a * b
            for step in (1, 2, 4):
                shift = SUBLANES - step if reverse else step
                valid = row < SUBLANES - step if reverse else row >= step
                b = jnp.where(valid, a * pltpu.roll(b, shift, 0) + b, b)
                a = jnp.where(valid, a * pltpu.roll(a, shift, 0), a)
            h = a * h_in + b
            h_ref[pl.ds(r0, SUBLANES), :] = h
            return h[0:1, :] if reverse else h[SUBLANES - 1:SUBLANES, :]

        carry[0:1, :] = lax.fori_loop(0, n_blocks, block, carry[0:1, :])

    return pl.pallas_call(
        body, name=name, grid=(n_tiles,),
        in_specs=[pl.BlockSpec((None, ts, c), lambda i: (a_idx, tile_of(i), 0)),
                  pl.BlockSpec((None, ts, c), lambda i: (b_idx, tile_of(i), 0))],
        out_specs=pl.BlockSpec((ts, c), lambda i: (tile_of(i), 0)),
        out_shape=jax.ShapeDtypeStruct((s, c), F32),
        scratch_shapes=[pltpu.VMEM((SUBLANES, c), F32)],
        compiler_params=_params(1),
    )(a_arr, b_arr)


def _sc_conv(p, before, after, w):
    return w[0:1, :] * _shift_rows(p, before, after, -1) + w[1:2, :] * p + w[2:3, :] * _shift_rows(p, before, after, 1)


def even_mix_fwd(hf, hb, proj, sc_w):
    s = proj.shape[0]
    ts = min(ROW_TILE, s)
    n_tiles = s // ts
    row = pl.BlockSpec((ts, D_MODEL), lambda i: (i, 0))

    def col(c):
        return pl.BlockSpec((ts, D_MODEL), lambda i: (i, c))

    def body(hf_ref, hb_ref, za_ref, xb_ref, xbb_ref, xbn_ref, gb_ref, gc_ref, gcb_ref, gcn_ref, zb_ref, w_ref, u_ref):
        xb, xb_before, xb_after = _halo_load(xb_ref, xbb_ref, xbn_ref, n_tiles)
        gc, gc_before, gc_after = _halo_load(gc_ref, gcb_ref, gcn_ref, n_tiles)
        silu_za, _ = _silu_and_grad(za_ref[...])
        silu_zb, _ = _silu_and_grad(zb_ref[...])
        u_ref[:, :D_MODEL] = ((hf_ref[...] + hb_ref[...]) * silu_za).astype(BF16)
        cv = _sc_conv(gc * xb, gc_before * xb_before, gc_after * xb_after, w_ref[...])
        u_ref[:, D_MODEL:] = (gb_ref[...] * cv * silu_zb).astype(BF16)

    return pl.pallas_call(
        body, name="even_mix_fwd", grid=(n_tiles,),
        in_specs=[row, row, col(1)] + _halo_specs(ts, s, D_MODEL, 2) + [col(3)] + _halo_specs(ts, s, D_MODEL, 4)
        + [col(5), _full(sc_w.shape)],
        out_specs=pl.BlockSpec((ts, 2 * D_MODEL), lambda i: (i, 0)),
        out_shape=jax.ShapeDtypeStruct((s, 2 * D_MODEL), BF16),
        compiler_params=_params(1),
    )(hf, hb, proj, proj, proj, proj, proj, proj, proj, proj, proj, sc_w)


def even_mix_bwd(du, hf, hb, proj, sc_w, exchange=None):
    s = proj.shape[0]
    ts = min(ROW_TILE, s)
    n_tiles = s // ts
    row = pl.BlockSpec((ts, D_MODEL), lambda i: (i, 0))

    def body(dya_ref, dyb_ref, dybb_ref, dybn_ref, hf_ref, hb_ref, za_ref, xb_ref, xbb_ref, xbn_ref,
             gb_ref, gbb_ref, gbn_ref, gc_ref, gcb_ref, gcn_ref, zb_ref, zbb_ref, zbn_ref, w_ref,
             dh_ref, dp_ref, dw_ref):
        dyb, dyb_before, dyb_after = _halo_load(dyb_ref, dybb_ref, dybn_ref, n_tiles)
        xb, xb_before, xb_after = _halo_load(xb_ref, xbb_ref, xbn_ref, n_tiles)
        gb, gb_before, gb_after = _halo_load(gb_ref, gbb_ref, gbn_ref, n_tiles)
        gc, gc_before, gc_after = _halo_load(gc_ref, gcb_ref, gcn_ref, n_tiles)
        zb, zb_before, zb_after = _halo_load(zb_ref, zbb_ref, zbn_ref, n_tiles)
        w = w_ref[...]
        dya, za = dya_ref[...], za_ref[...]
        silu_za, dsilu_za = _silu_and_grad(za)
        dh_ref[...] = dya * silu_za
        dp_ref[:, 0:D_MODEL] = (dya * (hf_ref[...] + hb_ref[...]) * dsilu_za).astype(BF16)

        silu_zb, dsilu_zb = _silu_and_grad(zb)
        p, p_before, p_after = gc * xb, gc_before * xb_before, gc_after * xb_after
        cv = _sc_conv(p, p_before, p_after, w)
        dcv = dyb * gb * silu_zb
        dcv_before = dyb_before * gb_before * _silu_and_grad(zb_before)[0]
        dcv_after = dyb_after * gb_after * _silu_and_grad(zb_after)[0]
        dpp = (w[0:1, :] * _shift_rows(dcv, dcv_before, dcv_after, 1) + w[1:2, :] * dcv
               + w[2:3, :] * _shift_rows(dcv, dcv_before, dcv_after, -1))
        dp_ref[:, D_MODEL:2 * D_MODEL] = (dpp * gc).astype(BF16)
        dp_ref[:, 2 * D_MODEL:3 * D_MODEL] = (dyb * cv * silu_zb).astype(BF16)
        dp_ref[:, 3 * D_MODEL:4 * D_MODEL] = (dpp * xb).astype(BF16)
        dp_ref[:, 4 * D_MODEL:5 * D_MODEL] = (dyb * gb * cv * dsilu_zb).astype(BF16)

        @pl.when(pl.program_id(0) == 0)
        def _():
            dw_ref[...] = jnp.zeros_like(dw_ref)

        dw_ref[0:1, :] += jnp.sum(dcv * _shift_rows(p, p_before, p_after, -1), axis=0, keepdims=True)
        dw_ref[1:2, :] += jnp.sum(dcv * p, axis=0, keepdims=True)
        dw_ref[2:3, :] += jnp.sum(dcv * _shift_rows(p, p_before, p_after, 1), axis=0, keepdims=True)

    return _call(
        body, name="even_mix_bwd", grid=(n_tiles,),
        in_specs=[row] + _halo_specs(ts, s, D_MODEL, 1) + [row, row, pl.BlockSpec((ts, D_MODEL), lambda i: (i, 1))]
        + _halo_specs(ts, s, D_MODEL, 2) + _halo_specs(ts, s, D_MODEL, 3) + _halo_specs(ts, s, D_MODEL, 4)
        + _halo_specs(ts, s, D_MODEL, 5) + [_full(sc_w.shape)],
        out_specs=[row, pl.BlockSpec((ts, 5 * D_MODEL), lambda i: (i, 0)), _full(sc_w.shape)],
        out_shape=[jax.ShapeDtypeStruct((s, D_MODEL), F32), jax.ShapeDtypeStruct((s, 5 * D_MODEL), BF16),
                   jax.ShapeDtypeStruct(sc_w.shape, F32)],
        args=[du, du, du, du, hf, hb, proj, *([proj] * 12), sc_w], exchange=exchange)


def even_gates_bwd(proj, adj_f, adj_b, hf, hb, dh, conv_w, conv_b, gate_w, gate_b, lam, exchange=None):
    s = proj.shape[0]
    ts = min(ROW_TILE, s)
    n_tiles = s // ts
    row = pl.BlockSpec((ts, D_MODEL), lambda i: (i, 0))

    def body(xa_ref, xab_ref, xan_ref, af_ref, afb_ref, afn_ref, ab_ref, abb_ref, abn_ref,
             hf_ref, hfb_ref, hfn_ref, hb_ref, hbb_ref, hbn_ref, dh_ref,
             cw_ref, cb_ref, gw_ref, gb_ref, lam_ref, dua_ref, dgw_ref, dgb_ref, dlam_ref):
        @pl.when(pl.program_id(0) == 0)
        def _():
            dgw_ref[...] = jnp.zeros_like(dgw_ref)
            dgb_ref[...] = jnp.zeros_like(dgb_ref)
            dlam_ref[...] = jnp.zeros_like(dlam_ref)

        xa, before, after = _halo_load(xa_ref, xab_ref, xan_ref, n_tiles)
        ua = _rg_conv(xa, before, after, cw_ref[...], cb_ref[...])
        lam_v = lam_ref[...]
        c = RG_C * _softplus(-lam_v)
        dc_dlam = -RG_C * _sigmoid(-lam_v)
        dh = dh_ref[...]
        adj = (_halo_load(af_ref, afb_ref, afn_ref, n_tiles), _halo_load(ab_ref, abb_ref, abn_ref, n_tiles))
        hs = (_halo_load(hf_ref, hfb_ref, hfn_ref, n_tiles), _halo_load(hb_ref, hbb_ref, hbn_ref, n_tiles))
        dua = jnp.zeros_like(ua)
        for direction in range(2):
            step = 1 if direction == 0 else -1
            g = dh + _shift_rows(*adj[direction], step)
            da_all = g * _shift_rows(*hs[direction], -step)
            dua_parts = []
            for head in range(RG_HEADS):
                lanes = slice(head * RG_HEAD_DIM, (head + 1) * RG_HEAD_DIM)
                ua_h = ua[:, lanes]
                c_h = c[direction:direction + 1, lanes]
                r, i, a, beta = _rg_gates(ua_h, gw_ref, gb_ref, c_h, direction, head)
                db = g[:, lanes]
                d_i = db * beta * ua_h
                dbeta = db * (i * ua_h)
                dlog_a = (da_all[:, lanes] - dbeta * a / beta) * a
                dpr = -c_h * dlog_a * r * (1.0 - r)
                dpi = d_i * i * (1.0 - i)
                dua_parts.append(db * beta * i + _bdot_nt(dpr, gw_ref[2 * direction, head])
                                 + _bdot_nt(dpi, gw_ref[2 * direction + 1, head]))
                dgw_ref[2 * direction, head] += _bdot_tn(ua_h, dpr)
                dgw_ref[2 * direction + 1, head] += _bdot_tn(ua_h, dpi)
                dgb_ref[2 * direction, head:head + 1, :] += jnp.sum(dpr, axis=0, keepdims=True)
                dgb_ref[2 * direction + 1, head:head + 1, :] += jnp.sum(dpi, axis=0, keepdims=True)
                dlam_ref[direction:direction + 1, lanes] += (
                    jnp.sum(-r * dlog_a, axis=0, keepdims=True) * dc_dlam[direction:direction + 1, lanes])
            dua = dua + jnp.concatenate(dua_parts, axis=1)
        dua_ref[...] = dua

    return _call(
        body, name="even_gates_bwd", grid=(n_tiles,),
        in_specs=_halo_specs(ts, s, D_MODEL, 0) * 5 + [row] + [_full(conv_w.shape), _full(conv_b.shape), _full(gate_w.shape),
                                                             _full(gate_b.shape), _full(lam.shape)],
        out_specs=[row, _full(gate_w.shape), _full(gate_b.shape), _full(lam.shape)],
        out_shape=[jax.ShapeDtypeStruct((s, D_MODEL), F32), jax.ShapeDtypeStruct(gate_w.shape, F32),
                   jax.ShapeDtypeStruct(gate_b.shape, F32), jax.ShapeDtypeStruct(lam.shape, F32)],
        args=[proj, proj, proj, adj_f, adj_f, adj_f, adj_b, adj_b, adj_b, hf, hf, hf, hb, hb, hb, dh, conv_w, conv_b, gate_w,
              gate_b, lam], exchange=exchange)


def rg_conv_bwd(dua, proj, drest, conv_w, exchange=None):
    s = proj.shape[0]
    ts = min(ROW_TILE, s)
    n_tiles = s // ts

    def body(du_ref, dub_ref, dun_ref, xa_ref, xab_ref, xan_ref, dr_ref, cw_ref, dp_ref, dw_ref, db_ref):
        @pl.when(pl.program_id(0) == 0)
        def _():
            dw_ref[...] = jnp.zeros_like(dw_ref)
            db_ref[...] = jnp.zeros_like(db_ref)

        dua, dua_before, dua_after = _halo_load(du_ref, dub_ref, dun_ref, n_tiles)
        xa, xa_before, xa_after = _halo_load(xa_ref, xab_ref, xan_ref, n_tiles)
        cw = cw_ref[...]
        dxa = (cw[0:1, :] * _shift_rows(dua, dua_before, dua_after, 2) + cw[1:2, :] * _shift_rows(dua, dua_before, dua_after, 1)
               + cw[2:3, :] * dua + cw[3:4, :] * _shift_rows(dua, dua_before, dua_after, -1))
        dp_ref[:, :D_MODEL] = dxa.astype(BF16)
        dp_ref[:, D_MODEL:] = dr_ref[...]
        for tap, offset in enumerate((-2, -1, 0, 1)):
            shifted = xa if offset == 0 else _shift_rows(xa, xa_before, xa_after, offset)
            dw_ref[tap:tap + 1, :] += jnp.sum(dua * shifted, axis=0, keepdims=True)
        db_ref[...] += jnp.sum(dua, axis=0, keepdims=True)

    return _call(
        body, name="rg_conv_bwd", grid=(n_tiles,),
        in_specs=_halo_specs(ts, s, D_MODEL, 0) * 2 + [pl.BlockSpec((ts, 5 * D_MODEL), lambda i: (i, 0)), _full(conv_w.shape)],
        out_specs=[pl.BlockSpec((ts, EVEN_IN), lambda i: (i, 0)), _full(conv_w.shape), _full((1, D_MODEL))],
        out_shape=[jax.ShapeDtypeStruct((s, EVEN_IN), BF16), jax.ShapeDtypeStruct(conv_w.shape, F32),
                   jax.ShapeDtypeStruct((1, D_MODEL), F32)],
        args=[dua, dua, dua, proj, proj, proj, drest, conv_w], exchange=exchange)


def _split3(x):
    x1 = x.astype(BF16)
    rest = x - x1.astype(F32)
    x2 = rest.astype(BF16)
    return x1, x2, (rest - x2.astype(F32)).astype(BF16)


def _chunk_sum_matrix(t, reverse, transpose):
    i = lax.broadcasted_iota(jnp.int32, (t, t), 0)
    j = lax.broadcasted_iota(jnp.int32, (t, t), 1)
    if transpose:
        i, j = j, i
    same = (i // GLA_CHUNK) == (j // GLA_CHUNK)
    return jnp.where(same & ((j >= i) if reverse else (j <= i)), 1.0, 0.0).astype(BF16)


def _exact_dot(m, x):
    return sum(jnp.dot(m, part, preferred_element_type=F32) for part in _split3(x))


def _causal_mask(reverse):
    i = lax.broadcasted_iota(jnp.int32, (GLA_CHUNK, GLA_CHUNK), 0)
    j = lax.broadcasted_iota(jnp.int32, (GLA_CHUNK, GLA_CHUNK), 1)
    return (j >= i) if reverse else (j <= i)


def _gla_gate(lr, wg, bg):
    z = _bdot(lr, wg) + bg
    log_alpha = (jnp.minimum(z, 0.0) - jnp.log(1.0 + jnp.exp(-jnp.abs(z)))) * (1.0 / GLA_NORMALIZER)
    return z, log_alpha


def _gla_chunk_terms(q, k, bcum, rows, reverse):
    bc = bcum[rows, :]
    edge = rows.start if reverse else rows.stop - 1
    btot = bcum[edge:edge + 1, :]
    e_pos, e_neg, e_st = jnp.exp(bc), jnp.exp(-bc), jnp.exp(btot - bc)
    qc, kc = q[rows, :], k[rows, :]
    return qc * (GLA_DK ** -0.5) * e_pos, kc * e_neg, kc * e_st, e_pos, e_neg, e_st, jnp.exp(btot)


def _gla_specs(t, n_tiles, reverse_order):
    def tile(i):
        return n_tiles - 1 - i if reverse_order else i

    return tile, [
        pl.BlockSpec((t, GLA_KEY), lambda i: (tile(i), 0)),
        pl.BlockSpec((t, GLA_KEY), lambda i: (tile(i), 1)),
        pl.BlockSpec((t, D_MODEL), lambda i: (tile(i), 1)),
        pl.BlockSpec((t, LANES), lambda i: (tile(i), (ODD_IN_PAD - LANES) // LANES)),
    ]


def gla_fwd(proj, wg, bg, reverse, o_other=None, gnorm=None):
    s = proj.shape[0]
    t = min(ROW_TILE, s)
    n_tiles = s // t
    n_chunks = t // GLA_CHUNK
    final = o_other is not None
    tile, specs = _gla_specs(t, n_tiles, reverse)

    def body(*refs):
        if final:
            q_ref, k_ref, v_ref, lr_ref, wg_ref, bg_ref, oo_ref, r_ref, gn_ref, osum_ref, u_ref, st_ref, state = refs
        else:
            q_ref, k_ref, v_ref, lr_ref, wg_ref, bg_ref, o_ref, st_ref, state = refs
            osum_ref = o_ref

        @pl.when(pl.program_id(0) == 0)
        def _():
            state[...] = jnp.zeros_like(state)

        _, log_alpha = _gla_gate(lr_ref[...], wg_ref[...], bg_ref[...])
        bcum = _exact_dot(_chunk_sum_matrix(t, reverse, False), log_alpha)
        q, k, v = q_ref[...], k_ref[...], v_ref[...]
        mask = _causal_mask(reverse)
        for c in (range(n_chunks - 1, -1, -1) if reverse else range(n_chunks)):
            rows = slice(c * GLA_CHUNK, (c + 1) * GLA_CHUNK)
            q_in, k_in, k_st, _, _, _, decay = _gla_chunk_terms(q, k, bcum, rows, reverse)
            for head in range(GLA_HEADS):
                kl = slice(head * GLA_DK, (head + 1) * GLA_DK)
                vl = slice(head * GLA_DV, (head + 1) * GLA_DV)
                s_prev = state[head]
                st_ref[c, head] = s_prev
                scores = jnp.where(mask, _bdot_nt(q_in[:, kl], k_in[:, kl]), 0.0)
                osum_ref[rows, vl] = _bdot(scores, v[rows, vl]) + _bdot_nt(q_in[:, kl], s_prev)
                state[head] = s_prev * decay[:, kl] + _bdot_tn(v[rows, vl], k_st[:, kl])
        if final:
            osum = osum_ref[...] + oo_ref[...]
            osum_ref[...] = osum
            silu_r, _ = _silu_and_grad(r_ref[...])
            gn = gn_ref[...]
            for head in range(GLA_HEADS):
                vl = slice(head * GLA_DV, (head + 1) * GLA_DV)
                u_ref[:, vl] = (_rms(osum[:, vl], gn[:, vl]) * silu_r[:, vl]).astype(BF16)

    row = pl.BlockSpec((t, D_MODEL), lambda i: (tile(i), 0))
    st_spec = pl.BlockSpec((n_chunks, GLA_HEADS, GLA_DV, GLA_DK), lambda i: (tile(i), 0, 0, 0))
    st_shape = jax.ShapeDtypeStruct((s // GLA_CHUNK, GLA_HEADS, GLA_DV, GLA_DK), F32)
    in_specs = specs + [_full(wg.shape), _full(bg.shape)]
    args = [proj, proj, proj, proj, wg, bg]
    if final:
        in_specs += [row, pl.BlockSpec((t, D_MODEL), lambda i: (tile(i), 2)), _full(gnorm.shape)]
        args += [o_other, proj, gnorm]
        out_specs = [row, row, st_spec]
        out_shape = [jax.ShapeDtypeStruct((s, D_MODEL), F32), jax.ShapeDtypeStruct((s, D_MODEL), BF16), st_shape]
    else:
        out_specs = [row, st_spec]
        out_shape = [jax.ShapeDtypeStruct((s, D_MODEL), F32), st_shape]
    return pl.pallas_call(
        body, name="gla_fwd_rev" if reverse else "gla_fwd", grid=(n_tiles,), in_specs=in_specs, out_specs=out_specs,
        out_shape=out_shape, scratch_shapes=[pltpu.VMEM((GLA_HEADS, GLA_DV, GLA_DK), F32)], compiler_params=_params(1),
    )(*args)


def gla_out_bwd(du, proj, osum, gnorm):
    s = proj.shape[0]
    ts = min(ROW_TILE, s)
    row = pl.BlockSpec((ts, D_MODEL), lambda i: (i, 0))

    def body(du_ref, r_ref, o_ref, gn_ref, do_ref, dr_ref, dgn_ref):
        @pl.when(pl.program_id(0) == 0)
        def _():
            dgn_ref[...] = jnp.zeros_like(dgn_ref)

        du, osum, gn = du_ref[...], o_ref[...], gn_ref[...]
        silu_r, dsilu_r = _silu_and_grad(r_ref[...])
        for head in range(GLA_HEADS):
            vl = slice(head * GLA_DV, (head + 1) * GLA_DV)
            o_h, g_h, du_h = osum[:, vl], gn[:, vl], du[:, vl]
            dr_ref[:, vl] = (du_h * _rms(o_h, g_h) * dsilu_r[:, vl]).astype(BF16)
            do_h, dg_h = _rms_bwd(o_h, g_h, du_h * silu_r[:, vl])
            do_ref[:, vl] = do_h
            dgn_ref[...] += dg_h

    return pl.pallas_call(
        body, name="gla_out_bwd", grid=(s // ts,),
        in_specs=[row, pl.BlockSpec((ts, D_MODEL), lambda i: (i, 2)), row, _full(gnorm.shape)],
        out_specs=[row, row, _full((1, GLA_DV))],
        out_shape=[jax.ShapeDtypeStruct((s, D_MODEL), F32), jax.ShapeDtypeStruct((s, D_MODEL), BF16),
                   jax.ShapeDtypeStruct((1, GLA_DV), F32)],
        compiler_params=_params(1),
    )(du, proj, osum, gnorm)


def gla_bwd(proj, wg, bg, do, states, reverse, first=None):
    s = proj.shape[0]
    t = min(ROW_TILE, s)
    n_tiles = s // t
    n_chunks = t // GLA_CHUNK
    final = first is not None
    tile, specs = _gla_specs(t, n_tiles, not reverse)

    def body(*refs):
        if final:
            (q_ref, k_ref, v_ref, lr_ref, wg_ref, bg_ref, do_ref, st_ref, dqkv1_ref, dlr1_ref, dr_ref,
             dp_ref, dwg_ref, dbg_ref, dstate, dqkv, dbc, dbt) = refs
        else:
            (q_ref, k_ref, v_ref, lr_ref, wg_ref, bg_ref, do_ref, st_ref,
             dqkv, dlr_ref, dwg_ref, dbg_ref, dstate, dbc, dbt) = refs

        @pl.when(pl.program_id(0) == 0)
        def _():
            dstate[...] = jnp.zeros_like(dstate)
            dwg_ref[...] = jnp.zeros_like(dwg_ref)
            dbg_ref[...] = jnp.zeros_like(dbg_ref)

        lr, wg_v = lr_ref[...], wg_ref[...]
        z, log_alpha = _gla_gate(lr, wg_v, bg_ref[...])
        bcum = _exact_dot(_chunk_sum_matrix(t, reverse, False), log_alpha)
        q, k, v, do_v = q_ref[...], k_ref[...], v_ref[...], do_ref[...]
        mask = _causal_mask(reverse)
        for c in (range(n_chunks) if reverse else range(n_chunks - 1, -1, -1)):
            rows = slice(c * GLA_CHUNK, (c + 1) * GLA_CHUNK)
            q_in, k_in, k_st, e_pos, e_neg, e_st, decay = _gla_chunk_terms(q, k, bcum, rows, reverse)
            dbtot_parts = []
            for head in range(GLA_HEADS):
                kl = slice(head * GLA_DK, (head + 1) * GLA_DK)
                vl = slice(head * GLA_DV, (head + 1) * GLA_DV)
                q_h, k_h, ks_h, v_h, do_h = q_in[:, kl], k_in[:, kl], k_st[:, kl], v[rows, vl], do_v[rows, vl]
                s_prev = st_ref[c, head]
                ds_next = dstate[head]
                scores = jnp.where(mask, _bdot_nt(q_h, k_h), 0.0)
                dscores = jnp.where(mask, _bdot_nt(do_h, v_h), 0.0)
                dq_in = _bdot(dscores, k_h) + _bdot(do_h, s_prev)
                dk_in = _bdot_tn(dscores, q_h)
                dk_st = _bdot(v_h, ds_next)
                dqkv[rows, 2 * GLA_KEY + head * GLA_DV:2 * GLA_KEY + (head + 1) * GLA_DV] = (
                    _bdot_tn(scores, do_h) + _bdot_nt(ks_h, ds_next))
                ddecay = jnp.sum(ds_next * s_prev, axis=0, keepdims=True)
                dstate[head] = ds_next * decay[:, kl] + _bdot_tn(do_h, q_h)
                dqkv[rows, kl] = dq_in * (GLA_DK ** -0.5) * e_pos[:, kl]
                dqkv[rows, GLA_KEY + head * GLA_DK:GLA_KEY + (head + 1) * GLA_DK] = dk_in * e_neg[:, kl] + dk_st * e_st[:, kl]
                dbc[rows, kl] = dq_in * q_h - dk_in * k_h - dk_st * ks_h
                dbtot_parts.append(jnp.sum(dk_st * ks_h, axis=0, keepdims=True) + ddecay * decay[:, kl])
            dbt[rows, :] = jnp.broadcast_to(jnp.concatenate(dbtot_parts, axis=1), (GLA_CHUNK, GLA_KEY))
        dlog_alpha = _exact_dot(_chunk_sum_matrix(t, reverse, True), dbc[...]) + dbt[...]
        dz = dlog_alpha * _sigmoid(-z) * (1.0 / GLA_NORMALIZER)
        dlr = _bdot_nt(dz, wg_v)
        dwg_ref[...] += _bdot_tn(lr, dz)
        dbg_ref[...] += jnp.sum(dz, axis=0, keepdims=True)
        if final:
            dp_ref[:, :2 * D_MODEL] = (dqkv[...] + dqkv1_ref[...]).astype(BF16)
            dp_ref[:, 2 * D_MODEL:3 * D_MODEL] = dr_ref[...]
            dp_ref[:, 3 * D_MODEL:] = (dlr + dlr1_ref[...]).astype(BF16)
        else:
            dlr_ref[...] = dlr

    row = pl.BlockSpec((t, D_MODEL), lambda i: (tile(i), 0))
    wide = pl.BlockSpec((t, 2 * D_MODEL), lambda i: (tile(i), 0))
    narrow = pl.BlockSpec((t, LANES), lambda i: (tile(i), 0))
    st_spec = pl.BlockSpec((n_chunks, GLA_HEADS, GLA_DV, GLA_DK), lambda i: (tile(i), 0, 0, 0))
    in_specs = specs + [_full(wg.shape), _full(bg.shape), row, st_spec]
    args = [proj, proj, proj, proj, wg, bg, do, states]
    acc_specs = [_full(wg.shape), _full(bg.shape)]
    acc_shapes = [jax.ShapeDtypeStruct(wg.shape, F32), jax.ShapeDtypeStruct(bg.shape, F32)]
    scratch = [pltpu.VMEM((GLA_HEADS, GLA_DV, GLA_DK), F32)]
    work = [pltpu.VMEM((t, GLA_KEY), F32), pltpu.VMEM((t, GLA_KEY), F32)]
    if final:
        in_specs += [wide, narrow, row]
        args += list(first)
        out_specs = [pl.BlockSpec((t, ODD_IN_PAD), lambda i: (tile(i), 0))] + acc_specs
        out_shape = [jax.ShapeDtypeStruct((s, ODD_IN_PAD), BF16)] + acc_shapes
        scratch += [pltpu.VMEM((t, 2 * D_MODEL), F32)] + work
    else:
        out_specs = [wide, narrow] + acc_specs
        out_shape = [jax.ShapeDtypeStruct((s, 2 * D_MODEL), F32), jax.ShapeDtypeStruct((s, LANES), F32)] + acc_shapes
        scratch += work
    return pl.pallas_call(
        body, name="gla_bwd_rev" if reverse else "gla_bwd", grid=(n_tiles,), in_specs=in_specs, out_specs=out_specs,
        out_shape=out_shape, scratch_shapes=scratch, compiler_params=_params(1),
    )(*args)


def _adamw_update(g, w, m, v):
    new_m = ADAM_B1 * m + (1.0 - ADAM_B1) * g
    new_v = ADAM_B2 * v + (1.0 - ADAM_B2) * (g * g)
    m_hat = new_m / (1.0 - ADAM_B1 ** ADAM_STEP)
    v_hat = new_v / (1.0 - ADAM_B2 ** ADAM_STEP)
    return -ADAM_LR * (m_hat / (jnp.sqrt(v_hat) + ADAM_EPS) + ADAM_WD * w), new_m, new_v


def sum_parts(parts, name):
    _, r, c = parts.shape

    def body(p_ref, o_ref):
        total = p_ref[0].astype(F32)
        for j in range(1, N_DEV):
            total = total + p_ref[j].astype(F32)
        o_ref[...] = total

    return pl.pallas_call(body, name=name, in_specs=[_full(parts.shape)], out_specs=_full((r, c)), grid=(1,),
                          out_shape=jax.ShapeDtypeStruct((r, c), F32), compiler_params=_params(1))(parts)


def adamw(parts, w, m, v, name):
    n, r, c = parts.shape
    tr = r if r <= MM_TILE else ROW_TILE

    def body(p_ref, w_ref, m_ref, v_ref, g_ref, d_ref, nm_ref, nv_ref):
        g = p_ref[0].astype(F32)
        for j in range(1, n):
            g = g + p_ref[j].astype(F32)
        g_ref[...] = g
        d_ref[...], nm_ref[...], nv_ref[...] = _adamw_update(g, w_ref[...], m_ref[...], v_ref[...])

    row = pl.BlockSpec((tr, c), lambda i: (i, 0))
    return pl.pallas_call(
        body, name=name, grid=(r // tr,),
        in_specs=[pl.BlockSpec((n, tr, c), lambda i: (0, i, 0)), row, row, row], out_specs=[row] * 4,
        out_shape=[jax.ShapeDtypeStruct((r, c), F32)] * 4, compiler_params=_params(1),
    )(parts, w, m, v)


def _small_views(shape):
    if len(shape) == 2:
        return [((slice(None), slice(None)), (slice(None), slice(None)))]
    if len(shape) == 3:
        return [((slice(None), slice(None)), (0,))]
    rows = shape[2]
    return [((slice(k * rows, (k + 1) * rows), slice(None)), (0, k)) for k in range(shape[1])]


def adamw_small(landings, w, m, v):
    names = list(landings)
    n = len(names)
    shapes = [w[name].shape for name in names]

    def body(*refs):
        land, ws, ms, vs = refs[:n], refs[n:2 * n], refs[2 * n:3 * n], refs[3 * n:4 * n]
        outs = [refs[(4 + k) * n:(5 + k) * n] for k in range(4)]
        for k in range(n):
            total = land[k][0]
            for j in range(1, N_DEV):
                total = total + land[k][j]
            for rows, at in _small_views(shapes[k]):
                g = total[rows]
                outs[0][k][at] = g
                outs[1][k][at], outs[2][k][at], outs[3][k][at] = _adamw_update(g, ws[k][at], ms[k][at], vs[k][at])

    blocks = [_full(sh) for sh in shapes]
    outs = pl.pallas_call(
        body, name="adamw_small", grid=(1,),
        in_specs=[_full(landings[name].shape) for name in names] + blocks * 3, out_specs=blocks * 4,
        out_shape=[jax.ShapeDtypeStruct(sh, F32) for sh in shapes] * 4, compiler_params=_params(1),
    )(*[landings[name] for name in names], *[src[name] for src in (w, m, v) for name in names])
    return [dict(zip(names, outs[k * n:(k + 1) * n])) for k in range(4)]


def adamw_replicated(land_vec, land_gate_b, land_loss, names, w, m, v, gate_b):
    n = len(names)

    def body(*refs):
        vec_ref, gb_ref, loss_ref = refs[:3]
        ws, ms, vs = refs[3:3 + n], refs[3 + n:3 + 2 * n], refs[3 + 2 * n:3 + 3 * n]
        gw_ref, gm_ref, gv_ref = refs[3 + 3 * n:6 + 3 * n]
        outs = refs[6 + 3 * n:]
        vec, gb, loss = vec_ref[0], gb_ref[0], loss_ref[0]
        for j in range(1, N_DEV):
            vec, gb, loss = vec + vec_ref[j], gb + gb_ref[j], loss + loss_ref[j]
        for k in range(n):
            g = vec[k:k + 1, :]
            outs[k][...] = g
            outs[n + k][...], outs[2 * n + k][...], outs[3 * n + k][...] = _adamw_update(g, ws[k][...], ms[k][...], vs[k][...])
        outs[4 * n][...] = gb
        outs[4 * n + 1][...], outs[4 * n + 2][...], outs[4 * n + 3][...] = _adamw_update(gb, gw_ref[...], gm_ref[...], gv_ref[...])
        outs[4 * n + 4][...] = loss

    vec_block, gb_block = _full((1, D_MODEL)), _full(gate_b[0].shape)
    outs = pl.pallas_call(
        body, name="adamw_replicated", grid=(1,),
        in_specs=[_full(land_vec.shape), _full(land_gate_b.shape), _full(land_loss.shape)] + [vec_block] * (3 * n) + [gb_block] * 3,
        out_specs=[vec_block] * (4 * n) + [gb_block] * 4 + [_full(land_loss.shape[1:])],
        out_shape=[jax.ShapeDtypeStruct((1, D_MODEL), F32)] * (4 * n) + [jax.ShapeDtypeStruct(gate_b[0].shape, F32)] * 4
        + [jax.ShapeDtypeStruct(land_loss.shape[1:], F32)],
        compiler_params=_params(1),
    )(land_vec, land_gate_b, land_loss, *[src[name] for src in (w, m, v) for name in names], *gate_b)
    results = {name: [outs[k * n + i] for k in range(4)] for i, name in enumerate(names)}
    return results, outs[4 * n:4 * n + 4], outs[4 * n + 4]


SMALL_SHARDED = ("rg_conv_w", "rg_lambda", "sc_conv_w", "odd_norm_pre", "odd_norm_post", "gla_b_gate", "gla_norm_g", "gla_w_gate_lr")
SMALL_ROWS = {"rg_conv_w": (0, 4), "rg_lambda": (4, 2), "sc_conv_w": (6, 3), "odd_norm_pre": (9, 1), "odd_norm_post": (10, 1),
              "gla_b_gate": (11, 2), "gla_norm_g": (13, 1), "gla_w_gate_lr": (16, 32)}


def _pack_small(shards):
    pieces, at = [], 0
    for name in SMALL_SHARDED:
        start, rows = SMALL_ROWS[name]
        if start > at:
            pieces.append(jnp.zeros((start - at, LANES), F32))
        a = shards[name].reshape(rows, -1)
        pieces.append(jnp.pad(a, ((0, 0), (0, LANES - a.shape[1]))))
        at = start + rows
    return jnp.concatenate(pieces, axis=0)


def _unpack_gathered(g):
    def cols(name, width):
        start, rows = SMALL_ROWS[name]
        return jnp.transpose(g[:, start:start + rows, :width], (1, 0, 2)).reshape(rows, N_DEV * width)

    w_lr = cols("gla_w_gate_lr", GLA_KEY // N_DEV).reshape(2, GLA_RANK, GLA_KEY)
    return dict(rg_conv_w=cols("rg_conv_w", LANES), rg_lambda=cols("rg_lambda", LANES), sc_conv_w=cols("sc_conv_w", LANES),
                odd_norm_pre=cols("odd_norm_pre", LANES), odd_norm_post=cols("odd_norm_post", LANES),
                gla_b_gate=cols("gla_b_gate", GLA_KEY // N_DEV), gla_norm_g=cols("gla_norm_g", GLA_DV // N_DEV), gla_w_gate_lr=w_lr)


def _blocks_along_columns(a, rows):
    return jnp.transpose(a.reshape(rows, N_DEV, -1), (1, 0, 2))


def kernel(x, even_norm_pre, even_norm_post, even_w_in, rg_conv_w, rg_conv_b, rg_gate_w, rg_gate_b, rg_lambda, sc_conv_w, even_w_out, odd_norm_pre, odd_norm_post, odd_w_in, gla_w_gate_lr, gla_b_gate, gla_norm_g, odd_w_out, loss_target, m_even_norm_pre, m_even_norm_post, m_even_w_in, m_rg_conv_w, m_rg_conv_b, m_rg_gate_w, m_rg_gate_b, m_rg_lambda, m_sc_conv_w, m_even_w_out, m_odd_norm_pre, m_odd_norm_post, m_odd_w_in, m_gla_w_gate_lr, m_gla_b_gate, m_gla_norm_g, m_odd_w_out, v_even_norm_pre, v_even_norm_post, v_even_w_in, v_rg_conv_w, v_rg_conv_b, v_rg_gate_w, v_rg_gate_b, v_rg_lambda, v_sc_conv_w, v_even_w_out, v_odd_norm_pre, v_odd_norm_post, v_odd_w_in, v_gla_w_gate_lr, v_gla_b_gate, v_gla_norm_g, v_odd_w_out):
    weights = dict(even_norm_pre=even_norm_pre, even_norm_post=even_norm_post, even_w_in=even_w_in, rg_conv_w=rg_conv_w,
                   rg_conv_b=rg_conv_b, rg_gate_w=rg_gate_w, rg_gate_b=rg_gate_b, rg_lambda=rg_lambda, sc_conv_w=sc_conv_w,
                   even_w_out=even_w_out, odd_norm_pre=odd_norm_pre, odd_norm_post=odd_norm_post, odd_w_in=odd_w_in,
                   gla_w_gate_lr=gla_w_gate_lr, gla_b_gate=gla_b_gate, gla_norm_g=gla_norm_g, odd_w_out=odd_w_out)
    m_in = dict(even_norm_pre=m_even_norm_pre, even_norm_post=m_even_norm_post, even_w_in=m_even_w_in, rg_conv_w=m_rg_conv_w,
                rg_conv_b=m_rg_conv_b, rg_gate_w=m_rg_gate_w, rg_gate_b=m_rg_gate_b, rg_lambda=m_rg_lambda, sc_conv_w=m_sc_conv_w,
                even_w_out=m_even_w_out, odd_norm_pre=m_odd_norm_pre, odd_norm_post=m_odd_norm_post, odd_w_in=m_odd_w_in,
                gla_w_gate_lr=m_gla_w_gate_lr, gla_b_gate=m_gla_b_gate, gla_norm_g=m_gla_norm_g, odd_w_out=m_odd_w_out)
    v_in = dict(even_norm_pre=v_even_norm_pre, even_norm_post=v_even_norm_post, even_w_in=v_even_w_in, rg_conv_w=v_rg_conv_w,
                rg_conv_b=v_rg_conv_b, rg_gate_w=v_rg_gate_w, rg_gate_b=v_rg_gate_b, rg_lambda=v_rg_lambda, sc_conv_w=v_sc_conv_w,
                even_w_out=v_even_w_out, odd_norm_pre=v_odd_norm_pre, odd_norm_post=v_odd_norm_post, odd_w_in=v_odd_w_in,
                gla_w_gate_lr=v_gla_w_gate_lr, gla_b_gate=v_gla_b_gate, gla_norm_g=v_gla_norm_g, odd_w_out=v_odd_w_out)
    names = list(weights)
    shapes = {n: weights[n].shape for n in names}
    xs = x[0]
    tgt = loss_target[0]

    first = Exchange()
    first.gather(even_w_in[0].astype(BF16), columns=True)
    first.gather(_pack_small({n: weights[n][0] for n in SMALL_SHARDED}))
    w_in_e, small_all = run_exchange(first, "gather_first")
    small = _unpack_gathered(small_all)
    gate_w = rg_gate_w[0].reshape(4, RG_HEADS, RG_HEAD_DIM, RG_HEAD_DIM).astype(BF16)
    gate_b = rg_gate_b[0].reshape(4, RG_HEADS, RG_HEAD_DIM)
    conv_b = rg_conv_b
    wg_pad = [jnp.pad(small["gla_w_gate_lr"][d], ((GLA_RANK * d, LANES - GLA_RANK * (d + 1)), (0, 0))).astype(BF16) for d in range(2)]
    bg = [small["gla_b_gate"][d:d + 1] for d in range(2)]
    gnorm = jnp.tile(small["gla_norm_g"], (1, GLA_HEADS))

    behind_in = Exchange()
    behind_in.gather(even_w_out[0].astype(BF16))
    behind_in.gather(odd_w_in[0].astype(BF16))
    (proj_e, h_e), (w_out_e, w_in_o_blocks) = rms_matmul(xs, even_norm_pre, w_in_e, EVEN_SHARD, "even_in", exchange=behind_in)
    w_out_e = w_out_e.reshape(2 * D_MODEL, D_MODEL)
    w_in_o = jnp.transpose(w_in_o_blocks, (1, 0, 2)).reshape(D_MODEL, ODD_IN)
    w_in_o = jnp.pad(w_in_o, ((0, 0), (0, ODD_IN_PAD - ODD_IN)))
    behind_gates = Exchange()
    behind_gates.gather(odd_w_out[0].astype(BF16))
    ab, (w_out_o,) = even_gates_fwd(proj_e, small["rg_conv_w"], conv_b, gate_w, gate_b, small["rg_lambda"], exchange=behind_gates)
    w_out_o = w_out_o.reshape(D_MODEL, D_MODEL)
    hf = linear_scan(ab, 0, ab, 1, False, False, "scan_fwd")
    hb = linear_scan(ab, 2, ab, 3, True, False, "scan_rev")
    u_e = even_mix_fwd(hf, hb, proj_e, small["sc_conv_w"])
    y_e, x1 = matmul_post(u_e, w_out_e, xs, even_norm_post, "even_out")

    proj_o, h_o = rms_matmul(x1, small["odd_norm_pre"], w_in_o, ODD_IN_PAD // 5, "odd_in")
    o_f, st_f = gla_fwd(proj_o, wg_pad[0], bg[0], False)
    osum, u_o, st_b = gla_fwd(proj_o, wg_pad[1], bg[1], True, o_other=o_f, gnorm=gnorm)
    y_o, dout, loss_part = matmul_post(u_o, w_out_o, x1, small["odd_norm_post"], "odd_out", target=tgt)

    du_o, dy_o, d_odd_norm_post = normbwd_matmul_nt(y_o, small["odd_norm_post"], dout, w_out_o, D_MODEL, "odd_out_bwd")
    d_w_out_o = matmul_tn(u_o, dy_o, D_MODEL, D_MODEL, BF16, "odd_w_out_grad")
    do, dr, d_gnorm = gla_out_bwd(du_o, proj_o, osum, gnorm)
    dqkv_f, dlr_f, dwg_f, dbg_f = gla_bwd(proj_o, wg_pad[0], bg[0], do, st_f, False)
    dproj_o, dwg_b, dbg_b = gla_bwd(proj_o, wg_pad[1], bg[1], do, st_b, True, first=(dqkv_f, dlr_f, dr))
    dx1, d_odd_norm_pre = matmul_nt_normbwd(dproj_o, w_in_o, x1, small["odd_norm_pre"], dout, ODD_IN_PAD // 5, "odd_in_bwd")
    d_w_in_o = matmul_tn(h_o, dproj_o, D_MODEL, ODD_IN_PAD // 5, BF16, "odd_w_in_grad")

    landed = {}
    behind_out = Exchange()
    behind_out.scatter(d_w_out_o.reshape(N_DEV, D_MODEL // N_DEV, D_MODEL))
    behind_out.scatter(d_odd_norm_pre, columns=True)
    behind_out.scatter(d_odd_norm_post, columns=True)
    behind_out.scatter(_blocks_along_columns(jnp.concatenate([dbg_f, dbg_b], axis=0), 2))
    behind_out.scatter(_blocks_along_columns(d_gnorm, 1))
    behind_out.scatter(_blocks_along_columns(jnp.concatenate([dwg_f[:GLA_RANK], dwg_b[GLA_RANK:2 * GLA_RANK]], axis=0), 2 * GLA_RANK))
    (du_e, dy_e, d_even_norm_post), got = normbwd_matmul_nt(y_e, even_norm_post, dx1, w_out_e, D_MODEL, "even_out_bwd",
                                                           exchange=behind_out)
    p_w_out_o = got[0]
    for n, part in zip(("odd_norm_pre", "odd_norm_post", "gla_b_gate", "gla_norm_g", "gla_w_gate_lr"), got[1:]):
        landed[n] = part
    d_w_out_e = matmul_tn(u_e, dy_e, D_MODEL, D_MODEL, BF16, "even_w_out_grad")
    behind_mix = Exchange()
    behind_mix.scatter(jnp.transpose(d_w_in_o[:, :ODD_IN].reshape(D_MODEL, N_DEV, ODD_SHARD), (1, 0, 2)))
    (dh, drest, d_sc_w), (p_w_in_o,) = even_mix_bwd(du_e, hf, hb, proj_e, small["sc_conv_w"], exchange=behind_mix)
    dh3 = dh.reshape(1, *dh.shape)
    adj_f = linear_scan(ab, 0, dh3, 0, True, True, "scan_fwd_adjoint")
    adj_b = linear_scan(ab, 2, dh3, 0, False, True, "scan_rev_adjoint")
    behind_gates_bwd = Exchange()
    behind_gates_bwd.scatter(d_w_out_e.reshape(N_DEV, 2 * D_MODEL // N_DEV, D_MODEL))
    behind_gates_bwd.scatter(d_sc_w, columns=True)
    (dua, d_gate_w, d_gate_b, d_lam), (p_w_out_e, landed["sc_conv_w"]) = even_gates_bwd(
        proj_e, adj_f, adj_b, hf, hb, dh, small["rg_conv_w"], conv_b, gate_w, gate_b, small["rg_lambda"], exchange=behind_gates_bwd)
    gate_w_rows = 4 * RG_HEADS * RG_HEAD_DIM
    behind_conv = Exchange()
    behind_conv.scatter(d_gate_w.reshape(N_DEV, gate_w_rows // N_DEV, RG_HEAD_DIM))
    behind_conv.scatter(d_lam, columns=True)
    (dproj_e, d_conv_w, d_conv_b), (p_gate_w, landed["rg_lambda"]) = rg_conv_bwd(dua, proj_e, drest, small["rg_conv_w"],
                                                                                 exchange=behind_conv)
    behind_w_grad = Exchange()
    behind_w_grad.gather(sum_parts(p_gate_w, "sum_gate_w"))
    d_w_in_e, (g_gate_w_all,) = matmul_tn(h_e, dproj_e, D_MODEL, D_MODEL, BF16, "even_w_in_grad", exchange=behind_w_grad)
    behind_in_bwd = Exchange()
    behind_in_bwd.scatter(d_w_in_e, columns=True)
    behind_in_bwd.scatter(d_conv_w, columns=True)
    (grad_x, d_even_norm_pre), (p_w_in_e, landed["rg_conv_w"]) = matmul_nt_normbwd(
        dproj_e, w_in_e, xs, even_norm_pre, dx1, D_MODEL, "even_in_bwd", exchange=behind_in_bwd)
    last = Exchange()
    replicated_vecs = ("even_norm_pre", "even_norm_post", "rg_conv_b")
    last.gather(jnp.concatenate([d_even_norm_pre, d_even_norm_post, d_conv_b], axis=0))
    last.gather(d_gate_b.reshape(4 * RG_HEADS, RG_HEAD_DIM))
    last.gather(loss_part)
    land_vec, land_gate_b, land_loss = run_exchange(last, "gather_last")

    results = {}

    def update(name, parts_, shape2d):
        outs = adamw(parts_, weights[name][0].reshape(shape2d), m_in[name][0].reshape(shape2d), v_in[name][0].reshape(shape2d),
                     "adamw_" + name)
        results[name] = [o.reshape(shapes[name]) for o in outs]

    update("even_w_in", p_w_in_e, (D_MODEL, EVEN_SHARD))
    update("even_w_out", p_w_out_e, (2 * D_MODEL // N_DEV, D_MODEL))
    update("odd_w_in", p_w_in_o, (D_MODEL, ODD_SHARD))
    update("odd_w_out", p_w_out_o, (D_MODEL // N_DEV, D_MODEL))
    update("rg_gate_w", g_gate_w_all.reshape(1, gate_w_rows, RG_HEAD_DIM), (gate_w_rows, RG_HEAD_DIM))
    small_out = adamw_small({n: landed[n] for n in SMALL_SHARDED}, weights, m_in, v_in)
    for n in SMALL_SHARDED:
        results[n] = [o[n] for o in small_out]
    gate_b_shape = (4 * RG_HEADS, RG_HEAD_DIM)
    rep_out, gate_b_out, loss_all = adamw_replicated(land_vec, land_gate_b, land_loss, replicated_vecs, weights, m_in, v_in,
                                                     [src["rg_gate_b"].reshape(gate_b_shape) for src in (weights, m_in, v_in)])
    results.update(rep_out)
    results["rg_gate_b"] = [o.reshape(shapes["rg_gate_b"]) for o in gate_b_out]

    return (loss_all[0, 0], grad_x.reshape(x.shape), *[results[n][0] for n in names], *[results[n][1] for n in names],
            *[results[n][2] for n in names], *[results[n][3] for n in names])
```

```python
import functools

import jax
import jax.numpy as jnp
from jax import lax
from jax.experimental import pallas as pl
from jax.experimental.pallas import tpu as pltpu

F32 = jnp.float32
BF16 = jnp.bfloat16

N_DEV = 8
D_MODEL = 1024
NORM_EPS = 1e-6
RG_HEADS = 8
RG_HEAD_DIM = 128
RG_C = 8.0
GLA_HEADS = 4
GLA_DK = 128
GLA_DV = 256
GLA_KEY = 512
GLA_RANK = 16
GLA_NORMALIZER = 16.0
GLA_CHUNK = 64
EVEN_IN = 6144
ODD_IN = 3104
ODD_IN_PAD = 3200
ODD_SHARD = ODD_IN // N_DEV
EVEN_SHARD = EVEN_IN // N_DEV
ADAM_LR = 0.001
ADAM_B1 = 0.9
ADAM_B2 = 0.999
ADAM_EPS = 1e-08
ADAM_WD = 0.01
ADAM_STEP = 10

SUBLANES = 8
LANES = 128
VMEM_LIMIT_BYTES = 48 * 2 ** 20
ROW_TILE = 256
MM_TILE = 512
PACK_ROWS = 48
MESH_ID = pl.DeviceIdType.MESH


def _params(n_grid):
    return pltpu.CompilerParams(dimension_semantics=("arbitrary",) * n_grid, vmem_limit_bytes=VMEM_LIMIT_BYTES)


def _bdot(a, b):
    return jnp.dot(a.astype(BF16), b.astype(BF16), preferred_element_type=F32)


def _bdot_nt(a, b):
    return lax.dot_general(a.astype(BF16), b.astype(BF16), (((1,), (1,)), ((), ())), preferred_element_type=F32)


def _bdot_tn(a, b):
    return lax.dot_general(a.astype(BF16), b.astype(BF16), (((0,), (0,)), ((), ())), preferred_element_type=F32)


def _rstd(x):
    return lax.rsqrt(jnp.mean(x * x, axis=-1, keepdims=True) + NORM_EPS)


def _rms(x, g):
    return x * _rstd(x) * g


def _rms_bwd(x, g, dy):
    xh = x * _rstd(x)
    dyg = dy * g
    dx = _rstd(x) * (dyg - xh * jnp.mean(dyg * xh, axis=-1, keepdims=True))
    return dx, jnp.sum(dy * xh, axis=0, keepdims=True)


def _sigmoid(z):
    return jax.nn.sigmoid(z)


def _silu_and_grad(z):
    s = _sigmoid(z)
    return z * s, s * (1.0 + z * (1.0 - s))


def _softplus(z):
    return jnp.maximum(z, 0.0) + jnp.log(1.0 + jnp.exp(-jnp.abs(z)))


def _shift_rows(cur, before, after, d):
    ts = cur.shape[0]
    row = lax.broadcasted_iota(jnp.int32, cur.shape, 0)
    out = pltpu.roll(cur, (-d) % ts, 0)
    if d < 0:
        for j in range(-d):
            out = jnp.where(row == j, before[SUBLANES + j + d:SUBLANES + j + d + 1, :], out)
    else:
        for j in range(d):
            out = jnp.where(row == ts - d + j, after[j:j + 1, :], out)
    return out


def _halo_specs(ts, s, width, col):
    per = ts // SUBLANES
    last = s // SUBLANES - 1
    return [
        pl.BlockSpec((ts, width), lambda i: (i, col)),
        pl.BlockSpec((SUBLANES, width), lambda i: (jnp.maximum(i * per - 1, 0), col)),
        pl.BlockSpec((SUBLANES, width), lambda i: (jnp.minimum((i + 1) * per, last), col)),
    ]


def _halo_load(cur_ref, before_ref, after_ref, n_tiles):
    i = pl.program_id(0)
    before = jnp.where(i > 0, before_ref[...], 0.0)
    after = jnp.where(i < n_tiles - 1, after_ref[...], 0.0)
    return cur_ref[...], before, after


def _full(shape):
    return pl.BlockSpec(shape, lambda *_: (0,) * len(shape))


def _peer(x, y, c, mask):
    px, py, pc = x ^ (mask >> 2), y ^ ((mask >> 1) & 1), c ^ (mask & 1)
    return (px, py, pc), 4 * px + 2 * py + pc


class Exchange:
    SIBLING = 1
    OTHER_CHIPS = (2, 4, 6)

    def __init__(self):
        self.args, self.out_shape, self._kinds = [], [], []

    def gather(self, block, columns=False, via_sibling=False):
        shape = (block.shape[0], N_DEV * block.shape[1]) if columns else (N_DEV,) + block.shape
        return self._add(block, shape, ("gather", columns, via_sibling))

    def scatter(self, stack, columns=False):
        shape = (N_DEV, stack.shape[0], stack.shape[1] // N_DEV) if columns else stack.shape
        return self._add(stack, shape, ("scatter", columns, False))

    def _add(self, arg, shape, kind):
        self.args.append(arg)
        self.out_shape.append(jax.ShapeDtypeStruct(shape, arg.dtype))
        self._kinds.append(kind)
        return len(self.args) - 1

    def semaphores(self):
        n = len(self.args)
        return [pltpu.SemaphoreType.DMA((n, N_DEV - 1)), pltpu.SemaphoreType.DMA((n, N_DEV - 1)), pltpu.SemaphoreType.DMA((n,))]

    def _copies(self, position, in_refs, out_refs):
        x, y, c, me = position
        for arr, ((kind, columns, via_sibling), src, out) in enumerate(zip(self._kinds, in_refs, out_refs)):
            for mask in range(N_DEV):
                _, peer_id = _peer(x, y, c, mask)
                relayed = via_sibling and mask not in (0, self.SIBLING) + self.OTHER_CHIPS
                if kind == "gather":
                    if columns:
                        width = src.shape[-1]
                        yield (arr, mask, src, out.at[:, pl.ds(pl.multiple_of(me * width, LANES), width)],
                               out.at[:, pl.ds(pl.multiple_of(peer_id * width, LANES), width)], relayed)
                    else:
                        yield arr, mask, src, out.at[me], out.at[peer_id], relayed
                else:
                    if columns:
                        width = src.shape[-1] // N_DEV
                        block = src.at[:, pl.ds(pl.multiple_of(peer_id * width, LANES), width)]
                    else:
                        block = src.at[peer_id]
                    yield arr, mask, block, out.at[me], out.at[peer_id], False

    def _remote(self, position, sems, arr, mask, to_mask, src, dst):
        x, y, c, _ = position
        return pltpu.make_async_remote_copy(src_ref=src, dst_ref=dst, send_sem=sems[0].at[arr, mask - 1], recv_sem=sems[1].at[arr, mask - 1],
                                            device_id=_peer(x, y, c, to_mask)[0], device_id_type=MESH_ID)

    def start(self, position, in_refs, out_refs, sems):
        for arr, mask, src, dst, _, relayed in self._copies(position, in_refs, out_refs):
            if mask == 0:
                pltpu.make_async_copy(src, dst, sems[2].at[arr]).start()
            elif not relayed:
                self._remote(position, sems, arr, mask, mask, src, dst).start()

    def wait(self, position, in_refs, out_refs, sems):
        copies = list(self._copies(position, in_refs, out_refs))
        landings = {(arr, mask): landing for arr, mask, _, _, landing, _ in copies}
        passed_on = set()
        for arr, mask, src, _, landing, relayed in copies:
            if relayed:
                held = landings[arr, mask ^ self.SIBLING]
                self._remote(position, sems, arr, mask ^ self.SIBLING, mask ^ self.SIBLING, src, held).wait_recv()
                self._remote(position, sems, arr, mask, self.SIBLING, held, held).start()
                passed_on.add((arr, mask ^ self.SIBLING))
        for arr, mask, src, dst, landing, relayed in copies:
            if mask == 0:
                pltpu.make_async_copy(src, dst, sems[2].at[arr]).wait()
                continue
            if (arr, mask) not in passed_on:
                self._remote(position, sems, arr, mask, mask, src, landing).wait_recv()
            if relayed:
                held = landings[arr, mask ^ self.SIBLING]
                self._remote(position, sems, arr, mask, self.SIBLING, held, held).wait_send()
            else:
                self._remote(position, sems, arr, mask, mask, src, dst).wait_send()


def _call(body, *, name, grid, in_specs, out_specs, out_shape, args, scratch_shapes=(), exchange=None):
    single = not isinstance(out_shape, (list, tuple))
    if single:
        out_specs, out_shape = [out_specs], [out_shape]
    params = _params(len(grid))
    if exchange is None:
        outs = pl.pallas_call(body, name=name, grid=grid, in_specs=in_specs, out_specs=out_specs, out_shape=out_shape,
                              scratch_shapes=list(scratch_shapes), compiler_params=params)(*args)
        return outs[0] if single else outs
    counts = (len(args), len(exchange.args), len(out_shape), len(exchange.out_shape), len(scratch_shapes), 3)

    def wrapped(*refs):
        groups, at = [], 0
        for n in counts:
            groups.append(refs[at:at + n])
            at += n
        main_in, ex_in, main_out, ex_out, main_scratch, sems = groups
        x, y, c = lax.axis_index("x"), lax.axis_index("y"), lax.axis_index("c")
        position = (x, y, c, 4 * x + 2 * y + c)
        ids = [pl.program_id(a) for a in range(len(grid))]
        first = functools.reduce(jnp.logical_and, [i == 0 for i in ids])
        last = functools.reduce(jnp.logical_and, [i == g - 1 for i, g in zip(ids, grid)])

        @pl.when(first)
        def _():
            exchange.start(position, ex_in, ex_out, sems)

        body(*main_in, *main_out, *main_scratch)

        @pl.when(last)
        def _():
            exchange.wait(position, ex_in, ex_out, sems)

    hbm = pl.BlockSpec(memory_space=pl.ANY)
    outs = pl.pallas_call(
        wrapped, name=name, grid=grid, in_specs=list(in_specs) + [hbm] * counts[1], out_specs=list(out_specs) + [hbm] * counts[3],
        out_shape=list(out_shape) + exchange.out_shape, scratch_shapes=list(scratch_shapes) + exchange.semaphores(),
        compiler_params=params)(*args, *exchange.args)
    main = outs[:counts[2]]
    return (main[0] if single else main), outs[counts[2]:]


def run_exchange(exchange, name):
    return _call(lambda: None, name=name, grid=(1,), in_specs=[], out_specs=[], out_shape=[], args=[], exchange=exchange)[1]


def rms_matmul(x, g, w, tm, tn, name, exchange=None):
    s, d = x.shape
    n = w.shape[1]
    tm = min(tm, s)

    def body(x_ref, g_ref, w_ref, o_ref, h_ref):
        @pl.when(pl.program_id(1) == 0)
        def _():
            h_ref[...] = _rms(x_ref[...], g_ref[...]).astype(BF16)

        o_ref[...] = jnp.dot(h_ref[...], w_ref[...], preferred_element_type=F32)

    return _call(
        body, name=name, grid=(s // tm, n // tn),
        in_specs=[pl.BlockSpec((tm, d), lambda i, j: (i, 0)), _full((1, d)), pl.BlockSpec((d, tn), lambda i, j: (0, j))],
        out_specs=[pl.BlockSpec((tm, tn), lambda i, j: (i, j)), pl.BlockSpec((tm, d), lambda i, j: (i, 0))],
        out_shape=[jax.ShapeDtypeStruct((s, n), F32), jax.ShapeDtypeStruct((s, d), BF16)],
        args=[x, g, w], exchange=exchange)


def matmul_post(u, w, xres, g, name, target=None):
    s, k = u.shape
    d = w.shape[1]
    tm = min(MM_TILE, s)
    with_loss = target is not None

    def body(*refs):
        if with_loss:
            u_ref, w_ref, x_ref, g_ref, t_ref, y_ref, dout_ref, loss_ref = refs
        else:
            u_ref, w_ref, x_ref, g_ref, y_ref, out_ref = refs
        y = jnp.dot(u_ref[...], w_ref[...], preferred_element_type=F32)
        y_ref[...] = y
        out = x_ref[...] + _rms(y, g_ref[...])
        if with_loss:
            @pl.when(pl.program_id(0) == 0)
            def _():
                loss_ref[...] = jnp.zeros_like(loss_ref)

            diff = out - t_ref[...]
            dout_ref[...] = diff * (1.0 / d)
            loss_ref[...] += 0.5 * jnp.sum(jnp.mean(diff * diff, axis=-1, keepdims=True))
        else:
            out_ref[...] = out

    row = pl.BlockSpec((tm, d), lambda i: (i, 0))
    in_specs = [pl.BlockSpec((tm, k), lambda i: (i, 0)), _full((k, d)), row, _full((1, d))]
    args = [u, w, xres, g]
    out_specs = [row, row]
    out_shape = [jax.ShapeDtypeStruct((s, d), F32), jax.ShapeDtypeStruct((s, d), F32)]
    if with_loss:
        in_specs.append(row)
        args.append(target)
        out_specs.append(_full((SUBLANES, LANES)))
        out_shape.append(jax.ShapeDtypeStruct((SUBLANES, LANES), F32))
    return pl.pallas_call(body, name=name, grid=(s // tm,), in_specs=in_specs, out_specs=out_specs,
                          out_shape=out_shape, compiler_params=_params(1))(*args)


def normbwd_matmul_nt(y, g, dout, w, tn, name, exchange=None):
    s, d = y.shape
    n = w.shape[0]
    tm = min(MM_TILE, s)

    def body(y_ref, g_ref, dout_ref, w_ref, du_ref, dy_ref, dg_ref):
        i, j = pl.program_id(0), pl.program_id(1)

        @pl.when(j == 0)
        def _():
            dy, dg = _rms_bwd(y_ref[...], g_ref[...], dout_ref[...])
            dy_ref[...] = dy.astype(BF16)

            @pl.when(i == 0)
            def _():
                dg_ref[...] = jnp.zeros_like(dg_ref)

            dg_ref[...] += dg

        du_ref[...] = lax.dot_general(dy_ref[...], w_ref[...], (((1,), (1,)), ((), ())), preferred_element_type=F32)

    row = pl.BlockSpec((tm, d), lambda i, j: (i, 0))
    return _call(
        body, name=name, grid=(s // tm, n // tn),
        in_specs=[row, _full((1, d)), row, pl.BlockSpec((tn, d), lambda i, j: (j, 0))],
        out_specs=[pl.BlockSpec((tm, tn), lambda i, j: (i, j)), row, _full((1, d))],
        out_shape=[jax.ShapeDtypeStruct((s, n), F32), jax.ShapeDtypeStruct((s, d), BF16), jax.ShapeDtypeStruct((1, d), F32)],
        args=[y, g, dout, w], exchange=exchange)


def matmul_tn(a, b, tm, tn, ts, out_dtype, name, exchange=None):
    s, m = a.shape
    n = b.shape[1]
    ts = min(ts, s)
    n_k = s // ts

    def body(a_ref, b_ref, o_ref, acc):
        k = pl.program_id(2)

        @pl.when(k == 0)
        def _():
            acc[...] = jnp.zeros_like(acc)

        acc[...] += lax.dot_general(a_ref[...], b_ref[...], (((0,), (0,)), ((), ())), preferred_element_type=F32)

        @pl.when(k == n_k - 1)
        def _():
            o_ref[...] = acc[...].astype(out_dtype)

    return _call(
        body, name=name, grid=(m // tm, n // tn, n_k),
        in_specs=[pl.BlockSpec((ts, tm), lambda i, j, k: (k, i)), pl.BlockSpec((ts, tn), lambda i, j, k: (k, j))],
        out_specs=pl.BlockSpec((tm, tn), lambda i, j, k: (i, j)),
        out_shape=jax.ShapeDtypeStruct((m, n), out_dtype),
        scratch_shapes=[pltpu.VMEM((tm, tn), F32)], args=[a, b], exchange=exchange)


def matmul_nt_normbwd(dproj, w, x, g, dres, tk, name, exchange=None):
    s, kt = dproj.shape
    d = w.shape[0]
    tm = min(MM_TILE, s)
    n_k = kt // tk

    def body(a_ref, w_ref, x_ref, g_ref, r_ref, dx_ref, dg_ref, acc):
        i, k = pl.program_id(0), pl.program_id(1)

        @pl.when(k == 0)
        def _():
            acc[...] = jnp.zeros_like(acc)

        acc[...] += lax.dot_general(a_ref[...], w_ref[...], (((1,), (1,)), ((), ())), preferred_element_type=F32)

        @pl.when(k == n_k - 1)
        def _():
            dx, dg = _rms_bwd(x_ref[...], g_ref[...], acc[...])
            dx_ref[...] = r_ref[...] + dx

            @pl.when(i == 0)
            def _():
                dg_ref[...] = jnp.zeros_like(dg_ref)

            dg_ref[...] += dg

    row = pl.BlockSpec((tm, d), lambda i, k: (i, 0))
    return _call(
        body, name=name, grid=(s // tm, n_k),
        in_specs=[pl.BlockSpec((tm, tk), lambda i, k: (i, k)), pl.BlockSpec((d, tk), lambda i, k: (0, k)), row, _full((1, d)), row],
        out_specs=[row, _full((1, d))],
        out_shape=[jax.ShapeDtypeStruct((s, d), F32), jax.ShapeDtypeStruct((1, d), F32)],
        scratch_shapes=[pltpu.VMEM((tm, d), F32)], args=[dproj, w, x, g, dres], exchange=exchange)


def _rg_conv(xa, before, after, cw, cb):
    return (cw[0:1, :] * _shift_rows(xa, before, after, -2) + cw[1:2, :] * _shift_rows(xa, before, after, -1)
            + cw[2:3, :] * xa + cw[3:4, :] * _shift_rows(xa, before, after, 1) + cb)


def _rg_gates(ua_h, gw_ref, gb_ref, c_h, direction, head):
    r = _sigmoid(_bdot(ua_h, gw_ref[2 * direction, head]) + gb_ref[2 * direction, head:head + 1, :])
    i = _sigmoid(_bdot(ua_h, gw_ref[2 * direction + 1, head]) + gb_ref[2 * direction + 1, head:head + 1, :])
    log_a = -c_h * r
    a = jnp.exp(log_a)
    beta = jnp.sqrt(-jnp.tanh(log_a) * (1.0 + a * a))
    return r, i, a, beta


def even_gates_fwd(proj, conv_w, conv_b, gate_w, gate_b, lam, exchange=None):
    s = proj.shape[0]
    ts = min(ROW_TILE, s)
    n_tiles = s // ts

    def body(xa_ref, xb_ref, xn_ref, cw_ref, cb_ref, gw_ref, gb_ref, lam_ref, o_ref):
        xa, before, after = _halo_load(xa_ref, xb_ref, xn_ref, n_tiles)
        ua = _rg_conv(xa, before, after, cw_ref[...], cb_ref[...])
        c = RG_C * _softplus(-lam_ref[...])
        for direction in range(2):
            for head in range(RG_HEADS):
                lanes = slice(head * RG_HEAD_DIM, (head + 1) * RG_HEAD_DIM)
                ua_h = ua[:, lanes]
                _, i, a, beta = _rg_gates(ua_h, gw_ref, gb_ref, c[direction:direction + 1, lanes], direction, head)
                o_ref[2 * direction, :, lanes] = a
                o_ref[2 * direction + 1, :, lanes] = beta * (i * ua_h)

    return _call(
        body, name="even_gates_fwd", grid=(n_tiles,),
        in_specs=_halo_specs(ts, s, D_MODEL, 0) + [_full(conv_w.shape), _full(conv_b.shape), _full(gate_w.shape),
                                                   _full(gate_b.shape), _full(lam.shape)],
        out_specs=pl.BlockSpec((4, ts, D_MODEL), lambda i: (0, i, 0)),
        out_shape=jax.ShapeDtypeStruct((4, s, D_MODEL), F32),
        args=[proj, proj, proj, conv_w, conv_b, gate_w, gate_b, lam], exchange=exchange)


def linear_scan(a_arr, a_idx, b_arr, b_idx, reverse, b_times_a, name):
    _, s, c = a_arr.shape
    ts = min(MM_TILE, s)
    n_tiles = s // ts
    n_blocks = ts // SUBLANES

    def tile_of(i):
        return n_tiles - 1 - i if reverse else i

    def body(a_ref, b_ref, h_ref, carry):
        @pl.when(pl.program_id(0) == 0)
        def _():
            carry[...] = jnp.zeros_like(carry)

        row = lax.broadcasted_iota(jnp.int32, (SUBLANES, c), 0)

        def block(j, h_in):
            r0 = pl.multiple_of((n_blocks - 1 - j if reverse else j) * SUBLANES, SUBLANES)
            a = a_ref[pl.ds(r0, SUBLANES), :]
            b = b_ref[pl.ds(r0, SUBLANES), :]
            if b_times_a:
                b = a * b
            for step in (1, 2, 4):
                shift = SUBLANES - step if reverse else step
                valid = row < SUBLANES - step if reverse else row >= step
                b = jnp.where(valid, a * pltpu.roll(b, shift, 0) + b, b)
                a = jnp.where(valid, a * pltpu.roll(a, shift, 0), a)
            h = a * h_in + b
            h_ref[pl.ds(r0, SUBLANES), :] = h
            return h[0:1, :] if reverse else h[SUBLANES - 1:SUBLANES, :]

        carry[0:1, :] = lax.fori_loop(0, n_blocks, block, carry[0:1, :])

    return pl.pallas_call(
        body, name=name, grid=(n_tiles,),
        in_specs=[pl.BlockSpec((None, ts, c), lambda i: (a_idx, tile_of(i), 0)),
                  pl.BlockSpec((None, ts, c), lambda i: (b_idx, tile_of(i), 0))],
        out_specs=pl.BlockSpec((ts, c), lambda i: (tile_of(i), 0)),
        out_shape=jax.ShapeDtypeStruct((s, c), F32),
        scratch_shapes=[pltpu.VMEM((SUBLANES, c), F32)],
        compiler_params=_params(1),
    )(a_arr, b_arr)


def _sc_conv(p, before, after, w):
    return w[0:1, :] * _shift_rows(p, before, after, -1) + w[1:2, :] * p + w[2:3, :] * _shift_rows(p, before, after, 1)


def even_mix_fwd(hf, hb, proj, sc_w):
    s = proj.shape[0]
    ts = min(ROW_TILE, s)
    n_tiles = s // ts
    row = pl.BlockSpec((ts, D_MODEL), lambda i: (i, 0))

    def col(c):
        return pl.BlockSpec((ts, D_MODEL), lambda i: (i, c))

    def body(hf_ref, hb_ref, za_ref, xb_ref, xbb_ref, xbn_ref, gb_ref, gc_ref, gcb_ref, gcn_ref, zb_ref, w_ref, u_ref):
        xb, xb_before, xb_after = _halo_load(xb_ref, xbb_ref, xbn_ref, n_tiles)
        gc, gc_before, gc_after = _halo_load(gc_ref, gcb_ref, gcn_ref, n_tiles)
        silu_za, _ = _silu_and_grad(za_ref[...])
        silu_zb, _ = _silu_and_grad(zb_ref[...])
        u_ref[:, :D_MODEL] = ((hf_ref[...] + hb_ref[...]) * silu_za).astype(BF16)
        cv = _sc_conv(gc * xb, gc_before * xb_before, gc_after * xb_after, w_ref[...])
        u_ref[:, D_MODEL:] = (gb_ref[...] * cv * silu_zb).astype(BF16)

    return pl.pallas_call(
        body, name="even_mix_fwd", grid=(n_tiles,),
        in_specs=[row, row, col(1)] + _halo_specs(ts, s, D_MODEL, 2) + [col(3)] + _halo_specs(ts, s, D_MODEL, 4)
        + [col(5), _full(sc_w.shape)],
        out_specs=pl.BlockSpec((ts, 2 * D_MODEL), lambda i: (i, 0)),
        out_shape=jax.ShapeDtypeStruct((s, 2 * D_MODEL), BF16),
        compiler_params=_params(1),
    )(hf, hb, proj, proj, proj, proj, proj, proj, proj, proj, proj, sc_w)


def even_mix_bwd(du, hf, hb, proj, sc_w, exchange=None):
    s = proj.shape[0]
    ts = min(ROW_TILE, s)
    n_tiles = s // ts
    row = pl.BlockSpec((ts, D_MODEL), lambda i: (i, 0))

    def body(dya_ref, dyb_ref, dybb_ref, dybn_ref, hf_ref, hb_ref, za_ref, xb_ref, xbb_ref, xbn_ref,
             gb_ref, gbb_ref, gbn_ref, gc_ref, gcb_ref, gcn_ref, zb_ref, zbb_ref, zbn_ref, w_ref,
             dh_ref, dp_ref, dw_ref):
        dyb, dyb_before, dyb_after = _halo_load(dyb_ref, dybb_ref, dybn_ref, n_tiles)
        xb, xb_before, xb_after = _halo_load(xb_ref, xbb_ref, xbn_ref, n_tiles)
        gb, gb_before, gb_after = _halo_load(gb_ref, gbb_ref, gbn_ref, n_tiles)
        gc, gc_before, gc_after = _halo_load(gc_ref, gcb_ref, gcn_ref, n_tiles)
        zb, zb_before, zb_after = _halo_load(zb_ref, zbb_ref, zbn_ref, n_tiles)
        w = w_ref[...]
        dya, za = dya_ref[...], za_ref[...]
        silu_za, dsilu_za = _silu_and_grad(za)
        dh_ref[...] = dya * silu_za
        dp_ref[:, 0:D_MODEL] = (dya * (hf_ref[...] + hb_ref[...]) * dsilu_za).astype(BF16)

        silu_zb, dsilu_zb = _silu_and_grad(zb)
        p, p_before, p_after = gc * xb, gc_before * xb_before, gc_after * xb_after
        cv = _sc_conv(p, p_before, p_after, w)
        dcv = dyb * gb * silu_zb
        dcv_before = dyb_before * gb_before * _silu_and_grad(zb_before)[0]
        dcv_after = dyb_after * gb_after * _silu_and_grad(zb_after)[0]
        dpp = (w[0:1, :] * _shift_rows(dcv, dcv_before, dcv_after, 1) + w[1:2, :] * dcv
               + w[2:3, :] * _shift_rows(dcv, dcv_before, dcv_after, -1))
        dp_ref[:, D_MODEL:2 * D_MODEL] = (dpp * gc).astype(BF16)
        dp_ref[:, 2 * D_MODEL:3 * D_MODEL] = (dyb * cv * silu_zb).astype(BF16)
        dp_ref[:, 3 * D_MODEL:4 * D_MODEL] = (dpp * xb).astype(BF16)
        dp_ref[:, 4 * D_MODEL:5 * D_MODEL] = (dyb * gb * cv * dsilu_zb).astype(BF16)

        @pl.when(pl.program_id(0) == 0)
        def _():
            dw_ref[...] = jnp.zeros_like(dw_ref)

        dw_ref[0:1, :] += jnp.sum(dcv * _shift_rows(p, p_before, p_after, -1), axis=0, keepdims=True)
        dw_ref[1:2, :] += jnp.sum(dcv * p, axis=0, keepdims=True)
        dw_ref[2:3, :] += jnp.sum(dcv * _shift_rows(p, p_before, p_after, 1), axis=0, keepdims=True)

    return _call(
        body, name="even_mix_bwd", grid=(n_tiles,),
        in_specs=[row] + _halo_specs(ts, s, D_MODEL, 1) + [row, row, pl.BlockSpec((ts, D_MODEL), lambda i: (i, 1))]
        + _halo_specs(ts, s, D_MODEL, 2) + _halo_specs(ts, s, D_MODEL, 3) + _halo_specs(ts, s, D_MODEL, 4)
        + _halo_specs(ts, s, D_MODEL, 5) + [_full(sc_w.shape)],
        out_specs=[row, pl.BlockSpec((ts, 5 * D_MODEL), lambda i: (i, 0)), _full(sc_w.shape)],
        out_shape=[jax.ShapeDtypeStruct((s, D_MODEL), F32), jax.ShapeDtypeStruct((s, 5 * D_MODEL), BF16),
                   jax.ShapeDtypeStruct(sc_w.shape, F32)],
        args=[du, du, du, du, hf, hb, proj, *([proj] * 12), sc_w], exchange=exchange)


def even_gates_bwd(proj, adj_f, adj_b, hf, hb, dh, conv_w, conv_b, gate_w, gate_b, lam, exchange=None):
    s = proj.shape[0]
    ts = min(ROW_TILE, s)
    n_tiles = s // ts
    row = pl.BlockSpec((ts, D_MODEL), lambda i: (i, 0))

    def body(xa_ref, xab_ref, xan_ref, af_ref, afb_ref, afn_ref, ab_ref, abb_ref, abn_ref,
             hf_ref, hfb_ref, hfn_ref, hb_ref, hbb_ref, hbn_ref, dh_ref,
             cw_ref, cb_ref, gw_ref, gb_ref, lam_ref, dua_ref, dgw_ref, dgb_ref, dlam_ref):
        @pl.when(pl.program_id(0) == 0)
        def _():
            dgw_ref[...] = jnp.zeros_like(dgw_ref)
            dgb_ref[...] = jnp.zeros_like(dgb_ref)
            dlam_ref[...] = jnp.zeros_like(dlam_ref)

        xa, before, after = _halo_load(xa_ref, xab_ref, xan_ref, n_tiles)
        ua = _rg_conv(xa, before, after, cw_ref[...], cb_ref[...])
        lam_v = lam_ref[...]
        c = RG_C * _softplus(-lam_v)
        dc_dlam = -RG_C * _sigmoid(-lam_v)
        dh = dh_ref[...]
        adj = (_halo_load(af_ref, afb_ref, afn_ref, n_tiles), _halo_load(ab_ref, abb_ref, abn_ref, n_tiles))
        hs = (_halo_load(hf_ref, hfb_ref, hfn_ref, n_tiles), _halo_load(hb_ref, hbb_ref, hbn_ref, n_tiles))
        dua = jnp.zeros_like(ua)
        for direction in range(2):
            step = 1 if direction == 0 else -1
            g = dh + _shift_rows(*adj[direction], step)
            da_all = g * _shift_rows(*hs[direction], -step)
            dua_parts = []
            for head in range(RG_HEADS):
                lanes = slice(head * RG_HEAD_DIM, (head + 1) * RG_HEAD_DIM)
                ua_h = ua[:, lanes]
                c_h = c[direction:direction + 1, lanes]
                r, i, a, beta = _rg_gates(ua_h, gw_ref, gb_ref, c_h, direction, head)
                db = g[:, lanes]
                d_i = db * beta * ua_h
                dbeta = db * (i * ua_h)
                dlog_a = (da_all[:, lanes] - dbeta * a / beta) * a
                dpr = -c_h * dlog_a * r * (1.0 - r)
                dpi = d_i * i * (1.0 - i)
                dua_parts.append(db * beta * i + _bdot_nt(dpr, gw_ref[2 * direction, head])
                                 + _bdot_nt(dpi, gw_ref[2 * direction + 1, head]))
                dgw_ref[2 * direction, head] += _bdot_tn(ua_h, dpr)
                dgw_ref[2 * direction + 1, head] += _bdot_tn(ua_h, dpi)
                dgb_ref[2 * direction, head:head + 1, :] += jnp.sum(dpr, axis=0, keepdims=True)
                dgb_ref[2 * direction + 1, head:head + 1, :] += jnp.sum(dpi, axis=0, keepdims=True)
                dlam_ref[direction:direction + 1, lanes] += (
                    jnp.sum(-r * dlog_a, axis=0, keepdims=True) * dc_dlam[direction:direction + 1, lanes])
            dua = dua + jnp.concatenate(dua_parts, axis=1)
        dua_ref[...] = dua

    return _call(
        body, name="even_gates_bwd", grid=(n_tiles,),
        in_specs=_halo_specs(ts, s, D_MODEL, 0) * 5 + [row] + [_full(conv_w.shape), _full(conv_b.shape), _full(gate_w.shape),
                                                             _full(gate_b.shape), _full(lam.shape)],
        out_specs=[row, _full(gate_w.shape), _full(gate_b.shape), _full(lam.shape)],
        out_shape=[jax.ShapeDtypeStruct((s, D_MODEL), F32), jax.ShapeDtypeStruct(gate_w.shape, F32),
                   jax.ShapeDtypeStruct(gate_b.shape, F32), jax.ShapeDtypeStruct(lam.shape, F32)],
        args=[proj, proj, proj, adj_f, adj_f, adj_f, adj_b, adj_b, adj_b, hf, hf, hf, hb, hb, hb, dh, conv_w, conv_b, gate_w,
              gate_b, lam], exchange=exchange)


def rg_conv_bwd(dua, proj, drest, conv_w, exchange=None):
    s = proj.shape[0]
    ts = min(ROW_TILE, s)
    n_tiles = s // ts

    def body(du_ref, dub_ref, dun_ref, xa_ref, xab_ref, xan_ref, dr_ref, cw_ref, dp_ref, dw_ref, db_ref):
        @pl.when(pl.program_id(0) == 0)
        def _():
            dw_ref[...] = jnp.zeros_like(dw_ref)
            db_ref[...] = jnp.zeros_like(db_ref)

        dua, dua_before, dua_after = _halo_load(du_ref, dub_ref, dun_ref, n_tiles)
        xa, xa_before, xa_after = _halo_load(xa_ref, xab_ref, xan_ref, n_tiles)
        cw = cw_ref[...]
        dxa = (cw[0:1, :] * _shift_rows(dua, dua_before, dua_after, 2) + cw[1:2, :] * _shift_rows(dua, dua_before, dua_after, 1)
               + cw[2:3, :] * dua + cw[3:4, :] * _shift_rows(dua, dua_before, dua_after, -1))
        dp_ref[:, :D_MODEL] = dxa.astype(BF16)
        dp_ref[:, D_MODEL:] = dr_ref[...]
        for tap, offset in enumerate((-2, -1, 0, 1)):
            shifted = xa if offset == 0 else _shift_rows(xa, xa_before, xa_after, offset)
            dw_ref[tap:tap + 1, :] += jnp.sum(dua * shifted, axis=0, keepdims=True)
        db_ref[...] += jnp.sum(dua, axis=0, keepdims=True)

    return _call(
        body, name="rg_conv_bwd", grid=(n_tiles,),
        in_specs=_halo_specs(ts, s, D_MODEL, 0) * 2 + [pl.BlockSpec((ts, 5 * D_MODEL), lambda i: (i, 0)), _full(conv_w.shape)],
        out_specs=[pl.BlockSpec((ts, EVEN_IN), lambda i: (i, 0)), _full(conv_w.shape), _full((1, D_MODEL))],
        out_shape=[jax.ShapeDtypeStruct((s, EVEN_IN), BF16), jax.ShapeDtypeStruct(conv_w.shape, F32),
                   jax.ShapeDtypeStruct((1, D_MODEL), F32)],
        args=[dua, dua, dua, proj, proj, proj, drest, conv_w], exchange=exchange)


def _split3(x):
    x1 = x.astype(BF16)
    rest = x - x1.astype(F32)
    x2 = rest.astype(BF16)
    return x1, x2, (rest - x2.astype(F32)).astype(BF16)


def _chunk_sum_matrix(t, reverse, transpose):
    i = lax.broadcasted_iota(jnp.int32, (t, t), 0)
    j = lax.broadcasted_iota(jnp.int32, (t, t), 1)
    if transpose:
        i, j = j, i
    same = (i // GLA_CHUNK) == (j // GLA_CHUNK)
    return jnp.where(same & ((j >= i) if reverse else (j <= i)), 1.0, 0.0).astype(BF16)


def _exact_dot(m, x):
    return sum(jnp.dot(m, part, preferred_element_type=F32) for part in _split3(x))


def _causal_mask(reverse):
    i = lax.broadcasted_iota(jnp.int32, (GLA_CHUNK, GLA_CHUNK), 0)
    j = lax.broadcasted_iota(jnp.int32, (GLA_CHUNK, GLA_CHUNK), 1)
    return (j >= i) if reverse else (j <= i)


def _gla_gate(lr, wg, bg):
    z = _bdot(lr, wg) + bg
    log_alpha = (jnp.minimum(z, 0.0) - jnp.log(1.0 + jnp.exp(-jnp.abs(z)))) * (1.0 / GLA_NORMALIZER)
    return z, log_alpha


def _gla_chunk_terms(q, k, bcum, rows, reverse):
    bc = bcum[rows, :]
    edge = rows.start if reverse else rows.stop - 1
    btot = bcum[edge:edge + 1, :]
    e_pos, e_neg, e_st = jnp.exp(bc), jnp.exp(-bc), jnp.exp(btot - bc)
    qc, kc = q[rows, :], k[rows, :]
    return qc * (GLA_DK ** -0.5) * e_pos, kc * e_neg, kc * e_st, e_pos, e_neg, e_st, jnp.exp(btot)


def _gla_specs(t, n_tiles, reverse_order):
    def tile(i):
        return n_tiles - 1 - i if reverse_order else i

    return tile, [
        pl.BlockSpec((t, GLA_KEY), lambda i: (tile(i), 0)),
        pl.BlockSpec((t, GLA_KEY), lambda i: (tile(i), 1)),
        pl.BlockSpec((t, D_MODEL), lambda i: (tile(i), 1)),
        pl.BlockSpec((t, LANES), lambda i: (tile(i), (ODD_IN_PAD - LANES) // LANES)),
    ]


def gla_fwd(proj, wg, bg, reverse, o_other=None, gnorm=None):
    s = proj.shape[0]
    t = min(ROW_TILE, s)
    n_tiles = s // t
    n_chunks = t // GLA_CHUNK
    final = o_other is not None
    tile, specs = _gla_specs(t, n_tiles, reverse)

    def body(*refs):
        if final:
            q_ref, k_ref, v_ref, lr_ref, wg_ref, bg_ref, oo_ref, r_ref, gn_ref, osum_ref, u_ref, st_ref, state = refs
        else:
            q_ref, k_ref, v_ref, lr_ref, wg_ref, bg_ref, o_ref, st_ref, state = refs
            osum_ref = o_ref

        @pl.when(pl.program_id(0) == 0)
        def _():
            state[...] = jnp.zeros_like(state)

        _, log_alpha = _gla_gate(lr_ref[...], wg_ref[...], bg_ref[...])
        bcum = _exact_dot(_chunk_sum_matrix(t, reverse, False), log_alpha)
        q, k, v = q_ref[...], k_ref[...], v_ref[...]
        mask = _causal_mask(reverse)
        for c in (range(n_chunks - 1, -1, -1) if reverse else range(n_chunks)):
            rows = slice(c * GLA_CHUNK, (c + 1) * GLA_CHUNK)
            q_in, k_in, k_st, _, _, _, decay = _gla_chunk_terms(q, k, bcum, rows, reverse)
            for head in range(GLA_HEADS):
                kl = slice(head * GLA_DK, (head + 1) * GLA_DK)
                vl = slice(head * GLA_DV, (head + 1) * GLA_DV)
                s_prev = state[head]
                st_ref[c, head] = s_prev
                scores = jnp.where(mask, _bdot_nt(q_in[:, kl], k_in[:, kl]), 0.0)
                osum_ref[rows, vl] = _bdot(scores, v[rows, vl]) + _bdot_nt(q_in[:, kl], s_prev)
                state[head] = s_prev * decay[:, kl] + _bdot_tn(v[rows, vl], k_st[:, kl])
        if final:
            osum = osum_ref[...] + oo_ref[...]
            osum_ref[...] = osum
            silu_r, _ = _silu_and_grad(r_ref[...])
            gn = gn_ref[...]
            for head in range(GLA_HEADS):
                vl = slice(head * GLA_DV, (head + 1) * GLA_DV)
                u_ref[:, vl] = (_rms(osum[:, vl], gn[:, vl]) * silu_r[:, vl]).astype(BF16)

    row = pl.BlockSpec((t, D_MODEL), lambda i: (tile(i), 0))
    st_spec = pl.BlockSpec((n_chunks, GLA_HEADS, GLA_DV, GLA_DK), lambda i: (tile(i), 0, 0, 0))
    st_shape = jax.ShapeDtypeStruct((s // GLA_CHUNK, GLA_HEADS, GLA_DV, GLA_DK), F32)
    in_specs = specs + [_full(wg.shape), _full(bg.shape)]
    args = [proj, proj, proj, proj, wg, bg]
    if final:
        in_specs += [row, pl.BlockSpec((t, D_MODEL), lambda i: (tile(i), 2)), _full(gnorm.shape)]
        args += [o_other, proj, gnorm]
        out_specs = [row, row, st_spec]
        out_shape = [jax.ShapeDtypeStruct((s, D_MODEL), F32), jax.ShapeDtypeStruct((s, D_MODEL), BF16), st_shape]
    else:
        out_specs = [row, st_spec]
        out_shape = [jax.ShapeDtypeStruct((s, D_MODEL), F32), st_shape]
    return pl.pallas_call(
        body, name="gla_fwd_rev" if reverse else "gla_fwd", grid=(n_tiles,), in_specs=in_specs, out_specs=out_specs,
        out_shape=out_shape, scratch_shapes=[pltpu.VMEM((GLA_HEADS, GLA_DV, GLA_DK), F32)], compiler_params=_params(1),
    )(*args)


def gla_out_bwd(du, proj, osum, gnorm):
    s = proj.shape[0]
    ts = min(ROW_TILE, s)
    row = pl.BlockSpec((ts, D_MODEL), lambda i: (i, 0))

    def body(du_ref, r_ref, o_ref, gn_ref, do_ref, dr_ref, dgn_ref):
        @pl.when(pl.program_id(0) == 0)
        def _():
            dgn_ref[...] = jnp.zeros_like(dgn_ref)

        du, osum, gn = du_ref[...], o_ref[...], gn_ref[...]
        silu_r, dsilu_r = _silu_and_grad(r_ref[...])
        for head in range(GLA_HEADS):
            vl = slice(head * GLA_DV, (head + 1) * GLA_DV)
            o_h, g_h, du_h = osum[:, vl], gn[:, vl], du[:, vl]
            dr_ref[:, vl] = (du_h * _rms(o_h, g_h) * dsilu_r[:, vl]).astype(BF16)
            do_h, dg_h = _rms_bwd(o_h, g_h, du_h * silu_r[:, vl])
            do_ref[:, vl] = do_h
            dgn_ref[...] += dg_h

    return pl.pallas_call(
        body, name="gla_out_bwd", grid=(s // ts,),
        in_specs=[row, pl.BlockSpec((ts, D_MODEL), lambda i: (i, 2)), row, _full(gnorm.shape)],
        out_specs=[row, row, _full((1, GLA_DV))],
        out_shape=[jax.ShapeDtypeStruct((s, D_MODEL), F32), jax.ShapeDtypeStruct((s, D_MODEL), BF16),
                   jax.ShapeDtypeStruct((1, GLA_DV), F32)],
        compiler_params=_params(1),
    )(du, proj, osum, gnorm)


def gla_bwd(proj, wg, bg, do, states, reverse, first=None):
    s = proj.shape[0]
    t = min(ROW_TILE, s)
    n_tiles = s // t
    n_chunks = t // GLA_CHUNK
    final = first is not None
    tile, specs = _gla_specs(t, n_tiles, not reverse)

    def body(*refs):
        if final:
            (q_ref, k_ref, v_ref, lr_ref, wg_ref, bg_ref, do_ref, st_ref, dqkv1_ref, dlr1_ref, dr_ref,
             dp_ref, dwg_ref, dbg_ref, dstate, dqkv, dbc, dbt) = refs
        else:
            (q_ref, k_ref, v_ref, lr_ref, wg_ref, bg_ref, do_ref, st_ref,
             dqkv, dlr_ref, dwg_ref, dbg_ref, dstate, dbc, dbt) = refs

        @pl.when(pl.program_id(0) == 0)
        def _():
            dstate[...] = jnp.zeros_like(dstate)
            dwg_ref[...] = jnp.zeros_like(dwg_ref)
            dbg_ref[...] = jnp.zeros_like(dbg_ref)

        lr, wg_v = lr_ref[...], wg_ref[...]
        z, log_alpha = _gla_gate(lr, wg_v, bg_ref[...])
        bcum = _exact_dot(_chunk_sum_matrix(t, reverse, False), log_alpha)
        q, k, v, do_v = q_ref[...], k_ref[...], v_ref[...], do_ref[...]
        mask = _causal_mask(reverse)
        for c in (range(n_chunks) if reverse else range(n_chunks - 1, -1, -1)):
            rows = slice(c * GLA_CHUNK, (c + 1) * GLA_CHUNK)
            q_in, k_in, k_st, e_pos, e_neg, e_st, decay = _gla_chunk_terms(q, k, bcum, rows, reverse)
            dbtot_parts = []
            for head in range(GLA_HEADS):
                kl = slice(head * GLA_DK, (head + 1) * GLA_DK)
                vl = slice(head * GLA_DV, (head + 1) * GLA_DV)
                q_h, k_h, ks_h, v_h, do_h = q_in[:, kl], k_in[:, kl], k_st[:, kl], v[rows, vl], do_v[rows, vl]
                s_prev = st_ref[c, head]
                ds_next = dstate[head]
                scores = jnp.where(mask, _bdot_nt(q_h, k_h), 0.0)
                dscores = jnp.where(mask, _bdot_nt(do_h, v_h), 0.0)
                dq_in = _bdot(dscores, k_h) + _bdot(do_h, s_prev)
                dk_in = _bdot_tn(dscores, q_h)
                dk_st = _bdot(v_h, ds_next)
                dqkv[rows, 2 * GLA_KEY + head * GLA_DV:2 * GLA_KEY + (head + 1) * GLA_DV] = (
                    _bdot_tn(scores, do_h) + _bdot_nt(ks_h, ds_next))
                ddecay = jnp.sum(ds_next * s_prev, axis=0, keepdims=True)
                dstate[head] = ds_next * decay[:, kl] + _bdot_tn(do_h, q_h)
                dqkv[rows, kl] = dq_in * (GLA_DK ** -0.5) * e_pos[:, kl]
                dqkv[rows, GLA_KEY + head * GLA_DK:GLA_KEY + (head + 1) * GLA_DK] = dk_in * e_neg[:, kl] + dk_st * e_st[:, kl]
                dbc[rows, kl] = dq_in * q_h - dk_in * k_h - dk_st * ks_h
                dbtot_parts.append(jnp.sum(dk_st * ks_h, axis=0, keepdims=True) + ddecay * decay[:, kl])
            dbt[rows, :] = jnp.broadcast_to(jnp.concatenate(dbtot_parts, axis=1), (GLA_CHUNK, GLA_KEY))
        dlog_alpha = _exact_dot(_chunk_sum_matrix(t, reverse, True), dbc[...]) + dbt[...]
        dz = dlog_alpha * _sigmoid(-z) * (1.0 / GLA_NORMALIZER)
        dlr = _bdot_nt(dz, wg_v)
        dwg_ref[...] += _bdot_tn(lr, dz)
        dbg_ref[...] += jnp.sum(dz, axis=0, keepdims=True)
        if final:
            dp_ref[:, :2 * D_MODEL] = (dqkv[...] + dqkv1_ref[...]).astype(BF16)
            dp_ref[:, 2 * D_MODEL:3 * D_MODEL] = dr_ref[...]
            dp_ref[:, 3 * D_MODEL:] = (dlr + dlr1_ref[...]).astype(BF16)
        else:
            dlr_ref[...] = dlr

    row = pl.BlockSpec((t, D_MODEL), lambda i: (tile(i), 0))
    wide = pl.BlockSpec((t, 2 * D_MODEL), lambda i: (tile(i), 0))
    narrow = pl.BlockSpec((t, LANES), lambda i: (tile(i), 0))
    st_spec = pl.BlockSpec((n_chunks, GLA_HEADS, GLA_DV, GLA_DK), lambda i: (tile(i), 0, 0, 0))
    in_specs = specs + [_full(wg.shape), _full(bg.shape), row, st_spec]
    args = [proj, proj, proj, proj, wg, bg, do, states]
    acc_specs = [_full(wg.shape), _full(bg.shape)]
    acc_shapes = [jax.ShapeDtypeStruct(wg.shape, F32), jax.ShapeDtypeStruct(bg.shape, F32)]
    scratch = [pltpu.VMEM((GLA_HEADS, GLA_DV, GLA_DK), F32)]
    work = [pltpu.VMEM((t, GLA_KEY), F32), pltpu.VMEM((t, GLA_KEY), F32)]
    if final:
        in_specs += [wide, narrow, row]
        args += list(first)
        out_specs = [pl.BlockSpec((t, ODD_IN_PAD), lambda i: (tile(i), 0))] + acc_specs
        out_shape = [jax.ShapeDtypeStruct((s, ODD_IN_PAD), BF16)] + acc_shapes
        scratch += [pltpu.VMEM((t, 2 * D_MODEL), F32)] + work
    else:
        out_specs = [wide, narrow] + acc_specs
        out_shape = [jax.ShapeDtypeStruct((s, 2 * D_MODEL), F32), jax.ShapeDtypeStruct((s, LANES), F32)] + acc_shapes
        scratch += work
    return pl.pallas_call(
        body, name="gla_bwd_rev" if reverse else "gla_bwd", grid=(n_tiles,), in_specs=in_specs, out_specs=out_specs,
        out_shape=out_shape, scratch_shapes=scratch, compiler_params=_params(1),
    )(*args)


def _adamw_update(g, w, m, v):
    new_m = ADAM_B1 * m + (1.0 - ADAM_B1) * g
    new_v = ADAM_B2 * v + (1.0 - ADAM_B2) * (g * g)
    m_hat = new_m / (1.0 - ADAM_B1 ** ADAM_STEP)
    v_hat = new_v / (1.0 - ADAM_B2 ** ADAM_STEP)
    return -ADAM_LR * (m_hat / (jnp.sqrt(v_hat) + ADAM_EPS) + ADAM_WD * w), new_m, new_v


def sum_parts(parts, name):
    _, r, c = parts.shape

    def body(p_ref, o_ref):
        total = p_ref[0].astype(F32)
        for j in range(1, N_DEV):
            total = total + p_ref[j].astype(F32)
        o_ref[...] = total

    return pl.pallas_call(body, name=name, in_specs=[_full(parts.shape)], out_specs=_full((r, c)), grid=(1,),
                          out_shape=jax.ShapeDtypeStruct((r, c), F32), compiler_params=_params(1))(parts)


def adamw(parts, w, m, v, name):
    n, r, c = parts.shape
    tr = r if r <= MM_TILE else ROW_TILE

    def body(p_ref, w_ref, m_ref, v_ref, g_ref, d_ref, nm_ref, nv_ref):
        g = p_ref[0].astype(F32)
        for j in range(1, n):
            g = g + p_ref[j].astype(F32)
        g_ref[...] = g
        d_ref[...], nm_ref[...], nv_ref[...] = _adamw_update(g, w_ref[...], m_ref[...], v_ref[...])

    row = pl.BlockSpec((tr, c), lambda i: (i, 0))
    return pl.pallas_call(
        body, name=name, grid=(r // tr,),
        in_specs=[pl.BlockSpec((n, tr, c), lambda i: (0, i, 0)), row, row, row], out_specs=[row] * 4,
        out_shape=[jax.ShapeDtypeStruct((r, c), F32)] * 4, compiler_params=_params(1),
    )(parts, w, m, v)


def _small_views(shape):
    if len(shape) == 2:
        return [((slice(None), slice(None)), (slice(None), slice(None)))]
    if len(shape) == 3:
        return [((slice(None), slice(None)), (0,))]
    rows = shape[2]
    return [((slice(k * rows, (k + 1) * rows), slice(None)), (0, k)) for k in range(shape[1])]


def adamw_small(landings, w, m, v):
    names = list(landings)
    n = len(names)
    shapes = [w[name].shape for name in names]

    def body(*refs):
        land, ws, ms, vs = refs[:n], refs[n:2 * n], refs[2 * n:3 * n], refs[3 * n:4 * n]
        outs = [refs[(4 + k) * n:(5 + k) * n] for k in range(4)]
        for k in range(n):
            total = land[k][0]
            for j in range(1, N_DEV):
                total = total + land[k][j]
            for rows, at in _small_views(shapes[k]):
                g = total[rows]
                outs[0][k][at] = g
                outs[1][k][at], outs[2][k][at], outs[3][k][at] = _adamw_update(g, ws[k][at], ms[k][at], vs[k][at])

    blocks = [_full(sh) for sh in shapes]
    outs = pl.pallas_call(
        body, name="adamw_small", grid=(1,),
        in_specs=[_full(landings[name].shape) for name in names] + blocks * 3, out_specs=blocks * 4,
        out_shape=[jax.ShapeDtypeStruct(sh, F32) for sh in shapes] * 4, compiler_params=_params(1),
    )(*[landings[name] for name in names], *[src[name] for src in (w, m, v) for name in names])
    return [dict(zip(names, outs[k * n:(k + 1) * n])) for k in range(4)]


def adamw_replicated(land_vec, land_gate_b, land_loss, names, w, m, v, gate_b):
    n = len(names)

    def body(*refs):
        vec_ref, gb_ref, loss_ref = refs[:3]
        ws, ms, vs = refs[3:3 + n], refs[3 + n:3 + 2 * n], refs[3 + 2 * n:3 + 3 * n]
        gw_ref, gm_ref, gv_ref = refs[3 + 3 * n:6 + 3 * n]
        outs = refs[6 + 3 * n:]
        vec, gb, loss = vec_ref[0], gb_ref[0], loss_ref[0]
        for j in range(1, N_DEV):
            vec, gb, loss = vec + vec_ref[j], gb + gb_ref[j], loss + loss_ref[j]
        for k in range(n):
            g = vec[k:k + 1, :]
            outs[k][...] = g
            outs[n + k][...], outs[2 * n + k][...], outs[3 * n + k][...] = _adamw_update(g, ws[k][...], ms[k][...], vs[k][...])
        outs[4 * n][...] = gb
        outs[4 * n + 1][...], outs[4 * n + 2][...], outs[4 * n + 3][...] = _adamw_update(gb, gw_ref[...], gm_ref[...], gv_ref[...])
        outs[4 * n + 4][...] = loss

    vec_block, gb_block = _full((1, D_MODEL)), _full(gate_b[0].shape)
    outs = pl.pallas_call(
        body, name="adamw_replicated", grid=(1,),
        in_specs=[_full(land_vec.shape), _full(land_gate_b.shape), _full(land_loss.shape)] + [vec_block] * (3 * n) + [gb_block] * 3,
        out_specs=[vec_block] * (4 * n) + [gb_block] * 4 + [_full(land_loss.shape[1:])],
        out_shape=[jax.ShapeDtypeStruct((1, D_MODEL), F32)] * (4 * n) + [jax.ShapeDtypeStruct(gate_b[0].shape, F32)] * 4
        + [jax.ShapeDtypeStruct(land_loss.shape[1:], F32)],
        compiler_params=_params(1),
    )(land_vec, land_gate_b, land_loss, *[src[name] for src in (w, m, v) for name in names], *gate_b)
    results = {name: [outs[k * n + i] for k in range(4)] for i, name in enumerate(names)}
    return results, outs[4 * n:4 * n + 4], outs[4 * n + 4]


SMALL_SHARDED = ("rg_conv_w", "rg_lambda", "sc_conv_w", "odd_norm_pre", "odd_norm_post", "gla_b_gate", "gla_norm_g", "gla_w_gate_lr")
SMALL_ROWS = {"rg_conv_w": (0, 4), "rg_lambda": (4, 2), "sc_conv_w": (6, 3), "odd_norm_pre": (9, 1), "odd_norm_post": (10, 1),
              "gla_b_gate": (11, 2), "gla_norm_g": (13, 1), "gla_w_gate_lr": (16, 32)}


def _pack_small(shards):
    pieces, at = [], 0
    for name in SMALL_SHARDED:
        start, rows = SMALL_ROWS[name]
        if start > at:
            pieces.append(jnp.zeros((start - at, LANES), F32))
        a = shards[name].reshape(rows, -1)
        pieces.append(jnp.pad(a, ((0, 0), (0, LANES - a.shape[1]))))
        at = start + rows
    return jnp.concatenate(pieces, axis=0)


def _unpack_gathered(g):
    def cols(name, width):
        start, rows = SMALL_ROWS[name]
        return jnp.transpose(g[:, start:start + rows, :width], (1, 0, 2)).reshape(rows, N_DEV * width)

    w_lr = cols("gla_w_gate_lr", GLA_KEY // N_DEV).reshape(2, GLA_RANK, GLA_KEY)
    return dict(rg_conv_w=cols("rg_conv_w", LANES), rg_lambda=cols("rg_lambda", LANES), sc_conv_w=cols("sc_conv_w", LANES),
                odd_norm_pre=cols("odd_norm_pre", LANES), odd_norm_post=cols("odd_norm_post", LANES),
                gla_b_gate=cols("gla_b_gate", GLA_KEY // N_DEV), gla_norm_g=cols("gla_norm_g", GLA_DV // N_DEV), gla_w_gate_lr=w_lr)


def _blocks_along_columns(a, rows):
    return jnp.transpose(a.reshape(rows, N_DEV, -1), (1, 0, 2))


def kernel(x, even_norm_pre, even_norm_post, even_w_in, rg_conv_w, rg_conv_b, rg_gate_w, rg_gate_b, rg_lambda, sc_conv_w, even_w_out, odd_norm_pre, odd_norm_post, odd_w_in, gla_w_gate_lr, gla_b_gate, gla_norm_g, odd_w_out, loss_target, m_even_norm_pre, m_even_norm_post, m_even_w_in, m_rg_conv_w, m_rg_conv_b, m_rg_gate_w, m_rg_gate_b, m_rg_lambda, m_sc_conv_w, m_even_w_out, m_odd_norm_pre, m_odd_norm_post, m_odd_w_in, m_gla_w_gate_lr, m_gla_b_gate, m_gla_norm_g, m_odd_w_out, v_even_norm_pre, v_even_norm_post, v_even_w_in, v_rg_conv_w, v_rg_conv_b, v_rg_gate_w, v_rg_gate_b, v_rg_lambda, v_sc_conv_w, v_even_w_out, v_odd_norm_pre, v_odd_norm_post, v_odd_w_in, v_gla_w_gate_lr, v_gla_b_gate, v_gla_norm_g, v_odd_w_out):
    weights = dict(even_norm_pre=even_norm_pre, even_norm_post=even_norm_post, even_w_in=even_w_in, rg_conv_w=rg_conv_w,
                   rg_conv_b=rg_conv_b, rg_gate_w=rg_gate_w, rg_gate_b=rg_gate_b, rg_lambda=rg_lambda, sc_conv_w=sc_conv_w,
                   even_w_out=even_w_out, odd_norm_pre=odd_norm_pre, odd_norm_post=odd_norm_post, odd_w_in=odd_w_in,
                   gla_w_gate_lr=gla_w_gate_lr, gla_b_gate=gla_b_gate, gla_norm_g=gla_norm_g, odd_w_out=odd_w_out)
    m_in = dict(even_norm_pre=m_even_norm_pre, even_norm_post=m_even_norm_post, even_w_in=m_even_w_in, rg_conv_w=m_rg_conv_w,
                rg_conv_b=m_rg_conv_b, rg_gate_w=m_rg_gate_w, rg_gate_b=m_rg_gate_b, rg_lambda=m_rg_lambda, sc_conv_w=m_sc_conv_w,
                even_w_out=m_even_w_out, odd_norm_pre=m_odd_norm_pre, odd_norm_post=m_odd_norm_post, odd_w_in=m_odd_w_in,
                gla_w_gate_lr=m_gla_w_gate_lr, gla_b_gate=m_gla_b_gate, gla_norm_g=m_gla_norm_g, odd_w_out=m_odd_w_out)
    v_in = dict(even_norm_pre=v_even_norm_pre, even_norm_post=v_even_norm_post, even_w_in=v_even_w_in, rg_conv_w=v_rg_conv_w,
                rg_conv_b=v_rg_conv_b, rg_gate_w=v_rg_gate_w, rg_gate_b=v_rg_gate_b, rg_lambda=v_rg_lambda, sc_conv_w=v_sc_conv_w,
                even_w_out=v_even_w_out, odd_norm_pre=v_odd_norm_pre, odd_norm_post=v_odd_norm_post, odd_w_in=v_odd_w_in,
                gla_w_gate_lr=v_gla_w_gate_lr, gla_b_gate=v_gla_b_gate, gla_norm_g=v_gla_norm_g, odd_w_out=v_odd_w_out)
    names = list(weights)
    shapes = {n: weights[n].shape for n in names}
    xs = x[0]
    tgt = loss_target[0]

    first = Exchange()
    first.gather(even_w_in[0].astype(BF16), columns=True, via_sibling=True)
    first.gather(_pack_small({n: weights[n][0] for n in SMALL_SHARDED}))
    w_in_e, small_all = run_exchange(first, "gather_first")
    small = _unpack_gathered(small_all)
    gate_w = rg_gate_w[0].reshape(4, RG_HEADS, RG_HEAD_DIM, RG_HEAD_DIM).astype(BF16)
    gate_b = rg_gate_b[0].reshape(4, RG_HEADS, RG_HEAD_DIM)
    conv_b = rg_conv_b
    wg_pad = [jnp.pad(small["gla_w_gate_lr"][d], ((GLA_RANK * d, LANES - GLA_RANK * (d + 1)), (0, 0))).astype(BF16) for d in range(2)]
    bg = [small["gla_b_gate"][d:d + 1] for d in range(2)]
    gnorm = jnp.tile(small["gla_norm_g"], (1, GLA_HEADS))

    behind_in = Exchange()
    behind_in.gather(even_w_out[0].astype(BF16))
    behind_in.gather(odd_w_in[0].astype(BF16))
    (proj_e, h_e), (w_out_e, w_in_o_blocks) = rms_matmul(xs, even_norm_pre, w_in_e, 2 * MM_TILE, 2 * EVEN_SHARD, "even_in",
                                                                exchange=behind_in)
    w_out_e = w_out_e.reshape(2 * D_MODEL, D_MODEL)
    w_in_o = jnp.transpose(w_in_o_blocks, (1, 0, 2)).reshape(D_MODEL, ODD_IN)
    w_in_o = jnp.pad(w_in_o, ((0, 0), (0, ODD_IN_PAD - ODD_IN)))
    behind_gates = Exchange()
    behind_gates.gather(odd_w_out[0].astype(BF16))
    ab, (w_out_o,) = even_gates_fwd(proj_e, small["rg_conv_w"], conv_b, gate_w, gate_b, small["rg_lambda"], exchange=behind_gates)
    w_out_o = w_out_o.reshape(D_MODEL, D_MODEL)
    hf = linear_scan(ab, 0, ab, 1, False, False, "scan_fwd")
    hb = linear_scan(ab, 2, ab, 3, True, False, "scan_rev")
    u_e = even_mix_fwd(hf, hb, proj_e, small["sc_conv_w"])
    y_e, x1 = matmul_post(u_e, w_out_e, xs, even_norm_post, "even_out")

    proj_o, h_o = rms_matmul(x1, small["odd_norm_pre"], w_in_o, MM_TILE, ODD_IN_PAD, "odd_in")
    o_f, st_f = gla_fwd(proj_o, wg_pad[0], bg[0], False)
    osum, u_o, st_b = gla_fwd(proj_o, wg_pad[1], bg[1], True, o_other=o_f, gnorm=gnorm)
    y_o, dout, loss_part = matmul_post(u_o, w_out_o, x1, small["odd_norm_post"], "odd_out", target=tgt)

    du_o, dy_o, d_odd_norm_post = normbwd_matmul_nt(y_o, small["odd_norm_post"], dout, w_out_o, D_MODEL, "odd_out_bwd")
    d_w_out_o = matmul_tn(u_o, dy_o, D_MODEL, D_MODEL, 4 * MM_TILE, BF16, "odd_w_out_grad")
    do, dr, d_gnorm = gla_out_bwd(du_o, proj_o, osum, gnorm)
    dqkv_f, dlr_f, dwg_f, dbg_f = gla_bwd(proj_o, wg_pad[0], bg[0], do, st_f, False)
    dproj_o, dwg_b, dbg_b = gla_bwd(proj_o, wg_pad[1], bg[1], do, st_b, True, first=(dqkv_f, dlr_f, dr))
    dx1, d_odd_norm_pre = matmul_nt_normbwd(dproj_o, w_in_o, x1, small["odd_norm_pre"], dout, ODD_IN_PAD, "odd_in_bwd")
    d_w_in_o = matmul_tn(h_o, dproj_o, D_MODEL, ODD_IN_PAD // 5, 8 * MM_TILE, BF16, "odd_w_in_grad")

    landed = {}
    behind_out = Exchange()
    behind_out.scatter(d_w_out_o.reshape(N_DEV, D_MODEL // N_DEV, D_MODEL))
    behind_out.scatter(d_odd_norm_pre, columns=True)
    behind_out.scatter(d_odd_norm_post, columns=True)
    behind_out.scatter(_blocks_along_columns(jnp.concatenate([dbg_f, dbg_b], axis=0), 2))
    behind_out.scatter(_blocks_along_columns(d_gnorm, 1))
    behind_out.scatter(_blocks_along_columns(jnp.concatenate([dwg_f[:GLA_RANK], dwg_b[GLA_RANK:2 * GLA_RANK]], axis=0), 2 * GLA_RANK))
    (du_e, dy_e, d_even_norm_post), got = normbwd_matmul_nt(y_e, even_norm_post, dx1, w_out_e, 2 * D_MODEL, "even_out_bwd",
                                                           exchange=behind_out)
    p_w_out_o = got[0]
    for n, part in zip(("odd_norm_pre", "odd_norm_post", "gla_b_gate", "gla_norm_g", "gla_w_gate_lr"), got[1:]):
        landed[n] = part
    d_w_out_e = matmul_tn(u_e, dy_e, D_MODEL, D_MODEL, 4 * MM_TILE, BF16, "even_w_out_grad")
    behind_mix = Exchange()
    behind_mix.scatter(jnp.transpose(d_w_in_o[:, :ODD_IN].reshape(D_MODEL, N_DEV, ODD_SHARD), (1, 0, 2)))
    (dh, drest, d_sc_w), (p_w_in_o,) = even_mix_bwd(du_e, hf, hb, proj_e, small["sc_conv_w"], exchange=behind_mix)
    dh3 = dh.reshape(1, *dh.shape)
    adj_f = linear_scan(ab, 0, dh3, 0, True, True, "scan_fwd_adjoint")
    adj_b = linear_scan(ab, 2, dh3, 0, False, True, "scan_rev_adjoint")
    behind_gates_bwd = Exchange()
    behind_gates_bwd.scatter(d_w_out_e.reshape(N_DEV, 2 * D_MODEL // N_DEV, D_MODEL))
    behind_gates_bwd.scatter(d_sc_w, columns=True)
    (dua, d_gate_w, d_gate_b, d_lam), (p_w_out_e, landed["sc_conv_w"]) = even_gates_bwd(
        proj_e, adj_f, adj_b, hf, hb, dh, small["rg_conv_w"], conv_b, gate_w, gate_b, small["rg_lambda"], exchange=behind_gates_bwd)
    gate_w_rows = 4 * RG_HEADS * RG_HEAD_DIM
    behind_conv = Exchange()
    behind_conv.scatter(d_gate_w.reshape(N_DEV, gate_w_rows // N_DEV, RG_HEAD_DIM))
    behind_conv.scatter(d_lam, columns=True)
    (dproj_e, d_conv_w, d_conv_b), (p_gate_w, landed["rg_lambda"]) = rg_conv_bwd(dua, proj_e, drest, small["rg_conv_w"],
                                                                                 exchange=behind_conv)
    behind_w_grad = Exchange()
    behind_w_grad.gather(sum_parts(p_gate_w, "sum_gate_w"))
    d_w_in_e, (g_gate_w_all,) = matmul_tn(h_e, dproj_e, D_MODEL, EVEN_SHARD, 8 * MM_TILE, BF16, "even_w_in_grad",
                                          exchange=behind_w_grad)
    behind_in_bwd = Exchange()
    behind_in_bwd.scatter(d_w_in_e, columns=True)
    behind_in_bwd.scatter(d_conv_w, columns=True)
    (grad_x, d_even_norm_pre), (p_w_in_e, landed["rg_conv_w"]) = matmul_nt_normbwd(
        dproj_e, w_in_e, xs, even_norm_pre, dx1, 3 * D_MODEL, "even_in_bwd", exchange=behind_in_bwd)
    last = Exchange()
    replicated_vecs = ("even_norm_pre", "even_norm_post", "rg_conv_b")
    last.gather(jnp.concatenate([d_even_norm_pre, d_even_norm_post, d_conv_b], axis=0))
    last.gather(d_gate_b.reshape(4 * RG_HEADS, RG_HEAD_DIM))
    last.gather(loss_part)
    land_vec, land_gate_b, land_loss = run_exchange(last, "gather_last")

    results = {}

    def update(name, parts_, shape2d):
        outs = adamw(parts_, weights[name][0].reshape(shape2d), m_in[name][0].reshape(shape2d), v_in[name][0].reshape(shape2d),
                     "adamw_" + name)
        results[name] = [o.reshape(shapes[name]) for o in outs]

    update("even_w_in", p_w_in_e, (D_MODEL, EVEN_SHARD))
    update("even_w_out", p_w_out_e, (2 * D_MODEL // N_DEV, D_MODEL))
    update("odd_w_in", p_w_in_o, (D_MODEL, ODD_SHARD))
    update("odd_w_out", p_w_out_o, (D_MODEL // N_DEV, D_MODEL))
    update("rg_gate_w", g_gate_w_all.reshape(1, gate_w_rows, RG_HEAD_DIM), (gate_w_rows, RG_HEAD_DIM))
    small_out = adamw_small({n: landed[n] for n in SMALL_SHARDED}, weights, m_in, v_in)
    for n in SMALL_SHARDED:
        results[n] = [o[n] for o in small_out]
    gate_b_shape = (4 * RG_HEADS, RG_HEAD_DIM)
    rep_out, gate_b_out, loss_all = adamw_replicated(land_vec, land_gate_b, land_loss, replicated_vecs, weights, m_in, v_in,
                                                     [src["rg_gate_b"].reshape(gate_b_shape) for src in (weights, m_in, v_in)])
    results.update(rep_out)
    results["rg_gate_b"] = [o.reshape(shapes["rg_gate_b"]) for o in gate_b_out]

    return (loss_all[0, 0], grad_x.reshape(x.shape), *[results[n][0] for n in names], *[results[n][1] for n in names],
            *[results[n][2] for n in names], *[results[n][3] for n in names])
```

```python
import functools

import jax
import jax.numpy as jnp
from jax import lax
from jax.experimental import pallas as pl
from jax.experimental.pallas import tpu as pltpu

F32 = jnp.float32
BF16 = jnp.bfloat16

N_DEV = 8
D_MODEL = 1024
NORM_EPS = 1e-6
RG_HEADS = 8
RG_HEAD_DIM = 128
RG_C = 8.0
GLA_HEADS = 4
GLA_DK = 128
GLA_DV = 256
GLA_KEY = 512
GLA_RANK = 16
GLA_NORMALIZER = 16.0
GLA_CHUNK = 64
EVEN_IN = 6144
ODD_IN = 3104
ODD_IN_PAD = 3200
ODD_SHARD = ODD_IN // N_DEV
EVEN_SHARD = EVEN_IN // N_DEV
ADAM_LR = 0.001
ADAM_B1 = 0.9
ADAM_B2 = 0.999
ADAM_EPS = 1e-08
ADAM_WD = 0.01
ADAM_STEP = 10

SUBLANES = 8
LANES = 128
VMEM_LIMIT_BYTES = 48 * 2 ** 20
ROW_TILE = 256
MM_TILE = 512
PACK_ROWS = 48
MESH_ID = pl.DeviceIdType.MESH


def _params(n_grid):
    return pltpu.CompilerParams(dimension_semantics=("arbitrary",) * n_grid, vmem_limit_bytes=VMEM_LIMIT_BYTES)


def _bdot(a, b):
    return jnp.dot(a.astype(BF16), b.astype(BF16), preferred_element_type=F32)


def _bdot_nt(a, b):
    return lax.dot_general(a.astype(BF16), b.astype(BF16), (((1,), (1,)), ((), ())), preferred_element_type=F32)


def _bdot_tn(a, b):
    return lax.dot_general(a.astype(BF16), b.astype(BF16), (((0,), (0,)), ((), ())), preferred_element_type=F32)


def _rstd(x):
    return lax.rsqrt(jnp.mean(x * x, axis=-1, keepdims=True) + NORM_EPS)


def _rms(x, g):
    return x * _rstd(x) * g


def _rms_bwd(x, g, dy):
    xh = x * _rstd(x)
    dyg = dy * g
    dx = _rstd(x) * (dyg - xh * jnp.mean(dyg * xh, axis=-1, keepdims=True))
    return dx, jnp.sum(dy * xh, axis=0, keepdims=True)


def _sigmoid(z):
    return 0.5 * jnp.tanh(0.5 * z) + 0.5


def _silu_and_grad(z):
    s = _sigmoid(z)
    return z * s, s * (1.0 + z * (1.0 - s))


def _softplus(z):
    return jnp.maximum(z, 0.0) + jnp.log(1.0 + jnp.exp(-jnp.abs(z)))


def _shift_rows(cur, before, after, d):
    ts = cur.shape[0]
    row = lax.broadcasted_iota(jnp.int32, (SUBLANES, cur.shape[1]), 0)
    out = pltpu.roll(cur, (-d) % ts, 0)
    if d < 0:
        edge = jnp.where(row < -d, pltpu.roll(before, (-d) % SUBLANES, 0), out[:SUBLANES])
        return jnp.concatenate([edge, out[SUBLANES:]], axis=0)
    edge = jnp.where(row >= SUBLANES - d, pltpu.roll(after, (-d) % SUBLANES, 0), out[ts - SUBLANES:])
    return jnp.concatenate([out[:ts - SUBLANES], edge], axis=0)


def _halo_specs(ts, s, width, col):
    per = ts // SUBLANES
    last = s // SUBLANES - 1
    return [
        pl.BlockSpec((ts, width), lambda i: (i, col)),
        pl.BlockSpec((SUBLANES, width), lambda i: (jnp.maximum(i * per - 1, 0), col)),
        pl.BlockSpec((SUBLANES, width), lambda i: (jnp.minimum((i + 1) * per, last), col)),
    ]


def _halo_load(cur_ref, before_ref, after_ref, n_tiles):
    i = pl.program_id(0)
    before = jnp.where(i > 0, before_ref[...], 0.0)
    after = jnp.where(i < n_tiles - 1, after_ref[...], 0.0)
    return cur_ref[...], before, after


def _full(shape):
    return pl.BlockSpec(shape, lambda *_: (0,) * len(shape))


def _peer(x, y, c, mask):
    px, py, pc = x ^ (mask >> 2), y ^ ((mask >> 1) & 1), c ^ (mask & 1)
    return (px, py, pc), 4 * px + 2 * py + pc


class Exchange:
    SIBLING = 1
    OTHER_CHIPS = (2, 4, 6)

    def __init__(self):
        self.args, self.out_shape, self._kinds = [], [], []

    def gather(self, block, columns=False, via_sibling=False):
        shape = (block.shape[0], N_DEV * block.shape[1]) if columns else (N_DEV,) + block.shape
        return self._add(block, shape, ("gather", columns, via_sibling))

    def scatter(self, stack, columns=False):
        shape = (N_DEV, stack.shape[0], stack.shape[1] // N_DEV) if columns else stack.shape
        return self._add(stack, shape, ("scatter", columns, False))

    def _add(self, arg, shape, kind):
        self.args.append(arg)
        self.out_shape.append(jax.ShapeDtypeStruct(shape, arg.dtype))
        self._kinds.append(kind)
        return len(self.args) - 1

    def semaphores(self):
        n = len(self.args)
        return [pltpu.SemaphoreType.DMA((n, N_DEV - 1)), pltpu.SemaphoreType.DMA((n, N_DEV - 1)), pltpu.SemaphoreType.DMA((n,))]

    def _copies(self, position, in_refs, out_refs):
        x, y, c, me = position
        for arr, ((kind, columns, via_sibling), src, out) in enumerate(zip(self._kinds, in_refs, out_refs)):
            for mask in range(N_DEV):
                _, peer_id = _peer(x, y, c, mask)
                relayed = via_sibling and mask not in (0, self.SIBLING) + self.OTHER_CHIPS
                if kind == "gather":
                    if columns:
                        width = src.shape[-1]
                        yield (arr, mask, src, out.at[:, pl.ds(pl.multiple_of(me * width, LANES), width)],
                               out.at[:, pl.ds(pl.multiple_of(peer_id * width, LANES), width)], relayed)
                    else:
                        yield arr, mask, src, out.at[me], out.at[peer_id], relayed
                else:
                    if columns:
                        width = src.shape[-1] // N_DEV
                        block = src.at[:, pl.ds(pl.multiple_of(peer_id * width, LANES), width)]
                    else:
                        block = src.at[peer_id]
                    yield arr, mask, block, out.at[me], out.at[peer_id], False

    def _remote(self, position, sems, arr, mask, to_mask, src, dst):
        x, y, c, _ = position
        return pltpu.make_async_remote_copy(src_ref=src, dst_ref=dst, send_sem=sems[0].at[arr, mask - 1], recv_sem=sems[1].at[arr, mask - 1],
                                            device_id=_peer(x, y, c, to_mask)[0], device_id_type=MESH_ID)

    def start(self, position, in_refs, out_refs, sems):
        for arr, mask, src, dst, _, relayed in self._copies(position, in_refs, out_refs):
            if mask == 0:
                pltpu.make_async_copy(src, dst, sems[2].at[arr]).start()
            elif not relayed:
                self._remote(position, sems, arr, mask, mask, src, dst).start()

    def wait(self, position, in_refs, out_refs, sems):
        copies = list(self._copies(position, in_refs, out_refs))
        landings = {(arr, mask): landing for arr, mask, _, _, landing, _ in copies}
        passed_on = set()
        for arr, mask, src, _, landing, relayed in copies:
            if relayed:
                held = landings[arr, mask ^ self.SIBLING]
                self._remote(position, sems, arr, mask ^ self.SIBLING, mask ^ self.SIBLING, src, held).wait_recv()
                self._remote(position, sems, arr, mask, self.SIBLING, held, held).start()
                passed_on.add((arr, mask ^ self.SIBLING))
        for arr, mask, src, dst, landing, relayed in copies:
            if mask == 0:
                pltpu.make_async_copy(src, dst, sems[2].at[arr]).wait()
                continue
            if (arr, mask) not in passed_on:
                self._remote(position, sems, arr, mask, mask, src, landing).wait_recv()
            if relayed:
                held = landings[arr, mask ^ self.SIBLING]
                self._remote(position, sems, arr, mask, self.SIBLING, held, held).wait_send()
            else:
                self._remote(position, sems, arr, mask, mask, src, dst).wait_send()


def _call(body, *, name, grid, in_specs, out_specs, out_shape, args, scratch_shapes=(), exchange=None):
    single = not isinstance(out_shape, (list, tuple))
    if single:
        out_specs, out_shape = [out_specs], [out_shape]
    params = _params(len(grid))
    if exchange is None:
        outs = pl.pallas_call(body, name=name, grid=grid, in_specs=in_specs, out_specs=out_specs, out_shape=out_shape,
                              scratch_shapes=list(scratch_shapes), compiler_params=params)(*args)
        return outs[0] if single else outs
    counts = (len(args), len(exchange.args), len(out_shape), len(exchange.out_shape), len(scratch_shapes), 3)

    def wrapped(*refs):
        groups, at = [], 0
        for n in counts:
            groups.append(refs[at:at + n])
            at += n
        main_in, ex_in, main_out, ex_out, main_scratch, sems = groups
        x, y, c = lax.axis_index("x"), lax.axis_index("y"), lax.axis_index("c")
        position = (x, y, c, 4 * x + 2 * y + c)
        ids = [pl.program_id(a) for a in range(len(grid))]
        first = functools.reduce(jnp.logical_and, [i == 0 for i in ids])
        last = functools.reduce(jnp.logical_and, [i == g - 1 for i, g in zip(ids, grid)])

        @pl.when(first)
        def _():
            exchange.start(position, ex_in, ex_out, sems)

        body(*main_in, *main_out, *main_scratch)

        @pl.when(last)
        def _():
            exchange.wait(position, ex_in, ex_out, sems)

    hbm = pl.BlockSpec(memory_space=pl.ANY)
    outs = pl.pallas_call(
        wrapped, name=name, grid=grid, in_specs=list(in_specs) + [hbm] * counts[1], out_specs=list(out_specs) + [hbm] * counts[3],
        out_shape=list(out_shape) + exchange.out_shape, scratch_shapes=list(scratch_shapes) + exchange.semaphores(),
        compiler_params=params)(*args, *exchange.args)
    main = outs[:counts[2]]
    return (main[0] if single else main), outs[counts[2]:]


def run_exchange(exchange, name):
    return _call(lambda: None, name=name, grid=(1,), in_specs=[], out_specs=[], out_shape=[], args=[], exchange=exchange)[1]


def rms_matmul(x, g, w, tm, tn, name, exchange=None):
    s, d = x.shape
    n = w.shape[1]
    tm = min(tm, s)

    def body(x_ref, g_ref, w_ref, o_ref, h_ref):
        @pl.when(pl.program_id(1) == 0)
        def _():
            h_ref[...] = _rms(x_ref[...], g_ref[...]).astype(BF16)

        o_ref[...] = jnp.dot(h_ref[...], w_ref[...], preferred_element_type=F32)

    return _call(
        body, name=name, grid=(s // tm, n // tn),
        in_specs=[pl.BlockSpec((tm, d), lambda i, j: (i, 0)), _full((1, d)), pl.BlockSpec((d, tn), lambda i, j: (0, j))],
        out_specs=[pl.BlockSpec((tm, tn), lambda i, j: (i, j)), pl.BlockSpec((tm, d), lambda i, j: (i, 0))],
        out_shape=[jax.ShapeDtypeStruct((s, n), F32), jax.ShapeDtypeStruct((s, d), BF16)],
        args=[x, g, w], exchange=exchange)


def matmul_post(u, w, xres, g, name, target=None):
    s, k = u.shape
    d = w.shape[1]
    tm = min(MM_TILE, s)
    with_loss = target is not None

    def body(*refs):
        if with_loss:
            u_ref, w_ref, x_ref, g_ref, t_ref, y_ref, dout_ref, loss_ref = refs
        else:
            u_ref, w_ref, x_ref, g_ref, y_ref, out_ref = refs
        y = jnp.dot(u_ref[...], w_ref[...], preferred_element_type=F32)
        y_ref[...] = y
        out = x_ref[...] + _rms(y, g_ref[...])
        if with_loss:
            @pl.when(pl.program_id(0) == 0)
            def _():
                loss_ref[...] = jnp.zeros_like(loss_ref)

            diff = out - t_ref[...]
            dout_ref[...] = diff * (1.0 / d)
            loss_ref[...] += 0.5 * jnp.sum(jnp.mean(diff * diff, axis=-1, keepdims=True))
        else:
            out_ref[...] = out

    row = pl.BlockSpec((tm, d), lambda i: (i, 0))
    in_specs = [pl.BlockSpec((tm, k), lambda i: (i, 0)), _full((k, d)), row, _full((1, d))]
    args = [u, w, xres, g]
    out_specs = [row, row]
    out_shape = [jax.ShapeDtypeStruct((s, d), F32), jax.ShapeDtypeStruct((s, d), F32)]
    if with_loss:
        in_specs.append(row)
        args.append(target)
        out_specs.append(_full((SUBLANES, LANES)))
        out_shape.append(jax.ShapeDtypeStruct((SUBLANES, LANES), F32))
    return pl.pallas_call(body, name=name, grid=(s // tm,), in_specs=in_specs, out_specs=out_specs,
                          out_shape=out_shape, compiler_params=_params(1))(*args)


def normbwd_matmul_nt(y, g, dout, w, tn, name, exchange=None):
    s, d = y.shape
    n = w.shape[0]
    tm = min(MM_TILE, s)

    def body(y_ref, g_ref, dout_ref, w_ref, du_ref, dy_ref, dg_ref):
        i, j = pl.program_id(0), pl.program_id(1)

        @pl.when(j == 0)
        def _():
            dy, dg = _rms_bwd(y_ref[...], g_ref[...], dout_ref[...])
            dy_ref[...] = dy.astype(BF16)

            @pl.when(i == 0)
            def _():
                dg_ref[...] = jnp.zeros_like(dg_ref)

            dg_ref[...] += dg

        du_ref[...] = lax.dot_general(dy_ref[...], w_ref[...], (((1,), (1,)), ((), ())), preferred_element_type=F32)

    row = pl.BlockSpec((tm, d), lambda i, j: (i, 0))
    return _call(
        body, name=name, grid=(s // tm, n // tn),
        in_specs=[row, _full((1, d)), row, pl.BlockSpec((tn, d), lambda i, j: (j, 0))],
        out_specs=[pl.BlockSpec((tm, tn), lambda i, j: (i, j)), row, _full((1, d))],
        out_shape=[jax.ShapeDtypeStruct((s, n), F32), jax.ShapeDtypeStruct((s, d), BF16), jax.ShapeDtypeStruct((1, d), F32)],
        args=[y, g, dout, w], exchange=exchange)


def matmul_tn(a, b, tm, tn, ts, out_dtype, name, exchange=None):
    s, m = a.shape
    n = b.shape[1]
    ts = min(ts, s)
    n_k = s // ts

    def body(a_ref, b_ref, o_ref, acc):
        k = pl.program_id(2)

        @pl.when(k == 0)
        def _():
            acc[...] = jnp.zeros_like(acc)

        acc[...] += lax.dot_general(a_ref[...], b_ref[...], (((0,), (0,)), ((), ())), preferred_element_type=F32)

        @pl.when(k == n_k - 1)
        def _():
            o_ref[...] = acc[...].astype(out_dtype)

    return _call(
        body, name=name, grid=(m // tm, n // tn, n_k),
        in_specs=[pl.BlockSpec((ts, tm), lambda i, j, k: (k, i)), pl.BlockSpec((ts, tn), lambda i, j, k: (k, j))],
        out_specs=pl.BlockSpec((tm, tn), lambda i, j, k: (i, j)),
        out_shape=jax.ShapeDtypeStruct((m, n), out_dtype),
        scratch_shapes=[pltpu.VMEM((tm, tn), F32)], args=[a, b], exchange=exchange)


def matmul_nt_normbwd(dproj, w, x, g, dres, tk, name, exchange=None):
    s, kt = dproj.shape
    d = w.shape[0]
    tm = min(MM_TILE, s)
    n_k = kt // tk

    def body(a_ref, w_ref, x_ref, g_ref, r_ref, dx_ref, dg_ref, acc):
        i, k = pl.program_id(0), pl.program_id(1)

        @pl.when(k == 0)
        def _():
            acc[...] = jnp.zeros_like(acc)

        acc[...] += lax.dot_general(a_ref[...], w_ref[...], (((1,), (1,)), ((), ())), preferred_element_type=F32)

        @pl.when(k == n_k - 1)
        def _():
            dx, dg = _rms_bwd(x_ref[...], g_ref[...], acc[...])
            dx_ref[...] = r_ref[...] + dx

            @pl.when(i == 0)
            def _():
                dg_ref[...] = jnp.zeros_like(dg_ref)

            dg_ref[...] += dg

    row = pl.BlockSpec((tm, d), lambda i, k: (i, 0))
    return _call(
        body, name=name, grid=(s // tm, n_k),
        in_specs=[pl.BlockSpec((tm, tk), lambda i, k: (i, k)), pl.BlockSpec((d, tk), lambda i, k: (0, k)), row, _full((1, d)), row],
        out_specs=[row, _full((1, d))],
        out_shape=[jax.ShapeDtypeStruct((s, d), F32), jax.ShapeDtypeStruct((1, d), F32)],
        scratch_shapes=[pltpu.VMEM((tm, d), F32)], args=[dproj, w, x, g, dres], exchange=exchange)


def _rg_conv(xa, before, after, cw, cb):
    return (cw[0:1, :] * _shift_rows(xa, before, after, -2) + cw[1:2, :] * _shift_rows(xa, before, after, -1)
            + cw[2:3, :] * xa + cw[3:4, :] * _shift_rows(xa, before, after, 1) + cb)


def _rg_gates(ua_h, gw_ref, gb_ref, c_h, direction, head):
    r = _sigmoid(_bdot(ua_h, gw_ref[2 * direction, head]) + gb_ref[2 * direction, head:head + 1, :])
    i = _sigmoid(_bdot(ua_h, gw_ref[2 * direction + 1, head]) + gb_ref[2 * direction + 1, head:head + 1, :])
    log_a = -c_h * r
    a = jnp.exp(log_a)
    beta = jnp.sqrt(-jnp.tanh(log_a) * (1.0 + a * a))
    return r, i, a, beta


def even_gates_fwd(proj, conv_w, conv_b, gate_w, gate_b, lam, exchange=None):
    s = proj.shape[0]
    ts = min(ROW_TILE, s)
    n_tiles = s // ts

    def body(xa_ref, xb_ref, xn_ref, cw_ref, cb_ref, gw_ref, gb_ref, lam_ref, o_ref):
        xa, before, after = _halo_load(xa_ref, xb_ref, xn_ref, n_tiles)
        ua = _rg_conv(xa, before, after, cw_ref[...], cb_ref[...])
        c = RG_C * _softplus(-lam_ref[...])
        for direction in range(2):
            for head in range(RG_HEADS):
                lanes = slice(head * RG_HEAD_DIM, (head + 1) * RG_HEAD_DIM)
                ua_h = ua[:, lanes]
                _, i, a, beta = _rg_gates(ua_h, gw_ref, gb_ref, c[direction:direction + 1, lanes], direction, head)
                o_ref[2 * direction, :, lanes] = a
                o_ref[2 * direction + 1, :, lanes] = beta * (i * ua_h)

    return _call(
        body, name="even_gates_fwd", grid=(n_tiles,),
        in_specs=_halo_specs(ts, s, D_MODEL, 0) + [_full(conv_w.shape), _full(conv_b.shape), _full(gate_w.shape),
                                                   _full(gate_b.shape), _full(lam.shape)],
        out_specs=pl.BlockSpec((4, ts, D_MODEL), lambda i: (0, i, 0)),
        out_shape=jax.ShapeDtypeStruct((4, s, D_MODEL), F32),
        args=[proj, proj, proj, conv_w, conv_b, gate_w, gate_b, lam], exchange=exchange)


def linear_scan(a_arr, a_idx, b_arr, b_idx, reverse, b_times_a, name, exchange=None):
    _, s, c = a_arr.shape
    ts = min(MM_TILE, s)
    n_tiles = s // ts
    n_blocks = ts // SUBLANES

    def tile_of(i):
        return n_tiles - 1 - i if reverse else i

    def body(a_ref, b_ref, h_ref, carry):
        @pl.when(pl.program_id(0) == 0)
        def _():
            carry[...] = jnp.zeros_like(carry)

        row = lax.broadcasted_iota(jnp.int32, (SUBLANES, c), 0)

        def block(j, h_in):
            r0 = pl.multiple_of((n_blocks - 1 - j if reverse else j) * SUBLANES, SUBLANES)
            a = a_ref[pl.ds(r0, SUBLANES), :]
            b = b_ref[pl.ds(r0, SUBLANES), :]
            if b_times_a:
                b = a * b
            for step in (1, 2, 4):
                shift = SUBLANES - step if reverse else step
                valid = row < SUBLANES - step if reverse else row >= step
                b = jnp.where(valid, a * pltpu.roll(b, shift, 0) + b, b)
                a = jnp.where(valid, a * pltpu.roll(a, shift, 0), a)
            h = a * h_in + b
            h_ref[pl.ds(r0, SUBLANES), :] = h
            return h[0:1, :] if reverse else h[SUBLANES - 1:SUBLANES, :]

        carry[0:1, :] = lax.fori_loop(0, n_blocks, block, carry[0:1, :])

    return _call(
        body, name=name, grid=(n_tiles,),
        in_specs=[pl.BlockSpec((None, ts, c), lambda i: (a_idx, tile_of(i), 0)),
                  pl.BlockSpec((None, ts, c), lambda i: (b_idx, tile_of(i), 0))],
        out_specs=pl.BlockSpec((ts, c), lambda i: (tile_of(i), 0)),
        out_shape=jax.ShapeDtypeStruct((s, c), F32),
        scratch_shapes=[pltpu.VMEM((SUBLANES, c), F32)], args=[a_arr, b_arr], exchange=exchange)


def _sc_conv(p, before, after, w):
    return w[0:1, :] * _shift_rows(p, before, after, -1) + w[1:2, :] * p + w[2:3, :] * _shift_rows(p, before, after, 1)


def even_mix_fwd(hf, hb, proj, sc_w):
    s = proj.shape[0]
    ts = min(ROW_TILE, s)
    n_tiles = s // ts
    row = pl.BlockSpec((ts, D_MODEL), lambda i: (i, 0))

    def col(c):
        return pl.BlockSpec((ts, D_MODEL), lambda i: (i, c))

    def body(hf_ref, hb_ref, za_ref, xb_ref, xbb_ref, xbn_ref, gb_ref, gc_ref, gcb_ref, gcn_ref, zb_ref, w_ref, u_ref):
        xb, xb_before, xb_after = _halo_load(xb_ref, xbb_ref, xbn_ref, n_tiles)
        gc, gc_before, gc_after = _halo_load(gc_ref, gcb_ref, gcn_ref, n_tiles)
        silu_za, _ = _silu_and_grad(za_ref[...])
        silu_zb, _ = _silu_and_grad(zb_ref[...])
        u_ref[:, :D_MODEL] = ((hf_ref[...] + hb_ref[...]) * silu_za).astype(BF16)
        cv = _sc_conv(gc * xb, gc_before * xb_before, gc_after * xb_after, w_ref[...])
        u_ref[:, D_MODEL:] = (gb_ref[...] * cv * silu_zb).astype(BF16)

    return pl.pallas_call(
        body, name="even_mix_fwd", grid=(n_tiles,),
        in_specs=[row, row, col(1)] + _halo_specs(ts, s, D_MODEL, 2) + [col(3)] + _halo_specs(ts, s, D_MODEL, 4)
        + [col(5), _full(sc_w.shape)],
        out_specs=pl.BlockSpec((ts, 2 * D_MODEL), lambda i: (i, 0)),
        out_shape=jax.ShapeDtypeStruct((s, 2 * D_MODEL), BF16),
        compiler_params=_params(1),
    )(hf, hb, proj, proj, proj, proj, proj, proj, proj, proj, proj, sc_w)


def even_mix_bwd(du, hf, hb, proj, sc_w, exchange=None):
    s = proj.shape[0]
    ts = min(ROW_TILE, s)
    n_tiles = s // ts
    row = pl.BlockSpec((ts, D_MODEL), lambda i: (i, 0))

    def body(dya_ref, dyb_ref, dybb_ref, dybn_ref, hf_ref, hb_ref, za_ref, xb_ref, xbb_ref, xbn_ref,
             gb_ref, gbb_ref, gbn_ref, gc_ref, gcb_ref, gcn_ref, zb_ref, zbb_ref, zbn_ref, w_ref,
             dh_ref, dp_ref, dw_ref):
        dyb, dyb_before, dyb_after = _halo_load(dyb_ref, dybb_ref, dybn_ref, n_tiles)
        xb, xb_before, xb_after = _halo_load(xb_ref, xbb_ref, xbn_ref, n_tiles)
        gb, gb_before, gb_after = _halo_load(gb_ref, gbb_ref, gbn_ref, n_tiles)
        gc, gc_before, gc_after = _halo_load(gc_ref, gcb_ref, gcn_ref, n_tiles)
        zb, zb_before, zb_after = _halo_load(zb_ref, zbb_ref, zbn_ref, n_tiles)
        w = w_ref[...]
        dya, za = dya_ref[...], za_ref[...]
        silu_za, dsilu_za = _silu_and_grad(za)
        dh_ref[...] = dya * silu_za
        dp_ref[:, 0:D_MODEL] = (dya * (hf_ref[...] + hb_ref[...]) * dsilu_za).astype(BF16)

        silu_zb, dsilu_zb = _silu_and_grad(zb)
        p, p_before, p_after = gc * xb, gc_before * xb_before, gc_after * xb_after
        cv = _sc_conv(p, p_before, p_after, w)
        dcv = dyb * gb * silu_zb
        dcv_before = dyb_before * gb_before * _silu_and_grad(zb_before)[0]
        dcv_after = dyb_after * gb_after * _silu_and_grad(zb_after)[0]
        dpp = (w[0:1, :] * _shift_rows(dcv, dcv_before, dcv_after, 1) + w[1:2, :] * dcv
               + w[2:3, :] * _shift_rows(dcv, dcv_before, dcv_after, -1))
        dp_ref[:, D_MODEL:2 * D_MODEL] = (dpp * gc).astype(BF16)
        dp_ref[:, 2 * D_MODEL:3 * D_MODEL] = (dyb * cv * silu_zb).astype(BF16)
        dp_ref[:, 3 * D_MODEL:4 * D_MODEL] = (dpp * xb).astype(BF16)
        dp_ref[:, 4 * D_MODEL:5 * D_MODEL] = (dyb * gb * cv * dsilu_zb).astype(BF16)

        @pl.when(pl.program_id(0) == 0)
        def _():
            dw_ref[...] = jnp.zeros_like(dw_ref)

        dw_ref[0:1, :] += jnp.sum(dcv * _shift_rows(p, p_before, p_after, -1), axis=0, keepdims=True)
        dw_ref[1:2, :] += jnp.sum(dcv * p, axis=0, keepdims=True)
        dw_ref[2:3, :] += jnp.sum(dcv * _shift_rows(p, p_before, p_after, 1), axis=0, keepdims=True)

    return _call(
        body, name="even_mix_bwd", grid=(n_tiles,),
        in_specs=[row] + _halo_specs(ts, s, D_MODEL, 1) + [row, row, pl.BlockSpec((ts, D_MODEL), lambda i: (i, 1))]
        + _halo_specs(ts, s, D_MODEL, 2) + _halo_specs(ts, s, D_MODEL, 3) + _halo_specs(ts, s, D_MODEL, 4)
        + _halo_specs(ts, s, D_MODEL, 5) + [_full(sc_w.shape)],
        out_specs=[row, pl.BlockSpec((ts, 5 * D_MODEL), lambda i: (i, 0)), _full(sc_w.shape)],
        out_shape=[jax.ShapeDtypeStruct((s, D_MODEL), F32), jax.ShapeDtypeStruct((s, 5 * D_MODEL), BF16),
                   jax.ShapeDtypeStruct(sc_w.shape, F32)],
        args=[du, du, du, du, hf, hb, proj, *([proj] * 12), sc_w], exchange=exchange)


def even_gates_bwd(proj, adj_f, adj_b, hf, hb, dh, conv_w, conv_b, gate_w, gate_b, lam, exchange=None):
    s = proj.shape[0]
    ts = min(ROW_TILE, s)
    n_tiles = s // ts
    row = pl.BlockSpec((ts, D_MODEL), lambda i: (i, 0))

    def body(xa_ref, xab_ref, xan_ref, af_ref, afb_ref, afn_ref, ab_ref, abb_ref, abn_ref,
             hf_ref, hfb_ref, hfn_ref, hb_ref, hbb_ref, hbn_ref, dh_ref,
             cw_ref, cb_ref, gw_ref, gb_ref, lam_ref, dua_ref, dgw_ref, dgb_ref, dlam_ref):
        @pl.when(pl.program_id(0) == 0)
        def _():
            dgw_ref[...] = jnp.zeros_like(dgw_ref)
            dgb_ref[...] = jnp.zeros_like(dgb_ref)
            dlam_ref[...] = jnp.zeros_like(dlam_ref)

        xa, before, after = _halo_load(xa_ref, xab_ref, xan_ref, n_tiles)
        ua = _rg_conv(xa, before, after, cw_ref[...], cb_ref[...])
        lam_v = lam_ref[...]
        c = RG_C * _softplus(-lam_v)
        dc_dlam = -RG_C * _sigmoid(-lam_v)
        dh = dh_ref[...]
        adj = (_halo_load(af_ref, afb_ref, afn_ref, n_tiles), _halo_load(ab_ref, abb_ref, abn_ref, n_tiles))
        hs = (_halo_load(hf_ref, hfb_ref, hfn_ref, n_tiles), _halo_load(hb_ref, hbb_ref, hbn_ref, n_tiles))
        dua = jnp.zeros_like(ua)
        for direction in range(2):
            step = 1 if direction == 0 else -1
            g = dh + _shift_rows(*adj[direction], step)
            da_all = g * _shift_rows(*hs[direction], -step)
            dua_parts = []
            for head in range(RG_HEADS):
                lanes = slice(head * RG_HEAD_DIM, (head + 1) * RG_HEAD_DIM)
                ua_h = ua[:, lanes]
                c_h = c[direction:direction + 1, lanes]
                r, i, a, beta = _rg_gates(ua_h, gw_ref, gb_ref, c_h, direction, head)
                db = g[:, lanes]
                d_i = db * beta * ua_h
                dbeta = db * (i * ua_h)
                dlog_a = (da_all[:, lanes] - dbeta * a / beta) * a
                dpr = -c_h * dlog_a * r * (1.0 - r)
                dpi = d_i * i * (1.0 - i)
                dua_parts.append(db * beta * i + _bdot_nt(dpr, gw_ref[2 * direction, head])
                                 + _bdot_nt(dpi, gw_ref[2 * direction + 1, head]))
                dgw_ref[2 * direction, head] += _bdot_tn(ua_h, dpr)
                dgw_ref[2 * direction + 1, head] += _bdot_tn(ua_h, dpi)
                dgb_ref[2 * direction, head:head + 1, :] += jnp.sum(dpr, axis=0, keepdims=True)
                dgb_ref[2 * direction + 1, head:head + 1, :] += jnp.sum(dpi, axis=0, keepdims=True)
                dlam_ref[direction:direction + 1, lanes] += (
                    jnp.sum(-r * dlog_a, axis=0, keepdims=True) * dc_dlam[direction:direction + 1, lanes])
            dua = dua + jnp.concatenate(dua_parts, axis=1)
        dua_ref[...] = dua

    return _call(
        body, name="even_gates_bwd", grid=(n_tiles,),
        in_specs=_halo_specs(ts, s, D_MODEL, 0) * 5 + [row] + [_full(conv_w.shape), _full(conv_b.shape), _full(gate_w.shape),
                                                             _full(gate_b.shape), _full(lam.shape)],
        out_specs=[row, _full(gate_w.shape), _full(gate_b.shape), _full(lam.shape)],
        out_shape=[jax.ShapeDtypeStruct((s, D_MODEL), F32), jax.ShapeDtypeStruct(gate_w.shape, F32),
                   jax.ShapeDtypeStruct(gate_b.shape, F32), jax.ShapeDtypeStruct(lam.shape, F32)],
        args=[proj, proj, proj, adj_f, adj_f, adj_f, adj_b, adj_b, adj_b, hf, hf, hf, hb, hb, hb, dh, conv_w, conv_b, gate_w,
              gate_b, lam], exchange=exchange)


def rg_conv_bwd(dua, proj, drest, conv_w, exchange=None):
    s = proj.shape[0]
    ts = min(ROW_TILE, s)
    n_tiles = s // ts

    def body(du_ref, dub_ref, dun_ref, xa_ref, xab_ref, xan_ref, dr_ref, cw_ref, dp_ref, dw_ref, db_ref):
        @pl.when(pl.program_id(0) == 0)
        def _():
            dw_ref[...] = jnp.zeros_like(dw_ref)
            db_ref[...] = jnp.zeros_like(db_ref)

        dua, dua_before, dua_after = _halo_load(du_ref, dub_ref, dun_ref, n_tiles)
        xa, xa_before, xa_after = _halo_load(xa_ref, xab_ref, xan_ref, n_tiles)
        cw = cw_ref[...]
        dxa = (cw[0:1, :] * _shift_rows(dua, dua_before, dua_after, 2) + cw[1:2, :] * _shift_rows(dua, dua_before, dua_after, 1)
               + cw[2:3, :] * dua + cw[3:4, :] * _shift_rows(dua, dua_before, dua_after, -1))
        dp_ref[:, :D_MODEL] = dxa.astype(BF16)
        dp_ref[:, D_MODEL:] = dr_ref[...]
        for tap, offset in enumerate((-2, -1, 0, 1)):
            shifted = xa if offset == 0 else _shift_rows(xa, xa_before, xa_after, offset)
            dw_ref[tap:tap + 1, :] += jnp.sum(dua * shifted, axis=0, keepdims=True)
        db_ref[...] += jnp.sum(dua, axis=0, keepdims=True)

    return _call(
        body, name="rg_conv_bwd", grid=(n_tiles,),
        in_specs=_halo_specs(ts, s, D_MODEL, 0) * 2 + [pl.BlockSpec((ts, 5 * D_MODEL), lambda i: (i, 0)), _full(conv_w.shape)],
        out_specs=[pl.BlockSpec((ts, EVEN_IN), lambda i: (i, 0)), _full(conv_w.shape), _full((1, D_MODEL))],
        out_shape=[jax.ShapeDtypeStruct((s, EVEN_IN), BF16), jax.ShapeDtypeStruct(conv_w.shape, F32),
                   jax.ShapeDtypeStruct((1, D_MODEL), F32)],
        args=[dua, dua, dua, proj, proj, proj, drest, conv_w], exchange=exchange)


def _split3(x):
    x1 = x.astype(BF16)
    rest = x - x1.astype(F32)
    x2 = rest.astype(BF16)
    return x1, x2, (rest - x2.astype(F32)).astype(BF16)


def _chunk_sum_matrix(t, reverse, transpose):
    i = lax.broadcasted_iota(jnp.int32, (t, t), 0)
    j = lax.broadcasted_iota(jnp.int32, (t, t), 1)
    if transpose:
        i, j = j, i
    same = (i // GLA_CHUNK) == (j // GLA_CHUNK)
    return jnp.where(same & ((j >= i) if reverse else (j <= i)), 1.0, 0.0).astype(BF16)


def _exact_dot(m, x):
    return sum(jnp.dot(m, part, preferred_element_type=F32) for part in _split3(x))


def _causal_mask(reverse):
    i = lax.broadcasted_iota(jnp.int32, (GLA_CHUNK, GLA_CHUNK), 0)
    j = lax.broadcasted_iota(jnp.int32, (GLA_CHUNK, GLA_CHUNK), 1)
    return (j >= i) if reverse else (j <= i)


def _gla_gate(lr, wg, bg):
    z = _bdot(lr, wg) + bg
    log_alpha = (jnp.minimum(z, 0.0) - jnp.log(1.0 + jnp.exp(-jnp.abs(z)))) * (1.0 / GLA_NORMALIZER)
    return z, log_alpha


def _gla_chunk_terms(q, k, bcum, rows, reverse):
    bc = bcum[rows, :]
    edge = rows.start if reverse else rows.stop - 1
    btot = bcum[edge:edge + 1, :]
    e_pos, e_neg, e_st = jnp.exp(bc), jnp.exp(-bc), jnp.exp(btot - bc)
    qc, kc = q[rows, :], k[rows, :]
    return qc * (GLA_DK ** -0.5) * e_pos, kc * e_neg, kc * e_st, e_pos, e_neg, e_st, jnp.exp(btot)


def _gla_specs(t, n_tiles, reverse_order):
    def tile(i):
        return n_tiles - 1 - i if reverse_order else i

    return tile, [
        pl.BlockSpec((t, GLA_KEY), lambda i: (tile(i), 0)),
        pl.BlockSpec((t, GLA_KEY), lambda i: (tile(i), 1)),
        pl.BlockSpec((t, D_MODEL), lambda i: (tile(i), 1)),
        pl.BlockSpec((t, LANES), lambda i: (tile(i), (ODD_IN_PAD - LANES) // LANES)),
    ]


def gla_fwd(proj, wg, bg, reverse, o_other=None, gnorm=None):
    s = proj.shape[0]
    t = min(ROW_TILE, s)
    n_tiles = s // t
    n_chunks = t // GLA_CHUNK
    final = o_other is not None
    tile, specs = _gla_specs(t, n_tiles, reverse)

    def body(*refs):
        if final:
            q_ref, k_ref, v_ref, lr_ref, wg_ref, bg_ref, oo_ref, r_ref, gn_ref, osum_ref, u_ref, st_ref, state = refs
        else:
            q_ref, k_ref, v_ref, lr_ref, wg_ref, bg_ref, o_ref, st_ref, state = refs
            osum_ref = o_ref

        @pl.when(pl.program_id(0) == 0)
        def _():
            state[...] = jnp.zeros_like(state)

        _, log_alpha = _gla_gate(lr_ref[...], wg_ref[...], bg_ref[...])
        bcum = _exact_dot(_chunk_sum_matrix(t, reverse, False), log_alpha)
        q, k, v = q_ref[...], k_ref[...], v_ref[...]
        mask = _causal_mask(reverse)
        for c in (range(n_chunks - 1, -1, -1) if reverse else range(n_chunks)):
            rows = slice(c * GLA_CHUNK, (c + 1) * GLA_CHUNK)
            q_in, k_in, k_st, _, _, _, decay = _gla_chunk_terms(q, k, bcum, rows, reverse)
            for head in range(GLA_HEADS):
                kl = slice(head * GLA_DK, (head + 1) * GLA_DK)
                vl = slice(head * GLA_DV, (head + 1) * GLA_DV)
                s_prev = state[head]
                st_ref[c, head] = s_prev
                scores = jnp.where(mask, _bdot_nt(q_in[:, kl], k_in[:, kl]), 0.0)
                osum_ref[rows, vl] = _bdot(scores, v[rows, vl]) + _bdot_nt(q_in[:, kl], s_prev)
                state[head] = s_prev * decay[:, kl] + _bdot_tn(v[rows, vl], k_st[:, kl])
        if final:
            osum = osum_ref[...] + oo_ref[...]
            osum_ref[...] = osum
            silu_r, _ = _silu_and_grad(r_ref[...])
            gn = gn_ref[...]
            for head in range(GLA_HEADS):
                vl = slice(head * GLA_DV, (head + 1) * GLA_DV)
                u_ref[:, vl] = (_rms(osum[:, vl], gn[:, vl]) * silu_r[:, vl]).astype(BF16)

    row = pl.BlockSpec((t, D_MODEL), lambda i: (tile(i), 0))
    st_spec = pl.BlockSpec((n_chunks, GLA_HEADS, GLA_DV, GLA_DK), lambda i: (tile(i), 0, 0, 0))
    st_shape = jax.ShapeDtypeStruct((s // GLA_CHUNK, GLA_HEADS, GLA_DV, GLA_DK), F32)
    in_specs = specs + [_full(wg.shape), _full(bg.shape)]
    args = [proj, proj, proj, proj, wg, bg]
    if final:
        in_specs += [row, pl.BlockSpec((t, D_MODEL), lambda i: (tile(i), 2)), _full(gnorm.shape)]
        args += [o_other, proj, gnorm]
        out_specs = [row, row, st_spec]
        out_shape = [jax.ShapeDtypeStruct((s, D_MODEL), F32), jax.ShapeDtypeStruct((s, D_MODEL), BF16), st_shape]
    else:
        out_specs = [row, st_spec]
        out_shape = [jax.ShapeDtypeStruct((s, D_MODEL), F32), st_shape]
    return pl.pallas_call(
        body, name="gla_fwd_rev" if reverse else "gla_fwd", grid=(n_tiles,), in_specs=in_specs, out_specs=out_specs,
        out_shape=out_shape, scratch_shapes=[pltpu.VMEM((GLA_HEADS, GLA_DV, GLA_DK), F32)], compiler_params=_params(1),
    )(*args)


def gla_out_bwd(du, proj, osum, gnorm):
    s = proj.shape[0]
    ts = min(ROW_TILE, s)
    row = pl.BlockSpec((ts, D_MODEL), lambda i: (i, 0))

    def body(du_ref, r_ref, o_ref, gn_ref, do_ref, dr_ref, dgn_ref):
        @pl.when(pl.program_id(0) == 0)
        def _():
            dgn_ref[...] = jnp.zeros_like(dgn_ref)

        du, osum, gn = du_ref[...], o_ref[...], gn_ref[...]
        silu_r, dsilu_r = _silu_and_grad(r_ref[...])
        for head in range(GLA_HEADS):
            vl = slice(head * GLA_DV, (head + 1) * GLA_DV)
            o_h, g_h, du_h = osum[:, vl], gn[:, vl], du[:, vl]
            dr_ref[:, vl] = (du_h * _rms(o_h, g_h) * dsilu_r[:, vl]).astype(BF16)
            do_h, dg_h = _rms_bwd(o_h, g_h, du_h * silu_r[:, vl])
            do_ref[:, vl] = do_h
            dgn_ref[...] += dg_h

    return pl.pallas_call(
        body, name="gla_out_bwd", grid=(s // ts,),
        in_specs=[row, pl.BlockSpec((ts, D_MODEL), lambda i: (i, 2)), row, _full(gnorm.shape)],
        out_specs=[row, row, _full((1, GLA_DV))],
        out_shape=[jax.ShapeDtypeStruct((s, D_MODEL), F32), jax.ShapeDtypeStruct((s, D_MODEL), BF16),
                   jax.ShapeDtypeStruct((1, GLA_DV), F32)],
        compiler_params=_params(1),
    )(du, proj, osum, gnorm)


def gla_bwd(proj, wg, bg, do, states, reverse, first=None):
    s = proj.shape[0]
    t = min(ROW_TILE, s)
    n_tiles = s // t
    n_chunks = t // GLA_CHUNK
    final = first is not None
    tile, specs = _gla_specs(t, n_tiles, not reverse)

    def body(*refs):
        if final:
            (q_ref, k_ref, v_ref, lr_ref, wg_ref, bg_ref, do_ref, st_ref, dqkv1_ref, dlr1_ref, dr_ref,
             dp_ref, dwg_ref, dbg_ref, dstate, dqkv, dbc, dbt) = refs
        else:
            (q_ref, k_ref, v_ref, lr_ref, wg_ref, bg_ref, do_ref, st_ref,
             dqkv, dlr_ref, dwg_ref, dbg_ref, dstate, dbc, dbt) = refs

        @pl.when(pl.program_id(0) == 0)
        def _():
            dstate[...] = jnp.zeros_like(dstate)
            dwg_ref[...] = jnp.zeros_like(dwg_ref)
            dbg_ref[...] = jnp.zeros_like(dbg_ref)

        lr, wg_v = lr_ref[...], wg_ref[...]
        z, log_alpha = _gla_gate(lr, wg_v, bg_ref[...])
        bcum = _exact_dot(_chunk_sum_matrix(t, reverse, False), log_alpha)
        q, k, v, do_v = q_ref[...], k_ref[...], v_ref[...], do_ref[...]
        mask = _causal_mask(reverse)
        for c in (range(n_chunks) if reverse else range(n_chunks - 1, -1, -1)):
            rows = slice(c * GLA_CHUNK, (c + 1) * GLA_CHUNK)
            q_in, k_in, k_st, e_pos, e_neg, e_st, decay = _gla_chunk_terms(q, k, bcum, rows, reverse)
            dbtot_parts = []
            for head in range(GLA_HEADS):
                kl = slice(head * GLA_DK, (head + 1) * GLA_DK)
                vl = slice(head * GLA_DV, (head + 1) * GLA_DV)
                q_h, k_h, ks_h, v_h, do_h = q_in[:, kl], k_in[:, kl], k_st[:, kl], v[rows, vl], do_v[rows, vl]
                s_prev = st_ref[c, head]
                ds_next = dstate[head]
                scores = jnp.where(mask, _bdot_nt(q_h, k_h), 0.0)
                dscores = jnp.where(mask, _bdot_nt(do_h, v_h), 0.0)
                dq_in = _bdot(dscores, k_h) + _bdot(do_h, s_prev)
                dk_in = _bdot_tn(dscores, q_h)
                dk_st = _bdot(v_h, ds_next)
                dqkv[rows, 2 * GLA_KEY + head * GLA_DV:2 * GLA_KEY + (head + 1) * GLA_DV] = (
                    _bdot_tn(scores, do_h) + _bdot_nt(ks_h, ds_next))
                ddecay = jnp.sum(ds_next * s_prev, axis=0, keepdims=True)
                dstate[head] = ds_next * decay[:, kl] + _bdot_tn(do_h, q_h)
                dqkv[rows, kl] = dq_in * (GLA_DK ** -0.5) * e_pos[:, kl]
                dqkv[rows, GLA_KEY + head * GLA_DK:GLA_KEY + (head + 1) * GLA_DK] = dk_in * e_neg[:, kl] + dk_st * e_st[:, kl]
                dbc[rows, kl] = dq_in * q_h - dk_in * k_h - dk_st * ks_h
                dbtot_parts.append(jnp.sum(dk_st * ks_h, axis=0, keepdims=True) + ddecay * decay[:, kl])
            dbt[rows, :] = jnp.broadcast_to(jnp.concatenate(dbtot_parts, axis=1), (GLA_CHUNK, GLA_KEY))
        dlog_alpha = _exact_dot(_chunk_sum_matrix(t, reverse, True), dbc[...]) + dbt[...]
        dz = dlog_alpha * _sigmoid(-z) * (1.0 / GLA_NORMALIZER)
        dlr = _bdot_nt(dz, wg_v)
        dwg_ref[...] += _bdot_tn(lr, dz)
        dbg_ref[...] += jnp.sum(dz, axis=0, keepdims=True)
        if final:
            dp_ref[:, :2 * D_MODEL] = (dqkv[...] + dqkv1_ref[...]).astype(BF16)
            dp_ref[:, 2 * D_MODEL:3 * D_MODEL] = dr_ref[...]
            dp_ref[:, 3 * D_MODEL:] = (dlr + dlr1_ref[...]).astype(BF16)
        else:
            dlr_ref[...] = dlr

    row = pl.BlockSpec((t, D_MODEL), lambda i: (tile(i), 0))
    wide = pl.BlockSpec((t, 2 * D_MODEL), lambda i: (tile(i), 0))
    narrow = pl.BlockSpec((t, LANES), lambda i: (tile(i), 0))
    st_spec = pl.BlockSpec((n_chunks, GLA_HEADS, GLA_DV, GLA_DK), lambda i: (tile(i), 0, 0, 0))
    in_specs = specs + [_full(wg.shape), _full(bg.shape), row, st_spec]
    args = [proj, proj, proj, proj, wg, bg, do, states]
    acc_specs = [_full(wg.shape), _full(bg.shape)]
    acc_shapes = [jax.ShapeDtypeStruct(wg.shape, F32), jax.ShapeDtypeStruct(bg.shape, F32)]
    scratch = [pltpu.VMEM((GLA_HEADS, GLA_DV, GLA_DK), F32)]
    work = [pltpu.VMEM((t, GLA_KEY), F32), pltpu.VMEM((t, GLA_KEY), F32)]
    if final:
        in_specs += [wide, narrow, row]
        args += list(first)
        out_specs = [pl.BlockSpec((t, ODD_IN_PAD), lambda i: (tile(i), 0))] + acc_specs
        out_shape = [jax.ShapeDtypeStruct((s, ODD_IN_PAD), BF16)] + acc_shapes
        scratch += [pltpu.VMEM((t, 2 * D_MODEL), F32)] + work
    else:
        out_specs = [wide, narrow] + acc_specs
        out_shape = [jax.ShapeDtypeStruct((s, 2 * D_MODEL), F32), jax.ShapeDtypeStruct((s, LANES), F32)] + acc_shapes
        scratch += work
    return pl.pallas_call(
        body, name="gla_bwd_rev" if reverse else "gla_bwd", grid=(n_tiles,), in_specs=in_specs, out_specs=out_specs,
        out_shape=out_shape, scratch_shapes=scratch, compiler_params=_params(1),
    )(*args)


def _adamw_update(g, w, m, v):
    new_m = ADAM_B1 * m + (1.0 - ADAM_B1) * g
    new_v = ADAM_B2 * v + (1.0 - ADAM_B2) * (g * g)
    m_hat = new_m / (1.0 - ADAM_B1 ** ADAM_STEP)
    v_hat = new_v / (1.0 - ADAM_B2 ** ADAM_STEP)
    return -ADAM_LR * (m_hat / (jnp.sqrt(v_hat) + ADAM_EPS) + ADAM_WD * w), new_m, new_v


def sum_parts(parts, name):
    _, r, c = parts.shape

    def body(p_ref, o_ref):
        total = p_ref[0].astype(F32)
        for j in range(1, N_DEV):
            total = total + p_ref[j].astype(F32)
        o_ref[...] = total

    return pl.pallas_call(body, name=name, in_specs=[_full(parts.shape)], out_specs=_full((r, c)), grid=(1,),
                          out_shape=jax.ShapeDtypeStruct((r, c), F32), compiler_params=_params(1))(parts)


def adamw(parts, w, m, v, name):
    n, r, c = parts.shape
    tr = r if r <= MM_TILE else ROW_TILE

    def body(p_ref, w_ref, m_ref, v_ref, g_ref, d_ref, nm_ref, nv_ref):
        g = p_ref[0].astype(F32)
        for j in range(1, n):
            g = g + p_ref[j].astype(F32)
        g_ref[...] = g
        d_ref[...], nm_ref[...], nv_ref[...] = _adamw_update(g, w_ref[...], m_ref[...], v_ref[...])

    row = pl.BlockSpec((tr, c), lambda i: (i, 0))
    return pl.pallas_call(
        body, name=name, grid=(r // tr,),
        in_specs=[pl.BlockSpec((n, tr, c), lambda i: (0, i, 0)), row, row, row], out_specs=[row] * 4,
        out_shape=[jax.ShapeDtypeStruct((r, c), F32)] * 4, compiler_params=_params(1),
    )(parts, w, m, v)


def _small_views(shape):
    if len(shape) == 2:
        return [((slice(None), slice(None)), (slice(None), slice(None)))]
    if len(shape) == 3:
        return [((slice(None), slice(None)), (0,))]
    rows = shape[2]
    return [((slice(k * rows, (k + 1) * rows), slice(None)), (0, k)) for k in range(shape[1])]


def adamw_small(landings, w, m, v):
    names = list(landings)
    n = len(names)
    shapes = [w[name].shape for name in names]

    def body(*refs):
        land, ws, ms, vs = refs[:n], refs[n:2 * n], refs[2 * n:3 * n], refs[3 * n:4 * n]
        outs = [refs[(4 + k) * n:(5 + k) * n] for k in range(4)]
        for k in range(n):
            total = land[k][0]
            for j in range(1, N_DEV):
                total = total + land[k][j]
            for rows, at in _small_views(shapes[k]):
                g = total[rows]
                outs[0][k][at] = g
                outs[1][k][at], outs[2][k][at], outs[3][k][at] = _adamw_update(g, ws[k][at], ms[k][at], vs[k][at])

    blocks = [_full(sh) for sh in shapes]
    outs = pl.pallas_call(
        body, name="adamw_small", grid=(1,),
        in_specs=[_full(landings[name].shape) for name in names] + blocks * 3, out_specs=blocks * 4,
        out_shape=[jax.ShapeDtypeStruct(sh, F32) for sh in shapes] * 4, compiler_params=_params(1),
    )(*[landings[name] for name in names], *[src[name] for src in (w, m, v) for name in names])
    return [dict(zip(names, outs[k * n:(k + 1) * n])) for k in range(4)]


def adamw_replicated(land_vec, land_gate_b, land_loss, names, w, m, v, gate_b):
    n = len(names)

    def body(*refs):
        vec_ref, gb_ref, loss_ref = refs[:3]
        ws, ms, vs = refs[3:3 + n], refs[3 + n:3 + 2 * n], refs[3 + 2 * n:3 + 3 * n]
        gw_ref, gm_ref, gv_ref = refs[3 + 3 * n:6 + 3 * n]
        outs = refs[6 + 3 * n:]
        vec, gb, loss = vec_ref[0], gb_ref[0], loss_ref[0]
        for j in range(1, N_DEV):
            vec, gb, loss = vec + vec_ref[j], gb + gb_ref[j], loss + loss_ref[j]
        for k in range(n):
            g = vec[k:k + 1, :]
            outs[k][...] = g
            outs[n + k][...], outs[2 * n + k][...], outs[3 * n + k][...] = _adamw_update(g, ws[k][...], ms[k][...], vs[k][...])
        outs[4 * n][...] = gb
        outs[4 * n + 1][...], outs[4 * n + 2][...], outs[4 * n + 3][...] = _adamw_update(gb, gw_ref[...], gm_ref[...], gv_ref[...])
        outs[4 * n + 4][...] = loss

    vec_block, gb_block = _full((1, D_MODEL)), _full(gate_b[0].shape)
    outs = pl.pallas_call(
        body, name="adamw_replicated", grid=(1,),
        in_specs=[_full(land_vec.shape), _full(land_gate_b.shape), _full(land_loss.shape)] + [vec_block] * (3 * n) + [gb_block] * 3,
        out_specs=[vec_block] * (4 * n) + [gb_block] * 4 + [_full(land_loss.shape[1:])],
        out_shape=[jax.ShapeDtypeStruct((1, D_MODEL), F32)] * (4 * n) + [jax.ShapeDtypeStruct(gate_b[0].shape, F32)] * 4
        + [jax.ShapeDtypeStruct(land_loss.shape[1:], F32)],
        compiler_params=_params(1),
    )(land_vec, land_gate_b, land_loss, *[src[name] for src in (w, m, v) for name in names], *gate_b)
    results = {name: [outs[k * n + i] for k in range(4)] for i, name in enumerate(names)}
    return results, outs[4 * n:4 * n + 4], outs[4 * n + 4]


SMALL_SHARDED = ("rg_conv_w", "rg_lambda", "sc_conv_w", "odd_norm_pre", "odd_norm_post", "gla_b_gate", "gla_norm_g", "gla_w_gate_lr")
SMALL_ROWS = {"rg_conv_w": (0, 4), "rg_lambda": (4, 2), "sc_conv_w": (6, 3), "odd_norm_pre": (9, 1), "odd_norm_post": (10, 1),
              "gla_b_gate": (11, 2), "gla_norm_g": (13, 1), "gla_w_gate_lr": (16, 32)}


def _pack_small(shards):
    pieces, at = [], 0
    for name in SMALL_SHARDED:
        start, rows = SMALL_ROWS[name]
        if start > at:
            pieces.append(jnp.zeros((start - at, LANES), F32))
        a = shards[name].reshape(rows, -1)
        pieces.append(jnp.pad(a, ((0, 0), (0, LANES - a.shape[1]))))
        at = start + rows
    return jnp.concatenate(pieces, axis=0)


def _unpack_gathered(g):
    def cols(name, width):
        start, rows = SMALL_ROWS[name]
        return jnp.transpose(g[:, start:start + rows, :width], (1, 0, 2)).reshape(rows, N_DEV * width)

    w_lr = cols("gla_w_gate_lr", GLA_KEY // N_DEV).reshape(2, GLA_RANK, GLA_KEY)
    return dict(rg_conv_w=cols("rg_conv_w", LANES), rg_lambda=cols("rg_lambda", LANES), sc_conv_w=cols("sc_conv_w", LANES),
                odd_norm_pre=cols("odd_norm_pre", LANES), odd_norm_post=cols("odd_norm_post", LANES),
                gla_b_gate=cols("gla_b_gate", GLA_KEY // N_DEV), gla_norm_g=cols("gla_norm_g", GLA_DV // N_DEV), gla_w_gate_lr=w_lr)


def _blocks_along_columns(a, rows):
    return jnp.transpose(a.reshape(rows, N_DEV, -1), (1, 0, 2))


def kernel(x, even_norm_pre, even_norm_post, even_w_in, rg_conv_w, rg_conv_b, rg_gate_w, rg_gate_b, rg_lambda, sc_conv_w, even_w_out, odd_norm_pre, odd_norm_post, odd_w_in, gla_w_gate_lr, gla_b_gate, gla_norm_g, odd_w_out, loss_target, m_even_norm_pre, m_even_norm_post, m_even_w_in, m_rg_conv_w, m_rg_conv_b, m_rg_gate_w, m_rg_gate_b, m_rg_lambda, m_sc_conv_w, m_even_w_out, m_odd_norm_pre, m_odd_norm_post, m_odd_w_in, m_gla_w_gate_lr, m_gla_b_gate, m_gla_norm_g, m_odd_w_out, v_even_norm_pre, v_even_norm_post, v_even_w_in, v_rg_conv_w, v_rg_conv_b, v_rg_gate_w, v_rg_gate_b, v_rg_lambda, v_sc_conv_w, v_even_w_out, v_odd_norm_pre, v_odd_norm_post, v_odd_w_in, v_gla_w_gate_lr, v_gla_b_gate, v_gla_norm_g, v_odd_w_out):
    weights = dict(even_norm_pre=even_norm_pre, even_norm_post=even_norm_post, even_w_in=even_w_in, rg_conv_w=rg_conv_w,
                   rg_conv_b=rg_conv_b, rg_gate_w=rg_gate_w, rg_gate_b=rg_gate_b, rg_lambda=rg_lambda, sc_conv_w=sc_conv_w,
                   even_w_out=even_w_out, odd_norm_pre=odd_norm_pre, odd_norm_post=odd_norm_post, odd_w_in=odd_w_in,
                   gla_w_gate_lr=gla_w_gate_lr, gla_b_gate=gla_b_gate, gla_norm_g=gla_norm_g, odd_w_out=odd_w_out)
    m_in = dict(even_norm_pre=m_even_norm_pre, even_norm_post=m_even_norm_post, even_w_in=m_even_w_in, rg_conv_w=m_rg_conv_w,
                rg_conv_b=m_rg_conv_b, rg_gate_w=m_rg_gate_w, rg_gate_b=m_rg_gate_b, rg_lambda=m_rg_lambda, sc_conv_w=m_sc_conv_w,
                even_w_out=m_even_w_out, odd_norm_pre=m_odd_norm_pre, odd_norm_post=m_odd_norm_post, odd_w_in=m_odd_w_in,
                gla_w_gate_lr=m_gla_w_gate_lr, gla_b_gate=m_gla_b_gate, gla_norm_g=m_gla_norm_g, odd_w_out=m_odd_w_out)
    v_in = dict(even_norm_pre=v_even_norm_pre, even_norm_post=v_even_norm_post, even_w_in=v_even_w_in, rg_conv_w=v_rg_conv_w,
                rg_conv_b=v_rg_conv_b, rg_gate_w=v_rg_gate_w, rg_gate_b=v_rg_gate_b, rg_lambda=v_rg_lambda, sc_conv_w=v_sc_conv_w,
                even_w_out=v_even_w_out, odd_norm_pre=v_odd_norm_pre, odd_norm_post=v_odd_norm_post, odd_w_in=v_odd_w_in,
                gla_w_gate_lr=v_gla_w_gate_lr, gla_b_gate=v_gla_b_gate, gla_norm_g=v_gla_norm_g, odd_w_out=v_odd_w_out)
    names = list(weights)
    shapes = {n: weights[n].shape for n in names}
    xs = x[0]
    tgt = loss_target[0]

    first = Exchange()
    first.gather(even_w_in[0].astype(BF16), columns=True, via_sibling=True)
    first.gather(_pack_small({n: weights[n][0] for n in SMALL_SHARDED}))
    w_in_e, small_all = run_exchange(first, "gather_first")
    small = _unpack_gathered(small_all)
    gate_w = rg_gate_w[0].reshape(4, RG_HEADS, RG_HEAD_DIM, RG_HEAD_DIM).astype(BF16)
    gate_b = rg_gate_b[0].reshape(4, RG_HEADS, RG_HEAD_DIM)
    conv_b = rg_conv_b
    wg_pad = [jnp.pad(small["gla_w_gate_lr"][d], ((GLA_RANK * d, LANES - GLA_RANK * (d + 1)), (0, 0))).astype(BF16) for d in range(2)]
    bg = [small["gla_b_gate"][d:d + 1] for d in range(2)]
    gnorm = jnp.tile(small["gla_norm_g"], (1, GLA_HEADS))

    behind_in = Exchange()
    behind_in.gather(odd_w_in[0].astype(BF16))
    (proj_e, h_e), (w_in_o_blocks,) = rms_matmul(xs, even_norm_pre, w_in_e, 2 * MM_TILE, 2 * EVEN_SHARD, "even_in", exchange=behind_in)
    w_in_o = jnp.transpose(w_in_o_blocks, (1, 0, 2)).reshape(D_MODEL, ODD_IN)
    w_in_o = jnp.pad(w_in_o, ((0, 0), (0, ODD_IN_PAD - ODD_IN)))
    behind_gates = Exchange()
    behind_gates.gather(even_w_out[0].astype(BF16))
    ab, (w_out_e,) = even_gates_fwd(proj_e, small["rg_conv_w"], conv_b, gate_w, gate_b, small["rg_lambda"], exchange=behind_gates)
    w_out_e = w_out_e.reshape(2 * D_MODEL, D_MODEL)
    behind_scan = Exchange()
    behind_scan.gather(odd_w_out[0].astype(BF16))
    hf, (w_out_o,) = linear_scan(ab, 0, ab, 1, False, False, "scan_fwd", exchange=behind_scan)
    w_out_o = w_out_o.reshape(D_MODEL, D_MODEL)
    hb = linear_scan(ab, 2, ab, 3, True, False, "scan_rev")
    u_e = even_mix_fwd(hf, hb, proj_e, small["sc_conv_w"])
    y_e, x1 = matmul_post(u_e, w_out_e, xs, even_norm_post, "even_out")

    proj_o, h_o = rms_matmul(x1, small["odd_norm_pre"], w_in_o, MM_TILE, ODD_IN_PAD, "odd_in")
    o_f, st_f = gla_fwd(proj_o, wg_pad[0], bg[0], False)
    osum, u_o, st_b = gla_fwd(proj_o, wg_pad[1], bg[1], True, o_other=o_f, gnorm=gnorm)
    y_o, dout, loss_part = matmul_post(u_o, w_out_o, x1, small["odd_norm_post"], "odd_out", target=tgt)

    du_o, dy_o, d_odd_norm_post = normbwd_matmul_nt(y_o, small["odd_norm_post"], dout, w_out_o, D_MODEL, "odd_out_bwd")
    d_w_out_o = matmul_tn(u_o, dy_o, D_MODEL, D_MODEL, 4 * MM_TILE, BF16, "odd_w_out_grad")
    do, dr, d_gnorm = gla_out_bwd(du_o, proj_o, osum, gnorm)
    dqkv_f, dlr_f, dwg_f, dbg_f = gla_bwd(proj_o, wg_pad[0], bg[0], do, st_f, False)
    dproj_o, dwg_b, dbg_b = gla_bwd(proj_o, wg_pad[1], bg[1], do, st_b, True, first=(dqkv_f, dlr_f, dr))
    dx1, d_odd_norm_pre = matmul_nt_normbwd(dproj_o, w_in_o, x1, small["odd_norm_pre"], dout, ODD_IN_PAD, "odd_in_bwd")
    d_w_in_o = matmul_tn(h_o, dproj_o, D_MODEL, ODD_IN_PAD // 5, 8 * MM_TILE, BF16, "odd_w_in_grad")

    landed = {}
    behind_out = Exchange()
    behind_out.scatter(d_w_out_o.reshape(N_DEV, D_MODEL // N_DEV, D_MODEL))
    behind_out.scatter(d_odd_norm_pre, columns=True)
    behind_out.scatter(d_odd_norm_post, columns=True)
    behind_out.scatter(_blocks_along_columns(jnp.concatenate([dbg_f, dbg_b], axis=0), 2))
    behind_out.scatter(_blocks_along_columns(d_gnorm, 1))
    behind_out.scatter(_blocks_along_columns(jnp.concatenate([dwg_f[:GLA_RANK], dwg_b[GLA_RANK:2 * GLA_RANK]], axis=0), 2 * GLA_RANK))
    (du_e, dy_e, d_even_norm_post), got = normbwd_matmul_nt(y_e, even_norm_post, dx1, w_out_e, 2 * D_MODEL, "even_out_bwd",
                                                           exchange=behind_out)
    p_w_out_o = got[0]
    for n, part in zip(("odd_norm_pre", "odd_norm_post", "gla_b_gate", "gla_norm_g", "gla_w_gate_lr"), got[1:]):
        landed[n] = part
    d_w_out_e = matmul_tn(u_e, dy_e, D_MODEL, D_MODEL, 4 * MM_TILE, BF16, "even_w_out_grad")
    behind_mix = Exchange()
    behind_mix.scatter(d_w_out_e.reshape(N_DEV, 2 * D_MODEL // N_DEV, D_MODEL))
    (dh, drest, d_sc_w), (p_w_out_e,) = even_mix_bwd(du_e, hf, hb, proj_e, small["sc_conv_w"], exchange=behind_mix)
    dh3 = dh.reshape(1, *dh.shape)
    adj_f = linear_scan(ab, 0, dh3, 0, True, True, "scan_fwd_adjoint")
    adj_b = linear_scan(ab, 2, dh3, 0, False, True, "scan_rev_adjoint")
    behind_gates_bwd = Exchange()
    behind_gates_bwd.scatter(jnp.transpose(d_w_in_o[:, :ODD_IN].reshape(D_MODEL, N_DEV, ODD_SHARD), (1, 0, 2)))
    behind_gates_bwd.scatter(d_sc_w, columns=True)
    (dua, d_gate_w, d_gate_b, d_lam), (p_w_in_o, landed["sc_conv_w"]) = even_gates_bwd(
        proj_e, adj_f, adj_b, hf, hb, dh, small["rg_conv_w"], conv_b, gate_w, gate_b, small["rg_lambda"], exchange=behind_gates_bwd)
    gate_w_rows = 4 * RG_HEADS * RG_HEAD_DIM
    behind_conv = Exchange()
    behind_conv.scatter(d_gate_w.reshape(N_DEV, gate_w_rows // N_DEV, RG_HEAD_DIM))
    behind_conv.scatter(d_lam, columns=True)
    (dproj_e, d_conv_w, d_conv_b), (p_gate_w, landed["rg_lambda"]) = rg_conv_bwd(dua, proj_e, drest, small["rg_conv_w"],
                                                                                 exchange=behind_conv)
    behind_w_grad = Exchange()
    behind_w_grad.gather(sum_parts(p_gate_w, "sum_gate_w"))
    d_w_in_e, (g_gate_w_all,) = matmul_tn(h_e, dproj_e, D_MODEL, EVEN_SHARD, 8 * MM_TILE, BF16, "even_w_in_grad",
                                          exchange=behind_w_grad)
    behind_in_bwd = Exchange()
    behind_in_bwd.scatter(d_w_in_e, columns=True)
    behind_in_bwd.scatter(d_conv_w, columns=True)
    (grad_x, d_even_norm_pre), (p_w_in_e, landed["rg_conv_w"]) = matmul_nt_normbwd(
        dproj_e, w_in_e, xs, even_norm_pre, dx1, 3 * D_MODEL, "even_in_bwd", exchange=behind_in_bwd)
    last = Exchange()
    replicated_vecs = ("even_norm_pre", "even_norm_post", "rg_conv_b")
    last.gather(jnp.concatenate([d_even_norm_pre, d_even_norm_post, d_conv_b], axis=0))
    last.gather(d_gate_b.reshape(4 * RG_HEADS, RG_HEAD_DIM))
    last.gather(loss_part)
    land_vec, land_gate_b, land_loss = run_exchange(last, "gather_last")

    results = {}

    def update(name, parts_, shape2d):
        outs = adamw(parts_, weights[name][0].reshape(shape2d), m_in[name][0].reshape(shape2d), v_in[name][0].reshape(shape2d),
                     "adamw_" + name)
        results[name] = [o.reshape(shapes[name]) for o in outs]

    update("even_w_in", p_w_in_e, (D_MODEL, EVEN_SHARD))
    update("even_w_out", p_w_out_e, (2 * D_MODEL // N_DEV, D_MODEL))
    update("odd_w_in", p_w_in_o, (D_MODEL, ODD_SHARD))
    update("odd_w_out", p_w_out_o, (D_MODEL // N_DEV, D_MODEL))
    update("rg_gate_w", g_gate_w_all.reshape(1, gate_w_rows, RG_HEAD_DIM), (gate_w_rows, RG_HEAD_DIM))
    small_out = adamw_small({n: landed[n] for n in SMALL_SHARDED}, weights, m_in, v_in)
    for n in SMALL_SHARDED:
        results[n] = [o[n] for o in small_out]
    gate_b_shape = (4 * RG_HEADS, RG_HEAD_DIM)
    rep_out, gate_b_out, loss_all = adamw_replicated(land_vec, land_gate_b, land_loss, replicated_vecs, weights, m_in, v_in,
                                                     [src["rg_gate_b"].reshape(gate_b_shape) for src in (weights, m_in, v_in)])
    results.update(rep_out)
    results["rg_gate_b"] = [o.reshape(shapes["rg_gate_b"]) for o in gate_b_out]

    return (loss_all[0, 0], grad_x.reshape(x.shape), *[results[n][0] for n in names], *[results[n][1] for n in names],
            *[results[n][2] for n in names], *[results[n][3] for n in names])
```

```python
import functools

import jax
import jax.numpy as jnp
from jax import lax
from jax.experimental import pallas as pl
from jax.experimental.pallas import tpu as pltpu

F32 = jnp.float32
BF16 = jnp.bfloat16

N_DEV = 8
D_MODEL = 1024
NORM_EPS = 1e-6
RG_HEADS = 8
RG_HEAD_DIM = 128
RG_C = 8.0
GLA_HEADS = 4
GLA_DK = 128
GLA_DV = 256
GLA_KEY = 512
GLA_RANK = 16
GLA_NORMALIZER = 16.0
GLA_CHUNK = 64
EVEN_IN = 6144
ODD_IN = 3104
ODD_IN_PAD = 3200
ODD_SHARD = ODD_IN // N_DEV
EVEN_SHARD = EVEN_IN // N_DEV
ADAM_LR = 0.001
ADAM_B1 = 0.9
ADAM_B2 = 0.999
ADAM_EPS = 1e-08
ADAM_WD = 0.01
ADAM_STEP = 10

SUBLANES = 8
LANES = 128
VMEM_LIMIT_BYTES = 48 * 2 ** 20
ROW_TILE = 256
MM_TILE = 512
PACK_ROWS = 48
MESH_ID = pl.DeviceIdType.MESH


def _params(n_grid):
    return pltpu.CompilerParams(dimension_semantics=("arbitrary",) * n_grid, vmem_limit_bytes=VMEM_LIMIT_BYTES)


def _bdot(a, b):
    return jnp.dot(a.astype(BF16), b.astype(BF16), preferred_element_type=F32)


def _bdot_nt(a, b):
    return lax.dot_general(a.astype(BF16), b.astype(BF16), (((1,), (1,)), ((), ())), preferred_element_type=F32)


def _bdot_tn(a, b):
    return lax.dot_general(a.astype(BF16), b.astype(BF16), (((0,), (0,)), ((), ())), preferred_element_type=F32)


def _rstd(x):
    return lax.rsqrt(jnp.mean(x * x, axis=-1, keepdims=True) + NORM_EPS)


def _rms(x, g):
    return x * _rstd(x) * g


def _rms_bwd(x, g, dy):
    xh = x * _rstd(x)
    dyg = dy * g
    dx = _rstd(x) * (dyg - xh * jnp.mean(dyg * xh, axis=-1, keepdims=True))
    return dx, jnp.sum(dy * xh, axis=0, keepdims=True)


def _sigmoid(z):
    return 0.5 * jnp.tanh(0.5 * z) + 0.5


def _silu_and_grad(z):
    s = _sigmoid(z)
    return z * s, s * (1.0 + z * (1.0 - s))


def _softplus(z):
    return jnp.maximum(z, 0.0) + jnp.log(1.0 + jnp.exp(-jnp.abs(z)))


def _shift_rows(cur, before, after, d):
    ts = cur.shape[0]
    row = lax.broadcasted_iota(jnp.int32, (SUBLANES, cur.shape[1]), 0)
    out = pltpu.roll(cur, (-d) % ts, 0)
    if d < 0:
        edge = jnp.where(row < -d, pltpu.roll(before, (-d) % SUBLANES, 0), out[:SUBLANES])
        return jnp.concatenate([edge, out[SUBLANES:]], axis=0)
    edge = jnp.where(row >= SUBLANES - d, pltpu.roll(after, (-d) % SUBLANES, 0), out[ts - SUBLANES:])
    return jnp.concatenate([out[:ts - SUBLANES], edge], axis=0)


def _halo_specs(ts, s, width, col):
    per = ts // SUBLANES
    last = s // SUBLANES - 1
    return [
        pl.BlockSpec((ts, width), lambda i: (i, col)),
        pl.BlockSpec((SUBLANES, width), lambda i: (jnp.maximum(i * per - 1, 0), col)),
        pl.BlockSpec((SUBLANES, width), lambda i: (jnp.minimum((i + 1) * per, last), col)),
    ]


def _halo_load(cur_ref, before_ref, after_ref, n_tiles):
    i = pl.program_id(0)
    before = jnp.where(i > 0, before_ref[...], 0.0)
    after = jnp.where(i < n_tiles - 1, after_ref[...], 0.0)
    return cur_ref[...], before, after


def _full(shape):
    return pl.BlockSpec(shape, lambda *_: (0,) * len(shape))


def _peer(x, y, c, mask):
    px, py, pc = x ^ (mask >> 2), y ^ ((mask >> 1) & 1), c ^ (mask & 1)
    return (px, py, pc), 4 * px + 2 * py + pc


class Exchange:
    SIBLING = 1
    OTHER_CHIPS = (2, 4, 6)

    def __init__(self):
        self.args, self.out_shape, self._kinds = [], [], []

    def gather(self, block, columns=False, via_sibling=False):
        shape = (block.shape[0], N_DEV * block.shape[1]) if columns else (N_DEV,) + block.shape
        return self._add(block, shape, ("gather", columns, via_sibling))

    def scatter(self, stack, columns=False):
        shape = (N_DEV, stack.shape[0], stack.shape[1] // N_DEV) if columns else stack.shape
        return self._add(stack, shape, ("scatter", columns, False))

    def _add(self, arg, shape, kind):
        self.args.append(arg)
        self.out_shape.append(jax.ShapeDtypeStruct(shape, arg.dtype))
        self._kinds.append(kind)
        return len(self.args) - 1

    def semaphores(self):
        n = len(self.args)
        return [pltpu.SemaphoreType.DMA((n, N_DEV - 1)), pltpu.SemaphoreType.DMA((n, N_DEV - 1)), pltpu.SemaphoreType.DMA((n,))]

    def to_sibling(self, array):
        shape = (N_DEV // 2, array.shape[0], array.shape[1] // N_DEV)
        return self._add(array, shape, ("to_sibling", True, False))

    def among_chips(self, stack):
        return self._add(stack, stack.shape, ("among_chips", False, False))

    def _copies(self, position, in_refs, out_refs):
        x, y, c, me = position
        for arr, ((kind, columns, via_sibling), src, out) in enumerate(zip(self._kinds, in_refs, out_refs)):
            if kind == "to_sibling":
                width = src.shape[-1] // N_DEV
                for k in range(N_DEV // 2):
                    block = src.at[:, pl.ds(pl.multiple_of((2 * k + 1 - c) * width, LANES), width)]
                    yield arr, k + 1, block, out.at[k], out.at[k], False, self.SIBLING
                continue
            for mask in range(N_DEV):
                _, peer_id = _peer(x, y, c, mask)
                relayed = via_sibling and mask not in (0, self.SIBLING) + self.OTHER_CHIPS
                if kind == "among_chips":
                    if mask in (0,) + self.OTHER_CHIPS:
                        yield arr, mask, src.at[peer_id // 2], out.at[me // 2], out.at[peer_id // 2], False, mask
                elif kind == "gather":
                    if columns:
                        width = src.shape[-1]
                        yield (arr, mask, src, out.at[:, pl.ds(pl.multiple_of(me * width, LANES), width)],
                               out.at[:, pl.ds(pl.multiple_of(peer_id * width, LANES), width)], relayed, mask)
                    else:
                        yield arr, mask, src, out.at[me], out.at[peer_id], relayed, mask
                else:
                    if columns:
                        width = src.shape[-1] // N_DEV
                        block = src.at[:, pl.ds(pl.multiple_of(peer_id * width, LANES), width)]
                    else:
                        block = src.at[peer_id]
                    yield arr, mask, block, out.at[me], out.at[peer_id], False, mask

    def _remote(self, position, sems, arr, slot, to_mask, src, dst):
        x, y, c, _ = position
        return pltpu.make_async_remote_copy(src_ref=src, dst_ref=dst, send_sem=sems[0].at[arr, slot - 1], recv_sem=sems[1].at[arr, slot - 1],
                                            device_id=_peer(x, y, c, to_mask)[0], device_id_type=MESH_ID)

    def start(self, position, in_refs, out_refs, sems):
        for arr, slot, src, dst, _, relayed, to_mask in self._copies(position, in_refs, out_refs):
            if slot == 0:
                pltpu.make_async_copy(src, dst, sems[2].at[arr]).start()
            elif not relayed:
                self._remote(position, sems, arr, slot, to_mask, src, dst).start()

    def wait(self, position, in_refs, out_refs, sems):
        copies = list(self._copies(position, in_refs, out_refs))
        landings = {(arr, slot): landing for arr, slot, _, _, landing, _, _ in copies}
        passed_on = set()
        for arr, mask, src, _, landing, relayed, _ in copies:
            if relayed:
                held = landings[arr, mask ^ self.SIBLING]
                self._remote(position, sems, arr, mask ^ self.SIBLING, mask ^ self.SIBLING, src, held).wait_recv()
                self._remote(position, sems, arr, mask, self.SIBLING, held, held).start()
                passed_on.add((arr, mask ^ self.SIBLING))
        for arr, slot, src, dst, landing, relayed, to_mask in copies:
            if slot == 0:
                pltpu.make_async_copy(src, dst, sems[2].at[arr]).wait()
                continue
            if (arr, slot) not in passed_on:
                self._remote(position, sems, arr, slot, to_mask, src, landing).wait_recv()
            if relayed:
                held = landings[arr, slot ^ self.SIBLING]
                self._remote(position, sems, arr, slot, self.SIBLING, held, held).wait_send()
            else:
                self._remote(position, sems, arr, slot, to_mask, src, dst).wait_send()


def _call(body, *, name, grid, in_specs, out_specs, out_shape, args, scratch_shapes=(), exchange=None):
    single = not isinstance(out_shape, (list, tuple))
    if single:
        out_specs, out_shape = [out_specs], [out_shape]
    params = _params(len(grid))
    if exchange is None:
        outs = pl.pallas_call(body, name=name, grid=grid, in_specs=in_specs, out_specs=out_specs, out_shape=out_shape,
                              scratch_shapes=list(scratch_shapes), compiler_params=params)(*args)
        return outs[0] if single else outs
    counts = (len(args), len(exchange.args), len(out_shape), len(exchange.out_shape), len(scratch_shapes), 3)

    def wrapped(*refs):
        groups, at = [], 0
        for n in counts:
            groups.append(refs[at:at + n])
            at += n
        main_in, ex_in, main_out, ex_out, main_scratch, sems = groups
        x, y, c = lax.axis_index("x"), lax.axis_index("y"), lax.axis_index("c")
        position = (x, y, c, 4 * x + 2 * y + c)
        ids = [pl.program_id(a) for a in range(len(grid))]
        first = functools.reduce(jnp.logical_and, [i == 0 for i in ids])
        last = functools.reduce(jnp.logical_and, [i == g - 1 for i, g in zip(ids, grid)])

        @pl.when(first)
        def _():
            exchange.start(position, ex_in, ex_out, sems)

        body(*main_in, *main_out, *main_scratch)

        @pl.when(last)
        def _():
            exchange.wait(position, ex_in, ex_out, sems)

    hbm = pl.BlockSpec(memory_space=pl.ANY)
    outs = pl.pallas_call(
        wrapped, name=name, grid=grid, in_specs=list(in_specs) + [hbm] * counts[1], out_specs=list(out_specs) + [hbm] * counts[3],
        out_shape=list(out_shape) + exchange.out_shape, scratch_shapes=list(scratch_shapes) + exchange.semaphores(),
        compiler_params=params)(*args, *exchange.args)
    main = outs[:counts[2]]
    return (main[0] if single else main), outs[counts[2]:]


def run_exchange(exchange, name):
    return _call(lambda: None, name=name, grid=(1,), in_specs=[], out_specs=[], out_shape=[], args=[], exchange=exchange)[1]


def rms_matmul(x, g, w, tm, tn, name, exchange=None):
    s, d = x.shape
    n = w.shape[1]
    tm = min(tm, s)

    def body(x_ref, g_ref, w_ref, o_ref, h_ref):
        @pl.when(pl.program_id(1) == 0)
        def _():
            h_ref[...] = _rms(x_ref[...], g_ref[...]).astype(BF16)

        o_ref[...] = jnp.dot(h_ref[...], w_ref[...], preferred_element_type=F32)

    return _call(
        body, name=name, grid=(s // tm, n // tn),
        in_specs=[pl.BlockSpec((tm, d), lambda i, j: (i, 0)), _full((1, d)), pl.BlockSpec((d, tn), lambda i, j: (0, j))],
        out_specs=[pl.BlockSpec((tm, tn), lambda i, j: (i, j)), pl.BlockSpec((tm, d), lambda i, j: (i, 0))],
        out_shape=[jax.ShapeDtypeStruct((s, n), F32), jax.ShapeDtypeStruct((s, d), BF16)],
        args=[x, g, w], exchange=exchange)


def matmul_post(u, w, xres, g, name, target=None):
    s, k = u.shape
    d = w.shape[1]
    tm = min(MM_TILE, s)
    with_loss = target is not None

    def body(*refs):
        if with_loss:
            u_ref, w_ref, x_ref, g_ref, t_ref, y_ref, dout_ref, loss_ref = refs
        else:
            u_ref, w_ref, x_ref, g_ref, y_ref, out_ref = refs
        y = jnp.dot(u_ref[...], w_ref[...], preferred_element_type=F32)
        y_ref[...] = y
        out = x_ref[...] + _rms(y, g_ref[...])
        if with_loss:
            @pl.when(pl.program_id(0) == 0)
            def _():
                loss_ref[...] = jnp.zeros_like(loss_ref)

            diff = out - t_ref[...]
            dout_ref[...] = diff * (1.0 / d)
            loss_ref[...] += 0.5 * jnp.sum(jnp.mean(diff * diff, axis=-1, keepdims=True))
        else:
            out_ref[...] = out

    row = pl.BlockSpec((tm, d), lambda i: (i, 0))
    in_specs = [pl.BlockSpec((tm, k), lambda i: (i, 0)), _full((k, d)), row, _full((1, d))]
    args = [u, w, xres, g]
    out_specs = [row, row]
    out_shape = [jax.ShapeDtypeStruct((s, d), F32), jax.ShapeDtypeStruct((s, d), F32)]
    if with_loss:
        in_specs.append(row)
        args.append(target)
        out_specs.append(_full((SUBLANES, LANES)))
        out_shape.append(jax.ShapeDtypeStruct((SUBLANES, LANES), F32))
    return pl.pallas_call(body, name=name, grid=(s // tm,), in_specs=in_specs, out_specs=out_specs,
                          out_shape=out_shape, compiler_params=_params(1))(*args)


def normbwd_matmul_nt(y, g, dout, w, tn, name, exchange=None):
    s, d = y.shape
    n = w.shape[0]
    tm = min(MM_TILE, s)

    def body(y_ref, g_ref, dout_ref, w_ref, du_ref, dy_ref, dg_ref):
        i, j = pl.program_id(0), pl.program_id(1)

        @pl.when(j == 0)
        def _():
            dy, dg = _rms_bwd(y_ref[...], g_ref[...], dout_ref[...])
            dy_ref[...] = dy.astype(BF16)

            @pl.when(i == 0)
            def _():
                dg_ref[...] = jnp.zeros_like(dg_ref)

            dg_ref[...] += dg

        du_ref[...] = lax.dot_general(dy_ref[...], w_ref[...], (((1,), (1,)), ((), ())), preferred_element_type=F32)

    row = pl.BlockSpec((tm, d), lambda i, j: (i, 0))
    return _call(
        body, name=name, grid=(s // tm, n // tn),
        in_specs=[row, _full((1, d)), row, pl.BlockSpec((tn, d), lambda i, j: (j, 0))],
        out_specs=[pl.BlockSpec((tm, tn), lambda i, j: (i, j)), row, _full((1, d))],
        out_shape=[jax.ShapeDtypeStruct((s, n), F32), jax.ShapeDtypeStruct((s, d), BF16), jax.ShapeDtypeStruct((1, d), F32)],
        args=[y, g, dout, w], exchange=exchange)


def matmul_tn(a, b, tm, tn, ts, out_dtype, name, exchange=None):
    s, m = a.shape
    n = b.shape[1]
    ts = min(ts, s)
    n_k = s // ts

    def body(a_ref, b_ref, o_ref, acc):
        k = pl.program_id(2)

        @pl.when(k == 0)
        def _():
            acc[...] = jnp.zeros_like(acc)

        acc[...] += lax.dot_general(a_ref[...], b_ref[...], (((0,), (0,)), ((), ())), preferred_element_type=F32)

        @pl.when(k == n_k - 1)
        def _():
            o_ref[...] = acc[...].astype(out_dtype)

    return _call(
        body, name=name, grid=(m // tm, n // tn, n_k),
        in_specs=[pl.BlockSpec((ts, tm), lambda i, j, k: (k, i)), pl.BlockSpec((ts, tn), lambda i, j, k: (k, j))],
        out_specs=pl.BlockSpec((tm, tn), lambda i, j, k: (i, j)),
        out_shape=jax.ShapeDtypeStruct((m, n), out_dtype),
        scratch_shapes=[pltpu.VMEM((tm, tn), F32)], args=[a, b], exchange=exchange)


def matmul_nt_normbwd(dproj, w, x, g, dres, tk, name, exchange=None):
    s, kt = dproj.shape
    d = w.shape[0]
    tm = min(MM_TILE, s)
    n_k = kt // tk

    def body(a_ref, w_ref, x_ref, g_ref, r_ref, dx_ref, dg_ref, acc):
        i, k = pl.program_id(0), pl.program_id(1)

        @pl.when(k == 0)
        def _():
            acc[...] = jnp.zeros_like(acc)

        acc[...] += lax.dot_general(a_ref[...], w_ref[...], (((1,), (1,)), ((), ())), preferred_element_type=F32)

        @pl.when(k == n_k - 1)
        def _():
            dx, dg = _rms_bwd(x_ref[...], g_ref[...], acc[...])
            dx_ref[...] = r_ref[...] + dx

            @pl.when(i == 0)
            def _():
                dg_ref[...] = jnp.zeros_like(dg_ref)

            dg_ref[...] += dg

    row = pl.BlockSpec((tm, d), lambda i, k: (i, 0))
    return _call(
        body, name=name, grid=(s // tm, n_k),
        in_specs=[pl.BlockSpec((tm, tk), lambda i, k: (i, k)), pl.BlockSpec((d, tk), lambda i, k: (0, k)), row, _full((1, d)), row],
        out_specs=[row, _full((1, d))],
        out_shape=[jax.ShapeDtypeStruct((s, d), F32), jax.ShapeDtypeStruct((1, d), F32)],
        scratch_shapes=[pltpu.VMEM((tm, d), F32)], args=[dproj, w, x, g, dres], exchange=exchange)


def _rg_conv(xa, before, after, cw, cb):
    return (cw[0:1, :] * _shift_rows(xa, before, after, -2) + cw[1:2, :] * _shift_rows(xa, before, after, -1)
            + cw[2:3, :] * xa + cw[3:4, :] * _shift_rows(xa, before, after, 1) + cb)


def _rg_gates(ua_h, gw_ref, gb_ref, c_h, direction, head):
    r = _sigmoid(_bdot(ua_h, gw_ref[2 * direction, head]) + gb_ref[2 * direction, head:head + 1, :])
    i = _sigmoid(_bdot(ua_h, gw_ref[2 * direction + 1, head]) + gb_ref[2 * direction + 1, head:head + 1, :])
    log_a = -c_h * r
    a = jnp.exp(log_a)
    beta = jnp.sqrt(-jnp.tanh(log_a) * (1.0 + a * a))
    return r, i, a, beta


def even_gates_fwd(proj, conv_w, conv_b, gate_w, gate_b, lam, exchange=None):
    s = proj.shape[0]
    ts = min(ROW_TILE, s)
    n_tiles = s // ts

    def body(xa_ref, xb_ref, xn_ref, cw_ref, cb_ref, gw_ref, gb_ref, lam_ref, o_ref):
        xa, before, after = _halo_load(xa_ref, xb_ref, xn_ref, n_tiles)
        ua = _rg_conv(xa, before, after, cw_ref[...], cb_ref[...])
        c = RG_C * _softplus(-lam_ref[...])
        for direction in range(2):
            for head in range(RG_HEADS):
                lanes = slice(head * RG_HEAD_DIM, (head + 1) * RG_HEAD_DIM)
                ua_h = ua[:, lanes]
                _, i, a, beta = _rg_gates(ua_h, gw_ref, gb_ref, c[direction:direction + 1, lanes], direction, head)
                o_ref[2 * direction, :, lanes] = a
                o_ref[2 * direction + 1, :, lanes] = beta * (i * ua_h)

    return _call(
        body, name="even_gates_fwd", grid=(n_tiles,),
        in_specs=_halo_specs(ts, s, D_MODEL, 0) + [_full(conv_w.shape), _full(conv_b.shape), _full(gate_w.shape),
                                                   _full(gate_b.shape), _full(lam.shape)],
        out_specs=pl.BlockSpec((4, ts, D_MODEL), lambda i: (0, i, 0)),
        out_shape=jax.ShapeDtypeStruct((4, s, D_MODEL), F32),
        args=[proj, proj, proj, conv_w, conv_b, gate_w, gate_b, lam], exchange=exchange)


def linear_scan(a_arr, a_idx, b_arr, b_idx, reverse, b_times_a, name, exchange=None):
    _, s, c = a_arr.shape
    ts = min(MM_TILE, s)
    n_tiles = s // ts
    n_blocks = ts // SUBLANES

    def tile_of(i):
        return n_tiles - 1 - i if reverse else i

    def body(a_ref, b_ref, h_ref, carry):
        @pl.when(pl.program_id(0) == 0)
        def _():
            carry[...] = jnp.zeros_like(carry)

        row = lax.broadcasted_iota(jnp.int32, (SUBLANES, c), 0)

        def block(j, h_in):
            r0 = pl.multiple_of((n_blocks - 1 - j if reverse else j) * SUBLANES, SUBLANES)
            a = a_ref[pl.ds(r0, SUBLANES), :]
            b = b_ref[pl.ds(r0, SUBLANES), :]
            if b_times_a:
                b = a * b
            for step in (1, 2, 4):
                shift = SUBLANES - step if reverse else step
                valid = row < SUBLANES - step if reverse else row >= step
                b = jnp.where(valid, a * pltpu.roll(b, shift, 0) + b, b)
                a = jnp.where(valid, a * pltpu.roll(a, shift, 0), a)
            h = a * h_in + b
            h_ref[pl.ds(r0, SUBLANES), :] = h
            return h[0:1, :] if reverse else h[SUBLANES - 1:SUBLANES, :]

        carry[0:1, :] = lax.fori_loop(0, n_blocks, block, carry[0:1, :])

    return _call(
        body, name=name, grid=(n_tiles,),
        in_specs=[pl.BlockSpec((None, ts, c), lambda i: (a_idx, tile_of(i), 0)),
                  pl.BlockSpec((None, ts, c), lambda i: (b_idx, tile_of(i), 0))],
        out_specs=pl.BlockSpec((ts, c), lambda i: (tile_of(i), 0)),
        out_shape=jax.ShapeDtypeStruct((s, c), F32),
        scratch_shapes=[pltpu.VMEM((SUBLANES, c), F32)], args=[a_arr, b_arr], exchange=exchange)


def _sc_conv(p, before, after, w):
    return w[0:1, :] * _shift_rows(p, before, after, -1) + w[1:2, :] * p + w[2:3, :] * _shift_rows(p, before, after, 1)


def even_mix_fwd(hf, hb, proj, sc_w, exchange=None):
    s = proj.shape[0]
    ts = min(ROW_TILE, s)
    n_tiles = s // ts
    row = pl.BlockSpec((ts, D_MODEL), lambda i: (i, 0))

    def col(c):
        return pl.BlockSpec((ts, D_MODEL), lambda i: (i, c))

    def body(hf_ref, hb_ref, za_ref, xb_ref, xbb_ref, xbn_ref, gb_ref, gc_ref, gcb_ref, gcn_ref, zb_ref, w_ref, u_ref):
        xb, xb_before, xb_after = _halo_load(xb_ref, xbb_ref, xbn_ref, n_tiles)
        gc, gc_before, gc_after = _halo_load(gc_ref, gcb_ref, gcn_ref, n_tiles)
        silu_za, _ = _silu_and_grad(za_ref[...])
        silu_zb, _ = _silu_and_grad(zb_ref[...])
        u_ref[:, :D_MODEL] = ((hf_ref[...] + hb_ref[...]) * silu_za).astype(BF16)
        cv = _sc_conv(gc * xb, gc_before * xb_before, gc_after * xb_after, w_ref[...])
        u_ref[:, D_MODEL:] = (gb_ref[...] * cv * silu_zb).astype(BF16)

    return _call(
        body, name="even_mix_fwd", grid=(n_tiles,),
        in_specs=[row, row, col(1)] + _halo_specs(ts, s, D_MODEL, 2) + [col(3)] + _halo_specs(ts, s, D_MODEL, 4)
        + [col(5), _full(sc_w.shape)],
        out_specs=pl.BlockSpec((ts, 2 * D_MODEL), lambda i: (i, 0)),
        out_shape=jax.ShapeDtypeStruct((s, 2 * D_MODEL), BF16),
        args=[hf, hb, proj, proj, proj, proj, proj, proj, proj, proj, proj, sc_w], exchange=exchange)


def even_mix_bwd(du, hf, hb, proj, sc_w, exchange=None):
    s = proj.shape[0]
    ts = min(ROW_TILE, s)
    n_tiles = s // ts
    row = pl.BlockSpec((ts, D_MODEL), lambda i: (i, 0))

    def body(dya_ref, dyb_ref, dybb_ref, dybn_ref, hf_ref, hb_ref, za_ref, xb_ref, xbb_ref, xbn_ref,
             gb_ref, gbb_ref, gbn_ref, gc_ref, gcb_ref, gcn_ref, zb_ref, zbb_ref, zbn_ref, w_ref,
             dh_ref, dp_ref, dw_ref):
        dyb, dyb_before, dyb_after = _halo_load(dyb_ref, dybb_ref, dybn_ref, n_tiles)
        xb, xb_before, xb_after = _halo_load(xb_ref, xbb_ref, xbn_ref, n_tiles)
        gb, gb_before, gb_after = _halo_load(gb_ref, gbb_ref, gbn_ref, n_tiles)
        gc, gc_before, gc_after = _halo_load(gc_ref, gcb_ref, gcn_ref, n_tiles)
        zb, zb_before, zb_after = _halo_load(zb_ref, zbb_ref, zbn_ref, n_tiles)
        w = w_ref[...]
        dya, za = dya_ref[...], za_ref[...]
        silu_za, dsilu_za = _silu_and_grad(za)
        dh_ref[...] = dya * silu_za
        dp_ref[:, 0:D_MODEL] = (dya * (hf_ref[...] + hb_ref[...]) * dsilu_za).astype(BF16)

        silu_zb, dsilu_zb = _silu_and_grad(zb)
        p, p_before, p_after = gc * xb, gc_before * xb_before, gc_after * xb_after
        cv = _sc_conv(p, p_before, p_after, w)
        dcv = dyb * gb * silu_zb
        dcv_before = dyb_before * gb_before * _silu_and_grad(zb_before)[0]
        dcv_after = dyb_after * gb_after * _silu_and_grad(zb_after)[0]
        dpp = (w[0:1, :] * _shift_rows(dcv, dcv_before, dcv_after, 1) + w[1:2, :] * dcv
               + w[2:3, :] * _shift_rows(dcv, dcv_before, dcv_after, -1))
        dp_ref[:, D_MODEL:2 * D_MODEL] = (dpp * gc).astype(BF16)
        dp_ref[:, 2 * D_MODEL:3 * D_MODEL] = (dyb * cv * silu_zb).astype(BF16)
        dp_ref[:, 3 * D_MODEL:4 * D_MODEL] = (dpp * xb).astype(BF16)
        dp_ref[:, 4 * D_MODEL:5 * D_MODEL] = (dyb * gb * cv * dsilu_zb).astype(BF16)

        @pl.when(pl.program_id(0) == 0)
        def _():
            dw_ref[...] = jnp.zeros_like(dw_ref)

        dw_ref[0:1, :] += jnp.sum(dcv * _shift_rows(p, p_before, p_after, -1), axis=0, keepdims=True)
        dw_ref[1:2, :] += jnp.sum(dcv * p, axis=0, keepdims=True)
        dw_ref[2:3, :] += jnp.sum(dcv * _shift_rows(p, p_before, p_after, 1), axis=0, keepdims=True)

    return _call(
        body, name="even_mix_bwd", grid=(n_tiles,),
        in_specs=[row] + _halo_specs(ts, s, D_MODEL, 1) + [row, row, pl.BlockSpec((ts, D_MODEL), lambda i: (i, 1))]
        + _halo_specs(ts, s, D_MODEL, 2) + _halo_specs(ts, s, D_MODEL, 3) + _halo_specs(ts, s, D_MODEL, 4)
        + _halo_specs(ts, s, D_MODEL, 5) + [_full(sc_w.shape)],
        out_specs=[row, pl.BlockSpec((ts, 5 * D_MODEL), lambda i: (i, 0)), _full(sc_w.shape)],
        out_shape=[jax.ShapeDtypeStruct((s, D_MODEL), F32), jax.ShapeDtypeStruct((s, 5 * D_MODEL), BF16),
                   jax.ShapeDtypeStruct(sc_w.shape, F32)],
        args=[du, du, du, du, hf, hb, proj, *([proj] * 12), sc_w], exchange=exchange)


def even_gates_bwd(proj, adj_f, adj_b, hf, hb, dh, conv_w, conv_b, gate_w, gate_b, lam, exchange=None):
    s = proj.shape[0]
    ts = min(ROW_TILE, s)
    n_tiles = s // ts
    row = pl.BlockSpec((ts, D_MODEL), lambda i: (i, 0))

    def body(xa_ref, xab_ref, xan_ref, af_ref, afb_ref, afn_ref, ab_ref, abb_ref, abn_ref,
             hf_ref, hfb_ref, hfn_ref, hb_ref, hbb_ref, hbn_ref, dh_ref,
             cw_ref, cb_ref, gw_ref, gb_ref, lam_ref, dua_ref, dgw_ref, dgb_ref, dlam_ref):
        @pl.when(pl.program_id(0) == 0)
        def _():
            dgw_ref[...] = jnp.zeros_like(dgw_ref)
            dgb_ref[...] = jnp.zeros_like(dgb_ref)
            dlam_ref[...] = jnp.zeros_like(dlam_ref)

        xa, before, after = _halo_load(xa_ref, xab_ref, xan_ref, n_tiles)
        ua = _rg_conv(xa, before, after, cw_ref[...], cb_ref[...])
        lam_v = lam_ref[...]
        c = RG_C * _softplus(-lam_v)
        dc_dlam = -RG_C * _sigmoid(-lam_v)
        dh = dh_ref[...]
        adj = (_halo_load(af_ref, afb_ref, afn_ref, n_tiles), _halo_load(ab_ref, abb_ref, abn_ref, n_tiles))
        hs = (_halo_load(hf_ref, hfb_ref, hfn_ref, n_tiles), _halo_load(hb_ref, hbb_ref, hbn_ref, n_tiles))
        dua = jnp.zeros_like(ua)
        for direction in range(2):
            step = 1 if direction == 0 else -1
            g = dh + _shift_rows(*adj[direction], step)
            da_all = g * _shift_rows(*hs[direction], -step)
            dua_parts = []
            for head in range(RG_HEADS):
                lanes = slice(head * RG_HEAD_DIM, (head + 1) * RG_HEAD_DIM)
                ua_h = ua[:, lanes]
                c_h = c[direction:direction + 1, lanes]
                r, i, a, beta = _rg_gates(ua_h, gw_ref, gb_ref, c_h, direction, head)
                db = g[:, lanes]
                d_i = db * beta * ua_h
                dbeta = db * (i * ua_h)
                dlog_a = (da_all[:, lanes] - dbeta * a / beta) * a
                dpr = -c_h * dlog_a * r * (1.0 - r)
                dpi = d_i * i * (1.0 - i)
                dua_parts.append(db * beta * i + _bdot_nt(dpr, gw_ref[2 * direction, head])
                                 + _bdot_nt(dpi, gw_ref[2 * direction + 1, head]))
                dgw_ref[2 * direction, head] += _bdot_tn(ua_h, dpr)
                dgw_ref[2 * direction + 1, head] += _bdot_tn(ua_h, dpi)
                dgb_ref[2 * direction, head:head + 1, :] += jnp.sum(dpr, axis=0, keepdims=True)
                dgb_ref[2 * direction + 1, head:head + 1, :] += jnp.sum(dpi, axis=0, keepdims=True)
                dlam_ref[direction:direction + 1, lanes] += (
                    jnp.sum(-r * dlog_a, axis=0, keepdims=True) * dc_dlam[direction:direction + 1, lanes])
            dua = dua + jnp.concatenate(dua_parts, axis=1)
        dua_ref[...] = dua

    return _call(
        body, name="even_gates_bwd", grid=(n_tiles,),
        in_specs=_halo_specs(ts, s, D_MODEL, 0) * 5 + [row] + [_full(conv_w.shape), _full(conv_b.shape), _full(gate_w.shape),
                                                             _full(gate_b.shape), _full(lam.shape)],
        out_specs=[row, _full(gate_w.shape), _full(gate_b.shape), _full(lam.shape)],
        out_shape=[jax.ShapeDtypeStruct((s, D_MODEL), F32), jax.ShapeDtypeStruct(gate_w.shape, F32),
                   jax.ShapeDtypeStruct(gate_b.shape, F32), jax.ShapeDtypeStruct(lam.shape, F32)],
        args=[proj, proj, proj, adj_f, adj_f, adj_f, adj_b, adj_b, adj_b, hf, hf, hf, hb, hb, hb, dh, conv_w, conv_b, gate_w,
              gate_b, lam], exchange=exchange)


def rg_conv_bwd(dua, proj, drest, conv_w, exchange=None):
    s = proj.shape[0]
    ts = min(ROW_TILE, s)
    n_tiles = s // ts

    def body(du_ref, dub_ref, dun_ref, xa_ref, xab_ref, xan_ref, dr_ref, cw_ref, dp_ref, dw_ref, db_ref):
        @pl.when(pl.program_id(0) == 0)
        def _():
            dw_ref[...] = jnp.zeros_like(dw_ref)
            db_ref[...] = jnp.zeros_like(db_ref)

        dua, dua_before, dua_after = _halo_load(du_ref, dub_ref, dun_ref, n_tiles)
        xa, xa_before, xa_after = _halo_load(xa_ref, xab_ref, xan_ref, n_tiles)
        cw = cw_ref[...]
        dxa = (cw[0:1, :] * _shift_rows(dua, dua_before, dua_after, 2) + cw[1:2, :] * _shift_rows(dua, dua_before, dua_after, 1)
               + cw[2:3, :] * dua + cw[3:4, :] * _shift_rows(dua, dua_before, dua_after, -1))
        dp_ref[:, :D_MODEL] = dxa.astype(BF16)
        dp_ref[:, D_MODEL:] = dr_ref[...]
        for tap, offset in enumerate((-2, -1, 0, 1)):
            shifted = xa if offset == 0 else _shift_rows(xa, xa_before, xa_after, offset)
            dw_ref[tap:tap + 1, :] += jnp.sum(dua * shifted, axis=0, keepdims=True)
        db_ref[...] += jnp.sum(dua, axis=0, keepdims=True)

    return _call(
        body, name="rg_conv_bwd", grid=(n_tiles,),
        in_specs=_halo_specs(ts, s, D_MODEL, 0) * 2 + [pl.BlockSpec((ts, 5 * D_MODEL), lambda i: (i, 0)), _full(conv_w.shape)],
        out_specs=[pl.BlockSpec((ts, EVEN_IN), lambda i: (i, 0)), _full(conv_w.shape), _full((1, D_MODEL))],
        out_shape=[jax.ShapeDtypeStruct((s, EVEN_IN), BF16), jax.ShapeDtypeStruct(conv_w.shape, F32),
                   jax.ShapeDtypeStruct((1, D_MODEL), F32)],
        args=[dua, dua, dua, proj, proj, proj, drest, conv_w], exchange=exchange)


def _split3(x):
    x1 = x.astype(BF16)
    rest = x - x1.astype(F32)
    x2 = rest.astype(BF16)
    return x1, x2, (rest - x2.astype(F32)).astype(BF16)


def _chunk_sum_matrix(t, reverse, transpose):
    i = lax.broadcasted_iota(jnp.int32, (t, t), 0)
    j = lax.broadcasted_iota(jnp.int32, (t, t), 1)
    if transpose:
        i, j = j, i
    same = (i // GLA_CHUNK) == (j // GLA_CHUNK)
    return jnp.where(same & ((j >= i) if reverse else (j <= i)), 1.0, 0.0).astype(BF16)


def _exact_dot(m, x):
    return sum(jnp.dot(m, part, preferred_element_type=F32) for part in _split3(x))


def _causal_mask(reverse):
    i = lax.broadcasted_iota(jnp.int32, (GLA_CHUNK, GLA_CHUNK), 0)
    j = lax.broadcasted_iota(jnp.int32, (GLA_CHUNK, GLA_CHUNK), 1)
    return (j >= i) if reverse else (j <= i)


def _gla_gate(lr, wg, bg):
    z = _bdot(lr, wg) + bg
    log_alpha = (jnp.minimum(z, 0.0) - jnp.log(1.0 + jnp.exp(-jnp.abs(z)))) * (1.0 / GLA_NORMALIZER)
    return z, log_alpha


def _gla_chunk_terms(q, k, bcum, rows, reverse):
    bc = bcum[rows, :]
    edge = rows.start if reverse else rows.stop - 1
    btot = bcum[edge:edge + 1, :]
    e_pos, e_neg, e_st = jnp.exp(bc), jnp.exp(-bc), jnp.exp(btot - bc)
    qc, kc = q[rows, :], k[rows, :]
    return qc * (GLA_DK ** -0.5) * e_pos, kc * e_neg, kc * e_st, e_pos, e_neg, e_st, jnp.exp(btot)


def _gla_specs(t, n_tiles, reverse_order):
    def tile(i):
        return n_tiles - 1 - i if reverse_order else i

    return tile, [
        pl.BlockSpec((t, GLA_KEY), lambda i: (tile(i), 0)),
        pl.BlockSpec((t, GLA_KEY), lambda i: (tile(i), 1)),
        pl.BlockSpec((t, D_MODEL), lambda i: (tile(i), 1)),
        pl.BlockSpec((t, LANES), lambda i: (tile(i), (ODD_IN_PAD - LANES) // LANES)),
    ]


def gla_fwd(proj, wg, bg, reverse, o_other=None, gnorm=None):
    s = proj.shape[0]
    t = min(ROW_TILE, s)
    n_tiles = s // t
    n_chunks = t // GLA_CHUNK
    final = o_other is not None
    tile, specs = _gla_specs(t, n_tiles, reverse)

    def body(*refs):
        if final:
            q_ref, k_ref, v_ref, lr_ref, wg_ref, bg_ref, oo_ref, r_ref, gn_ref, osum_ref, u_ref, st_ref, state = refs
        else:
            q_ref, k_ref, v_ref, lr_ref, wg_ref, bg_ref, o_ref, st_ref, state = refs
            osum_ref = o_ref

        @pl.when(pl.program_id(0) == 0)
        def _():
            state[...] = jnp.zeros_like(state)

        _, log_alpha = _gla_gate(lr_ref[...], wg_ref[...], bg_ref[...])
        bcum = _exact_dot(_chunk_sum_matrix(t, reverse, False), log_alpha)
        q, k, v = q_ref[...], k_ref[...], v_ref[...]
        mask = _causal_mask(reverse)
        for c in (range(n_chunks - 1, -1, -1) if reverse else range(n_chunks)):
            rows = slice(c * GLA_CHUNK, (c + 1) * GLA_CHUNK)
            q_in, k_in, k_st, _, _, _, decay = _gla_chunk_terms(q, k, bcum, rows, reverse)
            for head in range(GLA_HEADS):
                kl = slice(head * GLA_DK, (head + 1) * GLA_DK)
                vl = slice(head * GLA_DV, (head + 1) * GLA_DV)
                s_prev = state[head]
                st_ref[c, head] = s_prev
                scores = jnp.where(mask, _bdot_nt(q_in[:, kl], k_in[:, kl]), 0.0)
                osum_ref[rows, vl] = _bdot(scores, v[rows, vl]) + _bdot_nt(q_in[:, kl], s_prev)
                state[head] = s_prev * decay[:, kl] + _bdot_tn(v[rows, vl], k_st[:, kl])
        if final:
            osum = osum_ref[...] + oo_ref[...]
            osum_ref[...] = osum
            silu_r, _ = _silu_and_grad(r_ref[...])
            gn = gn_ref[...]
            for head in range(GLA_HEADS):
                vl = slice(head * GLA_DV, (head + 1) * GLA_DV)
                u_ref[:, vl] = (_rms(osum[:, vl], gn[:, vl]) * silu_r[:, vl]).astype(BF16)

    row = pl.BlockSpec((t, D_MODEL), lambda i: (tile(i), 0))
    st_spec = pl.BlockSpec((n_chunks, GLA_HEADS, GLA_DV, GLA_DK), lambda i: (tile(i), 0, 0, 0))
    st_shape = jax.ShapeDtypeStruct((s // GLA_CHUNK, GLA_HEADS, GLA_DV, GLA_DK), F32)
    in_specs = specs + [_full(wg.shape), _full(bg.shape)]
    args = [proj, proj, proj, proj, wg, bg]
    if final:
        in_specs += [row, pl.BlockSpec((t, D_MODEL), lambda i: (tile(i), 2)), _full(gnorm.shape)]
        args += [o_other, proj, gnorm]
        out_specs = [row, row, st_spec]
        out_shape = [jax.ShapeDtypeStruct((s, D_MODEL), F32), jax.ShapeDtypeStruct((s, D_MODEL), BF16), st_shape]
    else:
        out_specs = [row, st_spec]
        out_shape = [jax.ShapeDtypeStruct((s, D_MODEL), F32), st_shape]
    return pl.pallas_call(
        body, name="gla_fwd_rev" if reverse else "gla_fwd", grid=(n_tiles,), in_specs=in_specs, out_specs=out_specs,
        out_shape=out_shape, scratch_shapes=[pltpu.VMEM((GLA_HEADS, GLA_DV, GLA_DK), F32)], compiler_params=_params(1),
    )(*args)


def gla_out_bwd(du, proj, osum, gnorm):
    s = proj.shape[0]
    ts = min(ROW_TILE, s)
    row = pl.BlockSpec((ts, D_MODEL), lambda i: (i, 0))

    def body(du_ref, r_ref, o_ref, gn_ref, do_ref, dr_ref, dgn_ref):
        @pl.when(pl.program_id(0) == 0)
        def _():
            dgn_ref[...] = jnp.zeros_like(dgn_ref)

        du, osum, gn = du_ref[...], o_ref[...], gn_ref[...]
        silu_r, dsilu_r = _silu_and_grad(r_ref[...])
        for head in range(GLA_HEADS):
            vl = slice(head * GLA_DV, (head + 1) * GLA_DV)
            o_h, g_h, du_h = osum[:, vl], gn[:, vl], du[:, vl]
            dr_ref[:, vl] = (du_h * _rms(o_h, g_h) * dsilu_r[:, vl]).astype(BF16)
            do_h, dg_h = _rms_bwd(o_h, g_h, du_h * silu_r[:, vl])
            do_ref[:, vl] = do_h
            dgn_ref[...] += dg_h

    return pl.pallas_call(
        body, name="gla_out_bwd", grid=(s // ts,),
        in_specs=[row, pl.BlockSpec((ts, D_MODEL), lambda i: (i, 2)), row, _full(gnorm.shape)],
        out_specs=[row, row, _full((1, GLA_DV))],
        out_shape=[jax.ShapeDtypeStruct((s, D_MODEL), F32), jax.ShapeDtypeStruct((s, D_MODEL), BF16),
                   jax.ShapeDtypeStruct((1, GLA_DV), F32)],
        compiler_params=_params(1),
    )(du, proj, osum, gnorm)


def gla_bwd(proj, wg, bg, do, states, reverse, first=None):
    s = proj.shape[0]
    t = min(ROW_TILE, s)
    n_tiles = s // t
    n_chunks = t // GLA_CHUNK
    final = first is not None
    tile, specs = _gla_specs(t, n_tiles, not reverse)

    def body(*refs):
        if final:
            (q_ref, k_ref, v_ref, lr_ref, wg_ref, bg_ref, do_ref, st_ref, dqkv1_ref, dlr1_ref, dr_ref,
             dp_ref, dwg_ref, dbg_ref, dstate, dqkv, dbc, dbt) = refs
        else:
            (q_ref, k_ref, v_ref, lr_ref, wg_ref, bg_ref, do_ref, st_ref,
             dqkv, dlr_ref, dwg_ref, dbg_ref, dstate, dbc, dbt) = refs

        @pl.when(pl.program_id(0) == 0)
        def _():
            dstate[...] = jnp.zeros_like(dstate)
            dwg_ref[...] = jnp.zeros_like(dwg_ref)
            dbg_ref[...] = jnp.zeros_like(dbg_ref)

        lr, wg_v = lr_ref[...], wg_ref[...]
        z, log_alpha = _gla_gate(lr, wg_v, bg_ref[...])
        bcum = _exact_dot(_chunk_sum_matrix(t, reverse, False), log_alpha)
        q, k, v, do_v = q_ref[...], k_ref[...], v_ref[...], do_ref[...]
        mask = _causal_mask(reverse)
        for c in (range(n_chunks) if reverse else range(n_chunks - 1, -1, -1)):
            rows = slice(c * GLA_CHUNK, (c + 1) * GLA_CHUNK)
            q_in, k_in, k_st, e_pos, e_neg, e_st, decay = _gla_chunk_terms(q, k, bcum, rows, reverse)
            dbtot_parts = []
            for head in range(GLA_HEADS):
                kl = slice(head * GLA_DK, (head + 1) * GLA_DK)
                vl = slice(head * GLA_DV, (head + 1) * GLA_DV)
                q_h, k_h, ks_h, v_h, do_h = q_in[:, kl], k_in[:, kl], k_st[:, kl], v[rows, vl], do_v[rows, vl]
                s_prev = st_ref[c, head]
                ds_next = dstate[head]
                scores = jnp.where(mask, _bdot_nt(q_h, k_h), 0.0)
                dscores = jnp.where(mask, _bdot_nt(do_h, v_h), 0.0)
                dq_in = _bdot(dscores, k_h) + _bdot(do_h, s_prev)
                dk_in = _bdot_tn(dscores, q_h)
                dk_st = _bdot(v_h, ds_next)
                dqkv[rows, 2 * GLA_KEY + head * GLA_DV:2 * GLA_KEY + (head + 1) * GLA_DV] = (
                    _bdot_tn(scores, do_h) + _bdot_nt(ks_h, ds_next))
                ddecay = jnp.sum(ds_next * s_prev, axis=0, keepdims=True)
                dstate[head] = ds_next * decay[:, kl] + _bdot_tn(do_h, q_h)
                dqkv[rows, kl] = dq_in * (GLA_DK ** -0.5) * e_pos[:, kl]
                dqkv[rows, GLA_KEY + head * GLA_DK:GLA_KEY + (head + 1) * GLA_DK] = dk_in * e_neg[:, kl] + dk_st * e_st[:, kl]
                dbc[rows, kl] = dq_in * q_h - dk_in * k_h - dk_st * ks_h
                dbtot_parts.append(jnp.sum(dk_st * ks_h, axis=0, keepdims=True) + ddecay * decay[:, kl])
            dbt[rows, :] = jnp.broadcast_to(jnp.concatenate(dbtot_parts, axis=1), (GLA_CHUNK, GLA_KEY))
        dlog_alpha = _exact_dot(_chunk_sum_matrix(t, reverse, True), dbc[...]) + dbt[...]
        dz = dlog_alpha * _sigmoid(-z) * (1.0 / GLA_NORMALIZER)
        dlr = _bdot_nt(dz, wg_v)
        dwg_ref[...] += _bdot_tn(lr, dz)
        dbg_ref[...] += jnp.sum(dz, axis=0, keepdims=True)
        if final:
            dp_ref[:, :2 * D_MODEL] = (dqkv[...] + dqkv1_ref[...]).astype(BF16)
            dp_ref[:, 2 * D_MODEL:3 * D_MODEL] = dr_ref[...]
            dp_ref[:, 3 * D_MODEL:] = (dlr + dlr1_ref[...]).astype(BF16)
        else:
            dlr_ref[...] = dlr

    row = pl.BlockSpec((t, D_MODEL), lambda i: (tile(i), 0))
    wide = pl.BlockSpec((t, 2 * D_MODEL), lambda i: (tile(i), 0))
    narrow = pl.BlockSpec((t, LANES), lambda i: (tile(i), 0))
    st_spec = pl.BlockSpec((n_chunks, GLA_HEADS, GLA_DV, GLA_DK), lambda i: (tile(i), 0, 0, 0))
    in_specs = specs + [_full(wg.shape), _full(bg.shape), row, st_spec]
    args = [proj, proj, proj, proj, wg, bg, do, states]
    acc_specs = [_full(wg.shape), _full(bg.shape)]
    acc_shapes = [jax.ShapeDtypeStruct(wg.shape, F32), jax.ShapeDtypeStruct(bg.shape, F32)]
    scratch = [pltpu.VMEM((GLA_HEADS, GLA_DV, GLA_DK), F32)]
    work = [pltpu.VMEM((t, GLA_KEY), F32), pltpu.VMEM((t, GLA_KEY), F32)]
    if final:
        in_specs += [wide, narrow, row]
        args += list(first)
        out_specs = [pl.BlockSpec((t, ODD_IN_PAD), lambda i: (tile(i), 0))] + acc_specs
        out_shape = [jax.ShapeDtypeStruct((s, ODD_IN_PAD), BF16)] + acc_shapes
        scratch += [pltpu.VMEM((t, 2 * D_MODEL), F32)] + work
    else:
        out_specs = [wide, narrow] + acc_specs
        out_shape = [jax.ShapeDtypeStruct((s, 2 * D_MODEL), F32), jax.ShapeDtypeStruct((s, LANES), F32)] + acc_shapes
        scratch += work
    return pl.pallas_call(
        body, name="gla_bwd_rev" if reverse else "gla_bwd", grid=(n_tiles,), in_specs=in_specs, out_specs=out_specs,
        out_shape=out_shape, scratch_shapes=scratch, compiler_params=_params(1),
    )(*args)


def pair_sum(grad, from_sibling):
    n_chips, r, w = from_sibling.shape

    def body(even_ref, odd_ref, sib_ref, o_ref):
        mine = jnp.where(lax.axis_index("c") == 1, odd_ref[...], even_ref[...])
        o_ref[...] = (mine.astype(F32) + sib_ref[...].astype(F32)).astype(o_ref.dtype)

    return pl.pallas_call(
        body, name="pair_sum", grid=(n_chips,),
        in_specs=[pl.BlockSpec((r, w), lambda k: (0, 2 * k)), pl.BlockSpec((r, w), lambda k: (0, 2 * k + 1)),
                  pl.BlockSpec((None, r, w), lambda k: (k, 0, 0))],
        out_specs=pl.BlockSpec((None, r, w), lambda k: (k, 0, 0)),
        out_shape=jax.ShapeDtypeStruct(from_sibling.shape, from_sibling.dtype), compiler_params=_params(1),
    )(grad, grad, from_sibling)


def _adamw_update(g, w, m, v):
    new_m = ADAM_B1 * m + (1.0 - ADAM_B1) * g
    new_v = ADAM_B2 * v + (1.0 - ADAM_B2) * (g * g)
    m_hat = new_m / (1.0 - ADAM_B1 ** ADAM_STEP)
    v_hat = new_v / (1.0 - ADAM_B2 ** ADAM_STEP)
    return -ADAM_LR * (m_hat / (jnp.sqrt(v_hat) + ADAM_EPS) + ADAM_WD * w), new_m, new_v


def sum_parts(parts, name):
    _, r, c = parts.shape

    def body(p_ref, o_ref):
        total = p_ref[0].astype(F32)
        for j in range(1, N_DEV):
            total = total + p_ref[j].astype(F32)
        o_ref[...] = total

    return pl.pallas_call(body, name=name, in_specs=[_full(parts.shape)], out_specs=_full((r, c)), grid=(1,),
                          out_shape=jax.ShapeDtypeStruct((r, c), F32), compiler_params=_params(1))(parts)


def adamw(parts, w, m, v, name):
    n, r, c = parts.shape
    tr = r if r <= MM_TILE else ROW_TILE

    def body(p_ref, w_ref, m_ref, v_ref, g_ref, d_ref, nm_ref, nv_ref):
        g = p_ref[0].astype(F32)
        for j in range(1, n):
            g = g + p_ref[j].astype(F32)
        g_ref[...] = g
        d_ref[...], nm_ref[...], nv_ref[...] = _adamw_update(g, w_ref[...], m_ref[...], v_ref[...])

    row = pl.BlockSpec((tr, c), lambda i: (i, 0))
    return pl.pallas_call(
        body, name=name, grid=(r // tr,),
        in_specs=[pl.BlockSpec((n, tr, c), lambda i: (0, i, 0)), row, row, row], out_specs=[row] * 4,
        out_shape=[jax.ShapeDtypeStruct((r, c), F32)] * 4, compiler_params=_params(1),
    )(parts, w, m, v)


def _small_views(shape):
    if len(shape) == 2:
        return [((slice(None), slice(None)), (slice(None), slice(None)))]
    if len(shape) == 3:
        return [((slice(None), slice(None)), (0,))]
    rows = shape[2]
    return [((slice(k * rows, (k + 1) * rows), slice(None)), (0, k)) for k in range(shape[1])]


def adamw_small(landings, w, m, v):
    names = list(landings)
    n = len(names)
    shapes = [w[name].shape for name in names]

    def body(*refs):
        land, ws, ms, vs = refs[:n], refs[n:2 * n], refs[2 * n:3 * n], refs[3 * n:4 * n]
        outs = [refs[(4 + k) * n:(5 + k) * n] for k in range(4)]
        for k in range(n):
            total = land[k][0]
            for j in range(1, N_DEV):
                total = total + land[k][j]
            for rows, at in _small_views(shapes[k]):
                g = total[rows]
                outs[0][k][at] = g
                outs[1][k][at], outs[2][k][at], outs[3][k][at] = _adamw_update(g, ws[k][at], ms[k][at], vs[k][at])

    blocks = [_full(sh) for sh in shapes]
    outs = pl.pallas_call(
        body, name="adamw_small", grid=(1,),
        in_specs=[_full(landings[name].shape) for name in names] + blocks * 3, out_specs=blocks * 4,
        out_shape=[jax.ShapeDtypeStruct(sh, F32) for sh in shapes] * 4, compiler_params=_params(1),
    )(*[landings[name] for name in names], *[src[name] for src in (w, m, v) for name in names])
    return [dict(zip(names, outs[k * n:(k + 1) * n])) for k in range(4)]


def adamw_replicated(land_vec, land_gate_b, land_loss, names, w, m, v, gate_b):
    n = len(names)

    def body(*refs):
        vec_ref, gb_ref, loss_ref = refs[:3]
        ws, ms, vs = refs[3:3 + n], refs[3 + n:3 + 2 * n], refs[3 + 2 * n:3 + 3 * n]
        gw_ref, gm_ref, gv_ref = refs[3 + 3 * n:6 + 3 * n]
        outs = refs[6 + 3 * n:]
        vec, gb, loss = vec_ref[0], gb_ref[0], loss_ref[0]
        for j in range(1, N_DEV):
            vec, gb, loss = vec + vec_ref[j], gb + gb_ref[j], loss + loss_ref[j]
        for k in range(n):
            g = vec[k:k + 1, :]
            outs[k][...] = g
            outs[n + k][...], outs[2 * n + k][...], outs[3 * n + k][...] = _adamw_update(g, ws[k][...], ms[k][...], vs[k][...])
        outs[4 * n][...] = gb
        outs[4 * n + 1][...], outs[4 * n + 2][...], outs[4 * n + 3][...] = _adamw_update(gb, gw_ref[...], gm_ref[...], gv_ref[...])
        outs[4 * n + 4][...] = loss

    vec_block, gb_block = _full((1, D_MODEL)), _full(gate_b[0].shape)
    outs = pl.pallas_call(
        body, name="adamw_replicated", grid=(1,),
        in_specs=[_full(land_vec.shape), _full(land_gate_b.shape), _full(land_loss.shape)] + [vec_block] * (3 * n) + [gb_block] * 3,
        out_specs=[vec_block] * (4 * n) + [gb_block] * 4 + [_full(land_loss.shape[1:])],
        out_shape=[jax.ShapeDtypeStruct((1, D_MODEL), F32)] * (4 * n) + [jax.ShapeDtypeStruct(gate_b[0].shape, F32)] * 4
        + [jax.ShapeDtypeStruct(land_loss.shape[1:], F32)],
        compiler_params=_params(1),
    )(land_vec, land_gate_b, land_loss, *[src[name] for src in (w, m, v) for name in names], *gate_b)
    results = {name: [outs[k * n + i] for k in range(4)] for i, name in enumerate(names)}
    return results, outs[4 * n:4 * n + 4], outs[4 * n + 4]


SMALL_SHARDED = ("rg_conv_w", "rg_lambda", "sc_conv_w", "odd_norm_pre", "odd_norm_post", "gla_b_gate", "gla_norm_g", "gla_w_gate_lr")
SMALL_ROWS = {"rg_conv_w": (0, 4), "rg_lambda": (4, 2), "sc_conv_w": (6, 3), "odd_norm_pre": (9, 1), "odd_norm_post": (10, 1),
              "gla_b_gate": (11, 2), "gla_norm_g": (13, 1), "gla_w_gate_lr": (16, 32)}


def _pack_small(shards):
    pieces, at = [], 0
    for name in SMALL_SHARDED:
        start, rows = SMALL_ROWS[name]
        if start > at:
            pieces.append(jnp.zeros((start - at, LANES), F32))
        a = shards[name].reshape(rows, -1)
        pieces.append(jnp.pad(a, ((0, 0), (0, LANES - a.shape[1]))))
        at = start + rows
    return jnp.concatenate(pieces, axis=0)


def _unpack_gathered(g):
    def cols(name, width):
        start, rows = SMALL_ROWS[name]
        return jnp.transpose(g[:, start:start + rows, :width], (1, 0, 2)).reshape(rows, N_DEV * width)

    w_lr = cols("gla_w_gate_lr", GLA_KEY // N_DEV).reshape(2, GLA_RANK, GLA_KEY)
    return dict(rg_conv_w=cols("rg_conv_w", LANES), rg_lambda=cols("rg_lambda", LANES), sc_conv_w=cols("sc_conv_w", LANES),
                odd_norm_pre=cols("odd_norm_pre", LANES), odd_norm_post=cols("odd_norm_post", LANES),
                gla_b_gate=cols("gla_b_gate", GLA_KEY // N_DEV), gla_norm_g=cols("gla_norm_g", GLA_DV // N_DEV), gla_w_gate_lr=w_lr)


def _blocks_along_columns(a, rows):
    return jnp.transpose(a.reshape(rows, N_DEV, -1), (1, 0, 2))


def kernel(x, even_norm_pre, even_norm_post, even_w_in, rg_conv_w, rg_conv_b, rg_gate_w, rg_gate_b, rg_lambda, sc_conv_w, even_w_out, odd_norm_pre, odd_norm_post, odd_w_in, gla_w_gate_lr, gla_b_gate, gla_norm_g, odd_w_out, loss_target, m_even_norm_pre, m_even_norm_post, m_even_w_in, m_rg_conv_w, m_rg_conv_b, m_rg_gate_w, m_rg_gate_b, m_rg_lambda, m_sc_conv_w, m_even_w_out, m_odd_norm_pre, m_odd_norm_post, m_odd_w_in, m_gla_w_gate_lr, m_gla_b_gate, m_gla_norm_g, m_odd_w_out, v_even_norm_pre, v_even_norm_post, v_even_w_in, v_rg_conv_w, v_rg_conv_b, v_rg_gate_w, v_rg_gate_b, v_rg_lambda, v_sc_conv_w, v_even_w_out, v_odd_norm_pre, v_odd_norm_post, v_odd_w_in, v_gla_w_gate_lr, v_gla_b_gate, v_gla_norm_g, v_odd_w_out):
    weights = dict(even_norm_pre=even_norm_pre, even_norm_post=even_norm_post, even_w_in=even_w_in, rg_conv_w=rg_conv_w,
                   rg_conv_b=rg_conv_b, rg_gate_w=rg_gate_w, rg_gate_b=rg_gate_b, rg_lambda=rg_lambda, sc_conv_w=sc_conv_w,
                   even_w_out=even_w_out, odd_norm_pre=odd_norm_pre, odd_norm_post=odd_norm_post, odd_w_in=odd_w_in,
                   gla_w_gate_lr=gla_w_gate_lr, gla_b_gate=gla_b_gate, gla_norm_g=gla_norm_g, odd_w_out=odd_w_out)
    m_in = dict(even_norm_pre=m_even_norm_pre, even_norm_post=m_even_norm_post, even_w_in=m_even_w_in, rg_conv_w=m_rg_conv_w,
                rg_conv_b=m_rg_conv_b, rg_gate_w=m_rg_gate_w, rg_gate_b=m_rg_gate_b, rg_lambda=m_rg_lambda, sc_conv_w=m_sc_conv_w,
                even_w_out=m_even_w_out, odd_norm_pre=m_odd_norm_pre, odd_norm_post=m_odd_norm_post, odd_w_in=m_odd_w_in,
                gla_w_gate_lr=m_gla_w_gate_lr, gla_b_gate=m_gla_b_gate, gla_norm_g=m_gla_norm_g, odd_w_out=m_odd_w_out)
    v_in = dict(even_norm_pre=v_even_norm_pre, even_norm_post=v_even_norm_post, even_w_in=v_even_w_in, rg_conv_w=v_rg_conv_w,
                rg_conv_b=v_rg_conv_b, rg_gate_w=v_rg_gate_w, rg_gate_b=v_rg_gate_b, rg_lambda=v_rg_lambda, sc_conv_w=v_sc_conv_w,
                even_w_out=v_even_w_out, odd_norm_pre=v_odd_norm_pre, odd_norm_post=v_odd_norm_post, odd_w_in=v_odd_w_in,
                gla_w_gate_lr=v_gla_w_gate_lr, gla_b_gate=v_gla_b_gate, gla_norm_g=v_gla_norm_g, odd_w_out=v_odd_w_out)
    names = list(weights)
    shapes = {n: weights[n].shape for n in names}
    xs = x[0]
    tgt = loss_target[0]

    first = Exchange()
    first.gather(even_w_in[0].astype(BF16), columns=True, via_sibling=True)
    first.gather(_pack_small({n: weights[n][0] for n in SMALL_SHARDED}))
    w_in_e, small_all = run_exchange(first, "gather_first")
    small = _unpack_gathered(small_all)
    gate_w = rg_gate_w[0].reshape(4, RG_HEADS, RG_HEAD_DIM, RG_HEAD_DIM).astype(BF16)
    gate_b = rg_gate_b[0].reshape(4, RG_HEADS, RG_HEAD_DIM)
    conv_b = rg_conv_b
    wg_pad = [jnp.pad(small["gla_w_gate_lr"][d], ((GLA_RANK * d, LANES - GLA_RANK * (d + 1)), (0, 0))).astype(BF16) for d in range(2)]
    bg = [small["gla_b_gate"][d:d + 1] for d in range(2)]
    gnorm = jnp.tile(small["gla_norm_g"], (1, GLA_HEADS))

    half = D_MODEL // 2
    behind_in = Exchange()
    behind_in.gather(even_w_out[0].astype(BF16))
    (proj_e, h_e), (w_out_e,) = rms_matmul(xs, even_norm_pre, w_in_e, 2 * MM_TILE, 2 * EVEN_SHARD, "even_in", exchange=behind_in)
    w_out_e = w_out_e.reshape(2 * D_MODEL, D_MODEL)
    behind_gates = Exchange()
    behind_gates.gather(odd_w_in[0, :half].astype(BF16))
    ab, (w_in_o_top,) = even_gates_fwd(proj_e, small["rg_conv_w"], conv_b, gate_w, gate_b, small["rg_lambda"], exchange=behind_gates)
    behind_scan = Exchange()
    behind_scan.gather(odd_w_out[0].astype(BF16))
    hf, (w_out_o,) = linear_scan(ab, 0, ab, 1, False, False, "scan_fwd", exchange=behind_scan)
    w_out_o = w_out_o.reshape(D_MODEL, D_MODEL)
    hb = linear_scan(ab, 2, ab, 3, True, False, "scan_rev")
    behind_mix_fwd = Exchange()
    behind_mix_fwd.gather(odd_w_in[0, half:].astype(BF16))
    u_e, (w_in_o_bottom,) = even_mix_fwd(hf, hb, proj_e, small["sc_conv_w"], exchange=behind_mix_fwd)
    w_in_o = jnp.concatenate([jnp.transpose(part, (1, 0, 2)).reshape(half, ODD_IN) for part in (w_in_o_top, w_in_o_bottom)], axis=0)
    w_in_o = jnp.pad(w_in_o, ((0, 0), (0, ODD_IN_PAD - ODD_IN)))
    y_e, x1 = matmul_post(u_e, w_out_e, xs, even_norm_post, "even_out")

    proj_o, h_o = rms_matmul(x1, small["odd_norm_pre"], w_in_o, MM_TILE, ODD_IN_PAD, "odd_in")
    o_f, st_f = gla_fwd(proj_o, wg_pad[0], bg[0], False)
    osum, u_o, st_b = gla_fwd(proj_o, wg_pad[1], bg[1], True, o_other=o_f, gnorm=gnorm)
    y_o, dout, loss_part = matmul_post(u_o, w_out_o, x1, small["odd_norm_post"], "odd_out", target=tgt)

    du_o, dy_o, d_odd_norm_post = normbwd_matmul_nt(y_o, small["odd_norm_post"], dout, w_out_o, D_MODEL, "odd_out_bwd")
    d_w_out_o = matmul_tn(u_o, dy_o, D_MODEL, D_MODEL, 4 * MM_TILE, BF16, "odd_w_out_grad")
    do, dr, d_gnorm = gla_out_bwd(du_o, proj_o, osum, gnorm)
    dqkv_f, dlr_f, dwg_f, dbg_f = gla_bwd(proj_o, wg_pad[0], bg[0], do, st_f, False)
    dproj_o, dwg_b, dbg_b = gla_bwd(proj_o, wg_pad[1], bg[1], do, st_b, True, first=(dqkv_f, dlr_f, dr))
    dx1, d_odd_norm_pre = matmul_nt_normbwd(dproj_o, w_in_o, x1, small["odd_norm_pre"], dout, ODD_IN_PAD, "odd_in_bwd")
    d_w_in_o = matmul_tn(h_o, dproj_o, D_MODEL, ODD_IN_PAD // 5, 8 * MM_TILE, BF16, "odd_w_in_grad")

    landed = {}
    behind_out = Exchange()
    behind_out.scatter(d_w_out_o.reshape(N_DEV, D_MODEL // N_DEV, D_MODEL))
    behind_out.scatter(d_odd_norm_pre, columns=True)
    behind_out.scatter(d_odd_norm_post, columns=True)
    behind_out.scatter(_blocks_along_columns(jnp.concatenate([dbg_f, dbg_b], axis=0), 2))
    behind_out.scatter(_blocks_along_columns(d_gnorm, 1))
    behind_out.scatter(_blocks_along_columns(jnp.concatenate([dwg_f[:GLA_RANK], dwg_b[GLA_RANK:2 * GLA_RANK]], axis=0), 2 * GLA_RANK))
    (du_e, dy_e, d_even_norm_post), got = normbwd_matmul_nt(y_e, even_norm_post, dx1, w_out_e, 2 * D_MODEL, "even_out_bwd",
                                                           exchange=behind_out)
    p_w_out_o = got[0]
    for n, part in zip(("odd_norm_pre", "odd_norm_post", "gla_b_gate", "gla_norm_g", "gla_w_gate_lr"), got[1:]):
        landed[n] = part
    d_w_out_e = matmul_tn(u_e, dy_e, D_MODEL, D_MODEL, 4 * MM_TILE, BF16, "even_w_out_grad")
    behind_mix = Exchange()
    behind_mix.scatter(d_w_out_e.reshape(N_DEV, 2 * D_MODEL // N_DEV, D_MODEL))
    (dh, drest, d_sc_w), (p_w_out_e,) = even_mix_bwd(du_e, hf, hb, proj_e, small["sc_conv_w"], exchange=behind_mix)
    dh3 = dh.reshape(1, *dh.shape)
    adj_f = linear_scan(ab, 0, dh3, 0, True, True, "scan_fwd_adjoint")
    adj_b = linear_scan(ab, 2, dh3, 0, False, True, "scan_rev_adjoint")
    behind_gates_bwd = Exchange()
    behind_gates_bwd.scatter(jnp.transpose(d_w_in_o[:, :ODD_IN].reshape(D_MODEL, N_DEV, ODD_SHARD), (1, 0, 2)))
    behind_gates_bwd.scatter(d_sc_w, columns=True)
    (dua, d_gate_w, d_gate_b, d_lam), (p_w_in_o, landed["sc_conv_w"]) = even_gates_bwd(
        proj_e, adj_f, adj_b, hf, hb, dh, small["rg_conv_w"], conv_b, gate_w, gate_b, small["rg_lambda"], exchange=behind_gates_bwd)
    gate_w_rows = 4 * RG_HEADS * RG_HEAD_DIM
    behind_conv = Exchange()
    behind_conv.scatter(d_gate_w.reshape(N_DEV, gate_w_rows // N_DEV, RG_HEAD_DIM))
    behind_conv.scatter(d_lam, columns=True)
    (dproj_e, d_conv_w, d_conv_b), (p_gate_w, landed["rg_lambda"]) = rg_conv_bwd(dua, proj_e, drest, small["rg_conv_w"],
                                                                                 exchange=behind_conv)
    behind_w_grad = Exchange()
    behind_w_grad.gather(sum_parts(p_gate_w, "sum_gate_w"))
    d_w_in_e, (g_gate_w_all,) = matmul_tn(h_e, dproj_e, D_MODEL, EVEN_SHARD, 8 * MM_TILE, BF16, "even_w_in_grad",
                                          exchange=behind_w_grad)
    to_sibling = Exchange()
    to_sibling.to_sibling(d_w_in_e)
    to_sibling.scatter(d_conv_w, columns=True)
    from_sibling, landed["rg_conv_w"] = run_exchange(to_sibling, "scatter_to_sibling")
    behind_in_bwd = Exchange()
    behind_in_bwd.among_chips(pair_sum(d_w_in_e, from_sibling))
    (grad_x, d_even_norm_pre), (p_w_in_e,) = matmul_nt_normbwd(
        dproj_e, w_in_e, xs, even_norm_pre, dx1, 3 * D_MODEL, "even_in_bwd", exchange=behind_in_bwd)
    last = Exchange()
    replicated_vecs = ("even_norm_pre", "even_norm_post", "rg_conv_b")
    last.gather(jnp.concatenate([d_even_norm_pre, d_even_norm_post, d_conv_b], axis=0))
    last.gather(d_gate_b.reshape(4 * RG_HEADS, RG_HEAD_DIM))
    last.gather(loss_part)
    land_vec, land_gate_b, land_loss = run_exchange(last, "gather_last")

    results = {}

    def update(name, parts_, shape2d):
        outs = adamw(parts_, weights[name][0].reshape(shape2d), m_in[name][0].reshape(shape2d), v_in[name][0].reshape(shape2d),
                     "adamw_" + name)
        results[name] = [o.reshape(shapes[name]) for o in outs]

    update("even_w_in", p_w_in_e, (D_MODEL, EVEN_SHARD))
    update("even_w_out", p_w_out_e, (2 * D_MODEL // N_DEV, D_MODEL))
    update("odd_w_in", p_w_in_o, (D_MODEL, ODD_SHARD))
    update("odd_w_out", p_w_out_o, (D_MODEL // N_DEV, D_MODEL))
    update("rg_gate_w", g_gate_w_all.reshape(1, gate_w_rows, RG_HEAD_DIM), (gate_w_rows, RG_HEAD_DIM))
    small_out = adamw_small({n: landed[n] for n in SMALL_SHARDED}, weights, m_in, v_in)
    for n in SMALL_SHARDED:
        results[n] = [o[n] for o in small_out]
    gate_b_shape = (4 * RG_HEADS, RG_HEAD_DIM)
    rep_out, gate_b_out, loss_all = adamw_replicated(land_vec, land_gate_b, land_loss, replicated_vecs, weights, m_in, v_in,
                                                     [src["rg_gate_b"].reshape(gate_b_shape) for src in (weights, m_in, v_in)])
    results.update(rep_out)
    results["rg_gate_b"] = [o.reshape(shapes["rg_gate_b"]) for o in gate_b_out]

    return (loss_all[0, 0], grad_x.reshape(x.shape), *[results[n][0] for n in names], *[results[n][1] for n in names],
            *[results[n][2] for n in names], *[results[n][3] for n in names])
```

```python
import functools

import jax
import jax.numpy as jnp
from jax import lax
from jax.experimental import pallas as pl
from jax.experimental.pallas import tpu as pltpu

F32 = jnp.float32
BF16 = jnp.bfloat16

N_DEV = 8
D_MODEL = 1024
NORM_EPS = 1e-6
RG_HEADS = 8
RG_HEAD_DIM = 128
RG_C = 8.0
GLA_HEADS = 4
GLA_DK = 128
GLA_DV = 256
GLA_KEY = 512
GLA_RANK = 16
GLA_NORMALIZER = 16.0
GLA_CHUNK = 64
EVEN_IN = 6144
ODD_IN = 3104
ODD_IN_PAD = 3200
ODD_SHARD = ODD_IN // N_DEV
EVEN_SHARD = EVEN_IN // N_DEV
ADAM_LR = 0.001
ADAM_B1 = 0.9
ADAM_B2 = 0.999
ADAM_EPS = 1e-08
ADAM_WD = 0.01
ADAM_STEP = 10

SUBLANES = 8
LANES = 128
VMEM_LIMIT_BYTES = 48 * 2 ** 20
ROW_TILE = 256
MM_TILE = 512
PACK_ROWS = 48
MESH_ID = pl.DeviceIdType.MESH


def _params(n_grid):
    return pltpu.CompilerParams(dimension_semantics=("arbitrary",) * n_grid, vmem_limit_bytes=VMEM_LIMIT_BYTES)


def _bdot(a, b):
    return jnp.dot(a.astype(BF16), b.astype(BF16), preferred_element_type=F32)


def _bdot_nt(a, b):
    return lax.dot_general(a.astype(BF16), b.astype(BF16), (((1,), (1,)), ((), ())), preferred_element_type=F32)


def _bdot_tn(a, b):
    return lax.dot_general(a.astype(BF16), b.astype(BF16), (((0,), (0,)), ((), ())), preferred_element_type=F32)


def _rstd(x):
    return lax.rsqrt(jnp.mean(x * x, axis=-1, keepdims=True) + NORM_EPS)


def _rms(x, g):
    return x * _rstd(x) * g


def _rms_bwd(x, g, dy):
    xh = x * _rstd(x)
    dyg = dy * g
    dx = _rstd(x) * (dyg - xh * jnp.mean(dyg * xh, axis=-1, keepdims=True))
    return dx, jnp.sum(dy * xh, axis=0, keepdims=True)


def _sigmoid(z):
    return 0.5 * jnp.tanh(0.5 * z) + 0.5


def _silu_and_grad(z):
    s = _sigmoid(z)
    return z * s, s * (1.0 + z * (1.0 - s))


def _softplus(z):
    return jnp.maximum(z, 0.0) + jnp.log(1.0 + jnp.exp(-jnp.abs(z)))


def _shift_rows(cur, before, after, d):
    ts = cur.shape[0]
    row = lax.broadcasted_iota(jnp.int32, (SUBLANES, cur.shape[1]), 0)
    out = pltpu.roll(cur, (-d) % ts, 0)
    if d < 0:
        edge = jnp.where(row < -d, pltpu.roll(before, (-d) % SUBLANES, 0), out[:SUBLANES])
        return jnp.concatenate([edge, out[SUBLANES:]], axis=0)
    edge = jnp.where(row >= SUBLANES - d, pltpu.roll(after, (-d) % SUBLANES, 0), out[ts - SUBLANES:])
    return jnp.concatenate([out[:ts - SUBLANES], edge], axis=0)


def _halo_specs(ts, s, width, col):
    per = ts // SUBLANES
    last = s // SUBLANES - 1
    return [
        pl.BlockSpec((ts, width), lambda i: (i, col)),
        pl.BlockSpec((SUBLANES, width), lambda i: (jnp.maximum(i * per - 1, 0), col)),
        pl.BlockSpec((SUBLANES, width), lambda i: (jnp.minimum((i + 1) * per, last), col)),
    ]


def _halo_load(cur_ref, before_ref, after_ref, n_tiles):
    i = pl.program_id(0)
    before = jnp.where(i > 0, before_ref[...], 0.0)
    after = jnp.where(i < n_tiles - 1, after_ref[...], 0.0)
    return cur_ref[...], before, after


def _full(shape):
    return pl.BlockSpec(shape, lambda *_: (0,) * len(shape))


def _peer(x, y, c, mask):
    px, py, pc = x ^ (mask >> 2), y ^ ((mask >> 1) & 1), c ^ (mask & 1)
    return (px, py, pc), 4 * px + 2 * py + pc


class Exchange:
    SIBLING = 1
    OTHER_CHIPS = (2, 4, 6)

    def __init__(self):
        self.args, self.out_shape, self._kinds = [], [], []

    def gather(self, block, columns=False, via_sibling=False):
        shape = (block.shape[0], N_DEV * block.shape[1]) if columns else (N_DEV,) + block.shape
        return self._add(block, shape, ("gather", columns, via_sibling))

    def scatter(self, stack, columns=False):
        shape = (N_DEV, stack.shape[0], stack.shape[1] // N_DEV) if columns else stack.shape
        return self._add(stack, shape, ("scatter", columns, False))

    def _add(self, arg, shape, kind):
        self.args.append(arg)
        self.out_shape.append(jax.ShapeDtypeStruct(shape, arg.dtype))
        self._kinds.append(kind)
        return len(self.args) - 1

    def semaphores(self):
        n = len(self.args)
        return [pltpu.SemaphoreType.DMA((n, N_DEV - 1)), pltpu.SemaphoreType.DMA((n, N_DEV - 1)), pltpu.SemaphoreType.DMA((n,))]

    def to_sibling(self, array):
        shape = (N_DEV // 2, array.shape[0], array.shape[1] // N_DEV)
        return self._add(array, shape, ("to_sibling", True, False))

    def among_chips(self, stack):
        return self._add(stack, stack.shape, ("among_chips", False, False))

    def _copies(self, position, in_refs, out_refs):
        x, y, c, me = position
        for arr, ((kind, columns, via_sibling), src, out) in enumerate(zip(self._kinds, in_refs, out_refs)):
            if kind == "to_sibling":
                width = src.shape[-1] // N_DEV
                for k in range(N_DEV // 2):
                    block = src.at[:, pl.ds(pl.multiple_of((2 * k + 1 - c) * width, LANES), width)]
                    yield arr, k + 1, block, out.at[k], out.at[k], False, self.SIBLING
                continue
            for mask in range(N_DEV):
                _, peer_id = _peer(x, y, c, mask)
                relayed = via_sibling and mask not in (0, self.SIBLING) + self.OTHER_CHIPS
                if kind == "among_chips":
                    if mask in (0,) + self.OTHER_CHIPS:
                        yield arr, mask, src.at[peer_id // 2], out.at[me // 2], out.at[peer_id // 2], False, mask
                elif kind == "gather":
                    if columns:
                        width = src.shape[-1]
                        yield (arr, mask, src, out.at[:, pl.ds(pl.multiple_of(me * width, LANES), width)],
                               out.at[:, pl.ds(pl.multiple_of(peer_id * width, LANES), width)], relayed, mask)
                    else:
                        yield arr, mask, src, out.at[me], out.at[peer_id], relayed, mask
                else:
                    if columns:
                        width = src.shape[-1] // N_DEV
                        block = src.at[:, pl.ds(pl.multiple_of(peer_id * width, LANES), width)]
                    else:
                        block = src.at[peer_id]
                    yield arr, mask, block, out.at[me], out.at[peer_id], False, mask

    def _remote(self, position, sems, arr, slot, to_mask, src, dst):
        x, y, c, _ = position
        return pltpu.make_async_remote_copy(src_ref=src, dst_ref=dst, send_sem=sems[0].at[arr, slot - 1], recv_sem=sems[1].at[arr, slot - 1],
                                            device_id=_peer(x, y, c, to_mask)[0], device_id_type=MESH_ID)

    def start(self, position, in_refs, out_refs, sems):
        for arr, slot, src, dst, _, relayed, to_mask in self._copies(position, in_refs, out_refs):
            if slot == 0:
                pltpu.make_async_copy(src, dst, sems[2].at[arr]).start()
            elif not relayed:
                self._remote(position, sems, arr, slot, to_mask, src, dst).start()

    def wait(self, position, in_refs, out_refs, sems):
        copies = list(self._copies(position, in_refs, out_refs))
        landings = {(arr, slot): landing for arr, slot, _, _, landing, _, _ in copies}
        passed_on = set()
        for arr, mask, src, _, landing, relayed, _ in copies:
            if relayed:
                held = landings[arr, mask ^ self.SIBLING]
                self._remote(position, sems, arr, mask ^ self.SIBLING, mask ^ self.SIBLING, src, held).wait_recv()
                self._remote(position, sems, arr, mask, self.SIBLING, held, held).start()
                passed_on.add((arr, mask ^ self.SIBLING))
        for arr, slot, src, dst, landing, relayed, to_mask in copies:
            if slot == 0:
                pltpu.make_async_copy(src, dst, sems[2].at[arr]).wait()
                continue
            if (arr, slot) not in passed_on:
                self._remote(position, sems, arr, slot, to_mask, src, landing).wait_recv()
            if relayed:
                held = landings[arr, slot ^ self.SIBLING]
                self._remote(position, sems, arr, slot, self.SIBLING, held, held).wait_send()
            else:
                self._remote(position, sems, arr, slot, to_mask, src, dst).wait_send()


def _call(body, *, name, grid, in_specs, out_specs, out_shape, args, scratch_shapes=(), exchange=None):
    single = not isinstance(out_shape, (list, tuple))
    if single:
        out_specs, out_shape = [out_specs], [out_shape]
    params = _params(len(grid))
    if exchange is None:
        outs = pl.pallas_call(body, name=name, grid=grid, in_specs=in_specs, out_specs=out_specs, out_shape=out_shape,
                              scratch_shapes=list(scratch_shapes), compiler_params=params)(*args)
        return outs[0] if single else outs
    counts = (len(args), len(exchange.args), len(out_shape), len(exchange.out_shape), len(scratch_shapes), 3)

    def wrapped(*refs):
        groups, at = [], 0
        for n in counts:
            groups.append(refs[at:at + n])
            at += n
        main_in, ex_in, main_out, ex_out, main_scratch, sems = groups
        x, y, c = lax.axis_index("x"), lax.axis_index("y"), lax.axis_index("c")
        position = (x, y, c, 4 * x + 2 * y + c)
        ids = [pl.program_id(a) for a in range(len(grid))]
        first = functools.reduce(jnp.logical_and, [i == 0 for i in ids])
        last = functools.reduce(jnp.logical_and, [i == g - 1 for i, g in zip(ids, grid)])

        @pl.when(first)
        def _():
            exchange.start(position, ex_in, ex_out, sems)

        body(*main_in, *main_out, *main_scratch)

        @pl.when(last)
        def _():
            exchange.wait(position, ex_in, ex_out, sems)

    hbm = pl.BlockSpec(memory_space=pl.ANY)
    outs = pl.pallas_call(
        wrapped, name=name, grid=grid, in_specs=list(in_specs) + [hbm] * counts[1], out_specs=list(out_specs) + [hbm] * counts[3],
        out_shape=list(out_shape) + exchange.out_shape, scratch_shapes=list(scratch_shapes) + exchange.semaphores(),
        compiler_params=params)(*args, *exchange.args)
    main = outs[:counts[2]]
    return (main[0] if single else main), outs[counts[2]:]


def run_exchange(exchange, name):
    return _call(lambda: None, name=name, grid=(1,), in_specs=[], out_specs=[], out_shape=[], args=[], exchange=exchange)[1]


def rms_matmul(x, g, w, tm, tn, name, exchange=None):
    s, d = x.shape
    n = w.shape[1]
    tm = min(tm, s)

    def body(x_ref, g_ref, w_ref, o_ref, h_ref):
        @pl.when(pl.program_id(1) == 0)
        def _():
            h_ref[...] = _rms(x_ref[...], g_ref[...]).astype(BF16)

        o_ref[...] = jnp.dot(h_ref[...], w_ref[...], preferred_element_type=F32)

    return _call(
        body, name=name, grid=(s // tm, n // tn),
        in_specs=[pl.BlockSpec((tm, d), lambda i, j: (i, 0)), _full((1, d)), pl.BlockSpec((d, tn), lambda i, j: (0, j))],
        out_specs=[pl.BlockSpec((tm, tn), lambda i, j: (i, j)), pl.BlockSpec((tm, d), lambda i, j: (i, 0))],
        out_shape=[jax.ShapeDtypeStruct((s, n), F32), jax.ShapeDtypeStruct((s, d), BF16)],
        args=[x, g, w], exchange=exchange)


def matmul_post(u, w, xres, g, name, target=None):
    s, k = u.shape
    d = w.shape[1]
    tm = min(MM_TILE, s)
    with_loss = target is not None

    def body(*refs):
        if with_loss:
            u_ref, w_ref, x_ref, g_ref, t_ref, y_ref, dout_ref, loss_ref = refs
        else:
            u_ref, w_ref, x_ref, g_ref, y_ref, out_ref = refs
        y = jnp.dot(u_ref[...], w_ref[...], preferred_element_type=F32)
        y_ref[...] = y
        out = x_ref[...] + _rms(y, g_ref[...])
        if with_loss:
            @pl.when(pl.program_id(0) == 0)
            def _():
                loss_ref[...] = jnp.zeros_like(loss_ref)

            diff = out - t_ref[...]
            dout_ref[...] = diff * (1.0 / d)
            loss_ref[...] += 0.5 * jnp.sum(jnp.mean(diff * diff, axis=-1, keepdims=True))
        else:
            out_ref[...] = out

    row = pl.BlockSpec((tm, d), lambda i: (i, 0))
    in_specs = [pl.BlockSpec((tm, k), lambda i: (i, 0)), _full((k, d)), row, _full((1, d))]
    args = [u, w, xres, g]
    out_specs = [row, row]
    out_shape = [jax.ShapeDtypeStruct((s, d), F32), jax.ShapeDtypeStruct((s, d), F32)]
    if with_loss:
        in_specs.append(row)
        args.append(target)
        out_specs.append(_full((SUBLANES, LANES)))
        out_shape.append(jax.ShapeDtypeStruct((SUBLANES, LANES), F32))
    return pl.pallas_call(body, name=name, grid=(s // tm,), in_specs=in_specs, out_specs=out_specs,
                          out_shape=out_shape, compiler_params=_params(1))(*args)


def normbwd_matmul_nt(y, g, dout, w, tn, name, exchange=None):
    s, d = y.shape
    n = w.shape[0]
    tm = min(MM_TILE, s)

    def body(y_ref, g_ref, dout_ref, w_ref, du_ref, dy_ref, dg_ref):
        i, j = pl.program_id(0), pl.program_id(1)

        @pl.when(j == 0)
        def _():
            dy, dg = _rms_bwd(y_ref[...], g_ref[...], dout_ref[...])
            dy_ref[...] = dy.astype(BF16)

            @pl.when(i == 0)
            def _():
                dg_ref[...] = jnp.zeros_like(dg_ref)

            dg_ref[...] += dg

        du_ref[...] = lax.dot_general(dy_ref[...], w_ref[...], (((1,), (1,)), ((), ())), preferred_element_type=F32)

    row = pl.BlockSpec((tm, d), lambda i, j: (i, 0))
    return _call(
        body, name=name, grid=(s // tm, n // tn),
        in_specs=[row, _full((1, d)), row, pl.BlockSpec((tn, d), lambda i, j: (j, 0))],
        out_specs=[pl.BlockSpec((tm, tn), lambda i, j: (i, j)), row, _full((1, d))],
        out_shape=[jax.ShapeDtypeStruct((s, n), F32), jax.ShapeDtypeStruct((s, d), BF16), jax.ShapeDtypeStruct((1, d), F32)],
        args=[y, g, dout, w], exchange=exchange)


def matmul_tn(a, b, tm, tn, ts, out_dtype, name, exchange=None):
    s, m = a.shape
    n = b.shape[1]
    ts = min(ts, s)
    n_k = s // ts

    def body(a_ref, b_ref, o_ref, acc):
        k = pl.program_id(2)

        @pl.when(k == 0)
        def _():
            acc[...] = jnp.zeros_like(acc)

        acc[...] += lax.dot_general(a_ref[...], b_ref[...], (((0,), (0,)), ((), ())), preferred_element_type=F32)

        @pl.when(k == n_k - 1)
        def _():
            o_ref[...] = acc[...].astype(out_dtype)

    return _call(
        body, name=name, grid=(m // tm, n // tn, n_k),
        in_specs=[pl.BlockSpec((ts, tm), lambda i, j, k: (k, i)), pl.BlockSpec((ts, tn), lambda i, j, k: (k, j))],
        out_specs=pl.BlockSpec((tm, tn), lambda i, j, k: (i, j)),
        out_shape=jax.ShapeDtypeStruct((m, n), out_dtype),
        scratch_shapes=[pltpu.VMEM((tm, tn), F32)], args=[a, b], exchange=exchange)


def matmul_nt_normbwd(dproj, w, x, g, dres, tk, name, exchange=None):
    s, kt = dproj.shape
    d = w.shape[0]
    tm = min(MM_TILE, s)
    n_k = kt // tk

    def body(a_ref, w_ref, x_ref, g_ref, r_ref, dx_ref, dg_ref, acc):
        i, k = pl.program_id(0), pl.program_id(1)

        @pl.when(k == 0)
        def _():
            acc[...] = jnp.zeros_like(acc)

        acc[...] += lax.dot_general(a_ref[...], w_ref[...], (((1,), (1,)), ((), ())), preferred_element_type=F32)

        @pl.when(k == n_k - 1)
        def _():
            dx, dg = _rms_bwd(x_ref[...], g_ref[...], acc[...])
            dx_ref[...] = r_ref[...] + dx

            @pl.when(i == 0)
            def _():
                dg_ref[...] = jnp.zeros_like(dg_ref)

            dg_ref[...] += dg

    row = pl.BlockSpec((tm, d), lambda i, k: (i, 0))
    return _call(
        body, name=name, grid=(s // tm, n_k),
        in_specs=[pl.BlockSpec((tm, tk), lambda i, k: (i, k)), pl.BlockSpec((d, tk), lambda i, k: (0, k)), row, _full((1, d)), row],
        out_specs=[row, _full((1, d))],
        out_shape=[jax.ShapeDtypeStruct((s, d), F32), jax.ShapeDtypeStruct((1, d), F32)],
        scratch_shapes=[pltpu.VMEM((tm, d), F32)], args=[dproj, w, x, g, dres], exchange=exchange)


def _rg_conv(xa, before, after, cw, cb):
    return (cw[0:1, :] * _shift_rows(xa, before, after, -2) + cw[1:2, :] * _shift_rows(xa, before, after, -1)
            + cw[2:3, :] * xa + cw[3:4, :] * _shift_rows(xa, before, after, 1) + cb)


def _rg_gates(ua_h, gw_ref, gb_ref, c_h, direction, head):
    r = _sigmoid(_bdot(ua_h, gw_ref[2 * direction, head]) + gb_ref[2 * direction, head:head + 1, :])
    i = _sigmoid(_bdot(ua_h, gw_ref[2 * direction + 1, head]) + gb_ref[2 * direction + 1, head:head + 1, :])
    log_a = -c_h * r
    a = jnp.exp(log_a)
    beta = jnp.sqrt(-jnp.tanh(log_a) * (1.0 + a * a))
    return r, i, a, beta


def even_gates_fwd(proj, conv_w, conv_b, gate_w, gate_b, lam, exchange=None):
    s = proj.shape[0]
    ts = min(ROW_TILE, s)
    n_tiles = s // ts

    def body(xa_ref, xb_ref, xn_ref, cw_ref, cb_ref, gw_ref, gb_ref, lam_ref, o_ref):
        xa, before, after = _halo_load(xa_ref, xb_ref, xn_ref, n_tiles)
        ua = _rg_conv(xa, before, after, cw_ref[...], cb_ref[...])
        c = RG_C * _softplus(-lam_ref[...])
        for direction in range(2):
            for head in range(RG_HEADS):
                lanes = slice(head * RG_HEAD_DIM, (head + 1) * RG_HEAD_DIM)
                ua_h = ua[:, lanes]
                _, i, a, beta = _rg_gates(ua_h, gw_ref, gb_ref, c[direction:direction + 1, lanes], direction, head)
                o_ref[2 * direction, :, lanes] = a
                o_ref[2 * direction + 1, :, lanes] = beta * (i * ua_h)

    return _call(
        body, name="even_gates_fwd", grid=(n_tiles,),
        in_specs=_halo_specs(ts, s, D_MODEL, 0) + [_full(conv_w.shape), _full(conv_b.shape), _full(gate_w.shape),
                                                   _full(gate_b.shape), _full(lam.shape)],
        out_specs=pl.BlockSpec((4, ts, D_MODEL), lambda i: (0, i, 0)),
        out_shape=jax.ShapeDtypeStruct((4, s, D_MODEL), F32),
        args=[proj, proj, proj, conv_w, conv_b, gate_w, gate_b, lam], exchange=exchange)


def linear_scan(a_arr, a_idx, b_arr, b_idx, reverse, b_times_a, name, exchange=None):
    _, s, c = a_arr.shape
    ts = min(MM_TILE, s)
    n_tiles = s // ts
    n_blocks = ts // SUBLANES

    def tile_of(i):
        return n_tiles - 1 - i if reverse else i

    def body(a_ref, b_ref, h_ref, carry):
        @pl.when(pl.program_id(0) == 0)
        def _():
            carry[...] = jnp.zeros_like(carry)

        row = lax.broadcasted_iota(jnp.int32, (SUBLANES, c), 0)

        def block(j, h_in):
            r0 = pl.multiple_of((n_blocks - 1 - j if reverse else j) * SUBLANES, SUBLANES)
            a = a_ref[pl.ds(r0, SUBLANES), :]
            b = b_ref[pl.ds(r0, SUBLANES), :]
            if b_times_a:
                b = a * b
            for step in (1, 2, 4):
                shift = SUBLANES - step if reverse else step
                valid = row < SUBLANES - step if reverse else row >= step
                b = jnp.where(valid, a * pltpu.roll(b, shift, 0) + b, b)
                a = jnp.where(valid, a * pltpu.roll(a, shift, 0), a)
            h = a * h_in + b
            h_ref[pl.ds(r0, SUBLANES), :] = h
            return h[0:1, :] if reverse else h[SUBLANES - 1:SUBLANES, :]

        carry[0:1, :] = lax.fori_loop(0, n_blocks, block, carry[0:1, :])

    return _call(
        body, name=name, grid=(n_tiles,),
        in_specs=[pl.BlockSpec((None, ts, c), lambda i: (a_idx, tile_of(i), 0)),
                  pl.BlockSpec((None, ts, c), lambda i: (b_idx, tile_of(i), 0))],
        out_specs=pl.BlockSpec((ts, c), lambda i: (tile_of(i), 0)),
        out_shape=jax.ShapeDtypeStruct((s, c), F32),
        scratch_shapes=[pltpu.VMEM((SUBLANES, c), F32)], args=[a_arr, b_arr], exchange=exchange)


def _sc_conv(p, before, after, w):
    return w[0:1, :] * _shift_rows(p, before, after, -1) + w[1:2, :] * p + w[2:3, :] * _shift_rows(p, before, after, 1)


def even_mix_fwd(hf, hb, proj, sc_w, exchange=None):
    s = proj.shape[0]
    ts = min(ROW_TILE, s)
    n_tiles = s // ts
    row = pl.BlockSpec((ts, D_MODEL), lambda i: (i, 0))

    def col(c):
        return pl.BlockSpec((ts, D_MODEL), lambda i: (i, c))

    def body(hf_ref, hb_ref, za_ref, xb_ref, xbb_ref, xbn_ref, gb_ref, gc_ref, gcb_ref, gcn_ref, zb_ref, w_ref, u_ref):
        xb, xb_before, xb_after = _halo_load(xb_ref, xbb_ref, xbn_ref, n_tiles)
        gc, gc_before, gc_after = _halo_load(gc_ref, gcb_ref, gcn_ref, n_tiles)
        silu_za, _ = _silu_and_grad(za_ref[...])
        silu_zb, _ = _silu_and_grad(zb_ref[...])
        u_ref[:, :D_MODEL] = ((hf_ref[...] + hb_ref[...]) * silu_za).astype(BF16)
        cv = _sc_conv(gc * xb, gc_before * xb_before, gc_after * xb_after, w_ref[...])
        u_ref[:, D_MODEL:] = (gb_ref[...] * cv * silu_zb).astype(BF16)

    return _call(
        body, name="even_mix_fwd", grid=(n_tiles,),
        in_specs=[row, row, col(1)] + _halo_specs(ts, s, D_MODEL, 2) + [col(3)] + _halo_specs(ts, s, D_MODEL, 4)
        + [col(5), _full(sc_w.shape)],
        out_specs=pl.BlockSpec((ts, 2 * D_MODEL), lambda i: (i, 0)),
        out_shape=jax.ShapeDtypeStruct((s, 2 * D_MODEL), BF16),
        args=[hf, hb, proj, proj, proj, proj, proj, proj, proj, proj, proj, sc_w], exchange=exchange)


def even_mix_bwd(du, hf, hb, proj, sc_w, exchange=None):
    s = proj.shape[0]
    ts = min(ROW_TILE, s)
    n_tiles = s // ts
    row = pl.BlockSpec((ts, D_MODEL), lambda i: (i, 0))

    def body(dya_ref, dyb_ref, dybb_ref, dybn_ref, hf_ref, hb_ref, za_ref, xb_ref, xbb_ref, xbn_ref,
             gb_ref, gbb_ref, gbn_ref, gc_ref, gcb_ref, gcn_ref, zb_ref, zbb_ref, zbn_ref, w_ref,
             dh_ref, dp_ref, dw_ref):
        dyb, dyb_before, dyb_after = _halo_load(dyb_ref, dybb_ref, dybn_ref, n_tiles)
        xb, xb_before, xb_after = _halo_load(xb_ref, xbb_ref, xbn_ref, n_tiles)
        gb, gb_before, gb_after = _halo_load(gb_ref, gbb_ref, gbn_ref, n_tiles)
        gc, gc_before, gc_after = _halo_load(gc_ref, gcb_ref, gcn_ref, n_tiles)
        zb, zb_before, zb_after = _halo_load(zb_ref, zbb_ref, zbn_ref, n_tiles)
        w = w_ref[...]
        dya, za = dya_ref[...], za_ref[...]
        silu_za, dsilu_za = _silu_and_grad(za)
        dh_ref[...] = dya * silu_za
        dp_ref[:, 0:D_MODEL] = (dya * (hf_ref[...] + hb_ref[...]) * dsilu_za).astype(BF16)

        silu_zb, dsilu_zb = _silu_and_grad(zb)
        p, p_before, p_after = gc * xb, gc_before * xb_before, gc_after * xb_after
        cv = _sc_conv(p, p_before, p_after, w)
        dcv = dyb * gb * silu_zb
        dcv_before = dyb_before * gb_before * _silu_and_grad(zb_before)[0]
        dcv_after = dyb_after * gb_after * _silu_and_grad(zb_after)[0]
        dpp = (w[0:1, :] * _shift_rows(dcv, dcv_before, dcv_after, 1) + w[1:2, :] * dcv
               + w[2:3, :] * _shift_rows(dcv, dcv_before, dcv_after, -1))
        dp_ref[:, D_MODEL:2 * D_MODEL] = (dpp * gc).astype(BF16)
        dp_ref[:, 2 * D_MODEL:3 * D_MODEL] = (dyb * cv * silu_zb).astype(BF16)
        dp_ref[:, 3 * D_MODEL:4 * D_MODEL] = (dpp * xb).astype(BF16)
        dp_ref[:, 4 * D_MODEL:5 * D_MODEL] = (dyb * gb * cv * dsilu_zb).astype(BF16)

        @pl.when(pl.program_id(0) == 0)
        def _():
            dw_ref[...] = jnp.zeros_like(dw_ref)

        dw_ref[0:1, :] += jnp.sum(dcv * _shift_rows(p, p_before, p_after, -1), axis=0, keepdims=True)
        dw_ref[1:2, :] += jnp.sum(dcv * p, axis=0, keepdims=True)
        dw_ref[2:3, :] += jnp.sum(dcv * _shift_rows(p, p_before, p_after, 1), axis=0, keepdims=True)

    return _call(
        body, name="even_mix_bwd", grid=(n_tiles,),
        in_specs=[row] + _halo_specs(ts, s, D_MODEL, 1) + [row, row, pl.BlockSpec((ts, D_MODEL), lambda i: (i, 1))]
        + _halo_specs(ts, s, D_MODEL, 2) + _halo_specs(ts, s, D_MODEL, 3) + _halo_specs(ts, s, D_MODEL, 4)
        + _halo_specs(ts, s, D_MODEL, 5) + [_full(sc_w.shape)],
        out_specs=[row, pl.BlockSpec((ts, 5 * D_MODEL), lambda i: (i, 0)), _full(sc_w.shape)],
        out_shape=[jax.ShapeDtypeStruct((s, D_MODEL), F32), jax.ShapeDtypeStruct((s, 5 * D_MODEL), BF16),
                   jax.ShapeDtypeStruct(sc_w.shape, F32)],
        args=[du, du, du, du, hf, hb, proj, *([proj] * 12), sc_w], exchange=exchange)


def even_gates_bwd(proj, adj_f, adj_b, hf, hb, dh, conv_w, conv_b, gate_w, gate_b, lam, exchange=None):
    s = proj.shape[0]
    ts = min(ROW_TILE, s)
    n_tiles = s // ts
    row = pl.BlockSpec((ts, D_MODEL), lambda i: (i, 0))

    def body(xa_ref, xab_ref, xan_ref, af_ref, afb_ref, afn_ref, ab_ref, abb_ref, abn_ref,
             hf_ref, hfb_ref, hfn_ref, hb_ref, hbb_ref, hbn_ref, dh_ref,
             cw_ref, cb_ref, gw_ref, gb_ref, lam_ref, dua_ref, dgw_ref, dgb_ref, dlam_ref):
        @pl.when(pl.program_id(0) == 0)
        def _():
            dgw_ref[...] = jnp.zeros_like(dgw_ref)
            dgb_ref[...] = jnp.zeros_like(dgb_ref)
            dlam_ref[...] = jnp.zeros_like(dlam_ref)

        xa, before, after = _halo_load(xa_ref, xab_ref, xan_ref, n_tiles)
        ua = _rg_conv(xa, before, after, cw_ref[...], cb_ref[...])
        lam_v = lam_ref[...]
        c = RG_C * _softplus(-lam_v)
        dc_dlam = -RG_C * _sigmoid(-lam_v)
        dh = dh_ref[...]
        adj = (_halo_load(af_ref, afb_ref, afn_ref, n_tiles), _halo_load(ab_ref, abb_ref, abn_ref, n_tiles))
        hs = (_halo_load(hf_ref, hfb_ref, hfn_ref, n_tiles), _halo_load(hb_ref, hbb_ref, hbn_ref, n_tiles))
        dua = jnp.zeros_like(ua)
        for direction in range(2):
            step = 1 if direction == 0 else -1
            g = dh + _shift_rows(*adj[direction], step)
            da_all = g * _shift_rows(*hs[direction], -step)
            dua_parts = []
            for head in range(RG_HEADS):
                lanes = slice(head * RG_HEAD_DIM, (head + 1) * RG_HEAD_DIM)
                ua_h = ua[:, lanes]
                c_h = c[direction:direction + 1, lanes]
                r, i, a, beta = _rg_gates(ua_h, gw_ref, gb_ref, c_h, direction, head)
                db = g[:, lanes]
                d_i = db * beta * ua_h
                dbeta = db * (i * ua_h)
                dlog_a = (da_all[:, lanes] - dbeta * a / beta) * a
                dpr = -c_h * dlog_a * r * (1.0 - r)
                dpi = d_i * i * (1.0 - i)
                dua_parts.append(db * beta * i + _bdot_nt(dpr, gw_ref[2 * direction, head])
                                 + _bdot_nt(dpi, gw_ref[2 * direction + 1, head]))
                dgw_ref[2 * direction, head] += _bdot_tn(ua_h, dpr)
                dgw_ref[2 * direction + 1, head] += _bdot_tn(ua_h, dpi)
                dgb_ref[2 * direction, head:head + 1, :] += jnp.sum(dpr, axis=0, keepdims=True)
                dgb_ref[2 * direction + 1, head:head + 1, :] += jnp.sum(dpi, axis=0, keepdims=True)
                dlam_ref[direction:direction + 1, lanes] += (
                    jnp.sum(-r * dlog_a, axis=0, keepdims=True) * dc_dlam[direction:direction + 1, lanes])
            dua = dua + jnp.concatenate(dua_parts, axis=1)
        dua_ref[...] = dua

    return _call(
        body, name="even_gates_bwd", grid=(n_tiles,),
        in_specs=_halo_specs(ts, s, D_MODEL, 0) * 5 + [row] + [_full(conv_w.shape), _full(conv_b.shape), _full(gate_w.shape),
                                                             _full(gate_b.shape), _full(lam.shape)],
        out_specs=[row, _full(gate_w.shape), _full(gate_b.shape), _full(lam.shape)],
        out_shape=[jax.ShapeDtypeStruct((s, D_MODEL), F32), jax.ShapeDtypeStruct(gate_w.shape, F32),
                   jax.ShapeDtypeStruct(gate_b.shape, F32), jax.ShapeDtypeStruct(lam.shape, F32)],
        args=[proj, proj, proj, adj_f, adj_f, adj_f, adj_b, adj_b, adj_b, hf, hf, hf, hb, hb, hb, dh, conv_w, conv_b, gate_w,
              gate_b, lam], exchange=exchange)


def rg_conv_bwd(dua, proj, drest, conv_w, exchange=None):
    s = proj.shape[0]
    ts = min(ROW_TILE, s)
    n_tiles = s // ts

    def body(du_ref, dub_ref, dun_ref, xa_ref, xab_ref, xan_ref, dr_ref, cw_ref, dp_ref, dw_ref, db_ref):
        @pl.when(pl.program_id(0) == 0)
        def _():
            dw_ref[...] = jnp.zeros_like(dw_ref)
            db_ref[...] = jnp.zeros_like(db_ref)

        dua, dua_before, dua_after = _halo_load(du_ref, dub_ref, dun_ref, n_tiles)
        xa, xa_before, xa_after = _halo_load(xa_ref, xab_ref, xan_ref, n_tiles)
        cw = cw_ref[...]
        dxa = (cw[0:1, :] * _shift_rows(dua, dua_before, dua_after, 2) + cw[1:2, :] * _shift_rows(dua, dua_before, dua_after, 1)
               + cw[2:3, :] * dua + cw[3:4, :] * _shift_rows(dua, dua_before, dua_after, -1))
        dp_ref[:, :D_MODEL] = dxa.astype(BF16)
        dp_ref[:, D_MODEL:] = dr_ref[...]
        for tap, offset in enumerate((-2, -1, 0, 1)):
            shifted = xa if offset == 0 else _shift_rows(xa, xa_before, xa_after, offset)
            dw_ref[tap:tap + 1, :] += jnp.sum(dua * shifted, axis=0, keepdims=True)
        db_ref[...] += jnp.sum(dua, axis=0, keepdims=True)

    return _call(
        body, name="rg_conv_bwd", grid=(n_tiles,),
        in_specs=_halo_specs(ts, s, D_MODEL, 0) * 2 + [pl.BlockSpec((ts, 5 * D_MODEL), lambda i: (i, 0)), _full(conv_w.shape)],
        out_specs=[pl.BlockSpec((ts, EVEN_IN), lambda i: (i, 0)), _full(conv_w.shape), _full((1, D_MODEL))],
        out_shape=[jax.ShapeDtypeStruct((s, EVEN_IN), BF16), jax.ShapeDtypeStruct(conv_w.shape, F32),
                   jax.ShapeDtypeStruct((1, D_MODEL), F32)],
        args=[dua, dua, dua, proj, proj, proj, drest, conv_w], exchange=exchange)


def _split3(x):
    x1 = x.astype(BF16)
    rest = x - x1.astype(F32)
    x2 = rest.astype(BF16)
    return x1, x2, (rest - x2.astype(F32)).astype(BF16)


def _chunk_sum_matrix(t, reverse, transpose):
    i = lax.broadcasted_iota(jnp.int32, (t, t), 0)
    j = lax.broadcasted_iota(jnp.int32, (t, t), 1)
    if transpose:
        i, j = j, i
    same = (i // GLA_CHUNK) == (j // GLA_CHUNK)
    return jnp.where(same & ((j >= i) if reverse else (j <= i)), 1.0, 0.0).astype(BF16)


def _exact_dot(m, x):
    return sum(jnp.dot(m, part, preferred_element_type=F32) for part in _split3(x))


def _chunk_mask(t, reverse):
    i = lax.broadcasted_iota(jnp.int32, (t, t), 0)
    j = lax.broadcasted_iota(jnp.int32, (t, t), 1)
    return ((i // GLA_CHUNK) == (j // GLA_CHUNK)) & ((j >= i) if reverse else (j <= i))


def _chunk_rows(c):
    return slice(c * GLA_CHUNK, (c + 1) * GLA_CHUNK)


def _gla_gate(lr, wg, bg):
    z = _bdot(lr, wg) + bg
    log_alpha = (jnp.minimum(z, 0.0) - jnp.log(1.0 + jnp.exp(-jnp.abs(z)))) * (1.0 / GLA_NORMALIZER)
    return z, log_alpha


def _gla_tile_terms(q, k, bcum, reverse):
    n_chunks = q.shape[0] // GLA_CHUNK
    totals = []
    for c in range(n_chunks):
        edge = c * GLA_CHUNK if reverse else (c + 1) * GLA_CHUNK - 1
        totals.append(bcum[edge:edge + 1, :])
    btot = jnp.concatenate([jnp.broadcast_to(total, (GLA_CHUNK, total.shape[1])) for total in totals], axis=0)
    e_pos, e_neg, e_st = jnp.exp(bcum), jnp.exp(-bcum), jnp.exp(btot - bcum)
    return q * (GLA_DK ** -0.5) * e_pos, k * e_neg, k * e_st, e_pos, e_neg, e_st, [jnp.exp(total) for total in totals]


def _gla_specs(t, n_tiles, reverse_order):
    def tile(i):
        return n_tiles - 1 - i if reverse_order else i

    return tile, [
        pl.BlockSpec((t, GLA_KEY), lambda i: (tile(i), 0)),
        pl.BlockSpec((t, GLA_KEY), lambda i: (tile(i), 1)),
        pl.BlockSpec((t, D_MODEL), lambda i: (tile(i), 1)),
        pl.BlockSpec((t, LANES), lambda i: (tile(i), (ODD_IN_PAD - LANES) // LANES)),
    ]


def gla_fwd(proj, wg, bg, reverse, o_other=None, gnorm=None):
    s = proj.shape[0]
    t = min(ROW_TILE, s)
    n_tiles = s // t
    n_chunks = t // GLA_CHUNK
    final = o_other is not None
    tile, specs = _gla_specs(t, n_tiles, reverse)

    def body(*refs):
        if final:
            q_ref, k_ref, v_ref, lr_ref, wg_ref, bg_ref, oo_ref, r_ref, gn_ref, osum_ref, u_ref, st_ref, state = refs
        else:
            q_ref, k_ref, v_ref, lr_ref, wg_ref, bg_ref, o_ref, st_ref, state = refs
            osum_ref = o_ref

        @pl.when(pl.program_id(0) == 0)
        def _():
            state[...] = jnp.zeros_like(state)

        _, log_alpha = _gla_gate(lr_ref[...], wg_ref[...], bg_ref[...])
        bcum = _exact_dot(_chunk_sum_matrix(t, reverse, False), log_alpha)
        q, k, v = q_ref[...], k_ref[...], v_ref[...]
        q_in, k_in, k_st, _, _, _, decays = _gla_tile_terms(q, k, bcum, reverse)
        mask = _chunk_mask(t, reverse)
        order = list(range(n_chunks))[::-1] if reverse else list(range(n_chunks))
        intra, increments = [], []
        for head in range(GLA_HEADS):
            kl = slice(head * GLA_DK, (head + 1) * GLA_DK)
            vl = slice(head * GLA_DV, (head + 1) * GLA_DV)
            scores = jnp.where(mask, _bdot_nt(q_in[:, kl], k_in[:, kl]), 0.0)
            intra.append(_bdot(scores, v[:, vl]))
            increments.append([_bdot_tn(v[_chunk_rows(c), vl], k_st[_chunk_rows(c), kl]) for c in range(n_chunks)])
        for head in range(GLA_HEADS):
            kl = slice(head * GLA_DK, (head + 1) * GLA_DK)
            vl = slice(head * GLA_DV, (head + 1) * GLA_DV)
            running = state[head]
            before = [None] * n_chunks
            for c in order:
                before[c] = running
                st_ref[c, head] = running
                running = running * decays[c][:, kl] + increments[head][c]
            state[head] = running
            inter = [_bdot_nt(q_in[_chunk_rows(c), kl], before[c]) for c in range(n_chunks)]
            osum_ref[:, vl] = intra[head] + jnp.concatenate(inter, axis=0)
        if final:
            osum = osum_ref[...] + oo_ref[...]
            osum_ref[...] = osum
            silu_r, _ = _silu_and_grad(r_ref[...])
            gn = gn_ref[...]
            for head in range(GLA_HEADS):
                vl = slice(head * GLA_DV, (head + 1) * GLA_DV)
                u_ref[:, vl] = (_rms(osum[:, vl], gn[:, vl]) * silu_r[:, vl]).astype(BF16)

    row = pl.BlockSpec((t, D_MODEL), lambda i: (tile(i), 0))
    st_spec = pl.BlockSpec((n_chunks, GLA_HEADS, GLA_DV, GLA_DK), lambda i: (tile(i), 0, 0, 0))
    st_shape = jax.ShapeDtypeStruct((s // GLA_CHUNK, GLA_HEADS, GLA_DV, GLA_DK), F32)
    in_specs = specs + [_full(wg.shape), _full(bg.shape)]
    args = [proj, proj, proj, proj, wg, bg]
    if final:
        in_specs += [row, pl.BlockSpec((t, D_MODEL), lambda i: (tile(i), 2)), _full(gnorm.shape)]
        args += [o_other, proj, gnorm]
        out_specs = [row, row, st_spec]
        out_shape = [jax.ShapeDtypeStruct((s, D_MODEL), F32), jax.ShapeDtypeStruct((s, D_MODEL), BF16), st_shape]
    else:
        out_specs = [row, st_spec]
        out_shape = [jax.ShapeDtypeStruct((s, D_MODEL), F32), st_shape]
    return pl.pallas_call(
        body, name="gla_fwd_rev" if reverse else "gla_fwd", grid=(n_tiles,), in_specs=in_specs, out_specs=out_specs,
        out_shape=out_shape, scratch_shapes=[pltpu.VMEM((GLA_HEADS, GLA_DV, GLA_DK), F32)], compiler_params=_params(1),
    )(*args)


def gla_out_bwd(du, proj, osum, gnorm):
    s = proj.shape[0]
    ts = min(ROW_TILE, s)
    row = pl.BlockSpec((ts, D_MODEL), lambda i: (i, 0))

    def body(du_ref, r_ref, o_ref, gn_ref, do_ref, dr_ref, dgn_ref):
        @pl.when(pl.program_id(0) == 0)
        def _():
            dgn_ref[...] = jnp.zeros_like(dgn_ref)

        du, osum, gn = du_ref[...], o_ref[...], gn_ref[...]
        silu_r, dsilu_r = _silu_and_grad(r_ref[...])
        for head in range(GLA_HEADS):
            vl = slice(head * GLA_DV, (head + 1) * GLA_DV)
            o_h, g_h, du_h = osum[:, vl], gn[:, vl], du[:, vl]
            dr_ref[:, vl] = (du_h * _rms(o_h, g_h) * dsilu_r[:, vl]).astype(BF16)
            do_h, dg_h = _rms_bwd(o_h, g_h, du_h * silu_r[:, vl])
            do_ref[:, vl] = do_h
            dgn_ref[...] += dg_h

    return pl.pallas_call(
        body, name="gla_out_bwd", grid=(s // ts,),
        in_specs=[row, pl.BlockSpec((ts, D_MODEL), lambda i: (i, 2)), row, _full(gnorm.shape)],
        out_specs=[row, row, _full((1, GLA_DV))],
        out_shape=[jax.ShapeDtypeStruct((s, D_MODEL), F32), jax.ShapeDtypeStruct((s, D_MODEL), BF16),
                   jax.ShapeDtypeStruct((1, GLA_DV), F32)],
        compiler_params=_params(1),
    )(du, proj, osum, gnorm)


def gla_bwd(proj, wg, bg, do, states, reverse, first=None):
    s = proj.shape[0]
    t = min(ROW_TILE, s)
    n_tiles = s // t
    n_chunks = t // GLA_CHUNK
    final = first is not None
    tile, specs = _gla_specs(t, n_tiles, not reverse)

    def body(*refs):
        if final:
            (q_ref, k_ref, v_ref, lr_ref, wg_ref, bg_ref, do_ref, st_ref, dqkv1_ref, dlr1_ref, dr_ref,
             dp_ref, dwg_ref, dbg_ref, dstate, dqkv, dbc, dbt) = refs
        else:
            (q_ref, k_ref, v_ref, lr_ref, wg_ref, bg_ref, do_ref, st_ref,
             dqkv, dlr_ref, dwg_ref, dbg_ref, dstate, dbc, dbt) = refs

        @pl.when(pl.program_id(0) == 0)
        def _():
            dstate[...] = jnp.zeros_like(dstate)
            dwg_ref[...] = jnp.zeros_like(dwg_ref)
            dbg_ref[...] = jnp.zeros_like(dbg_ref)

        lr, wg_v = lr_ref[...], wg_ref[...]
        z, log_alpha = _gla_gate(lr, wg_v, bg_ref[...])
        bcum = _exact_dot(_chunk_sum_matrix(t, reverse, False), log_alpha)
        q, k, v, do_v = q_ref[...], k_ref[...], v_ref[...], do_ref[...]
        q_in, k_in, k_st, e_pos, e_neg, e_st, decays = _gla_tile_terms(q, k, bcum, reverse)
        mask = _chunk_mask(t, reverse)
        order = list(range(n_chunks)) if reverse else list(range(n_chunks))[::-1]
        dq_intra, dk_intra, dv_intra, increments = [], [], [], []
        for head in range(GLA_HEADS):
            kl = slice(head * GLA_DK, (head + 1) * GLA_DK)
            vl = slice(head * GLA_DV, (head + 1) * GLA_DV)
            scores = jnp.where(mask, _bdot_nt(q_in[:, kl], k_in[:, kl]), 0.0)
            dscores = jnp.where(mask, _bdot_nt(do_v[:, vl], v[:, vl]), 0.0)
            dv_intra.append(_bdot_tn(scores, do_v[:, vl]))
            dq_intra.append(_bdot(dscores, k_in[:, kl]))
            dk_intra.append(_bdot_tn(dscores, q_in[:, kl]))
            increments.append([_bdot_tn(do_v[_chunk_rows(c), vl], q_in[_chunk_rows(c), kl]) for c in range(n_chunks)])
        for head in range(GLA_HEADS):
            kl = slice(head * GLA_DK, (head + 1) * GLA_DK)
            vl = slice(head * GLA_DV, (head + 1) * GLA_DV)
            running = dstate[head]
            after, ddecay = [None] * n_chunks, [None] * n_chunks
            for c in order:
                after[c] = running
                ddecay[c] = jnp.sum(running * st_ref[c, head], axis=0, keepdims=True)
                running = running * decays[c][:, kl] + increments[head][c]
            dstate[head] = running
            dq_inter = jnp.concatenate([_bdot(do_v[_chunk_rows(c), vl], st_ref[c, head]) for c in range(n_chunks)], axis=0)
            dv_inter = jnp.concatenate([_bdot_nt(k_st[_chunk_rows(c), kl], after[c]) for c in range(n_chunks)], axis=0)
            dk_st = jnp.concatenate([_bdot(v[_chunk_rows(c), vl], after[c]) for c in range(n_chunks)], axis=0)
            dq_in = dq_intra[head] + dq_inter
            ks_h = k_st[:, kl]
            dqkv[:, 2 * GLA_KEY + head * GLA_DV:2 * GLA_KEY + (head + 1) * GLA_DV] = dv_intra[head] + dv_inter
            dqkv[:, kl] = dq_in * (GLA_DK ** -0.5) * e_pos[:, kl]
            dqkv[:, GLA_KEY + head * GLA_DK:GLA_KEY + (head + 1) * GLA_DK] = dk_intra[head] * e_neg[:, kl] + dk_st * e_st[:, kl]
            dbc[:, kl] = dq_in * q_in[:, kl] - dk_intra[head] * k_in[:, kl] - dk_st * ks_h
            weighted = dk_st * ks_h
            for c in range(n_chunks):
                dbtot = jnp.sum(weighted[_chunk_rows(c)], axis=0, keepdims=True) + ddecay[c] * decays[c][:, kl]
                dbt[_chunk_rows(c), kl] = jnp.broadcast_to(dbtot, (GLA_CHUNK, GLA_DK))
        dlog_alpha = _exact_dot(_chunk_sum_matrix(t, reverse, True), dbc[...]) + dbt[...]
        dz = dlog_alpha * _sigmoid(-z) * (1.0 / GLA_NORMALIZER)
        dlr = _bdot_nt(dz, wg_v)
        dwg_ref[...] += _bdot_tn(lr, dz)
        dbg_ref[...] += jnp.sum(dz, axis=0, keepdims=True)
        if final:
            dp_ref[:, :2 * D_MODEL] = (dqkv[...] + dqkv1_ref[...]).astype(BF16)
            dp_ref[:, 2 * D_MODEL:3 * D_MODEL] = dr_ref[...]
            dp_ref[:, 3 * D_MODEL:] = (dlr + dlr1_ref[...]).astype(BF16)
        else:
            dlr_ref[...] = dlr

    row = pl.BlockSpec((t, D_MODEL), lambda i: (tile(i), 0))
    wide = pl.BlockSpec((t, 2 * D_MODEL), lambda i: (tile(i), 0))
    narrow = pl.BlockSpec((t, LANES), lambda i: (tile(i), 0))
    st_spec = pl.BlockSpec((n_chunks, GLA_HEADS, GLA_DV, GLA_DK), lambda i: (tile(i), 0, 0, 0))
    in_specs = specs + [_full(wg.shape), _full(bg.shape), row, st_spec]
    args = [proj, proj, proj, proj, wg, bg, do, states]
    acc_specs = [_full(wg.shape), _full(bg.shape)]
    acc_shapes = [jax.ShapeDtypeStruct(wg.shape, F32), jax.ShapeDtypeStruct(bg.shape, F32)]
    scratch = [pltpu.VMEM((GLA_HEADS, GLA_DV, GLA_DK), F32)]
    work = [pltpu.VMEM((t, GLA_KEY), F32), pltpu.VMEM((t, GLA_KEY), F32)]
    if final:
        in_specs += [wide, narrow, row]
        args += list(first)
        out_specs = [pl.BlockSpec((t, ODD_IN_PAD), lambda i: (tile(i), 0))] + acc_specs
        out_shape = [jax.ShapeDtypeStruct((s, ODD_IN_PAD), BF16)] + acc_shapes
        scratch += [pltpu.VMEM((t, 2 * D_MODEL), F32)] + work
    else:
        out_specs = [wide, narrow] + acc_specs
        out_shape = [jax.ShapeDtypeStruct((s, 2 * D_MODEL), F32), jax.ShapeDtypeStruct((s, LANES), F32)] + acc_shapes
        scratch += work
    return pl.pallas_call(
        body, name="gla_bwd_rev" if reverse else "gla_bwd", grid=(n_tiles,), in_specs=in_specs, out_specs=out_specs,
        out_shape=out_shape, scratch_shapes=scratch, compiler_params=_params(1),
    )(*args)


def pair_sum(grad, from_sibling):
    n_chips, r, w = from_sibling.shape

    def body(even_ref, odd_ref, sib_ref, o_ref):
        mine = jnp.where(lax.axis_index("c") == 1, odd_ref[...], even_ref[...])
        o_ref[...] = (mine.astype(F32) + sib_ref[...].astype(F32)).astype(o_ref.dtype)

    return pl.pallas_call(
        body, name="pair_sum", grid=(n_chips,),
        in_specs=[pl.BlockSpec((r, w), lambda k: (0, 2 * k)), pl.BlockSpec((r, w), lambda k: (0, 2 * k + 1)),
                  pl.BlockSpec((None, r, w), lambda k: (k, 0, 0))],
        out_specs=pl.BlockSpec((None, r, w), lambda k: (k, 0, 0)),
        out_shape=jax.ShapeDtypeStruct(from_sibling.shape, from_sibling.dtype), compiler_params=_params(1),
    )(grad, grad, from_sibling)


def _adamw_update(g, w, m, v):
    new_m = ADAM_B1 * m + (1.0 - ADAM_B1) * g
    new_v = ADAM_B2 * v + (1.0 - ADAM_B2) * (g * g)
    m_hat = new_m / (1.0 - ADAM_B1 ** ADAM_STEP)
    v_hat = new_v / (1.0 - ADAM_B2 ** ADAM_STEP)
    return -ADAM_LR * (m_hat / (jnp.sqrt(v_hat) + ADAM_EPS) + ADAM_WD * w), new_m, new_v


def sum_parts(parts, name):
    _, r, c = parts.shape

    def body(p_ref, o_ref):
        total = p_ref[0].astype(F32)
        for j in range(1, N_DEV):
            total = total + p_ref[j].astype(F32)
        o_ref[...] = total

    return pl.pallas_call(body, name=name, in_specs=[_full(parts.shape)], out_specs=_full((r, c)), grid=(1,),
                          out_shape=jax.ShapeDtypeStruct((r, c), F32), compiler_params=_params(1))(parts)


def adamw(parts, w, m, v, name):
    n, r, c = parts.shape
    tr = r if r <= MM_TILE else ROW_TILE

    def body(p_ref, w_ref, m_ref, v_ref, g_ref, d_ref, nm_ref, nv_ref):
        g = p_ref[0].astype(F32)
        for j in range(1, n):
            g = g + p_ref[j].astype(F32)
        g_ref[...] = g
        d_ref[...], nm_ref[...], nv_ref[...] = _adamw_update(g, w_ref[...], m_ref[...], v_ref[...])

    row = pl.BlockSpec((tr, c), lambda i: (i, 0))
    return pl.pallas_call(
        body, name=name, grid=(r // tr,),
        in_specs=[pl.BlockSpec((n, tr, c), lambda i: (0, i, 0)), row, row, row], out_specs=[row] * 4,
        out_shape=[jax.ShapeDtypeStruct((r, c), F32)] * 4, compiler_params=_params(1),
    )(parts, w, m, v)


def _small_views(shape):
    if len(shape) == 2:
        return [((slice(None), slice(None)), (slice(None), slice(None)))]
    if len(shape) == 3:
        return [((slice(None), slice(None)), (0,))]
    rows = shape[2]
    return [((slice(k * rows, (k + 1) * rows), slice(None)), (0, k)) for k in range(shape[1])]


def adamw_small(landings, w, m, v):
    names = list(landings)
    n = len(names)
    shapes = [w[name].shape for name in names]

    def body(*refs):
        land, ws, ms, vs = refs[:n], refs[n:2 * n], refs[2 * n:3 * n], refs[3 * n:4 * n]
        outs = [refs[(4 + k) * n:(5 + k) * n] for k in range(4)]
        for k in range(n):
            total = land[k][0]
            for j in range(1, N_DEV):
                total = total + land[k][j]
            for rows, at in _small_views(shapes[k]):
                g = total[rows]
                outs[0][k][at] = g
                outs[1][k][at], outs[2][k][at], outs[3][k][at] = _adamw_update(g, ws[k][at], ms[k][at], vs[k][at])

    blocks = [_full(sh) for sh in shapes]
    outs = pl.pallas_call(
        body, name="adamw_small", grid=(1,),
        in_specs=[_full(landings[name].shape) for name in names] + blocks * 3, out_specs=blocks * 4,
        out_shape=[jax.ShapeDtypeStruct(sh, F32) for sh in shapes] * 4, compiler_params=_params(1),
    )(*[landings[name] for name in names], *[src[name] for src in (w, m, v) for name in names])
    return [dict(zip(names, outs[k * n:(k + 1) * n])) for k in range(4)]


def adamw_replicated(land_vec, land_gate_b, land_loss, names, w, m, v, gate_b):
    n = len(names)

    def body(*refs):
        vec_ref, gb_ref, loss_ref = refs[:3]
        ws, ms, vs = refs[3:3 + n], refs[3 + n:3 + 2 * n], refs[3 + 2 * n:3 + 3 * n]
        gw_ref, gm_ref, gv_ref = refs[3 + 3 * n:6 + 3 * n]
        outs = refs[6 + 3 * n:]
        vec, gb, loss = vec_ref[0], gb_ref[0], loss_ref[0]
        for j in range(1, N_DEV):
            vec, gb, loss = vec + vec_ref[j], gb + gb_ref[j], loss + loss_ref[j]
        for k in range(n):
            g = vec[k:k + 1, :]
            outs[k][...] = g
            outs[n + k][...], outs[2 * n + k][...], outs[3 * n + k][...] = _adamw_update(g, ws[k][...], ms[k][...], vs[k][...])
        outs[4 * n][...] = gb
        outs[4 * n + 1][...], outs[4 * n + 2][...], outs[4 * n + 3][...] = _adamw_update(gb, gw_ref[...], gm_ref[...], gv_ref[...])
        outs[4 * n + 4][...] = loss

    vec_block, gb_block = _full((1, D_MODEL)), _full(gate_b[0].shape)
    outs = pl.pallas_call(
        body, name="adamw_replicated", grid=(1,),
        in_specs=[_full(land_vec.shape), _full(land_gate_b.shape), _full(land_loss.shape)] + [vec_block] * (3 * n) + [gb_block] * 3,
        out_specs=[vec_block] * (4 * n) + [gb_block] * 4 + [_full(land_loss.shape[1:])],
        out_shape=[jax.ShapeDtypeStruct((1, D_MODEL), F32)] * (4 * n) + [jax.ShapeDtypeStruct(gate_b[0].shape, F32)] * 4
        + [jax.ShapeDtypeStruct(land_loss.shape[1:], F32)],
        compiler_params=_params(1),
    )(land_vec, land_gate_b, land_loss, *[src[name] for src in (w, m, v) for name in names], *gate_b)
    results = {name: [outs[k * n + i] for k in range(4)] for i, name in enumerate(names)}
    return results, outs[4 * n:4 * n + 4], outs[4 * n + 4]


SMALL_SHARDED = ("rg_conv_w", "rg_lambda", "sc_conv_w", "odd_norm_pre", "odd_norm_post", "gla_b_gate", "gla_norm_g", "gla_w_gate_lr")
SMALL_ROWS = {"rg_conv_w": (0, 4), "rg_lambda": (4, 2), "sc_conv_w": (6, 3), "odd_norm_pre": (9, 1), "odd_norm_post": (10, 1),
              "gla_b_gate": (11, 2), "gla_norm_g": (13, 1), "gla_w_gate_lr": (16, 32)}


def _pack_small(shards):
    pieces, at = [], 0
    for name in SMALL_SHARDED:
        start, rows = SMALL_ROWS[name]
        if start > at:
            pieces.append(jnp.zeros((start - at, LANES), F32))
        a = shards[name].reshape(rows, -1)
        pieces.append(jnp.pad(a, ((0, 0), (0, LANES - a.shape[1]))))
        at = start + rows
    return jnp.concatenate(pieces, axis=0)


def _unpack_gathered(g):
    def cols(name, width):
        start, rows = SMALL_ROWS[name]
        return jnp.transpose(g[:, start:start + rows, :width], (1, 0, 2)).reshape(rows, N_DEV * width)

    w_lr = cols("gla_w_gate_lr", GLA_KEY // N_DEV).reshape(2, GLA_RANK, GLA_KEY)
    return dict(rg_conv_w=cols("rg_conv_w", LANES), rg_lambda=cols("rg_lambda", LANES), sc_conv_w=cols("sc_conv_w", LANES),
                odd_norm_pre=cols("odd_norm_pre", LANES), odd_norm_post=cols("odd_norm_post", LANES),
                gla_b_gate=cols("gla_b_gate", GLA_KEY // N_DEV), gla_norm_g=cols("gla_norm_g", GLA_DV // N_DEV), gla_w_gate_lr=w_lr)


def _blocks_along_columns(a, rows):
    return jnp.transpose(a.reshape(rows, N_DEV, -1), (1, 0, 2))


def kernel(x, even_norm_pre, even_norm_post, even_w_in, rg_conv_w, rg_conv_b, rg_gate_w, rg_gate_b, rg_lambda, sc_conv_w, even_w_out, odd_norm_pre, odd_norm_post, odd_w_in, gla_w_gate_lr, gla_b_gate, gla_norm_g, odd_w_out, loss_target, m_even_norm_pre, m_even_norm_post, m_even_w_in, m_rg_conv_w, m_rg_conv_b, m_rg_gate_w, m_rg_gate_b, m_rg_lambda, m_sc_conv_w, m_even_w_out, m_odd_norm_pre, m_odd_norm_post, m_odd_w_in, m_gla_w_gate_lr, m_gla_b_gate, m_gla_norm_g, m_odd_w_out, v_even_norm_pre, v_even_norm_post, v_even_w_in, v_rg_conv_w, v_rg_conv_b, v_rg_gate_w, v_rg_gate_b, v_rg_lambda, v_sc_conv_w, v_even_w_out, v_odd_norm_pre, v_odd_norm_post, v_odd_w_in, v_gla_w_gate_lr, v_gla_b_gate, v_gla_norm_g, v_odd_w_out):
    weights = dict(even_norm_pre=even_norm_pre, even_norm_post=even_norm_post, even_w_in=even_w_in, rg_conv_w=rg_conv_w,
                   rg_conv_b=rg_conv_b, rg_gate_w=rg_gate_w, rg_gate_b=rg_gate_b, rg_lambda=rg_lambda, sc_conv_w=sc_conv_w,
                   even_w_out=even_w_out, odd_norm_pre=odd_norm_pre, odd_norm_post=odd_norm_post, odd_w_in=odd_w_in,
                   gla_w_gate_lr=gla_w_gate_lr, gla_b_gate=gla_b_gate, gla_norm_g=gla_norm_g, odd_w_out=odd_w_out)
    m_in = dict(even_norm_pre=m_even_norm_pre, even_norm_post=m_even_norm_post, even_w_in=m_even_w_in, rg_conv_w=m_rg_conv_w,
                rg_conv_b=m_rg_conv_b, rg_gate_w=m_rg_gate_w, rg_gate_b=m_rg_gate_b, rg_lambda=m_rg_lambda, sc_conv_w=m_sc_conv_w,
                even_w_out=m_even_w_out, odd_norm_pre=m_odd_norm_pre, odd_norm_post=m_odd_norm_post, odd_w_in=m_odd_w_in,
                gla_w_gate_lr=m_gla_w_gate_lr, gla_b_gate=m_gla_b_gate, gla_norm_g=m_gla_norm_g, odd_w_out=m_odd_w_out)
    v_in = dict(even_norm_pre=v_even_norm_pre, even_norm_post=v_even_norm_post, even_w_in=v_even_w_in, rg_conv_w=v_rg_conv_w,
                rg_conv_b=v_rg_conv_b, rg_gate_w=v_rg_gate_w, rg_gate_b=v_rg_gate_b, rg_lambda=v_rg_lambda, sc_conv_w=v_sc_conv_w,
                even_w_out=v_even_w_out, odd_norm_pre=v_odd_norm_pre, odd_norm_post=v_odd_norm_post, odd_w_in=v_odd_w_in,
                gla_w_gate_lr=v_gla_w_gate_lr, gla_b_gate=v_gla_b_gate, gla_norm_g=v_gla_norm_g, odd_w_out=v_odd_w_out)
    names = list(weights)
    shapes = {n: weights[n].shape for n in names}
    xs = x[0]
    tgt = loss_target[0]

    first = Exchange()
    first.gather(even_w_in[0].astype(BF16), columns=True, via_sibling=True)
    first.gather(_pack_small({n: weights[n][0] for n in SMALL_SHARDED}))
    w_in_e, small_all = run_exchange(first, "gather_first")
    small = _unpack_gathered(small_all)
    gate_w = rg_gate_w[0].reshape(4, RG_HEADS, RG_HEAD_DIM, RG_HEAD_DIM).astype(BF16)
    gate_b = rg_gate_b[0].reshape(4, RG_HEADS, RG_HEAD_DIM)
    conv_b = rg_conv_b
    wg_pad = [jnp.pad(small["gla_w_gate_lr"][d], ((GLA_RANK * d, LANES - GLA_RANK * (d + 1)), (0, 0))).astype(BF16) for d in range(2)]
    bg = [small["gla_b_gate"][d:d + 1] for d in range(2)]
    gnorm = jnp.tile(small["gla_norm_g"], (1, GLA_HEADS))

    half = D_MODEL // 2
    behind_in = Exchange()
    behind_in.gather(even_w_out[0].astype(BF16))
    (proj_e, h_e), (w_out_e,) = rms_matmul(xs, even_norm_pre, w_in_e, 2 * MM_TILE, 2 * EVEN_SHARD, "even_in", exchange=behind_in)
    w_out_e = w_out_e.reshape(2 * D_MODEL, D_MODEL)
    behind_gates = Exchange()
    behind_gates.gather(odd_w_in[0, :half].astype(BF16))
    ab, (w_in_o_top,) = even_gates_fwd(proj_e, small["rg_conv_w"], conv_b, gate_w, gate_b, small["rg_lambda"], exchange=behind_gates)
    behind_scan = Exchange()
    behind_scan.gather(odd_w_out[0].astype(BF16))
    hf, (w_out_o,) = linear_scan(ab, 0, ab, 1, False, False, "scan_fwd", exchange=behind_scan)
    w_out_o = w_out_o.reshape(D_MODEL, D_MODEL)
    hb = linear_scan(ab, 2, ab, 3, True, False, "scan_rev")
    behind_mix_fwd = Exchange()
    behind_mix_fwd.gather(odd_w_in[0, half:].astype(BF16))
    u_e, (w_in_o_bottom,) = even_mix_fwd(hf, hb, proj_e, small["sc_conv_w"], exchange=behind_mix_fwd)
    w_in_o = jnp.concatenate([jnp.transpose(part, (1, 0, 2)).reshape(half, ODD_IN) for part in (w_in_o_top, w_in_o_bottom)], axis=0)
    w_in_o = jnp.pad(w_in_o, ((0, 0), (0, ODD_IN_PAD - ODD_IN)))
    y_e, x1 = matmul_post(u_e, w_out_e, xs, even_norm_post, "even_out")

    proj_o, h_o = rms_matmul(x1, small["odd_norm_pre"], w_in_o, MM_TILE, ODD_IN_PAD, "odd_in")
    o_f, st_f = gla_fwd(proj_o, wg_pad[0], bg[0], False)
    osum, u_o, st_b = gla_fwd(proj_o, wg_pad[1], bg[1], True, o_other=o_f, gnorm=gnorm)
    y_o, dout, loss_part = matmul_post(u_o, w_out_o, x1, small["odd_norm_post"], "odd_out", target=tgt)

    du_o, dy_o, d_odd_norm_post = normbwd_matmul_nt(y_o, small["odd_norm_post"], dout, w_out_o, D_MODEL, "odd_out_bwd")
    d_w_out_o = matmul_tn(u_o, dy_o, D_MODEL, D_MODEL, 4 * MM_TILE, BF16, "odd_w_out_grad")
    do, dr, d_gnorm = gla_out_bwd(du_o, proj_o, osum, gnorm)
    dqkv_f, dlr_f, dwg_f, dbg_f = gla_bwd(proj_o, wg_pad[0], bg[0], do, st_f, False)
    dproj_o, dwg_b, dbg_b = gla_bwd(proj_o, wg_pad[1], bg[1], do, st_b, True, first=(dqkv_f, dlr_f, dr))
    dx1, d_odd_norm_pre = matmul_nt_normbwd(dproj_o, w_in_o, x1, small["odd_norm_pre"], dout, ODD_IN_PAD, "odd_in_bwd")
    d_w_in_o = matmul_tn(h_o, dproj_o, D_MODEL, ODD_IN_PAD // 5, 8 * MM_TILE, BF16, "odd_w_in_grad")

    landed = {}
    behind_out = Exchange()
    behind_out.scatter(d_w_out_o.reshape(N_DEV, D_MODEL // N_DEV, D_MODEL))
    behind_out.scatter(d_odd_norm_pre, columns=True)
    behind_out.scatter(d_odd_norm_post, columns=True)
    behind_out.scatter(_blocks_along_columns(jnp.concatenate([dbg_f, dbg_b], axis=0), 2))
    behind_out.scatter(_blocks_along_columns(d_gnorm, 1))
    behind_out.scatter(_blocks_along_columns(jnp.concatenate([dwg_f[:GLA_RANK], dwg_b[GLA_RANK:2 * GLA_RANK]], axis=0), 2 * GLA_RANK))
    (du_e, dy_e, d_even_norm_post), got = normbwd_matmul_nt(y_e, even_norm_post, dx1, w_out_e, 2 * D_MODEL, "even_out_bwd",
                                                           exchange=behind_out)
    p_w_out_o = got[0]
    for n, part in zip(("odd_norm_pre", "odd_norm_post", "gla_b_gate", "gla_norm_g", "gla_w_gate_lr"), got[1:]):
        landed[n] = part
    d_w_out_e = matmul_tn(u_e, dy_e, D_MODEL, D_MODEL, 4 * MM_TILE, BF16, "even_w_out_grad")
    behind_mix = Exchange()
    behind_mix.scatter(d_w_out_e.reshape(N_DEV, 2 * D_MODEL // N_DEV, D_MODEL))
    (dh, drest, d_sc_w), (p_w_out_e,) = even_mix_bwd(du_e, hf, hb, proj_e, small["sc_conv_w"], exchange=behind_mix)
    dh3 = dh.reshape(1, *dh.shape)
    adj_f = linear_scan(ab, 0, dh3, 0, True, True, "scan_fwd_adjoint")
    adj_b = linear_scan(ab, 2, dh3, 0, False, True, "scan_rev_adjoint")
    behind_gates_bwd = Exchange()
    behind_gates_bwd.scatter(jnp.transpose(d_w_in_o[:, :ODD_IN].reshape(D_MODEL, N_DEV, ODD_SHARD), (1, 0, 2)))
    behind_gates_bwd.scatter(d_sc_w, columns=True)
    (dua, d_gate_w, d_gate_b, d_lam), (p_w_in_o, landed["sc_conv_w"]) = even_gates_bwd(
        proj_e, adj_f, adj_b, hf, hb, dh, small["rg_conv_w"], conv_b, gate_w, gate_b, small["rg_lambda"], exchange=behind_gates_bwd)
    gate_w_rows = 4 * RG_HEADS * RG_HEAD_DIM
    behind_conv = Exchange()
    behind_conv.scatter(d_gate_w.reshape(N_DEV, gate_w_rows // N_DEV, RG_HEAD_DIM))
    behind_conv.scatter(d_lam, columns=True)
    (dproj_e, d_conv_w, d_conv_b), (p_gate_w, landed["rg_lambda"]) = rg_conv_bwd(dua, proj_e, drest, small["rg_conv_w"],
                                                                                 exchange=behind_conv)
    behind_w_grad = Exchange()
    behind_w_grad.gather(sum_parts(p_gate_w, "sum_gate_w"))
    d_w_in_e, (g_gate_w_all,) = matmul_tn(h_e, dproj_e, D_MODEL, EVEN_SHARD, 8 * MM_TILE, BF16, "even_w_in_grad",
                                          exchange=behind_w_grad)
    to_sibling = Exchange()
    to_sibling.to_sibling(d_w_in_e)
    to_sibling.scatter(d_conv_w, columns=True)
    from_sibling, landed["rg_conv_w"] = run_exchange(to_sibling, "scatter_to_sibling")
    behind_in_bwd = Exchange()
    behind_in_bwd.among_chips(pair_sum(d_w_in_e, from_sibling))
    (grad_x, d_even_norm_pre), (p_w_in_e,) = matmul_nt_normbwd(
        dproj_e, w_in_e, xs, even_norm_pre, dx1, 3 * D_MODEL, "even_in_bwd", exchange=behind_in_bwd)
    last = Exchange()
    replicated_vecs = ("even_norm_pre", "even_norm_post", "rg_conv_b")
    last.gather(jnp.concatenate([d_even_norm_pre, d_even_norm_post, d_conv_b], axis=0))
    last.gather(d_gate_b.reshape(4 * RG_HEADS, RG_HEAD_DIM))
    last.gather(loss_part)
    land_vec, land_gate_b, land_loss = run_exchange(last, "gather_last")

    results = {}

    def update(name, parts_, shape2d):
        outs = adamw(parts_, weights[name][0].reshape(shape2d), m_in[name][0].reshape(shape2d), v_in[name][0].reshape(shape2d),
                     "adamw_" + name)
        results[name] = [o.reshape(shapes[name]) for o in outs]

    update("even_w_in", p_w_in_e, (D_MODEL, EVEN_SHARD))
    update("even_w_out", p_w_out_e, (2 * D_MODEL // N_DEV, D_MODEL))
    update("odd_w_in", p_w_in_o, (D_MODEL, ODD_SHARD))
    update("odd_w_out", p_w_out_o, (D_MODEL // N_DEV, D_MODEL))
    update("rg_gate_w", g_gate_w_all.reshape(1, gate_w_rows, RG_HEAD_DIM), (gate_w_rows, RG_HEAD_DIM))
    small_out = adamw_small({n: landed[n] for n in SMALL_SHARDED}, weights, m_in, v_in)
    for n in SMALL_SHARDED:
        results[n] = [o[n] for o in small_out]
    gate_b_shape = (4 * RG_HEADS, RG_HEAD_DIM)
    rep_out, gate_b_out, loss_all = adamw_replicated(land_vec, land_gate_b, land_loss, replicated_vecs, weights, m_in, v_in,
                                                     [src["rg_gate_b"].reshape(gate_b_shape) for src in (weights, m_in, v_in)])
    results.update(rep_out)
    results["rg_gate_b"] = [o.reshape(shapes["rg_gate_b"]) for o in gate_b_out]

    return (loss_all[0, 0], grad_x.reshape(x.shape), *[results[n][0] for n in names], *[results[n][1] for n in names],
            *[results[n][2] for n in names], *[results[n][3] for n in names])
```

```python
import functools

import jax
import jax.numpy as jnp
from jax import lax
from jax.experimental import pallas as pl
from jax.experimental.pallas import tpu as pltpu

F32 = jnp.float32
BF16 = jnp.bfloat16

N_DEV = 8
D_MODEL = 1024
NORM_EPS = 1e-6
RG_HEADS = 8
RG_HEAD_DIM = 128
RG_C = 8.0
GLA_HEADS = 4
GLA_DK = 128
GLA_DV = 256
GLA_KEY = 512
GLA_RANK = 16
GLA_NORMALIZER = 16.0
GLA_CHUNK = 64
EVEN_IN = 6144
ODD_IN = 3104
ODD_IN_PAD = 3200
ODD_SHARD = ODD_IN // N_DEV
EVEN_SHARD = EVEN_IN // N_DEV
ADAM_LR = 0.001
ADAM_B1 = 0.9
ADAM_B2 = 0.999
ADAM_EPS = 1e-08
ADAM_WD = 0.01
ADAM_STEP = 10

SMALLEST_NORMAL = 1.1754944e-38
SUBLANES = 8
LANES = 128
VMEM_LIMIT_BYTES = 48 * 2 ** 20
ROW_TILE = 256
MM_TILE = 512
PACK_ROWS = 48
MESH_ID = pl.DeviceIdType.MESH


def _params(n_grid):
    return pltpu.CompilerParams(dimension_semantics=("arbitrary",) * n_grid, vmem_limit_bytes=VMEM_LIMIT_BYTES)


def _bdot(a, b):
    return jnp.dot(a.astype(BF16), b.astype(BF16), preferred_element_type=F32)


def _bdot_nt(a, b):
    return lax.dot_general(a.astype(BF16), b.astype(BF16), (((1,), (1,)), ((), ())), preferred_element_type=F32)


def _bdot_tn(a, b):
    return lax.dot_general(a.astype(BF16), b.astype(BF16), (((0,), (0,)), ((), ())), preferred_element_type=F32)


def _rstd(x):
    return lax.rsqrt(jnp.mean(x * x, axis=-1, keepdims=True) + NORM_EPS)


def _rms(x, g):
    return x * _rstd(x) * g


def _rms_bwd(x, g, dy):
    xh = x * _rstd(x)
    dyg = dy * g
    dx = _rstd(x) * (dyg - xh * jnp.mean(dyg * xh, axis=-1, keepdims=True))
    return dx, jnp.sum(dy * xh, axis=0, keepdims=True)


def _sigmoid(z):
    return 0.5 * jnp.tanh(0.5 * z) + 0.5


def _silu_and_grad(z):
    s = _sigmoid(z)
    return z * s, s * (1.0 + z * (1.0 - s))


def _softplus(z):
    return jnp.maximum(z, 0.0) + jnp.log(1.0 + jnp.exp(-jnp.abs(z)))


def _shift_rows(cur, before, after, d):
    ts = cur.shape[0]
    row = lax.broadcasted_iota(jnp.int32, (SUBLANES, cur.shape[1]), 0)
    out = pltpu.roll(cur, (-d) % ts, 0)
    if d < 0:
        edge = jnp.where(row < -d, pltpu.roll(before, (-d) % SUBLANES, 0), out[:SUBLANES])
        return jnp.concatenate([edge, out[SUBLANES:]], axis=0)
    edge = jnp.where(row >= SUBLANES - d, pltpu.roll(after, (-d) % SUBLANES, 0), out[ts - SUBLANES:])
    return jnp.concatenate([out[:ts - SUBLANES], edge], axis=0)


def _halo_specs(ts, s, width, col, tile=lambda i: i):
    per = ts // SUBLANES
    last = s // SUBLANES - 1
    return [
        pl.BlockSpec((ts, width), lambda i: (tile(i), col)),
        pl.BlockSpec((SUBLANES, width), lambda i: (jnp.maximum(tile(i) * per - 1, 0), col)),
        pl.BlockSpec((SUBLANES, width), lambda i: (jnp.minimum((tile(i) + 1) * per, last), col)),
    ]


def _halo_load(cur_ref, before_ref, after_ref, n_tiles, tile=lambda i: i):
    i = tile(pl.program_id(0))
    before = jnp.where(i > 0, before_ref[...], 0.0)
    after = jnp.where(i < n_tiles - 1, after_ref[...], 0.0)
    return cur_ref[...], before, after


def _full(shape):
    return pl.BlockSpec(shape, lambda *_: (0,) * len(shape))


def _peer(x, y, c, mask):
    px, py, pc = x ^ (mask >> 2), y ^ ((mask >> 1) & 1), c ^ (mask & 1)
    return (px, py, pc), 4 * px + 2 * py + pc


class Exchange:
    SIBLING = 1
    OTHER_CHIPS = (2, 4, 6)

    def __init__(self):
        self.args, self.out_shape, self._kinds = [], [], []

    def gather(self, block, columns=False, via_sibling=False):
        shape = (block.shape[0], N_DEV * block.shape[1]) if columns else (N_DEV,) + block.shape
        return self._add(block, shape, ("gather", columns, via_sibling))

    def scatter(self, stack, columns=False):
        shape = (N_DEV, stack.shape[0], stack.shape[1] // N_DEV) if columns else stack.shape
        return self._add(stack, shape, ("scatter", columns, False))

    def _add(self, arg, shape, kind):
        self.args.append(arg)
        self.out_shape.append(jax.ShapeDtypeStruct(shape, arg.dtype))
        self._kinds.append(kind)
        return len(self.args) - 1

    def semaphores(self):
        n = len(self.args)
        return [pltpu.SemaphoreType.DMA((n, N_DEV - 1)), pltpu.SemaphoreType.DMA((n, N_DEV - 1)), pltpu.SemaphoreType.DMA((n,))]

    def to_sibling(self, array):
        shape = (N_DEV // 2, array.shape[0], array.shape[1] // N_DEV)
        return self._add(array, shape, ("to_sibling", True, False))

    def among_chips(self, stack):
        return self._add(stack, stack.shape, ("among_chips", False, False))

    def _copies(self, position, in_refs, out_refs):
        x, y, c, me = position
        for arr, ((kind, columns, via_sibling), src, out) in enumerate(zip(self._kinds, in_refs, out_refs)):
            if kind == "to_sibling":
                width = src.shape[-1] // N_DEV
                for k in range(N_DEV // 2):
                    block = src.at[:, pl.ds(pl.multiple_of((2 * k + 1 - c) * width, LANES), width)]
                    yield arr, k + 1, block, out.at[k], out.at[k], False, self.SIBLING
                continue
            for mask in range(N_DEV):
                _, peer_id = _peer(x, y, c, mask)
                relayed = via_sibling and mask not in (0, self.SIBLING) + self.OTHER_CHIPS
                if kind == "among_chips":
                    if mask in (0,) + self.OTHER_CHIPS:
                        yield arr, mask, src.at[peer_id // 2], out.at[me // 2], out.at[peer_id // 2], False, mask
                elif kind == "gather":
                    if columns:
                        width = src.shape[-1]
                        yield (arr, mask, src, out.at[:, pl.ds(pl.multiple_of(me * width, LANES), width)],
                               out.at[:, pl.ds(pl.multiple_of(peer_id * width, LANES), width)], relayed, mask)
                    else:
                        yield arr, mask, src, out.at[me], out.at[peer_id], relayed, mask
                else:
                    if columns:
                        width = src.shape[-1] // N_DEV
                        block = src.at[:, pl.ds(pl.multiple_of(peer_id * width, LANES), width)]
                    else:
                        block = src.at[peer_id]
                    yield arr, mask, block, out.at[me], out.at[peer_id], False, mask

    def _remote(self, position, sems, arr, slot, to_mask, src, dst):
        x, y, c, _ = position
        return pltpu.make_async_remote_copy(src_ref=src, dst_ref=dst, send_sem=sems[0].at[arr, slot - 1], recv_sem=sems[1].at[arr, slot - 1],
                                            device_id=_peer(x, y, c, to_mask)[0], device_id_type=MESH_ID)

    def start(self, position, in_refs, out_refs, sems):
        for arr, slot, src, dst, _, relayed, to_mask in self._copies(position, in_refs, out_refs):
            if slot == 0:
                pltpu.make_async_copy(src, dst, sems[2].at[arr]).start()
            elif not relayed:
                self._remote(position, sems, arr, slot, to_mask, src, dst).start()

    def wait(self, position, in_refs, out_refs, sems):
        copies = list(self._copies(position, in_refs, out_refs))
        landings = {(arr, slot): landing for arr, slot, _, _, landing, _, _ in copies}
        passed_on = set()
        for arr, mask, src, _, landing, relayed, _ in copies:
            if relayed:
                held = landings[arr, mask ^ self.SIBLING]
                self._remote(position, sems, arr, mask ^ self.SIBLING, mask ^ self.SIBLING, src, held).wait_recv()
                self._remote(position, sems, arr, mask, self.SIBLING, held, held).start()
                passed_on.add((arr, mask ^ self.SIBLING))
        for arr, slot, src, dst, landing, relayed, to_mask in copies:
            if slot == 0:
                pltpu.make_async_copy(src, dst, sems[2].at[arr]).wait()
                continue
            if (arr, slot) not in passed_on:
                self._remote(position, sems, arr, slot, to_mask, src, landing).wait_recv()
            if relayed:
                held = landings[arr, slot ^ self.SIBLING]
                self._remote(position, sems, arr, slot, self.SIBLING, held, held).wait_send()
            else:
                self._remote(position, sems, arr, slot, to_mask, src, dst).wait_send()


def _call(body, *, name, grid, in_specs, out_specs, out_shape, args, scratch_shapes=(), exchange=None):
    single = not isinstance(out_shape, (list, tuple))
    if single:
        out_specs, out_shape = [out_specs], [out_shape]
    params = _params(len(grid))
    if exchange is None:
        outs = pl.pallas_call(body, name=name, grid=grid, in_specs=in_specs, out_specs=out_specs, out_shape=out_shape,
                              scratch_shapes=list(scratch_shapes), compiler_params=params)(*args)
        return outs[0] if single else outs
    counts = (len(args), len(exchange.args), len(out_shape), len(exchange.out_shape), len(scratch_shapes), 3)

    def wrapped(*refs):
        groups, at = [], 0
        for n in counts:
            groups.append(refs[at:at + n])
            at += n
        main_in, ex_in, main_out, ex_out, main_scratch, sems = groups
        x, y, c = lax.axis_index("x"), lax.axis_index("y"), lax.axis_index("c")
        position = (x, y, c, 4 * x + 2 * y + c)
        ids = [pl.program_id(a) for a in range(len(grid))]
        first = functools.reduce(jnp.logical_and, [i == 0 for i in ids])
        last = functools.reduce(jnp.logical_and, [i == g - 1 for i, g in zip(ids, grid)])

        @pl.when(first)
        def _():
            exchange.start(position, ex_in, ex_out, sems)

        body(*main_in, *main_out, *main_scratch)

        @pl.when(last)
        def _():
            exchange.wait(position, ex_in, ex_out, sems)

    hbm = pl.BlockSpec(memory_space=pl.ANY)
    outs = pl.pallas_call(
        wrapped, name=name, grid=grid, in_specs=list(in_specs) + [hbm] * counts[1], out_specs=list(out_specs) + [hbm] * counts[3],
        out_shape=list(out_shape) + exchange.out_shape, scratch_shapes=list(scratch_shapes) + exchange.semaphores(),
        compiler_params=params)(*args, *exchange.args)
    main = outs[:counts[2]]
    return (main[0] if single else main), outs[counts[2]:]


def run_exchange(exchange, name):
    return _call(lambda: None, name=name, grid=(1,), in_specs=[], out_specs=[], out_shape=[], args=[], exchange=exchange)[1]


def rms_matmul(x, g, w, tm, tn, name, exchange=None):
    s, d = x.shape
    n = w.shape[1]
    tm = min(tm, s)

    def body(x_ref, g_ref, w_ref, o_ref, h_ref):
        @pl.when(pl.program_id(1) == 0)
        def _():
            h_ref[...] = _rms(x_ref[...], g_ref[...]).astype(BF16)

        o_ref[...] = jnp.dot(h_ref[...], w_ref[...], preferred_element_type=F32)

    return _call(
        body, name=name, grid=(s // tm, n // tn),
        in_specs=[pl.BlockSpec((tm, d), lambda i, j: (i, 0)), _full((1, d)), pl.BlockSpec((d, tn), lambda i, j: (0, j))],
        out_specs=[pl.BlockSpec((tm, tn), lambda i, j: (i, j)), pl.BlockSpec((tm, d), lambda i, j: (i, 0))],
        out_shape=[jax.ShapeDtypeStruct((s, n), F32), jax.ShapeDtypeStruct((s, d), BF16)],
        args=[x, g, w], exchange=exchange)


def matmul_post(u, w, xres, g, name, target=None):
    s, k = u.shape
    d = w.shape[1]
    tm = min(MM_TILE, s)
    with_loss = target is not None

    def body(*refs):
        if with_loss:
            u_ref, w_ref, x_ref, g_ref, t_ref, y_ref, dout_ref, loss_ref = refs
        else:
            u_ref, w_ref, x_ref, g_ref, y_ref, out_ref = refs
        y = jnp.dot(u_ref[...], w_ref[...], preferred_element_type=F32)
        y_ref[...] = y
        out = x_ref[...] + _rms(y, g_ref[...])
        if with_loss:
            @pl.when(pl.program_id(0) == 0)
            def _():
                loss_ref[...] = jnp.zeros_like(loss_ref)

            diff = out - t_ref[...]
            dout_ref[...] = diff * (1.0 / d)
            loss_ref[...] += 0.5 * jnp.sum(jnp.mean(diff * diff, axis=-1, keepdims=True))
        else:
            out_ref[...] = out

    row = pl.BlockSpec((tm, d), lambda i: (i, 0))
    in_specs = [pl.BlockSpec((tm, k), lambda i: (i, 0)), _full((k, d)), row, _full((1, d))]
    args = [u, w, xres, g]
    out_specs = [row, row]
    out_shape = [jax.ShapeDtypeStruct((s, d), F32), jax.ShapeDtypeStruct((s, d), F32)]
    if with_loss:
        in_specs.append(row)
        args.append(target)
        out_specs.append(_full((SUBLANES, LANES)))
        out_shape.append(jax.ShapeDtypeStruct((SUBLANES, LANES), F32))
    return pl.pallas_call(body, name=name, grid=(s // tm,), in_specs=in_specs, out_specs=out_specs,
                          out_shape=out_shape, compiler_params=_params(1))(*args)


def normbwd_matmul_nt(y, g, dout, w, tn, name, exchange=None):
    s, d = y.shape
    n = w.shape[0]
    tm = min(MM_TILE, s)

    def body(y_ref, g_ref, dout_ref, w_ref, du_ref, dy_ref, dg_ref):
        i, j = pl.program_id(0), pl.program_id(1)

        @pl.when(j == 0)
        def _():
            dy, dg = _rms_bwd(y_ref[...], g_ref[...], dout_ref[...])
            dy_ref[...] = dy.astype(BF16)

            @pl.when(i == 0)
            def _():
                dg_ref[...] = jnp.zeros_like(dg_ref)

            dg_ref[...] += dg

        du_ref[...] = lax.dot_general(dy_ref[...], w_ref[...], (((1,), (1,)), ((), ())), preferred_element_type=F32)

    row = pl.BlockSpec((tm, d), lambda i, j: (i, 0))
    return _call(
        body, name=name, grid=(s // tm, n // tn),
        in_specs=[row, _full((1, d)), row, pl.BlockSpec((tn, d), lambda i, j: (j, 0))],
        out_specs=[pl.BlockSpec((tm, tn), lambda i, j: (i, j)), row, _full((1, d))],
        out_shape=[jax.ShapeDtypeStruct((s, n), F32), jax.ShapeDtypeStruct((s, d), BF16), jax.ShapeDtypeStruct((1, d), F32)],
        args=[y, g, dout, w], exchange=exchange)


def matmul_tn(a, b, tm, tn, ts, out_dtype, name, exchange=None):
    s, m = a.shape
    n = b.shape[1]
    ts = min(ts, s)
    n_k = s // ts

    def body(a_ref, b_ref, o_ref, acc):
        k = pl.program_id(2)

        @pl.when(k == 0)
        def _():
            acc[...] = jnp.zeros_like(acc)

        acc[...] += lax.dot_general(a_ref[...], b_ref[...], (((0,), (0,)), ((), ())), preferred_element_type=F32)

        @pl.when(k == n_k - 1)
        def _():
            o_ref[...] = acc[...].astype(out_dtype)

    return _call(
        body, name=name, grid=(m // tm, n // tn, n_k),
        in_specs=[pl.BlockSpec((ts, tm), lambda i, j, k: (k, i)), pl.BlockSpec((ts, tn), lambda i, j, k: (k, j))],
        out_specs=pl.BlockSpec((tm, tn), lambda i, j, k: (i, j)),
        out_shape=jax.ShapeDtypeStruct((m, n), out_dtype),
        scratch_shapes=[pltpu.VMEM((tm, tn), F32)], args=[a, b], exchange=exchange)


def matmul_nt_normbwd(dproj, w, x, g, dres, tk, name, exchange=None):
    s, kt = dproj.shape
    d = w.shape[0]
    tm = min(MM_TILE, s)
    n_k = kt // tk

    def body(a_ref, w_ref, x_ref, g_ref, r_ref, dx_ref, dg_ref, acc):
        i, k = pl.program_id(0), pl.program_id(1)

        @pl.when(k == 0)
        def _():
            acc[...] = jnp.zeros_like(acc)

        acc[...] += lax.dot_general(a_ref[...], w_ref[...], (((1,), (1,)), ((), ())), preferred_element_type=F32)

        @pl.when(k == n_k - 1)
        def _():
            dx, dg = _rms_bwd(x_ref[...], g_ref[...], acc[...])
            dx_ref[...] = r_ref[...] + dx

            @pl.when(i == 0)
            def _():
                dg_ref[...] = jnp.zeros_like(dg_ref)

            dg_ref[...] += dg

    row = pl.BlockSpec((tm, d), lambda i, k: (i, 0))
    return _call(
        body, name=name, grid=(s // tm, n_k),
        in_specs=[pl.BlockSpec((tm, tk), lambda i, k: (i, k)), pl.BlockSpec((d, tk), lambda i, k: (0, k)), row, _full((1, d)), row],
        out_specs=[row, _full((1, d))],
        out_shape=[jax.ShapeDtypeStruct((s, d), F32), jax.ShapeDtypeStruct((1, d), F32)],
        scratch_shapes=[pltpu.VMEM((tm, d), F32)], args=[dproj, w, x, g, dres], exchange=exchange)


def _rg_conv(xa, before, after, cw, cb):
    return (cw[0:1, :] * _shift_rows(xa, before, after, -2) + cw[1:2, :] * _shift_rows(xa, before, after, -1)
            + cw[2:3, :] * xa + cw[3:4, :] * _shift_rows(xa, before, after, 1) + cb)


def _rg_gates(ua_h, gw_ref, gb_ref, c_h, direction, head):
    r = _sigmoid(_bdot(ua_h, gw_ref[2 * direction, head]) + gb_ref[2 * direction, head:head + 1, :])
    i = _sigmoid(_bdot(ua_h, gw_ref[2 * direction + 1, head]) + gb_ref[2 * direction + 1, head:head + 1, :])
    log_a = -c_h * r
    a = jnp.exp(log_a)
    beta_sq = -jnp.tanh(log_a) * (1.0 + a * a)
    inv_beta = lax.rsqrt(jnp.maximum(beta_sq, SMALLEST_NORMAL))
    return r, i, a, beta_sq * inv_beta, inv_beta


def even_gates_fwd(proj, conv_w, conv_b, gate_w, gate_b, lam, exchange=None):
    s = proj.shape[0]
    ts = min(ROW_TILE, s)
    n_tiles = s // ts

    def body(xa_ref, xb_ref, xn_ref, cw_ref, cb_ref, gw_ref, gb_ref, lam_ref, o_ref, hf_ref, carry):
        @pl.when(pl.program_id(0) == 0)
        def _():
            carry[...] = jnp.zeros_like(carry)

        xa, before, after = _halo_load(xa_ref, xb_ref, xn_ref, n_tiles)
        ua = _rg_conv(xa, before, after, cw_ref[...], cb_ref[...])
        c = RG_C * _softplus(-lam_ref[...])
        for direction in range(2):
            for head in range(RG_HEADS):
                lanes = slice(head * RG_HEAD_DIM, (head + 1) * RG_HEAD_DIM)
                ua_h = ua[:, lanes]
                _, i, a, beta, _ = _rg_gates(ua_h, gw_ref, gb_ref, c[direction:direction + 1, lanes], direction, head)
                o_ref[2 * direction, :, lanes] = a
                o_ref[2 * direction + 1, :, lanes] = beta * (i * ua_h)
        _scan_tile(o_ref.at[0], o_ref.at[1], hf_ref, carry, False, False)

    return _call(
        body, name="even_gates_fwd", grid=(n_tiles,),
        in_specs=_halo_specs(ts, s, D_MODEL, 0) + [_full(conv_w.shape), _full(conv_b.shape), _full(gate_w.shape),
                                                   _full(gate_b.shape), _full(lam.shape)],
        out_specs=[pl.BlockSpec((4, ts, D_MODEL), lambda i: (0, i, 0)), pl.BlockSpec((ts, D_MODEL), lambda i: (i, 0))],
        out_shape=[jax.ShapeDtypeStruct((4, s, D_MODEL), F32), jax.ShapeDtypeStruct((s, D_MODEL), F32)],
        scratch_shapes=[pltpu.VMEM((SUBLANES, D_MODEL), F32)],
        args=[proj, proj, proj, conv_w, conv_b, gate_w, gate_b, lam], exchange=exchange)


def _scan_tile(a_ref, b_ref, h_ref, carry, reverse, b_times_a):
    ts, c = h_ref.shape
    n_blocks = ts // SUBLANES
    row = lax.broadcasted_iota(jnp.int32, (SUBLANES, c), 0)

    def block(j, h_in):
        r0 = pl.multiple_of((n_blocks - 1 - j if reverse else j) * SUBLANES, SUBLANES)
        a = a_ref[pl.ds(r0, SUBLANES), :]
        b = b_ref[pl.ds(r0, SUBLANES), :]
        if b_times_a:
            b = a * b
        for step in (1, 2, 4):
            shift = SUBLANES - step if reverse else step
            valid = row < SUBLANES - step if reverse else row >= step
            b = jnp.where(valid, a * pltpu.roll(b, shift, 0) + b, b)
            a = jnp.where(valid, a * pltpu.roll(a, shift, 0), a)
        h = a * h_in + b
        h_ref[pl.ds(r0, SUBLANES), :] = h
        return h[0:1, :] if reverse else h[SUBLANES - 1:SUBLANES, :]

    carry[0:1, :] = lax.fori_loop(0, n_blocks, block, carry[0:1, :])


def linear_scan(a_arr, a_idx, b_arr, b_idx, reverse, b_times_a, name, exchange=None):
    _, s, c = a_arr.shape
    ts = min(MM_TILE, s)
    n_tiles = s // ts

    def tile_of(i):
        return n_tiles - 1 - i if reverse else i

    def body(a_ref, b_ref, h_ref, carry):
        @pl.when(pl.program_id(0) == 0)
        def _():
            carry[...] = jnp.zeros_like(carry)

        _scan_tile(a_ref, b_ref, h_ref, carry, reverse, b_times_a)

    return _call(
        body, name=name, grid=(n_tiles,),
        in_specs=[pl.BlockSpec((None, ts, c), lambda i: (a_idx, tile_of(i), 0)),
                  pl.BlockSpec((None, ts, c), lambda i: (b_idx, tile_of(i), 0))],
        out_specs=pl.BlockSpec((ts, c), lambda i: (tile_of(i), 0)),
        out_shape=jax.ShapeDtypeStruct((s, c), F32),
        scratch_shapes=[pltpu.VMEM((SUBLANES, c), F32)], args=[a_arr, b_arr], exchange=exchange)


def _sc_conv(p, before, after, w):
    return w[0:1, :] * _shift_rows(p, before, after, -1) + w[1:2, :] * p + w[2:3, :] * _shift_rows(p, before, after, 1)


def even_mix_fwd(ab, hf, proj, sc_w, exchange=None):
    s = proj.shape[0]
    ts = min(ROW_TILE, s)
    n_tiles = s // ts

    def tile(i):
        return n_tiles - 1 - i

    row = pl.BlockSpec((ts, D_MODEL), lambda i: (tile(i), 0))

    def col(c):
        return pl.BlockSpec((ts, D_MODEL), lambda i: (tile(i), c))

    def body(a_ref, b_ref, hf_ref, za_ref, xb_ref, xbb_ref, xbn_ref, gb_ref, gc_ref, gcb_ref, gcn_ref, zb_ref, w_ref,
             u_ref, hb_ref, carry):
        @pl.when(pl.program_id(0) == 0)
        def _():
            carry[...] = jnp.zeros_like(carry)

        _scan_tile(a_ref, b_ref, hb_ref, carry, True, False)
        xb, xb_before, xb_after = _halo_load(xb_ref, xbb_ref, xbn_ref, n_tiles, tile)
        gc, gc_before, gc_after = _halo_load(gc_ref, gcb_ref, gcn_ref, n_tiles, tile)
        silu_za, _ = _silu_and_grad(za_ref[...])
        silu_zb, _ = _silu_and_grad(zb_ref[...])
        u_ref[:, :D_MODEL] = ((hf_ref[...] + hb_ref[...]) * silu_za).astype(BF16)
        cv = _sc_conv(gc * xb, gc_before * xb_before, gc_after * xb_after, w_ref[...])
        u_ref[:, D_MODEL:] = (gb_ref[...] * cv * silu_zb).astype(BF16)

    return _call(
        body, name="even_mix_fwd", grid=(n_tiles,),
        in_specs=[pl.BlockSpec((None, ts, D_MODEL), lambda i: (2, tile(i), 0)), pl.BlockSpec((None, ts, D_MODEL), lambda i: (3, tile(i), 0)),
                  row, col(1)] + _halo_specs(ts, s, D_MODEL, 2, tile) + [col(3)] + _halo_specs(ts, s, D_MODEL, 4, tile)
        + [col(5), _full(sc_w.shape)],
        out_specs=[pl.BlockSpec((ts, 2 * D_MODEL), lambda i: (tile(i), 0)), row],
        out_shape=[jax.ShapeDtypeStruct((s, 2 * D_MODEL), BF16), jax.ShapeDtypeStruct((s, D_MODEL), F32)],
        scratch_shapes=[pltpu.VMEM((SUBLANES, D_MODEL), F32)],
        args=[ab, ab, hf, proj, proj, proj, proj, proj, proj, proj, proj, proj, sc_w], exchange=exchange)


def even_mix_bwd(du, hf, hb, proj, sc_w, ab, exchange=None):
    s = proj.shape[0]
    ts = min(ROW_TILE, s)
    n_tiles = s // ts
    row = pl.BlockSpec((ts, D_MODEL), lambda i: (i, 0))

    def body(dya_ref, dyb_ref, dybb_ref, dybn_ref, hf_ref, hb_ref, za_ref, xb_ref, xbb_ref, xbn_ref,
             gb_ref, gbb_ref, gbn_ref, gc_ref, gcb_ref, gcn_ref, zb_ref, zbb_ref, zbn_ref, w_ref, a_ref,
             dh_ref, dp_ref, dw_ref, adj_ref, carry):
        @pl.when(pl.program_id(0) == 0)
        def _():
            carry[...] = jnp.zeros_like(carry)

        dyb, dyb_before, dyb_after = _halo_load(dyb_ref, dybb_ref, dybn_ref, n_tiles)
        xb, xb_before, xb_after = _halo_load(xb_ref, xbb_ref, xbn_ref, n_tiles)
        gb, gb_before, gb_after = _halo_load(gb_ref, gbb_ref, gbn_ref, n_tiles)
        gc, gc_before, gc_after = _halo_load(gc_ref, gcb_ref, gcn_ref, n_tiles)
        zb, zb_before, zb_after = _halo_load(zb_ref, zbb_ref, zbn_ref, n_tiles)
        w = w_ref[...]
        dya, za = dya_ref[...], za_ref[...]
        silu_za, dsilu_za = _silu_and_grad(za)
        dh_ref[...] = dya * silu_za
        _scan_tile(a_ref, dh_ref, adj_ref, carry, False, True)
        dp_ref[:, 0:D_MODEL] = (dya * (hf_ref[...] + hb_ref[...]) * dsilu_za).astype(BF16)

        silu_zb, dsilu_zb = _silu_and_grad(zb)
        p, p_before, p_after = gc * xb, gc_before * xb_before, gc_after * xb_after
        cv = _sc_conv(p, p_before, p_after, w)
        dcv = dyb * gb * silu_zb
        dcv_before = dyb_before * gb_before * _silu_and_grad(zb_before)[0]
        dcv_after = dyb_after * gb_after * _silu_and_grad(zb_after)[0]
        dpp = (w[0:1, :] * _shift_rows(dcv, dcv_before, dcv_after, 1) + w[1:2, :] * dcv
               + w[2:3, :] * _shift_rows(dcv, dcv_before, dcv_after, -1))
        dp_ref[:, D_MODEL:2 * D_MODEL] = (dpp * gc).astype(BF16)
        dp_ref[:, 2 * D_MODEL:3 * D_MODEL] = (dyb * cv * silu_zb).astype(BF16)
        dp_ref[:, 3 * D_MODEL:4 * D_MODEL] = (dpp * xb).astype(BF16)
        dp_ref[:, 4 * D_MODEL:5 * D_MODEL] = (dyb * gb * cv * dsilu_zb).astype(BF16)

        @pl.when(pl.program_id(0) == 0)
        def _():
            dw_ref[...] = jnp.zeros_like(dw_ref)

        dw_ref[0:1, :] += jnp.sum(dcv * _shift_rows(p, p_before, p_after, -1), axis=0, keepdims=True)
        dw_ref[1:2, :] += jnp.sum(dcv * p, axis=0, keepdims=True)
        dw_ref[2:3, :] += jnp.sum(dcv * _shift_rows(p, p_before, p_after, 1), axis=0, keepdims=True)

    return _call(
        body, name="even_mix_bwd", grid=(n_tiles,),
        in_specs=[row] + _halo_specs(ts, s, D_MODEL, 1) + [row, row, pl.BlockSpec((ts, D_MODEL), lambda i: (i, 1))]
        + _halo_specs(ts, s, D_MODEL, 2) + _halo_specs(ts, s, D_MODEL, 3) + _halo_specs(ts, s, D_MODEL, 4)
        + _halo_specs(ts, s, D_MODEL, 5) + [_full(sc_w.shape), pl.BlockSpec((None, ts, D_MODEL), lambda i: (2, i, 0))],
        out_specs=[row, pl.BlockSpec((ts, 5 * D_MODEL), lambda i: (i, 0)), _full(sc_w.shape), row],
        out_shape=[jax.ShapeDtypeStruct((s, D_MODEL), F32), jax.ShapeDtypeStruct((s, 5 * D_MODEL), BF16),
                   jax.ShapeDtypeStruct(sc_w.shape, F32), jax.ShapeDtypeStruct((s, D_MODEL), F32)],
        scratch_shapes=[pltpu.VMEM((SUBLANES, D_MODEL), F32)],
        args=[du, du, du, du, hf, hb, proj, *([proj] * 12), sc_w, ab], exchange=exchange)


def even_gates_bwd(proj, adj_f, adj_b, hf, hb, dh, conv_w, conv_b, gate_w, gate_b, lam, exchange=None):
    s = proj.shape[0]
    ts = min(ROW_TILE, s)
    n_tiles = s // ts
    row = pl.BlockSpec((ts, D_MODEL), lambda i: (i, 0))

    def body(xa_ref, xab_ref, xan_ref, af_ref, afb_ref, afn_ref, ab_ref, abb_ref, abn_ref,
             hf_ref, hfb_ref, hfn_ref, hb_ref, hbb_ref, hbn_ref, dh_ref,
             cw_ref, cb_ref, gw_ref, gb_ref, lam_ref, dua_ref, dgw_ref, dgb_ref, dlam_ref):
        @pl.when(pl.program_id(0) == 0)
        def _():
            dgw_ref[...] = jnp.zeros_like(dgw_ref)
            dgb_ref[...] = jnp.zeros_like(dgb_ref)
            dlam_ref[...] = jnp.zeros_like(dlam_ref)

        xa, before, after = _halo_load(xa_ref, xab_ref, xan_ref, n_tiles)
        ua = _rg_conv(xa, before, after, cw_ref[...], cb_ref[...])
        lam_v = lam_ref[...]
        c = RG_C * _softplus(-lam_v)
        dc_dlam = -RG_C * _sigmoid(-lam_v)
        dh = dh_ref[...]
        adj = (_halo_load(af_ref, afb_ref, afn_ref, n_tiles), _halo_load(ab_ref, abb_ref, abn_ref, n_tiles))
        hs = (_halo_load(hf_ref, hfb_ref, hfn_ref, n_tiles), _halo_load(hb_ref, hbb_ref, hbn_ref, n_tiles))
        dua = jnp.zeros_like(ua)
        for direction in range(2):
            step = 1 if direction == 0 else -1
            g = dh + _shift_rows(*adj[direction], step)
            da_all = g * _shift_rows(*hs[direction], -step)
            dua_parts = []
            for head in range(RG_HEADS):
                lanes = slice(head * RG_HEAD_DIM, (head + 1) * RG_HEAD_DIM)
                ua_h = ua[:, lanes]
                c_h = c[direction:direction + 1, lanes]
                r, i, a, beta, inv_beta = _rg_gates(ua_h, gw_ref, gb_ref, c_h, direction, head)
                db = g[:, lanes]
                d_i = db * beta * ua_h
                dbeta = db * (i * ua_h)
                dlog_a = (da_all[:, lanes] - dbeta * a * inv_beta) * a
                dpr = -c_h * dlog_a * r * (1.0 - r)
                dpi = d_i * i * (1.0 - i)
                dua_parts.append(db * beta * i + _bdot_nt(dpr, gw_ref[2 * direction, head])
                                 + _bdot_nt(dpi, gw_ref[2 * direction + 1, head]))
                dgw_ref[2 * direction, head] += _bdot_tn(ua_h, dpr)
                dgw_ref[2 * direction + 1, head] += _bdot_tn(ua_h, dpi)
                dgb_ref[2 * direction, head:head + 1, :] += jnp.sum(dpr, axis=0, keepdims=True)
                dgb_ref[2 * direction + 1, head:head + 1, :] += jnp.sum(dpi, axis=0, keepdims=True)
                dlam_ref[direction:direction + 1, lanes] += (
                    jnp.sum(-r * dlog_a, axis=0, keepdims=True) * dc_dlam[direction:direction + 1, lanes])
            dua = dua + jnp.concatenate(dua_parts, axis=1)
        dua_ref[...] = dua

    return _call(
        body, name="even_gates_bwd", grid=(n_tiles,),
        in_specs=_halo_specs(ts, s, D_MODEL, 0) * 5 + [row] + [_full(conv_w.shape), _full(conv_b.shape), _full(gate_w.shape),
                                                             _full(gate_b.shape), _full(lam.shape)],
        out_specs=[row, _full(gate_w.shape), _full(gate_b.shape), _full(lam.shape)],
        out_shape=[jax.ShapeDtypeStruct((s, D_MODEL), F32), jax.ShapeDtypeStruct(gate_w.shape, F32),
                   jax.ShapeDtypeStruct(gate_b.shape, F32), jax.ShapeDtypeStruct(lam.shape, F32)],
        args=[proj, proj, proj, adj_f, adj_f, adj_f, adj_b, adj_b, adj_b, hf, hf, hf, hb, hb, hb, dh, conv_w, conv_b, gate_w,
              gate_b, lam], exchange=exchange)


def rg_conv_bwd(dua, proj, drest, conv_w, exchange=None):
    s = proj.shape[0]
    ts = min(ROW_TILE, s)
    n_tiles = s // ts

    def body(du_ref, dub_ref, dun_ref, xa_ref, xab_ref, xan_ref, dr_ref, cw_ref, dp_ref, dw_ref, db_ref):
        @pl.when(pl.program_id(0) == 0)
        def _():
            dw_ref[...] = jnp.zeros_like(dw_ref)
            db_ref[...] = jnp.zeros_like(db_ref)

        dua, dua_before, dua_after = _halo_load(du_ref, dub_ref, dun_ref, n_tiles)
        xa, xa_before, xa_after = _halo_load(xa_ref, xab_ref, xan_ref, n_tiles)
        cw = cw_ref[...]
        dxa = (cw[0:1, :] * _shift_rows(dua, dua_before, dua_after, 2) + cw[1:2, :] * _shift_rows(dua, dua_before, dua_after, 1)
               + cw[2:3, :] * dua + cw[3:4, :] * _shift_rows(dua, dua_before, dua_after, -1))
        dp_ref[:, :D_MODEL] = dxa.astype(BF16)
        dp_ref[:, D_MODEL:] = dr_ref[...]
        for tap, offset in enumerate((-2, -1, 0, 1)):
            shifted = xa if offset == 0 else _shift_rows(xa, xa_before, xa_after, offset)
            dw_ref[tap:tap + 1, :] += jnp.sum(dua * shifted, axis=0, keepdims=True)
        db_ref[...] += jnp.sum(dua, axis=0, keepdims=True)

    return _call(
        body, name="rg_conv_bwd", grid=(n_tiles,),
        in_specs=_halo_specs(ts, s, D_MODEL, 0) * 2 + [pl.BlockSpec((ts, 5 * D_MODEL), lambda i: (i, 0)), _full(conv_w.shape)],
        out_specs=[pl.BlockSpec((ts, EVEN_IN), lambda i: (i, 0)), _full(conv_w.shape), _full((1, D_MODEL))],
        out_shape=[jax.ShapeDtypeStruct((s, EVEN_IN), BF16), jax.ShapeDtypeStruct(conv_w.shape, F32),
                   jax.ShapeDtypeStruct((1, D_MODEL), F32)],
        args=[dua, dua, dua, proj, proj, proj, drest, conv_w], exchange=exchange)


def _split3(x):
    x1 = x.astype(BF16)
    rest = x - x1.astype(F32)
    x2 = rest.astype(BF16)
    return x1, x2, (rest - x2.astype(F32)).astype(BF16)


def _chunk_sum_matrix(t, reverse, transpose):
    i = lax.broadcasted_iota(jnp.int32, (t, t), 0)
    j = lax.broadcasted_iota(jnp.int32, (t, t), 1)
    if transpose:
        i, j = j, i
    same = (i // GLA_CHUNK) == (j // GLA_CHUNK)
    return jnp.where(same & ((j >= i) if reverse else (j <= i)), 1.0, 0.0).astype(BF16)


def _exact_dot(m, x):
    return sum(jnp.dot(m, part, preferred_element_type=F32) for part in _split3(x))


def _chunk_mask(t, reverse):
    i = lax.broadcasted_iota(jnp.int32, (t, t), 0)
    j = lax.broadcasted_iota(jnp.int32, (t, t), 1)
    return ((i // GLA_CHUNK) == (j // GLA_CHUNK)) & ((j >= i) if reverse else (j <= i))


def _chunk_rows(c):
    return slice(c * GLA_CHUNK, (c + 1) * GLA_CHUNK)


def _gla_gate(lr, wg, bg):
    z = _bdot(lr, wg) + bg
    log_alpha = (jnp.minimum(z, 0.0) - jnp.log(1.0 + jnp.exp(-jnp.abs(z)))) * (1.0 / GLA_NORMALIZER)
    return z, log_alpha


def _gla_tile_terms(q, k, bcum, reverse):
    n_chunks = q.shape[0] // GLA_CHUNK
    totals = []
    for c in range(n_chunks):
        edge = c * GLA_CHUNK if reverse else (c + 1) * GLA_CHUNK - 1
        totals.append(bcum[edge:edge + 1, :])
    btot = jnp.concatenate([jnp.broadcast_to(total, (GLA_CHUNK, total.shape[1])) for total in totals], axis=0)
    e_pos, e_neg, e_st = jnp.exp(bcum), jnp.exp(-bcum), jnp.exp(btot - bcum)
    return q * (GLA_DK ** -0.5) * e_pos, k * e_neg, k * e_st, e_pos, e_neg, e_st, [jnp.exp(total) for total in totals]


def _gla_specs(t, n_tiles, reverse_order):
    def tile(i):
        return n_tiles - 1 - i if reverse_order else i

    return tile, [
        pl.BlockSpec((t, GLA_KEY), lambda i: (tile(i), 0)),
        pl.BlockSpec((t, GLA_KEY), lambda i: (tile(i), 1)),
        pl.BlockSpec((t, D_MODEL), lambda i: (tile(i), 1)),
        pl.BlockSpec((t, LANES), lambda i: (tile(i), (ODD_IN_PAD - LANES) // LANES)),
    ]


def gla_fwd(proj, wg, bg, reverse, o_other=None, gnorm=None):
    s = proj.shape[0]
    t = min(ROW_TILE, s)
    n_tiles = s // t
    n_chunks = t // GLA_CHUNK
    final = o_other is not None
    tile, specs = _gla_specs(t, n_tiles, reverse)

    def body(*refs):
        if final:
            q_ref, k_ref, v_ref, lr_ref, wg_ref, bg_ref, oo_ref, r_ref, gn_ref, osum_ref, u_ref, st_ref, state = refs
        else:
            q_ref, k_ref, v_ref, lr_ref, wg_ref, bg_ref, o_ref, st_ref, state = refs
            osum_ref = o_ref

        @pl.when(pl.program_id(0) == 0)
        def _():
            state[...] = jnp.zeros_like(state)

        _, log_alpha = _gla_gate(lr_ref[...], wg_ref[...], bg_ref[...])
        bcum = _exact_dot(_chunk_sum_matrix(t, reverse, False), log_alpha)
        q, k, v = q_ref[...], k_ref[...], v_ref[...]
        q_in, k_in, k_st, _, _, _, decays = _gla_tile_terms(q, k, bcum, reverse)
        mask = _chunk_mask(t, reverse)
        order = list(range(n_chunks))[::-1] if reverse else list(range(n_chunks))
        intra, increments = [], []
        for head in range(GLA_HEADS):
            kl = slice(head * GLA_DK, (head + 1) * GLA_DK)
            vl = slice(head * GLA_DV, (head + 1) * GLA_DV)
            scores = jnp.where(mask, _bdot_nt(q_in[:, kl], k_in[:, kl]), 0.0)
            intra.append(_bdot(scores, v[:, vl]))
            increments.append([_bdot_tn(v[_chunk_rows(c), vl], k_st[_chunk_rows(c), kl]) for c in range(n_chunks)])
        for head in range(GLA_HEADS):
            kl = slice(head * GLA_DK, (head + 1) * GLA_DK)
            vl = slice(head * GLA_DV, (head + 1) * GLA_DV)
            running = state[head]
            before = [None] * n_chunks
            for c in order:
                before[c] = running
                st_ref[c, head] = running
                running = running * decays[c][:, kl] + increments[head][c]
            state[head] = running
            inter = [_bdot_nt(q_in[_chunk_rows(c), kl], before[c]) for c in range(n_chunks)]
            osum_ref[:, vl] = intra[head] + jnp.concatenate(inter, axis=0)
        if final:
            osum = osum_ref[...] + oo_ref[...]
            osum_ref[...] = osum
            silu_r, _ = _silu_and_grad(r_ref[...])
            gn = gn_ref[...]
            for head in range(GLA_HEADS):
                vl = slice(head * GLA_DV, (head + 1) * GLA_DV)
                u_ref[:, vl] = (_rms(osum[:, vl], gn[:, vl]) * silu_r[:, vl]).astype(BF16)

    row = pl.BlockSpec((t, D_MODEL), lambda i: (tile(i), 0))
    st_spec = pl.BlockSpec((n_chunks, GLA_HEADS, GLA_DV, GLA_DK), lambda i: (tile(i), 0, 0, 0))
    st_shape = jax.ShapeDtypeStruct((s // GLA_CHUNK, GLA_HEADS, GLA_DV, GLA_DK), F32)
    in_specs = specs + [_full(wg.shape), _full(bg.shape)]
    args = [proj, proj, proj, proj, wg, bg]
    if final:
        in_specs += [row, pl.BlockSpec((t, D_MODEL), lambda i: (tile(i), 2)), _full(gnorm.shape)]
        args += [o_other, proj, gnorm]
        out_specs = [row, row, st_spec]
        out_shape = [jax.ShapeDtypeStruct((s, D_MODEL), F32), jax.ShapeDtypeStruct((s, D_MODEL), BF16), st_shape]
    else:
        out_specs = [row, st_spec]
        out_shape = [jax.ShapeDtypeStruct((s, D_MODEL), F32), st_shape]
    return pl.pallas_call(
        body, name="gla_fwd_rev" if reverse else "gla_fwd", grid=(n_tiles,), in_specs=in_specs, out_specs=out_specs,
        out_shape=out_shape, scratch_shapes=[pltpu.VMEM((GLA_HEADS, GLA_DV, GLA_DK), F32)], compiler_params=_params(1),
    )(*args)


def gla_out_bwd(du, proj, osum, gnorm):
    s = proj.shape[0]
    ts = min(ROW_TILE, s)
    row = pl.BlockSpec((ts, D_MODEL), lambda i: (i, 0))

    def body(du_ref, r_ref, o_ref, gn_ref, do_ref, dr_ref, dgn_ref):
        @pl.when(pl.program_id(0) == 0)
        def _():
            dgn_ref[...] = jnp.zeros_like(dgn_ref)

        du, osum, gn = du_ref[...], o_ref[...], gn_ref[...]
        silu_r, dsilu_r = _silu_and_grad(r_ref[...])
        for head in range(GLA_HEADS):
            vl = slice(head * GLA_DV, (head + 1) * GLA_DV)
            o_h, g_h, du_h = osum[:, vl], gn[:, vl], du[:, vl]
            dr_ref[:, vl] = (du_h * _rms(o_h, g_h) * dsilu_r[:, vl]).astype(BF16)
            do_h, dg_h = _rms_bwd(o_h, g_h, du_h * silu_r[:, vl])
            do_ref[:, vl] = do_h
            dgn_ref[...] += dg_h

    return pl.pallas_call(
        body, name="gla_out_bwd", grid=(s // ts,),
        in_specs=[row, pl.BlockSpec((ts, D_MODEL), lambda i: (i, 2)), row, _full(gnorm.shape)],
        out_specs=[row, row, _full((1, GLA_DV))],
        out_shape=[jax.ShapeDtypeStruct((s, D_MODEL), F32), jax.ShapeDtypeStruct((s, D_MODEL), BF16),
                   jax.ShapeDtypeStruct((1, GLA_DV), F32)],
        compiler_params=_params(1),
    )(du, proj, osum, gnorm)


def gla_bwd(proj, wg, bg, do, states, reverse, first=None):
    s = proj.shape[0]
    t = min(ROW_TILE, s)
    n_tiles = s // t
    n_chunks = t // GLA_CHUNK
    final = first is not None
    tile, specs = _gla_specs(t, n_tiles, not reverse)

    def body(*refs):
        if final:
            (q_ref, k_ref, v_ref, lr_ref, wg_ref, bg_ref, do_ref, st_ref, dqkv1_ref, dlr1_ref, dr_ref,
             dp_ref, dwg_ref, dbg_ref, dstate, dqkv, dbc, dbt) = refs
        else:
            (q_ref, k_ref, v_ref, lr_ref, wg_ref, bg_ref, do_ref, st_ref,
             dqkv, dlr_ref, dwg_ref, dbg_ref, dstate, dbc, dbt) = refs

        @pl.when(pl.program_id(0) == 0)
        def _():
            dstate[...] = jnp.zeros_like(dstate)
            dwg_ref[...] = jnp.zeros_like(dwg_ref)
            dbg_ref[...] = jnp.zeros_like(dbg_ref)

        lr, wg_v = lr_ref[...], wg_ref[...]
        z, log_alpha = _gla_gate(lr, wg_v, bg_ref[...])
        bcum = _exact_dot(_chunk_sum_matrix(t, reverse, False), log_alpha)
        q, k, v, do_v = q_ref[...], k_ref[...], v_ref[...], do_ref[...]
        q_in, k_in, k_st, e_pos, e_neg, e_st, decays = _gla_tile_terms(q, k, bcum, reverse)
        mask = _chunk_mask(t, reverse)
        order = list(range(n_chunks)) if reverse else list(range(n_chunks))[::-1]
        dq_intra, dk_intra, dv_intra, increments = [], [], [], []
        for head in range(GLA_HEADS):
            kl = slice(head * GLA_DK, (head + 1) * GLA_DK)
            vl = slice(head * GLA_DV, (head + 1) * GLA_DV)
            scores = jnp.where(mask, _bdot_nt(q_in[:, kl], k_in[:, kl]), 0.0)
            dscores = jnp.where(mask, _bdot_nt(do_v[:, vl], v[:, vl]), 0.0)
            dv_intra.append(_bdot_tn(scores, do_v[:, vl]))
            dq_intra.append(_bdot(dscores, k_in[:, kl]))
            dk_intra.append(_bdot_tn(dscores, q_in[:, kl]))
            increments.append([_bdot_tn(do_v[_chunk_rows(c), vl], q_in[_chunk_rows(c), kl]) for c in range(n_chunks)])
        for head in range(GLA_HEADS):
            kl = slice(head * GLA_DK, (head + 1) * GLA_DK)
            vl = slice(head * GLA_DV, (head + 1) * GLA_DV)
            running = dstate[head]
            after, ddecay = [None] * n_chunks, [None] * n_chunks
            for c in order:
                after[c] = running
                ddecay[c] = jnp.sum(running * st_ref[c, head], axis=0, keepdims=True)
                running = running * decays[c][:, kl] + increments[head][c]
            dstate[head] = running
            dq_inter = jnp.concatenate([_bdot(do_v[_chunk_rows(c), vl], st_ref[c, head]) for c in range(n_chunks)], axis=0)
            dv_inter = jnp.concatenate([_bdot_nt(k_st[_chunk_rows(c), kl], after[c]) for c in range(n_chunks)], axis=0)
            dk_st = jnp.concatenate([_bdot(v[_chunk_rows(c), vl], after[c]) for c in range(n_chunks)], axis=0)
            dq_in = dq_intra[head] + dq_inter
            ks_h = k_st[:, kl]
            dqkv[:, 2 * GLA_KEY + head * GLA_DV:2 * GLA_KEY + (head + 1) * GLA_DV] = dv_intra[head] + dv_inter
            dqkv[:, kl] = dq_in * (GLA_DK ** -0.5) * e_pos[:, kl]
            dqkv[:, GLA_KEY + head * GLA_DK:GLA_KEY + (head + 1) * GLA_DK] = dk_intra[head] * e_neg[:, kl] + dk_st * e_st[:, kl]
            dbc[:, kl] = dq_in * q_in[:, kl] - dk_intra[head] * k_in[:, kl] - dk_st * ks_h
            weighted = dk_st * ks_h
            for c in range(n_chunks):
                dbtot = jnp.sum(weighted[_chunk_rows(c)], axis=0, keepdims=True) + ddecay[c] * decays[c][:, kl]
                dbt[_chunk_rows(c), kl] = jnp.broadcast_to(dbtot, (GLA_CHUNK, GLA_DK))
        dlog_alpha = _exact_dot(_chunk_sum_matrix(t, reverse, True), dbc[...]) + dbt[...]
        dz = dlog_alpha * _sigmoid(-z) * (1.0 / GLA_NORMALIZER)
        dlr = _bdot_nt(dz, wg_v)
        dwg_ref[...] += _bdot_tn(lr, dz)
        dbg_ref[...] += jnp.sum(dz, axis=0, keepdims=True)
        if final:
            dp_ref[:, :2 * D_MODEL] = (dqkv[...] + dqkv1_ref[...]).astype(BF16)
            dp_ref[:, 2 * D_MODEL:3 * D_MODEL] = dr_ref[...]
            dp_ref[:, 3 * D_MODEL:] = (dlr + dlr1_ref[...]).astype(BF16)
        else:
            dlr_ref[...] = dlr

    row = pl.BlockSpec((t, D_MODEL), lambda i: (tile(i), 0))
    wide = pl.BlockSpec((t, 2 * D_MODEL), lambda i: (tile(i), 0))
    narrow = pl.BlockSpec((t, LANES), lambda i: (tile(i), 0))
    st_spec = pl.BlockSpec((n_chunks, GLA_HEADS, GLA_DV, GLA_DK), lambda i: (tile(i), 0, 0, 0))
    in_specs = specs + [_full(wg.shape), _full(bg.shape), row, st_spec]
    args = [proj, proj, proj, proj, wg, bg, do, states]
    acc_specs = [_full(wg.shape), _full(bg.shape)]
    acc_shapes = [jax.ShapeDtypeStruct(wg.shape, F32), jax.ShapeDtypeStruct(bg.shape, F32)]
    scratch = [pltpu.VMEM((GLA_HEADS, GLA_DV, GLA_DK), F32)]
    work = [pltpu.VMEM((t, GLA_KEY), F32), pltpu.VMEM((t, GLA_KEY), F32)]
    if final:
        in_specs += [wide, narrow, row]
        args += list(first)
        out_specs = [pl.BlockSpec((t, ODD_IN_PAD), lambda i: (tile(i), 0))] + acc_specs
        out_shape = [jax.ShapeDtypeStruct((s, ODD_IN_PAD), BF16)] + acc_shapes
        scratch += [pltpu.VMEM((t, 2 * D_MODEL), F32)] + work
    else:
        out_specs = [wide, narrow] + acc_specs
        out_shape = [jax.ShapeDtypeStruct((s, 2 * D_MODEL), F32), jax.ShapeDtypeStruct((s, LANES), F32)] + acc_shapes
        scratch += work
    return pl.pallas_call(
        body, name="gla_bwd_rev" if reverse else "gla_bwd", grid=(n_tiles,), in_specs=in_specs, out_specs=out_specs,
        out_shape=out_shape, scratch_shapes=scratch, compiler_params=_params(1),
    )(*args)


def pair_sum(grad, from_sibling):
    n_chips, r, w = from_sibling.shape

    def body(even_ref, odd_ref, sib_ref, o_ref):
        mine = jnp.where(lax.axis_index("c") == 1, odd_ref[...], even_ref[...])
        o_ref[...] = (mine.astype(F32) + sib_ref[...].astype(F32)).astype(o_ref.dtype)

    return pl.pallas_call(
        body, name="pair_sum", grid=(n_chips,),
        in_specs=[pl.BlockSpec((r, w), lambda k: (0, 2 * k)), pl.BlockSpec((r, w), lambda k: (0, 2 * k + 1)),
                  pl.BlockSpec((None, r, w), lambda k: (k, 0, 0))],
        out_specs=pl.BlockSpec((None, r, w), lambda k: (k, 0, 0)),
        out_shape=jax.ShapeDtypeStruct(from_sibling.shape, from_sibling.dtype), compiler_params=_params(1),
    )(grad, grad, from_sibling)


def _adamw_update(g, w, m, v):
    new_m = ADAM_B1 * m + (1.0 - ADAM_B1) * g
    new_v = ADAM_B2 * v + (1.0 - ADAM_B2) * (g * g)
    m_hat = new_m / (1.0 - ADAM_B1 ** ADAM_STEP)
    v_hat = new_v / (1.0 - ADAM_B2 ** ADAM_STEP)
    return -ADAM_LR * (m_hat / (jnp.sqrt(v_hat) + ADAM_EPS) + ADAM_WD * w), new_m, new_v


def sum_parts(parts, name):
    _, r, c = parts.shape

    def body(p_ref, o_ref):
        total = p_ref[0].astype(F32)
        for j in range(1, N_DEV):
            total = total + p_ref[j].astype(F32)
        o_ref[...] = total

    return pl.pallas_call(body, name=name, in_specs=[_full(parts.shape)], out_specs=_full((r, c)), grid=(1,),
                          out_shape=jax.ShapeDtypeStruct((r, c), F32), compiler_params=_params(1))(parts)


def adamw(parts, w, m, v, name):
    n, r, c = parts.shape
    tr = r if r <= MM_TILE else ROW_TILE

    def body(p_ref, w_ref, m_ref, v_ref, g_ref, d_ref, nm_ref, nv_ref):
        g = p_ref[0].astype(F32)
        for j in range(1, n):
            g = g + p_ref[j].astype(F32)
        g_ref[...] = g
        d_ref[...], nm_ref[...], nv_ref[...] = _adamw_update(g, w_ref[...], m_ref[...], v_ref[...])

    row = pl.BlockSpec((tr, c), lambda i: (i, 0))
    return pl.pallas_call(
        body, name=name, grid=(r // tr,),
        in_specs=[pl.BlockSpec((n, tr, c), lambda i: (0, i, 0)), row, row, row], out_specs=[row] * 4,
        out_shape=[jax.ShapeDtypeStruct((r, c), F32)] * 4, compiler_params=_params(1),
    )(parts, w, m, v)


def _small_views(shape):
    if len(shape) == 2:
        return [((slice(None), slice(None)), (slice(None), slice(None)))]
    if len(shape) == 3:
        return [((slice(None), slice(None)), (0,))]
    rows = shape[2]
    return [((slice(k * rows, (k + 1) * rows), slice(None)), (0, k)) for k in range(shape[1])]


def adamw_small(landings, w, m, v):
    names = list(landings)
    n = len(names)
    shapes = [w[name].shape for name in names]

    def body(*refs):
        land, ws, ms, vs = refs[:n], refs[n:2 * n], refs[2 * n:3 * n], refs[3 * n:4 * n]
        outs = [refs[(4 + k) * n:(5 + k) * n] for k in range(4)]
        for k in range(n):
            total = land[k][0]
            for j in range(1, N_DEV):
                total = total + land[k][j]
            for rows, at in _small_views(shapes[k]):
                g = total[rows]
                outs[0][k][at] = g
                outs[1][k][at], outs[2][k][at], outs[3][k][at] = _adamw_update(g, ws[k][at], ms[k][at], vs[k][at])

    blocks = [_full(sh) for sh in shapes]
    outs = pl.pallas_call(
        body, name="adamw_small", grid=(1,),
        in_specs=[_full(landings[name].shape) for name in names] + blocks * 3, out_specs=blocks * 4,
        out_shape=[jax.ShapeDtypeStruct(sh, F32) for sh in shapes] * 4, compiler_params=_params(1),
    )(*[landings[name] for name in names], *[src[name] for src in (w, m, v) for name in names])
    return [dict(zip(names, outs[k * n:(k + 1) * n])) for k in range(4)]


def adamw_replicated(land_vec, land_gate_b, land_loss, names, w, m, v, gate_b):
    n = len(names)

    def body(*refs):
        vec_ref, gb_ref, loss_ref = refs[:3]
        ws, ms, vs = refs[3:3 + n], refs[3 + n:3 + 2 * n], refs[3 + 2 * n:3 + 3 * n]
        gw_ref, gm_ref, gv_ref = refs[3 + 3 * n:6 + 3 * n]
        outs = refs[6 + 3 * n:]
        vec, gb, loss = vec_ref[0], gb_ref[0], loss_ref[0]
        for j in range(1, N_DEV):
            vec, gb, loss = vec + vec_ref[j], gb + gb_ref[j], loss + loss_ref[j]
        for k in range(n):
            g = vec[k:k + 1, :]
            outs[k][...] = g
            outs[n + k][...], outs[2 * n + k][...], outs[3 * n + k][...] = _adamw_update(g, ws[k][...], ms[k][...], vs[k][...])
        outs[4 * n][...] = gb
        outs[4 * n + 1][...], outs[4 * n + 2][...], outs[4 * n + 3][...] = _adamw_update(gb, gw_ref[...], gm_ref[...], gv_ref[...])
        outs[4 * n + 4][...] = loss

    vec_block, gb_block = _full((1, D_MODEL)), _full(gate_b[0].shape)
    outs = pl.pallas_call(
        body, name="adamw_replicated", grid=(1,),
        in_specs=[_full(land_vec.shape), _full(land_gate_b.shape), _full(land_loss.shape)] + [vec_block] * (3 * n) + [gb_block] * 3,
        out_specs=[vec_block] * (4 * n) + [gb_block] * 4 + [_full(land_loss.shape[1:])],
        out_shape=[jax.ShapeDtypeStruct((1, D_MODEL), F32)] * (4 * n) + [jax.ShapeDtypeStruct(gate_b[0].shape, F32)] * 4
        + [jax.ShapeDtypeStruct(land_loss.shape[1:], F32)],
        compiler_params=_params(1),
    )(land_vec, land_gate_b, land_loss, *[src[name] for src in (w, m, v) for name in names], *gate_b)
    results = {name: [outs[k * n + i] for k in range(4)] for i, name in enumerate(names)}
    return results, outs[4 * n:4 * n + 4], outs[4 * n + 4]


SMALL_SHARDED = ("rg_conv_w", "rg_lambda", "sc_conv_w", "odd_norm_pre", "odd_norm_post", "gla_b_gate", "gla_norm_g", "gla_w_gate_lr")
SMALL_ROWS = {"rg_conv_w": (0, 4), "rg_lambda": (4, 2), "sc_conv_w": (6, 3), "odd_norm_pre": (9, 1), "odd_norm_post": (10, 1),
              "gla_b_gate": (11, 2), "gla_norm_g": (13, 1), "gla_w_gate_lr": (16, 32)}


def _pack_small(shards):
    pieces, at = [], 0
    for name in SMALL_SHARDED:
        start, rows = SMALL_ROWS[name]
        if start > at:
            pieces.append(jnp.zeros((start - at, LANES), F32))
        a = shards[name].reshape(rows, -1)
        pieces.append(jnp.pad(a, ((0, 0), (0, LANES - a.shape[1]))))
        at = start + rows
    return jnp.concatenate(pieces, axis=0)


def _unpack_gathered(g):
    def cols(name, width):
        start, rows = SMALL_ROWS[name]
        return jnp.transpose(g[:, start:start + rows, :width], (1, 0, 2)).reshape(rows, N_DEV * width)

    w_lr = cols("gla_w_gate_lr", GLA_KEY // N_DEV).reshape(2, GLA_RANK, GLA_KEY)
    return dict(rg_conv_w=cols("rg_conv_w", LANES), rg_lambda=cols("rg_lambda", LANES), sc_conv_w=cols("sc_conv_w", LANES),
                odd_norm_pre=cols("odd_norm_pre", LANES), odd_norm_post=cols("odd_norm_post", LANES),
                gla_b_gate=cols("gla_b_gate", GLA_KEY // N_DEV), gla_norm_g=cols("gla_norm_g", GLA_DV // N_DEV), gla_w_gate_lr=w_lr)


def _blocks_along_columns(a, rows):
    return jnp.transpose(a.reshape(rows, N_DEV, -1), (1, 0, 2))


def kernel(x, even_norm_pre, even_norm_post, even_w_in, rg_conv_w, rg_conv_b, rg_gate_w, rg_gate_b, rg_lambda, sc_conv_w, even_w_out, odd_norm_pre, odd_norm_post, odd_w_in, gla_w_gate_lr, gla_b_gate, gla_norm_g, odd_w_out, loss_target, m_even_norm_pre, m_even_norm_post, m_even_w_in, m_rg_conv_w, m_rg_conv_b, m_rg_gate_w, m_rg_gate_b, m_rg_lambda, m_sc_conv_w, m_even_w_out, m_odd_norm_pre, m_odd_norm_post, m_odd_w_in, m_gla_w_gate_lr, m_gla_b_gate, m_gla_norm_g, m_odd_w_out, v_even_norm_pre, v_even_norm_post, v_even_w_in, v_rg_conv_w, v_rg_conv_b, v_rg_gate_w, v_rg_gate_b, v_rg_lambda, v_sc_conv_w, v_even_w_out, v_odd_norm_pre, v_odd_norm_post, v_odd_w_in, v_gla_w_gate_lr, v_gla_b_gate, v_gla_norm_g, v_odd_w_out):
    weights = dict(even_norm_pre=even_norm_pre, even_norm_post=even_norm_post, even_w_in=even_w_in, rg_conv_w=rg_conv_w,
                   rg_conv_b=rg_conv_b, rg_gate_w=rg_gate_w, rg_gate_b=rg_gate_b, rg_lambda=rg_lambda, sc_conv_w=sc_conv_w,
                   even_w_out=even_w_out, odd_norm_pre=odd_norm_pre, odd_norm_post=odd_norm_post, odd_w_in=odd_w_in,
                   gla_w_gate_lr=gla_w_gate_lr, gla_b_gate=gla_b_gate, gla_norm_g=gla_norm_g, odd_w_out=odd_w_out)
    m_in = dict(even_norm_pre=m_even_norm_pre, even_norm_post=m_even_norm_post, even_w_in=m_even_w_in, rg_conv_w=m_rg_conv_w,
                rg_conv_b=m_rg_conv_b, rg_gate_w=m_rg_gate_w, rg_gate_b=m_rg_gate_b, rg_lambda=m_rg_lambda, sc_conv_w=m_sc_conv_w,
                even_w_out=m_even_w_out, odd_norm_pre=m_odd_norm_pre, odd_norm_post=m_odd_norm_post, odd_w_in=m_odd_w_in,
                gla_w_gate_lr=m_gla_w_gate_lr, gla_b_gate=m_gla_b_gate, gla_norm_g=m_gla_norm_g, odd_w_out=m_odd_w_out)
    v_in = dict(even_norm_pre=v_even_norm_pre, even_norm_post=v_even_norm_post, even_w_in=v_even_w_in, rg_conv_w=v_rg_conv_w,
                rg_conv_b=v_rg_conv_b, rg_gate_w=v_rg_gate_w, rg_gate_b=v_rg_gate_b, rg_lambda=v_rg_lambda, sc_conv_w=v_sc_conv_w,
                even_w_out=v_even_w_out, odd_norm_pre=v_odd_norm_pre, odd_norm_post=v_odd_norm_post, odd_w_in=v_odd_w_in,
                gla_w_gate_lr=v_gla_w_gate_lr, gla_b_gate=v_gla_b_gate, gla_norm_g=v_gla_norm_g, odd_w_out=v_odd_w_out)
    names = list(weights)
    shapes = {n: weights[n].shape for n in names}
    xs = x[0]
    tgt = loss_target[0]

    first = Exchange()
    first.gather(even_w_in[0].astype(BF16), columns=True, via_sibling=True)
    first.gather(_pack_small({n: weights[n][0] for n in SMALL_SHARDED}))
    w_in_e, small_all = run_exchange(first, "gather_first")
    small = _unpack_gathered(small_all)
    gate_w = rg_gate_w[0].reshape(4, RG_HEADS, RG_HEAD_DIM, RG_HEAD_DIM).astype(BF16)
    gate_b = rg_gate_b[0].reshape(4, RG_HEADS, RG_HEAD_DIM)
    conv_b = rg_conv_b
    wg_pad = [jnp.pad(small["gla_w_gate_lr"][d], ((GLA_RANK * d, LANES - GLA_RANK * (d + 1)), (0, 0))).astype(BF16) for d in range(2)]
    bg = [small["gla_b_gate"][d:d + 1] for d in range(2)]
    gnorm = jnp.tile(small["gla_norm_g"], (1, GLA_HEADS))

    half = D_MODEL // 2
    behind_in = Exchange()
    behind_in.gather(even_w_out[0].astype(BF16))
    (proj_e, h_e), (w_out_e,) = rms_matmul(xs, even_norm_pre, w_in_e, 2 * MM_TILE, 2 * EVEN_SHARD, "even_in", exchange=behind_in)
    w_out_e = w_out_e.reshape(2 * D_MODEL, D_MODEL)
    behind_gates = Exchange()
    behind_gates.gather(odd_w_in[0, :half].astype(BF16))
    behind_gates.gather(odd_w_out[0].astype(BF16))
    (ab, hf), (w_in_o_top, w_out_o) = even_gates_fwd(proj_e, small["rg_conv_w"], conv_b, gate_w, gate_b, small["rg_lambda"],
                                                     exchange=behind_gates)
    w_out_o = w_out_o.reshape(D_MODEL, D_MODEL)
    behind_mix_fwd = Exchange()
    behind_mix_fwd.gather(odd_w_in[0, half:].astype(BF16))
    (u_e, hb), (w_in_o_bottom,) = even_mix_fwd(ab, hf, proj_e, small["sc_conv_w"], exchange=behind_mix_fwd)
    w_in_o = jnp.concatenate([jnp.transpose(part, (1, 0, 2)).reshape(half, ODD_IN) for part in (w_in_o_top, w_in_o_bottom)], axis=0)
    w_in_o = jnp.pad(w_in_o, ((0, 0), (0, ODD_IN_PAD - ODD_IN)))
    y_e, x1 = matmul_post(u_e, w_out_e, xs, even_norm_post, "even_out")

    proj_o, h_o = rms_matmul(x1, small["odd_norm_pre"], w_in_o, MM_TILE, ODD_IN_PAD, "odd_in")
    o_f, st_f = gla_fwd(proj_o, wg_pad[0], bg[0], False)
    osum, u_o, st_b = gla_fwd(proj_o, wg_pad[1], bg[1], True, o_other=o_f, gnorm=gnorm)
    y_o, dout, loss_part = matmul_post(u_o, w_out_o, x1, small["odd_norm_post"], "odd_out", target=tgt)

    du_o, dy_o, d_odd_norm_post = normbwd_matmul_nt(y_o, small["odd_norm_post"], dout, w_out_o, D_MODEL, "odd_out_bwd")
    d_w_out_o = matmul_tn(u_o, dy_o, D_MODEL, D_MODEL, 4 * MM_TILE, BF16, "odd_w_out_grad")
    do, dr, d_gnorm = gla_out_bwd(du_o, proj_o, osum, gnorm)
    dqkv_f, dlr_f, dwg_f, dbg_f = gla_bwd(proj_o, wg_pad[0], bg[0], do, st_f, False)
    dproj_o, dwg_b, dbg_b = gla_bwd(proj_o, wg_pad[1], bg[1], do, st_b, True, first=(dqkv_f, dlr_f, dr))
    dx1, d_odd_norm_pre = matmul_nt_normbwd(dproj_o, w_in_o, x1, small["odd_norm_pre"], dout, ODD_IN_PAD, "odd_in_bwd")
    d_w_in_o = matmul_tn(h_o, dproj_o, D_MODEL, ODD_IN_PAD // 5, 8 * MM_TILE, BF16, "odd_w_in_grad")

    landed = {}
    behind_out = Exchange()
    behind_out.scatter(d_w_out_o.reshape(N_DEV, D_MODEL // N_DEV, D_MODEL))
    behind_out.scatter(d_odd_norm_pre, columns=True)
    behind_out.scatter(d_odd_norm_post, columns=True)
    behind_out.scatter(_blocks_along_columns(jnp.concatenate([dbg_f, dbg_b], axis=0), 2))
    behind_out.scatter(_blocks_along_columns(d_gnorm, 1))
    behind_out.scatter(_blocks_along_columns(jnp.concatenate([dwg_f[:GLA_RANK], dwg_b[GLA_RANK:2 * GLA_RANK]], axis=0), 2 * GLA_RANK))
    (du_e, dy_e, d_even_norm_post), got = normbwd_matmul_nt(y_e, even_norm_post, dx1, w_out_e, 2 * D_MODEL, "even_out_bwd",
                                                           exchange=behind_out)
    p_w_out_o = got[0]
    for n, part in zip(("odd_norm_pre", "odd_norm_post", "gla_b_gate", "gla_norm_g", "gla_w_gate_lr"), got[1:]):
        landed[n] = part
    d_w_out_e = matmul_tn(u_e, dy_e, D_MODEL, D_MODEL, 4 * MM_TILE, BF16, "even_w_out_grad")
    behind_mix = Exchange()
    behind_mix.scatter(d_w_out_e.reshape(N_DEV, 2 * D_MODEL // N_DEV, D_MODEL))
    (dh, drest, d_sc_w, adj_b), (p_w_out_e,) = even_mix_bwd(du_e, hf, hb, proj_e, small["sc_conv_w"], ab, exchange=behind_mix)
    adj_f = linear_scan(ab, 0, dh.reshape(1, *dh.shape), 0, True, True, "scan_fwd_adjoint")
    behind_gates_bwd = Exchange()
    behind_gates_bwd.scatter(jnp.transpose(d_w_in_o[:, :ODD_IN].reshape(D_MODEL, N_DEV, ODD_SHARD), (1, 0, 2)))
    behind_gates_bwd.scatter(d_sc_w, columns=True)
    (dua, d_gate_w, d_gate_b, d_lam), (p_w_in_o, landed["sc_conv_w"]) = even_gates_bwd(
        proj_e, adj_f, adj_b, hf, hb, dh, small["rg_conv_w"], conv_b, gate_w, gate_b, small["rg_lambda"], exchange=behind_gates_bwd)
    gate_w_rows = 4 * RG_HEADS * RG_HEAD_DIM
    behind_conv = Exchange()
    behind_conv.scatter(d_gate_w.reshape(N_DEV, gate_w_rows // N_DEV, RG_HEAD_DIM))
    behind_conv.scatter(d_lam, columns=True)
    (dproj_e, d_conv_w, d_conv_b), (p_gate_w, landed["rg_lambda"]) = rg_conv_bwd(dua, proj_e, drest, small["rg_conv_w"],
                                                                                 exchange=behind_conv)
    behind_w_grad = Exchange()
    behind_w_grad.gather(sum_parts(p_gate_w, "sum_gate_w"))
    d_w_in_e, (g_gate_w_all,) = matmul_tn(h_e, dproj_e, D_MODEL, EVEN_SHARD, 8 * MM_TILE, BF16, "even_w_in_grad",
                                          exchange=behind_w_grad)
    to_sibling = Exchange()
    to_sibling.to_sibling(d_w_in_e)
    to_sibling.scatter(d_conv_w, columns=True)
    from_sibling, landed["rg_conv_w"] = run_exchange(to_sibling, "scatter_to_sibling")
    behind_in_bwd = Exchange()
    behind_in_bwd.among_chips(pair_sum(d_w_in_e, from_sibling))
    (grad_x, d_even_norm_pre), (p_w_in_e,) = matmul_nt_normbwd(
        dproj_e, w_in_e, xs, even_norm_pre, dx1, 3 * D_MODEL, "even_in_bwd", exchange=behind_in_bwd)
    last = Exchange()
    replicated_vecs = ("even_norm_pre", "even_norm_post", "rg_conv_b")
    last.gather(jnp.concatenate([d_even_norm_pre, d_even_norm_post, d_conv_b], axis=0))
    last.gather(d_gate_b.reshape(4 * RG_HEADS, RG_HEAD_DIM))
    last.gather(loss_part)
    land_vec, land_gate_b, land_loss = run_exchange(last, "gather_last")

    results = {}

    def update(name, parts_, shape2d):
        outs = adamw(parts_, weights[name][0].reshape(shape2d), m_in[name][0].reshape(shape2d), v_in[name][0].reshape(shape2d),
                     "adamw_" + name)
        results[name] = [o.reshape(shapes[name]) for o in outs]

    update("even_w_in", p_w_in_e, (D_MODEL, EVEN_SHARD))
    update("even_w_out", p_w_out_e, (2 * D_MODEL // N_DEV, D_MODEL))
    update("odd_w_in", p_w_in_o, (D_MODEL, ODD_SHARD))
    update("odd_w_out", p_w_out_o, (D_MODEL // N_DEV, D_MODEL))
    update("rg_gate_w", g_gate_w_all.reshape(1, gate_w_rows, RG_HEAD_DIM), (gate_w_rows, RG_HEAD_DIM))
    small_out = adamw_small({n: landed[n] for n in SMALL_SHARDED}, weights, m_in, v_in)
    for n in SMALL_SHARDED:
        results[n] = [o[n] for o in small_out]
    gate_b_shape = (4 * RG_HEADS, RG_HEAD_DIM)
    rep_out, gate_b_out, loss_all = adamw_replicated(land_vec, land_gate_b, land_loss, replicated_vecs, weights, m_in, v_in,
                                                     [src["rg_gate_b"].reshape(gate_b_shape) for src in (weights, m_in, v_in)])
    results.update(rep_out)
    results["rg_gate_b"] = [o.reshape(shapes["rg_gate_b"]) for o in gate_b_out]

    return (loss_all[0, 0], grad_x.reshape(x.shape), *[results[n][0] for n in names], *[results[n][1] for n in names],
            *[results[n][2] for n in names], *[results[n][3] for n in names])
```

```python
import functools

import jax
import jax.numpy as jnp
from jax import lax
from jax.experimental import pallas as pl
from jax.experimental.pallas import tpu as pltpu

F32 = jnp.float32
BF16 = jnp.bfloat16

N_DEV = 8
D_MODEL = 1024
NORM_EPS = 1e-6
RG_HEADS = 8
RG_HEAD_DIM = 128
RG_C = 8.0
GLA_HEADS = 4
GLA_DK = 128
GLA_DV = 256
GLA_KEY = 512
GLA_RANK = 16
GLA_NORMALIZER = 16.0
GLA_CHUNK = 64
EVEN_IN = 6144
ODD_IN = 3104
ODD_IN_PAD = 3200
ODD_SHARD = ODD_IN // N_DEV
EVEN_SHARD = EVEN_IN // N_DEV
ADAM_LR = 0.001
ADAM_B1 = 0.9
ADAM_B2 = 0.999
ADAM_EPS = 1e-08
ADAM_WD = 0.01
ADAM_STEP = 10

SMALLEST_NORMAL = 1.1754944e-38
SUBLANES = 8
LANES = 128
VMEM_LIMIT_BYTES = 48 * 2 ** 20
ROW_TILE = 256
MM_TILE = 512
PACK_ROWS = 48
MESH_ID = pl.DeviceIdType.MESH


def _params(n_grid):
    return pltpu.CompilerParams(dimension_semantics=("arbitrary",) * n_grid, vmem_limit_bytes=VMEM_LIMIT_BYTES)


def _bdot(a, b):
    return jnp.dot(a.astype(BF16), b.astype(BF16), preferred_element_type=F32)


def _bdot_nt(a, b):
    return lax.dot_general(a.astype(BF16), b.astype(BF16), (((1,), (1,)), ((), ())), preferred_element_type=F32)


def _bdot_tn(a, b):
    return lax.dot_general(a.astype(BF16), b.astype(BF16), (((0,), (0,)), ((), ())), preferred_element_type=F32)


def _rstd(x):
    return lax.rsqrt(jnp.mean(x * x, axis=-1, keepdims=True) + NORM_EPS)


def _rms(x, g):
    return x * _rstd(x) * g


def _rms_bwd(x, g, dy):
    xh = x * _rstd(x)
    dyg = dy * g
    dx = _rstd(x) * (dyg - xh * jnp.mean(dyg * xh, axis=-1, keepdims=True))
    return dx, jnp.sum(dy * xh, axis=0, keepdims=True)


def _sigmoid(z):
    return 0.5 * jnp.tanh(0.5 * z) + 0.5


def _silu_and_grad(z):
    s = _sigmoid(z)
    return z * s, s * (1.0 + z * (1.0 - s))


def _softplus(z):
    return jnp.maximum(z, 0.0) + jnp.log(1.0 + jnp.exp(-jnp.abs(z)))


def _shift_rows(cur, before, after, d):
    ts = cur.shape[0]
    row = lax.broadcasted_iota(jnp.int32, (SUBLANES, cur.shape[1]), 0)
    out = pltpu.roll(cur, (-d) % ts, 0)
    if d < 0:
        edge = jnp.where(row < -d, pltpu.roll(before, (-d) % SUBLANES, 0), out[:SUBLANES])
        return jnp.concatenate([edge, out[SUBLANES:]], axis=0)
    edge = jnp.where(row >= SUBLANES - d, pltpu.roll(after, (-d) % SUBLANES, 0), out[ts - SUBLANES:])
    return jnp.concatenate([out[:ts - SUBLANES], edge], axis=0)


def _halo_specs(ts, s, width, col, tile=lambda i: i):
    per = ts // SUBLANES
    last = s // SUBLANES - 1
    return [
        pl.BlockSpec((ts, width), lambda i: (tile(i), col)),
        pl.BlockSpec((SUBLANES, width), lambda i: (jnp.maximum(tile(i) * per - 1, 0), col)),
        pl.BlockSpec((SUBLANES, width), lambda i: (jnp.minimum((tile(i) + 1) * per, last), col)),
    ]


def _halo_load(cur_ref, before_ref, after_ref, n_tiles, tile=lambda i: i):
    i = tile(pl.program_id(0))
    before = jnp.where(i > 0, before_ref[...], 0.0)
    after = jnp.where(i < n_tiles - 1, after_ref[...], 0.0)
    return cur_ref[...], before, after


def _full(shape):
    return pl.BlockSpec(shape, lambda *_: (0,) * len(shape))


def _peer(x, y, c, mask):
    px, py, pc = x ^ (mask >> 2), y ^ ((mask >> 1) & 1), c ^ (mask & 1)
    return (px, py, pc), 4 * px + 2 * py + pc


class Exchange:
    SIBLING = 1
    OTHER_CHIPS = (2, 4, 6)

    def __init__(self):
        self.args, self.out_shape, self._kinds = [], [], []

    def gather(self, block, columns=False, via_sibling=False):
        shape = (block.shape[0], N_DEV * block.shape[1]) if columns else (N_DEV,) + block.shape
        return self._add(block, shape, ("gather", columns, via_sibling))

    def scatter(self, stack, columns=False):
        shape = (N_DEV, stack.shape[0], stack.shape[1] // N_DEV) if columns else stack.shape
        return self._add(stack, shape, ("scatter", columns, False))

    def _add(self, arg, shape, kind):
        self.args.append(arg)
        self.out_shape.append(jax.ShapeDtypeStruct(shape, arg.dtype))
        self._kinds.append(kind)
        return len(self.args) - 1

    def semaphores(self):
        n = len(self.args)
        return [pltpu.SemaphoreType.DMA((n, N_DEV - 1)), pltpu.SemaphoreType.DMA((n, N_DEV - 1)), pltpu.SemaphoreType.DMA((n,))]

    def to_sibling(self, array):
        shape = (N_DEV // 2, array.shape[0], array.shape[1] // N_DEV)
        return self._add(array, shape, ("to_sibling", True, False))

    def among_chips(self, stack):
        return self._add(stack, stack.shape, ("among_chips", False, False))

    def _copies(self, position, in_refs, out_refs):
        x, y, c, me = position
        for arr, ((kind, columns, via_sibling), src, out) in enumerate(zip(self._kinds, in_refs, out_refs)):
            if kind == "to_sibling":
                width = src.shape[-1] // N_DEV
                for k in range(N_DEV // 2):
                    block = src.at[:, pl.ds(pl.multiple_of((2 * k + 1 - c) * width, LANES), width)]
                    yield arr, k + 1, block, out.at[k], out.at[k], False, self.SIBLING
                continue
            for mask in range(N_DEV):
                _, peer_id = _peer(x, y, c, mask)
                relayed = via_sibling and mask not in (0, self.SIBLING) + self.OTHER_CHIPS
                if kind == "among_chips":
                    if mask in (0,) + self.OTHER_CHIPS:
                        yield arr, mask, src.at[peer_id // 2], out.at[me // 2], out.at[peer_id // 2], False, mask
                elif kind == "gather":
                    if columns:
                        width = src.shape[-1]
                        yield (arr, mask, src, out.at[:, pl.ds(pl.multiple_of(me * width, LANES), width)],
                               out.at[:, pl.ds(pl.multiple_of(peer_id * width, LANES), width)], relayed, mask)
                    else:
                        yield arr, mask, src, out.at[me], out.at[peer_id], relayed, mask
                else:
                    if columns:
                        width = src.shape[-1] // N_DEV
                        block = src.at[:, pl.ds(pl.multiple_of(peer_id * width, LANES), width)]
                    else:
                        block = src.at[peer_id]
                    yield arr, mask, block, out.at[me], out.at[peer_id], False, mask

    def _remote(self, position, sems, arr, slot, to_mask, src, dst):
        x, y, c, _ = position
        return pltpu.make_async_remote_copy(src_ref=src, dst_ref=dst, send_sem=sems[0].at[arr, slot - 1], recv_sem=sems[1].at[arr, slot - 1],
                                            device_id=_peer(x, y, c, to_mask)[0], device_id_type=MESH_ID)

    def start(self, position, in_refs, out_refs, sems):
        for arr, slot, src, dst, _, relayed, to_mask in self._copies(position, in_refs, out_refs):
            if slot == 0:
                pltpu.make_async_copy(src, dst, sems[2].at[arr]).start()
            elif not relayed:
                self._remote(position, sems, arr, slot, to_mask, src, dst).start()

    def wait(self, position, in_refs, out_refs, sems):
        copies = list(self._copies(position, in_refs, out_refs))
        landings = {(arr, slot): landing for arr, slot, _, _, landing, _, _ in copies}
        passed_on = set()
        for arr, mask, src, _, landing, relayed, _ in copies:
            if relayed:
                held = landings[arr, mask ^ self.SIBLING]
                self._remote(position, sems, arr, mask ^ self.SIBLING, mask ^ self.SIBLING, src, held).wait_recv()
                self._remote(position, sems, arr, mask, self.SIBLING, held, held).start()
                passed_on.add((arr, mask ^ self.SIBLING))
        for arr, slot, src, dst, landing, relayed, to_mask in copies:
            if slot == 0:
                pltpu.make_async_copy(src, dst, sems[2].at[arr]).wait()
                continue
            if (arr, slot) not in passed_on:
                self._remote(position, sems, arr, slot, to_mask, src, landing).wait_recv()
            if relayed:
                held = landings[arr, slot ^ self.SIBLING]
                self._remote(position, sems, arr, slot, self.SIBLING, held, held).wait_send()
            else:
                self._remote(position, sems, arr, slot, to_mask, src, dst).wait_send()


def _call(body, *, name, grid, in_specs, out_specs, out_shape, args, scratch_shapes=(), exchange=None):
    single = not isinstance(out_shape, (list, tuple))
    if single:
        out_specs, out_shape = [out_specs], [out_shape]
    params = _params(len(grid))
    if exchange is None:
        outs = pl.pallas_call(body, name=name, grid=grid, in_specs=in_specs, out_specs=out_specs, out_shape=out_shape,
                              scratch_shapes=list(scratch_shapes), compiler_params=params)(*args)
        return outs[0] if single else outs
    counts = (len(args), len(exchange.args), len(out_shape), len(exchange.out_shape), len(scratch_shapes), 3)

    def wrapped(*refs):
        groups, at = [], 0
        for n in counts:
            groups.append(refs[at:at + n])
            at += n
        main_in, ex_in, main_out, ex_out, main_scratch, sems = groups
        x, y, c = lax.axis_index("x"), lax.axis_index("y"), lax.axis_index("c")
        position = (x, y, c, 4 * x + 2 * y + c)
        ids = [pl.program_id(a) for a in range(len(grid))]
        first = functools.reduce(jnp.logical_and, [i == 0 for i in ids])
        last = functools.reduce(jnp.logical_and, [i == g - 1 for i, g in zip(ids, grid)])

        @pl.when(first)
        def _():
            exchange.start(position, ex_in, ex_out, sems)

        body(*main_in, *main_out, *main_scratch)

        @pl.when(last)
        def _():
            exchange.wait(position, ex_in, ex_out, sems)

    hbm = pl.BlockSpec(memory_space=pl.ANY)
    outs = pl.pallas_call(
        wrapped, name=name, grid=grid, in_specs=list(in_specs) + [hbm] * counts[1], out_specs=list(out_specs) + [hbm] * counts[3],
        out_shape=list(out_shape) + exchange.out_shape, scratch_shapes=list(scratch_shapes) + exchange.semaphores(),
        compiler_params=params)(*args, *exchange.args)
    main = outs[:counts[2]]
    return (main[0] if single else main), outs[counts[2]:]


def run_exchange(exchange, name):
    return _call(lambda: None, name=name, grid=(1,), in_specs=[], out_specs=[], out_shape=[], args=[], exchange=exchange)[1]


def gather_matmul(x, g, w_block, small_block, tm):
    s, d = x.shape
    width = w_block.shape[1]
    pair = 2 * width
    n_chips = N_DEV // 2
    tm = min(tm, s)
    n_i = s // tm
    sibling = Exchange.SIBLING

    def body(chips_ref, x_ref, g_ref, wb_ref, sb_ref, proj_ref, h_ref, w_ref, small_ref, h_all, w_pair, send, recv, local, load_sem):
        j, i = pl.program_id(0), pl.program_id(1)
        xx, yy, cc = lax.axis_index("x"), lax.axis_index("y"), lax.axis_index("c")
        me = 4 * xx + 2 * yy + cc

        def block_of(dev):
            return w_ref.at[:, pl.ds(pl.multiple_of(dev * width, LANES), width)]

        def remote(arr, slot, to_mask, src, dst):
            return pltpu.make_async_remote_copy(src_ref=src, dst_ref=dst, send_sem=send.at[arr, slot - 1], recv_sem=recv.at[arr, slot - 1],
                                                device_id=_peer(xx, yy, cc, to_mask)[0], device_id_type=MESH_ID)

        @pl.when((j == 0) & (i == 0))
        def _():
            pltpu.make_async_copy(wb_ref, block_of(me), local.at[0]).start()
            pltpu.make_async_copy(sb_ref, small_ref.at[me], local.at[1]).start()
            for mask in (sibling,) + Exchange.OTHER_CHIPS:
                remote(0, mask, mask, wb_ref, block_of(me)).start()
            for mask in range(1, N_DEV):
                remote(1, mask, mask, sb_ref, small_ref.at[me]).start()

        for step in range(n_chips):
            @pl.when((j == step) & (i == 0))
            def _(step=step):
                if step == 0:
                    pltpu.make_async_copy(wb_ref, block_of(me), local.at[0]).wait()
                    remote(0, sibling, sibling, wb_ref, block_of(me ^ sibling)).wait_recv()
                else:
                    mask = 2 * step
                    remote(0, mask, mask, wb_ref, block_of(me ^ mask)).wait_recv()
                    remote(0, mask | sibling, sibling, block_of(me ^ mask), block_of(me ^ mask)).start()
                    remote(0, mask | sibling, mask | sibling, wb_ref, block_of(me ^ (mask | sibling))).wait_recv()
                load = pltpu.make_async_copy(w_ref.at[:, pl.ds(pl.multiple_of(chips_ref[step] * pair, LANES), pair)], w_pair, load_sem)
                load.start()
                load.wait()

        rows = pl.ds(pl.multiple_of(i * tm, tm), tm)

        @pl.when(j == 0)
        def _():
            h = _rms(x_ref[...], g_ref[...]).astype(BF16)
            h_all[rows, :] = h
            h_ref[...] = h

        proj_ref[...] = jnp.dot(h_all[rows, :], w_pair[...], preferred_element_type=F32)

        @pl.when((j == n_chips - 1) & (i == n_i - 1))
        def _():
            pltpu.make_async_copy(sb_ref, small_ref.at[me], local.at[1]).wait()
            for mask in range(1, N_DEV):
                remote(1, mask, mask, sb_ref, small_ref.at[me ^ mask]).wait_recv()
                remote(1, mask, mask, sb_ref, small_ref.at[me]).wait_send()
            for mask in (sibling,) + Exchange.OTHER_CHIPS:
                remote(0, mask, mask, wb_ref, block_of(me)).wait_send()
            for mask in Exchange.OTHER_CHIPS:
                remote(0, mask | sibling, sibling, block_of(me ^ mask), block_of(me ^ mask)).wait_send()

    def first_pass_row(j, i, chips):
        return jnp.where(j == 0, i, n_i - 1), 0

    hbm = pl.BlockSpec(memory_space=pl.ANY)
    my_chip = 2 * lax.axis_index("x") + lax.axis_index("y")
    chips = (my_chip ^ jnp.arange(n_chips)).astype(jnp.int32)
    grid_spec = pltpu.PrefetchScalarGridSpec(
        num_scalar_prefetch=1, grid=(n_chips, n_i),
        in_specs=[pl.BlockSpec((tm, d), first_pass_row), pl.BlockSpec((1, d), lambda j, i, chips: (0, 0)), hbm, hbm],
        out_specs=[pl.BlockSpec((tm, pair), lambda j, i, chips: (i, chips[j])), pl.BlockSpec((tm, d), first_pass_row), hbm, hbm],
        scratch_shapes=[pltpu.VMEM((s, d), BF16), pltpu.VMEM((d, pair), BF16), pltpu.SemaphoreType.DMA((2, N_DEV - 1)),
                        pltpu.SemaphoreType.DMA((2, N_DEV - 1)), pltpu.SemaphoreType.DMA((2,)), pltpu.SemaphoreType.DMA(())])
    return pl.pallas_call(
        body, name="even_in", grid_spec=grid_spec,
        out_shape=[jax.ShapeDtypeStruct((s, N_DEV * width), F32), jax.ShapeDtypeStruct((s, d), BF16),
                   jax.ShapeDtypeStruct((d, N_DEV * width), w_block.dtype), jax.ShapeDtypeStruct((N_DEV,) + small_block.shape, small_block.dtype)],
        compiler_params=_params(2),
    )(chips, x, g, w_block, small_block)


def rms_matmul(x, g, w, tm, tn, name, exchange=None):
    s, d = x.shape
    n = w.shape[1]
    tm = min(tm, s)

    def body(x_ref, g_ref, w_ref, o_ref, h_ref):
        @pl.when(pl.program_id(1) == 0)
        def _():
            h_ref[...] = _rms(x_ref[...], g_ref[...]).astype(BF16)

        o_ref[...] = jnp.dot(h_ref[...], w_ref[...], preferred_element_type=F32)

    return _call(
        body, name=name, grid=(s // tm, n // tn),
        in_specs=[pl.BlockSpec((tm, d), lambda i, j: (i, 0)), _full((1, d)), pl.BlockSpec((d, tn), lambda i, j: (0, j))],
        out_specs=[pl.BlockSpec((tm, tn), lambda i, j: (i, j)), pl.BlockSpec((tm, d), lambda i, j: (i, 0))],
        out_shape=[jax.ShapeDtypeStruct((s, n), F32), jax.ShapeDtypeStruct((s, d), BF16)],
        args=[x, g, w], exchange=exchange)


def matmul_post(u, w, xres, g, name, target=None):
    s, k = u.shape
    d = w.shape[1]
    tm = min(MM_TILE, s)
    with_loss = target is not None

    def body(*refs):
        if with_loss:
            u_ref, w_ref, x_ref, g_ref, t_ref, y_ref, dout_ref, loss_ref = refs
        else:
            u_ref, w_ref, x_ref, g_ref, y_ref, out_ref = refs
        y = jnp.dot(u_ref[...], w_ref[...], preferred_element_type=F32)
        y_ref[...] = y
        out = x_ref[...] + _rms(y, g_ref[...])
        if with_loss:
            @pl.when(pl.program_id(0) == 0)
            def _():
                loss_ref[...] = jnp.zeros_like(loss_ref)

            diff = out - t_ref[...]
            dout_ref[...] = diff * (1.0 / d)
            loss_ref[...] += 0.5 * jnp.sum(jnp.mean(diff * diff, axis=-1, keepdims=True))
        else:
            out_ref[...] = out

    row = pl.BlockSpec((tm, d), lambda i: (i, 0))
    in_specs = [pl.BlockSpec((tm, k), lambda i: (i, 0)), _full((k, d)), row, _full((1, d))]
    args = [u, w, xres, g]
    out_specs = [row, row]
    out_shape = [jax.ShapeDtypeStruct((s, d), F32), jax.ShapeDtypeStruct((s, d), F32)]
    if with_loss:
        in_specs.append(row)
        args.append(target)
        out_specs.append(_full((SUBLANES, LANES)))
        out_shape.append(jax.ShapeDtypeStruct((SUBLANES, LANES), F32))
    return pl.pallas_call(body, name=name, grid=(s // tm,), in_specs=in_specs, out_specs=out_specs,
                          out_shape=out_shape, compiler_params=_params(1))(*args)


def normbwd_matmul_nt(y, g, dout, w, tn, name, exchange=None):
    s, d = y.shape
    n = w.shape[0]
    tm = min(MM_TILE, s)

    def body(y_ref, g_ref, dout_ref, w_ref, du_ref, dy_ref, dg_ref):
        i, j = pl.program_id(0), pl.program_id(1)

        @pl.when(j == 0)
        def _():
            dy, dg = _rms_bwd(y_ref[...], g_ref[...], dout_ref[...])
            dy_ref[...] = dy.astype(BF16)

            @pl.when(i == 0)
            def _():
                dg_ref[...] = jnp.zeros_like(dg_ref)

            dg_ref[...] += dg

        du_ref[...] = lax.dot_general(dy_ref[...], w_ref[...], (((1,), (1,)), ((), ())), preferred_element_type=F32)

    row = pl.BlockSpec((tm, d), lambda i, j: (i, 0))
    return _call(
        body, name=name, grid=(s // tm, n // tn),
        in_specs=[row, _full((1, d)), row, pl.BlockSpec((tn, d), lambda i, j: (j, 0))],
        out_specs=[pl.BlockSpec((tm, tn), lambda i, j: (i, j)), row, _full((1, d))],
        out_shape=[jax.ShapeDtypeStruct((s, n), F32), jax.ShapeDtypeStruct((s, d), BF16), jax.ShapeDtypeStruct((1, d), F32)],
        args=[y, g, dout, w], exchange=exchange)


def matmul_tn(a, b, tm, tn, ts, out_dtype, name, exchange=None):
    s, m = a.shape
    n = b.shape[1]
    ts = min(ts, s)
    n_k = s // ts

    def body(a_ref, b_ref, o_ref, acc):
        k = pl.program_id(2)

        @pl.when(k == 0)
        def _():
            acc[...] = jnp.zeros_like(acc)

        acc[...] += lax.dot_general(a_ref[...], b_ref[...], (((0,), (0,)), ((), ())), preferred_element_type=F32)

        @pl.when(k == n_k - 1)
        def _():
            o_ref[...] = acc[...].astype(out_dtype)

    return _call(
        body, name=name, grid=(m // tm, n // tn, n_k),
        in_specs=[pl.BlockSpec((ts, tm), lambda i, j, k: (k, i)), pl.BlockSpec((ts, tn), lambda i, j, k: (k, j))],
        out_specs=pl.BlockSpec((tm, tn), lambda i, j, k: (i, j)),
        out_shape=jax.ShapeDtypeStruct((m, n), out_dtype),
        scratch_shapes=[pltpu.VMEM((tm, tn), F32)], args=[a, b], exchange=exchange)


def matmul_nt_normbwd(dproj, w, x, g, dres, tk, name, exchange=None):
    s, kt = dproj.shape
    d = w.shape[0]
    tm = min(MM_TILE, s)
    n_k = kt // tk

    def body(a_ref, w_ref, x_ref, g_ref, r_ref, dx_ref, dg_ref, acc):
        i, k = pl.program_id(0), pl.program_id(1)

        @pl.when(k == 0)
        def _():
            acc[...] = jnp.zeros_like(acc)

        acc[...] += lax.dot_general(a_ref[...], w_ref[...], (((1,), (1,)), ((), ())), preferred_element_type=F32)

        @pl.when(k == n_k - 1)
        def _():
            dx, dg = _rms_bwd(x_ref[...], g_ref[...], acc[...])
            dx_ref[...] = r_ref[...] + dx

            @pl.when(i == 0)
            def _():
                dg_ref[...] = jnp.zeros_like(dg_ref)

            dg_ref[...] += dg

    row = pl.BlockSpec((tm, d), lambda i, k: (i, 0))
    return _call(
        body, name=name, grid=(s // tm, n_k),
        in_specs=[pl.BlockSpec((tm, tk), lambda i, k: (i, k)), pl.BlockSpec((d, tk), lambda i, k: (0, k)), row, _full((1, d)), row],
        out_specs=[row, _full((1, d))],
        out_shape=[jax.ShapeDtypeStruct((s, d), F32), jax.ShapeDtypeStruct((1, d), F32)],
        scratch_shapes=[pltpu.VMEM((tm, d), F32)], args=[dproj, w, x, g, dres], exchange=exchange)


def _rg_conv(xa, before, after, cw, cb):
    return (cw[0:1, :] * _shift_rows(xa, before, after, -2) + cw[1:2, :] * _shift_rows(xa, before, after, -1)
            + cw[2:3, :] * xa + cw[3:4, :] * _shift_rows(xa, before, after, 1) + cb)


def _rg_gates(ua_h, gw_ref, gb_ref, c_h, direction, head):
    r = _sigmoid(_bdot(ua_h, gw_ref[2 * direction, head]) + gb_ref[2 * direction, head:head + 1, :])
    i = _sigmoid(_bdot(ua_h, gw_ref[2 * direction + 1, head]) + gb_ref[2 * direction + 1, head:head + 1, :])
    log_a = -c_h * r
    a = jnp.exp(log_a)
    beta_sq = -jnp.tanh(log_a) * (1.0 + a * a)
    inv_beta = lax.rsqrt(jnp.maximum(beta_sq, SMALLEST_NORMAL))
    return r, i, a, beta_sq * inv_beta, inv_beta


def even_gates_fwd(proj, conv_w, conv_b, gate_w, gate_b, lam, exchange=None):
    s = proj.shape[0]
    ts = min(2 * ROW_TILE, s)
    n_tiles = s // ts

    def body(xa_ref, xb_ref, xn_ref, cw_ref, cb_ref, gw_ref, gb_ref, lam_ref, o_ref, hf_ref, carry):
        @pl.when(pl.program_id(0) == 0)
        def _():
            carry[...] = jnp.zeros_like(carry)

        xa, before, after = _halo_load(xa_ref, xb_ref, xn_ref, n_tiles)
        ua = _rg_conv(xa, before, after, cw_ref[...], cb_ref[...])
        c = RG_C * _softplus(-lam_ref[...])
        for direction in range(2):
            for head in range(RG_HEADS):
                lanes = slice(head * RG_HEAD_DIM, (head + 1) * RG_HEAD_DIM)
                ua_h = ua[:, lanes]
                _, i, a, beta, _ = _rg_gates(ua_h, gw_ref, gb_ref, c[direction:direction + 1, lanes], direction, head)
                o_ref[2 * direction, :, lanes] = a
                o_ref[2 * direction + 1, :, lanes] = beta * (i * ua_h)
        _scan_tile(o_ref.at[0], o_ref.at[1], hf_ref, carry, False, False)

    return _call(
        body, name="even_gates_fwd", grid=(n_tiles,),
        in_specs=_halo_specs(ts, s, D_MODEL, 0) + [_full(conv_w.shape), _full(conv_b.shape), _full(gate_w.shape),
                                                   _full(gate_b.shape), _full(lam.shape)],
        out_specs=[pl.BlockSpec((4, ts, D_MODEL), lambda i: (0, i, 0)), pl.BlockSpec((ts, D_MODEL), lambda i: (i, 0))],
        out_shape=[jax.ShapeDtypeStruct((4, s, D_MODEL), F32), jax.ShapeDtypeStruct((s, D_MODEL), F32)],
        scratch_shapes=[pltpu.VMEM((SUBLANES, D_MODEL), F32)],
        args=[proj, proj, proj, conv_w, conv_b, gate_w, gate_b, lam], exchange=exchange)


def _scan_tile(a_ref, b_ref, h_ref, carry, reverse, b_times_a):
    ts, c = h_ref.shape
    n_blocks = ts // SUBLANES
    row = lax.broadcasted_iota(jnp.int32, (SUBLANES, c), 0)

    def block(j, h_in):
        r0 = pl.multiple_of((n_blocks - 1 - j if reverse else j) * SUBLANES, SUBLANES)
        a = a_ref[pl.ds(r0, SUBLANES), :]
        b = b_ref[pl.ds(r0, SUBLANES), :]
        if b_times_a:
            b = a * b
        for step in (1, 2, 4):
            shift = SUBLANES - step if reverse else step
            valid = row < SUBLANES - step if reverse else row >= step
            b = jnp.where(valid, a * pltpu.roll(b, shift, 0) + b, b)
            a = jnp.where(valid, a * pltpu.roll(a, shift, 0), a)
        h = a * h_in + b
        h_ref[pl.ds(r0, SUBLANES), :] = h
        return h[0:1, :] if reverse else h[SUBLANES - 1:SUBLANES, :]

    carry[0:1, :] = lax.fori_loop(0, n_blocks, block, carry[0:1, :])


def linear_scan(a_arr, a_idx, b_arr, b_idx, reverse, b_times_a, name, exchange=None):
    _, s, c = a_arr.shape
    ts = min(MM_TILE, s)
    n_tiles = s // ts

    def tile_of(i):
        return n_tiles - 1 - i if reverse else i

    def body(a_ref, b_ref, h_ref, carry):
        @pl.when(pl.program_id(0) == 0)
        def _():
            carry[...] = jnp.zeros_like(carry)

        _scan_tile(a_ref, b_ref, h_ref, carry, reverse, b_times_a)

    return _call(
        body, name=name, grid=(n_tiles,),
        in_specs=[pl.BlockSpec((None, ts, c), lambda i: (a_idx, tile_of(i), 0)),
                  pl.BlockSpec((None, ts, c), lambda i: (b_idx, tile_of(i), 0))],
        out_specs=pl.BlockSpec((ts, c), lambda i: (tile_of(i), 0)),
        out_shape=jax.ShapeDtypeStruct((s, c), F32),
        scratch_shapes=[pltpu.VMEM((SUBLANES, c), F32)], args=[a_arr, b_arr], exchange=exchange)


def _sc_conv(p, before, after, w):
    return w[0:1, :] * _shift_rows(p, before, after, -1) + w[1:2, :] * p + w[2:3, :] * _shift_rows(p, before, after, 1)


def even_mix_fwd(ab, hf, proj, sc_w, exchange=None):
    s = proj.shape[0]
    ts = min(2 * ROW_TILE, s)
    n_tiles = s // ts

    def tile(i):
        return n_tiles - 1 - i

    row = pl.BlockSpec((ts, D_MODEL), lambda i: (tile(i), 0))

    def col(c):
        return pl.BlockSpec((ts, D_MODEL), lambda i: (tile(i), c))

    def body(a_ref, b_ref, hf_ref, za_ref, xb_ref, xbb_ref, xbn_ref, gb_ref, gc_ref, gcb_ref, gcn_ref, zb_ref, w_ref,
             u_ref, hb_ref, carry):
        @pl.when(pl.program_id(0) == 0)
        def _():
            carry[...] = jnp.zeros_like(carry)

        _scan_tile(a_ref, b_ref, hb_ref, carry, True, False)
        xb, xb_before, xb_after = _halo_load(xb_ref, xbb_ref, xbn_ref, n_tiles, tile)
        gc, gc_before, gc_after = _halo_load(gc_ref, gcb_ref, gcn_ref, n_tiles, tile)
        silu_za, _ = _silu_and_grad(za_ref[...])
        silu_zb, _ = _silu_and_grad(zb_ref[...])
        u_ref[:, :D_MODEL] = ((hf_ref[...] + hb_ref[...]) * silu_za).astype(BF16)
        cv = _sc_conv(gc * xb, gc_before * xb_before, gc_after * xb_after, w_ref[...])
        u_ref[:, D_MODEL:] = (gb_ref[...] * cv * silu_zb).astype(BF16)

    return _call(
        body, name="even_mix_fwd", grid=(n_tiles,),
        in_specs=[pl.BlockSpec((None, ts, D_MODEL), lambda i: (2, tile(i), 0)), pl.BlockSpec((None, ts, D_MODEL), lambda i: (3, tile(i), 0)),
                  row, col(1)] + _halo_specs(ts, s, D_MODEL, 2, tile) + [col(3)] + _halo_specs(ts, s, D_MODEL, 4, tile)
        + [col(5), _full(sc_w.shape)],
        out_specs=[pl.BlockSpec((ts, 2 * D_MODEL), lambda i: (tile(i), 0)), row],
        out_shape=[jax.ShapeDtypeStruct((s, 2 * D_MODEL), BF16), jax.ShapeDtypeStruct((s, D_MODEL), F32)],
        scratch_shapes=[pltpu.VMEM((SUBLANES, D_MODEL), F32)],
        args=[ab, ab, hf, proj, proj, proj, proj, proj, proj, proj, proj, proj, sc_w], exchange=exchange)


def even_mix_bwd(du, hf, hb, proj, sc_w, ab, exchange=None):
    s = proj.shape[0]
    ts = min(ROW_TILE, s)
    n_tiles = s // ts
    row = pl.BlockSpec((ts, D_MODEL), lambda i: (i, 0))

    def body(dya_ref, dyb_ref, dybb_ref, dybn_ref, hf_ref, hb_ref, za_ref, xb_ref, xbb_ref, xbn_ref,
             gb_ref, gbb_ref, gbn_ref, gc_ref, gcb_ref, gcn_ref, zb_ref, zbb_ref, zbn_ref, w_ref, a_ref,
             dh_ref, dp_ref, dw_ref, adj_ref, carry):
        @pl.when(pl.program_id(0) == 0)
        def _():
            carry[...] = jnp.zeros_like(carry)

        dyb, dyb_before, dyb_after = _halo_load(dyb_ref, dybb_ref, dybn_ref, n_tiles)
        xb, xb_before, xb_after = _halo_load(xb_ref, xbb_ref, xbn_ref, n_tiles)
        gb, gb_before, gb_after = _halo_load(gb_ref, gbb_ref, gbn_ref, n_tiles)
        gc, gc_before, gc_after = _halo_load(gc_ref, gcb_ref, gcn_ref, n_tiles)
        zb, zb_before, zb_after = _halo_load(zb_ref, zbb_ref, zbn_ref, n_tiles)
        w = w_ref[...]
        dya, za = dya_ref[...], za_ref[...]
        silu_za, dsilu_za = _silu_and_grad(za)
        dh_ref[...] = dya * silu_za
        _scan_tile(a_ref, dh_ref, adj_ref, carry, False, True)
        dp_ref[:, 0:D_MODEL] = (dya * (hf_ref[...] + hb_ref[...]) * dsilu_za).astype(BF16)

        silu_zb, dsilu_zb = _silu_and_grad(zb)
        p, p_before, p_after = gc * xb, gc_before * xb_before, gc_after * xb_after
        cv = _sc_conv(p, p_before, p_after, w)
        dcv = dyb * gb * silu_zb
        dcv_before = dyb_before * gb_before * _silu_and_grad(zb_before)[0]
        dcv_after = dyb_after * gb_after * _silu_and_grad(zb_after)[0]
        dpp = (w[0:1, :] * _shift_rows(dcv, dcv_before, dcv_after, 1) + w[1:2, :] * dcv
               + w[2:3, :] * _shift_rows(dcv, dcv_before, dcv_after, -1))
        dp_ref[:, D_MODEL:2 * D_MODEL] = (dpp * gc).astype(BF16)
        dp_ref[:, 2 * D_MODEL:3 * D_MODEL] = (dyb * cv * silu_zb).astype(BF16)
        dp_ref[:, 3 * D_MODEL:4 * D_MODEL] = (dpp * xb).astype(BF16)
        dp_ref[:, 4 * D_MODEL:5 * D_MODEL] = (dyb * gb * cv * dsilu_zb).astype(BF16)

        @pl.when(pl.program_id(0) == 0)
        def _():
            dw_ref[...] = jnp.zeros_like(dw_ref)

        dw_ref[0:1, :] += jnp.sum(dcv * _shift_rows(p, p_before, p_after, -1), axis=0, keepdims=True)
        dw_ref[1:2, :] += jnp.sum(dcv * p, axis=0, keepdims=True)
        dw_ref[2:3, :] += jnp.sum(dcv * _shift_rows(p, p_before, p_after, 1), axis=0, keepdims=True)

    return _call(
        body, name="even_mix_bwd", grid=(n_tiles,),
        in_specs=[row] + _halo_specs(ts, s, D_MODEL, 1) + [row, row, pl.BlockSpec((ts, D_MODEL), lambda i: (i, 1))]
        + _halo_specs(ts, s, D_MODEL, 2) + _halo_specs(ts, s, D_MODEL, 3) + _halo_specs(ts, s, D_MODEL, 4)
        + _halo_specs(ts, s, D_MODEL, 5) + [_full(sc_w.shape), pl.BlockSpec((None, ts, D_MODEL), lambda i: (2, i, 0))],
        out_specs=[row, pl.BlockSpec((ts, 5 * D_MODEL), lambda i: (i, 0)), _full(sc_w.shape), row],
        out_shape=[jax.ShapeDtypeStruct((s, D_MODEL), F32), jax.ShapeDtypeStruct((s, 5 * D_MODEL), BF16),
                   jax.ShapeDtypeStruct(sc_w.shape, F32), jax.ShapeDtypeStruct((s, D_MODEL), F32)],
        scratch_shapes=[pltpu.VMEM((SUBLANES, D_MODEL), F32)],
        args=[du, du, du, du, hf, hb, proj, *([proj] * 12), sc_w, ab], exchange=exchange)


def even_gates_bwd(proj, adj_f, adj_b, hf, hb, dh, conv_w, conv_b, gate_w, gate_b, lam, exchange=None):
    s = proj.shape[0]
    ts = min(2 * ROW_TILE, s)
    n_tiles = s // ts
    row = pl.BlockSpec((ts, D_MODEL), lambda i: (i, 0))

    def body(xa_ref, xab_ref, xan_ref, af_ref, afb_ref, afn_ref, ab_ref, abb_ref, abn_ref,
             hf_ref, hfb_ref, hfn_ref, hb_ref, hbb_ref, hbn_ref, dh_ref,
             cw_ref, cb_ref, gw_ref, gb_ref, lam_ref, dua_ref, dgw_ref, dgb_ref, dlam_ref):
        @pl.when(pl.program_id(0) == 0)
        def _():
            dgw_ref[...] = jnp.zeros_like(dgw_ref)
            dgb_ref[...] = jnp.zeros_like(dgb_ref)
            dlam_ref[...] = jnp.zeros_like(dlam_ref)

        xa, before, after = _halo_load(xa_ref, xab_ref, xan_ref, n_tiles)
        ua = _rg_conv(xa, before, after, cw_ref[...], cb_ref[...])
        lam_v = lam_ref[...]
        c = RG_C * _softplus(-lam_v)
        dc_dlam = -RG_C * _sigmoid(-lam_v)
        dh = dh_ref[...]
        adj = (_halo_load(af_ref, afb_ref, afn_ref, n_tiles), _halo_load(ab_ref, abb_ref, abn_ref, n_tiles))
        hs = (_halo_load(hf_ref, hfb_ref, hfn_ref, n_tiles), _halo_load(hb_ref, hbb_ref, hbn_ref, n_tiles))
        dua = jnp.zeros_like(ua)
        for direction in range(2):
            step = 1 if direction == 0 else -1
            g = dh + _shift_rows(*adj[direction], step)
            da_all = g * _shift_rows(*hs[direction], -step)
            dua_parts = []
            for head in range(RG_HEADS):
                lanes = slice(head * RG_HEAD_DIM, (head + 1) * RG_HEAD_DIM)
                ua_h = ua[:, lanes]
                c_h = c[direction:direction + 1, lanes]
                r, i, a, beta, inv_beta = _rg_gates(ua_h, gw_ref, gb_ref, c_h, direction, head)
                db = g[:, lanes]
                d_i = db * beta * ua_h
                dbeta = db * (i * ua_h)
                dlog_a = (da_all[:, lanes] - dbeta * a * inv_beta) * a
                dpr = -c_h * dlog_a * r * (1.0 - r)
                dpi = d_i * i * (1.0 - i)
                dua_parts.append(db * beta * i + _bdot_nt(dpr, gw_ref[2 * direction, head])
                                 + _bdot_nt(dpi, gw_ref[2 * direction + 1, head]))
                dgw_ref[2 * direction, head] += _bdot_tn(ua_h, dpr)
                dgw_ref[2 * direction + 1, head] += _bdot_tn(ua_h, dpi)
                dgb_ref[2 * direction, head:head + 1, :] += jnp.sum(dpr, axis=0, keepdims=True)
                dgb_ref[2 * direction + 1, head:head + 1, :] += jnp.sum(dpi, axis=0, keepdims=True)
                dlam_ref[direction:direction + 1, lanes] += (
                    jnp.sum(-r * dlog_a, axis=0, keepdims=True) * dc_dlam[direction:direction + 1, lanes])
            dua = dua + jnp.concatenate(dua_parts, axis=1)
        dua_ref[...] = dua

    return _call(
        body, name="even_gates_bwd", grid=(n_tiles,),
        in_specs=_halo_specs(ts, s, D_MODEL, 0) * 5 + [row] + [_full(conv_w.shape), _full(conv_b.shape), _full(gate_w.shape),
                                                             _full(gate_b.shape), _full(lam.shape)],
        out_specs=[row, _full(gate_w.shape), _full(gate_b.shape), _full(lam.shape)],
        out_shape=[jax.ShapeDtypeStruct((s, D_MODEL), F32), jax.ShapeDtypeStruct(gate_w.shape, F32),
                   jax.ShapeDtypeStruct(gate_b.shape, F32), jax.ShapeDtypeStruct(lam.shape, F32)],
        args=[proj, proj, proj, adj_f, adj_f, adj_f, adj_b, adj_b, adj_b, hf, hf, hf, hb, hb, hb, dh, conv_w, conv_b, gate_w,
              gate_b, lam], exchange=exchange)


def rg_conv_bwd(dua, proj, drest, conv_w, exchange=None):
    s = proj.shape[0]
    ts = min(2 * ROW_TILE, s)
    n_tiles = s // ts

    def body(du_ref, dub_ref, dun_ref, xa_ref, xab_ref, xan_ref, dr_ref, cw_ref, dp_ref, dw_ref, db_ref):
        @pl.when(pl.program_id(0) == 0)
        def _():
            dw_ref[...] = jnp.zeros_like(dw_ref)
            db_ref[...] = jnp.zeros_like(db_ref)

        dua, dua_before, dua_after = _halo_load(du_ref, dub_ref, dun_ref, n_tiles)
        xa, xa_before, xa_after = _halo_load(xa_ref, xab_ref, xan_ref, n_tiles)
        cw = cw_ref[...]
        dxa = (cw[0:1, :] * _shift_rows(dua, dua_before, dua_after, 2) + cw[1:2, :] * _shift_rows(dua, dua_before, dua_after, 1)
               + cw[2:3, :] * dua + cw[3:4, :] * _shift_rows(dua, dua_before, dua_after, -1))
        dp_ref[:, :D_MODEL] = dxa.astype(BF16)
        dp_ref[:, D_MODEL:] = dr_ref[...]
        for tap, offset in enumerate((-2, -1, 0, 1)):
            shifted = xa if offset == 0 else _shift_rows(xa, xa_before, xa_after, offset)
            dw_ref[tap:tap + 1, :] += jnp.sum(dua * shifted, axis=0, keepdims=True)
        db_ref[...] += jnp.sum(dua, axis=0, keepdims=True)

    return _call(
        body, name="rg_conv_bwd", grid=(n_tiles,),
        in_specs=_halo_specs(ts, s, D_MODEL, 0) * 2 + [pl.BlockSpec((ts, 5 * D_MODEL), lambda i: (i, 0)), _full(conv_w.shape)],
        out_specs=[pl.BlockSpec((ts, EVEN_IN), lambda i: (i, 0)), _full(conv_w.shape), _full((1, D_MODEL))],
        out_shape=[jax.ShapeDtypeStruct((s, EVEN_IN), BF16), jax.ShapeDtypeStruct(conv_w.shape, F32),
                   jax.ShapeDtypeStruct((1, D_MODEL), F32)],
        args=[dua, dua, dua, proj, proj, proj, drest, conv_w], exchange=exchange)


def _split3(x):
    x1 = x.astype(BF16)
    rest = x - x1.astype(F32)
    x2 = rest.astype(BF16)
    return x1, x2, (rest - x2.astype(F32)).astype(BF16)


def _chunk_sum_matrix(t, reverse, transpose):
    i = lax.broadcasted_iota(jnp.int32, (t, t), 0)
    j = lax.broadcasted_iota(jnp.int32, (t, t), 1)
    if transpose:
        i, j = j, i
    same = (i // GLA_CHUNK) == (j // GLA_CHUNK)
    return jnp.where(same & ((j >= i) if reverse else (j <= i)), 1.0, 0.0).astype(BF16)


def _exact_dot(m, x):
    return sum(jnp.dot(m, part, preferred_element_type=F32) for part in _split3(x))


def _chunk_mask(t, reverse):
    i = lax.broadcasted_iota(jnp.int32, (t, t), 0)
    j = lax.broadcasted_iota(jnp.int32, (t, t), 1)
    return ((i // GLA_CHUNK) == (j // GLA_CHUNK)) & ((j >= i) if reverse else (j <= i))


def _chunk_rows(c):
    return slice(c * GLA_CHUNK, (c + 1) * GLA_CHUNK)


def _gla_gate(lr, wg, bg):
    z = _bdot(lr, wg) + bg
    log_alpha = (jnp.minimum(z, 0.0) - jnp.log(1.0 + jnp.exp(-jnp.abs(z)))) * (1.0 / GLA_NORMALIZER)
    return z, log_alpha


def _gla_tile_terms(q, k, bcum, reverse):
    n_chunks = q.shape[0] // GLA_CHUNK
    totals = []
    for c in range(n_chunks):
        edge = c * GLA_CHUNK if reverse else (c + 1) * GLA_CHUNK - 1
        totals.append(bcum[edge:edge + 1, :])
    btot = jnp.concatenate([jnp.broadcast_to(total, (GLA_CHUNK, total.shape[1])) for total in totals], axis=0)
    e_pos, e_neg, e_st = jnp.exp(bcum), jnp.exp(-bcum), jnp.exp(btot - bcum)
    return q * (GLA_DK ** -0.5) * e_pos, k * e_neg, k * e_st, e_pos, e_neg, e_st, [jnp.exp(total) for total in totals]


def _gla_specs(t, n_tiles, reverse_order):
    def tile(i):
        return n_tiles - 1 - i if reverse_order else i

    return tile, [
        pl.BlockSpec((t, GLA_KEY), lambda i: (tile(i), 0)),
        pl.BlockSpec((t, GLA_KEY), lambda i: (tile(i), 1)),
        pl.BlockSpec((t, D_MODEL), lambda i: (tile(i), 1)),
        pl.BlockSpec((t, LANES), lambda i: (tile(i), (ODD_IN_PAD - LANES) // LANES)),
    ]


def gla_fwd(proj, wg, bg, reverse, o_other=None, gnorm=None):
    s = proj.shape[0]
    t = min(ROW_TILE, s)
    n_tiles = s // t
    n_chunks = t // GLA_CHUNK
    final = o_other is not None
    tile, specs = _gla_specs(t, n_tiles, reverse)

    def body(*refs):
        if final:
            q_ref, k_ref, v_ref, lr_ref, wg_ref, bg_ref, oo_ref, r_ref, gn_ref, osum_ref, u_ref, st_ref, state = refs
        else:
            q_ref, k_ref, v_ref, lr_ref, wg_ref, bg_ref, o_ref, st_ref, state = refs
            osum_ref = o_ref

        @pl.when(pl.program_id(0) == 0)
        def _():
            state[...] = jnp.zeros_like(state)

        _, log_alpha = _gla_gate(lr_ref[...], wg_ref[...], bg_ref[...])
        bcum = _exact_dot(_chunk_sum_matrix(t, reverse, False), log_alpha)
        q, k, v = q_ref[...], k_ref[...], v_ref[...]
        q_in, k_in, k_st, _, _, _, decays = _gla_tile_terms(q, k, bcum, reverse)
        mask = _chunk_mask(t, reverse)
        order = list(range(n_chunks))[::-1] if reverse else list(range(n_chunks))
        intra, increments = [], []
        for head in range(GLA_HEADS):
            kl = slice(head * GLA_DK, (head + 1) * GLA_DK)
            vl = slice(head * GLA_DV, (head + 1) * GLA_DV)
            scores = jnp.where(mask, _bdot_nt(q_in[:, kl], k_in[:, kl]), 0.0)
            intra.append(_bdot(scores, v[:, vl]))
            increments.append([_bdot_tn(v[_chunk_rows(c), vl], k_st[_chunk_rows(c), kl]) for c in range(n_chunks)])
        for head in range(GLA_HEADS):
            kl = slice(head * GLA_DK, (head + 1) * GLA_DK)
            vl = slice(head * GLA_DV, (head + 1) * GLA_DV)
            running = state[head]
            before = [None] * n_chunks
            for c in order:
                before[c] = running
                st_ref[c, head] = running
                running = running * decays[c][:, kl] + increments[head][c]
            state[head] = running
            inter = [_bdot_nt(q_in[_chunk_rows(c), kl], before[c]) for c in range(n_chunks)]
            osum_ref[:, vl] = intra[head] + jnp.concatenate(inter, axis=0)
        if final:
            osum = osum_ref[...] + oo_ref[...]
            osum_ref[...] = osum
            silu_r, _ = _silu_and_grad(r_ref[...])
            gn = gn_ref[...]
            for head in range(GLA_HEADS):
                vl = slice(head * GLA_DV, (head + 1) * GLA_DV)
                u_ref[:, vl] = (_rms(osum[:, vl], gn[:, vl]) * silu_r[:, vl]).astype(BF16)

    row = pl.BlockSpec((t, D_MODEL), lambda i: (tile(i), 0))
    st_spec = pl.BlockSpec((n_chunks, GLA_HEADS, GLA_DV, GLA_DK), lambda i: (tile(i), 0, 0, 0))
    st_shape = jax.ShapeDtypeStruct((s // GLA_CHUNK, GLA_HEADS, GLA_DV, GLA_DK), F32)
    in_specs = specs + [_full(wg.shape), _full(bg.shape)]
    args = [proj, proj, proj, proj, wg, bg]
    if final:
        in_specs += [row, pl.BlockSpec((t, D_MODEL), lambda i: (tile(i), 2)), _full(gnorm.shape)]
        args += [o_other, proj, gnorm]
        out_specs = [row, row, st_spec]
        out_shape = [jax.ShapeDtypeStruct((s, D_MODEL), F32), jax.ShapeDtypeStruct((s, D_MODEL), BF16), st_shape]
    else:
        out_specs = [row, st_spec]
        out_shape = [jax.ShapeDtypeStruct((s, D_MODEL), F32), st_shape]
    return pl.pallas_call(
        body, name="gla_fwd_rev" if reverse else "gla_fwd", grid=(n_tiles,), in_specs=in_specs, out_specs=out_specs,
        out_shape=out_shape, scratch_shapes=[pltpu.VMEM((GLA_HEADS, GLA_DV, GLA_DK), F32)], compiler_params=_params(1),
    )(*args)


def gla_out_bwd(du, proj, osum, gnorm):
    s = proj.shape[0]
    ts = min(2 * ROW_TILE, s)
    row = pl.BlockSpec((ts, D_MODEL), lambda i: (i, 0))

    def body(du_ref, r_ref, o_ref, gn_ref, do_ref, dr_ref, dgn_ref):
        @pl.when(pl.program_id(0) == 0)
        def _():
            dgn_ref[...] = jnp.zeros_like(dgn_ref)

        du, osum, gn = du_ref[...], o_ref[...], gn_ref[...]
        silu_r, dsilu_r = _silu_and_grad(r_ref[...])
        for head in range(GLA_HEADS):
            vl = slice(head * GLA_DV, (head + 1) * GLA_DV)
            o_h, g_h, du_h = osum[:, vl], gn[:, vl], du[:, vl]
            dr_ref[:, vl] = (du_h * _rms(o_h, g_h) * dsilu_r[:, vl]).astype(BF16)
            do_h, dg_h = _rms_bwd(o_h, g_h, du_h * silu_r[:, vl])
            do_ref[:, vl] = do_h
            dgn_ref[...] += dg_h

    return pl.pallas_call(
        body, name="gla_out_bwd", grid=(s // ts,),
        in_specs=[row, pl.BlockSpec((ts, D_MODEL), lambda i: (i, 2)), row, _full(gnorm.shape)],
        out_specs=[row, row, _full((1, GLA_DV))],
        out_shape=[jax.ShapeDtypeStruct((s, D_MODEL), F32), jax.ShapeDtypeStruct((s, D_MODEL), BF16),
                   jax.ShapeDtypeStruct((1, GLA_DV), F32)],
        compiler_params=_params(1),
    )(du, proj, osum, gnorm)


def gla_bwd(proj, wg, bg, do, states, reverse, first=None):
    s = proj.shape[0]
    t = min(ROW_TILE, s)
    n_tiles = s // t
    n_chunks = t // GLA_CHUNK
    final = first is not None
    tile, specs = _gla_specs(t, n_tiles, not reverse)

    def body(*refs):
        if final:
            (q_ref, k_ref, v_ref, lr_ref, wg_ref, bg_ref, do_ref, st_ref, dqkv1_ref, dlr1_ref, dr_ref,
             dp_ref, dwg_ref, dbg_ref, dstate, dqkv, dbc, dbt) = refs
        else:
            (q_ref, k_ref, v_ref, lr_ref, wg_ref, bg_ref, do_ref, st_ref,
             dqkv, dlr_ref, dwg_ref, dbg_ref, dstate, dbc, dbt) = refs

        @pl.when(pl.program_id(0) == 0)
        def _():
            dstate[...] = jnp.zeros_like(dstate)
            dwg_ref[...] = jnp.zeros_like(dwg_ref)
            dbg_ref[...] = jnp.zeros_like(dbg_ref)

        lr, wg_v = lr_ref[...], wg_ref[...]
        z, log_alpha = _gla_gate(lr, wg_v, bg_ref[...])
        bcum = _exact_dot(_chunk_sum_matrix(t, reverse, False), log_alpha)
        q, k, v, do_v = q_ref[...], k_ref[...], v_ref[...], do_ref[...]
        q_in, k_in, k_st, e_pos, e_neg, e_st, decays = _gla_tile_terms(q, k, bcum, reverse)
        mask = _chunk_mask(t, reverse)
        order = list(range(n_chunks)) if reverse else list(range(n_chunks))[::-1]
        dq_intra, dk_intra, dv_intra, increments = [], [], [], []
        for head in range(GLA_HEADS):
            kl = slice(head * GLA_DK, (head + 1) * GLA_DK)
            vl = slice(head * GLA_DV, (head + 1) * GLA_DV)
            scores = jnp.where(mask, _bdot_nt(q_in[:, kl], k_in[:, kl]), 0.0)
            dscores = jnp.where(mask, _bdot_nt(do_v[:, vl], v[:, vl]), 0.0)
            dv_intra.append(_bdot_tn(scores, do_v[:, vl]))
            dq_intra.append(_bdot(dscores, k_in[:, kl]))
            dk_intra.append(_bdot_tn(dscores, q_in[:, kl]))
            increments.append([_bdot_tn(do_v[_chunk_rows(c), vl], q_in[_chunk_rows(c), kl]) for c in range(n_chunks)])
        for head in range(GLA_HEADS):
            kl = slice(head * GLA_DK, (head + 1) * GLA_DK)
            vl = slice(head * GLA_DV, (head + 1) * GLA_DV)
            running = dstate[head]
            after, ddecay = [None] * n_chunks, [None] * n_chunks
            for c in order:
                after[c] = running
                ddecay[c] = jnp.sum(running * st_ref[c, head], axis=0, keepdims=True)
                running = running * decays[c][:, kl] + increments[head][c]
            dstate[head] = running
            dq_inter = jnp.concatenate([_bdot(do_v[_chunk_rows(c), vl], st_ref[c, head]) for c in range(n_chunks)], axis=0)
            dv_inter = jnp.concatenate([_bdot_nt(k_st[_chunk_rows(c), kl], after[c]) for c in range(n_chunks)], axis=0)
            dk_st = jnp.concatenate([_bdot(v[_chunk_rows(c), vl], after[c]) for c in range(n_chunks)], axis=0)
            dq_in = dq_intra[head] + dq_inter
            ks_h = k_st[:, kl]
            dqkv[:, 2 * GLA_KEY + head * GLA_DV:2 * GLA_KEY + (head + 1) * GLA_DV] = dv_intra[head] + dv_inter
            dqkv[:, kl] = dq_in * (GLA_DK ** -0.5) * e_pos[:, kl]
            dqkv[:, GLA_KEY + head * GLA_DK:GLA_KEY + (head + 1) * GLA_DK] = dk_intra[head] * e_neg[:, kl] + dk_st * e_st[:, kl]
            dbc[:, kl] = dq_in * q_in[:, kl] - dk_intra[head] * k_in[:, kl] - dk_st * ks_h
            weighted = dk_st * ks_h
            for c in range(n_chunks):
                dbtot = jnp.sum(weighted[_chunk_rows(c)], axis=0, keepdims=True) + ddecay[c] * decays[c][:, kl]
                dbt[_chunk_rows(c), kl] = jnp.broadcast_to(dbtot, (GLA_CHUNK, GLA_DK))
        dlog_alpha = _exact_dot(_chunk_sum_matrix(t, reverse, True), dbc[...]) + dbt[...]
        dz = dlog_alpha * _sigmoid(-z) * (1.0 / GLA_NORMALIZER)
        dlr = _bdot_nt(dz, wg_v)
        dwg_ref[...] += _bdot_tn(lr, dz)
        dbg_ref[...] += jnp.sum(dz, axis=0, keepdims=True)
        if final:
            dp_ref[:, :2 * D_MODEL] = (dqkv[...] + dqkv1_ref[...]).astype(BF16)
            dp_ref[:, 2 * D_MODEL:3 * D_MODEL] = dr_ref[...]
            dp_ref[:, 3 * D_MODEL:] = (dlr + dlr1_ref[...]).astype(BF16)
        else:
            dlr_ref[...] = dlr

    row = pl.BlockSpec((t, D_MODEL), lambda i: (tile(i), 0))
    wide = pl.BlockSpec((t, 2 * D_MODEL), lambda i: (tile(i), 0))
    narrow = pl.BlockSpec((t, LANES), lambda i: (tile(i), 0))
    st_spec = pl.BlockSpec((n_chunks, GLA_HEADS, GLA_DV, GLA_DK), lambda i: (tile(i), 0, 0, 0))
    in_specs = specs + [_full(wg.shape), _full(bg.shape), row, st_spec]
    args = [proj, proj, proj, proj, wg, bg, do, states]
    acc_specs = [_full(wg.shape), _full(bg.shape)]
    acc_shapes = [jax.ShapeDtypeStruct(wg.shape, F32), jax.ShapeDtypeStruct(bg.shape, F32)]
    scratch = [pltpu.VMEM((GLA_HEADS, GLA_DV, GLA_DK), F32)]
    work = [pltpu.VMEM((t, GLA_KEY), F32), pltpu.VMEM((t, GLA_KEY), F32)]
    if final:
        in_specs += [wide, narrow, row]
        args += list(first)
        out_specs = [pl.BlockSpec((t, ODD_IN_PAD), lambda i: (tile(i), 0))] + acc_specs
        out_shape = [jax.ShapeDtypeStruct((s, ODD_IN_PAD), BF16)] + acc_shapes
        scratch += [pltpu.VMEM((t, 2 * D_MODEL), F32)] + work
    else:
        out_specs = [wide, narrow] + acc_specs
        out_shape = [jax.ShapeDtypeStruct((s, 2 * D_MODEL), F32), jax.ShapeDtypeStruct((s, LANES), F32)] + acc_shapes
        scratch += work
    return pl.pallas_call(
        body, name="gla_bwd_rev" if reverse else "gla_bwd", grid=(n_tiles,), in_specs=in_specs, out_specs=out_specs,
        out_shape=out_shape, scratch_shapes=scratch, compiler_params=_params(1),
    )(*args)


def pair_sum(grad, from_sibling):
    n_chips, r, w = from_sibling.shape

    def body(even_ref, odd_ref, sib_ref, o_ref):
        mine = jnp.where(lax.axis_index("c") == 1, odd_ref[...], even_ref[...])
        o_ref[...] = (mine.astype(F32) + sib_ref[...].astype(F32)).astype(o_ref.dtype)

    return pl.pallas_call(
        body, name="pair_sum", grid=(n_chips,),
        in_specs=[pl.BlockSpec((r, w), lambda k: (0, 2 * k)), pl.BlockSpec((r, w), lambda k: (0, 2 * k + 1)),
                  pl.BlockSpec((None, r, w), lambda k: (k, 0, 0))],
        out_specs=pl.BlockSpec((None, r, w), lambda k: (k, 0, 0)),
        out_shape=jax.ShapeDtypeStruct(from_sibling.shape, from_sibling.dtype), compiler_params=_params(1),
    )(grad, grad, from_sibling)


def _adamw_update(g, w, m, v):
    new_m = ADAM_B1 * m + (1.0 - ADAM_B1) * g
    new_v = ADAM_B2 * v + (1.0 - ADAM_B2) * (g * g)
    m_hat = new_m / (1.0 - ADAM_B1 ** ADAM_STEP)
    v_hat = new_v / (1.0 - ADAM_B2 ** ADAM_STEP)
    return -ADAM_LR * (m_hat / (jnp.sqrt(v_hat) + ADAM_EPS) + ADAM_WD * w), new_m, new_v


def sum_parts(parts, name):
    _, r, c = parts.shape

    def body(p_ref, o_ref):
        total = p_ref[0].astype(F32)
        for j in range(1, N_DEV):
            total = total + p_ref[j].astype(F32)
        o_ref[...] = total

    return pl.pallas_call(body, name=name, in_specs=[_full(parts.shape)], out_specs=_full((r, c)), grid=(1,),
                          out_shape=jax.ShapeDtypeStruct((r, c), F32), compiler_params=_params(1))(parts)


def adamw(parts, w, m, v, name):
    n, r, c = parts.shape
    tr = r if r <= MM_TILE else ROW_TILE

    def body(p_ref, w_ref, m_ref, v_ref, g_ref, d_ref, nm_ref, nv_ref):
        g = p_ref[0].astype(F32)
        for j in range(1, n):
            g = g + p_ref[j].astype(F32)
        g_ref[...] = g
        d_ref[...], nm_ref[...], nv_ref[...] = _adamw_update(g, w_ref[...], m_ref[...], v_ref[...])

    row = pl.BlockSpec((tr, c), lambda i: (i, 0))
    return pl.pallas_call(
        body, name=name, grid=(r // tr,),
        in_specs=[pl.BlockSpec((n, tr, c), lambda i: (0, i, 0)), row, row, row], out_specs=[row] * 4,
        out_shape=[jax.ShapeDtypeStruct((r, c), F32)] * 4, compiler_params=_params(1),
    )(parts, w, m, v)


def _small_views(shape):
    if len(shape) == 2:
        return [((slice(None), slice(None)), (slice(None), slice(None)))]
    if len(shape) == 3:
        return [((slice(None), slice(None)), (0,))]
    rows = shape[2]
    return [((slice(k * rows, (k + 1) * rows), slice(None)), (0, k)) for k in range(shape[1])]


def adamw_small(landings, w, m, v):
    names = list(landings)
    n = len(names)
    shapes = [w[name].shape for name in names]

    def body(*refs):
        land, ws, ms, vs = refs[:n], refs[n:2 * n], refs[2 * n:3 * n], refs[3 * n:4 * n]
        outs = [refs[(4 + k) * n:(5 + k) * n] for k in range(4)]
        for k in range(n):
            total = land[k][0]
            for j in range(1, N_DEV):
                total = total + land[k][j]
            for rows, at in _small_views(shapes[k]):
                g = total[rows]
                outs[0][k][at] = g
                outs[1][k][at], outs[2][k][at], outs[3][k][at] = _adamw_update(g, ws[k][at], ms[k][at], vs[k][at])

    blocks = [_full(sh) for sh in shapes]
    outs = pl.pallas_call(
        body, name="adamw_small", grid=(1,),
        in_specs=[_full(landings[name].shape) for name in names] + blocks * 3, out_specs=blocks * 4,
        out_shape=[jax.ShapeDtypeStruct(sh, F32) for sh in shapes] * 4, compiler_params=_params(1),
    )(*[landings[name] for name in names], *[src[name] for src in (w, m, v) for name in names])
    return [dict(zip(names, outs[k * n:(k + 1) * n])) for k in range(4)]


def adamw_replicated(land_vec, land_gate_b, land_loss, names, w, m, v, gate_b):
    n = len(names)

    def body(*refs):
        vec_ref, gb_ref, loss_ref = refs[:3]
        ws, ms, vs = refs[3:3 + n], refs[3 + n:3 + 2 * n], refs[3 + 2 * n:3 + 3 * n]
        gw_ref, gm_ref, gv_ref = refs[3 + 3 * n:6 + 3 * n]
        outs = refs[6 + 3 * n:]
        vec, gb, loss = vec_ref[0], gb_ref[0], loss_ref[0]
        for j in range(1, N_DEV):
            vec, gb, loss = vec + vec_ref[j], gb + gb_ref[j], loss + loss_ref[j]
        for k in range(n):
            g = vec[k:k + 1, :]
            outs[k][...] = g
            outs[n + k][...], outs[2 * n + k][...], outs[3 * n + k][...] = _adamw_update(g, ws[k][...], ms[k][...], vs[k][...])
        outs[4 * n][...] = gb
        outs[4 * n + 1][...], outs[4 * n + 2][...], outs[4 * n + 3][...] = _adamw_update(gb, gw_ref[...], gm_ref[...], gv_ref[...])
        outs[4 * n + 4][...] = loss

    vec_block, gb_block = _full((1, D_MODEL)), _full(gate_b[0].shape)
    outs = pl.pallas_call(
        body, name="adamw_replicated", grid=(1,),
        in_specs=[_full(land_vec.shape), _full(land_gate_b.shape), _full(land_loss.shape)] + [vec_block] * (3 * n) + [gb_block] * 3,
        out_specs=[vec_block] * (4 * n) + [gb_block] * 4 + [_full(land_loss.shape[1:])],
        out_shape=[jax.ShapeDtypeStruct((1, D_MODEL), F32)] * (4 * n) + [jax.ShapeDtypeStruct(gate_b[0].shape, F32)] * 4
        + [jax.ShapeDtypeStruct(land_loss.shape[1:], F32)],
        compiler_params=_params(1),
    )(land_vec, land_gate_b, land_loss, *[src[name] for src in (w, m, v) for name in names], *gate_b)
    results = {name: [outs[k * n + i] for k in range(4)] for i, name in enumerate(names)}
    return results, outs[4 * n:4 * n + 4], outs[4 * n + 4]


SMALL_SHARDED = ("rg_conv_w", "rg_lambda", "sc_conv_w", "odd_norm_pre", "odd_norm_post", "gla_b_gate", "gla_norm_g", "gla_w_gate_lr")
SMALL_ROWS = {"rg_conv_w": (0, 4), "rg_lambda": (4, 2), "sc_conv_w": (6, 3), "odd_norm_pre": (9, 1), "odd_norm_post": (10, 1),
              "gla_b_gate": (11, 2), "gla_norm_g": (13, 1), "gla_w_gate_lr": (16, 32)}


def _pack_small(shards):
    pieces, at = [], 0
    for name in SMALL_SHARDED:
        start, rows = SMALL_ROWS[name]
        if start > at:
            pieces.append(jnp.zeros((start - at, LANES), F32))
        a = shards[name].reshape(rows, -1)
        pieces.append(jnp.pad(a, ((0, 0), (0, LANES - a.shape[1]))))
        at = start + rows
    return jnp.concatenate(pieces, axis=0)


def _unpack_gathered(g):
    def cols(name, width):
        start, rows = SMALL_ROWS[name]
        return jnp.transpose(g[:, start:start + rows, :width], (1, 0, 2)).reshape(rows, N_DEV * width)

    w_lr = cols("gla_w_gate_lr", GLA_KEY // N_DEV).reshape(2, GLA_RANK, GLA_KEY)
    return dict(rg_conv_w=cols("rg_conv_w", LANES), rg_lambda=cols("rg_lambda", LANES), sc_conv_w=cols("sc_conv_w", LANES),
                odd_norm_pre=cols("odd_norm_pre", LANES), odd_norm_post=cols("odd_norm_post", LANES),
                gla_b_gate=cols("gla_b_gate", GLA_KEY // N_DEV), gla_norm_g=cols("gla_norm_g", GLA_DV // N_DEV), gla_w_gate_lr=w_lr)


def _blocks_along_columns(a, rows):
    return jnp.transpose(a.reshape(rows, N_DEV, -1), (1, 0, 2))


def kernel(x, even_norm_pre, even_norm_post, even_w_in, rg_conv_w, rg_conv_b, rg_gate_w, rg_gate_b, rg_lambda, sc_conv_w, even_w_out, odd_norm_pre, odd_norm_post, odd_w_in, gla_w_gate_lr, gla_b_gate, gla_norm_g, odd_w_out, loss_target, m_even_norm_pre, m_even_norm_post, m_even_w_in, m_rg_conv_w, m_rg_conv_b, m_rg_gate_w, m_rg_gate_b, m_rg_lambda, m_sc_conv_w, m_even_w_out, m_odd_norm_pre, m_odd_norm_post, m_odd_w_in, m_gla_w_gate_lr, m_gla_b_gate, m_gla_norm_g, m_odd_w_out, v_even_norm_pre, v_even_norm_post, v_even_w_in, v_rg_conv_w, v_rg_conv_b, v_rg_gate_w, v_rg_gate_b, v_rg_lambda, v_sc_conv_w, v_even_w_out, v_odd_norm_pre, v_odd_norm_post, v_odd_w_in, v_gla_w_gate_lr, v_gla_b_gate, v_gla_norm_g, v_odd_w_out):
    weights = dict(even_norm_pre=even_norm_pre, even_norm_post=even_norm_post, even_w_in=even_w_in, rg_conv_w=rg_conv_w,
                   rg_conv_b=rg_conv_b, rg_gate_w=rg_gate_w, rg_gate_b=rg_gate_b, rg_lambda=rg_lambda, sc_conv_w=sc_conv_w,
                   even_w_out=even_w_out, odd_norm_pre=odd_norm_pre, odd_norm_post=odd_norm_post, odd_w_in=odd_w_in,
                   gla_w_gate_lr=gla_w_gate_lr, gla_b_gate=gla_b_gate, gla_norm_g=gla_norm_g, odd_w_out=odd_w_out)
    m_in = dict(even_norm_pre=m_even_norm_pre, even_norm_post=m_even_norm_post, even_w_in=m_even_w_in, rg_conv_w=m_rg_conv_w,
                rg_conv_b=m_rg_conv_b, rg_gate_w=m_rg_gate_w, rg_gate_b=m_rg_gate_b, rg_lambda=m_rg_lambda, sc_conv_w=m_sc_conv_w,
                even_w_out=m_even_w_out, odd_norm_pre=m_odd_norm_pre, odd_norm_post=m_odd_norm_post, odd_w_in=m_odd_w_in,
                gla_w_gate_lr=m_gla_w_gate_lr, gla_b_gate=m_gla_b_gate, gla_norm_g=m_gla_norm_g, odd_w_out=m_odd_w_out)
    v_in = dict(even_norm_pre=v_even_norm_pre, even_norm_post=v_even_norm_post, even_w_in=v_even_w_in, rg_conv_w=v_rg_conv_w,
                rg_conv_b=v_rg_conv_b, rg_gate_w=v_rg_gate_w, rg_gate_b=v_rg_gate_b, rg_lambda=v_rg_lambda, sc_conv_w=v_sc_conv_w,
                even_w_out=v_even_w_out, odd_norm_pre=v_odd_norm_pre, odd_norm_post=v_odd_norm_post, odd_w_in=v_odd_w_in,
                gla_w_gate_lr=v_gla_w_gate_lr, gla_b_gate=v_gla_b_gate, gla_norm_g=v_gla_norm_g, odd_w_out=v_odd_w_out)
    names = list(weights)
    shapes = {n: weights[n].shape for n in names}
    xs = x[0]
    tgt = loss_target[0]

    proj_e, h_e, w_in_e, small_all = gather_matmul(xs, even_norm_pre, even_w_in[0].astype(BF16),
                                                   _pack_small({n: weights[n][0] for n in SMALL_SHARDED}), 2 * MM_TILE)
    small = _unpack_gathered(small_all)
    gate_w = rg_gate_w[0].reshape(4, RG_HEADS, RG_HEAD_DIM, RG_HEAD_DIM).astype(BF16)
    gate_b = rg_gate_b[0].reshape(4, RG_HEADS, RG_HEAD_DIM)
    conv_b = rg_conv_b
    wg_pad = [jnp.pad(small["gla_w_gate_lr"][d], ((GLA_RANK * d, LANES - GLA_RANK * (d + 1)), (0, 0))).astype(BF16) for d in range(2)]
    bg = [small["gla_b_gate"][d:d + 1] for d in range(2)]
    gnorm = jnp.tile(small["gla_norm_g"], (1, GLA_HEADS))

    half = D_MODEL // 2
    behind_gates = Exchange()
    behind_gates.gather(even_w_out[0].astype(BF16), via_sibling=True)
    behind_gates.gather(odd_w_in[0, :half].astype(BF16), via_sibling=True)
    (ab, hf), (w_out_e, w_in_o_top) = even_gates_fwd(proj_e, small["rg_conv_w"], conv_b, gate_w, gate_b, small["rg_lambda"],
                                                     exchange=behind_gates)
    w_out_e = w_out_e.reshape(2 * D_MODEL, D_MODEL)
    behind_mix_fwd = Exchange()
    behind_mix_fwd.gather(odd_w_in[0, half:].astype(BF16), via_sibling=True)
    behind_mix_fwd.gather(odd_w_out[0].astype(BF16), via_sibling=True)
    (u_e, hb), (w_in_o_bottom, w_out_o) = even_mix_fwd(ab, hf, proj_e, small["sc_conv_w"], exchange=behind_mix_fwd)
    w_out_o = w_out_o.reshape(D_MODEL, D_MODEL)
    w_in_o = jnp.concatenate([jnp.transpose(part, (1, 0, 2)).reshape(half, ODD_IN) for part in (w_in_o_top, w_in_o_bottom)], axis=0)
    w_in_o = jnp.pad(w_in_o, ((0, 0), (0, ODD_IN_PAD - ODD_IN)))
    y_e, x1 = matmul_post(u_e, w_out_e, xs, even_norm_post, "even_out")

    proj_o, h_o = rms_matmul(x1, small["odd_norm_pre"], w_in_o, MM_TILE, ODD_IN_PAD, "odd_in")
    o_f, st_f = gla_fwd(proj_o, wg_pad[0], bg[0], False)
    osum, u_o, st_b = gla_fwd(proj_o, wg_pad[1], bg[1], True, o_other=o_f, gnorm=gnorm)
    y_o, dout, loss_part = matmul_post(u_o, w_out_o, x1, small["odd_norm_post"], "odd_out", target=tgt)

    du_o, dy_o, d_odd_norm_post = normbwd_matmul_nt(y_o, small["odd_norm_post"], dout, w_out_o, D_MODEL, "odd_out_bwd")
    d_w_out_o = matmul_tn(u_o, dy_o, D_MODEL, D_MODEL, 4 * MM_TILE, BF16, "odd_w_out_grad")
    do, dr, d_gnorm = gla_out_bwd(du_o, proj_o, osum, gnorm)
    dqkv_f, dlr_f, dwg_f, dbg_f = gla_bwd(proj_o, wg_pad[0], bg[0], do, st_f, False)
    dproj_o, dwg_b, dbg_b = gla_bwd(proj_o, wg_pad[1], bg[1], do, st_b, True, first=(dqkv_f, dlr_f, dr))
    dx1, d_odd_norm_pre = matmul_nt_normbwd(dproj_o, w_in_o, x1, small["odd_norm_pre"], dout, ODD_IN_PAD, "odd_in_bwd")
    d_w_in_o = matmul_tn(h_o, dproj_o, D_MODEL, ODD_IN_PAD // 5, 8 * MM_TILE, BF16, "odd_w_in_grad")

    landed = {}
    behind_out = Exchange()
    behind_out.scatter(d_w_out_o.reshape(N_DEV, D_MODEL // N_DEV, D_MODEL))
    behind_out.scatter(d_odd_norm_pre, columns=True)
    behind_out.scatter(d_odd_norm_post, columns=True)
    behind_out.scatter(_blocks_along_columns(jnp.concatenate([dbg_f, dbg_b], axis=0), 2))
    behind_out.scatter(_blocks_along_columns(d_gnorm, 1))
    behind_out.scatter(_blocks_along_columns(jnp.concatenate([dwg_f[:GLA_RANK], dwg_b[GLA_RANK:2 * GLA_RANK]], axis=0), 2 * GLA_RANK))
    (du_e, dy_e, d_even_norm_post), got = normbwd_matmul_nt(y_e, even_norm_post, dx1, w_out_e, 2 * D_MODEL, "even_out_bwd",
                                                           exchange=behind_out)
    p_w_out_o = got[0]
    for n, part in zip(("odd_norm_pre", "odd_norm_post", "gla_b_gate", "gla_norm_g", "gla_w_gate_lr"), got[1:]):
        landed[n] = part
    d_w_out_e = matmul_tn(u_e, dy_e, D_MODEL, D_MODEL, 4 * MM_TILE, BF16, "even_w_out_grad")
    behind_mix = Exchange()
    behind_mix.scatter(d_w_out_e.reshape(N_DEV, 2 * D_MODEL // N_DEV, D_MODEL))
    (dh, drest, d_sc_w, adj_b), (p_w_out_e,) = even_mix_bwd(du_e, hf, hb, proj_e, small["sc_conv_w"], ab, exchange=behind_mix)
    adj_f = linear_scan(ab, 0, dh.reshape(1, *dh.shape), 0, True, True, "scan_fwd_adjoint")
    behind_gates_bwd = Exchange()
    behind_gates_bwd.scatter(jnp.transpose(d_w_in_o[:, :ODD_IN].reshape(D_MODEL, N_DEV, ODD_SHARD), (1, 0, 2)))
    behind_gates_bwd.scatter(d_sc_w, columns=True)
    (dua, d_gate_w, d_gate_b, d_lam), (p_w_in_o, landed["sc_conv_w"]) = even_gates_bwd(
        proj_e, adj_f, adj_b, hf, hb, dh, small["rg_conv_w"], conv_b, gate_w, gate_b, small["rg_lambda"], exchange=behind_gates_bwd)
    gate_w_rows = 4 * RG_HEADS * RG_HEAD_DIM
    behind_conv = Exchange()
    behind_conv.scatter(d_gate_w.reshape(N_DEV, gate_w_rows // N_DEV, RG_HEAD_DIM))
    behind_conv.scatter(d_lam, columns=True)
    (dproj_e, d_conv_w, d_conv_b), (p_gate_w, landed["rg_lambda"]) = rg_conv_bwd(dua, proj_e, drest, small["rg_conv_w"],
                                                                                 exchange=behind_conv)
    behind_w_grad = Exchange()
    behind_w_grad.gather(sum_parts(p_gate_w, "sum_gate_w"))
    d_w_in_e, (g_gate_w_all,) = matmul_tn(h_e, dproj_e, D_MODEL, EVEN_SHARD, 8 * MM_TILE, BF16, "even_w_in_grad",
                                          exchange=behind_w_grad)
    to_sibling = Exchange()
    to_sibling.to_sibling(d_w_in_e)
    to_sibling.scatter(d_conv_w, columns=True)
    from_sibling, landed["rg_conv_w"] = run_exchange(to_sibling, "scatter_to_sibling")
    behind_in_bwd = Exchange()
    behind_in_bwd.among_chips(pair_sum(d_w_in_e, from_sibling))
    (grad_x, d_even_norm_pre), (p_w_in_e,) = matmul_nt_normbwd(
        dproj_e, w_in_e, xs, even_norm_pre, dx1, 3 * D_MODEL, "even_in_bwd", exchange=behind_in_bwd)
    last = Exchange()
    replicated_vecs = ("even_norm_pre", "even_norm_post", "rg_conv_b")
    last.gather(jnp.concatenate([d_even_norm_pre, d_even_norm_post, d_conv_b], axis=0))
    last.gather(d_gate_b.reshape(4 * RG_HEADS, RG_HEAD_DIM))
    last.gather(loss_part)
    land_vec, land_gate_b, land_loss = run_exchange(last, "gather_last")

    results = {}

    def update(name, parts_, shape2d):
        outs = adamw(parts_, weights[name][0].reshape(shape2d), m_in[name][0].reshape(shape2d), v_in[name][0].reshape(shape2d),
                     "adamw_" + name)
        results[name] = [o.reshape(shapes[name]) for o in outs]

    update("even_w_in", p_w_in_e, (D_MODEL, EVEN_SHARD))
    update("even_w_out", p_w_out_e, (2 * D_MODEL // N_DEV, D_MODEL))
    update("odd_w_in", p_w_in_o, (D_MODEL, ODD_SHARD))
    update("odd_w_out", p_w_out_o, (D_MODEL // N_DEV, D_MODEL))
    update("rg_gate_w", g_gate_w_all.reshape(1, gate_w_rows, RG_HEAD_DIM), (gate_w_rows, RG_HEAD_DIM))
    small_out = adamw_small({n: landed[n] for n in SMALL_SHARDED}, weights, m_in, v_in)
    for n in SMALL_SHARDED:
        results[n] = [o[n] for o in small_out]
    gate_b_shape = (4 * RG_HEADS, RG_HEAD_DIM)
    rep_out, gate_b_out, loss_all = adamw_replicated(land_vec, land_gate_b, land_loss, replicated_vecs, weights, m_in, v_in,
                                                     [src["rg_gate_b"].reshape(gate_b_shape) for src in (weights, m_in, v_in)])
    results.update(rep_out)
    results["rg_gate_b"] = [o.reshape(shapes["rg_gate_b"]) for o in gate_b_out]

    return (loss_all[0, 0], grad_x.reshape(x.shape), *[results[n][0] for n in names], *[results[n][1] for n in names],
            *[results[n][2] for n in names], *[results[n][3] for n in names])
```

```python
import functools

import jax
import jax.numpy as jnp
from jax import lax
from jax.experimental import pallas as pl
from jax.experimental.pallas import tpu as pltpu

F32 = jnp.float32
BF16 = jnp.bfloat16

N_DEV = 8
D_MODEL = 1024
NORM_EPS = 1e-6
RG_HEADS = 8
RG_HEAD_DIM = 128
RG_C = 8.0
GLA_HEADS = 4
GLA_DK = 128
GLA_DV = 256
GLA_KEY = 512
GLA_RANK = 16
GLA_NORMALIZER = 16.0
GLA_CHUNK = 64
EVEN_IN = 6144
ODD_IN = 3104
ODD_IN_PAD = 3200
ODD_SHARD = ODD_IN // N_DEV
EVEN_SHARD = EVEN_IN // N_DEV
ADAM_LR = 0.001
ADAM_B1 = 0.9
ADAM_B2 = 0.999
ADAM_EPS = 1e-08
ADAM_WD = 0.01
ADAM_STEP = 10

SMALLEST_NORMAL = 1.1754944e-38
SUBLANES = 8
LANES = 128
VMEM_LIMIT_BYTES = 48 * 2 ** 20
ROW_TILE = 256
MM_TILE = 512
PACK_ROWS = 48
MESH_ID = pl.DeviceIdType.MESH


def _params(n_grid):
    return pltpu.CompilerParams(dimension_semantics=("arbitrary",) * n_grid, vmem_limit_bytes=VMEM_LIMIT_BYTES)


def _bdot(a, b):
    return jnp.dot(a.astype(BF16), b.astype(BF16), preferred_element_type=F32)


def _bdot_nt(a, b):
    return lax.dot_general(a.astype(BF16), b.astype(BF16), (((1,), (1,)), ((), ())), preferred_element_type=F32)


def _bdot_tn(a, b):
    return lax.dot_general(a.astype(BF16), b.astype(BF16), (((0,), (0,)), ((), ())), preferred_element_type=F32)


def _rstd(x):
    return lax.rsqrt(jnp.mean(x * x, axis=-1, keepdims=True) + NORM_EPS)


def _rms(x, g):
    return x * _rstd(x) * g


def _rms_bwd(x, g, dy):
    xh = x * _rstd(x)
    dyg = dy * g
    dx = _rstd(x) * (dyg - xh * jnp.mean(dyg * xh, axis=-1, keepdims=True))
    return dx, jnp.sum(dy * xh, axis=0, keepdims=True)


def _sigmoid(z):
    return 0.5 * jnp.tanh(0.5 * z) + 0.5


def _silu_and_grad(z):
    s = _sigmoid(z)
    return z * s, s * (1.0 + z * (1.0 - s))


def _softplus(z):
    return jnp.maximum(z, 0.0) + jnp.log(1.0 + jnp.exp(-jnp.abs(z)))


def _shift_rows(cur, before, after, d):
    ts = cur.shape[0]
    row = lax.broadcasted_iota(jnp.int32, (SUBLANES, cur.shape[1]), 0)
    out = pltpu.roll(cur, (-d) % ts, 0)
    if d < 0:
        edge = jnp.where(row < -d, pltpu.roll(before, (-d) % SUBLANES, 0), out[:SUBLANES])
        return jnp.concatenate([edge, out[SUBLANES:]], axis=0)
    edge = jnp.where(row >= SUBLANES - d, pltpu.roll(after, (-d) % SUBLANES, 0), out[ts - SUBLANES:])
    return jnp.concatenate([out[:ts - SUBLANES], edge], axis=0)


def _halo_specs(ts, s, width, col, tile=lambda i: i):
    per = ts // SUBLANES
    last = s // SUBLANES - 1
    return [
        pl.BlockSpec((ts, width), lambda i: (tile(i), col)),
        pl.BlockSpec((SUBLANES, width), lambda i: (jnp.maximum(tile(i) * per - 1, 0), col)),
        pl.BlockSpec((SUBLANES, width), lambda i: (jnp.minimum((tile(i) + 1) * per, last), col)),
    ]


def _halo_load(cur_ref, before_ref, after_ref, n_tiles, tile=lambda i: i):
    i = tile(pl.program_id(0))
    before = jnp.where(i > 0, before_ref[...], 0.0)
    after = jnp.where(i < n_tiles - 1, after_ref[...], 0.0)
    return cur_ref[...], before, after


def _full(shape):
    return pl.BlockSpec(shape, lambda *_: (0,) * len(shape))


def _peer(x, y, c, mask):
    px, py, pc = x ^ (mask >> 2), y ^ ((mask >> 1) & 1), c ^ (mask & 1)
    return (px, py, pc), 4 * px + 2 * py + pc


class Exchange:
    SIBLING = 1
    OTHER_CHIPS = (2, 4, 6)

    def __init__(self):
        self.args, self.out_shape, self._kinds = [], [], []

    def gather(self, block, columns=False, via_sibling=False):
        shape = (block.shape[0], N_DEV * block.shape[1]) if columns else (N_DEV,) + block.shape
        return self._add(block, shape, ("gather", columns, via_sibling))

    def scatter(self, stack, columns=False):
        shape = (N_DEV, stack.shape[0], stack.shape[1] // N_DEV) if columns else stack.shape
        return self._add(stack, shape, ("scatter", columns, False))

    def _add(self, arg, shape, kind):
        self.args.append(arg)
        self.out_shape.append(jax.ShapeDtypeStruct(shape, arg.dtype))
        self._kinds.append(kind)
        return len(self.args) - 1

    def semaphores(self):
        n = len(self.args)
        return [pltpu.SemaphoreType.DMA((n, N_DEV - 1)), pltpu.SemaphoreType.DMA((n, N_DEV - 1)), pltpu.SemaphoreType.DMA((n,))]

    def to_sibling(self, array):
        shape = (N_DEV // 2, array.shape[0], array.shape[1] // N_DEV)
        return self._add(array, shape, ("to_sibling", True, False))

    def among_chips(self, stack):
        return self._add(stack, stack.shape, ("among_chips", False, False))

    def _copies(self, position, in_refs, out_refs):
        x, y, c, me = position
        for arr, ((kind, columns, via_sibling), src, out) in enumerate(zip(self._kinds, in_refs, out_refs)):
            if kind == "to_sibling":
                width = src.shape[-1] // N_DEV
                for k in range(N_DEV // 2):
                    block = src.at[:, pl.ds(pl.multiple_of((2 * k + 1 - c) * width, LANES), width)]
                    yield arr, k + 1, block, out.at[k], out.at[k], False, self.SIBLING
                continue
            for mask in range(N_DEV):
                _, peer_id = _peer(x, y, c, mask)
                relayed = via_sibling and mask not in (0, self.SIBLING) + self.OTHER_CHIPS
                if kind == "among_chips":
                    if mask in (0,) + self.OTHER_CHIPS:
                        yield arr, mask, src.at[peer_id // 2], out.at[me // 2], out.at[peer_id // 2], False, mask
                elif kind == "gather":
                    if columns:
                        width = src.shape[-1]
                        yield (arr, mask, src, out.at[:, pl.ds(pl.multiple_of(me * width, LANES), width)],
                               out.at[:, pl.ds(pl.multiple_of(peer_id * width, LANES), width)], relayed, mask)
                    else:
                        yield arr, mask, src, out.at[me], out.at[peer_id], relayed, mask
                else:
                    if columns:
                        width = src.shape[-1] // N_DEV
                        block = src.at[:, pl.ds(pl.multiple_of(peer_id * width, LANES), width)]
                    else:
                        block = src.at[peer_id]
                    yield arr, mask, block, out.at[me], out.at[peer_id], False, mask

    def _remote(self, position, sems, arr, slot, to_mask, src, dst):
        x, y, c, _ = position
        return pltpu.make_async_remote_copy(src_ref=src, dst_ref=dst, send_sem=sems[0].at[arr, slot - 1], recv_sem=sems[1].at[arr, slot - 1],
                                            device_id=_peer(x, y, c, to_mask)[0], device_id_type=MESH_ID)

    def start(self, position, in_refs, out_refs, sems):
        for arr, slot, src, dst, _, relayed, to_mask in self._copies(position, in_refs, out_refs):
            if slot == 0:
                pltpu.make_async_copy(src, dst, sems[2].at[arr]).start()
            elif not relayed:
                self._remote(position, sems, arr, slot, to_mask, src, dst).start()

    def wait(self, position, in_refs, out_refs, sems):
        copies = list(self._copies(position, in_refs, out_refs))
        landings = {(arr, slot): landing for arr, slot, _, _, landing, _, _ in copies}
        passed_on = set()
        for arr, mask, src, _, landing, relayed, _ in copies:
            if relayed:
                held = landings[arr, mask ^ self.SIBLING]
                self._remote(position, sems, arr, mask ^ self.SIBLING, mask ^ self.SIBLING, src, held).wait_recv()
                self._remote(position, sems, arr, mask, self.SIBLING, held, held).start()
                passed_on.add((arr, mask ^ self.SIBLING))
        for arr, slot, src, dst, landing, relayed, to_mask in copies:
            if slot == 0:
                pltpu.make_async_copy(src, dst, sems[2].at[arr]).wait()
                continue
            if (arr, slot) not in passed_on:
                self._remote(position, sems, arr, slot, to_mask, src, landing).wait_recv()
            if relayed:
                held = landings[arr, slot ^ self.SIBLING]
                self._remote(position, sems, arr, slot, self.SIBLING, held, held).wait_send()
            else:
                self._remote(position, sems, arr, slot, to_mask, src, dst).wait_send()


def _call(body, *, name, grid, in_specs, out_specs, out_shape, args, scratch_shapes=(), exchange=None):
    single = not isinstance(out_shape, (list, tuple))
    if single:
        out_specs, out_shape = [out_specs], [out_shape]
    params = _params(len(grid))
    if exchange is None:
        outs = pl.pallas_call(body, name=name, grid=grid, in_specs=in_specs, out_specs=out_specs, out_shape=out_shape,
                              scratch_shapes=list(scratch_shapes), compiler_params=params)(*args)
        return outs[0] if single else outs
    counts = (len(args), len(exchange.args), len(out_shape), len(exchange.out_shape), len(scratch_shapes), 3)

    def wrapped(*refs):
        groups, at = [], 0
        for n in counts:
            groups.append(refs[at:at + n])
            at += n
        main_in, ex_in, main_out, ex_out, main_scratch, sems = groups
        x, y, c = lax.axis_index("x"), lax.axis_index("y"), lax.axis_index("c")
        position = (x, y, c, 4 * x + 2 * y + c)
        ids = [pl.program_id(a) for a in range(len(grid))]
        first = functools.reduce(jnp.logical_and, [i == 0 for i in ids])
        last = functools.reduce(jnp.logical_and, [i == g - 1 for i, g in zip(ids, grid)])

        @pl.when(first)
        def _():
            exchange.start(position, ex_in, ex_out, sems)

        body(*main_in, *main_out, *main_scratch)

        @pl.when(last)
        def _():
            exchange.wait(position, ex_in, ex_out, sems)

    hbm = pl.BlockSpec(memory_space=pl.ANY)
    outs = pl.pallas_call(
        wrapped, name=name, grid=grid, in_specs=list(in_specs) + [hbm] * counts[1], out_specs=list(out_specs) + [hbm] * counts[3],
        out_shape=list(out_shape) + exchange.out_shape, scratch_shapes=list(scratch_shapes) + exchange.semaphores(),
        compiler_params=params)(*args, *exchange.args)
    main = outs[:counts[2]]
    return (main[0] if single else main), outs[counts[2]:]


def run_exchange(exchange, name):
    return _call(lambda: None, name=name, grid=(1,), in_specs=[], out_specs=[], out_shape=[], args=[], exchange=exchange)[1]


def gather_matmul(x, g, w_block, small_block, tm):
    s, d = x.shape
    width = w_block.shape[1]
    pair = 2 * width
    n_chips = N_DEV // 2
    tm = min(tm, s)
    n_i = s // tm
    sibling = Exchange.SIBLING

    def body(chips_ref, x_ref, g_ref, wb_ref, sb_ref, proj_ref, h_ref, w_ref, small_ref, h_all, w_pair, send, recv, local, load_sem):
        j, i = pl.program_id(0), pl.program_id(1)
        xx, yy, cc = lax.axis_index("x"), lax.axis_index("y"), lax.axis_index("c")
        me = 4 * xx + 2 * yy + cc

        def block_of(dev):
            return w_ref.at[:, pl.ds(pl.multiple_of(dev * width, LANES), width)]

        def remote(arr, slot, to_mask, src, dst):
            return pltpu.make_async_remote_copy(src_ref=src, dst_ref=dst, send_sem=send.at[arr, slot - 1], recv_sem=recv.at[arr, slot - 1],
                                                device_id=_peer(xx, yy, cc, to_mask)[0], device_id_type=MESH_ID)

        @pl.when((j == 0) & (i == 0))
        def _():
            pltpu.make_async_copy(wb_ref, block_of(me), local.at[0]).start()
            pltpu.make_async_copy(sb_ref, small_ref.at[me], local.at[1]).start()
            for mask in (sibling,) + Exchange.OTHER_CHIPS[:-1]:
                remote(0, mask, mask, wb_ref, block_of(me)).start()
            for mask in range(1, N_DEV):
                remote(1, mask, mask, sb_ref, small_ref.at[me]).start()

        for step in range(n_chips):
            @pl.when((j == step) & (i == 0))
            def _(step=step):
                if step == 0:
                    pltpu.make_async_copy(wb_ref, block_of(me), local.at[0]).wait()
                    remote(0, sibling, sibling, wb_ref, block_of(me ^ sibling)).wait_recv()
                else:
                    mask = 2 * step
                    remote(0, mask, mask, wb_ref, block_of(me ^ mask)).wait_recv()
                    if step == 1:
                        remote(0, Exchange.OTHER_CHIPS[-1], Exchange.OTHER_CHIPS[-1], wb_ref, block_of(me)).start()
                    remote(0, mask | sibling, sibling, block_of(me ^ mask), block_of(me ^ mask)).start()
                    remote(0, mask | sibling, mask | sibling, wb_ref, block_of(me ^ (mask | sibling))).wait_recv()
                load = pltpu.make_async_copy(w_ref.at[:, pl.ds(pl.multiple_of(chips_ref[step] * pair, LANES), pair)], w_pair, load_sem)
                load.start()
                load.wait()

        rows = pl.ds(pl.multiple_of(i * tm, tm), tm)

        @pl.when(j == 0)
        def _():
            h = _rms(x_ref[...], g_ref[...]).astype(BF16)
            h_all[rows, :] = h
            h_ref[...] = h

        proj_ref[...] = jnp.dot(h_all[rows, :], w_pair[...], preferred_element_type=F32)

        @pl.when((j == n_chips - 1) & (i == n_i - 1))
        def _():
            pltpu.make_async_copy(sb_ref, small_ref.at[me], local.at[1]).wait()
            for mask in range(1, N_DEV):
                remote(1, mask, mask, sb_ref, small_ref.at[me ^ mask]).wait_recv()
                remote(1, mask, mask, sb_ref, small_ref.at[me]).wait_send()
            for mask in (sibling,) + Exchange.OTHER_CHIPS:
                remote(0, mask, mask, wb_ref, block_of(me)).wait_send()
            for mask in Exchange.OTHER_CHIPS:
                remote(0, mask | sibling, sibling, block_of(me ^ mask), block_of(me ^ mask)).wait_send()

    def first_pass_row(j, i, chips):
        return jnp.where(j == 0, i, n_i - 1), 0

    hbm = pl.BlockSpec(memory_space=pl.ANY)
    my_chip = 2 * lax.axis_index("x") + lax.axis_index("y")
    chips = (my_chip ^ jnp.arange(n_chips)).astype(jnp.int32)
    grid_spec = pltpu.PrefetchScalarGridSpec(
        num_scalar_prefetch=1, grid=(n_chips, n_i),
        in_specs=[pl.BlockSpec((tm, d), first_pass_row), pl.BlockSpec((1, d), lambda j, i, chips: (0, 0)), hbm, hbm],
        out_specs=[pl.BlockSpec((tm, pair), lambda j, i, chips: (i, chips[j])), pl.BlockSpec((tm, d), first_pass_row), hbm, hbm],
        scratch_shapes=[pltpu.VMEM((s, d), BF16), pltpu.VMEM((d, pair), BF16), pltpu.SemaphoreType.DMA((2, N_DEV - 1)),
                        pltpu.SemaphoreType.DMA((2, N_DEV - 1)), pltpu.SemaphoreType.DMA((2,)), pltpu.SemaphoreType.DMA(())])
    return pl.pallas_call(
        body, name="even_in", grid_spec=grid_spec,
        out_shape=[jax.ShapeDtypeStruct((s, N_DEV * width), F32), jax.ShapeDtypeStruct((s, d), BF16),
                   jax.ShapeDtypeStruct((d, N_DEV * width), w_block.dtype), jax.ShapeDtypeStruct((N_DEV,) + small_block.shape, small_block.dtype)],
        compiler_params=_params(2),
    )(chips, x, g, w_block, small_block)


def rms_matmul(x, g, w, tm, tn, name, exchange=None):
    s, d = x.shape
    n = w.shape[1]
    tm = min(tm, s)

    def body(x_ref, g_ref, w_ref, o_ref, h_ref):
        @pl.when(pl.program_id(1) == 0)
        def _():
            h_ref[...] = _rms(x_ref[...], g_ref[...]).astype(BF16)

        o_ref[...] = jnp.dot(h_ref[...], w_ref[...], preferred_element_type=F32)

    return _call(
        body, name=name, grid=(s // tm, n // tn),
        in_specs=[pl.BlockSpec((tm, d), lambda i, j: (i, 0)), _full((1, d)), pl.BlockSpec((d, tn), lambda i, j: (0, j))],
        out_specs=[pl.BlockSpec((tm, tn), lambda i, j: (i, j)), pl.BlockSpec((tm, d), lambda i, j: (i, 0))],
        out_shape=[jax.ShapeDtypeStruct((s, n), F32), jax.ShapeDtypeStruct((s, d), BF16)],
        args=[x, g, w], exchange=exchange)


def matmul_post(u, w, xres, g, name, target=None):
    s, k = u.shape
    d = w.shape[1]
    tm = min(MM_TILE, s)
    with_loss = target is not None

    def body(*refs):
        if with_loss:
            u_ref, w_ref, x_ref, g_ref, t_ref, y_ref, dout_ref, loss_ref = refs
        else:
            u_ref, w_ref, x_ref, g_ref, y_ref, out_ref = refs
        y = jnp.dot(u_ref[...], w_ref[...], preferred_element_type=F32)
        y_ref[...] = y
        out = x_ref[...] + _rms(y, g_ref[...])
        if with_loss:
            @pl.when(pl.program_id(0) == 0)
            def _():
                loss_ref[...] = jnp.zeros_like(loss_ref)

            diff = out - t_ref[...]
            dout_ref[...] = diff * (1.0 / d)
            loss_ref[...] += 0.5 * jnp.sum(jnp.mean(diff * diff, axis=-1, keepdims=True))
        else:
            out_ref[...] = out

    row = pl.BlockSpec((tm, d), lambda i: (i, 0))
    in_specs = [pl.BlockSpec((tm, k), lambda i: (i, 0)), _full((k, d)), row, _full((1, d))]
    args = [u, w, xres, g]
    out_specs = [row, row]
    out_shape = [jax.ShapeDtypeStruct((s, d), F32), jax.ShapeDtypeStruct((s, d), F32)]
    if with_loss:
        in_specs.append(row)
        args.append(target)
        out_specs.append(_full((SUBLANES, LANES)))
        out_shape.append(jax.ShapeDtypeStruct((SUBLANES, LANES), F32))
    return pl.pallas_call(body, name=name, grid=(s // tm,), in_specs=in_specs, out_specs=out_specs,
                          out_shape=out_shape, compiler_params=_params(1))(*args)


def normbwd_matmul_nt(y, g, dout, w, tn, name, exchange=None):
    s, d = y.shape
    n = w.shape[0]
    tm = min(MM_TILE, s)

    def body(y_ref, g_ref, dout_ref, w_ref, du_ref, dy_ref, dg_ref):
        i, j = pl.program_id(0), pl.program_id(1)

        @pl.when(j == 0)
        def _():
            dy, dg = _rms_bwd(y_ref[...], g_ref[...], dout_ref[...])
            dy_ref[...] = dy.astype(BF16)

            @pl.when(i == 0)
            def _():
                dg_ref[...] = jnp.zeros_like(dg_ref)

            dg_ref[...] += dg

        du_ref[...] = lax.dot_general(dy_ref[...], w_ref[...], (((1,), (1,)), ((), ())), preferred_element_type=F32)

    row = pl.BlockSpec((tm, d), lambda i, j: (i, 0))
    return _call(
        body, name=name, grid=(s // tm, n // tn),
        in_specs=[row, _full((1, d)), row, pl.BlockSpec((tn, d), lambda i, j: (j, 0))],
        out_specs=[pl.BlockSpec((tm, tn), lambda i, j: (i, j)), row, _full((1, d))],
        out_shape=[jax.ShapeDtypeStruct((s, n), F32), jax.ShapeDtypeStruct((s, d), BF16), jax.ShapeDtypeStruct((1, d), F32)],
        args=[y, g, dout, w], exchange=exchange)


def matmul_tn(a, b, tm, tn, ts, out_dtype, name, exchange=None):
    s, m = a.shape
    n = b.shape[1]
    ts = min(ts, s)
    n_k = s // ts

    def body(a_ref, b_ref, o_ref, acc):
        k = pl.program_id(2)

        @pl.when(k == 0)
        def _():
            acc[...] = jnp.zeros_like(acc)

        acc[...] += lax.dot_general(a_ref[...], b_ref[...], (((0,), (0,)), ((), ())), preferred_element_type=F32)

        @pl.when(k == n_k - 1)
        def _():
            o_ref[...] = acc[...].astype(out_dtype)

    return _call(
        body, name=name, grid=(m // tm, n // tn, n_k),
        in_specs=[pl.BlockSpec((ts, tm), lambda i, j, k: (k, i)), pl.BlockSpec((ts, tn), lambda i, j, k: (k, j))],
        out_specs=pl.BlockSpec((tm, tn), lambda i, j, k: (i, j)),
        out_shape=jax.ShapeDtypeStruct((m, n), out_dtype),
        scratch_shapes=[pltpu.VMEM((tm, tn), F32)], args=[a, b], exchange=exchange)


def matmul_nt_normbwd(dproj, w, x, g, dres, tk, name, exchange=None):
    s, kt = dproj.shape
    d = w.shape[0]
    tm = min(MM_TILE, s)
    n_k = kt // tk

    def body(a_ref, w_ref, x_ref, g_ref, r_ref, dx_ref, dg_ref, acc):
        i, k = pl.program_id(0), pl.program_id(1)

        @pl.when(k == 0)
        def _():
            acc[...] = jnp.zeros_like(acc)

        acc[...] += lax.dot_general(a_ref[...], w_ref[...], (((1,), (1,)), ((), ())), preferred_element_type=F32)

        @pl.when(k == n_k - 1)
        def _():
            dx, dg = _rms_bwd(x_ref[...], g_ref[...], acc[...])
            dx_ref[...] = r_ref[...] + dx

            @pl.when(i == 0)
            def _():
                dg_ref[...] = jnp.zeros_like(dg_ref)

            dg_ref[...] += dg

    row = pl.BlockSpec((tm, d), lambda i, k: (i, 0))
    return _call(
        body, name=name, grid=(s // tm, n_k),
        in_specs=[pl.BlockSpec((tm, tk), lambda i, k: (i, k)), pl.BlockSpec((d, tk), lambda i, k: (0, k)), row, _full((1, d)), row],
        out_specs=[row, _full((1, d))],
        out_shape=[jax.ShapeDtypeStruct((s, d), F32), jax.ShapeDtypeStruct((1, d), F32)],
        scratch_shapes=[pltpu.VMEM((tm, d), F32)], args=[dproj, w, x, g, dres], exchange=exchange)


def _rg_conv(xa, before, after, cw, cb):
    return (cw[0:1, :] * _shift_rows(xa, before, after, -2) + cw[1:2, :] * _shift_rows(xa, before, after, -1)
            + cw[2:3, :] * xa + cw[3:4, :] * _shift_rows(xa, before, after, 1) + cb)


def _rg_gates(ua_h, gw_ref, gb_ref, c_h, direction, head):
    r = _sigmoid(_bdot(ua_h, gw_ref[2 * direction, head]) + gb_ref[2 * direction, head:head + 1, :])
    i = _sigmoid(_bdot(ua_h, gw_ref[2 * direction + 1, head]) + gb_ref[2 * direction + 1, head:head + 1, :])
    log_a = -c_h * r
    a = jnp.exp(log_a)
    beta_sq = -jnp.tanh(log_a) * (1.0 + a * a)
    inv_beta = lax.rsqrt(jnp.maximum(beta_sq, SMALLEST_NORMAL))
    return r, i, a, beta_sq * inv_beta, inv_beta


def even_gates_fwd(proj, conv_w, conv_b, gate_w, gate_b, lam, exchange=None):
    s = proj.shape[0]
    ts = min(2 * ROW_TILE, s)
    n_tiles = s // ts

    def body(xa_ref, xb_ref, xn_ref, cw_ref, cb_ref, gw_ref, gb_ref, lam_ref, o_ref, hf_ref, carry):
        @pl.when(pl.program_id(0) == 0)
        def _():
            carry[...] = jnp.zeros_like(carry)

        xa, before, after = _halo_load(xa_ref, xb_ref, xn_ref, n_tiles)
        ua = _rg_conv(xa, before, after, cw_ref[...], cb_ref[...])
        c = RG_C * _softplus(-lam_ref[...])
        for direction in range(2):
            for head in range(RG_HEADS):
                lanes = slice(head * RG_HEAD_DIM, (head + 1) * RG_HEAD_DIM)
                ua_h = ua[:, lanes]
                _, i, a, beta, _ = _rg_gates(ua_h, gw_ref, gb_ref, c[direction:direction + 1, lanes], direction, head)
                o_ref[2 * direction, :, lanes] = a
                o_ref[2 * direction + 1, :, lanes] = beta * (i * ua_h)
        _scan_tile(o_ref.at[0], o_ref.at[1], hf_ref, carry, False, False)

    return _call(
        body, name="even_gates_fwd", grid=(n_tiles,),
        in_specs=_halo_specs(ts, s, D_MODEL, 0) + [_full(conv_w.shape), _full(conv_b.shape), _full(gate_w.shape),
                                                   _full(gate_b.shape), _full(lam.shape)],
        out_specs=[pl.BlockSpec((4, ts, D_MODEL), lambda i: (0, i, 0)), pl.BlockSpec((ts, D_MODEL), lambda i: (i, 0))],
        out_shape=[jax.ShapeDtypeStruct((4, s, D_MODEL), F32), jax.ShapeDtypeStruct((s, D_MODEL), F32)],
        scratch_shapes=[pltpu.VMEM((SUBLANES, D_MODEL), F32)],
        args=[proj, proj, proj, conv_w, conv_b, gate_w, gate_b, lam], exchange=exchange)


def _scan_tile(a_ref, b_ref, h_ref, carry, reverse, b_times_a):
    ts, c = h_ref.shape
    n_blocks = ts // SUBLANES
    row = lax.broadcasted_iota(jnp.int32, (SUBLANES, c), 0)

    def block(j, h_in):
        r0 = pl.multiple_of((n_blocks - 1 - j if reverse else j) * SUBLANES, SUBLANES)
        a = a_ref[pl.ds(r0, SUBLANES), :]
        b = b_ref[pl.ds(r0, SUBLANES), :]
        if b_times_a:
            b = a * b
        for step in (1, 2, 4):
            shift = SUBLANES - step if reverse else step
            valid = row < SUBLANES - step if reverse else row >= step
            b = jnp.where(valid, a * pltpu.roll(b, shift, 0) + b, b)
            a = jnp.where(valid, a * pltpu.roll(a, shift, 0), a)
        h = a * h_in + b
        h_ref[pl.ds(r0, SUBLANES), :] = h
        return h[0:1, :] if reverse else h[SUBLANES - 1:SUBLANES, :]

    carry[0:1, :] = lax.fori_loop(0, n_blocks, block, carry[0:1, :])


def linear_scan(a_arr, a_idx, b_arr, b_idx, reverse, b_times_a, name, exchange=None):
    _, s, c = a_arr.shape
    ts = min(MM_TILE, s)
    n_tiles = s // ts

    def tile_of(i):
        return n_tiles - 1 - i if reverse else i

    def body(a_ref, b_ref, h_ref, carry):
        @pl.when(pl.program_id(0) == 0)
        def _():
            carry[...] = jnp.zeros_like(carry)

        _scan_tile(a_ref, b_ref, h_ref, carry, reverse, b_times_a)

    return _call(
        body, name=name, grid=(n_tiles,),
        in_specs=[pl.BlockSpec((None, ts, c), lambda i: (a_idx, tile_of(i), 0)),
                  pl.BlockSpec((None, ts, c), lambda i: (b_idx, tile_of(i), 0))],
        out_specs=pl.BlockSpec((ts, c), lambda i: (tile_of(i), 0)),
        out_shape=jax.ShapeDtypeStruct((s, c), F32),
        scratch_shapes=[pltpu.VMEM((SUBLANES, c), F32)], args=[a_arr, b_arr], exchange=exchange)


def _sc_conv(p, before, after, w):
    return w[0:1, :] * _shift_rows(p, before, after, -1) + w[1:2, :] * p + w[2:3, :] * _shift_rows(p, before, after, 1)


def even_mix_fwd(ab, hf, proj, sc_w, exchange=None):
    s = proj.shape[0]
    ts = min(2 * ROW_TILE, s)
    n_tiles = s // ts

    def tile(i):
        return n_tiles - 1 - i

    row = pl.BlockSpec((ts, D_MODEL), lambda i: (tile(i), 0))

    def col(c):
        return pl.BlockSpec((ts, D_MODEL), lambda i: (tile(i), c))

    def body(a_ref, b_ref, hf_ref, za_ref, xb_ref, xbb_ref, xbn_ref, gb_ref, gc_ref, gcb_ref, gcn_ref, zb_ref, w_ref,
             u_ref, hb_ref, carry):
        @pl.when(pl.program_id(0) == 0)
        def _():
            carry[...] = jnp.zeros_like(carry)

        _scan_tile(a_ref, b_ref, hb_ref, carry, True, False)
        xb, xb_before, xb_after = _halo_load(xb_ref, xbb_ref, xbn_ref, n_tiles, tile)
        gc, gc_before, gc_after = _halo_load(gc_ref, gcb_ref, gcn_ref, n_tiles, tile)
        silu_za, _ = _silu_and_grad(za_ref[...])
        silu_zb, _ = _silu_and_grad(zb_ref[...])
        u_ref[:, :D_MODEL] = ((hf_ref[...] + hb_ref[...]) * silu_za).astype(BF16)
        cv = _sc_conv(gc * xb, gc_before * xb_before, gc_after * xb_after, w_ref[...])
        u_ref[:, D_MODEL:] = (gb_ref[...] * cv * silu_zb).astype(BF16)

    return _call(
        body, name="even_mix_fwd", grid=(n_tiles,),
        in_specs=[pl.BlockSpec((None, ts, D_MODEL), lambda i: (2, tile(i), 0)), pl.BlockSpec((None, ts, D_MODEL), lambda i: (3, tile(i), 0)),
                  row, col(1)] + _halo_specs(ts, s, D_MODEL, 2, tile) + [col(3)] + _halo_specs(ts, s, D_MODEL, 4, tile)
        + [col(5), _full(sc_w.shape)],
        out_specs=[pl.BlockSpec((ts, 2 * D_MODEL), lambda i: (tile(i), 0)), row],
        out_shape=[jax.ShapeDtypeStruct((s, 2 * D_MODEL), BF16), jax.ShapeDtypeStruct((s, D_MODEL), F32)],
        scratch_shapes=[pltpu.VMEM((SUBLANES, D_MODEL), F32)],
        args=[ab, ab, hf, proj, proj, proj, proj, proj, proj, proj, proj, proj, sc_w], exchange=exchange)


def even_mix_bwd(du, hf, hb, proj, sc_w, ab, exchange=None):
    s = proj.shape[0]
    ts = min(ROW_TILE, s)
    n_tiles = s // ts
    row = pl.BlockSpec((ts, D_MODEL), lambda i: (i, 0))

    def body(dya_ref, dyb_ref, dybb_ref, dybn_ref, hf_ref, hb_ref, za_ref, xb_ref, xbb_ref, xbn_ref,
             gb_ref, gbb_ref, gbn_ref, gc_ref, gcb_ref, gcn_ref, zb_ref, zbb_ref, zbn_ref, w_ref, a_ref,
             dh_ref, dp_ref, dw_ref, adj_ref, carry):
        @pl.when(pl.program_id(0) == 0)
        def _():
            carry[...] = jnp.zeros_like(carry)

        dyb, dyb_before, dyb_after = _halo_load(dyb_ref, dybb_ref, dybn_ref, n_tiles)
        xb, xb_before, xb_after = _halo_load(xb_ref, xbb_ref, xbn_ref, n_tiles)
        gb, gb_before, gb_after = _halo_load(gb_ref, gbb_ref, gbn_ref, n_tiles)
        gc, gc_before, gc_after = _halo_load(gc_ref, gcb_ref, gcn_ref, n_tiles)
        zb, zb_before, zb_after = _halo_load(zb_ref, zbb_ref, zbn_ref, n_tiles)
        w = w_ref[...]
        dya, za = dya_ref[...], za_ref[...]
        silu_za, dsilu_za = _silu_and_grad(za)
        dh_ref[...] = dya * silu_za
        _scan_tile(a_ref, dh_ref, adj_ref, carry, False, True)
        dp_ref[:, 0:D_MODEL] = (dya * (hf_ref[...] + hb_ref[...]) * dsilu_za).astype(BF16)

        silu_zb, dsilu_zb = _silu_and_grad(zb)
        p, p_before, p_after = gc * xb, gc_before * xb_before, gc_after * xb_after
        cv = _sc_conv(p, p_before, p_after, w)
        dcv = dyb * gb * silu_zb
        dcv_before = dyb_before * gb_before * _silu_and_grad(zb_before)[0]
        dcv_after = dyb_after * gb_after * _silu_and_grad(zb_after)[0]
        dpp = (w[0:1, :] * _shift_rows(dcv, dcv_before, dcv_after, 1) + w[1:2, :] * dcv
               + w[2:3, :] * _shift_rows(dcv, dcv_before, dcv_after, -1))
        dp_ref[:, D_MODEL:2 * D_MODEL] = (dpp * gc).astype(BF16)
        dp_ref[:, 2 * D_MODEL:3 * D_MODEL] = (dyb * cv * silu_zb).astype(BF16)
        dp_ref[:, 3 * D_MODEL:4 * D_MODEL] = (dpp * xb).astype(BF16)
        dp_ref[:, 4 * D_MODEL:5 * D_MODEL] = (dyb * gb * cv * dsilu_zb).astype(BF16)

        @pl.when(pl.program_id(0) == 0)
        def _():
            dw_ref[...] = jnp.zeros_like(dw_ref)

        dw_ref[0:1, :] += jnp.sum(dcv * _shift_rows(p, p_before, p_after, -1), axis=0, keepdims=True)
        dw_ref[1:2, :] += jnp.sum(dcv * p, axis=0, keepdims=True)
        dw_ref[2:3, :] += jnp.sum(dcv * _shift_rows(p, p_before, p_after, 1), axis=0, keepdims=True)

    return _call(
        body, name="even_mix_bwd", grid=(n_tiles,),
        in_specs=[row] + _halo_specs(ts, s, D_MODEL, 1) + [row, row, pl.BlockSpec((ts, D_MODEL), lambda i: (i, 1))]
        + _halo_specs(ts, s, D_MODEL, 2) + _halo_specs(ts, s, D_MODEL, 3) + _halo_specs(ts, s, D_MODEL, 4)
        + _halo_specs(ts, s, D_MODEL, 5) + [_full(sc_w.shape), pl.BlockSpec((None, ts, D_MODEL), lambda i: (2, i, 0))],
        out_specs=[row, pl.BlockSpec((ts, 5 * D_MODEL), lambda i: (i, 0)), _full(sc_w.shape), row],
        out_shape=[jax.ShapeDtypeStruct((s, D_MODEL), F32), jax.ShapeDtypeStruct((s, 5 * D_MODEL), BF16),
                   jax.ShapeDtypeStruct(sc_w.shape, F32), jax.ShapeDtypeStruct((s, D_MODEL), F32)],
        scratch_shapes=[pltpu.VMEM((SUBLANES, D_MODEL), F32)],
        args=[du, du, du, du, hf, hb, proj, *([proj] * 12), sc_w, ab], exchange=exchange)


def even_gates_bwd(proj, adj_f, adj_b, hf, hb, dh, conv_w, conv_b, gate_w, gate_b, lam, exchange=None):
    s = proj.shape[0]
    ts = min(2 * ROW_TILE, s)
    n_tiles = s // ts
    row = pl.BlockSpec((ts, D_MODEL), lambda i: (i, 0))

    def body(xa_ref, xab_ref, xan_ref, af_ref, afb_ref, afn_ref, ab_ref, abb_ref, abn_ref,
             hf_ref, hfb_ref, hfn_ref, hb_ref, hbb_ref, hbn_ref, dh_ref,
             cw_ref, cb_ref, gw_ref, gb_ref, lam_ref, dua_ref, dgw_ref, dgb_ref, dlam_ref):
        @pl.when(pl.program_id(0) == 0)
        def _():
            dgw_ref[...] = jnp.zeros_like(dgw_ref)
            dgb_ref[...] = jnp.zeros_like(dgb_ref)
            dlam_ref[...] = jnp.zeros_like(dlam_ref)

        xa, before, after = _halo_load(xa_ref, xab_ref, xan_ref, n_tiles)
        ua = _rg_conv(xa, before, after, cw_ref[...], cb_ref[...])
        lam_v = lam_ref[...]
        c = RG_C * _softplus(-lam_v)
        dc_dlam = -RG_C * _sigmoid(-lam_v)
        dh = dh_ref[...]
        adj = (_halo_load(af_ref, afb_ref, afn_ref, n_tiles), _halo_load(ab_ref, abb_ref, abn_ref, n_tiles))
        hs = (_halo_load(hf_ref, hfb_ref, hfn_ref, n_tiles), _halo_load(hb_ref, hbb_ref, hbn_ref, n_tiles))
        dua = jnp.zeros_like(ua)
        for direction in range(2):
            step = 1 if direction == 0 else -1
            g = dh + _shift_rows(*adj[direction], step)
            da_all = g * _shift_rows(*hs[direction], -step)
            dua_parts = []
            for head in range(RG_HEADS):
                lanes = slice(head * RG_HEAD_DIM, (head + 1) * RG_HEAD_DIM)
                ua_h = ua[:, lanes]
                c_h = c[direction:direction + 1, lanes]
                r, i, a, beta, inv_beta = _rg_gates(ua_h, gw_ref, gb_ref, c_h, direction, head)
                db = g[:, lanes]
                d_i = db * beta * ua_h
                dbeta = db * (i * ua_h)
                dlog_a = (da_all[:, lanes] - dbeta * a * inv_beta) * a
                dpr = -c_h * dlog_a * r * (1.0 - r)
                dpi = d_i * i * (1.0 - i)
                dua_parts.append(db * beta * i + _bdot_nt(dpr, gw_ref[2 * direction, head])
                                 + _bdot_nt(dpi, gw_ref[2 * direction + 1, head]))
                dgw_ref[2 * direction, head] += _bdot_tn(ua_h, dpr)
                dgw_ref[2 * direction + 1, head] += _bdot_tn(ua_h, dpi)
                dgb_ref[2 * direction, head:head + 1, :] += jnp.sum(dpr, axis=0, keepdims=True)
                dgb_ref[2 * direction + 1, head:head + 1, :] += jnp.sum(dpi, axis=0, keepdims=True)
                dlam_ref[direction:direction + 1, lanes] += (
                    jnp.sum(-r * dlog_a, axis=0, keepdims=True) * dc_dlam[direction:direction + 1, lanes])
            dua = dua + jnp.concatenate(dua_parts, axis=1)
        dua_ref[...] = dua

    return _call(
        body, name="even_gates_bwd", grid=(n_tiles,),
        in_specs=_halo_specs(ts, s, D_MODEL, 0) * 5 + [row] + [_full(conv_w.shape), _full(conv_b.shape), _full(gate_w.shape),
                                                             _full(gate_b.shape), _full(lam.shape)],
        out_specs=[row, _full(gate_w.shape), _full(gate_b.shape), _full(lam.shape)],
        out_shape=[jax.ShapeDtypeStruct((s, D_MODEL), F32), jax.ShapeDtypeStruct(gate_w.shape, F32),
                   jax.ShapeDtypeStruct(gate_b.shape, F32), jax.ShapeDtypeStruct(lam.shape, F32)],
        args=[proj, proj, proj, adj_f, adj_f, adj_f, adj_b, adj_b, adj_b, hf, hf, hf, hb, hb, hb, dh, conv_w, conv_b, gate_w,
              gate_b, lam], exchange=exchange)


def rg_conv_bwd(dua, proj, drest, conv_w, exchange=None):
    s = proj.shape[0]
    ts = min(2 * ROW_TILE, s)
    n_tiles = s // ts

    def body(du_ref, dub_ref, dun_ref, xa_ref, xab_ref, xan_ref, dr_ref, cw_ref, dp_ref, dw_ref, db_ref):
        @pl.when(pl.program_id(0) == 0)
        def _():
            dw_ref[...] = jnp.zeros_like(dw_ref)
            db_ref[...] = jnp.zeros_like(db_ref)

        dua, dua_before, dua_after = _halo_load(du_ref, dub_ref, dun_ref, n_tiles)
        xa, xa_before, xa_after = _halo_load(xa_ref, xab_ref, xan_ref, n_tiles)
        cw = cw_ref[...]
        dxa = (cw[0:1, :] * _shift_rows(dua, dua_before, dua_after, 2) + cw[1:2, :] * _shift_rows(dua, dua_before, dua_after, 1)
               + cw[2:3, :] * dua + cw[3:4, :] * _shift_rows(dua, dua_before, dua_after, -1))
        dp_ref[:, :D_MODEL] = dxa.astype(BF16)
        dp_ref[:, D_MODEL:] = dr_ref[...]
        for tap, offset in enumerate((-2, -1, 0, 1)):
            shifted = xa if offset == 0 else _shift_rows(xa, xa_before, xa_after, offset)
            dw_ref[tap:tap + 1, :] += jnp.sum(dua * shifted, axis=0, keepdims=True)
        db_ref[...] += jnp.sum(dua, axis=0, keepdims=True)

    return _call(
        body, name="rg_conv_bwd", grid=(n_tiles,),
        in_specs=_halo_specs(ts, s, D_MODEL, 0) * 2 + [pl.BlockSpec((ts, 5 * D_MODEL), lambda i: (i, 0)), _full(conv_w.shape)],
        out_specs=[pl.BlockSpec((ts, EVEN_IN), lambda i: (i, 0)), _full(conv_w.shape), _full((1, D_MODEL))],
        out_shape=[jax.ShapeDtypeStruct((s, EVEN_IN), BF16), jax.ShapeDtypeStruct(conv_w.shape, F32),
                   jax.ShapeDtypeStruct((1, D_MODEL), F32)],
        args=[dua, dua, dua, proj, proj, proj, drest, conv_w], exchange=exchange)


def _split3(x):
    x1 = x.astype(BF16)
    rest = x - x1.astype(F32)
    x2 = rest.astype(BF16)
    return x1, x2, (rest - x2.astype(F32)).astype(BF16)


def _chunk_sum_matrix(t, reverse, transpose):
    i = lax.broadcasted_iota(jnp.int32, (t, t), 0)
    j = lax.broadcasted_iota(jnp.int32, (t, t), 1)
    if transpose:
        i, j = j, i
    same = (i // GLA_CHUNK) == (j // GLA_CHUNK)
    return jnp.where(same & ((j >= i) if reverse else (j <= i)), 1.0, 0.0).astype(BF16)


def _exact_dot(m, x):
    return sum(jnp.dot(m, part, preferred_element_type=F32) for part in _split3(x))


def _chunk_mask(t, reverse):
    i = lax.broadcasted_iota(jnp.int32, (t, t), 0)
    j = lax.broadcasted_iota(jnp.int32, (t, t), 1)
    return ((i // GLA_CHUNK) == (j // GLA_CHUNK)) & ((j >= i) if reverse else (j <= i))


def _chunk_rows(c):
    return slice(c * GLA_CHUNK, (c + 1) * GLA_CHUNK)


def _gla_gate(lr, wg, bg):
    z = _bdot(lr, wg) + bg
    log_alpha = (jnp.minimum(z, 0.0) - jnp.log(1.0 + jnp.exp(-jnp.abs(z)))) * (1.0 / GLA_NORMALIZER)
    return z, log_alpha


def _gla_tile_terms(q, k, bcum, reverse):
    n_chunks = q.shape[0] // GLA_CHUNK
    totals = []
    for c in range(n_chunks):
        edge = c * GLA_CHUNK if reverse else (c + 1) * GLA_CHUNK - 1
        totals.append(bcum[edge:edge + 1, :])
    btot = jnp.concatenate([jnp.broadcast_to(total, (GLA_CHUNK, total.shape[1])) for total in totals], axis=0)
    e_pos, e_neg, e_st = jnp.exp(bcum), jnp.exp(-bcum), jnp.exp(btot - bcum)
    return q * (GLA_DK ** -0.5) * e_pos, k * e_neg, k * e_st, e_pos, e_neg, e_st, [jnp.exp(total) for total in totals]


def _gla_specs(t, n_tiles, reverse_order):
    def tile(i):
        return n_tiles - 1 - i if reverse_order else i

    return tile, [
        pl.BlockSpec((t, GLA_KEY), lambda i: (tile(i), 0)),
        pl.BlockSpec((t, GLA_KEY), lambda i: (tile(i), 1)),
        pl.BlockSpec((t, D_MODEL), lambda i: (tile(i), 1)),
        pl.BlockSpec((t, LANES), lambda i: (tile(i), (ODD_IN_PAD - LANES) // LANES)),
    ]


def gla_fwd(proj, wg, bg, reverse, o_other=None, gnorm=None):
    s = proj.shape[0]
    t = min(ROW_TILE, s)
    n_tiles = s // t
    n_chunks = t // GLA_CHUNK
    final = o_other is not None
    tile, specs = _gla_specs(t, n_tiles, reverse)

    def body(*refs):
        if final:
            q_ref, k_ref, v_ref, lr_ref, wg_ref, bg_ref, oo_ref, r_ref, gn_ref, osum_ref, u_ref, st_ref, state = refs
        else:
            q_ref, k_ref, v_ref, lr_ref, wg_ref, bg_ref, o_ref, st_ref, state = refs
            osum_ref = o_ref

        @pl.when(pl.program_id(0) == 0)
        def _():
            state[...] = jnp.zeros_like(state)

        _, log_alpha = _gla_gate(lr_ref[...], wg_ref[...], bg_ref[...])
        bcum = _exact_dot(_chunk_sum_matrix(t, reverse, False), log_alpha)
        q, k, v = q_ref[...], k_ref[...], v_ref[...]
        q_in, k_in, k_st, _, _, _, decays = _gla_tile_terms(q, k, bcum, reverse)
        mask = _chunk_mask(t, reverse)
        order = list(range(n_chunks))[::-1] if reverse else list(range(n_chunks))
        intra, increments = [], []
        for head in range(GLA_HEADS):
            kl = slice(head * GLA_DK, (head + 1) * GLA_DK)
            vl = slice(head * GLA_DV, (head + 1) * GLA_DV)
            scores = jnp.where(mask, _bdot_nt(q_in[:, kl], k_in[:, kl]), 0.0)
            intra.append(_bdot(scores, v[:, vl]))
            increments.append([_bdot_tn(v[_chunk_rows(c), vl], k_st[_chunk_rows(c), kl]) for c in range(n_chunks)])
        for head in range(GLA_HEADS):
            kl = slice(head * GLA_DK, (head + 1) * GLA_DK)
            vl = slice(head * GLA_DV, (head + 1) * GLA_DV)
            running = state[head]
            before = [None] * n_chunks
            for c in order:
                before[c] = running
                st_ref[c, head] = running
                running = running * decays[c][:, kl] + increments[head][c]
            state[head] = running
            inter = [_bdot_nt(q_in[_chunk_rows(c), kl], before[c]) for c in range(n_chunks)]
            osum_ref[:, vl] = intra[head] + jnp.concatenate(inter, axis=0)
        if final:
            osum = osum_ref[...] + oo_ref[...]
            osum_ref[...] = osum
            silu_r, _ = _silu_and_grad(r_ref[...])
            gn = gn_ref[...]
            for head in range(GLA_HEADS):
                vl = slice(head * GLA_DV, (head + 1) * GLA_DV)
                u_ref[:, vl] = (_rms(osum[:, vl], gn[:, vl]) * silu_r[:, vl]).astype(BF16)

    row = pl.BlockSpec((t, D_MODEL), lambda i: (tile(i), 0))
    st_spec = pl.BlockSpec((n_chunks, GLA_HEADS, GLA_DV, GLA_DK), lambda i: (tile(i), 0, 0, 0))
    st_shape = jax.ShapeDtypeStruct((s // GLA_CHUNK, GLA_HEADS, GLA_DV, GLA_DK), F32)
    in_specs = specs + [_full(wg.shape), _full(bg.shape)]
    args = [proj, proj, proj, proj, wg, bg]
    if final:
        in_specs += [row, pl.BlockSpec((t, D_MODEL), lambda i: (tile(i), 2)), _full(gnorm.shape)]
        args += [o_other, proj, gnorm]
        out_specs = [row, row, st_spec]
        out_shape = [jax.ShapeDtypeStruct((s, D_MODEL), F32), jax.ShapeDtypeStruct((s, D_MODEL), BF16), st_shape]
    else:
        out_specs = [row, st_spec]
        out_shape = [jax.ShapeDtypeStruct((s, D_MODEL), F32), st_shape]
    return pl.pallas_call(
        body, name="gla_fwd_rev" if reverse else "gla_fwd", grid=(n_tiles,), in_specs=in_specs, out_specs=out_specs,
        out_shape=out_shape, scratch_shapes=[pltpu.VMEM((GLA_HEADS, GLA_DV, GLA_DK), F32)], compiler_params=_params(1),
    )(*args)


def gla_out_bwd(du, proj, osum, gnorm):
    s = proj.shape[0]
    ts = min(2 * ROW_TILE, s)
    row = pl.BlockSpec((ts, D_MODEL), lambda i: (i, 0))

    def body(du_ref, r_ref, o_ref, gn_ref, do_ref, dr_ref, dgn_ref):
        @pl.when(pl.program_id(0) == 0)
        def _():
            dgn_ref[...] = jnp.zeros_like(dgn_ref)

        du, osum, gn = du_ref[...], o_ref[...], gn_ref[...]
        silu_r, dsilu_r = _silu_and_grad(r_ref[...])
        for head in range(GLA_HEADS):
            vl = slice(head * GLA_DV, (head + 1) * GLA_DV)
            o_h, g_h, du_h = osum[:, vl], gn[:, vl], du[:, vl]
            dr_ref[:, vl] = (du_h * _rms(o_h, g_h) * dsilu_r[:, vl]).astype(BF16)
            do_h, dg_h = _rms_bwd(o_h, g_h, du_h * silu_r[:, vl])
            do_ref[:, vl] = do_h
            dgn_ref[...] += dg_h

    return pl.pallas_call(
        body, name="gla_out_bwd", grid=(s // ts,),
        in_specs=[row, pl.BlockSpec((ts, D_MODEL), lambda i: (i, 2)), row, _full(gnorm.shape)],
        out_specs=[row, row, _full((1, GLA_DV))],
        out_shape=[jax.ShapeDtypeStruct((s, D_MODEL), F32), jax.ShapeDtypeStruct((s, D_MODEL), BF16),
                   jax.ShapeDtypeStruct((1, GLA_DV), F32)],
        compiler_params=_params(1),
    )(du, proj, osum, gnorm)


def gla_bwd(proj, wg, bg, do, states, reverse, first=None):
    s = proj.shape[0]
    t = min(ROW_TILE, s)
    n_tiles = s // t
    n_chunks = t // GLA_CHUNK
    final = first is not None
    tile, specs = _gla_specs(t, n_tiles, not reverse)

    def body(*refs):
        if final:
            (q_ref, k_ref, v_ref, lr_ref, wg_ref, bg_ref, do_ref, st_ref, dqkv1_ref, dlr1_ref, dr_ref,
             dp_ref, dwg_ref, dbg_ref, dstate, dqkv, dbc, dbt) = refs
        else:
            (q_ref, k_ref, v_ref, lr_ref, wg_ref, bg_ref, do_ref, st_ref,
             dqkv, dlr_ref, dwg_ref, dbg_ref, dstate, dbc, dbt) = refs

        @pl.when(pl.program_id(0) == 0)
        def _():
            dstate[...] = jnp.zeros_like(dstate)
            dwg_ref[...] = jnp.zeros_like(dwg_ref)
            dbg_ref[...] = jnp.zeros_like(dbg_ref)

        lr, wg_v = lr_ref[...], wg_ref[...]
        z, log_alpha = _gla_gate(lr, wg_v, bg_ref[...])
        bcum = _exact_dot(_chunk_sum_matrix(t, reverse, False), log_alpha)
        q, k, v, do_v = q_ref[...], k_ref[...], v_ref[...], do_ref[...]
        q_in, k_in, k_st, e_pos, e_neg, e_st, decays = _gla_tile_terms(q, k, bcum, reverse)
        mask = _chunk_mask(t, reverse)
        order = list(range(n_chunks)) if reverse else list(range(n_chunks))[::-1]
        dq_intra, dk_intra, dv_intra, increments = [], [], [], []
        for head in range(GLA_HEADS):
            kl = slice(head * GLA_DK, (head + 1) * GLA_DK)
            vl = slice(head * GLA_DV, (head + 1) * GLA_DV)
            scores = jnp.where(mask, _bdot_nt(q_in[:, kl], k_in[:, kl]), 0.0)
            dscores = jnp.where(mask, _bdot_nt(do_v[:, vl], v[:, vl]), 0.0)
            dv_intra.append(_bdot_tn(scores, do_v[:, vl]))
            dq_intra.append(_bdot(dscores, k_in[:, kl]))
            dk_intra.append(_bdot_tn(dscores, q_in[:, kl]))
            increments.append([_bdot_tn(do_v[_chunk_rows(c), vl], q_in[_chunk_rows(c), kl]) for c in range(n_chunks)])
        for head in range(GLA_HEADS):
            kl = slice(head * GLA_DK, (head + 1) * GLA_DK)
            vl = slice(head * GLA_DV, (head + 1) * GLA_DV)
            running = dstate[head]
            after, ddecay = [None] * n_chunks, [None] * n_chunks
            for c in order:
                after[c] = running
                ddecay[c] = jnp.sum(running * st_ref[c, head], axis=0, keepdims=True)
                running = running * decays[c][:, kl] + increments[head][c]
            dstate[head] = running
            dq_inter = jnp.concatenate([_bdot(do_v[_chunk_rows(c), vl], st_ref[c, head]) for c in range(n_chunks)], axis=0)
            dv_inter = jnp.concatenate([_bdot_nt(k_st[_chunk_rows(c), kl], after[c]) for c in range(n_chunks)], axis=0)
            dk_st = jnp.concatenate([_bdot(v[_chunk_rows(c), vl], after[c]) for c in range(n_chunks)], axis=0)
            dq_in = dq_intra[head] + dq_inter
            ks_h = k_st[:, kl]
            dqkv[:, 2 * GLA_KEY + head * GLA_DV:2 * GLA_KEY + (head + 1) * GLA_DV] = dv_intra[head] + dv_inter
            dqkv[:, kl] = dq_in * (GLA_DK ** -0.5) * e_pos[:, kl]
            dqkv[:, GLA_KEY + head * GLA_DK:GLA_KEY + (head + 1) * GLA_DK] = dk_intra[head] * e_neg[:, kl] + dk_st * e_st[:, kl]
            dbc[:, kl] = dq_in * q_in[:, kl] - dk_intra[head] * k_in[:, kl] - dk_st * ks_h
            weighted = dk_st * ks_h
            for c in range(n_chunks):
                dbtot = jnp.sum(weighted[_chunk_rows(c)], axis=0, keepdims=True) + ddecay[c] * decays[c][:, kl]
                dbt[_chunk_rows(c), kl] = jnp.broadcast_to(dbtot, (GLA_CHUNK, GLA_DK))
        dlog_alpha = _exact_dot(_chunk_sum_matrix(t, reverse, True), dbc[...]) + dbt[...]
        dz = dlog_alpha * _sigmoid(-z) * (1.0 / GLA_NORMALIZER)
        dlr = _bdot_nt(dz, wg_v)
        dwg_ref[...] += _bdot_tn(lr, dz)
        dbg_ref[...] += jnp.sum(dz, axis=0, keepdims=True)
        if final:
            dp_ref[:, :2 * D_MODEL] = (dqkv[...] + dqkv1_ref[...]).astype(BF16)
            dp_ref[:, 2 * D_MODEL:3 * D_MODEL] = dr_ref[...]
            dp_ref[:, 3 * D_MODEL:] = (dlr + dlr1_ref[...]).astype(BF16)
        else:
            dlr_ref[...] = dlr

    row = pl.BlockSpec((t, D_MODEL), lambda i: (tile(i), 0))
    wide = pl.BlockSpec((t, 2 * D_MODEL), lambda i: (tile(i), 0))
    narrow = pl.BlockSpec((t, LANES), lambda i: (tile(i), 0))
    st_spec = pl.BlockSpec((n_chunks, GLA_HEADS, GLA_DV, GLA_DK), lambda i: (tile(i), 0, 0, 0))
    in_specs = specs + [_full(wg.shape), _full(bg.shape), row, st_spec]
    args = [proj, proj, proj, proj, wg, bg, do, states]
    acc_specs = [_full(wg.shape), _full(bg.shape)]
    acc_shapes = [jax.ShapeDtypeStruct(wg.shape, F32), jax.ShapeDtypeStruct(bg.shape, F32)]
    scratch = [pltpu.VMEM((GLA_HEADS, GLA_DV, GLA_DK), F32)]
    work = [pltpu.VMEM((t, GLA_KEY), F32), pltpu.VMEM((t, GLA_KEY), F32)]
    if final:
        in_specs += [wide, narrow, row]
        args += list(first)
        out_specs = [pl.BlockSpec((t, ODD_IN_PAD), lambda i: (tile(i), 0))] + acc_specs
        out_shape = [jax.ShapeDtypeStruct((s, ODD_IN_PAD), BF16)] + acc_shapes
        scratch += [pltpu.VMEM((t, 2 * D_MODEL), F32)] + work
    else:
        out_specs = [wide, narrow] + acc_specs
        out_shape = [jax.ShapeDtypeStruct((s, 2 * D_MODEL), F32), jax.ShapeDtypeStruct((s, LANES), F32)] + acc_shapes
        scratch += work
    return pl.pallas_call(
        body, name="gla_bwd_rev" if reverse else "gla_bwd", grid=(n_tiles,), in_specs=in_specs, out_specs=out_specs,
        out_shape=out_shape, scratch_shapes=scratch, compiler_params=_params(1),
    )(*args)


def pair_sum(grad, from_sibling):
    n_chips, r, w = from_sibling.shape

    def body(even_ref, odd_ref, sib_ref, o_ref):
        mine = jnp.where(lax.axis_index("c") == 1, odd_ref[...], even_ref[...])
        o_ref[...] = (mine.astype(F32) + sib_ref[...].astype(F32)).astype(o_ref.dtype)

    return pl.pallas_call(
        body, name="pair_sum", grid=(n_chips,),
        in_specs=[pl.BlockSpec((r, w), lambda k: (0, 2 * k)), pl.BlockSpec((r, w), lambda k: (0, 2 * k + 1)),
                  pl.BlockSpec((None, r, w), lambda k: (k, 0, 0))],
        out_specs=pl.BlockSpec((None, r, w), lambda k: (k, 0, 0)),
        out_shape=jax.ShapeDtypeStruct(from_sibling.shape, from_sibling.dtype), compiler_params=_params(1),
    )(grad, grad, from_sibling)


def _adamw_update(g, w, m, v):
    new_m = ADAM_B1 * m + (1.0 - ADAM_B1) * g
    new_v = ADAM_B2 * v + (1.0 - ADAM_B2) * (g * g)
    m_hat = new_m / (1.0 - ADAM_B1 ** ADAM_STEP)
    v_hat = new_v / (1.0 - ADAM_B2 ** ADAM_STEP)
    return -ADAM_LR * (m_hat / (jnp.sqrt(v_hat) + ADAM_EPS) + ADAM_WD * w), new_m, new_v


def sum_parts(parts, name):
    _, r, c = parts.shape

    def body(p_ref, o_ref):
        total = p_ref[0].astype(F32)
        for j in range(1, N_DEV):
            total = total + p_ref[j].astype(F32)
        o_ref[...] = total

    return pl.pallas_call(body, name=name, in_specs=[_full(parts.shape)], out_specs=_full((r, c)), grid=(1,),
                          out_shape=jax.ShapeDtypeStruct((r, c), F32), compiler_params=_params(1))(parts)


def adamw(parts, w, m, v, name):
    n, r, c = parts.shape
    tr = r if r <= MM_TILE else ROW_TILE

    def body(p_ref, w_ref, m_ref, v_ref, g_ref, d_ref, nm_ref, nv_ref):
        g = p_ref[0].astype(F32)
        for j in range(1, n):
            g = g + p_ref[j].astype(F32)
        g_ref[...] = g
        d_ref[...], nm_ref[...], nv_ref[...] = _adamw_update(g, w_ref[...], m_ref[...], v_ref[...])

    row = pl.BlockSpec((tr, c), lambda i: (i, 0))
    return pl.pallas_call(
        body, name=name, grid=(r // tr,),
        in_specs=[pl.BlockSpec((n, tr, c), lambda i: (0, i, 0)), row, row, row], out_specs=[row] * 4,
        out_shape=[jax.ShapeDtypeStruct((r, c), F32)] * 4, compiler_params=_params(1),
    )(parts, w, m, v)


def _small_views(shape):
    if len(shape) == 2:
        return [((slice(None), slice(None)), (slice(None), slice(None)))]
    if len(shape) == 3:
        return [((slice(None), slice(None)), (0,))]
    rows = shape[2]
    return [((slice(k * rows, (k + 1) * rows), slice(None)), (0, k)) for k in range(shape[1])]


def adamw_small(landings, w, m, v):
    names = list(landings)
    n = len(names)
    shapes = [w[name].shape for name in names]

    def body(*refs):
        land, ws, ms, vs = refs[:n], refs[n:2 * n], refs[2 * n:3 * n], refs[3 * n:4 * n]
        outs = [refs[(4 + k) * n:(5 + k) * n] for k in range(4)]
        for k in range(n):
            total = land[k][0]
            for j in range(1, N_DEV):
                total = total + land[k][j]
            for rows, at in _small_views(shapes[k]):
                g = total[rows]
                outs[0][k][at] = g
                outs[1][k][at], outs[2][k][at], outs[3][k][at] = _adamw_update(g, ws[k][at], ms[k][at], vs[k][at])

    blocks = [_full(sh) for sh in shapes]
    outs = pl.pallas_call(
        body, name="adamw_small", grid=(1,),
        in_specs=[_full(landings[name].shape) for name in names] + blocks * 3, out_specs=blocks * 4,
        out_shape=[jax.ShapeDtypeStruct(sh, F32) for sh in shapes] * 4, compiler_params=_params(1),
    )(*[landings[name] for name in names], *[src[name] for src in (w, m, v) for name in names])
    return [dict(zip(names, outs[k * n:(k + 1) * n])) for k in range(4)]


def adamw_replicated(land_vec, land_gate_b, land_loss, names, w, m, v, gate_b):
    n = len(names)

    def body(*refs):
        vec_ref, gb_ref, loss_ref = refs[:3]
        ws, ms, vs = refs[3:3 + n], refs[3 + n:3 + 2 * n], refs[3 + 2 * n:3 + 3 * n]
        gw_ref, gm_ref, gv_ref = refs[3 + 3 * n:6 + 3 * n]
        outs = refs[6 + 3 * n:]
        vec, gb, loss = vec_ref[0], gb_ref[0], loss_ref[0]
        for j in range(1, N_DEV):
            vec, gb, loss = vec + vec_ref[j], gb + gb_ref[j], loss + loss_ref[j]
        for k in range(n):
            g = vec[k:k + 1, :]
            outs[k][...] = g
            outs[n + k][...], outs[2 * n + k][...], outs[3 * n + k][...] = _adamw_update(g, ws[k][...], ms[k][...], vs[k][...])
        outs[4 * n][...] = gb
        outs[4 * n + 1][...], outs[4 * n + 2][...], outs[4 * n + 3][...] = _adamw_update(gb, gw_ref[...], gm_ref[...], gv_ref[...])
        outs[4 * n + 4][...] = loss

    vec_block, gb_block = _full((1, D_MODEL)), _full(gate_b[0].shape)
    outs = pl.pallas_call(
        body, name="adamw_replicated", grid=(1,),
        in_specs=[_full(land_vec.shape), _full(land_gate_b.shape), _full(land_loss.shape)] + [vec_block] * (3 * n) + [gb_block] * 3,
        out_specs=[vec_block] * (4 * n) + [gb_block] * 4 + [_full(land_loss.shape[1:])],
        out_shape=[jax.ShapeDtypeStruct((1, D_MODEL), F32)] * (4 * n) + [jax.ShapeDtypeStruct(gate_b[0].shape, F32)] * 4
        + [jax.ShapeDtypeStruct(land_loss.shape[1:], F32)],
        compiler_params=_params(1),
    )(land_vec, land_gate_b, land_loss, *[src[name] for src in (w, m, v) for name in names], *gate_b)
    results = {name: [outs[k * n + i] for k in range(4)] for i, name in enumerate(names)}
    return results, outs[4 * n:4 * n + 4], outs[4 * n + 4]


SMALL_SHARDED = ("rg_conv_w", "rg_lambda", "sc_conv_w", "odd_norm_pre", "odd_norm_post", "gla_b_gate", "gla_norm_g", "gla_w_gate_lr")
SMALL_ROWS = {"rg_conv_w": (0, 4), "rg_lambda": (4, 2), "sc_conv_w": (6, 3), "odd_norm_pre": (9, 1), "odd_norm_post": (10, 1),
              "gla_b_gate": (11, 2), "gla_norm_g": (13, 1), "gla_w_gate_lr": (16, 32)}


def _pack_small(shards):
    pieces, at = [], 0
    for name in SMALL_SHARDED:
        start, rows = SMALL_ROWS[name]
        if start > at:
            pieces.append(jnp.zeros((start - at, LANES), F32))
        a = shards[name].reshape(rows, -1)
        pieces.append(jnp.pad(a, ((0, 0), (0, LANES - a.shape[1]))))
        at = start + rows
    return jnp.concatenate(pieces, axis=0)


def _unpack_gathered(g):
    def cols(name, width):
        start, rows = SMALL_ROWS[name]
        return jnp.transpose(g[:, start:start + rows, :width], (1, 0, 2)).reshape(rows, N_DEV * width)

    w_lr = cols("gla_w_gate_lr", GLA_KEY // N_DEV).reshape(2, GLA_RANK, GLA_KEY)
    return dict(rg_conv_w=cols("rg_conv_w", LANES), rg_lambda=cols("rg_lambda", LANES), sc_conv_w=cols("sc_conv_w", LANES),
                odd_norm_pre=cols("odd_norm_pre", LANES), odd_norm_post=cols("odd_norm_post", LANES),
                gla_b_gate=cols("gla_b_gate", GLA_KEY // N_DEV), gla_norm_g=cols("gla_norm_g", GLA_DV // N_DEV), gla_w_gate_lr=w_lr)


def _blocks_along_columns(a, rows):
    return jnp.transpose(a.reshape(rows, N_DEV, -1), (1, 0, 2))


def kernel(x, even_norm_pre, even_norm_post, even_w_in, rg_conv_w, rg_conv_b, rg_gate_w, rg_gate_b, rg_lambda, sc_conv_w, even_w_out, odd_norm_pre, odd_norm_post, odd_w_in, gla_w_gate_lr, gla_b_gate, gla_norm_g, odd_w_out, loss_target, m_even_norm_pre, m_even_norm_post, m_even_w_in, m_rg_conv_w, m_rg_conv_b, m_rg_gate_w, m_rg_gate_b, m_rg_lambda, m_sc_conv_w, m_even_w_out, m_odd_norm_pre, m_odd_norm_post, m_odd_w_in, m_gla_w_gate_lr, m_gla_b_gate, m_gla_norm_g, m_odd_w_out, v_even_norm_pre, v_even_norm_post, v_even_w_in, v_rg_conv_w, v_rg_conv_b, v_rg_gate_w, v_rg_gate_b, v_rg_lambda, v_sc_conv_w, v_even_w_out, v_odd_norm_pre, v_odd_norm_post, v_odd_w_in, v_gla_w_gate_lr, v_gla_b_gate, v_gla_norm_g, v_odd_w_out):
    weights = dict(even_norm_pre=even_norm_pre, even_norm_post=even_norm_post, even_w_in=even_w_in, rg_conv_w=rg_conv_w,
                   rg_conv_b=rg_conv_b, rg_gate_w=rg_gate_w, rg_gate_b=rg_gate_b, rg_lambda=rg_lambda, sc_conv_w=sc_conv_w,
                   even_w_out=even_w_out, odd_norm_pre=odd_norm_pre, odd_norm_post=odd_norm_post, odd_w_in=odd_w_in,
                   gla_w_gate_lr=gla_w_gate_lr, gla_b_gate=gla_b_gate, gla_norm_g=gla_norm_g, odd_w_out=odd_w_out)
    m_in = dict(even_norm_pre=m_even_norm_pre, even_norm_post=m_even_norm_post, even_w_in=m_even_w_in, rg_conv_w=m_rg_conv_w,
                rg_conv_b=m_rg_conv_b, rg_gate_w=m_rg_gate_w, rg_gate_b=m_rg_gate_b, rg_lambda=m_rg_lambda, sc_conv_w=m_sc_conv_w,
                even_w_out=m_even_w_out, odd_norm_pre=m_odd_norm_pre, odd_norm_post=m_odd_norm_post, odd_w_in=m_odd_w_in,
                gla_w_gate_lr=m_gla_w_gate_lr, gla_b_gate=m_gla_b_gate, gla_norm_g=m_gla_norm_g, odd_w_out=m_odd_w_out)
    v_in = dict(even_norm_pre=v_even_norm_pre, even_norm_post=v_even_norm_post, even_w_in=v_even_w_in, rg_conv_w=v_rg_conv_w,
                rg_conv_b=v_rg_conv_b, rg_gate_w=v_rg_gate_w, rg_gate_b=v_rg_gate_b, rg_lambda=v_rg_lambda, sc_conv_w=v_sc_conv_w,
                even_w_out=v_even_w_out, odd_norm_pre=v_odd_norm_pre, odd_norm_post=v_odd_norm_post, odd_w_in=v_odd_w_in,
                gla_w_gate_lr=v_gla_w_gate_lr, gla_b_gate=v_gla_b_gate, gla_norm_g=v_gla_norm_g, odd_w_out=v_odd_w_out)
    names = list(weights)
    shapes = {n: weights[n].shape for n in names}
    xs = x[0]
    tgt = loss_target[0]

    proj_e, h_e, w_in_e, small_all = gather_matmul(xs, even_norm_pre, even_w_in[0].astype(BF16),
                                                   _pack_small({n: weights[n][0] for n in SMALL_SHARDED}), 2 * MM_TILE)
    small = _unpack_gathered(small_all)
    gate_w = rg_gate_w[0].reshape(4, RG_HEADS, RG_HEAD_DIM, RG_HEAD_DIM).astype(BF16)
    gate_b = rg_gate_b[0].reshape(4, RG_HEADS, RG_HEAD_DIM)
    conv_b = rg_conv_b
    wg_pad = [jnp.pad(small["gla_w_gate_lr"][d], ((GLA_RANK * d, LANES - GLA_RANK * (d + 1)), (0, 0))).astype(BF16) for d in range(2)]
    bg = [small["gla_b_gate"][d:d + 1] for d in range(2)]
    gnorm = jnp.tile(small["gla_norm_g"], (1, GLA_HEADS))

    half = D_MODEL // 2
    behind_gates = Exchange()
    behind_gates.gather(even_w_out[0].astype(BF16), via_sibling=True)
    behind_gates.gather(odd_w_in[0, :half].astype(BF16), via_sibling=True)
    (ab, hf), (w_out_e, w_in_o_top) = even_gates_fwd(proj_e, small["rg_conv_w"], conv_b, gate_w, gate_b, small["rg_lambda"],
                                                     exchange=behind_gates)
    w_out_e = w_out_e.reshape(2 * D_MODEL, D_MODEL)
    behind_mix_fwd = Exchange()
    behind_mix_fwd.gather(odd_w_in[0, half:].astype(BF16), via_sibling=True)
    behind_mix_fwd.gather(odd_w_out[0].astype(BF16), via_sibling=True)
    (u_e, hb), (w_in_o_bottom, w_out_o) = even_mix_fwd(ab, hf, proj_e, small["sc_conv_w"], exchange=behind_mix_fwd)
    w_out_o = w_out_o.reshape(D_MODEL, D_MODEL)
    w_in_o = jnp.concatenate([jnp.transpose(part, (1, 0, 2)).reshape(half, ODD_IN) for part in (w_in_o_top, w_in_o_bottom)], axis=0)
    w_in_o = jnp.pad(w_in_o, ((0, 0), (0, ODD_IN_PAD - ODD_IN)))
    y_e, x1 = matmul_post(u_e, w_out_e, xs, even_norm_post, "even_out")

    proj_o, h_o = rms_matmul(x1, small["odd_norm_pre"], w_in_o, MM_TILE, ODD_IN_PAD, "odd_in")
    o_f, st_f = gla_fwd(proj_o, wg_pad[0], bg[0], False)
    osum, u_o, st_b = gla_fwd(proj_o, wg_pad[1], bg[1], True, o_other=o_f, gnorm=gnorm)
    y_o, dout, loss_part = matmul_post(u_o, w_out_o, x1, small["odd_norm_post"], "odd_out", target=tgt)

    du_o, dy_o, d_odd_norm_post = normbwd_matmul_nt(y_o, small["odd_norm_post"], dout, w_out_o, D_MODEL, "odd_out_bwd")
    d_w_out_o = matmul_tn(u_o, dy_o, D_MODEL, D_MODEL, 4 * MM_TILE, BF16, "odd_w_out_grad")
    do, dr, d_gnorm = gla_out_bwd(du_o, proj_o, osum, gnorm)
    dqkv_f, dlr_f, dwg_f, dbg_f = gla_bwd(proj_o, wg_pad[0], bg[0], do, st_f, False)
    dproj_o, dwg_b, dbg_b = gla_bwd(proj_o, wg_pad[1], bg[1], do, st_b, True, first=(dqkv_f, dlr_f, dr))
    dx1, d_odd_norm_pre = matmul_nt_normbwd(dproj_o, w_in_o, x1, small["odd_norm_pre"], dout, ODD_IN_PAD, "odd_in_bwd")
    d_w_in_o = matmul_tn(h_o, dproj_o, D_MODEL, ODD_IN_PAD // 5, 8 * MM_TILE, BF16, "odd_w_in_grad")

    landed = {}
    behind_out = Exchange()
    behind_out.scatter(d_w_out_o.reshape(N_DEV, D_MODEL // N_DEV, D_MODEL))
    behind_out.scatter(d_odd_norm_pre, columns=True)
    behind_out.scatter(d_odd_norm_post, columns=True)
    behind_out.scatter(_blocks_along_columns(jnp.concatenate([dbg_f, dbg_b], axis=0), 2))
    behind_out.scatter(_blocks_along_columns(d_gnorm, 1))
    behind_out.scatter(_blocks_along_columns(jnp.concatenate([dwg_f[:GLA_RANK], dwg_b[GLA_RANK:2 * GLA_RANK]], axis=0), 2 * GLA_RANK))
    (du_e, dy_e, d_even_norm_post), got = normbwd_matmul_nt(y_e, even_norm_post, dx1, w_out_e, 2 * D_MODEL, "even_out_bwd",
                                                           exchange=behind_out)
    p_w_out_o = got[0]
    for n, part in zip(("odd_norm_pre", "odd_norm_post", "gla_b_gate", "gla_norm_g", "gla_w_gate_lr"), got[1:]):
        landed[n] = part
    d_w_out_e = matmul_tn(u_e, dy_e, D_MODEL, D_MODEL, 4 * MM_TILE, BF16, "even_w_out_grad")
    behind_mix = Exchange()
    behind_mix.scatter(d_w_out_e.reshape(N_DEV, 2 * D_MODEL // N_DEV, D_MODEL))
    (dh, drest, d_sc_w, adj_b), (p_w_out_e,) = even_mix_bwd(du_e, hf, hb, proj_e, small["sc_conv_w"], ab, exchange=behind_mix)
    adj_f = linear_scan(ab, 0, dh.reshape(1, *dh.shape), 0, True, True, "scan_fwd_adjoint")
    behind_gates_bwd = Exchange()
    behind_gates_bwd.scatter(jnp.transpose(d_w_in_o[:, :ODD_IN].reshape(D_MODEL, N_DEV, ODD_SHARD), (1, 0, 2)))
    behind_gates_bwd.scatter(d_sc_w, columns=True)
    (dua, d_gate_w, d_gate_b, d_lam), (p_w_in_o, landed["sc_conv_w"]) = even_gates_bwd(
        proj_e, adj_f, adj_b, hf, hb, dh, small["rg_conv_w"], conv_b, gate_w, gate_b, small["rg_lambda"], exchange=behind_gates_bwd)
    gate_w_rows = 4 * RG_HEADS * RG_HEAD_DIM
    behind_conv = Exchange()
    behind_conv.scatter(d_gate_w.reshape(N_DEV, gate_w_rows // N_DEV, RG_HEAD_DIM))
    behind_conv.scatter(d_lam, columns=True)
    (dproj_e, d_conv_w, d_conv_b), (p_gate_w, landed["rg_lambda"]) = rg_conv_bwd(dua, proj_e, drest, small["rg_conv_w"],
                                                                                 exchange=behind_conv)
    behind_w_grad = Exchange()
    behind_w_grad.gather(sum_parts(p_gate_w, "sum_gate_w"))
    d_w_in_e, (g_gate_w_all,) = matmul_tn(h_e, dproj_e, D_MODEL, EVEN_SHARD, 8 * MM_TILE, BF16, "even_w_in_grad",
                                          exchange=behind_w_grad)
    to_sibling = Exchange()
    to_sibling.to_sibling(d_w_in_e)
    to_sibling.scatter(d_conv_w, columns=True)
    from_sibling, landed["rg_conv_w"] = run_exchange(to_sibling, "scatter_to_sibling")
    behind_in_bwd = Exchange()
    behind_in_bwd.among_chips(pair_sum(d_w_in_e, from_sibling))
    (grad_x, d_even_norm_pre), (p_w_in_e,) = matmul_nt_normbwd(
        dproj_e, w_in_e, xs, even_norm_pre, dx1, 3 * D_MODEL, "even_in_bwd", exchange=behind_in_bwd)
    last = Exchange()
    replicated_vecs = ("even_norm_pre", "even_norm_post", "rg_conv_b")
    last.gather(jnp.concatenate([d_even_norm_pre, d_even_norm_post, d_conv_b], axis=0))
    last.gather(d_gate_b.reshape(4 * RG_HEADS, RG_HEAD_DIM))
    last.gather(loss_part)
    land_vec, land_gate_b, land_loss = run_exchange(last, "gather_last")

    results = {}

    def update(name, parts_, shape2d):
        outs = adamw(parts_, weights[name][0].reshape(shape2d), m_in[name][0].reshape(shape2d), v_in[name][0].reshape(shape2d),
                     "adamw_" + name)
        results[name] = [o.reshape(shapes[name]) for o in outs]

    update("even_w_in", p_w_in_e, (D_MODEL, EVEN_SHARD))
    update("even_w_out", p_w_out_e, (2 * D_MODEL // N_DEV, D_MODEL))
    update("odd_w_in", p_w_in_o, (D_MODEL, ODD_SHARD))
    update("odd_w_out", p_w_out_o, (D_MODEL // N_DEV, D_MODEL))
    update("rg_gate_w", g_gate_w_all.reshape(1, gate_w_rows, RG_HEAD_DIM), (gate_w_rows, RG_HEAD_DIM))
    small_out = adamw_small({n: landed[n] for n in SMALL_SHARDED}, weights, m_in, v_in)
    for n in SMALL_SHARDED:
        results[n] = [o[n] for o in small_out]
    gate_b_shape = (4 * RG_HEADS, RG_HEAD_DIM)
    rep_out, gate_b_out, loss_all = adamw_replicated(land_vec, land_gate_b, land_loss, replicated_vecs, weights, m_in, v_in,
                                                     [src["rg_gate_b"].reshape(gate_b_shape) for src in (weights, m_in, v_in)])
    results.update(rep_out)
    results["rg_gate_b"] = [o.reshape(shapes["rg_gate_b"]) for o in gate_b_out]

    return (loss_all[0, 0], grad_x.reshape(x.shape), *[results[n][0] for n in names], *[results[n][1] for n in names],
            *[results[n][2] for n in names], *[results[n][3] for n in names])
```

```python
import functools

import jax
import jax.numpy as jnp
from jax import lax
from jax.experimental import pallas as pl
from jax.experimental.pallas import tpu as pltpu

F32 = jnp.float32
BF16 = jnp.bfloat16

N_DEV = 8
D_MODEL = 1024
NORM_EPS = 1e-6
RG_HEADS = 8
RG_HEAD_DIM = 128
RG_C = 8.0
GLA_HEADS = 4
GLA_DK = 128
GLA_DV = 256
GLA_KEY = 512
GLA_RANK = 16
GLA_NORMALIZER = 16.0
GLA_CHUNK = 64
EVEN_IN = 6144
ODD_IN = 3104
ODD_IN_PAD = 3200
ODD_SHARD = ODD_IN // N_DEV
EVEN_SHARD = EVEN_IN // N_DEV
ADAM_LR = 0.001
ADAM_B1 = 0.9
ADAM_B2 = 0.999
ADAM_EPS = 1e-08
ADAM_WD = 0.01
ADAM_STEP = 10

SMALLEST_NORMAL = 1.1754944e-38
SUBLANES = 8
LANES = 128
VMEM_LIMIT_BYTES = 48 * 2 ** 20
ROW_TILE = 256
MM_TILE = 512
ADAMW_BLOCK_BYTES = 2 ** 20
PACK_ROWS = 48
MESH_ID = pl.DeviceIdType.MESH


def _params(n_grid):
    return pltpu.CompilerParams(dimension_semantics=("arbitrary",) * n_grid, vmem_limit_bytes=VMEM_LIMIT_BYTES)


def _bdot(a, b):
    return jnp.dot(a.astype(BF16), b.astype(BF16), preferred_element_type=F32)


def _bdot_nt(a, b):
    return lax.dot_general(a.astype(BF16), b.astype(BF16), (((1,), (1,)), ((), ())), preferred_element_type=F32)


def _bdot_tn(a, b):
    return lax.dot_general(a.astype(BF16), b.astype(BF16), (((0,), (0,)), ((), ())), preferred_element_type=F32)


def _rstd(x):
    return lax.rsqrt(jnp.mean(x * x, axis=-1, keepdims=True) + NORM_EPS)


def _rms(x, g):
    return x * _rstd(x) * g


def _rms_bwd(x, g, dy):
    xh = x * _rstd(x)
    dyg = dy * g
    dx = _rstd(x) * (dyg - xh * jnp.mean(dyg * xh, axis=-1, keepdims=True))
    return dx, jnp.sum(dy * xh, axis=0, keepdims=True)


def _sigmoid(z):
    return 0.5 * jnp.tanh(0.5 * z) + 0.5


def _silu_and_grad(z):
    s = _sigmoid(z)
    return z * s, s * (1.0 + z * (1.0 - s))


def _softplus(z):
    return jnp.maximum(z, 0.0) + jnp.log(1.0 + jnp.exp(-jnp.abs(z)))


def _shift_rows(cur, before, after, d):
    ts = cur.shape[0]
    row = lax.broadcasted_iota(jnp.int32, (SUBLANES, cur.shape[1]), 0)
    out = pltpu.roll(cur, (-d) % ts, 0)
    if d < 0:
        edge = jnp.where(row < -d, pltpu.roll(before, (-d) % SUBLANES, 0), out[:SUBLANES])
        return jnp.concatenate([edge, out[SUBLANES:]], axis=0)
    edge = jnp.where(row >= SUBLANES - d, pltpu.roll(after, (-d) % SUBLANES, 0), out[ts - SUBLANES:])
    return jnp.concatenate([out[:ts - SUBLANES], edge], axis=0)


def _halo_specs(ts, s, width, col, tile=lambda i: i):
    per = ts // SUBLANES
    last = s // SUBLANES - 1
    return [
        pl.BlockSpec((ts, width), lambda i: (tile(i), col)),
        pl.BlockSpec((SUBLANES, width), lambda i: (jnp.maximum(tile(i) * per - 1, 0), col)),
        pl.BlockSpec((SUBLANES, width), lambda i: (jnp.minimum((tile(i) + 1) * per, last), col)),
    ]


def _halo_load(cur_ref, before_ref, after_ref, n_tiles, tile=lambda i: i):
    i = tile(pl.program_id(0))
    before = jnp.where(i > 0, before_ref[...], 0.0)
    after = jnp.where(i < n_tiles - 1, after_ref[...], 0.0)
    return cur_ref[...], before, after


def _full(shape):
    return pl.BlockSpec(shape, lambda *_: (0,) * len(shape))


def _peer(x, y, c, mask):
    px, py, pc = x ^ (mask >> 2), y ^ ((mask >> 1) & 1), c ^ (mask & 1)
    return (px, py, pc), 4 * px + 2 * py + pc


class Exchange:
    SIBLING = 1
    OTHER_CHIPS = (2, 4, 6)

    def __init__(self):
        self.args, self.out_shape, self._kinds = [], [], []

    def gather(self, block, columns=False, via_sibling=False):
        shape = (block.shape[0], N_DEV * block.shape[1]) if columns else (N_DEV,) + block.shape
        return self._add(block, shape, ("gather", columns, via_sibling))

    def scatter(self, stack, columns=False):
        shape = (N_DEV, stack.shape[0], stack.shape[1] // N_DEV) if columns else stack.shape
        return self._add(stack, shape, ("scatter", columns, False))

    def _add(self, arg, shape, kind):
        self.args.append(arg)
        self.out_shape.append(jax.ShapeDtypeStruct(shape, arg.dtype))
        self._kinds.append(kind)
        return len(self.args) - 1

    def semaphores(self):
        n = len(self.args)
        return [pltpu.SemaphoreType.DMA((n, N_DEV - 1)), pltpu.SemaphoreType.DMA((n, N_DEV - 1)), pltpu.SemaphoreType.DMA((n,))]

    def to_sibling(self, array):
        shape = (N_DEV // 2, array.shape[0], array.shape[1] // N_DEV)
        return self._add(array, shape, ("to_sibling", True, False))

    def among_chips(self, stack):
        return self._add(stack, stack.shape, ("among_chips", False, False))

    def _copies(self, position, in_refs, out_refs):
        x, y, c, me = position
        for arr, ((kind, columns, via_sibling), src, out) in enumerate(zip(self._kinds, in_refs, out_refs)):
            if kind == "to_sibling":
                width = src.shape[-1] // N_DEV
                for k in range(N_DEV // 2):
                    block = src.at[:, pl.ds(pl.multiple_of((2 * k + 1 - c) * width, LANES), width)]
                    yield arr, k + 1, block, out.at[k], out.at[k], False, self.SIBLING
                continue
            for mask in range(N_DEV):
                _, peer_id = _peer(x, y, c, mask)
                relayed = via_sibling and mask not in (0, self.SIBLING) + self.OTHER_CHIPS
                if kind == "among_chips":
                    if mask in (0,) + self.OTHER_CHIPS:
                        yield arr, mask, src.at[peer_id // 2], out.at[me // 2], out.at[peer_id // 2], False, mask
                elif kind == "gather":
                    if columns:
                        width = src.shape[-1]
                        yield (arr, mask, src, out.at[:, pl.ds(pl.multiple_of(me * width, LANES), width)],
                               out.at[:, pl.ds(pl.multiple_of(peer_id * width, LANES), width)], relayed, mask)
                    else:
                        yield arr, mask, src, out.at[me], out.at[peer_id], relayed, mask
                else:
                    if columns:
                        width = src.shape[-1] // N_DEV
                        block = src.at[:, pl.ds(pl.multiple_of(peer_id * width, LANES), width)]
                    else:
                        block = src.at[peer_id]
                    yield arr, mask, block, out.at[me], out.at[peer_id], False, mask

    def _remote(self, position, sems, arr, slot, to_mask, src, dst):
        x, y, c, _ = position
        return pltpu.make_async_remote_copy(src_ref=src, dst_ref=dst, send_sem=sems[0].at[arr, slot - 1], recv_sem=sems[1].at[arr, slot - 1],
                                            device_id=_peer(x, y, c, to_mask)[0], device_id_type=MESH_ID)

    def start(self, position, in_refs, out_refs, sems):
        for arr, slot, src, dst, _, relayed, to_mask in self._copies(position, in_refs, out_refs):
            if slot == 0:
                pltpu.make_async_copy(src, dst, sems[2].at[arr]).start()
            elif not relayed:
                self._remote(position, sems, arr, slot, to_mask, src, dst).start()

    def wait(self, position, in_refs, out_refs, sems):
        copies = list(self._copies(position, in_refs, out_refs))
        landings = {(arr, slot): landing for arr, slot, _, _, landing, _, _ in copies}
        passed_on = set()
        for arr, mask, src, _, landing, relayed, _ in copies:
            if relayed:
                held = landings[arr, mask ^ self.SIBLING]
                self._remote(position, sems, arr, mask ^ self.SIBLING, mask ^ self.SIBLING, src, held).wait_recv()
                self._remote(position, sems, arr, mask, self.SIBLING, held, held).start()
                passed_on.add((arr, mask ^ self.SIBLING))
        for arr, slot, src, dst, landing, relayed, to_mask in copies:
            if slot == 0:
                pltpu.make_async_copy(src, dst, sems[2].at[arr]).wait()
                continue
            if (arr, slot) not in passed_on:
                self._remote(position, sems, arr, slot, to_mask, src, landing).wait_recv()
            if relayed:
                held = landings[arr, slot ^ self.SIBLING]
                self._remote(position, sems, arr, slot, self.SIBLING, held, held).wait_send()
            else:
                self._remote(position, sems, arr, slot, to_mask, src, dst).wait_send()


def _call(body, *, name, grid, in_specs, out_specs, out_shape, args, scratch_shapes=(), exchange=None):
    single = not isinstance(out_shape, (list, tuple))
    if single:
        out_specs, out_shape = [out_specs], [out_shape]
    params = _params(len(grid))
    if exchange is None:
        outs = pl.pallas_call(body, name=name, grid=grid, in_specs=in_specs, out_specs=out_specs, out_shape=out_shape,
                              scratch_shapes=list(scratch_shapes), compiler_params=params)(*args)
        return outs[0] if single else outs
    counts = (len(args), len(exchange.args), len(out_shape), len(exchange.out_shape), len(scratch_shapes), 3)

    def wrapped(*refs):
        groups, at = [], 0
        for n in counts:
            groups.append(refs[at:at + n])
            at += n
        main_in, ex_in, main_out, ex_out, main_scratch, sems = groups
        x, y, c = lax.axis_index("x"), lax.axis_index("y"), lax.axis_index("c")
        position = (x, y, c, 4 * x + 2 * y + c)
        ids = [pl.program_id(a) for a in range(len(grid))]
        first = functools.reduce(jnp.logical_and, [i == 0 for i in ids])
        last = functools.reduce(jnp.logical_and, [i == g - 1 for i, g in zip(ids, grid)])

        @pl.when(first)
        def _():
            exchange.start(position, ex_in, ex_out, sems)

        body(*main_in, *main_out, *main_scratch)

        @pl.when(last)
        def _():
            exchange.wait(position, ex_in, ex_out, sems)

    hbm = pl.BlockSpec(memory_space=pl.ANY)
    outs = pl.pallas_call(
        wrapped, name=name, grid=grid, in_specs=list(in_specs) + [hbm] * counts[1], out_specs=list(out_specs) + [hbm] * counts[3],
        out_shape=list(out_shape) + exchange.out_shape, scratch_shapes=list(scratch_shapes) + exchange.semaphores(),
        compiler_params=params)(*args, *exchange.args)
    main = outs[:counts[2]]
    return (main[0] if single else main), outs[counts[2]:]


def run_exchange(exchange, name):
    return _call(lambda: None, name=name, grid=(1,), in_specs=[], out_specs=[], out_shape=[], args=[], exchange=exchange)[1]


def gather_matmul(x, g, w_block, small_block, tm):
    s, d = x.shape
    width = w_block.shape[1]
    pair = 2 * width
    n_chips = N_DEV // 2
    tm = min(tm, s)
    n_i = s // tm
    sibling = Exchange.SIBLING

    def body(chips_ref, x_ref, g_ref, wb_ref, sb_ref, proj_ref, h_ref, w_ref, small_ref, h_all, w_pair, send, recv, local, load_sem):
        j, i = pl.program_id(0), pl.program_id(1)
        xx, yy, cc = lax.axis_index("x"), lax.axis_index("y"), lax.axis_index("c")
        me = 4 * xx + 2 * yy + cc

        def block_of(dev):
            return w_ref.at[:, pl.ds(pl.multiple_of(dev * width, LANES), width)]

        def remote(arr, slot, to_mask, src, dst):
            return pltpu.make_async_remote_copy(src_ref=src, dst_ref=dst, send_sem=send.at[arr, slot - 1], recv_sem=recv.at[arr, slot - 1],
                                                device_id=_peer(xx, yy, cc, to_mask)[0], device_id_type=MESH_ID)

        @pl.when((j == 0) & (i == 0))
        def _():
            pltpu.make_async_copy(wb_ref, block_of(me), local.at[0]).start()
            pltpu.make_async_copy(sb_ref, small_ref.at[me], local.at[1]).start()
            for mask in (sibling,) + Exchange.OTHER_CHIPS[:-1]:
                remote(0, mask, mask, wb_ref, block_of(me)).start()
            for mask in range(1, N_DEV):
                remote(1, mask, mask, sb_ref, small_ref.at[me]).start()

        for step in range(n_chips):
            @pl.when((j == step) & (i == 0))
            def _(step=step):
                if step == 0:
                    pltpu.make_async_copy(wb_ref, block_of(me), local.at[0]).wait()
                    remote(0, sibling, sibling, wb_ref, block_of(me ^ sibling)).wait_recv()
                else:
                    mask = 2 * step
                    remote(0, mask, mask, wb_ref, block_of(me ^ mask)).wait_recv()
                    if step == 1:
                        remote(0, Exchange.OTHER_CHIPS[-1], Exchange.OTHER_CHIPS[-1], wb_ref, block_of(me)).start()
                    remote(0, mask | sibling, sibling, block_of(me ^ mask), block_of(me ^ mask)).start()
                    remote(0, mask | sibling, mask | sibling, wb_ref, block_of(me ^ (mask | sibling))).wait_recv()
                load = pltpu.make_async_copy(w_ref.at[:, pl.ds(pl.multiple_of(chips_ref[step] * pair, LANES), pair)], w_pair, load_sem)
                load.start()
                load.wait()

        rows = pl.ds(pl.multiple_of(i * tm, tm), tm)

        @pl.when(j == 0)
        def _():
            h = _rms(x_ref[...], g_ref[...]).astype(BF16)
            h_all[rows, :] = h
            h_ref[...] = h

        proj_ref[...] = jnp.dot(h_all[rows, :], w_pair[...], preferred_element_type=F32)

        @pl.when((j == n_chips - 1) & (i == n_i - 1))
        def _():
            pltpu.make_async_copy(sb_ref, small_ref.at[me], local.at[1]).wait()
            for mask in range(1, N_DEV):
                remote(1, mask, mask, sb_ref, small_ref.at[me ^ mask]).wait_recv()
                remote(1, mask, mask, sb_ref, small_ref.at[me]).wait_send()
            for mask in (sibling,) + Exchange.OTHER_CHIPS:
                remote(0, mask, mask, wb_ref, block_of(me)).wait_send()
            for mask in Exchange.OTHER_CHIPS:
                remote(0, mask | sibling, sibling, block_of(me ^ mask), block_of(me ^ mask)).wait_send()

    def first_pass_row(j, i, chips):
        return jnp.where(j == 0, i, n_i - 1), 0

    hbm = pl.BlockSpec(memory_space=pl.ANY)
    my_chip = 2 * lax.axis_index("x") + lax.axis_index("y")
    chips = (my_chip ^ jnp.arange(n_chips)).astype(jnp.int32)
    grid_spec = pltpu.PrefetchScalarGridSpec(
        num_scalar_prefetch=1, grid=(n_chips, n_i),
        in_specs=[pl.BlockSpec((tm, d), first_pass_row), pl.BlockSpec((1, d), lambda j, i, chips: (0, 0)), hbm, hbm],
        out_specs=[pl.BlockSpec((tm, pair), lambda j, i, chips: (i, chips[j])), pl.BlockSpec((tm, d), first_pass_row), hbm, hbm],
        scratch_shapes=[pltpu.VMEM((s, d), BF16), pltpu.VMEM((d, pair), BF16), pltpu.SemaphoreType.DMA((2, N_DEV - 1)),
                        pltpu.SemaphoreType.DMA((2, N_DEV - 1)), pltpu.SemaphoreType.DMA((2,)), pltpu.SemaphoreType.DMA(())])
    return pl.pallas_call(
        body, name="even_in", grid_spec=grid_spec,
        out_shape=[jax.ShapeDtypeStruct((s, N_DEV * width), F32), jax.ShapeDtypeStruct((s, d), BF16),
                   jax.ShapeDtypeStruct((d, N_DEV * width), w_block.dtype), jax.ShapeDtypeStruct((N_DEV,) + small_block.shape, small_block.dtype)],
        compiler_params=_params(2),
    )(chips, x, g, w_block, small_block)


def rms_matmul(x, g, w, tm, tn, name, exchange=None):
    s, d = x.shape
    n = w.shape[1]
    tm = min(tm, s)

    def body(x_ref, g_ref, w_ref, o_ref, h_ref):
        @pl.when(pl.program_id(1) == 0)
        def _():
            h_ref[...] = _rms(x_ref[...], g_ref[...]).astype(BF16)

        o_ref[...] = jnp.dot(h_ref[...], w_ref[...], preferred_element_type=F32)

    return _call(
        body, name=name, grid=(s // tm, n // tn),
        in_specs=[pl.BlockSpec((tm, d), lambda i, j: (i, 0)), _full((1, d)), pl.BlockSpec((d, tn), lambda i, j: (0, j))],
        out_specs=[pl.BlockSpec((tm, tn), lambda i, j: (i, j)), pl.BlockSpec((tm, d), lambda i, j: (i, 0))],
        out_shape=[jax.ShapeDtypeStruct((s, n), F32), jax.ShapeDtypeStruct((s, d), BF16)],
        args=[x, g, w], exchange=exchange)


def matmul_post(u, w, xres, g, name, target=None):
    s, k = u.shape
    d = w.shape[1]
    tm = min(MM_TILE, s)
    with_loss = target is not None

    def body(*refs):
        if with_loss:
            u_ref, w_ref, x_ref, g_ref, t_ref, y_ref, dout_ref, loss_ref = refs
        else:
            u_ref, w_ref, x_ref, g_ref, y_ref, out_ref = refs
        y = jnp.dot(u_ref[...], w_ref[...], preferred_element_type=F32)
        y_ref[...] = y
        out = x_ref[...] + _rms(y, g_ref[...])
        if with_loss:
            @pl.when(pl.program_id(0) == 0)
            def _():
                loss_ref[...] = jnp.zeros_like(loss_ref)

            diff = out - t_ref[...]
            dout_ref[...] = diff * (1.0 / d)
            loss_ref[...] += 0.5 * jnp.sum(jnp.mean(diff * diff, axis=-1, keepdims=True))
        else:
            out_ref[...] = out

    row = pl.BlockSpec((tm, d), lambda i: (i, 0))
    in_specs = [pl.BlockSpec((tm, k), lambda i: (i, 0)), _full((k, d)), row, _full((1, d))]
    args = [u, w, xres, g]
    out_specs = [row, row]
    out_shape = [jax.ShapeDtypeStruct((s, d), F32), jax.ShapeDtypeStruct((s, d), F32)]
    if with_loss:
        in_specs.append(row)
        args.append(target)
        out_specs.append(_full((SUBLANES, LANES)))
        out_shape.append(jax.ShapeDtypeStruct((SUBLANES, LANES), F32))
    return pl.pallas_call(body, name=name, grid=(s // tm,), in_specs=in_specs, out_specs=out_specs,
                          out_shape=out_shape, compiler_params=_params(1))(*args)


def normbwd_matmul_nt(y, g, dout, w, tn, name, exchange=None):
    s, d = y.shape
    n = w.shape[0]
    tm = min(MM_TILE, s)

    def body(y_ref, g_ref, dout_ref, w_ref, du_ref, dy_ref, dg_ref):
        i, j = pl.program_id(0), pl.program_id(1)

        @pl.when(j == 0)
        def _():
            dy, dg = _rms_bwd(y_ref[...], g_ref[...], dout_ref[...])
            dy_ref[...] = dy.astype(BF16)

            @pl.when(i == 0)
            def _():
                dg_ref[...] = jnp.zeros_like(dg_ref)

            dg_ref[...] += dg

        du_ref[...] = lax.dot_general(dy_ref[...], w_ref[...], (((1,), (1,)), ((), ())), preferred_element_type=F32)

    row = pl.BlockSpec((tm, d), lambda i, j: (i, 0))
    return _call(
        body, name=name, grid=(s // tm, n // tn),
        in_specs=[row, _full((1, d)), row, pl.BlockSpec((tn, d), lambda i, j: (j, 0))],
        out_specs=[pl.BlockSpec((tm, tn), lambda i, j: (i, j)), row, _full((1, d))],
        out_shape=[jax.ShapeDtypeStruct((s, n), F32), jax.ShapeDtypeStruct((s, d), BF16), jax.ShapeDtypeStruct((1, d), F32)],
        args=[y, g, dout, w], exchange=exchange)


def matmul_tn(a, b, tm, tn, ts, out_dtype, name, exchange=None, b_first=None):
    s, m = a.shape
    n = b.shape[1] + (0 if b_first is None else tn)
    ts = min(ts, s)
    n_k = s // ts
    dims = (((0,), (0,)), ((), ()))

    def body(*refs):
        if b_first is None:
            a_ref, b_ref, o_ref, acc = refs
        else:
            a_ref, first_ref, b_ref, o_ref, acc = refs
        j, k = pl.program_id(1), pl.program_id(2)

        @pl.when(k == 0)
        def _():
            acc[...] = jnp.zeros_like(acc)

        if b_first is None:
            acc[...] += lax.dot_general(a_ref[...], b_ref[...], dims, preferred_element_type=F32)
        else:
            @pl.when(j == 0)
            def _():
                acc[...] += lax.dot_general(a_ref[...], first_ref[...], dims, preferred_element_type=F32)

            @pl.when(j > 0)
            def _():
                acc[...] += lax.dot_general(a_ref[...], b_ref[...], dims, preferred_element_type=F32)

        @pl.when(k == n_k - 1)
        def _():
            o_ref[...] = acc[...].astype(out_dtype)

    if b_first is None:
        b_specs, b_args = [pl.BlockSpec((ts, tn), lambda i, j, k: (k, j))], [b]
    else:
        b_specs = [pl.BlockSpec((ts, tn), lambda i, j, k: (k, 0)), pl.BlockSpec((ts, tn), lambda i, j, k: (k, jnp.maximum(j - 1, 0)))]
        b_args = [b_first, b]
    return _call(
        body, name=name, grid=(m // tm, n // tn, n_k),
        in_specs=[pl.BlockSpec((ts, tm), lambda i, j, k: (k, i))] + b_specs,
        out_specs=pl.BlockSpec((tm, tn), lambda i, j, k: (i, j)),
        out_shape=jax.ShapeDtypeStruct((m, n), out_dtype),
        scratch_shapes=[pltpu.VMEM((tm, tn), F32)], args=[a] + b_args, exchange=exchange)


def matmul_nt_normbwd(dproj, w, x, g, dres, tm, tk, name, exchange=None, first=None):
    s, kt = dproj.shape
    kt += 0 if first is None else tk
    d = w.shape[0]
    tm = min(tm, s)
    n_k = kt // tk
    dims = (((1,), (1,)), ((), ()))

    def body(*refs):
        if first is None:
            a_ref, w_ref, x_ref, g_ref, r_ref, dx_ref, dg_ref, acc = refs
        else:
            first_ref, a_ref, w_ref, x_ref, g_ref, r_ref, dx_ref, dg_ref, acc = refs
        i, k = pl.program_id(0), pl.program_id(1)

        @pl.when(k == 0)
        def _():
            acc[...] = jnp.zeros_like(acc)

        if first is None:
            acc[...] += lax.dot_general(a_ref[...], w_ref[...], dims, preferred_element_type=F32)
        else:
            @pl.when(k == 0)
            def _():
                acc[...] += lax.dot_general(first_ref[...], w_ref[...], dims, preferred_element_type=F32)

            @pl.when(k > 0)
            def _():
                acc[...] += lax.dot_general(a_ref[...], w_ref[...], dims, preferred_element_type=F32)

        @pl.when(k == n_k - 1)
        def _():
            dx, dg = _rms_bwd(x_ref[...], g_ref[...], acc[...])
            dx_ref[...] = r_ref[...] + dx

            @pl.when(i == 0)
            def _():
                dg_ref[...] = jnp.zeros_like(dg_ref)

            dg_ref[...] += dg

    row = pl.BlockSpec((tm, d), lambda i, k: (i, 0))
    if first is None:
        a_specs, a_args = [pl.BlockSpec((tm, tk), lambda i, k: (i, k))], [dproj]
    else:
        a_specs = [pl.BlockSpec((tm, tk), lambda i, k: (i, 0)), pl.BlockSpec((tm, tk), lambda i, k: (i, jnp.maximum(k - 1, 0)))]
        a_args = [first, dproj]
    return _call(
        body, name=name, grid=(s // tm, n_k),
        in_specs=a_specs + [pl.BlockSpec((d, tk), lambda i, k: (0, k)), row, _full((1, d)), row],
        out_specs=[row, _full((1, d))],
        out_shape=[jax.ShapeDtypeStruct((s, d), F32), jax.ShapeDtypeStruct((1, d), F32)],
        scratch_shapes=[pltpu.VMEM((tm, d), F32)], args=a_args + [w, x, g, dres], exchange=exchange)


def _rg_conv(xa, before, after, cw, cb):
    return (cw[0:1, :] * _shift_rows(xa, before, after, -2) + cw[1:2, :] * _shift_rows(xa, before, after, -1)
            + cw[2:3, :] * xa + cw[3:4, :] * _shift_rows(xa, before, after, 1) + cb)


def _rg_gates(ua_h, gw_ref, gb_ref, c_h, direction, head):
    r = _sigmoid(_bdot(ua_h, gw_ref[2 * direction, head]) + gb_ref[2 * direction, head:head + 1, :])
    i = _sigmoid(_bdot(ua_h, gw_ref[2 * direction + 1, head]) + gb_ref[2 * direction + 1, head:head + 1, :])
    log_a = -c_h * r
    a = jnp.exp(log_a)
    beta_sq = -jnp.tanh(log_a) * (1.0 + a * a)
    inv_beta = lax.rsqrt(jnp.maximum(beta_sq, SMALLEST_NORMAL))
    return r, i, a, beta_sq * inv_beta, inv_beta


def even_gates_fwd(proj, conv_w, conv_b, gate_w, gate_b, lam, exchange=None):
    s = proj.shape[0]
    ts = min(2 * ROW_TILE, s)
    n_tiles = s // ts

    def body(xa_ref, xb_ref, xn_ref, cw_ref, cb_ref, gw_ref, gb_ref, lam_ref, o_ref, hf_ref, carry):
        @pl.when(pl.program_id(0) == 0)
        def _():
            carry[...] = jnp.zeros_like(carry)

        xa, before, after = _halo_load(xa_ref, xb_ref, xn_ref, n_tiles)
        ua = _rg_conv(xa, before, after, cw_ref[...], cb_ref[...])
        c = RG_C * _softplus(-lam_ref[...])
        for direction in range(2):
            for head in range(RG_HEADS):
                lanes = slice(head * RG_HEAD_DIM, (head + 1) * RG_HEAD_DIM)
                ua_h = ua[:, lanes]
                _, i, a, beta, _ = _rg_gates(ua_h, gw_ref, gb_ref, c[direction:direction + 1, lanes], direction, head)
                o_ref[2 * direction, :, lanes] = a
                o_ref[2 * direction + 1, :, lanes] = beta * (i * ua_h)
        _scan_tile(o_ref.at[0], o_ref.at[1], hf_ref, carry, False, False)

    return _call(
        body, name="even_gates_fwd", grid=(n_tiles,),
        in_specs=_halo_specs(ts, s, D_MODEL, 0) + [_full(conv_w.shape), _full(conv_b.shape), _full(gate_w.shape),
                                                   _full(gate_b.shape), _full(lam.shape)],
        out_specs=[pl.BlockSpec((4, ts, D_MODEL), lambda i: (0, i, 0)), pl.BlockSpec((ts, D_MODEL), lambda i: (i, 0))],
        out_shape=[jax.ShapeDtypeStruct((4, s, D_MODEL), F32), jax.ShapeDtypeStruct((s, D_MODEL), F32)],
        scratch_shapes=[pltpu.VMEM((SUBLANES, D_MODEL), F32)],
        args=[proj, proj, proj, conv_w, conv_b, gate_w, gate_b, lam], exchange=exchange)


def _scan_tile(a_ref, b_ref, h_ref, carry, reverse, b_times_a):
    ts, c = h_ref.shape
    n_blocks = ts // SUBLANES
    row = lax.broadcasted_iota(jnp.int32, (SUBLANES, c), 0)

    def block(j, h_in):
        r0 = pl.multiple_of((n_blocks - 1 - j if reverse else j) * SUBLANES, SUBLANES)
        a = a_ref[pl.ds(r0, SUBLANES), :]
        b = b_ref[pl.ds(r0, SUBLANES), :]
        if b_times_a:
            b = a * b
        for step in (1, 2, 4):
            shift = SUBLANES - step if reverse else step
            valid = row < SUBLANES - step if reverse else row >= step
            b = jnp.where(valid, a * pltpu.roll(b, shift, 0) + b, b)
            a = jnp.where(valid, a * pltpu.roll(a, shift, 0), a)
        h = a * h_in + b
        h_ref[pl.ds(r0, SUBLANES), :] = h
        return h[0:1, :] if reverse else h[SUBLANES - 1:SUBLANES, :]

    carry[0:1, :] = lax.fori_loop(0, n_blocks, block, carry[0:1, :])


def linear_scan(a_arr, a_idx, b_arr, b_idx, reverse, b_times_a, name, exchange=None):
    _, s, c = a_arr.shape
    ts = min(MM_TILE, s)
    n_tiles = s // ts

    def tile_of(i):
        return n_tiles - 1 - i if reverse else i

    def body(a_ref, b_ref, h_ref, carry):
        @pl.when(pl.program_id(0) == 0)
        def _():
            carry[...] = jnp.zeros_like(carry)

        _scan_tile(a_ref, b_ref, h_ref, carry, reverse, b_times_a)

    return _call(
        body, name=name, grid=(n_tiles,),
        in_specs=[pl.BlockSpec((None, ts, c), lambda i: (a_idx, tile_of(i), 0)),
                  pl.BlockSpec((None, ts, c), lambda i: (b_idx, tile_of(i), 0))],
        out_specs=pl.BlockSpec((ts, c), lambda i: (tile_of(i), 0)),
        out_shape=jax.ShapeDtypeStruct((s, c), F32),
        scratch_shapes=[pltpu.VMEM((SUBLANES, c), F32)], args=[a_arr, b_arr], exchange=exchange)


def _sc_conv(p, before, after, w):
    return w[0:1, :] * _shift_rows(p, before, after, -1) + w[1:2, :] * p + w[2:3, :] * _shift_rows(p, before, after, 1)


def even_mix_fwd(ab, hf, proj, sc_w, exchange=None):
    s = proj.shape[0]
    ts = min(2 * ROW_TILE, s)
    n_tiles = s // ts

    def tile(i):
        return n_tiles - 1 - i

    row = pl.BlockSpec((ts, D_MODEL), lambda i: (tile(i), 0))

    def col(c):
        return pl.BlockSpec((ts, D_MODEL), lambda i: (tile(i), c))

    def body(a_ref, b_ref, hf_ref, za_ref, xb_ref, xbb_ref, xbn_ref, gb_ref, gc_ref, gcb_ref, gcn_ref, zb_ref, w_ref,
             u_ref, hb_ref, carry):
        @pl.when(pl.program_id(0) == 0)
        def _():
            carry[...] = jnp.zeros_like(carry)

        _scan_tile(a_ref, b_ref, hb_ref, carry, True, False)
        xb, xb_before, xb_after = _halo_load(xb_ref, xbb_ref, xbn_ref, n_tiles, tile)
        gc, gc_before, gc_after = _halo_load(gc_ref, gcb_ref, gcn_ref, n_tiles, tile)
        silu_za, _ = _silu_and_grad(za_ref[...])
        silu_zb, _ = _silu_and_grad(zb_ref[...])
        u_ref[:, :D_MODEL] = ((hf_ref[...] + hb_ref[...]) * silu_za).astype(BF16)
        cv = _sc_conv(gc * xb, gc_before * xb_before, gc_after * xb_after, w_ref[...])
        u_ref[:, D_MODEL:] = (gb_ref[...] * cv * silu_zb).astype(BF16)

    return _call(
        body, name="even_mix_fwd", grid=(n_tiles,),
        in_specs=[pl.BlockSpec((None, ts, D_MODEL), lambda i: (2, tile(i), 0)), pl.BlockSpec((None, ts, D_MODEL), lambda i: (3, tile(i), 0)),
                  row, col(1)] + _halo_specs(ts, s, D_MODEL, 2, tile) + [col(3)] + _halo_specs(ts, s, D_MODEL, 4, tile)
        + [col(5), _full(sc_w.shape)],
        out_specs=[pl.BlockSpec((ts, 2 * D_MODEL), lambda i: (tile(i), 0)), row],
        out_shape=[jax.ShapeDtypeStruct((s, 2 * D_MODEL), BF16), jax.ShapeDtypeStruct((s, D_MODEL), F32)],
        scratch_shapes=[pltpu.VMEM((SUBLANES, D_MODEL), F32)],
        args=[ab, ab, hf, proj, proj, proj, proj, proj, proj, proj, proj, proj, sc_w], exchange=exchange)


def even_mix_bwd(du, hf, hb, proj, sc_w, ab, exchange=None):
    s = proj.shape[0]
    ts = min(ROW_TILE, s)
    n_tiles = s // ts
    row = pl.BlockSpec((ts, D_MODEL), lambda i: (i, 0))

    def body(dya_ref, dyb_ref, dybb_ref, dybn_ref, hf_ref, hb_ref, za_ref, xb_ref, xbb_ref, xbn_ref,
             gb_ref, gbb_ref, gbn_ref, gc_ref, gcb_ref, gcn_ref, zb_ref, zbb_ref, zbn_ref, w_ref, a_ref,
             dh_ref, dp_ref, dw_ref, adj_ref, carry):
        @pl.when(pl.program_id(0) == 0)
        def _():
            carry[...] = jnp.zeros_like(carry)

        dyb, dyb_before, dyb_after = _halo_load(dyb_ref, dybb_ref, dybn_ref, n_tiles)
        xb, xb_before, xb_after = _halo_load(xb_ref, xbb_ref, xbn_ref, n_tiles)
        gb, gb_before, gb_after = _halo_load(gb_ref, gbb_ref, gbn_ref, n_tiles)
        gc, gc_before, gc_after = _halo_load(gc_ref, gcb_ref, gcn_ref, n_tiles)
        zb, zb_before, zb_after = _halo_load(zb_ref, zbb_ref, zbn_ref, n_tiles)
        w = w_ref[...]
        dya, za = dya_ref[...], za_ref[...]
        silu_za, dsilu_za = _silu_and_grad(za)
        dh_ref[...] = dya * silu_za
        _scan_tile(a_ref, dh_ref, adj_ref, carry, False, True)
        dp_ref[:, 0:D_MODEL] = (dya * (hf_ref[...] + hb_ref[...]) * dsilu_za).astype(BF16)

        silu_zb, dsilu_zb = _silu_and_grad(zb)
        p, p_before, p_after = gc * xb, gc_before * xb_before, gc_after * xb_after
        cv = _sc_conv(p, p_before, p_after, w)
        dcv = dyb * gb * silu_zb
        dcv_before = dyb_before * gb_before * _silu_and_grad(zb_before)[0]
        dcv_after = dyb_after * gb_after * _silu_and_grad(zb_after)[0]
        dpp = (w[0:1, :] * _shift_rows(dcv, dcv_before, dcv_after, 1) + w[1:2, :] * dcv
               + w[2:3, :] * _shift_rows(dcv, dcv_before, dcv_after, -1))
        dp_ref[:, D_MODEL:2 * D_MODEL] = (dpp * gc).astype(BF16)
        dp_ref[:, 2 * D_MODEL:3 * D_MODEL] = (dyb * cv * silu_zb).astype(BF16)
        dp_ref[:, 3 * D_MODEL:4 * D_MODEL] = (dpp * xb).astype(BF16)
        dp_ref[:, 4 * D_MODEL:5 * D_MODEL] = (dyb * gb * cv * dsilu_zb).astype(BF16)

        @pl.when(pl.program_id(0) == 0)
        def _():
            dw_ref[...] = jnp.zeros_like(dw_ref)

        dw_ref[0:1, :] += jnp.sum(dcv * _shift_rows(p, p_before, p_after, -1), axis=0, keepdims=True)
        dw_ref[1:2, :] += jnp.sum(dcv * p, axis=0, keepdims=True)
        dw_ref[2:3, :] += jnp.sum(dcv * _shift_rows(p, p_before, p_after, 1), axis=0, keepdims=True)

    return _call(
        body, name="even_mix_bwd", grid=(n_tiles,),
        in_specs=[row] + _halo_specs(ts, s, D_MODEL, 1) + [row, row, pl.BlockSpec((ts, D_MODEL), lambda i: (i, 1))]
        + _halo_specs(ts, s, D_MODEL, 2) + _halo_specs(ts, s, D_MODEL, 3) + _halo_specs(ts, s, D_MODEL, 4)
        + _halo_specs(ts, s, D_MODEL, 5) + [_full(sc_w.shape), pl.BlockSpec((None, ts, D_MODEL), lambda i: (2, i, 0))],
        out_specs=[row, pl.BlockSpec((ts, 5 * D_MODEL), lambda i: (i, 0)), _full(sc_w.shape), row],
        out_shape=[jax.ShapeDtypeStruct((s, D_MODEL), F32), jax.ShapeDtypeStruct((s, 5 * D_MODEL), BF16),
                   jax.ShapeDtypeStruct(sc_w.shape, F32), jax.ShapeDtypeStruct((s, D_MODEL), F32)],
        scratch_shapes=[pltpu.VMEM((SUBLANES, D_MODEL), F32)],
        args=[du, du, du, du, hf, hb, proj, *([proj] * 12), sc_w, ab], exchange=exchange)


def even_gates_bwd(proj, adj_f, adj_b, hf, hb, dh, conv_w, conv_b, gate_w, gate_b, lam, exchange=None):
    s = proj.shape[0]
    ts = min(2 * ROW_TILE, s)
    n_tiles = s // ts
    row = pl.BlockSpec((ts, D_MODEL), lambda i: (i, 0))

    def body(xa_ref, xab_ref, xan_ref, af_ref, afb_ref, afn_ref, ab_ref, abb_ref, abn_ref,
             hf_ref, hfb_ref, hfn_ref, hb_ref, hbb_ref, hbn_ref, dh_ref,
             cw_ref, cb_ref, gw_ref, gb_ref, lam_ref, dua_ref, dgw_ref, dgb_ref, dlam_ref):
        @pl.when(pl.program_id(0) == 0)
        def _():
            dgw_ref[...] = jnp.zeros_like(dgw_ref)
            dgb_ref[...] = jnp.zeros_like(dgb_ref)
            dlam_ref[...] = jnp.zeros_like(dlam_ref)

        xa, before, after = _halo_load(xa_ref, xab_ref, xan_ref, n_tiles)
        ua = _rg_conv(xa, before, after, cw_ref[...], cb_ref[...])
        lam_v = lam_ref[...]
        c = RG_C * _softplus(-lam_v)
        dc_dlam = -RG_C * _sigmoid(-lam_v)
        dh = dh_ref[...]
        adj = (_halo_load(af_ref, afb_ref, afn_ref, n_tiles), _halo_load(ab_ref, abb_ref, abn_ref, n_tiles))
        hs = (_halo_load(hf_ref, hfb_ref, hfn_ref, n_tiles), _halo_load(hb_ref, hbb_ref, hbn_ref, n_tiles))
        dua = jnp.zeros_like(ua)
        for direction in range(2):
            step = 1 if direction == 0 else -1
            g = dh + _shift_rows(*adj[direction], step)
            da_all = g * _shift_rows(*hs[direction], -step)
            dua_parts = []
            for head in range(RG_HEADS):
                lanes = slice(head * RG_HEAD_DIM, (head + 1) * RG_HEAD_DIM)
                ua_h = ua[:, lanes]
                c_h = c[direction:direction + 1, lanes]
                r, i, a, beta, inv_beta = _rg_gates(ua_h, gw_ref, gb_ref, c_h, direction, head)
                db = g[:, lanes]
                d_i = db * beta * ua_h
                dbeta = db * (i * ua_h)
                dlog_a = (da_all[:, lanes] - dbeta * a * inv_beta) * a
                dpr = -c_h * dlog_a * r * (1.0 - r)
                dpi = d_i * i * (1.0 - i)
                dua_parts.append(db * beta * i + _bdot_nt(dpr, gw_ref[2 * direction, head])
                                 + _bdot_nt(dpi, gw_ref[2 * direction + 1, head]))
                dgw_ref[2 * direction, head] += _bdot_tn(ua_h, dpr)
                dgw_ref[2 * direction + 1, head] += _bdot_tn(ua_h, dpi)
                dgb_ref[2 * direction, head:head + 1, :] += jnp.sum(dpr, axis=0, keepdims=True)
                dgb_ref[2 * direction + 1, head:head + 1, :] += jnp.sum(dpi, axis=0, keepdims=True)
                dlam_ref[direction:direction + 1, lanes] += (
                    jnp.sum(-r * dlog_a, axis=0, keepdims=True) * dc_dlam[direction:direction + 1, lanes])
            dua = dua + jnp.concatenate(dua_parts, axis=1)
        dua_ref[...] = dua

    return _call(
        body, name="even_gates_bwd", grid=(n_tiles,),
        in_specs=_halo_specs(ts, s, D_MODEL, 0) * 5 + [row] + [_full(conv_w.shape), _full(conv_b.shape), _full(gate_w.shape),
                                                             _full(gate_b.shape), _full(lam.shape)],
        out_specs=[row, _full(gate_w.shape), _full(gate_b.shape), _full(lam.shape)],
        out_shape=[jax.ShapeDtypeStruct((s, D_MODEL), F32), jax.ShapeDtypeStruct(gate_w.shape, F32),
                   jax.ShapeDtypeStruct(gate_b.shape, F32), jax.ShapeDtypeStruct(lam.shape, F32)],
        args=[proj, proj, proj, adj_f, adj_f, adj_f, adj_b, adj_b, adj_b, hf, hf, hf, hb, hb, hb, dh, conv_w, conv_b, gate_w,
              gate_b, lam], exchange=exchange)


def rg_conv_bwd(dua, proj, conv_w, exchange=None):
    s = proj.shape[0]
    ts = min(2 * ROW_TILE, s)
    n_tiles = s // ts

    def body(du_ref, dub_ref, dun_ref, xa_ref, xab_ref, xan_ref, cw_ref, dp_ref, dw_ref, db_ref):
        @pl.when(pl.program_id(0) == 0)
        def _():
            dw_ref[...] = jnp.zeros_like(dw_ref)
            db_ref[...] = jnp.zeros_like(db_ref)

        dua, dua_before, dua_after = _halo_load(du_ref, dub_ref, dun_ref, n_tiles)
        xa, xa_before, xa_after = _halo_load(xa_ref, xab_ref, xan_ref, n_tiles)
        cw = cw_ref[...]
        dxa = (cw[0:1, :] * _shift_rows(dua, dua_before, dua_after, 2) + cw[1:2, :] * _shift_rows(dua, dua_before, dua_after, 1)
               + cw[2:3, :] * dua + cw[3:4, :] * _shift_rows(dua, dua_before, dua_after, -1))
        dp_ref[...] = dxa.astype(BF16)
        for tap, offset in enumerate((-2, -1, 0, 1)):
            shifted = xa if offset == 0 else _shift_rows(xa, xa_before, xa_after, offset)
            dw_ref[tap:tap + 1, :] += jnp.sum(dua * shifted, axis=0, keepdims=True)
        db_ref[...] += jnp.sum(dua, axis=0, keepdims=True)

    return _call(
        body, name="rg_conv_bwd", grid=(n_tiles,),
        in_specs=_halo_specs(ts, s, D_MODEL, 0) * 2 + [_full(conv_w.shape)],
        out_specs=[pl.BlockSpec((ts, D_MODEL), lambda i: (i, 0)), _full(conv_w.shape), _full((1, D_MODEL))],
        out_shape=[jax.ShapeDtypeStruct((s, D_MODEL), BF16), jax.ShapeDtypeStruct(conv_w.shape, F32),
                   jax.ShapeDtypeStruct((1, D_MODEL), F32)],
        args=[dua, dua, dua, proj, proj, proj, conv_w], exchange=exchange)


def _split3(x):
    x1 = x.astype(BF16)
    rest = x - x1.astype(F32)
    x2 = rest.astype(BF16)
    return x1, x2, (rest - x2.astype(F32)).astype(BF16)


def _chunk_sum_matrix(t, reverse, transpose):
    i = lax.broadcasted_iota(jnp.int32, (t, t), 0)
    j = lax.broadcasted_iota(jnp.int32, (t, t), 1)
    if transpose:
        i, j = j, i
    same = (i // GLA_CHUNK) == (j // GLA_CHUNK)
    return jnp.where(same & ((j >= i) if reverse else (j <= i)), 1.0, 0.0).astype(BF16)


def _exact_dot(m, x):
    return sum(jnp.dot(m, part, preferred_element_type=F32) for part in _split3(x))


def _chunk_mask(t, reverse):
    i = lax.broadcasted_iota(jnp.int32, (t, t), 0)
    j = lax.broadcasted_iota(jnp.int32, (t, t), 1)
    return ((i // GLA_CHUNK) == (j // GLA_CHUNK)) & ((j >= i) if reverse else (j <= i))


def _chunk_rows(c):
    return slice(c * GLA_CHUNK, (c + 1) * GLA_CHUNK)


def _gla_gate(lr, wg, bg):
    z = _bdot(lr, wg) + bg
    log_alpha = (jnp.minimum(z, 0.0) - jnp.log(1.0 + jnp.exp(-jnp.abs(z)))) * (1.0 / GLA_NORMALIZER)
    return z, log_alpha


def _gla_tile_terms(q, k, bcum, reverse):
    n_chunks = q.shape[0] // GLA_CHUNK
    totals = []
    for c in range(n_chunks):
        edge = c * GLA_CHUNK if reverse else (c + 1) * GLA_CHUNK - 1
        totals.append(bcum[edge:edge + 1, :])
    btot = jnp.concatenate([jnp.broadcast_to(total, (GLA_CHUNK, total.shape[1])) for total in totals], axis=0)
    e_pos, e_neg, e_st = jnp.exp(bcum), jnp.exp(-bcum), jnp.exp(btot - bcum)
    return q * (GLA_DK ** -0.5) * e_pos, k * e_neg, k * e_st, e_pos, e_neg, e_st, [jnp.exp(total) for total in totals]


def _gla_specs(t, n_tiles, reverse_order):
    def tile(i):
        return n_tiles - 1 - i if reverse_order else i

    return tile, [
        pl.BlockSpec((t, GLA_KEY), lambda i: (tile(i), 0)),
        pl.BlockSpec((t, GLA_KEY), lambda i: (tile(i), 1)),
        pl.BlockSpec((t, D_MODEL), lambda i: (tile(i), 1)),
        pl.BlockSpec((t, LANES), lambda i: (tile(i), (ODD_IN_PAD - LANES) // LANES)),
    ]


def gla_fwd(proj, wg, bg, reverse, o_other=None, gnorm=None):
    s = proj.shape[0]
    t = min(ROW_TILE, s)
    n_tiles = s // t
    n_chunks = t // GLA_CHUNK
    final = o_other is not None
    tile, specs = _gla_specs(t, n_tiles, reverse)

    def body(*refs):
        if final:
            q_ref, k_ref, v_ref, lr_ref, wg_ref, bg_ref, oo_ref, r_ref, gn_ref, osum_ref, u_ref, st_ref, state = refs
        else:
            q_ref, k_ref, v_ref, lr_ref, wg_ref, bg_ref, o_ref, st_ref, state = refs
            osum_ref = o_ref

        @pl.when(pl.program_id(0) == 0)
        def _():
            state[...] = jnp.zeros_like(state)

        _, log_alpha = _gla_gate(lr_ref[...], wg_ref[...], bg_ref[...])
        bcum = _exact_dot(_chunk_sum_matrix(t, reverse, False), log_alpha)
        q, k, v = q_ref[...], k_ref[...], v_ref[...]
        q_in, k_in, k_st, _, _, _, decays = _gla_tile_terms(q, k, bcum, reverse)
        mask = _chunk_mask(t, reverse)
        order = list(range(n_chunks))[::-1] if reverse else list(range(n_chunks))
        intra, increments = [], []
        for head in range(GLA_HEADS):
            kl = slice(head * GLA_DK, (head + 1) * GLA_DK)
            vl = slice(head * GLA_DV, (head + 1) * GLA_DV)
            scores = jnp.where(mask, _bdot_nt(q_in[:, kl], k_in[:, kl]), 0.0)
            intra.append(_bdot(scores, v[:, vl]))
            increments.append([_bdot_tn(v[_chunk_rows(c), vl], k_st[_chunk_rows(c), kl]) for c in range(n_chunks)])
        for head in range(GLA_HEADS):
            kl = slice(head * GLA_DK, (head + 1) * GLA_DK)
            vl = slice(head * GLA_DV, (head + 1) * GLA_DV)
            running = state[head]
            before = [None] * n_chunks
            for c in order:
                before[c] = running
                st_ref[c, head] = running
                running = running * decays[c][:, kl] + increments[head][c]
            state[head] = running
            inter = [_bdot_nt(q_in[_chunk_rows(c), kl], before[c]) for c in range(n_chunks)]
            osum_ref[:, vl] = intra[head] + jnp.concatenate(inter, axis=0)
        if final:
            osum = osum_ref[...] + oo_ref[...]
            osum_ref[...] = osum
            silu_r, _ = _silu_and_grad(r_ref[...])
            gn = gn_ref[...]
            for head in range(GLA_HEADS):
                vl = slice(head * GLA_DV, (head + 1) * GLA_DV)
                u_ref[:, vl] = (_rms(osum[:, vl], gn[:, vl]) * silu_r[:, vl]).astype(BF16)

    row = pl.BlockSpec((t, D_MODEL), lambda i: (tile(i), 0))
    st_spec = pl.BlockSpec((n_chunks, GLA_HEADS, GLA_DV, GLA_DK), lambda i: (tile(i), 0, 0, 0))
    st_shape = jax.ShapeDtypeStruct((s // GLA_CHUNK, GLA_HEADS, GLA_DV, GLA_DK), F32)
    in_specs = specs + [_full(wg.shape), _full(bg.shape)]
    args = [proj, proj, proj, proj, wg, bg]
    if final:
        in_specs += [row, pl.BlockSpec((t, D_MODEL), lambda i: (tile(i), 2)), _full(gnorm.shape)]
        args += [o_other, proj, gnorm]
        out_specs = [row, row, st_spec]
        out_shape = [jax.ShapeDtypeStruct((s, D_MODEL), F32), jax.ShapeDtypeStruct((s, D_MODEL), BF16), st_shape]
    else:
        out_specs = [row, st_spec]
        out_shape = [jax.ShapeDtypeStruct((s, D_MODEL), F32), st_shape]
    return pl.pallas_call(
        body, name="gla_fwd_rev" if reverse else "gla_fwd", grid=(n_tiles,), in_specs=in_specs, out_specs=out_specs,
        out_shape=out_shape, scratch_shapes=[pltpu.VMEM((GLA_HEADS, GLA_DV, GLA_DK), F32)], compiler_params=_params(1),
    )(*args)


def gla_out_bwd(du, proj, osum, gnorm):
    s = proj.shape[0]
    ts = min(2 * ROW_TILE, s)
    row = pl.BlockSpec((ts, D_MODEL), lambda i: (i, 0))

    def body(du_ref, r_ref, o_ref, gn_ref, do_ref, dr_ref, dgn_ref):
        @pl.when(pl.program_id(0) == 0)
        def _():
            dgn_ref[...] = jnp.zeros_like(dgn_ref)

        du, osum, gn = du_ref[...], o_ref[...], gn_ref[...]
        silu_r, dsilu_r = _silu_and_grad(r_ref[...])
        for head in range(GLA_HEADS):
            vl = slice(head * GLA_DV, (head + 1) * GLA_DV)
            o_h, g_h, du_h = osum[:, vl], gn[:, vl], du[:, vl]
            dr_ref[:, vl] = (du_h * _rms(o_h, g_h) * dsilu_r[:, vl]).astype(BF16)
            do_h, dg_h = _rms_bwd(o_h, g_h, du_h * silu_r[:, vl])
            do_ref[:, vl] = do_h
            dgn_ref[...] += dg_h

    return pl.pallas_call(
        body, name="gla_out_bwd", grid=(s // ts,),
        in_specs=[row, pl.BlockSpec((ts, D_MODEL), lambda i: (i, 2)), row, _full(gnorm.shape)],
        out_specs=[row, row, _full((1, GLA_DV))],
        out_shape=[jax.ShapeDtypeStruct((s, D_MODEL), F32), jax.ShapeDtypeStruct((s, D_MODEL), BF16),
                   jax.ShapeDtypeStruct((1, GLA_DV), F32)],
        compiler_params=_params(1),
    )(du, proj, osum, gnorm)


def gla_bwd(proj, wg, bg, do, states, reverse, first=None):
    s = proj.shape[0]
    t = min(ROW_TILE, s)
    n_tiles = s // t
    n_chunks = t // GLA_CHUNK
    final = first is not None
    tile, specs = _gla_specs(t, n_tiles, not reverse)

    def body(*refs):
        if final:
            (q_ref, k_ref, v_ref, lr_ref, wg_ref, bg_ref, do_ref, st_ref, dqkv1_ref, dlr1_ref, dr_ref,
             dp_ref, dwg_ref, dbg_ref, dstate, dqkv, dbc, dbt) = refs
        else:
            (q_ref, k_ref, v_ref, lr_ref, wg_ref, bg_ref, do_ref, st_ref,
             dqkv, dlr_ref, dwg_ref, dbg_ref, dstate, dbc, dbt) = refs

        @pl.when(pl.program_id(0) == 0)
        def _():
            dstate[...] = jnp.zeros_like(dstate)
            dwg_ref[...] = jnp.zeros_like(dwg_ref)
            dbg_ref[...] = jnp.zeros_like(dbg_ref)

        lr, wg_v = lr_ref[...], wg_ref[...]
        z, log_alpha = _gla_gate(lr, wg_v, bg_ref[...])
        bcum = _exact_dot(_chunk_sum_matrix(t, reverse, False), log_alpha)
        q, k, v, do_v = q_ref[...], k_ref[...], v_ref[...], do_ref[...]
        q_in, k_in, k_st, e_pos, e_neg, e_st, decays = _gla_tile_terms(q, k, bcum, reverse)
        mask = _chunk_mask(t, reverse)
        order = list(range(n_chunks)) if reverse else list(range(n_chunks))[::-1]
        dq_intra, dk_intra, dv_intra, increments = [], [], [], []
        for head in range(GLA_HEADS):
            kl = slice(head * GLA_DK, (head + 1) * GLA_DK)
            vl = slice(head * GLA_DV, (head + 1) * GLA_DV)
            scores = jnp.where(mask, _bdot_nt(q_in[:, kl], k_in[:, kl]), 0.0)
            dscores = jnp.where(mask, _bdot_nt(do_v[:, vl], v[:, vl]), 0.0)
            dv_intra.append(_bdot_tn(scores, do_v[:, vl]))
            dq_intra.append(_bdot(dscores, k_in[:, kl]))
            dk_intra.append(_bdot_tn(dscores, q_in[:, kl]))
            increments.append([_bdot_tn(do_v[_chunk_rows(c), vl], q_in[_chunk_rows(c), kl]) for c in range(n_chunks)])
        for head in range(GLA_HEADS):
            kl = slice(head * GLA_DK, (head + 1) * GLA_DK)
            vl = slice(head * GLA_DV, (head + 1) * GLA_DV)
            running = dstate[head]
            after, ddecay = [None] * n_chunks, [None] * n_chunks
            for c in order:
                after[c] = running
                ddecay[c] = jnp.sum(running * st_ref[c, head], axis=0, keepdims=True)
                running = running * decays[c][:, kl] + increments[head][c]
            dstate[head] = running
            dq_inter = jnp.concatenate([_bdot(do_v[_chunk_rows(c), vl], st_ref[c, head]) for c in range(n_chunks)], axis=0)
            dv_inter = jnp.concatenate([_bdot_nt(k_st[_chunk_rows(c), kl], after[c]) for c in range(n_chunks)], axis=0)
            dk_st = jnp.concatenate([_bdot(v[_chunk_rows(c), vl], after[c]) for c in range(n_chunks)], axis=0)
            dq_in = dq_intra[head] + dq_inter
            ks_h = k_st[:, kl]
            dqkv[:, 2 * GLA_KEY + head * GLA_DV:2 * GLA_KEY + (head + 1) * GLA_DV] = dv_intra[head] + dv_inter
            dqkv[:, kl] = dq_in * (GLA_DK ** -0.5) * e_pos[:, kl]
            dqkv[:, GLA_KEY + head * GLA_DK:GLA_KEY + (head + 1) * GLA_DK] = dk_intra[head] * e_neg[:, kl] + dk_st * e_st[:, kl]
            dbc[:, kl] = dq_in * q_in[:, kl] - dk_intra[head] * k_in[:, kl] - dk_st * ks_h
            weighted = dk_st * ks_h
            for c in range(n_chunks):
                dbtot = jnp.sum(weighted[_chunk_rows(c)], axis=0, keepdims=True) + ddecay[c] * decays[c][:, kl]
                dbt[_chunk_rows(c), kl] = jnp.broadcast_to(dbtot, (GLA_CHUNK, GLA_DK))
        dlog_alpha = _exact_dot(_chunk_sum_matrix(t, reverse, True), dbc[...]) + dbt[...]
        dz = dlog_alpha * _sigmoid(-z) * (1.0 / GLA_NORMALIZER)
        dlr = _bdot_nt(dz, wg_v)
        dwg_ref[...] += _bdot_tn(lr, dz)
        dbg_ref[...] += jnp.sum(dz, axis=0, keepdims=True)
        if final:
            dp_ref[:, :2 * D_MODEL] = (dqkv[...] + dqkv1_ref[...]).astype(BF16)
            dp_ref[:, 2 * D_MODEL:3 * D_MODEL] = dr_ref[...]
            dp_ref[:, 3 * D_MODEL:] = (dlr + dlr1_ref[...]).astype(BF16)
        else:
            dlr_ref[...] = dlr

    row = pl.BlockSpec((t, D_MODEL), lambda i: (tile(i), 0))
    wide = pl.BlockSpec((t, 2 * D_MODEL), lambda i: (tile(i), 0))
    narrow = pl.BlockSpec((t, LANES), lambda i: (tile(i), 0))
    st_spec = pl.BlockSpec((n_chunks, GLA_HEADS, GLA_DV, GLA_DK), lambda i: (tile(i), 0, 0, 0))
    in_specs = specs + [_full(wg.shape), _full(bg.shape), row, st_spec]
    args = [proj, proj, proj, proj, wg, bg, do, states]
    acc_specs = [_full(wg.shape), _full(bg.shape)]
    acc_shapes = [jax.ShapeDtypeStruct(wg.shape, F32), jax.ShapeDtypeStruct(bg.shape, F32)]
    scratch = [pltpu.VMEM((GLA_HEADS, GLA_DV, GLA_DK), F32)]
    work = [pltpu.VMEM((t, GLA_KEY), F32), pltpu.VMEM((t, GLA_KEY), F32)]
    if final:
        in_specs += [wide, narrow, row]
        args += list(first)
        out_specs = [pl.BlockSpec((t, ODD_IN_PAD), lambda i: (tile(i), 0))] + acc_specs
        out_shape = [jax.ShapeDtypeStruct((s, ODD_IN_PAD), BF16)] + acc_shapes
        scratch += [pltpu.VMEM((t, 2 * D_MODEL), F32)] + work
    else:
        out_specs = [wide, narrow] + acc_specs
        out_shape = [jax.ShapeDtypeStruct((s, 2 * D_MODEL), F32), jax.ShapeDtypeStruct((s, LANES), F32)] + acc_shapes
        scratch += work
    return pl.pallas_call(
        body, name="gla_bwd_rev" if reverse else "gla_bwd", grid=(n_tiles,), in_specs=in_specs, out_specs=out_specs,
        out_shape=out_shape, scratch_shapes=scratch, compiler_params=_params(1),
    )(*args)


def pair_sum(grad, from_sibling):
    n_chips, r, w = from_sibling.shape

    def body(even_ref, odd_ref, sib_ref, o_ref):
        mine = jnp.where(lax.axis_index("c") == 1, odd_ref[...], even_ref[...])
        o_ref[...] = (mine.astype(F32) + sib_ref[...].astype(F32)).astype(o_ref.dtype)

    return pl.pallas_call(
        body, name="pair_sum", grid=(n_chips,),
        in_specs=[pl.BlockSpec((r, w), lambda k: (0, 2 * k)), pl.BlockSpec((r, w), lambda k: (0, 2 * k + 1)),
                  pl.BlockSpec((None, r, w), lambda k: (k, 0, 0))],
        out_specs=pl.BlockSpec((None, r, w), lambda k: (k, 0, 0)),
        out_shape=jax.ShapeDtypeStruct(from_sibling.shape, from_sibling.dtype), compiler_params=_params(1),
    )(grad, grad, from_sibling)


def _adamw_update(g, w, m, v):
    new_m = ADAM_B1 * m + (1.0 - ADAM_B1) * g
    new_v = ADAM_B2 * v + (1.0 - ADAM_B2) * (g * g)
    m_hat = new_m / (1.0 - ADAM_B1 ** ADAM_STEP)
    v_hat = new_v / (1.0 - ADAM_B2 ** ADAM_STEP)
    return -ADAM_LR * (m_hat / (jnp.sqrt(v_hat) + ADAM_EPS) + ADAM_WD * w), new_m, new_v


def sum_parts(parts, name):
    _, r, c = parts.shape

    def body(p_ref, o_ref):
        total = p_ref[0].astype(F32)
        for j in range(1, N_DEV):
            total = total + p_ref[j].astype(F32)
        o_ref[...] = total

    return pl.pallas_call(body, name=name, in_specs=[_full(parts.shape)], out_specs=_full((r, c)), grid=(1,),
                          out_shape=jax.ShapeDtypeStruct((r, c), F32), compiler_params=_params(1))(parts)


def adamw(parts, w, m, v, name, exchange=None):
    n, r, c = parts.shape
    tr = r
    while tr * c * 4 > ADAMW_BLOCK_BYTES and tr % (2 * SUBLANES) == 0:
        tr //= 2

    def body(p_ref, w_ref, m_ref, v_ref, g_ref, d_ref, nm_ref, nv_ref):
        g = p_ref[0].astype(F32)
        for j in range(1, n):
            g = g + p_ref[j].astype(F32)
        g_ref[...] = g
        d_ref[...], nm_ref[...], nv_ref[...] = _adamw_update(g, w_ref[...], m_ref[...], v_ref[...])

    row = pl.BlockSpec((tr, c), lambda i: (i, 0))
    return _call(
        body, name=name, grid=(r // tr,),
        in_specs=[pl.BlockSpec((n, tr, c), lambda i: (0, i, 0)), row, row, row], out_specs=[row] * 4,
        out_shape=[jax.ShapeDtypeStruct((r, c), F32)] * 4, args=[parts, w, m, v], exchange=exchange)


def _small_views(shape):
    if len(shape) == 2:
        return [((slice(None), slice(None)), (slice(None), slice(None)))]
    if len(shape) == 3:
        return [((slice(None), slice(None)), (0,))]
    rows = shape[2]
    return [((slice(k * rows, (k + 1) * rows), slice(None)), (0, k)) for k in range(shape[1])]


def adamw_small(landings, w, m, v):
    names = list(landings)
    n = len(names)
    shapes = [w[name].shape for name in names]

    def body(*refs):
        land, ws, ms, vs = refs[:n], refs[n:2 * n], refs[2 * n:3 * n], refs[3 * n:4 * n]
        outs = [refs[(4 + k) * n:(5 + k) * n] for k in range(4)]
        for k in range(n):
            total = land[k][0]
            for j in range(1, N_DEV):
                total = total + land[k][j]
            for rows, at in _small_views(shapes[k]):
                g = total[rows]
                outs[0][k][at] = g
                outs[1][k][at], outs[2][k][at], outs[3][k][at] = _adamw_update(g, ws[k][at], ms[k][at], vs[k][at])

    blocks = [_full(sh) for sh in shapes]
    outs = pl.pallas_call(
        body, name="adamw_small", grid=(1,),
        in_specs=[_full(landings[name].shape) for name in names] + blocks * 3, out_specs=blocks * 4,
        out_shape=[jax.ShapeDtypeStruct(sh, F32) for sh in shapes] * 4, compiler_params=_params(1),
    )(*[landings[name] for name in names], *[src[name] for src in (w, m, v) for name in names])
    return [dict(zip(names, outs[k * n:(k + 1) * n])) for k in range(4)]


def adamw_replicated(land_vec, land_gate_b, land_loss, names, w, m, v, gate_b):
    n = len(names)

    def body(*refs):
        vec_ref, gb_ref, loss_ref = refs[:3]
        ws, ms, vs = refs[3:3 + n], refs[3 + n:3 + 2 * n], refs[3 + 2 * n:3 + 3 * n]
        gw_ref, gm_ref, gv_ref = refs[3 + 3 * n:6 + 3 * n]
        outs = refs[6 + 3 * n:]
        vec, gb, loss = vec_ref[0], gb_ref[0], loss_ref[0]
        for j in range(1, N_DEV):
            vec, gb, loss = vec + vec_ref[j], gb + gb_ref[j], loss + loss_ref[j]
        for k in range(n):
            g = vec[k:k + 1, :]
            outs[k][...] = g
            outs[n + k][...], outs[2 * n + k][...], outs[3 * n + k][...] = _adamw_update(g, ws[k][...], ms[k][...], vs[k][...])
        outs[4 * n][...] = gb
        outs[4 * n + 1][...], outs[4 * n + 2][...], outs[4 * n + 3][...] = _adamw_update(gb, gw_ref[...], gm_ref[...], gv_ref[...])
        outs[4 * n + 4][...] = loss

    vec_block, gb_block = _full((1, D_MODEL)), _full(gate_b[0].shape)
    outs = pl.pallas_call(
        body, name="adamw_replicated", grid=(1,),
        in_specs=[_full(land_vec.shape), _full(land_gate_b.shape), _full(land_loss.shape)] + [vec_block] * (3 * n) + [gb_block] * 3,
        out_specs=[vec_block] * (4 * n) + [gb_block] * 4 + [_full(land_loss.shape[1:])],
        out_shape=[jax.ShapeDtypeStruct((1, D_MODEL), F32)] * (4 * n) + [jax.ShapeDtypeStruct(gate_b[0].shape, F32)] * 4
        + [jax.ShapeDtypeStruct(land_loss.shape[1:], F32)],
        compiler_params=_params(1),
    )(land_vec, land_gate_b, land_loss, *[src[name] for src in (w, m, v) for name in names], *gate_b)
    results = {name: [outs[k * n + i] for k in range(4)] for i, name in enumerate(names)}
    return results, outs[4 * n:4 * n + 4], outs[4 * n + 4]


SMALL_SHARDED = ("rg_conv_w", "rg_lambda", "sc_conv_w", "odd_norm_pre", "odd_norm_post", "gla_b_gate", "gla_norm_g", "gla_w_gate_lr")
SMALL_ROWS = {"rg_conv_w": (0, 4), "rg_lambda": (4, 2), "sc_conv_w": (6, 3), "odd_norm_pre": (9, 1), "odd_norm_post": (10, 1),
              "gla_b_gate": (11, 2), "gla_norm_g": (13, 1), "gla_w_gate_lr": (16, 32)}


def _pack_small(shards):
    pieces, at = [], 0
    for name in SMALL_SHARDED:
        start, rows = SMALL_ROWS[name]
        if start > at:
            pieces.append(jnp.zeros((start - at, LANES), F32))
        a = shards[name].reshape(rows, -1)
        pieces.append(jnp.pad(a, ((0, 0), (0, LANES - a.shape[1]))))
        at = start + rows
    return jnp.concatenate(pieces, axis=0)


def _unpack_gathered(g):
    def cols(name, width):
        start, rows = SMALL_ROWS[name]
        return jnp.transpose(g[:, start:start + rows, :width], (1, 0, 2)).reshape(rows, N_DEV * width)

    w_lr = cols("gla_w_gate_lr", GLA_KEY // N_DEV).reshape(2, GLA_RANK, GLA_KEY)
    return dict(rg_conv_w=cols("rg_conv_w", LANES), rg_lambda=cols("rg_lambda", LANES), sc_conv_w=cols("sc_conv_w", LANES),
                odd_norm_pre=cols("odd_norm_pre", LANES), odd_norm_post=cols("odd_norm_post", LANES),
                gla_b_gate=cols("gla_b_gate", GLA_KEY // N_DEV), gla_norm_g=cols("gla_norm_g", GLA_DV // N_DEV), gla_w_gate_lr=w_lr)


def _blocks_along_columns(a, rows):
    return jnp.transpose(a.reshape(rows, N_DEV, -1), (1, 0, 2))


def kernel(x, even_norm_pre, even_norm_post, even_w_in, rg_conv_w, rg_conv_b, rg_gate_w, rg_gate_b, rg_lambda, sc_conv_w, even_w_out, odd_norm_pre, odd_norm_post, odd_w_in, gla_w_gate_lr, gla_b_gate, gla_norm_g, odd_w_out, loss_target, m_even_norm_pre, m_even_norm_post, m_even_w_in, m_rg_conv_w, m_rg_conv_b, m_rg_gate_w, m_rg_gate_b, m_rg_lambda, m_sc_conv_w, m_even_w_out, m_odd_norm_pre, m_odd_norm_post, m_odd_w_in, m_gla_w_gate_lr, m_gla_b_gate, m_gla_norm_g, m_odd_w_out, v_even_norm_pre, v_even_norm_post, v_even_w_in, v_rg_conv_w, v_rg_conv_b, v_rg_gate_w, v_rg_gate_b, v_rg_lambda, v_sc_conv_w, v_even_w_out, v_odd_norm_pre, v_odd_norm_post, v_odd_w_in, v_gla_w_gate_lr, v_gla_b_gate, v_gla_norm_g, v_odd_w_out):
    weights = dict(even_norm_pre=even_norm_pre, even_norm_post=even_norm_post, even_w_in=even_w_in, rg_conv_w=rg_conv_w,
                   rg_conv_b=rg_conv_b, rg_gate_w=rg_gate_w, rg_gate_b=rg_gate_b, rg_lambda=rg_lambda, sc_conv_w=sc_conv_w,
                   even_w_out=even_w_out, odd_norm_pre=odd_norm_pre, odd_norm_post=odd_norm_post, odd_w_in=odd_w_in,
                   gla_w_gate_lr=gla_w_gate_lr, gla_b_gate=gla_b_gate, gla_norm_g=gla_norm_g, odd_w_out=odd_w_out)
    m_in = dict(even_norm_pre=m_even_norm_pre, even_norm_post=m_even_norm_post, even_w_in=m_even_w_in, rg_conv_w=m_rg_conv_w,
                rg_conv_b=m_rg_conv_b, rg_gate_w=m_rg_gate_w, rg_gate_b=m_rg_gate_b, rg_lambda=m_rg_lambda, sc_conv_w=m_sc_conv_w,
                even_w_out=m_even_w_out, odd_norm_pre=m_odd_norm_pre, odd_norm_post=m_odd_norm_post, odd_w_in=m_odd_w_in,
                gla_w_gate_lr=m_gla_w_gate_lr, gla_b_gate=m_gla_b_gate, gla_norm_g=m_gla_norm_g, odd_w_out=m_odd_w_out)
    v_in = dict(even_norm_pre=v_even_norm_pre, even_norm_post=v_even_norm_post, even_w_in=v_even_w_in, rg_conv_w=v_rg_conv_w,
                rg_conv_b=v_rg_conv_b, rg_gate_w=v_rg_gate_w, rg_gate_b=v_rg_gate_b, rg_lambda=v_rg_lambda, sc_conv_w=v_sc_conv_w,
                even_w_out=v_even_w_out, odd_norm_pre=v_odd_norm_pre, odd_norm_post=v_odd_norm_post, odd_w_in=v_odd_w_in,
                gla_w_gate_lr=v_gla_w_gate_lr, gla_b_gate=v_gla_b_gate, gla_norm_g=v_gla_norm_g, odd_w_out=v_odd_w_out)
    names = list(weights)
    shapes = {n: weights[n].shape for n in names}
    xs = x[0]
    tgt = loss_target[0]

    proj_e, h_e, w_in_e, small_all = gather_matmul(xs, even_norm_pre, even_w_in[0].astype(BF16),
                                                   _pack_small({n: weights[n][0] for n in SMALL_SHARDED}), 2 * MM_TILE)
    small = _unpack_gathered(small_all)
    gate_w = rg_gate_w[0].reshape(4, RG_HEADS, RG_HEAD_DIM, RG_HEAD_DIM).astype(BF16)
    gate_b = rg_gate_b[0].reshape(4, RG_HEADS, RG_HEAD_DIM)
    conv_b = rg_conv_b
    wg_pad = [jnp.pad(small["gla_w_gate_lr"][d], ((GLA_RANK * d, LANES - GLA_RANK * (d + 1)), (0, 0))).astype(BF16) for d in range(2)]
    bg = [small["gla_b_gate"][d:d + 1] for d in range(2)]
    gnorm = jnp.tile(small["gla_norm_g"], (1, GLA_HEADS))

    half = D_MODEL // 2
    behind_gates = Exchange()
    behind_gates.gather(even_w_out[0].astype(BF16), via_sibling=True)
    behind_gates.gather(odd_w_in[0, :half].astype(BF16), via_sibling=True)
    (ab, hf), (w_out_e, w_in_o_top) = even_gates_fwd(proj_e, small["rg_conv_w"], conv_b, gate_w, gate_b, small["rg_lambda"],
                                                     exchange=behind_gates)
    w_out_e = w_out_e.reshape(2 * D_MODEL, D_MODEL)
    behind_mix_fwd = Exchange()
    behind_mix_fwd.gather(odd_w_in[0, half:].astype(BF16), via_sibling=True)
    behind_mix_fwd.gather(odd_w_out[0].astype(BF16), via_sibling=True)
    (u_e, hb), (w_in_o_bottom, w_out_o) = even_mix_fwd(ab, hf, proj_e, small["sc_conv_w"], exchange=behind_mix_fwd)
    w_out_o = w_out_o.reshape(D_MODEL, D_MODEL)
    w_in_o = jnp.concatenate([jnp.transpose(part, (1, 0, 2)).reshape(half, ODD_IN) for part in (w_in_o_top, w_in_o_bottom)], axis=0)
    w_in_o = jnp.pad(w_in_o, ((0, 0), (0, ODD_IN_PAD - ODD_IN)))
    y_e, x1 = matmul_post(u_e, w_out_e, xs, even_norm_post, "even_out")

    proj_o, h_o = rms_matmul(x1, small["odd_norm_pre"], w_in_o, MM_TILE, ODD_IN_PAD, "odd_in")
    o_f, st_f = gla_fwd(proj_o, wg_pad[0], bg[0], False)
    osum, u_o, st_b = gla_fwd(proj_o, wg_pad[1], bg[1], True, o_other=o_f, gnorm=gnorm)
    y_o, dout, loss_part = matmul_post(u_o, w_out_o, x1, small["odd_norm_post"], "odd_out", target=tgt)

    du_o, dy_o, d_odd_norm_post = normbwd_matmul_nt(y_o, small["odd_norm_post"], dout, w_out_o, D_MODEL, "odd_out_bwd")
    d_w_out_o = matmul_tn(u_o, dy_o, D_MODEL, D_MODEL, 4 * MM_TILE, BF16, "odd_w_out_grad")
    do, dr, d_gnorm = gla_out_bwd(du_o, proj_o, osum, gnorm)
    dqkv_f, dlr_f, dwg_f, dbg_f = gla_bwd(proj_o, wg_pad[0], bg[0], do, st_f, False)
    dproj_o, dwg_b, dbg_b = gla_bwd(proj_o, wg_pad[1], bg[1], do, st_b, True, first=(dqkv_f, dlr_f, dr))
    dx1, d_odd_norm_pre = matmul_nt_normbwd(dproj_o, w_in_o, x1, small["odd_norm_pre"], dout, MM_TILE, ODD_IN_PAD, "odd_in_bwd")
    d_w_in_o = matmul_tn(h_o, dproj_o, D_MODEL, ODD_IN_PAD // 5, 8 * MM_TILE, BF16, "odd_w_in_grad")

    landed = {}
    behind_out = Exchange()
    behind_out.scatter(d_w_out_o.reshape(N_DEV, D_MODEL // N_DEV, D_MODEL))
    behind_out.scatter(d_odd_norm_pre, columns=True)
    behind_out.scatter(d_odd_norm_post, columns=True)
    behind_out.scatter(_blocks_along_columns(jnp.concatenate([dbg_f, dbg_b], axis=0), 2))
    behind_out.scatter(_blocks_along_columns(d_gnorm, 1))
    behind_out.scatter(_blocks_along_columns(jnp.concatenate([dwg_f[:GLA_RANK], dwg_b[GLA_RANK:2 * GLA_RANK]], axis=0), 2 * GLA_RANK))
    (du_e, dy_e, d_even_norm_post), got = normbwd_matmul_nt(y_e, even_norm_post, dx1, w_out_e, 2 * D_MODEL, "even_out_bwd",
                                                           exchange=behind_out)
    p_w_out_o = got[0]
    for n, part in zip(("odd_norm_pre", "odd_norm_post", "gla_b_gate", "gla_norm_g", "gla_w_gate_lr"), got[1:]):
        landed[n] = part
    d_w_out_e = matmul_tn(u_e, dy_e, D_MODEL, D_MODEL, 4 * MM_TILE, BF16, "even_w_out_grad")
    behind_mix = Exchange()
    behind_mix.scatter(d_w_out_e.reshape(N_DEV, 2 * D_MODEL // N_DEV, D_MODEL))
    (dh, drest, d_sc_w, adj_b), (p_w_out_e,) = even_mix_bwd(du_e, hf, hb, proj_e, small["sc_conv_w"], ab, exchange=behind_mix)
    adj_f = linear_scan(ab, 0, dh.reshape(1, *dh.shape), 0, True, True, "scan_fwd_adjoint")
    behind_gates_bwd = Exchange()
    behind_gates_bwd.scatter(jnp.transpose(d_w_in_o[:, :ODD_IN].reshape(D_MODEL, N_DEV, ODD_SHARD), (1, 0, 2)))
    behind_gates_bwd.scatter(d_sc_w, columns=True)
    (dua, d_gate_w, d_gate_b, d_lam), (p_w_in_o, landed["sc_conv_w"]) = even_gates_bwd(
        proj_e, adj_f, adj_b, hf, hb, dh, small["rg_conv_w"], conv_b, gate_w, gate_b, small["rg_lambda"], exchange=behind_gates_bwd)
    gate_w_rows = 4 * RG_HEADS * RG_HEAD_DIM
    behind_conv = Exchange()
    behind_conv.scatter(d_gate_w.reshape(N_DEV, gate_w_rows // N_DEV, RG_HEAD_DIM))
    behind_conv.scatter(d_lam, columns=True)
    (dxa, d_conv_w, d_conv_b), (p_gate_w, landed["rg_lambda"]) = rg_conv_bwd(dua, proj_e, small["rg_conv_w"], exchange=behind_conv)
    behind_w_grad = Exchange()
    behind_w_grad.gather(sum_parts(p_gate_w, "sum_gate_w"))
    d_w_in_e, (g_gate_w_all,) = matmul_tn(h_e, drest, D_MODEL, D_MODEL, 4 * MM_TILE, BF16, "even_w_in_grad",
                                          exchange=behind_w_grad, b_first=dxa)
    to_sibling = Exchange()
    to_sibling.to_sibling(d_w_in_e)
    to_sibling.scatter(d_conv_w, columns=True)
    from_sibling, landed["rg_conv_w"] = run_exchange(to_sibling, "scatter_to_sibling")
    behind_in_bwd = Exchange()
    behind_in_bwd.among_chips(pair_sum(d_w_in_e, from_sibling))
    (grad_x, d_even_norm_pre), (p_w_in_e,) = matmul_nt_normbwd(
        drest, w_in_e, xs, even_norm_pre, dx1, 2 * MM_TILE, D_MODEL, "even_in_bwd", exchange=behind_in_bwd, first=dxa)
    last = Exchange()
    replicated_vecs = ("even_norm_pre", "even_norm_post", "rg_conv_b")
    last.gather(jnp.concatenate([d_even_norm_pre, d_even_norm_post, d_conv_b], axis=0))
    last.gather(d_gate_b.reshape(4 * RG_HEADS, RG_HEAD_DIM))
    last.gather(loss_part)

    results = {}

    def update(name, parts_, shape2d, exchange=None):
        outs = adamw(parts_, weights[name][0].reshape(shape2d), m_in[name][0].reshape(shape2d), v_in[name][0].reshape(shape2d),
                     "adamw_" + name, exchange=exchange)
        if exchange is not None:
            outs, gathered = outs
        results[name] = [o.reshape(shapes[name]) for o in outs]
        return gathered if exchange is not None else None

    land_vec, land_gate_b, land_loss = update("even_w_in", p_w_in_e, (D_MODEL, EVEN_SHARD), exchange=last)
    update("even_w_out", p_w_out_e, (2 * D_MODEL // N_DEV, D_MODEL))
    update("odd_w_in", p_w_in_o, (D_MODEL, ODD_SHARD))
    update("odd_w_out", p_w_out_o, (D_MODEL // N_DEV, D_MODEL))
    update("rg_gate_w", g_gate_w_all.reshape(1, gate_w_rows, RG_HEAD_DIM), (gate_w_rows, RG_HEAD_DIM))
    small_out = adamw_small({n: landed[n] for n in SMALL_SHARDED}, weights, m_in, v_in)
    for n in SMALL_SHARDED:
        results[n] = [o[n] for o in small_out]
    gate_b_shape = (4 * RG_HEADS, RG_HEAD_DIM)
    rep_out, gate_b_out, loss_all = adamw_replicated(land_vec, land_gate_b, land_loss, replicated_vecs, weights, m_in, v_in,
                                                     [src["rg_gate_b"].reshape(gate_b_shape) for src in (weights, m_in, v_in)])
    results.update(rep_out)
    results["rg_gate_b"] = [o.reshape(shapes["rg_gate_b"]) for o in gate_b_out]

    return (loss_all[0, 0], grad_x.reshape(x.shape), *[results[n][0] for n in names], *[results[n][1] for n in names],
            *[results[n][2] for n in names], *[results[n][3] for n in names])
```

```python
import functools

import jax
import jax.numpy as jnp
from jax import lax
from jax.experimental import pallas as pl
from jax.experimental.pallas import tpu as pltpu

F32 = jnp.float32
BF16 = jnp.bfloat16

N_DEV = 8
D_MODEL = 1024
NORM_EPS = 1e-6
RG_HEADS = 8
RG_HEAD_DIM = 128
RG_C = 8.0
GLA_HEADS = 4
GLA_DK = 128
GLA_DV = 256
GLA_KEY = 512
GLA_RANK = 16
GLA_NORMALIZER = 16.0
GLA_CHUNK = 64
EVEN_IN = 6144
ODD_IN = 3104
ODD_IN_PAD = 3200
ODD_SHARD = ODD_IN // N_DEV
EVEN_SHARD = EVEN_IN // N_DEV
ADAM_LR = 0.001
ADAM_B1 = 0.9
ADAM_B2 = 0.999
ADAM_EPS = 1e-08
ADAM_WD = 0.01
ADAM_STEP = 10

SMALLEST_NORMAL = 1.1754944e-38
SUBLANES = 8
LANES = 128
VMEM_LIMIT_BYTES = 48 * 2 ** 20
ROW_TILE = 256
MM_TILE = 512
ADAMW_BLOCK_BYTES = 2 ** 20
PACK_ROWS = 48
MESH_ID = pl.DeviceIdType.MESH


def _params(n_grid):
    return pltpu.CompilerParams(dimension_semantics=("arbitrary",) * n_grid, vmem_limit_bytes=VMEM_LIMIT_BYTES)


def _bdot(a, b):
    return jnp.dot(a.astype(BF16), b.astype(BF16), preferred_element_type=F32)


def _bdot_nt(a, b):
    return lax.dot_general(a.astype(BF16), b.astype(BF16), (((1,), (1,)), ((), ())), preferred_element_type=F32)


def _bdot_tn(a, b):
    return lax.dot_general(a.astype(BF16), b.astype(BF16), (((0,), (0,)), ((), ())), preferred_element_type=F32)


def _rstd(x):
    return lax.rsqrt(jnp.mean(x * x, axis=-1, keepdims=True) + NORM_EPS)


def _rms(x, g):
    return x * _rstd(x) * g


def _rms_bwd(x, g, dy):
    xh = x * _rstd(x)
    dyg = dy * g
    dx = _rstd(x) * (dyg - xh * jnp.mean(dyg * xh, axis=-1, keepdims=True))
    return dx, jnp.sum(dy * xh, axis=0, keepdims=True)


def _sigmoid(z):
    return 0.5 * jnp.tanh(0.5 * z) + 0.5


def _silu_and_grad(z):
    s = _sigmoid(z)
    return z * s, s * (1.0 + z * (1.0 - s))


def _softplus(z):
    return jnp.maximum(z, 0.0) + jnp.log(1.0 + jnp.exp(-jnp.abs(z)))


def _shift_rows(cur, before, after, d):
    ts = cur.shape[0]
    row = lax.broadcasted_iota(jnp.int32, (SUBLANES, cur.shape[1]), 0)
    out = pltpu.roll(cur, (-d) % ts, 0)
    if d < 0:
        edge = jnp.where(row < -d, pltpu.roll(before, (-d) % SUBLANES, 0), out[:SUBLANES])
        return jnp.concatenate([edge, out[SUBLANES:]], axis=0)
    edge = jnp.where(row >= SUBLANES - d, pltpu.roll(after, (-d) % SUBLANES, 0), out[ts - SUBLANES:])
    return jnp.concatenate([out[:ts - SUBLANES], edge], axis=0)


def _halo_specs(ts, s, width, col, tile=lambda i: i):
    per = ts // SUBLANES
    last = s // SUBLANES - 1
    return [
        pl.BlockSpec((ts, width), lambda i: (tile(i), col)),
        pl.BlockSpec((SUBLANES, width), lambda i: (jnp.maximum(tile(i) * per - 1, 0), col)),
        pl.BlockSpec((SUBLANES, width), lambda i: (jnp.minimum((tile(i) + 1) * per, last), col)),
    ]


def _halo_load(cur_ref, before_ref, after_ref, n_tiles, tile=lambda i: i):
    i = tile(pl.program_id(0))
    before = jnp.where(i > 0, before_ref[...], 0.0)
    after = jnp.where(i < n_tiles - 1, after_ref[...], 0.0)
    return cur_ref[...], before, after


def _full(shape):
    return pl.BlockSpec(shape, lambda *_: (0,) * len(shape))


def _peer(x, y, c, mask):
    px, py, pc = x ^ (mask >> 2), y ^ ((mask >> 1) & 1), c ^ (mask & 1)
    return (px, py, pc), 4 * px + 2 * py + pc


class Exchange:
    SIBLING = 1
    OTHER_CHIPS = (2, 4, 6)

    def __init__(self):
        self.args, self.out_shape, self._kinds = [], [], []

    def gather(self, block, columns=False, via_sibling=False):
        shape = (block.shape[0], N_DEV * block.shape[1]) if columns else (N_DEV,) + block.shape
        return self._add(block, shape, ("gather", columns, via_sibling))

    def scatter(self, stack, columns=False):
        shape = (N_DEV, stack.shape[0], stack.shape[1] // N_DEV) if columns else stack.shape
        return self._add(stack, shape, ("scatter", columns, False))

    def _add(self, arg, shape, kind):
        self.args.append(arg)
        self.out_shape.append(jax.ShapeDtypeStruct(shape, arg.dtype))
        self._kinds.append(kind)
        return len(self.args) - 1

    def semaphores(self):
        n = len(self.args)
        return [pltpu.SemaphoreType.DMA((n, N_DEV - 1)), pltpu.SemaphoreType.DMA((n, N_DEV - 1)), pltpu.SemaphoreType.DMA((n,))]

    def to_sibling(self, array):
        shape = (N_DEV // 2, array.shape[0], array.shape[1] // N_DEV)
        return self._add(array, shape, ("to_sibling", True, False))

    def among_chips(self, stack):
        return self._add(stack, stack.shape, ("among_chips", False, False))

    def _copies(self, position, in_refs, out_refs):
        x, y, c, me = position
        for arr, ((kind, columns, via_sibling), src, out) in enumerate(zip(self._kinds, in_refs, out_refs)):
            if kind == "to_sibling":
                width = src.shape[-1] // N_DEV
                for k in range(N_DEV // 2):
                    block = src.at[:, pl.ds(pl.multiple_of((2 * k + 1 - c) * width, LANES), width)]
                    yield arr, k + 1, block, out.at[k], out.at[k], False, self.SIBLING
                continue
            for mask in range(N_DEV):
                _, peer_id = _peer(x, y, c, mask)
                relayed = via_sibling and mask not in (0, self.SIBLING) + self.OTHER_CHIPS
                if kind == "among_chips":
                    if mask in (0,) + self.OTHER_CHIPS:
                        yield arr, mask, src.at[peer_id // 2], out.at[me // 2], out.at[peer_id // 2], False, mask
                elif kind == "gather":
                    if columns:
                        width = src.shape[-1]
                        yield (arr, mask, src, out.at[:, pl.ds(pl.multiple_of(me * width, LANES), width)],
                               out.at[:, pl.ds(pl.multiple_of(peer_id * width, LANES), width)], relayed, mask)
                    else:
                        yield arr, mask, src, out.at[me], out.at[peer_id], relayed, mask
                else:
                    if columns:
                        width = src.shape[-1] // N_DEV
                        block = src.at[:, pl.ds(pl.multiple_of(peer_id * width, LANES), width)]
                    else:
                        block = src.at[peer_id]
                    yield arr, mask, block, out.at[me], out.at[peer_id], False, mask

    def _remote(self, position, sems, arr, slot, to_mask, src, dst):
        x, y, c, _ = position
        return pltpu.make_async_remote_copy(src_ref=src, dst_ref=dst, send_sem=sems[0].at[arr, slot - 1], recv_sem=sems[1].at[arr, slot - 1],
                                            device_id=_peer(x, y, c, to_mask)[0], device_id_type=MESH_ID)

    def start(self, position, in_refs, out_refs, sems):
        for arr, slot, src, dst, _, relayed, to_mask in self._copies(position, in_refs, out_refs):
            if slot == 0:
                pltpu.make_async_copy(src, dst, sems[2].at[arr]).start()
            elif not relayed:
                self._remote(position, sems, arr, slot, to_mask, src, dst).start()

    def wait(self, position, in_refs, out_refs, sems):
        copies = list(self._copies(position, in_refs, out_refs))
        landings = {(arr, slot): landing for arr, slot, _, _, landing, _, _ in copies}
        passed_on = set()
        for arr, mask, src, _, landing, relayed, _ in copies:
            if relayed:
                held = landings[arr, mask ^ self.SIBLING]
                self._remote(position, sems, arr, mask ^ self.SIBLING, mask ^ self.SIBLING, src, held).wait_recv()
                self._remote(position, sems, arr, mask, self.SIBLING, held, held).start()
                passed_on.add((arr, mask ^ self.SIBLING))
        for arr, slot, src, dst, landing, relayed, to_mask in copies:
            if slot == 0:
                pltpu.make_async_copy(src, dst, sems[2].at[arr]).wait()
                continue
            if (arr, slot) not in passed_on:
                self._remote(position, sems, arr, slot, to_mask, src, landing).wait_recv()
            if relayed:
                held = landings[arr, slot ^ self.SIBLING]
                self._remote(position, sems, arr, slot, self.SIBLING, held, held).wait_send()
            else:
                self._remote(position, sems, arr, slot, to_mask, src, dst).wait_send()


def _call(body, *, name, grid, in_specs, out_specs, out_shape, args, scratch_shapes=(), exchange=None):
    single = not isinstance(out_shape, (list, tuple))
    if single:
        out_specs, out_shape = [out_specs], [out_shape]
    params = _params(len(grid))
    if exchange is None:
        outs = pl.pallas_call(body, name=name, grid=grid, in_specs=in_specs, out_specs=out_specs, out_shape=out_shape,
                              scratch_shapes=list(scratch_shapes), compiler_params=params)(*args)
        return outs[0] if single else outs
    counts = (len(args), len(exchange.args), len(out_shape), len(exchange.out_shape), len(scratch_shapes), 3)

    def wrapped(*refs):
        groups, at = [], 0
        for n in counts:
            groups.append(refs[at:at + n])
            at += n
        main_in, ex_in, main_out, ex_out, main_scratch, sems = groups
        x, y, c = lax.axis_index("x"), lax.axis_index("y"), lax.axis_index("c")
        position = (x, y, c, 4 * x + 2 * y + c)
        ids = [pl.program_id(a) for a in range(len(grid))]
        first = functools.reduce(jnp.logical_and, [i == 0 for i in ids])
        last = functools.reduce(jnp.logical_and, [i == g - 1 for i, g in zip(ids, grid)])

        @pl.when(first)
        def _():
            exchange.start(position, ex_in, ex_out, sems)

        body(*main_in, *main_out, *main_scratch)

        @pl.when(last)
        def _():
            exchange.wait(position, ex_in, ex_out, sems)

    hbm = pl.BlockSpec(memory_space=pl.ANY)
    outs = pl.pallas_call(
        wrapped, name=name, grid=grid, in_specs=list(in_specs) + [hbm] * counts[1], out_specs=list(out_specs) + [hbm] * counts[3],
        out_shape=list(out_shape) + exchange.out_shape, scratch_shapes=list(scratch_shapes) + exchange.semaphores(),
        compiler_params=params)(*args, *exchange.args)
    main = outs[:counts[2]]
    return (main[0] if single else main), outs[counts[2]:]


def run_exchange(exchange, name):
    return _call(lambda: None, name=name, grid=(1,), in_specs=[], out_specs=[], out_shape=[], args=[], exchange=exchange)[1]


def gather_matmul(x, g, w_block, small_block, tm):
    s, d = x.shape
    width = w_block.shape[1]
    pair = 2 * width
    n_chips = N_DEV // 2
    tm = min(tm, s)
    n_i = s // tm
    sibling = Exchange.SIBLING

    def body(chips_ref, x_ref, g_ref, wb_ref, sb_ref, proj_ref, h_ref, w_ref, small_ref, h_all, w_pair, send, recv, local, load_sem):
        j, i = pl.program_id(0), pl.program_id(1)
        xx, yy, cc = lax.axis_index("x"), lax.axis_index("y"), lax.axis_index("c")
        me = 4 * xx + 2 * yy + cc

        def block_of(dev):
            return w_ref.at[:, pl.ds(pl.multiple_of(dev * width, LANES), width)]

        def remote(arr, slot, to_mask, src, dst):
            return pltpu.make_async_remote_copy(src_ref=src, dst_ref=dst, send_sem=send.at[arr, slot - 1], recv_sem=recv.at[arr, slot - 1],
                                                device_id=_peer(xx, yy, cc, to_mask)[0], device_id_type=MESH_ID)

        @pl.when((j == 0) & (i == 0))
        def _():
            pltpu.make_async_copy(wb_ref, block_of(me), local.at[0]).start()
            pltpu.make_async_copy(sb_ref, small_ref.at[me], local.at[1]).start()
            for mask in (sibling,) + Exchange.OTHER_CHIPS[:-1]:
                remote(0, mask, mask, wb_ref, block_of(me)).start()
            for mask in range(1, N_DEV):
                remote(1, mask, mask, sb_ref, small_ref.at[me]).start()

        for step in range(n_chips):
            @pl.when((j == step) & (i == 0))
            def _(step=step):
                if step == 0:
                    pltpu.make_async_copy(wb_ref, block_of(me), local.at[0]).wait()
                    remote(0, sibling, sibling, wb_ref, block_of(me ^ sibling)).wait_recv()
                else:
                    mask = 2 * step
                    remote(0, mask, mask, wb_ref, block_of(me ^ mask)).wait_recv()
                    if step == 1:
                        remote(0, Exchange.OTHER_CHIPS[-1], Exchange.OTHER_CHIPS[-1], wb_ref, block_of(me)).start()
                    remote(0, mask | sibling, sibling, block_of(me ^ mask), block_of(me ^ mask)).start()
                    remote(0, mask | sibling, mask | sibling, wb_ref, block_of(me ^ (mask | sibling))).wait_recv()
                load = pltpu.make_async_copy(w_ref.at[:, pl.ds(pl.multiple_of(chips_ref[step] * pair, LANES), pair)], w_pair, load_sem)
                load.start()
                load.wait()

        rows = pl.ds(pl.multiple_of(i * tm, tm), tm)

        @pl.when(j == 0)
        def _():
            h = _rms(x_ref[...], g_ref[...]).astype(BF16)
            h_all[rows, :] = h
            h_ref[...] = h

        proj_ref[...] = jnp.dot(h_all[rows, :], w_pair[...], preferred_element_type=F32)

        @pl.when((j == n_chips - 1) & (i == n_i - 1))
        def _():
            pltpu.make_async_copy(sb_ref, small_ref.at[me], local.at[1]).wait()
            for mask in range(1, N_DEV):
                remote(1, mask, mask, sb_ref, small_ref.at[me ^ mask]).wait_recv()
                remote(1, mask, mask, sb_ref, small_ref.at[me]).wait_send()
            for mask in (sibling,) + Exchange.OTHER_CHIPS:
                remote(0, mask, mask, wb_ref, block_of(me)).wait_send()
            for mask in Exchange.OTHER_CHIPS:
                remote(0, mask | sibling, sibling, block_of(me ^ mask), block_of(me ^ mask)).wait_send()

    def first_pass_row(j, i, chips):
        return jnp.where(j == 0, i, n_i - 1), 0

    hbm = pl.BlockSpec(memory_space=pl.ANY)
    my_chip = 2 * lax.axis_index("x") + lax.axis_index("y")
    chips = (my_chip ^ jnp.arange(n_chips)).astype(jnp.int32)
    grid_spec = pltpu.PrefetchScalarGridSpec(
        num_scalar_prefetch=1, grid=(n_chips, n_i),
        in_specs=[pl.BlockSpec((tm, d), first_pass_row), pl.BlockSpec((1, d), lambda j, i, chips: (0, 0)), hbm, hbm],
        out_specs=[pl.BlockSpec((tm, pair), lambda j, i, chips: (i, chips[j])), pl.BlockSpec((tm, d), first_pass_row), hbm, hbm],
        scratch_shapes=[pltpu.VMEM((s, d), BF16), pltpu.VMEM((d, pair), BF16), pltpu.SemaphoreType.DMA((2, N_DEV - 1)),
                        pltpu.SemaphoreType.DMA((2, N_DEV - 1)), pltpu.SemaphoreType.DMA((2,)), pltpu.SemaphoreType.DMA(())])
    return pl.pallas_call(
        body, name="even_in", grid_spec=grid_spec,
        out_shape=[jax.ShapeDtypeStruct((s, N_DEV * width), F32), jax.ShapeDtypeStruct((s, d), BF16),
                   jax.ShapeDtypeStruct((d, N_DEV * width), w_block.dtype), jax.ShapeDtypeStruct((N_DEV,) + small_block.shape, small_block.dtype)],
        compiler_params=_params(2),
    )(chips, x, g, w_block, small_block)


def rms_matmul(x, g, w, tm, tn, name, exchange=None):
    s, d = x.shape
    n = w.shape[1]
    tm = min(tm, s)

    def body(x_ref, g_ref, w_ref, o_ref, h_ref):
        @pl.when(pl.program_id(1) == 0)
        def _():
            h_ref[...] = _rms(x_ref[...], g_ref[...]).astype(BF16)

        o_ref[...] = jnp.dot(h_ref[...], w_ref[...], preferred_element_type=F32)

    return _call(
        body, name=name, grid=(s // tm, n // tn),
        in_specs=[pl.BlockSpec((tm, d), lambda i, j: (i, 0)), _full((1, d)), pl.BlockSpec((d, tn), lambda i, j: (0, j))],
        out_specs=[pl.BlockSpec((tm, tn), lambda i, j: (i, j)), pl.BlockSpec((tm, d), lambda i, j: (i, 0))],
        out_shape=[jax.ShapeDtypeStruct((s, n), F32), jax.ShapeDtypeStruct((s, d), BF16)],
        args=[x, g, w], exchange=exchange)


def matmul_post(u, w, xres, g, name, target=None):
    s, k = u.shape
    d = w.shape[1]
    tm = min(MM_TILE, s)
    with_loss = target is not None

    def body(*refs):
        if with_loss:
            u_ref, w_ref, x_ref, g_ref, t_ref, y_ref, dout_ref, loss_ref = refs
        else:
            u_ref, w_ref, x_ref, g_ref, y_ref, out_ref = refs
        y = jnp.dot(u_ref[...], w_ref[...], preferred_element_type=F32)
        y_ref[...] = y
        out = x_ref[...] + _rms(y, g_ref[...])
        if with_loss:
            @pl.when(pl.program_id(0) == 0)
            def _():
                loss_ref[...] = jnp.zeros_like(loss_ref)

            diff = out - t_ref[...]
            dout_ref[...] = diff * (1.0 / d)
            loss_ref[...] += 0.5 * jnp.sum(jnp.mean(diff * diff, axis=-1, keepdims=True))
        else:
            out_ref[...] = out

    row = pl.BlockSpec((tm, d), lambda i: (i, 0))
    in_specs = [pl.BlockSpec((tm, k), lambda i: (i, 0)), _full((k, d)), row, _full((1, d))]
    args = [u, w, xres, g]
    out_specs = [row, row]
    out_shape = [jax.ShapeDtypeStruct((s, d), F32), jax.ShapeDtypeStruct((s, d), F32)]
    if with_loss:
        in_specs.append(row)
        args.append(target)
        out_specs.append(_full((SUBLANES, LANES)))
        out_shape.append(jax.ShapeDtypeStruct((SUBLANES, LANES), F32))
    return pl.pallas_call(body, name=name, grid=(s // tm,), in_specs=in_specs, out_specs=out_specs,
                          out_shape=out_shape, compiler_params=_params(1))(*args)


def _gla_out_bwd(du, r, osum, gn, do_ref, dr_ref, dgn_ref):
    silu_r, dsilu_r = _silu_and_grad(r)
    for head in range(GLA_HEADS):
        vl = slice(head * GLA_DV, (head + 1) * GLA_DV)
        o_h, g_h, du_h = osum[:, vl], gn[:, vl], du[:, vl]
        dr_ref[:, vl] = (du_h * _rms(o_h, g_h) * dsilu_r[:, vl]).astype(BF16)
        do_h, dg_h = _rms_bwd(o_h, g_h, du_h * silu_r[:, vl])
        do_ref[:, vl] = do_h
        dgn_ref[...] += dg_h


def normbwd_matmul_nt(y, g, dout, w, tn, name, exchange=None, gla=None):
    s, d = y.shape
    n = w.shape[0]
    tm = min(MM_TILE, s)

    def body(*refs):
        if gla is None:
            y_ref, g_ref, dout_ref, w_ref, du_ref, dy_ref, dg_ref = refs
        else:
            y_ref, g_ref, dout_ref, w_ref, r_ref, o_ref, gn_ref, do_ref, dr_ref, dy_ref, dg_ref, dgn_ref = refs
        i, j = pl.program_id(0), pl.program_id(1)

        @pl.when(j == 0)
        def _():
            dy, dg = _rms_bwd(y_ref[...], g_ref[...], dout_ref[...])
            dy_ref[...] = dy.astype(BF16)

            @pl.when(i == 0)
            def _():
                dg_ref[...] = jnp.zeros_like(dg_ref)
                if gla is not None:
                    dgn_ref[...] = jnp.zeros_like(dgn_ref)

            dg_ref[...] += dg

        du = lax.dot_general(dy_ref[...], w_ref[...], (((1,), (1,)), ((), ())), preferred_element_type=F32)
        if gla is None:
            du_ref[...] = du
        else:
            _gla_out_bwd(du, r_ref[...], o_ref[...], gn_ref[...], do_ref, dr_ref, dgn_ref)

    row = pl.BlockSpec((tm, d), lambda i, j: (i, 0))
    in_specs = [row, _full((1, d)), row, pl.BlockSpec((tn, d), lambda i, j: (j, 0))]
    args = [y, g, dout, w]
    tail_specs = [row, _full((1, d))]
    tail_shapes = [jax.ShapeDtypeStruct((s, d), BF16), jax.ShapeDtypeStruct((1, d), F32)]
    if gla is None:
        out_specs = [pl.BlockSpec((tm, tn), lambda i, j: (i, j))] + tail_specs
        out_shape = [jax.ShapeDtypeStruct((s, n), F32)] + tail_shapes
    else:
        proj, osum, gnorm = gla
        assert n == tn == D_MODEL
        in_specs += [pl.BlockSpec((tm, D_MODEL), lambda i, j: (i, 2)), row, _full(gnorm.shape)]
        args += [proj, osum, gnorm]
        out_specs = [row, row] + tail_specs + [_full((1, GLA_DV))]
        out_shape = [jax.ShapeDtypeStruct((s, D_MODEL), F32), jax.ShapeDtypeStruct((s, D_MODEL), BF16)] + tail_shapes + [
            jax.ShapeDtypeStruct((1, GLA_DV), F32)]
    return _call(body, name=name, grid=(s // tm, n // tn), in_specs=in_specs, out_specs=out_specs, out_shape=out_shape,
                 args=args, exchange=exchange)


def matmul_tn(a, b, tm, tn, ts, out_dtype, name, exchange=None, b_first=None):
    s, m = a.shape
    n = b.shape[1] + (0 if b_first is None else tn)
    ts = min(ts, s)
    n_k = s // ts
    dims = (((0,), (0,)), ((), ()))

    def body(*refs):
        if b_first is None:
            a_ref, b_ref, o_ref, acc = refs
        else:
            a_ref, first_ref, b_ref, o_ref, acc = refs
        j, k = pl.program_id(1), pl.program_id(2)

        @pl.when(k == 0)
        def _():
            acc[...] = jnp.zeros_like(acc)

        if b_first is None:
            acc[...] += lax.dot_general(a_ref[...], b_ref[...], dims, preferred_element_type=F32)
        else:
            @pl.when(j == 0)
            def _():
                acc[...] += lax.dot_general(a_ref[...], first_ref[...], dims, preferred_element_type=F32)

            @pl.when(j > 0)
            def _():
                acc[...] += lax.dot_general(a_ref[...], b_ref[...], dims, preferred_element_type=F32)

        @pl.when(k == n_k - 1)
        def _():
            o_ref[...] = acc[...].astype(out_dtype)

    if b_first is None:
        b_specs, b_args = [pl.BlockSpec((ts, tn), lambda i, j, k: (k, j))], [b]
    else:
        b_specs = [pl.BlockSpec((ts, tn), lambda i, j, k: (k, 0)), pl.BlockSpec((ts, tn), lambda i, j, k: (k, jnp.maximum(j - 1, 0)))]
        b_args = [b_first, b]
    return _call(
        body, name=name, grid=(m // tm, n // tn, n_k),
        in_specs=[pl.BlockSpec((ts, tm), lambda i, j, k: (k, i))] + b_specs,
        out_specs=pl.BlockSpec((tm, tn), lambda i, j, k: (i, j)),
        out_shape=jax.ShapeDtypeStruct((m, n), out_dtype),
        scratch_shapes=[pltpu.VMEM((tm, tn), F32)], args=[a] + b_args, exchange=exchange)


def matmul_nt_normbwd(dproj, w, x, g, dres, tm, tk, name, exchange=None, first=None):
    s, kt = dproj.shape
    kt += 0 if first is None else tk
    d = w.shape[0]
    tm = min(tm, s)
    n_k = kt // tk
    dims = (((1,), (1,)), ((), ()))

    def body(*refs):
        if first is None:
            a_ref, w_ref, x_ref, g_ref, r_ref, dx_ref, dg_ref, acc = refs
        else:
            first_ref, a_ref, w_ref, x_ref, g_ref, r_ref, dx_ref, dg_ref, acc = refs
        i, k = pl.program_id(0), pl.program_id(1)

        @pl.when(k == 0)
        def _():
            acc[...] = jnp.zeros_like(acc)

        if first is None:
            acc[...] += lax.dot_general(a_ref[...], w_ref[...], dims, preferred_element_type=F32)
        else:
            @pl.when(k == 0)
            def _():
                acc[...] += lax.dot_general(first_ref[...], w_ref[...], dims, preferred_element_type=F32)

            @pl.when(k > 0)
            def _():
                acc[...] += lax.dot_general(a_ref[...], w_ref[...], dims, preferred_element_type=F32)

        @pl.when(k == n_k - 1)
        def _():
            dx, dg = _rms_bwd(x_ref[...], g_ref[...], acc[...])
            dx_ref[...] = r_ref[...] + dx

            @pl.when(i == 0)
            def _():
                dg_ref[...] = jnp.zeros_like(dg_ref)

            dg_ref[...] += dg

    row = pl.BlockSpec((tm, d), lambda i, k: (i, 0))
    if first is None:
        a_specs, a_args = [pl.BlockSpec((tm, tk), lambda i, k: (i, k))], [dproj]
    else:
        a_specs = [pl.BlockSpec((tm, tk), lambda i, k: (i, 0)), pl.BlockSpec((tm, tk), lambda i, k: (i, jnp.maximum(k - 1, 0)))]
        a_args = [first, dproj]
    return _call(
        body, name=name, grid=(s // tm, n_k),
        in_specs=a_specs + [pl.BlockSpec((d, tk), lambda i, k: (0, k)), row, _full((1, d)), row],
        out_specs=[row, _full((1, d))],
        out_shape=[jax.ShapeDtypeStruct((s, d), F32), jax.ShapeDtypeStruct((1, d), F32)],
        scratch_shapes=[pltpu.VMEM((tm, d), F32)], args=a_args + [w, x, g, dres], exchange=exchange)


def _rg_conv(xa, before, after, cw, cb):
    return (cw[0:1, :] * _shift_rows(xa, before, after, -2) + cw[1:2, :] * _shift_rows(xa, before, after, -1)
            + cw[2:3, :] * xa + cw[3:4, :] * _shift_rows(xa, before, after, 1) + cb)


def _rg_gates(ua_h, gw_ref, gb_ref, c_h, direction, head):
    r = _sigmoid(_bdot(ua_h, gw_ref[2 * direction, head]) + gb_ref[2 * direction, head:head + 1, :])
    i = _sigmoid(_bdot(ua_h, gw_ref[2 * direction + 1, head]) + gb_ref[2 * direction + 1, head:head + 1, :])
    log_a = -c_h * r
    a = jnp.exp(log_a)
    beta_sq = -jnp.tanh(log_a) * (1.0 + a * a)
    inv_beta = lax.rsqrt(jnp.maximum(beta_sq, SMALLEST_NORMAL))
    return r, i, a, beta_sq * inv_beta, inv_beta


def even_gates_fwd(proj, conv_w, conv_b, gate_w, gate_b, lam, exchange=None):
    s = proj.shape[0]
    ts = min(2 * ROW_TILE, s)
    n_tiles = s // ts

    def body(xa_ref, xb_ref, xn_ref, cw_ref, cb_ref, gw_ref, gb_ref, lam_ref, o_ref, hf_ref, carry):
        @pl.when(pl.program_id(0) == 0)
        def _():
            carry[...] = jnp.zeros_like(carry)

        xa, before, after = _halo_load(xa_ref, xb_ref, xn_ref, n_tiles)
        ua = _rg_conv(xa, before, after, cw_ref[...], cb_ref[...])
        c = RG_C * _softplus(-lam_ref[...])
        for direction in range(2):
            for head in range(RG_HEADS):
                lanes = slice(head * RG_HEAD_DIM, (head + 1) * RG_HEAD_DIM)
                ua_h = ua[:, lanes]
                _, i, a, beta, _ = _rg_gates(ua_h, gw_ref, gb_ref, c[direction:direction + 1, lanes], direction, head)
                o_ref[2 * direction, :, lanes] = a
                o_ref[2 * direction + 1, :, lanes] = beta * (i * ua_h)
        _scan_tile(o_ref.at[0], o_ref.at[1], hf_ref, carry, False, False)

    return _call(
        body, name="even_gates_fwd", grid=(n_tiles,),
        in_specs=_halo_specs(ts, s, D_MODEL, 0) + [_full(conv_w.shape), _full(conv_b.shape), _full(gate_w.shape),
                                                   _full(gate_b.shape), _full(lam.shape)],
        out_specs=[pl.BlockSpec((4, ts, D_MODEL), lambda i: (0, i, 0)), pl.BlockSpec((ts, D_MODEL), lambda i: (i, 0))],
        out_shape=[jax.ShapeDtypeStruct((4, s, D_MODEL), F32), jax.ShapeDtypeStruct((s, D_MODEL), F32)],
        scratch_shapes=[pltpu.VMEM((SUBLANES, D_MODEL), F32)],
        args=[proj, proj, proj, conv_w, conv_b, gate_w, gate_b, lam], exchange=exchange)


def _scan_tile(a_ref, b_ref, h_ref, carry, reverse, b_times_a):
    ts, c = h_ref.shape
    n_blocks = ts // SUBLANES
    row = lax.broadcasted_iota(jnp.int32, (SUBLANES, c), 0)

    def block(j, h_in):
        r0 = pl.multiple_of((n_blocks - 1 - j if reverse else j) * SUBLANES, SUBLANES)
        a = a_ref[pl.ds(r0, SUBLANES), :]
        b = b_ref[pl.ds(r0, SUBLANES), :]
        if b_times_a:
            b = a * b
        for step in (1, 2, 4):
            shift = SUBLANES - step if reverse else step
            valid = row < SUBLANES - step if reverse else row >= step
            b = jnp.where(valid, a * pltpu.roll(b, shift, 0) + b, b)
            a = jnp.where(valid, a * pltpu.roll(a, shift, 0), a)
        h = a * h_in + b
        h_ref[pl.ds(r0, SUBLANES), :] = h
        return h[0:1, :] if reverse else h[SUBLANES - 1:SUBLANES, :]

    carry[0:1, :] = lax.fori_loop(0, n_blocks, block, carry[0:1, :])


def linear_scan(a_arr, a_idx, b_arr, b_idx, reverse, b_times_a, name, exchange=None):
    _, s, c = a_arr.shape
    ts = min(MM_TILE, s)
    n_tiles = s // ts

    def tile_of(i):
        return n_tiles - 1 - i if reverse else i

    def body(a_ref, b_ref, h_ref, carry):
        @pl.when(pl.program_id(0) == 0)
        def _():
            carry[...] = jnp.zeros_like(carry)

        _scan_tile(a_ref, b_ref, h_ref, carry, reverse, b_times_a)

    return _call(
        body, name=name, grid=(n_tiles,),
        in_specs=[pl.BlockSpec((None, ts, c), lambda i: (a_idx, tile_of(i), 0)),
                  pl.BlockSpec((None, ts, c), lambda i: (b_idx, tile_of(i), 0))],
        out_specs=pl.BlockSpec((ts, c), lambda i: (tile_of(i), 0)),
        out_shape=jax.ShapeDtypeStruct((s, c), F32),
        scratch_shapes=[pltpu.VMEM((SUBLANES, c), F32)], args=[a_arr, b_arr], exchange=exchange)


def _sc_conv(p, before, after, w):
    return w[0:1, :] * _shift_rows(p, before, after, -1) + w[1:2, :] * p + w[2:3, :] * _shift_rows(p, before, after, 1)


def even_mix_fwd(ab, hf, proj, sc_w, exchange=None):
    s = proj.shape[0]
    ts = min(2 * ROW_TILE, s)
    n_tiles = s // ts

    def tile(i):
        return n_tiles - 1 - i

    row = pl.BlockSpec((ts, D_MODEL), lambda i: (tile(i), 0))

    def col(c):
        return pl.BlockSpec((ts, D_MODEL), lambda i: (tile(i), c))

    def body(a_ref, b_ref, hf_ref, za_ref, xb_ref, xbb_ref, xbn_ref, gb_ref, gc_ref, gcb_ref, gcn_ref, zb_ref, w_ref,
             u_ref, hb_ref, carry):
        @pl.when(pl.program_id(0) == 0)
        def _():
            carry[...] = jnp.zeros_like(carry)

        _scan_tile(a_ref, b_ref, hb_ref, carry, True, False)
        xb, xb_before, xb_after = _halo_load(xb_ref, xbb_ref, xbn_ref, n_tiles, tile)
        gc, gc_before, gc_after = _halo_load(gc_ref, gcb_ref, gcn_ref, n_tiles, tile)
        silu_za, _ = _silu_and_grad(za_ref[...])
        silu_zb, _ = _silu_and_grad(zb_ref[...])
        u_ref[:, :D_MODEL] = ((hf_ref[...] + hb_ref[...]) * silu_za).astype(BF16)
        cv = _sc_conv(gc * xb, gc_before * xb_before, gc_after * xb_after, w_ref[...])
        u_ref[:, D_MODEL:] = (gb_ref[...] * cv * silu_zb).astype(BF16)

    return _call(
        body, name="even_mix_fwd", grid=(n_tiles,),
        in_specs=[pl.BlockSpec((None, ts, D_MODEL), lambda i: (2, tile(i), 0)), pl.BlockSpec((None, ts, D_MODEL), lambda i: (3, tile(i), 0)),
                  row, col(1)] + _halo_specs(ts, s, D_MODEL, 2, tile) + [col(3)] + _halo_specs(ts, s, D_MODEL, 4, tile)
        + [col(5), _full(sc_w.shape)],
        out_specs=[pl.BlockSpec((ts, 2 * D_MODEL), lambda i: (tile(i), 0)), row],
        out_shape=[jax.ShapeDtypeStruct((s, 2 * D_MODEL), BF16), jax.ShapeDtypeStruct((s, D_MODEL), F32)],
        scratch_shapes=[pltpu.VMEM((SUBLANES, D_MODEL), F32)],
        args=[ab, ab, hf, proj, proj, proj, proj, proj, proj, proj, proj, proj, sc_w], exchange=exchange)


def even_mix_bwd(du, hf, hb, proj, sc_w, ab, exchange=None):
    s = proj.shape[0]
    ts = min(ROW_TILE, s)
    n_tiles = s // ts
    row = pl.BlockSpec((ts, D_MODEL), lambda i: (i, 0))

    def body(dya_ref, dyb_ref, dybb_ref, dybn_ref, hf_ref, hb_ref, za_ref, xb_ref, xbb_ref, xbn_ref,
             gb_ref, gbb_ref, gbn_ref, gc_ref, gcb_ref, gcn_ref, zb_ref, zbb_ref, zbn_ref, w_ref, a_ref,
             dh_ref, dp_ref, dw_ref, adj_ref, carry):
        @pl.when(pl.program_id(0) == 0)
        def _():
            carry[...] = jnp.zeros_like(carry)

        dyb, dyb_before, dyb_after = _halo_load(dyb_ref, dybb_ref, dybn_ref, n_tiles)
        xb, xb_before, xb_after = _halo_load(xb_ref, xbb_ref, xbn_ref, n_tiles)
        gb, gb_before, gb_after = _halo_load(gb_ref, gbb_ref, gbn_ref, n_tiles)
        gc, gc_before, gc_after = _halo_load(gc_ref, gcb_ref, gcn_ref, n_tiles)
        zb, zb_before, zb_after = _halo_load(zb_ref, zbb_ref, zbn_ref, n_tiles)
        w = w_ref[...]
        dya, za = dya_ref[...], za_ref[...]
        silu_za, dsilu_za = _silu_and_grad(za)
        dh_ref[...] = dya * silu_za
        _scan_tile(a_ref, dh_ref, adj_ref, carry, False, True)
        dp_ref[:, 0:D_MODEL] = (dya * (hf_ref[...] + hb_ref[...]) * dsilu_za).astype(BF16)

        silu_zb, dsilu_zb = _silu_and_grad(zb)
        p, p_before, p_after = gc * xb, gc_before * xb_before, gc_after * xb_after
        cv = _sc_conv(p, p_before, p_after, w)
        dcv = dyb * gb * silu_zb
        dcv_before = dyb_before * gb_before * _silu_and_grad(zb_before)[0]
        dcv_after = dyb_after * gb_after * _silu_and_grad(zb_after)[0]
        dpp = (w[0:1, :] * _shift_rows(dcv, dcv_before, dcv_after, 1) + w[1:2, :] * dcv
               + w[2:3, :] * _shift_rows(dcv, dcv_before, dcv_after, -1))
        dp_ref[:, D_MODEL:2 * D_MODEL] = (dpp * gc).astype(BF16)
        dp_ref[:, 2 * D_MODEL:3 * D_MODEL] = (dyb * cv * silu_zb).astype(BF16)
        dp_ref[:, 3 * D_MODEL:4 * D_MODEL] = (dpp * xb).astype(BF16)
        dp_ref[:, 4 * D_MODEL:5 * D_MODEL] = (dyb * gb * cv * dsilu_zb).astype(BF16)

        @pl.when(pl.program_id(0) == 0)
        def _():
            dw_ref[...] = jnp.zeros_like(dw_ref)

        dw_ref[0:1, :] += jnp.sum(dcv * _shift_rows(p, p_before, p_after, -1), axis=0, keepdims=True)
        dw_ref[1:2, :] += jnp.sum(dcv * p, axis=0, keepdims=True)
        dw_ref[2:3, :] += jnp.sum(dcv * _shift_rows(p, p_before, p_after, 1), axis=0, keepdims=True)

    return _call(
        body, name="even_mix_bwd", grid=(n_tiles,),
        in_specs=[row] + _halo_specs(ts, s, D_MODEL, 1) + [row, row, pl.BlockSpec((ts, D_MODEL), lambda i: (i, 1))]
        + _halo_specs(ts, s, D_MODEL, 2) + _halo_specs(ts, s, D_MODEL, 3) + _halo_specs(ts, s, D_MODEL, 4)
        + _halo_specs(ts, s, D_MODEL, 5) + [_full(sc_w.shape), pl.BlockSpec((None, ts, D_MODEL), lambda i: (2, i, 0))],
        out_specs=[row, pl.BlockSpec((ts, 5 * D_MODEL), lambda i: (i, 0)), _full(sc_w.shape), row],
        out_shape=[jax.ShapeDtypeStruct((s, D_MODEL), F32), jax.ShapeDtypeStruct((s, 5 * D_MODEL), BF16),
                   jax.ShapeDtypeStruct(sc_w.shape, F32), jax.ShapeDtypeStruct((s, D_MODEL), F32)],
        scratch_shapes=[pltpu.VMEM((SUBLANES, D_MODEL), F32)],
        args=[du, du, du, du, hf, hb, proj, *([proj] * 12), sc_w, ab], exchange=exchange)


def even_gates_bwd(proj, adj_f, adj_b, hf, hb, dh, conv_w, conv_b, gate_w, gate_b, lam, exchange=None):
    s = proj.shape[0]
    ts = min(2 * ROW_TILE, s)
    n_tiles = s // ts
    row = pl.BlockSpec((ts, D_MODEL), lambda i: (i, 0))

    def body(xa_ref, xab_ref, xan_ref, af_ref, afb_ref, afn_ref, ab_ref, abb_ref, abn_ref,
             hf_ref, hfb_ref, hfn_ref, hb_ref, hbb_ref, hbn_ref, dh_ref,
             cw_ref, cb_ref, gw_ref, gb_ref, lam_ref, dua_ref, dgw_ref, dgb_ref, dlam_ref):
        @pl.when(pl.program_id(0) == 0)
        def _():
            dgw_ref[...] = jnp.zeros_like(dgw_ref)
            dgb_ref[...] = jnp.zeros_like(dgb_ref)
            dlam_ref[...] = jnp.zeros_like(dlam_ref)

        xa, before, after = _halo_load(xa_ref, xab_ref, xan_ref, n_tiles)
        ua = _rg_conv(xa, before, after, cw_ref[...], cb_ref[...])
        lam_v = lam_ref[...]
        c = RG_C * _softplus(-lam_v)
        dc_dlam = -RG_C * _sigmoid(-lam_v)
        dh = dh_ref[...]
        adj = (_halo_load(af_ref, afb_ref, afn_ref, n_tiles), _halo_load(ab_ref, abb_ref, abn_ref, n_tiles))
        hs = (_halo_load(hf_ref, hfb_ref, hfn_ref, n_tiles), _halo_load(hb_ref, hbb_ref, hbn_ref, n_tiles))
        dua = jnp.zeros_like(ua)
        for direction in range(2):
            step = 1 if direction == 0 else -1
            g = dh + _shift_rows(*adj[direction], step)
            da_all = g * _shift_rows(*hs[direction], -step)
            dua_parts = []
            for head in range(RG_HEADS):
                lanes = slice(head * RG_HEAD_DIM, (head + 1) * RG_HEAD_DIM)
                ua_h = ua[:, lanes]
                c_h = c[direction:direction + 1, lanes]
                r, i, a, beta, inv_beta = _rg_gates(ua_h, gw_ref, gb_ref, c_h, direction, head)
                db = g[:, lanes]
                d_i = db * beta * ua_h
                dbeta = db * (i * ua_h)
                dlog_a = (da_all[:, lanes] - dbeta * a * inv_beta) * a
                dpr = -c_h * dlog_a * r * (1.0 - r)
                dpi = d_i * i * (1.0 - i)
                dua_parts.append(db * beta * i + _bdot_nt(dpr, gw_ref[2 * direction, head])
                                 + _bdot_nt(dpi, gw_ref[2 * direction + 1, head]))
                dgw_ref[2 * direction, head] += _bdot_tn(ua_h, dpr)
                dgw_ref[2 * direction + 1, head] += _bdot_tn(ua_h, dpi)
                dgb_ref[2 * direction, head:head + 1, :] += jnp.sum(dpr, axis=0, keepdims=True)
                dgb_ref[2 * direction + 1, head:head + 1, :] += jnp.sum(dpi, axis=0, keepdims=True)
                dlam_ref[direction:direction + 1, lanes] += (
                    jnp.sum(-r * dlog_a, axis=0, keepdims=True) * dc_dlam[direction:direction + 1, lanes])
            dua = dua + jnp.concatenate(dua_parts, axis=1)
        dua_ref[...] = dua

    return _call(
        body, name="even_gates_bwd", grid=(n_tiles,),
        in_specs=_halo_specs(ts, s, D_MODEL, 0) * 5 + [row] + [_full(conv_w.shape), _full(conv_b.shape), _full(gate_w.shape),
                                                             _full(gate_b.shape), _full(lam.shape)],
        out_specs=[row, _full(gate_w.shape), _full(gate_b.shape), _full(lam.shape)],
        out_shape=[jax.ShapeDtypeStruct((s, D_MODEL), F32), jax.ShapeDtypeStruct(gate_w.shape, F32),
                   jax.ShapeDtypeStruct(gate_b.shape, F32), jax.ShapeDtypeStruct(lam.shape, F32)],
        args=[proj, proj, proj, adj_f, adj_f, adj_f, adj_b, adj_b, adj_b, hf, hf, hf, hb, hb, hb, dh, conv_w, conv_b, gate_w,
              gate_b, lam], exchange=exchange)


def rg_conv_bwd(dua, proj, conv_w, exchange=None):
    s = proj.shape[0]
    ts = min(2 * ROW_TILE, s)
    n_tiles = s // ts

    def body(du_ref, dub_ref, dun_ref, xa_ref, xab_ref, xan_ref, cw_ref, dp_ref, dw_ref, db_ref):
        @pl.when(pl.program_id(0) == 0)
        def _():
            dw_ref[...] = jnp.zeros_like(dw_ref)
            db_ref[...] = jnp.zeros_like(db_ref)

        dua, dua_before, dua_after = _halo_load(du_ref, dub_ref, dun_ref, n_tiles)
        xa, xa_before, xa_after = _halo_load(xa_ref, xab_ref, xan_ref, n_tiles)
        cw = cw_ref[...]
        dxa = (cw[0:1, :] * _shift_rows(dua, dua_before, dua_after, 2) + cw[1:2, :] * _shift_rows(dua, dua_before, dua_after, 1)
               + cw[2:3, :] * dua + cw[3:4, :] * _shift_rows(dua, dua_before, dua_after, -1))
        dp_ref[...] = dxa.astype(BF16)
        for tap, offset in enumerate((-2, -1, 0, 1)):
            shifted = xa if offset == 0 else _shift_rows(xa, xa_before, xa_after, offset)
            dw_ref[tap:tap + 1, :] += jnp.sum(dua * shifted, axis=0, keepdims=True)
        db_ref[...] += jnp.sum(dua, axis=0, keepdims=True)

    return _call(
        body, name="rg_conv_bwd", grid=(n_tiles,),
        in_specs=_halo_specs(ts, s, D_MODEL, 0) * 2 + [_full(conv_w.shape)],
        out_specs=[pl.BlockSpec((ts, D_MODEL), lambda i: (i, 0)), _full(conv_w.shape), _full((1, D_MODEL))],
        out_shape=[jax.ShapeDtypeStruct((s, D_MODEL), BF16), jax.ShapeDtypeStruct(conv_w.shape, F32),
                   jax.ShapeDtypeStruct((1, D_MODEL), F32)],
        args=[dua, dua, dua, proj, proj, proj, conv_w], exchange=exchange)


def _split3(x):
    x1 = x.astype(BF16)
    rest = x - x1.astype(F32)
    x2 = rest.astype(BF16)
    return x1, x2, (rest - x2.astype(F32)).astype(BF16)


def _chunk_sum_matrix(t, reverse, transpose):
    i = lax.broadcasted_iota(jnp.int32, (t, t), 0)
    j = lax.broadcasted_iota(jnp.int32, (t, t), 1)
    if transpose:
        i, j = j, i
    same = (i // GLA_CHUNK) == (j // GLA_CHUNK)
    return jnp.where(same & ((j >= i) if reverse else (j <= i)), 1.0, 0.0).astype(BF16)


def _exact_dot(m, x):
    return sum(jnp.dot(m, part, preferred_element_type=F32) for part in _split3(x))


def _chunk_mask(t, reverse):
    i = lax.broadcasted_iota(jnp.int32, (t, t), 0)
    j = lax.broadcasted_iota(jnp.int32, (t, t), 1)
    return ((i // GLA_CHUNK) == (j // GLA_CHUNK)) & ((j >= i) if reverse else (j <= i))


def _chunk_rows(c):
    return slice(c * GLA_CHUNK, (c + 1) * GLA_CHUNK)


def _gla_gate(lr, wg, bg):
    z = _bdot(lr, wg) + bg
    log_alpha = (jnp.minimum(z, 0.0) - jnp.log(1.0 + jnp.exp(-jnp.abs(z)))) * (1.0 / GLA_NORMALIZER)
    return z, log_alpha


def _gla_tile_terms(q, k, bcum, reverse):
    n_chunks = q.shape[0] // GLA_CHUNK
    totals = []
    for c in range(n_chunks):
        edge = c * GLA_CHUNK if reverse else (c + 1) * GLA_CHUNK - 1
        totals.append(bcum[edge:edge + 1, :])
    btot = jnp.concatenate([jnp.broadcast_to(total, (GLA_CHUNK, total.shape[1])) for total in totals], axis=0)
    e_pos, e_neg, e_st = jnp.exp(bcum), jnp.exp(-bcum), jnp.exp(btot - bcum)
    return q * (GLA_DK ** -0.5) * e_pos, k * e_neg, k * e_st, e_pos, e_neg, e_st, [jnp.exp(total) for total in totals]


def _gla_specs(t, n_tiles, reverse_order):
    def tile(i):
        return n_tiles - 1 - i if reverse_order else i

    return tile, [
        pl.BlockSpec((t, GLA_KEY), lambda i: (tile(i), 0)),
        pl.BlockSpec((t, GLA_KEY), lambda i: (tile(i), 1)),
        pl.BlockSpec((t, D_MODEL), lambda i: (tile(i), 1)),
        pl.BlockSpec((t, LANES), lambda i: (tile(i), (ODD_IN_PAD - LANES) // LANES)),
    ]


def gla_fwd(proj, wg, bg, reverse, o_other=None, gnorm=None):
    s = proj.shape[0]
    t = min(ROW_TILE, s)
    n_tiles = s // t
    n_chunks = t // GLA_CHUNK
    final = o_other is not None
    tile, specs = _gla_specs(t, n_tiles, reverse)

    def body(*refs):
        if final:
            q_ref, k_ref, v_ref, lr_ref, wg_ref, bg_ref, oo_ref, r_ref, gn_ref, osum_ref, u_ref, st_ref, state = refs
        else:
            q_ref, k_ref, v_ref, lr_ref, wg_ref, bg_ref, o_ref, st_ref, state = refs
            osum_ref = o_ref

        @pl.when(pl.program_id(0) == 0)
        def _():
            state[...] = jnp.zeros_like(state)

        _, log_alpha = _gla_gate(lr_ref[...], wg_ref[...], bg_ref[...])
        bcum = _exact_dot(_chunk_sum_matrix(t, reverse, False), log_alpha)
        q, k, v = q_ref[...], k_ref[...], v_ref[...]
        q_in, k_in, k_st, _, _, _, decays = _gla_tile_terms(q, k, bcum, reverse)
        mask = _chunk_mask(t, reverse)
        order = list(range(n_chunks))[::-1] if reverse else list(range(n_chunks))
        intra, increments = [], []
        for head in range(GLA_HEADS):
            kl = slice(head * GLA_DK, (head + 1) * GLA_DK)
            vl = slice(head * GLA_DV, (head + 1) * GLA_DV)
            scores = jnp.where(mask, _bdot_nt(q_in[:, kl], k_in[:, kl]), 0.0)
            intra.append(_bdot(scores, v[:, vl]))
            increments.append([_bdot_tn(v[_chunk_rows(c), vl], k_st[_chunk_rows(c), kl]) for c in range(n_chunks)])
        for head in range(GLA_HEADS):
            kl = slice(head * GLA_DK, (head + 1) * GLA_DK)
            vl = slice(head * GLA_DV, (head + 1) * GLA_DV)
            running = state[head]
            before = [None] * n_chunks
            for c in order:
                before[c] = running
                st_ref[c, head] = running
                running = running * decays[c][:, kl] + increments[head][c]
            state[head] = running
            inter = [_bdot_nt(q_in[_chunk_rows(c), kl], before[c]) for c in range(n_chunks)]
            osum_ref[:, vl] = intra[head] + jnp.concatenate(inter, axis=0)
        if final:
            osum = osum_ref[...] + oo_ref[...]
            osum_ref[...] = osum
            silu_r, _ = _silu_and_grad(r_ref[...])
            gn = gn_ref[...]
            for head in range(GLA_HEADS):
                vl = slice(head * GLA_DV, (head + 1) * GLA_DV)
                u_ref[:, vl] = (_rms(osum[:, vl], gn[:, vl]) * silu_r[:, vl]).astype(BF16)

    row = pl.BlockSpec((t, D_MODEL), lambda i: (tile(i), 0))
    st_spec = pl.BlockSpec((n_chunks, GLA_HEADS, GLA_DV, GLA_DK), lambda i: (tile(i), 0, 0, 0))
    st_shape = jax.ShapeDtypeStruct((s // GLA_CHUNK, GLA_HEADS, GLA_DV, GLA_DK), F32)
    in_specs = specs + [_full(wg.shape), _full(bg.shape)]
    args = [proj, proj, proj, proj, wg, bg]
    if final:
        in_specs += [row, pl.BlockSpec((t, D_MODEL), lambda i: (tile(i), 2)), _full(gnorm.shape)]
        args += [o_other, proj, gnorm]
        out_specs = [row, row, st_spec]
        out_shape = [jax.ShapeDtypeStruct((s, D_MODEL), F32), jax.ShapeDtypeStruct((s, D_MODEL), BF16), st_shape]
    else:
        out_specs = [row, st_spec]
        out_shape = [jax.ShapeDtypeStruct((s, D_MODEL), F32), st_shape]
    return pl.pallas_call(
        body, name="gla_fwd_rev" if reverse else "gla_fwd", grid=(n_tiles,), in_specs=in_specs, out_specs=out_specs,
        out_shape=out_shape, scratch_shapes=[pltpu.VMEM((GLA_HEADS, GLA_DV, GLA_DK), F32)], compiler_params=_params(1),
    )(*args)


def gla_bwd(proj, wg, bg, do, states, reverse, first=None):
    s = proj.shape[0]
    t = min(ROW_TILE, s)
    n_tiles = s // t
    n_chunks = t // GLA_CHUNK
    final = first is not None
    tile, specs = _gla_specs(t, n_tiles, not reverse)

    def body(*refs):
        if final:
            (q_ref, k_ref, v_ref, lr_ref, wg_ref, bg_ref, do_ref, st_ref, dqkv1_ref, dlr1_ref, dr_ref,
             dp_ref, dwg_ref, dbg_ref, dstate, dqkv, dbc, dbt) = refs
        else:
            (q_ref, k_ref, v_ref, lr_ref, wg_ref, bg_ref, do_ref, st_ref,
             dqkv, dlr_ref, dwg_ref, dbg_ref, dstate, dbc, dbt) = refs

        @pl.when(pl.program_id(0) == 0)
        def _():
            dstate[...] = jnp.zeros_like(dstate)
            dwg_ref[...] = jnp.zeros_like(dwg_ref)
            dbg_ref[...] = jnp.zeros_like(dbg_ref)

        lr, wg_v = lr_ref[...], wg_ref[...]
        z, log_alpha = _gla_gate(lr, wg_v, bg_ref[...])
        bcum = _exact_dot(_chunk_sum_matrix(t, reverse, False), log_alpha)
        q, k, v, do_v = q_ref[...], k_ref[...], v_ref[...], do_ref[...]
        q_in, k_in, k_st, e_pos, e_neg, e_st, decays = _gla_tile_terms(q, k, bcum, reverse)
        mask = _chunk_mask(t, reverse)
        order = list(range(n_chunks)) if reverse else list(range(n_chunks))[::-1]
        dq_intra, dk_intra, dv_intra, increments = [], [], [], []
        for head in range(GLA_HEADS):
            kl = slice(head * GLA_DK, (head + 1) * GLA_DK)
            vl = slice(head * GLA_DV, (head + 1) * GLA_DV)
            scores = jnp.where(mask, _bdot_nt(q_in[:, kl], k_in[:, kl]), 0.0)
            dscores = jnp.where(mask, _bdot_nt(do_v[:, vl], v[:, vl]), 0.0)
            dv_intra.append(_bdot_tn(scores, do_v[:, vl]))
            dq_intra.append(_bdot(dscores, k_in[:, kl]))
            dk_intra.append(_bdot_tn(dscores, q_in[:, kl]))
            increments.append([_bdot_tn(do_v[_chunk_rows(c), vl], q_in[_chunk_rows(c), kl]) for c in range(n_chunks)])
        for head in range(GLA_HEADS):
            kl = slice(head * GLA_DK, (head + 1) * GLA_DK)
            vl = slice(head * GLA_DV, (head + 1) * GLA_DV)
            running = dstate[head]
            after, ddecay = [None] * n_chunks, [None] * n_chunks
            for c in order:
                after[c] = running
                ddecay[c] = jnp.sum(running * st_ref[c, head], axis=0, keepdims=True)
                running = running * decays[c][:, kl] + increments[head][c]
            dstate[head] = running
            dq_inter = jnp.concatenate([_bdot(do_v[_chunk_rows(c), vl], st_ref[c, head]) for c in range(n_chunks)], axis=0)
            dv_inter = jnp.concatenate([_bdot_nt(k_st[_chunk_rows(c), kl], after[c]) for c in range(n_chunks)], axis=0)
            dk_st = jnp.concatenate([_bdot(v[_chunk_rows(c), vl], after[c]) for c in range(n_chunks)], axis=0)
            dq_in = dq_intra[head] + dq_inter
            ks_h = k_st[:, kl]
            dqkv[:, 2 * GLA_KEY + head * GLA_DV:2 * GLA_KEY + (head + 1) * GLA_DV] = dv_intra[head] + dv_inter
            dqkv[:, kl] = dq_in * (GLA_DK ** -0.5) * e_pos[:, kl]
            dqkv[:, GLA_KEY + head * GLA_DK:GLA_KEY + (head + 1) * GLA_DK] = dk_intra[head] * e_neg[:, kl] + dk_st * e_st[:, kl]
            dbc[:, kl] = dq_in * q_in[:, kl] - dk_intra[head] * k_in[:, kl] - dk_st * ks_h
            weighted = dk_st * ks_h
            for c in range(n_chunks):
                dbtot = jnp.sum(weighted[_chunk_rows(c)], axis=0, keepdims=True) + ddecay[c] * decays[c][:, kl]
                dbt[_chunk_rows(c), kl] = jnp.broadcast_to(dbtot, (GLA_CHUNK, GLA_DK))
        dlog_alpha = _exact_dot(_chunk_sum_matrix(t, reverse, True), dbc[...]) + dbt[...]
        dz = dlog_alpha * _sigmoid(-z) * (1.0 / GLA_NORMALIZER)
        dlr = _bdot_nt(dz, wg_v)
        dwg_ref[...] += _bdot_tn(lr, dz)
        dbg_ref[...] += jnp.sum(dz, axis=0, keepdims=True)
        if final:
            dp_ref[:, :2 * D_MODEL] = (dqkv[...] + dqkv1_ref[...]).astype(BF16)
            dp_ref[:, 2 * D_MODEL:3 * D_MODEL] = dr_ref[...]
            dp_ref[:, 3 * D_MODEL:] = (dlr + dlr1_ref[...]).astype(BF16)
        else:
            dlr_ref[...] = dlr

    row = pl.BlockSpec((t, D_MODEL), lambda i: (tile(i), 0))
    wide = pl.BlockSpec((t, 2 * D_MODEL), lambda i: (tile(i), 0))
    narrow = pl.BlockSpec((t, LANES), lambda i: (tile(i), 0))
    st_spec = pl.BlockSpec((n_chunks, GLA_HEADS, GLA_DV, GLA_DK), lambda i: (tile(i), 0, 0, 0))
    in_specs = specs + [_full(wg.shape), _full(bg.shape), row, st_spec]
    args = [proj, proj, proj, proj, wg, bg, do, states]
    acc_specs = [_full(wg.shape), _full(bg.shape)]
    acc_shapes = [jax.ShapeDtypeStruct(wg.shape, F32), jax.ShapeDtypeStruct(bg.shape, F32)]
    scratch = [pltpu.VMEM((GLA_HEADS, GLA_DV, GLA_DK), F32)]
    work = [pltpu.VMEM((t, GLA_KEY), F32), pltpu.VMEM((t, GLA_KEY), F32)]
    if final:
        in_specs += [wide, narrow, row]
        args += list(first)
        out_specs = [pl.BlockSpec((t, ODD_IN_PAD), lambda i: (tile(i), 0))] + acc_specs
        out_shape = [jax.ShapeDtypeStruct((s, ODD_IN_PAD), BF16)] + acc_shapes
        scratch += [pltpu.VMEM((t, 2 * D_MODEL), F32)] + work
    else:
        out_specs = [wide, narrow] + acc_specs
        out_shape = [jax.ShapeDtypeStruct((s, 2 * D_MODEL), F32), jax.ShapeDtypeStruct((s, LANES), F32)] + acc_shapes
        scratch += work
    return pl.pallas_call(
        body, name="gla_bwd_rev" if reverse else "gla_bwd", grid=(n_tiles,), in_specs=in_specs, out_specs=out_specs,
        out_shape=out_shape, scratch_shapes=scratch, compiler_params=_params(1),
    )(*args)


def pair_sum(grad, from_sibling):
    n_chips, r, w = from_sibling.shape

    def body(even_ref, odd_ref, sib_ref, o_ref):
        mine = jnp.where(lax.axis_index("c") == 1, odd_ref[...], even_ref[...])
        o_ref[...] = (mine.astype(F32) + sib_ref[...].astype(F32)).astype(o_ref.dtype)

    return pl.pallas_call(
        body, name="pair_sum", grid=(n_chips,),
        in_specs=[pl.BlockSpec((r, w), lambda k: (0, 2 * k)), pl.BlockSpec((r, w), lambda k: (0, 2 * k + 1)),
                  pl.BlockSpec((None, r, w), lambda k: (k, 0, 0))],
        out_specs=pl.BlockSpec((None, r, w), lambda k: (k, 0, 0)),
        out_shape=jax.ShapeDtypeStruct(from_sibling.shape, from_sibling.dtype), compiler_params=_params(1),
    )(grad, grad, from_sibling)


def _adamw_update(g, w, m, v):
    new_m = ADAM_B1 * m + (1.0 - ADAM_B1) * g
    new_v = ADAM_B2 * v + (1.0 - ADAM_B2) * (g * g)
    m_hat = new_m / (1.0 - ADAM_B1 ** ADAM_STEP)
    v_hat = new_v / (1.0 - ADAM_B2 ** ADAM_STEP)
    return -ADAM_LR * (m_hat / (jnp.sqrt(v_hat) + ADAM_EPS) + ADAM_WD * w), new_m, new_v


def sum_parts(parts, name):
    _, r, c = parts.shape

    def body(p_ref, o_ref):
        total = p_ref[0].astype(F32)
        for j in range(1, N_DEV):
            total = total + p_ref[j].astype(F32)
        o_ref[...] = total

    return pl.pallas_call(body, name=name, in_specs=[_full(parts.shape)], out_specs=_full((r, c)), grid=(1,),
                          out_shape=jax.ShapeDtypeStruct((r, c), F32), compiler_params=_params(1))(parts)


def adamw(parts, w, m, v, name, exchange=None):
    n, r, c = parts.shape
    tr = r
    while tr * c * 4 > ADAMW_BLOCK_BYTES and tr % (2 * SUBLANES) == 0:
        tr //= 2

    def body(p_ref, w_ref, m_ref, v_ref, g_ref, d_ref, nm_ref, nv_ref):
        g = p_ref[0].astype(F32)
        for j in range(1, n):
            g = g + p_ref[j].astype(F32)
        g_ref[...] = g
        d_ref[...], nm_ref[...], nv_ref[...] = _adamw_update(g, w_ref[...], m_ref[...], v_ref[...])

    row = pl.BlockSpec((tr, c), lambda i: (i, 0))
    return _call(
        body, name=name, grid=(r // tr,),
        in_specs=[pl.BlockSpec((n, tr, c), lambda i: (0, i, 0)), row, row, row], out_specs=[row] * 4,
        out_shape=[jax.ShapeDtypeStruct((r, c), F32)] * 4, args=[parts, w, m, v], exchange=exchange)


def _small_views(shape):
    if len(shape) == 2:
        return [((slice(None), slice(None)), (slice(None), slice(None)))]
    if len(shape) == 3:
        return [((slice(None), slice(None)), (0,))]
    rows = shape[2]
    return [((slice(k * rows, (k + 1) * rows), slice(None)), (0, k)) for k in range(shape[1])]


def adamw_small(landings, w, m, v):
    names = list(landings)
    n = len(names)
    shapes = [w[name].shape for name in names]

    def body(*refs):
        land, ws, ms, vs = refs[:n], refs[n:2 * n], refs[2 * n:3 * n], refs[3 * n:4 * n]
        outs = [refs[(4 + k) * n:(5 + k) * n] for k in range(4)]
        for k in range(n):
            total = land[k][0]
            for j in range(1, N_DEV):
                total = total + land[k][j]
            for rows, at in _small_views(shapes[k]):
                g = total[rows]
                outs[0][k][at] = g
                outs[1][k][at], outs[2][k][at], outs[3][k][at] = _adamw_update(g, ws[k][at], ms[k][at], vs[k][at])

    blocks = [_full(sh) for sh in shapes]
    outs = pl.pallas_call(
        body, name="adamw_small", grid=(1,),
        in_specs=[_full(landings[name].shape) for name in names] + blocks * 3, out_specs=blocks * 4,
        out_shape=[jax.ShapeDtypeStruct(sh, F32) for sh in shapes] * 4, compiler_params=_params(1),
    )(*[landings[name] for name in names], *[src[name] for src in (w, m, v) for name in names])
    return [dict(zip(names, outs[k * n:(k + 1) * n])) for k in range(4)]


def adamw_replicated(land_vec, land_gate_b, land_loss, names, w, m, v, gate_b):
    n = len(names)

    def body(*refs):
        vec_ref, gb_ref, loss_ref = refs[:3]
        ws, ms, vs = refs[3:3 + n], refs[3 + n:3 + 2 * n], refs[3 + 2 * n:3 + 3 * n]
        gw_ref, gm_ref, gv_ref = refs[3 + 3 * n:6 + 3 * n]
        outs = refs[6 + 3 * n:]
        vec, gb, loss = vec_ref[0], gb_ref[0], loss_ref[0]
        for j in range(1, N_DEV):
            vec, gb, loss = vec + vec_ref[j], gb + gb_ref[j], loss + loss_ref[j]
        for k in range(n):
            g = vec[k:k + 1, :]
            outs[k][...] = g
            outs[n + k][...], outs[2 * n + k][...], outs[3 * n + k][...] = _adamw_update(g, ws[k][...], ms[k][...], vs[k][...])
        outs[4 * n][...] = gb
        outs[4 * n + 1][...], outs[4 * n + 2][...], outs[4 * n + 3][...] = _adamw_update(gb, gw_ref[...], gm_ref[...], gv_ref[...])
        outs[4 * n + 4][...] = loss

    vec_block, gb_block = _full((1, D_MODEL)), _full(gate_b[0].shape)
    outs = pl.pallas_call(
        body, name="adamw_replicated", grid=(1,),
        in_specs=[_full(land_vec.shape), _full(land_gate_b.shape), _full(land_loss.shape)] + [vec_block] * (3 * n) + [gb_block] * 3,
        out_specs=[vec_block] * (4 * n) + [gb_block] * 4 + [_full(land_loss.shape[1:])],
        out_shape=[jax.ShapeDtypeStruct((1, D_MODEL), F32)] * (4 * n) + [jax.ShapeDtypeStruct(gate_b[0].shape, F32)] * 4
        + [jax.ShapeDtypeStruct(land_loss.shape[1:], F32)],
        compiler_params=_params(1),
    )(land_vec, land_gate_b, land_loss, *[src[name] for src in (w, m, v) for name in names], *gate_b)
    results = {name: [outs[k * n + i] for k in range(4)] for i, name in enumerate(names)}
    return results, outs[4 * n:4 * n + 4], outs[4 * n + 4]


SMALL_SHARDED = ("rg_conv_w", "rg_lambda", "sc_conv_w", "odd_norm_pre", "odd_norm_post", "gla_b_gate", "gla_norm_g", "gla_w_gate_lr")
SMALL_ROWS = {"rg_conv_w": (0, 4), "rg_lambda": (4, 2), "sc_conv_w": (6, 3), "odd_norm_pre": (9, 1), "odd_norm_post": (10, 1),
              "gla_b_gate": (11, 2), "gla_norm_g": (13, 1), "gla_w_gate_lr": (16, 32)}


def _pack_small(shards):
    pieces, at = [], 0
    for name in SMALL_SHARDED:
        start, rows = SMALL_ROWS[name]
        if start > at:
            pieces.append(jnp.zeros((start - at, LANES), F32))
        a = shards[name].reshape(rows, -1)
        pieces.append(jnp.pad(a, ((0, 0), (0, LANES - a.shape[1]))))
        at = start + rows
    return jnp.concatenate(pieces, axis=0)


def _unpack_gathered(g):
    def cols(name, width):
        start, rows = SMALL_ROWS[name]
        return jnp.transpose(g[:, start:start + rows, :width], (1, 0, 2)).reshape(rows, N_DEV * width)

    w_lr = cols("gla_w_gate_lr", GLA_KEY // N_DEV).reshape(2, GLA_RANK, GLA_KEY)
    return dict(rg_conv_w=cols("rg_conv_w", LANES), rg_lambda=cols("rg_lambda", LANES), sc_conv_w=cols("sc_conv_w", LANES),
                odd_norm_pre=cols("odd_norm_pre", LANES), odd_norm_post=cols("odd_norm_post", LANES),
                gla_b_gate=cols("gla_b_gate", GLA_KEY // N_DEV), gla_norm_g=cols("gla_norm_g", GLA_DV // N_DEV), gla_w_gate_lr=w_lr)


def _blocks_along_columns(a, rows):
    return jnp.transpose(a.reshape(rows, N_DEV, -1), (1, 0, 2))


def kernel(x, even_norm_pre, even_norm_post, even_w_in, rg_conv_w, rg_conv_b, rg_gate_w, rg_gate_b, rg_lambda, sc_conv_w, even_w_out, odd_norm_pre, odd_norm_post, odd_w_in, gla_w_gate_lr, gla_b_gate, gla_norm_g, odd_w_out, loss_target, m_even_norm_pre, m_even_norm_post, m_even_w_in, m_rg_conv_w, m_rg_conv_b, m_rg_gate_w, m_rg_gate_b, m_rg_lambda, m_sc_conv_w, m_even_w_out, m_odd_norm_pre, m_odd_norm_post, m_odd_w_in, m_gla_w_gate_lr, m_gla_b_gate, m_gla_norm_g, m_odd_w_out, v_even_norm_pre, v_even_norm_post, v_even_w_in, v_rg_conv_w, v_rg_conv_b, v_rg_gate_w, v_rg_gate_b, v_rg_lambda, v_sc_conv_w, v_even_w_out, v_odd_norm_pre, v_odd_norm_post, v_odd_w_in, v_gla_w_gate_lr, v_gla_b_gate, v_gla_norm_g, v_odd_w_out):
    weights = dict(even_norm_pre=even_norm_pre, even_norm_post=even_norm_post, even_w_in=even_w_in, rg_conv_w=rg_conv_w,
                   rg_conv_b=rg_conv_b, rg_gate_w=rg_gate_w, rg_gate_b=rg_gate_b, rg_lambda=rg_lambda, sc_conv_w=sc_conv_w,
                   even_w_out=even_w_out, odd_norm_pre=odd_norm_pre, odd_norm_post=odd_norm_post, odd_w_in=odd_w_in,
                   gla_w_gate_lr=gla_w_gate_lr, gla_b_gate=gla_b_gate, gla_norm_g=gla_norm_g, odd_w_out=odd_w_out)
    m_in = dict(even_norm_pre=m_even_norm_pre, even_norm_post=m_even_norm_post, even_w_in=m_even_w_in, rg_conv_w=m_rg_conv_w,
                rg_conv_b=m_rg_conv_b, rg_gate_w=m_rg_gate_w, rg_gate_b=m_rg_gate_b, rg_lambda=m_rg_lambda, sc_conv_w=m_sc_conv_w,
                even_w_out=m_even_w_out, odd_norm_pre=m_odd_norm_pre, odd_norm_post=m_odd_norm_post, odd_w_in=m_odd_w_in,
                gla_w_gate_lr=m_gla_w_gate_lr, gla_b_gate=m_gla_b_gate, gla_norm_g=m_gla_norm_g, odd_w_out=m_odd_w_out)
    v_in = dict(even_norm_pre=v_even_norm_pre, even_norm_post=v_even_norm_post, even_w_in=v_even_w_in, rg_conv_w=v_rg_conv_w,
                rg_conv_b=v_rg_conv_b, rg_gate_w=v_rg_gate_w, rg_gate_b=v_rg_gate_b, rg_lambda=v_rg_lambda, sc_conv_w=v_sc_conv_w,
                even_w_out=v_even_w_out, odd_norm_pre=v_odd_norm_pre, odd_norm_post=v_odd_norm_post, odd_w_in=v_odd_w_in,
                gla_w_gate_lr=v_gla_w_gate_lr, gla_b_gate=v_gla_b_gate, gla_norm_g=v_gla_norm_g, odd_w_out=v_odd_w_out)
    names = list(weights)
    shapes = {n: weights[n].shape for n in names}
    xs = x[0]
    tgt = loss_target[0]

    proj_e, h_e, w_in_e, small_all = gather_matmul(xs, even_norm_pre, even_w_in[0].astype(BF16),
                                                   _pack_small({n: weights[n][0] for n in SMALL_SHARDED}), 2 * MM_TILE)
    small = _unpack_gathered(small_all)
    gate_w = rg_gate_w[0].reshape(4, RG_HEADS, RG_HEAD_DIM, RG_HEAD_DIM).astype(BF16)
    gate_b = rg_gate_b[0].reshape(4, RG_HEADS, RG_HEAD_DIM)
    conv_b = rg_conv_b
    wg_pad = [jnp.pad(small["gla_w_gate_lr"][d], ((GLA_RANK * d, LANES - GLA_RANK * (d + 1)), (0, 0))).astype(BF16) for d in range(2)]
    bg = [small["gla_b_gate"][d:d + 1] for d in range(2)]
    gnorm = jnp.tile(small["gla_norm_g"], (1, GLA_HEADS))

    half = D_MODEL // 2
    behind_gates = Exchange()
    behind_gates.gather(even_w_out[0].astype(BF16), via_sibling=True)
    behind_gates.gather(odd_w_in[0, :half].astype(BF16), via_sibling=True)
    (ab, hf), (w_out_e, w_in_o_top) = even_gates_fwd(proj_e, small["rg_conv_w"], conv_b, gate_w, gate_b, small["rg_lambda"],
                                                     exchange=behind_gates)
    w_out_e = w_out_e.reshape(2 * D_MODEL, D_MODEL)
    behind_mix_fwd = Exchange()
    behind_mix_fwd.gather(odd_w_in[0, half:].astype(BF16), via_sibling=True)
    behind_mix_fwd.gather(odd_w_out[0].astype(BF16), via_sibling=True)
    (u_e, hb), (w_in_o_bottom, w_out_o) = even_mix_fwd(ab, hf, proj_e, small["sc_conv_w"], exchange=behind_mix_fwd)
    w_out_o = w_out_o.reshape(D_MODEL, D_MODEL)
    w_in_o = jnp.concatenate([jnp.transpose(part, (1, 0, 2)).reshape(half, ODD_IN) for part in (w_in_o_top, w_in_o_bottom)], axis=0)
    w_in_o = jnp.pad(w_in_o, ((0, 0), (0, ODD_IN_PAD - ODD_IN)))
    y_e, x1 = matmul_post(u_e, w_out_e, xs, even_norm_post, "even_out")

    proj_o, h_o = rms_matmul(x1, small["odd_norm_pre"], w_in_o, MM_TILE, ODD_IN_PAD, "odd_in")
    o_f, st_f = gla_fwd(proj_o, wg_pad[0], bg[0], False)
    osum, u_o, st_b = gla_fwd(proj_o, wg_pad[1], bg[1], True, o_other=o_f, gnorm=gnorm)
    y_o, dout, loss_part = matmul_post(u_o, w_out_o, x1, small["odd_norm_post"], "odd_out", target=tgt)

    do, dr, dy_o, d_odd_norm_post, d_gnorm = normbwd_matmul_nt(y_o, small["odd_norm_post"], dout, w_out_o, D_MODEL, "odd_out_bwd",
                                                               gla=(proj_o, osum, gnorm))
    d_w_out_o = matmul_tn(u_o, dy_o, D_MODEL, D_MODEL, 4 * MM_TILE, BF16, "odd_w_out_grad")
    dqkv_f, dlr_f, dwg_f, dbg_f = gla_bwd(proj_o, wg_pad[0], bg[0], do, st_f, False)
    dproj_o, dwg_b, dbg_b = gla_bwd(proj_o, wg_pad[1], bg[1], do, st_b, True, first=(dqkv_f, dlr_f, dr))
    dx1, d_odd_norm_pre = matmul_nt_normbwd(dproj_o, w_in_o, x1, small["odd_norm_pre"], dout, MM_TILE, ODD_IN_PAD, "odd_in_bwd")
    d_w_in_o = matmul_tn(h_o, dproj_o, D_MODEL, ODD_IN_PAD // 5, 8 * MM_TILE, BF16, "odd_w_in_grad")

    landed = {}
    behind_out = Exchange()
    behind_out.scatter(d_w_out_o.reshape(N_DEV, D_MODEL // N_DEV, D_MODEL))
    behind_out.scatter(d_odd_norm_pre, columns=True)
    behind_out.scatter(d_odd_norm_post, columns=True)
    behind_out.scatter(_blocks_along_columns(jnp.concatenate([dbg_f, dbg_b], axis=0), 2))
    behind_out.scatter(_blocks_along_columns(d_gnorm, 1))
    behind_out.scatter(_blocks_along_columns(jnp.concatenate([dwg_f[:GLA_RANK], dwg_b[GLA_RANK:2 * GLA_RANK]], axis=0), 2 * GLA_RANK))
    (du_e, dy_e, d_even_norm_post), got = normbwd_matmul_nt(y_e, even_norm_post, dx1, w_out_e, 2 * D_MODEL, "even_out_bwd",
                                                           exchange=behind_out)
    p_w_out_o = got[0]
    for n, part in zip(("odd_norm_pre", "odd_norm_post", "gla_b_gate", "gla_norm_g", "gla_w_gate_lr"), got[1:]):
        landed[n] = part
    d_w_out_e = matmul_tn(u_e, dy_e, D_MODEL, D_MODEL, 4 * MM_TILE, BF16, "even_w_out_grad")
    behind_mix = Exchange()
    behind_mix.scatter(d_w_out_e.reshape(N_DEV, 2 * D_MODEL // N_DEV, D_MODEL))
    (dh, drest, d_sc_w, adj_b), (p_w_out_e,) = even_mix_bwd(du_e, hf, hb, proj_e, small["sc_conv_w"], ab, exchange=behind_mix)
    adj_f = linear_scan(ab, 0, dh.reshape(1, *dh.shape), 0, True, True, "scan_fwd_adjoint")
    behind_gates_bwd = Exchange()
    behind_gates_bwd.scatter(jnp.transpose(d_w_in_o[:, :ODD_IN].reshape(D_MODEL, N_DEV, ODD_SHARD), (1, 0, 2)))
    behind_gates_bwd.scatter(d_sc_w, columns=True)
    (dua, d_gate_w, d_gate_b, d_lam), (p_w_in_o, landed["sc_conv_w"]) = even_gates_bwd(
        proj_e, adj_f, adj_b, hf, hb, dh, small["rg_conv_w"], conv_b, gate_w, gate_b, small["rg_lambda"], exchange=behind_gates_bwd)
    gate_w_rows = 4 * RG_HEADS * RG_HEAD_DIM
    behind_conv = Exchange()
    behind_conv.scatter(d_gate_w.reshape(N_DEV, gate_w_rows // N_DEV, RG_HEAD_DIM))
    behind_conv.scatter(d_lam, columns=True)
    (dxa, d_conv_w, d_conv_b), (p_gate_w, landed["rg_lambda"]) = rg_conv_bwd(dua, proj_e, small["rg_conv_w"], exchange=behind_conv)
    behind_w_grad = Exchange()
    behind_w_grad.gather(sum_parts(p_gate_w, "sum_gate_w"))
    d_w_in_e, (g_gate_w_all,) = matmul_tn(h_e, drest, D_MODEL, D_MODEL, 4 * MM_TILE, BF16, "even_w_in_grad",
                                          exchange=behind_w_grad, b_first=dxa)
    to_sibling = Exchange()
    to_sibling.to_sibling(d_w_in_e)
    to_sibling.scatter(d_conv_w, columns=True)
    from_sibling, landed["rg_conv_w"] = run_exchange(to_sibling, "scatter_to_sibling")
    behind_in_bwd = Exchange()
    behind_in_bwd.among_chips(pair_sum(d_w_in_e, from_sibling))
    (grad_x, d_even_norm_pre), (p_w_in_e,) = matmul_nt_normbwd(
        drest, w_in_e, xs, even_norm_pre, dx1, 2 * MM_TILE, D_MODEL, "even_in_bwd", exchange=behind_in_bwd, first=dxa)
    last = Exchange()
    replicated_vecs = ("even_norm_pre", "even_norm_post", "rg_conv_b")
    last.gather(jnp.concatenate([d_even_norm_pre, d_even_norm_post, d_conv_b], axis=0))
    last.gather(d_gate_b.reshape(4 * RG_HEADS, RG_HEAD_DIM))
    last.gather(loss_part)

    results = {}

    def update(name, parts_, shape2d, exchange=None):
        outs = adamw(parts_, weights[name][0].reshape(shape2d), m_in[name][0].reshape(shape2d), v_in[name][0].reshape(shape2d),
                     "adamw_" + name, exchange=exchange)
        if exchange is not None:
            outs, gathered = outs
        results[name] = [o.reshape(shapes[name]) for o in outs]
        return gathered if exchange is not None else None

    land_vec, land_gate_b, land_loss = update("even_w_in", p_w_in_e, (D_MODEL, EVEN_SHARD), exchange=last)
    update("even_w_out", p_w_out_e, (2 * D_MODEL // N_DEV, D_MODEL))
    update("odd_w_in", p_w_in_o, (D_MODEL, ODD_SHARD))
    update("odd_w_out", p_w_out_o, (D_MODEL // N_DEV, D_MODEL))
    update("rg_gate_w", g_gate_w_all.reshape(1, gate_w_rows, RG_HEAD_DIM), (gate_w_rows, RG_HEAD_DIM))
    small_out = adamw_small({n: landed[n] for n in SMALL_SHARDED}, weights, m_in, v_in)
    for n in SMALL_SHARDED:
        results[n] = [o[n] for o in small_out]
    gate_b_shape = (4 * RG_HEADS, RG_HEAD_DIM)
    rep_out, gate_b_out, loss_all = adamw_replicated(land_vec, land_gate_b, land_loss, replicated_vecs, weights, m_in, v_in,
                                                     [src["rg_gate_b"].reshape(gate_b_shape) for src in (weights, m_in, v_in)])
    results.update(rep_out)
    results["rg_gate_b"] = [o.reshape(shapes["rg_gate_b"]) for o in gate_b_out]

    return (loss_all[0, 0], grad_x.reshape(x.shape), *[results[n][0] for n in names], *[results[n][1] for n in names],
            *[results[n][2] for n in names], *[results[n][3] for n in names])
```

```python
import functools

import jax
import jax.numpy as jnp
from jax import lax
from jax.experimental import pallas as pl
from jax.experimental.pallas import tpu as pltpu

F32 = jnp.float32
BF16 = jnp.bfloat16

N_DEV = 8
D_MODEL = 1024
NORM_EPS = 1e-6
RG_HEADS = 8
RG_HEAD_DIM = 128
RG_C = 8.0
GLA_HEADS = 4
GLA_DK = 128
GLA_DV = 256
GLA_KEY = 512
GLA_RANK = 16
GLA_NORMALIZER = 16.0
GLA_CHUNK = 64
EVEN_IN = 6144
ODD_IN = 3104
ODD_IN_PAD = 3200
ODD_SHARD = ODD_IN // N_DEV
EVEN_SHARD = EVEN_IN // N_DEV
ADAM_LR = 0.001
ADAM_B1 = 0.9
ADAM_B2 = 0.999
ADAM_EPS = 1e-08
ADAM_WD = 0.01
ADAM_STEP = 10

SMALLEST_NORMAL = 1.1754944e-38
SUBLANES = 8
LANES = 128
VMEM_LIMIT_BYTES = 48 * 2 ** 20
ROW_TILE = 256
MM_TILE = 512
ADAMW_BLOCK_BYTES = 2 ** 20
PACK_ROWS = 48
MESH_ID = pl.DeviceIdType.MESH


def _params(n_grid):
    return pltpu.CompilerParams(dimension_semantics=("arbitrary",) * n_grid, vmem_limit_bytes=VMEM_LIMIT_BYTES)


def _bdot(a, b):
    return jnp.dot(a.astype(BF16), b.astype(BF16), preferred_element_type=F32)


def _bdot_nt(a, b):
    return lax.dot_general(a.astype(BF16), b.astype(BF16), (((1,), (1,)), ((), ())), preferred_element_type=F32)


def _bdot_tn(a, b):
    return lax.dot_general(a.astype(BF16), b.astype(BF16), (((0,), (0,)), ((), ())), preferred_element_type=F32)


def _rstd(x):
    return lax.rsqrt(jnp.mean(x * x, axis=-1, keepdims=True) + NORM_EPS)


def _rms(x, g):
    return x * _rstd(x) * g


def _rms_bwd(x, g, dy):
    xh = x * _rstd(x)
    dyg = dy * g
    dx = _rstd(x) * (dyg - xh * jnp.mean(dyg * xh, axis=-1, keepdims=True))
    return dx, jnp.sum(dy * xh, axis=0, keepdims=True)


def _sigmoid(z):
    return 0.5 * jnp.tanh(0.5 * z) + 0.5


def _silu_and_grad(z):
    s = _sigmoid(z)
    return z * s, s * (1.0 + z * (1.0 - s))


def _softplus(z):
    return jnp.maximum(z, 0.0) + jnp.log(1.0 + jnp.exp(-jnp.abs(z)))


def _shift_rows(cur, before, after, d):
    ts = cur.shape[0]
    row = lax.broadcasted_iota(jnp.int32, (SUBLANES, cur.shape[1]), 0)
    out = pltpu.roll(cur, (-d) % ts, 0)
    if d < 0:
        edge = jnp.where(row < -d, pltpu.roll(before, (-d) % SUBLANES, 0), out[:SUBLANES])
        return jnp.concatenate([edge, out[SUBLANES:]], axis=0)
    edge = jnp.where(row >= SUBLANES - d, pltpu.roll(after, (-d) % SUBLANES, 0), out[ts - SUBLANES:])
    return jnp.concatenate([out[:ts - SUBLANES], edge], axis=0)


def _halo_specs(ts, s, width, col, tile=lambda i: i):
    per = ts // SUBLANES
    last = s // SUBLANES - 1
    return [
        pl.BlockSpec((ts, width), lambda i: (tile(i), col)),
        pl.BlockSpec((SUBLANES, width), lambda i: (jnp.maximum(tile(i) * per - 1, 0), col)),
        pl.BlockSpec((SUBLANES, width), lambda i: (jnp.minimum((tile(i) + 1) * per, last), col)),
    ]


def _halo_load(cur_ref, before_ref, after_ref, n_tiles, tile=lambda i: i):
    i = tile(pl.program_id(0))
    before = jnp.where(i > 0, before_ref[...], 0.0)
    after = jnp.where(i < n_tiles - 1, after_ref[...], 0.0)
    return cur_ref[...], before, after


def _full(shape):
    return pl.BlockSpec(shape, lambda *_: (0,) * len(shape))


def _peer(x, y, c, mask):
    px, py, pc = x ^ (mask >> 2), y ^ ((mask >> 1) & 1), c ^ (mask & 1)
    return (px, py, pc), 4 * px + 2 * py + pc


class Exchange:
    SIBLING = 1
    OTHER_CHIPS = (2, 4, 6)

    def __init__(self):
        self.args, self.out_shape, self._kinds = [], [], []

    def gather(self, block, columns=False, via_sibling=False):
        shape = (block.shape[0], N_DEV * block.shape[1]) if columns else (N_DEV,) + block.shape
        return self._add(block, shape, ("gather", columns, via_sibling))

    def scatter(self, stack, columns=False):
        shape = (N_DEV, stack.shape[0], stack.shape[1] // N_DEV) if columns else stack.shape
        return self._add(stack, shape, ("scatter", columns, False))

    def _add(self, arg, shape, kind):
        self.args.append(arg)
        self.out_shape.append(jax.ShapeDtypeStruct(shape, arg.dtype))
        self._kinds.append(kind)
        return len(self.args) - 1

    def semaphores(self):
        n = len(self.args)
        return [pltpu.SemaphoreType.DMA((n, N_DEV - 1)), pltpu.SemaphoreType.DMA((n, N_DEV - 1)), pltpu.SemaphoreType.DMA((n,))]

    def to_sibling(self, array):
        shape = (N_DEV // 2, array.shape[0], array.shape[1] // N_DEV)
        return self._add(array, shape, ("to_sibling", True, False))

    def among_chips(self, stack):
        return self._add(stack, stack.shape, ("among_chips", False, False))

    def _copies(self, position, in_refs, out_refs):
        x, y, c, me = position
        for arr, ((kind, columns, via_sibling), src, out) in enumerate(zip(self._kinds, in_refs, out_refs)):
            if kind == "to_sibling":
                width = src.shape[-1] // N_DEV
                for k in range(N_DEV // 2):
                    block = src.at[:, pl.ds(pl.multiple_of((2 * k + 1 - c) * width, LANES), width)]
                    yield arr, k + 1, block, out.at[k], out.at[k], False, self.SIBLING
                continue
            for mask in range(N_DEV):
                _, peer_id = _peer(x, y, c, mask)
                relayed = via_sibling and mask not in (0, self.SIBLING) + self.OTHER_CHIPS
                if kind == "among_chips":
                    if mask in (0,) + self.OTHER_CHIPS:
                        yield arr, mask, src.at[peer_id // 2], out.at[me // 2], out.at[peer_id // 2], False, mask
                elif kind == "gather":
                    if columns:
                        width = src.shape[-1]
                        yield (arr, mask, src, out.at[:, pl.ds(pl.multiple_of(me * width, LANES), width)],
                               out.at[:, pl.ds(pl.multiple_of(peer_id * width, LANES), width)], relayed, mask)
                    else:
                        yield arr, mask, src, out.at[me], out.at[peer_id], relayed, mask
                else:
                    if columns:
                        width = src.shape[-1] // N_DEV
                        block = src.at[:, pl.ds(pl.multiple_of(peer_id * width, LANES), width)]
                    else:
                        block = src.at[peer_id]
                    yield arr, mask, block, out.at[me], out.at[peer_id], False, mask

    def _remote(self, position, sems, arr, slot, to_mask, src, dst):
        x, y, c, _ = position
        return pltpu.make_async_remote_copy(src_ref=src, dst_ref=dst, send_sem=sems[0].at[arr, slot - 1], recv_sem=sems[1].at[arr, slot - 1],
                                            device_id=_peer(x, y, c, to_mask)[0], device_id_type=MESH_ID)

    def start(self, position, in_refs, out_refs, sems):
        for arr, slot, src, dst, _, relayed, to_mask in self._copies(position, in_refs, out_refs):
            if slot == 0:
                pltpu.make_async_copy(src, dst, sems[2].at[arr]).start()
            elif not relayed:
                self._remote(position, sems, arr, slot, to_mask, src, dst).start()

    def wait(self, position, in_refs, out_refs, sems):
        copies = list(self._copies(position, in_refs, out_refs))
        landings = {(arr, slot): landing for arr, slot, _, _, landing, _, _ in copies}
        passed_on = set()
        for arr, mask, src, _, landing, relayed, _ in copies:
            if relayed:
                held = landings[arr, mask ^ self.SIBLING]
                self._remote(position, sems, arr, mask ^ self.SIBLING, mask ^ self.SIBLING, src, held).wait_recv()
                self._remote(position, sems, arr, mask, self.SIBLING, held, held).start()
                passed_on.add((arr, mask ^ self.SIBLING))
        for arr, slot, src, dst, landing, relayed, to_mask in copies:
            if slot == 0:
                pltpu.make_async_copy(src, dst, sems[2].at[arr]).wait()
                continue
            if (arr, slot) not in passed_on:
                self._remote(position, sems, arr, slot, to_mask, src, landing).wait_recv()
            if relayed:
                held = landings[arr, slot ^ self.SIBLING]
                self._remote(position, sems, arr, slot, self.SIBLING, held, held).wait_send()
            else:
                self._remote(position, sems, arr, slot, to_mask, src, dst).wait_send()


def _call(body, *, name, grid, in_specs, out_specs, out_shape, args, scratch_shapes=(), exchange=None):
    single = not isinstance(out_shape, (list, tuple))
    if single:
        out_specs, out_shape = [out_specs], [out_shape]
    params = _params(len(grid))
    if exchange is None:
        outs = pl.pallas_call(body, name=name, grid=grid, in_specs=in_specs, out_specs=out_specs, out_shape=out_shape,
                              scratch_shapes=list(scratch_shapes), compiler_params=params)(*args)
        return outs[0] if single else outs
    counts = (len(args), len(exchange.args), len(out_shape), len(exchange.out_shape), len(scratch_shapes), 3)

    def wrapped(*refs):
        groups, at = [], 0
        for n in counts:
            groups.append(refs[at:at + n])
            at += n
        main_in, ex_in, main_out, ex_out, main_scratch, sems = groups
        x, y, c = lax.axis_index("x"), lax.axis_index("y"), lax.axis_index("c")
        position = (x, y, c, 4 * x + 2 * y + c)
        ids = [pl.program_id(a) for a in range(len(grid))]
        first = functools.reduce(jnp.logical_and, [i == 0 for i in ids])
        last = functools.reduce(jnp.logical_and, [i == g - 1 for i, g in zip(ids, grid)])

        @pl.when(first)
        def _():
            exchange.start(position, ex_in, ex_out, sems)

        body(*main_in, *main_out, *main_scratch)

        @pl.when(last)
        def _():
            exchange.wait(position, ex_in, ex_out, sems)

    hbm = pl.BlockSpec(memory_space=pl.ANY)
    outs = pl.pallas_call(
        wrapped, name=name, grid=grid, in_specs=list(in_specs) + [hbm] * counts[1], out_specs=list(out_specs) + [hbm] * counts[3],
        out_shape=list(out_shape) + exchange.out_shape, scratch_shapes=list(scratch_shapes) + exchange.semaphores(),
        compiler_params=params)(*args, *exchange.args)
    main = outs[:counts[2]]
    return (main[0] if single else main), outs[counts[2]:]


def run_exchange(exchange, name):
    return _call(lambda: None, name=name, grid=(1,), in_specs=[], out_specs=[], out_shape=[], args=[], exchange=exchange)[1]


def gather_matmul(x, g, w_block, small_block, tm):
    s, d = x.shape
    width = w_block.shape[1]
    pair = 2 * width
    n_chips = N_DEV // 2
    tm = min(tm, s)
    n_i = s // tm
    sibling = Exchange.SIBLING

    def body(chips_ref, x_ref, g_ref, wb_ref, sb_ref, proj_ref, h_ref, w_ref, small_ref, h_all, w_pair, send, recv, local, load_sem):
        j, i = pl.program_id(0), pl.program_id(1)
        xx, yy, cc = lax.axis_index("x"), lax.axis_index("y"), lax.axis_index("c")
        me = 4 * xx + 2 * yy + cc

        def block_of(dev):
            return w_ref.at[:, pl.ds(pl.multiple_of(dev * width, LANES), width)]

        def remote(arr, slot, to_mask, src, dst):
            return pltpu.make_async_remote_copy(src_ref=src, dst_ref=dst, send_sem=send.at[arr, slot - 1], recv_sem=recv.at[arr, slot - 1],
                                                device_id=_peer(xx, yy, cc, to_mask)[0], device_id_type=MESH_ID)

        @pl.when((j == 0) & (i == 0))
        def _():
            pltpu.make_async_copy(wb_ref, block_of(me), local.at[0]).start()
            pltpu.make_async_copy(sb_ref, small_ref.at[me], local.at[1]).start()
            for mask in (sibling,) + Exchange.OTHER_CHIPS[:-1]:
                remote(0, mask, mask, wb_ref, block_of(me)).start()
            for mask in range(1, N_DEV):
                remote(1, mask, mask, sb_ref, small_ref.at[me]).start()

        for step in range(n_chips):
            @pl.when((j == step) & (i == 0))
            def _(step=step):
                if step == 0:
                    pltpu.make_async_copy(wb_ref, block_of(me), local.at[0]).wait()
                    remote(0, sibling, sibling, wb_ref, block_of(me ^ sibling)).wait_recv()
                else:
                    mask = 2 * step
                    remote(0, mask, mask, wb_ref, block_of(me ^ mask)).wait_recv()
                    if step == 1:
                        remote(0, Exchange.OTHER_CHIPS[-1], Exchange.OTHER_CHIPS[-1], wb_ref, block_of(me)).start()
                    remote(0, mask | sibling, sibling, block_of(me ^ mask), block_of(me ^ mask)).start()
                    remote(0, mask | sibling, mask | sibling, wb_ref, block_of(me ^ (mask | sibling))).wait_recv()
                load = pltpu.make_async_copy(w_ref.at[:, pl.ds(pl.multiple_of(chips_ref[step] * pair, LANES), pair)], w_pair, load_sem)
                load.start()
                load.wait()

        rows = pl.ds(pl.multiple_of(i * tm, tm), tm)

        @pl.when(j == 0)
        def _():
            h = _rms(x_ref[...], g_ref[...]).astype(BF16)
            h_all[rows, :] = h
            h_ref[...] = h

        proj_ref[...] = jnp.dot(h_all[rows, :], w_pair[...], preferred_element_type=F32)

        @pl.when((j == n_chips - 1) & (i == n_i - 1))
        def _():
            pltpu.make_async_copy(sb_ref, small_ref.at[me], local.at[1]).wait()
            for mask in range(1, N_DEV):
                remote(1, mask, mask, sb_ref, small_ref.at[me ^ mask]).wait_recv()
                remote(1, mask, mask, sb_ref, small_ref.at[me]).wait_send()
            for mask in (sibling,) + Exchange.OTHER_CHIPS:
                remote(0, mask, mask, wb_ref, block_of(me)).wait_send()
            for mask in Exchange.OTHER_CHIPS:
                remote(0, mask | sibling, sibling, block_of(me ^ mask), block_of(me ^ mask)).wait_send()

    def first_pass_row(j, i, chips):
        return jnp.where(j == 0, i, n_i - 1), 0

    hbm = pl.BlockSpec(memory_space=pl.ANY)
    my_chip = 2 * lax.axis_index("x") + lax.axis_index("y")
    chips = (my_chip ^ jnp.arange(n_chips)).astype(jnp.int32)
    grid_spec = pltpu.PrefetchScalarGridSpec(
        num_scalar_prefetch=1, grid=(n_chips, n_i),
        in_specs=[pl.BlockSpec((tm, d), first_pass_row), pl.BlockSpec((1, d), lambda j, i, chips: (0, 0)), hbm, hbm],
        out_specs=[pl.BlockSpec((tm, pair), lambda j, i, chips: (i, chips[j])), pl.BlockSpec((tm, d), first_pass_row), hbm, hbm],
        scratch_shapes=[pltpu.VMEM((s, d), BF16), pltpu.VMEM((d, pair), BF16), pltpu.SemaphoreType.DMA((2, N_DEV - 1)),
                        pltpu.SemaphoreType.DMA((2, N_DEV - 1)), pltpu.SemaphoreType.DMA((2,)), pltpu.SemaphoreType.DMA(())])
    return pl.pallas_call(
        body, name="even_in", grid_spec=grid_spec,
        out_shape=[jax.ShapeDtypeStruct((s, N_DEV * width), F32), jax.ShapeDtypeStruct((s, d), BF16),
                   jax.ShapeDtypeStruct((d, N_DEV * width), w_block.dtype), jax.ShapeDtypeStruct((N_DEV,) + small_block.shape, small_block.dtype)],
        compiler_params=_params(2),
    )(chips, x, g, w_block, small_block)


def rms_matmul(x, g, w, tm, tn, name, exchange=None):
    s, d = x.shape
    n = w.shape[1]
    tm = min(tm, s)

    def body(x_ref, g_ref, w_ref, o_ref, h_ref):
        @pl.when(pl.program_id(1) == 0)
        def _():
            h_ref[...] = _rms(x_ref[...], g_ref[...]).astype(BF16)

        o_ref[...] = jnp.dot(h_ref[...], w_ref[...], preferred_element_type=F32)

    return _call(
        body, name=name, grid=(s // tm, n // tn),
        in_specs=[pl.BlockSpec((tm, d), lambda i, j: (i, 0)), _full((1, d)), pl.BlockSpec((d, tn), lambda i, j: (0, j))],
        out_specs=[pl.BlockSpec((tm, tn), lambda i, j: (i, j)), pl.BlockSpec((tm, d), lambda i, j: (i, 0))],
        out_shape=[jax.ShapeDtypeStruct((s, n), F32), jax.ShapeDtypeStruct((s, d), BF16)],
        args=[x, g, w], exchange=exchange)


def _gla_out_bwd(du, r, osum, gn, do_ref, dr_ref, dgn_ref):
    silu_r, dsilu_r = _silu_and_grad(r)
    for head in range(GLA_HEADS):
        vl = slice(head * GLA_DV, (head + 1) * GLA_DV)
        o_h, g_h, du_h = osum[:, vl], gn[:, vl], du[:, vl]
        dr_ref[:, vl] = (du_h * _rms(o_h, g_h) * dsilu_r[:, vl]).astype(BF16)
        do_h, dg_h = _rms_bwd(o_h, g_h, du_h * silu_r[:, vl])
        do_ref[:, vl] = do_h
        dgn_ref[...] += dg_h


def normbwd_matmul_nt(y, g, dout, w, tn, name, exchange=None, gla=None):
    s, d = y.shape
    n = w.shape[0]
    tm = min(MM_TILE, s)

    def body(*refs):
        if gla is None:
            y_ref, g_ref, dout_ref, w_ref, du_ref, dy_ref, dg_ref = refs
        else:
            y_ref, g_ref, dout_ref, w_ref, r_ref, o_ref, gn_ref, do_ref, dr_ref, dy_ref, dg_ref, dgn_ref = refs
        i, j = pl.program_id(0), pl.program_id(1)

        @pl.when(j == 0)
        def _():
            dy, dg = _rms_bwd(y_ref[...], g_ref[...], dout_ref[...])
            dy_ref[...] = dy.astype(BF16)

            @pl.when(i == 0)
            def _():
                dg_ref[...] = jnp.zeros_like(dg_ref)
                if gla is not None:
                    dgn_ref[...] = jnp.zeros_like(dgn_ref)

            dg_ref[...] += dg

        du = lax.dot_general(dy_ref[...], w_ref[...], (((1,), (1,)), ((), ())), preferred_element_type=F32)
        if gla is None:
            du_ref[...] = du
        else:
            _gla_out_bwd(du, r_ref[...], o_ref[...], gn_ref[...], do_ref, dr_ref, dgn_ref)

    row = pl.BlockSpec((tm, d), lambda i, j: (i, 0))
    in_specs = [row, _full((1, d)), row, pl.BlockSpec((tn, d), lambda i, j: (j, 0))]
    args = [y, g, dout, w]
    tail_specs = [row, _full((1, d))]
    tail_shapes = [jax.ShapeDtypeStruct((s, d), BF16), jax.ShapeDtypeStruct((1, d), F32)]
    if gla is None:
        out_specs = [pl.BlockSpec((tm, tn), lambda i, j: (i, j))] + tail_specs
        out_shape = [jax.ShapeDtypeStruct((s, n), F32)] + tail_shapes
    else:
        proj, osum, gnorm = gla
        assert n == tn == D_MODEL
        in_specs += [pl.BlockSpec((tm, D_MODEL), lambda i, j: (i, 2)), row, _full(gnorm.shape)]
        args += [proj, osum, gnorm]
        out_specs = [row, row] + tail_specs + [_full((1, GLA_DV))]
        out_shape = [jax.ShapeDtypeStruct((s, D_MODEL), F32), jax.ShapeDtypeStruct((s, D_MODEL), BF16)] + tail_shapes + [
            jax.ShapeDtypeStruct((1, GLA_DV), F32)]
    return _call(body, name=name, grid=(s // tm, n // tn), in_specs=in_specs, out_specs=out_specs, out_shape=out_shape,
                 args=args, exchange=exchange)


def matmul_tn(a, b, tm, tn, ts, out_dtype, name, exchange=None, b_first=None):
    s, m = a.shape
    n = b.shape[1] + (0 if b_first is None else tn)
    ts = min(ts, s)
    n_k = s // ts
    dims = (((0,), (0,)), ((), ()))

    def body(*refs):
        if b_first is None:
            a_ref, b_ref, o_ref, acc = refs
        else:
            a_ref, first_ref, b_ref, o_ref, acc = refs
        j, k = pl.program_id(1), pl.program_id(2)

        @pl.when(k == 0)
        def _():
            acc[...] = jnp.zeros_like(acc)

        if b_first is None:
            acc[...] += lax.dot_general(a_ref[...], b_ref[...], dims, preferred_element_type=F32)
        else:
            @pl.when(j == 0)
            def _():
                acc[...] += lax.dot_general(a_ref[...], first_ref[...], dims, preferred_element_type=F32)

            @pl.when(j > 0)
            def _():
                acc[...] += lax.dot_general(a_ref[...], b_ref[...], dims, preferred_element_type=F32)

        @pl.when(k == n_k - 1)
        def _():
            o_ref[...] = acc[...].astype(out_dtype)

    if b_first is None:
        b_specs, b_args = [pl.BlockSpec((ts, tn), lambda i, j, k: (k, j))], [b]
    else:
        b_specs = [pl.BlockSpec((ts, tn), lambda i, j, k: (k, 0)), pl.BlockSpec((ts, tn), lambda i, j, k: (k, jnp.maximum(j - 1, 0)))]
        b_args = [b_first, b]
    return _call(
        body, name=name, grid=(m // tm, n // tn, n_k),
        in_specs=[pl.BlockSpec((ts, tm), lambda i, j, k: (k, i))] + b_specs,
        out_specs=pl.BlockSpec((tm, tn), lambda i, j, k: (i, j)),
        out_shape=jax.ShapeDtypeStruct((m, n), out_dtype),
        scratch_shapes=[pltpu.VMEM((tm, tn), F32)], args=[a] + b_args, exchange=exchange)


def matmul_nt_normbwd(dproj, w, x, g, dres, tm, tk, name, exchange=None, first=None):
    s, kt = dproj.shape
    kt += 0 if first is None else tk
    d = w.shape[0]
    tm = min(tm, s)
    n_k = kt // tk
    dims = (((1,), (1,)), ((), ()))

    def body(*refs):
        if first is None:
            a_ref, w_ref, x_ref, g_ref, r_ref, dx_ref, dg_ref, acc = refs
        else:
            first_ref, a_ref, w_ref, x_ref, g_ref, r_ref, dx_ref, dg_ref, acc = refs
        i, k = pl.program_id(0), pl.program_id(1)

        @pl.when(k == 0)
        def _():
            acc[...] = jnp.zeros_like(acc)

        if first is None:
            acc[...] += lax.dot_general(a_ref[...], w_ref[...], dims, preferred_element_type=F32)
        else:
            @pl.when(k == 0)
            def _():
                acc[...] += lax.dot_general(first_ref[...], w_ref[...], dims, preferred_element_type=F32)

            @pl.when(k > 0)
            def _():
                acc[...] += lax.dot_general(a_ref[...], w_ref[...], dims, preferred_element_type=F32)

        @pl.when(k == n_k - 1)
        def _():
            dx, dg = _rms_bwd(x_ref[...], g_ref[...], acc[...])
            dx_ref[...] = r_ref[...] + dx

            @pl.when(i == 0)
            def _():
                dg_ref[...] = jnp.zeros_like(dg_ref)

            dg_ref[...] += dg

    row = pl.BlockSpec((tm, d), lambda i, k: (i, 0))
    if first is None:
        a_specs, a_args = [pl.BlockSpec((tm, tk), lambda i, k: (i, k))], [dproj]
    else:
        a_specs = [pl.BlockSpec((tm, tk), lambda i, k: (i, 0)), pl.BlockSpec((tm, tk), lambda i, k: (i, jnp.maximum(k - 1, 0)))]
        a_args = [first, dproj]
    return _call(
        body, name=name, grid=(s // tm, n_k),
        in_specs=a_specs + [pl.BlockSpec((d, tk), lambda i, k: (0, k)), row, _full((1, d)), row],
        out_specs=[row, _full((1, d))],
        out_shape=[jax.ShapeDtypeStruct((s, d), F32), jax.ShapeDtypeStruct((1, d), F32)],
        scratch_shapes=[pltpu.VMEM((tm, d), F32)], args=a_args + [w, x, g, dres], exchange=exchange)


def _rg_conv(xa, before, after, cw, cb):
    return (cw[0:1, :] * _shift_rows(xa, before, after, -2) + cw[1:2, :] * _shift_rows(xa, before, after, -1)
            + cw[2:3, :] * xa + cw[3:4, :] * _shift_rows(xa, before, after, 1) + cb)


def _rg_gates(ua_h, gw_ref, gb_ref, c_h, direction, head):
    r = _sigmoid(_bdot(ua_h, gw_ref[2 * direction, head]) + gb_ref[2 * direction, head:head + 1, :])
    i = _sigmoid(_bdot(ua_h, gw_ref[2 * direction + 1, head]) + gb_ref[2 * direction + 1, head:head + 1, :])
    log_a = -c_h * r
    a = jnp.exp(log_a)
    beta_sq = -jnp.tanh(log_a) * (1.0 + a * a)
    inv_beta = lax.rsqrt(jnp.maximum(beta_sq, SMALLEST_NORMAL))
    return r, i, a, beta_sq * inv_beta, inv_beta


def even_gates_fwd(proj, conv_w, conv_b, gate_w, gate_b, lam, exchange=None):
    s = proj.shape[0]
    ts = min(2 * ROW_TILE, s)
    n_tiles = s // ts

    def body(xa_ref, xb_ref, xn_ref, cw_ref, cb_ref, gw_ref, gb_ref, lam_ref, o_ref, hf_ref, carry):
        @pl.when(pl.program_id(0) == 0)
        def _():
            carry[...] = jnp.zeros_like(carry)

        xa, before, after = _halo_load(xa_ref, xb_ref, xn_ref, n_tiles)
        ua = _rg_conv(xa, before, after, cw_ref[...], cb_ref[...])
        c = RG_C * _softplus(-lam_ref[...])
        for direction in range(2):
            for head in range(RG_HEADS):
                lanes = slice(head * RG_HEAD_DIM, (head + 1) * RG_HEAD_DIM)
                ua_h = ua[:, lanes]
                _, i, a, beta, _ = _rg_gates(ua_h, gw_ref, gb_ref, c[direction:direction + 1, lanes], direction, head)
                o_ref[2 * direction, :, lanes] = a
                o_ref[2 * direction + 1, :, lanes] = beta * (i * ua_h)
        _scan_tile(o_ref.at[0], o_ref.at[1], hf_ref, carry, False, False)

    return _call(
        body, name="even_gates_fwd", grid=(n_tiles,),
        in_specs=_halo_specs(ts, s, D_MODEL, 0) + [_full(conv_w.shape), _full(conv_b.shape), _full(gate_w.shape),
                                                   _full(gate_b.shape), _full(lam.shape)],
        out_specs=[pl.BlockSpec((4, ts, D_MODEL), lambda i: (0, i, 0)), pl.BlockSpec((ts, D_MODEL), lambda i: (i, 0))],
        out_shape=[jax.ShapeDtypeStruct((4, s, D_MODEL), F32), jax.ShapeDtypeStruct((s, D_MODEL), F32)],
        scratch_shapes=[pltpu.VMEM((SUBLANES, D_MODEL), F32)],
        args=[proj, proj, proj, conv_w, conv_b, gate_w, gate_b, lam], exchange=exchange)


def _scan_tile(a_ref, b_ref, h_ref, carry, reverse, b_times_a):
    ts, c = h_ref.shape
    n_blocks = ts // SUBLANES
    row = lax.broadcasted_iota(jnp.int32, (SUBLANES, c), 0)

    def block(j, h_in):
        r0 = pl.multiple_of((n_blocks - 1 - j if reverse else j) * SUBLANES, SUBLANES)
        a = a_ref[pl.ds(r0, SUBLANES), :]
        b = b_ref[pl.ds(r0, SUBLANES), :]
        if b_times_a:
            b = a * b
        for step in (1, 2, 4):
            shift = SUBLANES - step if reverse else step
            valid = row < SUBLANES - step if reverse else row >= step
            b = jnp.where(valid, a * pltpu.roll(b, shift, 0) + b, b)
            a = jnp.where(valid, a * pltpu.roll(a, shift, 0), a)
        h = a * h_in + b
        h_ref[pl.ds(r0, SUBLANES), :] = h
        return h[0:1, :] if reverse else h[SUBLANES - 1:SUBLANES, :]

    carry[0:1, :] = lax.fori_loop(0, n_blocks, block, carry[0:1, :])


def linear_scan(a_arr, a_idx, b_arr, b_idx, reverse, b_times_a, name, exchange=None):
    _, s, c = a_arr.shape
    ts = min(MM_TILE, s)
    n_tiles = s // ts

    def tile_of(i):
        return n_tiles - 1 - i if reverse else i

    def body(a_ref, b_ref, h_ref, carry):
        @pl.when(pl.program_id(0) == 0)
        def _():
            carry[...] = jnp.zeros_like(carry)

        _scan_tile(a_ref, b_ref, h_ref, carry, reverse, b_times_a)

    return _call(
        body, name=name, grid=(n_tiles,),
        in_specs=[pl.BlockSpec((None, ts, c), lambda i: (a_idx, tile_of(i), 0)),
                  pl.BlockSpec((None, ts, c), lambda i: (b_idx, tile_of(i), 0))],
        out_specs=pl.BlockSpec((ts, c), lambda i: (tile_of(i), 0)),
        out_shape=jax.ShapeDtypeStruct((s, c), F32),
        scratch_shapes=[pltpu.VMEM((SUBLANES, c), F32)], args=[a_arr, b_arr], exchange=exchange)


def _sc_conv(p, before, after, w):
    return w[0:1, :] * _shift_rows(p, before, after, -1) + w[1:2, :] * p + w[2:3, :] * _shift_rows(p, before, after, 1)


def even_mix_fwd(ab, hf, proj, sc_w, w_out, xres, g_post, exchange=None):
    s = proj.shape[0]
    ts = min(ROW_TILE, s)
    n_tiles = s // ts

    def tile(i):
        return n_tiles - 1 - i

    row = pl.BlockSpec((ts, D_MODEL), lambda i: (tile(i), 0))

    def col(c):
        return pl.BlockSpec((ts, D_MODEL), lambda i: (tile(i), c))

    def body(a_ref, b_ref, hf_ref, za_ref, xb_ref, xbb_ref, xbn_ref, gb_ref, gc_ref, gcb_ref, gcn_ref, zb_ref, w_ref,
             wo_ref, x_ref, g_ref, u_ref, hb_ref, y_ref, out_ref, carry):
        @pl.when(pl.program_id(0) == 0)
        def _():
            carry[...] = jnp.zeros_like(carry)

        _scan_tile(a_ref, b_ref, hb_ref, carry, True, False)
        xb, xb_before, xb_after = _halo_load(xb_ref, xbb_ref, xbn_ref, n_tiles, tile)
        gc, gc_before, gc_after = _halo_load(gc_ref, gcb_ref, gcn_ref, n_tiles, tile)
        silu_za, _ = _silu_and_grad(za_ref[...])
        silu_zb, _ = _silu_and_grad(zb_ref[...])
        u_ref[:, :D_MODEL] = ((hf_ref[...] + hb_ref[...]) * silu_za).astype(BF16)
        cv = _sc_conv(gc * xb, gc_before * xb_before, gc_after * xb_after, w_ref[...])
        u_ref[:, D_MODEL:] = (gb_ref[...] * cv * silu_zb).astype(BF16)
        y = jnp.dot(u_ref[...], wo_ref[...], preferred_element_type=F32)
        y_ref[...] = y
        out_ref[...] = x_ref[...] + _rms(y, g_ref[...])

    return _call(
        body, name="even_mix_fwd", grid=(n_tiles,),
        in_specs=[pl.BlockSpec((None, ts, D_MODEL), lambda i: (2, tile(i), 0)), pl.BlockSpec((None, ts, D_MODEL), lambda i: (3, tile(i), 0)),
                  row, col(1)] + _halo_specs(ts, s, D_MODEL, 2, tile) + [col(3)] + _halo_specs(ts, s, D_MODEL, 4, tile)
        + [col(5), _full(sc_w.shape), _full(w_out.shape), row, _full(g_post.shape)],
        out_specs=[pl.BlockSpec((ts, 2 * D_MODEL), lambda i: (tile(i), 0)), row, row, row],
        out_shape=[jax.ShapeDtypeStruct((s, 2 * D_MODEL), BF16)] + [jax.ShapeDtypeStruct((s, D_MODEL), F32)] * 3,
        scratch_shapes=[pltpu.VMEM((SUBLANES, D_MODEL), F32)],
        args=[ab, ab, hf, proj, proj, proj, proj, proj, proj, proj, proj, proj, sc_w, w_out, xres, g_post], exchange=exchange)


def even_mix_bwd(du, hf, hb, proj, sc_w, ab, exchange=None):
    s = proj.shape[0]
    ts = min(ROW_TILE, s)
    n_tiles = s // ts
    row = pl.BlockSpec((ts, D_MODEL), lambda i: (i, 0))

    def body(dya_ref, dyb_ref, dybb_ref, dybn_ref, hf_ref, hb_ref, za_ref, xb_ref, xbb_ref, xbn_ref,
             gb_ref, gbb_ref, gbn_ref, gc_ref, gcb_ref, gcn_ref, zb_ref, zbb_ref, zbn_ref, w_ref, a_ref,
             dh_ref, dp_ref, dw_ref, adj_ref, carry):
        @pl.when(pl.program_id(0) == 0)
        def _():
            carry[...] = jnp.zeros_like(carry)

        dyb, dyb_before, dyb_after = _halo_load(dyb_ref, dybb_ref, dybn_ref, n_tiles)
        xb, xb_before, xb_after = _halo_load(xb_ref, xbb_ref, xbn_ref, n_tiles)
        gb, gb_before, gb_after = _halo_load(gb_ref, gbb_ref, gbn_ref, n_tiles)
        gc, gc_before, gc_after = _halo_load(gc_ref, gcb_ref, gcn_ref, n_tiles)
        zb, zb_before, zb_after = _halo_load(zb_ref, zbb_ref, zbn_ref, n_tiles)
        w = w_ref[...]
        dya, za = dya_ref[...], za_ref[...]
        silu_za, dsilu_za = _silu_and_grad(za)
        dh_ref[...] = dya * silu_za
        _scan_tile(a_ref, dh_ref, adj_ref, carry, False, True)
        dp_ref[:, 0:D_MODEL] = (dya * (hf_ref[...] + hb_ref[...]) * dsilu_za).astype(BF16)

        silu_zb, dsilu_zb = _silu_and_grad(zb)
        p, p_before, p_after = gc * xb, gc_before * xb_before, gc_after * xb_after
        cv = _sc_conv(p, p_before, p_after, w)
        dcv = dyb * gb * silu_zb
        dcv_before = dyb_before * gb_before * _silu_and_grad(zb_before)[0]
        dcv_after = dyb_after * gb_after * _silu_and_grad(zb_after)[0]
        dpp = (w[0:1, :] * _shift_rows(dcv, dcv_before, dcv_after, 1) + w[1:2, :] * dcv
               + w[2:3, :] * _shift_rows(dcv, dcv_before, dcv_after, -1))
        dp_ref[:, D_MODEL:2 * D_MODEL] = (dpp * gc).astype(BF16)
        dp_ref[:, 2 * D_MODEL:3 * D_MODEL] = (dyb * cv * silu_zb).astype(BF16)
        dp_ref[:, 3 * D_MODEL:4 * D_MODEL] = (dpp * xb).astype(BF16)
        dp_ref[:, 4 * D_MODEL:5 * D_MODEL] = (dyb * gb * cv * dsilu_zb).astype(BF16)

        @pl.when(pl.program_id(0) == 0)
        def _():
            dw_ref[...] = jnp.zeros_like(dw_ref)

        dw_ref[0:1, :] += jnp.sum(dcv * _shift_rows(p, p_before, p_after, -1), axis=0, keepdims=True)
        dw_ref[1:2, :] += jnp.sum(dcv * p, axis=0, keepdims=True)
        dw_ref[2:3, :] += jnp.sum(dcv * _shift_rows(p, p_before, p_after, 1), axis=0, keepdims=True)

    return _call(
        body, name="even_mix_bwd", grid=(n_tiles,),
        in_specs=[row] + _halo_specs(ts, s, D_MODEL, 1) + [row, row, pl.BlockSpec((ts, D_MODEL), lambda i: (i, 1))]
        + _halo_specs(ts, s, D_MODEL, 2) + _halo_specs(ts, s, D_MODEL, 3) + _halo_specs(ts, s, D_MODEL, 4)
        + _halo_specs(ts, s, D_MODEL, 5) + [_full(sc_w.shape), pl.BlockSpec((None, ts, D_MODEL), lambda i: (2, i, 0))],
        out_specs=[row, pl.BlockSpec((ts, 5 * D_MODEL), lambda i: (i, 0)), _full(sc_w.shape), row],
        out_shape=[jax.ShapeDtypeStruct((s, D_MODEL), F32), jax.ShapeDtypeStruct((s, 5 * D_MODEL), BF16),
                   jax.ShapeDtypeStruct(sc_w.shape, F32), jax.ShapeDtypeStruct((s, D_MODEL), F32)],
        scratch_shapes=[pltpu.VMEM((SUBLANES, D_MODEL), F32)],
        args=[du, du, du, du, hf, hb, proj, *([proj] * 12), sc_w, ab], exchange=exchange)


def even_gates_bwd(proj, adj_f, adj_b, hf, hb, dh, conv_w, conv_b, gate_w, gate_b, lam, exchange=None):
    s = proj.shape[0]
    ts = min(2 * ROW_TILE, s)
    n_tiles = s // ts
    row = pl.BlockSpec((ts, D_MODEL), lambda i: (i, 0))

    def body(xa_ref, xab_ref, xan_ref, af_ref, afb_ref, afn_ref, ab_ref, abb_ref, abn_ref,
             hf_ref, hfb_ref, hfn_ref, hb_ref, hbb_ref, hbn_ref, dh_ref,
             cw_ref, cb_ref, gw_ref, gb_ref, lam_ref, dua_ref, dgw_ref, dgb_ref, dlam_ref):
        @pl.when(pl.program_id(0) == 0)
        def _():
            dgw_ref[...] = jnp.zeros_like(dgw_ref)
            dgb_ref[...] = jnp.zeros_like(dgb_ref)
            dlam_ref[...] = jnp.zeros_like(dlam_ref)

        xa, before, after = _halo_load(xa_ref, xab_ref, xan_ref, n_tiles)
        ua = _rg_conv(xa, before, after, cw_ref[...], cb_ref[...])
        lam_v = lam_ref[...]
        c = RG_C * _softplus(-lam_v)
        dc_dlam = -RG_C * _sigmoid(-lam_v)
        dh = dh_ref[...]
        adj = (_halo_load(af_ref, afb_ref, afn_ref, n_tiles), _halo_load(ab_ref, abb_ref, abn_ref, n_tiles))
        hs = (_halo_load(hf_ref, hfb_ref, hfn_ref, n_tiles), _halo_load(hb_ref, hbb_ref, hbn_ref, n_tiles))
        dua = jnp.zeros_like(ua)
        for direction in range(2):
            step = 1 if direction == 0 else -1
            g = dh + _shift_rows(*adj[direction], step)
            da_all = g * _shift_rows(*hs[direction], -step)
            dua_parts = []
            for head in range(RG_HEADS):
                lanes = slice(head * RG_HEAD_DIM, (head + 1) * RG_HEAD_DIM)
                ua_h = ua[:, lanes]
                c_h = c[direction:direction + 1, lanes]
                r, i, a, beta, inv_beta = _rg_gates(ua_h, gw_ref, gb_ref, c_h, direction, head)
                db = g[:, lanes]
                d_i = db * beta * ua_h
                dbeta = db * (i * ua_h)
                dlog_a = (da_all[:, lanes] - dbeta * a * inv_beta) * a
                dpr = -c_h * dlog_a * r * (1.0 - r)
                dpi = d_i * i * (1.0 - i)
                dua_parts.append(db * beta * i + _bdot_nt(dpr, gw_ref[2 * direction, head])
                                 + _bdot_nt(dpi, gw_ref[2 * direction + 1, head]))
                dgw_ref[2 * direction, head] += _bdot_tn(ua_h, dpr)
                dgw_ref[2 * direction + 1, head] += _bdot_tn(ua_h, dpi)
                dgb_ref[2 * direction, head:head + 1, :] += jnp.sum(dpr, axis=0, keepdims=True)
                dgb_ref[2 * direction + 1, head:head + 1, :] += jnp.sum(dpi, axis=0, keepdims=True)
                dlam_ref[direction:direction + 1, lanes] += (
                    jnp.sum(-r * dlog_a, axis=0, keepdims=True) * dc_dlam[direction:direction + 1, lanes])
            dua = dua + jnp.concatenate(dua_parts, axis=1)
        dua_ref[...] = dua

    return _call(
        body, name="even_gates_bwd", grid=(n_tiles,),
        in_specs=_halo_specs(ts, s, D_MODEL, 0) * 5 + [row] + [_full(conv_w.shape), _full(conv_b.shape), _full(gate_w.shape),
                                                             _full(gate_b.shape), _full(lam.shape)],
        out_specs=[row, _full(gate_w.shape), _full(gate_b.shape), _full(lam.shape)],
        out_shape=[jax.ShapeDtypeStruct((s, D_MODEL), F32), jax.ShapeDtypeStruct(gate_w.shape, F32),
                   jax.ShapeDtypeStruct(gate_b.shape, F32), jax.ShapeDtypeStruct(lam.shape, F32)],
        args=[proj, proj, proj, adj_f, adj_f, adj_f, adj_b, adj_b, adj_b, hf, hf, hf, hb, hb, hb, dh, conv_w, conv_b, gate_w,
              gate_b, lam], exchange=exchange)


def rg_conv_bwd(dua, proj, conv_w, exchange=None):
    s = proj.shape[0]
    ts = min(2 * ROW_TILE, s)
    n_tiles = s // ts

    def body(du_ref, dub_ref, dun_ref, xa_ref, xab_ref, xan_ref, cw_ref, dp_ref, dw_ref, db_ref):
        @pl.when(pl.program_id(0) == 0)
        def _():
            dw_ref[...] = jnp.zeros_like(dw_ref)
            db_ref[...] = jnp.zeros_like(db_ref)

        dua, dua_before, dua_after = _halo_load(du_ref, dub_ref, dun_ref, n_tiles)
        xa, xa_before, xa_after = _halo_load(xa_ref, xab_ref, xan_ref, n_tiles)
        cw = cw_ref[...]
        dxa = (cw[0:1, :] * _shift_rows(dua, dua_before, dua_after, 2) + cw[1:2, :] * _shift_rows(dua, dua_before, dua_after, 1)
               + cw[2:3, :] * dua + cw[3:4, :] * _shift_rows(dua, dua_before, dua_after, -1))
        dp_ref[...] = dxa.astype(BF16)
        for tap, offset in enumerate((-2, -1, 0, 1)):
            shifted = xa if offset == 0 else _shift_rows(xa, xa_before, xa_after, offset)
            dw_ref[tap:tap + 1, :] += jnp.sum(dua * shifted, axis=0, keepdims=True)
        db_ref[...] += jnp.sum(dua, axis=0, keepdims=True)

    return _call(
        body, name="rg_conv_bwd", grid=(n_tiles,),
        in_specs=_halo_specs(ts, s, D_MODEL, 0) * 2 + [_full(conv_w.shape)],
        out_specs=[pl.BlockSpec((ts, D_MODEL), lambda i: (i, 0)), _full(conv_w.shape), _full((1, D_MODEL))],
        out_shape=[jax.ShapeDtypeStruct((s, D_MODEL), BF16), jax.ShapeDtypeStruct(conv_w.shape, F32),
                   jax.ShapeDtypeStruct((1, D_MODEL), F32)],
        args=[dua, dua, dua, proj, proj, proj, conv_w], exchange=exchange)


def _split3(x):
    x1 = x.astype(BF16)
    rest = x - x1.astype(F32)
    x2 = rest.astype(BF16)
    return x1, x2, (rest - x2.astype(F32)).astype(BF16)


def _chunk_sum_matrix(t, reverse, transpose):
    i = lax.broadcasted_iota(jnp.int32, (t, t), 0)
    j = lax.broadcasted_iota(jnp.int32, (t, t), 1)
    if transpose:
        i, j = j, i
    same = (i // GLA_CHUNK) == (j // GLA_CHUNK)
    return jnp.where(same & ((j >= i) if reverse else (j <= i)), 1.0, 0.0).astype(BF16)


def _exact_dot(m, x):
    return sum(jnp.dot(m, part, preferred_element_type=F32) for part in _split3(x))


def _chunk_mask(t, reverse):
    i = lax.broadcasted_iota(jnp.int32, (t, t), 0)
    j = lax.broadcasted_iota(jnp.int32, (t, t), 1)
    return ((i // GLA_CHUNK) == (j // GLA_CHUNK)) & ((j >= i) if reverse else (j <= i))


def _chunk_rows(c):
    return slice(c * GLA_CHUNK, (c + 1) * GLA_CHUNK)


def _gla_gate(lr, wg, bg):
    z = _bdot(lr, wg) + bg
    log_alpha = (jnp.minimum(z, 0.0) - jnp.log(1.0 + jnp.exp(-jnp.abs(z)))) * (1.0 / GLA_NORMALIZER)
    return z, log_alpha


def _gla_tile_terms(q, k, bcum, reverse):
    n_chunks = q.shape[0] // GLA_CHUNK
    totals = []
    for c in range(n_chunks):
        edge = c * GLA_CHUNK if reverse else (c + 1) * GLA_CHUNK - 1
        totals.append(bcum[edge:edge + 1, :])
    btot = jnp.concatenate([jnp.broadcast_to(total, (GLA_CHUNK, total.shape[1])) for total in totals], axis=0)
    e_pos, e_neg, e_st = jnp.exp(bcum), jnp.exp(-bcum), jnp.exp(btot - bcum)
    return q * (GLA_DK ** -0.5) * e_pos, k * e_neg, k * e_st, e_pos, e_neg, e_st, [jnp.exp(total) for total in totals]


def _gla_specs(t, n_tiles, reverse_order):
    def tile(i):
        return n_tiles - 1 - i if reverse_order else i

    return tile, [
        pl.BlockSpec((t, GLA_KEY), lambda i: (tile(i), 0)),
        pl.BlockSpec((t, GLA_KEY), lambda i: (tile(i), 1)),
        pl.BlockSpec((t, D_MODEL), lambda i: (tile(i), 1)),
        pl.BlockSpec((t, LANES), lambda i: (tile(i), (ODD_IN_PAD - LANES) // LANES)),
    ]


def gla_fwd(proj, wg, bg, reverse, o_other=None, gnorm=None, post=None):
    s = proj.shape[0]
    t = min(ROW_TILE, s)
    n_tiles = s // t
    n_chunks = t // GLA_CHUNK
    final = o_other is not None
    tile, specs = _gla_specs(t, n_tiles, reverse)

    def body(*refs):
        if final:
            (q_ref, k_ref, v_ref, lr_ref, wg_ref, bg_ref, oo_ref, r_ref, gn_ref, wo_ref, x_ref, gp_ref, t_ref,
             osum_ref, u_ref, st_ref, y_ref, dout_ref, loss_ref, state) = refs
        else:
            q_ref, k_ref, v_ref, lr_ref, wg_ref, bg_ref, o_ref, st_ref, state = refs
            osum_ref = o_ref

        @pl.when(pl.program_id(0) == 0)
        def _():
            state[...] = jnp.zeros_like(state)

        _, log_alpha = _gla_gate(lr_ref[...], wg_ref[...], bg_ref[...])
        bcum = _exact_dot(_chunk_sum_matrix(t, reverse, False), log_alpha)
        q, k, v = q_ref[...], k_ref[...], v_ref[...]
        q_in, k_in, k_st, _, _, _, decays = _gla_tile_terms(q, k, bcum, reverse)
        mask = _chunk_mask(t, reverse)
        order = list(range(n_chunks))[::-1] if reverse else list(range(n_chunks))
        intra, increments = [], []
        for head in range(GLA_HEADS):
            kl = slice(head * GLA_DK, (head + 1) * GLA_DK)
            vl = slice(head * GLA_DV, (head + 1) * GLA_DV)
            scores = jnp.where(mask, _bdot_nt(q_in[:, kl], k_in[:, kl]), 0.0)
            intra.append(_bdot(scores, v[:, vl]))
            increments.append([_bdot_tn(v[_chunk_rows(c), vl], k_st[_chunk_rows(c), kl]) for c in range(n_chunks)])
        for head in range(GLA_HEADS):
            kl = slice(head * GLA_DK, (head + 1) * GLA_DK)
            vl = slice(head * GLA_DV, (head + 1) * GLA_DV)
            running = state[head]
            before = [None] * n_chunks
            for c in order:
                before[c] = running
                st_ref[c, head] = running
                running = running * decays[c][:, kl] + increments[head][c]
            state[head] = running
            inter = [_bdot_nt(q_in[_chunk_rows(c), kl], before[c]) for c in range(n_chunks)]
            osum_ref[:, vl] = intra[head] + jnp.concatenate(inter, axis=0)
        if final:
            osum = osum_ref[...] + oo_ref[...]
            osum_ref[...] = osum
            silu_r, _ = _silu_and_grad(r_ref[...])
            gn = gn_ref[...]
            for head in range(GLA_HEADS):
                vl = slice(head * GLA_DV, (head + 1) * GLA_DV)
                u_ref[:, vl] = (_rms(osum[:, vl], gn[:, vl]) * silu_r[:, vl]).astype(BF16)

            @pl.when(pl.program_id(0) == 0)
            def _():
                loss_ref[...] = jnp.zeros_like(loss_ref)

            y = jnp.dot(u_ref[...], wo_ref[...], preferred_element_type=F32)
            y_ref[...] = y
            diff = x_ref[...] + _rms(y, gp_ref[...]) - t_ref[...]
            dout_ref[...] = diff * (1.0 / D_MODEL)
            loss_ref[...] += 0.5 * jnp.sum(jnp.mean(diff * diff, axis=-1, keepdims=True))

    row = pl.BlockSpec((t, D_MODEL), lambda i: (tile(i), 0))
    st_spec = pl.BlockSpec((n_chunks, GLA_HEADS, GLA_DV, GLA_DK), lambda i: (tile(i), 0, 0, 0))
    st_shape = jax.ShapeDtypeStruct((s // GLA_CHUNK, GLA_HEADS, GLA_DV, GLA_DK), F32)
    in_specs = specs + [_full(wg.shape), _full(bg.shape)]
    args = [proj, proj, proj, proj, wg, bg]
    if final:
        w_out, xres, g_post, target = post
        in_specs += [row, pl.BlockSpec((t, D_MODEL), lambda i: (tile(i), 2)), _full(gnorm.shape), _full(w_out.shape), row,
                     _full(g_post.shape), row]
        args += [o_other, proj, gnorm, w_out, xres, g_post, target]
        out_specs = [row, row, st_spec, row, row, _full((SUBLANES, LANES))]
        out_shape = [jax.ShapeDtypeStruct((s, D_MODEL), F32), jax.ShapeDtypeStruct((s, D_MODEL), BF16), st_shape,
                     jax.ShapeDtypeStruct((s, D_MODEL), F32), jax.ShapeDtypeStruct((s, D_MODEL), F32),
                     jax.ShapeDtypeStruct((SUBLANES, LANES), F32)]
    else:
        out_specs = [row, st_spec]
        out_shape = [jax.ShapeDtypeStruct((s, D_MODEL), F32), st_shape]
    return pl.pallas_call(
        body, name="gla_fwd_rev" if reverse else "gla_fwd", grid=(n_tiles,), in_specs=in_specs, out_specs=out_specs,
        out_shape=out_shape, scratch_shapes=[pltpu.VMEM((GLA_HEADS, GLA_DV, GLA_DK), F32)], compiler_params=_params(1),
    )(*args)


def gla_bwd(proj, wg, bg, do, states, reverse, first=None):
    s = proj.shape[0]
    t = min(ROW_TILE, s)
    n_tiles = s // t
    n_chunks = t // GLA_CHUNK
    final = first is not None
    tile, specs = _gla_specs(t, n_tiles, not reverse)

    def body(*refs):
        if final:
            (q_ref, k_ref, v_ref, lr_ref, wg_ref, bg_ref, do_ref, st_ref, dqkv1_ref, dlr1_ref, dr_ref,
             dp_ref, dwg_ref, dbg_ref, dstate, dqkv, dbc, dbt) = refs
        else:
            (q_ref, k_ref, v_ref, lr_ref, wg_ref, bg_ref, do_ref, st_ref,
             dqkv, dlr_ref, dwg_ref, dbg_ref, dstate, dbc, dbt) = refs

        @pl.when(pl.program_id(0) == 0)
        def _():
            dstate[...] = jnp.zeros_like(dstate)
            dwg_ref[...] = jnp.zeros_like(dwg_ref)
            dbg_ref[...] = jnp.zeros_like(dbg_ref)

        lr, wg_v = lr_ref[...], wg_ref[...]
        z, log_alpha = _gla_gate(lr, wg_v, bg_ref[...])
        bcum = _exact_dot(_chunk_sum_matrix(t, reverse, False), log_alpha)
        q, k, v, do_v = q_ref[...], k_ref[...], v_ref[...], do_ref[...]
        q_in, k_in, k_st, e_pos, e_neg, e_st, decays = _gla_tile_terms(q, k, bcum, reverse)
        mask = _chunk_mask(t, reverse)
        order = list(range(n_chunks)) if reverse else list(range(n_chunks))[::-1]
        dq_intra, dk_intra, dv_intra, increments = [], [], [], []
        for head in range(GLA_HEADS):
            kl = slice(head * GLA_DK, (head + 1) * GLA_DK)
            vl = slice(head * GLA_DV, (head + 1) * GLA_DV)
            scores = jnp.where(mask, _bdot_nt(q_in[:, kl], k_in[:, kl]), 0.0)
            dscores = jnp.where(mask, _bdot_nt(do_v[:, vl], v[:, vl]), 0.0)
            dv_intra.append(_bdot_tn(scores, do_v[:, vl]))
            dq_intra.append(_bdot(dscores, k_in[:, kl]))
            dk_intra.append(_bdot_tn(dscores, q_in[:, kl]))
            increments.append([_bdot_tn(do_v[_chunk_rows(c), vl], q_in[_chunk_rows(c), kl]) for c in range(n_chunks)])
        for head in range(GLA_HEADS):
            kl = slice(head * GLA_DK, (head + 1) * GLA_DK)
            vl = slice(head * GLA_DV, (head + 1) * GLA_DV)
            running = dstate[head]
            after, ddecay = [None] * n_chunks, [None] * n_chunks
            for c in order:
                after[c] = running
                ddecay[c] = jnp.sum(running * st_ref[c, head], axis=0, keepdims=True)
                running = running * decays[c][:, kl] + increments[head][c]
            dstate[head] = running
            dq_inter = jnp.concatenate([_bdot(do_v[_chunk_rows(c), vl], st_ref[c, head]) for c in range(n_chunks)], axis=0)
            dv_inter = jnp.concatenate([_bdot_nt(k_st[_chunk_rows(c), kl], after[c]) for c in range(n_chunks)], axis=0)
            dk_st = jnp.concatenate([_bdot(v[_chunk_rows(c), vl], after[c]) for c in range(n_chunks)], axis=0)
            dq_in = dq_intra[head] + dq_inter
            ks_h = k_st[:, kl]
            dqkv[:, 2 * GLA_KEY + head * GLA_DV:2 * GLA_KEY + (head + 1) * GLA_DV] = dv_intra[head] + dv_inter
            dqkv[:, kl] = dq_in * (GLA_DK ** -0.5) * e_pos[:, kl]
            dqkv[:, GLA_KEY + head * GLA_DK:GLA_KEY + (head + 1) * GLA_DK] = dk_intra[head] * e_neg[:, kl] + dk_st * e_st[:, kl]
            dbc[:, kl] = dq_in * q_in[:, kl] - dk_intra[head] * k_in[:, kl] - dk_st * ks_h
            weighted = dk_st * ks_h
            for c in range(n_chunks):
                dbtot = jnp.sum(weighted[_chunk_rows(c)], axis=0, keepdims=True) + ddecay[c] * decays[c][:, kl]
                dbt[_chunk_rows(c), kl] = jnp.broadcast_to(dbtot, (GLA_CHUNK, GLA_DK))
        dlog_alpha = _exact_dot(_chunk_sum_matrix(t, reverse, True), dbc[...]) + dbt[...]
        dz = dlog_alpha * _sigmoid(-z) * (1.0 / GLA_NORMALIZER)
        dlr = _bdot_nt(dz, wg_v)
        dwg_ref[...] += _bdot_tn(lr, dz)
        dbg_ref[...] += jnp.sum(dz, axis=0, keepdims=True)
        if final:
            dp_ref[:, :2 * D_MODEL] = (dqkv[...] + dqkv1_ref[...]).astype(BF16)
            dp_ref[:, 2 * D_MODEL:3 * D_MODEL] = dr_ref[...]
            dp_ref[:, 3 * D_MODEL:] = (dlr + dlr1_ref[...]).astype(BF16)
        else:
            dlr_ref[...] = dlr

    row = pl.BlockSpec((t, D_MODEL), lambda i: (tile(i), 0))
    wide = pl.BlockSpec((t, 2 * D_MODEL), lambda i: (tile(i), 0))
    narrow = pl.BlockSpec((t, LANES), lambda i: (tile(i), 0))
    st_spec = pl.BlockSpec((n_chunks, GLA_HEADS, GLA_DV, GLA_DK), lambda i: (tile(i), 0, 0, 0))
    in_specs = specs + [_full(wg.shape), _full(bg.shape), row, st_spec]
    args = [proj, proj, proj, proj, wg, bg, do, states]
    acc_specs = [_full(wg.shape), _full(bg.shape)]
    acc_shapes = [jax.ShapeDtypeStruct(wg.shape, F32), jax.ShapeDtypeStruct(bg.shape, F32)]
    scratch = [pltpu.VMEM((GLA_HEADS, GLA_DV, GLA_DK), F32)]
    work = [pltpu.VMEM((t, GLA_KEY), F32), pltpu.VMEM((t, GLA_KEY), F32)]
    if final:
        in_specs += [wide, narrow, row]
        args += list(first)
        out_specs = [pl.BlockSpec((t, ODD_IN_PAD), lambda i: (tile(i), 0))] + acc_specs
        out_shape = [jax.ShapeDtypeStruct((s, ODD_IN_PAD), BF16)] + acc_shapes
        scratch += [pltpu.VMEM((t, 2 * D_MODEL), F32)] + work
    else:
        out_specs = [wide, narrow] + acc_specs
        out_shape = [jax.ShapeDtypeStruct((s, 2 * D_MODEL), F32), jax.ShapeDtypeStruct((s, LANES), F32)] + acc_shapes
        scratch += work
    return pl.pallas_call(
        body, name="gla_bwd_rev" if reverse else "gla_bwd", grid=(n_tiles,), in_specs=in_specs, out_specs=out_specs,
        out_shape=out_shape, scratch_shapes=scratch, compiler_params=_params(1),
    )(*args)


def pair_sum(grad, from_sibling):
    n_chips, r, w = from_sibling.shape

    def body(even_ref, odd_ref, sib_ref, o_ref):
        mine = jnp.where(lax.axis_index("c") == 1, odd_ref[...], even_ref[...])
        o_ref[...] = (mine.astype(F32) + sib_ref[...].astype(F32)).astype(o_ref.dtype)

    return pl.pallas_call(
        body, name="pair_sum", grid=(n_chips,),
        in_specs=[pl.BlockSpec((r, w), lambda k: (0, 2 * k)), pl.BlockSpec((r, w), lambda k: (0, 2 * k + 1)),
                  pl.BlockSpec((None, r, w), lambda k: (k, 0, 0))],
        out_specs=pl.BlockSpec((None, r, w), lambda k: (k, 0, 0)),
        out_shape=jax.ShapeDtypeStruct(from_sibling.shape, from_sibling.dtype), compiler_params=_params(1),
    )(grad, grad, from_sibling)


def _adamw_update(g, w, m, v):
    new_m = ADAM_B1 * m + (1.0 - ADAM_B1) * g
    new_v = ADAM_B2 * v + (1.0 - ADAM_B2) * (g * g)
    m_hat = new_m / (1.0 - ADAM_B1 ** ADAM_STEP)
    v_hat = new_v / (1.0 - ADAM_B2 ** ADAM_STEP)
    return -ADAM_LR * (m_hat / (jnp.sqrt(v_hat) + ADAM_EPS) + ADAM_WD * w), new_m, new_v


def sum_parts(parts, name):
    _, r, c = parts.shape

    def body(p_ref, o_ref):
        total = p_ref[0].astype(F32)
        for j in range(1, N_DEV):
            total = total + p_ref[j].astype(F32)
        o_ref[...] = total

    return pl.pallas_call(body, name=name, in_specs=[_full(parts.shape)], out_specs=_full((r, c)), grid=(1,),
                          out_shape=jax.ShapeDtypeStruct((r, c), F32), compiler_params=_params(1))(parts)


def adamw(parts, w, m, v, name, exchange=None):
    n, r, c = parts.shape
    tr = r
    while tr * c * 4 > ADAMW_BLOCK_BYTES and tr % (2 * SUBLANES) == 0:
        tr //= 2

    def body(p_ref, w_ref, m_ref, v_ref, g_ref, d_ref, nm_ref, nv_ref):
        g = p_ref[0].astype(F32)
        for j in range(1, n):
            g = g + p_ref[j].astype(F32)
        g_ref[...] = g
        d_ref[...], nm_ref[...], nv_ref[...] = _adamw_update(g, w_ref[...], m_ref[...], v_ref[...])

    row = pl.BlockSpec((tr, c), lambda i: (i, 0))
    return _call(
        body, name=name, grid=(r // tr,),
        in_specs=[pl.BlockSpec((n, tr, c), lambda i: (0, i, 0)), row, row, row], out_specs=[row] * 4,
        out_shape=[jax.ShapeDtypeStruct((r, c), F32)] * 4, args=[parts, w, m, v], exchange=exchange)


def _small_views(shape):
    if len(shape) == 2:
        return [((slice(None), slice(None)), (slice(None), slice(None)))]
    if len(shape) == 3:
        return [((slice(None), slice(None)), (0,))]
    rows = shape[2]
    return [((slice(k * rows, (k + 1) * rows), slice(None)), (0, k)) for k in range(shape[1])]


def adamw_small(landings, w, m, v):
    names = list(landings)
    n = len(names)
    shapes = [w[name].shape for name in names]

    def body(*refs):
        land, ws, ms, vs = refs[:n], refs[n:2 * n], refs[2 * n:3 * n], refs[3 * n:4 * n]
        outs = [refs[(4 + k) * n:(5 + k) * n] for k in range(4)]
        for k in range(n):
            total = land[k][0]
            for j in range(1, N_DEV):
                total = total + land[k][j]
            for rows, at in _small_views(shapes[k]):
                g = total[rows]
                outs[0][k][at] = g
                outs[1][k][at], outs[2][k][at], outs[3][k][at] = _adamw_update(g, ws[k][at], ms[k][at], vs[k][at])

    blocks = [_full(sh) for sh in shapes]
    outs = pl.pallas_call(
        body, name="adamw_small", grid=(1,),
        in_specs=[_full(landings[name].shape) for name in names] + blocks * 3, out_specs=blocks * 4,
        out_shape=[jax.ShapeDtypeStruct(sh, F32) for sh in shapes] * 4, compiler_params=_params(1),
    )(*[landings[name] for name in names], *[src[name] for src in (w, m, v) for name in names])
    return [dict(zip(names, outs[k * n:(k + 1) * n])) for k in range(4)]


def adamw_replicated(land_vec, land_gate_b, land_loss, names, w, m, v, gate_b):
    n = len(names)

    def body(*refs):
        vec_ref, gb_ref, loss_ref = refs[:3]
        ws, ms, vs = refs[3:3 + n], refs[3 + n:3 + 2 * n], refs[3 + 2 * n:3 + 3 * n]
        gw_ref, gm_ref, gv_ref = refs[3 + 3 * n:6 + 3 * n]
        outs = refs[6 + 3 * n:]
        vec, gb, loss = vec_ref[0], gb_ref[0], loss_ref[0]
        for j in range(1, N_DEV):
            vec, gb, loss = vec + vec_ref[j], gb + gb_ref[j], loss + loss_ref[j]
        for k in range(n):
            g = vec[k:k + 1, :]
            outs[k][...] = g
            outs[n + k][...], outs[2 * n + k][...], outs[3 * n + k][...] = _adamw_update(g, ws[k][...], ms[k][...], vs[k][...])
        outs[4 * n][...] = gb
        outs[4 * n + 1][...], outs[4 * n + 2][...], outs[4 * n + 3][...] = _adamw_update(gb, gw_ref[...], gm_ref[...], gv_ref[...])
        outs[4 * n + 4][...] = loss

    vec_block, gb_block = _full((1, D_MODEL)), _full(gate_b[0].shape)
    outs = pl.pallas_call(
        body, name="adamw_replicated", grid=(1,),
        in_specs=[_full(land_vec.shape), _full(land_gate_b.shape), _full(land_loss.shape)] + [vec_block] * (3 * n) + [gb_block] * 3,
        out_specs=[vec_block] * (4 * n) + [gb_block] * 4 + [_full(land_loss.shape[1:])],
        out_shape=[jax.ShapeDtypeStruct((1, D_MODEL), F32)] * (4 * n) + [jax.ShapeDtypeStruct(gate_b[0].shape, F32)] * 4
        + [jax.ShapeDtypeStruct(land_loss.shape[1:], F32)],
        compiler_params=_params(1),
    )(land_vec, land_gate_b, land_loss, *[src[name] for src in (w, m, v) for name in names], *gate_b)
    results = {name: [outs[k * n + i] for k in range(4)] for i, name in enumerate(names)}
    return results, outs[4 * n:4 * n + 4], outs[4 * n + 4]


SMALL_SHARDED = ("rg_conv_w", "rg_lambda", "sc_conv_w", "odd_norm_pre", "odd_norm_post", "gla_b_gate", "gla_norm_g", "gla_w_gate_lr")
SMALL_ROWS = {"rg_conv_w": (0, 4), "rg_lambda": (4, 2), "sc_conv_w": (6, 3), "odd_norm_pre": (9, 1), "odd_norm_post": (10, 1),
              "gla_b_gate": (11, 2), "gla_norm_g": (13, 1), "gla_w_gate_lr": (16, 32)}


def _pack_small(shards):
    pieces, at = [], 0
    for name in SMALL_SHARDED:
        start, rows = SMALL_ROWS[name]
        if start > at:
            pieces.append(jnp.zeros((start - at, LANES), F32))
        a = shards[name].reshape(rows, -1)
        pieces.append(jnp.pad(a, ((0, 0), (0, LANES - a.shape[1]))))
        at = start + rows
    return jnp.concatenate(pieces, axis=0)


def _unpack_gathered(g):
    def cols(name, width):
        start, rows = SMALL_ROWS[name]
        return jnp.transpose(g[:, start:start + rows, :width], (1, 0, 2)).reshape(rows, N_DEV * width)

    w_lr = cols("gla_w_gate_lr", GLA_KEY // N_DEV).reshape(2, GLA_RANK, GLA_KEY)
    return dict(rg_conv_w=cols("rg_conv_w", LANES), rg_lambda=cols("rg_lambda", LANES), sc_conv_w=cols("sc_conv_w", LANES),
                odd_norm_pre=cols("odd_norm_pre", LANES), odd_norm_post=cols("odd_norm_post", LANES),
                gla_b_gate=cols("gla_b_gate", GLA_KEY // N_DEV), gla_norm_g=cols("gla_norm_g", GLA_DV // N_DEV), gla_w_gate_lr=w_lr)


def _blocks_along_columns(a, rows):
    return jnp.transpose(a.reshape(rows, N_DEV, -1), (1, 0, 2))


def kernel(x, even_norm_pre, even_norm_post, even_w_in, rg_conv_w, rg_conv_b, rg_gate_w, rg_gate_b, rg_lambda, sc_conv_w, even_w_out, odd_norm_pre, odd_norm_post, odd_w_in, gla_w_gate_lr, gla_b_gate, gla_norm_g, odd_w_out, loss_target, m_even_norm_pre, m_even_norm_post, m_even_w_in, m_rg_conv_w, m_rg_conv_b, m_rg_gate_w, m_rg_gate_b, m_rg_lambda, m_sc_conv_w, m_even_w_out, m_odd_norm_pre, m_odd_norm_post, m_odd_w_in, m_gla_w_gate_lr, m_gla_b_gate, m_gla_norm_g, m_odd_w_out, v_even_norm_pre, v_even_norm_post, v_even_w_in, v_rg_conv_w, v_rg_conv_b, v_rg_gate_w, v_rg_gate_b, v_rg_lambda, v_sc_conv_w, v_even_w_out, v_odd_norm_pre, v_odd_norm_post, v_odd_w_in, v_gla_w_gate_lr, v_gla_b_gate, v_gla_norm_g, v_odd_w_out):
    weights = dict(even_norm_pre=even_norm_pre, even_norm_post=even_norm_post, even_w_in=even_w_in, rg_conv_w=rg_conv_w,
                   rg_conv_b=rg_conv_b, rg_gate_w=rg_gate_w, rg_gate_b=rg_gate_b, rg_lambda=rg_lambda, sc_conv_w=sc_conv_w,
                   even_w_out=even_w_out, odd_norm_pre=odd_norm_pre, odd_norm_post=odd_norm_post, odd_w_in=odd_w_in,
                   gla_w_gate_lr=gla_w_gate_lr, gla_b_gate=gla_b_gate, gla_norm_g=gla_norm_g, odd_w_out=odd_w_out)
    m_in = dict(even_norm_pre=m_even_norm_pre, even_norm_post=m_even_norm_post, even_w_in=m_even_w_in, rg_conv_w=m_rg_conv_w,
                rg_conv_b=m_rg_conv_b, rg_gate_w=m_rg_gate_w, rg_gate_b=m_rg_gate_b, rg_lambda=m_rg_lambda, sc_conv_w=m_sc_conv_w,
                even_w_out=m_even_w_out, odd_norm_pre=m_odd_norm_pre, odd_norm_post=m_odd_norm_post, odd_w_in=m_odd_w_in,
                gla_w_gate_lr=m_gla_w_gate_lr, gla_b_gate=m_gla_b_gate, gla_norm_g=m_gla_norm_g, odd_w_out=m_odd_w_out)
    v_in = dict(even_norm_pre=v_even_norm_pre, even_norm_post=v_even_norm_post, even_w_in=v_even_w_in, rg_conv_w=v_rg_conv_w,
                rg_conv_b=v_rg_conv_b, rg_gate_w=v_rg_gate_w, rg_gate_b=v_rg_gate_b, rg_lambda=v_rg_lambda, sc_conv_w=v_sc_conv_w,
                even_w_out=v_even_w_out, odd_norm_pre=v_odd_norm_pre, odd_norm_post=v_odd_norm_post, odd_w_in=v_odd_w_in,
                gla_w_gate_lr=v_gla_w_gate_lr, gla_b_gate=v_gla_b_gate, gla_norm_g=v_gla_norm_g, odd_w_out=v_odd_w_out)
    names = list(weights)
    shapes = {n: weights[n].shape for n in names}
    xs = x[0]
    tgt = loss_target[0]

    proj_e, h_e, w_in_e, small_all = gather_matmul(xs, even_norm_pre, even_w_in[0].astype(BF16),
                                                   _pack_small({n: weights[n][0] for n in SMALL_SHARDED}), 2 * MM_TILE)
    small = _unpack_gathered(small_all)
    gate_w = rg_gate_w[0].reshape(4, RG_HEADS, RG_HEAD_DIM, RG_HEAD_DIM).astype(BF16)
    gate_b = rg_gate_b[0].reshape(4, RG_HEADS, RG_HEAD_DIM)
    conv_b = rg_conv_b
    wg_pad = [jnp.pad(small["gla_w_gate_lr"][d], ((GLA_RANK * d, LANES - GLA_RANK * (d + 1)), (0, 0))).astype(BF16) for d in range(2)]
    bg = [small["gla_b_gate"][d:d + 1] for d in range(2)]
    gnorm = jnp.tile(small["gla_norm_g"], (1, GLA_HEADS))

    half = D_MODEL // 2
    behind_gates = Exchange()
    behind_gates.gather(even_w_out[0].astype(BF16), via_sibling=True)
    behind_gates.gather(odd_w_in[0, :half].astype(BF16), via_sibling=True)
    (ab, hf), (w_out_e, w_in_o_top) = even_gates_fwd(proj_e, small["rg_conv_w"], conv_b, gate_w, gate_b, small["rg_lambda"],
                                                     exchange=behind_gates)
    w_out_e = w_out_e.reshape(2 * D_MODEL, D_MODEL)
    behind_mix_fwd = Exchange()
    behind_mix_fwd.gather(odd_w_in[0, half:].astype(BF16), via_sibling=True)
    behind_mix_fwd.gather(odd_w_out[0].astype(BF16), via_sibling=True)
    (u_e, hb, y_e, x1), (w_in_o_bottom, w_out_o) = even_mix_fwd(ab, hf, proj_e, small["sc_conv_w"], w_out_e, xs, even_norm_post,
                                                                exchange=behind_mix_fwd)
    w_out_o = w_out_o.reshape(D_MODEL, D_MODEL)
    w_in_o = jnp.concatenate([jnp.transpose(part, (1, 0, 2)).reshape(half, ODD_IN) for part in (w_in_o_top, w_in_o_bottom)], axis=0)
    w_in_o = jnp.pad(w_in_o, ((0, 0), (0, ODD_IN_PAD - ODD_IN)))

    proj_o, h_o = rms_matmul(x1, small["odd_norm_pre"], w_in_o, MM_TILE, ODD_IN_PAD, "odd_in")
    o_f, st_f = gla_fwd(proj_o, wg_pad[0], bg[0], False)
    osum, u_o, st_b, y_o, dout, loss_part = gla_fwd(proj_o, wg_pad[1], bg[1], True, o_other=o_f, gnorm=gnorm,
                                                    post=(w_out_o, x1, small["odd_norm_post"], tgt))

    do, dr, dy_o, d_odd_norm_post, d_gnorm = normbwd_matmul_nt(y_o, small["odd_norm_post"], dout, w_out_o, D_MODEL, "odd_out_bwd",
                                                               gla=(proj_o, osum, gnorm))
    d_w_out_o = matmul_tn(u_o, dy_o, D_MODEL, D_MODEL, 4 * MM_TILE, BF16, "odd_w_out_grad")
    dqkv_f, dlr_f, dwg_f, dbg_f = gla_bwd(proj_o, wg_pad[0], bg[0], do, st_f, False)
    dproj_o, dwg_b, dbg_b = gla_bwd(proj_o, wg_pad[1], bg[1], do, st_b, True, first=(dqkv_f, dlr_f, dr))
    dx1, d_odd_norm_pre = matmul_nt_normbwd(dproj_o, w_in_o, x1, small["odd_norm_pre"], dout, MM_TILE, ODD_IN_PAD, "odd_in_bwd")
    d_w_in_o = matmul_tn(h_o, dproj_o, D_MODEL, ODD_IN_PAD // 5, 8 * MM_TILE, BF16, "odd_w_in_grad")

    landed = {}
    behind_out = Exchange()
    behind_out.scatter(d_w_out_o.reshape(N_DEV, D_MODEL // N_DEV, D_MODEL))
    behind_out.scatter(d_odd_norm_pre, columns=True)
    behind_out.scatter(d_odd_norm_post, columns=True)
    behind_out.scatter(_blocks_along_columns(jnp.concatenate([dbg_f, dbg_b], axis=0), 2))
    behind_out.scatter(_blocks_along_columns(d_gnorm, 1))
    behind_out.scatter(_blocks_along_columns(jnp.concatenate([dwg_f[:GLA_RANK], dwg_b[GLA_RANK:2 * GLA_RANK]], axis=0), 2 * GLA_RANK))
    (du_e, dy_e, d_even_norm_post), got = normbwd_matmul_nt(y_e, even_norm_post, dx1, w_out_e, 2 * D_MODEL, "even_out_bwd",
                                                           exchange=behind_out)
    p_w_out_o = got[0]
    for n, part in zip(("odd_norm_pre", "odd_norm_post", "gla_b_gate", "gla_norm_g", "gla_w_gate_lr"), got[1:]):
        landed[n] = part
    d_w_out_e = matmul_tn(u_e, dy_e, D_MODEL, D_MODEL, 4 * MM_TILE, BF16, "even_w_out_grad")
    behind_mix = Exchange()
    behind_mix.scatter(d_w_out_e.reshape(N_DEV, 2 * D_MODEL // N_DEV, D_MODEL))
    (dh, drest, d_sc_w, adj_b), (p_w_out_e,) = even_mix_bwd(du_e, hf, hb, proj_e, small["sc_conv_w"], ab, exchange=behind_mix)
    adj_f = linear_scan(ab, 0, dh.reshape(1, *dh.shape), 0, True, True, "scan_fwd_adjoint")
    behind_gates_bwd = Exchange()
    behind_gates_bwd.scatter(jnp.transpose(d_w_in_o[:, :ODD_IN].reshape(D_MODEL, N_DEV, ODD_SHARD), (1, 0, 2)))
    behind_gates_bwd.scatter(d_sc_w, columns=True)
    (dua, d_gate_w, d_gate_b, d_lam), (p_w_in_o, landed["sc_conv_w"]) = even_gates_bwd(
        proj_e, adj_f, adj_b, hf, hb, dh, small["rg_conv_w"], conv_b, gate_w, gate_b, small["rg_lambda"], exchange=behind_gates_bwd)
    gate_w_rows = 4 * RG_HEADS * RG_HEAD_DIM
    behind_conv = Exchange()
    behind_conv.scatter(d_gate_w.reshape(N_DEV, gate_w_rows // N_DEV, RG_HEAD_DIM))
    behind_conv.scatter(d_lam, columns=True)
    (dxa, d_conv_w, d_conv_b), (p_gate_w, landed["rg_lambda"]) = rg_conv_bwd(dua, proj_e, small["rg_conv_w"], exchange=behind_conv)
    behind_w_grad = Exchange()
    behind_w_grad.gather(sum_parts(p_gate_w, "sum_gate_w"))
    d_w_in_e, (g_gate_w_all,) = matmul_tn(h_e, drest, D_MODEL, D_MODEL, 4 * MM_TILE, BF16, "even_w_in_grad",
                                          exchange=behind_w_grad, b_first=dxa)
    to_sibling = Exchange()
    to_sibling.to_sibling(d_w_in_e)
    to_sibling.scatter(d_conv_w, columns=True)
    from_sibling, landed["rg_conv_w"] = run_exchange(to_sibling, "scatter_to_sibling")
    behind_in_bwd = Exchange()
    behind_in_bwd.among_chips(pair_sum(d_w_in_e, from_sibling))
    (grad_x, d_even_norm_pre), (p_w_in_e,) = matmul_nt_normbwd(
        drest, w_in_e, xs, even_norm_pre, dx1, 2 * MM_TILE, D_MODEL, "even_in_bwd", exchange=behind_in_bwd, first=dxa)
    last = Exchange()
    replicated_vecs = ("even_norm_pre", "even_norm_post", "rg_conv_b")
    last.gather(jnp.concatenate([d_even_norm_pre, d_even_norm_post, d_conv_b], axis=0))
    last.gather(d_gate_b.reshape(4 * RG_HEADS, RG_HEAD_DIM))
    last.gather(loss_part)

    results = {}

    def update(name, parts_, shape2d, exchange=None):
        outs = adamw(parts_, weights[name][0].reshape(shape2d), m_in[name][0].reshape(shape2d), v_in[name][0].reshape(shape2d),
                     "adamw_" + name, exchange=exchange)
        if exchange is not None:
            outs, gathered = outs
        results[name] = [o.reshape(shapes[name]) for o in outs]
        return gathered if exchange is not None else None

    land_vec, land_gate_b, land_loss = update("even_w_in", p_w_in_e, (D_MODEL, EVEN_SHARD), exchange=last)
    update("even_w_out", p_w_out_e, (2 * D_MODEL // N_DEV, D_MODEL))
    update("odd_w_in", p_w_in_o, (D_MODEL, ODD_SHARD))
    update("odd_w_out", p_w_out_o, (D_MODEL // N_DEV, D_MODEL))
    update("rg_gate_w", g_gate_w_all.reshape(1, gate_w_rows, RG_HEAD_DIM), (gate_w_rows, RG_HEAD_DIM))
    small_out = adamw_small({n: landed[n] for n in SMALL_SHARDED}, weights, m_in, v_in)
    for n in SMALL_SHARDED:
        results[n] = [o[n] for o in small_out]
    gate_b_shape = (4 * RG_HEADS, RG_HEAD_DIM)
    rep_out, gate_b_out, loss_all = adamw_replicated(land_vec, land_gate_b, land_loss, replicated_vecs, weights, m_in, v_in,
                                                     [src["rg_gate_b"].reshape(gate_b_shape) for src in (weights, m_in, v_in)])
    results.update(rep_out)
    results["rg_gate_b"] = [o.reshape(shapes["rg_gate_b"]) for o in gate_b_out]

    return (loss_all[0, 0], grad_x.reshape(x.shape), *[results[n][0] for n in names], *[results[n][1] for n in names],
            *[results[n][2] for n in names], *[results[n][3] for n in names])
```

```python
import functools

import jax
import jax.numpy as jnp
from jax import lax
from jax.experimental import pallas as pl
from jax.experimental.pallas import tpu as pltpu

F32 = jnp.float32
BF16 = jnp.bfloat16

N_DEV = 8
D_MODEL = 1024
NORM_EPS = 1e-6
RG_HEADS = 8
RG_HEAD_DIM = 128
RG_C = 8.0
GLA_HEADS = 4
GLA_DK = 128
GLA_DV = 256
GLA_KEY = 512
GLA_RANK = 16
GLA_NORMALIZER = 16.0
GLA_CHUNK = 64
EVEN_IN = 6144
ODD_IN = 3104
ODD_IN_PAD = 3200
ODD_SHARD = ODD_IN // N_DEV
EVEN_SHARD = EVEN_IN // N_DEV
ADAM_LR = 0.001
ADAM_B1 = 0.9
ADAM_B2 = 0.999
ADAM_EPS = 1e-08
ADAM_WD = 0.01
ADAM_STEP = 10

SMALLEST_NORMAL = 1.1754944e-38
SUBLANES = 8
LANES = 128
VMEM_LIMIT_BYTES = 48 * 2 ** 20
ROW_TILE = 256
MM_TILE = 512
ADAMW_BLOCK_BYTES = 2 ** 20
PACK_ROWS = 48
MESH_ID = pl.DeviceIdType.MESH


def _params(n_grid):
    return pltpu.CompilerParams(dimension_semantics=("arbitrary",) * n_grid, vmem_limit_bytes=VMEM_LIMIT_BYTES)


def _bdot(a, b):
    return jnp.dot(a.astype(BF16), b.astype(BF16), preferred_element_type=F32)


def _bdot_nt(a, b):
    return lax.dot_general(a.astype(BF16), b.astype(BF16), (((1,), (1,)), ((), ())), preferred_element_type=F32)


def _bdot_tn(a, b):
    return lax.dot_general(a.astype(BF16), b.astype(BF16), (((0,), (0,)), ((), ())), preferred_element_type=F32)


def _rstd(x):
    return lax.rsqrt(jnp.mean(x * x, axis=-1, keepdims=True) + NORM_EPS)


def _rms(x, g):
    return x * _rstd(x) * g


def _rms_bwd(x, g, dy):
    xh = x * _rstd(x)
    dyg = dy * g
    dx = _rstd(x) * (dyg - xh * jnp.mean(dyg * xh, axis=-1, keepdims=True))
    return dx, jnp.sum(dy * xh, axis=0, keepdims=True)


def _sigmoid(z):
    return 0.5 * jnp.tanh(0.5 * z) + 0.5


def _silu_and_grad(z):
    s = _sigmoid(z)
    return z * s, s * (1.0 + z * (1.0 - s))


def _softplus(z):
    return jnp.maximum(z, 0.0) + jnp.log(1.0 + jnp.exp(-jnp.abs(z)))


def _shift_rows(cur, before, after, d):
    ts = cur.shape[0]
    row = lax.broadcasted_iota(jnp.int32, (SUBLANES, cur.shape[1]), 0)
    out = pltpu.roll(cur, (-d) % ts, 0)
    if d < 0:
        edge = jnp.where(row < -d, pltpu.roll(before, (-d) % SUBLANES, 0), out[:SUBLANES])
        return jnp.concatenate([edge, out[SUBLANES:]], axis=0)
    edge = jnp.where(row >= SUBLANES - d, pltpu.roll(after, (-d) % SUBLANES, 0), out[ts - SUBLANES:])
    return jnp.concatenate([out[:ts - SUBLANES], edge], axis=0)


def _halo_specs(ts, s, width, col, tile=lambda i: i):
    per = ts // SUBLANES
    last = s // SUBLANES - 1
    return [
        pl.BlockSpec((ts, width), lambda i: (tile(i), col)),
        pl.BlockSpec((SUBLANES, width), lambda i: (jnp.maximum(tile(i) * per - 1, 0), col)),
        pl.BlockSpec((SUBLANES, width), lambda i: (jnp.minimum((tile(i) + 1) * per, last), col)),
    ]


def _halo_load(cur_ref, before_ref, after_ref, n_tiles, tile=lambda i: i):
    i = tile(pl.program_id(0))
    before = jnp.where(i > 0, before_ref[...], 0.0)
    after = jnp.where(i < n_tiles - 1, after_ref[...], 0.0)
    return cur_ref[...], before, after


def _full(shape):
    return pl.BlockSpec(shape, lambda *_: (0,) * len(shape))


def _peer(x, y, c, mask):
    px, py, pc = x ^ (mask >> 2), y ^ ((mask >> 1) & 1), c ^ (mask & 1)
    return (px, py, pc), 4 * px + 2 * py + pc


class Exchange:
    SIBLING = 1
    OTHER_CHIPS = (2, 4, 6)

    def __init__(self):
        self.args, self.out_shape, self._kinds = [], [], []

    def gather(self, block, columns=False, via_sibling=False):
        shape = (block.shape[0], N_DEV * block.shape[1]) if columns else (N_DEV,) + block.shape
        return self._add(block, shape, ("gather", columns, via_sibling))

    def scatter(self, stack, columns=False):
        shape = (N_DEV, stack.shape[0], stack.shape[1] // N_DEV) if columns else stack.shape
        return self._add(stack, shape, ("scatter", columns, False))

    def _add(self, arg, shape, kind):
        self.args.append(arg)
        self.out_shape.append(jax.ShapeDtypeStruct(shape, arg.dtype))
        self._kinds.append(kind)
        return len(self.args) - 1

    def semaphores(self):
        n = len(self.args)
        return [pltpu.SemaphoreType.DMA((n, N_DEV - 1)), pltpu.SemaphoreType.DMA((n, N_DEV - 1)), pltpu.SemaphoreType.DMA((n,))]

    def to_sibling(self, array):
        shape = (N_DEV // 2, array.shape[0], array.shape[1] // N_DEV)
        return self._add(array, shape, ("to_sibling", True, False))

    def among_chips(self, stack):
        return self._add(stack, stack.shape, ("among_chips", False, False))

    def _copies(self, position, in_refs, out_refs):
        x, y, c, me = position
        for arr, ((kind, columns, via_sibling), src, out) in enumerate(zip(self._kinds, in_refs, out_refs)):
            if kind == "to_sibling":
                width = src.shape[-1] // N_DEV
                for k in range(N_DEV // 2):
                    block = src.at[:, pl.ds(pl.multiple_of((2 * k + 1 - c) * width, LANES), width)]
                    yield arr, k + 1, block, out.at[k], out.at[k], False, self.SIBLING
                continue
            for mask in range(N_DEV):
                _, peer_id = _peer(x, y, c, mask)
                relayed = via_sibling and mask not in (0, self.SIBLING) + self.OTHER_CHIPS
                if kind == "among_chips":
                    if mask in (0,) + self.OTHER_CHIPS:
                        yield arr, mask, src.at[peer_id // 2], out.at[me // 2], out.at[peer_id // 2], False, mask
                elif kind == "gather":
                    if columns:
                        width = src.shape[-1]
                        yield (arr, mask, src, out.at[:, pl.ds(pl.multiple_of(me * width, LANES), width)],
                               out.at[:, pl.ds(pl.multiple_of(peer_id * width, LANES), width)], relayed, mask)
                    else:
                        yield arr, mask, src, out.at[me], out.at[peer_id], relayed, mask
                else:
                    if columns:
                        width = src.shape[-1] // N_DEV
                        block = src.at[:, pl.ds(pl.multiple_of(peer_id * width, LANES), width)]
                    else:
                        block = src.at[peer_id]
                    yield arr, mask, block, out.at[me], out.at[peer_id], False, mask

    def _remote(self, position, sems, arr, slot, to_mask, src, dst):
        x, y, c, _ = position
        return pltpu.make_async_remote_copy(src_ref=src, dst_ref=dst, send_sem=sems[0].at[arr, slot - 1], recv_sem=sems[1].at[arr, slot - 1],
                                            device_id=_peer(x, y, c, to_mask)[0], device_id_type=MESH_ID)

    def start(self, position, in_refs, out_refs, sems):
        for arr, slot, src, dst, _, relayed, to_mask in self._copies(position, in_refs, out_refs):
            if slot == 0:
                pltpu.make_async_copy(src, dst, sems[2].at[arr]).start()
            elif not relayed:
                self._remote(position, sems, arr, slot, to_mask, src, dst).start()

    def wait(self, position, in_refs, out_refs, sems):
        copies = list(self._copies(position, in_refs, out_refs))
        landings = {(arr, slot): landing for arr, slot, _, _, landing, _, _ in copies}
        passed_on = set()
        for arr, mask, src, _, landing, relayed, _ in copies:
            if relayed:
                held = landings[arr, mask ^ self.SIBLING]
                self._remote(position, sems, arr, mask ^ self.SIBLING, mask ^ self.SIBLING, src, held).wait_recv()
                self._remote(position, sems, arr, mask, self.SIBLING, held, held).start()
                passed_on.add((arr, mask ^ self.SIBLING))
        for arr, slot, src, dst, landing, relayed, to_mask in copies:
            if slot == 0:
                pltpu.make_async_copy(src, dst, sems[2].at[arr]).wait()
                continue
            if (arr, slot) not in passed_on:
                self._remote(position, sems, arr, slot, to_mask, src, landing).wait_recv()
            if relayed:
                held = landings[arr, slot ^ self.SIBLING]
                self._remote(position, sems, arr, slot, self.SIBLING, held, held).wait_send()
            else:
                self._remote(position, sems, arr, slot, to_mask, src, dst).wait_send()


def _call(body, *, name, grid, in_specs, out_specs, out_shape, args, scratch_shapes=(), exchange=None):
    single = not isinstance(out_shape, (list, tuple))
    if single:
        out_specs, out_shape = [out_specs], [out_shape]
    params = _params(len(grid))
    if exchange is None:
        outs = pl.pallas_call(body, name=name, grid=grid, in_specs=in_specs, out_specs=out_specs, out_shape=out_shape,
                              scratch_shapes=list(scratch_shapes), compiler_params=params)(*args)
        return outs[0] if single else outs
    counts = (len(args), len(exchange.args), len(out_shape), len(exchange.out_shape), len(scratch_shapes), 3)

    def wrapped(*refs):
        groups, at = [], 0
        for n in counts:
            groups.append(refs[at:at + n])
            at += n
        main_in, ex_in, main_out, ex_out, main_scratch, sems = groups
        x, y, c = lax.axis_index("x"), lax.axis_index("y"), lax.axis_index("c")
        position = (x, y, c, 4 * x + 2 * y + c)
        ids = [pl.program_id(a) for a in range(len(grid))]
        first = functools.reduce(jnp.logical_and, [i == 0 for i in ids])
        last = functools.reduce(jnp.logical_and, [i == g - 1 for i, g in zip(ids, grid)])

        @pl.when(first)
        def _():
            exchange.start(position, ex_in, ex_out, sems)

        body(*main_in, *main_out, *main_scratch)

        @pl.when(last)
        def _():
            exchange.wait(position, ex_in, ex_out, sems)

    hbm = pl.BlockSpec(memory_space=pl.ANY)
    outs = pl.pallas_call(
        wrapped, name=name, grid=grid, in_specs=list(in_specs) + [hbm] * counts[1], out_specs=list(out_specs) + [hbm] * counts[3],
        out_shape=list(out_shape) + exchange.out_shape, scratch_shapes=list(scratch_shapes) + exchange.semaphores(),
        compiler_params=params)(*args, *exchange.args)
    main = outs[:counts[2]]
    return (main[0] if single else main), outs[counts[2]:]


def run_exchange(exchange, name):
    return _call(lambda: None, name=name, grid=(1,), in_specs=[], out_specs=[], out_shape=[], args=[], exchange=exchange)[1]


def gather_matmul(x, g, w_block, small_block, tm):
    s, d = x.shape
    width = w_block.shape[1]
    pair = 2 * width
    n_chips = N_DEV // 2
    tm = min(tm, s)
    n_i = s // tm
    sibling = Exchange.SIBLING
    y_nbr, x_nbr, diagonal = Exchange.OTHER_CHIPS

    def body(chips_ref, x_ref, g_ref, wb_ref, sb_ref, proj_ref, h_ref, w_ref, small_ref, h_all, w_pair, send, recv, local, load_sem):
        j, i = pl.program_id(0), pl.program_id(1)
        xx, yy, cc = lax.axis_index("x"), lax.axis_index("y"), lax.axis_index("c")
        me = 4 * xx + 2 * yy + cc

        def block_of(dev):
            return w_ref.at[:, pl.ds(pl.multiple_of(dev * width, LANES), width)]

        def half_of(dev, part):
            return w_ref.at[pl.ds(part * (d // 2), d // 2), pl.ds(pl.multiple_of(dev * width, LANES), width)]

        def remote(arr, slot, to_mask, src, dst):
            return pltpu.make_async_remote_copy(src_ref=src, dst_ref=dst, send_sem=send.at[arr, slot - 1], recv_sem=recv.at[arr, slot - 1],
                                                device_id=_peer(xx, yy, cc, to_mask)[0], device_id_type=MESH_ID)

        @pl.when((j == 0) & (i == 0))
        def _():
            pltpu.make_async_copy(wb_ref, block_of(me), local.at[0]).start()
            pltpu.make_async_copy(sb_ref, small_ref.at[me], local.at[1]).start()
            for mask in (sibling, y_nbr, x_nbr):
                remote(0, mask, mask, wb_ref, block_of(me)).start()
            for mask in range(1, N_DEV):
                remote(1, mask, mask, sb_ref, small_ref.at[me]).start()

        def to_sibling(mask):
            return remote(0, mask | sibling, sibling, block_of(me ^ mask), block_of(me ^ mask))

        def from_sibling(mask):
            return remote(0, mask | sibling, mask | sibling, wb_ref, block_of(me ^ (mask | sibling)))

        for step in range(n_chips):
            @pl.when((j == step) & (i == 0))
            def _(step=step):
                if step == 0:
                    pltpu.make_async_copy(wb_ref, block_of(me), local.at[0]).wait()
                    remote(0, sibling, sibling, wb_ref, block_of(me ^ sibling)).wait_recv()
                elif step == 1:
                    remote(0, y_nbr, y_nbr, wb_ref, block_of(me ^ y_nbr)).wait_recv()
                    remote(0, x_nbr, x_nbr, wb_ref, block_of(me ^ x_nbr)).wait_recv()
                    remote(0, diagonal, x_nbr, half_of(me ^ y_nbr, 0), half_of(me ^ y_nbr, 0)).start()
                    remote(2, diagonal, y_nbr, half_of(me ^ x_nbr, 1), half_of(me ^ x_nbr, 1)).start()
                    to_sibling(y_nbr).start()
                    to_sibling(x_nbr).start()
                    from_sibling(y_nbr).wait_recv()
                elif step == 2:
                    from_sibling(x_nbr).wait_recv()
                else:
                    remote(0, diagonal, x_nbr, wb_ref.at[pl.ds(0, d // 2), :], half_of(me ^ diagonal, 0)).wait_recv()
                    remote(2, diagonal, y_nbr, wb_ref.at[pl.ds(0, d // 2), :], half_of(me ^ diagonal, 1)).wait_recv()
                    to_sibling(diagonal).start()
                    from_sibling(diagonal).wait_recv()
                load = pltpu.make_async_copy(w_ref.at[:, pl.ds(pl.multiple_of(chips_ref[step] * pair, LANES), pair)], w_pair, load_sem)
                load.start()
                load.wait()

        rows = pl.ds(pl.multiple_of(i * tm, tm), tm)

        @pl.when(j == 0)
        def _():
            h = _rms(x_ref[...], g_ref[...]).astype(BF16)
            h_all[rows, :] = h
            h_ref[...] = h

        proj_ref[...] = jnp.dot(h_all[rows, :], w_pair[...], preferred_element_type=F32)

        @pl.when((j == n_chips - 1) & (i == n_i - 1))
        def _():
            pltpu.make_async_copy(sb_ref, small_ref.at[me], local.at[1]).wait()
            for mask in range(1, N_DEV):
                remote(1, mask, mask, sb_ref, small_ref.at[me ^ mask]).wait_recv()
                remote(1, mask, mask, sb_ref, small_ref.at[me]).wait_send()
            for mask in (sibling, y_nbr, x_nbr):
                remote(0, mask, mask, wb_ref, block_of(me)).wait_send()
            remote(0, diagonal, x_nbr, half_of(me ^ y_nbr, 0), half_of(me ^ y_nbr, 0)).wait_send()
            remote(2, diagonal, y_nbr, half_of(me ^ x_nbr, 1), half_of(me ^ x_nbr, 1)).wait_send()
            for mask in Exchange.OTHER_CHIPS:
                to_sibling(mask).wait_send()

    def first_pass_row(j, i, chips):
        return jnp.where(j == 0, i, n_i - 1), 0

    hbm = pl.BlockSpec(memory_space=pl.ANY)
    my_chip = 2 * lax.axis_index("x") + lax.axis_index("y")
    chips = (my_chip ^ jnp.arange(n_chips)).astype(jnp.int32)
    grid_spec = pltpu.PrefetchScalarGridSpec(
        num_scalar_prefetch=1, grid=(n_chips, n_i),
        in_specs=[pl.BlockSpec((tm, d), first_pass_row), pl.BlockSpec((1, d), lambda j, i, chips: (0, 0)), hbm, hbm],
        out_specs=[pl.BlockSpec((tm, pair), lambda j, i, chips: (i, chips[j])), pl.BlockSpec((tm, d), first_pass_row), hbm, hbm],
        scratch_shapes=[pltpu.VMEM((s, d), BF16), pltpu.VMEM((d, pair), BF16), pltpu.SemaphoreType.DMA((3, N_DEV - 1)),
                        pltpu.SemaphoreType.DMA((3, N_DEV - 1)), pltpu.SemaphoreType.DMA((2,)), pltpu.SemaphoreType.DMA(())])
    return pl.pallas_call(
        body, name="even_in", grid_spec=grid_spec,
        out_shape=[jax.ShapeDtypeStruct((s, N_DEV * width), F32), jax.ShapeDtypeStruct((s, d), BF16),
                   jax.ShapeDtypeStruct((d, N_DEV * width), w_block.dtype), jax.ShapeDtypeStruct((N_DEV,) + small_block.shape, small_block.dtype)],
        compiler_params=_params(2),
    )(chips, x, g, w_block, small_block)


def rms_matmul(x, g, w, tm, tn, name, exchange=None):
    s, d = x.shape
    n = w.shape[1]
    tm = min(tm, s)

    def body(x_ref, g_ref, w_ref, o_ref, h_ref):
        @pl.when(pl.program_id(1) == 0)
        def _():
            h_ref[...] = _rms(x_ref[...], g_ref[...]).astype(BF16)

        o_ref[...] = jnp.dot(h_ref[...], w_ref[...], preferred_element_type=F32)

    return _call(
        body, name=name, grid=(s // tm, n // tn),
        in_specs=[pl.BlockSpec((tm, d), lambda i, j: (i, 0)), _full((1, d)), pl.BlockSpec((d, tn), lambda i, j: (0, j))],
        out_specs=[pl.BlockSpec((tm, tn), lambda i, j: (i, j)), pl.BlockSpec((tm, d), lambda i, j: (i, 0))],
        out_shape=[jax.ShapeDtypeStruct((s, n), F32), jax.ShapeDtypeStruct((s, d), BF16)],
        args=[x, g, w], exchange=exchange)


def _gla_out_bwd(du, r, osum, gn, do_ref, dr_ref, dgn_ref):
    silu_r, dsilu_r = _silu_and_grad(r)
    for head in range(GLA_HEADS):
        vl = slice(head * GLA_DV, (head + 1) * GLA_DV)
        o_h, g_h, du_h = osum[:, vl], gn[:, vl], du[:, vl]
        dr_ref[:, vl] = (du_h * _rms(o_h, g_h) * dsilu_r[:, vl]).astype(BF16)
        do_h, dg_h = _rms_bwd(o_h, g_h, du_h * silu_r[:, vl])
        do_ref[:, vl] = do_h
        dgn_ref[...] += dg_h


def normbwd_matmul_nt(y, g, dout, w, tn, name, exchange=None, gla=None):
    s, d = y.shape
    n = w.shape[0]
    tm = min(MM_TILE, s)

    def body(*refs):
        if gla is None:
            y_ref, g_ref, dout_ref, w_ref, du_ref, dy_ref, dg_ref = refs
        else:
            y_ref, g_ref, dout_ref, w_ref, r_ref, o_ref, gn_ref, do_ref, dr_ref, dy_ref, dg_ref, dgn_ref = refs
        i, j = pl.program_id(0), pl.program_id(1)

        @pl.when(j == 0)
        def _():
            dy, dg = _rms_bwd(y_ref[...], g_ref[...], dout_ref[...])
            dy_ref[...] = dy.astype(BF16)

            @pl.when(i == 0)
            def _():
                dg_ref[...] = jnp.zeros_like(dg_ref)
                if gla is not None:
                    dgn_ref[...] = jnp.zeros_like(dgn_ref)

            dg_ref[...] += dg

        du = lax.dot_general(dy_ref[...], w_ref[...], (((1,), (1,)), ((), ())), preferred_element_type=F32)
        if gla is None:
            du_ref[...] = du
        else:
            _gla_out_bwd(du, r_ref[...], o_ref[...], gn_ref[...], do_ref, dr_ref, dgn_ref)

    row = pl.BlockSpec((tm, d), lambda i, j: (i, 0))
    in_specs = [row, _full((1, d)), row, pl.BlockSpec((tn, d), lambda i, j: (j, 0))]
    args = [y, g, dout, w]
    tail_specs = [row, _full((1, d))]
    tail_shapes = [jax.ShapeDtypeStruct((s, d), BF16), jax.ShapeDtypeStruct((1, d), F32)]
    if gla is None:
        out_specs = [pl.BlockSpec((tm, tn), lambda i, j: (i, j))] + tail_specs
        out_shape = [jax.ShapeDtypeStruct((s, n), F32)] + tail_shapes
    else:
        proj, osum, gnorm = gla
        assert n == tn == D_MODEL
        in_specs += [pl.BlockSpec((tm, D_MODEL), lambda i, j: (i, 2)), row, _full(gnorm.shape)]
        args += [proj, osum, gnorm]
        out_specs = [row, row] + tail_specs + [_full((1, GLA_DV))]
        out_shape = [jax.ShapeDtypeStruct((s, D_MODEL), F32), jax.ShapeDtypeStruct((s, D_MODEL), BF16)] + tail_shapes + [
            jax.ShapeDtypeStruct((1, GLA_DV), F32)]
    return _call(body, name=name, grid=(s // tm, n // tn), in_specs=in_specs, out_specs=out_specs, out_shape=out_shape,
                 args=args, exchange=exchange)


def matmul_tn(a, b, tm, tn, ts, out_dtype, name, exchange=None, b_first=None):
    s, m = a.shape
    n = b.shape[1] + (0 if b_first is None else tn)
    ts = min(ts, s)
    n_k = s // ts
    dims = (((0,), (0,)), ((), ()))

    def body(*refs):
        if b_first is None:
            a_ref, b_ref, o_ref, acc = refs
        else:
            a_ref, first_ref, b_ref, o_ref, acc = refs
        j, k = pl.program_id(1), pl.program_id(2)

        @pl.when(k == 0)
        def _():
            acc[...] = jnp.zeros_like(acc)

        if b_first is None:
            acc[...] += lax.dot_general(a_ref[...], b_ref[...], dims, preferred_element_type=F32)
        else:
            @pl.when(j == 0)
            def _():
                acc[...] += lax.dot_general(a_ref[...], first_ref[...], dims, preferred_element_type=F32)

            @pl.when(j > 0)
            def _():
                acc[...] += lax.dot_general(a_ref[...], b_ref[...], dims, preferred_element_type=F32)

        @pl.when(k == n_k - 1)
        def _():
            o_ref[...] = acc[...].astype(out_dtype)

    if b_first is None:
        b_specs, b_args = [pl.BlockSpec((ts, tn), lambda i, j, k: (k, j))], [b]
    else:
        b_specs = [pl.BlockSpec((ts, tn), lambda i, j, k: (k, 0)), pl.BlockSpec((ts, tn), lambda i, j, k: (k, jnp.maximum(j - 1, 0)))]
        b_args = [b_first, b]
    return _call(
        body, name=name, grid=(m // tm, n // tn, n_k),
        in_specs=[pl.BlockSpec((ts, tm), lambda i, j, k: (k, i))] + b_specs,
        out_specs=pl.BlockSpec((tm, tn), lambda i, j, k: (i, j)),
        out_shape=jax.ShapeDtypeStruct((m, n), out_dtype),
        scratch_shapes=[pltpu.VMEM((tm, tn), F32)], args=[a] + b_args, exchange=exchange)


def matmul_nt_normbwd(dproj, w, x, g, dres, tm, tk, name, exchange=None, first=None):
    s, kt = dproj.shape
    kt += 0 if first is None else tk
    d = w.shape[0]
    tm = min(tm, s)
    n_k = kt // tk
    dims = (((1,), (1,)), ((), ()))

    def body(*refs):
        if first is None:
            a_ref, w_ref, x_ref, g_ref, r_ref, dx_ref, dg_ref, acc = refs
        else:
            first_ref, a_ref, w_ref, x_ref, g_ref, r_ref, dx_ref, dg_ref, acc = refs
        i, k = pl.program_id(0), pl.program_id(1)

        @pl.when(k == 0)
        def _():
            acc[...] = jnp.zeros_like(acc)

        if first is None:
            acc[...] += lax.dot_general(a_ref[...], w_ref[...], dims, preferred_element_type=F32)
        else:
            @pl.when(k == 0)
            def _():
                acc[...] += lax.dot_general(first_ref[...], w_ref[...], dims, preferred_element_type=F32)

            @pl.when(k > 0)
            def _():
                acc[...] += lax.dot_general(a_ref[...], w_ref[...], dims, preferred_element_type=F32)

        @pl.when(k == n_k - 1)
        def _():
            dx, dg = _rms_bwd(x_ref[...], g_ref[...], acc[...])
            dx_ref[...] = r_ref[...] + dx

            @pl.when(i == 0)
            def _():
                dg_ref[...] = jnp.zeros_like(dg_ref)

            dg_ref[...] += dg

    row = pl.BlockSpec((tm, d), lambda i, k: (i, 0))
    if first is None:
        a_specs, a_args = [pl.BlockSpec((tm, tk), lambda i, k: (i, k))], [dproj]
    else:
        a_specs = [pl.BlockSpec((tm, tk), lambda i, k: (i, 0)), pl.BlockSpec((tm, tk), lambda i, k: (i, jnp.maximum(k - 1, 0)))]
        a_args = [first, dproj]
    return _call(
        body, name=name, grid=(s // tm, n_k),
        in_specs=a_specs + [pl.BlockSpec((d, tk), lambda i, k: (0, k)), row, _full((1, d)), row],
        out_specs=[row, _full((1, d))],
        out_shape=[jax.ShapeDtypeStruct((s, d), F32), jax.ShapeDtypeStruct((1, d), F32)],
        scratch_shapes=[pltpu.VMEM((tm, d), F32)], args=a_args + [w, x, g, dres], exchange=exchange)


def _rg_conv(xa, before, after, cw, cb):
    return (cw[0:1, :] * _shift_rows(xa, before, after, -2) + cw[1:2, :] * _shift_rows(xa, before, after, -1)
            + cw[2:3, :] * xa + cw[3:4, :] * _shift_rows(xa, before, after, 1) + cb)


def _rg_gates(ua_h, gw_ref, gb_ref, c_h, direction, head):
    r = _sigmoid(_bdot(ua_h, gw_ref[2 * direction, head]) + gb_ref[2 * direction, head:head + 1, :])
    i = _sigmoid(_bdot(ua_h, gw_ref[2 * direction + 1, head]) + gb_ref[2 * direction + 1, head:head + 1, :])
    log_a = -c_h * r
    a = jnp.exp(log_a)
    beta_sq = -jnp.tanh(log_a) * (1.0 + a * a)
    inv_beta = lax.rsqrt(jnp.maximum(beta_sq, SMALLEST_NORMAL))
    return r, i, a, beta_sq * inv_beta, inv_beta


def even_gates_fwd(proj, conv_w, conv_b, gate_w, gate_b, lam, exchange=None):
    s = proj.shape[0]
    ts = min(2 * ROW_TILE, s)
    n_tiles = s // ts

    def body(xa_ref, xb_ref, xn_ref, cw_ref, cb_ref, gw_ref, gb_ref, lam_ref, o_ref, hf_ref, carry):
        @pl.when(pl.program_id(0) == 0)
        def _():
            carry[...] = jnp.zeros_like(carry)

        xa, before, after = _halo_load(xa_ref, xb_ref, xn_ref, n_tiles)
        ua = _rg_conv(xa, before, after, cw_ref[...], cb_ref[...])
        c = RG_C * _softplus(-lam_ref[...])
        for direction in range(2):
            for head in range(RG_HEADS):
                lanes = slice(head * RG_HEAD_DIM, (head + 1) * RG_HEAD_DIM)
                ua_h = ua[:, lanes]
                _, i, a, beta, _ = _rg_gates(ua_h, gw_ref, gb_ref, c[direction:direction + 1, lanes], direction, head)
                o_ref[2 * direction, :, lanes] = a
                o_ref[2 * direction + 1, :, lanes] = beta * (i * ua_h)
        _scan_tile(o_ref.at[0], o_ref.at[1], hf_ref, carry, False, False)

    return _call(
        body, name="even_gates_fwd", grid=(n_tiles,),
        in_specs=_halo_specs(ts, s, D_MODEL, 0) + [_full(conv_w.shape), _full(conv_b.shape), _full(gate_w.shape),
                                                   _full(gate_b.shape), _full(lam.shape)],
        out_specs=[pl.BlockSpec((4, ts, D_MODEL), lambda i: (0, i, 0)), pl.BlockSpec((ts, D_MODEL), lambda i: (i, 0))],
        out_shape=[jax.ShapeDtypeStruct((4, s, D_MODEL), F32), jax.ShapeDtypeStruct((s, D_MODEL), F32)],
        scratch_shapes=[pltpu.VMEM((SUBLANES, D_MODEL), F32)],
        args=[proj, proj, proj, conv_w, conv_b, gate_w, gate_b, lam], exchange=exchange)


def _scan_tile(a_ref, b_ref, h_ref, carry, reverse, b_times_a):
    ts, c = h_ref.shape
    n_blocks = ts // SUBLANES
    row = lax.broadcasted_iota(jnp.int32, (SUBLANES, c), 0)

    def block(j, h_in):
        r0 = pl.multiple_of((n_blocks - 1 - j if reverse else j) * SUBLANES, SUBLANES)
        a = a_ref[pl.ds(r0, SUBLANES), :]
        b = b_ref[pl.ds(r0, SUBLANES), :]
        if b_times_a:
            b = a * b
        for step in (1, 2, 4):
            shift = SUBLANES - step if reverse else step
            valid = row < SUBLANES - step if reverse else row >= step
            b = jnp.where(valid, a * pltpu.roll(b, shift, 0) + b, b)
            a = jnp.where(valid, a * pltpu.roll(a, shift, 0), a)
        h = a * h_in + b
        h_ref[pl.ds(r0, SUBLANES), :] = h
        return h[0:1, :] if reverse else h[SUBLANES - 1:SUBLANES, :]

    carry[0:1, :] = lax.fori_loop(0, n_blocks, block, carry[0:1, :])


def linear_scan(a_arr, a_idx, b_arr, b_idx, reverse, b_times_a, name, exchange=None):
    _, s, c = a_arr.shape
    ts = min(MM_TILE, s)
    n_tiles = s // ts

    def tile_of(i):
        return n_tiles - 1 - i if reverse else i

    def body(a_ref, b_ref, h_ref, carry):
        @pl.when(pl.program_id(0) == 0)
        def _():
            carry[...] = jnp.zeros_like(carry)

        _scan_tile(a_ref, b_ref, h_ref, carry, reverse, b_times_a)

    return _call(
        body, name=name, grid=(n_tiles,),
        in_specs=[pl.BlockSpec((None, ts, c), lambda i: (a_idx, tile_of(i), 0)),
                  pl.BlockSpec((None, ts, c), lambda i: (b_idx, tile_of(i), 0))],
        out_specs=pl.BlockSpec((ts, c), lambda i: (tile_of(i), 0)),
        out_shape=jax.ShapeDtypeStruct((s, c), F32),
        scratch_shapes=[pltpu.VMEM((SUBLANES, c), F32)], args=[a_arr, b_arr], exchange=exchange)


def _sc_conv(p, before, after, w):
    return w[0:1, :] * _shift_rows(p, before, after, -1) + w[1:2, :] * p + w[2:3, :] * _shift_rows(p, before, after, 1)


def even_mix_fwd(ab, hf, proj, sc_w, w_out, xres, g_post, exchange=None):
    s = proj.shape[0]
    ts = min(ROW_TILE, s)
    n_tiles = s // ts

    def tile(i):
        return n_tiles - 1 - i

    row = pl.BlockSpec((ts, D_MODEL), lambda i: (tile(i), 0))

    def col(c):
        return pl.BlockSpec((ts, D_MODEL), lambda i: (tile(i), c))

    def body(a_ref, b_ref, hf_ref, za_ref, xb_ref, xbb_ref, xbn_ref, gb_ref, gc_ref, gcb_ref, gcn_ref, zb_ref, w_ref,
             wo_ref, x_ref, g_ref, u_ref, hb_ref, y_ref, out_ref, carry):
        @pl.when(pl.program_id(0) == 0)
        def _():
            carry[...] = jnp.zeros_like(carry)

        _scan_tile(a_ref, b_ref, hb_ref, carry, True, False)
        xb, xb_before, xb_after = _halo_load(xb_ref, xbb_ref, xbn_ref, n_tiles, tile)
        gc, gc_before, gc_after = _halo_load(gc_ref, gcb_ref, gcn_ref, n_tiles, tile)
        silu_za, _ = _silu_and_grad(za_ref[...])
        silu_zb, _ = _silu_and_grad(zb_ref[...])
        u_ref[:, :D_MODEL] = ((hf_ref[...] + hb_ref[...]) * silu_za).astype(BF16)
        cv = _sc_conv(gc * xb, gc_before * xb_before, gc_after * xb_after, w_ref[...])
        u_ref[:, D_MODEL:] = (gb_ref[...] * cv * silu_zb).astype(BF16)
        y = jnp.dot(u_ref[...], wo_ref[...], preferred_element_type=F32)
        y_ref[...] = y
        out_ref[...] = x_ref[...] + _rms(y, g_ref[...])

    return _call(
        body, name="even_mix_fwd", grid=(n_tiles,),
        in_specs=[pl.BlockSpec((None, ts, D_MODEL), lambda i: (2, tile(i), 0)), pl.BlockSpec((None, ts, D_MODEL), lambda i: (3, tile(i), 0)),
                  row, col(1)] + _halo_specs(ts, s, D_MODEL, 2, tile) + [col(3)] + _halo_specs(ts, s, D_MODEL, 4, tile)
        + [col(5), _full(sc_w.shape), _full(w_out.shape), row, _full(g_post.shape)],
        out_specs=[pl.BlockSpec((ts, 2 * D_MODEL), lambda i: (tile(i), 0)), row, row, row],
        out_shape=[jax.ShapeDtypeStruct((s, 2 * D_MODEL), BF16)] + [jax.ShapeDtypeStruct((s, D_MODEL), F32)] * 3,
        scratch_shapes=[pltpu.VMEM((SUBLANES, D_MODEL), F32)],
        args=[ab, ab, hf, proj, proj, proj, proj, proj, proj, proj, proj, proj, sc_w, w_out, xres, g_post], exchange=exchange)


def even_mix_bwd(du, hf, hb, proj, sc_w, ab, exchange=None):
    s = proj.shape[0]
    ts = min(ROW_TILE, s)
    n_tiles = s // ts
    row = pl.BlockSpec((ts, D_MODEL), lambda i: (i, 0))

    def body(dya_ref, dyb_ref, dybb_ref, dybn_ref, hf_ref, hb_ref, za_ref, xb_ref, xbb_ref, xbn_ref,
             gb_ref, gbb_ref, gbn_ref, gc_ref, gcb_ref, gcn_ref, zb_ref, zbb_ref, zbn_ref, w_ref, a_ref,
             dh_ref, dp_ref, dw_ref, adj_ref, carry):
        @pl.when(pl.program_id(0) == 0)
        def _():
            carry[...] = jnp.zeros_like(carry)

        dyb, dyb_before, dyb_after = _halo_load(dyb_ref, dybb_ref, dybn_ref, n_tiles)
        xb, xb_before, xb_after = _halo_load(xb_ref, xbb_ref, xbn_ref, n_tiles)
        gb, gb_before, gb_after = _halo_load(gb_ref, gbb_ref, gbn_ref, n_tiles)
        gc, gc_before, gc_after = _halo_load(gc_ref, gcb_ref, gcn_ref, n_tiles)
        zb, zb_before, zb_after = _halo_load(zb_ref, zbb_ref, zbn_ref, n_tiles)
        w = w_ref[...]
        dya, za = dya_ref[...], za_ref[...]
        silu_za, dsilu_za = _silu_and_grad(za)
        dh_ref[...] = dya * silu_za
        _scan_tile(a_ref, dh_ref, adj_ref, carry, False, True)
        dp_ref[:, 0:D_MODEL] = (dya * (hf_ref[...] + hb_ref[...]) * dsilu_za).astype(BF16)

        silu_zb, dsilu_zb = _silu_and_grad(zb)
        p, p_before, p_after = gc * xb, gc_before * xb_before, gc_after * xb_after
        cv = _sc_conv(p, p_before, p_after, w)
        dcv = dyb * gb * silu_zb
        dcv_before = dyb_before * gb_before * _silu_and_grad(zb_before)[0]
        dcv_after = dyb_after * gb_after * _silu_and_grad(zb_after)[0]
        dpp = (w[0:1, :] * _shift_rows(dcv, dcv_before, dcv_after, 1) + w[1:2, :] * dcv
               + w[2:3, :] * _shift_rows(dcv, dcv_before, dcv_after, -1))
        dp_ref[:, D_MODEL:2 * D_MODEL] = (dpp * gc).astype(BF16)
        dp_ref[:, 2 * D_MODEL:3 * D_MODEL] = (dyb * cv * silu_zb).astype(BF16)
        dp_ref[:, 3 * D_MODEL:4 * D_MODEL] = (dpp * xb).astype(BF16)
        dp_ref[:, 4 * D_MODEL:5 * D_MODEL] = (dyb * gb * cv * dsilu_zb).astype(BF16)

        @pl.when(pl.program_id(0) == 0)
        def _():
            dw_ref[...] = jnp.zeros_like(dw_ref)

        dw_ref[0:1, :] += jnp.sum(dcv * _shift_rows(p, p_before, p_after, -1), axis=0, keepdims=True)
        dw_ref[1:2, :] += jnp.sum(dcv * p, axis=0, keepdims=True)
        dw_ref[2:3, :] += jnp.sum(dcv * _shift_rows(p, p_before, p_after, 1), axis=0, keepdims=True)

    return _call(
        body, name="even_mix_bwd", grid=(n_tiles,),
        in_specs=[row] + _halo_specs(ts, s, D_MODEL, 1) + [row, row, pl.BlockSpec((ts, D_MODEL), lambda i: (i, 1))]
        + _halo_specs(ts, s, D_MODEL, 2) + _halo_specs(ts, s, D_MODEL, 3) + _halo_specs(ts, s, D_MODEL, 4)
        + _halo_specs(ts, s, D_MODEL, 5) + [_full(sc_w.shape), pl.BlockSpec((None, ts, D_MODEL), lambda i: (2, i, 0))],
        out_specs=[row, pl.BlockSpec((ts, 5 * D_MODEL), lambda i: (i, 0)), _full(sc_w.shape), row],
        out_shape=[jax.ShapeDtypeStruct((s, D_MODEL), F32), jax.ShapeDtypeStruct((s, 5 * D_MODEL), BF16),
                   jax.ShapeDtypeStruct(sc_w.shape, F32), jax.ShapeDtypeStruct((s, D_MODEL), F32)],
        scratch_shapes=[pltpu.VMEM((SUBLANES, D_MODEL), F32)],
        args=[du, du, du, du, hf, hb, proj, *([proj] * 12), sc_w, ab], exchange=exchange)


def even_gates_bwd(proj, adj_f, adj_b, hf, hb, dh, conv_w, conv_b, gate_w, gate_b, lam, exchange=None):
    s = proj.shape[0]
    ts = min(2 * ROW_TILE, s)
    n_tiles = s // ts
    row = pl.BlockSpec((ts, D_MODEL), lambda i: (i, 0))

    def body(xa_ref, xab_ref, xan_ref, af_ref, afb_ref, afn_ref, ab_ref, abb_ref, abn_ref,
             hf_ref, hfb_ref, hfn_ref, hb_ref, hbb_ref, hbn_ref, dh_ref,
             cw_ref, cb_ref, gw_ref, gb_ref, lam_ref, dua_ref, dgw_ref, dgb_ref, dlam_ref):
        @pl.when(pl.program_id(0) == 0)
        def _():
            dgw_ref[...] = jnp.zeros_like(dgw_ref)
            dgb_ref[...] = jnp.zeros_like(dgb_ref)
            dlam_ref[...] = jnp.zeros_like(dlam_ref)

        xa, before, after = _halo_load(xa_ref, xab_ref, xan_ref, n_tiles)
        ua = _rg_conv(xa, before, after, cw_ref[...], cb_ref[...])
        lam_v = lam_ref[...]
        c = RG_C * _softplus(-lam_v)
        dc_dlam = -RG_C * _sigmoid(-lam_v)
        dh = dh_ref[...]
        adj = (_halo_load(af_ref, afb_ref, afn_ref, n_tiles), _halo_load(ab_ref, abb_ref, abn_ref, n_tiles))
        hs = (_halo_load(hf_ref, hfb_ref, hfn_ref, n_tiles), _halo_load(hb_ref, hbb_ref, hbn_ref, n_tiles))
        dua = jnp.zeros_like(ua)
        for direction in range(2):
            step = 1 if direction == 0 else -1
            g = dh + _shift_rows(*adj[direction], step)
            da_all = g * _shift_rows(*hs[direction], -step)
            dua_parts = []
            for head in range(RG_HEADS):
                lanes = slice(head * RG_HEAD_DIM, (head + 1) * RG_HEAD_DIM)
                ua_h = ua[:, lanes]
                c_h = c[direction:direction + 1, lanes]
                r, i, a, beta, inv_beta = _rg_gates(ua_h, gw_ref, gb_ref, c_h, direction, head)
                db = g[:, lanes]
                d_i = db * beta * ua_h
                dbeta = db * (i * ua_h)
                dlog_a = (da_all[:, lanes] - dbeta * a * inv_beta) * a
                dpr = -c_h * dlog_a * r * (1.0 - r)
                dpi = d_i * i * (1.0 - i)
                dua_parts.append(db * beta * i + _bdot_nt(dpr, gw_ref[2 * direction, head])
                                 + _bdot_nt(dpi, gw_ref[2 * direction + 1, head]))
                dgw_ref[2 * direction, head] += _bdot_tn(ua_h, dpr)
                dgw_ref[2 * direction + 1, head] += _bdot_tn(ua_h, dpi)
                dgb_ref[2 * direction, head:head + 1, :] += jnp.sum(dpr, axis=0, keepdims=True)
                dgb_ref[2 * direction + 1, head:head + 1, :] += jnp.sum(dpi, axis=0, keepdims=True)
                dlam_ref[direction:direction + 1, lanes] += (
                    jnp.sum(-r * dlog_a, axis=0, keepdims=True) * dc_dlam[direction:direction + 1, lanes])
            dua = dua + jnp.concatenate(dua_parts, axis=1)
        dua_ref[...] = dua

    return _call(
        body, name="even_gates_bwd", grid=(n_tiles,),
        in_specs=_halo_specs(ts, s, D_MODEL, 0) * 5 + [row] + [_full(conv_w.shape), _full(conv_b.shape), _full(gate_w.shape),
                                                             _full(gate_b.shape), _full(lam.shape)],
        out_specs=[row, _full(gate_w.shape), _full(gate_b.shape), _full(lam.shape)],
        out_shape=[jax.ShapeDtypeStruct((s, D_MODEL), F32), jax.ShapeDtypeStruct(gate_w.shape, F32),
                   jax.ShapeDtypeStruct(gate_b.shape, F32), jax.ShapeDtypeStruct(lam.shape, F32)],
        args=[proj, proj, proj, adj_f, adj_f, adj_f, adj_b, adj_b, adj_b, hf, hf, hf, hb, hb, hb, dh, conv_w, conv_b, gate_w,
              gate_b, lam], exchange=exchange)


def rg_conv_bwd(dua, proj, conv_w, exchange=None):
    s = proj.shape[0]
    ts = min(2 * ROW_TILE, s)
    n_tiles = s // ts

    def body(du_ref, dub_ref, dun_ref, xa_ref, xab_ref, xan_ref, cw_ref, dp_ref, dw_ref, db_ref):
        @pl.when(pl.program_id(0) == 0)
        def _():
            dw_ref[...] = jnp.zeros_like(dw_ref)
            db_ref[...] = jnp.zeros_like(db_ref)

        dua, dua_before, dua_after = _halo_load(du_ref, dub_ref, dun_ref, n_tiles)
        xa, xa_before, xa_after = _halo_load(xa_ref, xab_ref, xan_ref, n_tiles)
        cw = cw_ref[...]
        dxa = (cw[0:1, :] * _shift_rows(dua, dua_before, dua_after, 2) + cw[1:2, :] * _shift_rows(dua, dua_before, dua_after, 1)
               + cw[2:3, :] * dua + cw[3:4, :] * _shift_rows(dua, dua_before, dua_after, -1))
        dp_ref[...] = dxa.astype(BF16)
        for tap, offset in enumerate((-2, -1, 0, 1)):
            shifted = xa if offset == 0 else _shift_rows(xa, xa_before, xa_after, offset)
            dw_ref[tap:tap + 1, :] += jnp.sum(dua * shifted, axis=0, keepdims=True)
        db_ref[...] += jnp.sum(dua, axis=0, keepdims=True)

    return _call(
        body, name="rg_conv_bwd", grid=(n_tiles,),
        in_specs=_halo_specs(ts, s, D_MODEL, 0) * 2 + [_full(conv_w.shape)],
        out_specs=[pl.BlockSpec((ts, D_MODEL), lambda i: (i, 0)), _full(conv_w.shape), _full((1, D_MODEL))],
        out_shape=[jax.ShapeDtypeStruct((s, D_MODEL), BF16), jax.ShapeDtypeStruct(conv_w.shape, F32),
                   jax.ShapeDtypeStruct((1, D_MODEL), F32)],
        args=[dua, dua, dua, proj, proj, proj, conv_w], exchange=exchange)


def _split3(x):
    x1 = x.astype(BF16)
    rest = x - x1.astype(F32)
    x2 = rest.astype(BF16)
    return x1, x2, (rest - x2.astype(F32)).astype(BF16)


def _chunk_sum_matrix(t, reverse, transpose):
    i = lax.broadcasted_iota(jnp.int32, (t, t), 0)
    j = lax.broadcasted_iota(jnp.int32, (t, t), 1)
    if transpose:
        i, j = j, i
    same = (i // GLA_CHUNK) == (j // GLA_CHUNK)
    return jnp.where(same & ((j >= i) if reverse else (j <= i)), 1.0, 0.0).astype(BF16)


def _exact_dot(m, x):
    return sum(jnp.dot(m, part, preferred_element_type=F32) for part in _split3(x))


def _chunk_mask(t, reverse):
    i = lax.broadcasted_iota(jnp.int32, (t, t), 0)
    j = lax.broadcasted_iota(jnp.int32, (t, t), 1)
    return ((i // GLA_CHUNK) == (j // GLA_CHUNK)) & ((j >= i) if reverse else (j <= i))


def _chunk_rows(c):
    return slice(c * GLA_CHUNK, (c + 1) * GLA_CHUNK)


def _gla_gate(lr, wg, bg):
    z = _bdot(lr, wg) + bg
    log_alpha = (jnp.minimum(z, 0.0) - jnp.log(1.0 + jnp.exp(-jnp.abs(z)))) * (1.0 / GLA_NORMALIZER)
    return z, log_alpha


def _gla_tile_terms(q, k, bcum, reverse):
    n_chunks = q.shape[0] // GLA_CHUNK
    totals = []
    for c in range(n_chunks):
        edge = c * GLA_CHUNK if reverse else (c + 1) * GLA_CHUNK - 1
        totals.append(bcum[edge:edge + 1, :])
    btot = jnp.concatenate([jnp.broadcast_to(total, (GLA_CHUNK, total.shape[1])) for total in totals], axis=0)
    e_pos, e_neg, e_st = jnp.exp(bcum), jnp.exp(-bcum), jnp.exp(btot - bcum)
    return q * (GLA_DK ** -0.5) * e_pos, k * e_neg, k * e_st, e_pos, e_neg, e_st, [jnp.exp(total) for total in totals]


def _gla_specs(t, n_tiles, reverse_order):
    def tile(i):
        return n_tiles - 1 - i if reverse_order else i

    return tile, [
        pl.BlockSpec((t, GLA_KEY), lambda i: (tile(i), 0)),
        pl.BlockSpec((t, GLA_KEY), lambda i: (tile(i), 1)),
        pl.BlockSpec((t, D_MODEL), lambda i: (tile(i), 1)),
        pl.BlockSpec((t, LANES), lambda i: (tile(i), (ODD_IN_PAD - LANES) // LANES)),
    ]


def gla_fwd(proj, wg, bg, reverse, o_other=None, gnorm=None, post=None):
    s = proj.shape[0]
    t = min(ROW_TILE, s)
    n_tiles = s // t
    n_chunks = t // GLA_CHUNK
    final = o_other is not None
    tile, specs = _gla_specs(t, n_tiles, reverse)

    def body(*refs):
        if final:
            (q_ref, k_ref, v_ref, lr_ref, wg_ref, bg_ref, oo_ref, r_ref, gn_ref, wo_ref, x_ref, gp_ref, t_ref,
             osum_ref, u_ref, st_ref, y_ref, dout_ref, loss_ref, state) = refs
        else:
            q_ref, k_ref, v_ref, lr_ref, wg_ref, bg_ref, o_ref, st_ref, state = refs
            osum_ref = o_ref

        @pl.when(pl.program_id(0) == 0)
        def _():
            state[...] = jnp.zeros_like(state)

        _, log_alpha = _gla_gate(lr_ref[...], wg_ref[...], bg_ref[...])
        bcum = _exact_dot(_chunk_sum_matrix(t, reverse, False), log_alpha)
        q, k, v = q_ref[...], k_ref[...], v_ref[...]
        q_in, k_in, k_st, _, _, _, decays = _gla_tile_terms(q, k, bcum, reverse)
        mask = _chunk_mask(t, reverse)
        order = list(range(n_chunks))[::-1] if reverse else list(range(n_chunks))
        intra, increments = [], []
        for head in range(GLA_HEADS):
            kl = slice(head * GLA_DK, (head + 1) * GLA_DK)
            vl = slice(head * GLA_DV, (head + 1) * GLA_DV)
            scores = jnp.where(mask, _bdot_nt(q_in[:, kl], k_in[:, kl]), 0.0)
            intra.append(_bdot(scores, v[:, vl]))
            increments.append([_bdot_tn(v[_chunk_rows(c), vl], k_st[_chunk_rows(c), kl]) for c in range(n_chunks)])
        for head in range(GLA_HEADS):
            kl = slice(head * GLA_DK, (head + 1) * GLA_DK)
            vl = slice(head * GLA_DV, (head + 1) * GLA_DV)
            running = state[head]
            before = [None] * n_chunks
            for c in order:
                before[c] = running
                st_ref[c, head] = running
                running = running * decays[c][:, kl] + increments[head][c]
            state[head] = running
            inter = [_bdot_nt(q_in[_chunk_rows(c), kl], before[c]) for c in range(n_chunks)]
            osum_ref[:, vl] = intra[head] + jnp.concatenate(inter, axis=0)
        if final:
            osum = osum_ref[...] + oo_ref[...]
            osum_ref[...] = osum
            silu_r, _ = _silu_and_grad(r_ref[...])
            gn = gn_ref[...]
            for head in range(GLA_HEADS):
                vl = slice(head * GLA_DV, (head + 1) * GLA_DV)
                u_ref[:, vl] = (_rms(osum[:, vl], gn[:, vl]) * silu_r[:, vl]).astype(BF16)

            @pl.when(pl.program_id(0) == 0)
            def _():
                loss_ref[...] = jnp.zeros_like(loss_ref)

            y = jnp.dot(u_ref[...], wo_ref[...], preferred_element_type=F32)
            y_ref[...] = y
            diff = x_ref[...] + _rms(y, gp_ref[...]) - t_ref[...]
            dout_ref[...] = diff * (1.0 / D_MODEL)
            loss_ref[...] += 0.5 * jnp.sum(jnp.mean(diff * diff, axis=-1, keepdims=True))

    row = pl.BlockSpec((t, D_MODEL), lambda i: (tile(i), 0))
    st_spec = pl.BlockSpec((n_chunks, GLA_HEADS, GLA_DV, GLA_DK), lambda i: (tile(i), 0, 0, 0))
    st_shape = jax.ShapeDtypeStruct((s // GLA_CHUNK, GLA_HEADS, GLA_DV, GLA_DK), F32)
    in_specs = specs + [_full(wg.shape), _full(bg.shape)]
    args = [proj, proj, proj, proj, wg, bg]
    if final:
        w_out, xres, g_post, target = post
        in_specs += [row, pl.BlockSpec((t, D_MODEL), lambda i: (tile(i), 2)), _full(gnorm.shape), _full(w_out.shape), row,
                     _full(g_post.shape), row]
        args += [o_other, proj, gnorm, w_out, xres, g_post, target]
        out_specs = [row, row, st_spec, row, row, _full((SUBLANES, LANES))]
        out_shape = [jax.ShapeDtypeStruct((s, D_MODEL), F32), jax.ShapeDtypeStruct((s, D_MODEL), BF16), st_shape,
                     jax.ShapeDtypeStruct((s, D_MODEL), F32), jax.ShapeDtypeStruct((s, D_MODEL), F32),
                     jax.ShapeDtypeStruct((SUBLANES, LANES), F32)]
    else:
        out_specs = [row, st_spec]
        out_shape = [jax.ShapeDtypeStruct((s, D_MODEL), F32), st_shape]
    return pl.pallas_call(
        body, name="gla_fwd_rev" if reverse else "gla_fwd", grid=(n_tiles,), in_specs=in_specs, out_specs=out_specs,
        out_shape=out_shape, scratch_shapes=[pltpu.VMEM((GLA_HEADS, GLA_DV, GLA_DK), F32)], compiler_params=_params(1),
    )(*args)


def gla_bwd(proj, wg, bg, do, states, reverse, first=None):
    s = proj.shape[0]
    t = min(ROW_TILE, s)
    n_tiles = s // t
    n_chunks = t // GLA_CHUNK
    final = first is not None
    tile, specs = _gla_specs(t, n_tiles, not reverse)

    def body(*refs):
        if final:
            (q_ref, k_ref, v_ref, lr_ref, wg_ref, bg_ref, do_ref, st_ref, dqkv1_ref, dlr1_ref, dr_ref,
             dp_ref, dwg_ref, dbg_ref, dstate, dqkv, dbc, dbt) = refs
        else:
            (q_ref, k_ref, v_ref, lr_ref, wg_ref, bg_ref, do_ref, st_ref,
             dqkv, dlr_ref, dwg_ref, dbg_ref, dstate, dbc, dbt) = refs

        @pl.when(pl.program_id(0) == 0)
        def _():
            dstate[...] = jnp.zeros_like(dstate)
            dwg_ref[...] = jnp.zeros_like(dwg_ref)
            dbg_ref[...] = jnp.zeros_like(dbg_ref)

        lr, wg_v = lr_ref[...], wg_ref[...]
        z, log_alpha = _gla_gate(lr, wg_v, bg_ref[...])
        bcum = _exact_dot(_chunk_sum_matrix(t, reverse, False), log_alpha)
        q, k, v, do_v = q_ref[...], k_ref[...], v_ref[...], do_ref[...]
        q_in, k_in, k_st, e_pos, e_neg, e_st, decays = _gla_tile_terms(q, k, bcum, reverse)
        mask = _chunk_mask(t, reverse)
        order = list(range(n_chunks)) if reverse else list(range(n_chunks))[::-1]
        dq_intra, dk_intra, dv_intra, increments = [], [], [], []
        for head in range(GLA_HEADS):
            kl = slice(head * GLA_DK, (head + 1) * GLA_DK)
            vl = slice(head * GLA_DV, (head + 1) * GLA_DV)
            scores = jnp.where(mask, _bdot_nt(q_in[:, kl], k_in[:, kl]), 0.0)
            dscores = jnp.where(mask, _bdot_nt(do_v[:, vl], v[:, vl]), 0.0)
            dv_intra.append(_bdot_tn(scores, do_v[:, vl]))
            dq_intra.append(_bdot(dscores, k_in[:, kl]))
            dk_intra.append(_bdot_tn(dscores, q_in[:, kl]))
            increments.append([_bdot_tn(do_v[_chunk_rows(c), vl], q_in[_chunk_rows(c), kl]) for c in range(n_chunks)])
        for head in range(GLA_HEADS):
            kl = slice(head * GLA_DK, (head + 1) * GLA_DK)
            vl = slice(head * GLA_DV, (head + 1) * GLA_DV)
            running = dstate[head]
            after, ddecay = [None] * n_chunks, [None] * n_chunks
            for c in order:
                after[c] = running
                ddecay[c] = jnp.sum(running * st_ref[c, head], axis=0, keepdims=True)
                running = running * decays[c][:, kl] + increments[head][c]
            dstate[head] = running
            dq_inter = jnp.concatenate([_bdot(do_v[_chunk_rows(c), vl], st_ref[c, head]) for c in range(n_chunks)], axis=0)
            dv_inter = jnp.concatenate([_bdot_nt(k_st[_chunk_rows(c), kl], after[c]) for c in range(n_chunks)], axis=0)
            dk_st = jnp.concatenate([_bdot(v[_chunk_rows(c), vl], after[c]) for c in range(n_chunks)], axis=0)
            dq_in = dq_intra[head] + dq_inter
            ks_h = k_st[:, kl]
            dqkv[:, 2 * GLA_KEY + head * GLA_DV:2 * GLA_KEY + (head + 1) * GLA_DV] = dv_intra[head] + dv_inter
            dqkv[:, kl] = dq_in * (GLA_DK ** -0.5) * e_pos[:, kl]
            dqkv[:, GLA_KEY + head * GLA_DK:GLA_KEY + (head + 1) * GLA_DK] = dk_intra[head] * e_neg[:, kl] + dk_st * e_st[:, kl]
            dbc[:, kl] = dq_in * q_in[:, kl] - dk_intra[head] * k_in[:, kl] - dk_st * ks_h
            weighted = dk_st * ks_h
            for c in range(n_chunks):
                dbtot = jnp.sum(weighted[_chunk_rows(c)], axis=0, keepdims=True) + ddecay[c] * decays[c][:, kl]
                dbt[_chunk_rows(c), kl] = jnp.broadcast_to(dbtot, (GLA_CHUNK, GLA_DK))
        dlog_alpha = _exact_dot(_chunk_sum_matrix(t, reverse, True), dbc[...]) + dbt[...]
        dz = dlog_alpha * _sigmoid(-z) * (1.0 / GLA_NORMALIZER)
        dlr = _bdot_nt(dz, wg_v)
        dwg_ref[...] += _bdot_tn(lr, dz)
        dbg_ref[...] += jnp.sum(dz, axis=0, keepdims=True)
        if final:
            dp_ref[:, :2 * D_MODEL] = (dqkv[...] + dqkv1_ref[...]).astype(BF16)
            dp_ref[:, 2 * D_MODEL:3 * D_MODEL] = dr_ref[...]
            dp_ref[:, 3 * D_MODEL:] = (dlr + dlr1_ref[...]).astype(BF16)
        else:
            dlr_ref[...] = dlr

    row = pl.BlockSpec((t, D_MODEL), lambda i: (tile(i), 0))
    wide = pl.BlockSpec((t, 2 * D_MODEL), lambda i: (tile(i), 0))
    narrow = pl.BlockSpec((t, LANES), lambda i: (tile(i), 0))
    st_spec = pl.BlockSpec((n_chunks, GLA_HEADS, GLA_DV, GLA_DK), lambda i: (tile(i), 0, 0, 0))
    in_specs = specs + [_full(wg.shape), _full(bg.shape), row, st_spec]
    args = [proj, proj, proj, proj, wg, bg, do, states]
    acc_specs = [_full(wg.shape), _full(bg.shape)]
    acc_shapes = [jax.ShapeDtypeStruct(wg.shape, F32), jax.ShapeDtypeStruct(bg.shape, F32)]
    scratch = [pltpu.VMEM((GLA_HEADS, GLA_DV, GLA_DK), F32)]
    work = [pltpu.VMEM((t, GLA_KEY), F32), pltpu.VMEM((t, GLA_KEY), F32)]
    if final:
        in_specs += [wide, narrow, row]
        args += list(first)
        out_specs = [pl.BlockSpec((t, ODD_IN_PAD), lambda i: (tile(i), 0))] + acc_specs
        out_shape = [jax.ShapeDtypeStruct((s, ODD_IN_PAD), BF16)] + acc_shapes
        scratch += [pltpu.VMEM((t, 2 * D_MODEL), F32)] + work
    else:
        out_specs = [wide, narrow] + acc_specs
        out_shape = [jax.ShapeDtypeStruct((s, 2 * D_MODEL), F32), jax.ShapeDtypeStruct((s, LANES), F32)] + acc_shapes
        scratch += work
    return pl.pallas_call(
        body, name="gla_bwd_rev" if reverse else "gla_bwd", grid=(n_tiles,), in_specs=in_specs, out_specs=out_specs,
        out_shape=out_shape, scratch_shapes=scratch, compiler_params=_params(1),
    )(*args)


def pair_sum(grad, from_sibling):
    n_chips, r, w = from_sibling.shape

    def body(even_ref, odd_ref, sib_ref, o_ref):
        mine = jnp.where(lax.axis_index("c") == 1, odd_ref[...], even_ref[...])
        o_ref[...] = (mine.astype(F32) + sib_ref[...].astype(F32)).astype(o_ref.dtype)

    return pl.pallas_call(
        body, name="pair_sum", grid=(n_chips,),
        in_specs=[pl.BlockSpec((r, w), lambda k: (0, 2 * k)), pl.BlockSpec((r, w), lambda k: (0, 2 * k + 1)),
                  pl.BlockSpec((None, r, w), lambda k: (k, 0, 0))],
        out_specs=pl.BlockSpec((None, r, w), lambda k: (k, 0, 0)),
        out_shape=jax.ShapeDtypeStruct(from_sibling.shape, from_sibling.dtype), compiler_params=_params(1),
    )(grad, grad, from_sibling)


def _adamw_update(g, w, m, v):
    new_m = ADAM_B1 * m + (1.0 - ADAM_B1) * g
    new_v = ADAM_B2 * v + (1.0 - ADAM_B2) * (g * g)
    m_hat = new_m / (1.0 - ADAM_B1 ** ADAM_STEP)
    v_hat = new_v / (1.0 - ADAM_B2 ** ADAM_STEP)
    return -ADAM_LR * (m_hat / (jnp.sqrt(v_hat) + ADAM_EPS) + ADAM_WD * w), new_m, new_v


def sum_parts(parts, name):
    _, r, c = parts.shape

    def body(p_ref, o_ref):
        total = p_ref[0].astype(F32)
        for j in range(1, N_DEV):
            total = total + p_ref[j].astype(F32)
        o_ref[...] = total

    return pl.pallas_call(body, name=name, in_specs=[_full(parts.shape)], out_specs=_full((r, c)), grid=(1,),
                          out_shape=jax.ShapeDtypeStruct((r, c), F32), compiler_params=_params(1))(parts)


def adamw(parts, w, m, v, name, exchange=None):
    n, r, c = parts.shape
    tr = r
    while tr * c * 4 > ADAMW_BLOCK_BYTES and tr % (2 * SUBLANES) == 0:
        tr //= 2

    def body(p_ref, w_ref, m_ref, v_ref, g_ref, d_ref, nm_ref, nv_ref):
        g = p_ref[0].astype(F32)
        for j in range(1, n):
            g = g + p_ref[j].astype(F32)
        g_ref[...] = g
        d_ref[...], nm_ref[...], nv_ref[...] = _adamw_update(g, w_ref[...], m_ref[...], v_ref[...])

    row = pl.BlockSpec((tr, c), lambda i: (i, 0))
    return _call(
        body, name=name, grid=(r // tr,),
        in_specs=[pl.BlockSpec((n, tr, c), lambda i: (0, i, 0)), row, row, row], out_specs=[row] * 4,
        out_shape=[jax.ShapeDtypeStruct((r, c), F32)] * 4, args=[parts, w, m, v], exchange=exchange)


def _small_views(shape):
    if len(shape) == 2:
        return [((slice(None), slice(None)), (slice(None), slice(None)))]
    if len(shape) == 3:
        return [((slice(None), slice(None)), (0,))]
    rows = shape[2]
    return [((slice(k * rows, (k + 1) * rows), slice(None)), (0, k)) for k in range(shape[1])]


def adamw_small(landings, w, m, v):
    names = list(landings)
    n = len(names)
    shapes = [w[name].shape for name in names]

    def body(*refs):
        land, ws, ms, vs = refs[:n], refs[n:2 * n], refs[2 * n:3 * n], refs[3 * n:4 * n]
        outs = [refs[(4 + k) * n:(5 + k) * n] for k in range(4)]
        for k in range(n):
            total = land[k][0]
            for j in range(1, N_DEV):
                total = total + land[k][j]
            for rows, at in _small_views(shapes[k]):
                g = total[rows]
                outs[0][k][at] = g
                outs[1][k][at], outs[2][k][at], outs[3][k][at] = _adamw_update(g, ws[k][at], ms[k][at], vs[k][at])

    blocks = [_full(sh) for sh in shapes]
    outs = pl.pallas_call(
        body, name="adamw_small", grid=(1,),
        in_specs=[_full(landings[name].shape) for name in names] + blocks * 3, out_specs=blocks * 4,
        out_shape=[jax.ShapeDtypeStruct(sh, F32) for sh in shapes] * 4, compiler_params=_params(1),
    )(*[landings[name] for name in names], *[src[name] for src in (w, m, v) for name in names])
    return [dict(zip(names, outs[k * n:(k + 1) * n])) for k in range(4)]


def adamw_replicated(land_vec, land_gate_b, land_loss, names, w, m, v, gate_b):
    n = len(names)

    def body(*refs):
        vec_ref, gb_ref, loss_ref = refs[:3]
        ws, ms, vs = refs[3:3 + n], refs[3 + n:3 + 2 * n], refs[3 + 2 * n:3 + 3 * n]
        gw_ref, gm_ref, gv_ref = refs[3 + 3 * n:6 + 3 * n]
        outs = refs[6 + 3 * n:]
        vec, gb, loss = vec_ref[0], gb_ref[0], loss_ref[0]
        for j in range(1, N_DEV):
            vec, gb, loss = vec + vec_ref[j], gb + gb_ref[j], loss + loss_ref[j]
        for k in range(n):
            g = vec[k:k + 1, :]
            outs[k][...] = g
            outs[n + k][...], outs[2 * n + k][...], outs[3 * n + k][...] = _adamw_update(g, ws[k][...], ms[k][...], vs[k][...])
        outs[4 * n][...] = gb
        outs[4 * n + 1][...], outs[4 * n + 2][...], outs[4 * n + 3][...] = _adamw_update(gb, gw_ref[...], gm_ref[...], gv_ref[...])
        outs[4 * n + 4][...] = loss

    vec_block, gb_block = _full((1, D_MODEL)), _full(gate_b[0].shape)
    outs = pl.pallas_call(
        body, name="adamw_replicated", grid=(1,),
        in_specs=[_full(land_vec.shape), _full(land_gate_b.shape), _full(land_loss.shape)] + [vec_block] * (3 * n) + [gb_block] * 3,
        out_specs=[vec_block] * (4 * n) + [gb_block] * 4 + [_full(land_loss.shape[1:])],
        out_shape=[jax.ShapeDtypeStruct((1, D_MODEL), F32)] * (4 * n) + [jax.ShapeDtypeStruct(gate_b[0].shape, F32)] * 4
        + [jax.ShapeDtypeStruct(land_loss.shape[1:], F32)],
        compiler_params=_params(1),
    )(land_vec, land_gate_b, land_loss, *[src[name] for src in (w, m, v) for name in names], *gate_b)
    results = {name: [outs[k * n + i] for k in range(4)] for i, name in enumerate(names)}
    return results, outs[4 * n:4 * n + 4], outs[4 * n + 4]


SMALL_SHARDED = ("rg_conv_w", "rg_lambda", "sc_conv_w", "odd_norm_pre", "odd_norm_post", "gla_b_gate", "gla_norm_g", "gla_w_gate_lr")
SMALL_ROWS = {"rg_conv_w": (0, 4), "rg_lambda": (4, 2), "sc_conv_w": (6, 3), "odd_norm_pre": (9, 1), "odd_norm_post": (10, 1),
              "gla_b_gate": (11, 2), "gla_norm_g": (13, 1), "gla_w_gate_lr": (16, 32)}


def _pack_small(shards):
    pieces, at = [], 0
    for name in SMALL_SHARDED:
        start, rows = SMALL_ROWS[name]
        if start > at:
            pieces.append(jnp.zeros((start - at, LANES), F32))
        a = shards[name].reshape(rows, -1)
        pieces.append(jnp.pad(a, ((0, 0), (0, LANES - a.shape[1]))))
        at = start + rows
    return jnp.concatenate(pieces, axis=0)


def _unpack_gathered(g):
    def cols(name, width):
        start, rows = SMALL_ROWS[name]
        return jnp.transpose(g[:, start:start + rows, :width], (1, 0, 2)).reshape(rows, N_DEV * width)

    w_lr = cols("gla_w_gate_lr", GLA_KEY // N_DEV).reshape(2, GLA_RANK, GLA_KEY)
    return dict(rg_conv_w=cols("rg_conv_w", LANES), rg_lambda=cols("rg_lambda", LANES), sc_conv_w=cols("sc_conv_w", LANES),
                odd_norm_pre=cols("odd_norm_pre", LANES), odd_norm_post=cols("odd_norm_post", LANES),
                gla_b_gate=cols("gla_b_gate", GLA_KEY // N_DEV), gla_norm_g=cols("gla_norm_g", GLA_DV // N_DEV), gla_w_gate_lr=w_lr)


def _blocks_along_columns(a, rows):
    return jnp.transpose(a.reshape(rows, N_DEV, -1), (1, 0, 2))


def kernel(x, even_norm_pre, even_norm_post, even_w_in, rg_conv_w, rg_conv_b, rg_gate_w, rg_gate_b, rg_lambda, sc_conv_w, even_w_out, odd_norm_pre, odd_norm_post, odd_w_in, gla_w_gate_lr, gla_b_gate, gla_norm_g, odd_w_out, loss_target, m_even_norm_pre, m_even_norm_post, m_even_w_in, m_rg_conv_w, m_rg_conv_b, m_rg_gate_w, m_rg_gate_b, m_rg_lambda, m_sc_conv_w, m_even_w_out, m_odd_norm_pre, m_odd_norm_post, m_odd_w_in, m_gla_w_gate_lr, m_gla_b_gate, m_gla_norm_g, m_odd_w_out, v_even_norm_pre, v_even_norm_post, v_even_w_in, v_rg_conv_w, v_rg_conv_b, v_rg_gate_w, v_rg_gate_b, v_rg_lambda, v_sc_conv_w, v_even_w_out, v_odd_norm_pre, v_odd_norm_post, v_odd_w_in, v_gla_w_gate_lr, v_gla_b_gate, v_gla_norm_g, v_odd_w_out):
    weights = dict(even_norm_pre=even_norm_pre, even_norm_post=even_norm_post, even_w_in=even_w_in, rg_conv_w=rg_conv_w,
                   rg_conv_b=rg_conv_b, rg_gate_w=rg_gate_w, rg_gate_b=rg_gate_b, rg_lambda=rg_lambda, sc_conv_w=sc_conv_w,
                   even_w_out=even_w_out, odd_norm_pre=odd_norm_pre, odd_norm_post=odd_norm_post, odd_w_in=odd_w_in,
                   gla_w_gate_lr=gla_w_gate_lr, gla_b_gate=gla_b_gate, gla_norm_g=gla_norm_g, odd_w_out=odd_w_out)
    m_in = dict(even_norm_pre=m_even_norm_pre, even_norm_post=m_even_norm_post, even_w_in=m_even_w_in, rg_conv_w=m_rg_conv_w,
                rg_conv_b=m_rg_conv_b, rg_gate_w=m_rg_gate_w, rg_gate_b=m_rg_gate_b, rg_lambda=m_rg_lambda, sc_conv_w=m_sc_conv_w,
                even_w_out=m_even_w_out, odd_norm_pre=m_odd_norm_pre, odd_norm_post=m_odd_norm_post, odd_w_in=m_odd_w_in,
                gla_w_gate_lr=m_gla_w_gate_lr, gla_b_gate=m_gla_b_gate, gla_norm_g=m_gla_norm_g, odd_w_out=m_odd_w_out)
    v_in = dict(even_norm_pre=v_even_norm_pre, even_norm_post=v_even_norm_post, even_w_in=v_even_w_in, rg_conv_w=v_rg_conv_w,
                rg_conv_b=v_rg_conv_b, rg_gate_w=v_rg_gate_w, rg_gate_b=v_rg_gate_b, rg_lambda=v_rg_lambda, sc_conv_w=v_sc_conv_w,
                even_w_out=v_even_w_out, odd_norm_pre=v_odd_norm_pre, odd_norm_post=v_odd_norm_post, odd_w_in=v_odd_w_in,
                gla_w_gate_lr=v_gla_w_gate_lr, gla_b_gate=v_gla_b_gate, gla_norm_g=v_gla_norm_g, odd_w_out=v_odd_w_out)
    names = list(weights)
    shapes = {n: weights[n].shape for n in names}
    xs = x[0]
    tgt = loss_target[0]

    proj_e, h_e, w_in_e, small_all = gather_matmul(xs, even_norm_pre, even_w_in[0].astype(BF16),
                                                   _pack_small({n: weights[n][0] for n in SMALL_SHARDED}), 2 * MM_TILE)
    small = _unpack_gathered(small_all)
    gate_w = rg_gate_w[0].reshape(4, RG_HEADS, RG_HEAD_DIM, RG_HEAD_DIM).astype(BF16)
    gate_b = rg_gate_b[0].reshape(4, RG_HEADS, RG_HEAD_DIM)
    conv_b = rg_conv_b
    wg_pad = [jnp.pad(small["gla_w_gate_lr"][d], ((GLA_RANK * d, LANES - GLA_RANK * (d + 1)), (0, 0))).astype(BF16) for d in range(2)]
    bg = [small["gla_b_gate"][d:d + 1] for d in range(2)]
    gnorm = jnp.tile(small["gla_norm_g"], (1, GLA_HEADS))

    half = D_MODEL // 2
    behind_gates = Exchange()
    behind_gates.gather(even_w_out[0].astype(BF16), via_sibling=True)
    behind_gates.gather(odd_w_in[0, :half].astype(BF16), via_sibling=True)
    (ab, hf), (w_out_e, w_in_o_top) = even_gates_fwd(proj_e, small["rg_conv_w"], conv_b, gate_w, gate_b, small["rg_lambda"],
                                                     exchange=behind_gates)
    w_out_e = w_out_e.reshape(2 * D_MODEL, D_MODEL)
    behind_mix_fwd = Exchange()
    behind_mix_fwd.gather(odd_w_in[0, half:].astype(BF16), via_sibling=True)
    behind_mix_fwd.gather(odd_w_out[0].astype(BF16), via_sibling=True)
    (u_e, hb, y_e, x1), (w_in_o_bottom, w_out_o) = even_mix_fwd(ab, hf, proj_e, small["sc_conv_w"], w_out_e, xs, even_norm_post,
                                                                exchange=behind_mix_fwd)
    w_out_o = w_out_o.reshape(D_MODEL, D_MODEL)
    w_in_o = jnp.concatenate([jnp.transpose(part, (1, 0, 2)).reshape(half, ODD_IN) for part in (w_in_o_top, w_in_o_bottom)], axis=0)
    w_in_o = jnp.pad(w_in_o, ((0, 0), (0, ODD_IN_PAD - ODD_IN)))

    proj_o, h_o = rms_matmul(x1, small["odd_norm_pre"], w_in_o, MM_TILE, ODD_IN_PAD, "odd_in")
    o_f, st_f = gla_fwd(proj_o, wg_pad[0], bg[0], False)
    osum, u_o, st_b, y_o, dout, loss_part = gla_fwd(proj_o, wg_pad[1], bg[1], True, o_other=o_f, gnorm=gnorm,
                                                    post=(w_out_o, x1, small["odd_norm_post"], tgt))

    do, dr, dy_o, d_odd_norm_post, d_gnorm = normbwd_matmul_nt(y_o, small["odd_norm_post"], dout, w_out_o, D_MODEL, "odd_out_bwd",
                                                               gla=(proj_o, osum, gnorm))
    d_w_out_o = matmul_tn(u_o, dy_o, D_MODEL, D_MODEL, 4 * MM_TILE, BF16, "odd_w_out_grad")
    dqkv_f, dlr_f, dwg_f, dbg_f = gla_bwd(proj_o, wg_pad[0], bg[0], do, st_f, False)
    dproj_o, dwg_b, dbg_b = gla_bwd(proj_o, wg_pad[1], bg[1], do, st_b, True, first=(dqkv_f, dlr_f, dr))
    dx1, d_odd_norm_pre = matmul_nt_normbwd(dproj_o, w_in_o, x1, small["odd_norm_pre"], dout, MM_TILE, ODD_IN_PAD, "odd_in_bwd")
    d_w_in_o = matmul_tn(h_o, dproj_o, D_MODEL, ODD_IN_PAD // 5, 8 * MM_TILE, BF16, "odd_w_in_grad")

    landed = {}
    behind_out = Exchange()
    behind_out.scatter(d_w_out_o.reshape(N_DEV, D_MODEL // N_DEV, D_MODEL))
    behind_out.scatter(d_odd_norm_pre, columns=True)
    behind_out.scatter(d_odd_norm_post, columns=True)
    behind_out.scatter(_blocks_along_columns(jnp.concatenate([dbg_f, dbg_b], axis=0), 2))
    behind_out.scatter(_blocks_along_columns(d_gnorm, 1))
    behind_out.scatter(_blocks_along_columns(jnp.concatenate([dwg_f[:GLA_RANK], dwg_b[GLA_RANK:2 * GLA_RANK]], axis=0), 2 * GLA_RANK))
    (du_e, dy_e, d_even_norm_post), got = normbwd_matmul_nt(y_e, even_norm_post, dx1, w_out_e, 2 * D_MODEL, "even_out_bwd",
                                                           exchange=behind_out)
    p_w_out_o = got[0]
    for n, part in zip(("odd_norm_pre", "odd_norm_post", "gla_b_gate", "gla_norm_g", "gla_w_gate_lr"), got[1:]):
        landed[n] = part
    d_w_out_e = matmul_tn(u_e, dy_e, D_MODEL, D_MODEL, 4 * MM_TILE, BF16, "even_w_out_grad")
    behind_mix = Exchange()
    behind_mix.scatter(d_w_out_e.reshape(N_DEV, 2 * D_MODEL // N_DEV, D_MODEL))
    (dh, drest, d_sc_w, adj_b), (p_w_out_e,) = even_mix_bwd(du_e, hf, hb, proj_e, small["sc_conv_w"], ab, exchange=behind_mix)
    adj_f = linear_scan(ab, 0, dh.reshape(1, *dh.shape), 0, True, True, "scan_fwd_adjoint")
    behind_gates_bwd = Exchange()
    behind_gates_bwd.scatter(jnp.transpose(d_w_in_o[:, :ODD_IN].reshape(D_MODEL, N_DEV, ODD_SHARD), (1, 0, 2)))
    behind_gates_bwd.scatter(d_sc_w, columns=True)
    (dua, d_gate_w, d_gate_b, d_lam), (p_w_in_o, landed["sc_conv_w"]) = even_gates_bwd(
        proj_e, adj_f, adj_b, hf, hb, dh, small["rg_conv_w"], conv_b, gate_w, gate_b, small["rg_lambda"], exchange=behind_gates_bwd)
    gate_w_rows = 4 * RG_HEADS * RG_HEAD_DIM
    behind_conv = Exchange()
    behind_conv.scatter(d_gate_w.reshape(N_DEV, gate_w_rows // N_DEV, RG_HEAD_DIM))
    behind_conv.scatter(d_lam, columns=True)
    (dxa, d_conv_w, d_conv_b), (p_gate_w, landed["rg_lambda"]) = rg_conv_bwd(dua, proj_e, small["rg_conv_w"], exchange=behind_conv)
    behind_w_grad = Exchange()
    behind_w_grad.gather(sum_parts(p_gate_w, "sum_gate_w"))
    d_w_in_e, (g_gate_w_all,) = matmul_tn(h_e, drest, D_MODEL, D_MODEL, 4 * MM_TILE, BF16, "even_w_in_grad",
                                          exchange=behind_w_grad, b_first=dxa)
    to_sibling = Exchange()
    to_sibling.to_sibling(d_w_in_e)
    to_sibling.scatter(d_conv_w, columns=True)
    from_sibling, landed["rg_conv_w"] = run_exchange(to_sibling, "scatter_to_sibling")
    behind_in_bwd = Exchange()
    behind_in_bwd.among_chips(pair_sum(d_w_in_e, from_sibling))
    (grad_x, d_even_norm_pre), (p_w_in_e,) = matmul_nt_normbwd(
        drest, w_in_e, xs, even_norm_pre, dx1, 2 * MM_TILE, D_MODEL, "even_in_bwd", exchange=behind_in_bwd, first=dxa)
    last = Exchange()
    replicated_vecs = ("even_norm_pre", "even_norm_post", "rg_conv_b")
    last.gather(jnp.concatenate([d_even_norm_pre, d_even_norm_post, d_conv_b], axis=0))
    last.gather(d_gate_b.reshape(4 * RG_HEADS, RG_HEAD_DIM))
    last.gather(loss_part)

    results = {}

    def update(name, parts_, shape2d, exchange=None):
        outs = adamw(parts_, weights[name][0].reshape(shape2d), m_in[name][0].reshape(shape2d), v_in[name][0].reshape(shape2d),
                     "adamw_" + name, exchange=exchange)
        if exchange is not None:
            outs, gathered = outs
        results[name] = [o.reshape(shapes[name]) for o in outs]
        return gathered if exchange is not None else None

    land_vec, land_gate_b, land_loss = update("even_w_in", p_w_in_e, (D_MODEL, EVEN_SHARD), exchange=last)
    update("even_w_out", p_w_out_e, (2 * D_MODEL // N_DEV, D_MODEL))
    update("odd_w_in", p_w_in_o, (D_MODEL, ODD_SHARD))
    update("odd_w_out", p_w_out_o, (D_MODEL // N_DEV, D_MODEL))
    update("rg_gate_w", g_gate_w_all.reshape(1, gate_w_rows, RG_HEAD_DIM), (gate_w_rows, RG_HEAD_DIM))
    small_out = adamw_small({n: landed[n] for n in SMALL_SHARDED}, weights, m_in, v_in)
    for n in SMALL_SHARDED:
        results[n] = [o[n] for o in small_out]
    gate_b_shape = (4 * RG_HEADS, RG_HEAD_DIM)
    rep_out, gate_b_out, loss_all = adamw_replicated(land_vec, land_gate_b, land_loss, replicated_vecs, weights, m_in, v_in,
                                                     [src["rg_gate_b"].reshape(gate_b_shape) for src in (weights, m_in, v_in)])
    results.update(rep_out)
    results["rg_gate_b"] = [o.reshape(shapes["rg_gate_b"]) for o in gate_b_out]

    return (loss_all[0, 0], grad_x.reshape(x.shape), *[results[n][0] for n in names], *[results[n][1] for n in names],
            *[results[n][2] for n in names], *[results[n][3] for n in names])
```

```python
import functools

import jax
import jax.numpy as jnp
from jax import lax
from jax.experimental import pallas as pl
from jax.experimental.pallas import tpu as pltpu

F32 = jnp.float32
BF16 = jnp.bfloat16

N_DEV = 8
D_MODEL = 1024
NORM_EPS = 1e-6
RG_HEADS = 8
RG_HEAD_DIM = 128
RG_C = 8.0
GLA_HEADS = 4
GLA_DK = 128
GLA_DV = 256
GLA_KEY = 512
GLA_RANK = 16
GLA_NORMALIZER = 16.0
GLA_CHUNK = 64
EVEN_IN = 6144
ODD_IN = 3104
ODD_IN_PAD = 3200
ODD_SHARD = ODD_IN // N_DEV
EVEN_SHARD = EVEN_IN // N_DEV
ADAM_LR = 0.001
ADAM_B1 = 0.9
ADAM_B2 = 0.999
ADAM_EPS = 1e-08
ADAM_WD = 0.01
ADAM_STEP = 10

SMALLEST_NORMAL = 1.1754944e-38
SUBLANES = 8
LANES = 128
VMEM_LIMIT_BYTES = 48 * 2 ** 20
ROW_TILE = 256
MM_TILE = 512
ADAMW_BLOCK_BYTES = 2 ** 20
PACK_ROWS = 48
MESH_ID = pl.DeviceIdType.MESH


def _params(n_grid):
    return pltpu.CompilerParams(dimension_semantics=("arbitrary",) * n_grid, vmem_limit_bytes=VMEM_LIMIT_BYTES)


def _bdot(a, b):
    return jnp.dot(a.astype(BF16), b.astype(BF16), preferred_element_type=F32)


def _bdot_nt(a, b):
    return lax.dot_general(a.astype(BF16), b.astype(BF16), (((1,), (1,)), ((), ())), preferred_element_type=F32)


def _bdot_tn(a, b):
    return lax.dot_general(a.astype(BF16), b.astype(BF16), (((0,), (0,)), ((), ())), preferred_element_type=F32)


def _rstd(x):
    return lax.rsqrt(jnp.mean(x * x, axis=-1, keepdims=True) + NORM_EPS)


def _rms(x, g):
    return x * _rstd(x) * g


def _rms_bwd(x, g, dy):
    xh = x * _rstd(x)
    dyg = dy * g
    dx = _rstd(x) * (dyg - xh * jnp.mean(dyg * xh, axis=-1, keepdims=True))
    return dx, jnp.sum(dy * xh, axis=0, keepdims=True)


def _sigmoid(z):
    return 0.5 * jnp.tanh(0.5 * z) + 0.5


def _silu_and_grad(z):
    s = _sigmoid(z)
    return z * s, s * (1.0 + z * (1.0 - s))


def _softplus(z):
    return jnp.maximum(z, 0.0) + jnp.log(1.0 + jnp.exp(-jnp.abs(z)))


def _shift_rows(cur, before, after, d):
    ts = cur.shape[0]
    row = lax.broadcasted_iota(jnp.int32, (SUBLANES, cur.shape[1]), 0)
    out = pltpu.roll(cur, (-d) % ts, 0)
    if d < 0:
        edge = jnp.where(row < -d, pltpu.roll(before, (-d) % SUBLANES, 0), out[:SUBLANES])
        return jnp.concatenate([edge, out[SUBLANES:]], axis=0)
    edge = jnp.where(row >= SUBLANES - d, pltpu.roll(after, (-d) % SUBLANES, 0), out[ts - SUBLANES:])
    return jnp.concatenate([out[:ts - SUBLANES], edge], axis=0)


def _halo_specs(ts, s, width, col, tile=lambda i: i):
    per = ts // SUBLANES
    last = s // SUBLANES - 1
    return [
        pl.BlockSpec((ts, width), lambda i: (tile(i), col)),
        pl.BlockSpec((SUBLANES, width), lambda i: (jnp.maximum(tile(i) * per - 1, 0), col)),
        pl.BlockSpec((SUBLANES, width), lambda i: (jnp.minimum((tile(i) + 1) * per, last), col)),
    ]


def _halo_load(cur_ref, before_ref, after_ref, n_tiles, tile=lambda i: i):
    i = tile(pl.program_id(0))
    before = jnp.where(i > 0, before_ref[...], 0.0)
    after = jnp.where(i < n_tiles - 1, after_ref[...], 0.0)
    return cur_ref[...], before, after


def _full(shape):
    return pl.BlockSpec(shape, lambda *_: (0,) * len(shape))


def _peer(x, y, c, mask):
    px, py, pc = x ^ (mask >> 2), y ^ ((mask >> 1) & 1), c ^ (mask & 1)
    return (px, py, pc), 4 * px + 2 * py + pc


class Exchange:
    SIBLING = 1
    OTHER_CHIPS = (2, 4, 6)

    def __init__(self):
        self.args, self.out_shape, self._kinds = [], [], []

    def gather(self, block, columns=False, via_sibling=False):
        shape = (block.shape[0], N_DEV * block.shape[1]) if columns else (N_DEV,) + block.shape
        return self._add(block, shape, ("gather", columns, via_sibling))

    def scatter(self, stack, columns=False):
        shape = (N_DEV, stack.shape[0], stack.shape[1] // N_DEV) if columns else stack.shape
        return self._add(stack, shape, ("scatter", columns, False))

    def _add(self, arg, shape, kind):
        self.args.append(arg)
        self.out_shape.append(jax.ShapeDtypeStruct(shape, arg.dtype))
        self._kinds.append(kind)
        return len(self.args) - 1

    def semaphores(self):
        n = len(self.args)
        return [pltpu.SemaphoreType.DMA((n, N_DEV - 1)), pltpu.SemaphoreType.DMA((n, N_DEV - 1)), pltpu.SemaphoreType.DMA((n,))]

    def to_sibling(self, array):
        shape = (N_DEV // 2, array.shape[0], array.shape[1] // N_DEV)
        return self._add(array, shape, ("to_sibling", True, False))

    def among_chips(self, stack):
        return self._add(stack, stack.shape, ("among_chips", False, False))

    def _copies(self, position, in_refs, out_refs):
        x, y, c, me = position
        for arr, ((kind, columns, via_sibling), src, out) in enumerate(zip(self._kinds, in_refs, out_refs)):
            if kind == "to_sibling":
                width = src.shape[-1] // N_DEV
                for k in range(N_DEV // 2):
                    block = src.at[:, pl.ds(pl.multiple_of((2 * k + 1 - c) * width, LANES), width)]
                    yield arr, k + 1, block, out.at[k], out.at[k], False, self.SIBLING
                continue
            for mask in range(N_DEV):
                _, peer_id = _peer(x, y, c, mask)
                relayed = via_sibling and mask not in (0, self.SIBLING) + self.OTHER_CHIPS
                if kind == "among_chips":
                    if mask in (0,) + self.OTHER_CHIPS:
                        yield arr, mask, src.at[peer_id // 2], out.at[me // 2], out.at[peer_id // 2], False, mask
                elif kind == "gather":
                    if columns:
                        width = src.shape[-1]
                        yield (arr, mask, src, out.at[:, pl.ds(pl.multiple_of(me * width, LANES), width)],
                               out.at[:, pl.ds(pl.multiple_of(peer_id * width, LANES), width)], relayed, mask)
                    else:
                        yield arr, mask, src, out.at[me], out.at[peer_id], relayed, mask
                else:
                    if columns:
                        width = src.shape[-1] // N_DEV
                        block = src.at[:, pl.ds(pl.multiple_of(peer_id * width, LANES), width)]
                    else:
                        block = src.at[peer_id]
                    yield arr, mask, block, out.at[me], out.at[peer_id], False, mask

    def _remote(self, position, sems, arr, slot, to_mask, src, dst):
        x, y, c, _ = position
        return pltpu.make_async_remote_copy(src_ref=src, dst_ref=dst, send_sem=sems[0].at[arr, slot - 1], recv_sem=sems[1].at[arr, slot - 1],
                                            device_id=_peer(x, y, c, to_mask)[0], device_id_type=MESH_ID)

    def start(self, position, in_refs, out_refs, sems):
        for arr, slot, src, dst, _, relayed, to_mask in self._copies(position, in_refs, out_refs):
            if slot == 0:
                pltpu.make_async_copy(src, dst, sems[2].at[arr]).start()
            elif not relayed:
                self._remote(position, sems, arr, slot, to_mask, src, dst).start()

    def wait(self, position, in_refs, out_refs, sems):
        copies = list(self._copies(position, in_refs, out_refs))
        landings = {(arr, slot): landing for arr, slot, _, _, landing, _, _ in copies}
        passed_on = set()
        for arr, mask, src, _, landing, relayed, _ in copies:
            if relayed:
                held = landings[arr, mask ^ self.SIBLING]
                self._remote(position, sems, arr, mask ^ self.SIBLING, mask ^ self.SIBLING, src, held).wait_recv()
                self._remote(position, sems, arr, mask, self.SIBLING, held, held).start()
                passed_on.add((arr, mask ^ self.SIBLING))
        for arr, slot, src, dst, landing, relayed, to_mask in copies:
            if slot == 0:
                pltpu.make_async_copy(src, dst, sems[2].at[arr]).wait()
                continue
            if (arr, slot) not in passed_on:
                self._remote(position, sems, arr, slot, to_mask, src, landing).wait_recv()
            if relayed:
                held = landings[arr, slot ^ self.SIBLING]
                self._remote(position, sems, arr, slot, self.SIBLING, held, held).wait_send()
            else:
                self._remote(position, sems, arr, slot, to_mask, src, dst).wait_send()


def _call(body, *, name, grid, in_specs, out_specs, out_shape, args, scratch_shapes=(), exchange=None):
    single = not isinstance(out_shape, (list, tuple))
    if single:
        out_specs, out_shape = [out_specs], [out_shape]
    params = _params(len(grid))
    if exchange is None:
        outs = pl.pallas_call(body, name=name, grid=grid, in_specs=in_specs, out_specs=out_specs, out_shape=out_shape,
                              scratch_shapes=list(scratch_shapes), compiler_params=params)(*args)
        return outs[0] if single else outs
    counts = (len(args), len(exchange.args), len(out_shape), len(exchange.out_shape), len(scratch_shapes), 3)

    def wrapped(*refs):
        groups, at = [], 0
        for n in counts:
            groups.append(refs[at:at + n])
            at += n
        main_in, ex_in, main_out, ex_out, main_scratch, sems = groups
        x, y, c = lax.axis_index("x"), lax.axis_index("y"), lax.axis_index("c")
        position = (x, y, c, 4 * x + 2 * y + c)
        ids = [pl.program_id(a) for a in range(len(grid))]
        first = functools.reduce(jnp.logical_and, [i == 0 for i in ids])
        last = functools.reduce(jnp.logical_and, [i == g - 1 for i, g in zip(ids, grid)])

        @pl.when(first)
        def _():
            exchange.start(position, ex_in, ex_out, sems)

        body(*main_in, *main_out, *main_scratch)

        @pl.when(last)
        def _():
            exchange.wait(position, ex_in, ex_out, sems)

    hbm = pl.BlockSpec(memory_space=pl.ANY)
    outs = pl.pallas_call(
        wrapped, name=name, grid=grid, in_specs=list(in_specs) + [hbm] * counts[1], out_specs=list(out_specs) + [hbm] * counts[3],
        out_shape=list(out_shape) + exchange.out_shape, scratch_shapes=list(scratch_shapes) + exchange.semaphores(),
        compiler_params=params)(*args, *exchange.args)
    main = outs[:counts[2]]
    return (main[0] if single else main), outs[counts[2]:]


def run_exchange(exchange, name):
    return _call(lambda: None, name=name, grid=(1,), in_specs=[], out_specs=[], out_shape=[], args=[], exchange=exchange)[1]


def gather_matmul(x, g, w_block, small_block, tm):
    s, d = x.shape
    width = w_block.shape[1]
    pair = 2 * width
    n_chips = N_DEV // 2
    tm = min(tm, s)
    n_i = s // tm
    sibling = Exchange.SIBLING
    y_nbr, x_nbr, diagonal = Exchange.OTHER_CHIPS

    def body(chips_ref, x_ref, g_ref, wb_ref, sb_ref, proj_ref, h_ref, w_ref, small_ref, h_all, w_pair, send, recv, local, load_sem):
        j, i = pl.program_id(0), pl.program_id(1)
        xx, yy, cc = lax.axis_index("x"), lax.axis_index("y"), lax.axis_index("c")
        me = 4 * xx + 2 * yy + cc

        def block_of(dev):
            return w_ref.at[:, pl.ds(pl.multiple_of(dev * width, LANES), width)]

        def half_of(dev, part):
            return w_ref.at[pl.ds(part * (d // 2), d // 2), pl.ds(pl.multiple_of(dev * width, LANES), width)]

        def remote(arr, slot, to_mask, src, dst):
            return pltpu.make_async_remote_copy(src_ref=src, dst_ref=dst, send_sem=send.at[arr, slot - 1], recv_sem=recv.at[arr, slot - 1],
                                                device_id=_peer(xx, yy, cc, to_mask)[0], device_id_type=MESH_ID)

        @pl.when((j == 0) & (i == 0))
        def _():
            pltpu.make_async_copy(wb_ref, block_of(me), local.at[0]).start()
            pltpu.make_async_copy(sb_ref, small_ref.at[me], local.at[1]).start()
            for mask in (sibling, y_nbr, x_nbr):
                remote(0, mask, mask, wb_ref, block_of(me)).start()
            for mask in range(1, N_DEV):
                remote(1, mask, mask, sb_ref, small_ref.at[me]).start()

        def to_sibling(mask):
            return remote(0, mask | sibling, sibling, block_of(me ^ mask), block_of(me ^ mask))

        def from_sibling(mask):
            return remote(0, mask | sibling, mask | sibling, wb_ref, block_of(me ^ (mask | sibling)))

        for step in range(n_chips):
            @pl.when((j == step) & (i == 0))
            def _(step=step):
                if step == 0:
                    pltpu.make_async_copy(wb_ref, block_of(me), local.at[0]).wait()
                    remote(0, sibling, sibling, wb_ref, block_of(me ^ sibling)).wait_recv()
                elif step == 1:
                    remote(0, y_nbr, y_nbr, wb_ref, block_of(me ^ y_nbr)).wait_recv()
                    remote(0, x_nbr, x_nbr, wb_ref, block_of(me ^ x_nbr)).wait_recv()
                    remote(0, diagonal, x_nbr, half_of(me ^ y_nbr, 0), half_of(me ^ y_nbr, 0)).start()
                    remote(2, diagonal, y_nbr, half_of(me ^ x_nbr, 1), half_of(me ^ x_nbr, 1)).start()
                    to_sibling(y_nbr).start()
                    to_sibling(x_nbr).start()
                    from_sibling(y_nbr).wait_recv()
                elif step == 2:
                    from_sibling(x_nbr).wait_recv()
                else:
                    remote(0, diagonal, x_nbr, wb_ref.at[pl.ds(0, d // 2), :], half_of(me ^ diagonal, 0)).wait_recv()
                    remote(2, diagonal, y_nbr, wb_ref.at[pl.ds(0, d // 2), :], half_of(me ^ diagonal, 1)).wait_recv()
                    to_sibling(diagonal).start()
                    from_sibling(diagonal).wait_recv()
                load = pltpu.make_async_copy(w_ref.at[:, pl.ds(pl.multiple_of(chips_ref[step] * pair, LANES), pair)], w_pair, load_sem)
                load.start()
                load.wait()

        rows = pl.ds(pl.multiple_of(i * tm, tm), tm)

        @pl.when(j == 0)
        def _():
            h = _rms(x_ref[...], g_ref[...]).astype(BF16)
            h_all[rows, :] = h
            h_ref[...] = h

        proj_ref[...] = jnp.dot(h_all[rows, :], w_pair[...], preferred_element_type=F32)

        @pl.when((j == n_chips - 1) & (i == n_i - 1))
        def _():
            pltpu.make_async_copy(sb_ref, small_ref.at[me], local.at[1]).wait()
            for mask in range(1, N_DEV):
                remote(1, mask, mask, sb_ref, small_ref.at[me ^ mask]).wait_recv()
                remote(1, mask, mask, sb_ref, small_ref.at[me]).wait_send()
            for mask in (sibling, y_nbr, x_nbr):
                remote(0, mask, mask, wb_ref, block_of(me)).wait_send()
            remote(0, diagonal, x_nbr, half_of(me ^ y_nbr, 0), half_of(me ^ y_nbr, 0)).wait_send()
            remote(2, diagonal, y_nbr, half_of(me ^ x_nbr, 1), half_of(me ^ x_nbr, 1)).wait_send()
            for mask in Exchange.OTHER_CHIPS:
                to_sibling(mask).wait_send()

    def first_pass_row(j, i, chips):
        return jnp.where(j == 0, i, n_i - 1), 0

    hbm = pl.BlockSpec(memory_space=pl.ANY)
    my_chip = 2 * lax.axis_index("x") + lax.axis_index("y")
    chips = (my_chip ^ jnp.arange(n_chips)).astype(jnp.int32)
    grid_spec = pltpu.PrefetchScalarGridSpec(
        num_scalar_prefetch=1, grid=(n_chips, n_i),
        in_specs=[pl.BlockSpec((tm, d), first_pass_row), pl.BlockSpec((1, d), lambda j, i, chips: (0, 0)), hbm, hbm],
        out_specs=[pl.BlockSpec((tm, pair), lambda j, i, chips: (i, chips[j])), pl.BlockSpec((tm, d), first_pass_row), hbm, hbm],
        scratch_shapes=[pltpu.VMEM((s, d), BF16), pltpu.VMEM((d, pair), BF16), pltpu.SemaphoreType.DMA((3, N_DEV - 1)),
                        pltpu.SemaphoreType.DMA((3, N_DEV - 1)), pltpu.SemaphoreType.DMA((2,)), pltpu.SemaphoreType.DMA(())])
    return pl.pallas_call(
        body, name="even_in", grid_spec=grid_spec,
        out_shape=[jax.ShapeDtypeStruct((s, N_DEV * width), F32), jax.ShapeDtypeStruct((s, d), BF16),
                   jax.ShapeDtypeStruct((d, N_DEV * width), w_block.dtype), jax.ShapeDtypeStruct((N_DEV,) + small_block.shape, small_block.dtype)],
        compiler_params=_params(2),
    )(chips, x, g, w_block, small_block)


def rms_matmul(x, g, w, tm, tn, name, exchange=None):
    s, d = x.shape
    n = w.shape[1]
    tm = min(tm, s)

    def body(x_ref, g_ref, w_ref, o_ref, h_ref):
        @pl.when(pl.program_id(1) == 0)
        def _():
            h_ref[...] = _rms(x_ref[...], g_ref[...]).astype(BF16)

        o_ref[...] = jnp.dot(h_ref[...], w_ref[...], preferred_element_type=F32)

    return _call(
        body, name=name, grid=(s // tm, n // tn),
        in_specs=[pl.BlockSpec((tm, d), lambda i, j: (i, 0)), _full((1, d)), pl.BlockSpec((d, tn), lambda i, j: (0, j))],
        out_specs=[pl.BlockSpec((tm, tn), lambda i, j: (i, j)), pl.BlockSpec((tm, d), lambda i, j: (i, 0))],
        out_shape=[jax.ShapeDtypeStruct((s, n), F32), jax.ShapeDtypeStruct((s, d), BF16)],
        args=[x, g, w], exchange=exchange)


def _gla_out_bwd(du, r, osum, gn, do_ref, dr_ref, dgn_ref):
    silu_r, dsilu_r = _silu_and_grad(r)
    for head in range(GLA_HEADS):
        vl = slice(head * GLA_DV, (head + 1) * GLA_DV)
        o_h, g_h, du_h = osum[:, vl], gn[:, vl], du[:, vl]
        dr_ref[:, vl] = (du_h * _rms(o_h, g_h) * dsilu_r[:, vl]).astype(BF16)
        do_h, dg_h = _rms_bwd(o_h, g_h, du_h * silu_r[:, vl])
        do_ref[:, vl] = do_h
        dgn_ref[...] += dg_h


def normbwd_matmul_nt(y, g, dout, w, tn, name, exchange=None, gla=None):
    s, d = y.shape
    n = w.shape[0]
    tm = min(MM_TILE, s)

    def body(*refs):
        if gla is None:
            y_ref, g_ref, dout_ref, w_ref, du_ref, dy_ref, dg_ref = refs
        else:
            y_ref, g_ref, dout_ref, w_ref, r_ref, o_ref, gn_ref, do_ref, dr_ref, dy_ref, dg_ref, dgn_ref = refs
        i, j = pl.program_id(0), pl.program_id(1)

        @pl.when(j == 0)
        def _():
            dy, dg = _rms_bwd(y_ref[...], g_ref[...], dout_ref[...])
            dy_ref[...] = dy.astype(BF16)

            @pl.when(i == 0)
            def _():
                dg_ref[...] = jnp.zeros_like(dg_ref)
                if gla is not None:
                    dgn_ref[...] = jnp.zeros_like(dgn_ref)

            dg_ref[...] += dg

        du = lax.dot_general(dy_ref[...], w_ref[...], (((1,), (1,)), ((), ())), preferred_element_type=F32)
        if gla is None:
            du_ref[...] = du
        else:
            _gla_out_bwd(du, r_ref[...], o_ref[...], gn_ref[...], do_ref, dr_ref, dgn_ref)

    row = pl.BlockSpec((tm, d), lambda i, j: (i, 0))
    in_specs = [row, _full((1, d)), row, pl.BlockSpec((tn, d), lambda i, j: (j, 0))]
    args = [y, g, dout, w]
    tail_specs = [row, _full((1, d))]
    tail_shapes = [jax.ShapeDtypeStruct((s, d), BF16), jax.ShapeDtypeStruct((1, d), F32)]
    if gla is None:
        out_specs = [pl.BlockSpec((tm, tn), lambda i, j: (i, j))] + tail_specs
        out_shape = [jax.ShapeDtypeStruct((s, n), F32)] + tail_shapes
    else:
        proj, osum, gnorm = gla
        assert n == tn == D_MODEL
        in_specs += [pl.BlockSpec((tm, D_MODEL), lambda i, j: (i, 2)), row, _full(gnorm.shape)]
        args += [proj, osum, gnorm]
        out_specs = [row, row] + tail_specs + [_full((1, GLA_DV))]
        out_shape = [jax.ShapeDtypeStruct((s, D_MODEL), F32), jax.ShapeDtypeStruct((s, D_MODEL), BF16)] + tail_shapes + [
            jax.ShapeDtypeStruct((1, GLA_DV), F32)]
    return _call(body, name=name, grid=(s // tm, n // tn), in_specs=in_specs, out_specs=out_specs, out_shape=out_shape,
                 args=args, exchange=exchange)


def matmul_tn(a, b, tm, tn, ts, out_dtype, name, exchange=None, b_first=None):
    s, m = a.shape
    n = b.shape[1] + (0 if b_first is None else tn)
    ts = min(ts, s)
    n_k = s // ts
    dims = (((0,), (0,)), ((), ()))

    def body(*refs):
        if b_first is None:
            a_ref, b_ref, o_ref, acc = refs
        else:
            a_ref, first_ref, b_ref, o_ref, acc = refs
        j, k = pl.program_id(1), pl.program_id(2)

        @pl.when(k == 0)
        def _():
            acc[...] = jnp.zeros_like(acc)

        if b_first is None:
            acc[...] += lax.dot_general(a_ref[...], b_ref[...], dims, preferred_element_type=F32)
        else:
            @pl.when(j == 0)
            def _():
                acc[...] += lax.dot_general(a_ref[...], first_ref[...], dims, preferred_element_type=F32)

            @pl.when(j > 0)
            def _():
                acc[...] += lax.dot_general(a_ref[...], b_ref[...], dims, preferred_element_type=F32)

        @pl.when(k == n_k - 1)
        def _():
            o_ref[...] = acc[...].astype(out_dtype)

    if b_first is None:
        b_specs, b_args = [pl.BlockSpec((ts, tn), lambda i, j, k: (k, j))], [b]
    else:
        b_specs = [pl.BlockSpec((ts, tn), lambda i, j, k: (k, 0)), pl.BlockSpec((ts, tn), lambda i, j, k: (k, jnp.maximum(j - 1, 0)))]
        b_args = [b_first, b]
    return _call(
        body, name=name, grid=(m // tm, n // tn, n_k),
        in_specs=[pl.BlockSpec((ts, tm), lambda i, j, k: (k, i))] + b_specs,
        out_specs=pl.BlockSpec((tm, tn), lambda i, j, k: (i, j)),
        out_shape=jax.ShapeDtypeStruct((m, n), out_dtype),
        scratch_shapes=[pltpu.VMEM((tm, tn), F32)], args=[a] + b_args, exchange=exchange)


def matmul_nt_normbwd(dproj, w, x, g, dres, tm, tk, name, exchange=None, first=None):
    s, kt = dproj.shape
    kt += 0 if first is None else tk
    d = w.shape[0]
    tm = min(tm, s)
    n_k = kt // tk
    dims = (((1,), (1,)), ((), ()))

    def body(*refs):
        if first is None:
            a_ref, w_ref, x_ref, g_ref, r_ref, dx_ref, dg_ref, acc = refs
        else:
            first_ref, a_ref, w_ref, x_ref, g_ref, r_ref, dx_ref, dg_ref, acc = refs
        i, k = pl.program_id(0), pl.program_id(1)

        @pl.when(k == 0)
        def _():
            acc[...] = jnp.zeros_like(acc)

        if first is None:
            acc[...] += lax.dot_general(a_ref[...], w_ref[...], dims, preferred_element_type=F32)
        else:
            @pl.when(k == 0)
            def _():
                acc[...] += lax.dot_general(first_ref[...], w_ref[...], dims, preferred_element_type=F32)

            @pl.when(k > 0)
            def _():
                acc[...] += lax.dot_general(a_ref[...], w_ref[...], dims, preferred_element_type=F32)

        @pl.when(k == n_k - 1)
        def _():
            dx, dg = _rms_bwd(x_ref[...], g_ref[...], acc[...])
            dx_ref[...] = r_ref[...] + dx

            @pl.when(i == 0)
            def _():
                dg_ref[...] = jnp.zeros_like(dg_ref)

            dg_ref[...] += dg

    row = pl.BlockSpec((tm, d), lambda i, k: (i, 0))
    if first is None:
        a_specs, a_args = [pl.BlockSpec((tm, tk), lambda i, k: (i, k))], [dproj]
    else:
        a_specs = [pl.BlockSpec((tm, tk), lambda i, k: (i, 0)), pl.BlockSpec((tm, tk), lambda i, k: (i, jnp.maximum(k - 1, 0)))]
        a_args = [first, dproj]
    return _call(
        body, name=name, grid=(s // tm, n_k),
        in_specs=a_specs + [pl.BlockSpec((d, tk), lambda i, k: (0, k)), row, _full((1, d)), row],
        out_specs=[row, _full((1, d))],
        out_shape=[jax.ShapeDtypeStruct((s, d), F32), jax.ShapeDtypeStruct((1, d), F32)],
        scratch_shapes=[pltpu.VMEM((tm, d), F32)], args=a_args + [w, x, g, dres], exchange=exchange)


def _rg_conv(xa, before, after, cw, cb):
    return (cw[0:1, :] * _shift_rows(xa, before, after, -2) + cw[1:2, :] * _shift_rows(xa, before, after, -1)
            + cw[2:3, :] * xa + cw[3:4, :] * _shift_rows(xa, before, after, 1) + cb)


def _rg_gates(ua_h, gw_ref, gb_ref, c_h, direction, head):
    r = _sigmoid(_bdot(ua_h, gw_ref[2 * direction, head]) + gb_ref[2 * direction, head:head + 1, :])
    i = _sigmoid(_bdot(ua_h, gw_ref[2 * direction + 1, head]) + gb_ref[2 * direction + 1, head:head + 1, :])
    log_a = -c_h * r
    a = jnp.exp(log_a)
    beta_sq = -jnp.tanh(log_a) * (1.0 + a * a)
    inv_beta = lax.rsqrt(jnp.maximum(beta_sq, SMALLEST_NORMAL))
    return r, i, a, beta_sq * inv_beta, inv_beta


def even_gates_fwd(proj, conv_w, conv_b, gate_w, gate_b, lam, exchange=None):
    s = proj.shape[0]
    ts = min(2 * ROW_TILE, s)
    n_tiles = s // ts

    def body(xa_ref, xb_ref, xn_ref, cw_ref, cb_ref, gw_ref, gb_ref, lam_ref, o_ref, hf_ref, carry):
        @pl.when(pl.program_id(0) == 0)
        def _():
            carry[...] = jnp.zeros_like(carry)

        xa, before, after = _halo_load(xa_ref, xb_ref, xn_ref, n_tiles)
        ua = _rg_conv(xa, before, after, cw_ref[...], cb_ref[...])
        c = RG_C * _softplus(-lam_ref[...])
        ua_bf16 = ua.astype(BF16)
        for direction in range(2):
            for head in range(RG_HEADS):
                lanes = slice(head * RG_HEAD_DIM, (head + 1) * RG_HEAD_DIM)
                ua_h = ua[:, lanes]
                _, i, a, beta, _ = _rg_gates(ua_bf16[:, lanes], gw_ref, gb_ref, c[direction:direction + 1, lanes], direction, head)
                o_ref[2 * direction, :, lanes] = a
                o_ref[2 * direction + 1, :, lanes] = beta * (i * ua_h)
        _scan_tile(o_ref.at[0], o_ref.at[1], hf_ref, carry, False, False)

    return _call(
        body, name="even_gates_fwd", grid=(n_tiles,),
        in_specs=_halo_specs(ts, s, D_MODEL, 0) + [_full(conv_w.shape), _full(conv_b.shape), _full(gate_w.shape),
                                                   _full(gate_b.shape), _full(lam.shape)],
        out_specs=[pl.BlockSpec((4, ts, D_MODEL), lambda i: (0, i, 0)), pl.BlockSpec((ts, D_MODEL), lambda i: (i, 0))],
        out_shape=[jax.ShapeDtypeStruct((4, s, D_MODEL), F32), jax.ShapeDtypeStruct((s, D_MODEL), F32)],
        scratch_shapes=[pltpu.VMEM((SUBLANES, D_MODEL), F32)],
        args=[proj, proj, proj, conv_w, conv_b, gate_w, gate_b, lam], exchange=exchange)


def _scan_tile(a_ref, b_ref, h_ref, carry, reverse, b_times_a):
    ts, c = h_ref.shape
    n_blocks = ts // SUBLANES
    row = lax.broadcasted_iota(jnp.int32, (SUBLANES, c), 0)

    def block(j, h_in):
        r0 = pl.multiple_of((n_blocks - 1 - j if reverse else j) * SUBLANES, SUBLANES)
        a = a_ref[pl.ds(r0, SUBLANES), :]
        b = b_ref[pl.ds(r0, SUBLANES), :]
        if b_times_a:
            b = a * b
        for step in (1, 2, 4):
            shift = SUBLANES - step if reverse else step
            valid = row < SUBLANES - step if reverse else row >= step
            b = jnp.where(valid, a * pltpu.roll(b, shift, 0) + b, b)
            a = jnp.where(valid, a * pltpu.roll(a, shift, 0), a)
        h = a * h_in + b
        h_ref[pl.ds(r0, SUBLANES), :] = h
        return h[0:1, :] if reverse else h[SUBLANES - 1:SUBLANES, :]

    carry[0:1, :] = lax.fori_loop(0, n_blocks, block, carry[0:1, :])


def linear_scan(a_arr, a_idx, b_arr, b_idx, reverse, b_times_a, name, exchange=None):
    _, s, c = a_arr.shape
    ts = min(MM_TILE, s)
    n_tiles = s // ts

    def tile_of(i):
        return n_tiles - 1 - i if reverse else i

    def body(a_ref, b_ref, h_ref, carry):
        @pl.when(pl.program_id(0) == 0)
        def _():
            carry[...] = jnp.zeros_like(carry)

        _scan_tile(a_ref, b_ref, h_ref, carry, reverse, b_times_a)

    return _call(
        body, name=name, grid=(n_tiles,),
        in_specs=[pl.BlockSpec((None, ts, c), lambda i: (a_idx, tile_of(i), 0)),
                  pl.BlockSpec((None, ts, c), lambda i: (b_idx, tile_of(i), 0))],
        out_specs=pl.BlockSpec((ts, c), lambda i: (tile_of(i), 0)),
        out_shape=jax.ShapeDtypeStruct((s, c), F32),
        scratch_shapes=[pltpu.VMEM((SUBLANES, c), F32)], args=[a_arr, b_arr], exchange=exchange)


def _sc_conv(p, before, after, w):
    return w[0:1, :] * _shift_rows(p, before, after, -1) + w[1:2, :] * p + w[2:3, :] * _shift_rows(p, before, after, 1)


def even_mix_fwd(ab, hf, proj, sc_w, w_out, xres, g_post, exchange=None):
    s = proj.shape[0]
    ts = min(ROW_TILE, s)
    n_tiles = s // ts

    def tile(i):
        return n_tiles - 1 - i

    row = pl.BlockSpec((ts, D_MODEL), lambda i: (tile(i), 0))

    def col(c):
        return pl.BlockSpec((ts, D_MODEL), lambda i: (tile(i), c))

    def body(a_ref, b_ref, hf_ref, za_ref, xb_ref, xbb_ref, xbn_ref, gb_ref, gc_ref, gcb_ref, gcn_ref, zb_ref, w_ref,
             wo_ref, x_ref, g_ref, u_ref, hb_ref, y_ref, out_ref, carry):
        @pl.when(pl.program_id(0) == 0)
        def _():
            carry[...] = jnp.zeros_like(carry)

        _scan_tile(a_ref, b_ref, hb_ref, carry, True, False)
        xb, xb_before, xb_after = _halo_load(xb_ref, xbb_ref, xbn_ref, n_tiles, tile)
        gc, gc_before, gc_after = _halo_load(gc_ref, gcb_ref, gcn_ref, n_tiles, tile)
        silu_za, _ = _silu_and_grad(za_ref[...])
        silu_zb, _ = _silu_and_grad(zb_ref[...])
        u_ref[:, :D_MODEL] = ((hf_ref[...] + hb_ref[...]) * silu_za).astype(BF16)
        cv = _sc_conv(gc * xb, gc_before * xb_before, gc_after * xb_after, w_ref[...])
        u_ref[:, D_MODEL:] = (gb_ref[...] * cv * silu_zb).astype(BF16)
        y = jnp.dot(u_ref[...], wo_ref[...], preferred_element_type=F32)
        y_ref[...] = y
        out_ref[...] = x_ref[...] + _rms(y, g_ref[...])

    return _call(
        body, name="even_mix_fwd", grid=(n_tiles,),
        in_specs=[pl.BlockSpec((None, ts, D_MODEL), lambda i: (2, tile(i), 0)), pl.BlockSpec((None, ts, D_MODEL), lambda i: (3, tile(i), 0)),
                  row, col(1)] + _halo_specs(ts, s, D_MODEL, 2, tile) + [col(3)] + _halo_specs(ts, s, D_MODEL, 4, tile)
        + [col(5), _full(sc_w.shape), _full(w_out.shape), row, _full(g_post.shape)],
        out_specs=[pl.BlockSpec((ts, 2 * D_MODEL), lambda i: (tile(i), 0)), row, row, row],
        out_shape=[jax.ShapeDtypeStruct((s, 2 * D_MODEL), BF16)] + [jax.ShapeDtypeStruct((s, D_MODEL), F32)] * 3,
        scratch_shapes=[pltpu.VMEM((SUBLANES, D_MODEL), F32)],
        args=[ab, ab, hf, proj, proj, proj, proj, proj, proj, proj, proj, proj, sc_w, w_out, xres, g_post], exchange=exchange)


def even_mix_bwd(du, hf, hb, proj, sc_w, ab, exchange=None):
    s = proj.shape[0]
    ts = min(ROW_TILE, s)
    n_tiles = s // ts
    row = pl.BlockSpec((ts, D_MODEL), lambda i: (i, 0))

    def body(dya_ref, dyb_ref, dybb_ref, dybn_ref, hf_ref, hb_ref, za_ref, xb_ref, xbb_ref, xbn_ref,
             gb_ref, gbb_ref, gbn_ref, gc_ref, gcb_ref, gcn_ref, zb_ref, zbb_ref, zbn_ref, w_ref, a_ref,
             dh_ref, dp_ref, dw_ref, adj_ref, carry):
        @pl.when(pl.program_id(0) == 0)
        def _():
            carry[...] = jnp.zeros_like(carry)

        dyb, dyb_before, dyb_after = _halo_load(dyb_ref, dybb_ref, dybn_ref, n_tiles)
        xb, xb_before, xb_after = _halo_load(xb_ref, xbb_ref, xbn_ref, n_tiles)
        gb, gb_before, gb_after = _halo_load(gb_ref, gbb_ref, gbn_ref, n_tiles)
        gc, gc_before, gc_after = _halo_load(gc_ref, gcb_ref, gcn_ref, n_tiles)
        zb, zb_before, zb_after = _halo_load(zb_ref, zbb_ref, zbn_ref, n_tiles)
        w = w_ref[...]
        dya, za = dya_ref[...], za_ref[...]
        silu_za, dsilu_za = _silu_and_grad(za)
        dh_ref[...] = dya * silu_za
        _scan_tile(a_ref, dh_ref, adj_ref, carry, False, True)
        dp_ref[:, 0:D_MODEL] = (dya * (hf_ref[...] + hb_ref[...]) * dsilu_za).astype(BF16)

        silu_zb, dsilu_zb = _silu_and_grad(zb)
        p, p_before, p_after = gc * xb, gc_before * xb_before, gc_after * xb_after
        cv = _sc_conv(p, p_before, p_after, w)
        dcv = dyb * gb * silu_zb
        dcv_before = dyb_before * gb_before * _silu_and_grad(zb_before)[0]
        dcv_after = dyb_after * gb_after * _silu_and_grad(zb_after)[0]
        dpp = (w[0:1, :] * _shift_rows(dcv, dcv_before, dcv_after, 1) + w[1:2, :] * dcv
               + w[2:3, :] * _shift_rows(dcv, dcv_before, dcv_after, -1))
        dp_ref[:, D_MODEL:2 * D_MODEL] = (dpp * gc).astype(BF16)
        dp_ref[:, 2 * D_MODEL:3 * D_MODEL] = (dyb * cv * silu_zb).astype(BF16)
        dp_ref[:, 3 * D_MODEL:4 * D_MODEL] = (dpp * xb).astype(BF16)
        dp_ref[:, 4 * D_MODEL:5 * D_MODEL] = (dyb * gb * cv * dsilu_zb).astype(BF16)

        @pl.when(pl.program_id(0) == 0)
        def _():
            dw_ref[...] = jnp.zeros_like(dw_ref)

        dw_ref[0:1, :] += jnp.sum(dcv * _shift_rows(p, p_before, p_after, -1), axis=0, keepdims=True)
        dw_ref[1:2, :] += jnp.sum(dcv * p, axis=0, keepdims=True)
        dw_ref[2:3, :] += jnp.sum(dcv * _shift_rows(p, p_before, p_after, 1), axis=0, keepdims=True)

    return _call(
        body, name="even_mix_bwd", grid=(n_tiles,),
        in_specs=[row] + _halo_specs(ts, s, D_MODEL, 1) + [row, row, pl.BlockSpec((ts, D_MODEL), lambda i: (i, 1))]
        + _halo_specs(ts, s, D_MODEL, 2) + _halo_specs(ts, s, D_MODEL, 3) + _halo_specs(ts, s, D_MODEL, 4)
        + _halo_specs(ts, s, D_MODEL, 5) + [_full(sc_w.shape), pl.BlockSpec((None, ts, D_MODEL), lambda i: (2, i, 0))],
        out_specs=[row, pl.BlockSpec((ts, 5 * D_MODEL), lambda i: (i, 0)), _full(sc_w.shape), row],
        out_shape=[jax.ShapeDtypeStruct((s, D_MODEL), F32), jax.ShapeDtypeStruct((s, 5 * D_MODEL), BF16),
                   jax.ShapeDtypeStruct(sc_w.shape, F32), jax.ShapeDtypeStruct((s, D_MODEL), F32)],
        scratch_shapes=[pltpu.VMEM((SUBLANES, D_MODEL), F32)],
        args=[du, du, du, du, hf, hb, proj, *([proj] * 12), sc_w, ab], exchange=exchange)


def even_gates_bwd(proj, adj_f, adj_b, hf, hb, dh, conv_w, conv_b, gate_w, gate_b, lam, exchange=None):
    s = proj.shape[0]
    ts = min(2 * ROW_TILE, s)
    n_tiles = s // ts
    row = pl.BlockSpec((ts, D_MODEL), lambda i: (i, 0))

    def body(xa_ref, xab_ref, xan_ref, af_ref, afb_ref, afn_ref, ab_ref, abb_ref, abn_ref,
             hf_ref, hfb_ref, hfn_ref, hb_ref, hbb_ref, hbn_ref, dh_ref,
             cw_ref, cb_ref, gw_ref, gb_ref, lam_ref, dua_ref, dgw_ref, dgb_ref, dlam_ref):
        @pl.when(pl.program_id(0) == 0)
        def _():
            dgw_ref[...] = jnp.zeros_like(dgw_ref)
            dgb_ref[...] = jnp.zeros_like(dgb_ref)
            dlam_ref[...] = jnp.zeros_like(dlam_ref)

        xa, before, after = _halo_load(xa_ref, xab_ref, xan_ref, n_tiles)
        ua = _rg_conv(xa, before, after, cw_ref[...], cb_ref[...])
        lam_v = lam_ref[...]
        c = RG_C * _softplus(-lam_v)
        dc_dlam = -RG_C * _sigmoid(-lam_v)
        dh = dh_ref[...]
        adj = (_halo_load(af_ref, afb_ref, afn_ref, n_tiles), _halo_load(ab_ref, abb_ref, abn_ref, n_tiles))
        hs = (_halo_load(hf_ref, hfb_ref, hfn_ref, n_tiles), _halo_load(hb_ref, hbb_ref, hbn_ref, n_tiles))
        dua = jnp.zeros_like(ua)
        ua_bf16 = ua.astype(BF16)
        for direction in range(2):
            step = 1 if direction == 0 else -1
            g = dh + _shift_rows(*adj[direction], step)
            da_all = g * _shift_rows(*hs[direction], -step)
            dua_parts = []
            for head in range(RG_HEADS):
                lanes = slice(head * RG_HEAD_DIM, (head + 1) * RG_HEAD_DIM)
                ua_h = ua[:, lanes]
                c_h = c[direction:direction + 1, lanes]
                ua_hb = ua_bf16[:, lanes]
                r, i, a, beta, inv_beta = _rg_gates(ua_hb, gw_ref, gb_ref, c_h, direction, head)
                db_beta = g[:, lanes] * beta
                d_i = db_beta * ua_h
                dbeta = g[:, lanes] * (i * ua_h)
                dlog_a = (da_all[:, lanes] - dbeta * a * inv_beta) * a
                dpr = -c_h * dlog_a * r * (1.0 - r)
                dpi = d_i * i * (1.0 - i)
                dpr_b, dpi_b = dpr.astype(BF16), dpi.astype(BF16)
                dua_parts.append(db_beta * i + _bdot_nt(dpr_b, gw_ref[2 * direction, head])
                                 + _bdot_nt(dpi_b, gw_ref[2 * direction + 1, head]))
                dgw_ref[2 * direction, head] += _bdot_tn(ua_hb, dpr_b)
                dgw_ref[2 * direction + 1, head] += _bdot_tn(ua_hb, dpi_b)
                dgb_ref[2 * direction, head:head + 1, :] += jnp.sum(dpr, axis=0, keepdims=True)
                dgb_ref[2 * direction + 1, head:head + 1, :] += jnp.sum(dpi, axis=0, keepdims=True)
                dlam_ref[direction:direction + 1, lanes] += (
                    jnp.sum(-r * dlog_a, axis=0, keepdims=True) * dc_dlam[direction:direction + 1, lanes])
            dua = dua + jnp.concatenate(dua_parts, axis=1)
        dua_ref[...] = dua

    return _call(
        body, name="even_gates_bwd", grid=(n_tiles,),
        in_specs=_halo_specs(ts, s, D_MODEL, 0) * 5 + [row] + [_full(conv_w.shape), _full(conv_b.shape), _full(gate_w.shape),
                                                             _full(gate_b.shape), _full(lam.shape)],
        out_specs=[row, _full(gate_w.shape), _full(gate_b.shape), _full(lam.shape)],
        out_shape=[jax.ShapeDtypeStruct((s, D_MODEL), F32), jax.ShapeDtypeStruct(gate_w.shape, F32),
                   jax.ShapeDtypeStruct(gate_b.shape, F32), jax.ShapeDtypeStruct(lam.shape, F32)],
        args=[proj, proj, proj, adj_f, adj_f, adj_f, adj_b, adj_b, adj_b, hf, hf, hf, hb, hb, hb, dh, conv_w, conv_b, gate_w,
              gate_b, lam], exchange=exchange)


def rg_conv_bwd(dua, proj, conv_w, exchange=None):
    s = proj.shape[0]
    ts = min(2 * ROW_TILE, s)
    n_tiles = s // ts

    def body(du_ref, dub_ref, dun_ref, xa_ref, xab_ref, xan_ref, cw_ref, dp_ref, dw_ref, db_ref):
        @pl.when(pl.program_id(0) == 0)
        def _():
            dw_ref[...] = jnp.zeros_like(dw_ref)
            db_ref[...] = jnp.zeros_like(db_ref)

        dua, dua_before, dua_after = _halo_load(du_ref, dub_ref, dun_ref, n_tiles)
        xa, xa_before, xa_after = _halo_load(xa_ref, xab_ref, xan_ref, n_tiles)
        cw = cw_ref[...]
        dxa = (cw[0:1, :] * _shift_rows(dua, dua_before, dua_after, 2) + cw[1:2, :] * _shift_rows(dua, dua_before, dua_after, 1)
               + cw[2:3, :] * dua + cw[3:4, :] * _shift_rows(dua, dua_before, dua_after, -1))
        dp_ref[...] = dxa.astype(BF16)
        for tap, offset in enumerate((-2, -1, 0, 1)):
            shifted = xa if offset == 0 else _shift_rows(xa, xa_before, xa_after, offset)
            dw_ref[tap:tap + 1, :] += jnp.sum(dua * shifted, axis=0, keepdims=True)
        db_ref[...] += jnp.sum(dua, axis=0, keepdims=True)

    return _call(
        body, name="rg_conv_bwd", grid=(n_tiles,),
        in_specs=_halo_specs(ts, s, D_MODEL, 0) * 2 + [_full(conv_w.shape)],
        out_specs=[pl.BlockSpec((ts, D_MODEL), lambda i: (i, 0)), _full(conv_w.shape), _full((1, D_MODEL))],
        out_shape=[jax.ShapeDtypeStruct((s, D_MODEL), BF16), jax.ShapeDtypeStruct(conv_w.shape, F32),
                   jax.ShapeDtypeStruct((1, D_MODEL), F32)],
        args=[dua, dua, dua, proj, proj, proj, conv_w], exchange=exchange)


def _split3(x):
    x1 = x.astype(BF16)
    rest = x - x1.astype(F32)
    x2 = rest.astype(BF16)
    return x1, x2, (rest - x2.astype(F32)).astype(BF16)


def _chunk_sum_matrix(t, reverse, transpose):
    i = lax.broadcasted_iota(jnp.int32, (t, t), 0)
    j = lax.broadcasted_iota(jnp.int32, (t, t), 1)
    if transpose:
        i, j = j, i
    same = (i // GLA_CHUNK) == (j // GLA_CHUNK)
    return jnp.where(same & ((j >= i) if reverse else (j <= i)), 1.0, 0.0).astype(BF16)


def _exact_dot(m, x):
    return sum(jnp.dot(m, part, preferred_element_type=F32) for part in _split3(x))


def _chunk_mask(t, reverse):
    i = lax.broadcasted_iota(jnp.int32, (t, t), 0)
    j = lax.broadcasted_iota(jnp.int32, (t, t), 1)
    return ((i // GLA_CHUNK) == (j // GLA_CHUNK)) & ((j >= i) if reverse else (j <= i))


def _chunk_rows(c):
    return slice(c * GLA_CHUNK, (c + 1) * GLA_CHUNK)


def _gla_gate(lr, wg, bg):
    z = _bdot(lr, wg) + bg
    log_alpha = (jnp.minimum(z, 0.0) - jnp.log(1.0 + jnp.exp(-jnp.abs(z)))) * (1.0 / GLA_NORMALIZER)
    return z, log_alpha


def _gla_tile_terms(q, k, bcum, reverse):
    n_chunks = q.shape[0] // GLA_CHUNK
    totals = []
    for c in range(n_chunks):
        edge = c * GLA_CHUNK if reverse else (c + 1) * GLA_CHUNK - 1
        totals.append(bcum[edge:edge + 1, :])
    btot = jnp.concatenate([jnp.broadcast_to(total, (GLA_CHUNK, total.shape[1])) for total in totals], axis=0)
    e_pos, e_neg, e_st = jnp.exp(bcum), jnp.exp(-bcum), jnp.exp(btot - bcum)
    return q * (GLA_DK ** -0.5) * e_pos, k * e_neg, k * e_st, e_pos, e_neg, e_st, [jnp.exp(total) for total in totals]


def _gla_specs(t, n_tiles, reverse_order):
    def tile(i):
        return n_tiles - 1 - i if reverse_order else i

    return tile, [
        pl.BlockSpec((t, GLA_KEY), lambda i: (tile(i), 0)),
        pl.BlockSpec((t, GLA_KEY), lambda i: (tile(i), 1)),
        pl.BlockSpec((t, D_MODEL), lambda i: (tile(i), 1)),
        pl.BlockSpec((t, LANES), lambda i: (tile(i), (ODD_IN_PAD - LANES) // LANES)),
    ]


def gla_fwd(proj, wg, bg, reverse, o_other=None, gnorm=None, post=None):
    s = proj.shape[0]
    t = min(ROW_TILE, s)
    n_tiles = s // t
    n_chunks = t // GLA_CHUNK
    final = o_other is not None
    tile, specs = _gla_specs(t, n_tiles, reverse)

    def body(*refs):
        if final:
            (q_ref, k_ref, v_ref, lr_ref, wg_ref, bg_ref, oo_ref, r_ref, gn_ref, wo_ref, x_ref, gp_ref, t_ref,
             osum_ref, u_ref, st_ref, y_ref, dout_ref, loss_ref, state) = refs
        else:
            q_ref, k_ref, v_ref, lr_ref, wg_ref, bg_ref, o_ref, st_ref, state = refs
            osum_ref = o_ref

        @pl.when(pl.program_id(0) == 0)
        def _():
            state[...] = jnp.zeros_like(state)

        _, log_alpha = _gla_gate(lr_ref[...], wg_ref[...], bg_ref[...])
        bcum = _exact_dot(_chunk_sum_matrix(t, reverse, False), log_alpha)
        q, k, v = q_ref[...], k_ref[...], v_ref[...]
        q_in, k_in, k_st, _, _, _, decays = _gla_tile_terms(q, k, bcum, reverse)
        mask = _chunk_mask(t, reverse)
        order = list(range(n_chunks))[::-1] if reverse else list(range(n_chunks))
        intra, increments = [], []
        for head in range(GLA_HEADS):
            kl = slice(head * GLA_DK, (head + 1) * GLA_DK)
            vl = slice(head * GLA_DV, (head + 1) * GLA_DV)
            scores = jnp.where(mask, _bdot_nt(q_in[:, kl], k_in[:, kl]), 0.0)
            intra.append(_bdot(scores, v[:, vl]))
            increments.append([_bdot_tn(v[_chunk_rows(c), vl], k_st[_chunk_rows(c), kl]) for c in range(n_chunks)])
        for head in range(GLA_HEADS):
            kl = slice(head * GLA_DK, (head + 1) * GLA_DK)
            vl = slice(head * GLA_DV, (head + 1) * GLA_DV)
            running = state[head]
            before = [None] * n_chunks
            for c in order:
                before[c] = running
                st_ref[c, head] = running
                running = running * decays[c][:, kl] + increments[head][c]
            state[head] = running
            inter = [_bdot_nt(q_in[_chunk_rows(c), kl], before[c]) for c in range(n_chunks)]
            osum_ref[:, vl] = intra[head] + jnp.concatenate(inter, axis=0)
        if final:
            osum = osum_ref[...] + oo_ref[...]
            osum_ref[...] = osum
            silu_r, _ = _silu_and_grad(r_ref[...])
            gn = gn_ref[...]
            for head in range(GLA_HEADS):
                vl = slice(head * GLA_DV, (head + 1) * GLA_DV)
                u_ref[:, vl] = (_rms(osum[:, vl], gn[:, vl]) * silu_r[:, vl]).astype(BF16)

            @pl.when(pl.program_id(0) == 0)
            def _():
                loss_ref[...] = jnp.zeros_like(loss_ref)

            y = jnp.dot(u_ref[...], wo_ref[...], preferred_element_type=F32)
            y_ref[...] = y
            diff = x_ref[...] + _rms(y, gp_ref[...]) - t_ref[...]
            dout_ref[...] = diff * (1.0 / D_MODEL)
            loss_ref[...] += 0.5 * jnp.sum(jnp.mean(diff * diff, axis=-1, keepdims=True))

    row = pl.BlockSpec((t, D_MODEL), lambda i: (tile(i), 0))
    st_spec = pl.BlockSpec((n_chunks, GLA_HEADS, GLA_DV, GLA_DK), lambda i: (tile(i), 0, 0, 0))
    st_shape = jax.ShapeDtypeStruct((s // GLA_CHUNK, GLA_HEADS, GLA_DV, GLA_DK), F32)
    in_specs = specs + [_full(wg.shape), _full(bg.shape)]
    args = [proj, proj, proj, proj, wg, bg]
    if final:
        w_out, xres, g_post, target = post
        in_specs += [row, pl.BlockSpec((t, D_MODEL), lambda i: (tile(i), 2)), _full(gnorm.shape), _full(w_out.shape), row,
                     _full(g_post.shape), row]
        args += [o_other, proj, gnorm, w_out, xres, g_post, target]
        out_specs = [row, row, st_spec, row, row, _full((SUBLANES, LANES))]
        out_shape = [jax.ShapeDtypeStruct((s, D_MODEL), F32), jax.ShapeDtypeStruct((s, D_MODEL), BF16), st_shape,
                     jax.ShapeDtypeStruct((s, D_MODEL), F32), jax.ShapeDtypeStruct((s, D_MODEL), F32),
                     jax.ShapeDtypeStruct((SUBLANES, LANES), F32)]
    else:
        out_specs = [row, st_spec]
        out_shape = [jax.ShapeDtypeStruct((s, D_MODEL), F32), st_shape]
    return pl.pallas_call(
        body, name="gla_fwd_rev" if reverse else "gla_fwd", grid=(n_tiles,), in_specs=in_specs, out_specs=out_specs,
        out_shape=out_shape, scratch_shapes=[pltpu.VMEM((GLA_HEADS, GLA_DV, GLA_DK), F32)], compiler_params=_params(1),
    )(*args)


def gla_bwd(proj, wg, bg, do, states, reverse, first=None):
    s = proj.shape[0]
    t = min(ROW_TILE, s)
    n_tiles = s // t
    n_chunks = t // GLA_CHUNK
    final = first is not None
    tile, specs = _gla_specs(t, n_tiles, not reverse)

    def body(*refs):
        if final:
            (q_ref, k_ref, v_ref, lr_ref, wg_ref, bg_ref, do_ref, st_ref, dqkv1_ref, dlr1_ref, dr_ref,
             dp_ref, dwg_ref, dbg_ref, dstate, dqkv, dbc, dbt) = refs
        else:
            (q_ref, k_ref, v_ref, lr_ref, wg_ref, bg_ref, do_ref, st_ref,
             dqkv, dlr_ref, dwg_ref, dbg_ref, dstate, dbc, dbt) = refs

        @pl.when(pl.program_id(0) == 0)
        def _():
            dstate[...] = jnp.zeros_like(dstate)
            dwg_ref[...] = jnp.zeros_like(dwg_ref)
            dbg_ref[...] = jnp.zeros_like(dbg_ref)

        lr, wg_v = lr_ref[...], wg_ref[...]
        z, log_alpha = _gla_gate(lr, wg_v, bg_ref[...])
        bcum = _exact_dot(_chunk_sum_matrix(t, reverse, False), log_alpha)
        q, k, v, do_v = q_ref[...], k_ref[...], v_ref[...], do_ref[...]
        q_in, k_in, k_st, e_pos, e_neg, e_st, decays = _gla_tile_terms(q, k, bcum, reverse)
        mask = _chunk_mask(t, reverse)
        order = list(range(n_chunks)) if reverse else list(range(n_chunks))[::-1]
        dq_intra, dk_intra, dv_intra, increments = [], [], [], []
        for head in range(GLA_HEADS):
            kl = slice(head * GLA_DK, (head + 1) * GLA_DK)
            vl = slice(head * GLA_DV, (head + 1) * GLA_DV)
            scores = jnp.where(mask, _bdot_nt(q_in[:, kl], k_in[:, kl]), 0.0)
            dscores = jnp.where(mask, _bdot_nt(do_v[:, vl], v[:, vl]), 0.0)
            dv_intra.append(_bdot_tn(scores, do_v[:, vl]))
            dq_intra.append(_bdot(dscores, k_in[:, kl]))
            dk_intra.append(_bdot_tn(dscores, q_in[:, kl]))
            increments.append([_bdot_tn(do_v[_chunk_rows(c), vl], q_in[_chunk_rows(c), kl]) for c in range(n_chunks)])
        for head in range(GLA_HEADS):
            kl = slice(head * GLA_DK, (head + 1) * GLA_DK)
            vl = slice(head * GLA_DV, (head + 1) * GLA_DV)
            running = dstate[head]
            after, ddecay = [None] * n_chunks, [None] * n_chunks
            for c in order:
                after[c] = running
                ddecay[c] = jnp.sum(running * st_ref[c, head], axis=0, keepdims=True)
                running = running * decays[c][:, kl] + increments[head][c]
            dstate[head] = running
            dq_inter = jnp.concatenate([_bdot(do_v[_chunk_rows(c), vl], st_ref[c, head]) for c in range(n_chunks)], axis=0)
            dv_inter = jnp.concatenate([_bdot_nt(k_st[_chunk_rows(c), kl], after[c]) for c in range(n_chunks)], axis=0)
            dk_st = jnp.concatenate([_bdot(v[_chunk_rows(c), vl], after[c]) for c in range(n_chunks)], axis=0)
            dq_in = dq_intra[head] + dq_inter
            ks_h = k_st[:, kl]
            dqkv[:, 2 * GLA_KEY + head * GLA_DV:2 * GLA_KEY + (head + 1) * GLA_DV] = dv_intra[head] + dv_inter
            dqkv[:, kl] = dq_in * (GLA_DK ** -0.5) * e_pos[:, kl]
            dqkv[:, GLA_KEY + head * GLA_DK:GLA_KEY + (head + 1) * GLA_DK] = dk_intra[head] * e_neg[:, kl] + dk_st * e_st[:, kl]
            dbc[:, kl] = dq_in * q_in[:, kl] - dk_intra[head] * k_in[:, kl] - dk_st * ks_h
            weighted = dk_st * ks_h
            for c in range(n_chunks):
                dbtot = jnp.sum(weighted[_chunk_rows(c)], axis=0, keepdims=True) + ddecay[c] * decays[c][:, kl]
                dbt[_chunk_rows(c), kl] = jnp.broadcast_to(dbtot, (GLA_CHUNK, GLA_DK))
        dlog_alpha = _exact_dot(_chunk_sum_matrix(t, reverse, True), dbc[...]) + dbt[...]
        dz = dlog_alpha * _sigmoid(-z) * (1.0 / GLA_NORMALIZER)
        dlr = _bdot_nt(dz, wg_v)
        dwg_ref[...] += _bdot_tn(lr, dz)
        dbg_ref[...] += jnp.sum(dz, axis=0, keepdims=True)
        if final:
            dp_ref[:, :2 * D_MODEL] = (dqkv[...] + dqkv1_ref[...]).astype(BF16)
            dp_ref[:, 2 * D_MODEL:3 * D_MODEL] = dr_ref[...]
            dp_ref[:, 3 * D_MODEL:] = (dlr + dlr1_ref[...]).astype(BF16)
        else:
            dlr_ref[...] = dlr

    row = pl.BlockSpec((t, D_MODEL), lambda i: (tile(i), 0))
    wide = pl.BlockSpec((t, 2 * D_MODEL), lambda i: (tile(i), 0))
    narrow = pl.BlockSpec((t, LANES), lambda i: (tile(i), 0))
    st_spec = pl.BlockSpec((n_chunks, GLA_HEADS, GLA_DV, GLA_DK), lambda i: (tile(i), 0, 0, 0))
    in_specs = specs + [_full(wg.shape), _full(bg.shape), row, st_spec]
    args = [proj, proj, proj, proj, wg, bg, do, states]
    acc_specs = [_full(wg.shape), _full(bg.shape)]
    acc_shapes = [jax.ShapeDtypeStruct(wg.shape, F32), jax.ShapeDtypeStruct(bg.shape, F32)]
    scratch = [pltpu.VMEM((GLA_HEADS, GLA_DV, GLA_DK), F32)]
    work = [pltpu.VMEM((t, GLA_KEY), F32), pltpu.VMEM((t, GLA_KEY), F32)]
    if final:
        in_specs += [wide, narrow, row]
        args += list(first)
        out_specs = [pl.BlockSpec((t, ODD_IN_PAD), lambda i: (tile(i), 0))] + acc_specs
        out_shape = [jax.ShapeDtypeStruct((s, ODD_IN_PAD), BF16)] + acc_shapes
        scratch += [pltpu.VMEM((t, 2 * D_MODEL), F32)] + work
    else:
        out_specs = [wide, narrow] + acc_specs
        out_shape = [jax.ShapeDtypeStruct((s, 2 * D_MODEL), F32), jax.ShapeDtypeStruct((s, LANES), F32)] + acc_shapes
        scratch += work
    return pl.pallas_call(
        body, name="gla_bwd_rev" if reverse else "gla_bwd", grid=(n_tiles,), in_specs=in_specs, out_specs=out_specs,
        out_shape=out_shape, scratch_shapes=scratch, compiler_params=_params(1),
    )(*args)


def pair_sum(grad, from_sibling):
    n_chips, r, w = from_sibling.shape

    def body(even_ref, odd_ref, sib_ref, o_ref):
        mine = jnp.where(lax.axis_index("c") == 1, odd_ref[...], even_ref[...])
        o_ref[...] = (mine.astype(F32) + sib_ref[...].astype(F32)).astype(o_ref.dtype)

    return pl.pallas_call(
        body, name="pair_sum", grid=(n_chips,),
        in_specs=[pl.BlockSpec((r, w), lambda k: (0, 2 * k)), pl.BlockSpec((r, w), lambda k: (0, 2 * k + 1)),
                  pl.BlockSpec((None, r, w), lambda k: (k, 0, 0))],
        out_specs=pl.BlockSpec((None, r, w), lambda k: (k, 0, 0)),
        out_shape=jax.ShapeDtypeStruct(from_sibling.shape, from_sibling.dtype), compiler_params=_params(1),
    )(grad, grad, from_sibling)


def _adamw_update(g, w, m, v):
    new_m = ADAM_B1 * m + (1.0 - ADAM_B1) * g
    new_v = ADAM_B2 * v + (1.0 - ADAM_B2) * (g * g)
    m_hat = new_m / (1.0 - ADAM_B1 ** ADAM_STEP)
    v_hat = new_v / (1.0 - ADAM_B2 ** ADAM_STEP)
    return -ADAM_LR * (m_hat / (jnp.sqrt(v_hat) + ADAM_EPS) + ADAM_WD * w), new_m, new_v


def sum_parts(parts, name):
    _, r, c = parts.shape

    def body(p_ref, o_ref):
        total = p_ref[0].astype(F32)
        for j in range(1, N_DEV):
            total = total + p_ref[j].astype(F32)
        o_ref[...] = total

    return pl.pallas_call(body, name=name, in_specs=[_full(parts.shape)], out_specs=_full((r, c)), grid=(1,),
                          out_shape=jax.ShapeDtypeStruct((r, c), F32), compiler_params=_params(1))(parts)


def adamw(parts, w, m, v, name, exchange=None):
    n, r, c = parts.shape
    tr = r
    while tr * c * 4 > ADAMW_BLOCK_BYTES and tr % (2 * SUBLANES) == 0:
        tr //= 2

    def body(p_ref, w_ref, m_ref, v_ref, g_ref, d_ref, nm_ref, nv_ref):
        g = p_ref[0].astype(F32)
        for j in range(1, n):
            g = g + p_ref[j].astype(F32)
        g_ref[...] = g
        d_ref[...], nm_ref[...], nv_ref[...] = _adamw_update(g, w_ref[...], m_ref[...], v_ref[...])

    row = pl.BlockSpec((tr, c), lambda i: (i, 0))
    return _call(
        body, name=name, grid=(r // tr,),
        in_specs=[pl.BlockSpec((n, tr, c), lambda i: (0, i, 0)), row, row, row], out_specs=[row] * 4,
        out_shape=[jax.ShapeDtypeStruct((r, c), F32)] * 4, args=[parts, w, m, v], exchange=exchange)


def _small_views(shape):
    if len(shape) == 2:
        return [((slice(None), slice(None)), (slice(None), slice(None)))]
    if len(shape) == 3:
        return [((slice(None), slice(None)), (0,))]
    rows = shape[2]
    return [((slice(k * rows, (k + 1) * rows), slice(None)), (0, k)) for k in range(shape[1])]


def adamw_small(landings, w, m, v):
    names = list(landings)
    n = len(names)
    shapes = [w[name].shape for name in names]

    def body(*refs):
        land, ws, ms, vs = refs[:n], refs[n:2 * n], refs[2 * n:3 * n], refs[3 * n:4 * n]
        outs = [refs[(4 + k) * n:(5 + k) * n] for k in range(4)]
        for k in range(n):
            total = land[k][0]
            for j in range(1, N_DEV):
                total = total + land[k][j]
            for rows, at in _small_views(shapes[k]):
                g = total[rows]
                outs[0][k][at] = g
                outs[1][k][at], outs[2][k][at], outs[3][k][at] = _adamw_update(g, ws[k][at], ms[k][at], vs[k][at])

    blocks = [_full(sh) for sh in shapes]
    outs = pl.pallas_call(
        body, name="adamw_small", grid=(1,),
        in_specs=[_full(landings[name].shape) for name in names] + blocks * 3, out_specs=blocks * 4,
        out_shape=[jax.ShapeDtypeStruct(sh, F32) for sh in shapes] * 4, compiler_params=_params(1),
    )(*[landings[name] for name in names], *[src[name] for src in (w, m, v) for name in names])
    return [dict(zip(names, outs[k * n:(k + 1) * n])) for k in range(4)]


def adamw_replicated(land_vec, land_gate_b, land_loss, names, w, m, v, gate_b):
    n = len(names)

    def body(*refs):
        vec_ref, gb_ref, loss_ref = refs[:3]
        ws, ms, vs = refs[3:3 + n], refs[3 + n:3 + 2 * n], refs[3 + 2 * n:3 + 3 * n]
        gw_ref, gm_ref, gv_ref = refs[3 + 3 * n:6 + 3 * n]
        outs = refs[6 + 3 * n:]
        vec, gb, loss = vec_ref[0], gb_ref[0], loss_ref[0]
        for j in range(1, N_DEV):
            vec, gb, loss = vec + vec_ref[j], gb + gb_ref[j], loss + loss_ref[j]
        for k in range(n):
            g = vec[k:k + 1, :]
            outs[k][...] = g
            outs[n + k][...], outs[2 * n + k][...], outs[3 * n + k][...] = _adamw_update(g, ws[k][...], ms[k][...], vs[k][...])
        outs[4 * n][...] = gb
        outs[4 * n + 1][...], outs[4 * n + 2][...], outs[4 * n + 3][...] = _adamw_update(gb, gw_ref[...], gm_ref[...], gv_ref[...])
        outs[4 * n + 4][...] = loss

    vec_block, gb_block = _full((1, D_MODEL)), _full(gate_b[0].shape)
    outs = pl.pallas_call(
        body, name="adamw_replicated", grid=(1,),
        in_specs=[_full(land_vec.shape), _full(land_gate_b.shape), _full(land_loss.shape)] + [vec_block] * (3 * n) + [gb_block] * 3,
        out_specs=[vec_block] * (4 * n) + [gb_block] * 4 + [_full(land_loss.shape[1:])],
        out_shape=[jax.ShapeDtypeStruct((1, D_MODEL), F32)] * (4 * n) + [jax.ShapeDtypeStruct(gate_b[0].shape, F32)] * 4
        + [jax.ShapeDtypeStruct(land_loss.shape[1:], F32)],
        compiler_params=_params(1),
    )(land_vec, land_gate_b, land_loss, *[src[name] for src in (w, m, v) for name in names], *gate_b)
    results = {name: [outs[k * n + i] for k in range(4)] for i, name in enumerate(names)}
    return results, outs[4 * n:4 * n + 4], outs[4 * n + 4]


SMALL_SHARDED = ("rg_conv_w", "rg_lambda", "sc_conv_w", "odd_norm_pre", "odd_norm_post", "gla_b_gate", "gla_norm_g", "gla_w_gate_lr")
SMALL_ROWS = {"rg_conv_w": (0, 4), "rg_lambda": (4, 2), "sc_conv_w": (6, 3), "odd_norm_pre": (9, 1), "odd_norm_post": (10, 1),
              "gla_b_gate": (11, 2), "gla_norm_g": (13, 1), "gla_w_gate_lr": (16, 32)}


def _pack_small(shards):
    pieces, at = [], 0
    for name in SMALL_SHARDED:
        start, rows = SMALL_ROWS[name]
        if start > at:
            pieces.append(jnp.zeros((start - at, LANES), F32))
        a = shards[name].reshape(rows, -1)
        pieces.append(jnp.pad(a, ((0, 0), (0, LANES - a.shape[1]))))
        at = start + rows
    return jnp.concatenate(pieces, axis=0)


def _unpack_gathered(g):
    def cols(name, width):
        start, rows = SMALL_ROWS[name]
        return jnp.transpose(g[:, start:start + rows, :width], (1, 0, 2)).reshape(rows, N_DEV * width)

    w_lr = cols("gla_w_gate_lr", GLA_KEY // N_DEV).reshape(2, GLA_RANK, GLA_KEY)
    return dict(rg_conv_w=cols("rg_conv_w", LANES), rg_lambda=cols("rg_lambda", LANES), sc_conv_w=cols("sc_conv_w", LANES),
                odd_norm_pre=cols("odd_norm_pre", LANES), odd_norm_post=cols("odd_norm_post", LANES),
                gla_b_gate=cols("gla_b_gate", GLA_KEY // N_DEV), gla_norm_g=cols("gla_norm_g", GLA_DV // N_DEV), gla_w_gate_lr=w_lr)


def _blocks_along_columns(a, rows):
    return jnp.transpose(a.reshape(rows, N_DEV, -1), (1, 0, 2))


def kernel(x, even_norm_pre, even_norm_post, even_w_in, rg_conv_w, rg_conv_b, rg_gate_w, rg_gate_b, rg_lambda, sc_conv_w, even_w_out, odd_norm_pre, odd_norm_post, odd_w_in, gla_w_gate_lr, gla_b_gate, gla_norm_g, odd_w_out, loss_target, m_even_norm_pre, m_even_norm_post, m_even_w_in, m_rg_conv_w, m_rg_conv_b, m_rg_gate_w, m_rg_gate_b, m_rg_lambda, m_sc_conv_w, m_even_w_out, m_odd_norm_pre, m_odd_norm_post, m_odd_w_in, m_gla_w_gate_lr, m_gla_b_gate, m_gla_norm_g, m_odd_w_out, v_even_norm_pre, v_even_norm_post, v_even_w_in, v_rg_conv_w, v_rg_conv_b, v_rg_gate_w, v_rg_gate_b, v_rg_lambda, v_sc_conv_w, v_even_w_out, v_odd_norm_pre, v_odd_norm_post, v_odd_w_in, v_gla_w_gate_lr, v_gla_b_gate, v_gla_norm_g, v_odd_w_out):
    weights = dict(even_norm_pre=even_norm_pre, even_norm_post=even_norm_post, even_w_in=even_w_in, rg_conv_w=rg_conv_w,
                   rg_conv_b=rg_conv_b, rg_gate_w=rg_gate_w, rg_gate_b=rg_gate_b, rg_lambda=rg_lambda, sc_conv_w=sc_conv_w,
                   even_w_out=even_w_out, odd_norm_pre=odd_norm_pre, odd_norm_post=odd_norm_post, odd_w_in=odd_w_in,
                   gla_w_gate_lr=gla_w_gate_lr, gla_b_gate=gla_b_gate, gla_norm_g=gla_norm_g, odd_w_out=odd_w_out)
    m_in = dict(even_norm_pre=m_even_norm_pre, even_norm_post=m_even_norm_post, even_w_in=m_even_w_in, rg_conv_w=m_rg_conv_w,
                rg_conv_b=m_rg_conv_b, rg_gate_w=m_rg_gate_w, rg_gate_b=m_rg_gate_b, rg_lambda=m_rg_lambda, sc_conv_w=m_sc_conv_w,
                even_w_out=m_even_w_out, odd_norm_pre=m_odd_norm_pre, odd_norm_post=m_odd_norm_post, odd_w_in=m_odd_w_in,
                gla_w_gate_lr=m_gla_w_gate_lr, gla_b_gate=m_gla_b_gate, gla_norm_g=m_gla_norm_g, odd_w_out=m_odd_w_out)
    v_in = dict(even_norm_pre=v_even_norm_pre, even_norm_post=v_even_norm_post, even_w_in=v_even_w_in, rg_conv_w=v_rg_conv_w,
                rg_conv_b=v_rg_conv_b, rg_gate_w=v_rg_gate_w, rg_gate_b=v_rg_gate_b, rg_lambda=v_rg_lambda, sc_conv_w=v_sc_conv_w,
                even_w_out=v_even_w_out, odd_norm_pre=v_odd_norm_pre, odd_norm_post=v_odd_norm_post, odd_w_in=v_odd_w_in,
                gla_w_gate_lr=v_gla_w_gate_lr, gla_b_gate=v_gla_b_gate, gla_norm_g=v_gla_norm_g, odd_w_out=v_odd_w_out)
    names = list(weights)
    shapes = {n: weights[n].shape for n in names}
    xs = x[0]
    tgt = loss_target[0]

    proj_e, h_e, w_in_e, small_all = gather_matmul(xs, even_norm_pre, even_w_in[0].astype(BF16),
                                                   _pack_small({n: weights[n][0] for n in SMALL_SHARDED}), 2 * MM_TILE)
    small = _unpack_gathered(small_all)
    gate_w = rg_gate_w[0].reshape(4, RG_HEADS, RG_HEAD_DIM, RG_HEAD_DIM).astype(BF16)
    gate_b = rg_gate_b[0].reshape(4, RG_HEADS, RG_HEAD_DIM)
    conv_b = rg_conv_b
    wg_pad = [jnp.pad(small["gla_w_gate_lr"][d], ((GLA_RANK * d, LANES - GLA_RANK * (d + 1)), (0, 0))).astype(BF16) for d in range(2)]
    bg = [small["gla_b_gate"][d:d + 1] for d in range(2)]
    gnorm = jnp.tile(small["gla_norm_g"], (1, GLA_HEADS))

    half = D_MODEL // 2
    behind_gates = Exchange()
    behind_gates.gather(even_w_out[0].astype(BF16), via_sibling=True)
    behind_gates.gather(odd_w_in[0, :half].astype(BF16), via_sibling=True)
    (ab, hf), (w_out_e, w_in_o_top) = even_gates_fwd(proj_e, small["rg_conv_w"], conv_b, gate_w, gate_b, small["rg_lambda"],
                                                     exchange=behind_gates)
    w_out_e = w_out_e.reshape(2 * D_MODEL, D_MODEL)
    behind_mix_fwd = Exchange()
    behind_mix_fwd.gather(odd_w_in[0, half:].astype(BF16), via_sibling=True)
    behind_mix_fwd.gather(odd_w_out[0].astype(BF16), via_sibling=True)
    (u_e, hb, y_e, x1), (w_in_o_bottom, w_out_o) = even_mix_fwd(ab, hf, proj_e, small["sc_conv_w"], w_out_e, xs, even_norm_post,
                                                                exchange=behind_mix_fwd)
    w_out_o = w_out_o.reshape(D_MODEL, D_MODEL)
    w_in_o = jnp.concatenate([jnp.transpose(part, (1, 0, 2)).reshape(half, ODD_IN) for part in (w_in_o_top, w_in_o_bottom)], axis=0)
    w_in_o = jnp.pad(w_in_o, ((0, 0), (0, ODD_IN_PAD - ODD_IN)))

    proj_o, h_o = rms_matmul(x1, small["odd_norm_pre"], w_in_o, MM_TILE, ODD_IN_PAD, "odd_in")
    o_f, st_f = gla_fwd(proj_o, wg_pad[0], bg[0], False)
    osum, u_o, st_b, y_o, dout, loss_part = gla_fwd(proj_o, wg_pad[1], bg[1], True, o_other=o_f, gnorm=gnorm,
                                                    post=(w_out_o, x1, small["odd_norm_post"], tgt))

    do, dr, dy_o, d_odd_norm_post, d_gnorm = normbwd_matmul_nt(y_o, small["odd_norm_post"], dout, w_out_o, D_MODEL, "odd_out_bwd",
                                                               gla=(proj_o, osum, gnorm))
    d_w_out_o = matmul_tn(u_o, dy_o, D_MODEL, D_MODEL, 4 * MM_TILE, BF16, "odd_w_out_grad")
    dqkv_f, dlr_f, dwg_f, dbg_f = gla_bwd(proj_o, wg_pad[0], bg[0], do, st_f, False)
    dproj_o, dwg_b, dbg_b = gla_bwd(proj_o, wg_pad[1], bg[1], do, st_b, True, first=(dqkv_f, dlr_f, dr))
    dx1, d_odd_norm_pre = matmul_nt_normbwd(dproj_o, w_in_o, x1, small["odd_norm_pre"], dout, MM_TILE, ODD_IN_PAD, "odd_in_bwd")
    d_w_in_o = matmul_tn(h_o, dproj_o, D_MODEL, ODD_IN_PAD // 5, 8 * MM_TILE, BF16, "odd_w_in_grad")

    landed = {}
    behind_out = Exchange()
    behind_out.scatter(d_w_out_o.reshape(N_DEV, D_MODEL // N_DEV, D_MODEL))
    behind_out.scatter(d_odd_norm_pre, columns=True)
    behind_out.scatter(d_odd_norm_post, columns=True)
    behind_out.scatter(_blocks_along_columns(jnp.concatenate([dbg_f, dbg_b], axis=0), 2))
    behind_out.scatter(_blocks_along_columns(d_gnorm, 1))
    behind_out.scatter(_blocks_along_columns(jnp.concatenate([dwg_f[:GLA_RANK], dwg_b[GLA_RANK:2 * GLA_RANK]], axis=0), 2 * GLA_RANK))
    (du_e, dy_e, d_even_norm_post), got = normbwd_matmul_nt(y_e, even_norm_post, dx1, w_out_e, 2 * D_MODEL, "even_out_bwd",
                                                           exchange=behind_out)
    p_w_out_o = got[0]
    for n, part in zip(("odd_norm_pre", "odd_norm_post", "gla_b_gate", "gla_norm_g", "gla_w_gate_lr"), got[1:]):
        landed[n] = part
    d_w_out_e = matmul_tn(u_e, dy_e, D_MODEL, D_MODEL, 4 * MM_TILE, BF16, "even_w_out_grad")
    behind_mix = Exchange()
    behind_mix.scatter(d_w_out_e.reshape(N_DEV, 2 * D_MODEL // N_DEV, D_MODEL))
    (dh, drest, d_sc_w, adj_b), (p_w_out_e,) = even_mix_bwd(du_e, hf, hb, proj_e, small["sc_conv_w"], ab, exchange=behind_mix)
    adj_f = linear_scan(ab, 0, dh.reshape(1, *dh.shape), 0, True, True, "scan_fwd_adjoint")
    behind_gates_bwd = Exchange()
    behind_gates_bwd.scatter(jnp.transpose(d_w_in_o[:, :ODD_IN].reshape(D_MODEL, N_DEV, ODD_SHARD), (1, 0, 2)))
    behind_gates_bwd.scatter(d_sc_w, columns=True)
    (dua, d_gate_w, d_gate_b, d_lam), (p_w_in_o, landed["sc_conv_w"]) = even_gates_bwd(
        proj_e, adj_f, adj_b, hf, hb, dh, small["rg_conv_w"], conv_b, gate_w, gate_b, small["rg_lambda"], exchange=behind_gates_bwd)
    gate_w_rows = 4 * RG_HEADS * RG_HEAD_DIM
    behind_conv = Exchange()
    behind_conv.scatter(d_gate_w.reshape(N_DEV, gate_w_rows // N_DEV, RG_HEAD_DIM))
    behind_conv.scatter(d_lam, columns=True)
    (dxa, d_conv_w, d_conv_b), (p_gate_w, landed["rg_lambda"]) = rg_conv_bwd(dua, proj_e, small["rg_conv_w"], exchange=behind_conv)
    behind_w_grad = Exchange()
    behind_w_grad.gather(sum_parts(p_gate_w, "sum_gate_w"))
    d_w_in_e, (g_gate_w_all,) = matmul_tn(h_e, drest, D_MODEL, D_MODEL, 4 * MM_TILE, BF16, "even_w_in_grad",
                                          exchange=behind_w_grad, b_first=dxa)
    to_sibling = Exchange()
    to_sibling.to_sibling(d_w_in_e)
    to_sibling.scatter(d_conv_w, columns=True)
    from_sibling, landed["rg_conv_w"] = run_exchange(to_sibling, "scatter_to_sibling")
    behind_in_bwd = Exchange()
    behind_in_bwd.among_chips(pair_sum(d_w_in_e, from_sibling))
    (grad_x, d_even_norm_pre), (p_w_in_e,) = matmul_nt_normbwd(
        drest, w_in_e, xs, even_norm_pre, dx1, 2 * MM_TILE, D_MODEL, "even_in_bwd", exchange=behind_in_bwd, first=dxa)
    last = Exchange()
    replicated_vecs = ("even_norm_pre", "even_norm_post", "rg_conv_b")
    last.gather(jnp.concatenate([d_even_norm_pre, d_even_norm_post, d_conv_b], axis=0))
    last.gather(d_gate_b.reshape(4 * RG_HEADS, RG_HEAD_DIM))
    last.gather(loss_part)

    results = {}

    def update(name, parts_, shape2d, exchange=None):
        outs = adamw(parts_, weights[name][0].reshape(shape2d), m_in[name][0].reshape(shape2d), v_in[name][0].reshape(shape2d),
                     "adamw_" + name, exchange=exchange)
        if exchange is not None:
            outs, gathered = outs
        results[name] = [o.reshape(shapes[name]) for o in outs]
        return gathered if exchange is not None else None

    land_vec, land_gate_b, land_loss = update("even_w_in", p_w_in_e, (D_MODEL, EVEN_SHARD), exchange=last)
    update("even_w_out", p_w_out_e, (2 * D_MODEL // N_DEV, D_MODEL))
    update("odd_w_in", p_w_in_o, (D_MODEL, ODD_SHARD))
    update("odd_w_out", p_w_out_o, (D_MODEL // N_DEV, D_MODEL))
    update("rg_gate_w", g_gate_w_all.reshape(1, gate_w_rows, RG_HEAD_DIM), (gate_w_rows, RG_HEAD_DIM))
    small_out = adamw_small({n: landed[n] for n in SMALL_SHARDED}, weights, m_in, v_in)
    for n in SMALL_SHARDED:
        results[n] = [o[n] for o in small_out]
    gate_b_shape = (4 * RG_HEADS, RG_HEAD_DIM)
    rep_out, gate_b_out, loss_all = adamw_replicated(land_vec, land_gate_b, land_loss, replicated_vecs, weights, m_in, v_in,
                                                     [src["rg_gate_b"].reshape(gate_b_shape) for src in (weights, m_in, v_in)])
    results.update(rep_out)
    results["rg_gate_b"] = [o.reshape(shapes["rg_gate_b"]) for o in gate_b_out]

    return (loss_all[0, 0], grad_x.reshape(x.shape), *[results[n][0] for n in names], *[results[n][1] for n in names],
            *[results[n][2] for n in names], *[results[n][3] for n in names])
```

```python
import functools

import jax
import jax.numpy as jnp
from jax import lax
from jax.experimental import pallas as pl
from jax.experimental.pallas import tpu as pltpu

F32 = jnp.float32
BF16 = jnp.bfloat16

N_DEV = 8
D_MODEL = 1024
NORM_EPS = 1e-6
RG_HEADS = 8
RG_HEAD_DIM = 128
RG_C = 8.0
GLA_HEADS = 4
GLA_DK = 128
GLA_DV = 256
GLA_KEY = 512
GLA_RANK = 16
GLA_NORMALIZER = 16.0
GLA_CHUNK = 64
EVEN_IN = 6144
ODD_IN = 3104
ODD_IN_PAD = 3200
ODD_SHARD = ODD_IN // N_DEV
EVEN_SHARD = EVEN_IN // N_DEV
ADAM_LR = 0.001
ADAM_B1 = 0.9
ADAM_B2 = 0.999
ADAM_EPS = 1e-08
ADAM_WD = 0.01
ADAM_STEP = 10

SMALLEST_NORMAL = 1.1754944e-38
SUBLANES = 8
LANES = 128
VMEM_LIMIT_BYTES = 48 * 2 ** 20
ROW_TILE = 256
GLA_TILE = 256
MM_TILE = 512
ADAMW_BLOCK_BYTES = 2 ** 20
PACK_ROWS = 48
MESH_ID = pl.DeviceIdType.MESH


def _params(n_grid):
    return pltpu.CompilerParams(dimension_semantics=("arbitrary",) * n_grid, vmem_limit_bytes=VMEM_LIMIT_BYTES)


def _bdot(a, b):
    return jnp.dot(a.astype(BF16), b.astype(BF16), preferred_element_type=F32)


def _bdot_nt(a, b):
    return lax.dot_general(a.astype(BF16), b.astype(BF16), (((1,), (1,)), ((), ())), preferred_element_type=F32)


def _bdot_tn(a, b):
    return lax.dot_general(a.astype(BF16), b.astype(BF16), (((0,), (0,)), ((), ())), preferred_element_type=F32)


def _rstd(x):
    return lax.rsqrt(jnp.mean(x * x, axis=-1, keepdims=True) + NORM_EPS)


def _rms(x, g):
    return x * _rstd(x) * g


def _rms_bwd(x, g, dy):
    xh = x * _rstd(x)
    dyg = dy * g
    dx = _rstd(x) * (dyg - xh * jnp.mean(dyg * xh, axis=-1, keepdims=True))
    return dx, jnp.sum(dy * xh, axis=0, keepdims=True)


def _sigmoid(z):
    return 0.5 * jnp.tanh(0.5 * z) + 0.5


def _silu_and_grad(z):
    s = _sigmoid(z)
    return z * s, s * (1.0 + z * (1.0 - s))


def _softplus(z):
    return jnp.maximum(z, 0.0) + jnp.log(1.0 + jnp.exp(-jnp.abs(z)))


def _shift_rows(cur, before, after, d):
    ts = cur.shape[0]
    row = lax.broadcasted_iota(jnp.int32, (SUBLANES, cur.shape[1]), 0)
    out = pltpu.roll(cur, (-d) % ts, 0)
    if d < 0:
        edge = jnp.where(row < -d, pltpu.roll(before, (-d) % SUBLANES, 0), out[:SUBLANES])
        return jnp.concatenate([edge, out[SUBLANES:]], axis=0)
    edge = jnp.where(row >= SUBLANES - d, pltpu.roll(after, (-d) % SUBLANES, 0), out[ts - SUBLANES:])
    return jnp.concatenate([out[:ts - SUBLANES], edge], axis=0)


def _halo_specs(ts, s, width, col, tile=lambda i: i):
    per = ts // SUBLANES
    last = s // SUBLANES - 1
    return [
        pl.BlockSpec((ts, width), lambda i: (tile(i), col)),
        pl.BlockSpec((SUBLANES, width), lambda i: (jnp.maximum(tile(i) * per - 1, 0), col)),
        pl.BlockSpec((SUBLANES, width), lambda i: (jnp.minimum((tile(i) + 1) * per, last), col)),
    ]


def _halo_load(cur_ref, before_ref, after_ref, n_tiles, tile=lambda i: i):
    i = tile(pl.program_id(0))
    before = jnp.where(i > 0, before_ref[...], 0.0)
    after = jnp.where(i < n_tiles - 1, after_ref[...], 0.0)
    return cur_ref[...], before, after


def _full(shape):
    return pl.BlockSpec(shape, lambda *_: (0,) * len(shape))


def _peer(x, y, c, mask):
    px, py, pc = x ^ (mask >> 2), y ^ ((mask >> 1) & 1), c ^ (mask & 1)
    return (px, py, pc), 4 * px + 2 * py + pc


class Exchange:
    SIBLING = 1
    OTHER_CHIPS = (2, 4, 6)

    def __init__(self):
        self.args, self.out_shape, self._kinds = [], [], []

    def gather(self, block, columns=False, via_sibling=False):
        shape = (block.shape[0], N_DEV * block.shape[1]) if columns else (N_DEV,) + block.shape
        return self._add(block, shape, ("gather", columns, via_sibling))

    def scatter(self, stack, columns=False):
        shape = (N_DEV, stack.shape[0], stack.shape[1] // N_DEV) if columns else stack.shape
        return self._add(stack, shape, ("scatter", columns, False))

    def _add(self, arg, shape, kind):
        self.args.append(arg)
        self.out_shape.append(jax.ShapeDtypeStruct(shape, arg.dtype))
        self._kinds.append(kind)
        return len(self.args) - 1

    def semaphores(self):
        n = len(self.args)
        return [pltpu.SemaphoreType.DMA((n, N_DEV - 1)), pltpu.SemaphoreType.DMA((n, N_DEV - 1)), pltpu.SemaphoreType.DMA((n,))]

    def to_sibling(self, array):
        shape = (N_DEV // 2, array.shape[0], array.shape[1] // N_DEV)
        return self._add(array, shape, ("to_sibling", True, False))

    def among_chips(self, stack):
        return self._add(stack, stack.shape, ("among_chips", False, False))

    def _copies(self, position, in_refs, out_refs):
        x, y, c, me = position
        for arr, ((kind, columns, via_sibling), src, out) in enumerate(zip(self._kinds, in_refs, out_refs)):
            if kind == "to_sibling":
                width = src.shape[-1] // N_DEV
                for k in range(N_DEV // 2):
                    block = src.at[:, pl.ds(pl.multiple_of((2 * k + 1 - c) * width, LANES), width)]
                    yield arr, k + 1, block, out.at[k], out.at[k], False, self.SIBLING
                continue
            for mask in range(N_DEV):
                _, peer_id = _peer(x, y, c, mask)
                relayed = via_sibling and mask not in (0, self.SIBLING) + self.OTHER_CHIPS
                if kind == "among_chips":
                    if mask in (0,) + self.OTHER_CHIPS:
                        yield arr, mask, src.at[peer_id // 2], out.at[me // 2], out.at[peer_id // 2], False, mask
                elif kind == "gather":
                    if columns:
                        width = src.shape[-1]
                        yield (arr, mask, src, out.at[:, pl.ds(pl.multiple_of(me * width, LANES), width)],
                               out.at[:, pl.ds(pl.multiple_of(peer_id * width, LANES), width)], relayed, mask)
                    else:
                        yield arr, mask, src, out.at[me], out.at[peer_id], relayed, mask
                else:
                    if columns:
                        width = src.shape[-1] // N_DEV
                        block = src.at[:, pl.ds(pl.multiple_of(peer_id * width, LANES), width)]
                    else:
                        block = src.at[peer_id]
                    yield arr, mask, block, out.at[me], out.at[peer_id], False, mask

    def _remote(self, position, sems, arr, slot, to_mask, src, dst):
        x, y, c, _ = position
        return pltpu.make_async_remote_copy(src_ref=src, dst_ref=dst, send_sem=sems[0].at[arr, slot - 1], recv_sem=sems[1].at[arr, slot - 1],
                                            device_id=_peer(x, y, c, to_mask)[0], device_id_type=MESH_ID)

    def start(self, position, in_refs, out_refs, sems):
        for arr, slot, src, dst, _, relayed, to_mask in self._copies(position, in_refs, out_refs):
            if slot == 0:
                pltpu.make_async_copy(src, dst, sems[2].at[arr]).start()
            elif not relayed:
                self._remote(position, sems, arr, slot, to_mask, src, dst).start()

    def wait(self, position, in_refs, out_refs, sems):
        copies = list(self._copies(position, in_refs, out_refs))
        landings = {(arr, slot): landing for arr, slot, _, _, landing, _, _ in copies}
        passed_on = set()
        for arr, mask, src, _, landing, relayed, _ in copies:
            if relayed:
                held = landings[arr, mask ^ self.SIBLING]
                self._remote(position, sems, arr, mask ^ self.SIBLING, mask ^ self.SIBLING, src, held).wait_recv()
                self._remote(position, sems, arr, mask, self.SIBLING, held, held).start()
                passed_on.add((arr, mask ^ self.SIBLING))
        for arr, slot, src, dst, landing, relayed, to_mask in copies:
            if slot == 0:
                pltpu.make_async_copy(src, dst, sems[2].at[arr]).wait()
                continue
            if (arr, slot) not in passed_on:
                self._remote(position, sems, arr, slot, to_mask, src, landing).wait_recv()
            if relayed:
                held = landings[arr, slot ^ self.SIBLING]
                self._remote(position, sems, arr, slot, self.SIBLING, held, held).wait_send()
            else:
                self._remote(position, sems, arr, slot, to_mask, src, dst).wait_send()


def _call(body, *, name, grid, in_specs, out_specs, out_shape, args, scratch_shapes=(), exchange=None):
    single = not isinstance(out_shape, (list, tuple))
    if single:
        out_specs, out_shape = [out_specs], [out_shape]
    params = _params(len(grid))
    if exchange is None:
        outs = pl.pallas_call(body, name=name, grid=grid, in_specs=in_specs, out_specs=out_specs, out_shape=out_shape,
                              scratch_shapes=list(scratch_shapes), compiler_params=params)(*args)
        return outs[0] if single else outs
    counts = (len(args), len(exchange.args), len(out_shape), len(exchange.out_shape), len(scratch_shapes), 3)

    def wrapped(*refs):
        groups, at = [], 0
        for n in counts:
            groups.append(refs[at:at + n])
            at += n
        main_in, ex_in, main_out, ex_out, main_scratch, sems = groups
        x, y, c = lax.axis_index("x"), lax.axis_index("y"), lax.axis_index("c")
        position = (x, y, c, 4 * x + 2 * y + c)
        ids = [pl.program_id(a) for a in range(len(grid))]
        first = functools.reduce(jnp.logical_and, [i == 0 for i in ids])
        last = functools.reduce(jnp.logical_and, [i == g - 1 for i, g in zip(ids, grid)])

        @pl.when(first)
        def _():
            exchange.start(position, ex_in, ex_out, sems)

        body(*main_in, *main_out, *main_scratch)

        @pl.when(last)
        def _():
            exchange.wait(position, ex_in, ex_out, sems)

    hbm = pl.BlockSpec(memory_space=pl.ANY)
    outs = pl.pallas_call(
        wrapped, name=name, grid=grid, in_specs=list(in_specs) + [hbm] * counts[1], out_specs=list(out_specs) + [hbm] * counts[3],
        out_shape=list(out_shape) + exchange.out_shape, scratch_shapes=list(scratch_shapes) + exchange.semaphores(),
        compiler_params=params)(*args, *exchange.args)
    main = outs[:counts[2]]
    return (main[0] if single else main), outs[counts[2]:]


def run_exchange(exchange, name):
    return _call(lambda: None, name=name, grid=(1,), in_specs=[], out_specs=[], out_shape=[], args=[], exchange=exchange)[1]


def gather_matmul(x, g, w_block, small_block, tm):
    s, d = x.shape
    width = w_block.shape[1]
    pair = 2 * width
    n_chips = N_DEV // 2
    tm = min(tm, s)
    n_i = s // tm
    sibling = Exchange.SIBLING
    y_nbr, x_nbr, diagonal = Exchange.OTHER_CHIPS

    def body(chips_ref, x_ref, g_ref, wb_ref, sb_ref, proj_ref, h_ref, w_ref, small_ref, h_all, w_pair, send, recv, local, load_sem):
        j, i = pl.program_id(0), pl.program_id(1)
        xx, yy, cc = lax.axis_index("x"), lax.axis_index("y"), lax.axis_index("c")
        me = 4 * xx + 2 * yy + cc

        def block_of(dev):
            return w_ref.at[:, pl.ds(pl.multiple_of(dev * width, LANES), width)]

        def half_of(dev, part):
            return w_ref.at[pl.ds(part * (d // 2), d // 2), pl.ds(pl.multiple_of(dev * width, LANES), width)]

        def remote(arr, slot, to_mask, src, dst):
            return pltpu.make_async_remote_copy(src_ref=src, dst_ref=dst, send_sem=send.at[arr, slot - 1], recv_sem=recv.at[arr, slot - 1],
                                                device_id=_peer(xx, yy, cc, to_mask)[0], device_id_type=MESH_ID)

        @pl.when((j == 0) & (i == 0))
        def _():
            pltpu.make_async_copy(wb_ref, block_of(me), local.at[0]).start()
            pltpu.make_async_copy(sb_ref, small_ref.at[me], local.at[1]).start()
            for mask in (sibling, y_nbr, x_nbr):
                remote(0, mask, mask, wb_ref, block_of(me)).start()
            for mask in range(1, N_DEV):
                remote(1, mask, mask, sb_ref, small_ref.at[me]).start()

        def to_sibling(mask):
            return remote(0, mask | sibling, sibling, block_of(me ^ mask), block_of(me ^ mask))

        def from_sibling(mask):
            return remote(0, mask | sibling, mask | sibling, wb_ref, block_of(me ^ (mask | sibling)))

        for step in range(n_chips):
            @pl.when((j == step) & (i == 0))
            def _(step=step):
                if step == 0:
                    pltpu.make_async_copy(wb_ref, block_of(me), local.at[0]).wait()
                    remote(0, sibling, sibling, wb_ref, block_of(me ^ sibling)).wait_recv()
                elif step == 1:
                    remote(0, y_nbr, y_nbr, wb_ref, block_of(me ^ y_nbr)).wait_recv()
                    remote(0, x_nbr, x_nbr, wb_ref, block_of(me ^ x_nbr)).wait_recv()
                    remote(0, diagonal, x_nbr, half_of(me ^ y_nbr, 0), half_of(me ^ y_nbr, 0)).start()
                    remote(2, diagonal, y_nbr, half_of(me ^ x_nbr, 1), half_of(me ^ x_nbr, 1)).start()
                    to_sibling(y_nbr).start()
                    to_sibling(x_nbr).start()
                    from_sibling(y_nbr).wait_recv()
                elif step == 2:
                    from_sibling(x_nbr).wait_recv()
                else:
                    remote(0, diagonal, x_nbr, wb_ref.at[pl.ds(0, d // 2), :], half_of(me ^ diagonal, 0)).wait_recv()
                    remote(2, diagonal, y_nbr, wb_ref.at[pl.ds(0, d // 2), :], half_of(me ^ diagonal, 1)).wait_recv()
                    to_sibling(diagonal).start()
                    from_sibling(diagonal).wait_recv()
                load = pltpu.make_async_copy(w_ref.at[:, pl.ds(pl.multiple_of(chips_ref[step] * pair, LANES), pair)], w_pair, load_sem)
                load.start()
                load.wait()

        rows = pl.ds(pl.multiple_of(i * tm, tm), tm)

        @pl.when(j == 0)
        def _():
            h = _rms(x_ref[...], g_ref[...]).astype(BF16)
            h_all[rows, :] = h
            h_ref[...] = h

        proj_ref[...] = jnp.dot(h_all[rows, :], w_pair[...], preferred_element_type=F32)

        @pl.when((j == n_chips - 1) & (i == n_i - 1))
        def _():
            pltpu.make_async_copy(sb_ref, small_ref.at[me], local.at[1]).wait()
            for mask in range(1, N_DEV):
                remote(1, mask, mask, sb_ref, small_ref.at[me ^ mask]).wait_recv()
                remote(1, mask, mask, sb_ref, small_ref.at[me]).wait_send()
            for mask in (sibling, y_nbr, x_nbr):
                remote(0, mask, mask, wb_ref, block_of(me)).wait_send()
            remote(0, diagonal, x_nbr, half_of(me ^ y_nbr, 0), half_of(me ^ y_nbr, 0)).wait_send()
            remote(2, diagonal, y_nbr, half_of(me ^ x_nbr, 1), half_of(me ^ x_nbr, 1)).wait_send()
            for mask in Exchange.OTHER_CHIPS:
                to_sibling(mask).wait_send()

    def first_pass_row(j, i, chips):
        return jnp.where(j == 0, i, n_i - 1), 0

    hbm = pl.BlockSpec(memory_space=pl.ANY)
    my_chip = 2 * lax.axis_index("x") + lax.axis_index("y")
    chips = (my_chip ^ jnp.arange(n_chips)).astype(jnp.int32)
    grid_spec = pltpu.PrefetchScalarGridSpec(
        num_scalar_prefetch=1, grid=(n_chips, n_i),
        in_specs=[pl.BlockSpec((tm, d), first_pass_row), pl.BlockSpec((1, d), lambda j, i, chips: (0, 0)), hbm, hbm],
        out_specs=[pl.BlockSpec((tm, pair), lambda j, i, chips: (i, chips[j])), pl.BlockSpec((tm, d), first_pass_row), hbm, hbm],
        scratch_shapes=[pltpu.VMEM((s, d), BF16), pltpu.VMEM((d, pair), BF16), pltpu.SemaphoreType.DMA((3, N_DEV - 1)),
                        pltpu.SemaphoreType.DMA((3, N_DEV - 1)), pltpu.SemaphoreType.DMA((2,)), pltpu.SemaphoreType.DMA(())])
    return pl.pallas_call(
        body, name="even_in", grid_spec=grid_spec,
        out_shape=[jax.ShapeDtypeStruct((s, N_DEV * width), F32), jax.ShapeDtypeStruct((s, d), BF16),
                   jax.ShapeDtypeStruct((d, N_DEV * width), w_block.dtype), jax.ShapeDtypeStruct((N_DEV,) + small_block.shape, small_block.dtype)],
        compiler_params=_params(2),
    )(chips, x, g, w_block, small_block)


def rms_matmul(x, g, w, tm, tn, name, exchange=None):
    s, d = x.shape
    n = w.shape[1]
    tm = min(tm, s)

    def body(x_ref, g_ref, w_ref, o_ref, h_ref):
        @pl.when(pl.program_id(1) == 0)
        def _():
            h_ref[...] = _rms(x_ref[...], g_ref[...]).astype(BF16)

        o_ref[...] = jnp.dot(h_ref[...], w_ref[...], preferred_element_type=F32)

    return _call(
        body, name=name, grid=(s // tm, n // tn),
        in_specs=[pl.BlockSpec((tm, d), lambda i, j: (i, 0)), _full((1, d)), pl.BlockSpec((d, tn), lambda i, j: (0, j))],
        out_specs=[pl.BlockSpec((tm, tn), lambda i, j: (i, j)), pl.BlockSpec((tm, d), lambda i, j: (i, 0))],
        out_shape=[jax.ShapeDtypeStruct((s, n), F32), jax.ShapeDtypeStruct((s, d), BF16)],
        args=[x, g, w], exchange=exchange)


def _gla_out_bwd(du, r, osum, gn, do_ref, dr_ref, dgn_ref):
    silu_r, dsilu_r = _silu_and_grad(r)
    for head in range(GLA_HEADS):
        vl = slice(head * GLA_DV, (head + 1) * GLA_DV)
        o_h, g_h, du_h = osum[:, vl], gn[:, vl], du[:, vl]
        dr_ref[:, vl] = (du_h * _rms(o_h, g_h) * dsilu_r[:, vl]).astype(BF16)
        do_h, dg_h = _rms_bwd(o_h, g_h, du_h * silu_r[:, vl])
        do_ref[:, vl] = do_h
        dgn_ref[...] += dg_h


def normbwd_matmul_nt(y, g, dout, w, tn, name, exchange=None, gla=None):
    s, d = y.shape
    n = w.shape[0]
    tm = min(MM_TILE, s)

    def body(*refs):
        if gla is None:
            y_ref, g_ref, dout_ref, w_ref, du_ref, dy_ref, dg_ref = refs
        else:
            y_ref, g_ref, dout_ref, w_ref, r_ref, o_ref, gn_ref, do_ref, dr_ref, dy_ref, dg_ref, dgn_ref = refs
        i, j = pl.program_id(0), pl.program_id(1)

        @pl.when(j == 0)
        def _():
            dy, dg = _rms_bwd(y_ref[...], g_ref[...], dout_ref[...])
            dy_ref[...] = dy.astype(BF16)

            @pl.when(i == 0)
            def _():
                dg_ref[...] = jnp.zeros_like(dg_ref)
                if gla is not None:
                    dgn_ref[...] = jnp.zeros_like(dgn_ref)

            dg_ref[...] += dg

        du = lax.dot_general(dy_ref[...], w_ref[...], (((1,), (1,)), ((), ())), preferred_element_type=F32)
        if gla is None:
            du_ref[...] = du
        else:
            _gla_out_bwd(du, r_ref[...], o_ref[...], gn_ref[...], do_ref, dr_ref, dgn_ref)

    row = pl.BlockSpec((tm, d), lambda i, j: (i, 0))
    in_specs = [row, _full((1, d)), row, pl.BlockSpec((tn, d), lambda i, j: (j, 0))]
    args = [y, g, dout, w]
    tail_specs = [row, _full((1, d))]
    tail_shapes = [jax.ShapeDtypeStruct((s, d), BF16), jax.ShapeDtypeStruct((1, d), F32)]
    if gla is None:
        out_specs = [pl.BlockSpec((tm, tn), lambda i, j: (i, j))] + tail_specs
        out_shape = [jax.ShapeDtypeStruct((s, n), F32)] + tail_shapes
    else:
        proj, osum, gnorm = gla
        assert n == tn == D_MODEL
        in_specs += [pl.BlockSpec((tm, D_MODEL), lambda i, j: (i, 2)), row, _full(gnorm.shape)]
        args += [proj, osum, gnorm]
        out_specs = [row, row] + tail_specs + [_full((1, GLA_DV))]
        out_shape = [jax.ShapeDtypeStruct((s, D_MODEL), F32), jax.ShapeDtypeStruct((s, D_MODEL), BF16)] + tail_shapes + [
            jax.ShapeDtypeStruct((1, GLA_DV), F32)]
    return _call(body, name=name, grid=(s // tm, n // tn), in_specs=in_specs, out_specs=out_specs, out_shape=out_shape,
                 args=args, exchange=exchange)


def matmul_tn(a, b, tm, tn, ts, out_dtype, name, exchange=None, b_first=None):
    s, m = a.shape
    n = b.shape[1] + (0 if b_first is None else tn)
    ts = min(ts, s)
    n_k = s // ts
    dims = (((0,), (0,)), ((), ()))

    def body(*refs):
        if b_first is None:
            a_ref, b_ref, o_ref, acc = refs
        else:
            a_ref, first_ref, b_ref, o_ref, acc = refs
        j, k = pl.program_id(1), pl.program_id(2)

        @pl.when(k == 0)
        def _():
            acc[...] = jnp.zeros_like(acc)

        if b_first is None:
            acc[...] += lax.dot_general(a_ref[...], b_ref[...], dims, preferred_element_type=F32)
        else:
            @pl.when(j == 0)
            def _():
                acc[...] += lax.dot_general(a_ref[...], first_ref[...], dims, preferred_element_type=F32)

            @pl.when(j > 0)
            def _():
                acc[...] += lax.dot_general(a_ref[...], b_ref[...], dims, preferred_element_type=F32)

        @pl.when(k == n_k - 1)
        def _():
            o_ref[...] = acc[...].astype(out_dtype)

    if b_first is None:
        b_specs, b_args = [pl.BlockSpec((ts, tn), lambda i, j, k: (k, j))], [b]
    else:
        b_specs = [pl.BlockSpec((ts, tn), lambda i, j, k: (k, 0)), pl.BlockSpec((ts, tn), lambda i, j, k: (k, jnp.maximum(j - 1, 0)))]
        b_args = [b_first, b]
    return _call(
        body, name=name, grid=(m // tm, n // tn, n_k),
        in_specs=[pl.BlockSpec((ts, tm), lambda i, j, k: (k, i))] + b_specs,
        out_specs=pl.BlockSpec((tm, tn), lambda i, j, k: (i, j)),
        out_shape=jax.ShapeDtypeStruct((m, n), out_dtype),
        scratch_shapes=[pltpu.VMEM((tm, tn), F32)], args=[a] + b_args, exchange=exchange)


def matmul_nt_normbwd(dproj, w, x, g, dres, tm, tk, name, exchange=None, first=None):
    s, kt = dproj.shape
    kt += 0 if first is None else tk
    d = w.shape[0]
    tm = min(tm, s)
    n_k = kt // tk
    dims = (((1,), (1,)), ((), ()))

    def body(*refs):
        if first is None:
            a_ref, w_ref, x_ref, g_ref, r_ref, dx_ref, dg_ref, acc = refs
        else:
            first_ref, a_ref, w_ref, x_ref, g_ref, r_ref, dx_ref, dg_ref, acc = refs
        i, k = pl.program_id(0), pl.program_id(1)

        @pl.when(k == 0)
        def _():
            acc[...] = jnp.zeros_like(acc)

        if first is None:
            acc[...] += lax.dot_general(a_ref[...], w_ref[...], dims, preferred_element_type=F32)
        else:
            @pl.when(k == 0)
            def _():
                acc[...] += lax.dot_general(first_ref[...], w_ref[...], dims, preferred_element_type=F32)

            @pl.when(k > 0)
            def _():
                acc[...] += lax.dot_general(a_ref[...], w_ref[...], dims, preferred_element_type=F32)

        @pl.when(k == n_k - 1)
        def _():
            dx, dg = _rms_bwd(x_ref[...], g_ref[...], acc[...])
            dx_ref[...] = r_ref[...] + dx

            @pl.when(i == 0)
            def _():
                dg_ref[...] = jnp.zeros_like(dg_ref)

            dg_ref[...] += dg

    row = pl.BlockSpec((tm, d), lambda i, k: (i, 0))
    if first is None:
        a_specs, a_args = [pl.BlockSpec((tm, tk), lambda i, k: (i, k))], [dproj]
    else:
        a_specs = [pl.BlockSpec((tm, tk), lambda i, k: (i, 0)), pl.BlockSpec((tm, tk), lambda i, k: (i, jnp.maximum(k - 1, 0)))]
        a_args = [first, dproj]
    return _call(
        body, name=name, grid=(s // tm, n_k),
        in_specs=a_specs + [pl.BlockSpec((d, tk), lambda i, k: (0, k)), row, _full((1, d)), row],
        out_specs=[row, _full((1, d))],
        out_shape=[jax.ShapeDtypeStruct((s, d), F32), jax.ShapeDtypeStruct((1, d), F32)],
        scratch_shapes=[pltpu.VMEM((tm, d), F32)], args=a_args + [w, x, g, dres], exchange=exchange)


def _rg_conv(xa, before, after, cw, cb):
    return (cw[0:1, :] * _shift_rows(xa, before, after, -2) + cw[1:2, :] * _shift_rows(xa, before, after, -1)
            + cw[2:3, :] * xa + cw[3:4, :] * _shift_rows(xa, before, after, 1) + cb)


def _rg_gates(ua_h, gw_ref, gb_ref, c_h, direction, head):
    r = _sigmoid(_bdot(ua_h, gw_ref[2 * direction, head]) + gb_ref[2 * direction, head:head + 1, :])
    i = _sigmoid(_bdot(ua_h, gw_ref[2 * direction + 1, head]) + gb_ref[2 * direction + 1, head:head + 1, :])
    log_a = -c_h * r
    a = jnp.exp(log_a)
    beta_sq = -jnp.tanh(log_a) * (1.0 + a * a)
    inv_beta = lax.rsqrt(jnp.maximum(beta_sq, SMALLEST_NORMAL))
    return r, i, a, beta_sq * inv_beta, inv_beta


def even_gates_fwd(proj, conv_w, conv_b, gate_w, gate_b, lam, exchange=None):
    s = proj.shape[0]
    ts = min(2 * ROW_TILE, s)
    n_tiles = s // ts

    def body(xa_ref, xb_ref, xn_ref, cw_ref, cb_ref, gw_ref, gb_ref, lam_ref, o_ref, hf_ref, carry):
        @pl.when(pl.program_id(0) == 0)
        def _():
            carry[...] = jnp.zeros_like(carry)

        xa, before, after = _halo_load(xa_ref, xb_ref, xn_ref, n_tiles)
        ua = _rg_conv(xa, before, after, cw_ref[...], cb_ref[...])
        c = RG_C * _softplus(-lam_ref[...])
        ua_bf16 = ua.astype(BF16)
        for direction in range(2):
            for head in range(RG_HEADS):
                lanes = slice(head * RG_HEAD_DIM, (head + 1) * RG_HEAD_DIM)
                ua_h = ua[:, lanes]
                _, i, a, beta, _ = _rg_gates(ua_bf16[:, lanes], gw_ref, gb_ref, c[direction:direction + 1, lanes], direction, head)
                o_ref[2 * direction, :, lanes] = a
                o_ref[2 * direction + 1, :, lanes] = beta * (i * ua_h)
        _scan_tile(o_ref.at[0], o_ref.at[1], hf_ref, carry, False, False)

    return _call(
        body, name="even_gates_fwd", grid=(n_tiles,),
        in_specs=_halo_specs(ts, s, D_MODEL, 0) + [_full(conv_w.shape), _full(conv_b.shape), _full(gate_w.shape),
                                                   _full(gate_b.shape), _full(lam.shape)],
        out_specs=[pl.BlockSpec((4, ts, D_MODEL), lambda i: (0, i, 0)), pl.BlockSpec((ts, D_MODEL), lambda i: (i, 0))],
        out_shape=[jax.ShapeDtypeStruct((4, s, D_MODEL), F32), jax.ShapeDtypeStruct((s, D_MODEL), F32)],
        scratch_shapes=[pltpu.VMEM((SUBLANES, D_MODEL), F32)],
        args=[proj, proj, proj, conv_w, conv_b, gate_w, gate_b, lam], exchange=exchange)


def _scan_tile(a_ref, b_ref, h_ref, carry, reverse, b_times_a):
    ts, c = h_ref.shape
    n_blocks = ts // SUBLANES
    row = lax.broadcasted_iota(jnp.int32, (SUBLANES, c), 0)

    def block(j, h_in):
        r0 = pl.multiple_of((n_blocks - 1 - j if reverse else j) * SUBLANES, SUBLANES)
        a = a_ref[pl.ds(r0, SUBLANES), :]
        b = b_ref[pl.ds(r0, SUBLANES), :]
        if b_times_a:
            b = a * b
        for step in (1, 2, 4):
            shift = SUBLANES - step if reverse else step
            valid = row < SUBLANES - step if reverse else row >= step
            b = jnp.where(valid, a * pltpu.roll(b, shift, 0) + b, b)
            a = jnp.where(valid, a * pltpu.roll(a, shift, 0), a)
        h = a * h_in + b
        h_ref[pl.ds(r0, SUBLANES), :] = h
        return h[0:1, :] if reverse else h[SUBLANES - 1:SUBLANES, :]

    carry[0:1, :] = lax.fori_loop(0, n_blocks, block, carry[0:1, :])


def linear_scan(a_arr, a_idx, b_arr, b_idx, reverse, b_times_a, name, exchange=None):
    _, s, c = a_arr.shape
    ts = min(MM_TILE, s)
    n_tiles = s // ts

    def tile_of(i):
        return n_tiles - 1 - i if reverse else i

    def body(a_ref, b_ref, h_ref, carry):
        @pl.when(pl.program_id(0) == 0)
        def _():
            carry[...] = jnp.zeros_like(carry)

        _scan_tile(a_ref, b_ref, h_ref, carry, reverse, b_times_a)

    return _call(
        body, name=name, grid=(n_tiles,),
        in_specs=[pl.BlockSpec((None, ts, c), lambda i: (a_idx, tile_of(i), 0)),
                  pl.BlockSpec((None, ts, c), lambda i: (b_idx, tile_of(i), 0))],
        out_specs=pl.BlockSpec((ts, c), lambda i: (tile_of(i), 0)),
        out_shape=jax.ShapeDtypeStruct((s, c), F32),
        scratch_shapes=[pltpu.VMEM((SUBLANES, c), F32)], args=[a_arr, b_arr], exchange=exchange)


def _sc_conv(p, before, after, w):
    return w[0:1, :] * _shift_rows(p, before, after, -1) + w[1:2, :] * p + w[2:3, :] * _shift_rows(p, before, after, 1)


def even_mix_fwd(ab, hf, proj, sc_w, w_out, xres, g_post, exchange=None):
    s = proj.shape[0]
    ts = min(ROW_TILE, s)
    n_tiles = s // ts

    def tile(i):
        return n_tiles - 1 - i

    row = pl.BlockSpec((ts, D_MODEL), lambda i: (tile(i), 0))

    def col(c):
        return pl.BlockSpec((ts, D_MODEL), lambda i: (tile(i), c))

    def body(a_ref, b_ref, hf_ref, za_ref, xb_ref, xbb_ref, xbn_ref, gb_ref, gc_ref, gcb_ref, gcn_ref, zb_ref, w_ref,
             wo_ref, x_ref, g_ref, u_ref, hb_ref, y_ref, out_ref, carry):
        @pl.when(pl.program_id(0) == 0)
        def _():
            carry[...] = jnp.zeros_like(carry)

        _scan_tile(a_ref, b_ref, hb_ref, carry, True, False)
        xb, xb_before, xb_after = _halo_load(xb_ref, xbb_ref, xbn_ref, n_tiles, tile)
        gc, gc_before, gc_after = _halo_load(gc_ref, gcb_ref, gcn_ref, n_tiles, tile)
        silu_za, _ = _silu_and_grad(za_ref[...])
        silu_zb, _ = _silu_and_grad(zb_ref[...])
        u_ref[:, :D_MODEL] = ((hf_ref[...] + hb_ref[...]) * silu_za).astype(BF16)
        cv = _sc_conv(gc * xb, gc_before * xb_before, gc_after * xb_after, w_ref[...])
        u_ref[:, D_MODEL:] = (gb_ref[...] * cv * silu_zb).astype(BF16)
        y = jnp.dot(u_ref[...], wo_ref[...], preferred_element_type=F32)
        y_ref[...] = y
        out_ref[...] = x_ref[...] + _rms(y, g_ref[...])

    return _call(
        body, name="even_mix_fwd", grid=(n_tiles,),
        in_specs=[pl.BlockSpec((None, ts, D_MODEL), lambda i: (2, tile(i), 0)), pl.BlockSpec((None, ts, D_MODEL), lambda i: (3, tile(i), 0)),
                  row, col(1)] + _halo_specs(ts, s, D_MODEL, 2, tile) + [col(3)] + _halo_specs(ts, s, D_MODEL, 4, tile)
        + [col(5), _full(sc_w.shape), _full(w_out.shape), row, _full(g_post.shape)],
        out_specs=[pl.BlockSpec((ts, 2 * D_MODEL), lambda i: (tile(i), 0)), row, row, row],
        out_shape=[jax.ShapeDtypeStruct((s, 2 * D_MODEL), BF16)] + [jax.ShapeDtypeStruct((s, D_MODEL), F32)] * 3,
        scratch_shapes=[pltpu.VMEM((SUBLANES, D_MODEL), F32)],
        args=[ab, ab, hf, proj, proj, proj, proj, proj, proj, proj, proj, proj, sc_w, w_out, xres, g_post], exchange=exchange)


def even_mix_bwd(du, hf, hb, proj, sc_w, ab, exchange=None):
    s = proj.shape[0]
    ts = min(ROW_TILE, s)
    n_tiles = s // ts
    row = pl.BlockSpec((ts, D_MODEL), lambda i: (i, 0))

    def body(dya_ref, dyb_ref, dybb_ref, dybn_ref, hf_ref, hb_ref, za_ref, xb_ref, xbb_ref, xbn_ref,
             gb_ref, gbb_ref, gbn_ref, gc_ref, gcb_ref, gcn_ref, zb_ref, zbb_ref, zbn_ref, w_ref, a_ref,
             dh_ref, dp_ref, dw_ref, adj_ref, carry):
        @pl.when(pl.program_id(0) == 0)
        def _():
            carry[...] = jnp.zeros_like(carry)

        dyb, dyb_before, dyb_after = _halo_load(dyb_ref, dybb_ref, dybn_ref, n_tiles)
        xb, xb_before, xb_after = _halo_load(xb_ref, xbb_ref, xbn_ref, n_tiles)
        gb, gb_before, gb_after = _halo_load(gb_ref, gbb_ref, gbn_ref, n_tiles)
        gc, gc_before, gc_after = _halo_load(gc_ref, gcb_ref, gcn_ref, n_tiles)
        zb, zb_before, zb_after = _halo_load(zb_ref, zbb_ref, zbn_ref, n_tiles)
        w = w_ref[...]
        dya, za = dya_ref[...], za_ref[...]
        silu_za, dsilu_za = _silu_and_grad(za)
        dh_ref[...] = dya * silu_za
        _scan_tile(a_ref, dh_ref, adj_ref, carry, False, True)
        dp_ref[:, 0:D_MODEL] = (dya * (hf_ref[...] + hb_ref[...]) * dsilu_za).astype(BF16)

        silu_zb, dsilu_zb = _silu_and_grad(zb)
        p, p_before, p_after = gc * xb, gc_before * xb_before, gc_after * xb_after
        cv = _sc_conv(p, p_before, p_after, w)
        dcv = dyb * gb * silu_zb
        dcv_before = dyb_before * gb_before * _silu_and_grad(zb_before)[0]
        dcv_after = dyb_after * gb_after * _silu_and_grad(zb_after)[0]
        dpp = (w[0:1, :] * _shift_rows(dcv, dcv_before, dcv_after, 1) + w[1:2, :] * dcv
               + w[2:3, :] * _shift_rows(dcv, dcv_before, dcv_after, -1))
        dp_ref[:, D_MODEL:2 * D_MODEL] = (dpp * gc).astype(BF16)
        dp_ref[:, 2 * D_MODEL:3 * D_MODEL] = (dyb * cv * silu_zb).astype(BF16)
        dp_ref[:, 3 * D_MODEL:4 * D_MODEL] = (dpp * xb).astype(BF16)
        dp_ref[:, 4 * D_MODEL:5 * D_MODEL] = (dyb * gb * cv * dsilu_zb).astype(BF16)

        @pl.when(pl.program_id(0) == 0)
        def _():
            dw_ref[...] = jnp.zeros_like(dw_ref)

        dw_ref[0:1, :] += jnp.sum(dcv * _shift_rows(p, p_before, p_after, -1), axis=0, keepdims=True)
        dw_ref[1:2, :] += jnp.sum(dcv * p, axis=0, keepdims=True)
        dw_ref[2:3, :] += jnp.sum(dcv * _shift_rows(p, p_before, p_after, 1), axis=0, keepdims=True)

    return _call(
        body, name="even_mix_bwd", grid=(n_tiles,),
        in_specs=[row] + _halo_specs(ts, s, D_MODEL, 1) + [row, row, pl.BlockSpec((ts, D_MODEL), lambda i: (i, 1))]
        + _halo_specs(ts, s, D_MODEL, 2) + _halo_specs(ts, s, D_MODEL, 3) + _halo_specs(ts, s, D_MODEL, 4)
        + _halo_specs(ts, s, D_MODEL, 5) + [_full(sc_w.shape), pl.BlockSpec((None, ts, D_MODEL), lambda i: (2, i, 0))],
        out_specs=[row, pl.BlockSpec((ts, 5 * D_MODEL), lambda i: (i, 0)), _full(sc_w.shape), row],
        out_shape=[jax.ShapeDtypeStruct((s, D_MODEL), F32), jax.ShapeDtypeStruct((s, 5 * D_MODEL), BF16),
                   jax.ShapeDtypeStruct(sc_w.shape, F32), jax.ShapeDtypeStruct((s, D_MODEL), F32)],
        scratch_shapes=[pltpu.VMEM((SUBLANES, D_MODEL), F32)],
        args=[du, du, du, du, hf, hb, proj, *([proj] * 12), sc_w, ab], exchange=exchange)


def even_gates_bwd(proj, adj_f, adj_b, hf, hb, dh, conv_w, conv_b, gate_w, gate_b, lam, exchange=None):
    s = proj.shape[0]
    ts = min(2 * ROW_TILE, s)
    n_tiles = s // ts
    row = pl.BlockSpec((ts, D_MODEL), lambda i: (i, 0))

    def body(xa_ref, xab_ref, xan_ref, af_ref, afb_ref, afn_ref, ab_ref, abb_ref, abn_ref,
             hf_ref, hfb_ref, hfn_ref, hb_ref, hbb_ref, hbn_ref, dh_ref,
             cw_ref, cb_ref, gw_ref, gb_ref, lam_ref, dua_ref, dgw_ref, dgb_ref, dlam_ref):
        @pl.when(pl.program_id(0) == 0)
        def _():
            dgw_ref[...] = jnp.zeros_like(dgw_ref)
            dgb_ref[...] = jnp.zeros_like(dgb_ref)
            dlam_ref[...] = jnp.zeros_like(dlam_ref)

        xa, before, after = _halo_load(xa_ref, xab_ref, xan_ref, n_tiles)
        ua = _rg_conv(xa, before, after, cw_ref[...], cb_ref[...])
        lam_v = lam_ref[...]
        c = RG_C * _softplus(-lam_v)
        dc_dlam = -RG_C * _sigmoid(-lam_v)
        dh = dh_ref[...]
        adj = (_halo_load(af_ref, afb_ref, afn_ref, n_tiles), _halo_load(ab_ref, abb_ref, abn_ref, n_tiles))
        hs = (_halo_load(hf_ref, hfb_ref, hfn_ref, n_tiles), _halo_load(hb_ref, hbb_ref, hbn_ref, n_tiles))
        dua = jnp.zeros_like(ua)
        ua_bf16 = ua.astype(BF16)
        for direction in range(2):
            step = 1 if direction == 0 else -1
            g = dh + _shift_rows(*adj[direction], step)
            da_all = g * _shift_rows(*hs[direction], -step)
            dua_parts = []
            for head in range(RG_HEADS):
                lanes = slice(head * RG_HEAD_DIM, (head + 1) * RG_HEAD_DIM)
                ua_h = ua[:, lanes]
                c_h = c[direction:direction + 1, lanes]
                ua_hb = ua_bf16[:, lanes]
                r, i, a, beta, inv_beta = _rg_gates(ua_hb, gw_ref, gb_ref, c_h, direction, head)
                db_beta = g[:, lanes] * beta
                d_i = db_beta * ua_h
                dbeta = g[:, lanes] * (i * ua_h)
                dlog_a = (da_all[:, lanes] - dbeta * a * inv_beta) * a
                dpr = -c_h * dlog_a * r * (1.0 - r)
                dpi = d_i * i * (1.0 - i)
                dpr_b, dpi_b = dpr.astype(BF16), dpi.astype(BF16)
                dua_parts.append(db_beta * i + _bdot_nt(dpr_b, gw_ref[2 * direction, head])
                                 + _bdot_nt(dpi_b, gw_ref[2 * direction + 1, head]))
                dgw_ref[2 * direction, head] += _bdot_tn(ua_hb, dpr_b)
                dgw_ref[2 * direction + 1, head] += _bdot_tn(ua_hb, dpi_b)
                dgb_ref[2 * direction, head:head + 1, :] += jnp.sum(dpr, axis=0, keepdims=True)
                dgb_ref[2 * direction + 1, head:head + 1, :] += jnp.sum(dpi, axis=0, keepdims=True)
                dlam_ref[direction:direction + 1, lanes] += (
                    jnp.sum(-r * dlog_a, axis=0, keepdims=True) * dc_dlam[direction:direction + 1, lanes])
            dua = dua + jnp.concatenate(dua_parts, axis=1)
        dua_ref[...] = dua

    return _call(
        body, name="even_gates_bwd", grid=(n_tiles,),
        in_specs=_halo_specs(ts, s, D_MODEL, 0) * 5 + [row] + [_full(conv_w.shape), _full(conv_b.shape), _full(gate_w.shape),
                                                             _full(gate_b.shape), _full(lam.shape)],
        out_specs=[row, _full(gate_w.shape), _full(gate_b.shape), _full(lam.shape)],
        out_shape=[jax.ShapeDtypeStruct((s, D_MODEL), F32), jax.ShapeDtypeStruct(gate_w.shape, F32),
                   jax.ShapeDtypeStruct(gate_b.shape, F32), jax.ShapeDtypeStruct(lam.shape, F32)],
        args=[proj, proj, proj, adj_f, adj_f, adj_f, adj_b, adj_b, adj_b, hf, hf, hf, hb, hb, hb, dh, conv_w, conv_b, gate_w,
              gate_b, lam], exchange=exchange)


def rg_conv_bwd(dua, proj, conv_w, exchange=None):
    s = proj.shape[0]
    ts = min(2 * ROW_TILE, s)
    n_tiles = s // ts

    def body(du_ref, dub_ref, dun_ref, xa_ref, xab_ref, xan_ref, cw_ref, dp_ref, dw_ref, db_ref):
        @pl.when(pl.program_id(0) == 0)
        def _():
            dw_ref[...] = jnp.zeros_like(dw_ref)
            db_ref[...] = jnp.zeros_like(db_ref)

        dua, dua_before, dua_after = _halo_load(du_ref, dub_ref, dun_ref, n_tiles)
        xa, xa_before, xa_after = _halo_load(xa_ref, xab_ref, xan_ref, n_tiles)
        cw = cw_ref[...]
        dxa = (cw[0:1, :] * _shift_rows(dua, dua_before, dua_after, 2) + cw[1:2, :] * _shift_rows(dua, dua_before, dua_after, 1)
               + cw[2:3, :] * dua + cw[3:4, :] * _shift_rows(dua, dua_before, dua_after, -1))
        dp_ref[...] = dxa.astype(BF16)
        for tap, offset in enumerate((-2, -1, 0, 1)):
            shifted = xa if offset == 0 else _shift_rows(xa, xa_before, xa_after, offset)
            dw_ref[tap:tap + 1, :] += jnp.sum(dua * shifted, axis=0, keepdims=True)
        db_ref[...] += jnp.sum(dua, axis=0, keepdims=True)

    return _call(
        body, name="rg_conv_bwd", grid=(n_tiles,),
        in_specs=_halo_specs(ts, s, D_MODEL, 0) * 2 + [_full(conv_w.shape)],
        out_specs=[pl.BlockSpec((ts, D_MODEL), lambda i: (i, 0)), _full(conv_w.shape), _full((1, D_MODEL))],
        out_shape=[jax.ShapeDtypeStruct((s, D_MODEL), BF16), jax.ShapeDtypeStruct(conv_w.shape, F32),
                   jax.ShapeDtypeStruct((1, D_MODEL), F32)],
        args=[dua, dua, dua, proj, proj, proj, conv_w], exchange=exchange)


def _split3(x):
    x1 = x.astype(BF16)
    rest = x - x1.astype(F32)
    x2 = rest.astype(BF16)
    return x1, x2, (rest - x2.astype(F32)).astype(BF16)


def _chunk_sum_matrix(t, reverse, transpose):
    i = lax.broadcasted_iota(jnp.int32, (t, t), 0)
    j = lax.broadcasted_iota(jnp.int32, (t, t), 1)
    if transpose:
        i, j = j, i
    same = (i // GLA_CHUNK) == (j // GLA_CHUNK)
    return jnp.where(same & ((j >= i) if reverse else (j <= i)), 1.0, 0.0).astype(BF16)


def _exact_dot(m, x):
    return sum(jnp.dot(m, part, preferred_element_type=F32) for part in _split3(x))


def _chunk_mask(t, reverse):
    i = lax.broadcasted_iota(jnp.int32, (t, t), 0)
    j = lax.broadcasted_iota(jnp.int32, (t, t), 1)
    return ((i // GLA_CHUNK) == (j // GLA_CHUNK)) & ((j >= i) if reverse else (j <= i))


def _chunk_rows(c):
    return slice(c * GLA_CHUNK, (c + 1) * GLA_CHUNK)


def _gla_gate(lr, wg, bg):
    z = _bdot(lr, wg) + bg
    log_alpha = (jnp.minimum(z, 0.0) - jnp.log(1.0 + jnp.exp(-jnp.abs(z)))) * (1.0 / GLA_NORMALIZER)
    return z, log_alpha


def _gla_tile_terms(q, k, bcum, reverse):
    n_chunks = q.shape[0] // GLA_CHUNK
    totals = []
    for c in range(n_chunks):
        edge = c * GLA_CHUNK if reverse else (c + 1) * GLA_CHUNK - 1
        totals.append(bcum[edge:edge + 1, :])
    btot = jnp.concatenate([jnp.broadcast_to(total, (GLA_CHUNK, total.shape[1])) for total in totals], axis=0)
    e_pos, e_neg, e_st = jnp.exp(bcum), jnp.exp(-bcum), jnp.exp(btot - bcum)
    return q * (GLA_DK ** -0.5) * e_pos, k * e_neg, k * e_st, e_pos, e_neg, e_st, [jnp.exp(total) for total in totals]


def _gla_specs(t, n_tiles, reverse_order):
    def tile(i):
        return n_tiles - 1 - i if reverse_order else i

    return tile, [
        pl.BlockSpec((t, GLA_KEY), lambda i: (tile(i), 0)),
        pl.BlockSpec((t, GLA_KEY), lambda i: (tile(i), 1)),
        pl.BlockSpec((t, D_MODEL), lambda i: (tile(i), 1)),
        pl.BlockSpec((t, LANES), lambda i: (tile(i), (ODD_IN_PAD - LANES) // LANES)),
    ]


def gla_fwd(proj, wg, bg, reverse, o_other=None, gnorm=None, post=None):
    s = proj.shape[0]
    t = min(GLA_TILE, s)
    n_tiles = s // t
    n_chunks = t // GLA_CHUNK
    final = o_other is not None
    tile, specs = _gla_specs(t, n_tiles, reverse)

    def body(*refs):
        if final:
            (q_ref, k_ref, v_ref, lr_ref, wg_ref, bg_ref, oo_ref, r_ref, gn_ref, wo_ref, x_ref, gp_ref, t_ref,
             osum_ref, u_ref, st_ref, y_ref, dout_ref, loss_ref, state) = refs
        else:
            q_ref, k_ref, v_ref, lr_ref, wg_ref, bg_ref, o_ref, st_ref, state = refs
            osum_ref = o_ref

        @pl.when(pl.program_id(0) == 0)
        def _():
            state[...] = jnp.zeros_like(state)

        _, log_alpha = _gla_gate(lr_ref[...], wg_ref[...], bg_ref[...])
        bcum = _exact_dot(_chunk_sum_matrix(t, reverse, False), log_alpha)
        q, k, v = q_ref[...], k_ref[...], v_ref[...]
        q_in, k_in, k_st, _, _, _, decays = _gla_tile_terms(q, k, bcum, reverse)
        mask = _chunk_mask(t, reverse)
        order = list(range(n_chunks))[::-1] if reverse else list(range(n_chunks))
        intra, increments = [], []
        for head in range(GLA_HEADS):
            kl = slice(head * GLA_DK, (head + 1) * GLA_DK)
            vl = slice(head * GLA_DV, (head + 1) * GLA_DV)
            scores = jnp.where(mask, _bdot_nt(q_in[:, kl], k_in[:, kl]), 0.0)
            intra.append(_bdot(scores, v[:, vl]))
            increments.append([_bdot_tn(v[_chunk_rows(c), vl], k_st[_chunk_rows(c), kl]) for c in range(n_chunks)])
        for head in range(GLA_HEADS):
            kl = slice(head * GLA_DK, (head + 1) * GLA_DK)
            vl = slice(head * GLA_DV, (head + 1) * GLA_DV)
            running = state[head]
            before = [None] * n_chunks
            for c in order:
                before[c] = running
                st_ref[c, head] = running
                running = running * decays[c][:, kl] + increments[head][c]
            state[head] = running
            inter = [_bdot_nt(q_in[_chunk_rows(c), kl], before[c]) for c in range(n_chunks)]
            osum_ref[:, vl] = intra[head] + jnp.concatenate(inter, axis=0)
        if final:
            osum = osum_ref[...] + oo_ref[...]
            osum_ref[...] = osum
            silu_r, _ = _silu_and_grad(r_ref[...])
            gn = gn_ref[...]
            for head in range(GLA_HEADS):
                vl = slice(head * GLA_DV, (head + 1) * GLA_DV)
                u_ref[:, vl] = (_rms(osum[:, vl], gn[:, vl]) * silu_r[:, vl]).astype(BF16)

            @pl.when(pl.program_id(0) == 0)
            def _():
                loss_ref[...] = jnp.zeros_like(loss_ref)

            y = jnp.dot(u_ref[...], wo_ref[...], preferred_element_type=F32)
            y_ref[...] = y
            diff = x_ref[...] + _rms(y, gp_ref[...]) - t_ref[...]
            dout_ref[...] = diff * (1.0 / D_MODEL)
            loss_ref[...] += 0.5 * jnp.sum(jnp.mean(diff * diff, axis=-1, keepdims=True))

    row = pl.BlockSpec((t, D_MODEL), lambda i: (tile(i), 0))
    st_spec = pl.BlockSpec((n_chunks, GLA_HEADS, GLA_DV, GLA_DK), lambda i: (tile(i), 0, 0, 0))
    st_shape = jax.ShapeDtypeStruct((s // GLA_CHUNK, GLA_HEADS, GLA_DV, GLA_DK), F32)
    in_specs = specs + [_full(wg.shape), _full(bg.shape)]
    args = [proj, proj, proj, proj, wg, bg]
    if final:
        w_out, xres, g_post, target = post
        in_specs += [row, pl.BlockSpec((t, D_MODEL), lambda i: (tile(i), 2)), _full(gnorm.shape), _full(w_out.shape), row,
                     _full(g_post.shape), row]
        args += [o_other, proj, gnorm, w_out, xres, g_post, target]
        out_specs = [row, row, st_spec, row, row, _full((SUBLANES, LANES))]
        out_shape = [jax.ShapeDtypeStruct((s, D_MODEL), F32), jax.ShapeDtypeStruct((s, D_MODEL), BF16), st_shape,
                     jax.ShapeDtypeStruct((s, D_MODEL), F32), jax.ShapeDtypeStruct((s, D_MODEL), F32),
                     jax.ShapeDtypeStruct((SUBLANES, LANES), F32)]
    else:
        out_specs = [row, st_spec]
        out_shape = [jax.ShapeDtypeStruct((s, D_MODEL), F32), st_shape]
    return pl.pallas_call(
        body, name="gla_fwd_rev" if reverse else "gla_fwd", grid=(n_tiles,), in_specs=in_specs, out_specs=out_specs,
        out_shape=out_shape, scratch_shapes=[pltpu.VMEM((GLA_HEADS, GLA_DV, GLA_DK), F32)], compiler_params=_params(1),
    )(*args)


def gla_bwd(proj, wg, bg, do, states, reverse, first=None):
    s = proj.shape[0]
    t = min(GLA_TILE, s)
    n_tiles = s // t
    n_chunks = t // GLA_CHUNK
    final = first is not None
    tile, specs = _gla_specs(t, n_tiles, not reverse)

    def body(*refs):
        if final:
            (q_ref, k_ref, v_ref, lr_ref, wg_ref, bg_ref, do_ref, st_ref, dqkv1_ref, dlr1_ref, dr_ref,
             dp_ref, dwg_ref, dbg_ref, dstate, dqkv, dbc, dbt) = refs
        else:
            (q_ref, k_ref, v_ref, lr_ref, wg_ref, bg_ref, do_ref, st_ref,
             dqkv, dlr_ref, dwg_ref, dbg_ref, dstate, dbc, dbt) = refs

        @pl.when(pl.program_id(0) == 0)
        def _():
            dstate[...] = jnp.zeros_like(dstate)
            dwg_ref[...] = jnp.zeros_like(dwg_ref)
            dbg_ref[...] = jnp.zeros_like(dbg_ref)

        lr, wg_v = lr_ref[...], wg_ref[...]
        z, log_alpha = _gla_gate(lr, wg_v, bg_ref[...])
        bcum = _exact_dot(_chunk_sum_matrix(t, reverse, False), log_alpha)
        q, k, v, do_v = q_ref[...], k_ref[...], v_ref[...], do_ref[...]
        q_in, k_in, k_st, e_pos, e_neg, e_st, decays = _gla_tile_terms(q, k, bcum, reverse)
        mask = _chunk_mask(t, reverse)
        order = list(range(n_chunks)) if reverse else list(range(n_chunks))[::-1]
        q_b, k_b, ks_b, v_b, do_b = (a.astype(BF16) for a in (q_in, k_in, k_st, v, do_v))
        dq_intra, dk_intra, dv_intra, increments = [], [], [], []
        for head in range(GLA_HEADS):
            kl = slice(head * GLA_DK, (head + 1) * GLA_DK)
            vl = slice(head * GLA_DV, (head + 1) * GLA_DV)
            scores = jnp.where(mask, _bdot_nt(q_b[:, kl], k_b[:, kl]), 0.0).astype(BF16)
            dscores = jnp.where(mask, _bdot_nt(do_b[:, vl], v_b[:, vl]), 0.0).astype(BF16)
            dv_intra.append(_bdot_tn(scores, do_b[:, vl]))
            dq_intra.append(_bdot(dscores, k_b[:, kl]))
            dk_intra.append(_bdot_tn(dscores, q_b[:, kl]))
            increments.append([_bdot_tn(do_b[_chunk_rows(c), vl], q_b[_chunk_rows(c), kl]) for c in range(n_chunks)])
        after_all, ddecay_all = [], []
        for head in range(GLA_HEADS):
            kl = slice(head * GLA_DK, (head + 1) * GLA_DK)
            running = dstate[head]
            after, ddecay = [None] * n_chunks, [None] * n_chunks
            for c in order:
                after[c] = running
                ddecay[c] = jnp.sum(running * st_ref[c, head], axis=0, keepdims=True)
                running = running * decays[c][:, kl] + increments[head][c]
            dstate[head] = running
            after_all.append(after)
            ddecay_all.append(ddecay)
        for head in range(GLA_HEADS):
            kl = slice(head * GLA_DK, (head + 1) * GLA_DK)
            vl = slice(head * GLA_DV, (head + 1) * GLA_DV)
            after, ddecay = after_all[head], ddecay_all[head]
            dq_inter = jnp.concatenate([_bdot(do_b[_chunk_rows(c), vl], st_ref[c, head]) for c in range(n_chunks)], axis=0)
            dv_inter = jnp.concatenate([_bdot_nt(ks_b[_chunk_rows(c), kl], after[c]) for c in range(n_chunks)], axis=0)
            dk_st = jnp.concatenate([_bdot(v_b[_chunk_rows(c), vl], after[c]) for c in range(n_chunks)], axis=0)
            dq_in = dq_intra[head] + dq_inter
            ks_h = k_st[:, kl]
            dqkv[:, 2 * GLA_KEY + head * GLA_DV:2 * GLA_KEY + (head + 1) * GLA_DV] = dv_intra[head] + dv_inter
            dqkv[:, kl] = dq_in * (GLA_DK ** -0.5) * e_pos[:, kl]
            dqkv[:, GLA_KEY + head * GLA_DK:GLA_KEY + (head + 1) * GLA_DK] = dk_intra[head] * e_neg[:, kl] + dk_st * e_st[:, kl]
            dbc[:, kl] = dq_in * q_in[:, kl] - dk_intra[head] * k_in[:, kl] - dk_st * ks_h
            weighted = dk_st * ks_h
            for c in range(n_chunks):
                dbtot = jnp.sum(weighted[_chunk_rows(c)], axis=0, keepdims=True) + ddecay[c] * decays[c][:, kl]
                dbt[_chunk_rows(c), kl] = jnp.broadcast_to(dbtot, (GLA_CHUNK, GLA_DK))
        dlog_alpha = _exact_dot(_chunk_sum_matrix(t, reverse, True), dbc[...]) + dbt[...]
        dz = dlog_alpha * _sigmoid(-z) * (1.0 / GLA_NORMALIZER)
        dlr = _bdot_nt(dz, wg_v)
        dwg_ref[...] += _bdot_tn(lr, dz)
        dbg_ref[...] += jnp.sum(dz, axis=0, keepdims=True)
        if final:
            dp_ref[:, :2 * D_MODEL] = (dqkv[...] + dqkv1_ref[...]).astype(BF16)
            dp_ref[:, 2 * D_MODEL:3 * D_MODEL] = dr_ref[...]
            dp_ref[:, 3 * D_MODEL:] = (dlr + dlr1_ref[...]).astype(BF16)
        else:
            dlr_ref[...] = dlr

    row = pl.BlockSpec((t, D_MODEL), lambda i: (tile(i), 0))
    wide = pl.BlockSpec((t, 2 * D_MODEL), lambda i: (tile(i), 0))
    narrow = pl.BlockSpec((t, LANES), lambda i: (tile(i), 0))
    st_spec = pl.BlockSpec((n_chunks, GLA_HEADS, GLA_DV, GLA_DK), lambda i: (tile(i), 0, 0, 0))
    in_specs = specs + [_full(wg.shape), _full(bg.shape), row, st_spec]
    args = [proj, proj, proj, proj, wg, bg, do, states]
    acc_specs = [_full(wg.shape), _full(bg.shape)]
    acc_shapes = [jax.ShapeDtypeStruct(wg.shape, F32), jax.ShapeDtypeStruct(bg.shape, F32)]
    scratch = [pltpu.VMEM((GLA_HEADS, GLA_DV, GLA_DK), F32)]
    work = [pltpu.VMEM((t, GLA_KEY), F32), pltpu.VMEM((t, GLA_KEY), F32)]
    if final:
        in_specs += [wide, narrow, row]
        args += list(first)
        out_specs = [pl.BlockSpec((t, ODD_IN_PAD), lambda i: (tile(i), 0))] + acc_specs
        out_shape = [jax.ShapeDtypeStruct((s, ODD_IN_PAD), BF16)] + acc_shapes
        scratch += [pltpu.VMEM((t, 2 * D_MODEL), F32)] + work
    else:
        out_specs = [wide, narrow] + acc_specs
        out_shape = [jax.ShapeDtypeStruct((s, 2 * D_MODEL), F32), jax.ShapeDtypeStruct((s, LANES), F32)] + acc_shapes
        scratch += work
    return pl.pallas_call(
        body, name="gla_bwd_rev" if reverse else "gla_bwd", grid=(n_tiles,), in_specs=in_specs, out_specs=out_specs,
        out_shape=out_shape, scratch_shapes=scratch, compiler_params=_params(1),
    )(*args)


def pair_sum(grad, from_sibling):
    n_chips, r, w = from_sibling.shape

    def body(even_ref, odd_ref, sib_ref, o_ref):
        mine = jnp.where(lax.axis_index("c") == 1, odd_ref[...], even_ref[...])
        o_ref[...] = (mine.astype(F32) + sib_ref[...].astype(F32)).astype(o_ref.dtype)

    return pl.pallas_call(
        body, name="pair_sum", grid=(n_chips,),
        in_specs=[pl.BlockSpec((r, w), lambda k: (0, 2 * k)), pl.BlockSpec((r, w), lambda k: (0, 2 * k + 1)),
                  pl.BlockSpec((None, r, w), lambda k: (k, 0, 0))],
        out_specs=pl.BlockSpec((None, r, w), lambda k: (k, 0, 0)),
        out_shape=jax.ShapeDtypeStruct(from_sibling.shape, from_sibling.dtype), compiler_params=_params(1),
    )(grad, grad, from_sibling)


def _adamw_update(g, w, m, v):
    new_m = ADAM_B1 * m + (1.0 - ADAM_B1) * g
    new_v = ADAM_B2 * v + (1.0 - ADAM_B2) * (g * g)
    m_hat = new_m / (1.0 - ADAM_B1 ** ADAM_STEP)
    v_hat = new_v / (1.0 - ADAM_B2 ** ADAM_STEP)
    return -ADAM_LR * (m_hat / (jnp.sqrt(v_hat) + ADAM_EPS) + ADAM_WD * w), new_m, new_v


def sum_parts(parts, name):
    _, r, c = parts.shape

    def body(p_ref, o_ref):
        total = p_ref[0].astype(F32)
        for j in range(1, N_DEV):
            total = total + p_ref[j].astype(F32)
        o_ref[...] = total

    return pl.pallas_call(body, name=name, in_specs=[_full(parts.shape)], out_specs=_full((r, c)), grid=(1,),
                          out_shape=jax.ShapeDtypeStruct((r, c), F32), compiler_params=_params(1))(parts)


def adamw(parts, w, m, v, name, exchange=None):
    n, r, c = parts.shape
    tr = r
    while tr * c * 4 > ADAMW_BLOCK_BYTES and tr % (2 * SUBLANES) == 0:
        tr //= 2

    def body(p_ref, w_ref, m_ref, v_ref, g_ref, d_ref, nm_ref, nv_ref):
        g = p_ref[0].astype(F32)
        for j in range(1, n):
            g = g + p_ref[j].astype(F32)
        g_ref[...] = g
        d_ref[...], nm_ref[...], nv_ref[...] = _adamw_update(g, w_ref[...], m_ref[...], v_ref[...])

    row = pl.BlockSpec((tr, c), lambda i: (i, 0))
    return _call(
        body, name=name, grid=(r // tr,),
        in_specs=[pl.BlockSpec((n, tr, c), lambda i: (0, i, 0)), row, row, row], out_specs=[row] * 4,
        out_shape=[jax.ShapeDtypeStruct((r, c), F32)] * 4, args=[parts, w, m, v], exchange=exchange)


def _small_views(shape):
    if len(shape) == 2:
        return [((slice(None), slice(None)), (slice(None), slice(None)))]
    if len(shape) == 3:
        return [((slice(None), slice(None)), (0,))]
    rows = shape[2]
    return [((slice(k * rows, (k + 1) * rows), slice(None)), (0, k)) for k in range(shape[1])]


def adamw_small(landings, w, m, v):
    names = list(landings)
    n = len(names)
    shapes = [w[name].shape for name in names]

    def body(*refs):
        land, ws, ms, vs = refs[:n], refs[n:2 * n], refs[2 * n:3 * n], refs[3 * n:4 * n]
        outs = [refs[(4 + k) * n:(5 + k) * n] for k in range(4)]
        for k in range(n):
            total = land[k][0]
            for j in range(1, N_DEV):
                total = total + land[k][j]
            for rows, at in _small_views(shapes[k]):
                g = total[rows]
                outs[0][k][at] = g
                outs[1][k][at], outs[2][k][at], outs[3][k][at] = _adamw_update(g, ws[k][at], ms[k][at], vs[k][at])

    blocks = [_full(sh) for sh in shapes]
    outs = pl.pallas_call(
        body, name="adamw_small", grid=(1,),
        in_specs=[_full(landings[name].shape) for name in names] + blocks * 3, out_specs=blocks * 4,
        out_shape=[jax.ShapeDtypeStruct(sh, F32) for sh in shapes] * 4, compiler_params=_params(1),
    )(*[landings[name] for name in names], *[src[name] for src in (w, m, v) for name in names])
    return [dict(zip(names, outs[k * n:(k + 1) * n])) for k in range(4)]


def adamw_replicated(land_vec, land_gate_b, land_loss, names, w, m, v, gate_b):
    n = len(names)

    def body(*refs):
        vec_ref, gb_ref, loss_ref = refs[:3]
        ws, ms, vs = refs[3:3 + n], refs[3 + n:3 + 2 * n], refs[3 + 2 * n:3 + 3 * n]
        gw_ref, gm_ref, gv_ref = refs[3 + 3 * n:6 + 3 * n]
        outs = refs[6 + 3 * n:]
        vec, gb, loss = vec_ref[0], gb_ref[0], loss_ref[0]
        for j in range(1, N_DEV):
            vec, gb, loss = vec + vec_ref[j], gb + gb_ref[j], loss + loss_ref[j]
        for k in range(n):
            g = vec[k:k + 1, :]
            outs[k][...] = g
            outs[n + k][...], outs[2 * n + k][...], outs[3 * n + k][...] = _adamw_update(g, ws[k][...], ms[k][...], vs[k][...])
        outs[4 * n][...] = gb
        outs[4 * n + 1][...], outs[4 * n + 2][...], outs[4 * n + 3][...] = _adamw_update(gb, gw_ref[...], gm_ref[...], gv_ref[...])
        outs[4 * n + 4][...] = loss

    vec_block, gb_block = _full((1, D_MODEL)), _full(gate_b[0].shape)
    outs = pl.pallas_call(
        body, name="adamw_replicated", grid=(1,),
        in_specs=[_full(land_vec.shape), _full(land_gate_b.shape), _full(land_loss.shape)] + [vec_block] * (3 * n) + [gb_block] * 3,
        out_specs=[vec_block] * (4 * n) + [gb_block] * 4 + [_full(land_loss.shape[1:])],
        out_shape=[jax.ShapeDtypeStruct((1, D_MODEL), F32)] * (4 * n) + [jax.ShapeDtypeStruct(gate_b[0].shape, F32)] * 4
        + [jax.ShapeDtypeStruct(land_loss.shape[1:], F32)],
        compiler_params=_params(1),
    )(land_vec, land_gate_b, land_loss, *[src[name] for src in (w, m, v) for name in names], *gate_b)
    results = {name: [outs[k * n + i] for k in range(4)] for i, name in enumerate(names)}
    return results, outs[4 * n:4 * n + 4], outs[4 * n + 4]


SMALL_SHARDED = ("rg_conv_w", "rg_lambda", "sc_conv_w", "odd_norm_pre", "odd_norm_post", "gla_b_gate", "gla_norm_g", "gla_w_gate_lr")
SMALL_ROWS = {"rg_conv_w": (0, 4), "rg_lambda": (4, 2), "sc_conv_w": (6, 3), "odd_norm_pre": (9, 1), "odd_norm_post": (10, 1),
              "gla_b_gate": (11, 2), "gla_norm_g": (13, 1), "gla_w_gate_lr": (16, 32)}


def _pack_small(shards):
    pieces, at = [], 0
    for name in SMALL_SHARDED:
        start, rows = SMALL_ROWS[name]
        if start > at:
            pieces.append(jnp.zeros((start - at, LANES), F32))
        a = shards[name].reshape(rows, -1)
        pieces.append(jnp.pad(a, ((0, 0), (0, LANES - a.shape[1]))))
        at = start + rows
    return jnp.concatenate(pieces, axis=0)


def _unpack_gathered(g):
    def cols(name, width):
        start, rows = SMALL_ROWS[name]
        return jnp.transpose(g[:, start:start + rows, :width], (1, 0, 2)).reshape(rows, N_DEV * width)

    w_lr = cols("gla_w_gate_lr", GLA_KEY // N_DEV).reshape(2, GLA_RANK, GLA_KEY)
    return dict(rg_conv_w=cols("rg_conv_w", LANES), rg_lambda=cols("rg_lambda", LANES), sc_conv_w=cols("sc_conv_w", LANES),
                odd_norm_pre=cols("odd_norm_pre", LANES), odd_norm_post=cols("odd_norm_post", LANES),
                gla_b_gate=cols("gla_b_gate", GLA_KEY // N_DEV), gla_norm_g=cols("gla_norm_g", GLA_DV // N_DEV), gla_w_gate_lr=w_lr)


def _blocks_along_columns(a, rows):
    return jnp.transpose(a.reshape(rows, N_DEV, -1), (1, 0, 2))


def kernel(x, even_norm_pre, even_norm_post, even_w_in, rg_conv_w, rg_conv_b, rg_gate_w, rg_gate_b, rg_lambda, sc_conv_w, even_w_out, odd_norm_pre, odd_norm_post, odd_w_in, gla_w_gate_lr, gla_b_gate, gla_norm_g, odd_w_out, loss_target, m_even_norm_pre, m_even_norm_post, m_even_w_in, m_rg_conv_w, m_rg_conv_b, m_rg_gate_w, m_rg_gate_b, m_rg_lambda, m_sc_conv_w, m_even_w_out, m_odd_norm_pre, m_odd_norm_post, m_odd_w_in, m_gla_w_gate_lr, m_gla_b_gate, m_gla_norm_g, m_odd_w_out, v_even_norm_pre, v_even_norm_post, v_even_w_in, v_rg_conv_w, v_rg_conv_b, v_rg_gate_w, v_rg_gate_b, v_rg_lambda, v_sc_conv_w, v_even_w_out, v_odd_norm_pre, v_odd_norm_post, v_odd_w_in, v_gla_w_gate_lr, v_gla_b_gate, v_gla_norm_g, v_odd_w_out):
    weights = dict(even_norm_pre=even_norm_pre, even_norm_post=even_norm_post, even_w_in=even_w_in, rg_conv_w=rg_conv_w,
                   rg_conv_b=rg_conv_b, rg_gate_w=rg_gate_w, rg_gate_b=rg_gate_b, rg_lambda=rg_lambda, sc_conv_w=sc_conv_w,
                   even_w_out=even_w_out, odd_norm_pre=odd_norm_pre, odd_norm_post=odd_norm_post, odd_w_in=odd_w_in,
                   gla_w_gate_lr=gla_w_gate_lr, gla_b_gate=gla_b_gate, gla_norm_g=gla_norm_g, odd_w_out=odd_w_out)
    m_in = dict(even_norm_pre=m_even_norm_pre, even_norm_post=m_even_norm_post, even_w_in=m_even_w_in, rg_conv_w=m_rg_conv_w,
                rg_conv_b=m_rg_conv_b, rg_gate_w=m_rg_gate_w, rg_gate_b=m_rg_gate_b, rg_lambda=m_rg_lambda, sc_conv_w=m_sc_conv_w,
                even_w_out=m_even_w_out, odd_norm_pre=m_odd_norm_pre, odd_norm_post=m_odd_norm_post, odd_w_in=m_odd_w_in,
                gla_w_gate_lr=m_gla_w_gate_lr, gla_b_gate=m_gla_b_gate, gla_norm_g=m_gla_norm_g, odd_w_out=m_odd_w_out)
    v_in = dict(even_norm_pre=v_even_norm_pre, even_norm_post=v_even_norm_post, even_w_in=v_even_w_in, rg_conv_w=v_rg_conv_w,
                rg_conv_b=v_rg_conv_b, rg_gate_w=v_rg_gate_w, rg_gate_b=v_rg_gate_b, rg_lambda=v_rg_lambda, sc_conv_w=v_sc_conv_w,
                even_w_out=v_even_w_out, odd_norm_pre=v_odd_norm_pre, odd_norm_post=v_odd_norm_post, odd_w_in=v_odd_w_in,
                gla_w_gate_lr=v_gla_w_gate_lr, gla_b_gate=v_gla_b_gate, gla_norm_g=v_gla_norm_g, odd_w_out=v_odd_w_out)
    names = list(weights)
    shapes = {n: weights[n].shape for n in names}
    xs = x[0]
    tgt = loss_target[0]

    proj_e, h_e, w_in_e, small_all = gather_matmul(xs, even_norm_pre, even_w_in[0].astype(BF16),
                                                   _pack_small({n: weights[n][0] for n in SMALL_SHARDED}), 2 * MM_TILE)
    small = _unpack_gathered(small_all)
    gate_w = rg_gate_w[0].reshape(4, RG_HEADS, RG_HEAD_DIM, RG_HEAD_DIM).astype(BF16)
    gate_b = rg_gate_b[0].reshape(4, RG_HEADS, RG_HEAD_DIM)
    conv_b = rg_conv_b
    wg_pad = [jnp.pad(small["gla_w_gate_lr"][d], ((GLA_RANK * d, LANES - GLA_RANK * (d + 1)), (0, 0))).astype(BF16) for d in range(2)]
    bg = [small["gla_b_gate"][d:d + 1] for d in range(2)]
    gnorm = jnp.tile(small["gla_norm_g"], (1, GLA_HEADS))

    half = D_MODEL // 2
    behind_gates = Exchange()
    behind_gates.gather(even_w_out[0].astype(BF16), via_sibling=True)
    behind_gates.gather(odd_w_in[0, :half].astype(BF16), via_sibling=True)
    (ab, hf), (w_out_e, w_in_o_top) = even_gates_fwd(proj_e, small["rg_conv_w"], conv_b, gate_w, gate_b, small["rg_lambda"],
                                                     exchange=behind_gates)
    w_out_e = w_out_e.reshape(2 * D_MODEL, D_MODEL)
    behind_mix_fwd = Exchange()
    behind_mix_fwd.gather(odd_w_in[0, half:].astype(BF16), via_sibling=True)
    behind_mix_fwd.gather(odd_w_out[0].astype(BF16), via_sibling=True)
    (u_e, hb, y_e, x1), (w_in_o_bottom, w_out_o) = even_mix_fwd(ab, hf, proj_e, small["sc_conv_w"], w_out_e, xs, even_norm_post,
                                                                exchange=behind_mix_fwd)
    w_out_o = w_out_o.reshape(D_MODEL, D_MODEL)
    w_in_o = jnp.concatenate([jnp.transpose(part, (1, 0, 2)).reshape(half, ODD_IN) for part in (w_in_o_top, w_in_o_bottom)], axis=0)
    w_in_o = jnp.pad(w_in_o, ((0, 0), (0, ODD_IN_PAD - ODD_IN)))

    proj_o, h_o = rms_matmul(x1, small["odd_norm_pre"], w_in_o, MM_TILE, ODD_IN_PAD, "odd_in")
    o_f, st_f = gla_fwd(proj_o, wg_pad[0], bg[0], False)
    osum, u_o, st_b, y_o, dout, loss_part = gla_fwd(proj_o, wg_pad[1], bg[1], True, o_other=o_f, gnorm=gnorm,
                                                    post=(w_out_o, x1, small["odd_norm_post"], tgt))

    do, dr, dy_o, d_odd_norm_post, d_gnorm = normbwd_matmul_nt(y_o, small["odd_norm_post"], dout, w_out_o, D_MODEL, "odd_out_bwd",
                                                               gla=(proj_o, osum, gnorm))
    d_w_out_o = matmul_tn(u_o, dy_o, D_MODEL, D_MODEL, 4 * MM_TILE, BF16, "odd_w_out_grad")
    dqkv_f, dlr_f, dwg_f, dbg_f = gla_bwd(proj_o, wg_pad[0], bg[0], do, st_f, False)
    dproj_o, dwg_b, dbg_b = gla_bwd(proj_o, wg_pad[1], bg[1], do, st_b, True, first=(dqkv_f, dlr_f, dr))
    dx1, d_odd_norm_pre = matmul_nt_normbwd(dproj_o, w_in_o, x1, small["odd_norm_pre"], dout, MM_TILE, ODD_IN_PAD, "odd_in_bwd")
    d_w_in_o = matmul_tn(h_o, dproj_o, D_MODEL, ODD_IN_PAD // 5, 8 * MM_TILE, BF16, "odd_w_in_grad")

    landed = {}
    behind_out = Exchange()
    behind_out.scatter(d_w_out_o.reshape(N_DEV, D_MODEL // N_DEV, D_MODEL))
    behind_out.scatter(d_odd_norm_pre, columns=True)
    behind_out.scatter(d_odd_norm_post, columns=True)
    behind_out.scatter(_blocks_along_columns(jnp.concatenate([dbg_f, dbg_b], axis=0), 2))
    behind_out.scatter(_blocks_along_columns(d_gnorm, 1))
    behind_out.scatter(_blocks_along_columns(jnp.concatenate([dwg_f[:GLA_RANK], dwg_b[GLA_RANK:2 * GLA_RANK]], axis=0), 2 * GLA_RANK))
    (du_e, dy_e, d_even_norm_post), got = normbwd_matmul_nt(y_e, even_norm_post, dx1, w_out_e, 2 * D_MODEL, "even_out_bwd",
                                                           exchange=behind_out)
    p_w_out_o = got[0]
    for n, part in zip(("odd_norm_pre", "odd_norm_post", "gla_b_gate", "gla_norm_g", "gla_w_gate_lr"), got[1:]):
        landed[n] = part
    d_w_out_e = matmul_tn(u_e, dy_e, D_MODEL, D_MODEL, 4 * MM_TILE, BF16, "even_w_out_grad")
    behind_mix = Exchange()
    behind_mix.scatter(d_w_out_e.reshape(N_DEV, 2 * D_MODEL // N_DEV, D_MODEL))
    (dh, drest, d_sc_w, adj_b), (p_w_out_e,) = even_mix_bwd(du_e, hf, hb, proj_e, small["sc_conv_w"], ab, exchange=behind_mix)
    adj_f = linear_scan(ab, 0, dh.reshape(1, *dh.shape), 0, True, True, "scan_fwd_adjoint")
    behind_gates_bwd = Exchange()
    behind_gates_bwd.scatter(jnp.transpose(d_w_in_o[:, :ODD_IN].reshape(D_MODEL, N_DEV, ODD_SHARD), (1, 0, 2)))
    behind_gates_bwd.scatter(d_sc_w, columns=True)
    (dua, d_gate_w, d_gate_b, d_lam), (p_w_in_o, landed["sc_conv_w"]) = even_gates_bwd(
        proj_e, adj_f, adj_b, hf, hb, dh, small["rg_conv_w"], conv_b, gate_w, gate_b, small["rg_lambda"], exchange=behind_gates_bwd)
    gate_w_rows = 4 * RG_HEADS * RG_HEAD_DIM
    behind_conv = Exchange()
    behind_conv.scatter(d_gate_w.reshape(N_DEV, gate_w_rows // N_DEV, RG_HEAD_DIM))
    behind_conv.scatter(d_lam, columns=True)
    (dxa, d_conv_w, d_conv_b), (p_gate_w, landed["rg_lambda"]) = rg_conv_bwd(dua, proj_e, small["rg_conv_w"], exchange=behind_conv)
    behind_w_grad = Exchange()
    behind_w_grad.gather(sum_parts(p_gate_w, "sum_gate_w"))
    d_w_in_e, (g_gate_w_all,) = matmul_tn(h_e, drest, D_MODEL, D_MODEL, 4 * MM_TILE, BF16, "even_w_in_grad",
                                          exchange=behind_w_grad, b_first=dxa)
    to_sibling = Exchange()
    to_sibling.to_sibling(d_w_in_e)
    to_sibling.scatter(d_conv_w, columns=True)
    from_sibling, landed["rg_conv_w"] = run_exchange(to_sibling, "scatter_to_sibling")
    behind_in_bwd = Exchange()
    behind_in_bwd.among_chips(pair_sum(d_w_in_e, from_sibling))
    (grad_x, d_even_norm_pre), (p_w_in_e,) = matmul_nt_normbwd(
        drest, w_in_e, xs, even_norm_pre, dx1, 2 * MM_TILE, D_MODEL, "even_in_bwd", exchange=behind_in_bwd, first=dxa)
    last = Exchange()
    replicated_vecs = ("even_norm_pre", "even_norm_post", "rg_conv_b")
    last.gather(jnp.concatenate([d_even_norm_pre, d_even_norm_post, d_conv_b], axis=0))
    last.gather(d_gate_b.reshape(4 * RG_HEADS, RG_HEAD_DIM))
    last.gather(loss_part)

    results = {}

    def update(name, parts_, shape2d, exchange=None):
        outs = adamw(parts_, weights[name][0].reshape(shape2d), m_in[name][0].reshape(shape2d), v_in[name][0].reshape(shape2d),
                     "adamw_" + name, exchange=exchange)
        if exchange is not None:
            outs, gathered = outs
        results[name] = [o.reshape(shapes[name]) for o in outs]
        return gathered if exchange is not None else None

    land_vec, land_gate_b, land_loss = update("even_w_in", p_w_in_e, (D_MODEL, EVEN_SHARD), exchange=last)
    update("even_w_out", p_w_out_e, (2 * D_MODEL // N_DEV, D_MODEL))
    update("odd_w_in", p_w_in_o, (D_MODEL, ODD_SHARD))
    update("odd_w_out", p_w_out_o, (D_MODEL // N_DEV, D_MODEL))
    update("rg_gate_w", g_gate_w_all.reshape(1, gate_w_rows, RG_HEAD_DIM), (gate_w_rows, RG_HEAD_DIM))
    small_out = adamw_small({n: landed[n] for n in SMALL_SHARDED}, weights, m_in, v_in)
    for n in SMALL_SHARDED:
        results[n] = [o[n] for o in small_out]
    gate_b_shape = (4 * RG_HEADS, RG_HEAD_DIM)
    rep_out, gate_b_out, loss_all = adamw_replicated(land_vec, land_gate_b, land_loss, replicated_vecs, weights, m_in, v_in,
                                                     [src["rg_gate_b"].reshape(gate_b_shape) for src in (weights, m_in, v_in)])
    results.update(rep_out)
    results["rg_gate_b"] = [o.reshape(shapes["rg_gate_b"]) for o in gate_b_out]

    return (loss_all[0, 0], grad_x.reshape(x.shape), *[results[n][0] for n in names], *[results[n][1] for n in names],
            *[results[n][2] for n in names], *[results[n][3] for n in names])
```

```python
import functools

import jax
import jax.numpy as jnp
from jax import lax
from jax.experimental import pallas as pl
from jax.experimental.pallas import tpu as pltpu

F32 = jnp.float32
BF16 = jnp.bfloat16

N_DEV = 8
D_MODEL = 1024
NORM_EPS = 1e-6
RG_HEADS = 8
RG_HEAD_DIM = 128
RG_C = 8.0
GLA_HEADS = 4
GLA_DK = 128
GLA_DV = 256
GLA_KEY = 512
GLA_RANK = 16
GLA_NORMALIZER = 16.0
GLA_CHUNK = 64
EVEN_IN = 6144
ODD_IN = 3104
ODD_IN_PAD = 3200
ODD_SHARD = ODD_IN // N_DEV
EVEN_SHARD = EVEN_IN // N_DEV
ADAM_LR = 0.001
ADAM_B1 = 0.9
ADAM_B2 = 0.999
ADAM_EPS = 1e-08
ADAM_WD = 0.01
ADAM_STEP = 10

SMALLEST_NORMAL = 1.1754944e-38
SUBLANES = 8
LANES = 128
VMEM_LIMIT_BYTES = 48 * 2 ** 20
ROW_TILE = 256
GLA_TILE = 256
MM_TILE = 512
ADAMW_BLOCK_BYTES = 2 ** 20
PACK_ROWS = 48
MESH_ID = pl.DeviceIdType.MESH


def _params(n_grid):
    return pltpu.CompilerParams(dimension_semantics=("arbitrary",) * n_grid, vmem_limit_bytes=VMEM_LIMIT_BYTES)


def _bdot(a, b):
    return jnp.dot(a.astype(BF16), b.astype(BF16), preferred_element_type=F32)


def _bdot_nt(a, b):
    return lax.dot_general(a.astype(BF16), b.astype(BF16), (((1,), (1,)), ((), ())), preferred_element_type=F32)


def _bdot_tn(a, b):
    return lax.dot_general(a.astype(BF16), b.astype(BF16), (((0,), (0,)), ((), ())), preferred_element_type=F32)


def _rstd(x):
    return lax.rsqrt(jnp.mean(x * x, axis=-1, keepdims=True) + NORM_EPS)


def _rms(x, g):
    return x * _rstd(x) * g


def _rms_bwd(x, g, dy):
    xh = x * _rstd(x)
    dyg = dy * g
    dx = _rstd(x) * (dyg - xh * jnp.mean(dyg * xh, axis=-1, keepdims=True))
    return dx, jnp.sum(dy * xh, axis=0, keepdims=True)


def _sigmoid(z):
    return 0.5 * jnp.tanh(0.5 * z) + 0.5


def _silu_and_grad(z):
    s = _sigmoid(z)
    return z * s, s * (1.0 + z * (1.0 - s))


def _softplus(z):
    return jnp.maximum(z, 0.0) + jnp.log(1.0 + jnp.exp(-jnp.abs(z)))


def _shift_rows(cur, before, after, d):
    ts = cur.shape[0]
    row = lax.broadcasted_iota(jnp.int32, (SUBLANES, cur.shape[1]), 0)
    out = pltpu.roll(cur, (-d) % ts, 0)
    if d < 0:
        edge = jnp.where(row < -d, pltpu.roll(before, (-d) % SUBLANES, 0), out[:SUBLANES])
        return jnp.concatenate([edge, out[SUBLANES:]], axis=0)
    edge = jnp.where(row >= SUBLANES - d, pltpu.roll(after, (-d) % SUBLANES, 0), out[ts - SUBLANES:])
    return jnp.concatenate([out[:ts - SUBLANES], edge], axis=0)


def _halo_specs(ts, s, width, col, tile=lambda i: i):
    per = ts // SUBLANES
    last = s // SUBLANES - 1
    return [
        pl.BlockSpec((ts, width), lambda i: (tile(i), col)),
        pl.BlockSpec((SUBLANES, width), lambda i: (jnp.maximum(tile(i) * per - 1, 0), col)),
        pl.BlockSpec((SUBLANES, width), lambda i: (jnp.minimum((tile(i) + 1) * per, last), col)),
    ]


def _halo_load(cur_ref, before_ref, after_ref, n_tiles, tile=lambda i: i):
    i = tile(pl.program_id(0))
    before = jnp.where(i > 0, before_ref[...], 0.0)
    after = jnp.where(i < n_tiles - 1, after_ref[...], 0.0)
    return cur_ref[...], before, after


def _full(shape):
    return pl.BlockSpec(shape, lambda *_: (0,) * len(shape))


def _peer(x, y, c, mask):
    px, py, pc = x ^ (mask >> 2), y ^ ((mask >> 1) & 1), c ^ (mask & 1)
    return (px, py, pc), 4 * px + 2 * py + pc


class Exchange:
    SIBLING = 1
    OTHER_CHIPS = (2, 4, 6)

    def __init__(self):
        self.args, self.out_shape, self._kinds = [], [], []

    def gather(self, block, columns=False, via_sibling=False):
        shape = (block.shape[0], N_DEV * block.shape[1]) if columns else (N_DEV,) + block.shape
        return self._add(block, shape, ("gather", columns, via_sibling))

    def scatter(self, stack, columns=False):
        shape = (N_DEV, stack.shape[0], stack.shape[1] // N_DEV) if columns else stack.shape
        return self._add(stack, shape, ("scatter", columns, False))

    def _add(self, arg, shape, kind):
        self.args.append(arg)
        self.out_shape.append(jax.ShapeDtypeStruct(shape, arg.dtype))
        self._kinds.append(kind)
        return len(self.args) - 1

    def semaphores(self):
        n = len(self.args)
        return [pltpu.SemaphoreType.DMA((n, N_DEV - 1)), pltpu.SemaphoreType.DMA((n, N_DEV - 1)), pltpu.SemaphoreType.DMA((n,))]

    def to_sibling(self, array):
        shape = (N_DEV // 2, array.shape[0], array.shape[1] // N_DEV)
        return self._add(array, shape, ("to_sibling", True, False))

    def among_chips(self, stack):
        return self._add(stack, stack.shape, ("among_chips", False, False))

    def _copies(self, position, in_refs, out_refs):
        x, y, c, me = position
        for arr, ((kind, columns, via_sibling), src, out) in enumerate(zip(self._kinds, in_refs, out_refs)):
            if kind == "to_sibling":
                width = src.shape[-1] // N_DEV
                for k in range(N_DEV // 2):
                    block = src.at[:, pl.ds(pl.multiple_of((2 * k + 1 - c) * width, LANES), width)]
                    yield arr, k + 1, block, out.at[k], out.at[k], False, self.SIBLING
                continue
            for mask in range(N_DEV):
                _, peer_id = _peer(x, y, c, mask)
                relayed = via_sibling and mask not in (0, self.SIBLING) + self.OTHER_CHIPS
                if kind == "among_chips":
                    if mask in (0,) + self.OTHER_CHIPS:
                        yield arr, mask, src.at[peer_id // 2], out.at[me // 2], out.at[peer_id // 2], False, mask
                elif kind == "gather":
                    if columns:
                        width = src.shape[-1]
                        yield (arr, mask, src, out.at[:, pl.ds(pl.multiple_of(me * width, LANES), width)],
                               out.at[:, pl.ds(pl.multiple_of(peer_id * width, LANES), width)], relayed, mask)
                    else:
                        yield arr, mask, src, out.at[me], out.at[peer_id], relayed, mask
                else:
                    if columns:
                        width = src.shape[-1] // N_DEV
                        block = src.at[:, pl.ds(pl.multiple_of(peer_id * width, LANES), width)]
                    else:
                        block = src.at[peer_id]
                    yield arr, mask, block, out.at[me], out.at[peer_id], False, mask

    def _remote(self, position, sems, arr, slot, to_mask, src, dst):
        x, y, c, _ = position
        return pltpu.make_async_remote_copy(src_ref=src, dst_ref=dst, send_sem=sems[0].at[arr, slot - 1], recv_sem=sems[1].at[arr, slot - 1],
                                            device_id=_peer(x, y, c, to_mask)[0], device_id_type=MESH_ID)

    def start(self, position, in_refs, out_refs, sems):
        for arr, slot, src, dst, _, relayed, to_mask in self._copies(position, in_refs, out_refs):
            if slot == 0:
                pltpu.make_async_copy(src, dst, sems[2].at[arr]).start()
            elif not relayed:
                self._remote(position, sems, arr, slot, to_mask, src, dst).start()

    def wait(self, position, in_refs, out_refs, sems):
        copies = list(self._copies(position, in_refs, out_refs))
        landings = {(arr, slot): landing for arr, slot, _, _, landing, _, _ in copies}
        passed_on = set()
        for arr, mask, src, _, landing, relayed, _ in copies:
            if relayed:
                held = landings[arr, mask ^ self.SIBLING]
                self._remote(position, sems, arr, mask ^ self.SIBLING, mask ^ self.SIBLING, src, held).wait_recv()
                self._remote(position, sems, arr, mask, self.SIBLING, held, held).start()
                passed_on.add((arr, mask ^ self.SIBLING))
        for arr, slot, src, dst, landing, relayed, to_mask in copies:
            if slot == 0:
                pltpu.make_async_copy(src, dst, sems[2].at[arr]).wait()
                continue
            if (arr, slot) not in passed_on:
                self._remote(position, sems, arr, slot, to_mask, src, landing).wait_recv()
            if relayed:
                held = landings[arr, slot ^ self.SIBLING]
                self._remote(position, sems, arr, slot, self.SIBLING, held, held).wait_send()
            else:
                self._remote(position, sems, arr, slot, to_mask, src, dst).wait_send()


def _call(body, *, name, grid, in_specs, out_specs, out_shape, args, scratch_shapes=(), exchange=None):
    single = not isinstance(out_shape, (list, tuple))
    if single:
        out_specs, out_shape = [out_specs], [out_shape]
    params = _params(len(grid))
    if exchange is None:
        outs = pl.pallas_call(body, name=name, grid=grid, in_specs=in_specs, out_specs=out_specs, out_shape=out_shape,
                              scratch_shapes=list(scratch_shapes), compiler_params=params)(*args)
        return outs[0] if single else outs
    counts = (len(args), len(exchange.args), len(out_shape), len(exchange.out_shape), len(scratch_shapes), 3)

    def wrapped(*refs):
        groups, at = [], 0
        for n in counts:
            groups.append(refs[at:at + n])
            at += n
        main_in, ex_in, main_out, ex_out, main_scratch, sems = groups
        x, y, c = lax.axis_index("x"), lax.axis_index("y"), lax.axis_index("c")
        position = (x, y, c, 4 * x + 2 * y + c)
        ids = [pl.program_id(a) for a in range(len(grid))]
        first = functools.reduce(jnp.logical_and, [i == 0 for i in ids])
        last = functools.reduce(jnp.logical_and, [i == g - 1 for i, g in zip(ids, grid)])

        @pl.when(first)
        def _():
            exchange.start(position, ex_in, ex_out, sems)

        body(*main_in, *main_out, *main_scratch)

        @pl.when(last)
        def _():
            exchange.wait(position, ex_in, ex_out, sems)

    hbm = pl.BlockSpec(memory_space=pl.ANY)
    outs = pl.pallas_call(
        wrapped, name=name, grid=grid, in_specs=list(in_specs) + [hbm] * counts[1], out_specs=list(out_specs) + [hbm] * counts[3],
        out_shape=list(out_shape) + exchange.out_shape, scratch_shapes=list(scratch_shapes) + exchange.semaphores(),
        compiler_params=params)(*args, *exchange.args)
    main = outs[:counts[2]]
    return (main[0] if single else main), outs[counts[2]:]


def run_exchange(exchange, name):
    return _call(lambda: None, name=name, grid=(1,), in_specs=[], out_specs=[], out_shape=[], args=[], exchange=exchange)[1]


def gather_matmul(x, g, w_block, small_block, tm):
    s, d = x.shape
    width = w_block.shape[1]
    tm = min(tm, s)
    n_i = s // tm
    sibling = Exchange.SIBLING
    y_nbr, x_nbr, diagonal = Exchange.OTHER_CHIPS

    def links(core):
        return (y_nbr, x_nbr) if core == 1 else (x_nbr, y_nbr)

    def block_order(core):
        first, second = links(core)
        return [0, sibling, first, second | sibling, second, first | sibling, diagonal, diagonal | sibling]

    def body(order_ref, x_ref, g_ref, wb_ref, sb_ref, proj_ref, h_ref, w_ref, small_ref, h_all, w_buf, send, recv, local, load_sem):
        j, i = pl.program_id(0), pl.program_id(1)
        xx, yy, cc = lax.axis_index("x"), lax.axis_index("y"), lax.axis_index("c")
        me = 4 * xx + 2 * yy + cc

        def block_of(dev):
            return w_ref.at[:, pl.ds(pl.multiple_of(dev * width, LANES), width)]

        def half_of(dev, part):
            return w_ref.at[pl.ds(part * (d // 2), d // 2), pl.ds(pl.multiple_of(dev * width, LANES), width)]

        def remote(arr, slot, to_mask, src, dst):
            return pltpu.make_async_remote_copy(src_ref=src, dst_ref=dst, send_sem=send.at[arr, slot - 1], recv_sem=recv.at[arr, slot - 1],
                                                device_id=_peer(xx, yy, cc, to_mask)[0], device_id_type=MESH_ID)

        def mine_to(mask):
            return remote(0, mask, mask, wb_ref, block_of(me))

        def arrival(mask):
            return remote(0, mask, mask, wb_ref, block_of(me ^ mask))

        def to_sibling(mask):
            return remote(0, mask | sibling, sibling, block_of(me ^ mask), block_of(me ^ mask))

        def relay(of):
            along_x = of == y_nbr
            part = 0 if along_x else 1
            return remote(0 if along_x else 2, diagonal, x_nbr if along_x else y_nbr, half_of(me ^ of, part), half_of(me ^ of, part))

        def diagonal_half(part):
            return remote(0 if part == 0 else 2, diagonal, x_nbr if part == 0 else y_nbr, wb_ref.at[pl.ds(0, d // 2), :],
                          half_of(me ^ diagonal, part))

        @pl.when((j == 0) & (i == 0))
        def _():
            pltpu.make_async_copy(wb_ref, block_of(me), local.at[0]).start()
            pltpu.make_async_copy(sb_ref, small_ref.at[me], local.at[1]).start()
            mine_to(sibling).start()
            for mask in range(1, N_DEV):
                remote(1, mask, mask, sb_ref, small_ref.at[me]).start()

        for core in range(2):
            first, second = links(core)
            for step in range(N_DEV):
                @pl.when((j == step) & (i == 0) & (cc == core))
                def _(step=step, first=first, second=second):
                    if step == 0:
                        mine_to(first).start()
                        pltpu.make_async_copy(wb_ref, block_of(me), local.at[0]).wait()
                    elif step == 1:
                        arrival(sibling).wait_recv()
                    elif step == 2:
                        arrival(first).wait_recv()
                        to_sibling(first).start()
                        mine_to(first).wait_send()
                        mine_to(second).start()
                        relay(first).start()
                    elif step == 3:
                        arrival(second | sibling).wait_recv()
                    elif step == 4:
                        arrival(second).wait_recv()
                        to_sibling(second).start()
                        relay(second).start()
                    elif step == 5:
                        arrival(first | sibling).wait_recv()
                    elif step == 6:
                        diagonal_half(0).wait_recv()
                        diagonal_half(1).wait_recv()
                        to_sibling(diagonal).start()
                    else:
                        arrival(diagonal | sibling).wait_recv()

        @pl.when(i == 0)
        def _():
            load = pltpu.make_async_copy(w_ref.at[:, pl.ds(pl.multiple_of(order_ref[j] * width, LANES), width)], w_buf, load_sem)
            load.start()
            load.wait()

        rows = pl.ds(pl.multiple_of(i * tm, tm), tm)

        @pl.when(j == 0)
        def _():
            h = _rms(x_ref[...], g_ref[...]).astype(BF16)
            h_all[rows, :] = h
            h_ref[...] = h

        proj_ref[...] = jnp.dot(h_all[rows, :], w_buf[...], preferred_element_type=F32)

        @pl.when((j == N_DEV - 1) & (i == n_i - 1))
        def _():
            pltpu.make_async_copy(sb_ref, small_ref.at[me], local.at[1]).wait()
            for mask in range(1, N_DEV):
                remote(1, mask, mask, sb_ref, small_ref.at[me ^ mask]).wait_recv()
                remote(1, mask, mask, sb_ref, small_ref.at[me]).wait_send()
            mine_to(sibling).wait_send()
            for core in range(2):
                @pl.when(cc == core)
                def _(core=core):
                    mine_to(links(core)[1]).wait_send()
            for mask in Exchange.OTHER_CHIPS:
                to_sibling(mask).wait_send()
            relay(y_nbr).wait_send()
            relay(x_nbr).wait_send()

    def first_pass_row(j, i, order):
        return jnp.where(j == 0, i, n_i - 1), 0

    hbm = pl.BlockSpec(memory_space=pl.ANY)
    core = lax.axis_index("c")
    me = 4 * lax.axis_index("x") + 2 * lax.axis_index("y") + core
    order = (me ^ jnp.where(core == 1, jnp.array(block_order(1)), jnp.array(block_order(0)))).astype(jnp.int32)
    grid_spec = pltpu.PrefetchScalarGridSpec(
        num_scalar_prefetch=1, grid=(N_DEV, n_i),
        in_specs=[pl.BlockSpec((tm, d), first_pass_row), pl.BlockSpec((1, d), lambda j, i, order: (0, 0)), hbm, hbm],
        out_specs=[pl.BlockSpec((tm, width), lambda j, i, order: (i, order[j])), pl.BlockSpec((tm, d), first_pass_row), hbm, hbm],
        scratch_shapes=[pltpu.VMEM((s, d), BF16), pltpu.VMEM((d, width), BF16), pltpu.SemaphoreType.DMA((3, N_DEV - 1)),
                        pltpu.SemaphoreType.DMA((3, N_DEV - 1)), pltpu.SemaphoreType.DMA((2,)), pltpu.SemaphoreType.DMA(())])
    return pl.pallas_call(
        body, name="even_in", grid_spec=grid_spec,
        out_shape=[jax.ShapeDtypeStruct((s, N_DEV * width), F32), jax.ShapeDtypeStruct((s, d), BF16),
                   jax.ShapeDtypeStruct((d, N_DEV * width), w_block.dtype), jax.ShapeDtypeStruct((N_DEV,) + small_block.shape, small_block.dtype)],
        compiler_params=_params(2),
    )(order, x, g, w_block, small_block)


def rms_matmul(x, g, w, tm, tn, name, exchange=None):
    s, d = x.shape
    n = w.shape[1]
    tm = min(tm, s)

    def body(x_ref, g_ref, w_ref, o_ref, h_ref):
        @pl.when(pl.program_id(1) == 0)
        def _():
            h_ref[...] = _rms(x_ref[...], g_ref[...]).astype(BF16)

        o_ref[...] = jnp.dot(h_ref[...], w_ref[...], preferred_element_type=F32)

    return _call(
        body, name=name, grid=(s // tm, n // tn),
        in_specs=[pl.BlockSpec((tm, d), lambda i, j: (i, 0)), _full((1, d)), pl.BlockSpec((d, tn), lambda i, j: (0, j))],
        out_specs=[pl.BlockSpec((tm, tn), lambda i, j: (i, j)), pl.BlockSpec((tm, d), lambda i, j: (i, 0))],
        out_shape=[jax.ShapeDtypeStruct((s, n), F32), jax.ShapeDtypeStruct((s, d), BF16)],
        args=[x, g, w], exchange=exchange)


def _gla_out_bwd(du, r, osum, gn, do_ref, dr_ref, dgn_ref):
    silu_r, dsilu_r = _silu_and_grad(r)
    for head in range(GLA_HEADS):
        vl = slice(head * GLA_DV, (head + 1) * GLA_DV)
        o_h, g_h, du_h = osum[:, vl], gn[:, vl], du[:, vl]
        dr_ref[:, vl] = (du_h * _rms(o_h, g_h) * dsilu_r[:, vl]).astype(BF16)
        do_h, dg_h = _rms_bwd(o_h, g_h, du_h * silu_r[:, vl])
        do_ref[:, vl] = do_h
        dgn_ref[...] += dg_h


def normbwd_matmul_nt(y, g, dout, w, tn, name, exchange=None, gla=None):
    s, d = y.shape
    n = w.shape[0]
    tm = min(MM_TILE, s)

    def body(*refs):
        if gla is None:
            y_ref, g_ref, dout_ref, w_ref, du_ref, dy_ref, dg_ref = refs
        else:
            y_ref, g_ref, dout_ref, w_ref, r_ref, o_ref, gn_ref, do_ref, dr_ref, dy_ref, dg_ref, dgn_ref = refs
        i, j = pl.program_id(0), pl.program_id(1)

        @pl.when(j == 0)
        def _():
            dy, dg = _rms_bwd(y_ref[...], g_ref[...], dout_ref[...])
            dy_ref[...] = dy.astype(BF16)

            @pl.when(i == 0)
            def _():
                dg_ref[...] = jnp.zeros_like(dg_ref)
                if gla is not None:
                    dgn_ref[...] = jnp.zeros_like(dgn_ref)

            dg_ref[...] += dg

        du = lax.dot_general(dy_ref[...], w_ref[...], (((1,), (1,)), ((), ())), preferred_element_type=F32)
        if gla is None:
            du_ref[...] = du
        else:
            _gla_out_bwd(du, r_ref[...], o_ref[...], gn_ref[...], do_ref, dr_ref, dgn_ref)

    row = pl.BlockSpec((tm, d), lambda i, j: (i, 0))
    in_specs = [row, _full((1, d)), row, pl.BlockSpec((tn, d), lambda i, j: (j, 0))]
    args = [y, g, dout, w]
    tail_specs = [row, _full((1, d))]
    tail_shapes = [jax.ShapeDtypeStruct((s, d), BF16), jax.ShapeDtypeStruct((1, d), F32)]
    if gla is None:
        out_specs = [pl.BlockSpec((tm, tn), lambda i, j: (i, j))] + tail_specs
        out_shape = [jax.ShapeDtypeStruct((s, n), F32)] + tail_shapes
    else:
        proj, osum, gnorm = gla
        assert n == tn == D_MODEL
        in_specs += [pl.BlockSpec((tm, D_MODEL), lambda i, j: (i, 2)), row, _full(gnorm.shape)]
        args += [proj, osum, gnorm]
        out_specs = [row, row] + tail_specs + [_full((1, GLA_DV))]
        out_shape = [jax.ShapeDtypeStruct((s, D_MODEL), F32), jax.ShapeDtypeStruct((s, D_MODEL), BF16)] + tail_shapes + [
            jax.ShapeDtypeStruct((1, GLA_DV), F32)]
    return _call(body, name=name, grid=(s // tm, n // tn), in_specs=in_specs, out_specs=out_specs, out_shape=out_shape,
                 args=args, exchange=exchange)


def matmul_tn(a, b, tm, tn, ts, out_dtype, name, exchange=None, b_first=None):
    s, m = a.shape
    n = b.shape[1] + (0 if b_first is None else tn)
    ts = min(ts, s)
    n_k = s // ts
    dims = (((0,), (0,)), ((), ()))

    def body(*refs):
        if b_first is None:
            a_ref, b_ref, o_ref, acc = refs
        else:
            a_ref, first_ref, b_ref, o_ref, acc = refs
        j, k = pl.program_id(1), pl.program_id(2)

        @pl.when(k == 0)
        def _():
            acc[...] = jnp.zeros_like(acc)

        if b_first is None:
            acc[...] += lax.dot_general(a_ref[...], b_ref[...], dims, preferred_element_type=F32)
        else:
            @pl.when(j == 0)
            def _():
                acc[...] += lax.dot_general(a_ref[...], first_ref[...], dims, preferred_element_type=F32)

            @pl.when(j > 0)
            def _():
                acc[...] += lax.dot_general(a_ref[...], b_ref[...], dims, preferred_element_type=F32)

        @pl.when(k == n_k - 1)
        def _():
            o_ref[...] = acc[...].astype(out_dtype)

    if b_first is None:
        b_specs, b_args = [pl.BlockSpec((ts, tn), lambda i, j, k: (k, j))], [b]
    else:
        b_specs = [pl.BlockSpec((ts, tn), lambda i, j, k: (k, 0)), pl.BlockSpec((ts, tn), lambda i, j, k: (k, jnp.maximum(j - 1, 0)))]
        b_args = [b_first, b]
    return _call(
        body, name=name, grid=(m // tm, n // tn, n_k),
        in_specs=[pl.BlockSpec((ts, tm), lambda i, j, k: (k, i))] + b_specs,
        out_specs=pl.BlockSpec((tm, tn), lambda i, j, k: (i, j)),
        out_shape=jax.ShapeDtypeStruct((m, n), out_dtype),
        scratch_shapes=[pltpu.VMEM((tm, tn), F32)], args=[a] + b_args, exchange=exchange)


def matmul_nt_normbwd(dproj, w, x, g, dres, tm, tk, name, exchange=None, first=None):
    s, kt = dproj.shape
    kt += 0 if first is None else tk
    d = w.shape[0]
    tm = min(tm, s)
    n_k = kt // tk
    dims = (((1,), (1,)), ((), ()))

    def body(*refs):
        if first is None:
            a_ref, w_ref, x_ref, g_ref, r_ref, dx_ref, dg_ref, acc = refs
        else:
            first_ref, a_ref, w_ref, x_ref, g_ref, r_ref, dx_ref, dg_ref, acc = refs
        i, k = pl.program_id(0), pl.program_id(1)

        @pl.when(k == 0)
        def _():
            acc[...] = jnp.zeros_like(acc)

        if first is None:
            acc[...] += lax.dot_general(a_ref[...], w_ref[...], dims, preferred_element_type=F32)
        else:
            @pl.when(k == 0)
            def _():
                acc[...] += lax.dot_general(first_ref[...], w_ref[...], dims, preferred_element_type=F32)

            @pl.when(k > 0)
            def _():
                acc[...] += lax.dot_general(a_ref[...], w_ref[...], dims, preferred_element_type=F32)

        @pl.when(k == n_k - 1)
        def _():
            dx, dg = _rms_bwd(x_ref[...], g_ref[...], acc[...])
            dx_ref[...] = r_ref[...] + dx

            @pl.when(i == 0)
            def _():
                dg_ref[...] = jnp.zeros_like(dg_ref)

            dg_ref[...] += dg

    row = pl.BlockSpec((tm, d), lambda i, k: (i, 0))
    if first is None:
        a_specs, a_args = [pl.BlockSpec((tm, tk), lambda i, k: (i, k))], [dproj]
    else:
        a_specs = [pl.BlockSpec((tm, tk), lambda i, k: (i, 0)), pl.BlockSpec((tm, tk), lambda i, k: (i, jnp.maximum(k - 1, 0)))]
        a_args = [first, dproj]
    return _call(
        body, name=name, grid=(s // tm, n_k),
        in_specs=a_specs + [pl.BlockSpec((d, tk), lambda i, k: (0, k)), row, _full((1, d)), row],
        out_specs=[row, _full((1, d))],
        out_shape=[jax.ShapeDtypeStruct((s, d), F32), jax.ShapeDtypeStruct((1, d), F32)],
        scratch_shapes=[pltpu.VMEM((tm, d), F32)], args=a_args + [w, x, g, dres], exchange=exchange)


def _rg_conv(xa, before, after, cw, cb):
    return (cw[0:1, :] * _shift_rows(xa, before, after, -2) + cw[1:2, :] * _shift_rows(xa, before, after, -1)
            + cw[2:3, :] * xa + cw[3:4, :] * _shift_rows(xa, before, after, 1) + cb)


def _rg_gates(ua_h, gw_ref, gb_ref, c_h, direction, head):
    r = _sigmoid(_bdot(ua_h, gw_ref[2 * direction, head]) + gb_ref[2 * direction, head:head + 1, :])
    i = _sigmoid(_bdot(ua_h, gw_ref[2 * direction + 1, head]) + gb_ref[2 * direction + 1, head:head + 1, :])
    log_a = -c_h * r
    a = jnp.exp(log_a)
    beta_sq = -jnp.tanh(log_a) * (1.0 + a * a)
    inv_beta = lax.rsqrt(jnp.maximum(beta_sq, SMALLEST_NORMAL))
    return r, i, a, beta_sq * inv_beta, inv_beta


def even_gates_fwd(proj, conv_w, conv_b, gate_w, gate_b, lam, exchange=None):
    s = proj.shape[0]
    ts = min(2 * ROW_TILE, s)
    n_tiles = s // ts

    def body(xa_ref, xb_ref, xn_ref, cw_ref, cb_ref, gw_ref, gb_ref, lam_ref, o_ref, hf_ref, carry):
        @pl.when(pl.program_id(0) == 0)
        def _():
            carry[...] = jnp.zeros_like(carry)

        xa, before, after = _halo_load(xa_ref, xb_ref, xn_ref, n_tiles)
        ua = _rg_conv(xa, before, after, cw_ref[...], cb_ref[...])
        c = RG_C * _softplus(-lam_ref[...])
        ua_bf16 = ua.astype(BF16)
        for direction in range(2):
            for head in range(RG_HEADS):
                lanes = slice(head * RG_HEAD_DIM, (head + 1) * RG_HEAD_DIM)
                ua_h = ua[:, lanes]
                _, i, a, beta, _ = _rg_gates(ua_bf16[:, lanes], gw_ref, gb_ref, c[direction:direction + 1, lanes], direction, head)
                o_ref[2 * direction, :, lanes] = a
                o_ref[2 * direction + 1, :, lanes] = beta * (i * ua_h)
        _scan_tile(o_ref.at[0], o_ref.at[1], hf_ref, carry, False, False)

    return _call(
        body, name="even_gates_fwd", grid=(n_tiles,),
        in_specs=_halo_specs(ts, s, D_MODEL, 0) + [_full(conv_w.shape), _full(conv_b.shape), _full(gate_w.shape),
                                                   _full(gate_b.shape), _full(lam.shape)],
        out_specs=[pl.BlockSpec((4, ts, D_MODEL), lambda i: (0, i, 0)), pl.BlockSpec((ts, D_MODEL), lambda i: (i, 0))],
        out_shape=[jax.ShapeDtypeStruct((4, s, D_MODEL), F32), jax.ShapeDtypeStruct((s, D_MODEL), F32)],
        scratch_shapes=[pltpu.VMEM((SUBLANES, D_MODEL), F32)],
        args=[proj, proj, proj, conv_w, conv_b, gate_w, gate_b, lam], exchange=exchange)


def _scan_tile(a_ref, b_ref, h_ref, carry, reverse, b_times_a):
    ts, c = h_ref.shape
    n_blocks = ts // SUBLANES
    row = lax.broadcasted_iota(jnp.int32, (SUBLANES, c), 0)

    def block(j, h_in):
        r0 = pl.multiple_of((n_blocks - 1 - j if reverse else j) * SUBLANES, SUBLANES)
        a = a_ref[pl.ds(r0, SUBLANES), :]
        b = b_ref[pl.ds(r0, SUBLANES), :]
        if b_times_a:
            b = a * b
        for step in (1, 2, 4):
            shift = SUBLANES - step if reverse else step
            valid = row < SUBLANES - step if reverse else row >= step
            b = jnp.where(valid, a * pltpu.roll(b, shift, 0) + b, b)
            a = jnp.where(valid, a * pltpu.roll(a, shift, 0), a)
        h = a * h_in + b
        h_ref[pl.ds(r0, SUBLANES), :] = h
        return h[0:1, :] if reverse else h[SUBLANES - 1:SUBLANES, :]

    carry[0:1, :] = lax.fori_loop(0, n_blocks, block, carry[0:1, :])


def linear_scan(a_arr, a_idx, b_arr, b_idx, reverse, b_times_a, name, exchange=None):
    _, s, c = a_arr.shape
    ts = min(MM_TILE, s)
    n_tiles = s // ts

    def tile_of(i):
        return n_tiles - 1 - i if reverse else i

    def body(a_ref, b_ref, h_ref, carry):
        @pl.when(pl.program_id(0) == 0)
        def _():
            carry[...] = jnp.zeros_like(carry)

        _scan_tile(a_ref, b_ref, h_ref, carry, reverse, b_times_a)

    return _call(
        body, name=name, grid=(n_tiles,),
        in_specs=[pl.BlockSpec((None, ts, c), lambda i: (a_idx, tile_of(i), 0)),
                  pl.BlockSpec((None, ts, c), lambda i: (b_idx, tile_of(i), 0))],
        out_specs=pl.BlockSpec((ts, c), lambda i: (tile_of(i), 0)),
        out_shape=jax.ShapeDtypeStruct((s, c), F32),
        scratch_shapes=[pltpu.VMEM((SUBLANES, c), F32)], args=[a_arr, b_arr], exchange=exchange)


def _sc_conv(p, before, after, w):
    return w[0:1, :] * _shift_rows(p, before, after, -1) + w[1:2, :] * p + w[2:3, :] * _shift_rows(p, before, after, 1)


def even_mix_fwd(ab, hf, proj, sc_w, w_out, xres, g_post, exchange=None):
    s = proj.shape[0]
    ts = min(ROW_TILE, s)
    n_tiles = s // ts

    def tile(i):
        return n_tiles - 1 - i

    row = pl.BlockSpec((ts, D_MODEL), lambda i: (tile(i), 0))

    def col(c):
        return pl.BlockSpec((ts, D_MODEL), lambda i: (tile(i), c))

    def body(a_ref, b_ref, hf_ref, za_ref, xb_ref, xbb_ref, xbn_ref, gb_ref, gc_ref, gcb_ref, gcn_ref, zb_ref, w_ref,
             wo_ref, x_ref, g_ref, u_ref, hb_ref, y_ref, out_ref, carry):
        @pl.when(pl.program_id(0) == 0)
        def _():
            carry[...] = jnp.zeros_like(carry)

        _scan_tile(a_ref, b_ref, hb_ref, carry, True, False)
        xb, xb_before, xb_after = _halo_load(xb_ref, xbb_ref, xbn_ref, n_tiles, tile)
        gc, gc_before, gc_after = _halo_load(gc_ref, gcb_ref, gcn_ref, n_tiles, tile)
        silu_za, _ = _silu_and_grad(za_ref[...])
        silu_zb, _ = _silu_and_grad(zb_ref[...])
        u_ref[:, :D_MODEL] = ((hf_ref[...] + hb_ref[...]) * silu_za).astype(BF16)
        cv = _sc_conv(gc * xb, gc_before * xb_before, gc_after * xb_after, w_ref[...])
        u_ref[:, D_MODEL:] = (gb_ref[...] * cv * silu_zb).astype(BF16)
        y = jnp.dot(u_ref[...], wo_ref[...], preferred_element_type=F32)
        y_ref[...] = y
        out_ref[...] = x_ref[...] + _rms(y, g_ref[...])

    return _call(
        body, name="even_mix_fwd", grid=(n_tiles,),
        in_specs=[pl.BlockSpec((None, ts, D_MODEL), lambda i: (2, tile(i), 0)), pl.BlockSpec((None, ts, D_MODEL), lambda i: (3, tile(i), 0)),
                  row, col(1)] + _halo_specs(ts, s, D_MODEL, 2, tile) + [col(3)] + _halo_specs(ts, s, D_MODEL, 4, tile)
        + [col(5), _full(sc_w.shape), _full(w_out.shape), row, _full(g_post.shape)],
        out_specs=[pl.BlockSpec((ts, 2 * D_MODEL), lambda i: (tile(i), 0)), row, row, row],
        out_shape=[jax.ShapeDtypeStruct((s, 2 * D_MODEL), BF16)] + [jax.ShapeDtypeStruct((s, D_MODEL), F32)] * 3,
        scratch_shapes=[pltpu.VMEM((SUBLANES, D_MODEL), F32)],
        args=[ab, ab, hf, proj, proj, proj, proj, proj, proj, proj, proj, proj, sc_w, w_out, xres, g_post], exchange=exchange)


def even_mix_bwd(du, hf, hb, proj, sc_w, ab, exchange=None):
    s = proj.shape[0]
    ts = min(ROW_TILE, s)
    n_tiles = s // ts
    row = pl.BlockSpec((ts, D_MODEL), lambda i: (i, 0))

    def body(dya_ref, dyb_ref, dybb_ref, dybn_ref, hf_ref, hb_ref, za_ref, xb_ref, xbb_ref, xbn_ref,
             gb_ref, gbb_ref, gbn_ref, gc_ref, gcb_ref, gcn_ref, zb_ref, zbb_ref, zbn_ref, w_ref, a_ref,
             dh_ref, dp_ref, dw_ref, adj_ref, carry):
        @pl.when(pl.program_id(0) == 0)
        def _():
            carry[...] = jnp.zeros_like(carry)

        dyb, dyb_before, dyb_after = _halo_load(dyb_ref, dybb_ref, dybn_ref, n_tiles)
        xb, xb_before, xb_after = _halo_load(xb_ref, xbb_ref, xbn_ref, n_tiles)
        gb, gb_before, gb_after = _halo_load(gb_ref, gbb_ref, gbn_ref, n_tiles)
        gc, gc_before, gc_after = _halo_load(gc_ref, gcb_ref, gcn_ref, n_tiles)
        zb, zb_before, zb_after = _halo_load(zb_ref, zbb_ref, zbn_ref, n_tiles)
        w = w_ref[...]
        dya, za = dya_ref[...], za_ref[...]
        silu_za, dsilu_za = _silu_and_grad(za)
        dh_ref[...] = dya * silu_za
        _scan_tile(a_ref, dh_ref, adj_ref, carry, False, True)
        dp_ref[:, 0:D_MODEL] = (dya * (hf_ref[...] + hb_ref[...]) * dsilu_za).astype(BF16)

        silu_zb, dsilu_zb = _silu_and_grad(zb)
        p, p_before, p_after = gc * xb, gc_before * xb_before, gc_after * xb_after
        cv = _sc_conv(p, p_before, p_after, w)
        dcv = dyb * gb * silu_zb
        dcv_before = dyb_before * gb_before * _silu_and_grad(zb_before)[0]
        dcv_after = dyb_after * gb_after * _silu_and_grad(zb_after)[0]
        dpp = (w[0:1, :] * _shift_rows(dcv, dcv_before, dcv_after, 1) + w[1:2, :] * dcv
               + w[2:3, :] * _shift_rows(dcv, dcv_before, dcv_after, -1))
        dp_ref[:, D_MODEL:2 * D_MODEL] = (dpp * gc).astype(BF16)
        dp_ref[:, 2 * D_MODEL:3 * D_MODEL] = (dyb * cv * silu_zb).astype(BF16)
        dp_ref[:, 3 * D_MODEL:4 * D_MODEL] = (dpp * xb).astype(BF16)
        dp_ref[:, 4 * D_MODEL:5 * D_MODEL] = (dyb * gb * cv * dsilu_zb).astype(BF16)

        @pl.when(pl.program_id(0) == 0)
        def _():
            dw_ref[...] = jnp.zeros_like(dw_ref)

        dw_ref[0:1, :] += jnp.sum(dcv * _shift_rows(p, p_before, p_after, -1), axis=0, keepdims=True)
        dw_ref[1:2, :] += jnp.sum(dcv * p, axis=0, keepdims=True)
        dw_ref[2:3, :] += jnp.sum(dcv * _shift_rows(p, p_before, p_after, 1), axis=0, keepdims=True)

    return _call(
        body, name="even_mix_bwd", grid=(n_tiles,),
        in_specs=[row] + _halo_specs(ts, s, D_MODEL, 1) + [row, row, pl.BlockSpec((ts, D_MODEL), lambda i: (i, 1))]
        + _halo_specs(ts, s, D_MODEL, 2) + _halo_specs(ts, s, D_MODEL, 3) + _halo_specs(ts, s, D_MODEL, 4)
        + _halo_specs(ts, s, D_MODEL, 5) + [_full(sc_w.shape), pl.BlockSpec((None, ts, D_MODEL), lambda i: (2, i, 0))],
        out_specs=[row, pl.BlockSpec((ts, 5 * D_MODEL), lambda i: (i, 0)), _full(sc_w.shape), row],
        out_shape=[jax.ShapeDtypeStruct((s, D_MODEL), F32), jax.ShapeDtypeStruct((s, 5 * D_MODEL), BF16),
                   jax.ShapeDtypeStruct(sc_w.shape, F32), jax.ShapeDtypeStruct((s, D_MODEL), F32)],
        scratch_shapes=[pltpu.VMEM((SUBLANES, D_MODEL), F32)],
        args=[du, du, du, du, hf, hb, proj, *([proj] * 12), sc_w, ab], exchange=exchange)


def even_gates_bwd(proj, adj_f, adj_b, hf, hb, dh, conv_w, conv_b, gate_w, gate_b, lam, exchange=None):
    s = proj.shape[0]
    ts = min(2 * ROW_TILE, s)
    n_tiles = s // ts
    row = pl.BlockSpec((ts, D_MODEL), lambda i: (i, 0))

    def body(xa_ref, xab_ref, xan_ref, af_ref, afb_ref, afn_ref, ab_ref, abb_ref, abn_ref,
             hf_ref, hfb_ref, hfn_ref, hb_ref, hbb_ref, hbn_ref, dh_ref,
             cw_ref, cb_ref, gw_ref, gb_ref, lam_ref, dua_ref, dgw_ref, dgb_ref, dlam_ref):
        @pl.when(pl.program_id(0) == 0)
        def _():
            dgw_ref[...] = jnp.zeros_like(dgw_ref)
            dgb_ref[...] = jnp.zeros_like(dgb_ref)
            dlam_ref[...] = jnp.zeros_like(dlam_ref)

        xa, before, after = _halo_load(xa_ref, xab_ref, xan_ref, n_tiles)
        ua = _rg_conv(xa, before, after, cw_ref[...], cb_ref[...])
        lam_v = lam_ref[...]
        c = RG_C * _softplus(-lam_v)
        dc_dlam = -RG_C * _sigmoid(-lam_v)
        dh = dh_ref[...]
        adj = (_halo_load(af_ref, afb_ref, afn_ref, n_tiles), _halo_load(ab_ref, abb_ref, abn_ref, n_tiles))
        hs = (_halo_load(hf_ref, hfb_ref, hfn_ref, n_tiles), _halo_load(hb_ref, hbb_ref, hbn_ref, n_tiles))
        dua = jnp.zeros_like(ua)
        ua_bf16 = ua.astype(BF16)
        for direction in range(2):
            step = 1 if direction == 0 else -1
            g = dh + _shift_rows(*adj[direction], step)
            da_all = g * _shift_rows(*hs[direction], -step)
            dua_parts = []
            for head in range(RG_HEADS):
                lanes = slice(head * RG_HEAD_DIM, (head + 1) * RG_HEAD_DIM)
                ua_h = ua[:, lanes]
                c_h = c[direction:direction + 1, lanes]
                ua_hb = ua_bf16[:, lanes]
                r, i, a, beta, inv_beta = _rg_gates(ua_hb, gw_ref, gb_ref, c_h, direction, head)
                db_beta = g[:, lanes] * beta
                d_i = db_beta * ua_h
                dbeta = g[:, lanes] * (i * ua_h)
                dlog_a = (da_all[:, lanes] - dbeta * a * inv_beta) * a
                dpr = -c_h * dlog_a * r * (1.0 - r)
                dpi = d_i * i * (1.0 - i)
                dpr_b, dpi_b = dpr.astype(BF16), dpi.astype(BF16)
                dua_parts.append(db_beta * i + _bdot_nt(dpr_b, gw_ref[2 * direction, head])
                                 + _bdot_nt(dpi_b, gw_ref[2 * direction + 1, head]))
                dgw_ref[2 * direction, head] += _bdot_tn(ua_hb, dpr_b)
                dgw_ref[2 * direction + 1, head] += _bdot_tn(ua_hb, dpi_b)
                dgb_ref[2 * direction, head:head + 1, :] += jnp.sum(dpr, axis=0, keepdims=True)
                dgb_ref[2 * direction + 1, head:head + 1, :] += jnp.sum(dpi, axis=0, keepdims=True)
                dlam_ref[direction:direction + 1, lanes] += (
                    jnp.sum(-r * dlog_a, axis=0, keepdims=True) * dc_dlam[direction:direction + 1, lanes])
            dua = dua + jnp.concatenate(dua_parts, axis=1)
        dua_ref[...] = dua

    return _call(
        body, name="even_gates_bwd", grid=(n_tiles,),
        in_specs=_halo_specs(ts, s, D_MODEL, 0) * 5 + [row] + [_full(conv_w.shape), _full(conv_b.shape), _full(gate_w.shape),
                                                             _full(gate_b.shape), _full(lam.shape)],
        out_specs=[row, _full(gate_w.shape), _full(gate_b.shape), _full(lam.shape)],
        out_shape=[jax.ShapeDtypeStruct((s, D_MODEL), F32), jax.ShapeDtypeStruct(gate_w.shape, F32),
                   jax.ShapeDtypeStruct(gate_b.shape, F32), jax.ShapeDtypeStruct(lam.shape, F32)],
        args=[proj, proj, proj, adj_f, adj_f, adj_f, adj_b, adj_b, adj_b, hf, hf, hf, hb, hb, hb, dh, conv_w, conv_b, gate_w,
              gate_b, lam], exchange=exchange)


def rg_conv_bwd(dua, proj, conv_w, exchange=None):
    s = proj.shape[0]
    ts = min(2 * ROW_TILE, s)
    n_tiles = s // ts

    def body(du_ref, dub_ref, dun_ref, xa_ref, xab_ref, xan_ref, cw_ref, dp_ref, dw_ref, db_ref):
        @pl.when(pl.program_id(0) == 0)
        def _():
            dw_ref[...] = jnp.zeros_like(dw_ref)
            db_ref[...] = jnp.zeros_like(db_ref)

        dua, dua_before, dua_after = _halo_load(du_ref, dub_ref, dun_ref, n_tiles)
        xa, xa_before, xa_after = _halo_load(xa_ref, xab_ref, xan_ref, n_tiles)
        cw = cw_ref[...]
        dxa = (cw[0:1, :] * _shift_rows(dua, dua_before, dua_after, 2) + cw[1:2, :] * _shift_rows(dua, dua_before, dua_after, 1)
               + cw[2:3, :] * dua + cw[3:4, :] * _shift_rows(dua, dua_before, dua_after, -1))
        dp_ref[...] = dxa.astype(BF16)
        for tap, offset in enumerate((-2, -1, 0, 1)):
            shifted = xa if offset == 0 else _shift_rows(xa, xa_before, xa_after, offset)
            dw_ref[tap:tap + 1, :] += jnp.sum(dua * shifted, axis=0, keepdims=True)
        db_ref[...] += jnp.sum(dua, axis=0, keepdims=True)

    return _call(
        body, name="rg_conv_bwd", grid=(n_tiles,),
        in_specs=_halo_specs(ts, s, D_MODEL, 0) * 2 + [_full(conv_w.shape)],
        out_specs=[pl.BlockSpec((ts, D_MODEL), lambda i: (i, 0)), _full(conv_w.shape), _full((1, D_MODEL))],
        out_shape=[jax.ShapeDtypeStruct((s, D_MODEL), BF16), jax.ShapeDtypeStruct(conv_w.shape, F32),
                   jax.ShapeDtypeStruct((1, D_MODEL), F32)],
        args=[dua, dua, dua, proj, proj, proj, conv_w], exchange=exchange)


def _split3(x):
    x1 = x.astype(BF16)
    rest = x - x1.astype(F32)
    x2 = rest.astype(BF16)
    return x1, x2, (rest - x2.astype(F32)).astype(BF16)


def _chunk_sum_matrix(t, reverse, transpose):
    i = lax.broadcasted_iota(jnp.int32, (t, t), 0)
    j = lax.broadcasted_iota(jnp.int32, (t, t), 1)
    if transpose:
        i, j = j, i
    same = (i // GLA_CHUNK) == (j // GLA_CHUNK)
    return jnp.where(same & ((j >= i) if reverse else (j <= i)), 1.0, 0.0).astype(BF16)


def _exact_dot(m, x):
    return sum(jnp.dot(m, part, preferred_element_type=F32) for part in _split3(x))


def _chunk_mask(t, reverse):
    i = lax.broadcasted_iota(jnp.int32, (t, t), 0)
    j = lax.broadcasted_iota(jnp.int32, (t, t), 1)
    return ((i // GLA_CHUNK) == (j // GLA_CHUNK)) & ((j >= i) if reverse else (j <= i))


def _chunk_rows(c):
    return slice(c * GLA_CHUNK, (c + 1) * GLA_CHUNK)


def _gla_gate(lr, wg, bg):
    z = _bdot(lr, wg) + bg
    log_alpha = (jnp.minimum(z, 0.0) - jnp.log(1.0 + jnp.exp(-jnp.abs(z)))) * (1.0 / GLA_NORMALIZER)
    return z, log_alpha


def _gla_tile_terms(q, k, bcum, reverse):
    n_chunks = q.shape[0] // GLA_CHUNK
    totals = []
    for c in range(n_chunks):
        edge = c * GLA_CHUNK if reverse else (c + 1) * GLA_CHUNK - 1
        totals.append(bcum[edge:edge + 1, :])
    btot = jnp.concatenate([jnp.broadcast_to(total, (GLA_CHUNK, total.shape[1])) for total in totals], axis=0)
    e_pos, e_neg, e_st = jnp.exp(bcum), jnp.exp(-bcum), jnp.exp(btot - bcum)
    return q * (GLA_DK ** -0.5) * e_pos, k * e_neg, k * e_st, e_pos, e_neg, e_st, [jnp.exp(total) for total in totals]


def _gla_specs(t, n_tiles, reverse_order):
    def tile(i):
        return n_tiles - 1 - i if reverse_order else i

    return tile, [
        pl.BlockSpec((t, GLA_KEY), lambda i: (tile(i), 0)),
        pl.BlockSpec((t, GLA_KEY), lambda i: (tile(i), 1)),
        pl.BlockSpec((t, D_MODEL), lambda i: (tile(i), 1)),
        pl.BlockSpec((t, LANES), lambda i: (tile(i), (ODD_IN_PAD - LANES) // LANES)),
    ]


def gla_fwd(proj, wg, bg, reverse, o_other=None, gnorm=None, post=None):
    s = proj.shape[0]
    t = min(GLA_TILE, s)
    n_tiles = s // t
    n_chunks = t // GLA_CHUNK
    final = o_other is not None
    tile, specs = _gla_specs(t, n_tiles, reverse)

    def body(*refs):
        if final:
            (q_ref, k_ref, v_ref, lr_ref, wg_ref, bg_ref, oo_ref, r_ref, gn_ref, wo_ref, x_ref, gp_ref, t_ref,
             osum_ref, u_ref, st_ref, y_ref, dout_ref, loss_ref, state) = refs
        else:
            q_ref, k_ref, v_ref, lr_ref, wg_ref, bg_ref, o_ref, st_ref, state = refs
            osum_ref = o_ref

        @pl.when(pl.program_id(0) == 0)
        def _():
            state[...] = jnp.zeros_like(state)

        _, log_alpha = _gla_gate(lr_ref[...], wg_ref[...], bg_ref[...])
        bcum = _exact_dot(_chunk_sum_matrix(t, reverse, False), log_alpha)
        q, k, v = q_ref[...], k_ref[...], v_ref[...]
        q_in, k_in, k_st, _, _, _, decays = _gla_tile_terms(q, k, bcum, reverse)
        mask = _chunk_mask(t, reverse)
        order = list(range(n_chunks))[::-1] if reverse else list(range(n_chunks))
        intra, increments = [], []
        for head in range(GLA_HEADS):
            kl = slice(head * GLA_DK, (head + 1) * GLA_DK)
            vl = slice(head * GLA_DV, (head + 1) * GLA_DV)
            scores = jnp.where(mask, _bdot_nt(q_in[:, kl], k_in[:, kl]), 0.0)
            intra.append(_bdot(scores, v[:, vl]))
            increments.append([_bdot_tn(v[_chunk_rows(c), vl], k_st[_chunk_rows(c), kl]) for c in range(n_chunks)])
        for head in range(GLA_HEADS):
            kl = slice(head * GLA_DK, (head + 1) * GLA_DK)
            vl = slice(head * GLA_DV, (head + 1) * GLA_DV)
            running = state[head]
            before = [None] * n_chunks
            for c in order:
                before[c] = running
                st_ref[c, head] = running
                running = running * decays[c][:, kl] + increments[head][c]
            state[head] = running
            inter = [_bdot_nt(q_in[_chunk_rows(c), kl], before[c]) for c in range(n_chunks)]
            osum_ref[:, vl] = intra[head] + jnp.concatenate(inter, axis=0)
        if final:
            osum = osum_ref[...] + oo_ref[...]
            osum_ref[...] = osum
            silu_r, _ = _silu_and_grad(r_ref[...])
            gn = gn_ref[...]
            for head in range(GLA_HEADS):
                vl = slice(head * GLA_DV, (head + 1) * GLA_DV)
                u_ref[:, vl] = (_rms(osum[:, vl], gn[:, vl]) * silu_r[:, vl]).astype(BF16)

            @pl.when(pl.program_id(0) == 0)
            def _():
                loss_ref[...] = jnp.zeros_like(loss_ref)

            y = jnp.dot(u_ref[...], wo_ref[...], preferred_element_type=F32)
            y_ref[...] = y
            diff = x_ref[...] + _rms(y, gp_ref[...]) - t_ref[...]
            dout_ref[...] = diff * (1.0 / D_MODEL)
            loss_ref[...] += 0.5 * jnp.sum(jnp.mean(diff * diff, axis=-1, keepdims=True))

    row = pl.BlockSpec((t, D_MODEL), lambda i: (tile(i), 0))
    st_spec = pl.BlockSpec((n_chunks, GLA_HEADS, GLA_DV, GLA_DK), lambda i: (tile(i), 0, 0, 0))
    st_shape = jax.ShapeDtypeStruct((s // GLA_CHUNK, GLA_HEADS, GLA_DV, GLA_DK), F32)
    in_specs = specs + [_full(wg.shape), _full(bg.shape)]
    args = [proj, proj, proj, proj, wg, bg]
    if final:
        w_out, xres, g_post, target = post
        in_specs += [row, pl.BlockSpec((t, D_MODEL), lambda i: (tile(i), 2)), _full(gnorm.shape), _full(w_out.shape), row,
                     _full(g_post.shape), row]
        args += [o_other, proj, gnorm, w_out, xres, g_post, target]
        out_specs = [row, row, st_spec, row, row, _full((SUBLANES, LANES))]
        out_shape = [jax.ShapeDtypeStruct((s, D_MODEL), F32), jax.ShapeDtypeStruct((s, D_MODEL), BF16), st_shape,
                     jax.ShapeDtypeStruct((s, D_MODEL), F32), jax.ShapeDtypeStruct((s, D_MODEL), F32),
                     jax.ShapeDtypeStruct((SUBLANES, LANES), F32)]
    else:
        out_specs = [row, st_spec]
        out_shape = [jax.ShapeDtypeStruct((s, D_MODEL), F32), st_shape]
    return pl.pallas_call(
        body, name="gla_fwd_rev" if reverse else "gla_fwd", grid=(n_tiles,), in_specs=in_specs, out_specs=out_specs,
        out_shape=out_shape, scratch_shapes=[pltpu.VMEM((GLA_HEADS, GLA_DV, GLA_DK), F32)], compiler_params=_params(1),
    )(*args)


def gla_bwd(proj, wg, bg, do, states, reverse, first=None):
    s = proj.shape[0]
    t = min(GLA_TILE, s)
    n_tiles = s // t
    n_chunks = t // GLA_CHUNK
    final = first is not None
    tile, specs = _gla_specs(t, n_tiles, not reverse)

    def body(*refs):
        if final:
            (q_ref, k_ref, v_ref, lr_ref, wg_ref, bg_ref, do_ref, st_ref, dqkv1_ref, dlr1_ref, dr_ref,
             dp_ref, dwg_ref, dbg_ref, dstate, dqkv, dbc, dbt) = refs
        else:
            (q_ref, k_ref, v_ref, lr_ref, wg_ref, bg_ref, do_ref, st_ref,
             dqkv, dlr_ref, dwg_ref, dbg_ref, dstate, dbc, dbt) = refs

        @pl.when(pl.program_id(0) == 0)
        def _():
            dstate[...] = jnp.zeros_like(dstate)
            dwg_ref[...] = jnp.zeros_like(dwg_ref)
            dbg_ref[...] = jnp.zeros_like(dbg_ref)

        lr, wg_v = lr_ref[...], wg_ref[...]
        z, log_alpha = _gla_gate(lr, wg_v, bg_ref[...])
        bcum = _exact_dot(_chunk_sum_matrix(t, reverse, False), log_alpha)
        q, k, v, do_v = q_ref[...], k_ref[...], v_ref[...], do_ref[...]
        q_in, k_in, k_st, e_pos, e_neg, e_st, decays = _gla_tile_terms(q, k, bcum, reverse)
        mask = _chunk_mask(t, reverse)
        order = list(range(n_chunks)) if reverse else list(range(n_chunks))[::-1]
        q_b, k_b, ks_b, v_b, do_b = (a.astype(BF16) for a in (q_in, k_in, k_st, v, do_v))
        dq_intra, dk_intra, dv_intra, increments = [], [], [], []
        for head in range(GLA_HEADS):
            kl = slice(head * GLA_DK, (head + 1) * GLA_DK)
            vl = slice(head * GLA_DV, (head + 1) * GLA_DV)
            scores = jnp.where(mask, _bdot_nt(q_b[:, kl], k_b[:, kl]), 0.0).astype(BF16)
            dscores = jnp.where(mask, _bdot_nt(do_b[:, vl], v_b[:, vl]), 0.0).astype(BF16)
            dv_intra.append(_bdot_tn(scores, do_b[:, vl]))
            dq_intra.append(_bdot(dscores, k_b[:, kl]))
            dk_intra.append(_bdot_tn(dscores, q_b[:, kl]))
            increments.append([_bdot_tn(do_b[_chunk_rows(c), vl], q_b[_chunk_rows(c), kl]) for c in range(n_chunks)])
        after_all, ddecay_all = [], []
        for head in range(GLA_HEADS):
            kl = slice(head * GLA_DK, (head + 1) * GLA_DK)
            running = dstate[head]
            after, ddecay = [None] * n_chunks, [None] * n_chunks
            for c in order:
                after[c] = running
                ddecay[c] = jnp.sum(running * st_ref[c, head], axis=0, keepdims=True)
                running = running * decays[c][:, kl] + increments[head][c]
            dstate[head] = running
            after_all.append(after)
            ddecay_all.append(ddecay)
        for head in range(GLA_HEADS):
            kl = slice(head * GLA_DK, (head + 1) * GLA_DK)
            vl = slice(head * GLA_DV, (head + 1) * GLA_DV)
            after, ddecay = after_all[head], ddecay_all[head]
            dq_inter = jnp.concatenate([_bdot(do_b[_chunk_rows(c), vl], st_ref[c, head]) for c in range(n_chunks)], axis=0)
            dv_inter = jnp.concatenate([_bdot_nt(ks_b[_chunk_rows(c), kl], after[c]) for c in range(n_chunks)], axis=0)
            dk_st = jnp.concatenate([_bdot(v_b[_chunk_rows(c), vl], after[c]) for c in range(n_chunks)], axis=0)
            dq_in = dq_intra[head] + dq_inter
            ks_h = k_st[:, kl]
            dqkv[:, 2 * GLA_KEY + head * GLA_DV:2 * GLA_KEY + (head + 1) * GLA_DV] = dv_intra[head] + dv_inter
            dqkv[:, kl] = dq_in * (GLA_DK ** -0.5) * e_pos[:, kl]
            dqkv[:, GLA_KEY + head * GLA_DK:GLA_KEY + (head + 1) * GLA_DK] = dk_intra[head] * e_neg[:, kl] + dk_st * e_st[:, kl]
            dbc[:, kl] = dq_in * q_in[:, kl] - dk_intra[head] * k_in[:, kl] - dk_st * ks_h
            weighted = dk_st * ks_h
            for c in range(n_chunks):
                dbtot = jnp.sum(weighted[_chunk_rows(c)], axis=0, keepdims=True) + ddecay[c] * decays[c][:, kl]
                dbt[_chunk_rows(c), kl] = jnp.broadcast_to(dbtot, (GLA_CHUNK, GLA_DK))
        dlog_alpha = _exact_dot(_chunk_sum_matrix(t, reverse, True), dbc[...]) + dbt[...]
        dz = dlog_alpha * _sigmoid(-z) * (1.0 / GLA_NORMALIZER)
        dlr = _bdot_nt(dz, wg_v)
        dwg_ref[...] += _bdot_tn(lr, dz)
        dbg_ref[...] += jnp.sum(dz, axis=0, keepdims=True)
        if final:
            dp_ref[:, :2 * D_MODEL] = (dqkv[...] + dqkv1_ref[...]).astype(BF16)
            dp_ref[:, 2 * D_MODEL:3 * D_MODEL] = dr_ref[...]
            dp_ref[:, 3 * D_MODEL:] = (dlr + dlr1_ref[...]).astype(BF16)
        else:
            dlr_ref[...] = dlr

    row = pl.BlockSpec((t, D_MODEL), lambda i: (tile(i), 0))
    wide = pl.BlockSpec((t, 2 * D_MODEL), lambda i: (tile(i), 0))
    narrow = pl.BlockSpec((t, LANES), lambda i: (tile(i), 0))
    st_spec = pl.BlockSpec((n_chunks, GLA_HEADS, GLA_DV, GLA_DK), lambda i: (tile(i), 0, 0, 0))
    in_specs = specs + [_full(wg.shape), _full(bg.shape), row, st_spec]
    args = [proj, proj, proj, proj, wg, bg, do, states]
    acc_specs = [_full(wg.shape), _full(bg.shape)]
    acc_shapes = [jax.ShapeDtypeStruct(wg.shape, F32), jax.ShapeDtypeStruct(bg.shape, F32)]
    scratch = [pltpu.VMEM((GLA_HEADS, GLA_DV, GLA_DK), F32)]
    work = [pltpu.VMEM((t, GLA_KEY), F32), pltpu.VMEM((t, GLA_KEY), F32)]
    if final:
        in_specs += [wide, narrow, row]
        args += list(first)
        out_specs = [pl.BlockSpec((t, ODD_IN_PAD), lambda i: (tile(i), 0))] + acc_specs
        out_shape = [jax.ShapeDtypeStruct((s, ODD_IN_PAD), BF16)] + acc_shapes
        scratch += [pltpu.VMEM((t, 2 * D_MODEL), F32)] + work
    else:
        out_specs = [wide, narrow] + acc_specs
        out_shape = [jax.ShapeDtypeStruct((s, 2 * D_MODEL), F32), jax.ShapeDtypeStruct((s, LANES), F32)] + acc_shapes
        scratch += work
    return pl.pallas_call(
        body, name="gla_bwd_rev" if reverse else "gla_bwd", grid=(n_tiles,), in_specs=in_specs, out_specs=out_specs,
        out_shape=out_shape, scratch_shapes=scratch, compiler_params=_params(1),
    )(*args)


def pair_sum(grad, from_sibling):
    n_chips, r, w = from_sibling.shape

    def body(even_ref, odd_ref, sib_ref, o_ref):
        mine = jnp.where(lax.axis_index("c") == 1, odd_ref[...], even_ref[...])
        o_ref[...] = (mine.astype(F32) + sib_ref[...].astype(F32)).astype(o_ref.dtype)

    return pl.pallas_call(
        body, name="pair_sum", grid=(n_chips,),
        in_specs=[pl.BlockSpec((r, w), lambda k: (0, 2 * k)), pl.BlockSpec((r, w), lambda k: (0, 2 * k + 1)),
                  pl.BlockSpec((None, r, w), lambda k: (k, 0, 0))],
        out_specs=pl.BlockSpec((None, r, w), lambda k: (k, 0, 0)),
        out_shape=jax.ShapeDtypeStruct(from_sibling.shape, from_sibling.dtype), compiler_params=_params(1),
    )(grad, grad, from_sibling)


def _adamw_update(g, w, m, v):
    new_m = ADAM_B1 * m + (1.0 - ADAM_B1) * g
    new_v = ADAM_B2 * v + (1.0 - ADAM_B2) * (g * g)
    m_hat = new_m / (1.0 - ADAM_B1 ** ADAM_STEP)
    v_hat = new_v / (1.0 - ADAM_B2 ** ADAM_STEP)
    return -ADAM_LR * (m_hat / (jnp.sqrt(v_hat) + ADAM_EPS) + ADAM_WD * w), new_m, new_v


def sum_parts(parts, name):
    _, r, c = parts.shape

    def body(p_ref, o_ref):
        total = p_ref[0].astype(F32)
        for j in range(1, N_DEV):
            total = total + p_ref[j].astype(F32)
        o_ref[...] = total

    return pl.pallas_call(body, name=name, in_specs=[_full(parts.shape)], out_specs=_full((r, c)), grid=(1,),
                          out_shape=jax.ShapeDtypeStruct((r, c), F32), compiler_params=_params(1))(parts)


def adamw(parts, w, m, v, name, exchange=None):
    n, r, c = parts.shape
    tr = r
    while tr * c * 4 > ADAMW_BLOCK_BYTES and tr % (2 * SUBLANES) == 0:
        tr //= 2

    def body(p_ref, w_ref, m_ref, v_ref, g_ref, d_ref, nm_ref, nv_ref):
        g = p_ref[0].astype(F32)
        for j in range(1, n):
            g = g + p_ref[j].astype(F32)
        g_ref[...] = g
        d_ref[...], nm_ref[...], nv_ref[...] = _adamw_update(g, w_ref[...], m_ref[...], v_ref[...])

    row = pl.BlockSpec((tr, c), lambda i: (i, 0))
    return _call(
        body, name=name, grid=(r // tr,),
        in_specs=[pl.BlockSpec((n, tr, c), lambda i: (0, i, 0)), row, row, row], out_specs=[row] * 4,
        out_shape=[jax.ShapeDtypeStruct((r, c), F32)] * 4, args=[parts, w, m, v], exchange=exchange)


def _small_views(shape):
    if len(shape) == 2:
        return [((slice(None), slice(None)), (slice(None), slice(None)))]
    if len(shape) == 3:
        return [((slice(None), slice(None)), (0,))]
    rows = shape[2]
    return [((slice(k * rows, (k + 1) * rows), slice(None)), (0, k)) for k in range(shape[1])]


def adamw_small(landings, w, m, v):
    names = list(landings)
    n = len(names)
    shapes = [w[name].shape for name in names]

    def body(*refs):
        land, ws, ms, vs = refs[:n], refs[n:2 * n], refs[2 * n:3 * n], refs[3 * n:4 * n]
        outs = [refs[(4 + k) * n:(5 + k) * n] for k in range(4)]
        for k in range(n):
            total = land[k][0]
            for j in range(1, N_DEV):
                total = total + land[k][j]
            for rows, at in _small_views(shapes[k]):
                g = total[rows]
                outs[0][k][at] = g
                outs[1][k][at], outs[2][k][at], outs[3][k][at] = _adamw_update(g, ws[k][at], ms[k][at], vs[k][at])

    blocks = [_full(sh) for sh in shapes]
    outs = pl.pallas_call(
        body, name="adamw_small", grid=(1,),
        in_specs=[_full(landings[name].shape) for name in names] + blocks * 3, out_specs=blocks * 4,
        out_shape=[jax.ShapeDtypeStruct(sh, F32) for sh in shapes] * 4, compiler_params=_params(1),
    )(*[landings[name] for name in names], *[src[name] for src in (w, m, v) for name in names])
    return [dict(zip(names, outs[k * n:(k + 1) * n])) for k in range(4)]


def adamw_replicated(land_vec, land_gate_b, land_loss, names, w, m, v, gate_b):
    n = len(names)

    def body(*refs):
        vec_ref, gb_ref, loss_ref = refs[:3]
        ws, ms, vs = refs[3:3 + n], refs[3 + n:3 + 2 * n], refs[3 + 2 * n:3 + 3 * n]
        gw_ref, gm_ref, gv_ref = refs[3 + 3 * n:6 + 3 * n]
        outs = refs[6 + 3 * n:]
        vec, gb, loss = vec_ref[0], gb_ref[0], loss_ref[0]
        for j in range(1, N_DEV):
            vec, gb, loss = vec + vec_ref[j], gb + gb_ref[j], loss + loss_ref[j]
        for k in range(n):
            g = vec[k:k + 1, :]
            outs[k][...] = g
            outs[n + k][...], outs[2 * n + k][...], outs[3 * n + k][...] = _adamw_update(g, ws[k][...], ms[k][...], vs[k][...])
        outs[4 * n][...] = gb
        outs[4 * n + 1][...], outs[4 * n + 2][...], outs[4 * n + 3][...] = _adamw_update(gb, gw_ref[...], gm_ref[...], gv_ref[...])
        outs[4 * n + 4][...] = loss

    vec_block, gb_block = _full((1, D_MODEL)), _full(gate_b[0].shape)
    outs = pl.pallas_call(
        body, name="adamw_replicated", grid=(1,),
        in_specs=[_full(land_vec.shape), _full(land_gate_b.shape), _full(land_loss.shape)] + [vec_block] * (3 * n) + [gb_block] * 3,
        out_specs=[vec_block] * (4 * n) + [gb_block] * 4 + [_full(land_loss.shape[1:])],
        out_shape=[jax.ShapeDtypeStruct((1, D_MODEL), F32)] * (4 * n) + [jax.ShapeDtypeStruct(gate_b[0].shape, F32)] * 4
        + [jax.ShapeDtypeStruct(land_loss.shape[1:], F32)],
        compiler_params=_params(1),
    )(land_vec, land_gate_b, land_loss, *[src[name] for src in (w, m, v) for name in names], *gate_b)
    results = {name: [outs[k * n + i] for k in range(4)] for i, name in enumerate(names)}
    return results, outs[4 * n:4 * n + 4], outs[4 * n + 4]


SMALL_SHARDED = ("rg_conv_w", "rg_lambda", "sc_conv_w", "odd_norm_pre", "odd_norm_post", "gla_b_gate", "gla_norm_g", "gla_w_gate_lr")
SMALL_ROWS = {"rg_conv_w": (0, 4), "rg_lambda": (4, 2), "sc_conv_w": (6, 3), "odd_norm_pre": (9, 1), "odd_norm_post": (10, 1),
              "gla_b_gate": (11, 2), "gla_norm_g": (13, 1), "gla_w_gate_lr": (16, 32)}


def _pack_small(shards):
    pieces, at = [], 0
    for name in SMALL_SHARDED:
        start, rows = SMALL_ROWS[name]
        if start > at:
            pieces.append(jnp.zeros((start - at, LANES), F32))
        a = shards[name].reshape(rows, -1)
        pieces.append(jnp.pad(a, ((0, 0), (0, LANES - a.shape[1]))))
        at = start + rows
    return jnp.concatenate(pieces, axis=0)


def _unpack_gathered(g):
    def cols(name, width):
        start, rows = SMALL_ROWS[name]
        return jnp.transpose(g[:, start:start + rows, :width], (1, 0, 2)).reshape(rows, N_DEV * width)

    w_lr = cols("gla_w_gate_lr", GLA_KEY // N_DEV).reshape(2, GLA_RANK, GLA_KEY)
    return dict(rg_conv_w=cols("rg_conv_w", LANES), rg_lambda=cols("rg_lambda", LANES), sc_conv_w=cols("sc_conv_w", LANES),
                odd_norm_pre=cols("odd_norm_pre", LANES), odd_norm_post=cols("odd_norm_post", LANES),
                gla_b_gate=cols("gla_b_gate", GLA_KEY // N_DEV), gla_norm_g=cols("gla_norm_g", GLA_DV // N_DEV), gla_w_gate_lr=w_lr)


def _blocks_along_columns(a, rows):
    return jnp.transpose(a.reshape(rows, N_DEV, -1), (1, 0, 2))


def kernel(x, even_norm_pre, even_norm_post, even_w_in, rg_conv_w, rg_conv_b, rg_gate_w, rg_gate_b, rg_lambda, sc_conv_w, even_w_out, odd_norm_pre, odd_norm_post, odd_w_in, gla_w_gate_lr, gla_b_gate, gla_norm_g, odd_w_out, loss_target, m_even_norm_pre, m_even_norm_post, m_even_w_in, m_rg_conv_w, m_rg_conv_b, m_rg_gate_w, m_rg_gate_b, m_rg_lambda, m_sc_conv_w, m_even_w_out, m_odd_norm_pre, m_odd_norm_post, m_odd_w_in, m_gla_w_gate_lr, m_gla_b_gate, m_gla_norm_g, m_odd_w_out, v_even_norm_pre, v_even_norm_post, v_even_w_in, v_rg_conv_w, v_rg_conv_b, v_rg_gate_w, v_rg_gate_b, v_rg_lambda, v_sc_conv_w, v_even_w_out, v_odd_norm_pre, v_odd_norm_post, v_odd_w_in, v_gla_w_gate_lr, v_gla_b_gate, v_gla_norm_g, v_odd_w_out):
    weights = dict(even_norm_pre=even_norm_pre, even_norm_post=even_norm_post, even_w_in=even_w_in, rg_conv_w=rg_conv_w,
                   rg_conv_b=rg_conv_b, rg_gate_w=rg_gate_w, rg_gate_b=rg_gate_b, rg_lambda=rg_lambda, sc_conv_w=sc_conv_w,
                   even_w_out=even_w_out, odd_norm_pre=odd_norm_pre, odd_norm_post=odd_norm_post, odd_w_in=odd_w_in,
                   gla_w_gate_lr=gla_w_gate_lr, gla_b_gate=gla_b_gate, gla_norm_g=gla_norm_g, odd_w_out=odd_w_out)
    m_in = dict(even_norm_pre=m_even_norm_pre, even_norm_post=m_even_norm_post, even_w_in=m_even_w_in, rg_conv_w=m_rg_conv_w,
                rg_conv_b=m_rg_conv_b, rg_gate_w=m_rg_gate_w, rg_gate_b=m_rg_gate_b, rg_lambda=m_rg_lambda, sc_conv_w=m_sc_conv_w,
                even_w_out=m_even_w_out, odd_norm_pre=m_odd_norm_pre, odd_norm_post=m_odd_norm_post, odd_w_in=m_odd_w_in,
                gla_w_gate_lr=m_gla_w_gate_lr, gla_b_gate=m_gla_b_gate, gla_norm_g=m_gla_norm_g, odd_w_out=m_odd_w_out)
    v_in = dict(even_norm_pre=v_even_norm_pre, even_norm_post=v_even_norm_post, even_w_in=v_even_w_in, rg_conv_w=v_rg_conv_w,
                rg_conv_b=v_rg_conv_b, rg_gate_w=v_rg_gate_w, rg_gate_b=v_rg_gate_b, rg_lambda=v_rg_lambda, sc_conv_w=v_sc_conv_w,
                even_w_out=v_even_w_out, odd_norm_pre=v_odd_norm_pre, odd_norm_post=v_odd_norm_post, odd_w_in=v_odd_w_in,
                gla_w_gate_lr=v_gla_w_gate_lr, gla_b_gate=v_gla_b_gate, gla_norm_g=v_gla_norm_g, odd_w_out=v_odd_w_out)
    names = list(weights)
    shapes = {n: weights[n].shape for n in names}
    xs = x[0]
    tgt = loss_target[0]

    proj_e, h_e, w_in_e, small_all = gather_matmul(xs, even_norm_pre, even_w_in[0].astype(BF16),
                                                   _pack_small({n: weights[n][0] for n in SMALL_SHARDED}), 2 * MM_TILE)
    small = _unpack_gathered(small_all)
    gate_w = rg_gate_w[0].reshape(4, RG_HEADS, RG_HEAD_DIM, RG_HEAD_DIM).astype(BF16)
    gate_b = rg_gate_b[0].reshape(4, RG_HEADS, RG_HEAD_DIM)
    conv_b = rg_conv_b
    wg_pad = [jnp.pad(small["gla_w_gate_lr"][d], ((GLA_RANK * d, LANES - GLA_RANK * (d + 1)), (0, 0))).astype(BF16) for d in range(2)]
    bg = [small["gla_b_gate"][d:d + 1] for d in range(2)]
    gnorm = jnp.tile(small["gla_norm_g"], (1, GLA_HEADS))

    half = D_MODEL // 2
    behind_gates = Exchange()
    behind_gates.gather(even_w_out[0].astype(BF16), via_sibling=True)
    behind_gates.gather(odd_w_in[0, :half].astype(BF16), via_sibling=True)
    (ab, hf), (w_out_e, w_in_o_top) = even_gates_fwd(proj_e, small["rg_conv_w"], conv_b, gate_w, gate_b, small["rg_lambda"],
                                                     exchange=behind_gates)
    w_out_e = w_out_e.reshape(2 * D_MODEL, D_MODEL)
    behind_mix_fwd = Exchange()
    behind_mix_fwd.gather(odd_w_in[0, half:].astype(BF16), via_sibling=True)
    behind_mix_fwd.gather(odd_w_out[0].astype(BF16), via_sibling=True)
    (u_e, hb, y_e, x1), (w_in_o_bottom, w_out_o) = even_mix_fwd(ab, hf, proj_e, small["sc_conv_w"], w_out_e, xs, even_norm_post,
                                                                exchange=behind_mix_fwd)
    w_out_o = w_out_o.reshape(D_MODEL, D_MODEL)
    w_in_o = jnp.concatenate([jnp.transpose(part, (1, 0, 2)).reshape(half, ODD_IN) for part in (w_in_o_top, w_in_o_bottom)], axis=0)
    w_in_o = jnp.pad(w_in_o, ((0, 0), (0, ODD_IN_PAD - ODD_IN)))

    proj_o, h_o = rms_matmul(x1, small["odd_norm_pre"], w_in_o, MM_TILE, ODD_IN_PAD, "odd_in")
    o_f, st_f = gla_fwd(proj_o, wg_pad[0], bg[0], False)
    osum, u_o, st_b, y_o, dout, loss_part = gla_fwd(proj_o, wg_pad[1], bg[1], True, o_other=o_f, gnorm=gnorm,
                                                    post=(w_out_o, x1, small["odd_norm_post"], tgt))

    do, dr, dy_o, d_odd_norm_post, d_gnorm = normbwd_matmul_nt(y_o, small["odd_norm_post"], dout, w_out_o, D_MODEL, "odd_out_bwd",
                                                               gla=(proj_o, osum, gnorm))
    d_w_out_o = matmul_tn(u_o, dy_o, D_MODEL, D_MODEL, 4 * MM_TILE, BF16, "odd_w_out_grad")
    dqkv_f, dlr_f, dwg_f, dbg_f = gla_bwd(proj_o, wg_pad[0], bg[0], do, st_f, False)
    dproj_o, dwg_b, dbg_b = gla_bwd(proj_o, wg_pad[1], bg[1], do, st_b, True, first=(dqkv_f, dlr_f, dr))
    dx1, d_odd_norm_pre = matmul_nt_normbwd(dproj_o, w_in_o, x1, small["odd_norm_pre"], dout, MM_TILE, ODD_IN_PAD, "odd_in_bwd")
    d_w_in_o = matmul_tn(h_o, dproj_o, D_MODEL, ODD_IN_PAD // 5, 8 * MM_TILE, BF16, "odd_w_in_grad")

    landed = {}
    behind_out = Exchange()
    behind_out.scatter(d_w_out_o.reshape(N_DEV, D_MODEL // N_DEV, D_MODEL))
    behind_out.scatter(d_odd_norm_pre, columns=True)
    behind_out.scatter(d_odd_norm_post, columns=True)
    behind_out.scatter(_blocks_along_columns(jnp.concatenate([dbg_f, dbg_b], axis=0), 2))
    behind_out.scatter(_blocks_along_columns(d_gnorm, 1))
    behind_out.scatter(_blocks_along_columns(jnp.concatenate([dwg_f[:GLA_RANK], dwg_b[GLA_RANK:2 * GLA_RANK]], axis=0), 2 * GLA_RANK))
    (du_e, dy_e, d_even_norm_post), got = normbwd_matmul_nt(y_e, even_norm_post, dx1, w_out_e, 2 * D_MODEL, "even_out_bwd",
                                                           exchange=behind_out)
    p_w_out_o = got[0]
    for n, part in zip(("odd_norm_pre", "odd_norm_post", "gla_b_gate", "gla_norm_g", "gla_w_gate_lr"), got[1:]):
        landed[n] = part
    d_w_out_e = matmul_tn(u_e, dy_e, D_MODEL, D_MODEL, 4 * MM_TILE, BF16, "even_w_out_grad")
    behind_mix = Exchange()
    behind_mix.scatter(d_w_out_e.reshape(N_DEV, 2 * D_MODEL // N_DEV, D_MODEL))
    (dh, drest, d_sc_w, adj_b), (p_w_out_e,) = even_mix_bwd(du_e, hf, hb, proj_e, small["sc_conv_w"], ab, exchange=behind_mix)
    adj_f = linear_scan(ab, 0, dh.reshape(1, *dh.shape), 0, True, True, "scan_fwd_adjoint")
    behind_gates_bwd = Exchange()
    behind_gates_bwd.scatter(jnp.transpose(d_w_in_o[:, :ODD_IN].reshape(D_MODEL, N_DEV, ODD_SHARD), (1, 0, 2)))
    behind_gates_bwd.scatter(d_sc_w, columns=True)
    (dua, d_gate_w, d_gate_b, d_lam), (p_w_in_o, landed["sc_conv_w"]) = even_gates_bwd(
        proj_e, adj_f, adj_b, hf, hb, dh, small["rg_conv_w"], conv_b, gate_w, gate_b, small["rg_lambda"], exchange=behind_gates_bwd)
    gate_w_rows = 4 * RG_HEADS * RG_HEAD_DIM
    behind_conv = Exchange()
    behind_conv.scatter(d_gate_w.reshape(N_DEV, gate_w_rows // N_DEV, RG_HEAD_DIM))
    behind_conv.scatter(d_lam, columns=True)
    (dxa, d_conv_w, d_conv_b), (p_gate_w, landed["rg_lambda"]) = rg_conv_bwd(dua, proj_e, small["rg_conv_w"], exchange=behind_conv)
    behind_w_grad = Exchange()
    behind_w_grad.gather(sum_parts(p_gate_w, "sum_gate_w"))
    d_w_in_e, (g_gate_w_all,) = matmul_tn(h_e, drest, D_MODEL, D_MODEL, 4 * MM_TILE, BF16, "even_w_in_grad",
                                          exchange=behind_w_grad, b_first=dxa)
    to_sibling = Exchange()
    to_sibling.to_sibling(d_w_in_e)
    to_sibling.scatter(d_conv_w, columns=True)
    from_sibling, landed["rg_conv_w"] = run_exchange(to_sibling, "scatter_to_sibling")
    behind_in_bwd = Exchange()
    behind_in_bwd.among_chips(pair_sum(d_w_in_e, from_sibling))
    (grad_x, d_even_norm_pre), (p_w_in_e,) = matmul_nt_normbwd(
        drest, w_in_e, xs, even_norm_pre, dx1, 2 * MM_TILE, D_MODEL, "even_in_bwd", exchange=behind_in_bwd, first=dxa)
    last = Exchange()
    replicated_vecs = ("even_norm_pre", "even_norm_post", "rg_conv_b")
    last.gather(jnp.concatenate([d_even_norm_pre, d_even_norm_post, d_conv_b], axis=0))
    last.gather(d_gate_b.reshape(4 * RG_HEADS, RG_HEAD_DIM))
    last.gather(loss_part)

    results = {}

    def update(name, parts_, shape2d, exchange=None):
        outs = adamw(parts_, weights[name][0].reshape(shape2d), m_in[name][0].reshape(shape2d), v_in[name][0].reshape(shape2d),
                     "adamw_" + name, exchange=exchange)
        if exchange is not None:
            outs, gathered = outs
        results[name] = [o.reshape(shapes[name]) for o in outs]
        return gathered if exchange is not None else None

    land_vec, land_gate_b, land_loss = update("even_w_in", p_w_in_e, (D_MODEL, EVEN_SHARD), exchange=last)
    update("even_w_out", p_w_out_e, (2 * D_MODEL // N_DEV, D_MODEL))
    update("odd_w_in", p_w_in_o, (D_MODEL, ODD_SHARD))
    update("odd_w_out", p_w_out_o, (D_MODEL // N_DEV, D_MODEL))
    update("rg_gate_w", g_gate_w_all.reshape(1, gate_w_rows, RG_HEAD_DIM), (gate_w_rows, RG_HEAD_DIM))
    small_out = adamw_small({n: landed[n] for n in SMALL_SHARDED}, weights, m_in, v_in)
    for n in SMALL_SHARDED:
        results[n] = [o[n] for o in small_out]
    gate_b_shape = (4 * RG_HEADS, RG_HEAD_DIM)
    rep_out, gate_b_out, loss_all = adamw_replicated(land_vec, land_gate_b, land_loss, replicated_vecs, weights, m_in, v_in,
                                                     [src["rg_gate_b"].reshape(gate_b_shape) for src in (weights, m_in, v_in)])
    results.update(rep_out)
    results["rg_gate_b"] = [o.reshape(shapes["rg_gate_b"]) for o in gate_b_out]

    return (loss_all[0, 0], grad_x.reshape(x.shape), *[results[n][0] for n in names], *[results[n][1] for n in names],
            *[results[n][2] for n in names], *[results[n][3] for n in names])
```

```python
import functools

import jax
import jax.numpy as jnp
from jax import lax
from jax.experimental import pallas as pl
from jax.experimental.pallas import tpu as pltpu

F32 = jnp.float32
BF16 = jnp.bfloat16

N_DEV = 8
D_MODEL = 1024
NORM_EPS = 1e-6
RG_HEADS = 8
RG_HEAD_DIM = 128
RG_C = 8.0
GLA_HEADS = 4
GLA_DK = 128
GLA_DV = 256
GLA_KEY = 512
GLA_RANK = 16
GLA_NORMALIZER = 16.0
GLA_CHUNK = 64
EVEN_IN = 6144
ODD_IN = 3104
ODD_IN_PAD = 3200
ODD_SHARD = ODD_IN // N_DEV
EVEN_SHARD = EVEN_IN // N_DEV
ADAM_LR = 0.001
ADAM_B1 = 0.9
ADAM_B2 = 0.999
ADAM_EPS = 1e-08
ADAM_WD = 0.01
ADAM_STEP = 10

SMALLEST_NORMAL = 1.1754944e-38
SUBLANES = 8
LANES = 128
VMEM_LIMIT_BYTES = 48 * 2 ** 20
ROW_TILE = 256
GLA_TILE = 256
MM_TILE = 512
ADAMW_BLOCK_BYTES = 2 ** 20
PACK_ROWS = 48
MESH_ID = pl.DeviceIdType.MESH


def _params(n_grid):
    return pltpu.CompilerParams(dimension_semantics=("arbitrary",) * n_grid, vmem_limit_bytes=VMEM_LIMIT_BYTES)


def _bdot(a, b):
    return jnp.dot(a.astype(BF16), b.astype(BF16), preferred_element_type=F32)


def _bdot_nt(a, b):
    return lax.dot_general(a.astype(BF16), b.astype(BF16), (((1,), (1,)), ((), ())), preferred_element_type=F32)


def _bdot_tn(a, b):
    return lax.dot_general(a.astype(BF16), b.astype(BF16), (((0,), (0,)), ((), ())), preferred_element_type=F32)


def _rstd(x):
    return lax.rsqrt(jnp.mean(x * x, axis=-1, keepdims=True) + NORM_EPS)


def _rms(x, g):
    return x * _rstd(x) * g


def _rms_bwd(x, g, dy):
    xh = x * _rstd(x)
    dyg = dy * g
    dx = _rstd(x) * (dyg - xh * jnp.mean(dyg * xh, axis=-1, keepdims=True))
    return dx, jnp.sum(dy * xh, axis=0, keepdims=True)


def _sigmoid(z):
    return 0.5 * jnp.tanh(0.5 * z) + 0.5


def _silu_and_grad(z):
    s = _sigmoid(z)
    return z * s, s * (1.0 + z * (1.0 - s))


def _softplus(z):
    return jnp.maximum(z, 0.0) + jnp.log(1.0 + jnp.exp(-jnp.abs(z)))


def _shift_rows(cur, before, after, d):
    ts = cur.shape[0]
    row = lax.broadcasted_iota(jnp.int32, (SUBLANES, cur.shape[1]), 0)
    out = pltpu.roll(cur, (-d) % ts, 0)
    if d < 0:
        edge = jnp.where(row < -d, pltpu.roll(before, (-d) % SUBLANES, 0), out[:SUBLANES])
        return jnp.concatenate([edge, out[SUBLANES:]], axis=0)
    edge = jnp.where(row >= SUBLANES - d, pltpu.roll(after, (-d) % SUBLANES, 0), out[ts - SUBLANES:])
    return jnp.concatenate([out[:ts - SUBLANES], edge], axis=0)


def _halo_specs(ts, s, width, col, tile=lambda i: i):
    per = ts // SUBLANES
    last = s // SUBLANES - 1
    return [
        pl.BlockSpec((ts, width), lambda i: (tile(i), col)),
        pl.BlockSpec((SUBLANES, width), lambda i: (jnp.maximum(tile(i) * per - 1, 0), col)),
        pl.BlockSpec((SUBLANES, width), lambda i: (jnp.minimum((tile(i) + 1) * per, last), col)),
    ]


def _halo_load(cur_ref, before_ref, after_ref, n_tiles, tile=lambda i: i):
    i = tile(pl.program_id(0))
    before = jnp.where(i > 0, before_ref[...], 0.0)
    after = jnp.where(i < n_tiles - 1, after_ref[...], 0.0)
    return cur_ref[...], before, after


def _full(shape):
    return pl.BlockSpec(shape, lambda *_: (0,) * len(shape))


def _peer(x, y, c, mask):
    px, py, pc = x ^ (mask >> 2), y ^ ((mask >> 1) & 1), c ^ (mask & 1)
    return (px, py, pc), 4 * px + 2 * py + pc


class Exchange:
    SIBLING = 1
    OTHER_CHIPS = (2, 4, 6)

    def __init__(self):
        self.args, self.out_shape, self._kinds = [], [], []

    def gather(self, block, columns=False, via_sibling=False):
        shape = (block.shape[0], N_DEV * block.shape[1]) if columns else (N_DEV,) + block.shape
        return self._add(block, shape, ("gather", columns, via_sibling))

    def scatter(self, stack, columns=False):
        shape = (N_DEV, stack.shape[0], stack.shape[1] // N_DEV) if columns else stack.shape
        return self._add(stack, shape, ("scatter", columns, False))

    def _add(self, arg, shape, kind):
        self.args.append(arg)
        self.out_shape.append(jax.ShapeDtypeStruct(shape, arg.dtype))
        self._kinds.append(kind)
        return len(self.args) - 1

    def semaphores(self):
        n = len(self.args)
        return [pltpu.SemaphoreType.DMA((n, N_DEV - 1)), pltpu.SemaphoreType.DMA((n, N_DEV - 1)), pltpu.SemaphoreType.DMA((n,))]

    def to_sibling(self, array):
        shape = (N_DEV // 2, array.shape[0], array.shape[1] // N_DEV)
        return self._add(array, shape, ("to_sibling", True, False))

    def among_chips(self, stack):
        return self._add(stack, stack.shape, ("among_chips", False, False))

    def _copies(self, position, in_refs, out_refs):
        x, y, c, me = position
        for arr, ((kind, columns, via_sibling), src, out) in enumerate(zip(self._kinds, in_refs, out_refs)):
            if kind == "to_sibling":
                width = src.shape[-1] // N_DEV
                for k in range(N_DEV // 2):
                    block = src.at[:, pl.ds(pl.multiple_of((2 * k + 1 - c) * width, LANES), width)]
                    yield arr, k + 1, block, out.at[k], out.at[k], False, self.SIBLING
                continue
            for mask in range(N_DEV):
                _, peer_id = _peer(x, y, c, mask)
                relayed = via_sibling and mask not in (0, self.SIBLING) + self.OTHER_CHIPS
                if kind == "among_chips":
                    if mask in (0,) + self.OTHER_CHIPS:
                        yield arr, mask, src.at[peer_id // 2], out.at[me // 2], out.at[peer_id // 2], False, mask
                elif kind == "gather":
                    if columns:
                        width = src.shape[-1]
                        yield (arr, mask, src, out.at[:, pl.ds(pl.multiple_of(me * width, LANES), width)],
                               out.at[:, pl.ds(pl.multiple_of(peer_id * width, LANES), width)], relayed, mask)
                    else:
                        yield arr, mask, src, out.at[me], out.at[peer_id], relayed, mask
                else:
                    if columns:
                        width = src.shape[-1] // N_DEV
                        block = src.at[:, pl.ds(pl.multiple_of(peer_id * width, LANES), width)]
                    else:
                        block = src.at[peer_id]
                    yield arr, mask, block, out.at[me], out.at[peer_id], False, mask

    def _remote(self, position, sems, arr, slot, to_mask, src, dst):
        x, y, c, _ = position
        return pltpu.make_async_remote_copy(src_ref=src, dst_ref=dst, send_sem=sems[0].at[arr, slot - 1], recv_sem=sems[1].at[arr, slot - 1],
                                            device_id=_peer(x, y, c, to_mask)[0], device_id_type=MESH_ID)

    def start(self, position, in_refs, out_refs, sems):
        for arr, slot, src, dst, _, relayed, to_mask in self._copies(position, in_refs, out_refs):
            if slot == 0:
                pltpu.make_async_copy(src, dst, sems[2].at[arr]).start()
            elif not relayed:
                self._remote(position, sems, arr, slot, to_mask, src, dst).start()

    def wait(self, position, in_refs, out_refs, sems):
        copies = list(self._copies(position, in_refs, out_refs))
        landings = {(arr, slot): landing for arr, slot, _, _, landing, _, _ in copies}
        passed_on = set()
        for arr, mask, src, _, landing, relayed, _ in copies:
            if relayed:
                held = landings[arr, mask ^ self.SIBLING]
                self._remote(position, sems, arr, mask ^ self.SIBLING, mask ^ self.SIBLING, src, held).wait_recv()
                self._remote(position, sems, arr, mask, self.SIBLING, held, held).start()
                passed_on.add((arr, mask ^ self.SIBLING))
        for arr, slot, src, dst, landing, relayed, to_mask in copies:
            if slot == 0:
                pltpu.make_async_copy(src, dst, sems[2].at[arr]).wait()
                continue
            if (arr, slot) not in passed_on:
                self._remote(position, sems, arr, slot, to_mask, src, landing).wait_recv()
            if relayed:
                held = landings[arr, slot ^ self.SIBLING]
                self._remote(position, sems, arr, slot, self.SIBLING, held, held).wait_send()
            else:
                self._remote(position, sems, arr, slot, to_mask, src, dst).wait_send()


def _call(body, *, name, grid, in_specs, out_specs, out_shape, args, scratch_shapes=(), exchange=None):
    single = not isinstance(out_shape, (list, tuple))
    if single:
        out_specs, out_shape = [out_specs], [out_shape]
    params = _params(len(grid))
    if exchange is None:
        outs = pl.pallas_call(body, name=name, grid=grid, in_specs=in_specs, out_specs=out_specs, out_shape=out_shape,
                              scratch_shapes=list(scratch_shapes), compiler_params=params)(*args)
        return outs[0] if single else outs
    counts = (len(args), len(exchange.args), len(out_shape), len(exchange.out_shape), len(scratch_shapes), 3)

    def wrapped(*refs):
        groups, at = [], 0
        for n in counts:
            groups.append(refs[at:at + n])
            at += n
        main_in, ex_in, main_out, ex_out, main_scratch, sems = groups
        x, y, c = lax.axis_index("x"), lax.axis_index("y"), lax.axis_index("c")
        position = (x, y, c, 4 * x + 2 * y + c)
        ids = [pl.program_id(a) for a in range(len(grid))]
        first = functools.reduce(jnp.logical_and, [i == 0 for i in ids])
        last = functools.reduce(jnp.logical_and, [i == g - 1 for i, g in zip(ids, grid)])

        @pl.when(first)
        def _():
            exchange.start(position, ex_in, ex_out, sems)

        body(*main_in, *main_out, *main_scratch)

        @pl.when(last)
        def _():
            exchange.wait(position, ex_in, ex_out, sems)

    hbm = pl.BlockSpec(memory_space=pl.ANY)
    outs = pl.pallas_call(
        wrapped, name=name, grid=grid, in_specs=list(in_specs) + [hbm] * counts[1], out_specs=list(out_specs) + [hbm] * counts[3],
        out_shape=list(out_shape) + exchange.out_shape, scratch_shapes=list(scratch_shapes) + exchange.semaphores(),
        compiler_params=params)(*args, *exchange.args)
    main = outs[:counts[2]]
    return (main[0] if single else main), outs[counts[2]:]


def run_exchange(exchange, name):
    return _call(lambda: None, name=name, grid=(1,), in_specs=[], out_specs=[], out_shape=[], args=[], exchange=exchange)[1]


def gather_matmul(x, g, w_block, small_block, tm):
    s, d = x.shape
    width = w_block.shape[1]
    tm = min(tm, s)
    n_i = s // tm
    sibling = Exchange.SIBLING
    y_nbr, x_nbr, diagonal = Exchange.OTHER_CHIPS

    def links(core):
        return (y_nbr, x_nbr) if core == 1 else (x_nbr, y_nbr)

    def block_order(core):
        first, second = links(core)
        return [0, sibling, first, second | sibling, second, first | sibling, diagonal, diagonal | sibling]

    def body(order_ref, x_ref, g_ref, wb_ref, sb_ref, proj_ref, h_ref, w_ref, small_ref, h_all, w_buf, send, recv, local, load_sem):
        j, i = pl.program_id(0), pl.program_id(1)
        xx, yy, cc = lax.axis_index("x"), lax.axis_index("y"), lax.axis_index("c")
        me = 4 * xx + 2 * yy + cc

        def block_of(dev):
            return w_ref.at[:, pl.ds(pl.multiple_of(dev * width, LANES), width)]

        def half_of(dev, part):
            return w_ref.at[pl.ds(part * (d // 2), d // 2), pl.ds(pl.multiple_of(dev * width, LANES), width)]

        def remote(arr, slot, to_mask, src, dst):
            return pltpu.make_async_remote_copy(src_ref=src, dst_ref=dst, send_sem=send.at[arr, slot - 1], recv_sem=recv.at[arr, slot - 1],
                                                device_id=_peer(xx, yy, cc, to_mask)[0], device_id_type=MESH_ID)

        def mine_to(mask):
            return remote(0, mask, mask, wb_ref, block_of(me))

        def arrival(mask):
            return remote(0, mask, mask, wb_ref, block_of(me ^ mask))

        def to_sibling(mask):
            return remote(0, mask | sibling, sibling, block_of(me ^ mask), block_of(me ^ mask))

        def relay(of):
            along_x = of == y_nbr
            part = 0 if along_x else 1
            return remote(0 if along_x else 2, diagonal, x_nbr if along_x else y_nbr, half_of(me ^ of, part), half_of(me ^ of, part))

        def diagonal_half(part):
            return remote(0 if part == 0 else 2, diagonal, x_nbr if part == 0 else y_nbr, wb_ref.at[pl.ds(0, d // 2), :],
                          half_of(me ^ diagonal, part))

        @pl.when((j == 0) & (i == 0))
        def _():
            pltpu.make_async_copy(wb_ref, block_of(me), local.at[0]).start()
            pltpu.make_async_copy(sb_ref, small_ref.at[me], local.at[1]).start()
            mine_to(sibling).start()
            for mask in range(1, N_DEV):
                remote(1, mask, mask, sb_ref, small_ref.at[me]).start()

        def load(step):
            return pltpu.make_async_copy(w_ref.at[:, pl.ds(pl.multiple_of(order_ref[step] * width, LANES), width)],
                                         w_buf.at[step % 2], load_sem.at[step % 2])

        for core in range(2):
            first, second = links(core)
            for step in range(N_DEV):
                at_step = (j == 0) & (i == 0) if step == 0 else (j == step - 1) & (i == n_i - 1)

                @pl.when(at_step & (cc == core))
                def _(step=step, first=first, second=second):
                    if step == 0:
                        mine_to(first).start()
                        pltpu.make_async_copy(wb_ref, block_of(me), local.at[0]).wait()
                    elif step == 1:
                        arrival(sibling).wait_recv()
                    elif step == 2:
                        arrival(first).wait_recv()
                        to_sibling(first).start()
                        mine_to(first).wait_send()
                        mine_to(second).start()
                        relay(first).start()
                    elif step == 3:
                        arrival(second | sibling).wait_recv()
                    elif step == 4:
                        arrival(second).wait_recv()
                        to_sibling(second).start()
                        relay(second).start()
                    elif step == 5:
                        arrival(first | sibling).wait_recv()
                    elif step == 6:
                        diagonal_half(0).wait_recv()
                        diagonal_half(1).wait_recv()
                        to_sibling(diagonal).start()
                    else:
                        arrival(diagonal | sibling).wait_recv()

        @pl.when((j == 0) & (i == 0))
        def _():
            load(0).start()

        @pl.when(i == 0)
        def _():
            load(j).wait()

        @pl.when((i == n_i - 1) & (j < N_DEV - 1))
        def _():
            load(j + 1).start()

        rows = pl.ds(pl.multiple_of(i * tm, tm), tm)

        @pl.when(j == 0)
        def _():
            h = _rms(x_ref[...], g_ref[...]).astype(BF16)
            h_all[rows, :] = h
            h_ref[...] = h

        proj_ref[...] = jnp.dot(h_all[rows, :], w_buf[j % 2], preferred_element_type=F32)

        @pl.when((j == N_DEV - 1) & (i == n_i - 1))
        def _():
            pltpu.make_async_copy(sb_ref, small_ref.at[me], local.at[1]).wait()
            for mask in range(1, N_DEV):
                remote(1, mask, mask, sb_ref, small_ref.at[me ^ mask]).wait_recv()
                remote(1, mask, mask, sb_ref, small_ref.at[me]).wait_send()
            mine_to(sibling).wait_send()
            for core in range(2):
                @pl.when(cc == core)
                def _(core=core):
                    mine_to(links(core)[1]).wait_send()
            for mask in Exchange.OTHER_CHIPS:
                to_sibling(mask).wait_send()
            relay(y_nbr).wait_send()
            relay(x_nbr).wait_send()

    def first_pass_row(j, i, order):
        return jnp.where(j == 0, i, n_i - 1), 0

    hbm = pl.BlockSpec(memory_space=pl.ANY)
    core = lax.axis_index("c")
    me = 4 * lax.axis_index("x") + 2 * lax.axis_index("y") + core
    order = (me ^ jnp.where(core == 1, jnp.array(block_order(1)), jnp.array(block_order(0)))).astype(jnp.int32)
    grid_spec = pltpu.PrefetchScalarGridSpec(
        num_scalar_prefetch=1, grid=(N_DEV, n_i),
        in_specs=[pl.BlockSpec((tm, d), first_pass_row), pl.BlockSpec((1, d), lambda j, i, order: (0, 0)), hbm, hbm],
        out_specs=[pl.BlockSpec((tm, width), lambda j, i, order: (i, order[j])), pl.BlockSpec((tm, d), first_pass_row), hbm, hbm],
        scratch_shapes=[pltpu.VMEM((s, d), BF16), pltpu.VMEM((2, d, width), BF16), pltpu.SemaphoreType.DMA((3, N_DEV - 1)),
                        pltpu.SemaphoreType.DMA((3, N_DEV - 1)), pltpu.SemaphoreType.DMA((2,)), pltpu.SemaphoreType.DMA((2,))])
    return pl.pallas_call(
        body, name="even_in", grid_spec=grid_spec,
        out_shape=[jax.ShapeDtypeStruct((s, N_DEV * width), F32), jax.ShapeDtypeStruct((s, d), BF16),
                   jax.ShapeDtypeStruct((d, N_DEV * width), w_block.dtype), jax.ShapeDtypeStruct((N_DEV,) + small_block.shape, small_block.dtype)],
        compiler_params=_params(2),
    )(order, x, g, w_block, small_block)


def rms_matmul(x, g, w, tm, tn, name, exchange=None):
    s, d = x.shape
    n = w.shape[1]
    tm = min(tm, s)

    def body(x_ref, g_ref, w_ref, o_ref, h_ref):
        @pl.when(pl.program_id(1) == 0)
        def _():
            h_ref[...] = _rms(x_ref[...], g_ref[...]).astype(BF16)

        o_ref[...] = jnp.dot(h_ref[...], w_ref[...], preferred_element_type=F32)

    return _call(
        body, name=name, grid=(s // tm, n // tn),
        in_specs=[pl.BlockSpec((tm, d), lambda i, j: (i, 0)), _full((1, d)), pl.BlockSpec((d, tn), lambda i, j: (0, j))],
        out_specs=[pl.BlockSpec((tm, tn), lambda i, j: (i, j)), pl.BlockSpec((tm, d), lambda i, j: (i, 0))],
        out_shape=[jax.ShapeDtypeStruct((s, n), F32), jax.ShapeDtypeStruct((s, d), BF16)],
        args=[x, g, w], exchange=exchange)


def _gla_out_bwd(du, r, osum, gn, do_ref, dr_ref, dgn_ref):
    silu_r, dsilu_r = _silu_and_grad(r)
    for head in range(GLA_HEADS):
        vl = slice(head * GLA_DV, (head + 1) * GLA_DV)
        o_h, g_h, du_h = osum[:, vl], gn[:, vl], du[:, vl]
        dr_ref[:, vl] = (du_h * _rms(o_h, g_h) * dsilu_r[:, vl]).astype(BF16)
        do_h, dg_h = _rms_bwd(o_h, g_h, du_h * silu_r[:, vl])
        do_ref[:, vl] = do_h
        dgn_ref[...] += dg_h


def normbwd_matmul_nt(y, g, dout, w, tn, name, exchange=None, gla=None):
    s, d = y.shape
    n = w.shape[0]
    tm = min(MM_TILE, s)

    def body(*refs):
        if gla is None:
            y_ref, g_ref, dout_ref, w_ref, du_ref, dy_ref, dg_ref = refs
        else:
            y_ref, g_ref, dout_ref, w_ref, r_ref, o_ref, gn_ref, do_ref, dr_ref, dy_ref, dg_ref, dgn_ref = refs
        i, j = pl.program_id(0), pl.program_id(1)

        @pl.when(j == 0)
        def _():
            dy, dg = _rms_bwd(y_ref[...], g_ref[...], dout_ref[...])
            dy_ref[...] = dy.astype(BF16)

            @pl.when(i == 0)
            def _():
                dg_ref[...] = jnp.zeros_like(dg_ref)
                if gla is not None:
                    dgn_ref[...] = jnp.zeros_like(dgn_ref)

            dg_ref[...] += dg

        du = lax.dot_general(dy_ref[...], w_ref[...], (((1,), (1,)), ((), ())), preferred_element_type=F32)
        if gla is None:
            du_ref[...] = du
        else:
            _gla_out_bwd(du, r_ref[...], o_ref[...], gn_ref[...], do_ref, dr_ref, dgn_ref)

    row = pl.BlockSpec((tm, d), lambda i, j: (i, 0))
    in_specs = [row, _full((1, d)), row, pl.BlockSpec((tn, d), lambda i, j: (j, 0))]
    args = [y, g, dout, w]
    tail_specs = [row, _full((1, d))]
    tail_shapes = [jax.ShapeDtypeStruct((s, d), BF16), jax.ShapeDtypeStruct((1, d), F32)]
    if gla is None:
        out_specs = [pl.BlockSpec((tm, tn), lambda i, j: (i, j))] + tail_specs
        out_shape = [jax.ShapeDtypeStruct((s, n), F32)] + tail_shapes
    else:
        proj, osum, gnorm = gla
        assert n == tn == D_MODEL
        in_specs += [pl.BlockSpec((tm, D_MODEL), lambda i, j: (i, 2)), row, _full(gnorm.shape)]
        args += [proj, osum, gnorm]
        out_specs = [row, row] + tail_specs + [_full((1, GLA_DV))]
        out_shape = [jax.ShapeDtypeStruct((s, D_MODEL), F32), jax.ShapeDtypeStruct((s, D_MODEL), BF16)] + tail_shapes + [
            jax.ShapeDtypeStruct((1, GLA_DV), F32)]
    return _call(body, name=name, grid=(s // tm, n // tn), in_specs=in_specs, out_specs=out_specs, out_shape=out_shape,
                 args=args, exchange=exchange)


def matmul_tn(a, b, tm, tn, ts, out_dtype, name, exchange=None, b_first=None):
    s, m = a.shape
    n = b.shape[1] + (0 if b_first is None else tn)
    ts = min(ts, s)
    n_k = s // ts
    dims = (((0,), (0,)), ((), ()))

    def body(*refs):
        if b_first is None:
            a_ref, b_ref, o_ref, acc = refs
        else:
            a_ref, first_ref, b_ref, o_ref, acc = refs
        j, k = pl.program_id(1), pl.program_id(2)

        @pl.when(k == 0)
        def _():
            acc[...] = jnp.zeros_like(acc)

        if b_first is None:
            acc[...] += lax.dot_general(a_ref[...], b_ref[...], dims, preferred_element_type=F32)
        else:
            @pl.when(j == 0)
            def _():
                acc[...] += lax.dot_general(a_ref[...], first_ref[...], dims, preferred_element_type=F32)

            @pl.when(j > 0)
            def _():
                acc[...] += lax.dot_general(a_ref[...], b_ref[...], dims, preferred_element_type=F32)

        @pl.when(k == n_k - 1)
        def _():
            o_ref[...] = acc[...].astype(out_dtype)

    if b_first is None:
        b_specs, b_args = [pl.BlockSpec((ts, tn), lambda i, j, k: (k, j))], [b]
    else:
        b_specs = [pl.BlockSpec((ts, tn), lambda i, j, k: (k, 0)), pl.BlockSpec((ts, tn), lambda i, j, k: (k, jnp.maximum(j - 1, 0)))]
        b_args = [b_first, b]
    return _call(
        body, name=name, grid=(m // tm, n // tn, n_k),
        in_specs=[pl.BlockSpec((ts, tm), lambda i, j, k: (k, i))] + b_specs,
        out_specs=pl.BlockSpec((tm, tn), lambda i, j, k: (i, j)),
        out_shape=jax.ShapeDtypeStruct((m, n), out_dtype),
        scratch_shapes=[pltpu.VMEM((tm, tn), F32)], args=[a] + b_args, exchange=exchange)


def matmul_nt_normbwd(dproj, w, x, g, dres, tm, tk, name, exchange=None, first=None):
    s, kt = dproj.shape
    kt += 0 if first is None else tk
    d = w.shape[0]
    tm = min(tm, s)
    n_k = kt // tk
    dims = (((1,), (1,)), ((), ()))

    def body(*refs):
        if first is None:
            a_ref, w_ref, x_ref, g_ref, r_ref, dx_ref, dg_ref, acc = refs
        else:
            first_ref, a_ref, w_ref, x_ref, g_ref, r_ref, dx_ref, dg_ref, acc = refs
        i, k = pl.program_id(0), pl.program_id(1)

        @pl.when(k == 0)
        def _():
            acc[...] = jnp.zeros_like(acc)

        if first is None:
            acc[...] += lax.dot_general(a_ref[...], w_ref[...], dims, preferred_element_type=F32)
        else:
            @pl.when(k == 0)
            def _():
                acc[...] += lax.dot_general(first_ref[...], w_ref[...], dims, preferred_element_type=F32)

            @pl.when(k > 0)
            def _():
                acc[...] += lax.dot_general(a_ref[...], w_ref[...], dims, preferred_element_type=F32)

        @pl.when(k == n_k - 1)
        def _():
            dx, dg = _rms_bwd(x_ref[...], g_ref[...], acc[...])
            dx_ref[...] = r_ref[...] + dx

            @pl.when(i == 0)
            def _():
                dg_ref[...] = jnp.zeros_like(dg_ref)

            dg_ref[...] += dg

    row = pl.BlockSpec((tm, d), lambda i, k: (i, 0))
    if first is None:
        a_specs, a_args = [pl.BlockSpec((tm, tk), lambda i, k: (i, k))], [dproj]
    else:
        a_specs = [pl.BlockSpec((tm, tk), lambda i, k: (i, 0)), pl.BlockSpec((tm, tk), lambda i, k: (i, jnp.maximum(k - 1, 0)))]
        a_args = [first, dproj]
    return _call(
        body, name=name, grid=(s // tm, n_k),
        in_specs=a_specs + [pl.BlockSpec((d, tk), lambda i, k: (0, k)), row, _full((1, d)), row],
        out_specs=[row, _full((1, d))],
        out_shape=[jax.ShapeDtypeStruct((s, d), F32), jax.ShapeDtypeStruct((1, d), F32)],
        scratch_shapes=[pltpu.VMEM((tm, d), F32)], args=a_args + [w, x, g, dres], exchange=exchange)


def _rg_conv(xa, before, after, cw, cb):
    return (cw[0:1, :] * _shift_rows(xa, before, after, -2) + cw[1:2, :] * _shift_rows(xa, before, after, -1)
            + cw[2:3, :] * xa + cw[3:4, :] * _shift_rows(xa, before, after, 1) + cb)


def _rg_gates(ua_h, gw_ref, gb_ref, c_h, direction, head):
    r = _sigmoid(_bdot(ua_h, gw_ref[2 * direction, head]) + gb_ref[2 * direction, head:head + 1, :])
    i = _sigmoid(_bdot(ua_h, gw_ref[2 * direction + 1, head]) + gb_ref[2 * direction + 1, head:head + 1, :])
    log_a = -c_h * r
    a = jnp.exp(log_a)
    beta_sq = -jnp.tanh(log_a) * (1.0 + a * a)
    inv_beta = lax.rsqrt(jnp.maximum(beta_sq, SMALLEST_NORMAL))
    return r, i, a, beta_sq * inv_beta, inv_beta


def even_gates_fwd(proj, conv_w, conv_b, gate_w, gate_b, lam, exchange=None):
    s = proj.shape[0]
    ts = min(2 * ROW_TILE, s)
    n_tiles = s // ts

    def body(xa_ref, xb_ref, xn_ref, cw_ref, cb_ref, gw_ref, gb_ref, lam_ref, o_ref, hf_ref, carry):
        @pl.when(pl.program_id(0) == 0)
        def _():
            carry[...] = jnp.zeros_like(carry)

        xa, before, after = _halo_load(xa_ref, xb_ref, xn_ref, n_tiles)
        ua = _rg_conv(xa, before, after, cw_ref[...], cb_ref[...])
        c = RG_C * _softplus(-lam_ref[...])
        ua_bf16 = ua.astype(BF16)
        for direction in range(2):
            for head in range(RG_HEADS):
                lanes = slice(head * RG_HEAD_DIM, (head + 1) * RG_HEAD_DIM)
                ua_h = ua[:, lanes]
                _, i, a, beta, _ = _rg_gates(ua_bf16[:, lanes], gw_ref, gb_ref, c[direction:direction + 1, lanes], direction, head)
                o_ref[2 * direction, :, lanes] = a
                o_ref[2 * direction + 1, :, lanes] = beta * (i * ua_h)
        _scan_tile(o_ref.at[0], o_ref.at[1], hf_ref, carry, False, False)

    return _call(
        body, name="even_gates_fwd", grid=(n_tiles,),
        in_specs=_halo_specs(ts, s, D_MODEL, 0) + [_full(conv_w.shape), _full(conv_b.shape), _full(gate_w.shape),
                                                   _full(gate_b.shape), _full(lam.shape)],
        out_specs=[pl.BlockSpec((4, ts, D_MODEL), lambda i: (0, i, 0)), pl.BlockSpec((ts, D_MODEL), lambda i: (i, 0))],
        out_shape=[jax.ShapeDtypeStruct((4, s, D_MODEL), F32), jax.ShapeDtypeStruct((s, D_MODEL), F32)],
        scratch_shapes=[pltpu.VMEM((SUBLANES, D_MODEL), F32)],
        args=[proj, proj, proj, conv_w, conv_b, gate_w, gate_b, lam], exchange=exchange)


def _scan_tile(a_ref, b_ref, h_ref, carry, reverse, b_times_a):
    ts, c = h_ref.shape
    n_blocks = ts // SUBLANES
    row = lax.broadcasted_iota(jnp.int32, (SUBLANES, c), 0)

    def block(j, h_in):
        r0 = pl.multiple_of((n_blocks - 1 - j if reverse else j) * SUBLANES, SUBLANES)
        a = a_ref[pl.ds(r0, SUBLANES), :]
        b = b_ref[pl.ds(r0, SUBLANES), :]
        if b_times_a:
            b = a * b
        for step in (1, 2, 4):
            shift = SUBLANES - step if reverse else step
            valid = row < SUBLANES - step if reverse else row >= step
            b = jnp.where(valid, a * pltpu.roll(b, shift, 0) + b, b)
            a = jnp.where(valid, a * pltpu.roll(a, shift, 0), a)
        h = a * h_in + b
        h_ref[pl.ds(r0, SUBLANES), :] = h
        return h[0:1, :] if reverse else h[SUBLANES - 1:SUBLANES, :]

    carry[0:1, :] = lax.fori_loop(0, n_blocks, block, carry[0:1, :])


def linear_scan(a_arr, a_idx, b_arr, b_idx, reverse, b_times_a, name, exchange=None):
    _, s, c = a_arr.shape
    ts = min(MM_TILE, s)
    n_tiles = s // ts

    def tile_of(i):
        return n_tiles - 1 - i if reverse else i

    def body(a_ref, b_ref, h_ref, carry):
        @pl.when(pl.program_id(0) == 0)
        def _():
            carry[...] = jnp.zeros_like(carry)

        _scan_tile(a_ref, b_ref, h_ref, carry, reverse, b_times_a)

    return _call(
        body, name=name, grid=(n_tiles,),
        in_specs=[pl.BlockSpec((None, ts, c), lambda i: (a_idx, tile_of(i), 0)),
                  pl.BlockSpec((None, ts, c), lambda i: (b_idx, tile_of(i), 0))],
        out_specs=pl.BlockSpec((ts, c), lambda i: (tile_of(i), 0)),
        out_shape=jax.ShapeDtypeStruct((s, c), F32),
        scratch_shapes=[pltpu.VMEM((SUBLANES, c), F32)], args=[a_arr, b_arr], exchange=exchange)


def _sc_conv(p, before, after, w):
    return w[0:1, :] * _shift_rows(p, before, after, -1) + w[1:2, :] * p + w[2:3, :] * _shift_rows(p, before, after, 1)


def even_mix_fwd(ab, hf, proj, sc_w, w_out, xres, g_post, exchange=None):
    s = proj.shape[0]
    ts = min(ROW_TILE, s)
    n_tiles = s // ts

    def tile(i):
        return n_tiles - 1 - i

    row = pl.BlockSpec((ts, D_MODEL), lambda i: (tile(i), 0))

    def col(c):
        return pl.BlockSpec((ts, D_MODEL), lambda i: (tile(i), c))

    def body(a_ref, b_ref, hf_ref, za_ref, xb_ref, xbb_ref, xbn_ref, gb_ref, gc_ref, gcb_ref, gcn_ref, zb_ref, w_ref,
             wo_ref, x_ref, g_ref, u_ref, hb_ref, y_ref, out_ref, carry):
        @pl.when(pl.program_id(0) == 0)
        def _():
            carry[...] = jnp.zeros_like(carry)

        _scan_tile(a_ref, b_ref, hb_ref, carry, True, False)
        xb, xb_before, xb_after = _halo_load(xb_ref, xbb_ref, xbn_ref, n_tiles, tile)
        gc, gc_before, gc_after = _halo_load(gc_ref, gcb_ref, gcn_ref, n_tiles, tile)
        silu_za, _ = _silu_and_grad(za_ref[...])
        silu_zb, _ = _silu_and_grad(zb_ref[...])
        u_ref[:, :D_MODEL] = ((hf_ref[...] + hb_ref[...]) * silu_za).astype(BF16)
        cv = _sc_conv(gc * xb, gc_before * xb_before, gc_after * xb_after, w_ref[...])
        u_ref[:, D_MODEL:] = (gb_ref[...] * cv * silu_zb).astype(BF16)
        y = jnp.dot(u_ref[...], wo_ref[...], preferred_element_type=F32)
        y_ref[...] = y
        out_ref[...] = x_ref[...] + _rms(y, g_ref[...])

    return _call(
        body, name="even_mix_fwd", grid=(n_tiles,),
        in_specs=[pl.BlockSpec((None, ts, D_MODEL), lambda i: (2, tile(i), 0)), pl.BlockSpec((None, ts, D_MODEL), lambda i: (3, tile(i), 0)),
                  row, col(1)] + _halo_specs(ts, s, D_MODEL, 2, tile) + [col(3)] + _halo_specs(ts, s, D_MODEL, 4, tile)
        + [col(5), _full(sc_w.shape), _full(w_out.shape), row, _full(g_post.shape)],
        out_specs=[pl.BlockSpec((ts, 2 * D_MODEL), lambda i: (tile(i), 0)), row, row, row],
        out_shape=[jax.ShapeDtypeStruct((s, 2 * D_MODEL), BF16)] + [jax.ShapeDtypeStruct((s, D_MODEL), F32)] * 3,
        scratch_shapes=[pltpu.VMEM((SUBLANES, D_MODEL), F32)],
        args=[ab, ab, hf, proj, proj, proj, proj, proj, proj, proj, proj, proj, sc_w, w_out, xres, g_post], exchange=exchange)


def even_mix_bwd(du, hf, hb, proj, sc_w, ab, exchange=None):
    s = proj.shape[0]
    ts = min(ROW_TILE, s)
    n_tiles = s // ts
    row = pl.BlockSpec((ts, D_MODEL), lambda i: (i, 0))

    def body(dya_ref, dyb_ref, dybb_ref, dybn_ref, hf_ref, hb_ref, za_ref, xb_ref, xbb_ref, xbn_ref,
             gb_ref, gbb_ref, gbn_ref, gc_ref, gcb_ref, gcn_ref, zb_ref, zbb_ref, zbn_ref, w_ref, a_ref,
             dh_ref, dp_ref, dw_ref, adj_ref, carry):
        @pl.when(pl.program_id(0) == 0)
        def _():
            carry[...] = jnp.zeros_like(carry)

        dyb, dyb_before, dyb_after = _halo_load(dyb_ref, dybb_ref, dybn_ref, n_tiles)
        xb, xb_before, xb_after = _halo_load(xb_ref, xbb_ref, xbn_ref, n_tiles)
        gb, gb_before, gb_after = _halo_load(gb_ref, gbb_ref, gbn_ref, n_tiles)
        gc, gc_before, gc_after = _halo_load(gc_ref, gcb_ref, gcn_ref, n_tiles)
        zb, zb_before, zb_after = _halo_load(zb_ref, zbb_ref, zbn_ref, n_tiles)
        w = w_ref[...]
        dya, za = dya_ref[...], za_ref[...]
        silu_za, dsilu_za = _silu_and_grad(za)
        dh_ref[...] = dya * silu_za
        _scan_tile(a_ref, dh_ref, adj_ref, carry, False, True)
        dp_ref[:, 0:D_MODEL] = (dya * (hf_ref[...] + hb_ref[...]) * dsilu_za).astype(BF16)

        silu_zb, dsilu_zb = _silu_and_grad(zb)
        p, p_before, p_after = gc * xb, gc_before * xb_before, gc_after * xb_after
        cv = _sc_conv(p, p_before, p_after, w)
        dcv = dyb * gb * silu_zb
        dcv_before = dyb_before * gb_before * _silu_and_grad(zb_before)[0]
        dcv_after = dyb_after * gb_after * _silu_and_grad(zb_after)[0]
        dpp = (w[0:1, :] * _shift_rows(dcv, dcv_before, dcv_after, 1) + w[1:2, :] * dcv
               + w[2:3, :] * _shift_rows(dcv, dcv_before, dcv_after, -1))
        dp_ref[:, D_MODEL:2 * D_MODEL] = (dpp * gc).astype(BF16)
        dp_ref[:, 2 * D_MODEL:3 * D_MODEL] = (dyb * cv * silu_zb).astype(BF16)
        dp_ref[:, 3 * D_MODEL:4 * D_MODEL] = (dpp * xb).astype(BF16)
        dp_ref[:, 4 * D_MODEL:5 * D_MODEL] = (dyb * gb * cv * dsilu_zb).astype(BF16)

        @pl.when(pl.program_id(0) == 0)
        def _():
            dw_ref[...] = jnp.zeros_like(dw_ref)

        dw_ref[0:1, :] += jnp.sum(dcv * _shift_rows(p, p_before, p_after, -1), axis=0, keepdims=True)
        dw_ref[1:2, :] += jnp.sum(dcv * p, axis=0, keepdims=True)
        dw_ref[2:3, :] += jnp.sum(dcv * _shift_rows(p, p_before, p_after, 1), axis=0, keepdims=True)

    return _call(
        body, name="even_mix_bwd", grid=(n_tiles,),
        in_specs=[row] + _halo_specs(ts, s, D_MODEL, 1) + [row, row, pl.BlockSpec((ts, D_MODEL), lambda i: (i, 1))]
        + _halo_specs(ts, s, D_MODEL, 2) + _halo_specs(ts, s, D_MODEL, 3) + _halo_specs(ts, s, D_MODEL, 4)
        + _halo_specs(ts, s, D_MODEL, 5) + [_full(sc_w.shape), pl.BlockSpec((None, ts, D_MODEL), lambda i: (2, i, 0))],
        out_specs=[row, pl.BlockSpec((ts, 5 * D_MODEL), lambda i: (i, 0)), _full(sc_w.shape), row],
        out_shape=[jax.ShapeDtypeStruct((s, D_MODEL), F32), jax.ShapeDtypeStruct((s, 5 * D_MODEL), BF16),
                   jax.ShapeDtypeStruct(sc_w.shape, F32), jax.ShapeDtypeStruct((s, D_MODEL), F32)],
        scratch_shapes=[pltpu.VMEM((SUBLANES, D_MODEL), F32)],
        args=[du, du, du, du, hf, hb, proj, *([proj] * 12), sc_w, ab], exchange=exchange)


def even_gates_bwd(proj, adj_f, adj_b, hf, hb, dh, conv_w, conv_b, gate_w, gate_b, lam, exchange=None):
    s = proj.shape[0]
    ts = min(2 * ROW_TILE, s)
    n_tiles = s // ts
    row = pl.BlockSpec((ts, D_MODEL), lambda i: (i, 0))

    def body(xa_ref, xab_ref, xan_ref, af_ref, afb_ref, afn_ref, ab_ref, abb_ref, abn_ref,
             hf_ref, hfb_ref, hfn_ref, hb_ref, hbb_ref, hbn_ref, dh_ref,
             cw_ref, cb_ref, gw_ref, gb_ref, lam_ref, dua_ref, dgw_ref, dgb_ref, dlam_ref):
        @pl.when(pl.program_id(0) == 0)
        def _():
            dgw_ref[...] = jnp.zeros_like(dgw_ref)
            dgb_ref[...] = jnp.zeros_like(dgb_ref)
            dlam_ref[...] = jnp.zeros_like(dlam_ref)

        xa, before, after = _halo_load(xa_ref, xab_ref, xan_ref, n_tiles)
        ua = _rg_conv(xa, before, after, cw_ref[...], cb_ref[...])
        lam_v = lam_ref[...]
        c = RG_C * _softplus(-lam_v)
        dc_dlam = -RG_C * _sigmoid(-lam_v)
        dh = dh_ref[...]
        adj = (_halo_load(af_ref, afb_ref, afn_ref, n_tiles), _halo_load(ab_ref, abb_ref, abn_ref, n_tiles))
        hs = (_halo_load(hf_ref, hfb_ref, hfn_ref, n_tiles), _halo_load(hb_ref, hbb_ref, hbn_ref, n_tiles))
        dua = jnp.zeros_like(ua)
        ua_bf16 = ua.astype(BF16)
        for direction in range(2):
            step = 1 if direction == 0 else -1
            g = dh + _shift_rows(*adj[direction], step)
            da_all = g * _shift_rows(*hs[direction], -step)
            dua_parts = []
            for head in range(RG_HEADS):
                lanes = slice(head * RG_HEAD_DIM, (head + 1) * RG_HEAD_DIM)
                ua_h = ua[:, lanes]
                c_h = c[direction:direction + 1, lanes]
                ua_hb = ua_bf16[:, lanes]
                r, i, a, beta, inv_beta = _rg_gates(ua_hb, gw_ref, gb_ref, c_h, direction, head)
                db_beta = g[:, lanes] * beta
                d_i = db_beta * ua_h
                dbeta = g[:, lanes] * (i * ua_h)
                dlog_a = (da_all[:, lanes] - dbeta * a * inv_beta) * a
                dpr = -c_h * dlog_a * r * (1.0 - r)
                dpi = d_i * i * (1.0 - i)
                dpr_b, dpi_b = dpr.astype(BF16), dpi.astype(BF16)
                dua_parts.append(db_beta * i + _bdot_nt(dpr_b, gw_ref[2 * direction, head])
                                 + _bdot_nt(dpi_b, gw_ref[2 * direction + 1, head]))
                dgw_ref[2 * direction, head] += _bdot_tn(ua_hb, dpr_b)
                dgw_ref[2 * direction + 1, head] += _bdot_tn(ua_hb, dpi_b)
                dgb_ref[2 * direction, head:head + 1, :] += jnp.sum(dpr, axis=0, keepdims=True)
                dgb_ref[2 * direction + 1, head:head + 1, :] += jnp.sum(dpi, axis=0, keepdims=True)
                dlam_ref[direction:direction + 1, lanes] += (
                    jnp.sum(-r * dlog_a, axis=0, keepdims=True) * dc_dlam[direction:direction + 1, lanes])
            dua = dua + jnp.concatenate(dua_parts, axis=1)
        dua_ref[...] = dua

    return _call(
        body, name="even_gates_bwd", grid=(n_tiles,),
        in_specs=_halo_specs(ts, s, D_MODEL, 0) * 5 + [row] + [_full(conv_w.shape), _full(conv_b.shape), _full(gate_w.shape),
                                                             _full(gate_b.shape), _full(lam.shape)],
        out_specs=[row, _full(gate_w.shape), _full(gate_b.shape), _full(lam.shape)],
        out_shape=[jax.ShapeDtypeStruct((s, D_MODEL), F32), jax.ShapeDtypeStruct(gate_w.shape, F32),
                   jax.ShapeDtypeStruct(gate_b.shape, F32), jax.ShapeDtypeStruct(lam.shape, F32)],
        args=[proj, proj, proj, adj_f, adj_f, adj_f, adj_b, adj_b, adj_b, hf, hf, hf, hb, hb, hb, dh, conv_w, conv_b, gate_w,
              gate_b, lam], exchange=exchange)


def rg_conv_bwd(dua, proj, conv_w, exchange=None):
    s = proj.shape[0]
    ts = min(2 * ROW_TILE, s)
    n_tiles = s // ts

    def body(du_ref, dub_ref, dun_ref, xa_ref, xab_ref, xan_ref, cw_ref, dp_ref, dw_ref, db_ref):
        @pl.when(pl.program_id(0) == 0)
        def _():
            dw_ref[...] = jnp.zeros_like(dw_ref)
            db_ref[...] = jnp.zeros_like(db_ref)

        dua, dua_before, dua_after = _halo_load(du_ref, dub_ref, dun_ref, n_tiles)
        xa, xa_before, xa_after = _halo_load(xa_ref, xab_ref, xan_ref, n_tiles)
        cw = cw_ref[...]
        dxa = (cw[0:1, :] * _shift_rows(dua, dua_before, dua_after, 2) + cw[1:2, :] * _shift_rows(dua, dua_before, dua_after, 1)
               + cw[2:3, :] * dua + cw[3:4, :] * _shift_rows(dua, dua_before, dua_after, -1))
        dp_ref[...] = dxa.astype(BF16)
        for tap, offset in enumerate((-2, -1, 0, 1)):
            shifted = xa if offset == 0 else _shift_rows(xa, xa_before, xa_after, offset)
            dw_ref[tap:tap + 1, :] += jnp.sum(dua * shifted, axis=0, keepdims=True)
        db_ref[...] += jnp.sum(dua, axis=0, keepdims=True)

    return _call(
        body, name="rg_conv_bwd", grid=(n_tiles,),
        in_specs=_halo_specs(ts, s, D_MODEL, 0) * 2 + [_full(conv_w.shape)],
        out_specs=[pl.BlockSpec((ts, D_MODEL), lambda i: (i, 0)), _full(conv_w.shape), _full((1, D_MODEL))],
        out_shape=[jax.ShapeDtypeStruct((s, D_MODEL), BF16), jax.ShapeDtypeStruct(conv_w.shape, F32),
                   jax.ShapeDtypeStruct((1, D_MODEL), F32)],
        args=[dua, dua, dua, proj, proj, proj, conv_w], exchange=exchange)


def _split3(x):
    x1 = x.astype(BF16)
    rest = x - x1.astype(F32)
    x2 = rest.astype(BF16)
    return x1, x2, (rest - x2.astype(F32)).astype(BF16)


def _chunk_sum_matrix(t, reverse, transpose):
    i = lax.broadcasted_iota(jnp.int32, (t, t), 0)
    j = lax.broadcasted_iota(jnp.int32, (t, t), 1)
    if transpose:
        i, j = j, i
    same = (i // GLA_CHUNK) == (j // GLA_CHUNK)
    return jnp.where(same & ((j >= i) if reverse else (j <= i)), 1.0, 0.0).astype(BF16)


def _exact_dot(m, x):
    return sum(jnp.dot(m, part, preferred_element_type=F32) for part in _split3(x))


def _chunk_mask(t, reverse):
    i = lax.broadcasted_iota(jnp.int32, (t, t), 0)
    j = lax.broadcasted_iota(jnp.int32, (t, t), 1)
    return ((i // GLA_CHUNK) == (j // GLA_CHUNK)) & ((j >= i) if reverse else (j <= i))


def _chunk_rows(c):
    return slice(c * GLA_CHUNK, (c + 1) * GLA_CHUNK)


def _gla_gate(lr, wg, bg):
    z = _bdot(lr, wg) + bg
    log_alpha = (jnp.minimum(z, 0.0) - jnp.log(1.0 + jnp.exp(-jnp.abs(z)))) * (1.0 / GLA_NORMALIZER)
    return z, log_alpha


def _gla_tile_terms(q, k, bcum, reverse):
    n_chunks = q.shape[0] // GLA_CHUNK
    totals = []
    for c in range(n_chunks):
        edge = c * GLA_CHUNK if reverse else (c + 1) * GLA_CHUNK - 1
        totals.append(bcum[edge:edge + 1, :])
    btot = jnp.concatenate([jnp.broadcast_to(total, (GLA_CHUNK, total.shape[1])) for total in totals], axis=0)
    e_pos, e_neg, e_st = jnp.exp(bcum), jnp.exp(-bcum), jnp.exp(btot - bcum)
    return q * (GLA_DK ** -0.5) * e_pos, k * e_neg, k * e_st, e_pos, e_neg, e_st, [jnp.exp(total) for total in totals]


def _gla_specs(t, n_tiles, reverse_order):
    def tile(i):
        return n_tiles - 1 - i if reverse_order else i

    return tile, [
        pl.BlockSpec((t, GLA_KEY), lambda i: (tile(i), 0)),
        pl.BlockSpec((t, GLA_KEY), lambda i: (tile(i), 1)),
        pl.BlockSpec((t, D_MODEL), lambda i: (tile(i), 1)),
        pl.BlockSpec((t, LANES), lambda i: (tile(i), (ODD_IN_PAD - LANES) // LANES)),
    ]


def gla_fwd(proj, wg, bg, reverse, o_other=None, gnorm=None, post=None):
    s = proj.shape[0]
    t = min(GLA_TILE, s)
    n_tiles = s // t
    n_chunks = t // GLA_CHUNK
    final = o_other is not None
    tile, specs = _gla_specs(t, n_tiles, reverse)

    def body(*refs):
        if final:
            (q_ref, k_ref, v_ref, lr_ref, wg_ref, bg_ref, oo_ref, r_ref, gn_ref, wo_ref, x_ref, gp_ref, t_ref,
             osum_ref, u_ref, st_ref, y_ref, dout_ref, loss_ref, state) = refs
        else:
            q_ref, k_ref, v_ref, lr_ref, wg_ref, bg_ref, o_ref, st_ref, state = refs
            osum_ref = o_ref

        @pl.when(pl.program_id(0) == 0)
        def _():
            state[...] = jnp.zeros_like(state)

        _, log_alpha = _gla_gate(lr_ref[...], wg_ref[...], bg_ref[...])
        bcum = _exact_dot(_chunk_sum_matrix(t, reverse, False), log_alpha)
        q, k, v = q_ref[...], k_ref[...], v_ref[...]
        q_in, k_in, k_st, _, _, _, decays = _gla_tile_terms(q, k, bcum, reverse)
        mask = _chunk_mask(t, reverse)
        order = list(range(n_chunks))[::-1] if reverse else list(range(n_chunks))
        intra, increments = [], []
        for head in range(GLA_HEADS):
            kl = slice(head * GLA_DK, (head + 1) * GLA_DK)
            vl = slice(head * GLA_DV, (head + 1) * GLA_DV)
            scores = jnp.where(mask, _bdot_nt(q_in[:, kl], k_in[:, kl]), 0.0)
            intra.append(_bdot(scores, v[:, vl]))
            increments.append([_bdot_tn(v[_chunk_rows(c), vl], k_st[_chunk_rows(c), kl]) for c in range(n_chunks)])
        for head in range(GLA_HEADS):
            kl = slice(head * GLA_DK, (head + 1) * GLA_DK)
            vl = slice(head * GLA_DV, (head + 1) * GLA_DV)
            running = state[head]
            before = [None] * n_chunks
            for c in order:
                before[c] = running
                st_ref[c, head] = running
                running = running * decays[c][:, kl] + increments[head][c]
            state[head] = running
            inter = [_bdot_nt(q_in[_chunk_rows(c), kl], before[c]) for c in range(n_chunks)]
            osum_ref[:, vl] = intra[head] + jnp.concatenate(inter, axis=0)
        if final:
            osum = osum_ref[...] + oo_ref[...]
            osum_ref[...] = osum
            silu_r, _ = _silu_and_grad(r_ref[...])
            gn = gn_ref[...]
            for head in range(GLA_HEADS):
                vl = slice(head * GLA_DV, (head + 1) * GLA_DV)
                u_ref[:, vl] = (_rms(osum[:, vl], gn[:, vl]) * silu_r[:, vl]).astype(BF16)

            @pl.when(pl.program_id(0) == 0)
            def _():
                loss_ref[...] = jnp.zeros_like(loss_ref)

            y = jnp.dot(u_ref[...], wo_ref[...], preferred_element_type=F32)
            y_ref[...] = y
            diff = x_ref[...] + _rms(y, gp_ref[...]) - t_ref[...]
            dout_ref[...] = diff * (1.0 / D_MODEL)
            loss_ref[...] += 0.5 * jnp.sum(jnp.mean(diff * diff, axis=-1, keepdims=True))

    row = pl.BlockSpec((t, D_MODEL), lambda i: (tile(i), 0))
    st_spec = pl.BlockSpec((n_chunks, GLA_HEADS, GLA_DV, GLA_DK), lambda i: (tile(i), 0, 0, 0))
    st_shape = jax.ShapeDtypeStruct((s // GLA_CHUNK, GLA_HEADS, GLA_DV, GLA_DK), F32)
    in_specs = specs + [_full(wg.shape), _full(bg.shape)]
    args = [proj, proj, proj, proj, wg, bg]
    if final:
        w_out, xres, g_post, target = post
        in_specs += [row, pl.BlockSpec((t, D_MODEL), lambda i: (tile(i), 2)), _full(gnorm.shape), _full(w_out.shape), row,
                     _full(g_post.shape), row]
        args += [o_other, proj, gnorm, w_out, xres, g_post, target]
        out_specs = [row, row, st_spec, row, row, _full((SUBLANES, LANES))]
        out_shape = [jax.ShapeDtypeStruct((s, D_MODEL), F32), jax.ShapeDtypeStruct((s, D_MODEL), BF16), st_shape,
                     jax.ShapeDtypeStruct((s, D_MODEL), F32), jax.ShapeDtypeStruct((s, D_MODEL), F32),
                     jax.ShapeDtypeStruct((SUBLANES, LANES), F32)]
    else:
        out_specs = [row, st_spec]
        out_shape = [jax.ShapeDtypeStruct((s, D_MODEL), F32), st_shape]
    return pl.pallas_call(
        body, name="gla_fwd_rev" if reverse else "gla_fwd", grid=(n_tiles,), in_specs=in_specs, out_specs=out_specs,
        out_shape=out_shape, scratch_shapes=[pltpu.VMEM((GLA_HEADS, GLA_DV, GLA_DK), F32)], compiler_params=_params(1),
    )(*args)


def gla_bwd(proj, wg, bg, do, states, reverse, first=None):
    s = proj.shape[0]
    t = min(GLA_TILE, s)
    n_tiles = s // t
    n_chunks = t // GLA_CHUNK
    final = first is not None
    tile, specs = _gla_specs(t, n_tiles, not reverse)

    def body(*refs):
        if final:
            (q_ref, k_ref, v_ref, lr_ref, wg_ref, bg_ref, do_ref, st_ref, dqkv1_ref, dlr1_ref, dr_ref,
             dp_ref, dwg_ref, dbg_ref, dstate, dqkv, dbc, dbt) = refs
        else:
            (q_ref, k_ref, v_ref, lr_ref, wg_ref, bg_ref, do_ref, st_ref,
             dqkv, dlr_ref, dwg_ref, dbg_ref, dstate, dbc, dbt) = refs

        @pl.when(pl.program_id(0) == 0)
        def _():
            dstate[...] = jnp.zeros_like(dstate)
            dwg_ref[...] = jnp.zeros_like(dwg_ref)
            dbg_ref[...] = jnp.zeros_like(dbg_ref)

        lr, wg_v = lr_ref[...], wg_ref[...]
        z, log_alpha = _gla_gate(lr, wg_v, bg_ref[...])
        bcum = _exact_dot(_chunk_sum_matrix(t, reverse, False), log_alpha)
        q, k, v, do_v = q_ref[...], k_ref[...], v_ref[...], do_ref[...]
        q_in, k_in, k_st, e_pos, e_neg, e_st, decays = _gla_tile_terms(q, k, bcum, reverse)
        mask = _chunk_mask(t, reverse)
        order = list(range(n_chunks)) if reverse else list(range(n_chunks))[::-1]
        q_b, k_b, ks_b, v_b, do_b = (a.astype(BF16) for a in (q_in, k_in, k_st, v, do_v))
        dq_intra, dk_intra, dv_intra, increments = [], [], [], []
        for head in range(GLA_HEADS):
            kl = slice(head * GLA_DK, (head + 1) * GLA_DK)
            vl = slice(head * GLA_DV, (head + 1) * GLA_DV)
            scores = jnp.where(mask, _bdot_nt(q_b[:, kl], k_b[:, kl]), 0.0).astype(BF16)
            dscores = jnp.where(mask, _bdot_nt(do_b[:, vl], v_b[:, vl]), 0.0).astype(BF16)
            dv_intra.append(_bdot_tn(scores, do_b[:, vl]))
            dq_intra.append(_bdot(dscores, k_b[:, kl]))
            dk_intra.append(_bdot_tn(dscores, q_b[:, kl]))
            increments.append([_bdot_tn(do_b[_chunk_rows(c), vl], q_b[_chunk_rows(c), kl]) for c in range(n_chunks)])
        after_all, ddecay_all = [], []
        for head in range(GLA_HEADS):
            kl = slice(head * GLA_DK, (head + 1) * GLA_DK)
            running = dstate[head]
            after, ddecay = [None] * n_chunks, [None] * n_chunks
            for c in order:
                after[c] = running
                ddecay[c] = jnp.sum(running * st_ref[c, head], axis=0, keepdims=True)
                running = running * decays[c][:, kl] + increments[head][c]
            dstate[head] = running
            after_all.append(after)
            ddecay_all.append(ddecay)
        for head in range(GLA_HEADS):
            kl = slice(head * GLA_DK, (head + 1) * GLA_DK)
            vl = slice(head * GLA_DV, (head + 1) * GLA_DV)
            after, ddecay = after_all[head], ddecay_all[head]
            dq_inter = jnp.concatenate([_bdot(do_b[_chunk_rows(c), vl], st_ref[c, head]) for c in range(n_chunks)], axis=0)
            dv_inter = jnp.concatenate([_bdot_nt(ks_b[_chunk_rows(c), kl], after[c]) for c in range(n_chunks)], axis=0)
            dk_st = jnp.concatenate([_bdot(v_b[_chunk_rows(c), vl], after[c]) for c in range(n_chunks)], axis=0)
            dq_in = dq_intra[head] + dq_inter
            ks_h = k_st[:, kl]
            dqkv[:, 2 * GLA_KEY + head * GLA_DV:2 * GLA_KEY + (head + 1) * GLA_DV] = dv_intra[head] + dv_inter
            dqkv[:, kl] = dq_in * (GLA_DK ** -0.5) * e_pos[:, kl]
            dqkv[:, GLA_KEY + head * GLA_DK:GLA_KEY + (head + 1) * GLA_DK] = dk_intra[head] * e_neg[:, kl] + dk_st * e_st[:, kl]
            dbc[:, kl] = dq_in * q_in[:, kl] - dk_intra[head] * k_in[:, kl] - dk_st * ks_h
            weighted = dk_st * ks_h
            for c in range(n_chunks):
                dbtot = jnp.sum(weighted[_chunk_rows(c)], axis=0, keepdims=True) + ddecay[c] * decays[c][:, kl]
                dbt[_chunk_rows(c), kl] = jnp.broadcast_to(dbtot, (GLA_CHUNK, GLA_DK))
        dlog_alpha = _exact_dot(_chunk_sum_matrix(t, reverse, True), dbc[...]) + dbt[...]
        dz = dlog_alpha * _sigmoid(-z) * (1.0 / GLA_NORMALIZER)
        dlr = _bdot_nt(dz, wg_v)
        dwg_ref[...] += _bdot_tn(lr, dz)
        dbg_ref[...] += jnp.sum(dz, axis=0, keepdims=True)
        if final:
            dp_ref[:, :2 * D_MODEL] = (dqkv[...] + dqkv1_ref[...]).astype(BF16)
            dp_ref[:, 2 * D_MODEL:3 * D_MODEL] = dr_ref[...]
            dp_ref[:, 3 * D_MODEL:] = (dlr + dlr1_ref[...]).astype(BF16)
        else:
            dlr_ref[...] = dlr

    row = pl.BlockSpec((t, D_MODEL), lambda i: (tile(i), 0))
    wide = pl.BlockSpec((t, 2 * D_MODEL), lambda i: (tile(i), 0))
    narrow = pl.BlockSpec((t, LANES), lambda i: (tile(i), 0))
    st_spec = pl.BlockSpec((n_chunks, GLA_HEADS, GLA_DV, GLA_DK), lambda i: (tile(i), 0, 0, 0))
    in_specs = specs + [_full(wg.shape), _full(bg.shape), row, st_spec]
    args = [proj, proj, proj, proj, wg, bg, do, states]
    acc_specs = [_full(wg.shape), _full(bg.shape)]
    acc_shapes = [jax.ShapeDtypeStruct(wg.shape, F32), jax.ShapeDtypeStruct(bg.shape, F32)]
    scratch = [pltpu.VMEM((GLA_HEADS, GLA_DV, GLA_DK), F32)]
    work = [pltpu.VMEM((t, GLA_KEY), F32), pltpu.VMEM((t, GLA_KEY), F32)]
    if final:
        in_specs += [wide, narrow, row]
        args += list(first)
        out_specs = [pl.BlockSpec((t, ODD_IN_PAD), lambda i: (tile(i), 0))] + acc_specs
        out_shape = [jax.ShapeDtypeStruct((s, ODD_IN_PAD), BF16)] + acc_shapes
        scratch += [pltpu.VMEM((t, 2 * D_MODEL), F32)] + work
    else:
        out_specs = [wide, narrow] + acc_specs
        out_shape = [jax.ShapeDtypeStruct((s, 2 * D_MODEL), F32), jax.ShapeDtypeStruct((s, LANES), F32)] + acc_shapes
        scratch += work
    return pl.pallas_call(
        body, name="gla_bwd_rev" if reverse else "gla_bwd", grid=(n_tiles,), in_specs=in_specs, out_specs=out_specs,
        out_shape=out_shape, scratch_shapes=scratch, compiler_params=_params(1),
    )(*args)


def pair_sum(grad, from_sibling):
    n_chips, r, w = from_sibling.shape

    def body(even_ref, odd_ref, sib_ref, o_ref):
        mine = jnp.where(lax.axis_index("c") == 1, odd_ref[...], even_ref[...])
        o_ref[...] = (mine.astype(F32) + sib_ref[...].astype(F32)).astype(o_ref.dtype)

    return pl.pallas_call(
        body, name="pair_sum", grid=(n_chips,),
        in_specs=[pl.BlockSpec((r, w), lambda k: (0, 2 * k)), pl.BlockSpec((r, w), lambda k: (0, 2 * k + 1)),
                  pl.BlockSpec((None, r, w), lambda k: (k, 0, 0))],
        out_specs=pl.BlockSpec((None, r, w), lambda k: (k, 0, 0)),
        out_shape=jax.ShapeDtypeStruct(from_sibling.shape, from_sibling.dtype), compiler_params=_params(1),
    )(grad, grad, from_sibling)


def _adamw_update(g, w, m, v):
    new_m = ADAM_B1 * m + (1.0 - ADAM_B1) * g
    new_v = ADAM_B2 * v + (1.0 - ADAM_B2) * (g * g)
    m_hat = new_m / (1.0 - ADAM_B1 ** ADAM_STEP)
    v_hat = new_v / (1.0 - ADAM_B2 ** ADAM_STEP)
    return -ADAM_LR * (m_hat / (jnp.sqrt(v_hat) + ADAM_EPS) + ADAM_WD * w), new_m, new_v


def sum_parts(parts, name):
    _, r, c = parts.shape

    def body(p_ref, o_ref):
        total = p_ref[0].astype(F32)
        for j in range(1, N_DEV):
            total = total + p_ref[j].astype(F32)
        o_ref[...] = total

    return pl.pallas_call(body, name=name, in_specs=[_full(parts.shape)], out_specs=_full((r, c)), grid=(1,),
                          out_shape=jax.ShapeDtypeStruct((r, c), F32), compiler_params=_params(1))(parts)


def adamw(parts, w, m, v, name, exchange=None):
    n, r, c = parts.shape
    tr = r
    while tr * c * 4 > ADAMW_BLOCK_BYTES and tr % (2 * SUBLANES) == 0:
        tr //= 2

    def body(p_ref, w_ref, m_ref, v_ref, g_ref, d_ref, nm_ref, nv_ref):
        g = p_ref[0].astype(F32)
        for j in range(1, n):
            g = g + p_ref[j].astype(F32)
        g_ref[...] = g
        d_ref[...], nm_ref[...], nv_ref[...] = _adamw_update(g, w_ref[...], m_ref[...], v_ref[...])

    row = pl.BlockSpec((tr, c), lambda i: (i, 0))
    return _call(
        body, name=name, grid=(r // tr,),
        in_specs=[pl.BlockSpec((n, tr, c), lambda i: (0, i, 0)), row, row, row], out_specs=[row] * 4,
        out_shape=[jax.ShapeDtypeStruct((r, c), F32)] * 4, args=[parts, w, m, v], exchange=exchange)


def _small_views(shape):
    if len(shape) == 2:
        return [((slice(None), slice(None)), (slice(None), slice(None)))]
    if len(shape) == 3:
        return [((slice(None), slice(None)), (0,))]
    rows = shape[2]
    return [((slice(k * rows, (k + 1) * rows), slice(None)), (0, k)) for k in range(shape[1])]


def adamw_small(landings, w, m, v):
    names = list(landings)
    n = len(names)
    shapes = [w[name].shape for name in names]

    def body(*refs):
        land, ws, ms, vs = refs[:n], refs[n:2 * n], refs[2 * n:3 * n], refs[3 * n:4 * n]
        outs = [refs[(4 + k) * n:(5 + k) * n] for k in range(4)]
        for k in range(n):
            total = land[k][0]
            for j in range(1, N_DEV):
                total = total + land[k][j]
            for rows, at in _small_views(shapes[k]):
                g = total[rows]
                outs[0][k][at] = g
                outs[1][k][at], outs[2][k][at], outs[3][k][at] = _adamw_update(g, ws[k][at], ms[k][at], vs[k][at])

    blocks = [_full(sh) for sh in shapes]
    outs = pl.pallas_call(
        body, name="adamw_small", grid=(1,),
        in_specs=[_full(landings[name].shape) for name in names] + blocks * 3, out_specs=blocks * 4,
        out_shape=[jax.ShapeDtypeStruct(sh, F32) for sh in shapes] * 4, compiler_params=_params(1),
    )(*[landings[name] for name in names], *[src[name] for src in (w, m, v) for name in names])
    return [dict(zip(names, outs[k * n:(k + 1) * n])) for k in range(4)]


def adamw_replicated(land_vec, land_gate_b, land_loss, names, w, m, v, gate_b):
    n = len(names)

    def body(*refs):
        vec_ref, gb_ref, loss_ref = refs[:3]
        ws, ms, vs = refs[3:3 + n], refs[3 + n:3 + 2 * n], refs[3 + 2 * n:3 + 3 * n]
        gw_ref, gm_ref, gv_ref = refs[3 + 3 * n:6 + 3 * n]
        outs = refs[6 + 3 * n:]
        vec, gb, loss = vec_ref[0], gb_ref[0], loss_ref[0]
        for j in range(1, N_DEV):
            vec, gb, loss = vec + vec_ref[j], gb + gb_ref[j], loss + loss_ref[j]
        for k in range(n):
            g = vec[k:k + 1, :]
            outs[k][...] = g
            outs[n + k][...], outs[2 * n + k][...], outs[3 * n + k][...] = _adamw_update(g, ws[k][...], ms[k][...], vs[k][...])
        outs[4 * n][...] = gb
        outs[4 * n + 1][...], outs[4 * n + 2][...], outs[4 * n + 3][...] = _adamw_update(gb, gw_ref[...], gm_ref[...], gv_ref[...])
        outs[4 * n + 4][...] = loss

    vec_block, gb_block = _full((1, D_MODEL)), _full(gate_b[0].shape)
    outs = pl.pallas_call(
        body, name="adamw_replicated", grid=(1,),
        in_specs=[_full(land_vec.shape), _full(land_gate_b.shape), _full(land_loss.shape)] + [vec_block] * (3 * n) + [gb_block] * 3,
        out_specs=[vec_block] * (4 * n) + [gb_block] * 4 + [_full(land_loss.shape[1:])],
        out_shape=[jax.ShapeDtypeStruct((1, D_MODEL), F32)] * (4 * n) + [jax.ShapeDtypeStruct(gate_b[0].shape, F32)] * 4
        + [jax.ShapeDtypeStruct(land_loss.shape[1:], F32)],
        compiler_params=_params(1),
    )(land_vec, land_gate_b, land_loss, *[src[name] for src in (w, m, v) for name in names], *gate_b)
    results = {name: [outs[k * n + i] for k in range(4)] for i, name in enumerate(names)}
    return results, outs[4 * n:4 * n + 4], outs[4 * n + 4]


SMALL_SHARDED = ("rg_conv_w", "rg_lambda", "sc_conv_w", "odd_norm_pre", "odd_norm_post", "gla_b_gate", "gla_norm_g", "gla_w_gate_lr")
SMALL_ROWS = {"rg_conv_w": (0, 4), "rg_lambda": (4, 2), "sc_conv_w": (6, 3), "odd_norm_pre": (9, 1), "odd_norm_post": (10, 1),
              "gla_b_gate": (11, 2), "gla_norm_g": (13, 1), "gla_w_gate_lr": (16, 32)}


def _pack_small(shards):
    pieces, at = [], 0
    for name in SMALL_SHARDED:
        start, rows = SMALL_ROWS[name]
        if start > at:
            pieces.append(jnp.zeros((start - at, LANES), F32))
        a = shards[name].reshape(rows, -1)
        pieces.append(jnp.pad(a, ((0, 0), (0, LANES - a.shape[1]))))
        at = start + rows
    return jnp.concatenate(pieces, axis=0)


def _unpack_gathered(g):
    def cols(name, width):
        start, rows = SMALL_ROWS[name]
        return jnp.transpose(g[:, start:start + rows, :width], (1, 0, 2)).reshape(rows, N_DEV * width)

    w_lr = cols("gla_w_gate_lr", GLA_KEY // N_DEV).reshape(2, GLA_RANK, GLA_KEY)
    return dict(rg_conv_w=cols("rg_conv_w", LANES), rg_lambda=cols("rg_lambda", LANES), sc_conv_w=cols("sc_conv_w", LANES),
                odd_norm_pre=cols("odd_norm_pre", LANES), odd_norm_post=cols("odd_norm_post", LANES),
                gla_b_gate=cols("gla_b_gate", GLA_KEY // N_DEV), gla_norm_g=cols("gla_norm_g", GLA_DV // N_DEV), gla_w_gate_lr=w_lr)


def _blocks_along_columns(a, rows):
    return jnp.transpose(a.reshape(rows, N_DEV, -1), (1, 0, 2))


def kernel(x, even_norm_pre, even_norm_post, even_w_in, rg_conv_w, rg_conv_b, rg_gate_w, rg_gate_b, rg_lambda, sc_conv_w, even_w_out, odd_norm_pre, odd_norm_post, odd_w_in, gla_w_gate_lr, gla_b_gate, gla_norm_g, odd_w_out, loss_target, m_even_norm_pre, m_even_norm_post, m_even_w_in, m_rg_conv_w, m_rg_conv_b, m_rg_gate_w, m_rg_gate_b, m_rg_lambda, m_sc_conv_w, m_even_w_out, m_odd_norm_pre, m_odd_norm_post, m_odd_w_in, m_gla_w_gate_lr, m_gla_b_gate, m_gla_norm_g, m_odd_w_out, v_even_norm_pre, v_even_norm_post, v_even_w_in, v_rg_conv_w, v_rg_conv_b, v_rg_gate_w, v_rg_gate_b, v_rg_lambda, v_sc_conv_w, v_even_w_out, v_odd_norm_pre, v_odd_norm_post, v_odd_w_in, v_gla_w_gate_lr, v_gla_b_gate, v_gla_norm_g, v_odd_w_out):
    weights = dict(even_norm_pre=even_norm_pre, even_norm_post=even_norm_post, even_w_in=even_w_in, rg_conv_w=rg_conv_w,
                   rg_conv_b=rg_conv_b, rg_gate_w=rg_gate_w, rg_gate_b=rg_gate_b, rg_lambda=rg_lambda, sc_conv_w=sc_conv_w,
                   even_w_out=even_w_out, odd_norm_pre=odd_norm_pre, odd_norm_post=odd_norm_post, odd_w_in=odd_w_in,
                   gla_w_gate_lr=gla_w_gate_lr, gla_b_gate=gla_b_gate, gla_norm_g=gla_norm_g, odd_w_out=odd_w_out)
    m_in = dict(even_norm_pre=m_even_norm_pre, even_norm_post=m_even_norm_post, even_w_in=m_even_w_in, rg_conv_w=m_rg_conv_w,
                rg_conv_b=m_rg_conv_b, rg_gate_w=m_rg_gate_w, rg_gate_b=m_rg_gate_b, rg_lambda=m_rg_lambda, sc_conv_w=m_sc_conv_w,
                even_w_out=m_even_w_out, odd_norm_pre=m_odd_norm_pre, odd_norm_post=m_odd_norm_post, odd_w_in=m_odd_w_in,
                gla_w_gate_lr=m_gla_w_gate_lr, gla_b_gate=m_gla_b_gate, gla_norm_g=m_gla_norm_g, odd_w_out=m_odd_w_out)
    v_in = dict(even_norm_pre=v_even_norm_pre, even_norm_post=v_even_norm_post, even_w_in=v_even_w_in, rg_conv_w=v_rg_conv_w,
                rg_conv_b=v_rg_conv_b, rg_gate_w=v_rg_gate_w, rg_gate_b=v_rg_gate_b, rg_lambda=v_rg_lambda, sc_conv_w=v_sc_conv_w,
                even_w_out=v_even_w_out, odd_norm_pre=v_odd_norm_pre, odd_norm_post=v_odd_norm_post, odd_w_in=v_odd_w_in,
                gla_w_gate_lr=v_gla_w_gate_lr, gla_b_gate=v_gla_b_gate, gla_norm_g=v_gla_norm_g, odd_w_out=v_odd_w_out)
    names = list(weights)
    shapes = {n: weights[n].shape for n in names}
    xs = x[0]
    tgt = loss_target[0]

    proj_e, h_e, w_in_e, small_all = gather_matmul(xs, even_norm_pre, even_w_in[0].astype(BF16),
                                                   _pack_small({n: weights[n][0] for n in SMALL_SHARDED}), 2 * MM_TILE)
    small = _unpack_gathered(small_all)
    gate_w = rg_gate_w[0].reshape(4, RG_HEADS, RG_HEAD_DIM, RG_HEAD_DIM).astype(BF16)
    gate_b = rg_gate_b[0].reshape(4, RG_HEADS, RG_HEAD_DIM)
    conv_b = rg_conv_b
    wg_pad = [jnp.pad(small["gla_w_gate_lr"][d], ((GLA_RANK * d, LANES - GLA_RANK * (d + 1)), (0, 0))).astype(BF16) for d in range(2)]
    bg = [small["gla_b_gate"][d:d + 1] for d in range(2)]
    gnorm = jnp.tile(small["gla_norm_g"], (1, GLA_HEADS))

    half = D_MODEL // 2
    behind_gates = Exchange()
    behind_gates.gather(even_w_out[0].astype(BF16), via_sibling=True)
    behind_gates.gather(odd_w_in[0, :half].astype(BF16), via_sibling=True)
    (ab, hf), (w_out_e, w_in_o_top) = even_gates_fwd(proj_e, small["rg_conv_w"], conv_b, gate_w, gate_b, small["rg_lambda"],
                                                     exchange=behind_gates)
    w_out_e = w_out_e.reshape(2 * D_MODEL, D_MODEL)
    behind_mix_fwd = Exchange()
    behind_mix_fwd.gather(odd_w_in[0, half:].astype(BF16), via_sibling=True)
    behind_mix_fwd.gather(odd_w_out[0].astype(BF16), via_sibling=True)
    (u_e, hb, y_e, x1), (w_in_o_bottom, w_out_o) = even_mix_fwd(ab, hf, proj_e, small["sc_conv_w"], w_out_e, xs, even_norm_post,
                                                                exchange=behind_mix_fwd)
    w_out_o = w_out_o.reshape(D_MODEL, D_MODEL)
    w_in_o = jnp.concatenate([jnp.transpose(part, (1, 0, 2)).reshape(half, ODD_IN) for part in (w_in_o_top, w_in_o_bottom)], axis=0)
    w_in_o = jnp.pad(w_in_o, ((0, 0), (0, ODD_IN_PAD - ODD_IN)))

    proj_o, h_o = rms_matmul(x1, small["odd_norm_pre"], w_in_o, MM_TILE, ODD_IN_PAD, "odd_in")
    o_f, st_f = gla_fwd(proj_o, wg_pad[0], bg[0], False)
    osum, u_o, st_b, y_o, dout, loss_part = gla_fwd(proj_o, wg_pad[1], bg[1], True, o_other=o_f, gnorm=gnorm,
                                                    post=(w_out_o, x1, small["odd_norm_post"], tgt))

    do, dr, dy_o, d_odd_norm_post, d_gnorm = normbwd_matmul_nt(y_o, small["odd_norm_post"], dout, w_out_o, D_MODEL, "odd_out_bwd",
                                                               gla=(proj_o, osum, gnorm))
    d_w_out_o = matmul_tn(u_o, dy_o, D_MODEL, D_MODEL, 4 * MM_TILE, BF16, "odd_w_out_grad")
    dqkv_f, dlr_f, dwg_f, dbg_f = gla_bwd(proj_o, wg_pad[0], bg[0], do, st_f, False)
    dproj_o, dwg_b, dbg_b = gla_bwd(proj_o, wg_pad[1], bg[1], do, st_b, True, first=(dqkv_f, dlr_f, dr))
    dx1, d_odd_norm_pre = matmul_nt_normbwd(dproj_o, w_in_o, x1, small["odd_norm_pre"], dout, MM_TILE, ODD_IN_PAD, "odd_in_bwd")
    d_w_in_o = matmul_tn(h_o, dproj_o, D_MODEL, ODD_IN_PAD // 5, 8 * MM_TILE, BF16, "odd_w_in_grad")

    landed = {}
    behind_out = Exchange()
    behind_out.scatter(d_w_out_o.reshape(N_DEV, D_MODEL // N_DEV, D_MODEL))
    behind_out.scatter(d_odd_norm_pre, columns=True)
    behind_out.scatter(d_odd_norm_post, columns=True)
    behind_out.scatter(_blocks_along_columns(jnp.concatenate([dbg_f, dbg_b], axis=0), 2))
    behind_out.scatter(_blocks_along_columns(d_gnorm, 1))
    behind_out.scatter(_blocks_along_columns(jnp.concatenate([dwg_f[:GLA_RANK], dwg_b[GLA_RANK:2 * GLA_RANK]], axis=0), 2 * GLA_RANK))
    (du_e, dy_e, d_even_norm_post), got = normbwd_matmul_nt(y_e, even_norm_post, dx1, w_out_e, 2 * D_MODEL, "even_out_bwd",
                                                           exchange=behind_out)
    p_w_out_o = got[0]
    for n, part in zip(("odd_norm_pre", "odd_norm_post", "gla_b_gate", "gla_norm_g", "gla_w_gate_lr"), got[1:]):
        landed[n] = part
    d_w_out_e = matmul_tn(u_e, dy_e, D_MODEL, D_MODEL, 4 * MM_TILE, BF16, "even_w_out_grad")
    behind_mix = Exchange()
    behind_mix.scatter(d_w_out_e.reshape(N_DEV, 2 * D_MODEL // N_DEV, D_MODEL))
    (dh, drest, d_sc_w, adj_b), (p_w_out_e,) = even_mix_bwd(du_e, hf, hb, proj_e, small["sc_conv_w"], ab, exchange=behind_mix)
    adj_f = linear_scan(ab, 0, dh.reshape(1, *dh.shape), 0, True, True, "scan_fwd_adjoint")
    behind_gates_bwd = Exchange()
    behind_gates_bwd.scatter(jnp.transpose(d_w_in_o[:, :ODD_IN].reshape(D_MODEL, N_DEV, ODD_SHARD), (1, 0, 2)))
    behind_gates_bwd.scatter(d_sc_w, columns=True)
    (dua, d_gate_w, d_gate_b, d_lam), (p_w_in_o, landed["sc_conv_w"]) = even_gates_bwd(
        proj_e, adj_f, adj_b, hf, hb, dh, small["rg_conv_w"], conv_b, gate_w, gate_b, small["rg_lambda"], exchange=behind_gates_bwd)
    gate_w_rows = 4 * RG_HEADS * RG_HEAD_DIM
    behind_conv = Exchange()
    behind_conv.scatter(d_gate_w.reshape(N_DEV, gate_w_rows // N_DEV, RG_HEAD_DIM))
    behind_conv.scatter(d_lam, columns=True)
    (dxa, d_conv_w, d_conv_b), (p_gate_w, landed["rg_lambda"]) = rg_conv_bwd(dua, proj_e, small["rg_conv_w"], exchange=behind_conv)
    behind_w_grad = Exchange()
    behind_w_grad.gather(sum_parts(p_gate_w, "sum_gate_w"))
    d_w_in_e, (g_gate_w_all,) = matmul_tn(h_e, drest, D_MODEL, D_MODEL, 4 * MM_TILE, BF16, "even_w_in_grad",
                                          exchange=behind_w_grad, b_first=dxa)
    to_sibling = Exchange()
    to_sibling.to_sibling(d_w_in_e)
    to_sibling.scatter(d_conv_w, columns=True)
    from_sibling, landed["rg_conv_w"] = run_exchange(to_sibling, "scatter_to_sibling")
    behind_in_bwd = Exchange()
    behind_in_bwd.among_chips(pair_sum(d_w_in_e, from_sibling))
    (grad_x, d_even_norm_pre), (p_w_in_e,) = matmul_nt_normbwd(
        drest, w_in_e, xs, even_norm_pre, dx1, 2 * MM_TILE, D_MODEL, "even_in_bwd", exchange=behind_in_bwd, first=dxa)
    last = Exchange()
    replicated_vecs = ("even_norm_pre", "even_norm_post", "rg_conv_b")
    last.gather(jnp.concatenate([d_even_norm_pre, d_even_norm_post, d_conv_b], axis=0))
    last.gather(d_gate_b.reshape(4 * RG_HEADS, RG_HEAD_DIM))
    last.gather(loss_part)

    results = {}

    def update(name, parts_, shape2d, exchange=None):
        outs = adamw(parts_, weights[name][0].reshape(shape2d), m_in[name][0].reshape(shape2d), v_in[name][0].reshape(shape2d),
                     "adamw_" + name, exchange=exchange)
        if exchange is not None:
            outs, gathered = outs
        results[name] = [o.reshape(shapes[name]) for o in outs]
        return gathered if exchange is not None else None

    land_vec, land_gate_b, land_loss = update("even_w_in", p_w_in_e, (D_MODEL, EVEN_SHARD), exchange=last)
    update("even_w_out", p_w_out_e, (2 * D_MODEL // N_DEV, D_MODEL))
    update("odd_w_in", p_w_in_o, (D_MODEL, ODD_SHARD))
    update("odd_w_out", p_w_out_o, (D_MODEL // N_DEV, D_MODEL))
    update("rg_gate_w", g_gate_w_all.reshape(1, gate_w_rows, RG_HEAD_DIM), (gate_w_rows, RG_HEAD_DIM))
    small_out = adamw_small({n: landed[n] for n in SMALL_SHARDED}, weights, m_in, v_in)
    for n in SMALL_SHARDED:
        results[n] = [o[n] for o in small_out]
    gate_b_shape = (4 * RG_HEADS, RG_HEAD_DIM)
    rep_out, gate_b_out, loss_all = adamw_replicated(land_vec, land_gate_b, land_loss, replicated_vecs, weights, m_in, v_in,
                                                     [src["rg_gate_b"].reshape(gate_b_shape) for src in (weights, m_in, v_in)])
    results.update(rep_out)
    results["rg_gate_b"] = [o.reshape(shapes["rg_gate_b"]) for o in gate_b_out]

    return (loss_all[0, 0], grad_x.reshape(x.shape), *[results[n][0] for n in names], *[results[n][1] for n in names],
            *[results[n][2] for n in names], *[results[n][3] for n in names])
```

```python
import functools

import jax
import jax.numpy as jnp
from jax import lax
from jax.experimental import pallas as pl
from jax.experimental.pallas import tpu as pltpu

F32 = jnp.float32
BF16 = jnp.bfloat16

N_DEV = 8
D_MODEL = 1024
NORM_EPS = 1e-6
RG_HEADS = 8
RG_HEAD_DIM = 128
RG_C = 8.0
GLA_HEADS = 4
GLA_DK = 128
GLA_DV = 256
GLA_KEY = 512
GLA_RANK = 16
GLA_NORMALIZER = 16.0
GLA_CHUNK = 64
EVEN_IN = 6144
ODD_IN = 3104
ODD_IN_PAD = 3200
ODD_SHARD = ODD_IN // N_DEV
EVEN_SHARD = EVEN_IN // N_DEV
ADAM_LR = 0.001
ADAM_B1 = 0.9
ADAM_B2 = 0.999
ADAM_EPS = 1e-08
ADAM_WD = 0.01
ADAM_STEP = 10

SMALLEST_NORMAL = 1.1754944e-38
SUBLANES = 8
LANES = 128
VMEM_LIMIT_BYTES = 48 * 2 ** 20
ROW_TILE = 256
GLA_TILE = 256
MM_TILE = 512
ADAMW_BLOCK_BYTES = 2 ** 20
PACK_ROWS = 48
MESH_ID = pl.DeviceIdType.MESH


def _params(n_grid):
    return pltpu.CompilerParams(dimension_semantics=("arbitrary",) * n_grid, vmem_limit_bytes=VMEM_LIMIT_BYTES)


def _bdot(a, b):
    return jnp.dot(a.astype(BF16), b.astype(BF16), preferred_element_type=F32)


def _bdot_nt(a, b):
    return lax.dot_general(a.astype(BF16), b.astype(BF16), (((1,), (1,)), ((), ())), preferred_element_type=F32)


def _bdot_tn(a, b):
    return lax.dot_general(a.astype(BF16), b.astype(BF16), (((0,), (0,)), ((), ())), preferred_element_type=F32)


def _rstd(x):
    return lax.rsqrt(jnp.mean(x * x, axis=-1, keepdims=True) + NORM_EPS)


def _rms(x, g):
    return x * _rstd(x) * g


def _rms_bwd(x, g, dy):
    xh = x * _rstd(x)
    dyg = dy * g
    dx = _rstd(x) * (dyg - xh * jnp.mean(dyg * xh, axis=-1, keepdims=True))
    return dx, jnp.sum(dy * xh, axis=0, keepdims=True)


def _sigmoid(z):
    return 0.5 * jnp.tanh(0.5 * z) + 0.5


def _silu_and_grad(z):
    s = _sigmoid(z)
    return z * s, s * (1.0 + z * (1.0 - s))


def _softplus(z):
    return jnp.maximum(z, 0.0) + jnp.log(1.0 + jnp.exp(-jnp.abs(z)))


def _shift_rows(cur, before, after, d):
    ts = cur.shape[0]
    row = lax.broadcasted_iota(jnp.int32, (SUBLANES, cur.shape[1]), 0)
    out = pltpu.roll(cur, (-d) % ts, 0)
    if d < 0:
        edge = jnp.where(row < -d, pltpu.roll(before, (-d) % SUBLANES, 0), out[:SUBLANES])
        return jnp.concatenate([edge, out[SUBLANES:]], axis=0)
    edge = jnp.where(row >= SUBLANES - d, pltpu.roll(after, (-d) % SUBLANES, 0), out[ts - SUBLANES:])
    return jnp.concatenate([out[:ts - SUBLANES], edge], axis=0)


def _halo_specs(ts, s, width, col, tile=lambda i: i):
    per = ts // SUBLANES
    last = s // SUBLANES - 1
    return [
        pl.BlockSpec((ts, width), lambda i: (tile(i), col)),
        pl.BlockSpec((SUBLANES, width), lambda i: (jnp.maximum(tile(i) * per - 1, 0), col)),
        pl.BlockSpec((SUBLANES, width), lambda i: (jnp.minimum((tile(i) + 1) * per, last), col)),
    ]


def _halo_load(cur_ref, before_ref, after_ref, n_tiles, tile=lambda i: i):
    i = tile(pl.program_id(0))
    before = jnp.where(i > 0, before_ref[...], 0.0)
    after = jnp.where(i < n_tiles - 1, after_ref[...], 0.0)
    return cur_ref[...], before, after


def _full(shape):
    return pl.BlockSpec(shape, lambda *_: (0,) * len(shape))


def _peer(x, y, c, mask):
    px, py, pc = x ^ (mask >> 2), y ^ ((mask >> 1) & 1), c ^ (mask & 1)
    return (px, py, pc), 4 * px + 2 * py + pc


class Exchange:
    SIBLING = 1
    OTHER_CHIPS = (2, 4, 6)

    def __init__(self):
        self.args, self.out_shape, self._kinds = [], [], []

    def gather(self, block, columns=False, via_sibling=False):
        shape = (block.shape[0], N_DEV * block.shape[1]) if columns else (N_DEV,) + block.shape
        return self._add(block, shape, ("gather", columns, via_sibling))

    def scatter(self, stack, columns=False):
        shape = (N_DEV, stack.shape[0], stack.shape[1] // N_DEV) if columns else stack.shape
        return self._add(stack, shape, ("scatter", columns, False))

    def _add(self, arg, shape, kind):
        self.args.append(arg)
        self.out_shape.append(jax.ShapeDtypeStruct(shape, arg.dtype))
        self._kinds.append(kind)
        return len(self.args) - 1

    def semaphores(self):
        n = len(self.args)
        return [pltpu.SemaphoreType.DMA((n, N_DEV - 1)), pltpu.SemaphoreType.DMA((n, N_DEV - 1)), pltpu.SemaphoreType.DMA((n,))]

    def among_chips(self, stack):
        return self._add(stack, stack.shape, ("among_chips", False, False))

    def _copies(self, position, in_refs, out_refs):
        x, y, c, me = position
        for arr, ((kind, columns, via_sibling), src, out) in enumerate(zip(self._kinds, in_refs, out_refs)):
            for mask in range(N_DEV):
                _, peer_id = _peer(x, y, c, mask)
                relayed = via_sibling and mask not in (0, self.SIBLING) + self.OTHER_CHIPS
                if kind == "among_chips":
                    if mask in (0,) + self.OTHER_CHIPS:
                        yield arr, mask, src.at[peer_id // 2], out.at[me // 2], out.at[peer_id // 2], False, mask
                elif kind == "gather":
                    if columns:
                        width = src.shape[-1]
                        yield (arr, mask, src, out.at[:, pl.ds(pl.multiple_of(me * width, LANES), width)],
                               out.at[:, pl.ds(pl.multiple_of(peer_id * width, LANES), width)], relayed, mask)
                    else:
                        yield arr, mask, src, out.at[me], out.at[peer_id], relayed, mask
                else:
                    if columns:
                        width = src.shape[-1] // N_DEV
                        block = src.at[:, pl.ds(pl.multiple_of(peer_id * width, LANES), width)]
                    else:
                        block = src.at[peer_id]
                    yield arr, mask, block, out.at[me], out.at[peer_id], False, mask

    def _remote(self, position, sems, arr, slot, to_mask, src, dst):
        x, y, c, _ = position
        return pltpu.make_async_remote_copy(src_ref=src, dst_ref=dst, send_sem=sems[0].at[arr, slot - 1], recv_sem=sems[1].at[arr, slot - 1],
                                            device_id=_peer(x, y, c, to_mask)[0], device_id_type=MESH_ID)

    def start(self, position, in_refs, out_refs, sems):
        for arr, slot, src, dst, _, relayed, to_mask in self._copies(position, in_refs, out_refs):
            if slot == 0:
                pltpu.make_async_copy(src, dst, sems[2].at[arr]).start()
            elif not relayed:
                self._remote(position, sems, arr, slot, to_mask, src, dst).start()

    def wait(self, position, in_refs, out_refs, sems):
        copies = list(self._copies(position, in_refs, out_refs))
        landings = {(arr, slot): landing for arr, slot, _, _, landing, _, _ in copies}
        passed_on = set()
        for arr, mask, src, _, landing, relayed, _ in copies:
            if relayed:
                held = landings[arr, mask ^ self.SIBLING]
                self._remote(position, sems, arr, mask ^ self.SIBLING, mask ^ self.SIBLING, src, held).wait_recv()
                self._remote(position, sems, arr, mask, self.SIBLING, held, held).start()
                passed_on.add((arr, mask ^ self.SIBLING))
        for arr, slot, src, dst, landing, relayed, to_mask in copies:
            if slot == 0:
                pltpu.make_async_copy(src, dst, sems[2].at[arr]).wait()
                continue
            if (arr, slot) not in passed_on:
                self._remote(position, sems, arr, slot, to_mask, src, landing).wait_recv()
            if relayed:
                held = landings[arr, slot ^ self.SIBLING]
                self._remote(position, sems, arr, slot, self.SIBLING, held, held).wait_send()
            else:
                self._remote(position, sems, arr, slot, to_mask, src, dst).wait_send()


def _call(body, *, name, grid, in_specs, out_specs, out_shape, args, scratch_shapes=(), exchange=None):
    single = not isinstance(out_shape, (list, tuple))
    if single:
        out_specs, out_shape = [out_specs], [out_shape]
    params = _params(len(grid))
    if exchange is None:
        outs = pl.pallas_call(body, name=name, grid=grid, in_specs=in_specs, out_specs=out_specs, out_shape=out_shape,
                              scratch_shapes=list(scratch_shapes), compiler_params=params)(*args)
        return outs[0] if single else outs
    counts = (len(args), len(exchange.args), len(out_shape), len(exchange.out_shape), len(scratch_shapes), 3)

    def wrapped(*refs):
        groups, at = [], 0
        for n in counts:
            groups.append(refs[at:at + n])
            at += n
        main_in, ex_in, main_out, ex_out, main_scratch, sems = groups
        x, y, c = lax.axis_index("x"), lax.axis_index("y"), lax.axis_index("c")
        position = (x, y, c, 4 * x + 2 * y + c)
        ids = [pl.program_id(a) for a in range(len(grid))]
        first = functools.reduce(jnp.logical_and, [i == 0 for i in ids])
        last = functools.reduce(jnp.logical_and, [i == g - 1 for i, g in zip(ids, grid)])

        @pl.when(first)
        def _():
            exchange.start(position, ex_in, ex_out, sems)

        body(*main_in, *main_out, *main_scratch)

        @pl.when(last)
        def _():
            exchange.wait(position, ex_in, ex_out, sems)

    hbm = pl.BlockSpec(memory_space=pl.ANY)
    outs = pl.pallas_call(
        wrapped, name=name, grid=grid, in_specs=list(in_specs) + [hbm] * counts[1], out_specs=list(out_specs) + [hbm] * counts[3],
        out_shape=list(out_shape) + exchange.out_shape, scratch_shapes=list(scratch_shapes) + exchange.semaphores(),
        compiler_params=params)(*args, *exchange.args)
    main = outs[:counts[2]]
    return (main[0] if single else main), outs[counts[2]:]


def gather_matmul(x, g, w_block, small_block, tm):
    s, d = x.shape
    width = w_block.shape[1]
    tm = min(tm, s)
    n_i = s // tm
    sibling = Exchange.SIBLING
    y_nbr, x_nbr, diagonal = Exchange.OTHER_CHIPS

    def links(core):
        return (y_nbr, x_nbr) if core == 1 else (x_nbr, y_nbr)

    def block_order(core):
        first, second = links(core)
        return [0, sibling, first, second | sibling, second, first | sibling, diagonal, diagonal | sibling]

    def body(order_ref, x_ref, g_ref, wb_ref, sb_ref, proj_ref, h_ref, w_ref, small_ref, h_all, w_buf, send, recv, local, load_sem):
        j, i = pl.program_id(0), pl.program_id(1)
        xx, yy, cc = lax.axis_index("x"), lax.axis_index("y"), lax.axis_index("c")
        me = 4 * xx + 2 * yy + cc

        def block_of(dev):
            return w_ref.at[:, pl.ds(pl.multiple_of(dev * width, LANES), width)]

        def half_of(dev, part):
            return w_ref.at[pl.ds(part * (d // 2), d // 2), pl.ds(pl.multiple_of(dev * width, LANES), width)]

        def remote(arr, slot, to_mask, src, dst):
            return pltpu.make_async_remote_copy(src_ref=src, dst_ref=dst, send_sem=send.at[arr, slot - 1], recv_sem=recv.at[arr, slot - 1],
                                                device_id=_peer(xx, yy, cc, to_mask)[0], device_id_type=MESH_ID)

        def mine_to(mask):
            return remote(0, mask, mask, wb_ref, block_of(me))

        def arrival(mask):
            return remote(0, mask, mask, wb_ref, block_of(me ^ mask))

        def to_sibling(mask):
            return remote(0, mask | sibling, sibling, block_of(me ^ mask), block_of(me ^ mask))

        def relay(of):
            along_x = of == y_nbr
            part = 0 if along_x else 1
            return remote(0 if along_x else 2, diagonal, x_nbr if along_x else y_nbr, half_of(me ^ of, part), half_of(me ^ of, part))

        def diagonal_half(part):
            return remote(0 if part == 0 else 2, diagonal, x_nbr if part == 0 else y_nbr, wb_ref.at[pl.ds(0, d // 2), :],
                          half_of(me ^ diagonal, part))

        @pl.when((j == 0) & (i == 0))
        def _():
            pltpu.make_async_copy(wb_ref, block_of(me), local.at[0]).start()
            pltpu.make_async_copy(sb_ref, small_ref.at[me], local.at[1]).start()
            mine_to(sibling).start()
            for mask in range(1, N_DEV):
                remote(1, mask, mask, sb_ref, small_ref.at[me]).start()

        def load(step):
            return pltpu.make_async_copy(w_ref.at[:, pl.ds(pl.multiple_of(order_ref[step] * width, LANES), width)],
                                         w_buf.at[step % 2], load_sem.at[step % 2])

        for core in range(2):
            first, second = links(core)
            for step in range(N_DEV):
                at_step = (j == 0) & (i == 0) if step == 0 else (j == step - 1) & (i == n_i - 1)

                @pl.when(at_step & (cc == core))
                def _(step=step, first=first, second=second):
                    if step == 0:
                        mine_to(first).start()
                        pltpu.make_async_copy(wb_ref, block_of(me), local.at[0]).wait()
                    elif step == 1:
                        arrival(sibling).wait_recv()
                    elif step == 2:
                        arrival(first).wait_recv()
                        to_sibling(first).start()
                        mine_to(first).wait_send()
                        mine_to(second).start()
                        relay(first).start()
                    elif step == 3:
                        arrival(second | sibling).wait_recv()
                    elif step == 4:
                        arrival(second).wait_recv()
                        to_sibling(second).start()
                        relay(second).start()
                    elif step == 5:
                        arrival(first | sibling).wait_recv()
                    elif step == 6:
                        diagonal_half(0).wait_recv()
                        diagonal_half(1).wait_recv()
                        to_sibling(diagonal).start()
                    else:
                        arrival(diagonal | sibling).wait_recv()

        @pl.when((j == 0) & (i == 0))
        def _():
            load(0).start()

        @pl.when(i == 0)
        def _():
            load(j).wait()

        @pl.when((i == n_i - 1) & (j < N_DEV - 1))
        def _():
            load(j + 1).start()

        rows = pl.ds(pl.multiple_of(i * tm, tm), tm)

        @pl.when(j == 0)
        def _():
            h = _rms(x_ref[...], g_ref[...]).astype(BF16)
            h_all[rows, :] = h
            h_ref[...] = h

        proj_ref[...] = jnp.dot(h_all[rows, :], w_buf[j % 2], preferred_element_type=F32)

        @pl.when((j == N_DEV - 1) & (i == n_i - 1))
        def _():
            pltpu.make_async_copy(sb_ref, small_ref.at[me], local.at[1]).wait()
            for mask in range(1, N_DEV):
                remote(1, mask, mask, sb_ref, small_ref.at[me ^ mask]).wait_recv()
                remote(1, mask, mask, sb_ref, small_ref.at[me]).wait_send()
            mine_to(sibling).wait_send()
            for core in range(2):
                @pl.when(cc == core)
                def _(core=core):
                    mine_to(links(core)[1]).wait_send()
            for mask in Exchange.OTHER_CHIPS:
                to_sibling(mask).wait_send()
            relay(y_nbr).wait_send()
            relay(x_nbr).wait_send()

    def first_pass_row(j, i, order):
        return jnp.where(j == 0, i, n_i - 1), 0

    hbm = pl.BlockSpec(memory_space=pl.ANY)
    core = lax.axis_index("c")
    me = 4 * lax.axis_index("x") + 2 * lax.axis_index("y") + core
    order = (me ^ jnp.where(core == 1, jnp.array(block_order(1)), jnp.array(block_order(0)))).astype(jnp.int32)
    grid_spec = pltpu.PrefetchScalarGridSpec(
        num_scalar_prefetch=1, grid=(N_DEV, n_i),
        in_specs=[pl.BlockSpec((tm, d), first_pass_row), pl.BlockSpec((1, d), lambda j, i, order: (0, 0)), hbm, hbm],
        out_specs=[pl.BlockSpec((tm, width), lambda j, i, order: (i, order[j])), pl.BlockSpec((tm, d), first_pass_row), hbm, hbm],
        scratch_shapes=[pltpu.VMEM((s, d), BF16), pltpu.VMEM((2, d, width), BF16), pltpu.SemaphoreType.DMA((3, N_DEV - 1)),
                        pltpu.SemaphoreType.DMA((3, N_DEV - 1)), pltpu.SemaphoreType.DMA((2,)), pltpu.SemaphoreType.DMA((2,))])
    return pl.pallas_call(
        body, name="even_in", grid_spec=grid_spec,
        out_shape=[jax.ShapeDtypeStruct((s, N_DEV * width), F32), jax.ShapeDtypeStruct((s, d), BF16),
                   jax.ShapeDtypeStruct((d, N_DEV * width), w_block.dtype), jax.ShapeDtypeStruct((N_DEV,) + small_block.shape, small_block.dtype)],
        compiler_params=_params(2),
    )(order, x, g, w_block, small_block)


def rms_matmul(x, g, w, tm, tn, name, exchange=None):
    s, d = x.shape
    n = w.shape[1]
    tm = min(tm, s)

    def body(x_ref, g_ref, w_ref, o_ref, h_ref):
        @pl.when(pl.program_id(1) == 0)
        def _():
            h_ref[...] = _rms(x_ref[...], g_ref[...]).astype(BF16)

        o_ref[...] = jnp.dot(h_ref[...], w_ref[...], preferred_element_type=F32)

    return _call(
        body, name=name, grid=(s // tm, n // tn),
        in_specs=[pl.BlockSpec((tm, d), lambda i, j: (i, 0)), _full((1, d)), pl.BlockSpec((d, tn), lambda i, j: (0, j))],
        out_specs=[pl.BlockSpec((tm, tn), lambda i, j: (i, j)), pl.BlockSpec((tm, d), lambda i, j: (i, 0))],
        out_shape=[jax.ShapeDtypeStruct((s, n), F32), jax.ShapeDtypeStruct((s, d), BF16)],
        args=[x, g, w], exchange=exchange)


def _gla_out_bwd(du, r, osum, gn, do_ref, dr_ref, dgn_ref):
    silu_r, dsilu_r = _silu_and_grad(r)
    for head in range(GLA_HEADS):
        vl = slice(head * GLA_DV, (head + 1) * GLA_DV)
        o_h, g_h, du_h = osum[:, vl], gn[:, vl], du[:, vl]
        dr_ref[:, vl] = (du_h * _rms(o_h, g_h) * dsilu_r[:, vl]).astype(BF16)
        do_h, dg_h = _rms_bwd(o_h, g_h, du_h * silu_r[:, vl])
        do_ref[:, vl] = do_h
        dgn_ref[...] += dg_h


def normbwd_matmul_nt(y, g, dout, w, tn, name, exchange=None, gla=None):
    s, d = y.shape
    n = w.shape[0]
    tm = min(MM_TILE, s)

    def body(*refs):
        if gla is None:
            y_ref, g_ref, dout_ref, w_ref, du_ref, dy_ref, dg_ref = refs
        else:
            y_ref, g_ref, dout_ref, w_ref, r_ref, o_ref, gn_ref, do_ref, dr_ref, dy_ref, dg_ref, dgn_ref = refs
        i, j = pl.program_id(0), pl.program_id(1)

        @pl.when(j == 0)
        def _():
            dy, dg = _rms_bwd(y_ref[...], g_ref[...], dout_ref[...])
            dy_ref[...] = dy.astype(BF16)

            @pl.when(i == 0)
            def _():
                dg_ref[...] = jnp.zeros_like(dg_ref)
                if gla is not None:
                    dgn_ref[...] = jnp.zeros_like(dgn_ref)

            dg_ref[...] += dg

        du = lax.dot_general(dy_ref[...], w_ref[...], (((1,), (1,)), ((), ())), preferred_element_type=F32)
        if gla is None:
            du_ref[...] = du
        else:
            _gla_out_bwd(du, r_ref[...], o_ref[...], gn_ref[...], do_ref, dr_ref, dgn_ref)

    row = pl.BlockSpec((tm, d), lambda i, j: (i, 0))
    in_specs = [row, _full((1, d)), row, pl.BlockSpec((tn, d), lambda i, j: (j, 0))]
    args = [y, g, dout, w]
    tail_specs = [row, _full((1, d))]
    tail_shapes = [jax.ShapeDtypeStruct((s, d), BF16), jax.ShapeDtypeStruct((1, d), F32)]
    if gla is None:
        out_specs = [pl.BlockSpec((tm, tn), lambda i, j: (i, j))] + tail_specs
        out_shape = [jax.ShapeDtypeStruct((s, n), F32)] + tail_shapes
    else:
        proj, osum, gnorm = gla
        assert n == tn == D_MODEL
        in_specs += [pl.BlockSpec((tm, D_MODEL), lambda i, j: (i, 2)), row, _full(gnorm.shape)]
        args += [proj, osum, gnorm]
        out_specs = [row, row] + tail_specs + [_full((1, GLA_DV))]
        out_shape = [jax.ShapeDtypeStruct((s, D_MODEL), F32), jax.ShapeDtypeStruct((s, D_MODEL), BF16)] + tail_shapes + [
            jax.ShapeDtypeStruct((1, GLA_DV), F32)]
    return _call(body, name=name, grid=(s // tm, n // tn), in_specs=in_specs, out_specs=out_specs, out_shape=out_shape,
                 args=args, exchange=exchange)


def matmul_tn(a, b, tm, tn, ts, out_dtype, name, exchange=None, b_first=None):
    s, m = a.shape
    n = b.shape[1] + (0 if b_first is None else tn)
    ts = min(ts, s)
    n_k = s // ts
    dims = (((0,), (0,)), ((), ()))

    def body(*refs):
        if b_first is None:
            a_ref, b_ref, o_ref, acc = refs
        else:
            a_ref, first_ref, b_ref, o_ref, acc = refs
        j, k = pl.program_id(1), pl.program_id(2)

        @pl.when(k == 0)
        def _():
            acc[...] = jnp.zeros_like(acc)

        if b_first is None:
            acc[...] += lax.dot_general(a_ref[...], b_ref[...], dims, preferred_element_type=F32)
        else:
            @pl.when(j == 0)
            def _():
                acc[...] += lax.dot_general(a_ref[...], first_ref[...], dims, preferred_element_type=F32)

            @pl.when(j > 0)
            def _():
                acc[...] += lax.dot_general(a_ref[...], b_ref[...], dims, preferred_element_type=F32)

        @pl.when(k == n_k - 1)
        def _():
            o_ref[...] = acc[...].astype(out_dtype)

    if b_first is None:
        b_specs, b_args = [pl.BlockSpec((ts, tn), lambda i, j, k: (k, j))], [b]
    else:
        b_specs = [pl.BlockSpec((ts, tn), lambda i, j, k: (k, 0)), pl.BlockSpec((ts, tn), lambda i, j, k: (k, jnp.maximum(j - 1, 0)))]
        b_args = [b_first, b]
    return _call(
        body, name=name, grid=(m // tm, n // tn, n_k),
        in_specs=[pl.BlockSpec((ts, tm), lambda i, j, k: (k, i))] + b_specs,
        out_specs=pl.BlockSpec((tm, tn), lambda i, j, k: (i, j)),
        out_shape=jax.ShapeDtypeStruct((m, n), out_dtype),
        scratch_shapes=[pltpu.VMEM((tm, tn), F32)], args=[a] + b_args, exchange=exchange)


def matmul_nt_normbwd(dproj, w, x, g, dres, tm, tk, name, exchange=None, first=None):
    s, kt = dproj.shape
    kt += 0 if first is None else tk
    d = w.shape[0]
    tm = min(tm, s)
    n_k = kt // tk
    dims = (((1,), (1,)), ((), ()))

    def body(*refs):
        if first is None:
            a_ref, w_ref, x_ref, g_ref, r_ref, dx_ref, dg_ref, acc = refs
        else:
            first_ref, a_ref, w_ref, x_ref, g_ref, r_ref, dx_ref, dg_ref, acc = refs
        i, k = pl.program_id(0), pl.program_id(1)

        @pl.when(k == 0)
        def _():
            acc[...] = jnp.zeros_like(acc)

        if first is None:
            acc[...] += lax.dot_general(a_ref[...], w_ref[...], dims, preferred_element_type=F32)
        else:
            @pl.when(k == 0)
            def _():
                acc[...] += lax.dot_general(first_ref[...], w_ref[...], dims, preferred_element_type=F32)

            @pl.when(k > 0)
            def _():
                acc[...] += lax.dot_general(a_ref[...], w_ref[...], dims, preferred_element_type=F32)

        @pl.when(k == n_k - 1)
        def _():
            dx, dg = _rms_bwd(x_ref[...], g_ref[...], acc[...])
            dx_ref[...] = r_ref[...] + dx

            @pl.when(i == 0)
            def _():
                dg_ref[...] = jnp.zeros_like(dg_ref)

            dg_ref[...] += dg

    row = pl.BlockSpec((tm, d), lambda i, k: (i, 0))
    if first is None:
        a_specs, a_args = [pl.BlockSpec((tm, tk), lambda i, k: (i, k))], [dproj]
    else:
        a_specs = [pl.BlockSpec((tm, tk), lambda i, k: (i, 0)), pl.BlockSpec((tm, tk), lambda i, k: (i, jnp.maximum(k - 1, 0)))]
        a_args = [first, dproj]
    return _call(
        body, name=name, grid=(s // tm, n_k),
        in_specs=a_specs + [pl.BlockSpec((d, tk), lambda i, k: (0, k)), row, _full((1, d)), row],
        out_specs=[row, _full((1, d))],
        out_shape=[jax.ShapeDtypeStruct((s, d), F32), jax.ShapeDtypeStruct((1, d), F32)],
        scratch_shapes=[pltpu.VMEM((tm, d), F32)], args=a_args + [w, x, g, dres], exchange=exchange)


def _rg_conv(xa, before, after, cw, cb):
    return (cw[0:1, :] * _shift_rows(xa, before, after, -2) + cw[1:2, :] * _shift_rows(xa, before, after, -1)
            + cw[2:3, :] * xa + cw[3:4, :] * _shift_rows(xa, before, after, 1) + cb)


def _rg_gates(ua_h, gw_ref, gb_ref, c_h, direction, head):
    r = _sigmoid(_bdot(ua_h, gw_ref[2 * direction, head]) + gb_ref[2 * direction, head:head + 1, :])
    i = _sigmoid(_bdot(ua_h, gw_ref[2 * direction + 1, head]) + gb_ref[2 * direction + 1, head:head + 1, :])
    log_a = -c_h * r
    a = jnp.exp(log_a)
    beta_sq = -jnp.tanh(log_a) * (1.0 + a * a)
    inv_beta = lax.rsqrt(jnp.maximum(beta_sq, SMALLEST_NORMAL))
    return r, i, a, beta_sq * inv_beta, inv_beta


def even_gates_fwd(proj, conv_w, conv_b, gate_w, gate_b, lam, exchange=None):
    s = proj.shape[0]
    ts = min(2 * ROW_TILE, s)
    n_tiles = s // ts

    def body(xa_ref, xb_ref, xn_ref, cw_ref, cb_ref, gw_ref, gb_ref, lam_ref, o_ref, hf_ref, carry):
        @pl.when(pl.program_id(0) == 0)
        def _():
            carry[...] = jnp.zeros_like(carry)

        xa, before, after = _halo_load(xa_ref, xb_ref, xn_ref, n_tiles)
        ua = _rg_conv(xa, before, after, cw_ref[...], cb_ref[...])
        c = RG_C * _softplus(-lam_ref[...])
        ua_bf16 = ua.astype(BF16)
        for direction in range(2):
            for head in range(RG_HEADS):
                lanes = slice(head * RG_HEAD_DIM, (head + 1) * RG_HEAD_DIM)
                ua_h = ua[:, lanes]
                _, i, a, beta, _ = _rg_gates(ua_bf16[:, lanes], gw_ref, gb_ref, c[direction:direction + 1, lanes], direction, head)
                o_ref[2 * direction, :, lanes] = a
                o_ref[2 * direction + 1, :, lanes] = beta * (i * ua_h)
        _scan_tile(o_ref.at[0], o_ref.at[1], hf_ref, carry, False, False)

    return _call(
        body, name="even_gates_fwd", grid=(n_tiles,),
        in_specs=_halo_specs(ts, s, D_MODEL, 0) + [_full(conv_w.shape), _full(conv_b.shape), _full(gate_w.shape),
                                                   _full(gate_b.shape), _full(lam.shape)],
        out_specs=[pl.BlockSpec((4, ts, D_MODEL), lambda i: (0, i, 0)), pl.BlockSpec((ts, D_MODEL), lambda i: (i, 0))],
        out_shape=[jax.ShapeDtypeStruct((4, s, D_MODEL), F32), jax.ShapeDtypeStruct((s, D_MODEL), F32)],
        scratch_shapes=[pltpu.VMEM((SUBLANES, D_MODEL), F32)],
        args=[proj, proj, proj, conv_w, conv_b, gate_w, gate_b, lam], exchange=exchange)


def _scan_tile(a_ref, b_ref, h_ref, carry, reverse, b_times_a):
    ts, c = h_ref.shape
    n_blocks = ts // SUBLANES
    row = lax.broadcasted_iota(jnp.int32, (SUBLANES, c), 0)

    def block(j, h_in):
        r0 = pl.multiple_of((n_blocks - 1 - j if reverse else j) * SUBLANES, SUBLANES)
        a = a_ref[pl.ds(r0, SUBLANES), :]
        b = b_ref[pl.ds(r0, SUBLANES), :]
        if b_times_a:
            b = a * b
        for step in (1, 2, 4):
            shift = SUBLANES - step if reverse else step
            valid = row < SUBLANES - step if reverse else row >= step
            b = jnp.where(valid, a * pltpu.roll(b, shift, 0) + b, b)
            a = jnp.where(valid, a * pltpu.roll(a, shift, 0), a)
        h = a * h_in + b
        h_ref[pl.ds(r0, SUBLANES), :] = h
        return h[0:1, :] if reverse else h[SUBLANES - 1:SUBLANES, :]

    carry[0:1, :] = lax.fori_loop(0, n_blocks, block, carry[0:1, :])


def linear_scan(a_arr, a_idx, b_arr, b_idx, reverse, b_times_a, name, exchange=None):
    _, s, c = a_arr.shape
    ts = min(MM_TILE, s)
    n_tiles = s // ts

    def tile_of(i):
        return n_tiles - 1 - i if reverse else i

    def body(a_ref, b_ref, h_ref, carry):
        @pl.when(pl.program_id(0) == 0)
        def _():
            carry[...] = jnp.zeros_like(carry)

        _scan_tile(a_ref, b_ref, h_ref, carry, reverse, b_times_a)

    return _call(
        body, name=name, grid=(n_tiles,),
        in_specs=[pl.BlockSpec((None, ts, c), lambda i: (a_idx, tile_of(i), 0)),
                  pl.BlockSpec((None, ts, c), lambda i: (b_idx, tile_of(i), 0))],
        out_specs=pl.BlockSpec((ts, c), lambda i: (tile_of(i), 0)),
        out_shape=jax.ShapeDtypeStruct((s, c), F32),
        scratch_shapes=[pltpu.VMEM((SUBLANES, c), F32)], args=[a_arr, b_arr], exchange=exchange)


def _sc_conv(p, before, after, w):
    return w[0:1, :] * _shift_rows(p, before, after, -1) + w[1:2, :] * p + w[2:3, :] * _shift_rows(p, before, after, 1)


def even_mix_fwd(ab, hf, proj, sc_w, w_out, xres, g_post, exchange=None):
    s = proj.shape[0]
    ts = min(ROW_TILE, s)
    n_tiles = s // ts

    def tile(i):
        return n_tiles - 1 - i

    row = pl.BlockSpec((ts, D_MODEL), lambda i: (tile(i), 0))

    def col(c):
        return pl.BlockSpec((ts, D_MODEL), lambda i: (tile(i), c))

    def body(a_ref, b_ref, hf_ref, za_ref, xb_ref, xbb_ref, xbn_ref, gb_ref, gc_ref, gcb_ref, gcn_ref, zb_ref, w_ref,
             wo_ref, x_ref, g_ref, u_ref, hb_ref, y_ref, out_ref, carry):
        @pl.when(pl.program_id(0) == 0)
        def _():
            carry[...] = jnp.zeros_like(carry)

        _scan_tile(a_ref, b_ref, hb_ref, carry, True, False)
        xb, xb_before, xb_after = _halo_load(xb_ref, xbb_ref, xbn_ref, n_tiles, tile)
        gc, gc_before, gc_after = _halo_load(gc_ref, gcb_ref, gcn_ref, n_tiles, tile)
        silu_za, _ = _silu_and_grad(za_ref[...])
        silu_zb, _ = _silu_and_grad(zb_ref[...])
        u_ref[:, :D_MODEL] = ((hf_ref[...] + hb_ref[...]) * silu_za).astype(BF16)
        cv = _sc_conv(gc * xb, gc_before * xb_before, gc_after * xb_after, w_ref[...])
        u_ref[:, D_MODEL:] = (gb_ref[...] * cv * silu_zb).astype(BF16)
        y = jnp.dot(u_ref[...], wo_ref[...], preferred_element_type=F32)
        y_ref[...] = y
        out_ref[...] = x_ref[...] + _rms(y, g_ref[...])

    return _call(
        body, name="even_mix_fwd", grid=(n_tiles,),
        in_specs=[pl.BlockSpec((None, ts, D_MODEL), lambda i: (2, tile(i), 0)), pl.BlockSpec((None, ts, D_MODEL), lambda i: (3, tile(i), 0)),
                  row, col(1)] + _halo_specs(ts, s, D_MODEL, 2, tile) + [col(3)] + _halo_specs(ts, s, D_MODEL, 4, tile)
        + [col(5), _full(sc_w.shape), _full(w_out.shape), row, _full(g_post.shape)],
        out_specs=[pl.BlockSpec((ts, 2 * D_MODEL), lambda i: (tile(i), 0)), row, row, row],
        out_shape=[jax.ShapeDtypeStruct((s, 2 * D_MODEL), BF16)] + [jax.ShapeDtypeStruct((s, D_MODEL), F32)] * 3,
        scratch_shapes=[pltpu.VMEM((SUBLANES, D_MODEL), F32)],
        args=[ab, ab, hf, proj, proj, proj, proj, proj, proj, proj, proj, proj, sc_w, w_out, xres, g_post], exchange=exchange)


def even_mix_bwd(du, hf, hb, proj, sc_w, ab, exchange=None):
    s = proj.shape[0]
    ts = min(ROW_TILE, s)
    n_tiles = s // ts
    row = pl.BlockSpec((ts, D_MODEL), lambda i: (i, 0))

    def body(dya_ref, dyb_ref, dybb_ref, dybn_ref, hf_ref, hb_ref, za_ref, xb_ref, xbb_ref, xbn_ref,
             gb_ref, gbb_ref, gbn_ref, gc_ref, gcb_ref, gcn_ref, zb_ref, zbb_ref, zbn_ref, w_ref, a_ref,
             dh_ref, dp_ref, dw_ref, adj_ref, carry):
        @pl.when(pl.program_id(0) == 0)
        def _():
            carry[...] = jnp.zeros_like(carry)

        dyb, dyb_before, dyb_after = _halo_load(dyb_ref, dybb_ref, dybn_ref, n_tiles)
        xb, xb_before, xb_after = _halo_load(xb_ref, xbb_ref, xbn_ref, n_tiles)
        gb, gb_before, gb_after = _halo_load(gb_ref, gbb_ref, gbn_ref, n_tiles)
        gc, gc_before, gc_after = _halo_load(gc_ref, gcb_ref, gcn_ref, n_tiles)
        zb, zb_before, zb_after = _halo_load(zb_ref, zbb_ref, zbn_ref, n_tiles)
        w = w_ref[...]
        dya, za = dya_ref[...], za_ref[...]
        silu_za, dsilu_za = _silu_and_grad(za)
        dh_ref[...] = dya * silu_za
        _scan_tile(a_ref, dh_ref, adj_ref, carry, False, True)
        dp_ref[:, 0:D_MODEL] = (dya * (hf_ref[...] + hb_ref[...]) * dsilu_za).astype(BF16)

        silu_zb, dsilu_zb = _silu_and_grad(zb)
        p, p_before, p_after = gc * xb, gc_before * xb_before, gc_after * xb_after
        cv = _sc_conv(p, p_before, p_after, w)
        dcv = dyb * gb * silu_zb
        dcv_before = dyb_before * gb_before * _silu_and_grad(zb_before)[0]
        dcv_after = dyb_after * gb_after * _silu_and_grad(zb_after)[0]
        dpp = (w[0:1, :] * _shift_rows(dcv, dcv_before, dcv_after, 1) + w[1:2, :] * dcv
               + w[2:3, :] * _shift_rows(dcv, dcv_before, dcv_after, -1))
        dp_ref[:, D_MODEL:2 * D_MODEL] = (dpp * gc).astype(BF16)
        dp_ref[:, 2 * D_MODEL:3 * D_MODEL] = (dyb * cv * silu_zb).astype(BF16)
        dp_ref[:, 3 * D_MODEL:4 * D_MODEL] = (dpp * xb).astype(BF16)
        dp_ref[:, 4 * D_MODEL:5 * D_MODEL] = (dyb * gb * cv * dsilu_zb).astype(BF16)

        @pl.when(pl.program_id(0) == 0)
        def _():
            dw_ref[...] = jnp.zeros_like(dw_ref)

        dw_ref[0:1, :] += jnp.sum(dcv * _shift_rows(p, p_before, p_after, -1), axis=0, keepdims=True)
        dw_ref[1:2, :] += jnp.sum(dcv * p, axis=0, keepdims=True)
        dw_ref[2:3, :] += jnp.sum(dcv * _shift_rows(p, p_before, p_after, 1), axis=0, keepdims=True)

    return _call(
        body, name="even_mix_bwd", grid=(n_tiles,),
        in_specs=[row] + _halo_specs(ts, s, D_MODEL, 1) + [row, row, pl.BlockSpec((ts, D_MODEL), lambda i: (i, 1))]
        + _halo_specs(ts, s, D_MODEL, 2) + _halo_specs(ts, s, D_MODEL, 3) + _halo_specs(ts, s, D_MODEL, 4)
        + _halo_specs(ts, s, D_MODEL, 5) + [_full(sc_w.shape), pl.BlockSpec((None, ts, D_MODEL), lambda i: (2, i, 0))],
        out_specs=[row, pl.BlockSpec((ts, 5 * D_MODEL), lambda i: (i, 0)), _full(sc_w.shape), row],
        out_shape=[jax.ShapeDtypeStruct((s, D_MODEL), F32), jax.ShapeDtypeStruct((s, 5 * D_MODEL), BF16),
                   jax.ShapeDtypeStruct(sc_w.shape, F32), jax.ShapeDtypeStruct((s, D_MODEL), F32)],
        scratch_shapes=[pltpu.VMEM((SUBLANES, D_MODEL), F32)],
        args=[du, du, du, du, hf, hb, proj, *([proj] * 12), sc_w, ab], exchange=exchange)


def even_gates_bwd(proj, adj_f, adj_b, hf, hb, dh, conv_w, conv_b, gate_w, gate_b, lam, exchange=None):
    s = proj.shape[0]
    ts = min(2 * ROW_TILE, s)
    n_tiles = s // ts
    row = pl.BlockSpec((ts, D_MODEL), lambda i: (i, 0))

    def body(xa_ref, xab_ref, xan_ref, af_ref, afb_ref, afn_ref, ab_ref, abb_ref, abn_ref,
             hf_ref, hfb_ref, hfn_ref, hb_ref, hbb_ref, hbn_ref, dh_ref,
             cw_ref, cb_ref, gw_ref, gb_ref, lam_ref, dua_ref, dgw_ref, dgb_ref, dlam_ref):
        @pl.when(pl.program_id(0) == 0)
        def _():
            dgw_ref[...] = jnp.zeros_like(dgw_ref)
            dgb_ref[...] = jnp.zeros_like(dgb_ref)
            dlam_ref[...] = jnp.zeros_like(dlam_ref)

        xa, before, after = _halo_load(xa_ref, xab_ref, xan_ref, n_tiles)
        ua = _rg_conv(xa, before, after, cw_ref[...], cb_ref[...])
        lam_v = lam_ref[...]
        c = RG_C * _softplus(-lam_v)
        dc_dlam = -RG_C * _sigmoid(-lam_v)
        dh = dh_ref[...]
        adj = (_halo_load(af_ref, afb_ref, afn_ref, n_tiles), _halo_load(ab_ref, abb_ref, abn_ref, n_tiles))
        hs = (_halo_load(hf_ref, hfb_ref, hfn_ref, n_tiles), _halo_load(hb_ref, hbb_ref, hbn_ref, n_tiles))
        dua = jnp.zeros_like(ua)
        ua_bf16 = ua.astype(BF16)
        for direction in range(2):
            step = 1 if direction == 0 else -1
            g = dh + _shift_rows(*adj[direction], step)
            da_all = g * _shift_rows(*hs[direction], -step)
            dua_parts = []
            for head in range(RG_HEADS):
                lanes = slice(head * RG_HEAD_DIM, (head + 1) * RG_HEAD_DIM)
                ua_h = ua[:, lanes]
                c_h = c[direction:direction + 1, lanes]
                ua_hb = ua_bf16[:, lanes]
                r, i, a, beta, inv_beta = _rg_gates(ua_hb, gw_ref, gb_ref, c_h, direction, head)
                db_beta = g[:, lanes] * beta
                d_i = db_beta * ua_h
                dbeta = g[:, lanes] * (i * ua_h)
                dlog_a = (da_all[:, lanes] - dbeta * a * inv_beta) * a
                dpr = -c_h * dlog_a * r * (1.0 - r)
                dpi = d_i * i * (1.0 - i)
                dpr_b, dpi_b = dpr.astype(BF16), dpi.astype(BF16)
                dua_parts.append(db_beta * i + _bdot_nt(dpr_b, gw_ref[2 * direction, head])
                                 + _bdot_nt(dpi_b, gw_ref[2 * direction + 1, head]))
                dgw_ref[2 * direction, head] += _bdot_tn(ua_hb, dpr_b)
                dgw_ref[2 * direction + 1, head] += _bdot_tn(ua_hb, dpi_b)
                dgb_ref[2 * direction, head:head + 1, :] += jnp.sum(dpr, axis=0, keepdims=True)
                dgb_ref[2 * direction + 1, head:head + 1, :] += jnp.sum(dpi, axis=0, keepdims=True)
                dlam_ref[direction:direction + 1, lanes] += (
                    jnp.sum(-r * dlog_a, axis=0, keepdims=True) * dc_dlam[direction:direction + 1, lanes])
            dua = dua + jnp.concatenate(dua_parts, axis=1)
        dua_ref[...] = dua

    return _call(
        body, name="even_gates_bwd", grid=(n_tiles,),
        in_specs=_halo_specs(ts, s, D_MODEL, 0) * 5 + [row] + [_full(conv_w.shape), _full(conv_b.shape), _full(gate_w.shape),
                                                             _full(gate_b.shape), _full(lam.shape)],
        out_specs=[row, _full(gate_w.shape), _full(gate_b.shape), _full(lam.shape)],
        out_shape=[jax.ShapeDtypeStruct((s, D_MODEL), F32), jax.ShapeDtypeStruct(gate_w.shape, F32),
                   jax.ShapeDtypeStruct(gate_b.shape, F32), jax.ShapeDtypeStruct(lam.shape, F32)],
        args=[proj, proj, proj, adj_f, adj_f, adj_f, adj_b, adj_b, adj_b, hf, hf, hf, hb, hb, hb, dh, conv_w, conv_b, gate_w,
              gate_b, lam], exchange=exchange)


def rg_conv_bwd(dua, proj, conv_w, exchange=None):
    s = proj.shape[0]
    ts = min(2 * ROW_TILE, s)
    n_tiles = s // ts

    def body(du_ref, dub_ref, dun_ref, xa_ref, xab_ref, xan_ref, cw_ref, dp_ref, dw_ref, db_ref):
        @pl.when(pl.program_id(0) == 0)
        def _():
            dw_ref[...] = jnp.zeros_like(dw_ref)
            db_ref[...] = jnp.zeros_like(db_ref)

        dua, dua_before, dua_after = _halo_load(du_ref, dub_ref, dun_ref, n_tiles)
        xa, xa_before, xa_after = _halo_load(xa_ref, xab_ref, xan_ref, n_tiles)
        cw = cw_ref[...]
        dxa = (cw[0:1, :] * _shift_rows(dua, dua_before, dua_after, 2) + cw[1:2, :] * _shift_rows(dua, dua_before, dua_after, 1)
               + cw[2:3, :] * dua + cw[3:4, :] * _shift_rows(dua, dua_before, dua_after, -1))
        dp_ref[...] = dxa.astype(BF16)
        for tap, offset in enumerate((-2, -1, 0, 1)):
            shifted = xa if offset == 0 else _shift_rows(xa, xa_before, xa_after, offset)
            dw_ref[tap:tap + 1, :] += jnp.sum(dua * shifted, axis=0, keepdims=True)
        db_ref[...] += jnp.sum(dua, axis=0, keepdims=True)

    return _call(
        body, name="rg_conv_bwd", grid=(n_tiles,),
        in_specs=_halo_specs(ts, s, D_MODEL, 0) * 2 + [_full(conv_w.shape)],
        out_specs=[pl.BlockSpec((ts, D_MODEL), lambda i: (i, 0)), _full(conv_w.shape), _full((1, D_MODEL))],
        out_shape=[jax.ShapeDtypeStruct((s, D_MODEL), BF16), jax.ShapeDtypeStruct(conv_w.shape, F32),
                   jax.ShapeDtypeStruct((1, D_MODEL), F32)],
        args=[dua, dua, dua, proj, proj, proj, conv_w], exchange=exchange)


def _split3(x):
    x1 = x.astype(BF16)
    rest = x - x1.astype(F32)
    x2 = rest.astype(BF16)
    return x1, x2, (rest - x2.astype(F32)).astype(BF16)


def _chunk_sum_matrix(t, reverse, transpose):
    i = lax.broadcasted_iota(jnp.int32, (t, t), 0)
    j = lax.broadcasted_iota(jnp.int32, (t, t), 1)
    if transpose:
        i, j = j, i
    same = (i // GLA_CHUNK) == (j // GLA_CHUNK)
    return jnp.where(same & ((j >= i) if reverse else (j <= i)), 1.0, 0.0).astype(BF16)


def _exact_dot(m, x):
    return sum(jnp.dot(m, part, preferred_element_type=F32) for part in _split3(x))


def _chunk_mask(t, reverse):
    i = lax.broadcasted_iota(jnp.int32, (t, t), 0)
    j = lax.broadcasted_iota(jnp.int32, (t, t), 1)
    return ((i // GLA_CHUNK) == (j // GLA_CHUNK)) & ((j >= i) if reverse else (j <= i))


def _chunk_rows(c):
    return slice(c * GLA_CHUNK, (c + 1) * GLA_CHUNK)


def _gla_gate(lr, wg, bg):
    z = _bdot(lr, wg) + bg
    log_alpha = (jnp.minimum(z, 0.0) - jnp.log(1.0 + jnp.exp(-jnp.abs(z)))) * (1.0 / GLA_NORMALIZER)
    return z, log_alpha


def _gla_tile_terms(q, k, bcum, reverse):
    n_chunks = q.shape[0] // GLA_CHUNK
    totals = []
    for c in range(n_chunks):
        edge = c * GLA_CHUNK if reverse else (c + 1) * GLA_CHUNK - 1
        totals.append(bcum[edge:edge + 1, :])
    btot = jnp.concatenate([jnp.broadcast_to(total, (GLA_CHUNK, total.shape[1])) for total in totals], axis=0)
    e_pos, e_neg, e_st = jnp.exp(bcum), jnp.exp(-bcum), jnp.exp(btot - bcum)
    return q * (GLA_DK ** -0.5) * e_pos, k * e_neg, k * e_st, e_pos, e_neg, e_st, [jnp.exp(total) for total in totals]


def _gla_specs(t, n_tiles, reverse_order):
    def tile(i):
        return n_tiles - 1 - i if reverse_order else i

    return tile, [
        pl.BlockSpec((t, GLA_KEY), lambda i: (tile(i), 0)),
        pl.BlockSpec((t, GLA_KEY), lambda i: (tile(i), 1)),
        pl.BlockSpec((t, D_MODEL), lambda i: (tile(i), 1)),
        pl.BlockSpec((t, LANES), lambda i: (tile(i), (ODD_IN_PAD - LANES) // LANES)),
    ]


def gla_fwd(proj, wg, bg, reverse, o_other=None, gnorm=None, post=None):
    s = proj.shape[0]
    t = min(GLA_TILE, s)
    n_tiles = s // t
    n_chunks = t // GLA_CHUNK
    final = o_other is not None
    tile, specs = _gla_specs(t, n_tiles, reverse)

    def body(*refs):
        if final:
            (q_ref, k_ref, v_ref, lr_ref, wg_ref, bg_ref, oo_ref, r_ref, gn_ref, wo_ref, x_ref, gp_ref, t_ref,
             osum_ref, u_ref, st_ref, y_ref, dout_ref, loss_ref, state) = refs
        else:
            q_ref, k_ref, v_ref, lr_ref, wg_ref, bg_ref, o_ref, st_ref, state = refs
            osum_ref = o_ref

        @pl.when(pl.program_id(0) == 0)
        def _():
            state[...] = jnp.zeros_like(state)

        _, log_alpha = _gla_gate(lr_ref[...], wg_ref[...], bg_ref[...])
        bcum = _exact_dot(_chunk_sum_matrix(t, reverse, False), log_alpha)
        q, k, v = q_ref[...], k_ref[...], v_ref[...]
        q_in, k_in, k_st, _, _, _, decays = _gla_tile_terms(q, k, bcum, reverse)
        mask = _chunk_mask(t, reverse)
        order = list(range(n_chunks))[::-1] if reverse else list(range(n_chunks))
        intra, increments = [], []
        for head in range(GLA_HEADS):
            kl = slice(head * GLA_DK, (head + 1) * GLA_DK)
            vl = slice(head * GLA_DV, (head + 1) * GLA_DV)
            scores = jnp.where(mask, _bdot_nt(q_in[:, kl], k_in[:, kl]), 0.0)
            intra.append(_bdot(scores, v[:, vl]))
            increments.append([_bdot_tn(v[_chunk_rows(c), vl], k_st[_chunk_rows(c), kl]) for c in range(n_chunks)])
        for head in range(GLA_HEADS):
            kl = slice(head * GLA_DK, (head + 1) * GLA_DK)
            vl = slice(head * GLA_DV, (head + 1) * GLA_DV)
            running = state[head]
            before = [None] * n_chunks
            for c in order:
                before[c] = running
                st_ref[c, head] = running
                running = running * decays[c][:, kl] + increments[head][c]
            state[head] = running
            inter = [_bdot_nt(q_in[_chunk_rows(c), kl], before[c]) for c in range(n_chunks)]
            osum_ref[:, vl] = intra[head] + jnp.concatenate(inter, axis=0)
        if final:
            osum = osum_ref[...] + oo_ref[...]
            osum_ref[...] = osum
            silu_r, _ = _silu_and_grad(r_ref[...])
            gn = gn_ref[...]
            for head in range(GLA_HEADS):
                vl = slice(head * GLA_DV, (head + 1) * GLA_DV)
                u_ref[:, vl] = (_rms(osum[:, vl], gn[:, vl]) * silu_r[:, vl]).astype(BF16)

            @pl.when(pl.program_id(0) == 0)
            def _():
                loss_ref[...] = jnp.zeros_like(loss_ref)

            y = jnp.dot(u_ref[...], wo_ref[...], preferred_element_type=F32)
            y_ref[...] = y
            diff = x_ref[...] + _rms(y, gp_ref[...]) - t_ref[...]
            dout_ref[...] = diff * (1.0 / D_MODEL)
            loss_ref[...] += 0.5 * jnp.sum(jnp.mean(diff * diff, axis=-1, keepdims=True))

    row = pl.BlockSpec((t, D_MODEL), lambda i: (tile(i), 0))
    st_spec = pl.BlockSpec((n_chunks, GLA_HEADS, GLA_DV, GLA_DK), lambda i: (tile(i), 0, 0, 0))
    st_shape = jax.ShapeDtypeStruct((s // GLA_CHUNK, GLA_HEADS, GLA_DV, GLA_DK), F32)
    in_specs = specs + [_full(wg.shape), _full(bg.shape)]
    args = [proj, proj, proj, proj, wg, bg]
    if final:
        w_out, xres, g_post, target = post
        in_specs += [row, pl.BlockSpec((t, D_MODEL), lambda i: (tile(i), 2)), _full(gnorm.shape), _full(w_out.shape), row,
                     _full(g_post.shape), row]
        args += [o_other, proj, gnorm, w_out, xres, g_post, target]
        out_specs = [row, row, st_spec, row, row, _full((SUBLANES, LANES))]
        out_shape = [jax.ShapeDtypeStruct((s, D_MODEL), F32), jax.ShapeDtypeStruct((s, D_MODEL), BF16), st_shape,
                     jax.ShapeDtypeStruct((s, D_MODEL), F32), jax.ShapeDtypeStruct((s, D_MODEL), F32),
                     jax.ShapeDtypeStruct((SUBLANES, LANES), F32)]
    else:
        out_specs = [row, st_spec]
        out_shape = [jax.ShapeDtypeStruct((s, D_MODEL), F32), st_shape]
    return pl.pallas_call(
        body, name="gla_fwd_rev" if reverse else "gla_fwd", grid=(n_tiles,), in_specs=in_specs, out_specs=out_specs,
        out_shape=out_shape, scratch_shapes=[pltpu.VMEM((GLA_HEADS, GLA_DV, GLA_DK), F32)], compiler_params=_params(1),
    )(*args)


def gla_bwd(proj, wg, bg, do, states, reverse, first=None):
    s = proj.shape[0]
    t = min(GLA_TILE, s)
    n_tiles = s // t
    n_chunks = t // GLA_CHUNK
    final = first is not None
    tile, specs = _gla_specs(t, n_tiles, not reverse)

    def body(*refs):
        if final:
            (q_ref, k_ref, v_ref, lr_ref, wg_ref, bg_ref, do_ref, st_ref, dqkv1_ref, dlr1_ref, dr_ref,
             dp_ref, dwg_ref, dbg_ref, dstate, dqkv, dbc, dbt) = refs
        else:
            (q_ref, k_ref, v_ref, lr_ref, wg_ref, bg_ref, do_ref, st_ref,
             dqkv, dlr_ref, dwg_ref, dbg_ref, dstate, dbc, dbt) = refs

        @pl.when(pl.program_id(0) == 0)
        def _():
            dstate[...] = jnp.zeros_like(dstate)
            dwg_ref[...] = jnp.zeros_like(dwg_ref)
            dbg_ref[...] = jnp.zeros_like(dbg_ref)

        lr, wg_v = lr_ref[...], wg_ref[...]
        z, log_alpha = _gla_gate(lr, wg_v, bg_ref[...])
        bcum = _exact_dot(_chunk_sum_matrix(t, reverse, False), log_alpha)
        q, k, v, do_v = q_ref[...], k_ref[...], v_ref[...], do_ref[...]
        q_in, k_in, k_st, e_pos, e_neg, e_st, decays = _gla_tile_terms(q, k, bcum, reverse)
        mask = _chunk_mask(t, reverse)
        order = list(range(n_chunks)) if reverse else list(range(n_chunks))[::-1]
        q_b, k_b, ks_b, v_b, do_b = (a.astype(BF16) for a in (q_in, k_in, k_st, v, do_v))
        dq_intra, dk_intra, dv_intra, increments = [], [], [], []
        for head in range(GLA_HEADS):
            kl = slice(head * GLA_DK, (head + 1) * GLA_DK)
            vl = slice(head * GLA_DV, (head + 1) * GLA_DV)
            scores = jnp.where(mask, _bdot_nt(q_b[:, kl], k_b[:, kl]), 0.0).astype(BF16)
            dscores = jnp.where(mask, _bdot_nt(do_b[:, vl], v_b[:, vl]), 0.0).astype(BF16)
            dv_intra.append(_bdot_tn(scores, do_b[:, vl]))
            dq_intra.append(_bdot(dscores, k_b[:, kl]))
            dk_intra.append(_bdot_tn(dscores, q_b[:, kl]))
            increments.append([_bdot_tn(do_b[_chunk_rows(c), vl], q_b[_chunk_rows(c), kl]) for c in range(n_chunks)])
        after_all, ddecay_all = [], []
        for head in range(GLA_HEADS):
            kl = slice(head * GLA_DK, (head + 1) * GLA_DK)
            running = dstate[head]
            after, ddecay = [None] * n_chunks, [None] * n_chunks
            for c in order:
                after[c] = running
                ddecay[c] = jnp.sum(running * st_ref[c, head], axis=0, keepdims=True)
                running = running * decays[c][:, kl] + increments[head][c]
            dstate[head] = running
            after_all.append(after)
            ddecay_all.append(ddecay)
        for head in range(GLA_HEADS):
            kl = slice(head * GLA_DK, (head + 1) * GLA_DK)
            vl = slice(head * GLA_DV, (head + 1) * GLA_DV)
            after, ddecay = after_all[head], ddecay_all[head]
            dq_inter = jnp.concatenate([_bdot(do_b[_chunk_rows(c), vl], st_ref[c, head]) for c in range(n_chunks)], axis=0)
            dv_inter = jnp.concatenate([_bdot_nt(ks_b[_chunk_rows(c), kl], after[c]) for c in range(n_chunks)], axis=0)
            dk_st = jnp.concatenate([_bdot(v_b[_chunk_rows(c), vl], after[c]) for c in range(n_chunks)], axis=0)
            dq_in = dq_intra[head] + dq_inter
            ks_h = k_st[:, kl]
            dqkv[:, 2 * GLA_KEY + head * GLA_DV:2 * GLA_KEY + (head + 1) * GLA_DV] = dv_intra[head] + dv_inter
            dqkv[:, kl] = dq_in * (GLA_DK ** -0.5) * e_pos[:, kl]
            dqkv[:, GLA_KEY + head * GLA_DK:GLA_KEY + (head + 1) * GLA_DK] = dk_intra[head] * e_neg[:, kl] + dk_st * e_st[:, kl]
            dbc[:, kl] = dq_in * q_in[:, kl] - dk_intra[head] * k_in[:, kl] - dk_st * ks_h
            weighted = dk_st * ks_h
            for c in range(n_chunks):
                dbtot = jnp.sum(weighted[_chunk_rows(c)], axis=0, keepdims=True) + ddecay[c] * decays[c][:, kl]
                dbt[_chunk_rows(c), kl] = jnp.broadcast_to(dbtot, (GLA_CHUNK, GLA_DK))
        dlog_alpha = _exact_dot(_chunk_sum_matrix(t, reverse, True), dbc[...]) + dbt[...]
        dz = dlog_alpha * _sigmoid(-z) * (1.0 / GLA_NORMALIZER)
        dlr = _bdot_nt(dz, wg_v)
        dwg_ref[...] += _bdot_tn(lr, dz)
        dbg_ref[...] += jnp.sum(dz, axis=0, keepdims=True)
        if final:
            dp_ref[:, :2 * D_MODEL] = (dqkv[...] + dqkv1_ref[...]).astype(BF16)
            dp_ref[:, 2 * D_MODEL:3 * D_MODEL] = dr_ref[...]
            dp_ref[:, 3 * D_MODEL:] = (dlr + dlr1_ref[...]).astype(BF16)
        else:
            dlr_ref[...] = dlr

    row = pl.BlockSpec((t, D_MODEL), lambda i: (tile(i), 0))
    wide = pl.BlockSpec((t, 2 * D_MODEL), lambda i: (tile(i), 0))
    narrow = pl.BlockSpec((t, LANES), lambda i: (tile(i), 0))
    st_spec = pl.BlockSpec((n_chunks, GLA_HEADS, GLA_DV, GLA_DK), lambda i: (tile(i), 0, 0, 0))
    in_specs = specs + [_full(wg.shape), _full(bg.shape), row, st_spec]
    args = [proj, proj, proj, proj, wg, bg, do, states]
    acc_specs = [_full(wg.shape), _full(bg.shape)]
    acc_shapes = [jax.ShapeDtypeStruct(wg.shape, F32), jax.ShapeDtypeStruct(bg.shape, F32)]
    scratch = [pltpu.VMEM((GLA_HEADS, GLA_DV, GLA_DK), F32)]
    work = [pltpu.VMEM((t, GLA_KEY), F32), pltpu.VMEM((t, GLA_KEY), F32)]
    if final:
        in_specs += [wide, narrow, row]
        args += list(first)
        out_specs = [pl.BlockSpec((t, ODD_IN_PAD), lambda i: (tile(i), 0))] + acc_specs
        out_shape = [jax.ShapeDtypeStruct((s, ODD_IN_PAD), BF16)] + acc_shapes
        scratch += [pltpu.VMEM((t, 2 * D_MODEL), F32)] + work
    else:
        out_specs = [wide, narrow] + acc_specs
        out_shape = [jax.ShapeDtypeStruct((s, 2 * D_MODEL), F32), jax.ShapeDtypeStruct((s, LANES), F32)] + acc_shapes
        scratch += work
    return pl.pallas_call(
        body, name="gla_bwd_rev" if reverse else "gla_bwd", grid=(n_tiles,), in_specs=in_specs, out_specs=out_specs,
        out_shape=out_shape, scratch_shapes=scratch, compiler_params=_params(1),
    )(*args)


def pair_sum(grad):
    r, w = grad.shape[0], grad.shape[1] // N_DEV
    n_chips = N_DEV // 2
    sibling = Exchange.SIBLING

    def body(even_ref, odd_ref, g_ref, o_ref, land_ref, theirs, send, recv, load_sem):
        k = pl.program_id(0)
        xx, yy, cc = lax.axis_index("x"), lax.axis_index("y"), lax.axis_index("c")

        def to_sibling(kk):
            block = g_ref.at[:, pl.ds(pl.multiple_of((2 * kk + 1 - cc) * w, LANES), w)]
            return pltpu.make_async_remote_copy(src_ref=block, dst_ref=land_ref.at[kk], send_sem=send.at[kk], recv_sem=recv.at[kk],
                                                device_id=_peer(xx, yy, cc, sibling)[0], device_id_type=MESH_ID)

        @pl.when(k == 0)
        def _():
            for kk in range(n_chips):
                to_sibling(kk).start()

        to_sibling(k).wait_recv()
        load = pltpu.make_async_copy(land_ref.at[k], theirs, load_sem)
        load.start()
        load.wait()
        mine = jnp.where(cc == 1, odd_ref[...], even_ref[...])
        o_ref[...] = (mine.astype(F32) + theirs[...].astype(F32)).astype(o_ref.dtype)

        @pl.when(k == n_chips - 1)
        def _():
            for kk in range(n_chips):
                to_sibling(kk).wait_send()

    hbm = pl.BlockSpec(memory_space=pl.ANY)
    shape = jax.ShapeDtypeStruct((n_chips, r, w), grad.dtype)
    return pl.pallas_call(
        body, name="pair_sum", grid=(n_chips,),
        in_specs=[pl.BlockSpec((r, w), lambda k: (0, 2 * k)), pl.BlockSpec((r, w), lambda k: (0, 2 * k + 1)), hbm],
        out_specs=[pl.BlockSpec((None, r, w), lambda k: (k, 0, 0)), hbm], out_shape=[shape, shape],
        scratch_shapes=[pltpu.VMEM((r, w), grad.dtype), pltpu.SemaphoreType.DMA((n_chips,)), pltpu.SemaphoreType.DMA((n_chips,)),
                        pltpu.SemaphoreType.DMA(())],
        compiler_params=_params(1),
    )(grad, grad, grad)[0]


def _adamw_update(g, w, m, v):
    new_m = ADAM_B1 * m + (1.0 - ADAM_B1) * g
    new_v = ADAM_B2 * v + (1.0 - ADAM_B2) * (g * g)
    m_hat = new_m / (1.0 - ADAM_B1 ** ADAM_STEP)
    v_hat = new_v / (1.0 - ADAM_B2 ** ADAM_STEP)
    return -ADAM_LR * (m_hat / (jnp.sqrt(v_hat) + ADAM_EPS) + ADAM_WD * w), new_m, new_v


def sum_parts(parts, name):
    _, r, c = parts.shape

    def body(p_ref, o_ref):
        total = p_ref[0].astype(F32)
        for j in range(1, N_DEV):
            total = total + p_ref[j].astype(F32)
        o_ref[...] = total

    return pl.pallas_call(body, name=name, in_specs=[_full(parts.shape)], out_specs=_full((r, c)), grid=(1,),
                          out_shape=jax.ShapeDtypeStruct((r, c), F32), compiler_params=_params(1))(parts)


def adamw(parts, w, m, v, name, exchange=None):
    n, r, c = parts.shape
    tr = r
    while tr * c * 4 > ADAMW_BLOCK_BYTES and tr % (2 * SUBLANES) == 0:
        tr //= 2

    def body(p_ref, w_ref, m_ref, v_ref, g_ref, d_ref, nm_ref, nv_ref):
        g = p_ref[0].astype(F32)
        for j in range(1, n):
            g = g + p_ref[j].astype(F32)
        g_ref[...] = g
        d_ref[...], nm_ref[...], nv_ref[...] = _adamw_update(g, w_ref[...], m_ref[...], v_ref[...])

    row = pl.BlockSpec((tr, c), lambda i: (i, 0))
    return _call(
        body, name=name, grid=(r // tr,),
        in_specs=[pl.BlockSpec((n, tr, c), lambda i: (0, i, 0)), row, row, row], out_specs=[row] * 4,
        out_shape=[jax.ShapeDtypeStruct((r, c), F32)] * 4, args=[parts, w, m, v], exchange=exchange)


def _small_views(shape):
    if len(shape) == 2:
        return [((slice(None), slice(None)), (slice(None), slice(None)))]
    if len(shape) == 3:
        return [((slice(None), slice(None)), (0,))]
    rows = shape[2]
    return [((slice(k * rows, (k + 1) * rows), slice(None)), (0, k)) for k in range(shape[1])]


def adamw_small(landings, w, m, v):
    names = list(landings)
    n = len(names)
    shapes = [w[name].shape for name in names]

    def body(*refs):
        land, ws, ms, vs = refs[:n], refs[n:2 * n], refs[2 * n:3 * n], refs[3 * n:4 * n]
        outs = [refs[(4 + k) * n:(5 + k) * n] for k in range(4)]
        for k in range(n):
            total = land[k][0]
            for j in range(1, N_DEV):
                total = total + land[k][j]
            for rows, at in _small_views(shapes[k]):
                g = total[rows]
                outs[0][k][at] = g
                outs[1][k][at], outs[2][k][at], outs[3][k][at] = _adamw_update(g, ws[k][at], ms[k][at], vs[k][at])

    blocks = [_full(sh) for sh in shapes]
    outs = pl.pallas_call(
        body, name="adamw_small", grid=(1,),
        in_specs=[_full(landings[name].shape) for name in names] + blocks * 3, out_specs=blocks * 4,
        out_shape=[jax.ShapeDtypeStruct(sh, F32) for sh in shapes] * 4, compiler_params=_params(1),
    )(*[landings[name] for name in names], *[src[name] for src in (w, m, v) for name in names])
    return [dict(zip(names, outs[k * n:(k + 1) * n])) for k in range(4)]


def adamw_replicated(land_vec, land_gate_b, land_loss, names, w, m, v, gate_b):
    n = len(names)

    def body(*refs):
        vec_ref, gb_ref, loss_ref = refs[:3]
        ws, ms, vs = refs[3:3 + n], refs[3 + n:3 + 2 * n], refs[3 + 2 * n:3 + 3 * n]
        gw_ref, gm_ref, gv_ref = refs[3 + 3 * n:6 + 3 * n]
        outs = refs[6 + 3 * n:]
        vec, gb, loss = vec_ref[0], gb_ref[0], loss_ref[0]
        for j in range(1, N_DEV):
            vec, gb, loss = vec + vec_ref[j], gb + gb_ref[j], loss + loss_ref[j]
        for k in range(n):
            g = vec[k:k + 1, :]
            outs[k][...] = g
            outs[n + k][...], outs[2 * n + k][...], outs[3 * n + k][...] = _adamw_update(g, ws[k][...], ms[k][...], vs[k][...])
        outs[4 * n][...] = gb
        outs[4 * n + 1][...], outs[4 * n + 2][...], outs[4 * n + 3][...] = _adamw_update(gb, gw_ref[...], gm_ref[...], gv_ref[...])
        outs[4 * n + 4][...] = loss

    vec_block, gb_block = _full((1, D_MODEL)), _full(gate_b[0].shape)
    outs = pl.pallas_call(
        body, name="adamw_replicated", grid=(1,),
        in_specs=[_full(land_vec.shape), _full(land_gate_b.shape), _full(land_loss.shape)] + [vec_block] * (3 * n) + [gb_block] * 3,
        out_specs=[vec_block] * (4 * n) + [gb_block] * 4 + [_full(land_loss.shape[1:])],
        out_shape=[jax.ShapeDtypeStruct((1, D_MODEL), F32)] * (4 * n) + [jax.ShapeDtypeStruct(gate_b[0].shape, F32)] * 4
        + [jax.ShapeDtypeStruct(land_loss.shape[1:], F32)],
        compiler_params=_params(1),
    )(land_vec, land_gate_b, land_loss, *[src[name] for src in (w, m, v) for name in names], *gate_b)
    results = {name: [outs[k * n + i] for k in range(4)] for i, name in enumerate(names)}
    return results, outs[4 * n:4 * n + 4], outs[4 * n + 4]


SMALL_SHARDED = ("rg_conv_w", "rg_lambda", "sc_conv_w", "odd_norm_pre", "odd_norm_post", "gla_b_gate", "gla_norm_g", "gla_w_gate_lr")
SMALL_ROWS = {"rg_conv_w": (0, 4), "rg_lambda": (4, 2), "sc_conv_w": (6, 3), "odd_norm_pre": (9, 1), "odd_norm_post": (10, 1),
              "gla_b_gate": (11, 2), "gla_norm_g": (13, 1), "gla_w_gate_lr": (16, 32)}


def _pack_small(shards):
    pieces, at = [], 0
    for name in SMALL_SHARDED:
        start, rows = SMALL_ROWS[name]
        if start > at:
            pieces.append(jnp.zeros((start - at, LANES), F32))
        a = shards[name].reshape(rows, -1)
        pieces.append(jnp.pad(a, ((0, 0), (0, LANES - a.shape[1]))))
        at = start + rows
    return jnp.concatenate(pieces, axis=0)


def _unpack_gathered(g):
    def cols(name, width):
        start, rows = SMALL_ROWS[name]
        return jnp.transpose(g[:, start:start + rows, :width], (1, 0, 2)).reshape(rows, N_DEV * width)

    w_lr = cols("gla_w_gate_lr", GLA_KEY // N_DEV).reshape(2, GLA_RANK, GLA_KEY)
    return dict(rg_conv_w=cols("rg_conv_w", LANES), rg_lambda=cols("rg_lambda", LANES), sc_conv_w=cols("sc_conv_w", LANES),
                odd_norm_pre=cols("odd_norm_pre", LANES), odd_norm_post=cols("odd_norm_post", LANES),
                gla_b_gate=cols("gla_b_gate", GLA_KEY // N_DEV), gla_norm_g=cols("gla_norm_g", GLA_DV // N_DEV), gla_w_gate_lr=w_lr)


def _blocks_along_columns(a, rows):
    return jnp.transpose(a.reshape(rows, N_DEV, -1), (1, 0, 2))


def kernel(x, even_norm_pre, even_norm_post, even_w_in, rg_conv_w, rg_conv_b, rg_gate_w, rg_gate_b, rg_lambda, sc_conv_w, even_w_out, odd_norm_pre, odd_norm_post, odd_w_in, gla_w_gate_lr, gla_b_gate, gla_norm_g, odd_w_out, loss_target, m_even_norm_pre, m_even_norm_post, m_even_w_in, m_rg_conv_w, m_rg_conv_b, m_rg_gate_w, m_rg_gate_b, m_rg_lambda, m_sc_conv_w, m_even_w_out, m_odd_norm_pre, m_odd_norm_post, m_odd_w_in, m_gla_w_gate_lr, m_gla_b_gate, m_gla_norm_g, m_odd_w_out, v_even_norm_pre, v_even_norm_post, v_even_w_in, v_rg_conv_w, v_rg_conv_b, v_rg_gate_w, v_rg_gate_b, v_rg_lambda, v_sc_conv_w, v_even_w_out, v_odd_norm_pre, v_odd_norm_post, v_odd_w_in, v_gla_w_gate_lr, v_gla_b_gate, v_gla_norm_g, v_odd_w_out):
    weights = dict(even_norm_pre=even_norm_pre, even_norm_post=even_norm_post, even_w_in=even_w_in, rg_conv_w=rg_conv_w,
                   rg_conv_b=rg_conv_b, rg_gate_w=rg_gate_w, rg_gate_b=rg_gate_b, rg_lambda=rg_lambda, sc_conv_w=sc_conv_w,
                   even_w_out=even_w_out, odd_norm_pre=odd_norm_pre, odd_norm_post=odd_norm_post, odd_w_in=odd_w_in,
                   gla_w_gate_lr=gla_w_gate_lr, gla_b_gate=gla_b_gate, gla_norm_g=gla_norm_g, odd_w_out=odd_w_out)
    m_in = dict(even_norm_pre=m_even_norm_pre, even_norm_post=m_even_norm_post, even_w_in=m_even_w_in, rg_conv_w=m_rg_conv_w,
                rg_conv_b=m_rg_conv_b, rg_gate_w=m_rg_gate_w, rg_gate_b=m_rg_gate_b, rg_lambda=m_rg_lambda, sc_conv_w=m_sc_conv_w,
                even_w_out=m_even_w_out, odd_norm_pre=m_odd_norm_pre, odd_norm_post=m_odd_norm_post, odd_w_in=m_odd_w_in,
                gla_w_gate_lr=m_gla_w_gate_lr, gla_b_gate=m_gla_b_gate, gla_norm_g=m_gla_norm_g, odd_w_out=m_odd_w_out)
    v_in = dict(even_norm_pre=v_even_norm_pre, even_norm_post=v_even_norm_post, even_w_in=v_even_w_in, rg_conv_w=v_rg_conv_w,
                rg_conv_b=v_rg_conv_b, rg_gate_w=v_rg_gate_w, rg_gate_b=v_rg_gate_b, rg_lambda=v_rg_lambda, sc_conv_w=v_sc_conv_w,
                even_w_out=v_even_w_out, odd_norm_pre=v_odd_norm_pre, odd_norm_post=v_odd_norm_post, odd_w_in=v_odd_w_in,
                gla_w_gate_lr=v_gla_w_gate_lr, gla_b_gate=v_gla_b_gate, gla_norm_g=v_gla_norm_g, odd_w_out=v_odd_w_out)
    names = list(weights)
    shapes = {n: weights[n].shape for n in names}
    xs = x[0]
    tgt = loss_target[0]

    proj_e, h_e, w_in_e, small_all = gather_matmul(xs, even_norm_pre, even_w_in[0].astype(BF16),
                                                   _pack_small({n: weights[n][0] for n in SMALL_SHARDED}), 2 * MM_TILE)
    small = _unpack_gathered(small_all)
    gate_w = rg_gate_w[0].reshape(4, RG_HEADS, RG_HEAD_DIM, RG_HEAD_DIM).astype(BF16)
    gate_b = rg_gate_b[0].reshape(4, RG_HEADS, RG_HEAD_DIM)
    conv_b = rg_conv_b
    wg_pad = [jnp.pad(small["gla_w_gate_lr"][d], ((GLA_RANK * d, LANES - GLA_RANK * (d + 1)), (0, 0))).astype(BF16) for d in range(2)]
    bg = [small["gla_b_gate"][d:d + 1] for d in range(2)]
    gnorm = jnp.tile(small["gla_norm_g"], (1, GLA_HEADS))

    half = D_MODEL // 2
    behind_gates = Exchange()
    behind_gates.gather(even_w_out[0].astype(BF16), via_sibling=True)
    behind_gates.gather(odd_w_in[0, :half].astype(BF16), via_sibling=True)
    (ab, hf), (w_out_e, w_in_o_top) = even_gates_fwd(proj_e, small["rg_conv_w"], conv_b, gate_w, gate_b, small["rg_lambda"],
                                                     exchange=behind_gates)
    w_out_e = w_out_e.reshape(2 * D_MODEL, D_MODEL)
    behind_mix_fwd = Exchange()
    behind_mix_fwd.gather(odd_w_in[0, half:].astype(BF16), via_sibling=True)
    behind_mix_fwd.gather(odd_w_out[0].astype(BF16), via_sibling=True)
    (u_e, hb, y_e, x1), (w_in_o_bottom, w_out_o) = even_mix_fwd(ab, hf, proj_e, small["sc_conv_w"], w_out_e, xs, even_norm_post,
                                                                exchange=behind_mix_fwd)
    w_out_o = w_out_o.reshape(D_MODEL, D_MODEL)
    w_in_o = jnp.concatenate([jnp.transpose(part, (1, 0, 2)).reshape(half, ODD_IN) for part in (w_in_o_top, w_in_o_bottom)], axis=0)
    w_in_o = jnp.pad(w_in_o, ((0, 0), (0, ODD_IN_PAD - ODD_IN)))

    proj_o, h_o = rms_matmul(x1, small["odd_norm_pre"], w_in_o, MM_TILE, ODD_IN_PAD, "odd_in")
    o_f, st_f = gla_fwd(proj_o, wg_pad[0], bg[0], False)
    osum, u_o, st_b, y_o, dout, loss_part = gla_fwd(proj_o, wg_pad[1], bg[1], True, o_other=o_f, gnorm=gnorm,
                                                    post=(w_out_o, x1, small["odd_norm_post"], tgt))

    do, dr, dy_o, d_odd_norm_post, d_gnorm = normbwd_matmul_nt(y_o, small["odd_norm_post"], dout, w_out_o, D_MODEL, "odd_out_bwd",
                                                               gla=(proj_o, osum, gnorm))
    d_w_out_o = matmul_tn(u_o, dy_o, D_MODEL, D_MODEL, 4 * MM_TILE, BF16, "odd_w_out_grad")
    dqkv_f, dlr_f, dwg_f, dbg_f = gla_bwd(proj_o, wg_pad[0], bg[0], do, st_f, False)
    dproj_o, dwg_b, dbg_b = gla_bwd(proj_o, wg_pad[1], bg[1], do, st_b, True, first=(dqkv_f, dlr_f, dr))
    dx1, d_odd_norm_pre = matmul_nt_normbwd(dproj_o, w_in_o, x1, small["odd_norm_pre"], dout, MM_TILE, ODD_IN_PAD, "odd_in_bwd")
    d_w_in_o = matmul_tn(h_o, dproj_o, D_MODEL, ODD_IN_PAD // 5, 8 * MM_TILE, BF16, "odd_w_in_grad")

    landed = {}
    behind_out = Exchange()
    behind_out.scatter(d_w_out_o.reshape(N_DEV, D_MODEL // N_DEV, D_MODEL))
    behind_out.scatter(d_odd_norm_pre, columns=True)
    behind_out.scatter(d_odd_norm_post, columns=True)
    behind_out.scatter(_blocks_along_columns(jnp.concatenate([dbg_f, dbg_b], axis=0), 2))
    behind_out.scatter(_blocks_along_columns(d_gnorm, 1))
    behind_out.scatter(_blocks_along_columns(jnp.concatenate([dwg_f[:GLA_RANK], dwg_b[GLA_RANK:2 * GLA_RANK]], axis=0), 2 * GLA_RANK))
    (du_e, dy_e, d_even_norm_post), got = normbwd_matmul_nt(y_e, even_norm_post, dx1, w_out_e, 2 * D_MODEL, "even_out_bwd",
                                                           exchange=behind_out)
    p_w_out_o = got[0]
    for n, part in zip(("odd_norm_pre", "odd_norm_post", "gla_b_gate", "gla_norm_g", "gla_w_gate_lr"), got[1:]):
        landed[n] = part
    d_w_out_e = matmul_tn(u_e, dy_e, D_MODEL, D_MODEL, 4 * MM_TILE, BF16, "even_w_out_grad")
    behind_mix = Exchange()
    behind_mix.scatter(d_w_out_e.reshape(N_DEV, 2 * D_MODEL // N_DEV, D_MODEL))
    (dh, drest, d_sc_w, adj_b), (p_w_out_e,) = even_mix_bwd(du_e, hf, hb, proj_e, small["sc_conv_w"], ab, exchange=behind_mix)
    adj_f = linear_scan(ab, 0, dh.reshape(1, *dh.shape), 0, True, True, "scan_fwd_adjoint")
    behind_gates_bwd = Exchange()
    behind_gates_bwd.scatter(jnp.transpose(d_w_in_o[:, :ODD_IN].reshape(D_MODEL, N_DEV, ODD_SHARD), (1, 0, 2)))
    behind_gates_bwd.scatter(d_sc_w, columns=True)
    (dua, d_gate_w, d_gate_b, d_lam), (p_w_in_o, landed["sc_conv_w"]) = even_gates_bwd(
        proj_e, adj_f, adj_b, hf, hb, dh, small["rg_conv_w"], conv_b, gate_w, gate_b, small["rg_lambda"], exchange=behind_gates_bwd)
    gate_w_rows = 4 * RG_HEADS * RG_HEAD_DIM
    behind_conv = Exchange()
    behind_conv.scatter(d_gate_w.reshape(N_DEV, gate_w_rows // N_DEV, RG_HEAD_DIM))
    behind_conv.scatter(d_lam, columns=True)
    (dxa, d_conv_w, d_conv_b), (p_gate_w, landed["rg_lambda"]) = rg_conv_bwd(dua, proj_e, small["rg_conv_w"], exchange=behind_conv)
    behind_w_grad = Exchange()
    behind_w_grad.gather(sum_parts(p_gate_w, "sum_gate_w"))
    d_w_in_e, (g_gate_w_all,) = matmul_tn(h_e, drest, D_MODEL, D_MODEL, 4 * MM_TILE, BF16, "even_w_in_grad",
                                          exchange=behind_w_grad, b_first=dxa)
    behind_in_bwd = Exchange()
    behind_in_bwd.among_chips(pair_sum(d_w_in_e))
    behind_in_bwd.scatter(d_conv_w, columns=True)
    (grad_x, d_even_norm_pre), (p_w_in_e, landed["rg_conv_w"]) = matmul_nt_normbwd(
        drest, w_in_e, xs, even_norm_pre, dx1, 2 * MM_TILE, D_MODEL, "even_in_bwd", exchange=behind_in_bwd, first=dxa)
    last = Exchange()
    replicated_vecs = ("even_norm_pre", "even_norm_post", "rg_conv_b")
    last.gather(jnp.concatenate([d_even_norm_pre, d_even_norm_post, d_conv_b], axis=0))
    last.gather(d_gate_b.reshape(4 * RG_HEADS, RG_HEAD_DIM))
    last.gather(loss_part)

    results = {}

    def update(name, parts_, shape2d, exchange=None):
        outs = adamw(parts_, weights[name][0].reshape(shape2d), m_in[name][0].reshape(shape2d), v_in[name][0].reshape(shape2d),
                     "adamw_" + name, exchange=exchange)
        if exchange is not None:
            outs, gathered = outs
        results[name] = [o.reshape(shapes[name]) for o in outs]
        return gathered if exchange is not None else None

    land_vec, land_gate_b, land_loss = update("even_w_in", p_w_in_e, (D_MODEL, EVEN_SHARD), exchange=last)
    update("even_w_out", p_w_out_e, (2 * D_MODEL // N_DEV, D_MODEL))
    update("odd_w_in", p_w_in_o, (D_MODEL, ODD_SHARD))
    update("odd_w_out", p_w_out_o, (D_MODEL // N_DEV, D_MODEL))
    update("rg_gate_w", g_gate_w_all.reshape(1, gate_w_rows, RG_HEAD_DIM), (gate_w_rows, RG_HEAD_DIM))
    small_out = adamw_small({n: landed[n] for n in SMALL_SHARDED}, weights, m_in, v_in)
    for n in SMALL_SHARDED:
        results[n] = [o[n] for o in small_out]
    gate_b_shape = (4 * RG_HEADS, RG_HEAD_DIM)
    rep_out, gate_b_out, loss_all = adamw_replicated(land_vec, land_gate_b, land_loss, replicated_vecs, weights, m_in, v_in,
                                                     [src["rg_gate_b"].reshape(gate_b_shape) for src in (weights, m_in, v_in)])
    results.update(rep_out)
    results["rg_gate_b"] = [o.reshape(shapes["rg_gate_b"]) for o in gate_b_out]

    return (loss_all[0, 0], grad_x.reshape(x.shape), *[results[n][0] for n in names], *[results[n][1] for n in names],
            *[results[n][2] for n in names], *[results[n][3] for n in names])
```

```python
import functools

import jax
import jax.numpy as jnp
from jax import lax
from jax.experimental import pallas as pl
from jax.experimental.pallas import tpu as pltpu

F32 = jnp.float32
BF16 = jnp.bfloat16

N_DEV = 8
D_MODEL = 1024
NORM_EPS = 1e-6
RG_HEADS = 8
RG_HEAD_DIM = 128
RG_C = 8.0
GLA_HEADS = 4
GLA_DK = 128
GLA_DV = 256
GLA_KEY = 512
GLA_RANK = 16
GLA_NORMALIZER = 16.0
GLA_CHUNK = 64
EVEN_IN = 6144
ODD_IN = 3104
ODD_IN_PAD = 3200
ODD_SHARD = ODD_IN // N_DEV
EVEN_SHARD = EVEN_IN // N_DEV
ADAM_LR = 0.001
ADAM_B1 = 0.9
ADAM_B2 = 0.999
ADAM_EPS = 1e-08
ADAM_WD = 0.01
ADAM_STEP = 10

SMALLEST_NORMAL = 1.1754944e-38
SUBLANES = 8
LANES = 128
VMEM_LIMIT_BYTES = 48 * 2 ** 20
ROW_TILE = 256
GLA_TILE = 256
MM_TILE = 512
ADAMW_BLOCK_BYTES = 2 ** 20
PACK_ROWS = 48
MESH_ID = pl.DeviceIdType.MESH


def _params(n_grid):
    return pltpu.CompilerParams(dimension_semantics=("arbitrary",) * n_grid, vmem_limit_bytes=VMEM_LIMIT_BYTES)


def _bdot(a, b):
    return jnp.dot(a.astype(BF16), b.astype(BF16), preferred_element_type=F32)


def _bdot_nt(a, b):
    return lax.dot_general(a.astype(BF16), b.astype(BF16), (((1,), (1,)), ((), ())), preferred_element_type=F32)


def _bdot_tn(a, b):
    return lax.dot_general(a.astype(BF16), b.astype(BF16), (((0,), (0,)), ((), ())), preferred_element_type=F32)


def _rstd(x):
    return lax.rsqrt(jnp.mean(x * x, axis=-1, keepdims=True) + NORM_EPS)


def _rms(x, g):
    return x * _rstd(x) * g


def _rms_bwd(x, g, dy):
    xh = x * _rstd(x)
    dyg = dy * g
    dx = _rstd(x) * (dyg - xh * jnp.mean(dyg * xh, axis=-1, keepdims=True))
    return dx, jnp.sum(dy * xh, axis=0, keepdims=True)


def _sigmoid(z):
    return 0.5 * jnp.tanh(0.5 * z) + 0.5


def _silu_and_grad(z):
    s = _sigmoid(z)
    return z * s, s * (1.0 + z * (1.0 - s))


def _softplus(z):
    return jnp.maximum(z, 0.0) + jnp.log(1.0 + jnp.exp(-jnp.abs(z)))


def _shift_rows(cur, before, after, d):
    ts = cur.shape[0]
    row = lax.broadcasted_iota(jnp.int32, (SUBLANES, cur.shape[1]), 0)
    out = pltpu.roll(cur, (-d) % ts, 0)
    if d < 0:
        edge = jnp.where(row < -d, pltpu.roll(before, (-d) % SUBLANES, 0), out[:SUBLANES])
        return jnp.concatenate([edge, out[SUBLANES:]], axis=0)
    edge = jnp.where(row >= SUBLANES - d, pltpu.roll(after, (-d) % SUBLANES, 0), out[ts - SUBLANES:])
    return jnp.concatenate([out[:ts - SUBLANES], edge], axis=0)


def _halo_specs(ts, s, width, col, tile=lambda i: i):
    per = ts // SUBLANES
    last = s // SUBLANES - 1
    return [
        pl.BlockSpec((ts, width), lambda i: (tile(i), col)),
        pl.BlockSpec((SUBLANES, width), lambda i: (jnp.maximum(tile(i) * per - 1, 0), col)),
        pl.BlockSpec((SUBLANES, width), lambda i: (jnp.minimum((tile(i) + 1) * per, last), col)),
    ]


def _halo_load(cur_ref, before_ref, after_ref, n_tiles, tile=lambda i: i):
    i = tile(pl.program_id(0))
    before = jnp.where(i > 0, before_ref[...], 0.0)
    after = jnp.where(i < n_tiles - 1, after_ref[...], 0.0)
    return cur_ref[...], before, after


def _full(shape):
    return pl.BlockSpec(shape, lambda *_: (0,) * len(shape))


def _peer(x, y, c, mask):
    px, py, pc = x ^ (mask >> 2), y ^ ((mask >> 1) & 1), c ^ (mask & 1)
    return (px, py, pc), 4 * px + 2 * py + pc


class Exchange:
    SIBLING = 1
    OTHER_CHIPS = (2, 4, 6)

    def __init__(self):
        self.args, self.out_shape, self._kinds = [], [], []

    def gather(self, block, columns=False, via_sibling=False):
        shape = (block.shape[0], N_DEV * block.shape[1]) if columns else (N_DEV,) + block.shape
        return self._add(block, shape, ("gather", columns, via_sibling))

    def scatter(self, stack, columns=False):
        shape = (N_DEV, stack.shape[0], stack.shape[1] // N_DEV) if columns else stack.shape
        return self._add(stack, shape, ("scatter", columns, False))

    def _add(self, arg, shape, kind):
        self.args.append(arg)
        self.out_shape.append(jax.ShapeDtypeStruct(shape, arg.dtype))
        self._kinds.append(kind)
        return len(self.args) - 1

    def semaphores(self):
        n = len(self.args)
        return [pltpu.SemaphoreType.DMA((n, N_DEV - 1)), pltpu.SemaphoreType.DMA((n, N_DEV - 1)), pltpu.SemaphoreType.DMA((n,))]

    def to_sibling(self, array):
        shape = (N_DEV // 2, array.shape[0], array.shape[1] // N_DEV)
        return self._add(array, shape, ("to_sibling", True, False))

    def among_chips(self, stack):
        return self._add(stack, stack.shape, ("among_chips", False, False))

    def _copies(self, position, in_refs, out_refs):
        x, y, c, me = position
        for arr, ((kind, columns, via_sibling), src, out) in enumerate(zip(self._kinds, in_refs, out_refs)):
            if kind == "to_sibling":
                width = src.shape[-1] // N_DEV
                for k in range(N_DEV // 2):
                    block = src.at[:, pl.ds(pl.multiple_of((2 * k + 1 - c) * width, LANES), width)]
                    yield arr, k + 1, block, out.at[k], out.at[k], False, self.SIBLING
                continue
            for mask in range(N_DEV):
                _, peer_id = _peer(x, y, c, mask)
                relayed = via_sibling and mask not in (0, self.SIBLING) + self.OTHER_CHIPS
                if kind == "among_chips":
                    if mask in (0,) + self.OTHER_CHIPS:
                        yield arr, mask, src.at[peer_id // 2], out.at[me // 2], out.at[peer_id // 2], False, mask
                elif kind == "gather":
                    if columns:
                        width = src.shape[-1]
                        yield (arr, mask, src, out.at[:, pl.ds(pl.multiple_of(me * width, LANES), width)],
                               out.at[:, pl.ds(pl.multiple_of(peer_id * width, LANES), width)], relayed, mask)
                    else:
                        yield arr, mask, src, out.at[me], out.at[peer_id], relayed, mask
                else:
                    if columns:
                        width = src.shape[-1] // N_DEV
                        block = src.at[:, pl.ds(pl.multiple_of(peer_id * width, LANES), width)]
                    else:
                        block = src.at[peer_id]
                    yield arr, mask, block, out.at[me], out.at[peer_id], False, mask

    def _remote(self, position, sems, arr, slot, to_mask, src, dst):
        x, y, c, _ = position
        return pltpu.make_async_remote_copy(src_ref=src, dst_ref=dst, send_sem=sems[0].at[arr, slot - 1], recv_sem=sems[1].at[arr, slot - 1],
                                            device_id=_peer(x, y, c, to_mask)[0], device_id_type=MESH_ID)

    def start(self, position, in_refs, out_refs, sems):
        for arr, slot, src, dst, _, relayed, to_mask in self._copies(position, in_refs, out_refs):
            if slot == 0:
                pltpu.make_async_copy(src, dst, sems[2].at[arr]).start()
            elif not relayed:
                self._remote(position, sems, arr, slot, to_mask, src, dst).start()

    def wait(self, position, in_refs, out_refs, sems):
        copies = list(self._copies(position, in_refs, out_refs))
        landings = {(arr, slot): landing for arr, slot, _, _, landing, _, _ in copies}
        passed_on = set()
        for arr, mask, src, _, landing, relayed, _ in copies:
            if relayed:
                held = landings[arr, mask ^ self.SIBLING]
                self._remote(position, sems, arr, mask ^ self.SIBLING, mask ^ self.SIBLING, src, held).wait_recv()
                self._remote(position, sems, arr, mask, self.SIBLING, held, held).start()
                passed_on.add((arr, mask ^ self.SIBLING))
        for arr, slot, src, dst, landing, relayed, to_mask in copies:
            if slot == 0:
                pltpu.make_async_copy(src, dst, sems[2].at[arr]).wait()
                continue
            if (arr, slot) not in passed_on:
                self._remote(position, sems, arr, slot, to_mask, src, landing).wait_recv()
            if relayed:
                held = landings[arr, slot ^ self.SIBLING]
                self._remote(position, sems, arr, slot, self.SIBLING, held, held).wait_send()
            else:
                self._remote(position, sems, arr, slot, to_mask, src, dst).wait_send()


def _call(body, *, name, grid, in_specs, out_specs, out_shape, args, scratch_shapes=(), exchange=None):
    single = not isinstance(out_shape, (list, tuple))
    if single:
        out_specs, out_shape = [out_specs], [out_shape]
    params = _params(len(grid))
    if exchange is None:
        outs = pl.pallas_call(body, name=name, grid=grid, in_specs=in_specs, out_specs=out_specs, out_shape=out_shape,
                              scratch_shapes=list(scratch_shapes), compiler_params=params)(*args)
        return outs[0] if single else outs
    counts = (len(args), len(exchange.args), len(out_shape), len(exchange.out_shape), len(scratch_shapes), 3)

    def wrapped(*refs):
        groups, at = [], 0
        for n in counts:
            groups.append(refs[at:at + n])
            at += n
        main_in, ex_in, main_out, ex_out, main_scratch, sems = groups
        x, y, c = lax.axis_index("x"), lax.axis_index("y"), lax.axis_index("c")
        position = (x, y, c, 4 * x + 2 * y + c)
        ids = [pl.program_id(a) for a in range(len(grid))]
        first = functools.reduce(jnp.logical_and, [i == 0 for i in ids])
        last = functools.reduce(jnp.logical_and, [i == g - 1 for i, g in zip(ids, grid)])

        @pl.when(first)
        def _():
            exchange.start(position, ex_in, ex_out, sems)

        body(*main_in, *main_out, *main_scratch)

        @pl.when(last)
        def _():
            exchange.wait(position, ex_in, ex_out, sems)

    hbm = pl.BlockSpec(memory_space=pl.ANY)
    outs = pl.pallas_call(
        wrapped, name=name, grid=grid, in_specs=list(in_specs) + [hbm] * counts[1], out_specs=list(out_specs) + [hbm] * counts[3],
        out_shape=list(out_shape) + exchange.out_shape, scratch_shapes=list(scratch_shapes) + exchange.semaphores(),
        compiler_params=params)(*args, *exchange.args)
    main = outs[:counts[2]]
    return (main[0] if single else main), outs[counts[2]:]


def run_exchange(exchange, name):
    return _call(lambda: None, name=name, grid=(1,), in_specs=[], out_specs=[], out_shape=[], args=[], exchange=exchange)[1]


def gather_matmul(x, g, w_block, small_block, tm):
    s, d = x.shape
    width = w_block.shape[1]
    tm = min(tm, s)
    n_i = s // tm
    sibling = Exchange.SIBLING
    y_nbr, x_nbr, diagonal = Exchange.OTHER_CHIPS

    def links(core):
        return (y_nbr, x_nbr) if core == 1 else (x_nbr, y_nbr)

    def block_order(core):
        first, second = links(core)
        return [0, sibling, first, second | sibling, second, first | sibling, diagonal, diagonal | sibling]

    def body(order_ref, x_ref, g_ref, wb_ref, sb_ref, proj_ref, h_ref, w_ref, small_ref, h_all, w_buf, send, recv, local, load_sem):
        j, i = pl.program_id(0), pl.program_id(1)
        xx, yy, cc = lax.axis_index("x"), lax.axis_index("y"), lax.axis_index("c")
        me = 4 * xx + 2 * yy + cc

        def block_of(dev):
            return w_ref.at[:, pl.ds(pl.multiple_of(dev * width, LANES), width)]

        def half_of(dev, part):
            return w_ref.at[pl.ds(part * (d // 2), d // 2), pl.ds(pl.multiple_of(dev * width, LANES), width)]

        def remote(arr, slot, to_mask, src, dst):
            return pltpu.make_async_remote_copy(src_ref=src, dst_ref=dst, send_sem=send.at[arr, slot - 1], recv_sem=recv.at[arr, slot - 1],
                                                device_id=_peer(xx, yy, cc, to_mask)[0], device_id_type=MESH_ID)

        def mine_to(mask):
            return remote(0, mask, mask, wb_ref, block_of(me))

        def arrival(mask):
            return remote(0, mask, mask, wb_ref, block_of(me ^ mask))

        def to_sibling(mask):
            return remote(0, mask | sibling, sibling, block_of(me ^ mask), block_of(me ^ mask))

        def relay(of):
            along_x = of == y_nbr
            part = 0 if along_x else 1
            return remote(0 if along_x else 2, diagonal, x_nbr if along_x else y_nbr, half_of(me ^ of, part), half_of(me ^ of, part))

        def diagonal_half(part):
            return remote(0 if part == 0 else 2, diagonal, x_nbr if part == 0 else y_nbr, wb_ref.at[pl.ds(0, d // 2), :],
                          half_of(me ^ diagonal, part))

        @pl.when((j == 0) & (i == 0))
        def _():
            pltpu.make_async_copy(wb_ref, block_of(me), local.at[0]).start()
            pltpu.make_async_copy(sb_ref, small_ref.at[me], local.at[1]).start()
            mine_to(sibling).start()
            for mask in range(1, N_DEV):
                remote(1, mask, mask, sb_ref, small_ref.at[me]).start()

        def load(step):
            return pltpu.make_async_copy(w_ref.at[:, pl.ds(pl.multiple_of(order_ref[step] * width, LANES), width)],
                                         w_buf.at[step % 2], load_sem.at[step % 2])

        for core in range(2):
            first, second = links(core)
            for step in range(N_DEV):
                at_step = (j == 0) & (i == 0) if step == 0 else (j == step - 1) & (i == n_i - 1)

                @pl.when(at_step & (cc == core))
                def _(step=step, first=first, second=second):
                    if step == 0:
                        mine_to(first).start()
                        pltpu.make_async_copy(wb_ref, block_of(me), local.at[0]).wait()
                    elif step == 1:
                        arrival(sibling).wait_recv()
                    elif step == 2:
                        arrival(first).wait_recv()
                        to_sibling(first).start()
                        mine_to(first).wait_send()
                        mine_to(second).start()
                        relay(first).start()
                    elif step == 3:
                        arrival(second | sibling).wait_recv()
                    elif step == 4:
                        arrival(second).wait_recv()
                        to_sibling(second).start()
                        relay(second).start()
                    elif step == 5:
                        arrival(first | sibling).wait_recv()
                    elif step == 6:
                        diagonal_half(0).wait_recv()
                        diagonal_half(1).wait_recv()
                        to_sibling(diagonal).start()
                    else:
                        arrival(diagonal | sibling).wait_recv()

        @pl.when((j == 0) & (i == 0))
        def _():
            load(0).start()

        @pl.when(i == 0)
        def _():
            load(j).wait()

        @pl.when((i == n_i - 1) & (j < N_DEV - 1))
        def _():
            load(j + 1).start()

        rows = pl.ds(pl.multiple_of(i * tm, tm), tm)

        @pl.when(j == 0)
        def _():
            h = _rms(x_ref[...], g_ref[...]).astype(BF16)
            h_all[rows, :] = h
            h_ref[...] = h

        proj_ref[...] = jnp.dot(h_all[rows, :], w_buf[j % 2], preferred_element_type=F32)

        @pl.when((j == N_DEV - 1) & (i == n_i - 1))
        def _():
            pltpu.make_async_copy(sb_ref, small_ref.at[me], local.at[1]).wait()
            for mask in range(1, N_DEV):
                remote(1, mask, mask, sb_ref, small_ref.at[me ^ mask]).wait_recv()
                remote(1, mask, mask, sb_ref, small_ref.at[me]).wait_send()
            mine_to(sibling).wait_send()
            for core in range(2):
                @pl.when(cc == core)
                def _(core=core):
                    mine_to(links(core)[1]).wait_send()
            for mask in Exchange.OTHER_CHIPS:
                to_sibling(mask).wait_send()
            relay(y_nbr).wait_send()
            relay(x_nbr).wait_send()

    def first_pass_row(j, i, order):
        return jnp.where(j == 0, i, n_i - 1), 0

    hbm = pl.BlockSpec(memory_space=pl.ANY)
    core = lax.axis_index("c")
    me = 4 * lax.axis_index("x") + 2 * lax.axis_index("y") + core
    order = (me ^ jnp.where(core == 1, jnp.array(block_order(1)), jnp.array(block_order(0)))).astype(jnp.int32)
    grid_spec = pltpu.PrefetchScalarGridSpec(
        num_scalar_prefetch=1, grid=(N_DEV, n_i),
        in_specs=[pl.BlockSpec((tm, d), first_pass_row), pl.BlockSpec((1, d), lambda j, i, order: (0, 0)), hbm, hbm],
        out_specs=[pl.BlockSpec((tm, width), lambda j, i, order: (i, order[j])), pl.BlockSpec((tm, d), first_pass_row), hbm, hbm],
        scratch_shapes=[pltpu.VMEM((s, d), BF16), pltpu.VMEM((2, d, width), BF16), pltpu.SemaphoreType.DMA((3, N_DEV - 1)),
                        pltpu.SemaphoreType.DMA((3, N_DEV - 1)), pltpu.SemaphoreType.DMA((2,)), pltpu.SemaphoreType.DMA((2,))])
    return pl.pallas_call(
        body, name="even_in", grid_spec=grid_spec,
        out_shape=[jax.ShapeDtypeStruct((s, N_DEV * width), F32), jax.ShapeDtypeStruct((s, d), BF16),
                   jax.ShapeDtypeStruct((d, N_DEV * width), w_block.dtype), jax.ShapeDtypeStruct((N_DEV,) + small_block.shape, small_block.dtype)],
        compiler_params=_params(2),
    )(order, x, g, w_block, small_block)


def rms_matmul(x, g, w, tm, tn, name, exchange=None):
    s, d = x.shape
    n = w.shape[1]
    tm = min(tm, s)

    def body(x_ref, g_ref, w_ref, o_ref, h_ref):
        @pl.when(pl.program_id(1) == 0)
        def _():
            h_ref[...] = _rms(x_ref[...], g_ref[...]).astype(BF16)

        o_ref[...] = jnp.dot(h_ref[...], w_ref[...], preferred_element_type=F32)

    return _call(
        body, name=name, grid=(s // tm, n // tn),
        in_specs=[pl.BlockSpec((tm, d), lambda i, j: (i, 0)), _full((1, d)), pl.BlockSpec((d, tn), lambda i, j: (0, j))],
        out_specs=[pl.BlockSpec((tm, tn), lambda i, j: (i, j)), pl.BlockSpec((tm, d), lambda i, j: (i, 0))],
        out_shape=[jax.ShapeDtypeStruct((s, n), F32), jax.ShapeDtypeStruct((s, d), BF16)],
        args=[x, g, w], exchange=exchange)


def _gla_out_bwd(du, r, osum, gn, do_ref, dr_ref, dgn_ref):
    silu_r, dsilu_r = _silu_and_grad(r)
    for head in range(GLA_HEADS):
        vl = slice(head * GLA_DV, (head + 1) * GLA_DV)
        o_h, g_h, du_h = osum[:, vl], gn[:, vl], du[:, vl]
        dr_ref[:, vl] = (du_h * _rms(o_h, g_h) * dsilu_r[:, vl]).astype(BF16)
        do_h, dg_h = _rms_bwd(o_h, g_h, du_h * silu_r[:, vl])
        do_ref[:, vl] = do_h
        dgn_ref[...] += dg_h


def normbwd_matmul_nt(y, g, dout, w, tn, name, exchange=None, gla=None):
    s, d = y.shape
    n = w.shape[0]
    tm = min(MM_TILE, s)

    def body(*refs):
        if gla is None:
            y_ref, g_ref, dout_ref, w_ref, du_ref, dy_ref, dg_ref = refs
        else:
            y_ref, g_ref, dout_ref, w_ref, r_ref, o_ref, gn_ref, do_ref, dr_ref, dy_ref, dg_ref, dgn_ref = refs
        i, j = pl.program_id(0), pl.program_id(1)

        @pl.when(j == 0)
        def _():
            dy, dg = _rms_bwd(y_ref[...], g_ref[...], dout_ref[...])
            dy_ref[...] = dy.astype(BF16)

            @pl.when(i == 0)
            def _():
                dg_ref[...] = jnp.zeros_like(dg_ref)
                if gla is not None:
                    dgn_ref[...] = jnp.zeros_like(dgn_ref)

            dg_ref[...] += dg

        du = lax.dot_general(dy_ref[...], w_ref[...], (((1,), (1,)), ((), ())), preferred_element_type=F32)
        if gla is None:
            du_ref[...] = du
        else:
            _gla_out_bwd(du, r_ref[...], o_ref[...], gn_ref[...], do_ref, dr_ref, dgn_ref)

    row = pl.BlockSpec((tm, d), lambda i, j: (i, 0))
    in_specs = [row, _full((1, d)), row, pl.BlockSpec((tn, d), lambda i, j: (j, 0))]
    args = [y, g, dout, w]
    tail_specs = [row, _full((1, d))]
    tail_shapes = [jax.ShapeDtypeStruct((s, d), BF16), jax.ShapeDtypeStruct((1, d), F32)]
    if gla is None:
        out_specs = [pl.BlockSpec((tm, tn), lambda i, j: (i, j))] + tail_specs
        out_shape = [jax.ShapeDtypeStruct((s, n), F32)] + tail_shapes
    else:
        proj, osum, gnorm = gla
        assert n == tn == D_MODEL
        in_specs += [pl.BlockSpec((tm, D_MODEL), lambda i, j: (i, 2)), row, _full(gnorm.shape)]
        args += [proj, osum, gnorm]
        out_specs = [row, row] + tail_specs + [_full((1, GLA_DV))]
        out_shape = [jax.ShapeDtypeStruct((s, D_MODEL), F32), jax.ShapeDtypeStruct((s, D_MODEL), BF16)] + tail_shapes + [
            jax.ShapeDtypeStruct((1, GLA_DV), F32)]
    return _call(body, name=name, grid=(s // tm, n // tn), in_specs=in_specs, out_specs=out_specs, out_shape=out_shape,
                 args=args, exchange=exchange)


def matmul_tn(a, b, tm, tn, ts, out_dtype, name, exchange=None, b_first=None):
    s, m = a.shape
    n = b.shape[1] + (0 if b_first is None else tn)
    ts = min(ts, s)
    n_k = s // ts
    dims = (((0,), (0,)), ((), ()))

    def body(*refs):
        if b_first is None:
            a_ref, b_ref, o_ref, acc = refs
        else:
            a_ref, first_ref, b_ref, o_ref, acc = refs
        j, k = pl.program_id(1), pl.program_id(2)

        @pl.when(k == 0)
        def _():
            acc[...] = jnp.zeros_like(acc)

        if b_first is None:
            acc[...] += lax.dot_general(a_ref[...], b_ref[...], dims, preferred_element_type=F32)
        else:
            @pl.when(j == 0)
            def _():
                acc[...] += lax.dot_general(a_ref[...], first_ref[...], dims, preferred_element_type=F32)

            @pl.when(j > 0)
            def _():
                acc[...] += lax.dot_general(a_ref[...], b_ref[...], dims, preferred_element_type=F32)

        @pl.when(k == n_k - 1)
        def _():
            o_ref[...] = acc[...].astype(out_dtype)

    if b_first is None:
        b_specs, b_args = [pl.BlockSpec((ts, tn), lambda i, j, k: (k, j))], [b]
    else:
        b_specs = [pl.BlockSpec((ts, tn), lambda i, j, k: (k, 0)), pl.BlockSpec((ts, tn), lambda i, j, k: (k, jnp.maximum(j - 1, 0)))]
        b_args = [b_first, b]
    return _call(
        body, name=name, grid=(m // tm, n // tn, n_k),
        in_specs=[pl.BlockSpec((ts, tm), lambda i, j, k: (k, i))] + b_specs,
        out_specs=pl.BlockSpec((tm, tn), lambda i, j, k: (i, j)),
        out_shape=jax.ShapeDtypeStruct((m, n), out_dtype),
        scratch_shapes=[pltpu.VMEM((tm, tn), F32)], args=[a] + b_args, exchange=exchange)


def matmul_nt_normbwd(dproj, w, x, g, dres, tm, tk, name, exchange=None, first=None):
    s, kt = dproj.shape
    kt += 0 if first is None else tk
    d = w.shape[0]
    tm = min(tm, s)
    n_k = kt // tk
    dims = (((1,), (1,)), ((), ()))

    def body(*refs):
        if first is None:
            a_ref, w_ref, x_ref, g_ref, r_ref, dx_ref, dg_ref, acc = refs
        else:
            first_ref, a_ref, w_ref, x_ref, g_ref, r_ref, dx_ref, dg_ref, acc = refs
        i, k = pl.program_id(0), pl.program_id(1)

        @pl.when(k == 0)
        def _():
            acc[...] = jnp.zeros_like(acc)

        if first is None:
            acc[...] += lax.dot_general(a_ref[...], w_ref[...], dims, preferred_element_type=F32)
        else:
            @pl.when(k == 0)
            def _():
                acc[...] += lax.dot_general(first_ref[...], w_ref[...], dims, preferred_element_type=F32)

            @pl.when(k > 0)
            def _():
                acc[...] += lax.dot_general(a_ref[...], w_ref[...], dims, preferred_element_type=F32)

        @pl.when(k == n_k - 1)
        def _():
            dx, dg = _rms_bwd(x_ref[...], g_ref[...], acc[...])
            dx_ref[...] = r_ref[...] + dx

            @pl.when(i == 0)
            def _():
                dg_ref[...] = jnp.zeros_like(dg_ref)

            dg_ref[...] += dg

    row = pl.BlockSpec((tm, d), lambda i, k: (i, 0))
    if first is None:
        a_specs, a_args = [pl.BlockSpec((tm, tk), lambda i, k: (i, k))], [dproj]
    else:
        a_specs = [pl.BlockSpec((tm, tk), lambda i, k: (i, 0)), pl.BlockSpec((tm, tk), lambda i, k: (i, jnp.maximum(k - 1, 0)))]
        a_args = [first, dproj]
    return _call(
        body, name=name, grid=(s // tm, n_k),
        in_specs=a_specs + [pl.BlockSpec((d, tk), lambda i, k: (0, k)), row, _full((1, d)), row],
        out_specs=[row, _full((1, d))],
        out_shape=[jax.ShapeDtypeStruct((s, d), F32), jax.ShapeDtypeStruct((1, d), F32)],
        scratch_shapes=[pltpu.VMEM((tm, d), F32)], args=a_args + [w, x, g, dres], exchange=exchange)


def _rg_conv(xa, before, after, cw, cb):
    return (cw[0:1, :] * _shift_rows(xa, before, after, -2) + cw[1:2, :] * _shift_rows(xa, before, after, -1)
            + cw[2:3, :] * xa + cw[3:4, :] * _shift_rows(xa, before, after, 1) + cb)


def _rg_gates(ua_h, gw_ref, gb_ref, c_h, direction, head):
    r = _sigmoid(_bdot(ua_h, gw_ref[2 * direction, head]) + gb_ref[2 * direction, head:head + 1, :])
    i = _sigmoid(_bdot(ua_h, gw_ref[2 * direction + 1, head]) + gb_ref[2 * direction + 1, head:head + 1, :])
    log_a = -c_h * r
    a = jnp.exp(log_a)
    beta_sq = -jnp.tanh(log_a) * (1.0 + a * a)
    inv_beta = lax.rsqrt(jnp.maximum(beta_sq, SMALLEST_NORMAL))
    return r, i, a, beta_sq * inv_beta, inv_beta


def even_gates_fwd(proj, conv_w, conv_b, gate_w, gate_b, lam, exchange=None):
    s = proj.shape[0]
    ts = min(2 * ROW_TILE, s)
    n_tiles = s // ts

    def body(xa_ref, xb_ref, xn_ref, cw_ref, cb_ref, gw_ref, gb_ref, lam_ref, o_ref, hf_ref, carry):
        @pl.when(pl.program_id(0) == 0)
        def _():
            carry[...] = jnp.zeros_like(carry)

        xa, before, after = _halo_load(xa_ref, xb_ref, xn_ref, n_tiles)
        ua = _rg_conv(xa, before, after, cw_ref[...], cb_ref[...])
        c = RG_C * _softplus(-lam_ref[...])
        ua_bf16 = ua.astype(BF16)
        for direction in range(2):
            for head in range(RG_HEADS):
                lanes = slice(head * RG_HEAD_DIM, (head + 1) * RG_HEAD_DIM)
                ua_h = ua[:, lanes]
                _, i, a, beta, _ = _rg_gates(ua_bf16[:, lanes], gw_ref, gb_ref, c[direction:direction + 1, lanes], direction, head)
                o_ref[2 * direction, :, lanes] = a
                o_ref[2 * direction + 1, :, lanes] = beta * (i * ua_h)
        _scan_tile(o_ref.at[0], o_ref.at[1], hf_ref, carry, False, False)

    return _call(
        body, name="even_gates_fwd", grid=(n_tiles,),
        in_specs=_halo_specs(ts, s, D_MODEL, 0) + [_full(conv_w.shape), _full(conv_b.shape), _full(gate_w.shape),
                                                   _full(gate_b.shape), _full(lam.shape)],
        out_specs=[pl.BlockSpec((4, ts, D_MODEL), lambda i: (0, i, 0)), pl.BlockSpec((ts, D_MODEL), lambda i: (i, 0))],
        out_shape=[jax.ShapeDtypeStruct((4, s, D_MODEL), F32), jax.ShapeDtypeStruct((s, D_MODEL), F32)],
        scratch_shapes=[pltpu.VMEM((SUBLANES, D_MODEL), F32)],
        args=[proj, proj, proj, conv_w, conv_b, gate_w, gate_b, lam], exchange=exchange)


def _scan_tile(a_ref, b_ref, h_ref, carry, reverse, b_times_a):
    ts, c = h_ref.shape
    n_blocks = ts // SUBLANES
    row = lax.broadcasted_iota(jnp.int32, (SUBLANES, c), 0)

    def block(j, h_in):
        r0 = pl.multiple_of((n_blocks - 1 - j if reverse else j) * SUBLANES, SUBLANES)
        a = a_ref[pl.ds(r0, SUBLANES), :]
        b = b_ref[pl.ds(r0, SUBLANES), :]
        if b_times_a:
            b = a * b
        for step in (1, 2, 4):
            shift = SUBLANES - step if reverse else step
            valid = row < SUBLANES - step if reverse else row >= step
            b = jnp.where(valid, a * pltpu.roll(b, shift, 0) + b, b)
            a = jnp.where(valid, a * pltpu.roll(a, shift, 0), a)
        h = a * h_in + b
        h_ref[pl.ds(r0, SUBLANES), :] = h
        return h[0:1, :] if reverse else h[SUBLANES - 1:SUBLANES, :]

    carry[0:1, :] = lax.fori_loop(0, n_blocks, block, carry[0:1, :])


def linear_scan(a_arr, a_idx, b_arr, b_idx, reverse, b_times_a, name, exchange=None):
    _, s, c = a_arr.shape
    ts = min(MM_TILE, s)
    n_tiles = s // ts

    def tile_of(i):
        return n_tiles - 1 - i if reverse else i

    def body(a_ref, b_ref, h_ref, carry):
        @pl.when(pl.program_id(0) == 0)
        def _():
            carry[...] = jnp.zeros_like(carry)

        _scan_tile(a_ref, b_ref, h_ref, carry, reverse, b_times_a)

    return _call(
        body, name=name, grid=(n_tiles,),
        in_specs=[pl.BlockSpec((None, ts, c), lambda i: (a_idx, tile_of(i), 0)),
                  pl.BlockSpec((None, ts, c), lambda i: (b_idx, tile_of(i), 0))],
        out_specs=pl.BlockSpec((ts, c), lambda i: (tile_of(i), 0)),
        out_shape=jax.ShapeDtypeStruct((s, c), F32),
        scratch_shapes=[pltpu.VMEM((SUBLANES, c), F32)], args=[a_arr, b_arr], exchange=exchange)


def _sc_conv(p, before, after, w):
    return w[0:1, :] * _shift_rows(p, before, after, -1) + w[1:2, :] * p + w[2:3, :] * _shift_rows(p, before, after, 1)


def even_mix_fwd(ab, hf, proj, sc_w, w_out, xres, g_post, exchange=None):
    s = proj.shape[0]
    ts = min(ROW_TILE, s)
    n_tiles = s // ts

    def tile(i):
        return n_tiles - 1 - i

    row = pl.BlockSpec((ts, D_MODEL), lambda i: (tile(i), 0))

    def col(c):
        return pl.BlockSpec((ts, D_MODEL), lambda i: (tile(i), c))

    def body(a_ref, b_ref, hf_ref, za_ref, xb_ref, xbb_ref, xbn_ref, gb_ref, gc_ref, gcb_ref, gcn_ref, zb_ref, w_ref,
             wo_ref, x_ref, g_ref, u_ref, hb_ref, y_ref, out_ref, carry):
        @pl.when(pl.program_id(0) == 0)
        def _():
            carry[...] = jnp.zeros_like(carry)

        _scan_tile(a_ref, b_ref, hb_ref, carry, True, False)
        xb, xb_before, xb_after = _halo_load(xb_ref, xbb_ref, xbn_ref, n_tiles, tile)
        gc, gc_before, gc_after = _halo_load(gc_ref, gcb_ref, gcn_ref, n_tiles, tile)
        silu_za, _ = _silu_and_grad(za_ref[...])
        silu_zb, _ = _silu_and_grad(zb_ref[...])
        u_ref[:, :D_MODEL] = ((hf_ref[...] + hb_ref[...]) * silu_za).astype(BF16)
        cv = _sc_conv(gc * xb, gc_before * xb_before, gc_after * xb_after, w_ref[...])
        u_ref[:, D_MODEL:] = (gb_ref[...] * cv * silu_zb).astype(BF16)
        y = jnp.dot(u_ref[...], wo_ref[...], preferred_element_type=F32)
        y_ref[...] = y
        out_ref[...] = x_ref[...] + _rms(y, g_ref[...])

    return _call(
        body, name="even_mix_fwd", grid=(n_tiles,),
        in_specs=[pl.BlockSpec((None, ts, D_MODEL), lambda i: (2, tile(i), 0)), pl.BlockSpec((None, ts, D_MODEL), lambda i: (3, tile(i), 0)),
                  row, col(1)] + _halo_specs(ts, s, D_MODEL, 2, tile) + [col(3)] + _halo_specs(ts, s, D_MODEL, 4, tile)
        + [col(5), _full(sc_w.shape), _full(w_out.shape), row, _full(g_post.shape)],
        out_specs=[pl.BlockSpec((ts, 2 * D_MODEL), lambda i: (tile(i), 0)), row, row, row],
        out_shape=[jax.ShapeDtypeStruct((s, 2 * D_MODEL), BF16)] + [jax.ShapeDtypeStruct((s, D_MODEL), F32)] * 3,
        scratch_shapes=[pltpu.VMEM((SUBLANES, D_MODEL), F32)],
        args=[ab, ab, hf, proj, proj, proj, proj, proj, proj, proj, proj, proj, sc_w, w_out, xres, g_post], exchange=exchange)


def even_mix_bwd(du, hf, hb, proj, sc_w, ab, exchange=None):
    s = proj.shape[0]
    ts = min(ROW_TILE, s)
    n_tiles = s // ts
    row = pl.BlockSpec((ts, D_MODEL), lambda i: (i, 0))

    def body(dya_ref, dyb_ref, dybb_ref, dybn_ref, hf_ref, hb_ref, za_ref, xb_ref, xbb_ref, xbn_ref,
             gb_ref, gbb_ref, gbn_ref, gc_ref, gcb_ref, gcn_ref, zb_ref, zbb_ref, zbn_ref, w_ref, a_ref,
             dh_ref, dp_ref, dw_ref, adj_ref, carry):
        @pl.when(pl.program_id(0) == 0)
        def _():
            carry[...] = jnp.zeros_like(carry)

        dyb, dyb_before, dyb_after = _halo_load(dyb_ref, dybb_ref, dybn_ref, n_tiles)
        xb, xb_before, xb_after = _halo_load(xb_ref, xbb_ref, xbn_ref, n_tiles)
        gb, gb_before, gb_after = _halo_load(gb_ref, gbb_ref, gbn_ref, n_tiles)
        gc, gc_before, gc_after = _halo_load(gc_ref, gcb_ref, gcn_ref, n_tiles)
        zb, zb_before, zb_after = _halo_load(zb_ref, zbb_ref, zbn_ref, n_tiles)
        w = w_ref[...]
        dya, za = dya_ref[...], za_ref[...]
        silu_za, dsilu_za = _silu_and_grad(za)
        dh_ref[...] = dya * silu_za
        _scan_tile(a_ref, dh_ref, adj_ref, carry, False, True)
        dp_ref[:, 0:D_MODEL] = (dya * (hf_ref[...] + hb_ref[...]) * dsilu_za).astype(BF16)

        silu_zb, dsilu_zb = _silu_and_grad(zb)
        p, p_before, p_after = gc * xb, gc_before * xb_before, gc_after * xb_after
        cv = _sc_conv(p, p_before, p_after, w)
        dcv = dyb * gb * silu_zb
        dcv_before = dyb_before * gb_before * _silu_and_grad(zb_before)[0]
        dcv_after = dyb_after * gb_after * _silu_and_grad(zb_after)[0]
        dpp = (w[0:1, :] * _shift_rows(dcv, dcv_before, dcv_after, 1) + w[1:2, :] * dcv
               + w[2:3, :] * _shift_rows(dcv, dcv_before, dcv_after, -1))
        dp_ref[:, D_MODEL:2 * D_MODEL] = (dpp * gc).astype(BF16)
        dp_ref[:, 2 * D_MODEL:3 * D_MODEL] = (dyb * cv * silu_zb).astype(BF16)
        dp_ref[:, 3 * D_MODEL:4 * D_MODEL] = (dpp * xb).astype(BF16)
        dp_ref[:, 4 * D_MODEL:5 * D_MODEL] = (dyb * gb * cv * dsilu_zb).astype(BF16)

        @pl.when(pl.program_id(0) == 0)
        def _():
            dw_ref[...] = jnp.zeros_like(dw_ref)

        dw_ref[0:1, :] += jnp.sum(dcv * _shift_rows(p, p_before, p_after, -1), axis=0, keepdims=True)
        dw_ref[1:2, :] += jnp.sum(dcv * p, axis=0, keepdims=True)
        dw_ref[2:3, :] += jnp.sum(dcv * _shift_rows(p, p_before, p_after, 1), axis=0, keepdims=True)

    return _call(
        body, name="even_mix_bwd", grid=(n_tiles,),
        in_specs=[row] + _halo_specs(ts, s, D_MODEL, 1) + [row, row, pl.BlockSpec((ts, D_MODEL), lambda i: (i, 1))]
        + _halo_specs(ts, s, D_MODEL, 2) + _halo_specs(ts, s, D_MODEL, 3) + _halo_specs(ts, s, D_MODEL, 4)
        + _halo_specs(ts, s, D_MODEL, 5) + [_full(sc_w.shape), pl.BlockSpec((None, ts, D_MODEL), lambda i: (2, i, 0))],
        out_specs=[row, pl.BlockSpec((ts, 5 * D_MODEL), lambda i: (i, 0)), _full(sc_w.shape), row],
        out_shape=[jax.ShapeDtypeStruct((s, D_MODEL), F32), jax.ShapeDtypeStruct((s, 5 * D_MODEL), BF16),
                   jax.ShapeDtypeStruct(sc_w.shape, F32), jax.ShapeDtypeStruct((s, D_MODEL), F32)],
        scratch_shapes=[pltpu.VMEM((SUBLANES, D_MODEL), F32)],
        args=[du, du, du, du, hf, hb, proj, *([proj] * 12), sc_w, ab], exchange=exchange)


def even_gates_bwd(proj, adj_f, adj_b, hf, hb, dh, conv_w, conv_b, gate_w, gate_b, lam, exchange=None):
    s = proj.shape[0]
    ts = min(2 * ROW_TILE, s)
    n_tiles = s // ts
    row = pl.BlockSpec((ts, D_MODEL), lambda i: (i, 0))

    def body(xa_ref, xab_ref, xan_ref, af_ref, afb_ref, afn_ref, ab_ref, abb_ref, abn_ref,
             hf_ref, hfb_ref, hfn_ref, hb_ref, hbb_ref, hbn_ref, dh_ref,
             cw_ref, cb_ref, gw_ref, gb_ref, lam_ref, dua_ref, dgw_ref, dgb_ref, dlam_ref):
        @pl.when(pl.program_id(0) == 0)
        def _():
            dgw_ref[...] = jnp.zeros_like(dgw_ref)
            dgb_ref[...] = jnp.zeros_like(dgb_ref)
            dlam_ref[...] = jnp.zeros_like(dlam_ref)

        xa, before, after = _halo_load(xa_ref, xab_ref, xan_ref, n_tiles)
        ua = _rg_conv(xa, before, after, cw_ref[...], cb_ref[...])
        lam_v = lam_ref[...]
        c = RG_C * _softplus(-lam_v)
        dc_dlam = -RG_C * _sigmoid(-lam_v)
        dh = dh_ref[...]
        adj = (_halo_load(af_ref, afb_ref, afn_ref, n_tiles), _halo_load(ab_ref, abb_ref, abn_ref, n_tiles))
        hs = (_halo_load(hf_ref, hfb_ref, hfn_ref, n_tiles), _halo_load(hb_ref, hbb_ref, hbn_ref, n_tiles))
        dua = jnp.zeros_like(ua)
        ua_bf16 = ua.astype(BF16)
        for direction in range(2):
            step = 1 if direction == 0 else -1
            g = dh + _shift_rows(*adj[direction], step)
            da_all = g * _shift_rows(*hs[direction], -step)
            dua_parts = []
            for head in range(RG_HEADS):
                lanes = slice(head * RG_HEAD_DIM, (head + 1) * RG_HEAD_DIM)
                ua_h = ua[:, lanes]
                c_h = c[direction:direction + 1, lanes]
                ua_hb = ua_bf16[:, lanes]
                r, i, a, beta, inv_beta = _rg_gates(ua_hb, gw_ref, gb_ref, c_h, direction, head)
                db_beta = g[:, lanes] * beta
                d_i = db_beta * ua_h
                dbeta = g[:, lanes] * (i * ua_h)
                dlog_a = (da_all[:, lanes] - dbeta * a * inv_beta) * a
                dpr = -c_h * dlog_a * r * (1.0 - r)
                dpi = d_i * i * (1.0 - i)
                dpr_b, dpi_b = dpr.astype(BF16), dpi.astype(BF16)
                dua_parts.append(db_beta * i + _bdot_nt(dpr_b, gw_ref[2 * direction, head])
                                 + _bdot_nt(dpi_b, gw_ref[2 * direction + 1, head]))
                dgw_ref[2 * direction, head] += _bdot_tn(ua_hb, dpr_b)
                dgw_ref[2 * direction + 1, head] += _bdot_tn(ua_hb, dpi_b)
                dgb_ref[2 * direction, head:head + 1, :] += jnp.sum(dpr, axis=0, keepdims=True)
                dgb_ref[2 * direction + 1, head:head + 1, :] += jnp.sum(dpi, axis=0, keepdims=True)
                dlam_ref[direction:direction + 1, lanes] += (
                    jnp.sum(-r * dlog_a, axis=0, keepdims=True) * dc_dlam[direction:direction + 1, lanes])
            dua = dua + jnp.concatenate(dua_parts, axis=1)
        dua_ref[...] = dua

    return _call(
        body, name="even_gates_bwd", grid=(n_tiles,),
        in_specs=_halo_specs(ts, s, D_MODEL, 0) * 5 + [row] + [_full(conv_w.shape), _full(conv_b.shape), _full(gate_w.shape),
                                                             _full(gate_b.shape), _full(lam.shape)],
        out_specs=[row, _full(gate_w.shape), _full(gate_b.shape), _full(lam.shape)],
        out_shape=[jax.ShapeDtypeStruct((s, D_MODEL), F32), jax.ShapeDtypeStruct(gate_w.shape, F32),
                   jax.ShapeDtypeStruct(gate_b.shape, F32), jax.ShapeDtypeStruct(lam.shape, F32)],
        args=[proj, proj, proj, adj_f, adj_f, adj_f, adj_b, adj_b, adj_b, hf, hf, hf, hb, hb, hb, dh, conv_w, conv_b, gate_w,
              gate_b, lam], exchange=exchange)


def rg_conv_bwd(dua, proj, conv_w, exchange=None):
    s = proj.shape[0]
    ts = min(2 * ROW_TILE, s)
    n_tiles = s // ts

    def body(du_ref, dub_ref, dun_ref, xa_ref, xab_ref, xan_ref, cw_ref, dp_ref, dw_ref, db_ref):
        @pl.when(pl.program_id(0) == 0)
        def _():
            dw_ref[...] = jnp.zeros_like(dw_ref)
            db_ref[...] = jnp.zeros_like(db_ref)

        dua, dua_before, dua_after = _halo_load(du_ref, dub_ref, dun_ref, n_tiles)
        xa, xa_before, xa_after = _halo_load(xa_ref, xab_ref, xan_ref, n_tiles)
        cw = cw_ref[...]
        dxa = (cw[0:1, :] * _shift_rows(dua, dua_before, dua_after, 2) + cw[1:2, :] * _shift_rows(dua, dua_before, dua_after, 1)
               + cw[2:3, :] * dua + cw[3:4, :] * _shift_rows(dua, dua_before, dua_after, -1))
        dp_ref[...] = dxa.astype(BF16)
        for tap, offset in enumerate((-2, -1, 0, 1)):
            shifted = xa if offset == 0 else _shift_rows(xa, xa_before, xa_after, offset)
            dw_ref[tap:tap + 1, :] += jnp.sum(dua * shifted, axis=0, keepdims=True)
        db_ref[...] += jnp.sum(dua, axis=0, keepdims=True)

    return _call(
        body, name="rg_conv_bwd", grid=(n_tiles,),
        in_specs=_halo_specs(ts, s, D_MODEL, 0) * 2 + [_full(conv_w.shape)],
        out_specs=[pl.BlockSpec((ts, D_MODEL), lambda i: (i, 0)), _full(conv_w.shape), _full((1, D_MODEL))],
        out_shape=[jax.ShapeDtypeStruct((s, D_MODEL), BF16), jax.ShapeDtypeStruct(conv_w.shape, F32),
                   jax.ShapeDtypeStruct((1, D_MODEL), F32)],
        args=[dua, dua, dua, proj, proj, proj, conv_w], exchange=exchange)


def _split3(x):
    x1 = x.astype(BF16)
    rest = x - x1.astype(F32)
    x2 = rest.astype(BF16)
    return x1, x2, (rest - x2.astype(F32)).astype(BF16)


def _chunk_sum_matrix(t, reverse, transpose):
    i = lax.broadcasted_iota(jnp.int32, (t, t), 0)
    j = lax.broadcasted_iota(jnp.int32, (t, t), 1)
    if transpose:
        i, j = j, i
    same = (i // GLA_CHUNK) == (j // GLA_CHUNK)
    return jnp.where(same & ((j >= i) if reverse else (j <= i)), 1.0, 0.0).astype(BF16)


def _exact_dot(m, x):
    return sum(jnp.dot(m, part, preferred_element_type=F32) for part in _split3(x))


def _chunk_mask(t, reverse):
    i = lax.broadcasted_iota(jnp.int32, (t, t), 0)
    j = lax.broadcasted_iota(jnp.int32, (t, t), 1)
    return ((i // GLA_CHUNK) == (j // GLA_CHUNK)) & ((j >= i) if reverse else (j <= i))


def _chunk_rows(c):
    return slice(c * GLA_CHUNK, (c + 1) * GLA_CHUNK)


def _gla_gate(lr, wg, bg):
    z = _bdot(lr, wg) + bg
    log_alpha = (jnp.minimum(z, 0.0) - jnp.log(1.0 + jnp.exp(-jnp.abs(z)))) * (1.0 / GLA_NORMALIZER)
    return z, log_alpha


def _gla_tile_terms(q, k, bcum, reverse):
    n_chunks = q.shape[0] // GLA_CHUNK
    totals = []
    for c in range(n_chunks):
        edge = c * GLA_CHUNK if reverse else (c + 1) * GLA_CHUNK - 1
        totals.append(bcum[edge:edge + 1, :])
    btot = jnp.concatenate([jnp.broadcast_to(total, (GLA_CHUNK, total.shape[1])) for total in totals], axis=0)
    e_pos, e_neg, e_st = jnp.exp(bcum), jnp.exp(-bcum), jnp.exp(btot - bcum)
    return q * (GLA_DK ** -0.5) * e_pos, k * e_neg, k * e_st, e_pos, e_neg, e_st, [jnp.exp(total) for total in totals]


def _gla_specs(t, n_tiles, reverse_order):
    def tile(i):
        return n_tiles - 1 - i if reverse_order else i

    return tile, [
        pl.BlockSpec((t, GLA_KEY), lambda i: (tile(i), 0)),
        pl.BlockSpec((t, GLA_KEY), lambda i: (tile(i), 1)),
        pl.BlockSpec((t, D_MODEL), lambda i: (tile(i), 1)),
        pl.BlockSpec((t, LANES), lambda i: (tile(i), (ODD_IN_PAD - LANES) // LANES)),
    ]


def gla_fwd(proj, wg, bg, reverse, o_other=None, gnorm=None, post=None):
    s = proj.shape[0]
    t = min(GLA_TILE, s)
    n_tiles = s // t
    n_chunks = t // GLA_CHUNK
    final = o_other is not None
    tile, specs = _gla_specs(t, n_tiles, reverse)

    def body(*refs):
        if final:
            (q_ref, k_ref, v_ref, lr_ref, wg_ref, bg_ref, oo_ref, r_ref, gn_ref, wo_ref, x_ref, gp_ref, t_ref,
             osum_ref, u_ref, st_ref, y_ref, dout_ref, loss_ref, state) = refs
        else:
            q_ref, k_ref, v_ref, lr_ref, wg_ref, bg_ref, o_ref, st_ref, state = refs
            osum_ref = o_ref

        @pl.when(pl.program_id(0) == 0)
        def _():
            state[...] = jnp.zeros_like(state)

        _, log_alpha = _gla_gate(lr_ref[...], wg_ref[...], bg_ref[...])
        bcum = _exact_dot(_chunk_sum_matrix(t, reverse, False), log_alpha)
        q, k, v = q_ref[...], k_ref[...], v_ref[...]
        q_in, k_in, k_st, _, _, _, decays = _gla_tile_terms(q, k, bcum, reverse)
        mask = _chunk_mask(t, reverse)
        order = list(range(n_chunks))[::-1] if reverse else list(range(n_chunks))
        intra, increments = [], []
        for head in range(GLA_HEADS):
            kl = slice(head * GLA_DK, (head + 1) * GLA_DK)
            vl = slice(head * GLA_DV, (head + 1) * GLA_DV)
            scores = jnp.where(mask, _bdot_nt(q_in[:, kl], k_in[:, kl]), 0.0)
            intra.append(_bdot(scores, v[:, vl]))
            increments.append([_bdot_tn(v[_chunk_rows(c), vl], k_st[_chunk_rows(c), kl]) for c in range(n_chunks)])
        for head in range(GLA_HEADS):
            kl = slice(head * GLA_DK, (head + 1) * GLA_DK)
            vl = slice(head * GLA_DV, (head + 1) * GLA_DV)
            running = state[head]
            before = [None] * n_chunks
            for c in order:
                before[c] = running
                st_ref[c, head] = running
                running = running * decays[c][:, kl] + increments[head][c]
            state[head] = running
            inter = [_bdot_nt(q_in[_chunk_rows(c), kl], before[c]) for c in range(n_chunks)]
            osum_ref[:, vl] = intra[head] + jnp.concatenate(inter, axis=0)
        if final:
            osum = osum_ref[...] + oo_ref[...]
            osum_ref[...] = osum
            silu_r, _ = _silu_and_grad(r_ref[...])
            gn = gn_ref[...]
            for head in range(GLA_HEADS):
                vl = slice(head * GLA_DV, (head + 1) * GLA_DV)
                u_ref[:, vl] = (_rms(osum[:, vl], gn[:, vl]) * silu_r[:, vl]).astype(BF16)

            @pl.when(pl.program_id(0) == 0)
            def _():
                loss_ref[...] = jnp.zeros_like(loss_ref)

            y = jnp.dot(u_ref[...], wo_ref[...], preferred_element_type=F32)
            y_ref[...] = y
            diff = x_ref[...] + _rms(y, gp_ref[...]) - t_ref[...]
            dout_ref[...] = diff * (1.0 / D_MODEL)
            loss_ref[...] += 0.5 * jnp.sum(jnp.mean(diff * diff, axis=-1, keepdims=True))

    row = pl.BlockSpec((t, D_MODEL), lambda i: (tile(i), 0))
    st_spec = pl.BlockSpec((n_chunks, GLA_HEADS, GLA_DV, GLA_DK), lambda i: (tile(i), 0, 0, 0))
    st_shape = jax.ShapeDtypeStruct((s // GLA_CHUNK, GLA_HEADS, GLA_DV, GLA_DK), F32)
    in_specs = specs + [_full(wg.shape), _full(bg.shape)]
    args = [proj, proj, proj, proj, wg, bg]
    if final:
        w_out, xres, g_post, target = post
        in_specs += [row, pl.BlockSpec((t, D_MODEL), lambda i: (tile(i), 2)), _full(gnorm.shape), _full(w_out.shape), row,
                     _full(g_post.shape), row]
        args += [o_other, proj, gnorm, w_out, xres, g_post, target]
        out_specs = [row, row, st_spec, row, row, _full((SUBLANES, LANES))]
        out_shape = [jax.ShapeDtypeStruct((s, D_MODEL), F32), jax.ShapeDtypeStruct((s, D_MODEL), BF16), st_shape,
                     jax.ShapeDtypeStruct((s, D_MODEL), F32), jax.ShapeDtypeStruct((s, D_MODEL), F32),
                     jax.ShapeDtypeStruct((SUBLANES, LANES), F32)]
    else:
        out_specs = [row, st_spec]
        out_shape = [jax.ShapeDtypeStruct((s, D_MODEL), F32), st_shape]
    return pl.pallas_call(
        body, name="gla_fwd_rev" if reverse else "gla_fwd", grid=(n_tiles,), in_specs=in_specs, out_specs=out_specs,
        out_shape=out_shape, scratch_shapes=[pltpu.VMEM((GLA_HEADS, GLA_DV, GLA_DK), F32)], compiler_params=_params(1),
    )(*args)


def gla_bwd(proj, wg, bg, do, states, reverse, first=None):
    s = proj.shape[0]
    t = min(GLA_TILE, s)
    n_tiles = s // t
    n_chunks = t // GLA_CHUNK
    final = first is not None
    tile, specs = _gla_specs(t, n_tiles, not reverse)

    def body(*refs):
        if final:
            (q_ref, k_ref, v_ref, lr_ref, wg_ref, bg_ref, do_ref, st_ref, dqkv1_ref, dlr1_ref, dr_ref,
             dp_ref, dwg_ref, dbg_ref, dstate, dqkv, dbc, dbt) = refs
        else:
            (q_ref, k_ref, v_ref, lr_ref, wg_ref, bg_ref, do_ref, st_ref,
             dqkv, dlr_ref, dwg_ref, dbg_ref, dstate, dbc, dbt) = refs

        @pl.when(pl.program_id(0) == 0)
        def _():
            dstate[...] = jnp.zeros_like(dstate)
            dwg_ref[...] = jnp.zeros_like(dwg_ref)
            dbg_ref[...] = jnp.zeros_like(dbg_ref)

        lr, wg_v = lr_ref[...], wg_ref[...]
        z, log_alpha = _gla_gate(lr, wg_v, bg_ref[...])
        bcum = _exact_dot(_chunk_sum_matrix(t, reverse, False), log_alpha)
        q, k, v, do_v = q_ref[...], k_ref[...], v_ref[...], do_ref[...]
        q_in, k_in, k_st, e_pos, e_neg, e_st, decays = _gla_tile_terms(q, k, bcum, reverse)
        mask = _chunk_mask(t, reverse)
        order = list(range(n_chunks)) if reverse else list(range(n_chunks))[::-1]
        q_b, k_b, ks_b, v_b, do_b = (a.astype(BF16) for a in (q_in, k_in, k_st, v, do_v))
        dq_intra, dk_intra, dv_intra, increments = [], [], [], []
        for head in range(GLA_HEADS):
            kl = slice(head * GLA_DK, (head + 1) * GLA_DK)
            vl = slice(head * GLA_DV, (head + 1) * GLA_DV)
            scores = jnp.where(mask, _bdot_nt(q_b[:, kl], k_b[:, kl]), 0.0).astype(BF16)
            dscores = jnp.where(mask, _bdot_nt(do_b[:, vl], v_b[:, vl]), 0.0).astype(BF16)
            dv_intra.append(_bdot_tn(scores, do_b[:, vl]))
            dq_intra.append(_bdot(dscores, k_b[:, kl]))
            dk_intra.append(_bdot_tn(dscores, q_b[:, kl]))
            increments.append([_bdot_tn(do_b[_chunk_rows(c), vl], q_b[_chunk_rows(c), kl]) for c in range(n_chunks)])
        after_all, ddecay_all = [], []
        for head in range(GLA_HEADS):
            kl = slice(head * GLA_DK, (head + 1) * GLA_DK)
            running = dstate[head]
            after, ddecay = [None] * n_chunks, [None] * n_chunks
            for c in order:
                after[c] = running
                ddecay[c] = jnp.sum(running * st_ref[c, head], axis=0, keepdims=True)
                running = running * decays[c][:, kl] + increments[head][c]
            dstate[head] = running
            after_all.append(after)
            ddecay_all.append(ddecay)
        for head in range(GLA_HEADS):
            kl = slice(head * GLA_DK, (head + 1) * GLA_DK)
            vl = slice(head * GLA_DV, (head + 1) * GLA_DV)
            after, ddecay = after_all[head], ddecay_all[head]
            dq_inter = jnp.concatenate([_bdot(do_b[_chunk_rows(c), vl], st_ref[c, head]) for c in range(n_chunks)], axis=0)
            dv_inter = jnp.concatenate([_bdot_nt(ks_b[_chunk_rows(c), kl], after[c]) for c in range(n_chunks)], axis=0)
            dk_st = jnp.concatenate([_bdot(v_b[_chunk_rows(c), vl], after[c]) for c in range(n_chunks)], axis=0)
            dq_in = dq_intra[head] + dq_inter
            ks_h = k_st[:, kl]
            dqkv[:, 2 * GLA_KEY + head * GLA_DV:2 * GLA_KEY + (head + 1) * GLA_DV] = dv_intra[head] + dv_inter
            dqkv[:, kl] = dq_in * (GLA_DK ** -0.5) * e_pos[:, kl]
            dqkv[:, GLA_KEY + head * GLA_DK:GLA_KEY + (head + 1) * GLA_DK] = dk_intra[head] * e_neg[:, kl] + dk_st * e_st[:, kl]
            dbc[:, kl] = dq_in * q_in[:, kl] - dk_intra[head] * k_in[:, kl] - dk_st * ks_h
            weighted = dk_st * ks_h
            for c in range(n_chunks):
                dbtot = jnp.sum(weighted[_chunk_rows(c)], axis=0, keepdims=True) + ddecay[c] * decays[c][:, kl]
                dbt[_chunk_rows(c), kl] = jnp.broadcast_to(dbtot, (GLA_CHUNK, GLA_DK))
        dlog_alpha = _exact_dot(_chunk_sum_matrix(t, reverse, True), dbc[...]) + dbt[...]
        dz = dlog_alpha * _sigmoid(-z) * (1.0 / GLA_NORMALIZER)
        dlr = _bdot_nt(dz, wg_v)
        dwg_ref[...] += _bdot_tn(lr, dz)
        dbg_ref[...] += jnp.sum(dz, axis=0, keepdims=True)
        if final:
            dp_ref[:, :2 * D_MODEL] = (dqkv[...] + dqkv1_ref[...]).astype(BF16)
            dp_ref[:, 2 * D_MODEL:3 * D_MODEL] = dr_ref[...]
            dp_ref[:, 3 * D_MODEL:] = (dlr + dlr1_ref[...]).astype(BF16)
        else:
            dlr_ref[...] = dlr

    row = pl.BlockSpec((t, D_MODEL), lambda i: (tile(i), 0))
    wide = pl.BlockSpec((t, 2 * D_MODEL), lambda i: (tile(i), 0))
    narrow = pl.BlockSpec((t, LANES), lambda i: (tile(i), 0))
    st_spec = pl.BlockSpec((n_chunks, GLA_HEADS, GLA_DV, GLA_DK), lambda i: (tile(i), 0, 0, 0))
    in_specs = specs + [_full(wg.shape), _full(bg.shape), row, st_spec]
    args = [proj, proj, proj, proj, wg, bg, do, states]
    acc_specs = [_full(wg.shape), _full(bg.shape)]
    acc_shapes = [jax.ShapeDtypeStruct(wg.shape, F32), jax.ShapeDtypeStruct(bg.shape, F32)]
    scratch = [pltpu.VMEM((GLA_HEADS, GLA_DV, GLA_DK), F32)]
    work = [pltpu.VMEM((t, GLA_KEY), F32), pltpu.VMEM((t, GLA_KEY), F32)]
    if final:
        in_specs += [wide, narrow, row]
        args += list(first)
        out_specs = [pl.BlockSpec((t, ODD_IN_PAD), lambda i: (tile(i), 0))] + acc_specs
        out_shape = [jax.ShapeDtypeStruct((s, ODD_IN_PAD), BF16)] + acc_shapes
        scratch += [pltpu.VMEM((t, 2 * D_MODEL), F32)] + work
    else:
        out_specs = [wide, narrow] + acc_specs
        out_shape = [jax.ShapeDtypeStruct((s, 2 * D_MODEL), F32), jax.ShapeDtypeStruct((s, LANES), F32)] + acc_shapes
        scratch += work
    return pl.pallas_call(
        body, name="gla_bwd_rev" if reverse else "gla_bwd", grid=(n_tiles,), in_specs=in_specs, out_specs=out_specs,
        out_shape=out_shape, scratch_shapes=scratch, compiler_params=_params(1),
    )(*args)


def column_blocks(a, width):
    r, c = a.shape
    window = -(-(width + LANES) // LANES) * LANES
    padded = -(-width // LANES) * LANES
    assert window <= c

    def body(a_ref, o_ref):
        row = lax.broadcasted_iota(jnp.int32, (window, padded), 0)
        col = lax.broadcasted_iota(jnp.int32, (window, padded), 1)
        for j in range(N_DEV):
            start = min(j * width // LANES * LANES, c - window)
            pick = jnp.where((row == col + (j * width - start)) & (col < width), 1.0, 0.0).astype(BF16)
            picked = jnp.dot(a_ref[:, start:start + window], pick, preferred_element_type=F32)
            o_ref[j] = picked[:, :width].astype(o_ref.dtype)

    return pl.pallas_call(
        body, name="column_blocks", grid=(1,), in_specs=[_full((r, c))], out_specs=_full((N_DEV, r, width)),
        out_shape=jax.ShapeDtypeStruct((N_DEV, r, width), a.dtype), compiler_params=_params(1),
    )(a)


def columns_from_blocks(parts, total):
    width = parts[0].shape[2]
    rows = [p.shape[1] for p in parts]
    window = -(-(width + LANES) // LANES) * LANES

    def body(*refs):
        part_refs, o_ref, acc = refs[:len(parts)], refs[len(parts)], refs[len(parts) + 1]
        acc[...] = jnp.zeros_like(acc)
        row = lax.broadcasted_iota(jnp.int32, (width, window), 0)
        col = lax.broadcasted_iota(jnp.int32, (width, window), 1)
        for j in range(N_DEV):
            start = min(j * width // LANES * LANES, total - window)
            place = jnp.where(col == row + (j * width - start), 1.0, 0.0).astype(BF16)
            at = 0
            for part_ref, r in zip(part_refs, rows):
                acc[at:at + r, start:start + window] += jnp.dot(part_ref[j], place, preferred_element_type=F32)
                at += r
        o_ref[...] = acc[...].astype(o_ref.dtype)

    return pl.pallas_call(
        body, name="columns_from_blocks", grid=(1,), in_specs=[_full(p.shape) for p in parts], out_specs=_full((sum(rows), total)),
        out_shape=jax.ShapeDtypeStruct((sum(rows), total), parts[0].dtype), scratch_shapes=[pltpu.VMEM((sum(rows), total), F32)],
        compiler_params=_params(1),
    )(*parts)


def pair_sum(grad, from_sibling):
    n_chips, r, w = from_sibling.shape

    def body(even_ref, odd_ref, sib_ref, o_ref):
        mine = jnp.where(lax.axis_index("c") == 1, odd_ref[...], even_ref[...])
        o_ref[...] = (mine.astype(F32) + sib_ref[...].astype(F32)).astype(o_ref.dtype)

    return pl.pallas_call(
        body, name="pair_sum", grid=(n_chips,),
        in_specs=[pl.BlockSpec((r, w), lambda k: (0, 2 * k)), pl.BlockSpec((r, w), lambda k: (0, 2 * k + 1)),
                  pl.BlockSpec((None, r, w), lambda k: (k, 0, 0))],
        out_specs=pl.BlockSpec((None, r, w), lambda k: (k, 0, 0)),
        out_shape=jax.ShapeDtypeStruct(from_sibling.shape, from_sibling.dtype), compiler_params=_params(1),
    )(grad, grad, from_sibling)


def _adamw_update(g, w, m, v):
    new_m = ADAM_B1 * m + (1.0 - ADAM_B1) * g
    new_v = ADAM_B2 * v + (1.0 - ADAM_B2) * (g * g)
    m_hat = new_m / (1.0 - ADAM_B1 ** ADAM_STEP)
    v_hat = new_v / (1.0 - ADAM_B2 ** ADAM_STEP)
    return -ADAM_LR * (m_hat / (jnp.sqrt(v_hat) + ADAM_EPS) + ADAM_WD * w), new_m, new_v


def sum_parts(parts, name):
    _, r, c = parts.shape

    def body(p_ref, o_ref):
        total = p_ref[0].astype(F32)
        for j in range(1, N_DEV):
            total = total + p_ref[j].astype(F32)
        o_ref[...] = total

    return pl.pallas_call(body, name=name, in_specs=[_full(parts.shape)], out_specs=_full((r, c)), grid=(1,),
                          out_shape=jax.ShapeDtypeStruct((r, c), F32), compiler_params=_params(1))(parts)


def adamw(parts, w, m, v, name, exchange=None):
    n, r, c = parts.shape
    tr = r
    while tr * c * 4 > ADAMW_BLOCK_BYTES and tr % (2 * SUBLANES) == 0:
        tr //= 2

    def body(p_ref, w_ref, m_ref, v_ref, g_ref, d_ref, nm_ref, nv_ref):
        g = p_ref[0].astype(F32)
        for j in range(1, n):
            g = g + p_ref[j].astype(F32)
        g_ref[...] = g
        d_ref[...], nm_ref[...], nv_ref[...] = _adamw_update(g, w_ref[...], m_ref[...], v_ref[...])

    row = pl.BlockSpec((tr, c), lambda i: (i, 0))
    return _call(
        body, name=name, grid=(r // tr,),
        in_specs=[pl.BlockSpec((n, tr, c), lambda i: (0, i, 0)), row, row, row], out_specs=[row] * 4,
        out_shape=[jax.ShapeDtypeStruct((r, c), F32)] * 4, args=[parts, w, m, v], exchange=exchange)


def _small_views(shape):
    if len(shape) == 2:
        return [((slice(None), slice(None)), (slice(None), slice(None)))]
    if len(shape) == 3:
        return [((slice(None), slice(None)), (0,))]
    rows = shape[2]
    return [((slice(k * rows, (k + 1) * rows), slice(None)), (0, k)) for k in range(shape[1])]


def adamw_small(landings, w, m, v):
    names = list(landings)
    n = len(names)
    shapes = [w[name].shape for name in names]

    def body(*refs):
        land, ws, ms, vs = refs[:n], refs[n:2 * n], refs[2 * n:3 * n], refs[3 * n:4 * n]
        outs = [refs[(4 + k) * n:(5 + k) * n] for k in range(4)]
        for k in range(n):
            total = land[k][0]
            for j in range(1, N_DEV):
                total = total + land[k][j]
            for rows, at in _small_views(shapes[k]):
                g = total[rows]
                outs[0][k][at] = g
                outs[1][k][at], outs[2][k][at], outs[3][k][at] = _adamw_update(g, ws[k][at], ms[k][at], vs[k][at])

    blocks = [_full(sh) for sh in shapes]
    outs = pl.pallas_call(
        body, name="adamw_small", grid=(1,),
        in_specs=[_full(landings[name].shape) for name in names] + blocks * 3, out_specs=blocks * 4,
        out_shape=[jax.ShapeDtypeStruct(sh, F32) for sh in shapes] * 4, compiler_params=_params(1),
    )(*[landings[name] for name in names], *[src[name] for src in (w, m, v) for name in names])
    return [dict(zip(names, outs[k * n:(k + 1) * n])) for k in range(4)]


def adamw_replicated(land_vec, land_gate_b, land_loss, names, w, m, v, gate_b):
    n = len(names)

    def body(*refs):
        vec_ref, gb_ref, loss_ref = refs[:3]
        ws, ms, vs = refs[3:3 + n], refs[3 + n:3 + 2 * n], refs[3 + 2 * n:3 + 3 * n]
        gw_ref, gm_ref, gv_ref = refs[3 + 3 * n:6 + 3 * n]
        outs = refs[6 + 3 * n:]
        vec, gb, loss = vec_ref[0], gb_ref[0], loss_ref[0]
        for j in range(1, N_DEV):
            vec, gb, loss = vec + vec_ref[j], gb + gb_ref[j], loss + loss_ref[j]
        for k in range(n):
            g = vec[k:k + 1, :]
            outs[k][...] = g
            outs[n + k][...], outs[2 * n + k][...], outs[3 * n + k][...] = _adamw_update(g, ws[k][...], ms[k][...], vs[k][...])
        outs[4 * n][...] = gb
        outs[4 * n + 1][...], outs[4 * n + 2][...], outs[4 * n + 3][...] = _adamw_update(gb, gw_ref[...], gm_ref[...], gv_ref[...])
        outs[4 * n + 4][...] = loss

    vec_block, gb_block = _full((1, D_MODEL)), _full(gate_b[0].shape)
    outs = pl.pallas_call(
        body, name="adamw_replicated", grid=(1,),
        in_specs=[_full(land_vec.shape), _full(land_gate_b.shape), _full(land_loss.shape)] + [vec_block] * (3 * n) + [gb_block] * 3,
        out_specs=[vec_block] * (4 * n) + [gb_block] * 4 + [_full(land_loss.shape[1:])],
        out_shape=[jax.ShapeDtypeStruct((1, D_MODEL), F32)] * (4 * n) + [jax.ShapeDtypeStruct(gate_b[0].shape, F32)] * 4
        + [jax.ShapeDtypeStruct(land_loss.shape[1:], F32)],
        compiler_params=_params(1),
    )(land_vec, land_gate_b, land_loss, *[src[name] for src in (w, m, v) for name in names], *gate_b)
    results = {name: [outs[k * n + i] for k in range(4)] for i, name in enumerate(names)}
    return results, outs[4 * n:4 * n + 4], outs[4 * n + 4]


SMALL_SHARDED = ("rg_conv_w", "rg_lambda", "sc_conv_w", "odd_norm_pre", "odd_norm_post", "gla_b_gate", "gla_norm_g", "gla_w_gate_lr")
SMALL_ROWS = {"rg_conv_w": (0, 4), "rg_lambda": (4, 2), "sc_conv_w": (6, 3), "odd_norm_pre": (9, 1), "odd_norm_post": (10, 1),
              "gla_b_gate": (11, 2), "gla_norm_g": (13, 1), "gla_w_gate_lr": (16, 32)}


def _pack_small(shards):
    pieces, at = [], 0
    for name in SMALL_SHARDED:
        start, rows = SMALL_ROWS[name]
        if start > at:
            pieces.append(jnp.zeros((start - at, LANES), F32))
        a = shards[name].reshape(rows, -1)
        pieces.append(jnp.pad(a, ((0, 0), (0, LANES - a.shape[1]))))
        at = start + rows
    return jnp.concatenate(pieces, axis=0)


def _unpack_gathered(g):
    def cols(name, width):
        start, rows = SMALL_ROWS[name]
        return jnp.transpose(g[:, start:start + rows, :width], (1, 0, 2)).reshape(rows, N_DEV * width)

    w_lr = cols("gla_w_gate_lr", GLA_KEY // N_DEV).reshape(2, GLA_RANK, GLA_KEY)
    return dict(rg_conv_w=cols("rg_conv_w", LANES), rg_lambda=cols("rg_lambda", LANES), sc_conv_w=cols("sc_conv_w", LANES),
                odd_norm_pre=cols("odd_norm_pre", LANES), odd_norm_post=cols("odd_norm_post", LANES),
                gla_b_gate=cols("gla_b_gate", GLA_KEY // N_DEV), gla_norm_g=cols("gla_norm_g", GLA_DV // N_DEV), gla_w_gate_lr=w_lr)


def _blocks_along_columns(a, rows):
    return jnp.transpose(a.reshape(rows, N_DEV, -1), (1, 0, 2))


def kernel(x, even_norm_pre, even_norm_post, even_w_in, rg_conv_w, rg_conv_b, rg_gate_w, rg_gate_b, rg_lambda, sc_conv_w, even_w_out, odd_norm_pre, odd_norm_post, odd_w_in, gla_w_gate_lr, gla_b_gate, gla_norm_g, odd_w_out, loss_target, m_even_norm_pre, m_even_norm_post, m_even_w_in, m_rg_conv_w, m_rg_conv_b, m_rg_gate_w, m_rg_gate_b, m_rg_lambda, m_sc_conv_w, m_even_w_out, m_odd_norm_pre, m_odd_norm_post, m_odd_w_in, m_gla_w_gate_lr, m_gla_b_gate, m_gla_norm_g, m_odd_w_out, v_even_norm_pre, v_even_norm_post, v_even_w_in, v_rg_conv_w, v_rg_conv_b, v_rg_gate_w, v_rg_gate_b, v_rg_lambda, v_sc_conv_w, v_even_w_out, v_odd_norm_pre, v_odd_norm_post, v_odd_w_in, v_gla_w_gate_lr, v_gla_b_gate, v_gla_norm_g, v_odd_w_out):
    weights = dict(even_norm_pre=even_norm_pre, even_norm_post=even_norm_post, even_w_in=even_w_in, rg_conv_w=rg_conv_w,
                   rg_conv_b=rg_conv_b, rg_gate_w=rg_gate_w, rg_gate_b=rg_gate_b, rg_lambda=rg_lambda, sc_conv_w=sc_conv_w,
                   even_w_out=even_w_out, odd_norm_pre=odd_norm_pre, odd_norm_post=odd_norm_post, odd_w_in=odd_w_in,
                   gla_w_gate_lr=gla_w_gate_lr, gla_b_gate=gla_b_gate, gla_norm_g=gla_norm_g, odd_w_out=odd_w_out)
    m_in = dict(even_norm_pre=m_even_norm_pre, even_norm_post=m_even_norm_post, even_w_in=m_even_w_in, rg_conv_w=m_rg_conv_w,
                rg_conv_b=m_rg_conv_b, rg_gate_w=m_rg_gate_w, rg_gate_b=m_rg_gate_b, rg_lambda=m_rg_lambda, sc_conv_w=m_sc_conv_w,
                even_w_out=m_even_w_out, odd_norm_pre=m_odd_norm_pre, odd_norm_post=m_odd_norm_post, odd_w_in=m_odd_w_in,
                gla_w_gate_lr=m_gla_w_gate_lr, gla_b_gate=m_gla_b_gate, gla_norm_g=m_gla_norm_g, odd_w_out=m_odd_w_out)
    v_in = dict(even_norm_pre=v_even_norm_pre, even_norm_post=v_even_norm_post, even_w_in=v_even_w_in, rg_conv_w=v_rg_conv_w,
                rg_conv_b=v_rg_conv_b, rg_gate_w=v_rg_gate_w, rg_gate_b=v_rg_gate_b, rg_lambda=v_rg_lambda, sc_conv_w=v_sc_conv_w,
                even_w_out=v_even_w_out, odd_norm_pre=v_odd_norm_pre, odd_norm_post=v_odd_norm_post, odd_w_in=v_odd_w_in,
                gla_w_gate_lr=v_gla_w_gate_lr, gla_b_gate=v_gla_b_gate, gla_norm_g=v_gla_norm_g, odd_w_out=v_odd_w_out)
    names = list(weights)
    shapes = {n: weights[n].shape for n in names}
    xs = x[0]
    tgt = loss_target[0]

    proj_e, h_e, w_in_e, small_all = gather_matmul(xs, even_norm_pre, even_w_in[0].astype(BF16),
                                                   _pack_small({n: weights[n][0] for n in SMALL_SHARDED}), 2 * MM_TILE)
    small = _unpack_gathered(small_all)
    gate_w = rg_gate_w[0].reshape(4, RG_HEADS, RG_HEAD_DIM, RG_HEAD_DIM).astype(BF16)
    gate_b = rg_gate_b[0].reshape(4, RG_HEADS, RG_HEAD_DIM)
    conv_b = rg_conv_b
    wg_pad = [jnp.pad(small["gla_w_gate_lr"][d], ((GLA_RANK * d, LANES - GLA_RANK * (d + 1)), (0, 0))).astype(BF16) for d in range(2)]
    bg = [small["gla_b_gate"][d:d + 1] for d in range(2)]
    gnorm = jnp.tile(small["gla_norm_g"], (1, GLA_HEADS))

    half = D_MODEL // 2
    behind_gates = Exchange()
    behind_gates.gather(even_w_out[0].astype(BF16), via_sibling=True)
    behind_gates.gather(odd_w_in[0, :half].astype(BF16), via_sibling=True)
    (ab, hf), (w_out_e, w_in_o_top) = even_gates_fwd(proj_e, small["rg_conv_w"], conv_b, gate_w, gate_b, small["rg_lambda"],
                                                     exchange=behind_gates)
    w_out_e = w_out_e.reshape(2 * D_MODEL, D_MODEL)
    behind_mix_fwd = Exchange()
    behind_mix_fwd.gather(odd_w_in[0, half:].astype(BF16), via_sibling=True)
    behind_mix_fwd.gather(odd_w_out[0].astype(BF16), via_sibling=True)
    (u_e, hb, y_e, x1), (w_in_o_bottom, w_out_o) = even_mix_fwd(ab, hf, proj_e, small["sc_conv_w"], w_out_e, xs, even_norm_post,
                                                                exchange=behind_mix_fwd)
    w_out_o = w_out_o.reshape(D_MODEL, D_MODEL)
    w_in_o = columns_from_blocks([w_in_o_top, w_in_o_bottom], ODD_IN_PAD)

    proj_o, h_o = rms_matmul(x1, small["odd_norm_pre"], w_in_o, MM_TILE, ODD_IN_PAD, "odd_in")
    o_f, st_f = gla_fwd(proj_o, wg_pad[0], bg[0], False)
    osum, u_o, st_b, y_o, dout, loss_part = gla_fwd(proj_o, wg_pad[1], bg[1], True, o_other=o_f, gnorm=gnorm,
                                                    post=(w_out_o, x1, small["odd_norm_post"], tgt))

    do, dr, dy_o, d_odd_norm_post, d_gnorm = normbwd_matmul_nt(y_o, small["odd_norm_post"], dout, w_out_o, D_MODEL, "odd_out_bwd",
                                                               gla=(proj_o, osum, gnorm))
    d_w_out_o = matmul_tn(u_o, dy_o, D_MODEL, D_MODEL, 4 * MM_TILE, BF16, "odd_w_out_grad")
    dqkv_f, dlr_f, dwg_f, dbg_f = gla_bwd(proj_o, wg_pad[0], bg[0], do, st_f, False)
    dproj_o, dwg_b, dbg_b = gla_bwd(proj_o, wg_pad[1], bg[1], do, st_b, True, first=(dqkv_f, dlr_f, dr))
    dx1, d_odd_norm_pre = matmul_nt_normbwd(dproj_o, w_in_o, x1, small["odd_norm_pre"], dout, MM_TILE, ODD_IN_PAD, "odd_in_bwd")
    d_w_in_o = matmul_tn(h_o, dproj_o, D_MODEL, ODD_IN_PAD // 5, 8 * MM_TILE, BF16, "odd_w_in_grad")

    landed = {}
    behind_out = Exchange()
    behind_out.scatter(d_w_out_o.reshape(N_DEV, D_MODEL // N_DEV, D_MODEL))
    behind_out.scatter(d_odd_norm_pre, columns=True)
    behind_out.scatter(d_odd_norm_post, columns=True)
    behind_out.scatter(_blocks_along_columns(jnp.concatenate([dbg_f, dbg_b], axis=0), 2))
    behind_out.scatter(_blocks_along_columns(d_gnorm, 1))
    behind_out.scatter(_blocks_along_columns(jnp.concatenate([dwg_f[:GLA_RANK], dwg_b[GLA_RANK:2 * GLA_RANK]], axis=0), 2 * GLA_RANK))
    (du_e, dy_e, d_even_norm_post), got = normbwd_matmul_nt(y_e, even_norm_post, dx1, w_out_e, 2 * D_MODEL, "even_out_bwd",
                                                           exchange=behind_out)
    p_w_out_o = got[0]
    for n, part in zip(("odd_norm_pre", "odd_norm_post", "gla_b_gate", "gla_norm_g", "gla_w_gate_lr"), got[1:]):
        landed[n] = part
    d_w_out_e = matmul_tn(u_e, dy_e, D_MODEL, D_MODEL, 4 * MM_TILE, BF16, "even_w_out_grad")
    behind_mix = Exchange()
    behind_mix.scatter(d_w_out_e.reshape(N_DEV, 2 * D_MODEL // N_DEV, D_MODEL))
    (dh, drest, d_sc_w, adj_b), (p_w_out_e,) = even_mix_bwd(du_e, hf, hb, proj_e, small["sc_conv_w"], ab, exchange=behind_mix)
    adj_f = linear_scan(ab, 0, dh.reshape(1, *dh.shape), 0, True, True, "scan_fwd_adjoint")
    behind_gates_bwd = Exchange()
    behind_gates_bwd.scatter(column_blocks(d_w_in_o, ODD_SHARD))
    behind_gates_bwd.scatter(d_sc_w, columns=True)
    (dua, d_gate_w, d_gate_b, d_lam), (p_w_in_o, landed["sc_conv_w"]) = even_gates_bwd(
        proj_e, adj_f, adj_b, hf, hb, dh, small["rg_conv_w"], conv_b, gate_w, gate_b, small["rg_lambda"], exchange=behind_gates_bwd)
    gate_w_rows = 4 * RG_HEADS * RG_HEAD_DIM
    behind_conv = Exchange()
    behind_conv.scatter(d_gate_w.reshape(N_DEV, gate_w_rows // N_DEV, RG_HEAD_DIM))
    behind_conv.scatter(d_lam, columns=True)
    (dxa, d_conv_w, d_conv_b), (p_gate_w, landed["rg_lambda"]) = rg_conv_bwd(dua, proj_e, small["rg_conv_w"], exchange=behind_conv)
    behind_w_grad = Exchange()
    behind_w_grad.gather(sum_parts(p_gate_w, "sum_gate_w"))
    d_w_in_e, (g_gate_w_all,) = matmul_tn(h_e, drest, D_MODEL, D_MODEL, 4 * MM_TILE, BF16, "even_w_in_grad",
                                          exchange=behind_w_grad, b_first=dxa)
    to_sibling = Exchange()
    to_sibling.to_sibling(d_w_in_e)
    to_sibling.scatter(d_conv_w, columns=True)
    from_sibling, landed["rg_conv_w"] = run_exchange(to_sibling, "scatter_to_sibling")
    behind_in_bwd = Exchange()
    behind_in_bwd.among_chips(pair_sum(d_w_in_e, from_sibling))
    (grad_x, d_even_norm_pre), (p_w_in_e,) = matmul_nt_normbwd(
        drest, w_in_e, xs, even_norm_pre, dx1, 2 * MM_TILE, D_MODEL, "even_in_bwd", exchange=behind_in_bwd, first=dxa)
    last = Exchange()
    replicated_vecs = ("even_norm_pre", "even_norm_post", "rg_conv_b")
    last.gather(jnp.concatenate([d_even_norm_pre, d_even_norm_post, d_conv_b], axis=0))
    last.gather(d_gate_b.reshape(4 * RG_HEADS, RG_HEAD_DIM))
    last.gather(loss_part)

    results = {}

    def update(name, parts_, shape2d, exchange=None):
        outs = adamw(parts_, weights[name][0].reshape(shape2d), m_in[name][0].reshape(shape2d), v_in[name][0].reshape(shape2d),
                     "adamw_" + name, exchange=exchange)
        if exchange is not None:
            outs, gathered = outs
        results[name] = [o.reshape(shapes[name]) for o in outs]
        return gathered if exchange is not None else None

    land_vec, land_gate_b, land_loss = update("even_w_in", p_w_in_e, (D_MODEL, EVEN_SHARD), exchange=last)
    update("even_w_out", p_w_out_e, (2 * D_MODEL // N_DEV, D_MODEL))
    update("odd_w_in", p_w_in_o, (D_MODEL, ODD_SHARD))
    update("odd_w_out", p_w_out_o, (D_MODEL // N_DEV, D_MODEL))
    update("rg_gate_w", g_gate_w_all.reshape(1, gate_w_rows, RG_HEAD_DIM), (gate_w_rows, RG_HEAD_DIM))
    small_out = adamw_small({n: landed[n] for n in SMALL_SHARDED}, weights, m_in, v_in)
    for n in SMALL_SHARDED:
        results[n] = [o[n] for o in small_out]
    gate_b_shape = (4 * RG_HEADS, RG_HEAD_DIM)
    rep_out, gate_b_out, loss_all = adamw_replicated(land_vec, land_gate_b, land_loss, replicated_vecs, weights, m_in, v_in,
                                                     [src["rg_gate_b"].reshape(gate_b_shape) for src in (weights, m_in, v_in)])
    results.update(rep_out)
    results["rg_gate_b"] = [o.reshape(shapes["rg_gate_b"]) for o in gate_b_out]

    return (loss_all[0, 0], grad_x.reshape(x.shape), *[results[n][0] for n in names], *[results[n][1] for n in names],
            *[results[n][2] for n in names], *[results[n][3] for n in names])
```

```python
import functools

import jax
import jax.numpy as jnp
from jax import lax
from jax.experimental import pallas as pl
from jax.experimental.pallas import tpu as pltpu

F32 = jnp.float32
BF16 = jnp.bfloat16

N_DEV = 8
D_MODEL = 1024
NORM_EPS = 1e-6
RG_HEADS = 8
RG_HEAD_DIM = 128
RG_C = 8.0
GLA_HEADS = 4
GLA_DK = 128
GLA_DV = 256
GLA_KEY = 512
GLA_RANK = 16
GLA_NORMALIZER = 16.0
GLA_CHUNK = 64
EVEN_IN = 6144
ODD_IN = 3104
ODD_IN_PAD = 3200
ODD_SHARD = ODD_IN // N_DEV
EVEN_SHARD = EVEN_IN // N_DEV
ADAM_LR = 0.001
ADAM_B1 = 0.9
ADAM_B2 = 0.999
ADAM_EPS = 1e-08
ADAM_WD = 0.01
ADAM_STEP = 10

SMALLEST_NORMAL = 1.1754944e-38
SUBLANES = 8
LANES = 128
VMEM_LIMIT_BYTES = 48 * 2 ** 20
ROW_TILE = 256
GLA_TILE = 256
MM_TILE = 512
ADAMW_BLOCK_BYTES = 2 ** 20
PACK_ROWS = 48
MESH_ID = pl.DeviceIdType.MESH


def _params(n_grid):
    return pltpu.CompilerParams(dimension_semantics=("arbitrary",) * n_grid, vmem_limit_bytes=VMEM_LIMIT_BYTES)


def _bdot(a, b):
    return jnp.dot(a.astype(BF16), b.astype(BF16), preferred_element_type=F32)


def _bdot_nt(a, b):
    return lax.dot_general(a.astype(BF16), b.astype(BF16), (((1,), (1,)), ((), ())), preferred_element_type=F32)


def _bdot_tn(a, b):
    return lax.dot_general(a.astype(BF16), b.astype(BF16), (((0,), (0,)), ((), ())), preferred_element_type=F32)


def _rstd(x):
    return lax.rsqrt(jnp.mean(x * x, axis=-1, keepdims=True) + NORM_EPS)


def _rms(x, g):
    return x * _rstd(x) * g


def _rms_bwd(x, g, dy):
    xh = x * _rstd(x)
    dyg = dy * g
    dx = _rstd(x) * (dyg - xh * jnp.mean(dyg * xh, axis=-1, keepdims=True))
    return dx, jnp.sum(dy * xh, axis=0, keepdims=True)


def _sigmoid(z):
    return 0.5 * jnp.tanh(0.5 * z) + 0.5


def _silu_and_grad(z):
    s = _sigmoid(z)
    return z * s, s * (1.0 + z * (1.0 - s))


def _softplus(z):
    return jnp.maximum(z, 0.0) + jnp.log(1.0 + jnp.exp(-jnp.abs(z)))


def _shift_rows(cur, before, after, d):
    ts = cur.shape[0]
    row = lax.broadcasted_iota(jnp.int32, (SUBLANES, cur.shape[1]), 0)
    out = pltpu.roll(cur, (-d) % ts, 0)
    if d < 0:
        edge = jnp.where(row < -d, pltpu.roll(before, (-d) % SUBLANES, 0), out[:SUBLANES])
        return jnp.concatenate([edge, out[SUBLANES:]], axis=0)
    edge = jnp.where(row >= SUBLANES - d, pltpu.roll(after, (-d) % SUBLANES, 0), out[ts - SUBLANES:])
    return jnp.concatenate([out[:ts - SUBLANES], edge], axis=0)


def _halo_specs(ts, s, width, col, tile=lambda i: i):
    per = ts // SUBLANES
    last = s // SUBLANES - 1
    return [
        pl.BlockSpec((ts, width), lambda i: (tile(i), col)),
        pl.BlockSpec((SUBLANES, width), lambda i: (jnp.maximum(tile(i) * per - 1, 0), col)),
        pl.BlockSpec((SUBLANES, width), lambda i: (jnp.minimum((tile(i) + 1) * per, last), col)),
    ]


def _halo_load(cur_ref, before_ref, after_ref, n_tiles, tile=lambda i: i):
    i = tile(pl.program_id(0))
    before = jnp.where(i > 0, before_ref[...], 0.0)
    after = jnp.where(i < n_tiles - 1, after_ref[...], 0.0)
    return cur_ref[...], before, after


def _full(shape):
    return pl.BlockSpec(shape, lambda *_: (0,) * len(shape))


def _peer(x, y, c, mask):
    px, py, pc = x ^ (mask >> 2), y ^ ((mask >> 1) & 1), c ^ (mask & 1)
    return (px, py, pc), 4 * px + 2 * py + pc


class Exchange:
    SIBLING = 1
    OTHER_CHIPS = (2, 4, 6)

    def __init__(self):
        self.args, self.out_shape, self._kinds = [], [], []

    def gather(self, block, columns=False, via_sibling=False):
        shape = (block.shape[0], N_DEV * block.shape[1]) if columns else (N_DEV,) + block.shape
        return self._add(block, shape, ("gather", columns, via_sibling))

    def scatter(self, stack, columns=False):
        shape = (N_DEV, stack.shape[0], stack.shape[1] // N_DEV) if columns else stack.shape
        return self._add(stack, shape, ("scatter", columns, False))

    def _add(self, arg, shape, kind):
        self.args.append(arg)
        self.out_shape.append(jax.ShapeDtypeStruct(shape, arg.dtype))
        self._kinds.append(kind)
        return len(self.args) - 1

    def semaphores(self):
        n = len(self.args)
        return [pltpu.SemaphoreType.DMA((n, N_DEV - 1)), pltpu.SemaphoreType.DMA((n, N_DEV - 1)), pltpu.SemaphoreType.DMA((n,))]

    def to_sibling(self, array):
        shape = (N_DEV // 2, array.shape[0], array.shape[1] // N_DEV)
        return self._add(array, shape, ("to_sibling", True, False))

    def among_chips(self, stack):
        return self._add(stack, stack.shape, ("among_chips", False, False))

    def _copies(self, position, in_refs, out_refs):
        x, y, c, me = position
        for arr, ((kind, columns, via_sibling), src, out) in enumerate(zip(self._kinds, in_refs, out_refs)):
            if kind == "to_sibling":
                width = src.shape[-1] // N_DEV
                for k in range(N_DEV // 2):
                    block = src.at[:, pl.ds(pl.multiple_of((2 * k + 1 - c) * width, LANES), width)]
                    yield arr, k + 1, block, out.at[k], out.at[k], False, self.SIBLING
                continue
            for mask in range(N_DEV):
                _, peer_id = _peer(x, y, c, mask)
                relayed = via_sibling and mask not in (0, self.SIBLING) + self.OTHER_CHIPS
                if kind == "among_chips":
                    if mask in (0,) + self.OTHER_CHIPS:
                        yield arr, mask, src.at[peer_id // 2], out.at[me // 2], out.at[peer_id // 2], False, mask
                elif kind == "gather":
                    if columns:
                        width = src.shape[-1]
                        yield (arr, mask, src, out.at[:, pl.ds(pl.multiple_of(me * width, LANES), width)],
                               out.at[:, pl.ds(pl.multiple_of(peer_id * width, LANES), width)], relayed, mask)
                    else:
                        yield arr, mask, src, out.at[me], out.at[peer_id], relayed, mask
                else:
                    if columns:
                        width = src.shape[-1] // N_DEV
                        block = src.at[:, pl.ds(pl.multiple_of(peer_id * width, LANES), width)]
                    else:
                        block = src.at[peer_id]
                    yield arr, mask, block, out.at[me], out.at[peer_id], False, mask

    def _remote(self, position, sems, arr, slot, to_mask, src, dst):
        x, y, c, _ = position
        return pltpu.make_async_remote_copy(src_ref=src, dst_ref=dst, send_sem=sems[0].at[arr, slot - 1], recv_sem=sems[1].at[arr, slot - 1],
                                            device_id=_peer(x, y, c, to_mask)[0], device_id_type=MESH_ID)

    def start(self, position, in_refs, out_refs, sems):
        for arr, slot, src, dst, _, relayed, to_mask in self._copies(position, in_refs, out_refs):
            if slot == 0:
                pltpu.make_async_copy(src, dst, sems[2].at[arr]).start()
            elif not relayed:
                self._remote(position, sems, arr, slot, to_mask, src, dst).start()

    def wait(self, position, in_refs, out_refs, sems):
        copies = list(self._copies(position, in_refs, out_refs))
        landings = {(arr, slot): landing for arr, slot, _, _, landing, _, _ in copies}
        passed_on = set()
        for arr, mask, src, _, landing, relayed, _ in copies:
            if relayed:
                held = landings[arr, mask ^ self.SIBLING]
                self._remote(position, sems, arr, mask ^ self.SIBLING, mask ^ self.SIBLING, src, held).wait_recv()
                self._remote(position, sems, arr, mask, self.SIBLING, held, held).start()
                passed_on.add((arr, mask ^ self.SIBLING))
        for arr, slot, src, dst, landing, relayed, to_mask in copies:
            if slot == 0:
                pltpu.make_async_copy(src, dst, sems[2].at[arr]).wait()
                continue
            if (arr, slot) not in passed_on:
                self._remote(position, sems, arr, slot, to_mask, src, landing).wait_recv()
            if relayed:
                held = landings[arr, slot ^ self.SIBLING]
                self._remote(position, sems, arr, slot, self.SIBLING, held, held).wait_send()
            else:
                self._remote(position, sems, arr, slot, to_mask, src, dst).wait_send()


def _call(body, *, name, grid, in_specs, out_specs, out_shape, args, scratch_shapes=(), exchange=None):
    single = not isinstance(out_shape, (list, tuple))
    if single:
        out_specs, out_shape = [out_specs], [out_shape]
    params = _params(len(grid))
    if exchange is None:
        outs = pl.pallas_call(body, name=name, grid=grid, in_specs=in_specs, out_specs=out_specs, out_shape=out_shape,
                              scratch_shapes=list(scratch_shapes), compiler_params=params)(*args)
        return outs[0] if single else outs
    counts = (len(args), len(exchange.args), len(out_shape), len(exchange.out_shape), len(scratch_shapes), 3)

    def wrapped(*refs):
        groups, at = [], 0
        for n in counts:
            groups.append(refs[at:at + n])
            at += n
        main_in, ex_in, main_out, ex_out, main_scratch, sems = groups
        x, y, c = lax.axis_index("x"), lax.axis_index("y"), lax.axis_index("c")
        position = (x, y, c, 4 * x + 2 * y + c)
        ids = [pl.program_id(a) for a in range(len(grid))]
        first = functools.reduce(jnp.logical_and, [i == 0 for i in ids])
        last = functools.reduce(jnp.logical_and, [i == g - 1 for i, g in zip(ids, grid)])

        @pl.when(first)
        def _():
            exchange.start(position, ex_in, ex_out, sems)

        body(*main_in, *main_out, *main_scratch)

        @pl.when(last)
        def _():
            exchange.wait(position, ex_in, ex_out, sems)

    hbm = pl.BlockSpec(memory_space=pl.ANY)
    outs = pl.pallas_call(
        wrapped, name=name, grid=grid, in_specs=list(in_specs) + [hbm] * counts[1], out_specs=list(out_specs) + [hbm] * counts[3],
        out_shape=list(out_shape) + exchange.out_shape, scratch_shapes=list(scratch_shapes) + exchange.semaphores(),
        compiler_params=params)(*args, *exchange.args)
    main = outs[:counts[2]]
    return (main[0] if single else main), outs[counts[2]:]


def run_exchange(exchange, name):
    return _call(lambda: None, name=name, grid=(1,), in_specs=[], out_specs=[], out_shape=[], args=[], exchange=exchange)[1]


def gather_matmul(x, g, w_block, small_block, tm):
    s, d = x.shape
    width = w_block.shape[1]
    tm = min(tm, s)
    n_i = s // tm
    sibling = Exchange.SIBLING
    y_nbr, x_nbr, diagonal = Exchange.OTHER_CHIPS

    def links(core):
        return (y_nbr, x_nbr) if core == 1 else (x_nbr, y_nbr)

    def block_order(core):
        first, second = links(core)
        return [0, sibling, first, second | sibling, second, first | sibling, diagonal, diagonal | sibling]

    def body(order_ref, x_ref, g_ref, wb_ref, sb_ref, proj_ref, h_ref, w_ref, small_ref, h_all, w_buf, send, recv, local, load_sem):
        j, i = pl.program_id(0), pl.program_id(1)
        xx, yy, cc = lax.axis_index("x"), lax.axis_index("y"), lax.axis_index("c")
        me = 4 * xx + 2 * yy + cc

        def block_of(dev):
            return w_ref.at[:, pl.ds(pl.multiple_of(dev * width, LANES), width)]

        def half_of(dev, part):
            return w_ref.at[pl.ds(part * (d // 2), d // 2), pl.ds(pl.multiple_of(dev * width, LANES), width)]

        def remote(arr, slot, to_mask, src, dst):
            return pltpu.make_async_remote_copy(src_ref=src, dst_ref=dst, send_sem=send.at[arr, slot - 1], recv_sem=recv.at[arr, slot - 1],
                                                device_id=_peer(xx, yy, cc, to_mask)[0], device_id_type=MESH_ID)

        def mine_to(mask):
            return remote(0, mask, mask, wb_ref, block_of(me))

        def arrival(mask):
            return remote(0, mask, mask, wb_ref, block_of(me ^ mask))

        def to_sibling(mask):
            return remote(0, mask | sibling, sibling, block_of(me ^ mask), block_of(me ^ mask))

        def relay(of):
            along_x = of == y_nbr
            part = 0 if along_x else 1
            return remote(0 if along_x else 2, diagonal, x_nbr if along_x else y_nbr, half_of(me ^ of, part), half_of(me ^ of, part))

        def diagonal_half(part):
            return remote(0 if part == 0 else 2, diagonal, x_nbr if part == 0 else y_nbr, wb_ref.at[pl.ds(0, d // 2), :],
                          half_of(me ^ diagonal, part))

        @pl.when((j == 0) & (i == 0))
        def _():
            pltpu.make_async_copy(wb_ref, block_of(me), local.at[0]).start()
            pltpu.make_async_copy(sb_ref, small_ref.at[me], local.at[1]).start()
            mine_to(sibling).start()
            for mask in range(1, N_DEV):
                remote(1, mask, mask, sb_ref, small_ref.at[me]).start()

        def load(step):
            return pltpu.make_async_copy(w_ref.at[:, pl.ds(pl.multiple_of(order_ref[step] * width, LANES), width)],
                                         w_buf.at[step % 2], load_sem.at[step % 2])

        for core in range(2):
            first, second = links(core)
            for step in range(N_DEV):
                at_step = (j == 0) & (i == 0) if step == 0 else (j == step - 1) & (i == n_i - 1)

                @pl.when(at_step & (cc == core))
                def _(step=step, first=first, second=second):
                    if step == 0:
                        mine_to(first).start()
                        pltpu.make_async_copy(wb_ref, block_of(me), local.at[0]).wait()
                    elif step == 1:
                        arrival(sibling).wait_recv()
                    elif step == 2:
                        arrival(first).wait_recv()
                        to_sibling(first).start()
                        mine_to(first).wait_send()
                        mine_to(second).start()
                        relay(first).start()
                    elif step == 3:
                        arrival(second | sibling).wait_recv()
                    elif step == 4:
                        arrival(second).wait_recv()
                        to_sibling(second).start()
                        relay(second).start()
                    elif step == 5:
                        arrival(first | sibling).wait_recv()
                    elif step == 6:
                        diagonal_half(0).wait_recv()
                        diagonal_half(1).wait_recv()
                        to_sibling(diagonal).start()
                    else:
                        arrival(diagonal | sibling).wait_recv()

        @pl.when((j == 0) & (i == 0))
        def _():
            load(0).start()

        @pl.when(i == 0)
        def _():
            load(j).wait()

        @pl.when((i == n_i - 1) & (j < N_DEV - 1))
        def _():
            load(j + 1).start()

        rows = pl.ds(pl.multiple_of(i * tm, tm), tm)

        @pl.when(j == 0)
        def _():
            h = _rms(x_ref[...], g_ref[...]).astype(BF16)
            h_all[rows, :] = h
            h_ref[...] = h

        proj_ref[...] = jnp.dot(h_all[rows, :], w_buf[j % 2], preferred_element_type=F32)

        @pl.when((j == N_DEV - 1) & (i == n_i - 1))
        def _():
            pltpu.make_async_copy(sb_ref, small_ref.at[me], local.at[1]).wait()
            for mask in range(1, N_DEV):
                remote(1, mask, mask, sb_ref, small_ref.at[me ^ mask]).wait_recv()
                remote(1, mask, mask, sb_ref, small_ref.at[me]).wait_send()
            mine_to(sibling).wait_send()
            for core in range(2):
                @pl.when(cc == core)
                def _(core=core):
                    mine_to(links(core)[1]).wait_send()
            for mask in Exchange.OTHER_CHIPS:
                to_sibling(mask).wait_send()
            relay(y_nbr).wait_send()
            relay(x_nbr).wait_send()

    def first_pass_row(j, i, order):
        return jnp.where(j == 0, i, n_i - 1), 0

    hbm = pl.BlockSpec(memory_space=pl.ANY)
    core = lax.axis_index("c")
    me = 4 * lax.axis_index("x") + 2 * lax.axis_index("y") + core
    order = (me ^ jnp.where(core == 1, jnp.array(block_order(1)), jnp.array(block_order(0)))).astype(jnp.int32)
    grid_spec = pltpu.PrefetchScalarGridSpec(
        num_scalar_prefetch=1, grid=(N_DEV, n_i),
        in_specs=[pl.BlockSpec((tm, d), first_pass_row), pl.BlockSpec((1, d), lambda j, i, order: (0, 0)), hbm, hbm],
        out_specs=[pl.BlockSpec((tm, width), lambda j, i, order: (i, order[j])), pl.BlockSpec((tm, d), first_pass_row), hbm, hbm],
        scratch_shapes=[pltpu.VMEM((s, d), BF16), pltpu.VMEM((2, d, width), BF16), pltpu.SemaphoreType.DMA((3, N_DEV - 1)),
                        pltpu.SemaphoreType.DMA((3, N_DEV - 1)), pltpu.SemaphoreType.DMA((2,)), pltpu.SemaphoreType.DMA((2,))])
    return pl.pallas_call(
        body, name="even_in", grid_spec=grid_spec,
        out_shape=[jax.ShapeDtypeStruct((s, N_DEV * width), F32), jax.ShapeDtypeStruct((s, d), BF16),
                   jax.ShapeDtypeStruct((d, N_DEV * width), w_block.dtype), jax.ShapeDtypeStruct((N_DEV,) + small_block.shape, small_block.dtype)],
        compiler_params=_params(2),
    )(order, x, g, w_block, small_block)


def rms_matmul(x, g, w, tm, tn, name, exchange=None):
    s, d = x.shape
    n = w.shape[1]
    tm = min(tm, s)

    def body(x_ref, g_ref, w_ref, o_ref, h_ref):
        @pl.when(pl.program_id(1) == 0)
        def _():
            h_ref[...] = _rms(x_ref[...], g_ref[...]).astype(BF16)

        o_ref[...] = jnp.dot(h_ref[...], w_ref[...], preferred_element_type=F32)

    return _call(
        body, name=name, grid=(s // tm, n // tn),
        in_specs=[pl.BlockSpec((tm, d), lambda i, j: (i, 0)), _full((1, d)), pl.BlockSpec((d, tn), lambda i, j: (0, j))],
        out_specs=[pl.BlockSpec((tm, tn), lambda i, j: (i, j)), pl.BlockSpec((tm, d), lambda i, j: (i, 0))],
        out_shape=[jax.ShapeDtypeStruct((s, n), F32), jax.ShapeDtypeStruct((s, d), BF16)],
        args=[x, g, w], exchange=exchange)


def _gla_out_bwd(du, r, osum, gn, do_ref, dr_ref, dgn_ref):
    silu_r, dsilu_r = _silu_and_grad(r)
    for head in range(GLA_HEADS):
        vl = slice(head * GLA_DV, (head + 1) * GLA_DV)
        o_h, g_h, du_h = osum[:, vl], gn[:, vl], du[:, vl]
        dr_ref[:, vl] = (du_h * _rms(o_h, g_h) * dsilu_r[:, vl]).astype(BF16)
        do_h, dg_h = _rms_bwd(o_h, g_h, du_h * silu_r[:, vl])
        do_ref[:, vl] = do_h
        dgn_ref[...] += dg_h


def normbwd_matmul_nt(y, g, dout, w, tn, name, exchange=None, gla=None):
    s, d = y.shape
    n = w.shape[0]
    tm = min(MM_TILE, s)

    def body(*refs):
        if gla is None:
            y_ref, g_ref, dout_ref, w_ref, du_ref, dy_ref, dg_ref = refs
        else:
            y_ref, g_ref, dout_ref, w_ref, r_ref, o_ref, gn_ref, do_ref, dr_ref, dy_ref, dg_ref, dgn_ref = refs
        i, j = pl.program_id(0), pl.program_id(1)

        @pl.when(j == 0)
        def _():
            dy, dg = _rms_bwd(y_ref[...], g_ref[...], dout_ref[...])
            dy_ref[...] = dy.astype(BF16)

            @pl.when(i == 0)
            def _():
                dg_ref[...] = jnp.zeros_like(dg_ref)
                if gla is not None:
                    dgn_ref[...] = jnp.zeros_like(dgn_ref)

            dg_ref[...] += dg

        du = lax.dot_general(dy_ref[...], w_ref[...], (((1,), (1,)), ((), ())), preferred_element_type=F32)
        if gla is None:
            du_ref[...] = du
        else:
            _gla_out_bwd(du, r_ref[...], o_ref[...], gn_ref[...], do_ref, dr_ref, dgn_ref)

    row = pl.BlockSpec((tm, d), lambda i, j: (i, 0))
    in_specs = [row, _full((1, d)), row, pl.BlockSpec((tn, d), lambda i, j: (j, 0))]
    args = [y, g, dout, w]
    tail_specs = [row, _full((1, d))]
    tail_shapes = [jax.ShapeDtypeStruct((s, d), BF16), jax.ShapeDtypeStruct((1, d), F32)]
    if gla is None:
        out_specs = [pl.BlockSpec((tm, tn), lambda i, j: (i, j))] + tail_specs
        out_shape = [jax.ShapeDtypeStruct((s, n), F32)] + tail_shapes
    else:
        proj, osum, gnorm = gla
        assert n == tn == D_MODEL
        in_specs += [pl.BlockSpec((tm, D_MODEL), lambda i, j: (i, 2)), row, _full(gnorm.shape)]
        args += [proj, osum, gnorm]
        out_specs = [row, row] + tail_specs + [_full((1, GLA_DV))]
        out_shape = [jax.ShapeDtypeStruct((s, D_MODEL), F32), jax.ShapeDtypeStruct((s, D_MODEL), BF16)] + tail_shapes + [
            jax.ShapeDtypeStruct((1, GLA_DV), F32)]
    return _call(body, name=name, grid=(s // tm, n // tn), in_specs=in_specs, out_specs=out_specs, out_shape=out_shape,
                 args=args, exchange=exchange)


def matmul_tn(a, b, tm, tn, ts, out_dtype, name, exchange=None, b_first=None):
    s, m = a.shape
    n = b.shape[1] + (0 if b_first is None else tn)
    ts = min(ts, s)
    n_k = s // ts
    dims = (((0,), (0,)), ((), ()))

    def body(*refs):
        if b_first is None:
            a_ref, b_ref, o_ref, acc = refs
        else:
            a_ref, first_ref, b_ref, o_ref, acc = refs
        j, k = pl.program_id(1), pl.program_id(2)

        @pl.when(k == 0)
        def _():
            acc[...] = jnp.zeros_like(acc)

        if b_first is None:
            acc[...] += lax.dot_general(a_ref[...], b_ref[...], dims, preferred_element_type=F32)
        else:
            @pl.when(j == 0)
            def _():
                acc[...] += lax.dot_general(a_ref[...], first_ref[...], dims, preferred_element_type=F32)

            @pl.when(j > 0)
            def _():
                acc[...] += lax.dot_general(a_ref[...], b_ref[...], dims, preferred_element_type=F32)

        @pl.when(k == n_k - 1)
        def _():
            o_ref[...] = acc[...].astype(out_dtype)

    if b_first is None:
        b_specs, b_args = [pl.BlockSpec((ts, tn), lambda i, j, k: (k, j))], [b]
    else:
        b_specs = [pl.BlockSpec((ts, tn), lambda i, j, k: (k, 0)), pl.BlockSpec((ts, tn), lambda i, j, k: (k, jnp.maximum(j - 1, 0)))]
        b_args = [b_first, b]
    return _call(
        body, name=name, grid=(m // tm, n // tn, n_k),
        in_specs=[pl.BlockSpec((ts, tm), lambda i, j, k: (k, i))] + b_specs,
        out_specs=pl.BlockSpec((tm, tn), lambda i, j, k: (i, j)),
        out_shape=jax.ShapeDtypeStruct((m, n), out_dtype),
        scratch_shapes=[pltpu.VMEM((tm, tn), F32)], args=[a] + b_args, exchange=exchange)


def matmul_nt_normbwd(dproj, w, x, g, dres, tm, tk, name, exchange=None, first=None):
    s, kt = dproj.shape
    kt += 0 if first is None else tk
    d = w.shape[0]
    tm = min(tm, s)
    n_k = kt // tk
    dims = (((1,), (1,)), ((), ()))

    def body(*refs):
        if first is None:
            a_ref, w_ref, x_ref, g_ref, r_ref, dx_ref, dg_ref, acc = refs
        else:
            first_ref, a_ref, w_ref, x_ref, g_ref, r_ref, dx_ref, dg_ref, acc = refs
        i, k = pl.program_id(0), pl.program_id(1)

        @pl.when(k == 0)
        def _():
            acc[...] = jnp.zeros_like(acc)

        if first is None:
            acc[...] += lax.dot_general(a_ref[...], w_ref[...], dims, preferred_element_type=F32)
        else:
            @pl.when(k == 0)
            def _():
                acc[...] += lax.dot_general(first_ref[...], w_ref[...], dims, preferred_element_type=F32)

            @pl.when(k > 0)
            def _():
                acc[...] += lax.dot_general(a_ref[...], w_ref[...], dims, preferred_element_type=F32)

        @pl.when(k == n_k - 1)
        def _():
            dx, dg = _rms_bwd(x_ref[...], g_ref[...], acc[...])
            dx_ref[...] = r_ref[...] + dx

            @pl.when(i == 0)
            def _():
                dg_ref[...] = jnp.zeros_like(dg_ref)

            dg_ref[...] += dg

    row = pl.BlockSpec((tm, d), lambda i, k: (i, 0))
    if first is None:
        a_specs, a_args = [pl.BlockSpec((tm, tk), lambda i, k: (i, k))], [dproj]
    else:
        a_specs = [pl.BlockSpec((tm, tk), lambda i, k: (i, 0)), pl.BlockSpec((tm, tk), lambda i, k: (i, jnp.maximum(k - 1, 0)))]
        a_args = [first, dproj]
    return _call(
        body, name=name, grid=(s // tm, n_k),
        in_specs=a_specs + [pl.BlockSpec((d, tk), lambda i, k: (0, k)), row, _full((1, d)), row],
        out_specs=[row, _full((1, d))],
        out_shape=[jax.ShapeDtypeStruct((s, d), F32), jax.ShapeDtypeStruct((1, d), F32)],
        scratch_shapes=[pltpu.VMEM((tm, d), F32)], args=a_args + [w, x, g, dres], exchange=exchange)


def _rg_conv(xa, before, after, cw, cb):
    return (cw[0:1, :] * _shift_rows(xa, before, after, -2) + cw[1:2, :] * _shift_rows(xa, before, after, -1)
            + cw[2:3, :] * xa + cw[3:4, :] * _shift_rows(xa, before, after, 1) + cb)


def _rg_gates(ua_h, gw_ref, gb_ref, c_h, direction, head):
    r = _sigmoid(_bdot(ua_h, gw_ref[2 * direction, head]) + gb_ref[2 * direction, head:head + 1, :])
    i = _sigmoid(_bdot(ua_h, gw_ref[2 * direction + 1, head]) + gb_ref[2 * direction + 1, head:head + 1, :])
    log_a = -c_h * r
    a = jnp.exp(log_a)
    beta_sq = -jnp.tanh(log_a) * (1.0 + a * a)
    inv_beta = lax.rsqrt(jnp.maximum(beta_sq, SMALLEST_NORMAL))
    return r, i, a, beta_sq * inv_beta, inv_beta


def even_gates_fwd(proj, conv_w, conv_b, gate_w, gate_b, lam, exchange=None):
    s = proj.shape[0]
    ts = min(2 * ROW_TILE, s)
    n_tiles = s // ts

    def body(xa_ref, xb_ref, xn_ref, cw_ref, cb_ref, gw_ref, gb_ref, lam_ref, o_ref, hf_ref, carry):
        @pl.when(pl.program_id(0) == 0)
        def _():
            carry[...] = jnp.zeros_like(carry)

        xa, before, after = _halo_load(xa_ref, xb_ref, xn_ref, n_tiles)
        ua = _rg_conv(xa, before, after, cw_ref[...], cb_ref[...])
        c = RG_C * _softplus(-lam_ref[...])
        ua_bf16 = ua.astype(BF16)
        for direction in range(2):
            for head in range(RG_HEADS):
                lanes = slice(head * RG_HEAD_DIM, (head + 1) * RG_HEAD_DIM)
                ua_h = ua[:, lanes]
                _, i, a, beta, _ = _rg_gates(ua_bf16[:, lanes], gw_ref, gb_ref, c[direction:direction + 1, lanes], direction, head)
                o_ref[2 * direction, :, lanes] = a
                o_ref[2 * direction + 1, :, lanes] = beta * (i * ua_h)
        _scan_tile(o_ref.at[0], o_ref.at[1], hf_ref, carry, False, False)

    return _call(
        body, name="even_gates_fwd", grid=(n_tiles,),
        in_specs=_halo_specs(ts, s, D_MODEL, 0) + [_full(conv_w.shape), _full(conv_b.shape), _full(gate_w.shape),
                                                   _full(gate_b.shape), _full(lam.shape)],
        out_specs=[pl.BlockSpec((4, ts, D_MODEL), lambda i: (0, i, 0)), pl.BlockSpec((ts, D_MODEL), lambda i: (i, 0))],
        out_shape=[jax.ShapeDtypeStruct((4, s, D_MODEL), F32), jax.ShapeDtypeStruct((s, D_MODEL), F32)],
        scratch_shapes=[pltpu.VMEM((SUBLANES, D_MODEL), F32)],
        args=[proj, proj, proj, conv_w, conv_b, gate_w, gate_b, lam], exchange=exchange)


def _scan_tile(a_ref, b_ref, h_ref, carry, reverse, b_times_a):
    ts, c = h_ref.shape
    n_blocks = ts // SUBLANES
    row = lax.broadcasted_iota(jnp.int32, (SUBLANES, c), 0)

    def block(j, h_in):
        r0 = pl.multiple_of((n_blocks - 1 - j if reverse else j) * SUBLANES, SUBLANES)
        a = a_ref[pl.ds(r0, SUBLANES), :]
        b = b_ref[pl.ds(r0, SUBLANES), :]
        if b_times_a:
            b = a * b
        for step in (1, 2, 4):
            shift = SUBLANES - step if reverse else step
            valid = row < SUBLANES - step if reverse else row >= step
            b = jnp.where(valid, a * pltpu.roll(b, shift, 0) + b, b)
            a = jnp.where(valid, a * pltpu.roll(a, shift, 0), a)
        h = a * h_in + b
        h_ref[pl.ds(r0, SUBLANES), :] = h
        return h[0:1, :] if reverse else h[SUBLANES - 1:SUBLANES, :]

    carry[0:1, :] = lax.fori_loop(0, n_blocks, block, carry[0:1, :])


def linear_scan(a_arr, a_idx, b_arr, b_idx, reverse, b_times_a, name, exchange=None):
    _, s, c = a_arr.shape
    ts = min(MM_TILE, s)
    n_tiles = s // ts

    def tile_of(i):
        return n_tiles - 1 - i if reverse else i

    def body(a_ref, b_ref, h_ref, carry):
        @pl.when(pl.program_id(0) == 0)
        def _():
            carry[...] = jnp.zeros_like(carry)

        _scan_tile(a_ref, b_ref, h_ref, carry, reverse, b_times_a)

    return _call(
        body, name=name, grid=(n_tiles,),
        in_specs=[pl.BlockSpec((None, ts, c), lambda i: (a_idx, tile_of(i), 0)),
                  pl.BlockSpec((None, ts, c), lambda i: (b_idx, tile_of(i), 0))],
        out_specs=pl.BlockSpec((ts, c), lambda i: (tile_of(i), 0)),
        out_shape=jax.ShapeDtypeStruct((s, c), F32),
        scratch_shapes=[pltpu.VMEM((SUBLANES, c), F32)], args=[a_arr, b_arr], exchange=exchange)


def _sc_conv(p, before, after, w):
    return w[0:1, :] * _shift_rows(p, before, after, -1) + w[1:2, :] * p + w[2:3, :] * _shift_rows(p, before, after, 1)


def even_mix_fwd(ab, hf, proj, sc_w, w_out, xres, g_post, exchange=None):
    s = proj.shape[0]
    ts = min(ROW_TILE, s)
    n_tiles = s // ts

    def tile(i):
        return n_tiles - 1 - i

    row = pl.BlockSpec((ts, D_MODEL), lambda i: (tile(i), 0))

    def col(c):
        return pl.BlockSpec((ts, D_MODEL), lambda i: (tile(i), c))

    def body(a_ref, b_ref, hf_ref, za_ref, xb_ref, xbb_ref, xbn_ref, gb_ref, gc_ref, gcb_ref, gcn_ref, zb_ref, w_ref,
             wo_ref, x_ref, g_ref, u_ref, hb_ref, y_ref, out_ref, carry):
        @pl.when(pl.program_id(0) == 0)
        def _():
            carry[...] = jnp.zeros_like(carry)

        _scan_tile(a_ref, b_ref, hb_ref, carry, True, False)
        xb, xb_before, xb_after = _halo_load(xb_ref, xbb_ref, xbn_ref, n_tiles, tile)
        gc, gc_before, gc_after = _halo_load(gc_ref, gcb_ref, gcn_ref, n_tiles, tile)
        silu_za, _ = _silu_and_grad(za_ref[...])
        silu_zb, _ = _silu_and_grad(zb_ref[...])
        u_ref[:, :D_MODEL] = ((hf_ref[...] + hb_ref[...]) * silu_za).astype(BF16)
        cv = _sc_conv(gc * xb, gc_before * xb_before, gc_after * xb_after, w_ref[...])
        u_ref[:, D_MODEL:] = (gb_ref[...] * cv * silu_zb).astype(BF16)
        y = jnp.dot(u_ref[...], wo_ref[...], preferred_element_type=F32)
        y_ref[...] = y
        out_ref[...] = x_ref[...] + _rms(y, g_ref[...])

    return _call(
        body, name="even_mix_fwd", grid=(n_tiles,),
        in_specs=[pl.BlockSpec((None, ts, D_MODEL), lambda i: (2, tile(i), 0)), pl.BlockSpec((None, ts, D_MODEL), lambda i: (3, tile(i), 0)),
                  row, col(1)] + _halo_specs(ts, s, D_MODEL, 2, tile) + [col(3)] + _halo_specs(ts, s, D_MODEL, 4, tile)
        + [col(5), _full(sc_w.shape), _full(w_out.shape), row, _full(g_post.shape)],
        out_specs=[pl.BlockSpec((ts, 2 * D_MODEL), lambda i: (tile(i), 0)), row, row, row],
        out_shape=[jax.ShapeDtypeStruct((s, 2 * D_MODEL), BF16)] + [jax.ShapeDtypeStruct((s, D_MODEL), F32)] * 3,
        scratch_shapes=[pltpu.VMEM((SUBLANES, D_MODEL), F32)],
        args=[ab, ab, hf, proj, proj, proj, proj, proj, proj, proj, proj, proj, sc_w, w_out, xres, g_post], exchange=exchange)


def even_mix_bwd(du, hf, hb, proj, sc_w, ab, exchange=None):
    s = proj.shape[0]
    ts = min(ROW_TILE, s)
    n_tiles = s // ts
    row = pl.BlockSpec((ts, D_MODEL), lambda i: (i, 0))

    def body(dya_ref, dyb_ref, dybb_ref, dybn_ref, hf_ref, hb_ref, za_ref, xb_ref, xbb_ref, xbn_ref,
             gb_ref, gbb_ref, gbn_ref, gc_ref, gcb_ref, gcn_ref, zb_ref, zbb_ref, zbn_ref, w_ref, a_ref,
             dh_ref, dp_ref, dw_ref, adj_ref, carry):
        @pl.when(pl.program_id(0) == 0)
        def _():
            carry[...] = jnp.zeros_like(carry)

        dyb, dyb_before, dyb_after = _halo_load(dyb_ref, dybb_ref, dybn_ref, n_tiles)
        xb, xb_before, xb_after = _halo_load(xb_ref, xbb_ref, xbn_ref, n_tiles)
        gb, gb_before, gb_after = _halo_load(gb_ref, gbb_ref, gbn_ref, n_tiles)
        gc, gc_before, gc_after = _halo_load(gc_ref, gcb_ref, gcn_ref, n_tiles)
        zb, zb_before, zb_after = _halo_load(zb_ref, zbb_ref, zbn_ref, n_tiles)
        w = w_ref[...]
        dya, za = dya_ref[...], za_ref[...]
        silu_za, dsilu_za = _silu_and_grad(za)
        dh_ref[...] = dya * silu_za
        _scan_tile(a_ref, dh_ref, adj_ref, carry, False, True)
        dp_ref[:, 0:D_MODEL] = (dya * (hf_ref[...] + hb_ref[...]) * dsilu_za).astype(BF16)

        silu_zb, dsilu_zb = _silu_and_grad(zb)
        p, p_before, p_after = gc * xb, gc_before * xb_before, gc_after * xb_after
        cv = _sc_conv(p, p_before, p_after, w)
        dcv = dyb * gb * silu_zb
        dcv_before = dyb_before * gb_before * _silu_and_grad(zb_before)[0]
        dcv_after = dyb_after * gb_after * _silu_and_grad(zb_after)[0]
        dpp = (w[0:1, :] * _shift_rows(dcv, dcv_before, dcv_after, 1) + w[1:2, :] * dcv
               + w[2:3, :] * _shift_rows(dcv, dcv_before, dcv_after, -1))
        dp_ref[:, D_MODEL:2 * D_MODEL] = (dpp * gc).astype(BF16)
        dp_ref[:, 2 * D_MODEL:3 * D_MODEL] = (dyb * cv * silu_zb).astype(BF16)
        dp_ref[:, 3 * D_MODEL:4 * D_MODEL] = (dpp * xb).astype(BF16)
        dp_ref[:, 4 * D_MODEL:5 * D_MODEL] = (dyb * gb * cv * dsilu_zb).astype(BF16)

        @pl.when(pl.program_id(0) == 0)
        def _():
            dw_ref[...] = jnp.zeros_like(dw_ref)

        dw_ref[0:1, :] += jnp.sum(dcv * _shift_rows(p, p_before, p_after, -1), axis=0, keepdims=True)
        dw_ref[1:2, :] += jnp.sum(dcv * p, axis=0, keepdims=True)
        dw_ref[2:3, :] += jnp.sum(dcv * _shift_rows(p, p_before, p_after, 1), axis=0, keepdims=True)

    return _call(
        body, name="even_mix_bwd", grid=(n_tiles,),
        in_specs=[row] + _halo_specs(ts, s, D_MODEL, 1) + [row, row, pl.BlockSpec((ts, D_MODEL), lambda i: (i, 1))]
        + _halo_specs(ts, s, D_MODEL, 2) + _halo_specs(ts, s, D_MODEL, 3) + _halo_specs(ts, s, D_MODEL, 4)
        + _halo_specs(ts, s, D_MODEL, 5) + [_full(sc_w.shape), pl.BlockSpec((None, ts, D_MODEL), lambda i: (2, i, 0))],
        out_specs=[row, pl.BlockSpec((ts, 5 * D_MODEL), lambda i: (i, 0)), _full(sc_w.shape), row],
        out_shape=[jax.ShapeDtypeStruct((s, D_MODEL), F32), jax.ShapeDtypeStruct((s, 5 * D_MODEL), BF16),
                   jax.ShapeDtypeStruct(sc_w.shape, F32), jax.ShapeDtypeStruct((s, D_MODEL), F32)],
        scratch_shapes=[pltpu.VMEM((SUBLANES, D_MODEL), F32)],
        args=[du, du, du, du, hf, hb, proj, *([proj] * 12), sc_w, ab], exchange=exchange)


def even_gates_bwd(proj, adj_f, adj_b, hf, hb, dh, conv_w, conv_b, gate_w, gate_b, lam, exchange=None):
    s = proj.shape[0]
    ts = min(2 * ROW_TILE, s)
    n_tiles = s // ts
    row = pl.BlockSpec((ts, D_MODEL), lambda i: (i, 0))

    def body(xa_ref, xab_ref, xan_ref, af_ref, afb_ref, afn_ref, ab_ref, abb_ref, abn_ref,
             hf_ref, hfb_ref, hfn_ref, hb_ref, hbb_ref, hbn_ref, dh_ref,
             cw_ref, cb_ref, gw_ref, gb_ref, lam_ref, dua_ref, dgw_ref, dgb_ref, dlam_ref):
        @pl.when(pl.program_id(0) == 0)
        def _():
            dgw_ref[...] = jnp.zeros_like(dgw_ref)
            dgb_ref[...] = jnp.zeros_like(dgb_ref)
            dlam_ref[...] = jnp.zeros_like(dlam_ref)

        xa, before, after = _halo_load(xa_ref, xab_ref, xan_ref, n_tiles)
        ua = _rg_conv(xa, before, after, cw_ref[...], cb_ref[...])
        lam_v = lam_ref[...]
        c = RG_C * _softplus(-lam_v)
        dc_dlam = -RG_C * _sigmoid(-lam_v)
        dh = dh_ref[...]
        adj = (_halo_load(af_ref, afb_ref, afn_ref, n_tiles), _halo_load(ab_ref, abb_ref, abn_ref, n_tiles))
        hs = (_halo_load(hf_ref, hfb_ref, hfn_ref, n_tiles), _halo_load(hb_ref, hbb_ref, hbn_ref, n_tiles))
        dua = jnp.zeros_like(ua)
        ua_bf16 = ua.astype(BF16)
        for direction in range(2):
            step = 1 if direction == 0 else -1
            g = dh + _shift_rows(*adj[direction], step)
            da_all = g * _shift_rows(*hs[direction], -step)
            dua_parts = []
            for head in range(RG_HEADS):
                lanes = slice(head * RG_HEAD_DIM, (head + 1) * RG_HEAD_DIM)
                ua_h = ua[:, lanes]
                c_h = c[direction:direction + 1, lanes]
                ua_hb = ua_bf16[:, lanes]
                r, i, a, beta, inv_beta = _rg_gates(ua_hb, gw_ref, gb_ref, c_h, direction, head)
                db_beta = g[:, lanes] * beta
                d_i = db_beta * ua_h
                dbeta = g[:, lanes] * (i * ua_h)
                dlog_a = (da_all[:, lanes] - dbeta * a * inv_beta) * a
                dpr = -c_h * dlog_a * r * (1.0 - r)
                dpi = d_i * i * (1.0 - i)
                dpr_b, dpi_b = dpr.astype(BF16), dpi.astype(BF16)
                dua_parts.append(db_beta * i + _bdot_nt(dpr_b, gw_ref[2 * direction, head])
                                 + _bdot_nt(dpi_b, gw_ref[2 * direction + 1, head]))
                dgw_ref[2 * direction, head] += _bdot_tn(ua_hb, dpr_b)
                dgw_ref[2 * direction + 1, head] += _bdot_tn(ua_hb, dpi_b)
                dgb_ref[2 * direction, head:head + 1, :] += jnp.sum(dpr, axis=0, keepdims=True)
                dgb_ref[2 * direction + 1, head:head + 1, :] += jnp.sum(dpi, axis=0, keepdims=True)
                dlam_ref[direction:direction + 1, lanes] += (
                    jnp.sum(-r * dlog_a, axis=0, keepdims=True) * dc_dlam[direction:direction + 1, lanes])
            dua = dua + jnp.concatenate(dua_parts, axis=1)
        dua_ref[...] = dua

    return _call(
        body, name="even_gates_bwd", grid=(n_tiles,),
        in_specs=_halo_specs(ts, s, D_MODEL, 0) * 5 + [row] + [_full(conv_w.shape), _full(conv_b.shape), _full(gate_w.shape),
                                                             _full(gate_b.shape), _full(lam.shape)],
        out_specs=[row, _full(gate_w.shape), _full(gate_b.shape), _full(lam.shape)],
        out_shape=[jax.ShapeDtypeStruct((s, D_MODEL), F32), jax.ShapeDtypeStruct(gate_w.shape, F32),
                   jax.ShapeDtypeStruct(gate_b.shape, F32), jax.ShapeDtypeStruct(lam.shape, F32)],
        args=[proj, proj, proj, adj_f, adj_f, adj_f, adj_b, adj_b, adj_b, hf, hf, hf, hb, hb, hb, dh, conv_w, conv_b, gate_w,
              gate_b, lam], exchange=exchange)


def rg_conv_bwd(dua, proj, conv_w, exchange=None):
    s = proj.shape[0]
    ts = min(2 * ROW_TILE, s)
    n_tiles = s // ts

    def body(du_ref, dub_ref, dun_ref, xa_ref, xab_ref, xan_ref, cw_ref, dp_ref, dw_ref, db_ref):
        @pl.when(pl.program_id(0) == 0)
        def _():
            dw_ref[...] = jnp.zeros_like(dw_ref)
            db_ref[...] = jnp.zeros_like(db_ref)

        dua, dua_before, dua_after = _halo_load(du_ref, dub_ref, dun_ref, n_tiles)
        xa, xa_before, xa_after = _halo_load(xa_ref, xab_ref, xan_ref, n_tiles)
        cw = cw_ref[...]
        dxa = (cw[0:1, :] * _shift_rows(dua, dua_before, dua_after, 2) + cw[1:2, :] * _shift_rows(dua, dua_before, dua_after, 1)
               + cw[2:3, :] * dua + cw[3:4, :] * _shift_rows(dua, dua_before, dua_after, -1))
        dp_ref[...] = dxa.astype(BF16)
        for tap, offset in enumerate((-2, -1, 0, 1)):
            shifted = xa if offset == 0 else _shift_rows(xa, xa_before, xa_after, offset)
            dw_ref[tap:tap + 1, :] += jnp.sum(dua * shifted, axis=0, keepdims=True)
        db_ref[...] += jnp.sum(dua, axis=0, keepdims=True)

    return _call(
        body, name="rg_conv_bwd", grid=(n_tiles,),
        in_specs=_halo_specs(ts, s, D_MODEL, 0) * 2 + [_full(conv_w.shape)],
        out_specs=[pl.BlockSpec((ts, D_MODEL), lambda i: (i, 0)), _full(conv_w.shape), _full((1, D_MODEL))],
        out_shape=[jax.ShapeDtypeStruct((s, D_MODEL), BF16), jax.ShapeDtypeStruct(conv_w.shape, F32),
                   jax.ShapeDtypeStruct((1, D_MODEL), F32)],
        args=[dua, dua, dua, proj, proj, proj, conv_w], exchange=exchange)


def _split3(x):
    x1 = x.astype(BF16)
    rest = x - x1.astype(F32)
    x2 = rest.astype(BF16)
    return x1, x2, (rest - x2.astype(F32)).astype(BF16)


def _chunk_sum_matrix(t, reverse, transpose):
    i = lax.broadcasted_iota(jnp.int32, (t, t), 0)
    j = lax.broadcasted_iota(jnp.int32, (t, t), 1)
    if transpose:
        i, j = j, i
    same = (i // GLA_CHUNK) == (j // GLA_CHUNK)
    return jnp.where(same & ((j >= i) if reverse else (j <= i)), 1.0, 0.0).astype(BF16)


def _exact_dot(m, x):
    return sum(jnp.dot(m, part, preferred_element_type=F32) for part in _split3(x))


def _chunk_mask(t, reverse):
    i = lax.broadcasted_iota(jnp.int32, (t, t), 0)
    j = lax.broadcasted_iota(jnp.int32, (t, t), 1)
    return ((i // GLA_CHUNK) == (j // GLA_CHUNK)) & ((j >= i) if reverse else (j <= i))


def _chunk_rows(c):
    return slice(c * GLA_CHUNK, (c + 1) * GLA_CHUNK)


def _gla_gate(lr, wg, bg):
    z = _bdot(lr, wg) + bg
    log_alpha = (jnp.minimum(z, 0.0) - jnp.log(1.0 + jnp.exp(-jnp.abs(z)))) * (1.0 / GLA_NORMALIZER)
    return z, log_alpha


def _gla_tile_terms(q, k, bcum, reverse):
    n_chunks = q.shape[0] // GLA_CHUNK
    totals = []
    for c in range(n_chunks):
        edge = c * GLA_CHUNK if reverse else (c + 1) * GLA_CHUNK - 1
        totals.append(bcum[edge:edge + 1, :])
    btot = jnp.concatenate([jnp.broadcast_to(total, (GLA_CHUNK, total.shape[1])) for total in totals], axis=0)
    e_pos, e_neg, e_st = jnp.exp(bcum), jnp.exp(-bcum), jnp.exp(btot - bcum)
    return q * (GLA_DK ** -0.5) * e_pos, k * e_neg, k * e_st, e_pos, e_neg, e_st, [jnp.exp(total) for total in totals]


def _gla_specs(t, n_tiles, reverse_order):
    def tile(i):
        return n_tiles - 1 - i if reverse_order else i

    return tile, [
        pl.BlockSpec((t, GLA_KEY), lambda i: (tile(i), 0)),
        pl.BlockSpec((t, GLA_KEY), lambda i: (tile(i), 1)),
        pl.BlockSpec((t, D_MODEL), lambda i: (tile(i), 1)),
        pl.BlockSpec((t, LANES), lambda i: (tile(i), (ODD_IN_PAD - LANES) // LANES)),
    ]


def gla_fwd(proj, wg, bg, reverse, o_other=None, gnorm=None, post=None):
    s = proj.shape[0]
    t = min(GLA_TILE, s)
    n_tiles = s // t
    n_chunks = t // GLA_CHUNK
    final = o_other is not None
    tile, specs = _gla_specs(t, n_tiles, reverse)

    def body(*refs):
        if final:
            (q_ref, k_ref, v_ref, lr_ref, wg_ref, bg_ref, oo_ref, r_ref, gn_ref, wo_ref, x_ref, gp_ref, t_ref,
             osum_ref, u_ref, st_ref, y_ref, dout_ref, loss_ref, state) = refs
        else:
            q_ref, k_ref, v_ref, lr_ref, wg_ref, bg_ref, o_ref, st_ref, state = refs
            osum_ref = o_ref

        @pl.when(pl.program_id(0) == 0)
        def _():
            state[...] = jnp.zeros_like(state)

        _, log_alpha = _gla_gate(lr_ref[...], wg_ref[...], bg_ref[...])
        bcum = _exact_dot(_chunk_sum_matrix(t, reverse, False), log_alpha)
        q, k, v = q_ref[...], k_ref[...], v_ref[...]
        q_in, k_in, k_st, _, _, _, decays = _gla_tile_terms(q, k, bcum, reverse)
        mask = _chunk_mask(t, reverse)
        order = list(range(n_chunks))[::-1] if reverse else list(range(n_chunks))
        intra, increments = [], []
        for head in range(GLA_HEADS):
            kl = slice(head * GLA_DK, (head + 1) * GLA_DK)
            vl = slice(head * GLA_DV, (head + 1) * GLA_DV)
            scores = jnp.where(mask, _bdot_nt(q_in[:, kl], k_in[:, kl]), 0.0)
            intra.append(_bdot(scores, v[:, vl]))
            increments.append([_bdot_tn(v[_chunk_rows(c), vl], k_st[_chunk_rows(c), kl]) for c in range(n_chunks)])
        for head in range(GLA_HEADS):
            kl = slice(head * GLA_DK, (head + 1) * GLA_DK)
            vl = slice(head * GLA_DV, (head + 1) * GLA_DV)
            running = state[head]
            before = [None] * n_chunks
            for c in order:
                before[c] = running
                st_ref[c, head] = running
                running = running * decays[c][:, kl] + increments[head][c]
            state[head] = running
            inter = [_bdot_nt(q_in[_chunk_rows(c), kl], before[c]) for c in range(n_chunks)]
            osum_ref[:, vl] = intra[head] + jnp.concatenate(inter, axis=0)
        if final:
            osum = osum_ref[...] + oo_ref[...]
            osum_ref[...] = osum
            silu_r, _ = _silu_and_grad(r_ref[...])
            gn = gn_ref[...]
            for head in range(GLA_HEADS):
                vl = slice(head * GLA_DV, (head + 1) * GLA_DV)
                u_ref[:, vl] = (_rms(osum[:, vl], gn[:, vl]) * silu_r[:, vl]).astype(BF16)

            @pl.when(pl.program_id(0) == 0)
            def _():
                loss_ref[...] = jnp.zeros_like(loss_ref)

            y = jnp.dot(u_ref[...], wo_ref[...], preferred_element_type=F32)
            y_ref[...] = y
            diff = x_ref[...] + _rms(y, gp_ref[...]) - t_ref[...]
            dout_ref[...] = diff * (1.0 / D_MODEL)
            loss_ref[...] += 0.5 * jnp.sum(jnp.mean(diff * diff, axis=-1, keepdims=True))

    row = pl.BlockSpec((t, D_MODEL), lambda i: (tile(i), 0))
    st_spec = pl.BlockSpec((n_chunks, GLA_HEADS, GLA_DV, GLA_DK), lambda i: (tile(i), 0, 0, 0))
    st_shape = jax.ShapeDtypeStruct((s // GLA_CHUNK, GLA_HEADS, GLA_DV, GLA_DK), F32)
    in_specs = specs + [_full(wg.shape), _full(bg.shape)]
    args = [proj, proj, proj, proj, wg, bg]
    if final:
        w_out, xres, g_post, target = post
        in_specs += [row, pl.BlockSpec((t, D_MODEL), lambda i: (tile(i), 2)), _full(gnorm.shape), _full(w_out.shape), row,
                     _full(g_post.shape), row]
        args += [o_other, proj, gnorm, w_out, xres, g_post, target]
        out_specs = [row, row, st_spec, row, row, _full((SUBLANES, LANES))]
        out_shape = [jax.ShapeDtypeStruct((s, D_MODEL), F32), jax.ShapeDtypeStruct((s, D_MODEL), BF16), st_shape,
                     jax.ShapeDtypeStruct((s, D_MODEL), F32), jax.ShapeDtypeStruct((s, D_MODEL), F32),
                     jax.ShapeDtypeStruct((SUBLANES, LANES), F32)]
    else:
        out_specs = [row, st_spec]
        out_shape = [jax.ShapeDtypeStruct((s, D_MODEL), F32), st_shape]
    return pl.pallas_call(
        body, name="gla_fwd_rev" if reverse else "gla_fwd", grid=(n_tiles,), in_specs=in_specs, out_specs=out_specs,
        out_shape=out_shape, scratch_shapes=[pltpu.VMEM((GLA_HEADS, GLA_DV, GLA_DK), F32)], compiler_params=_params(1),
    )(*args)


def gla_bwd(proj, wg, bg, do, states, reverse, first=None):
    s = proj.shape[0]
    t = min(GLA_TILE, s)
    n_tiles = s // t
    n_chunks = t // GLA_CHUNK
    final = first is not None
    tile, specs = _gla_specs(t, n_tiles, not reverse)

    def body(*refs):
        if final:
            (q_ref, k_ref, v_ref, lr_ref, wg_ref, bg_ref, do_ref, st_ref, dqkv1_ref, dlr1_ref, dr_ref,
             dp_ref, dwg_ref, dbg_ref, dstate, dqkv, dbc, dbt) = refs
        else:
            (q_ref, k_ref, v_ref, lr_ref, wg_ref, bg_ref, do_ref, st_ref,
             dqkv, dlr_ref, dwg_ref, dbg_ref, dstate, dbc, dbt) = refs

        @pl.when(pl.program_id(0) == 0)
        def _():
            dstate[...] = jnp.zeros_like(dstate)
            dwg_ref[...] = jnp.zeros_like(dwg_ref)
            dbg_ref[...] = jnp.zeros_like(dbg_ref)

        lr, wg_v = lr_ref[...], wg_ref[...]
        z, log_alpha = _gla_gate(lr, wg_v, bg_ref[...])
        bcum = _exact_dot(_chunk_sum_matrix(t, reverse, False), log_alpha)
        q, k, v, do_v = q_ref[...], k_ref[...], v_ref[...], do_ref[...]
        q_in, k_in, k_st, e_pos, e_neg, e_st, decays = _gla_tile_terms(q, k, bcum, reverse)
        mask = _chunk_mask(t, reverse)
        order = list(range(n_chunks)) if reverse else list(range(n_chunks))[::-1]
        q_b, k_b, ks_b, v_b, do_b = (a.astype(BF16) for a in (q_in, k_in, k_st, v, do_v))
        dq_intra, dk_intra, dv_intra, increments = [], [], [], []
        for head in range(GLA_HEADS):
            kl = slice(head * GLA_DK, (head + 1) * GLA_DK)
            vl = slice(head * GLA_DV, (head + 1) * GLA_DV)
            scores = jnp.where(mask, _bdot_nt(q_b[:, kl], k_b[:, kl]), 0.0).astype(BF16)
            dscores = jnp.where(mask, _bdot_nt(do_b[:, vl], v_b[:, vl]), 0.0).astype(BF16)
            dv_intra.append(_bdot_tn(scores, do_b[:, vl]))
            dq_intra.append(_bdot(dscores, k_b[:, kl]))
            dk_intra.append(_bdot_tn(dscores, q_b[:, kl]))
            increments.append([_bdot_tn(do_b[_chunk_rows(c), vl], q_b[_chunk_rows(c), kl]) for c in range(n_chunks)])
        after_all, ddecay_all = [], []
        for head in range(GLA_HEADS):
            kl = slice(head * GLA_DK, (head + 1) * GLA_DK)
            running = dstate[head]
            after, ddecay = [None] * n_chunks, [None] * n_chunks
            for c in order:
                after[c] = running
                ddecay[c] = jnp.sum(running * st_ref[c, head], axis=0, keepdims=True)
                running = running * decays[c][:, kl] + increments[head][c]
            dstate[head] = running
            after_all.append(after)
            ddecay_all.append(ddecay)
        for head in range(GLA_HEADS):
            kl = slice(head * GLA_DK, (head + 1) * GLA_DK)
            vl = slice(head * GLA_DV, (head + 1) * GLA_DV)
            after, ddecay = after_all[head], ddecay_all[head]
            dq_inter = jnp.concatenate([_bdot(do_b[_chunk_rows(c), vl], st_ref[c, head]) for c in range(n_chunks)], axis=0)
            dv_inter = jnp.concatenate([_bdot_nt(ks_b[_chunk_rows(c), kl], after[c]) for c in range(n_chunks)], axis=0)
            dk_st = jnp.concatenate([_bdot(v_b[_chunk_rows(c), vl], after[c]) for c in range(n_chunks)], axis=0)
            dq_in = dq_intra[head] + dq_inter
            ks_h = k_st[:, kl]
            dqkv[:, 2 * GLA_KEY + head * GLA_DV:2 * GLA_KEY + (head + 1) * GLA_DV] = dv_intra[head] + dv_inter
            dqkv[:, kl] = dq_in * (GLA_DK ** -0.5) * e_pos[:, kl]
            dqkv[:, GLA_KEY + head * GLA_DK:GLA_KEY + (head + 1) * GLA_DK] = dk_intra[head] * e_neg[:, kl] + dk_st * e_st[:, kl]
            dbc[:, kl] = dq_in * q_in[:, kl] - dk_intra[head] * k_in[:, kl] - dk_st * ks_h
            weighted = dk_st * ks_h
            for c in range(n_chunks):
                dbtot = jnp.sum(weighted[_chunk_rows(c)], axis=0, keepdims=True) + ddecay[c] * decays[c][:, kl]
                dbt[_chunk_rows(c), kl] = jnp.broadcast_to(dbtot, (GLA_CHUNK, GLA_DK))
        dlog_alpha = _exact_dot(_chunk_sum_matrix(t, reverse, True), dbc[...]) + dbt[...]
        dz = dlog_alpha * _sigmoid(-z) * (1.0 / GLA_NORMALIZER)
        dlr = _bdot_nt(dz, wg_v)
        dwg_ref[...] += _bdot_tn(lr, dz)
        dbg_ref[...] += jnp.sum(dz, axis=0, keepdims=True)
        if final:
            dp_ref[:, :2 * D_MODEL] = (dqkv[...] + dqkv1_ref[...]).astype(BF16)
            dp_ref[:, 2 * D_MODEL:3 * D_MODEL] = dr_ref[...]
            dp_ref[:, 3 * D_MODEL:] = (dlr + dlr1_ref[...]).astype(BF16)
        else:
            dlr_ref[...] = dlr

    row = pl.BlockSpec((t, D_MODEL), lambda i: (tile(i), 0))
    wide = pl.BlockSpec((t, 2 * D_MODEL), lambda i: (tile(i), 0))
    narrow = pl.BlockSpec((t, LANES), lambda i: (tile(i), 0))
    st_spec = pl.BlockSpec((n_chunks, GLA_HEADS, GLA_DV, GLA_DK), lambda i: (tile(i), 0, 0, 0))
    in_specs = specs + [_full(wg.shape), _full(bg.shape), row, st_spec]
    args = [proj, proj, proj, proj, wg, bg, do, states]
    acc_specs = [_full(wg.shape), _full(bg.shape)]
    acc_shapes = [jax.ShapeDtypeStruct(wg.shape, F32), jax.ShapeDtypeStruct(bg.shape, F32)]
    scratch = [pltpu.VMEM((GLA_HEADS, GLA_DV, GLA_DK), F32)]
    work = [pltpu.VMEM((t, GLA_KEY), F32), pltpu.VMEM((t, GLA_KEY), F32)]
    if final:
        in_specs += [wide, narrow, row]
        args += list(first)
        out_specs = [pl.BlockSpec((t, ODD_IN_PAD), lambda i: (tile(i), 0))] + acc_specs
        out_shape = [jax.ShapeDtypeStruct((s, ODD_IN_PAD), BF16)] + acc_shapes
        scratch += [pltpu.VMEM((t, 2 * D_MODEL), F32)] + work
    else:
        out_specs = [wide, narrow] + acc_specs
        out_shape = [jax.ShapeDtypeStruct((s, 2 * D_MODEL), F32), jax.ShapeDtypeStruct((s, LANES), F32)] + acc_shapes
        scratch += work
    return pl.pallas_call(
        body, name="gla_bwd_rev" if reverse else "gla_bwd", grid=(n_tiles,), in_specs=in_specs, out_specs=out_specs,
        out_shape=out_shape, scratch_shapes=scratch, compiler_params=_params(1),
    )(*args)


def column_blocks(a, width):
    r, c = a.shape
    window = -(-(width + LANES) // LANES) * LANES
    padded = -(-width // LANES) * LANES
    assert window <= c

    def body(a_ref, o_ref):
        row = lax.broadcasted_iota(jnp.int32, (window, padded), 0)
        col = lax.broadcasted_iota(jnp.int32, (window, padded), 1)
        for j in range(N_DEV):
            start = min(j * width // LANES * LANES, c - window)
            pick = jnp.where((row == col + (j * width - start)) & (col < width), 1.0, 0.0).astype(BF16)
            picked = jnp.dot(a_ref[:, start:start + window], pick, preferred_element_type=F32)
            o_ref[j] = picked[:, :width].astype(o_ref.dtype)

    return pl.pallas_call(
        body, name="column_blocks", grid=(1,), in_specs=[_full((r, c))], out_specs=_full((N_DEV, r, width)),
        out_shape=jax.ShapeDtypeStruct((N_DEV, r, width), a.dtype), compiler_params=_params(1),
    )(a)


def columns_from_blocks(parts, total):
    width = parts[0].shape[2]
    rows = [p.shape[1] for p in parts]
    window = -(-(width + LANES) // LANES) * LANES

    def body(*refs):
        part_refs, o_ref, acc = refs[:len(parts)], refs[len(parts)], refs[len(parts) + 1]
        acc[...] = jnp.zeros_like(acc)
        row = lax.broadcasted_iota(jnp.int32, (width, window), 0)
        col = lax.broadcasted_iota(jnp.int32, (width, window), 1)
        for j in range(N_DEV):
            start = min(j * width // LANES * LANES, total - window)
            place = jnp.where(col == row + (j * width - start), 1.0, 0.0).astype(BF16)
            at = 0
            for part_ref, r in zip(part_refs, rows):
                acc[at:at + r, start:start + window] += jnp.dot(part_ref[j], place, preferred_element_type=F32)
                at += r
        o_ref[...] = acc[...].astype(o_ref.dtype)

    return pl.pallas_call(
        body, name="columns_from_blocks", grid=(1,), in_specs=[_full(p.shape) for p in parts], out_specs=_full((sum(rows), total)),
        out_shape=jax.ShapeDtypeStruct((sum(rows), total), parts[0].dtype), scratch_shapes=[pltpu.VMEM((sum(rows), total), F32)],
        compiler_params=_params(1),
    )(*parts)


def pair_sum(grad, from_sibling):
    n_chips, r, w = from_sibling.shape

    def body(even_ref, odd_ref, sib_ref, o_ref):
        mine = jnp.where(lax.axis_index("c") == 1, odd_ref[...], even_ref[...])
        o_ref[...] = (mine.astype(F32) + sib_ref[...].astype(F32)).astype(o_ref.dtype)

    return pl.pallas_call(
        body, name="pair_sum", grid=(n_chips,),
        in_specs=[pl.BlockSpec((r, w), lambda k: (0, 2 * k)), pl.BlockSpec((r, w), lambda k: (0, 2 * k + 1)),
                  pl.BlockSpec((None, r, w), lambda k: (k, 0, 0))],
        out_specs=pl.BlockSpec((None, r, w), lambda k: (k, 0, 0)),
        out_shape=jax.ShapeDtypeStruct(from_sibling.shape, from_sibling.dtype), compiler_params=_params(1),
    )(grad, grad, from_sibling)


def _adamw_update(g, w, m, v):
    new_m = ADAM_B1 * m + (1.0 - ADAM_B1) * g
    new_v = ADAM_B2 * v + (1.0 - ADAM_B2) * (g * g)
    m_hat = new_m / (1.0 - ADAM_B1 ** ADAM_STEP)
    v_hat = new_v / (1.0 - ADAM_B2 ** ADAM_STEP)
    return -ADAM_LR * (m_hat / (jnp.sqrt(v_hat) + ADAM_EPS) + ADAM_WD * w), new_m, new_v


def sum_parts(parts, name):
    _, r, c = parts.shape

    def body(p_ref, o_ref):
        total = p_ref[0].astype(F32)
        for j in range(1, N_DEV):
            total = total + p_ref[j].astype(F32)
        o_ref[...] = total

    return pl.pallas_call(body, name=name, in_specs=[_full(parts.shape)], out_specs=_full((r, c)), grid=(1,),
                          out_shape=jax.ShapeDtypeStruct((r, c), F32), compiler_params=_params(1))(parts)


def adamw(parts, w, m, v, name, exchange=None):
    n, r, c = parts.shape
    tr = r
    while tr * c * 4 > ADAMW_BLOCK_BYTES and tr % (2 * SUBLANES) == 0:
        tr //= 2

    def body(p_ref, w_ref, m_ref, v_ref, g_ref, d_ref, nm_ref, nv_ref):
        g = p_ref[0].astype(F32)
        for j in range(1, n):
            g = g + p_ref[j].astype(F32)
        g_ref[...] = g
        d_ref[...], nm_ref[...], nv_ref[...] = _adamw_update(g, w_ref[...], m_ref[...], v_ref[...])

    row = pl.BlockSpec((tr, c), lambda i: (i, 0))
    return _call(
        body, name=name, grid=(r // tr,),
        in_specs=[pl.BlockSpec((n, tr, c), lambda i: (0, i, 0)), row, row, row], out_specs=[row] * 4,
        out_shape=[jax.ShapeDtypeStruct((r, c), F32)] * 4, args=[parts, w, m, v], exchange=exchange)


def adamw_transposed(parts, w_t, m_t, v_t, name):
    n, r, c = parts.shape
    tr = min(MM_TILE, r)

    def body(p_ref, w_ref, m_ref, v_ref, g_ref, d_ref, nm_ref, nv_ref):
        eye = (lax.broadcasted_iota(jnp.int32, (tr, tr), 0) == lax.broadcasted_iota(jnp.int32, (tr, tr), 1)).astype(BF16)
        g = _bdot_tn(p_ref[0], eye)
        for j in range(1, n):
            g = g + _bdot_tn(p_ref[j], eye)
        g_ref[...] = g
        d_ref[...], nm_ref[...], nv_ref[...] = _adamw_update(g, w_ref[...], m_ref[...], v_ref[...])

    col = pl.BlockSpec((c, tr), lambda i: (0, i))
    return pl.pallas_call(
        body, name=name, grid=(r // tr,), in_specs=[pl.BlockSpec((n, tr, c), lambda i: (0, i, 0)), col, col, col], out_specs=[col] * 4,
        out_shape=[jax.ShapeDtypeStruct((c, r), F32)] * 4, compiler_params=_params(1),
    )(parts, w_t, m_t, v_t)


def _small_views(shape):
    if len(shape) == 2:
        return [((slice(None), slice(None)), (slice(None), slice(None)))]
    if len(shape) == 3:
        return [((slice(None), slice(None)), (0,))]
    rows = shape[2]
    return [((slice(k * rows, (k + 1) * rows), slice(None)), (0, k)) for k in range(shape[1])]


def adamw_small(landings, w, m, v):
    names = list(landings)
    n = len(names)
    shapes = [w[name].shape for name in names]

    def body(*refs):
        land, ws, ms, vs = refs[:n], refs[n:2 * n], refs[2 * n:3 * n], refs[3 * n:4 * n]
        outs = [refs[(4 + k) * n:(5 + k) * n] for k in range(4)]
        for k in range(n):
            total = land[k][0]
            for j in range(1, N_DEV):
                total = total + land[k][j]
            for rows, at in _small_views(shapes[k]):
                g = total[rows]
                outs[0][k][at] = g
                outs[1][k][at], outs[2][k][at], outs[3][k][at] = _adamw_update(g, ws[k][at], ms[k][at], vs[k][at])

    blocks = [_full(sh) for sh in shapes]
    outs = pl.pallas_call(
        body, name="adamw_small", grid=(1,),
        in_specs=[_full(landings[name].shape) for name in names] + blocks * 3, out_specs=blocks * 4,
        out_shape=[jax.ShapeDtypeStruct(sh, F32) for sh in shapes] * 4, compiler_params=_params(1),
    )(*[landings[name] for name in names], *[src[name] for src in (w, m, v) for name in names])
    return [dict(zip(names, outs[k * n:(k + 1) * n])) for k in range(4)]


def adamw_replicated(land_vec, land_gate_b, land_loss, names, w, m, v, gate_b):
    n = len(names)

    def body(*refs):
        vec_ref, gb_ref, loss_ref = refs[:3]
        ws, ms, vs = refs[3:3 + n], refs[3 + n:3 + 2 * n], refs[3 + 2 * n:3 + 3 * n]
        gw_ref, gm_ref, gv_ref = refs[3 + 3 * n:6 + 3 * n]
        outs = refs[6 + 3 * n:]
        vec, gb, loss = vec_ref[0], gb_ref[0], loss_ref[0]
        for j in range(1, N_DEV):
            vec, gb, loss = vec + vec_ref[j], gb + gb_ref[j], loss + loss_ref[j]
        for k in range(n):
            g = vec[k:k + 1, :]
            outs[k][...] = g
            outs[n + k][...], outs[2 * n + k][...], outs[3 * n + k][...] = _adamw_update(g, ws[k][...], ms[k][...], vs[k][...])
        outs[4 * n][...] = gb
        outs[4 * n + 1][...], outs[4 * n + 2][...], outs[4 * n + 3][...] = _adamw_update(gb, gw_ref[...], gm_ref[...], gv_ref[...])
        outs[4 * n + 4][...] = loss

    vec_block, gb_block = _full((1, D_MODEL)), _full(gate_b[0].shape)
    outs = pl.pallas_call(
        body, name="adamw_replicated", grid=(1,),
        in_specs=[_full(land_vec.shape), _full(land_gate_b.shape), _full(land_loss.shape)] + [vec_block] * (3 * n) + [gb_block] * 3,
        out_specs=[vec_block] * (4 * n) + [gb_block] * 4 + [_full(land_loss.shape[1:])],
        out_shape=[jax.ShapeDtypeStruct((1, D_MODEL), F32)] * (4 * n) + [jax.ShapeDtypeStruct(gate_b[0].shape, F32)] * 4
        + [jax.ShapeDtypeStruct(land_loss.shape[1:], F32)],
        compiler_params=_params(1),
    )(land_vec, land_gate_b, land_loss, *[src[name] for src in (w, m, v) for name in names], *gate_b)
    results = {name: [outs[k * n + i] for k in range(4)] for i, name in enumerate(names)}
    return results, outs[4 * n:4 * n + 4], outs[4 * n + 4]


SMALL_SHARDED = ("rg_conv_w", "rg_lambda", "sc_conv_w", "odd_norm_pre", "odd_norm_post", "gla_b_gate", "gla_norm_g", "gla_w_gate_lr")
SMALL_ROWS = {"rg_conv_w": (0, 4), "rg_lambda": (4, 2), "sc_conv_w": (6, 3), "odd_norm_pre": (9, 1), "odd_norm_post": (10, 1),
              "gla_b_gate": (11, 2), "gla_norm_g": (13, 1), "gla_w_gate_lr": (16, 32)}


def _pack_small(shards):
    pieces, at = [], 0
    for name in SMALL_SHARDED:
        start, rows = SMALL_ROWS[name]
        if start > at:
            pieces.append(jnp.zeros((start - at, LANES), F32))
        a = shards[name].reshape(rows, -1)
        pieces.append(jnp.pad(a, ((0, 0), (0, LANES - a.shape[1]))))
        at = start + rows
    return jnp.concatenate(pieces, axis=0)


def _unpack_gathered(g):
    def cols(name, width):
        start, rows = SMALL_ROWS[name]
        return jnp.transpose(g[:, start:start + rows, :width], (1, 0, 2)).reshape(rows, N_DEV * width)

    w_lr = cols("gla_w_gate_lr", GLA_KEY // N_DEV).reshape(2, GLA_RANK, GLA_KEY)
    return dict(rg_conv_w=cols("rg_conv_w", LANES), rg_lambda=cols("rg_lambda", LANES), sc_conv_w=cols("sc_conv_w", LANES),
                odd_norm_pre=cols("odd_norm_pre", LANES), odd_norm_post=cols("odd_norm_post", LANES),
                gla_b_gate=cols("gla_b_gate", GLA_KEY // N_DEV), gla_norm_g=cols("gla_norm_g", GLA_DV // N_DEV), gla_w_gate_lr=w_lr)


def _blocks_along_columns(a, rows):
    return jnp.transpose(a.reshape(rows, N_DEV, -1), (1, 0, 2))


def kernel(x, even_norm_pre, even_norm_post, even_w_in, rg_conv_w, rg_conv_b, rg_gate_w, rg_gate_b, rg_lambda, sc_conv_w, even_w_out, odd_norm_pre, odd_norm_post, odd_w_in, gla_w_gate_lr, gla_b_gate, gla_norm_g, odd_w_out, loss_target, m_even_norm_pre, m_even_norm_post, m_even_w_in, m_rg_conv_w, m_rg_conv_b, m_rg_gate_w, m_rg_gate_b, m_rg_lambda, m_sc_conv_w, m_even_w_out, m_odd_norm_pre, m_odd_norm_post, m_odd_w_in, m_gla_w_gate_lr, m_gla_b_gate, m_gla_norm_g, m_odd_w_out, v_even_norm_pre, v_even_norm_post, v_even_w_in, v_rg_conv_w, v_rg_conv_b, v_rg_gate_w, v_rg_gate_b, v_rg_lambda, v_sc_conv_w, v_even_w_out, v_odd_norm_pre, v_odd_norm_post, v_odd_w_in, v_gla_w_gate_lr, v_gla_b_gate, v_gla_norm_g, v_odd_w_out):
    weights = dict(even_norm_pre=even_norm_pre, even_norm_post=even_norm_post, even_w_in=even_w_in, rg_conv_w=rg_conv_w,
                   rg_conv_b=rg_conv_b, rg_gate_w=rg_gate_w, rg_gate_b=rg_gate_b, rg_lambda=rg_lambda, sc_conv_w=sc_conv_w,
                   even_w_out=even_w_out, odd_norm_pre=odd_norm_pre, odd_norm_post=odd_norm_post, odd_w_in=odd_w_in,
                   gla_w_gate_lr=gla_w_gate_lr, gla_b_gate=gla_b_gate, gla_norm_g=gla_norm_g, odd_w_out=odd_w_out)
    m_in = dict(even_norm_pre=m_even_norm_pre, even_norm_post=m_even_norm_post, even_w_in=m_even_w_in, rg_conv_w=m_rg_conv_w,
                rg_conv_b=m_rg_conv_b, rg_gate_w=m_rg_gate_w, rg_gate_b=m_rg_gate_b, rg_lambda=m_rg_lambda, sc_conv_w=m_sc_conv_w,
                even_w_out=m_even_w_out, odd_norm_pre=m_odd_norm_pre, odd_norm_post=m_odd_norm_post, odd_w_in=m_odd_w_in,
                gla_w_gate_lr=m_gla_w_gate_lr, gla_b_gate=m_gla_b_gate, gla_norm_g=m_gla_norm_g, odd_w_out=m_odd_w_out)
    v_in = dict(even_norm_pre=v_even_norm_pre, even_norm_post=v_even_norm_post, even_w_in=v_even_w_in, rg_conv_w=v_rg_conv_w,
                rg_conv_b=v_rg_conv_b, rg_gate_w=v_rg_gate_w, rg_gate_b=v_rg_gate_b, rg_lambda=v_rg_lambda, sc_conv_w=v_sc_conv_w,
                even_w_out=v_even_w_out, odd_norm_pre=v_odd_norm_pre, odd_norm_post=v_odd_norm_post, odd_w_in=v_odd_w_in,
                gla_w_gate_lr=v_gla_w_gate_lr, gla_b_gate=v_gla_b_gate, gla_norm_g=v_gla_norm_g, odd_w_out=v_odd_w_out)
    names = list(weights)
    shapes = {n: weights[n].shape for n in names}
    xs = x[0]
    tgt = loss_target[0]

    proj_e, h_e, w_in_e, small_all = gather_matmul(xs, even_norm_pre, even_w_in[0].astype(BF16),
                                                   _pack_small({n: weights[n][0] for n in SMALL_SHARDED}), 2 * MM_TILE)
    small = _unpack_gathered(small_all)
    gate_w = rg_gate_w[0].reshape(4, RG_HEADS, RG_HEAD_DIM, RG_HEAD_DIM).astype(BF16)
    gate_b = rg_gate_b[0].reshape(4, RG_HEADS, RG_HEAD_DIM)
    conv_b = rg_conv_b
    wg_pad = [jnp.pad(small["gla_w_gate_lr"][d], ((GLA_RANK * d, LANES - GLA_RANK * (d + 1)), (0, 0))).astype(BF16) for d in range(2)]
    bg = [small["gla_b_gate"][d:d + 1] for d in range(2)]
    gnorm = jnp.tile(small["gla_norm_g"], (1, GLA_HEADS))

    half = D_MODEL // 2
    behind_gates = Exchange()
    behind_gates.gather(even_w_out[0].astype(BF16), via_sibling=True)
    behind_gates.gather(odd_w_in[0, :half].astype(BF16), via_sibling=True)
    (ab, hf), (w_out_e, w_in_o_top) = even_gates_fwd(proj_e, small["rg_conv_w"], conv_b, gate_w, gate_b, small["rg_lambda"],
                                                     exchange=behind_gates)
    w_out_e = w_out_e.reshape(2 * D_MODEL, D_MODEL)
    behind_mix_fwd = Exchange()
    behind_mix_fwd.gather(odd_w_in[0, half:].astype(BF16), via_sibling=True)
    behind_mix_fwd.gather(odd_w_out[0].astype(BF16), via_sibling=True)
    (u_e, hb, y_e, x1), (w_in_o_bottom, w_out_o) = even_mix_fwd(ab, hf, proj_e, small["sc_conv_w"], w_out_e, xs, even_norm_post,
                                                                exchange=behind_mix_fwd)
    w_out_o = w_out_o.reshape(D_MODEL, D_MODEL)
    w_in_o = columns_from_blocks([w_in_o_top, w_in_o_bottom], ODD_IN_PAD)

    proj_o, h_o = rms_matmul(x1, small["odd_norm_pre"], w_in_o, MM_TILE, ODD_IN_PAD, "odd_in")
    o_f, st_f = gla_fwd(proj_o, wg_pad[0], bg[0], False)
    osum, u_o, st_b, y_o, dout, loss_part = gla_fwd(proj_o, wg_pad[1], bg[1], True, o_other=o_f, gnorm=gnorm,
                                                    post=(w_out_o, x1, small["odd_norm_post"], tgt))

    do, dr, dy_o, d_odd_norm_post, d_gnorm = normbwd_matmul_nt(y_o, small["odd_norm_post"], dout, w_out_o, D_MODEL, "odd_out_bwd",
                                                               gla=(proj_o, osum, gnorm))
    d_w_out_o = matmul_tn(u_o, dy_o, D_MODEL, D_MODEL, 4 * MM_TILE, BF16, "odd_w_out_grad")
    dqkv_f, dlr_f, dwg_f, dbg_f = gla_bwd(proj_o, wg_pad[0], bg[0], do, st_f, False)
    dproj_o, dwg_b, dbg_b = gla_bwd(proj_o, wg_pad[1], bg[1], do, st_b, True, first=(dqkv_f, dlr_f, dr))
    dx1, d_odd_norm_pre = matmul_nt_normbwd(dproj_o, w_in_o, x1, small["odd_norm_pre"], dout, MM_TILE, ODD_IN_PAD, "odd_in_bwd")
    d_w_in_o = matmul_tn(h_o, dproj_o, D_MODEL, ODD_IN_PAD // 5, 8 * MM_TILE, BF16, "odd_w_in_grad")

    landed = {}
    behind_out = Exchange()
    behind_out.scatter(d_w_out_o.reshape(N_DEV, D_MODEL // N_DEV, D_MODEL))
    behind_out.scatter(d_odd_norm_pre, columns=True)
    behind_out.scatter(d_odd_norm_post, columns=True)
    behind_out.scatter(_blocks_along_columns(jnp.concatenate([dbg_f, dbg_b], axis=0), 2))
    behind_out.scatter(_blocks_along_columns(d_gnorm, 1))
    behind_out.scatter(_blocks_along_columns(jnp.concatenate([dwg_f[:GLA_RANK], dwg_b[GLA_RANK:2 * GLA_RANK]], axis=0), 2 * GLA_RANK))
    (du_e, dy_e, d_even_norm_post), got = normbwd_matmul_nt(y_e, even_norm_post, dx1, w_out_e, 2 * D_MODEL, "even_out_bwd",
                                                           exchange=behind_out)
    p_w_out_o = got[0]
    for n, part in zip(("odd_norm_pre", "odd_norm_post", "gla_b_gate", "gla_norm_g", "gla_w_gate_lr"), got[1:]):
        landed[n] = part
    d_w_out_e = matmul_tn(u_e, dy_e, D_MODEL, D_MODEL, 4 * MM_TILE, BF16, "even_w_out_grad")
    behind_mix = Exchange()
    behind_mix.scatter(d_w_out_e.reshape(N_DEV, 2 * D_MODEL // N_DEV, D_MODEL))
    (dh, drest, d_sc_w, adj_b), (p_w_out_e,) = even_mix_bwd(du_e, hf, hb, proj_e, small["sc_conv_w"], ab, exchange=behind_mix)
    adj_f = linear_scan(ab, 0, dh.reshape(1, *dh.shape), 0, True, True, "scan_fwd_adjoint")
    behind_gates_bwd = Exchange()
    behind_gates_bwd.scatter(column_blocks(d_w_in_o, ODD_SHARD))
    behind_gates_bwd.scatter(d_sc_w, columns=True)
    (dua, d_gate_w, d_gate_b, d_lam), (p_w_in_o, landed["sc_conv_w"]) = even_gates_bwd(
        proj_e, adj_f, adj_b, hf, hb, dh, small["rg_conv_w"], conv_b, gate_w, gate_b, small["rg_lambda"], exchange=behind_gates_bwd)
    gate_w_rows = 4 * RG_HEADS * RG_HEAD_DIM
    behind_conv = Exchange()
    behind_conv.scatter(d_gate_w.reshape(N_DEV, gate_w_rows // N_DEV, RG_HEAD_DIM))
    behind_conv.scatter(d_lam, columns=True)
    (dxa, d_conv_w, d_conv_b), (p_gate_w, landed["rg_lambda"]) = rg_conv_bwd(dua, proj_e, small["rg_conv_w"], exchange=behind_conv)
    behind_w_grad = Exchange()
    behind_w_grad.gather(sum_parts(p_gate_w, "sum_gate_w"))
    d_w_in_e, (g_gate_w_all,) = matmul_tn(h_e, drest, D_MODEL, D_MODEL, 4 * MM_TILE, BF16, "even_w_in_grad",
                                          exchange=behind_w_grad, b_first=dxa)
    to_sibling = Exchange()
    to_sibling.to_sibling(d_w_in_e)
    to_sibling.scatter(d_conv_w, columns=True)
    from_sibling, landed["rg_conv_w"] = run_exchange(to_sibling, "scatter_to_sibling")
    behind_in_bwd = Exchange()
    behind_in_bwd.among_chips(pair_sum(d_w_in_e, from_sibling))
    (grad_x, d_even_norm_pre), (p_w_in_e,) = matmul_nt_normbwd(
        drest, w_in_e, xs, even_norm_pre, dx1, 2 * MM_TILE, D_MODEL, "even_in_bwd", exchange=behind_in_bwd, first=dxa)
    last = Exchange()
    replicated_vecs = ("even_norm_pre", "even_norm_post", "rg_conv_b")
    last.gather(jnp.concatenate([d_even_norm_pre, d_even_norm_post, d_conv_b], axis=0))
    last.gather(d_gate_b.reshape(4 * RG_HEADS, RG_HEAD_DIM))
    last.gather(loss_part)

    results = {}

    def update(name, parts_, shape2d, exchange=None):
        outs = adamw(parts_, weights[name][0].reshape(shape2d), m_in[name][0].reshape(shape2d), v_in[name][0].reshape(shape2d),
                     "adamw_" + name, exchange=exchange)
        if exchange is not None:
            outs, gathered = outs
        results[name] = [o.reshape(shapes[name]) for o in outs]
        return gathered if exchange is not None else None

    land_vec, land_gate_b, land_loss = update("even_w_in", p_w_in_e, (D_MODEL, EVEN_SHARD), exchange=last)
    update("even_w_out", p_w_out_e, (2 * D_MODEL // N_DEV, D_MODEL))
    results["odd_w_in"] = [jnp.transpose(o)[None] for o in adamw_transposed(
        p_w_in_o, *[jnp.transpose(src["odd_w_in"][0]) for src in (weights, m_in, v_in)], "adamw_odd_w_in")]
    update("odd_w_out", p_w_out_o, (D_MODEL // N_DEV, D_MODEL))
    update("rg_gate_w", g_gate_w_all.reshape(1, gate_w_rows, RG_HEAD_DIM), (gate_w_rows, RG_HEAD_DIM))
    small_out = adamw_small({n: landed[n] for n in SMALL_SHARDED}, weights, m_in, v_in)
    for n in SMALL_SHARDED:
        results[n] = [o[n] for o in small_out]
    gate_b_shape = (4 * RG_HEADS, RG_HEAD_DIM)
    rep_out, gate_b_out, loss_all = adamw_replicated(land_vec, land_gate_b, land_loss, replicated_vecs, weights, m_in, v_in,
                                                     [src["rg_gate_b"].reshape(gate_b_shape) for src in (weights, m_in, v_in)])
    results.update(rep_out)
    results["rg_gate_b"] = [o.reshape(shapes["rg_gate_b"]) for o in gate_b_out]

    return (loss_all[0, 0], grad_x.reshape(x.shape), *[results[n][0] for n in names], *[results[n][1] for n in names],
            *[results[n][2] for n in names], *[results[n][3] for n in names])
```

```python
import functools

import jax
import jax.numpy as jnp
from jax import lax
from jax.experimental import pallas as pl
from jax.experimental.pallas import tpu as pltpu

F32 = jnp.float32
BF16 = jnp.bfloat16

N_DEV = 8
D_MODEL = 1024
NORM_EPS = 1e-6
RG_HEADS = 8
RG_HEAD_DIM = 128
RG_C = 8.0
GLA_HEADS = 4
GLA_DK = 128
GLA_DV = 256
GLA_KEY = 512
GLA_RANK = 16
GLA_NORMALIZER = 16.0
GLA_CHUNK = 64
EVEN_IN = 6144
ODD_IN = 3104
ODD_IN_PAD = 3200
ODD_SHARD = ODD_IN // N_DEV
EVEN_SHARD = EVEN_IN // N_DEV
ADAM_LR = 0.001
ADAM_B1 = 0.9
ADAM_B2 = 0.999
ADAM_EPS = 1e-08
ADAM_WD = 0.01
ADAM_STEP = 10

SMALLEST_NORMAL = 1.1754944e-38
SUBLANES = 8
LANES = 128
VMEM_LIMIT_BYTES = 48 * 2 ** 20
ROW_TILE = 256
GLA_TILE = 256
MM_TILE = 512
ADAMW_BLOCK_BYTES = 2 ** 20
PACK_ROWS = 48
MESH_ID = pl.DeviceIdType.MESH


def _params(n_grid):
    return pltpu.CompilerParams(dimension_semantics=("arbitrary",) * n_grid, vmem_limit_bytes=VMEM_LIMIT_BYTES)


def _bdot(a, b):
    return jnp.dot(a.astype(BF16), b.astype(BF16), preferred_element_type=F32)


def _bdot_nt(a, b):
    return lax.dot_general(a.astype(BF16), b.astype(BF16), (((1,), (1,)), ((), ())), preferred_element_type=F32)


def _bdot_tn(a, b):
    return lax.dot_general(a.astype(BF16), b.astype(BF16), (((0,), (0,)), ((), ())), preferred_element_type=F32)


def _rstd(x):
    return lax.rsqrt(jnp.mean(x * x, axis=-1, keepdims=True) + NORM_EPS)


def _rms(x, g):
    return x * _rstd(x) * g


def _rms_bwd(x, g, dy):
    xh = x * _rstd(x)
    dyg = dy * g
    dx = _rstd(x) * (dyg - xh * jnp.mean(dyg * xh, axis=-1, keepdims=True))
    return dx, jnp.sum(dy * xh, axis=0, keepdims=True)


def _sigmoid(z):
    return 0.5 * jnp.tanh(0.5 * z) + 0.5


def _silu_and_grad(z):
    s = _sigmoid(z)
    return z * s, s * (1.0 + z * (1.0 - s))


def _softplus(z):
    return jnp.maximum(z, 0.0) + jnp.log(1.0 + jnp.exp(-jnp.abs(z)))


def _shift_rows(cur, before, after, d):
    ts = cur.shape[0]
    row = lax.broadcasted_iota(jnp.int32, (SUBLANES, cur.shape[1]), 0)
    out = pltpu.roll(cur, (-d) % ts, 0)
    if d < 0:
        edge = jnp.where(row < -d, pltpu.roll(before, (-d) % SUBLANES, 0), out[:SUBLANES])
        return jnp.concatenate([edge, out[SUBLANES:]], axis=0)
    edge = jnp.where(row >= SUBLANES - d, pltpu.roll(after, (-d) % SUBLANES, 0), out[ts - SUBLANES:])
    return jnp.concatenate([out[:ts - SUBLANES], edge], axis=0)


def _halo_specs(ts, s, width, col, tile=lambda i: i):
    per = ts // SUBLANES
    last = s // SUBLANES - 1
    return [
        pl.BlockSpec((ts, width), lambda i: (tile(i), col)),
        pl.BlockSpec((SUBLANES, width), lambda i: (jnp.maximum(tile(i) * per - 1, 0), col)),
        pl.BlockSpec((SUBLANES, width), lambda i: (jnp.minimum((tile(i) + 1) * per, last), col)),
    ]


def _halo_load(cur_ref, before_ref, after_ref, n_tiles, tile=lambda i: i):
    i = tile(pl.program_id(0))
    before = jnp.where(i > 0, before_ref[...], 0.0)
    after = jnp.where(i < n_tiles - 1, after_ref[...], 0.0)
    return cur_ref[...], before, after


def _full(shape):
    return pl.BlockSpec(shape, lambda *_: (0,) * len(shape))


def _peer(x, y, c, mask):
    px, py, pc = x ^ (mask >> 2), y ^ ((mask >> 1) & 1), c ^ (mask & 1)
    return (px, py, pc), 4 * px + 2 * py + pc


class Exchange:
    SIBLING = 1
    OTHER_CHIPS = (2, 4, 6)

    def __init__(self):
        self.args, self.out_shape, self._kinds = [], [], []

    def gather(self, block, columns=False, via_sibling=False):
        shape = (block.shape[0], N_DEV * block.shape[1]) if columns else (N_DEV,) + block.shape
        return self._add(block, shape, ("gather", columns, via_sibling))

    def scatter(self, stack, columns=False):
        shape = (N_DEV, stack.shape[0], stack.shape[1] // N_DEV) if columns else stack.shape
        return self._add(stack, shape, ("scatter", columns, False))

    def _add(self, arg, shape, kind):
        self.args.append(arg)
        self.out_shape.append(jax.ShapeDtypeStruct(shape, arg.dtype))
        self._kinds.append(kind)
        return len(self.args) - 1

    def semaphores(self):
        n = len(self.args)
        return [pltpu.SemaphoreType.DMA((n, N_DEV - 1)), pltpu.SemaphoreType.DMA((n, N_DEV - 1)), pltpu.SemaphoreType.DMA((n,))]

    def to_sibling(self, array):
        shape = (N_DEV // 2, array.shape[0], array.shape[1] // N_DEV)
        return self._add(array, shape, ("to_sibling", True, False))

    def among_chips(self, stack):
        return self._add(stack, stack.shape, ("among_chips", False, False))

    def _copies(self, position, in_refs, out_refs):
        x, y, c, me = position
        for arr, ((kind, columns, via_sibling), src, out) in enumerate(zip(self._kinds, in_refs, out_refs)):
            if kind == "to_sibling":
                width = src.shape[-1] // N_DEV
                for k in range(N_DEV // 2):
                    block = src.at[:, pl.ds(pl.multiple_of((2 * k + 1 - c) * width, LANES), width)]
                    yield arr, k + 1, block, out.at[k], out.at[k], False, self.SIBLING
                continue
            for mask in range(N_DEV):
                _, peer_id = _peer(x, y, c, mask)
                relayed = via_sibling and mask not in (0, self.SIBLING) + self.OTHER_CHIPS
                if kind == "among_chips":
                    if mask in (0,) + self.OTHER_CHIPS:
                        yield arr, mask, src.at[peer_id // 2], out.at[me // 2], out.at[peer_id // 2], False, mask
                elif kind == "gather":
                    if columns:
                        width = src.shape[-1]
                        yield (arr, mask, src, out.at[:, pl.ds(pl.multiple_of(me * width, LANES), width)],
                               out.at[:, pl.ds(pl.multiple_of(peer_id * width, LANES), width)], relayed, mask)
                    else:
                        yield arr, mask, src, out.at[me], out.at[peer_id], relayed, mask
                else:
                    if columns:
                        width = src.shape[-1] // N_DEV
                        block = src.at[:, pl.ds(pl.multiple_of(peer_id * width, LANES), width)]
                    else:
                        block = src.at[peer_id]
                    yield arr, mask, block, out.at[me], out.at[peer_id], False, mask

    def _remote(self, position, sems, arr, slot, to_mask, src, dst):
        x, y, c, _ = position
        return pltpu.make_async_remote_copy(src_ref=src, dst_ref=dst, send_sem=sems[0].at[arr, slot - 1], recv_sem=sems[1].at[arr, slot - 1],
                                            device_id=_peer(x, y, c, to_mask)[0], device_id_type=MESH_ID)

    def start(self, position, in_refs, out_refs, sems):
        for arr, slot, src, dst, _, relayed, to_mask in self._copies(position, in_refs, out_refs):
            if slot == 0:
                pltpu.make_async_copy(src, dst, sems[2].at[arr]).start()
            elif not relayed:
                self._remote(position, sems, arr, slot, to_mask, src, dst).start()

    def wait(self, position, in_refs, out_refs, sems):
        copies = list(self._copies(position, in_refs, out_refs))
        landings = {(arr, slot): landing for arr, slot, _, _, landing, _, _ in copies}
        passed_on = set()
        for arr, mask, src, _, landing, relayed, _ in copies:
            if relayed:
                held = landings[arr, mask ^ self.SIBLING]
                self._remote(position, sems, arr, mask ^ self.SIBLING, mask ^ self.SIBLING, src, held).wait_recv()
                self._remote(position, sems, arr, mask, self.SIBLING, held, held).start()
                passed_on.add((arr, mask ^ self.SIBLING))
        for arr, slot, src, dst, landing, relayed, to_mask in copies:
            if slot == 0:
                pltpu.make_async_copy(src, dst, sems[2].at[arr]).wait()
                continue
            if (arr, slot) not in passed_on:
                self._remote(position, sems, arr, slot, to_mask, src, landing).wait_recv()
            if relayed:
                held = landings[arr, slot ^ self.SIBLING]
                self._remote(position, sems, arr, slot, self.SIBLING, held, held).wait_send()
            else:
                self._remote(position, sems, arr, slot, to_mask, src, dst).wait_send()


def _call(body, *, name, grid, in_specs, out_specs, out_shape, args, scratch_shapes=(), exchange=None):
    single = not isinstance(out_shape, (list, tuple))
    if single:
        out_specs, out_shape = [out_specs], [out_shape]
    params = _params(len(grid))
    if exchange is None:
        outs = pl.pallas_call(body, name=name, grid=grid, in_specs=in_specs, out_specs=out_specs, out_shape=out_shape,
                              scratch_shapes=list(scratch_shapes), compiler_params=params)(*args)
        return outs[0] if single else outs
    counts = (len(args), len(exchange.args), len(out_shape), len(exchange.out_shape), len(scratch_shapes), 3)

    def wrapped(*refs):
        groups, at = [], 0
        for n in counts:
            groups.append(refs[at:at + n])
            at += n
        main_in, ex_in, main_out, ex_out, main_scratch, sems = groups
        x, y, c = lax.axis_index("x"), lax.axis_index("y"), lax.axis_index("c")
        position = (x, y, c, 4 * x + 2 * y + c)
        ids = [pl.program_id(a) for a in range(len(grid))]
        first = functools.reduce(jnp.logical_and, [i == 0 for i in ids])
        last = functools.reduce(jnp.logical_and, [i == g - 1 for i, g in zip(ids, grid)])

        @pl.when(first)
        def _():
            exchange.start(position, ex_in, ex_out, sems)

        body(*main_in, *main_out, *main_scratch)

        @pl.when(last)
        def _():
            exchange.wait(position, ex_in, ex_out, sems)

    hbm = pl.BlockSpec(memory_space=pl.ANY)
    outs = pl.pallas_call(
        wrapped, name=name, grid=grid, in_specs=list(in_specs) + [hbm] * counts[1], out_specs=list(out_specs) + [hbm] * counts[3],
        out_shape=list(out_shape) + exchange.out_shape, scratch_shapes=list(scratch_shapes) + exchange.semaphores(),
        compiler_params=params)(*args, *exchange.args)
    main = outs[:counts[2]]
    return (main[0] if single else main), outs[counts[2]:]


def run_exchange(exchange, name):
    return _call(lambda: None, name=name, grid=(1,), in_specs=[], out_specs=[], out_shape=[], args=[], exchange=exchange)[1]


def gather_matmul(x, g, w_block, small_block, tm):
    s, d = x.shape
    width = w_block.shape[1]
    tm = min(tm, s)
    n_i = s // tm
    sibling = Exchange.SIBLING
    y_nbr, x_nbr, diagonal = Exchange.OTHER_CHIPS

    def links(core):
        return (y_nbr, x_nbr) if core == 1 else (x_nbr, y_nbr)

    def block_order(core):
        first, second = links(core)
        return [0, sibling, first, second | sibling, second, first | sibling, diagonal, diagonal | sibling]

    def body(order_ref, x_ref, g_ref, wb_ref, sb_ref, proj_ref, h_ref, w_ref, small_ref, h_all, w_buf, send, recv, local, load_sem):
        j, i = pl.program_id(0), pl.program_id(1)
        xx, yy, cc = lax.axis_index("x"), lax.axis_index("y"), lax.axis_index("c")
        me = 4 * xx + 2 * yy + cc

        def block_of(dev):
            return w_ref.at[:, pl.ds(pl.multiple_of(dev * width, LANES), width)]

        def half_of(dev, part):
            return w_ref.at[pl.ds(part * (d // 2), d // 2), pl.ds(pl.multiple_of(dev * width, LANES), width)]

        def remote(arr, slot, to_mask, src, dst):
            return pltpu.make_async_remote_copy(src_ref=src, dst_ref=dst, send_sem=send.at[arr, slot - 1], recv_sem=recv.at[arr, slot - 1],
                                                device_id=_peer(xx, yy, cc, to_mask)[0], device_id_type=MESH_ID)

        def mine_to(mask):
            return remote(0, mask, mask, wb_ref, block_of(me))

        def arrival(mask):
            return remote(0, mask, mask, wb_ref, block_of(me ^ mask))

        def to_sibling(mask):
            return remote(0, mask | sibling, sibling, block_of(me ^ mask), block_of(me ^ mask))

        def relay(of):
            along_x = of == y_nbr
            part = 0 if along_x else 1
            return remote(0 if along_x else 2, diagonal, x_nbr if along_x else y_nbr, half_of(me ^ of, part), half_of(me ^ of, part))

        def diagonal_half(part):
            return remote(0 if part == 0 else 2, diagonal, x_nbr if part == 0 else y_nbr, wb_ref.at[pl.ds(0, d // 2), :],
                          half_of(me ^ diagonal, part))

        @pl.when((j == 0) & (i == 0))
        def _():
            pltpu.make_async_copy(wb_ref, block_of(me), local.at[0]).start()
            pltpu.make_async_copy(sb_ref, small_ref.at[me], local.at[1]).start()
            mine_to(sibling).start()
            for mask in range(1, N_DEV):
                remote(1, mask, mask, sb_ref, small_ref.at[me]).start()

        def load(step):
            return pltpu.make_async_copy(w_ref.at[:, pl.ds(pl.multiple_of(order_ref[step] * width, LANES), width)],
                                         w_buf.at[step % 2], load_sem.at[step % 2])

        for core in range(2):
            first, second = links(core)
            for step in range(N_DEV):
                at_step = (j == 0) & (i == 0) if step == 0 else (j == step - 1) & (i == n_i - 1)

                @pl.when(at_step & (cc == core))
                def _(step=step, first=first, second=second):
                    if step == 0:
                        mine_to(first).start()
                        pltpu.make_async_copy(wb_ref, block_of(me), local.at[0]).wait()
                    elif step == 1:
                        arrival(sibling).wait_recv()
                    elif step == 2:
                        arrival(first).wait_recv()
                        to_sibling(first).start()
                        mine_to(first).wait_send()
                        mine_to(second).start()
                        relay(first).start()
                    elif step == 3:
                        arrival(second | sibling).wait_recv()
                    elif step == 4:
                        arrival(second).wait_recv()
                        to_sibling(second).start()
                        relay(second).start()
                    elif step == 5:
                        arrival(first | sibling).wait_recv()
                    elif step == 6:
                        diagonal_half(0).wait_recv()
                        diagonal_half(1).wait_recv()
                        to_sibling(diagonal).start()
                    else:
                        arrival(diagonal | sibling).wait_recv()

        @pl.when((j == 0) & (i == 0))
        def _():
            load(0).start()

        @pl.when(i == 0)
        def _():
            load(j).wait()

        @pl.when((i == n_i - 1) & (j < N_DEV - 1))
        def _():
            load(j + 1).start()

        rows = pl.ds(pl.multiple_of(i * tm, tm), tm)

        @pl.when(j == 0)
        def _():
            h = _rms(x_ref[...], g_ref[...]).astype(BF16)
            h_all[rows, :] = h
            h_ref[...] = h

        proj_ref[...] = jnp.dot(h_all[rows, :], w_buf[j % 2], preferred_element_type=F32)

        @pl.when((j == N_DEV - 1) & (i == n_i - 1))
        def _():
            pltpu.make_async_copy(sb_ref, small_ref.at[me], local.at[1]).wait()
            for mask in range(1, N_DEV):
                remote(1, mask, mask, sb_ref, small_ref.at[me ^ mask]).wait_recv()
                remote(1, mask, mask, sb_ref, small_ref.at[me]).wait_send()
            mine_to(sibling).wait_send()
            for core in range(2):
                @pl.when(cc == core)
                def _(core=core):
                    mine_to(links(core)[1]).wait_send()
            for mask in Exchange.OTHER_CHIPS:
                to_sibling(mask).wait_send()
            relay(y_nbr).wait_send()
            relay(x_nbr).wait_send()

    def first_pass_row(j, i, order):
        return jnp.where(j == 0, i, n_i - 1), 0

    hbm = pl.BlockSpec(memory_space=pl.ANY)
    core = lax.axis_index("c")
    me = 4 * lax.axis_index("x") + 2 * lax.axis_index("y") + core
    order = (me ^ jnp.where(core == 1, jnp.array(block_order(1)), jnp.array(block_order(0)))).astype(jnp.int32)
    grid_spec = pltpu.PrefetchScalarGridSpec(
        num_scalar_prefetch=1, grid=(N_DEV, n_i),
        in_specs=[pl.BlockSpec((tm, d), first_pass_row), pl.BlockSpec((1, d), lambda j, i, order: (0, 0)), hbm, hbm],
        out_specs=[pl.BlockSpec((tm, width), lambda j, i, order: (i, order[j])), pl.BlockSpec((tm, d), first_pass_row), hbm, hbm],
        scratch_shapes=[pltpu.VMEM((s, d), BF16), pltpu.VMEM((2, d, width), BF16), pltpu.SemaphoreType.DMA((3, N_DEV - 1)),
                        pltpu.SemaphoreType.DMA((3, N_DEV - 1)), pltpu.SemaphoreType.DMA((2,)), pltpu.SemaphoreType.DMA((2,))])
    return pl.pallas_call(
        body, name="even_in", grid_spec=grid_spec,
        out_shape=[jax.ShapeDtypeStruct((s, N_DEV * width), F32), jax.ShapeDtypeStruct((s, d), BF16),
                   jax.ShapeDtypeStruct((d, N_DEV * width), w_block.dtype), jax.ShapeDtypeStruct((N_DEV,) + small_block.shape, small_block.dtype)],
        compiler_params=_params(2),
    )(order, x, g, w_block, small_block)


def rms_matmul(x, g, w, tm, tn, name, exchange=None):
    s, d = x.shape
    n = w.shape[1]
    tm = min(tm, s)

    def body(x_ref, g_ref, w_ref, o_ref, h_ref):
        @pl.when(pl.program_id(1) == 0)
        def _():
            h_ref[...] = _rms(x_ref[...], g_ref[...]).astype(BF16)

        o_ref[...] = jnp.dot(h_ref[...], w_ref[...], preferred_element_type=F32)

    return _call(
        body, name=name, grid=(s // tm, n // tn),
        in_specs=[pl.BlockSpec((tm, d), lambda i, j: (i, 0)), _full((1, d)), pl.BlockSpec((d, tn), lambda i, j: (0, j))],
        out_specs=[pl.BlockSpec((tm, tn), lambda i, j: (i, j)), pl.BlockSpec((tm, d), lambda i, j: (i, 0))],
        out_shape=[jax.ShapeDtypeStruct((s, n), F32), jax.ShapeDtypeStruct((s, d), BF16)],
        args=[x, g, w], exchange=exchange)


def _gla_out_bwd(du, r, osum, gn, do_ref, dr_ref, dgn_ref):
    silu_r, dsilu_r = _silu_and_grad(r)
    for head in range(GLA_HEADS):
        vl = slice(head * GLA_DV, (head + 1) * GLA_DV)
        o_h, g_h, du_h = osum[:, vl], gn[:, vl], du[:, vl]
        dr_ref[:, vl] = (du_h * _rms(o_h, g_h) * dsilu_r[:, vl]).astype(BF16)
        do_h, dg_h = _rms_bwd(o_h, g_h, du_h * silu_r[:, vl])
        do_ref[:, vl] = do_h
        dgn_ref[...] += dg_h


def normbwd_matmul_nt(y, g, dout, w, tn, name, exchange=None, gla=None):
    s, d = y.shape
    n = w.shape[0]
    tm = min(MM_TILE, s)

    def body(*refs):
        if gla is None:
            y_ref, g_ref, dout_ref, w_ref, du_ref, dy_ref, dg_ref = refs
        else:
            y_ref, g_ref, dout_ref, w_ref, r_ref, o_ref, gn_ref, do_ref, dr_ref, dy_ref, dg_ref, dgn_ref = refs
        i, j = pl.program_id(0), pl.program_id(1)

        @pl.when(j == 0)
        def _():
            dy, dg = _rms_bwd(y_ref[...], g_ref[...], dout_ref[...])
            dy_ref[...] = dy.astype(BF16)

            @pl.when(i == 0)
            def _():
                dg_ref[...] = jnp.zeros_like(dg_ref)
                if gla is not None:
                    dgn_ref[...] = jnp.zeros_like(dgn_ref)

            dg_ref[...] += dg

        du = lax.dot_general(dy_ref[...], w_ref[...], (((1,), (1,)), ((), ())), preferred_element_type=F32)
        if gla is None:
            du_ref[...] = du
        else:
            _gla_out_bwd(du, r_ref[...], o_ref[...], gn_ref[...], do_ref, dr_ref, dgn_ref)

    row = pl.BlockSpec((tm, d), lambda i, j: (i, 0))
    in_specs = [row, _full((1, d)), row, pl.BlockSpec((tn, d), lambda i, j: (j, 0))]
    args = [y, g, dout, w]
    tail_specs = [row, _full((1, d))]
    tail_shapes = [jax.ShapeDtypeStruct((s, d), BF16), jax.ShapeDtypeStruct((1, d), F32)]
    if gla is None:
        out_specs = [pl.BlockSpec((tm, tn), lambda i, j: (i, j))] + tail_specs
        out_shape = [jax.ShapeDtypeStruct((s, n), F32)] + tail_shapes
    else:
        proj, osum, gnorm = gla
        assert n == tn == D_MODEL
        in_specs += [pl.BlockSpec((tm, D_MODEL), lambda i, j: (i, 2)), row, _full(gnorm.shape)]
        args += [proj, osum, gnorm]
        out_specs = [row, row] + tail_specs + [_full((1, GLA_DV))]
        out_shape = [jax.ShapeDtypeStruct((s, D_MODEL), F32), jax.ShapeDtypeStruct((s, D_MODEL), BF16)] + tail_shapes + [
            jax.ShapeDtypeStruct((1, GLA_DV), F32)]
    return _call(body, name=name, grid=(s // tm, n // tn), in_specs=in_specs, out_specs=out_specs, out_shape=out_shape,
                 args=args, exchange=exchange)


def matmul_tn(a, b, tm, tn, ts, out_dtype, name, exchange=None, b_first=None):
    s, m = a.shape
    n = b.shape[1] + (0 if b_first is None else tn)
    ts = min(ts, s)
    n_k = s // ts
    dims = (((0,), (0,)), ((), ()))

    def body(*refs):
        if b_first is None:
            a_ref, b_ref, o_ref, acc = refs
        else:
            a_ref, first_ref, b_ref, o_ref, acc = refs
        j, k = pl.program_id(1), pl.program_id(2)

        @pl.when(k == 0)
        def _():
            acc[...] = jnp.zeros_like(acc)

        if b_first is None:
            acc[...] += lax.dot_general(a_ref[...], b_ref[...], dims, preferred_element_type=F32)
        else:
            @pl.when(j == 0)
            def _():
                acc[...] += lax.dot_general(a_ref[...], first_ref[...], dims, preferred_element_type=F32)

            @pl.when(j > 0)
            def _():
                acc[...] += lax.dot_general(a_ref[...], b_ref[...], dims, preferred_element_type=F32)

        @pl.when(k == n_k - 1)
        def _():
            o_ref[...] = acc[...].astype(out_dtype)

    if b_first is None:
        b_specs, b_args = [pl.BlockSpec((ts, tn), lambda i, j, k: (k, j))], [b]
    else:
        b_specs = [pl.BlockSpec((ts, tn), lambda i, j, k: (k, 0)), pl.BlockSpec((ts, tn), lambda i, j, k: (k, jnp.maximum(j - 1, 0)))]
        b_args = [b_first, b]
    return _call(
        body, name=name, grid=(m // tm, n // tn, n_k),
        in_specs=[pl.BlockSpec((ts, tm), lambda i, j, k: (k, i))] + b_specs,
        out_specs=pl.BlockSpec((tm, tn), lambda i, j, k: (i, j)),
        out_shape=jax.ShapeDtypeStruct((m, n), out_dtype),
        scratch_shapes=[pltpu.VMEM((tm, tn), F32)], args=[a] + b_args, exchange=exchange)


def matmul_nt_normbwd(dproj, w, x, g, dres, tm, tk, name, exchange=None, first=None):
    s, kt = dproj.shape
    kt += 0 if first is None else tk
    d = w.shape[0]
    tm = min(tm, s)
    n_k = kt // tk
    dims = (((1,), (1,)), ((), ()))

    def body(*refs):
        if first is None:
            a_ref, w_ref, x_ref, g_ref, r_ref, dx_ref, dg_ref, acc = refs
        else:
            first_ref, a_ref, w_ref, x_ref, g_ref, r_ref, dx_ref, dg_ref, acc = refs
        i, k = pl.program_id(0), pl.program_id(1)

        @pl.when(k == 0)
        def _():
            acc[...] = jnp.zeros_like(acc)

        if first is None:
            acc[...] += lax.dot_general(a_ref[...], w_ref[...], dims, preferred_element_type=F32)
        else:
            @pl.when(k == 0)
            def _():
                acc[...] += lax.dot_general(first_ref[...], w_ref[...], dims, preferred_element_type=F32)

            @pl.when(k > 0)
            def _():
                acc[...] += lax.dot_general(a_ref[...], w_ref[...], dims, preferred_element_type=F32)

        @pl.when(k == n_k - 1)
        def _():
            dx, dg = _rms_bwd(x_ref[...], g_ref[...], acc[...])
            dx_ref[...] = r_ref[...] + dx

            @pl.when(i == 0)
            def _():
                dg_ref[...] = jnp.zeros_like(dg_ref)

            dg_ref[...] += dg

    row = pl.BlockSpec((tm, d), lambda i, k: (i, 0))
    if first is None:
        a_specs, a_args = [pl.BlockSpec((tm, tk), lambda i, k: (i, k))], [dproj]
    else:
        a_specs = [pl.BlockSpec((tm, tk), lambda i, k: (i, 0)), pl.BlockSpec((tm, tk), lambda i, k: (i, jnp.maximum(k - 1, 0)))]
        a_args = [first, dproj]
    return _call(
        body, name=name, grid=(s // tm, n_k),
        in_specs=a_specs + [pl.BlockSpec((d, tk), lambda i, k: (0, k)), row, _full((1, d)), row],
        out_specs=[row, _full((1, d))],
        out_shape=[jax.ShapeDtypeStruct((s, d), F32), jax.ShapeDtypeStruct((1, d), F32)],
        scratch_shapes=[pltpu.VMEM((tm, d), F32)], args=a_args + [w, x, g, dres], exchange=exchange)


def _rg_conv(xa, before, after, cw, cb):
    return (cw[0:1, :] * _shift_rows(xa, before, after, -2) + cw[1:2, :] * _shift_rows(xa, before, after, -1)
            + cw[2:3, :] * xa + cw[3:4, :] * _shift_rows(xa, before, after, 1) + cb)


def _rg_gates(ua_h, gw_ref, gb_ref, c_h, direction, head):
    r = _sigmoid(_bdot(ua_h, gw_ref[2 * direction, head]) + gb_ref[2 * direction, head:head + 1, :])
    i = _sigmoid(_bdot(ua_h, gw_ref[2 * direction + 1, head]) + gb_ref[2 * direction + 1, head:head + 1, :])
    log_a = -c_h * r
    a = jnp.exp(log_a)
    beta_sq = -jnp.tanh(log_a) * (1.0 + a * a)
    inv_beta = lax.rsqrt(jnp.maximum(beta_sq, SMALLEST_NORMAL))
    return r, i, a, beta_sq * inv_beta, inv_beta


def even_gates_fwd(proj, conv_w, conv_b, gate_w, gate_b, lam, exchange=None):
    s = proj.shape[0]
    ts = min(2 * ROW_TILE, s)
    n_tiles = s // ts

    def body(xa_ref, xb_ref, xn_ref, cw_ref, cb_ref, gw_ref, gb_ref, lam_ref, o_ref, hf_ref, carry):
        @pl.when(pl.program_id(0) == 0)
        def _():
            carry[...] = jnp.zeros_like(carry)

        xa, before, after = _halo_load(xa_ref, xb_ref, xn_ref, n_tiles)
        ua = _rg_conv(xa, before, after, cw_ref[...], cb_ref[...])
        c = RG_C * _softplus(-lam_ref[...])
        ua_bf16 = ua.astype(BF16)
        for direction in range(2):
            for head in range(RG_HEADS):
                lanes = slice(head * RG_HEAD_DIM, (head + 1) * RG_HEAD_DIM)
                ua_h = ua[:, lanes]
                _, i, a, beta, _ = _rg_gates(ua_bf16[:, lanes], gw_ref, gb_ref, c[direction:direction + 1, lanes], direction, head)
                o_ref[2 * direction, :, lanes] = a
                o_ref[2 * direction + 1, :, lanes] = beta * (i * ua_h)
        _scan_tile(o_ref.at[0], o_ref.at[1], hf_ref, carry, False, False)

    return _call(
        body, name="even_gates_fwd", grid=(n_tiles,),
        in_specs=_halo_specs(ts, s, D_MODEL, 0) + [_full(conv_w.shape), _full(conv_b.shape), _full(gate_w.shape),
                                                   _full(gate_b.shape), _full(lam.shape)],
        out_specs=[pl.BlockSpec((4, ts, D_MODEL), lambda i: (0, i, 0)), pl.BlockSpec((ts, D_MODEL), lambda i: (i, 0))],
        out_shape=[jax.ShapeDtypeStruct((4, s, D_MODEL), F32), jax.ShapeDtypeStruct((s, D_MODEL), F32)],
        scratch_shapes=[pltpu.VMEM((SUBLANES, D_MODEL), F32)],
        args=[proj, proj, proj, conv_w, conv_b, gate_w, gate_b, lam], exchange=exchange)


def _scan_tile(a_ref, b_ref, h_ref, carry, reverse, b_times_a):
    ts, c = h_ref.shape
    n_blocks = ts // SUBLANES
    row = lax.broadcasted_iota(jnp.int32, (SUBLANES, c), 0)

    def block(j, h_in):
        r0 = pl.multiple_of((n_blocks - 1 - j if reverse else j) * SUBLANES, SUBLANES)
        a = a_ref[pl.ds(r0, SUBLANES), :]
        b = b_ref[pl.ds(r0, SUBLANES), :]
        if b_times_a:
            b = a * b
        for step in (1, 2, 4):
            shift = SUBLANES - step if reverse else step
            valid = row < SUBLANES - step if reverse else row >= step
            b = jnp.where(valid, a * pltpu.roll(b, shift, 0) + b, b)
            a = jnp.where(valid, a * pltpu.roll(a, shift, 0), a)
        h = a * h_in + b
        h_ref[pl.ds(r0, SUBLANES), :] = h
        return h[0:1, :] if reverse else h[SUBLANES - 1:SUBLANES, :]

    carry[0:1, :] = lax.fori_loop(0, n_blocks, block, carry[0:1, :])


def linear_scan(a_arr, a_idx, b_arr, b_idx, reverse, b_times_a, name, exchange=None):
    _, s, c = a_arr.shape
    ts = min(MM_TILE, s)
    n_tiles = s // ts

    def tile_of(i):
        return n_tiles - 1 - i if reverse else i

    def body(a_ref, b_ref, h_ref, carry):
        @pl.when(pl.program_id(0) == 0)
        def _():
            carry[...] = jnp.zeros_like(carry)

        _scan_tile(a_ref, b_ref, h_ref, carry, reverse, b_times_a)

    return _call(
        body, name=name, grid=(n_tiles,),
        in_specs=[pl.BlockSpec((None, ts, c), lambda i: (a_idx, tile_of(i), 0)),
                  pl.BlockSpec((None, ts, c), lambda i: (b_idx, tile_of(i), 0))],
        out_specs=pl.BlockSpec((ts, c), lambda i: (tile_of(i), 0)),
        out_shape=jax.ShapeDtypeStruct((s, c), F32),
        scratch_shapes=[pltpu.VMEM((SUBLANES, c), F32)], args=[a_arr, b_arr], exchange=exchange)


def _sc_conv(p, before, after, w):
    return w[0:1, :] * _shift_rows(p, before, after, -1) + w[1:2, :] * p + w[2:3, :] * _shift_rows(p, before, after, 1)


def even_mix_fwd(ab, hf, proj, sc_w, w_out, xres, g_post, exchange=None):
    s = proj.shape[0]
    ts = min(ROW_TILE, s)
    n_tiles = s // ts

    def tile(i):
        return n_tiles - 1 - i

    row = pl.BlockSpec((ts, D_MODEL), lambda i: (tile(i), 0))

    def col(c):
        return pl.BlockSpec((ts, D_MODEL), lambda i: (tile(i), c))

    def body(a_ref, b_ref, hf_ref, za_ref, xb_ref, xbb_ref, xbn_ref, gb_ref, gc_ref, gcb_ref, gcn_ref, zb_ref, w_ref,
             wo_ref, x_ref, g_ref, u_ref, hb_ref, y_ref, out_ref, carry):
        @pl.when(pl.program_id(0) == 0)
        def _():
            carry[...] = jnp.zeros_like(carry)

        _scan_tile(a_ref, b_ref, hb_ref, carry, True, False)
        xb, xb_before, xb_after = _halo_load(xb_ref, xbb_ref, xbn_ref, n_tiles, tile)
        gc, gc_before, gc_after = _halo_load(gc_ref, gcb_ref, gcn_ref, n_tiles, tile)
        silu_za, _ = _silu_and_grad(za_ref[...])
        silu_zb, _ = _silu_and_grad(zb_ref[...])
        u_ref[:, :D_MODEL] = ((hf_ref[...] + hb_ref[...]) * silu_za).astype(BF16)
        cv = _sc_conv(gc * xb, gc_before * xb_before, gc_after * xb_after, w_ref[...])
        u_ref[:, D_MODEL:] = (gb_ref[...] * cv * silu_zb).astype(BF16)
        y = jnp.dot(u_ref[...], wo_ref[...], preferred_element_type=F32)
        y_ref[...] = y
        out_ref[...] = x_ref[...] + _rms(y, g_ref[...])

    return _call(
        body, name="even_mix_fwd", grid=(n_tiles,),
        in_specs=[pl.BlockSpec((None, ts, D_MODEL), lambda i: (2, tile(i), 0)), pl.BlockSpec((None, ts, D_MODEL), lambda i: (3, tile(i), 0)),
                  row, col(1)] + _halo_specs(ts, s, D_MODEL, 2, tile) + [col(3)] + _halo_specs(ts, s, D_MODEL, 4, tile)
        + [col(5), _full(sc_w.shape), _full(w_out.shape), row, _full(g_post.shape)],
        out_specs=[pl.BlockSpec((ts, 2 * D_MODEL), lambda i: (tile(i), 0)), row, row, row],
        out_shape=[jax.ShapeDtypeStruct((s, 2 * D_MODEL), BF16)] + [jax.ShapeDtypeStruct((s, D_MODEL), F32)] * 3,
        scratch_shapes=[pltpu.VMEM((SUBLANES, D_MODEL), F32)],
        args=[ab, ab, hf, proj, proj, proj, proj, proj, proj, proj, proj, proj, sc_w, w_out, xres, g_post], exchange=exchange)


def even_mix_bwd(du, hf, hb, proj, sc_w, ab, exchange=None):
    s = proj.shape[0]
    ts = min(ROW_TILE, s)
    n_tiles = s // ts
    row = pl.BlockSpec((ts, D_MODEL), lambda i: (i, 0))

    def body(dya_ref, dyb_ref, dybb_ref, dybn_ref, hf_ref, hb_ref, za_ref, xb_ref, xbb_ref, xbn_ref,
             gb_ref, gbb_ref, gbn_ref, gc_ref, gcb_ref, gcn_ref, zb_ref, zbb_ref, zbn_ref, w_ref, a_ref,
             dh_ref, dp_ref, dw_ref, adj_ref, carry):
        @pl.when(pl.program_id(0) == 0)
        def _():
            carry[...] = jnp.zeros_like(carry)

        dyb, dyb_before, dyb_after = _halo_load(dyb_ref, dybb_ref, dybn_ref, n_tiles)
        xb, xb_before, xb_after = _halo_load(xb_ref, xbb_ref, xbn_ref, n_tiles)
        gb, gb_before, gb_after = _halo_load(gb_ref, gbb_ref, gbn_ref, n_tiles)
        gc, gc_before, gc_after = _halo_load(gc_ref, gcb_ref, gcn_ref, n_tiles)
        zb, zb_before, zb_after = _halo_load(zb_ref, zbb_ref, zbn_ref, n_tiles)
        w = w_ref[...]
        dya, za = dya_ref[...], za_ref[...]
        silu_za, dsilu_za = _silu_and_grad(za)
        dh_ref[...] = dya * silu_za
        _scan_tile(a_ref, dh_ref, adj_ref, carry, False, True)
        dp_ref[:, 0:D_MODEL] = (dya * (hf_ref[...] + hb_ref[...]) * dsilu_za).astype(BF16)

        silu_zb, dsilu_zb = _silu_and_grad(zb)
        p, p_before, p_after = gc * xb, gc_before * xb_before, gc_after * xb_after
        cv = _sc_conv(p, p_before, p_after, w)
        dcv = dyb * gb * silu_zb
        dcv_before = dyb_before * gb_before * _silu_and_grad(zb_before)[0]
        dcv_after = dyb_after * gb_after * _silu_and_grad(zb_after)[0]
        dpp = (w[0:1, :] * _shift_rows(dcv, dcv_before, dcv_after, 1) + w[1:2, :] * dcv
               + w[2:3, :] * _shift_rows(dcv, dcv_before, dcv_after, -1))
        dp_ref[:, D_MODEL:2 * D_MODEL] = (dpp * gc).astype(BF16)
        dp_ref[:, 2 * D_MODEL:3 * D_MODEL] = (dyb * cv * silu_zb).astype(BF16)
        dp_ref[:, 3 * D_MODEL:4 * D_MODEL] = (dpp * xb).astype(BF16)
        dp_ref[:, 4 * D_MODEL:5 * D_MODEL] = (dyb * gb * cv * dsilu_zb).astype(BF16)

        @pl.when(pl.program_id(0) == 0)
        def _():
            dw_ref[...] = jnp.zeros_like(dw_ref)

        dw_ref[0:1, :] += jnp.sum(dcv * _shift_rows(p, p_before, p_after, -1), axis=0, keepdims=True)
        dw_ref[1:2, :] += jnp.sum(dcv * p, axis=0, keepdims=True)
        dw_ref[2:3, :] += jnp.sum(dcv * _shift_rows(p, p_before, p_after, 1), axis=0, keepdims=True)

    return _call(
        body, name="even_mix_bwd", grid=(n_tiles,),
        in_specs=[row] + _halo_specs(ts, s, D_MODEL, 1) + [row, row, pl.BlockSpec((ts, D_MODEL), lambda i: (i, 1))]
        + _halo_specs(ts, s, D_MODEL, 2) + _halo_specs(ts, s, D_MODEL, 3) + _halo_specs(ts, s, D_MODEL, 4)
        + _halo_specs(ts, s, D_MODEL, 5) + [_full(sc_w.shape), pl.BlockSpec((None, ts, D_MODEL), lambda i: (2, i, 0))],
        out_specs=[row, pl.BlockSpec((ts, 5 * D_MODEL), lambda i: (i, 0)), _full(sc_w.shape), row],
        out_shape=[jax.ShapeDtypeStruct((s, D_MODEL), F32), jax.ShapeDtypeStruct((s, 5 * D_MODEL), BF16),
                   jax.ShapeDtypeStruct(sc_w.shape, F32), jax.ShapeDtypeStruct((s, D_MODEL), F32)],
        scratch_shapes=[pltpu.VMEM((SUBLANES, D_MODEL), F32)],
        args=[du, du, du, du, hf, hb, proj, *([proj] * 12), sc_w, ab], exchange=exchange)


def even_gates_bwd(proj, adj_f, adj_b, hf, hb, dh, conv_w, conv_b, gate_w, gate_b, lam, exchange=None):
    s = proj.shape[0]
    ts = min(2 * ROW_TILE, s)
    n_tiles = s // ts
    row = pl.BlockSpec((ts, D_MODEL), lambda i: (i, 0))

    def body(xa_ref, xab_ref, xan_ref, af_ref, afb_ref, afn_ref, ab_ref, abb_ref, abn_ref,
             hf_ref, hfb_ref, hfn_ref, hb_ref, hbb_ref, hbn_ref, dh_ref,
             cw_ref, cb_ref, gw_ref, gb_ref, lam_ref, dua_ref, dgw_ref, dgb_ref, dlam_ref):
        @pl.when(pl.program_id(0) == 0)
        def _():
            dgw_ref[...] = jnp.zeros_like(dgw_ref)
            dgb_ref[...] = jnp.zeros_like(dgb_ref)
            dlam_ref[...] = jnp.zeros_like(dlam_ref)

        xa, before, after = _halo_load(xa_ref, xab_ref, xan_ref, n_tiles)
        ua = _rg_conv(xa, before, after, cw_ref[...], cb_ref[...])
        lam_v = lam_ref[...]
        c = RG_C * _softplus(-lam_v)
        dc_dlam = -RG_C * _sigmoid(-lam_v)
        dh = dh_ref[...]
        adj = (_halo_load(af_ref, afb_ref, afn_ref, n_tiles), _halo_load(ab_ref, abb_ref, abn_ref, n_tiles))
        hs = (_halo_load(hf_ref, hfb_ref, hfn_ref, n_tiles), _halo_load(hb_ref, hbb_ref, hbn_ref, n_tiles))
        dua = jnp.zeros_like(ua)
        ua_bf16 = ua.astype(BF16)
        for direction in range(2):
            step = 1 if direction == 0 else -1
            g = dh + _shift_rows(*adj[direction], step)
            da_all = g * _shift_rows(*hs[direction], -step)
            dua_parts = []
            for head in range(RG_HEADS):
                lanes = slice(head * RG_HEAD_DIM, (head + 1) * RG_HEAD_DIM)
                ua_h = ua[:, lanes]
                c_h = c[direction:direction + 1, lanes]
                ua_hb = ua_bf16[:, lanes]
                r, i, a, beta, inv_beta = _rg_gates(ua_hb, gw_ref, gb_ref, c_h, direction, head)
                db_beta = g[:, lanes] * beta
                d_i = db_beta * ua_h
                dbeta = g[:, lanes] * (i * ua_h)
                dlog_a = (da_all[:, lanes] - dbeta * a * inv_beta) * a
                dpr = -c_h * dlog_a * r * (1.0 - r)
                dpi = d_i * i * (1.0 - i)
                dpr_b, dpi_b = dpr.astype(BF16), dpi.astype(BF16)
                dua_parts.append(db_beta * i + _bdot_nt(dpr_b, gw_ref[2 * direction, head])
                                 + _bdot_nt(dpi_b, gw_ref[2 * direction + 1, head]))
                dgw_ref[2 * direction, head] += _bdot_tn(ua_hb, dpr_b)
                dgw_ref[2 * direction + 1, head] += _bdot_tn(ua_hb, dpi_b)
                dgb_ref[2 * direction, head:head + 1, :] += jnp.sum(dpr, axis=0, keepdims=True)
                dgb_ref[2 * direction + 1, head:head + 1, :] += jnp.sum(dpi, axis=0, keepdims=True)
                dlam_ref[direction:direction + 1, lanes] += (
                    jnp.sum(-r * dlog_a, axis=0, keepdims=True) * dc_dlam[direction:direction + 1, lanes])
            dua = dua + jnp.concatenate(dua_parts, axis=1)
        dua_ref[...] = dua

    return _call(
        body, name="even_gates_bwd", grid=(n_tiles,),
        in_specs=_halo_specs(ts, s, D_MODEL, 0) * 5 + [row] + [_full(conv_w.shape), _full(conv_b.shape), _full(gate_w.shape),
                                                             _full(gate_b.shape), _full(lam.shape)],
        out_specs=[row, _full(gate_w.shape), _full(gate_b.shape), _full(lam.shape)],
        out_shape=[jax.ShapeDtypeStruct((s, D_MODEL), F32), jax.ShapeDtypeStruct(gate_w.shape, F32),
                   jax.ShapeDtypeStruct(gate_b.shape, F32), jax.ShapeDtypeStruct(lam.shape, F32)],
        args=[proj, proj, proj, adj_f, adj_f, adj_f, adj_b, adj_b, adj_b, hf, hf, hf, hb, hb, hb, dh, conv_w, conv_b, gate_w,
              gate_b, lam], exchange=exchange)


def rg_conv_bwd(dua, proj, conv_w, exchange=None):
    s = proj.shape[0]
    ts = min(2 * ROW_TILE, s)
    n_tiles = s // ts

    def body(du_ref, dub_ref, dun_ref, xa_ref, xab_ref, xan_ref, cw_ref, dp_ref, dw_ref, db_ref):
        @pl.when(pl.program_id(0) == 0)
        def _():
            dw_ref[...] = jnp.zeros_like(dw_ref)
            db_ref[...] = jnp.zeros_like(db_ref)

        dua, dua_before, dua_after = _halo_load(du_ref, dub_ref, dun_ref, n_tiles)
        xa, xa_before, xa_after = _halo_load(xa_ref, xab_ref, xan_ref, n_tiles)
        cw = cw_ref[...]
        dxa = (cw[0:1, :] * _shift_rows(dua, dua_before, dua_after, 2) + cw[1:2, :] * _shift_rows(dua, dua_before, dua_after, 1)
               + cw[2:3, :] * dua + cw[3:4, :] * _shift_rows(dua, dua_before, dua_after, -1))
        dp_ref[...] = dxa.astype(BF16)
        for tap, offset in enumerate((-2, -1, 0, 1)):
            shifted = xa if offset == 0 else _shift_rows(xa, xa_before, xa_after, offset)
            dw_ref[tap:tap + 1, :] += jnp.sum(dua * shifted, axis=0, keepdims=True)
        db_ref[...] += jnp.sum(dua, axis=0, keepdims=True)

    return _call(
        body, name="rg_conv_bwd", grid=(n_tiles,),
        in_specs=_halo_specs(ts, s, D_MODEL, 0) * 2 + [_full(conv_w.shape)],
        out_specs=[pl.BlockSpec((ts, D_MODEL), lambda i: (i, 0)), _full(conv_w.shape), _full((1, D_MODEL))],
        out_shape=[jax.ShapeDtypeStruct((s, D_MODEL), BF16), jax.ShapeDtypeStruct(conv_w.shape, F32),
                   jax.ShapeDtypeStruct((1, D_MODEL), F32)],
        args=[dua, dua, dua, proj, proj, proj, conv_w], exchange=exchange)


def _split3(x):
    x1 = x.astype(BF16)
    rest = x - x1.astype(F32)
    x2 = rest.astype(BF16)
    return x1, x2, (rest - x2.astype(F32)).astype(BF16)


def _chunk_sum_matrix(t, reverse, transpose):
    i = lax.broadcasted_iota(jnp.int32, (t, t), 0)
    j = lax.broadcasted_iota(jnp.int32, (t, t), 1)
    if transpose:
        i, j = j, i
    same = (i // GLA_CHUNK) == (j // GLA_CHUNK)
    return jnp.where(same & ((j >= i) if reverse else (j <= i)), 1.0, 0.0).astype(BF16)


def _exact_dot(m, x):
    return sum(jnp.dot(m, part, preferred_element_type=F32) for part in _split3(x))


def _chunk_mask(t, reverse):
    i = lax.broadcasted_iota(jnp.int32, (t, t), 0)
    j = lax.broadcasted_iota(jnp.int32, (t, t), 1)
    return ((i // GLA_CHUNK) == (j // GLA_CHUNK)) & ((j >= i) if reverse else (j <= i))


def _chunk_rows(c):
    return slice(c * GLA_CHUNK, (c + 1) * GLA_CHUNK)


def _gla_gate(lr, wg, bg):
    z = _bdot(lr, wg) + bg
    log_alpha = (jnp.minimum(z, 0.0) - jnp.log(1.0 + jnp.exp(-jnp.abs(z)))) * (1.0 / GLA_NORMALIZER)
    return z, log_alpha


def _gla_tile_terms(q, k, bcum, reverse):
    n_chunks = q.shape[0] // GLA_CHUNK
    totals = []
    for c in range(n_chunks):
        edge = c * GLA_CHUNK if reverse else (c + 1) * GLA_CHUNK - 1
        totals.append(bcum[edge:edge + 1, :])
    btot = jnp.concatenate([jnp.broadcast_to(total, (GLA_CHUNK, total.shape[1])) for total in totals], axis=0)
    e_pos, e_neg, e_st = jnp.exp(bcum), jnp.exp(-bcum), jnp.exp(btot - bcum)
    return q * (GLA_DK ** -0.5) * e_pos, k * e_neg, k * e_st, e_pos, e_neg, e_st, [jnp.exp(total) for total in totals]


def _gla_specs(t, n_tiles, reverse_order):
    def tile(i):
        return n_tiles - 1 - i if reverse_order else i

    return tile, [
        pl.BlockSpec((t, GLA_KEY), lambda i: (tile(i), 0)),
        pl.BlockSpec((t, GLA_KEY), lambda i: (tile(i), 1)),
        pl.BlockSpec((t, D_MODEL), lambda i: (tile(i), 1)),
        pl.BlockSpec((t, LANES), lambda i: (tile(i), (ODD_IN_PAD - LANES) // LANES)),
    ]


def gla_fwd(proj, wg, bg, reverse, o_other=None, gnorm=None, post=None):
    s = proj.shape[0]
    t = min(GLA_TILE, s)
    n_tiles = s // t
    n_chunks = t // GLA_CHUNK
    final = o_other is not None
    tile, specs = _gla_specs(t, n_tiles, reverse)

    def body(*refs):
        if final:
            (q_ref, k_ref, v_ref, lr_ref, wg_ref, bg_ref, oo_ref, r_ref, gn_ref, wo_ref, x_ref, gp_ref, t_ref,
             osum_ref, u_ref, st_ref, y_ref, dout_ref, loss_ref, state) = refs
        else:
            q_ref, k_ref, v_ref, lr_ref, wg_ref, bg_ref, o_ref, st_ref, state = refs
            osum_ref = o_ref

        @pl.when(pl.program_id(0) == 0)
        def _():
            state[...] = jnp.zeros_like(state)

        _, log_alpha = _gla_gate(lr_ref[...], wg_ref[...], bg_ref[...])
        bcum = _exact_dot(_chunk_sum_matrix(t, reverse, False), log_alpha)
        q, k, v = q_ref[...], k_ref[...], v_ref[...]
        q_in, k_in, k_st, _, _, _, decays = _gla_tile_terms(q, k, bcum, reverse)
        mask = _chunk_mask(t, reverse)
        order = list(range(n_chunks))[::-1] if reverse else list(range(n_chunks))
        intra, increments = [], []
        for head in range(GLA_HEADS):
            kl = slice(head * GLA_DK, (head + 1) * GLA_DK)
            vl = slice(head * GLA_DV, (head + 1) * GLA_DV)
            scores = jnp.where(mask, _bdot_nt(q_in[:, kl], k_in[:, kl]), 0.0)
            intra.append(_bdot(scores, v[:, vl]))
            increments.append([_bdot_tn(v[_chunk_rows(c), vl], k_st[_chunk_rows(c), kl]) for c in range(n_chunks)])
        for head in range(GLA_HEADS):
            kl = slice(head * GLA_DK, (head + 1) * GLA_DK)
            vl = slice(head * GLA_DV, (head + 1) * GLA_DV)
            running = state[head]
            before = [None] * n_chunks
            for c in order:
                before[c] = running
                st_ref[c, head] = running
                running = running * decays[c][:, kl] + increments[head][c]
            state[head] = running
            inter = [_bdot_nt(q_in[_chunk_rows(c), kl], before[c]) for c in range(n_chunks)]
            osum_ref[:, vl] = intra[head] + jnp.concatenate(inter, axis=0)
        if final:
            osum = osum_ref[...] + oo_ref[...]
            osum_ref[...] = osum
            silu_r, _ = _silu_and_grad(r_ref[...])
            gn = gn_ref[...]
            for head in range(GLA_HEADS):
                vl = slice(head * GLA_DV, (head + 1) * GLA_DV)
                u_ref[:, vl] = (_rms(osum[:, vl], gn[:, vl]) * silu_r[:, vl]).astype(BF16)

            @pl.when(pl.program_id(0) == 0)
            def _():
                loss_ref[...] = jnp.zeros_like(loss_ref)

            y = jnp.dot(u_ref[...], wo_ref[...], preferred_element_type=F32)
            y_ref[...] = y
            diff = x_ref[...] + _rms(y, gp_ref[...]) - t_ref[...]
            dout_ref[...] = diff * (1.0 / D_MODEL)
            loss_ref[...] += 0.5 * jnp.sum(jnp.mean(diff * diff, axis=-1, keepdims=True))

    row = pl.BlockSpec((t, D_MODEL), lambda i: (tile(i), 0))
    st_spec = pl.BlockSpec((n_chunks, GLA_HEADS, GLA_DV, GLA_DK), lambda i: (tile(i), 0, 0, 0))
    st_shape = jax.ShapeDtypeStruct((s // GLA_CHUNK, GLA_HEADS, GLA_DV, GLA_DK), F32)
    in_specs = specs + [_full(wg.shape), _full(bg.shape)]
    args = [proj, proj, proj, proj, wg, bg]
    if final:
        w_out, xres, g_post, target = post
        in_specs += [row, pl.BlockSpec((t, D_MODEL), lambda i: (tile(i), 2)), _full(gnorm.shape), _full(w_out.shape), row,
                     _full(g_post.shape), row]
        args += [o_other, proj, gnorm, w_out, xres, g_post, target]
        out_specs = [row, row, st_spec, row, row, _full((SUBLANES, LANES))]
        out_shape = [jax.ShapeDtypeStruct((s, D_MODEL), F32), jax.ShapeDtypeStruct((s, D_MODEL), BF16), st_shape,
                     jax.ShapeDtypeStruct((s, D_MODEL), F32), jax.ShapeDtypeStruct((s, D_MODEL), F32),
                     jax.ShapeDtypeStruct((SUBLANES, LANES), F32)]
    else:
        out_specs = [row, st_spec]
        out_shape = [jax.ShapeDtypeStruct((s, D_MODEL), F32), st_shape]
    return pl.pallas_call(
        body, name="gla_fwd_rev" if reverse else "gla_fwd", grid=(n_tiles,), in_specs=in_specs, out_specs=out_specs,
        out_shape=out_shape, scratch_shapes=[pltpu.VMEM((GLA_HEADS, GLA_DV, GLA_DK), F32)], compiler_params=_params(1),
    )(*args)


def gla_bwd(proj, wg, bg, do, states, reverse, first=None):
    s = proj.shape[0]
    t = min(GLA_TILE, s)
    n_tiles = s // t
    n_chunks = t // GLA_CHUNK
    final = first is not None
    tile, specs = _gla_specs(t, n_tiles, not reverse)

    def body(*refs):
        if final:
            (q_ref, k_ref, v_ref, lr_ref, wg_ref, bg_ref, do_ref, st_ref, dqkv1_ref, dlr1_ref, dr_ref,
             dp_ref, dwg_ref, dbg_ref, dstate, dqkv, dbc, dbt) = refs
        else:
            (q_ref, k_ref, v_ref, lr_ref, wg_ref, bg_ref, do_ref, st_ref,
             dqkv, dlr_ref, dwg_ref, dbg_ref, dstate, dbc, dbt) = refs

        @pl.when(pl.program_id(0) == 0)
        def _():
            dstate[...] = jnp.zeros_like(dstate)
            dwg_ref[...] = jnp.zeros_like(dwg_ref)
            dbg_ref[...] = jnp.zeros_like(dbg_ref)

        lr, wg_v = lr_ref[...], wg_ref[...]
        z, log_alpha = _gla_gate(lr, wg_v, bg_ref[...])
        bcum = _exact_dot(_chunk_sum_matrix(t, reverse, False), log_alpha)
        q, k, v, do_v = q_ref[...], k_ref[...], v_ref[...], do_ref[...]
        q_in, k_in, k_st, e_pos, e_neg, e_st, decays = _gla_tile_terms(q, k, bcum, reverse)
        mask = _chunk_mask(t, reverse)
        order = list(range(n_chunks)) if reverse else list(range(n_chunks))[::-1]
        q_b, k_b, ks_b, v_b, do_b = (a.astype(BF16) for a in (q_in, k_in, k_st, v, do_v))
        dq_intra, dk_intra, dv_intra, increments = [], [], [], []
        for head in range(GLA_HEADS):
            kl = slice(head * GLA_DK, (head + 1) * GLA_DK)
            vl = slice(head * GLA_DV, (head + 1) * GLA_DV)
            scores = jnp.where(mask, _bdot_nt(q_b[:, kl], k_b[:, kl]), 0.0).astype(BF16)
            dscores = jnp.where(mask, _bdot_nt(do_b[:, vl], v_b[:, vl]), 0.0).astype(BF16)
            dv_intra.append(_bdot_tn(scores, do_b[:, vl]))
            dq_intra.append(_bdot(dscores, k_b[:, kl]))
            dk_intra.append(_bdot_tn(dscores, q_b[:, kl]))
            increments.append([_bdot_tn(do_b[_chunk_rows(c), vl], q_b[_chunk_rows(c), kl]) for c in range(n_chunks)])
        after_all, ddecay_all = [], []
        for head in range(GLA_HEADS):
            kl = slice(head * GLA_DK, (head + 1) * GLA_DK)
            running = dstate[head]
            after, ddecay = [None] * n_chunks, [None] * n_chunks
            for c in order:
                after[c] = running
                ddecay[c] = jnp.sum(running * st_ref[c, head], axis=0, keepdims=True)
                running = running * decays[c][:, kl] + increments[head][c]
            dstate[head] = running
            after_all.append(after)
            ddecay_all.append(ddecay)
        for head in range(GLA_HEADS):
            kl = slice(head * GLA_DK, (head + 1) * GLA_DK)
            vl = slice(head * GLA_DV, (head + 1) * GLA_DV)
            after, ddecay = after_all[head], ddecay_all[head]
            dq_inter = jnp.concatenate([_bdot(do_b[_chunk_rows(c), vl], st_ref[c, head]) for c in range(n_chunks)], axis=0)
            dv_inter = jnp.concatenate([_bdot_nt(ks_b[_chunk_rows(c), kl], after[c]) for c in range(n_chunks)], axis=0)
            dk_st = jnp.concatenate([_bdot(v_b[_chunk_rows(c), vl], after[c]) for c in range(n_chunks)], axis=0)
            dq_in = dq_intra[head] + dq_inter
            ks_h = k_st[:, kl]
            dqkv[:, 2 * GLA_KEY + head * GLA_DV:2 * GLA_KEY + (head + 1) * GLA_DV] = dv_intra[head] + dv_inter
            dqkv[:, kl] = dq_in * (GLA_DK ** -0.5) * e_pos[:, kl]
            dqkv[:, GLA_KEY + head * GLA_DK:GLA_KEY + (head + 1) * GLA_DK] = dk_intra[head] * e_neg[:, kl] + dk_st * e_st[:, kl]
            dbc[:, kl] = dq_in * q_in[:, kl] - dk_intra[head] * k_in[:, kl] - dk_st * ks_h
            weighted = dk_st * ks_h
            for c in range(n_chunks):
                dbtot = jnp.sum(weighted[_chunk_rows(c)], axis=0, keepdims=True) + ddecay[c] * decays[c][:, kl]
                dbt[_chunk_rows(c), kl] = jnp.broadcast_to(dbtot, (GLA_CHUNK, GLA_DK))
        dlog_alpha = _exact_dot(_chunk_sum_matrix(t, reverse, True), dbc[...]) + dbt[...]
        dz = dlog_alpha * _sigmoid(-z) * (1.0 / GLA_NORMALIZER)
        dlr = _bdot_nt(dz, wg_v)
        dwg_ref[...] += _bdot_tn(lr, dz)
        dbg_ref[...] += jnp.sum(dz, axis=0, keepdims=True)
        if final:
            dp_ref[:, :2 * D_MODEL] = (dqkv[...] + dqkv1_ref[...]).astype(BF16)
            dp_ref[:, 2 * D_MODEL:3 * D_MODEL] = dr_ref[...]
            dp_ref[:, 3 * D_MODEL:] = (dlr + dlr1_ref[...]).astype(BF16)
        else:
            dlr_ref[...] = dlr

    row = pl.BlockSpec((t, D_MODEL), lambda i: (tile(i), 0))
    wide = pl.BlockSpec((t, 2 * D_MODEL), lambda i: (tile(i), 0))
    narrow = pl.BlockSpec((t, LANES), lambda i: (tile(i), 0))
    st_spec = pl.BlockSpec((n_chunks, GLA_HEADS, GLA_DV, GLA_DK), lambda i: (tile(i), 0, 0, 0))
    in_specs = specs + [_full(wg.shape), _full(bg.shape), row, st_spec]
    args = [proj, proj, proj, proj, wg, bg, do, states]
    acc_specs = [_full(wg.shape), _full(bg.shape)]
    acc_shapes = [jax.ShapeDtypeStruct(wg.shape, F32), jax.ShapeDtypeStruct(bg.shape, F32)]
    scratch = [pltpu.VMEM((GLA_HEADS, GLA_DV, GLA_DK), F32)]
    work = [pltpu.VMEM((t, GLA_KEY), F32), pltpu.VMEM((t, GLA_KEY), F32)]
    if final:
        in_specs += [wide, narrow, row]
        args += list(first)
        out_specs = [pl.BlockSpec((t, ODD_IN_PAD), lambda i: (tile(i), 0))] + acc_specs
        out_shape = [jax.ShapeDtypeStruct((s, ODD_IN_PAD), BF16)] + acc_shapes
        scratch += [pltpu.VMEM((t, 2 * D_MODEL), F32)] + work
    else:
        out_specs = [wide, narrow] + acc_specs
        out_shape = [jax.ShapeDtypeStruct((s, 2 * D_MODEL), F32), jax.ShapeDtypeStruct((s, LANES), F32)] + acc_shapes
        scratch += work
    return pl.pallas_call(
        body, name="gla_bwd_rev" if reverse else "gla_bwd", grid=(n_tiles,), in_specs=in_specs, out_specs=out_specs,
        out_shape=out_shape, scratch_shapes=scratch, compiler_params=_params(1),
    )(*args)


def column_blocks(a, width):
    r, c = a.shape
    window = -(-(width + LANES) // LANES) * LANES
    padded = -(-width // LANES) * LANES
    assert window <= c

    def body(a_ref, o_ref):
        row = lax.broadcasted_iota(jnp.int32, (window, padded), 0)
        col = lax.broadcasted_iota(jnp.int32, (window, padded), 1)
        for j in range(N_DEV):
            start = min(j * width // LANES * LANES, c - window)
            pick = jnp.where((row == col + (j * width - start)) & (col < width), 1.0, 0.0).astype(BF16)
            picked = jnp.dot(a_ref[:, start:start + window], pick, preferred_element_type=F32)
            o_ref[j] = picked[:, :width].astype(o_ref.dtype)

    return pl.pallas_call(
        body, name="column_blocks", grid=(1,), in_specs=[_full((r, c))], out_specs=_full((N_DEV, r, width)),
        out_shape=jax.ShapeDtypeStruct((N_DEV, r, width), a.dtype), compiler_params=_params(1),
    )(a)


def columns_from_blocks(parts, total):
    width = parts[0].shape[2]
    rows = [p.shape[1] for p in parts]
    window = -(-(width + LANES) // LANES) * LANES

    def body(*refs):
        part_refs, o_ref, acc = refs[:len(parts)], refs[len(parts)], refs[len(parts) + 1]
        acc[...] = jnp.zeros_like(acc)
        row = lax.broadcasted_iota(jnp.int32, (width, window), 0)
        col = lax.broadcasted_iota(jnp.int32, (width, window), 1)
        for j in range(N_DEV):
            start = min(j * width // LANES * LANES, total - window)
            place = jnp.where(col == row + (j * width - start), 1.0, 0.0).astype(BF16)
            at = 0
            for part_ref, r in zip(part_refs, rows):
                acc[at:at + r, start:start + window] += jnp.dot(part_ref[j], place, preferred_element_type=F32)
                at += r
        o_ref[...] = acc[...].astype(o_ref.dtype)

    return pl.pallas_call(
        body, name="columns_from_blocks", grid=(1,), in_specs=[_full(p.shape) for p in parts], out_specs=_full((sum(rows), total)),
        out_shape=jax.ShapeDtypeStruct((sum(rows), total), parts[0].dtype), scratch_shapes=[pltpu.VMEM((sum(rows), total), F32)],
        compiler_params=_params(1),
    )(*parts)


def pair_sum(grad, from_sibling):
    n_chips, r, w = from_sibling.shape

    def body(even_ref, odd_ref, sib_ref, o_ref):
        mine = jnp.where(lax.axis_index("c") == 1, odd_ref[...], even_ref[...])
        o_ref[...] = (mine.astype(F32) + sib_ref[...].astype(F32)).astype(o_ref.dtype)

    return pl.pallas_call(
        body, name="pair_sum", grid=(n_chips,),
        in_specs=[pl.BlockSpec((r, w), lambda k: (0, 2 * k)), pl.BlockSpec((r, w), lambda k: (0, 2 * k + 1)),
                  pl.BlockSpec((None, r, w), lambda k: (k, 0, 0))],
        out_specs=pl.BlockSpec((None, r, w), lambda k: (k, 0, 0)),
        out_shape=jax.ShapeDtypeStruct(from_sibling.shape, from_sibling.dtype), compiler_params=_params(1),
    )(grad, grad, from_sibling)


def _adamw_update(g, w, m, v):
    new_m = ADAM_B1 * m + (1.0 - ADAM_B1) * g
    new_v = ADAM_B2 * v + (1.0 - ADAM_B2) * (g * g)
    m_hat = new_m / (1.0 - ADAM_B1 ** ADAM_STEP)
    v_hat = new_v / (1.0 - ADAM_B2 ** ADAM_STEP)
    return -ADAM_LR * (m_hat / (jnp.sqrt(v_hat) + ADAM_EPS) + ADAM_WD * w), new_m, new_v


def sum_parts(parts, name):
    _, r, c = parts.shape

    def body(p_ref, o_ref):
        total = p_ref[0].astype(F32)
        for j in range(1, N_DEV):
            total = total + p_ref[j].astype(F32)
        o_ref[...] = total

    return pl.pallas_call(body, name=name, in_specs=[_full(parts.shape)], out_specs=_full((r, c)), grid=(1,),
                          out_shape=jax.ShapeDtypeStruct((r, c), F32), compiler_params=_params(1))(parts)


def adamw(parts, w, m, v, name, exchange=None):
    n, r, c = parts.shape
    tr = r
    while tr * c * 4 > ADAMW_BLOCK_BYTES and tr % (2 * SUBLANES) == 0:
        tr //= 2

    def body(p_ref, w_ref, m_ref, v_ref, g_ref, d_ref, nm_ref, nv_ref):
        g = p_ref[0].astype(F32)
        for j in range(1, n):
            g = g + p_ref[j].astype(F32)
        g_ref[...] = g
        d_ref[...], nm_ref[...], nv_ref[...] = _adamw_update(g, w_ref[...], m_ref[...], v_ref[...])

    row = pl.BlockSpec((tr, c), lambda i: (i, 0))
    return _call(
        body, name=name, grid=(r // tr,),
        in_specs=[pl.BlockSpec((n, tr, c), lambda i: (0, i, 0)), row, row, row], out_specs=[row] * 4,
        out_shape=[jax.ShapeDtypeStruct((r, c), F32)] * 4, args=[parts, w, m, v], exchange=exchange)


def adamw_transposed(parts, w_t, m_t, v_t, name, exchange=None):
    n, r, c = parts.shape
    tr = min(MM_TILE, r)

    def body(p_ref, w_ref, m_ref, v_ref, g_ref, d_ref, nm_ref, nv_ref):
        eye = (lax.broadcasted_iota(jnp.int32, (tr, tr), 0) == lax.broadcasted_iota(jnp.int32, (tr, tr), 1)).astype(BF16)
        g = _bdot_tn(p_ref[0], eye)
        for j in range(1, n):
            g = g + _bdot_tn(p_ref[j], eye)
        g_ref[...] = g
        d_ref[...], nm_ref[...], nv_ref[...] = _adamw_update(g, w_ref[...], m_ref[...], v_ref[...])

    col = pl.BlockSpec((c, tr), lambda i: (0, i))
    return _call(
        body, name=name, grid=(r // tr,), in_specs=[pl.BlockSpec((n, tr, c), lambda i: (0, i, 0)), col, col, col], out_specs=[col] * 4,
        out_shape=[jax.ShapeDtypeStruct((c, r), F32)] * 4, args=[parts, w_t, m_t, v_t], exchange=exchange)


def _small_views(shape):
    if len(shape) == 2:
        return [((slice(None), slice(None)), (slice(None), slice(None)))]
    if len(shape) == 3:
        return [((slice(None), slice(None)), (0,))]
    rows = shape[2]
    return [((slice(k * rows, (k + 1) * rows), slice(None)), (0, k)) for k in range(shape[1])]


def adamw_small(landings, w, m, v):
    names = list(landings)
    n = len(names)
    shapes = [w[name].shape for name in names]

    def body(*refs):
        land, ws, ms, vs = refs[:n], refs[n:2 * n], refs[2 * n:3 * n], refs[3 * n:4 * n]
        outs = [refs[(4 + k) * n:(5 + k) * n] for k in range(4)]
        for k in range(n):
            total = land[k][0]
            for j in range(1, N_DEV):
                total = total + land[k][j]
            for rows, at in _small_views(shapes[k]):
                g = total[rows]
                outs[0][k][at] = g
                outs[1][k][at], outs[2][k][at], outs[3][k][at] = _adamw_update(g, ws[k][at], ms[k][at], vs[k][at])

    blocks = [_full(sh) for sh in shapes]
    outs = pl.pallas_call(
        body, name="adamw_small", grid=(1,),
        in_specs=[_full(landings[name].shape) for name in names] + blocks * 3, out_specs=blocks * 4,
        out_shape=[jax.ShapeDtypeStruct(sh, F32) for sh in shapes] * 4, compiler_params=_params(1),
    )(*[landings[name] for name in names], *[src[name] for src in (w, m, v) for name in names])
    return [dict(zip(names, outs[k * n:(k + 1) * n])) for k in range(4)]


def adamw_replicated(land_vec, land_gate_b, land_loss, names, w, m, v, gate_b):
    n = len(names)

    def body(*refs):
        vec_ref, gb_ref, loss_ref = refs[:3]
        ws, ms, vs = refs[3:3 + n], refs[3 + n:3 + 2 * n], refs[3 + 2 * n:3 + 3 * n]
        gw_ref, gm_ref, gv_ref = refs[3 + 3 * n:6 + 3 * n]
        outs = refs[6 + 3 * n:]
        vec, gb, loss = vec_ref[0], gb_ref[0], loss_ref[0]
        for j in range(1, N_DEV):
            vec, gb, loss = vec + vec_ref[j], gb + gb_ref[j], loss + loss_ref[j]
        for k in range(n):
            g = vec[k:k + 1, :]
            outs[k][...] = g
            outs[n + k][...], outs[2 * n + k][...], outs[3 * n + k][...] = _adamw_update(g, ws[k][...], ms[k][...], vs[k][...])
        outs[4 * n][...] = gb
        outs[4 * n + 1][...], outs[4 * n + 2][...], outs[4 * n + 3][...] = _adamw_update(gb, gw_ref[...], gm_ref[...], gv_ref[...])
        outs[4 * n + 4][...] = loss

    vec_block, gb_block = _full((1, D_MODEL)), _full(gate_b[0].shape)
    outs = pl.pallas_call(
        body, name="adamw_replicated", grid=(1,),
        in_specs=[_full(land_vec.shape), _full(land_gate_b.shape), _full(land_loss.shape)] + [vec_block] * (3 * n) + [gb_block] * 3,
        out_specs=[vec_block] * (4 * n) + [gb_block] * 4 + [_full(land_loss.shape[1:])],
        out_shape=[jax.ShapeDtypeStruct((1, D_MODEL), F32)] * (4 * n) + [jax.ShapeDtypeStruct(gate_b[0].shape, F32)] * 4
        + [jax.ShapeDtypeStruct(land_loss.shape[1:], F32)],
        compiler_params=_params(1),
    )(land_vec, land_gate_b, land_loss, *[src[name] for src in (w, m, v) for name in names], *gate_b)
    results = {name: [outs[k * n + i] for k in range(4)] for i, name in enumerate(names)}
    return results, outs[4 * n:4 * n + 4], outs[4 * n + 4]


SMALL_SHARDED = ("rg_conv_w", "rg_lambda", "sc_conv_w", "odd_norm_pre", "odd_norm_post", "gla_b_gate", "gla_norm_g", "gla_w_gate_lr")
SMALL_ROWS = {"rg_conv_w": (0, 4), "rg_lambda": (4, 2), "sc_conv_w": (6, 3), "odd_norm_pre": (9, 1), "odd_norm_post": (10, 1),
              "gla_b_gate": (11, 2), "gla_norm_g": (13, 1), "gla_w_gate_lr": (16, 32)}


def _pack_small(shards):
    pieces, at = [], 0
    for name in SMALL_SHARDED:
        start, rows = SMALL_ROWS[name]
        if start > at:
            pieces.append(jnp.zeros((start - at, LANES), F32))
        a = shards[name].reshape(rows, -1)
        pieces.append(jnp.pad(a, ((0, 0), (0, LANES - a.shape[1]))))
        at = start + rows
    return jnp.concatenate(pieces, axis=0)


def _unpack_gathered(g):
    def cols(name, width):
        start, rows = SMALL_ROWS[name]
        return jnp.transpose(g[:, start:start + rows, :width], (1, 0, 2)).reshape(rows, N_DEV * width)

    w_lr = cols("gla_w_gate_lr", GLA_KEY // N_DEV).reshape(2, GLA_RANK, GLA_KEY)
    return dict(rg_conv_w=cols("rg_conv_w", LANES), rg_lambda=cols("rg_lambda", LANES), sc_conv_w=cols("sc_conv_w", LANES),
                odd_norm_pre=cols("odd_norm_pre", LANES), odd_norm_post=cols("odd_norm_post", LANES),
                gla_b_gate=cols("gla_b_gate", GLA_KEY // N_DEV), gla_norm_g=cols("gla_norm_g", GLA_DV // N_DEV), gla_w_gate_lr=w_lr)


def _blocks_along_columns(a, rows):
    return jnp.transpose(a.reshape(rows, N_DEV, -1), (1, 0, 2))


def kernel(x, even_norm_pre, even_norm_post, even_w_in, rg_conv_w, rg_conv_b, rg_gate_w, rg_gate_b, rg_lambda, sc_conv_w, even_w_out, odd_norm_pre, odd_norm_post, odd_w_in, gla_w_gate_lr, gla_b_gate, gla_norm_g, odd_w_out, loss_target, m_even_norm_pre, m_even_norm_post, m_even_w_in, m_rg_conv_w, m_rg_conv_b, m_rg_gate_w, m_rg_gate_b, m_rg_lambda, m_sc_conv_w, m_even_w_out, m_odd_norm_pre, m_odd_norm_post, m_odd_w_in, m_gla_w_gate_lr, m_gla_b_gate, m_gla_norm_g, m_odd_w_out, v_even_norm_pre, v_even_norm_post, v_even_w_in, v_rg_conv_w, v_rg_conv_b, v_rg_gate_w, v_rg_gate_b, v_rg_lambda, v_sc_conv_w, v_even_w_out, v_odd_norm_pre, v_odd_norm_post, v_odd_w_in, v_gla_w_gate_lr, v_gla_b_gate, v_gla_norm_g, v_odd_w_out):
    weights = dict(even_norm_pre=even_norm_pre, even_norm_post=even_norm_post, even_w_in=even_w_in, rg_conv_w=rg_conv_w,
                   rg_conv_b=rg_conv_b, rg_gate_w=rg_gate_w, rg_gate_b=rg_gate_b, rg_lambda=rg_lambda, sc_conv_w=sc_conv_w,
                   even_w_out=even_w_out, odd_norm_pre=odd_norm_pre, odd_norm_post=odd_norm_post, odd_w_in=odd_w_in,
                   gla_w_gate_lr=gla_w_gate_lr, gla_b_gate=gla_b_gate, gla_norm_g=gla_norm_g, odd_w_out=odd_w_out)
    m_in = dict(even_norm_pre=m_even_norm_pre, even_norm_post=m_even_norm_post, even_w_in=m_even_w_in, rg_conv_w=m_rg_conv_w,
                rg_conv_b=m_rg_conv_b, rg_gate_w=m_rg_gate_w, rg_gate_b=m_rg_gate_b, rg_lambda=m_rg_lambda, sc_conv_w=m_sc_conv_w,
                even_w_out=m_even_w_out, odd_norm_pre=m_odd_norm_pre, odd_norm_post=m_odd_norm_post, odd_w_in=m_odd_w_in,
                gla_w_gate_lr=m_gla_w_gate_lr, gla_b_gate=m_gla_b_gate, gla_norm_g=m_gla_norm_g, odd_w_out=m_odd_w_out)
    v_in = dict(even_norm_pre=v_even_norm_pre, even_norm_post=v_even_norm_post, even_w_in=v_even_w_in, rg_conv_w=v_rg_conv_w,
                rg_conv_b=v_rg_conv_b, rg_gate_w=v_rg_gate_w, rg_gate_b=v_rg_gate_b, rg_lambda=v_rg_lambda, sc_conv_w=v_sc_conv_w,
                even_w_out=v_even_w_out, odd_norm_pre=v_odd_norm_pre, odd_norm_post=v_odd_norm_post, odd_w_in=v_odd_w_in,
                gla_w_gate_lr=v_gla_w_gate_lr, gla_b_gate=v_gla_b_gate, gla_norm_g=v_gla_norm_g, odd_w_out=v_odd_w_out)
    names = list(weights)
    shapes = {n: weights[n].shape for n in names}
    xs = x[0]
    tgt = loss_target[0]

    proj_e, h_e, w_in_e, small_all = gather_matmul(xs, even_norm_pre, even_w_in[0].astype(BF16),
                                                   _pack_small({n: weights[n][0] for n in SMALL_SHARDED}), 2 * MM_TILE)
    small = _unpack_gathered(small_all)
    gate_w = rg_gate_w[0].reshape(4, RG_HEADS, RG_HEAD_DIM, RG_HEAD_DIM).astype(BF16)
    gate_b = rg_gate_b[0].reshape(4, RG_HEADS, RG_HEAD_DIM)
    conv_b = rg_conv_b
    wg_pad = [jnp.pad(small["gla_w_gate_lr"][d], ((GLA_RANK * d, LANES - GLA_RANK * (d + 1)), (0, 0))).astype(BF16) for d in range(2)]
    bg = [small["gla_b_gate"][d:d + 1] for d in range(2)]
    gnorm = jnp.tile(small["gla_norm_g"], (1, GLA_HEADS))

    half = D_MODEL // 2
    behind_gates = Exchange()
    behind_gates.gather(even_w_out[0].astype(BF16), via_sibling=True)
    behind_gates.gather(odd_w_in[0, :half].astype(BF16), via_sibling=True)
    (ab, hf), (w_out_e, w_in_o_top) = even_gates_fwd(proj_e, small["rg_conv_w"], conv_b, gate_w, gate_b, small["rg_lambda"],
                                                     exchange=behind_gates)
    w_out_e = w_out_e.reshape(2 * D_MODEL, D_MODEL)
    behind_mix_fwd = Exchange()
    behind_mix_fwd.gather(odd_w_in[0, half:].astype(BF16), via_sibling=True)
    behind_mix_fwd.gather(odd_w_out[0].astype(BF16), via_sibling=True)
    (u_e, hb, y_e, x1), (w_in_o_bottom, w_out_o) = even_mix_fwd(ab, hf, proj_e, small["sc_conv_w"], w_out_e, xs, even_norm_post,
                                                                exchange=behind_mix_fwd)
    w_out_o = w_out_o.reshape(D_MODEL, D_MODEL)
    w_in_o = columns_from_blocks([w_in_o_top, w_in_o_bottom], ODD_IN_PAD)

    proj_o, h_o = rms_matmul(x1, small["odd_norm_pre"], w_in_o, MM_TILE, ODD_IN_PAD, "odd_in")
    o_f, st_f = gla_fwd(proj_o, wg_pad[0], bg[0], False)
    osum, u_o, st_b, y_o, dout, loss_part = gla_fwd(proj_o, wg_pad[1], bg[1], True, o_other=o_f, gnorm=gnorm,
                                                    post=(w_out_o, x1, small["odd_norm_post"], tgt))

    do, dr, dy_o, d_odd_norm_post, d_gnorm = normbwd_matmul_nt(y_o, small["odd_norm_post"], dout, w_out_o, D_MODEL, "odd_out_bwd",
                                                               gla=(proj_o, osum, gnorm))
    d_w_out_o = matmul_tn(u_o, dy_o, D_MODEL, D_MODEL, 4 * MM_TILE, BF16, "odd_w_out_grad")
    dqkv_f, dlr_f, dwg_f, dbg_f = gla_bwd(proj_o, wg_pad[0], bg[0], do, st_f, False)
    dproj_o, dwg_b, dbg_b = gla_bwd(proj_o, wg_pad[1], bg[1], do, st_b, True, first=(dqkv_f, dlr_f, dr))
    dx1, d_odd_norm_pre = matmul_nt_normbwd(dproj_o, w_in_o, x1, small["odd_norm_pre"], dout, MM_TILE, ODD_IN_PAD, "odd_in_bwd")
    d_w_in_o = matmul_tn(h_o, dproj_o, D_MODEL, ODD_IN_PAD // 5, 8 * MM_TILE, BF16, "odd_w_in_grad")

    landed = {}
    behind_out = Exchange()
    behind_out.scatter(d_w_out_o.reshape(N_DEV, D_MODEL // N_DEV, D_MODEL))
    behind_out.scatter(d_odd_norm_pre, columns=True)
    behind_out.scatter(d_odd_norm_post, columns=True)
    behind_out.scatter(_blocks_along_columns(jnp.concatenate([dbg_f, dbg_b], axis=0), 2))
    behind_out.scatter(_blocks_along_columns(d_gnorm, 1))
    behind_out.scatter(_blocks_along_columns(jnp.concatenate([dwg_f[:GLA_RANK], dwg_b[GLA_RANK:2 * GLA_RANK]], axis=0), 2 * GLA_RANK))
    (du_e, dy_e, d_even_norm_post), got = normbwd_matmul_nt(y_e, even_norm_post, dx1, w_out_e, 2 * D_MODEL, "even_out_bwd",
                                                           exchange=behind_out)
    p_w_out_o = got[0]
    for n, part in zip(("odd_norm_pre", "odd_norm_post", "gla_b_gate", "gla_norm_g", "gla_w_gate_lr"), got[1:]):
        landed[n] = part
    d_w_out_e = matmul_tn(u_e, dy_e, D_MODEL, D_MODEL, 4 * MM_TILE, BF16, "even_w_out_grad")
    behind_mix = Exchange()
    behind_mix.scatter(d_w_out_e.reshape(N_DEV, 2 * D_MODEL // N_DEV, D_MODEL))
    (dh, drest, d_sc_w, adj_b), (p_w_out_e,) = even_mix_bwd(du_e, hf, hb, proj_e, small["sc_conv_w"], ab, exchange=behind_mix)
    adj_f = linear_scan(ab, 0, dh.reshape(1, *dh.shape), 0, True, True, "scan_fwd_adjoint")
    behind_gates_bwd = Exchange()
    behind_gates_bwd.scatter(column_blocks(d_w_in_o, ODD_SHARD))
    behind_gates_bwd.scatter(d_sc_w, columns=True)
    (dua, d_gate_w, d_gate_b, d_lam), (p_w_in_o, landed["sc_conv_w"]) = even_gates_bwd(
        proj_e, adj_f, adj_b, hf, hb, dh, small["rg_conv_w"], conv_b, gate_w, gate_b, small["rg_lambda"], exchange=behind_gates_bwd)
    gate_w_rows = 4 * RG_HEADS * RG_HEAD_DIM
    behind_conv = Exchange()
    behind_conv.scatter(d_gate_w.reshape(N_DEV, gate_w_rows // N_DEV, RG_HEAD_DIM))
    behind_conv.scatter(d_lam, columns=True)
    (dxa, d_conv_w, d_conv_b), (p_gate_w, landed["rg_lambda"]) = rg_conv_bwd(dua, proj_e, small["rg_conv_w"], exchange=behind_conv)
    behind_w_grad = Exchange()
    behind_w_grad.gather(sum_parts(p_gate_w, "sum_gate_w"))
    d_w_in_e, (g_gate_w_all,) = matmul_tn(h_e, drest, D_MODEL, D_MODEL, 4 * MM_TILE, BF16, "even_w_in_grad",
                                          exchange=behind_w_grad, b_first=dxa)
    to_sibling = Exchange()
    to_sibling.to_sibling(d_w_in_e)
    to_sibling.scatter(d_conv_w, columns=True)
    from_sibling, landed["rg_conv_w"] = run_exchange(to_sibling, "scatter_to_sibling")
    behind_in_bwd = Exchange()
    behind_in_bwd.among_chips(pair_sum(d_w_in_e, from_sibling))
    (grad_x, d_even_norm_pre), (p_w_in_e,) = matmul_nt_normbwd(
        drest, w_in_e, xs, even_norm_pre, dx1, 2 * MM_TILE, D_MODEL, "even_in_bwd", exchange=behind_in_bwd, first=dxa)
    last = Exchange()
    replicated_vecs = ("even_norm_pre", "even_norm_post", "rg_conv_b")
    last.gather(jnp.concatenate([d_even_norm_pre, d_even_norm_post, d_conv_b], axis=0))
    last.gather(d_gate_b.reshape(4 * RG_HEADS, RG_HEAD_DIM))
    last.gather(loss_part)

    results = {}

    def update(name, parts_, shape2d, exchange=None):
        outs = adamw(parts_, weights[name][0].reshape(shape2d), m_in[name][0].reshape(shape2d), v_in[name][0].reshape(shape2d),
                     "adamw_" + name, exchange=exchange)
        if exchange is not None:
            outs, gathered = outs
        results[name] = [o.reshape(shapes[name]) for o in outs]
        return gathered if exchange is not None else None

    update("even_w_in", p_w_in_e, (D_MODEL, EVEN_SHARD))
    update("even_w_out", p_w_out_e, (2 * D_MODEL // N_DEV, D_MODEL))
    odd_w_in_out, (land_vec, land_gate_b, land_loss) = adamw_transposed(
        p_w_in_o, *[jnp.transpose(src["odd_w_in"][0]) for src in (weights, m_in, v_in)], "adamw_odd_w_in", exchange=last)
    results["odd_w_in"] = [jnp.transpose(o)[None] for o in odd_w_in_out]
    update("odd_w_out", p_w_out_o, (D_MODEL // N_DEV, D_MODEL))
    update("rg_gate_w", g_gate_w_all.reshape(1, gate_w_rows, RG_HEAD_DIM), (gate_w_rows, RG_HEAD_DIM))
    small_out = adamw_small({n: landed[n] for n in SMALL_SHARDED}, weights, m_in, v_in)
    for n in SMALL_SHARDED:
        results[n] = [o[n] for o in small_out]
    gate_b_shape = (4 * RG_HEADS, RG_HEAD_DIM)
    rep_out, gate_b_out, loss_all = adamw_replicated(land_vec, land_gate_b, land_loss, replicated_vecs, weights, m_in, v_in,
                                                     [src["rg_gate_b"].reshape(gate_b_shape) for src in (weights, m_in, v_in)])
    results.update(rep_out)
    results["rg_gate_b"] = [o.reshape(shapes["rg_gate_b"]) for o in gate_b_out]

    return (loss_all[0, 0], grad_x.reshape(x.shape), *[results[n][0] for n in names], *[results[n][1] for n in names],
            *[results[n][2] for n in names], *[results[n][3] for n in names])
```

```python
import functools

import jax
import jax.numpy as jnp
from jax import lax
from jax.experimental import pallas as pl
from jax.experimental.pallas import tpu as pltpu

F32 = jnp.float32
BF16 = jnp.bfloat16

N_DEV = 8
D_MODEL = 1024
NORM_EPS = 1e-6
RG_HEADS = 8
RG_HEAD_DIM = 128
RG_C = 8.0
GLA_HEADS = 4
GLA_DK = 128
GLA_DV = 256
GLA_KEY = 512
GLA_RANK = 16
GLA_NORMALIZER = 16.0
GLA_CHUNK = 64
EVEN_IN = 6144
ODD_IN = 3104
ODD_IN_PAD = 3200
ODD_SHARD = ODD_IN // N_DEV
EVEN_SHARD = EVEN_IN // N_DEV
ADAM_LR = 0.001
ADAM_B1 = 0.9
ADAM_B2 = 0.999
ADAM_EPS = 1e-08
ADAM_WD = 0.01
ADAM_STEP = 10

SMALLEST_NORMAL = 1.1754944e-38
SUBLANES = 8
LANES = 128
VMEM_LIMIT_BYTES = 48 * 2 ** 20
ROW_TILE = 256
GLA_TILE = 256
MM_TILE = 512
ADAMW_BLOCK_BYTES = 2 ** 20
PACK_ROWS = 48
MESH_ID = pl.DeviceIdType.MESH


def _params(n_grid):
    return pltpu.CompilerParams(dimension_semantics=("arbitrary",) * n_grid, vmem_limit_bytes=VMEM_LIMIT_BYTES)


def _bdot(a, b):
    return jnp.dot(a.astype(BF16), b.astype(BF16), preferred_element_type=F32)


def _bdot_nt(a, b):
    return lax.dot_general(a.astype(BF16), b.astype(BF16), (((1,), (1,)), ((), ())), preferred_element_type=F32)


def _bdot_tn(a, b):
    return lax.dot_general(a.astype(BF16), b.astype(BF16), (((0,), (0,)), ((), ())), preferred_element_type=F32)


def _rstd(x):
    return lax.rsqrt(jnp.mean(x * x, axis=-1, keepdims=True) + NORM_EPS)


def _rms(x, g):
    return x * _rstd(x) * g


def _rms_bwd(x, g, dy):
    xh = x * _rstd(x)
    dyg = dy * g
    dx = _rstd(x) * (dyg - xh * jnp.mean(dyg * xh, axis=-1, keepdims=True))
    return dx, jnp.sum(dy * xh, axis=0, keepdims=True)


def _sigmoid(z):
    return 0.5 * jnp.tanh(0.5 * z) + 0.5


def _silu_and_grad(z):
    s = _sigmoid(z)
    return z * s, s * (1.0 + z * (1.0 - s))


def _softplus(z):
    return jnp.maximum(z, 0.0) + jnp.log(1.0 + jnp.exp(-jnp.abs(z)))


def _shift_rows(cur, before, after, d):
    ts = cur.shape[0]
    row = lax.broadcasted_iota(jnp.int32, (SUBLANES, cur.shape[1]), 0)
    out = pltpu.roll(cur, (-d) % ts, 0)
    if d < 0:
        edge = jnp.where(row < -d, pltpu.roll(before, (-d) % SUBLANES, 0), out[:SUBLANES])
        return jnp.concatenate([edge, out[SUBLANES:]], axis=0)
    edge = jnp.where(row >= SUBLANES - d, pltpu.roll(after, (-d) % SUBLANES, 0), out[ts - SUBLANES:])
    return jnp.concatenate([out[:ts - SUBLANES], edge], axis=0)


def _halo_specs(ts, s, width, col, tile=lambda i: i):
    per = ts // SUBLANES
    last = s // SUBLANES - 1
    return [
        pl.BlockSpec((ts, width), lambda i: (tile(i), col)),
        pl.BlockSpec((SUBLANES, width), lambda i: (jnp.maximum(tile(i) * per - 1, 0), col)),
        pl.BlockSpec((SUBLANES, width), lambda i: (jnp.minimum((tile(i) + 1) * per, last), col)),
    ]


def _halo_load(cur_ref, before_ref, after_ref, n_tiles, tile=lambda i: i):
    i = tile(pl.program_id(0))
    before = jnp.where(i > 0, before_ref[...], 0.0)
    after = jnp.where(i < n_tiles - 1, after_ref[...], 0.0)
    return cur_ref[...], before, after


def _full(shape):
    return pl.BlockSpec(shape, lambda *_: (0,) * len(shape))


def _peer(x, y, c, mask):
    px, py, pc = x ^ (mask >> 2), y ^ ((mask >> 1) & 1), c ^ (mask & 1)
    return (px, py, pc), 4 * px + 2 * py + pc


class Exchange:
    SIBLING = 1
    OTHER_CHIPS = (2, 4, 6)

    def __init__(self):
        self.args, self.out_shape, self._kinds = [], [], []

    def gather(self, block, columns=False, via_sibling=False):
        shape = (block.shape[0], N_DEV * block.shape[1]) if columns else (N_DEV,) + block.shape
        return self._add(block, shape, ("gather", columns, via_sibling))

    def scatter(self, stack, columns=False):
        shape = (N_DEV, stack.shape[0], stack.shape[1] // N_DEV) if columns else stack.shape
        return self._add(stack, shape, ("scatter", columns, False))

    def _add(self, arg, shape, kind):
        self.args.append(arg)
        self.out_shape.append(jax.ShapeDtypeStruct(shape, arg.dtype))
        self._kinds.append(kind)
        return len(self.args) - 1

    def semaphores(self):
        n = len(self.args)
        return [pltpu.SemaphoreType.DMA((n, N_DEV - 1)), pltpu.SemaphoreType.DMA((n, N_DEV - 1)), pltpu.SemaphoreType.DMA((n,))]

    def to_sibling(self, array):
        shape = (N_DEV // 2, array.shape[0], array.shape[1] // N_DEV)
        return self._add(array, shape, ("to_sibling", True, False))

    def among_chips(self, stack):
        return self._add(stack, stack.shape, ("among_chips", False, False))

    def _copies(self, position, in_refs, out_refs):
        x, y, c, me = position
        for arr, ((kind, columns, via_sibling), src, out) in enumerate(zip(self._kinds, in_refs, out_refs)):
            if kind == "to_sibling":
                width = src.shape[-1] // N_DEV
                for k in range(N_DEV // 2):
                    block = src.at[:, pl.ds(pl.multiple_of((2 * k + 1 - c) * width, LANES), width)]
                    yield arr, k + 1, block, out.at[k], out.at[k], False, self.SIBLING
                continue
            for mask in range(N_DEV):
                _, peer_id = _peer(x, y, c, mask)
                relayed = via_sibling and mask not in (0, self.SIBLING) + self.OTHER_CHIPS
                if kind == "among_chips":
                    if mask in (0,) + self.OTHER_CHIPS:
                        yield arr, mask, src.at[peer_id // 2], out.at[me // 2], out.at[peer_id // 2], False, mask
                elif kind == "gather":
                    if columns:
                        width = src.shape[-1]
                        yield (arr, mask, src, out.at[:, pl.ds(pl.multiple_of(me * width, LANES), width)],
                               out.at[:, pl.ds(pl.multiple_of(peer_id * width, LANES), width)], relayed, mask)
                    else:
                        yield arr, mask, src, out.at[me], out.at[peer_id], relayed, mask
                else:
                    if columns:
                        width = src.shape[-1] // N_DEV
                        block = src.at[:, pl.ds(pl.multiple_of(peer_id * width, LANES), width)]
                    else:
                        block = src.at[peer_id]
                    yield arr, mask, block, out.at[me], out.at[peer_id], False, mask

    def _remote(self, position, sems, arr, slot, to_mask, src, dst):
        x, y, c, _ = position
        return pltpu.make_async_remote_copy(src_ref=src, dst_ref=dst, send_sem=sems[0].at[arr, slot - 1], recv_sem=sems[1].at[arr, slot - 1],
                                            device_id=_peer(x, y, c, to_mask)[0], device_id_type=MESH_ID)

    def start(self, position, in_refs, out_refs, sems):
        for arr, slot, src, dst, _, relayed, to_mask in self._copies(position, in_refs, out_refs):
            if slot == 0:
                pltpu.make_async_copy(src, dst, sems[2].at[arr]).start()
            elif not relayed:
                self._remote(position, sems, arr, slot, to_mask, src, dst).start()

    def wait(self, position, in_refs, out_refs, sems):
        copies = list(self._copies(position, in_refs, out_refs))
        landings = {(arr, slot): landing for arr, slot, _, _, landing, _, _ in copies}
        passed_on = set()
        for arr, mask, src, _, landing, relayed, _ in copies:
            if relayed:
                held = landings[arr, mask ^ self.SIBLING]
                self._remote(position, sems, arr, mask ^ self.SIBLING, mask ^ self.SIBLING, src, held).wait_recv()
                self._remote(position, sems, arr, mask, self.SIBLING, held, held).start()
                passed_on.add((arr, mask ^ self.SIBLING))
        for arr, slot, src, dst, landing, relayed, to_mask in copies:
            if slot == 0:
                pltpu.make_async_copy(src, dst, sems[2].at[arr]).wait()
                continue
            if (arr, slot) not in passed_on:
                self._remote(position, sems, arr, slot, to_mask, src, landing).wait_recv()
            if relayed:
                held = landings[arr, slot ^ self.SIBLING]
                self._remote(position, sems, arr, slot, self.SIBLING, held, held).wait_send()
            else:
                self._remote(position, sems, arr, slot, to_mask, src, dst).wait_send()


def _call(body, *, name, grid, in_specs, out_specs, out_shape, args, scratch_shapes=(), exchange=None):
    single = not isinstance(out_shape, (list, tuple))
    if single:
        out_specs, out_shape = [out_specs], [out_shape]
    params = _params(len(grid))
    if exchange is None:
        outs = pl.pallas_call(body, name=name, grid=grid, in_specs=in_specs, out_specs=out_specs, out_shape=out_shape,
                              scratch_shapes=list(scratch_shapes), compiler_params=params)(*args)
        return outs[0] if single else outs
    counts = (len(args), len(exchange.args), len(out_shape), len(exchange.out_shape), len(scratch_shapes), 3)

    def wrapped(*refs):
        groups, at = [], 0
        for n in counts:
            groups.append(refs[at:at + n])
            at += n
        main_in, ex_in, main_out, ex_out, main_scratch, sems = groups
        x, y, c = lax.axis_index("x"), lax.axis_index("y"), lax.axis_index("c")
        position = (x, y, c, 4 * x + 2 * y + c)
        ids = [pl.program_id(a) for a in range(len(grid))]
        first = functools.reduce(jnp.logical_and, [i == 0 for i in ids])
        last = functools.reduce(jnp.logical_and, [i == g - 1 for i, g in zip(ids, grid)])

        @pl.when(first)
        def _():
            exchange.start(position, ex_in, ex_out, sems)

        body(*main_in, *main_out, *main_scratch)

        @pl.when(last)
        def _():
            exchange.wait(position, ex_in, ex_out, sems)

    hbm = pl.BlockSpec(memory_space=pl.ANY)
    outs = pl.pallas_call(
        wrapped, name=name, grid=grid, in_specs=list(in_specs) + [hbm] * counts[1], out_specs=list(out_specs) + [hbm] * counts[3],
        out_shape=list(out_shape) + exchange.out_shape, scratch_shapes=list(scratch_shapes) + exchange.semaphores(),
        compiler_params=params)(*args, *exchange.args)
    main = outs[:counts[2]]
    return (main[0] if single else main), outs[counts[2]:]


def run_exchange(exchange, name):
    return _call(lambda: None, name=name, grid=(1,), in_specs=[], out_specs=[], out_shape=[], args=[], exchange=exchange)[1]


def gather_matmul(x, g, w_block, small_block, tm):
    s, d = x.shape
    width = w_block.shape[1]
    tm = min(tm, s)
    n_i = s // tm
    sibling = Exchange.SIBLING
    y_nbr, x_nbr, diagonal = Exchange.OTHER_CHIPS

    def links(core):
        return (y_nbr, x_nbr) if core == 1 else (x_nbr, y_nbr)

    def block_order(core):
        first, second = links(core)
        return [0, sibling, first, second | sibling, second, first | sibling, diagonal, diagonal | sibling]

    def body(order_ref, x_ref, g_ref, wb_ref, sb_ref, proj_ref, h_ref, w_ref, small_ref, h_all, w_buf, send, recv, local, load_sem):
        j, i = pl.program_id(0), pl.program_id(1)
        xx, yy, cc = lax.axis_index("x"), lax.axis_index("y"), lax.axis_index("c")
        me = 4 * xx + 2 * yy + cc

        def block_of(dev):
            return w_ref.at[:, pl.ds(pl.multiple_of(dev * width, LANES), width)]

        def half_of(dev, part):
            return w_ref.at[pl.ds(part * (d // 2), d // 2), pl.ds(pl.multiple_of(dev * width, LANES), width)]

        def remote(arr, slot, to_mask, src, dst):
            return pltpu.make_async_remote_copy(src_ref=src, dst_ref=dst, send_sem=send.at[arr, slot - 1], recv_sem=recv.at[arr, slot - 1],
                                                device_id=_peer(xx, yy, cc, to_mask)[0], device_id_type=MESH_ID)

        def mine_to(mask):
            return remote(0, mask, mask, wb_ref, block_of(me))

        def arrival(mask):
            return remote(0, mask, mask, wb_ref, block_of(me ^ mask))

        def to_sibling(mask):
            return remote(0, mask | sibling, sibling, block_of(me ^ mask), block_of(me ^ mask))

        def relay(of):
            along_x = of == y_nbr
            part = 0 if along_x else 1
            return remote(0 if along_x else 2, diagonal, x_nbr if along_x else y_nbr, half_of(me ^ of, part), half_of(me ^ of, part))

        def diagonal_half(part):
            return remote(0 if part == 0 else 2, diagonal, x_nbr if part == 0 else y_nbr, wb_ref.at[pl.ds(0, d // 2), :],
                          half_of(me ^ diagonal, part))

        @pl.when((j == 0) & (i == 0))
        def _():
            pltpu.make_async_copy(wb_ref, block_of(me), local.at[0]).start()
            pltpu.make_async_copy(sb_ref, small_ref.at[me], local.at[1]).start()
            mine_to(sibling).start()
            for mask in range(1, N_DEV):
                remote(1, mask, mask, sb_ref, small_ref.at[me]).start()

        def load(step):
            return pltpu.make_async_copy(w_ref.at[:, pl.ds(pl.multiple_of(order_ref[step] * width, LANES), width)],
                                         w_buf.at[step % 2], load_sem.at[step % 2])

        for core in range(2):
            first, second = links(core)
            for step in range(N_DEV):
                at_step = (j == 0) & (i == 0) if step == 0 else (j == step - 1) & (i == n_i - 1)

                @pl.when(at_step & (cc == core))
                def _(step=step, first=first, second=second):
                    if step == 0:
                        mine_to(first).start()
                        pltpu.make_async_copy(wb_ref, block_of(me), local.at[0]).wait()
                    elif step == 1:
                        arrival(sibling).wait_recv()
                    elif step == 2:
                        arrival(first).wait_recv()
                        to_sibling(first).start()
                        mine_to(first).wait_send()
                        mine_to(second).start()
                        relay(first).start()
                    elif step == 3:
                        arrival(second | sibling).wait_recv()
                    elif step == 4:
                        arrival(second).wait_recv()
                        to_sibling(second).start()
                        relay(second).start()
                    elif step == 5:
                        arrival(first | sibling).wait_recv()
                    elif step == 6:
                        diagonal_half(0).wait_recv()
                        diagonal_half(1).wait_recv()
                        to_sibling(diagonal).start()
                    else:
                        arrival(diagonal | sibling).wait_recv()

        @pl.when((j == 0) & (i == 0))
        def _():
            load(0).start()

        @pl.when(i == 0)
        def _():
            load(j).wait()

        @pl.when((i == n_i - 1) & (j < N_DEV - 1))
        def _():
            load(j + 1).start()

        rows = pl.ds(pl.multiple_of(i * tm, tm), tm)

        @pl.when(j == 0)
        def _():
            h = _rms(x_ref[...], g_ref[...]).astype(BF16)
            h_all[rows, :] = h
            h_ref[...] = h

        proj_ref[...] = jnp.dot(h_all[rows, :], w_buf[j % 2], preferred_element_type=F32)

        @pl.when((j == N_DEV - 1) & (i == n_i - 1))
        def _():
            pltpu.make_async_copy(sb_ref, small_ref.at[me], local.at[1]).wait()
            for mask in range(1, N_DEV):
                remote(1, mask, mask, sb_ref, small_ref.at[me ^ mask]).wait_recv()
                remote(1, mask, mask, sb_ref, small_ref.at[me]).wait_send()
            mine_to(sibling).wait_send()
            for core in range(2):
                @pl.when(cc == core)
                def _(core=core):
                    mine_to(links(core)[1]).wait_send()
            for mask in Exchange.OTHER_CHIPS:
                to_sibling(mask).wait_send()
            relay(y_nbr).wait_send()
            relay(x_nbr).wait_send()

    def first_pass_row(j, i, order):
        return jnp.where(j == 0, i, n_i - 1), 0

    hbm = pl.BlockSpec(memory_space=pl.ANY)
    core = lax.axis_index("c")
    me = 4 * lax.axis_index("x") + 2 * lax.axis_index("y") + core
    order = (me ^ jnp.where(core == 1, jnp.array(block_order(1)), jnp.array(block_order(0)))).astype(jnp.int32)
    grid_spec = pltpu.PrefetchScalarGridSpec(
        num_scalar_prefetch=1, grid=(N_DEV, n_i),
        in_specs=[pl.BlockSpec((tm, d), first_pass_row), pl.BlockSpec((1, d), lambda j, i, order: (0, 0)), hbm, hbm],
        out_specs=[pl.BlockSpec((tm, width), lambda j, i, order: (i, order[j])), pl.BlockSpec((tm, d), first_pass_row), hbm, hbm],
        scratch_shapes=[pltpu.VMEM((s, d), BF16), pltpu.VMEM((2, d, width), BF16), pltpu.SemaphoreType.DMA((3, N_DEV - 1)),
                        pltpu.SemaphoreType.DMA((3, N_DEV - 1)), pltpu.SemaphoreType.DMA((2,)), pltpu.SemaphoreType.DMA((2,))])
    return pl.pallas_call(
        body, name="even_in", grid_spec=grid_spec,
        out_shape=[jax.ShapeDtypeStruct((s, N_DEV * width), F32), jax.ShapeDtypeStruct((s, d), BF16),
                   jax.ShapeDtypeStruct((d, N_DEV * width), w_block.dtype), jax.ShapeDtypeStruct((N_DEV,) + small_block.shape, small_block.dtype)],
        compiler_params=_params(2),
    )(order, x, g, w_block, small_block)


def rms_matmul(x, g, w, tm, tn, name, exchange=None):
    s, d = x.shape
    n = w.shape[1]
    tm = min(tm, s)

    def body(x_ref, g_ref, w_ref, o_ref, h_ref):
        @pl.when(pl.program_id(1) == 0)
        def _():
            h_ref[...] = _rms(x_ref[...], g_ref[...]).astype(BF16)

        o_ref[...] = jnp.dot(h_ref[...], w_ref[...], preferred_element_type=F32)

    return _call(
        body, name=name, grid=(s // tm, n // tn),
        in_specs=[pl.BlockSpec((tm, d), lambda i, j: (i, 0)), _full((1, d)), pl.BlockSpec((d, tn), lambda i, j: (0, j))],
        out_specs=[pl.BlockSpec((tm, tn), lambda i, j: (i, j)), pl.BlockSpec((tm, d), lambda i, j: (i, 0))],
        out_shape=[jax.ShapeDtypeStruct((s, n), F32), jax.ShapeDtypeStruct((s, d), BF16)],
        args=[x, g, w], exchange=exchange)


def _gla_out_bwd(du, r, osum, gn, do_ref, dr_ref, dgn_ref):
    silu_r, dsilu_r = _silu_and_grad(r)
    for head in range(GLA_HEADS):
        vl = slice(head * GLA_DV, (head + 1) * GLA_DV)
        o_h, g_h, du_h = osum[:, vl], gn[:, vl], du[:, vl]
        dr_ref[:, vl] = (du_h * _rms(o_h, g_h) * dsilu_r[:, vl]).astype(BF16)
        do_h, dg_h = _rms_bwd(o_h, g_h, du_h * silu_r[:, vl])
        do_ref[:, vl] = do_h
        dgn_ref[...] += dg_h


def normbwd_matmul_nt(y, g, dout, w, tn, name, exchange=None, gla=None):
    s, d = y.shape
    n = w.shape[0]
    tm = min(MM_TILE, s)

    def body(*refs):
        if gla is None:
            y_ref, g_ref, dout_ref, w_ref, du_ref, dy_ref, dg_ref = refs
        else:
            y_ref, g_ref, dout_ref, w_ref, r_ref, o_ref, gn_ref, do_ref, dr_ref, dy_ref, dg_ref, dgn_ref = refs
        i, j = pl.program_id(0), pl.program_id(1)

        @pl.when(j == 0)
        def _():
            dy, dg = _rms_bwd(y_ref[...], g_ref[...], dout_ref[...])
            dy_ref[...] = dy.astype(BF16)

            @pl.when(i == 0)
            def _():
                dg_ref[...] = jnp.zeros_like(dg_ref)
                if gla is not None:
                    dgn_ref[...] = jnp.zeros_like(dgn_ref)

            dg_ref[...] += dg

        du = lax.dot_general(dy_ref[...], w_ref[...], (((1,), (1,)), ((), ())), preferred_element_type=F32)
        if gla is None:
            du_ref[...] = du
        else:
            _gla_out_bwd(du, r_ref[...], o_ref[...], gn_ref[...], do_ref, dr_ref, dgn_ref)

    row = pl.BlockSpec((tm, d), lambda i, j: (i, 0))
    in_specs = [row, _full((1, d)), row, pl.BlockSpec((tn, d), lambda i, j: (j, 0))]
    args = [y, g, dout, w]
    tail_specs = [row, _full((1, d))]
    tail_shapes = [jax.ShapeDtypeStruct((s, d), BF16), jax.ShapeDtypeStruct((1, d), F32)]
    if gla is None:
        out_specs = [pl.BlockSpec((tm, tn), lambda i, j: (i, j))] + tail_specs
        out_shape = [jax.ShapeDtypeStruct((s, n), F32)] + tail_shapes
    else:
        proj, osum, gnorm = gla
        assert n == tn == D_MODEL
        in_specs += [pl.BlockSpec((tm, D_MODEL), lambda i, j: (i, 2)), row, _full(gnorm.shape)]
        args += [proj, osum, gnorm]
        out_specs = [row, row] + tail_specs + [_full((1, GLA_DV))]
        out_shape = [jax.ShapeDtypeStruct((s, D_MODEL), F32), jax.ShapeDtypeStruct((s, D_MODEL), BF16)] + tail_shapes + [
            jax.ShapeDtypeStruct((1, GLA_DV), F32)]
    return _call(body, name=name, grid=(s // tm, n // tn), in_specs=in_specs, out_specs=out_specs, out_shape=out_shape,
                 args=args, exchange=exchange)


def matmul_tn(a, b, tm, tn, ts, out_dtype, name, exchange=None, b_first=None):
    s, m = a.shape
    n = b.shape[1] + (0 if b_first is None else tn)
    ts = min(ts, s)
    n_k = s // ts
    dims = (((0,), (0,)), ((), ()))

    def body(*refs):
        if b_first is None:
            a_ref, b_ref, o_ref, acc = refs
        else:
            a_ref, first_ref, b_ref, o_ref, acc = refs
        j, k = pl.program_id(1), pl.program_id(2)

        @pl.when(k == 0)
        def _():
            acc[...] = jnp.zeros_like(acc)

        if b_first is None:
            acc[...] += lax.dot_general(a_ref[...], b_ref[...], dims, preferred_element_type=F32)
        else:
            @pl.when(j == 0)
            def _():
                acc[...] += lax.dot_general(a_ref[...], first_ref[...], dims, preferred_element_type=F32)

            @pl.when(j > 0)
            def _():
                acc[...] += lax.dot_general(a_ref[...], b_ref[...], dims, preferred_element_type=F32)

        @pl.when(k == n_k - 1)
        def _():
            o_ref[...] = acc[...].astype(out_dtype)

    if b_first is None:
        b_specs, b_args = [pl.BlockSpec((ts, tn), lambda i, j, k: (k, j))], [b]
    else:
        b_specs = [pl.BlockSpec((ts, tn), lambda i, j, k: (k, 0)), pl.BlockSpec((ts, tn), lambda i, j, k: (k, jnp.maximum(j - 1, 0)))]
        b_args = [b_first, b]
    return _call(
        body, name=name, grid=(m // tm, n // tn, n_k),
        in_specs=[pl.BlockSpec((ts, tm), lambda i, j, k: (k, i))] + b_specs,
        out_specs=pl.BlockSpec((tm, tn), lambda i, j, k: (i, j)),
        out_shape=jax.ShapeDtypeStruct((m, n), out_dtype),
        scratch_shapes=[pltpu.VMEM((tm, tn), F32)], args=[a] + b_args, exchange=exchange)


def matmul_nt_normbwd(dproj, w, x, g, dres, tm, tk, name, exchange=None, first=None):
    s, kt = dproj.shape
    kt += 0 if first is None else tk
    d = w.shape[0]
    tm = min(tm, s)
    n_k = kt // tk
    dims = (((1,), (1,)), ((), ()))

    def body(*refs):
        if first is None:
            a_ref, w_ref, x_ref, g_ref, r_ref, dx_ref, dg_ref, acc = refs
        else:
            first_ref, a_ref, w_ref, x_ref, g_ref, r_ref, dx_ref, dg_ref, acc = refs
        i, k = pl.program_id(0), pl.program_id(1)

        @pl.when(k == 0)
        def _():
            acc[...] = jnp.zeros_like(acc)

        if first is None:
            acc[...] += lax.dot_general(a_ref[...], w_ref[...], dims, preferred_element_type=F32)
        else:
            @pl.when(k == 0)
            def _():
                acc[...] += lax.dot_general(first_ref[...], w_ref[...], dims, preferred_element_type=F32)

            @pl.when(k > 0)
            def _():
                acc[...] += lax.dot_general(a_ref[...], w_ref[...], dims, preferred_element_type=F32)

        @pl.when(k == n_k - 1)
        def _():
            dx, dg = _rms_bwd(x_ref[...], g_ref[...], acc[...])
            dx_ref[...] = r_ref[...] + dx

            @pl.when(i == 0)
            def _():
                dg_ref[...] = jnp.zeros_like(dg_ref)

            dg_ref[...] += dg

    row = pl.BlockSpec((tm, d), lambda i, k: (i, 0))
    if first is None:
        a_specs, a_args = [pl.BlockSpec((tm, tk), lambda i, k: (i, k))], [dproj]
    else:
        a_specs = [pl.BlockSpec((tm, tk), lambda i, k: (i, 0)), pl.BlockSpec((tm, tk), lambda i, k: (i, jnp.maximum(k - 1, 0)))]
        a_args = [first, dproj]
    return _call(
        body, name=name, grid=(s // tm, n_k),
        in_specs=a_specs + [pl.BlockSpec((d, tk), lambda i, k: (0, k)), row, _full((1, d)), row],
        out_specs=[row, _full((1, d))],
        out_shape=[jax.ShapeDtypeStruct((s, d), F32), jax.ShapeDtypeStruct((1, d), F32)],
        scratch_shapes=[pltpu.VMEM((tm, d), F32)], args=a_args + [w, x, g, dres], exchange=exchange)


def _rg_conv(xa, before, after, cw, cb):
    return (cw[0:1, :] * _shift_rows(xa, before, after, -2) + cw[1:2, :] * _shift_rows(xa, before, after, -1)
            + cw[2:3, :] * xa + cw[3:4, :] * _shift_rows(xa, before, after, 1) + cb)


def _rg_gates(ua_h, gw_ref, gb_ref, c_h, direction, head):
    r = _sigmoid(_bdot(ua_h, gw_ref[2 * direction, head]) + gb_ref[2 * direction, head:head + 1, :])
    i = _sigmoid(_bdot(ua_h, gw_ref[2 * direction + 1, head]) + gb_ref[2 * direction + 1, head:head + 1, :])
    log_a = -c_h * r
    a = jnp.exp(log_a)
    beta_sq = -jnp.tanh(log_a) * (1.0 + a * a)
    inv_beta = lax.rsqrt(jnp.maximum(beta_sq, SMALLEST_NORMAL))
    return r, i, a, beta_sq * inv_beta, inv_beta


def even_gates_fwd(proj, conv_w, conv_b, gate_w, gate_b, lam, exchange=None):
    s = proj.shape[0]
    ts = min(2 * ROW_TILE, s)
    n_tiles = s // ts

    def body(xa_ref, xb_ref, xn_ref, cw_ref, cb_ref, gw_ref, gb_ref, lam_ref, o_ref, hf_ref, carry):
        @pl.when(pl.program_id(0) == 0)
        def _():
            carry[...] = jnp.zeros_like(carry)

        xa, before, after = _halo_load(xa_ref, xb_ref, xn_ref, n_tiles)
        ua = _rg_conv(xa, before, after, cw_ref[...], cb_ref[...])
        c = RG_C * _softplus(-lam_ref[...])
        ua_bf16 = ua.astype(BF16)
        for direction in range(2):
            for head in range(RG_HEADS):
                lanes = slice(head * RG_HEAD_DIM, (head + 1) * RG_HEAD_DIM)
                ua_h = ua[:, lanes]
                _, i, a, beta, _ = _rg_gates(ua_bf16[:, lanes], gw_ref, gb_ref, c[direction:direction + 1, lanes], direction, head)
                o_ref[2 * direction, :, lanes] = a
                o_ref[2 * direction + 1, :, lanes] = beta * (i * ua_h)
        _scan_tile(o_ref.at[0], o_ref.at[1], hf_ref, carry, False, False)

    return _call(
        body, name="even_gates_fwd", grid=(n_tiles,),
        in_specs=_halo_specs(ts, s, D_MODEL, 0) + [_full(conv_w.shape), _full(conv_b.shape), _full(gate_w.shape),
                                                   _full(gate_b.shape), _full(lam.shape)],
        out_specs=[pl.BlockSpec((4, ts, D_MODEL), lambda i: (0, i, 0)), pl.BlockSpec((ts, D_MODEL), lambda i: (i, 0))],
        out_shape=[jax.ShapeDtypeStruct((4, s, D_MODEL), F32), jax.ShapeDtypeStruct((s, D_MODEL), F32)],
        scratch_shapes=[pltpu.VMEM((SUBLANES, D_MODEL), F32)],
        args=[proj, proj, proj, conv_w, conv_b, gate_w, gate_b, lam], exchange=exchange)


def _scan_tile(a_ref, b_ref, h_ref, carry, reverse, b_times_a):
    ts, c = h_ref.shape
    n_blocks = ts // SUBLANES
    row = lax.broadcasted_iota(jnp.int32, (SUBLANES, c), 0)

    def block(j, h_in):
        r0 = pl.multiple_of((n_blocks - 1 - j if reverse else j) * SUBLANES, SUBLANES)
        a = a_ref[pl.ds(r0, SUBLANES), :]
        b = b_ref[pl.ds(r0, SUBLANES), :]
        if b_times_a:
            b = a * b
        for step in (1, 2, 4):
            shift = SUBLANES - step if reverse else step
            valid = row < SUBLANES - step if reverse else row >= step
            b = jnp.where(valid, a * pltpu.roll(b, shift, 0) + b, b)
            a = jnp.where(valid, a * pltpu.roll(a, shift, 0), a)
        h = a * h_in + b
        h_ref[pl.ds(r0, SUBLANES), :] = h
        return h[0:1, :] if reverse else h[SUBLANES - 1:SUBLANES, :]

    carry[0:1, :] = lax.fori_loop(0, n_blocks, block, carry[0:1, :])


def linear_scan(a_arr, a_idx, b_arr, b_idx, reverse, b_times_a, name, exchange=None):
    _, s, c = a_arr.shape
    ts = min(MM_TILE, s)
    n_tiles = s // ts

    def tile_of(i):
        return n_tiles - 1 - i if reverse else i

    def body(a_ref, b_ref, h_ref, carry):
        @pl.when(pl.program_id(0) == 0)
        def _():
            carry[...] = jnp.zeros_like(carry)

        _scan_tile(a_ref, b_ref, h_ref, carry, reverse, b_times_a)

    return _call(
        body, name=name, grid=(n_tiles,),
        in_specs=[pl.BlockSpec((None, ts, c), lambda i: (a_idx, tile_of(i), 0)),
                  pl.BlockSpec((None, ts, c), lambda i: (b_idx, tile_of(i), 0))],
        out_specs=pl.BlockSpec((ts, c), lambda i: (tile_of(i), 0)),
        out_shape=jax.ShapeDtypeStruct((s, c), F32),
        scratch_shapes=[pltpu.VMEM((SUBLANES, c), F32)], args=[a_arr, b_arr], exchange=exchange)


def _sc_conv(p, before, after, w):
    return w[0:1, :] * _shift_rows(p, before, after, -1) + w[1:2, :] * p + w[2:3, :] * _shift_rows(p, before, after, 1)


def even_mix_fwd(ab, hf, proj, sc_w, w_out, xres, g_post, exchange=None):
    s = proj.shape[0]
    ts = min(ROW_TILE, s)
    n_tiles = s // ts

    def tile(i):
        return n_tiles - 1 - i

    row = pl.BlockSpec((ts, D_MODEL), lambda i: (tile(i), 0))

    def col(c):
        return pl.BlockSpec((ts, D_MODEL), lambda i: (tile(i), c))

    def body(a_ref, b_ref, hf_ref, za_ref, xb_ref, xbb_ref, xbn_ref, gb_ref, gc_ref, gcb_ref, gcn_ref, zb_ref, w_ref,
             wo_ref, x_ref, g_ref, u_ref, hb_ref, y_ref, out_ref, carry):
        @pl.when(pl.program_id(0) == 0)
        def _():
            carry[...] = jnp.zeros_like(carry)

        _scan_tile(a_ref, b_ref, hb_ref, carry, True, False)
        xb, xb_before, xb_after = _halo_load(xb_ref, xbb_ref, xbn_ref, n_tiles, tile)
        gc, gc_before, gc_after = _halo_load(gc_ref, gcb_ref, gcn_ref, n_tiles, tile)
        silu_za, _ = _silu_and_grad(za_ref[...])
        silu_zb, _ = _silu_and_grad(zb_ref[...])
        u_ref[:, :D_MODEL] = ((hf_ref[...] + hb_ref[...]) * silu_za).astype(BF16)
        cv = _sc_conv(gc * xb, gc_before * xb_before, gc_after * xb_after, w_ref[...])
        u_ref[:, D_MODEL:] = (gb_ref[...] * cv * silu_zb).astype(BF16)
        y = jnp.dot(u_ref[...], wo_ref[...], preferred_element_type=F32)
        y_ref[...] = y
        out_ref[...] = x_ref[...] + _rms(y, g_ref[...])

    return _call(
        body, name="even_mix_fwd", grid=(n_tiles,),
        in_specs=[pl.BlockSpec((None, ts, D_MODEL), lambda i: (2, tile(i), 0)), pl.BlockSpec((None, ts, D_MODEL), lambda i: (3, tile(i), 0)),
                  row, col(1)] + _halo_specs(ts, s, D_MODEL, 2, tile) + [col(3)] + _halo_specs(ts, s, D_MODEL, 4, tile)
        + [col(5), _full(sc_w.shape), _full(w_out.shape), row, _full(g_post.shape)],
        out_specs=[pl.BlockSpec((ts, 2 * D_MODEL), lambda i: (tile(i), 0)), row, row, row],
        out_shape=[jax.ShapeDtypeStruct((s, 2 * D_MODEL), BF16)] + [jax.ShapeDtypeStruct((s, D_MODEL), F32)] * 3,
        scratch_shapes=[pltpu.VMEM((SUBLANES, D_MODEL), F32)],
        args=[ab, ab, hf, proj, proj, proj, proj, proj, proj, proj, proj, proj, sc_w, w_out, xres, g_post], exchange=exchange)


def even_mix_bwd(du, hf, hb, proj, sc_w, ab, exchange=None):
    s = proj.shape[0]
    ts = min(ROW_TILE, s)
    n_tiles = s // ts
    row = pl.BlockSpec((ts, D_MODEL), lambda i: (i, 0))

    def body(dya_ref, dyb_ref, dybb_ref, dybn_ref, hf_ref, hb_ref, za_ref, xb_ref, xbb_ref, xbn_ref,
             gb_ref, gbb_ref, gbn_ref, gc_ref, gcb_ref, gcn_ref, zb_ref, zbb_ref, zbn_ref, w_ref, a_ref,
             dh_ref, dp_ref, dw_ref, adj_ref, carry):
        @pl.when(pl.program_id(0) == 0)
        def _():
            carry[...] = jnp.zeros_like(carry)

        dyb, dyb_before, dyb_after = _halo_load(dyb_ref, dybb_ref, dybn_ref, n_tiles)
        xb, xb_before, xb_after = _halo_load(xb_ref, xbb_ref, xbn_ref, n_tiles)
        gb, gb_before, gb_after = _halo_load(gb_ref, gbb_ref, gbn_ref, n_tiles)
        gc, gc_before, gc_after = _halo_load(gc_ref, gcb_ref, gcn_ref, n_tiles)
        zb, zb_before, zb_after = _halo_load(zb_ref, zbb_ref, zbn_ref, n_tiles)
        w = w_ref[...]
        dya, za = dya_ref[...], za_ref[...]
        silu_za, dsilu_za = _silu_and_grad(za)
        dh_ref[...] = dya * silu_za
        _scan_tile(a_ref, dh_ref, adj_ref, carry, False, True)
        dp_ref[:, 0:D_MODEL] = (dya * (hf_ref[...] + hb_ref[...]) * dsilu_za).astype(BF16)

        silu_zb, dsilu_zb = _silu_and_grad(zb)
        p, p_before, p_after = gc * xb, gc_before * xb_before, gc_after * xb_after
        cv = _sc_conv(p, p_before, p_after, w)
        dcv = dyb * gb * silu_zb
        dcv_before = dyb_before * gb_before * _silu_and_grad(zb_before)[0]
        dcv_after = dyb_after * gb_after * _silu_and_grad(zb_after)[0]
        dpp = (w[0:1, :] * _shift_rows(dcv, dcv_before, dcv_after, 1) + w[1:2, :] * dcv
               + w[2:3, :] * _shift_rows(dcv, dcv_before, dcv_after, -1))
        dp_ref[:, D_MODEL:2 * D_MODEL] = (dpp * gc).astype(BF16)
        dp_ref[:, 2 * D_MODEL:3 * D_MODEL] = (dyb * cv * silu_zb).astype(BF16)
        dp_ref[:, 3 * D_MODEL:4 * D_MODEL] = (dpp * xb).astype(BF16)
        dp_ref[:, 4 * D_MODEL:5 * D_MODEL] = (dyb * gb * cv * dsilu_zb).astype(BF16)

        @pl.when(pl.program_id(0) == 0)
        def _():
            dw_ref[...] = jnp.zeros_like(dw_ref)

        dw_ref[0:1, :] += jnp.sum(dcv * _shift_rows(p, p_before, p_after, -1), axis=0, keepdims=True)
        dw_ref[1:2, :] += jnp.sum(dcv * p, axis=0, keepdims=True)
        dw_ref[2:3, :] += jnp.sum(dcv * _shift_rows(p, p_before, p_after, 1), axis=0, keepdims=True)

    return _call(
        body, name="even_mix_bwd", grid=(n_tiles,),
        in_specs=[row] + _halo_specs(ts, s, D_MODEL, 1) + [row, row, pl.BlockSpec((ts, D_MODEL), lambda i: (i, 1))]
        + _halo_specs(ts, s, D_MODEL, 2) + _halo_specs(ts, s, D_MODEL, 3) + _halo_specs(ts, s, D_MODEL, 4)
        + _halo_specs(ts, s, D_MODEL, 5) + [_full(sc_w.shape), pl.BlockSpec((None, ts, D_MODEL), lambda i: (2, i, 0))],
        out_specs=[row, pl.BlockSpec((ts, 5 * D_MODEL), lambda i: (i, 0)), _full(sc_w.shape), row],
        out_shape=[jax.ShapeDtypeStruct((s, D_MODEL), F32), jax.ShapeDtypeStruct((s, 5 * D_MODEL), BF16),
                   jax.ShapeDtypeStruct(sc_w.shape, F32), jax.ShapeDtypeStruct((s, D_MODEL), F32)],
        scratch_shapes=[pltpu.VMEM((SUBLANES, D_MODEL), F32)],
        args=[du, du, du, du, hf, hb, proj, *([proj] * 12), sc_w, ab], exchange=exchange)


def even_gates_bwd(proj, adj_f, adj_b, hf, hb, dh, conv_w, conv_b, gate_w, gate_b, lam, exchange=None):
    s = proj.shape[0]
    ts = min(2 * ROW_TILE, s)
    n_tiles = s // ts
    row = pl.BlockSpec((ts, D_MODEL), lambda i: (i, 0))

    def body(xa_ref, xab_ref, xan_ref, af_ref, afb_ref, afn_ref, ab_ref, abb_ref, abn_ref,
             hf_ref, hfb_ref, hfn_ref, hb_ref, hbb_ref, hbn_ref, dh_ref,
             cw_ref, cb_ref, gw_ref, gb_ref, lam_ref, dua_ref, dgw_ref, dgb_ref, dlam_ref):
        @pl.when(pl.program_id(0) == 0)
        def _():
            dgw_ref[...] = jnp.zeros_like(dgw_ref)
            dgb_ref[...] = jnp.zeros_like(dgb_ref)
            dlam_ref[...] = jnp.zeros_like(dlam_ref)

        xa, before, after = _halo_load(xa_ref, xab_ref, xan_ref, n_tiles)
        ua = _rg_conv(xa, before, after, cw_ref[...], cb_ref[...])
        lam_v = lam_ref[...]
        c = RG_C * _softplus(-lam_v)
        dc_dlam = -RG_C * _sigmoid(-lam_v)
        dh = dh_ref[...]
        adj = (_halo_load(af_ref, afb_ref, afn_ref, n_tiles), _halo_load(ab_ref, abb_ref, abn_ref, n_tiles))
        hs = (_halo_load(hf_ref, hfb_ref, hfn_ref, n_tiles), _halo_load(hb_ref, hbb_ref, hbn_ref, n_tiles))
        dua = jnp.zeros_like(ua)
        ua_bf16 = ua.astype(BF16)
        for direction in range(2):
            step = 1 if direction == 0 else -1
            g = dh + _shift_rows(*adj[direction], step)
            da_all = g * _shift_rows(*hs[direction], -step)
            dua_parts = []
            for head in range(RG_HEADS):
                lanes = slice(head * RG_HEAD_DIM, (head + 1) * RG_HEAD_DIM)
                ua_h = ua[:, lanes]
                c_h = c[direction:direction + 1, lanes]
                ua_hb = ua_bf16[:, lanes]
                r, i, a, beta, inv_beta = _rg_gates(ua_hb, gw_ref, gb_ref, c_h, direction, head)
                db_beta = g[:, lanes] * beta
                d_i = db_beta * ua_h
                dbeta = g[:, lanes] * (i * ua_h)
                dlog_a = (da_all[:, lanes] - dbeta * a * inv_beta) * a
                dpr = -c_h * dlog_a * r * (1.0 - r)
                dpi = d_i * i * (1.0 - i)
                dpr_b, dpi_b = dpr.astype(BF16), dpi.astype(BF16)
                dua_parts.append(db_beta * i + _bdot_nt(dpr_b, gw_ref[2 * direction, head])
                                 + _bdot_nt(dpi_b, gw_ref[2 * direction + 1, head]))
                dgw_ref[2 * direction, head] += _bdot_tn(ua_hb, dpr_b)
                dgw_ref[2 * direction + 1, head] += _bdot_tn(ua_hb, dpi_b)
                dgb_ref[2 * direction, head:head + 1, :] += jnp.sum(dpr, axis=0, keepdims=True)
                dgb_ref[2 * direction + 1, head:head + 1, :] += jnp.sum(dpi, axis=0, keepdims=True)
                dlam_ref[direction:direction + 1, lanes] += (
                    jnp.sum(-r * dlog_a, axis=0, keepdims=True) * dc_dlam[direction:direction + 1, lanes])
            dua = dua + jnp.concatenate(dua_parts, axis=1)
        dua_ref[...] = dua

    return _call(
        body, name="even_gates_bwd", grid=(n_tiles,),
        in_specs=_halo_specs(ts, s, D_MODEL, 0) * 5 + [row] + [_full(conv_w.shape), _full(conv_b.shape), _full(gate_w.shape),
                                                             _full(gate_b.shape), _full(lam.shape)],
        out_specs=[row, _full(gate_w.shape), _full(gate_b.shape), _full(lam.shape)],
        out_shape=[jax.ShapeDtypeStruct((s, D_MODEL), F32), jax.ShapeDtypeStruct(gate_w.shape, F32),
                   jax.ShapeDtypeStruct(gate_b.shape, F32), jax.ShapeDtypeStruct(lam.shape, F32)],
        args=[proj, proj, proj, adj_f, adj_f, adj_f, adj_b, adj_b, adj_b, hf, hf, hf, hb, hb, hb, dh, conv_w, conv_b, gate_w,
              gate_b, lam], exchange=exchange)


def rg_conv_bwd(dua, proj, conv_w, exchange=None):
    s = proj.shape[0]
    ts = min(2 * ROW_TILE, s)
    n_tiles = s // ts

    def body(du_ref, dub_ref, dun_ref, xa_ref, xab_ref, xan_ref, cw_ref, dp_ref, dw_ref, db_ref):
        @pl.when(pl.program_id(0) == 0)
        def _():
            dw_ref[...] = jnp.zeros_like(dw_ref)
            db_ref[...] = jnp.zeros_like(db_ref)

        dua, dua_before, dua_after = _halo_load(du_ref, dub_ref, dun_ref, n_tiles)
        xa, xa_before, xa_after = _halo_load(xa_ref, xab_ref, xan_ref, n_tiles)
        cw = cw_ref[...]
        dxa = (cw[0:1, :] * _shift_rows(dua, dua_before, dua_after, 2) + cw[1:2, :] * _shift_rows(dua, dua_before, dua_after, 1)
               + cw[2:3, :] * dua + cw[3:4, :] * _shift_rows(dua, dua_before, dua_after, -1))
        dp_ref[...] = dxa.astype(BF16)
        for tap, offset in enumerate((-2, -1, 0, 1)):
            shifted = xa if offset == 0 else _shift_rows(xa, xa_before, xa_after, offset)
            dw_ref[tap:tap + 1, :] += jnp.sum(dua * shifted, axis=0, keepdims=True)
        db_ref[...] += jnp.sum(dua, axis=0, keepdims=True)

    return _call(
        body, name="rg_conv_bwd", grid=(n_tiles,),
        in_specs=_halo_specs(ts, s, D_MODEL, 0) * 2 + [_full(conv_w.shape)],
        out_specs=[pl.BlockSpec((ts, D_MODEL), lambda i: (i, 0)), _full(conv_w.shape), _full((1, D_MODEL))],
        out_shape=[jax.ShapeDtypeStruct((s, D_MODEL), BF16), jax.ShapeDtypeStruct(conv_w.shape, F32),
                   jax.ShapeDtypeStruct((1, D_MODEL), F32)],
        args=[dua, dua, dua, proj, proj, proj, conv_w], exchange=exchange)


def _split3(x):
    x1 = x.astype(BF16)
    rest = x - x1.astype(F32)
    x2 = rest.astype(BF16)
    return x1, x2, (rest - x2.astype(F32)).astype(BF16)


def _chunk_sum_matrix(t, reverse, transpose):
    i = lax.broadcasted_iota(jnp.int32, (t, t), 0)
    j = lax.broadcasted_iota(jnp.int32, (t, t), 1)
    if transpose:
        i, j = j, i
    same = (i // GLA_CHUNK) == (j // GLA_CHUNK)
    return jnp.where(same & ((j >= i) if reverse else (j <= i)), 1.0, 0.0).astype(BF16)


def _exact_dot(m, x):
    return sum(jnp.dot(m, part, preferred_element_type=F32) for part in _split3(x))


def _chunk_mask(t, reverse):
    i = lax.broadcasted_iota(jnp.int32, (t, t), 0)
    j = lax.broadcasted_iota(jnp.int32, (t, t), 1)
    return ((i // GLA_CHUNK) == (j // GLA_CHUNK)) & ((j >= i) if reverse else (j <= i))


def _chunk_rows(c):
    return slice(c * GLA_CHUNK, (c + 1) * GLA_CHUNK)


def _gla_gate(lr, wg, bg):
    z = _bdot(lr, wg) + bg
    log_alpha = (jnp.minimum(z, 0.0) - jnp.log(1.0 + jnp.exp(-jnp.abs(z)))) * (1.0 / GLA_NORMALIZER)
    return z, log_alpha


def _gla_tile_terms(q, k, bcum, reverse):
    n_chunks = q.shape[0] // GLA_CHUNK
    totals = []
    for c in range(n_chunks):
        edge = c * GLA_CHUNK if reverse else (c + 1) * GLA_CHUNK - 1
        totals.append(bcum[edge:edge + 1, :])
    btot = jnp.concatenate([jnp.broadcast_to(total, (GLA_CHUNK, total.shape[1])) for total in totals], axis=0)
    e_pos, e_neg, e_st = jnp.exp(bcum), jnp.exp(-bcum), jnp.exp(btot - bcum)
    return q * (GLA_DK ** -0.5) * e_pos, k * e_neg, k * e_st, e_pos, e_neg, e_st, [jnp.exp(total) for total in totals]


def _gla_specs(t, n_tiles, reverse_order):
    def tile(i):
        return n_tiles - 1 - i if reverse_order else i

    return tile, [
        pl.BlockSpec((t, GLA_KEY), lambda i: (tile(i), 0)),
        pl.BlockSpec((t, GLA_KEY), lambda i: (tile(i), 1)),
        pl.BlockSpec((t, D_MODEL), lambda i: (tile(i), 1)),
        pl.BlockSpec((t, LANES), lambda i: (tile(i), (ODD_IN_PAD - LANES) // LANES)),
    ]


def gla_fwd(proj, wg, bg, reverse, o_other=None, gnorm=None, post=None):
    s = proj.shape[0]
    t = min(GLA_TILE, s)
    n_tiles = s // t
    n_chunks = t // GLA_CHUNK
    final = o_other is not None
    tile, specs = _gla_specs(t, n_tiles, reverse)

    def body(*refs):
        if final:
            (q_ref, k_ref, v_ref, lr_ref, wg_ref, bg_ref, oo_ref, r_ref, gn_ref, wo_ref, x_ref, gp_ref, t_ref,
             osum_ref, u_ref, st_ref, y_ref, dout_ref, loss_ref, state) = refs
        else:
            q_ref, k_ref, v_ref, lr_ref, wg_ref, bg_ref, o_ref, st_ref, state = refs
            osum_ref = o_ref

        @pl.when(pl.program_id(0) == 0)
        def _():
            state[...] = jnp.zeros_like(state)

        _, log_alpha = _gla_gate(lr_ref[...], wg_ref[...], bg_ref[...])
        bcum = _exact_dot(_chunk_sum_matrix(t, reverse, False), log_alpha)
        q, k, v = q_ref[...], k_ref[...], v_ref[...]
        q_in, k_in, k_st, _, _, _, decays = _gla_tile_terms(q, k, bcum, reverse)
        mask = _chunk_mask(t, reverse)
        order = list(range(n_chunks))[::-1] if reverse else list(range(n_chunks))
        intra, increments = [], []
        for head in range(GLA_HEADS):
            kl = slice(head * GLA_DK, (head + 1) * GLA_DK)
            vl = slice(head * GLA_DV, (head + 1) * GLA_DV)
            scores = jnp.where(mask, _bdot_nt(q_in[:, kl], k_in[:, kl]), 0.0)
            intra.append(_bdot(scores, v[:, vl]))
            increments.append([_bdot_tn(v[_chunk_rows(c), vl], k_st[_chunk_rows(c), kl]) for c in range(n_chunks)])
        for head in range(GLA_HEADS):
            kl = slice(head * GLA_DK, (head + 1) * GLA_DK)
            vl = slice(head * GLA_DV, (head + 1) * GLA_DV)
            running = state[head]
            before = [None] * n_chunks
            for c in order:
                before[c] = running
                st_ref[c, head] = running
                running = running * decays[c][:, kl] + increments[head][c]
            state[head] = running
            inter = [_bdot_nt(q_in[_chunk_rows(c), kl], before[c]) for c in range(n_chunks)]
            osum_ref[:, vl] = intra[head] + jnp.concatenate(inter, axis=0)
        if final:
            osum = osum_ref[...] + oo_ref[...]
            osum_ref[...] = osum
            silu_r, _ = _silu_and_grad(r_ref[...])
            gn = gn_ref[...]
            for head in range(GLA_HEADS):
                vl = slice(head * GLA_DV, (head + 1) * GLA_DV)
                u_ref[:, vl] = (_rms(osum[:, vl], gn[:, vl]) * silu_r[:, vl]).astype(BF16)

            @pl.when(pl.program_id(0) == 0)
            def _():
                loss_ref[...] = jnp.zeros_like(loss_ref)

            y = jnp.dot(u_ref[...], wo_ref[...], preferred_element_type=F32)
            y_ref[...] = y
            diff = x_ref[...] + _rms(y, gp_ref[...]) - t_ref[...]
            dout_ref[...] = diff * (1.0 / D_MODEL)
            loss_ref[...] += 0.5 * jnp.sum(jnp.mean(diff * diff, axis=-1, keepdims=True))

    row = pl.BlockSpec((t, D_MODEL), lambda i: (tile(i), 0))
    st_spec = pl.BlockSpec((n_chunks, GLA_HEADS, GLA_DV, GLA_DK), lambda i: (tile(i), 0, 0, 0))
    st_shape = jax.ShapeDtypeStruct((s // GLA_CHUNK, GLA_HEADS, GLA_DV, GLA_DK), F32)
    in_specs = specs + [_full(wg.shape), _full(bg.shape)]
    args = [proj, proj, proj, proj, wg, bg]
    if final:
        w_out, xres, g_post, target = post
        in_specs += [row, pl.BlockSpec((t, D_MODEL), lambda i: (tile(i), 2)), _full(gnorm.shape), _full(w_out.shape), row,
                     _full(g_post.shape), row]
        args += [o_other, proj, gnorm, w_out, xres, g_post, target]
        out_specs = [row, row, st_spec, row, row, _full((SUBLANES, LANES))]
        out_shape = [jax.ShapeDtypeStruct((s, D_MODEL), F32), jax.ShapeDtypeStruct((s, D_MODEL), BF16), st_shape,
                     jax.ShapeDtypeStruct((s, D_MODEL), F32), jax.ShapeDtypeStruct((s, D_MODEL), F32),
                     jax.ShapeDtypeStruct((SUBLANES, LANES), F32)]
    else:
        out_specs = [row, st_spec]
        out_shape = [jax.ShapeDtypeStruct((s, D_MODEL), F32), st_shape]
    return pl.pallas_call(
        body, name="gla_fwd_rev" if reverse else "gla_fwd", grid=(n_tiles,), in_specs=in_specs, out_specs=out_specs,
        out_shape=out_shape, scratch_shapes=[pltpu.VMEM((GLA_HEADS, GLA_DV, GLA_DK), F32)], compiler_params=_params(1),
    )(*args)


def gla_bwd(proj, wg, bg, do, states, reverse, first=None):
    s = proj.shape[0]
    t = min(GLA_TILE, s)
    n_tiles = s // t
    n_chunks = t // GLA_CHUNK
    final = first is not None
    tile, specs = _gla_specs(t, n_tiles, not reverse)

    def body(*refs):
        if final:
            (q_ref, k_ref, v_ref, lr_ref, wg_ref, bg_ref, do_ref, st_ref, dqkv1_ref, dlr1_ref, dr_ref,
             dp_ref, dwg_ref, dbg_ref, dstate, dqkv, dbc, dbt) = refs
        else:
            (q_ref, k_ref, v_ref, lr_ref, wg_ref, bg_ref, do_ref, st_ref,
             dqkv, dlr_ref, dwg_ref, dbg_ref, dstate, dbc, dbt) = refs

        @pl.when(pl.program_id(0) == 0)
        def _():
            dstate[...] = jnp.zeros_like(dstate)
            dwg_ref[...] = jnp.zeros_like(dwg_ref)
            dbg_ref[...] = jnp.zeros_like(dbg_ref)

        lr, wg_v = lr_ref[...], wg_ref[...]
        z, log_alpha = _gla_gate(lr, wg_v, bg_ref[...])
        bcum = _exact_dot(_chunk_sum_matrix(t, reverse, False), log_alpha)
        q, k, v, do_v = q_ref[...], k_ref[...], v_ref[...], do_ref[...]
        q_in, k_in, k_st, e_pos, e_neg, e_st, decays = _gla_tile_terms(q, k, bcum, reverse)
        mask = _chunk_mask(t, reverse)
        order = list(range(n_chunks)) if reverse else list(range(n_chunks))[::-1]
        q_b, k_b, ks_b, v_b, do_b = (a.astype(BF16) for a in (q_in, k_in, k_st, v, do_v))
        dq_intra, dk_intra, dv_intra, increments = [], [], [], []
        for head in range(GLA_HEADS):
            kl = slice(head * GLA_DK, (head + 1) * GLA_DK)
            vl = slice(head * GLA_DV, (head + 1) * GLA_DV)
            scores = jnp.where(mask, _bdot_nt(q_b[:, kl], k_b[:, kl]), 0.0).astype(BF16)
            dscores = jnp.where(mask, _bdot_nt(do_b[:, vl], v_b[:, vl]), 0.0).astype(BF16)
            dv_intra.append(_bdot_tn(scores, do_b[:, vl]))
            dq_intra.append(_bdot(dscores, k_b[:, kl]))
            dk_intra.append(_bdot_tn(dscores, q_b[:, kl]))
            increments.append([_bdot_tn(do_b[_chunk_rows(c), vl], q_b[_chunk_rows(c), kl]) for c in range(n_chunks)])
        after_all, ddecay_all = [], []
        for head in range(GLA_HEADS):
            kl = slice(head * GLA_DK, (head + 1) * GLA_DK)
            running = dstate[head]
            after, ddecay = [None] * n_chunks, [None] * n_chunks
            for c in order:
                after[c] = running
                ddecay[c] = jnp.sum(running * st_ref[c, head], axis=0, keepdims=True)
                running = running * decays[c][:, kl] + increments[head][c]
            dstate[head] = running
            after_all.append(after)
            ddecay_all.append(ddecay)
        for head in range(GLA_HEADS):
            kl = slice(head * GLA_DK, (head + 1) * GLA_DK)
            vl = slice(head * GLA_DV, (head + 1) * GLA_DV)
            after, ddecay = after_all[head], ddecay_all[head]
            dq_inter = jnp.concatenate([_bdot(do_b[_chunk_rows(c), vl], st_ref[c, head]) for c in range(n_chunks)], axis=0)
            dv_inter = jnp.concatenate([_bdot_nt(ks_b[_chunk_rows(c), kl], after[c]) for c in range(n_chunks)], axis=0)
            dk_st = jnp.concatenate([_bdot(v_b[_chunk_rows(c), vl], after[c]) for c in range(n_chunks)], axis=0)
            dq_in = dq_intra[head] + dq_inter
            ks_h = k_st[:, kl]
            dqkv[:, 2 * GLA_KEY + head * GLA_DV:2 * GLA_KEY + (head + 1) * GLA_DV] = dv_intra[head] + dv_inter
            dqkv[:, kl] = dq_in * (GLA_DK ** -0.5) * e_pos[:, kl]
            dqkv[:, GLA_KEY + head * GLA_DK:GLA_KEY + (head + 1) * GLA_DK] = dk_intra[head] * e_neg[:, kl] + dk_st * e_st[:, kl]
            dbc[:, kl] = dq_in * q_in[:, kl] - dk_intra[head] * k_in[:, kl] - dk_st * ks_h
            weighted = dk_st * ks_h
            for c in range(n_chunks):
                dbtot = jnp.sum(weighted[_chunk_rows(c)], axis=0, keepdims=True) + ddecay[c] * decays[c][:, kl]
                dbt[_chunk_rows(c), kl] = jnp.broadcast_to(dbtot, (GLA_CHUNK, GLA_DK))
        dlog_alpha = _exact_dot(_chunk_sum_matrix(t, reverse, True), dbc[...]) + dbt[...]
        dz = dlog_alpha * _sigmoid(-z) * (1.0 / GLA_NORMALIZER)
        dlr = _bdot_nt(dz, wg_v)
        dwg_ref[...] += _bdot_tn(lr, dz)
        dbg_ref[...] += jnp.sum(dz, axis=0, keepdims=True)
        if final:
            dp_ref[:, :2 * D_MODEL] = (dqkv[...] + dqkv1_ref[...]).astype(BF16)
            dp_ref[:, 2 * D_MODEL:3 * D_MODEL] = dr_ref[...]
            dp_ref[:, 3 * D_MODEL:] = (dlr + dlr1_ref[...]).astype(BF16)
        else:
            dlr_ref[...] = dlr

    row = pl.BlockSpec((t, D_MODEL), lambda i: (tile(i), 0))
    wide = pl.BlockSpec((t, 2 * D_MODEL), lambda i: (tile(i), 0))
    narrow = pl.BlockSpec((t, LANES), lambda i: (tile(i), 0))
    st_spec = pl.BlockSpec((n_chunks, GLA_HEADS, GLA_DV, GLA_DK), lambda i: (tile(i), 0, 0, 0))
    in_specs = specs + [_full(wg.shape), _full(bg.shape), row, st_spec]
    args = [proj, proj, proj, proj, wg, bg, do, states]
    acc_specs = [_full(wg.shape), _full(bg.shape)]
    acc_shapes = [jax.ShapeDtypeStruct(wg.shape, F32), jax.ShapeDtypeStruct(bg.shape, F32)]
    scratch = [pltpu.VMEM((GLA_HEADS, GLA_DV, GLA_DK), F32)]
    work = [pltpu.VMEM((t, GLA_KEY), F32), pltpu.VMEM((t, GLA_KEY), F32)]
    if final:
        in_specs += [wide, narrow, row]
        args += list(first)
        out_specs = [pl.BlockSpec((t, ODD_IN_PAD), lambda i: (tile(i), 0))] + acc_specs
        out_shape = [jax.ShapeDtypeStruct((s, ODD_IN_PAD), BF16)] + acc_shapes
        scratch += [pltpu.VMEM((t, 2 * D_MODEL), F32)] + work
    else:
        out_specs = [wide, narrow] + acc_specs
        out_shape = [jax.ShapeDtypeStruct((s, 2 * D_MODEL), F32), jax.ShapeDtypeStruct((s, LANES), F32)] + acc_shapes
        scratch += work
    return pl.pallas_call(
        body, name="gla_bwd_rev" if reverse else "gla_bwd", grid=(n_tiles,), in_specs=in_specs, out_specs=out_specs,
        out_shape=out_shape, scratch_shapes=scratch, compiler_params=_params(1),
    )(*args)


def column_blocks(a, width):
    r, c = a.shape
    window = -(-(width + LANES) // LANES) * LANES
    padded = -(-width // LANES) * LANES
    assert window <= c

    def body(a_ref, o_ref):
        row = lax.broadcasted_iota(jnp.int32, (window, padded), 0)
        col = lax.broadcasted_iota(jnp.int32, (window, padded), 1)
        for j in range(N_DEV):
            start = min(j * width // LANES * LANES, c - window)
            pick = jnp.where((row == col + (j * width - start)) & (col < width), 1.0, 0.0).astype(BF16)
            picked = jnp.dot(a_ref[:, start:start + window], pick, preferred_element_type=F32)
            o_ref[j] = picked[:, :width].astype(o_ref.dtype)

    return pl.pallas_call(
        body, name="column_blocks", grid=(1,), in_specs=[_full((r, c))], out_specs=_full((N_DEV, r, width)),
        out_shape=jax.ShapeDtypeStruct((N_DEV, r, width), a.dtype), compiler_params=_params(1),
    )(a)


def columns_from_blocks(parts, total):
    width = parts[0].shape[2]
    rows = [p.shape[1] for p in parts]
    window = -(-(width + LANES) // LANES) * LANES

    def body(*refs):
        part_refs, o_ref, acc = refs[:len(parts)], refs[len(parts)], refs[len(parts) + 1]
        acc[...] = jnp.zeros_like(acc)
        row = lax.broadcasted_iota(jnp.int32, (width, window), 0)
        col = lax.broadcasted_iota(jnp.int32, (width, window), 1)
        for j in range(N_DEV):
            start = min(j * width // LANES * LANES, total - window)
            place = jnp.where(col == row + (j * width - start), 1.0, 0.0).astype(BF16)
            at = 0
            for part_ref, r in zip(part_refs, rows):
                acc[at:at + r, start:start + window] += jnp.dot(part_ref[j], place, preferred_element_type=F32)
                at += r
        o_ref[...] = acc[...].astype(o_ref.dtype)

    return pl.pallas_call(
        body, name="columns_from_blocks", grid=(1,), in_specs=[_full(p.shape) for p in parts], out_specs=_full((sum(rows), total)),
        out_shape=jax.ShapeDtypeStruct((sum(rows), total), parts[0].dtype), scratch_shapes=[pltpu.VMEM((sum(rows), total), F32)],
        compiler_params=_params(1),
    )(*parts)


def pair_sum(grad, from_sibling):
    n_chips, r, w = from_sibling.shape

    def body(even_ref, odd_ref, sib_ref, o_ref):
        mine = jnp.where(lax.axis_index("c") == 1, odd_ref[...], even_ref[...])
        o_ref[...] = (mine.astype(F32) + sib_ref[...].astype(F32)).astype(o_ref.dtype)

    return pl.pallas_call(
        body, name="pair_sum", grid=(n_chips,),
        in_specs=[pl.BlockSpec((r, w), lambda k: (0, 2 * k)), pl.BlockSpec((r, w), lambda k: (0, 2 * k + 1)),
                  pl.BlockSpec((None, r, w), lambda k: (k, 0, 0))],
        out_specs=pl.BlockSpec((None, r, w), lambda k: (k, 0, 0)),
        out_shape=jax.ShapeDtypeStruct(from_sibling.shape, from_sibling.dtype), compiler_params=_params(1),
    )(grad, grad, from_sibling)


def _adamw_update(g, w, m, v):
    new_m = ADAM_B1 * m + (1.0 - ADAM_B1) * g
    new_v = ADAM_B2 * v + (1.0 - ADAM_B2) * (g * g)
    m_hat = new_m / (1.0 - ADAM_B1 ** ADAM_STEP)
    v_hat = new_v / (1.0 - ADAM_B2 ** ADAM_STEP)
    return -ADAM_LR * (m_hat / (jnp.sqrt(v_hat) + ADAM_EPS) + ADAM_WD * w), new_m, new_v


def sum_parts(parts, name):
    _, r, c = parts.shape

    def body(p_ref, o_ref):
        total = p_ref[0].astype(F32)
        for j in range(1, N_DEV):
            total = total + p_ref[j].astype(F32)
        o_ref[...] = total

    return pl.pallas_call(body, name=name, in_specs=[_full(parts.shape)], out_specs=_full((r, c)), grid=(1,),
                          out_shape=jax.ShapeDtypeStruct((r, c), F32), compiler_params=_params(1))(parts)


def adamw(parts, w, m, v, name, exchange=None):
    n, r, c = parts.shape
    tr = r
    while tr * c * 4 > ADAMW_BLOCK_BYTES and tr % (2 * SUBLANES) == 0:
        tr //= 2

    def body(p_ref, w_ref, m_ref, v_ref, g_ref, d_ref, nm_ref, nv_ref):
        g = p_ref[0].astype(F32)
        for j in range(1, n):
            g = g + p_ref[j].astype(F32)
        g_ref[...] = g
        d_ref[...], nm_ref[...], nv_ref[...] = _adamw_update(g, w_ref[...], m_ref[...], v_ref[...])

    row = pl.BlockSpec((tr, c), lambda i: (i, 0))
    return _call(
        body, name=name, grid=(r // tr,),
        in_specs=[pl.BlockSpec((n, tr, c), lambda i: (0, i, 0)), row, row, row], out_specs=[row] * 4,
        out_shape=[jax.ShapeDtypeStruct((r, c), F32)] * 4, args=[parts, w, m, v], exchange=exchange)


def adamw_transposed(parts, w_t, m_t, v_t, name, exchange=None):
    n, r, c = parts.shape
    tr = min(MM_TILE, r)

    def body(p_ref, w_ref, m_ref, v_ref, g_ref, d_ref, nm_ref, nv_ref):
        eye = (lax.broadcasted_iota(jnp.int32, (tr, tr), 0) == lax.broadcasted_iota(jnp.int32, (tr, tr), 1)).astype(BF16)
        g = _bdot_tn(p_ref[0], eye)
        for j in range(1, n):
            g = g + _bdot_tn(p_ref[j], eye)
        g_ref[:, 0, :] = g
        d_ref[:, 0, :], nm_ref[:, 0, :], nv_ref[:, 0, :] = _adamw_update(g, w_ref[:, 0, :], m_ref[:, 0, :], v_ref[:, 0, :])

    col = pl.BlockSpec((c, 1, tr), lambda i: (0, 0, i))
    return _call(
        body, name=name, grid=(r // tr,), in_specs=[pl.BlockSpec((n, tr, c), lambda i: (0, i, 0)), col, col, col], out_specs=[col] * 4,
        out_shape=[jax.ShapeDtypeStruct((c, 1, r), F32)] * 4, args=[parts, w_t, m_t, v_t], exchange=exchange)


def _small_views(shape):
    if len(shape) == 2:
        return [((slice(None), slice(None)), (slice(None), slice(None)))]
    if len(shape) == 3:
        return [((slice(None), slice(None)), (0,))]
    rows = shape[2]
    return [((slice(k * rows, (k + 1) * rows), slice(None)), (0, k)) for k in range(shape[1])]


def adamw_small(landings, w, m, v):
    names = list(landings)
    n = len(names)
    shapes = [w[name].shape for name in names]

    def body(*refs):
        land, ws, ms, vs = refs[:n], refs[n:2 * n], refs[2 * n:3 * n], refs[3 * n:4 * n]
        outs = [refs[(4 + k) * n:(5 + k) * n] for k in range(4)]
        for k in range(n):
            total = land[k][0]
            for j in range(1, N_DEV):
                total = total + land[k][j]
            for rows, at in _small_views(shapes[k]):
                g = total[rows]
                outs[0][k][at] = g
                outs[1][k][at], outs[2][k][at], outs[3][k][at] = _adamw_update(g, ws[k][at], ms[k][at], vs[k][at])

    blocks = [_full(sh) for sh in shapes]
    outs = pl.pallas_call(
        body, name="adamw_small", grid=(1,),
        in_specs=[_full(landings[name].shape) for name in names] + blocks * 3, out_specs=blocks * 4,
        out_shape=[jax.ShapeDtypeStruct(sh, F32) for sh in shapes] * 4, compiler_params=_params(1),
    )(*[landings[name] for name in names], *[src[name] for src in (w, m, v) for name in names])
    return [dict(zip(names, outs[k * n:(k + 1) * n])) for k in range(4)]


def adamw_replicated(land_vec, land_gate_b, land_loss, names, w, m, v, gate_b):
    n = len(names)

    def body(*refs):
        vec_ref, gb_ref, loss_ref = refs[:3]
        ws, ms, vs = refs[3:3 + n], refs[3 + n:3 + 2 * n], refs[3 + 2 * n:3 + 3 * n]
        gw_ref, gm_ref, gv_ref = refs[3 + 3 * n:6 + 3 * n]
        outs = refs[6 + 3 * n:]
        vec, gb, loss = vec_ref[0], gb_ref[0], loss_ref[0]
        for j in range(1, N_DEV):
            vec, gb, loss = vec + vec_ref[j], gb + gb_ref[j], loss + loss_ref[j]
        for k in range(n):
            g = vec[k:k + 1, :]
            outs[k][...] = g
            outs[n + k][...], outs[2 * n + k][...], outs[3 * n + k][...] = _adamw_update(g, ws[k][...], ms[k][...], vs[k][...])
        outs[4 * n][...] = gb
        outs[4 * n + 1][...], outs[4 * n + 2][...], outs[4 * n + 3][...] = _adamw_update(gb, gw_ref[...], gm_ref[...], gv_ref[...])
        outs[4 * n + 4][...] = loss

    vec_block, gb_block = _full((1, D_MODEL)), _full(gate_b[0].shape)
    outs = pl.pallas_call(
        body, name="adamw_replicated", grid=(1,),
        in_specs=[_full(land_vec.shape), _full(land_gate_b.shape), _full(land_loss.shape)] + [vec_block] * (3 * n) + [gb_block] * 3,
        out_specs=[vec_block] * (4 * n) + [gb_block] * 4 + [_full(land_loss.shape[1:])],
        out_shape=[jax.ShapeDtypeStruct((1, D_MODEL), F32)] * (4 * n) + [jax.ShapeDtypeStruct(gate_b[0].shape, F32)] * 4
        + [jax.ShapeDtypeStruct(land_loss.shape[1:], F32)],
        compiler_params=_params(1),
    )(land_vec, land_gate_b, land_loss, *[src[name] for src in (w, m, v) for name in names], *gate_b)
    results = {name: [outs[k * n + i] for k in range(4)] for i, name in enumerate(names)}
    return results, outs[4 * n:4 * n + 4], outs[4 * n + 4]


SMALL_SHARDED = ("rg_conv_w", "rg_lambda", "sc_conv_w", "odd_norm_pre", "odd_norm_post", "gla_b_gate", "gla_norm_g", "gla_w_gate_lr")
SMALL_ROWS = {"rg_conv_w": (0, 4), "rg_lambda": (4, 2), "sc_conv_w": (6, 3), "odd_norm_pre": (9, 1), "odd_norm_post": (10, 1),
              "gla_b_gate": (11, 2), "gla_norm_g": (13, 1), "gla_w_gate_lr": (16, 32)}


def _pack_small(shards):
    pieces, at = [], 0
    for name in SMALL_SHARDED:
        start, rows = SMALL_ROWS[name]
        if start > at:
            pieces.append(jnp.zeros((start - at, LANES), F32))
        a = shards[name].reshape(rows, -1)
        pieces.append(jnp.pad(a, ((0, 0), (0, LANES - a.shape[1]))))
        at = start + rows
    return jnp.concatenate(pieces, axis=0)


def _unpack_gathered(g):
    def cols(name, width):
        start, rows = SMALL_ROWS[name]
        return jnp.transpose(g[:, start:start + rows, :width], (1, 0, 2)).reshape(rows, N_DEV * width)

    w_lr = cols("gla_w_gate_lr", GLA_KEY // N_DEV).reshape(2, GLA_RANK, GLA_KEY)
    return dict(rg_conv_w=cols("rg_conv_w", LANES), rg_lambda=cols("rg_lambda", LANES), sc_conv_w=cols("sc_conv_w", LANES),
                odd_norm_pre=cols("odd_norm_pre", LANES), odd_norm_post=cols("odd_norm_post", LANES),
                gla_b_gate=cols("gla_b_gate", GLA_KEY // N_DEV), gla_norm_g=cols("gla_norm_g", GLA_DV // N_DEV), gla_w_gate_lr=w_lr)


def _blocks_along_columns(a, rows):
    return jnp.transpose(a.reshape(rows, N_DEV, -1), (1, 0, 2))


def kernel(x, even_norm_pre, even_norm_post, even_w_in, rg_conv_w, rg_conv_b, rg_gate_w, rg_gate_b, rg_lambda, sc_conv_w, even_w_out, odd_norm_pre, odd_norm_post, odd_w_in, gla_w_gate_lr, gla_b_gate, gla_norm_g, odd_w_out, loss_target, m_even_norm_pre, m_even_norm_post, m_even_w_in, m_rg_conv_w, m_rg_conv_b, m_rg_gate_w, m_rg_gate_b, m_rg_lambda, m_sc_conv_w, m_even_w_out, m_odd_norm_pre, m_odd_norm_post, m_odd_w_in, m_gla_w_gate_lr, m_gla_b_gate, m_gla_norm_g, m_odd_w_out, v_even_norm_pre, v_even_norm_post, v_even_w_in, v_rg_conv_w, v_rg_conv_b, v_rg_gate_w, v_rg_gate_b, v_rg_lambda, v_sc_conv_w, v_even_w_out, v_odd_norm_pre, v_odd_norm_post, v_odd_w_in, v_gla_w_gate_lr, v_gla_b_gate, v_gla_norm_g, v_odd_w_out):
    weights = dict(even_norm_pre=even_norm_pre, even_norm_post=even_norm_post, even_w_in=even_w_in, rg_conv_w=rg_conv_w,
                   rg_conv_b=rg_conv_b, rg_gate_w=rg_gate_w, rg_gate_b=rg_gate_b, rg_lambda=rg_lambda, sc_conv_w=sc_conv_w,
                   even_w_out=even_w_out, odd_norm_pre=odd_norm_pre, odd_norm_post=odd_norm_post, odd_w_in=odd_w_in,
                   gla_w_gate_lr=gla_w_gate_lr, gla_b_gate=gla_b_gate, gla_norm_g=gla_norm_g, odd_w_out=odd_w_out)
    m_in = dict(even_norm_pre=m_even_norm_pre, even_norm_post=m_even_norm_post, even_w_in=m_even_w_in, rg_conv_w=m_rg_conv_w,
                rg_conv_b=m_rg_conv_b, rg_gate_w=m_rg_gate_w, rg_gate_b=m_rg_gate_b, rg_lambda=m_rg_lambda, sc_conv_w=m_sc_conv_w,
                even_w_out=m_even_w_out, odd_norm_pre=m_odd_norm_pre, odd_norm_post=m_odd_norm_post, odd_w_in=m_odd_w_in,
                gla_w_gate_lr=m_gla_w_gate_lr, gla_b_gate=m_gla_b_gate, gla_norm_g=m_gla_norm_g, odd_w_out=m_odd_w_out)
    v_in = dict(even_norm_pre=v_even_norm_pre, even_norm_post=v_even_norm_post, even_w_in=v_even_w_in, rg_conv_w=v_rg_conv_w,
                rg_conv_b=v_rg_conv_b, rg_gate_w=v_rg_gate_w, rg_gate_b=v_rg_gate_b, rg_lambda=v_rg_lambda, sc_conv_w=v_sc_conv_w,
                even_w_out=v_even_w_out, odd_norm_pre=v_odd_norm_pre, odd_norm_post=v_odd_norm_post, odd_w_in=v_odd_w_in,
                gla_w_gate_lr=v_gla_w_gate_lr, gla_b_gate=v_gla_b_gate, gla_norm_g=v_gla_norm_g, odd_w_out=v_odd_w_out)
    names = list(weights)
    shapes = {n: weights[n].shape for n in names}
    xs = x[0]
    tgt = loss_target[0]

    proj_e, h_e, w_in_e, small_all = gather_matmul(xs, even_norm_pre, even_w_in[0].astype(BF16),
                                                   _pack_small({n: weights[n][0] for n in SMALL_SHARDED}), 2 * MM_TILE)
    small = _unpack_gathered(small_all)
    gate_w = rg_gate_w[0].reshape(4, RG_HEADS, RG_HEAD_DIM, RG_HEAD_DIM).astype(BF16)
    gate_b = rg_gate_b[0].reshape(4, RG_HEADS, RG_HEAD_DIM)
    conv_b = rg_conv_b
    wg_pad = [jnp.pad(small["gla_w_gate_lr"][d], ((GLA_RANK * d, LANES - GLA_RANK * (d + 1)), (0, 0))).astype(BF16) for d in range(2)]
    bg = [small["gla_b_gate"][d:d + 1] for d in range(2)]
    gnorm = jnp.tile(small["gla_norm_g"], (1, GLA_HEADS))

    half = D_MODEL // 2
    behind_gates = Exchange()
    behind_gates.gather(even_w_out[0].astype(BF16), via_sibling=True)
    behind_gates.gather(odd_w_in[0, :half].astype(BF16), via_sibling=True)
    (ab, hf), (w_out_e, w_in_o_top) = even_gates_fwd(proj_e, small["rg_conv_w"], conv_b, gate_w, gate_b, small["rg_lambda"],
                                                     exchange=behind_gates)
    w_out_e = w_out_e.reshape(2 * D_MODEL, D_MODEL)
    behind_mix_fwd = Exchange()
    behind_mix_fwd.gather(odd_w_in[0, half:].astype(BF16), via_sibling=True)
    behind_mix_fwd.gather(odd_w_out[0].astype(BF16), via_sibling=True)
    (u_e, hb, y_e, x1), (w_in_o_bottom, w_out_o) = even_mix_fwd(ab, hf, proj_e, small["sc_conv_w"], w_out_e, xs, even_norm_post,
                                                                exchange=behind_mix_fwd)
    w_out_o = w_out_o.reshape(D_MODEL, D_MODEL)
    w_in_o = columns_from_blocks([w_in_o_top, w_in_o_bottom], ODD_IN_PAD)

    proj_o, h_o = rms_matmul(x1, small["odd_norm_pre"], w_in_o, MM_TILE, ODD_IN_PAD, "odd_in")
    o_f, st_f = gla_fwd(proj_o, wg_pad[0], bg[0], False)
    osum, u_o, st_b, y_o, dout, loss_part = gla_fwd(proj_o, wg_pad[1], bg[1], True, o_other=o_f, gnorm=gnorm,
                                                    post=(w_out_o, x1, small["odd_norm_post"], tgt))

    do, dr, dy_o, d_odd_norm_post, d_gnorm = normbwd_matmul_nt(y_o, small["odd_norm_post"], dout, w_out_o, D_MODEL, "odd_out_bwd",
                                                               gla=(proj_o, osum, gnorm))
    d_w_out_o = matmul_tn(u_o, dy_o, D_MODEL, D_MODEL, 4 * MM_TILE, BF16, "odd_w_out_grad")
    dqkv_f, dlr_f, dwg_f, dbg_f = gla_bwd(proj_o, wg_pad[0], bg[0], do, st_f, False)
    dproj_o, dwg_b, dbg_b = gla_bwd(proj_o, wg_pad[1], bg[1], do, st_b, True, first=(dqkv_f, dlr_f, dr))
    dx1, d_odd_norm_pre = matmul_nt_normbwd(dproj_o, w_in_o, x1, small["odd_norm_pre"], dout, MM_TILE, ODD_IN_PAD, "odd_in_bwd")
    d_w_in_o = matmul_tn(h_o, dproj_o, D_MODEL, ODD_IN_PAD // 5, 8 * MM_TILE, BF16, "odd_w_in_grad")

    landed = {}
    behind_out = Exchange()
    behind_out.scatter(d_w_out_o.reshape(N_DEV, D_MODEL // N_DEV, D_MODEL))
    behind_out.scatter(d_odd_norm_pre, columns=True)
    behind_out.scatter(d_odd_norm_post, columns=True)
    behind_out.scatter(_blocks_along_columns(jnp.concatenate([dbg_f, dbg_b], axis=0), 2))
    behind_out.scatter(_blocks_along_columns(d_gnorm, 1))
    behind_out.scatter(_blocks_along_columns(jnp.concatenate([dwg_f[:GLA_RANK], dwg_b[GLA_RANK:2 * GLA_RANK]], axis=0), 2 * GLA_RANK))
    (du_e, dy_e, d_even_norm_post), got = normbwd_matmul_nt(y_e, even_norm_post, dx1, w_out_e, 2 * D_MODEL, "even_out_bwd",
                                                           exchange=behind_out)
    p_w_out_o = got[0]
    for n, part in zip(("odd_norm_pre", "odd_norm_post", "gla_b_gate", "gla_norm_g", "gla_w_gate_lr"), got[1:]):
        landed[n] = part
    d_w_out_e = matmul_tn(u_e, dy_e, D_MODEL, D_MODEL, 4 * MM_TILE, BF16, "even_w_out_grad")
    behind_mix = Exchange()
    behind_mix.scatter(d_w_out_e.reshape(N_DEV, 2 * D_MODEL // N_DEV, D_MODEL))
    (dh, drest, d_sc_w, adj_b), (p_w_out_e,) = even_mix_bwd(du_e, hf, hb, proj_e, small["sc_conv_w"], ab, exchange=behind_mix)
    adj_f = linear_scan(ab, 0, dh.reshape(1, *dh.shape), 0, True, True, "scan_fwd_adjoint")
    behind_gates_bwd = Exchange()
    behind_gates_bwd.scatter(column_blocks(d_w_in_o, ODD_SHARD))
    behind_gates_bwd.scatter(d_sc_w, columns=True)
    (dua, d_gate_w, d_gate_b, d_lam), (p_w_in_o, landed["sc_conv_w"]) = even_gates_bwd(
        proj_e, adj_f, adj_b, hf, hb, dh, small["rg_conv_w"], conv_b, gate_w, gate_b, small["rg_lambda"], exchange=behind_gates_bwd)
    gate_w_rows = 4 * RG_HEADS * RG_HEAD_DIM
    behind_conv = Exchange()
    behind_conv.scatter(d_gate_w.reshape(N_DEV, gate_w_rows // N_DEV, RG_HEAD_DIM))
    behind_conv.scatter(d_lam, columns=True)
    (dxa, d_conv_w, d_conv_b), (p_gate_w, landed["rg_lambda"]) = rg_conv_bwd(dua, proj_e, small["rg_conv_w"], exchange=behind_conv)
    behind_w_grad = Exchange()
    behind_w_grad.gather(sum_parts(p_gate_w, "sum_gate_w"))
    d_w_in_e, (g_gate_w_all,) = matmul_tn(h_e, drest, D_MODEL, D_MODEL, 4 * MM_TILE, BF16, "even_w_in_grad",
                                          exchange=behind_w_grad, b_first=dxa)
    to_sibling = Exchange()
    to_sibling.to_sibling(d_w_in_e)
    to_sibling.scatter(d_conv_w, columns=True)
    from_sibling, landed["rg_conv_w"] = run_exchange(to_sibling, "scatter_to_sibling")
    behind_in_bwd = Exchange()
    behind_in_bwd.among_chips(pair_sum(d_w_in_e, from_sibling))
    (grad_x, d_even_norm_pre), (p_w_in_e,) = matmul_nt_normbwd(
        drest, w_in_e, xs, even_norm_pre, dx1, 2 * MM_TILE, D_MODEL, "even_in_bwd", exchange=behind_in_bwd, first=dxa)
    last = Exchange()
    replicated_vecs = ("even_norm_pre", "even_norm_post", "rg_conv_b")
    last.gather(jnp.concatenate([d_even_norm_pre, d_even_norm_post, d_conv_b], axis=0))
    last.gather(d_gate_b.reshape(4 * RG_HEADS, RG_HEAD_DIM))
    last.gather(loss_part)

    results = {}

    def update(name, parts_, shape2d, exchange=None):
        outs = adamw(parts_, weights[name][0].reshape(shape2d), m_in[name][0].reshape(shape2d), v_in[name][0].reshape(shape2d),
                     "adamw_" + name, exchange=exchange)
        if exchange is not None:
            outs, gathered = outs
        results[name] = [o.reshape(shapes[name]) for o in outs]
        return gathered if exchange is not None else None

    update("even_w_in", p_w_in_e, (D_MODEL, EVEN_SHARD))
    land_vec, land_gate_b, land_loss = update("even_w_out", p_w_out_e, (2 * D_MODEL // N_DEV, D_MODEL), exchange=last)
    odd_w_in_out = adamw_transposed(
        p_w_in_o, *[jnp.transpose(src["odd_w_in"], (2, 0, 1)) for src in (weights, m_in, v_in)], "adamw_odd_w_in")
    results["odd_w_in"] = [jnp.transpose(o, (1, 2, 0)) for o in odd_w_in_out]
    update("odd_w_out", p_w_out_o, (D_MODEL // N_DEV, D_MODEL))
    update("rg_gate_w", g_gate_w_all.reshape(1, gate_w_rows, RG_HEAD_DIM), (gate_w_rows, RG_HEAD_DIM))
    small_out = adamw_small({n: landed[n] for n in SMALL_SHARDED}, weights, m_in, v_in)
    for n in SMALL_SHARDED:
        results[n] = [o[n] for o in small_out]
    gate_b_shape = (4 * RG_HEADS, RG_HEAD_DIM)
    rep_out, gate_b_out, loss_all = adamw_replicated(land_vec, land_gate_b, land_loss, replicated_vecs, weights, m_in, v_in,
                                                     [src["rg_gate_b"].reshape(gate_b_shape) for src in (weights, m_in, v_in)])
    results.update(rep_out)
    results["rg_gate_b"] = [o.reshape(shapes["rg_gate_b"]) for o in gate_b_out]

    return (loss_all[0, 0], grad_x.reshape(x.shape), *[results[n][0] for n in names], *[results[n][1] for n in names],
            *[results[n][2] for n in names], *[results[n][3] for n in names])
```

```python
import functools

import jax
import jax.numpy as jnp
from jax import lax
from jax.experimental import pallas as pl
from jax.experimental.pallas import tpu as pltpu

F32 = jnp.float32
BF16 = jnp.bfloat16

N_DEV = 8
D_MODEL = 1024
NORM_EPS = 1e-6
RG_HEADS = 8
RG_HEAD_DIM = 128
RG_C = 8.0
GLA_HEADS = 4
GLA_DK = 128
GLA_DV = 256
GLA_KEY = 512
GLA_RANK = 16
GLA_NORMALIZER = 16.0
GLA_CHUNK = 64
EVEN_IN = 6144
ODD_IN = 3104
ODD_IN_PAD = 3200
ODD_SHARD = ODD_IN // N_DEV
EVEN_SHARD = EVEN_IN // N_DEV
ADAM_LR = 0.001
ADAM_B1 = 0.9
ADAM_B2 = 0.999
ADAM_EPS = 1e-08
ADAM_WD = 0.01
ADAM_STEP = 10

SMALLEST_NORMAL = 1.1754944e-38
SUBLANES = 8
LANES = 128
VMEM_LIMIT_BYTES = 48 * 2 ** 20
ROW_TILE = 256
GLA_TILE = 256
MM_TILE = 512
ADAMW_BLOCK_BYTES = 2 ** 20
PACK_ROWS = 48
MESH_ID = pl.DeviceIdType.MESH


def _params(n_grid):
    return pltpu.CompilerParams(dimension_semantics=("arbitrary",) * n_grid, vmem_limit_bytes=VMEM_LIMIT_BYTES)


def _bdot(a, b):
    return jnp.dot(a.astype(BF16), b.astype(BF16), preferred_element_type=F32)


def _bdot_nt(a, b):
    return lax.dot_general(a.astype(BF16), b.astype(BF16), (((1,), (1,)), ((), ())), preferred_element_type=F32)


def _bdot_tn(a, b):
    return lax.dot_general(a.astype(BF16), b.astype(BF16), (((0,), (0,)), ((), ())), preferred_element_type=F32)


def _rstd(x):
    return lax.rsqrt(jnp.mean(x * x, axis=-1, keepdims=True) + NORM_EPS)


def _rms(x, g):
    return x * _rstd(x) * g


def _rms_bwd(x, g, dy):
    xh = x * _rstd(x)
    dyg = dy * g
    dx = _rstd(x) * (dyg - xh * jnp.mean(dyg * xh, axis=-1, keepdims=True))
    return dx, jnp.sum(dy * xh, axis=0, keepdims=True)


def _sigmoid(z):
    return 0.5 * jnp.tanh(0.5 * z) + 0.5


def _silu_and_grad(z):
    s = _sigmoid(z)
    return z * s, s * (1.0 + z * (1.0 - s))


def _softplus(z):
    return jnp.maximum(z, 0.0) + jnp.log(1.0 + jnp.exp(-jnp.abs(z)))


def _shift_rows(cur, before, after, d):
    ts = cur.shape[0]
    row = lax.broadcasted_iota(jnp.int32, (SUBLANES, cur.shape[1]), 0)
    out = pltpu.roll(cur, (-d) % ts, 0)
    if d < 0:
        edge = jnp.where(row < -d, pltpu.roll(before, (-d) % SUBLANES, 0), out[:SUBLANES])
        return jnp.concatenate([edge, out[SUBLANES:]], axis=0)
    edge = jnp.where(row >= SUBLANES - d, pltpu.roll(after, (-d) % SUBLANES, 0), out[ts - SUBLANES:])
    return jnp.concatenate([out[:ts - SUBLANES], edge], axis=0)


def _halo_specs(ts, s, width, col, tile=lambda i: i):
    per = ts // SUBLANES
    last = s // SUBLANES - 1
    return [
        pl.BlockSpec((ts, width), lambda i: (tile(i), col)),
        pl.BlockSpec((SUBLANES, width), lambda i: (jnp.maximum(tile(i) * per - 1, 0), col)),
        pl.BlockSpec((SUBLANES, width), lambda i: (jnp.minimum((tile(i) + 1) * per, last), col)),
    ]


def _halo_load(cur_ref, before_ref, after_ref, n_tiles, tile=lambda i: i):
    i = tile(pl.program_id(0))
    before = jnp.where(i > 0, before_ref[...], 0.0)
    after = jnp.where(i < n_tiles - 1, after_ref[...], 0.0)
    return cur_ref[...], before, after


def _full(shape):
    return pl.BlockSpec(shape, lambda *_: (0,) * len(shape))


def _peer(x, y, c, mask):
    px, py, pc = x ^ (mask >> 2), y ^ ((mask >> 1) & 1), c ^ (mask & 1)
    return (px, py, pc), 4 * px + 2 * py + pc


class Exchange:
    SIBLING = 1
    OTHER_CHIPS = (2, 4, 6)

    def __init__(self):
        self.args, self.out_shape, self._kinds = [], [], []

    def gather(self, block, columns=False, via_sibling=False):
        shape = (block.shape[0], N_DEV * block.shape[1]) if columns else (N_DEV,) + block.shape
        return self._add(block, shape, ("gather", columns, via_sibling))

    def scatter(self, stack, columns=False):
        shape = (N_DEV, stack.shape[0], stack.shape[1] // N_DEV) if columns else stack.shape
        return self._add(stack, shape, ("scatter", columns, False))

    def _add(self, arg, shape, kind):
        self.args.append(arg)
        self.out_shape.append(jax.ShapeDtypeStruct(shape, arg.dtype))
        self._kinds.append(kind)
        return len(self.args) - 1

    def semaphores(self):
        n = len(self.args)
        return [pltpu.SemaphoreType.DMA((n, N_DEV - 1)), pltpu.SemaphoreType.DMA((n, N_DEV - 1)), pltpu.SemaphoreType.DMA((n,))]

    def to_sibling(self, array):
        shape = (N_DEV // 2, array.shape[0], array.shape[1] // N_DEV)
        return self._add(array, shape, ("to_sibling", True, False))

    def among_chips(self, stack):
        return self._add(stack, stack.shape, ("among_chips", False, False))

    def _copies(self, position, in_refs, out_refs):
        x, y, c, me = position
        for arr, ((kind, columns, via_sibling), src, out) in enumerate(zip(self._kinds, in_refs, out_refs)):
            if kind == "to_sibling":
                width = src.shape[-1] // N_DEV
                for k in range(N_DEV // 2):
                    block = src.at[:, pl.ds(pl.multiple_of((2 * k + 1 - c) * width, LANES), width)]
                    yield arr, k + 1, block, out.at[k], out.at[k], False, self.SIBLING
                continue
            for mask in range(N_DEV):
                _, peer_id = _peer(x, y, c, mask)
                relayed = via_sibling and mask not in (0, self.SIBLING) + self.OTHER_CHIPS
                if kind == "among_chips":
                    if mask in (0,) + self.OTHER_CHIPS:
                        yield arr, mask, src.at[peer_id // 2], out.at[me // 2], out.at[peer_id // 2], False, mask
                elif kind == "gather":
                    if columns:
                        width = src.shape[-1]
                        yield (arr, mask, src, out.at[:, pl.ds(pl.multiple_of(me * width, LANES), width)],
                               out.at[:, pl.ds(pl.multiple_of(peer_id * width, LANES), width)], relayed, mask)
                    else:
                        yield arr, mask, src, out.at[me], out.at[peer_id], relayed, mask
                else:
                    if columns:
                        width = src.shape[-1] // N_DEV
                        block = src.at[:, pl.ds(pl.multiple_of(peer_id * width, LANES), width)]
                    else:
                        block = src.at[peer_id]
                    yield arr, mask, block, out.at[me], out.at[peer_id], False, mask

    def _remote(self, position, sems, arr, slot, to_mask, src, dst):
        x, y, c, _ = position
        return pltpu.make_async_remote_copy(src_ref=src, dst_ref=dst, send_sem=sems[0].at[arr, slot - 1], recv_sem=sems[1].at[arr, slot - 1],
                                            device_id=_peer(x, y, c, to_mask)[0], device_id_type=MESH_ID)

    def start(self, position, in_refs, out_refs, sems):
        for arr, slot, src, dst, _, relayed, to_mask in self._copies(position, in_refs, out_refs):
            if slot == 0:
                pltpu.make_async_copy(src, dst, sems[2].at[arr]).start()
            elif not relayed:
                self._remote(position, sems, arr, slot, to_mask, src, dst).start()

    def wait(self, position, in_refs, out_refs, sems):
        copies = list(self._copies(position, in_refs, out_refs))
        landings = {(arr, slot): landing for arr, slot, _, _, landing, _, _ in copies}
        passed_on = set()
        for arr, mask, src, _, landing, relayed, _ in copies:
            if relayed:
                held = landings[arr, mask ^ self.SIBLING]
                self._remote(position, sems, arr, mask ^ self.SIBLING, mask ^ self.SIBLING, src, held).wait_recv()
                self._remote(position, sems, arr, mask, self.SIBLING, held, held).start()
                passed_on.add((arr, mask ^ self.SIBLING))
        for arr, slot, src, dst, landing, relayed, to_mask in copies:
            if slot == 0:
                pltpu.make_async_copy(src, dst, sems[2].at[arr]).wait()
                continue
            if (arr, slot) not in passed_on:
                self._remote(position, sems, arr, slot, to_mask, src, landing).wait_recv()
            if relayed:
                held = landings[arr, slot ^ self.SIBLING]
                self._remote(position, sems, arr, slot, self.SIBLING, held, held).wait_send()
            else:
                self._remote(position, sems, arr, slot, to_mask, src, dst).wait_send()


def _call(body, *, name, grid, in_specs, out_specs, out_shape, args, scratch_shapes=(), exchange=None):
    single = not isinstance(out_shape, (list, tuple))
    if single:
        out_specs, out_shape = [out_specs], [out_shape]
    params = _params(len(grid))
    if exchange is None:
        outs = pl.pallas_call(body, name=name, grid=grid, in_specs=in_specs, out_specs=out_specs, out_shape=out_shape,
                              scratch_shapes=list(scratch_shapes), compiler_params=params)(*args)
        return outs[0] if single else outs
    counts = (len(args), len(exchange.args), len(out_shape), len(exchange.out_shape), len(scratch_shapes), 3)

    def wrapped(*refs):
        groups, at = [], 0
        for n in counts:
            groups.append(refs[at:at + n])
            at += n
        main_in, ex_in, main_out, ex_out, main_scratch, sems = groups
        x, y, c = lax.axis_index("x"), lax.axis_index("y"), lax.axis_index("c")
        position = (x, y, c, 4 * x + 2 * y + c)
        ids = [pl.program_id(a) for a in range(len(grid))]
        first = functools.reduce(jnp.logical_and, [i == 0 for i in ids])
        last = functools.reduce(jnp.logical_and, [i == g - 1 for i, g in zip(ids, grid)])

        @pl.when(first)
        def _():
            exchange.start(position, ex_in, ex_out, sems)

        body(*main_in, *main_out, *main_scratch)

        @pl.when(last)
        def _():
            exchange.wait(position, ex_in, ex_out, sems)

    hbm = pl.BlockSpec(memory_space=pl.ANY)
    outs = pl.pallas_call(
        wrapped, name=name, grid=grid, in_specs=list(in_specs) + [hbm] * counts[1], out_specs=list(out_specs) + [hbm] * counts[3],
        out_shape=list(out_shape) + exchange.out_shape, scratch_shapes=list(scratch_shapes) + exchange.semaphores(),
        compiler_params=params)(*args, *exchange.args)
    main = outs[:counts[2]]
    return (main[0] if single else main), outs[counts[2]:]


def run_exchange(exchange, name):
    return _call(lambda: None, name=name, grid=(1,), in_specs=[], out_specs=[], out_shape=[], args=[], exchange=exchange)[1]


def gather_matmul(x, g, w_block, small_block, tm):
    s, d = x.shape
    width = w_block.shape[1]
    tm = min(tm, s)
    n_i = s // tm
    sibling = Exchange.SIBLING
    y_nbr, x_nbr, diagonal = Exchange.OTHER_CHIPS

    def links(core):
        return (y_nbr, x_nbr) if core == 1 else (x_nbr, y_nbr)

    def block_order(core):
        first, second = links(core)
        return [0, sibling, first, second | sibling, second, first | sibling, diagonal, diagonal | sibling]

    def body(order_ref, x_ref, g_ref, wb_ref, sb_ref, proj_ref, h_ref, w_ref, small_ref, h_all, w_buf, send, recv, local, load_sem):
        j, i = pl.program_id(0), pl.program_id(1)
        xx, yy, cc = lax.axis_index("x"), lax.axis_index("y"), lax.axis_index("c")
        me = 4 * xx + 2 * yy + cc

        def block_of(dev):
            return w_ref.at[:, pl.ds(pl.multiple_of(dev * width, LANES), width)]

        def half_of(dev, part):
            return w_ref.at[pl.ds(part * (d // 2), d // 2), pl.ds(pl.multiple_of(dev * width, LANES), width)]

        def remote(arr, slot, to_mask, src, dst):
            return pltpu.make_async_remote_copy(src_ref=src, dst_ref=dst, send_sem=send.at[arr, slot - 1], recv_sem=recv.at[arr, slot - 1],
                                                device_id=_peer(xx, yy, cc, to_mask)[0], device_id_type=MESH_ID)

        def mine_to(mask):
            return remote(0, mask, mask, wb_ref, block_of(me))

        def arrival(mask):
            return remote(0, mask, mask, wb_ref, block_of(me ^ mask))

        def to_sibling(mask):
            return remote(0, mask | sibling, sibling, block_of(me ^ mask), block_of(me ^ mask))

        def relay(of):
            along_x = of == y_nbr
            part = 0 if along_x else 1
            return remote(0 if along_x else 2, diagonal, x_nbr if along_x else y_nbr, half_of(me ^ of, part), half_of(me ^ of, part))

        def diagonal_half(part):
            return remote(0 if part == 0 else 2, diagonal, x_nbr if part == 0 else y_nbr, wb_ref.at[pl.ds(0, d // 2), :],
                          half_of(me ^ diagonal, part))

        @pl.when((j == 0) & (i == 0))
        def _():
            pltpu.make_async_copy(wb_ref, block_of(me), local.at[0]).start()
            pltpu.make_async_copy(sb_ref, small_ref.at[me], local.at[1]).start()
            mine_to(sibling).start()
            for mask in range(1, N_DEV):
                remote(1, mask, mask, sb_ref, small_ref.at[me]).start()

        def load(step):
            return pltpu.make_async_copy(w_ref.at[:, pl.ds(pl.multiple_of(order_ref[step] * width, LANES), width)],
                                         w_buf.at[step % 2], load_sem.at[step % 2])

        for core in range(2):
            first, second = links(core)
            for step in range(N_DEV):
                at_step = (j == 0) & (i == 0) if step == 0 else (j == step - 1) & (i == n_i - 1)

                @pl.when(at_step & (cc == core))
                def _(step=step, first=first, second=second):
                    if step == 0:
                        mine_to(first).start()
                        pltpu.make_async_copy(wb_ref, block_of(me), local.at[0]).wait()
                    elif step == 1:
                        arrival(sibling).wait_recv()
                    elif step == 2:
                        arrival(first).wait_recv()
                        to_sibling(first).start()
                        mine_to(first).wait_send()
                        mine_to(second).start()
                        relay(first).start()
                    elif step == 3:
                        arrival(second | sibling).wait_recv()
                    elif step == 4:
                        arrival(second).wait_recv()
                        to_sibling(second).start()
                        relay(second).start()
                    elif step == 5:
                        arrival(first | sibling).wait_recv()
                    elif step == 6:
                        diagonal_half(0).wait_recv()
                        diagonal_half(1).wait_recv()
                        to_sibling(diagonal).start()
                    else:
                        arrival(diagonal | sibling).wait_recv()

        @pl.when((j == 0) & (i == 0))
        def _():
            load(0).start()

        @pl.when(i == 0)
        def _():
            load(j).wait()

        @pl.when((i == n_i - 1) & (j < N_DEV - 1))
        def _():
            load(j + 1).start()

        rows = pl.ds(pl.multiple_of(i * tm, tm), tm)

        @pl.when(j == 0)
        def _():
            h = _rms(x_ref[...], g_ref[...]).astype(BF16)
            h_all[rows, :] = h
            h_ref[...] = h

        proj_ref[...] = jnp.dot(h_all[rows, :], w_buf[j % 2], preferred_element_type=F32)

        @pl.when((j == N_DEV - 1) & (i == n_i - 1))
        def _():
            pltpu.make_async_copy(sb_ref, small_ref.at[me], local.at[1]).wait()
            for mask in range(1, N_DEV):
                remote(1, mask, mask, sb_ref, small_ref.at[me ^ mask]).wait_recv()
                remote(1, mask, mask, sb_ref, small_ref.at[me]).wait_send()
            mine_to(sibling).wait_send()
            for core in range(2):
                @pl.when(cc == core)
                def _(core=core):
                    mine_to(links(core)[1]).wait_send()
            for mask in Exchange.OTHER_CHIPS:
                to_sibling(mask).wait_send()
            relay(y_nbr).wait_send()
            relay(x_nbr).wait_send()

    def first_pass_row(j, i, order):
        return jnp.where(j == 0, i, n_i - 1), 0

    hbm = pl.BlockSpec(memory_space=pl.ANY)
    core = lax.axis_index("c")
    me = 4 * lax.axis_index("x") + 2 * lax.axis_index("y") + core
    order = (me ^ jnp.where(core == 1, jnp.array(block_order(1)), jnp.array(block_order(0)))).astype(jnp.int32)
    grid_spec = pltpu.PrefetchScalarGridSpec(
        num_scalar_prefetch=1, grid=(N_DEV, n_i),
        in_specs=[pl.BlockSpec((tm, d), first_pass_row), pl.BlockSpec((1, d), lambda j, i, order: (0, 0)), hbm, hbm],
        out_specs=[pl.BlockSpec((tm, width), lambda j, i, order: (i, order[j])), pl.BlockSpec((tm, d), first_pass_row), hbm, hbm],
        scratch_shapes=[pltpu.VMEM((s, d), BF16), pltpu.VMEM((2, d, width), BF16), pltpu.SemaphoreType.DMA((3, N_DEV - 1)),
                        pltpu.SemaphoreType.DMA((3, N_DEV - 1)), pltpu.SemaphoreType.DMA((2,)), pltpu.SemaphoreType.DMA((2,))])
    return pl.pallas_call(
        body, name="even_in", grid_spec=grid_spec,
        out_shape=[jax.ShapeDtypeStruct((s, N_DEV * width), F32), jax.ShapeDtypeStruct((s, d), BF16),
                   jax.ShapeDtypeStruct((d, N_DEV * width), w_block.dtype), jax.ShapeDtypeStruct((N_DEV,) + small_block.shape, small_block.dtype)],
        compiler_params=_params(2),
    )(order, x, g, w_block, small_block)


def rms_matmul(x, g, w, tm, tn, name, exchange=None):
    s, d = x.shape
    n = w.shape[1]
    tm = min(tm, s)

    def body(x_ref, g_ref, w_ref, o_ref, h_ref):
        @pl.when(pl.program_id(1) == 0)
        def _():
            h_ref[...] = _rms(x_ref[...], g_ref[...]).astype(BF16)

        o_ref[...] = jnp.dot(h_ref[...], w_ref[...], preferred_element_type=F32)

    return _call(
        body, name=name, grid=(s // tm, n // tn),
        in_specs=[pl.BlockSpec((tm, d), lambda i, j: (i, 0)), _full((1, d)), pl.BlockSpec((d, tn), lambda i, j: (0, j))],
        out_specs=[pl.BlockSpec((tm, tn), lambda i, j: (i, j)), pl.BlockSpec((tm, d), lambda i, j: (i, 0))],
        out_shape=[jax.ShapeDtypeStruct((s, n), F32), jax.ShapeDtypeStruct((s, d), BF16)],
        args=[x, g, w], exchange=exchange)


def _gla_out_bwd(du, r, osum, gn, do_ref, dr_ref, dgn_ref):
    silu_r, dsilu_r = _silu_and_grad(r)
    for head in range(GLA_HEADS):
        vl = slice(head * GLA_DV, (head + 1) * GLA_DV)
        o_h, g_h, du_h = osum[:, vl], gn[:, vl], du[:, vl]
        dr_ref[:, vl] = (du_h * _rms(o_h, g_h) * dsilu_r[:, vl]).astype(BF16)
        do_h, dg_h = _rms_bwd(o_h, g_h, du_h * silu_r[:, vl])
        do_ref[:, vl] = do_h
        dgn_ref[...] += dg_h


def normbwd_matmul_nt(y, g, dout, w, tn, name, exchange=None, gla=None):
    s, d = y.shape
    n = w.shape[0]
    tm = min(MM_TILE, s)

    def body(*refs):
        if gla is None:
            y_ref, g_ref, dout_ref, w_ref, du_ref, dy_ref, dg_ref = refs
        else:
            y_ref, g_ref, dout_ref, w_ref, r_ref, o_ref, gn_ref, do_ref, dr_ref, dy_ref, dg_ref, dgn_ref = refs
        i, j = pl.program_id(0), pl.program_id(1)

        @pl.when(j == 0)
        def _():
            dy, dg = _rms_bwd(y_ref[...], g_ref[...], dout_ref[...])
            dy_ref[...] = dy.astype(BF16)

            @pl.when(i == 0)
            def _():
                dg_ref[...] = jnp.zeros_like(dg_ref)
                if gla is not None:
                    dgn_ref[...] = jnp.zeros_like(dgn_ref)

            dg_ref[...] += dg

        du = lax.dot_general(dy_ref[...], w_ref[...], (((1,), (1,)), ((), ())), preferred_element_type=F32)
        if gla is None:
            du_ref[...] = du
        else:
            _gla_out_bwd(du, r_ref[...], o_ref[...], gn_ref[...], do_ref, dr_ref, dgn_ref)

    row = pl.BlockSpec((tm, d), lambda i, j: (i, 0))
    in_specs = [row, _full((1, d)), row, pl.BlockSpec((tn, d), lambda i, j: (j, 0))]
    args = [y, g, dout, w]
    tail_specs = [row, _full((1, d))]
    tail_shapes = [jax.ShapeDtypeStruct((s, d), BF16), jax.ShapeDtypeStruct((1, d), F32)]
    if gla is None:
        out_specs = [pl.BlockSpec((tm, tn), lambda i, j: (i, j))] + tail_specs
        out_shape = [jax.ShapeDtypeStruct((s, n), F32)] + tail_shapes
    else:
        proj, osum, gnorm = gla
        assert n == tn == D_MODEL
        in_specs += [pl.BlockSpec((tm, D_MODEL), lambda i, j: (i, 2)), row, _full(gnorm.shape)]
        args += [proj, osum, gnorm]
        out_specs = [row, row] + tail_specs + [_full((1, GLA_DV))]
        out_shape = [jax.ShapeDtypeStruct((s, D_MODEL), F32), jax.ShapeDtypeStruct((s, D_MODEL), BF16)] + tail_shapes + [
            jax.ShapeDtypeStruct((1, GLA_DV), F32)]
    return _call(body, name=name, grid=(s // tm, n // tn), in_specs=in_specs, out_specs=out_specs, out_shape=out_shape,
                 args=args, exchange=exchange)


def matmul_tn(a, b, tm, tn, ts, out_dtype, name, exchange=None, b_first=None):
    s, m = a.shape
    n = b.shape[1] + (0 if b_first is None else tn)
    ts = min(ts, s)
    n_k = s // ts
    dims = (((0,), (0,)), ((), ()))

    def body(*refs):
        if b_first is None:
            a_ref, b_ref, o_ref, acc = refs
        else:
            a_ref, first_ref, b_ref, o_ref, acc = refs
        j, k = pl.program_id(1), pl.program_id(2)

        @pl.when(k == 0)
        def _():
            acc[...] = jnp.zeros_like(acc)

        if b_first is None:
            acc[...] += lax.dot_general(a_ref[...], b_ref[...], dims, preferred_element_type=F32)
        else:
            @pl.when(j == 0)
            def _():
                acc[...] += lax.dot_general(a_ref[...], first_ref[...], dims, preferred_element_type=F32)

            @pl.when(j > 0)
            def _():
                acc[...] += lax.dot_general(a_ref[...], b_ref[...], dims, preferred_element_type=F32)

        @pl.when(k == n_k - 1)
        def _():
            o_ref[...] = acc[...].astype(out_dtype)

    if b_first is None:
        b_specs, b_args = [pl.BlockSpec((ts, tn), lambda i, j, k: (k, j))], [b]
    else:
        b_specs = [pl.BlockSpec((ts, tn), lambda i, j, k: (k, 0)), pl.BlockSpec((ts, tn), lambda i, j, k: (k, jnp.maximum(j - 1, 0)))]
        b_args = [b_first, b]
    return _call(
        body, name=name, grid=(m // tm, n // tn, n_k),
        in_specs=[pl.BlockSpec((ts, tm), lambda i, j, k: (k, i))] + b_specs,
        out_specs=pl.BlockSpec((tm, tn), lambda i, j, k: (i, j)),
        out_shape=jax.ShapeDtypeStruct((m, n), out_dtype),
        scratch_shapes=[pltpu.VMEM((tm, tn), F32)], args=[a] + b_args, exchange=exchange)


def matmul_nt_normbwd(dproj, w, x, g, dres, tm, tk, name, exchange=None, first=None):
    s, kt = dproj.shape
    kt += 0 if first is None else tk
    d = w.shape[0]
    tm = min(tm, s)
    n_k = kt // tk
    dims = (((1,), (1,)), ((), ()))

    def body(*refs):
        if first is None:
            a_ref, w_ref, x_ref, g_ref, r_ref, dx_ref, dg_ref, acc = refs
        else:
            first_ref, a_ref, w_ref, x_ref, g_ref, r_ref, dx_ref, dg_ref, acc = refs
        i, k = pl.program_id(0), pl.program_id(1)

        @pl.when(k == 0)
        def _():
            acc[...] = jnp.zeros_like(acc)

        if first is None:
            acc[...] += lax.dot_general(a_ref[...], w_ref[...], dims, preferred_element_type=F32)
        else:
            @pl.when(k == 0)
            def _():
                acc[...] += lax.dot_general(first_ref[...], w_ref[...], dims, preferred_element_type=F32)

            @pl.when(k > 0)
            def _():
                acc[...] += lax.dot_general(a_ref[...], w_ref[...], dims, preferred_element_type=F32)

        @pl.when(k == n_k - 1)
        def _():
            dx, dg = _rms_bwd(x_ref[...], g_ref[...], acc[...])
            dx_ref[...] = r_ref[...] + dx

            @pl.when(i == 0)
            def _():
                dg_ref[...] = jnp.zeros_like(dg_ref)

            dg_ref[...] += dg

    row = pl.BlockSpec((tm, d), lambda i, k: (i, 0))
    if first is None:
        a_specs, a_args = [pl.BlockSpec((tm, tk), lambda i, k: (i, k))], [dproj]
    else:
        a_specs = [pl.BlockSpec((tm, tk), lambda i, k: (i, 0)), pl.BlockSpec((tm, tk), lambda i, k: (i, jnp.maximum(k - 1, 0)))]
        a_args = [first, dproj]
    return _call(
        body, name=name, grid=(s // tm, n_k),
        in_specs=a_specs + [pl.BlockSpec((d, tk), lambda i, k: (0, k)), row, _full((1, d)), row],
        out_specs=[row, _full((1, d))],
        out_shape=[jax.ShapeDtypeStruct((s, d), F32), jax.ShapeDtypeStruct((1, d), F32)],
        scratch_shapes=[pltpu.VMEM((tm, d), F32)], args=a_args + [w, x, g, dres], exchange=exchange)


def _rg_conv(xa, before, after, cw, cb):
    return (cw[0:1, :] * _shift_rows(xa, before, after, -2) + cw[1:2, :] * _shift_rows(xa, before, after, -1)
            + cw[2:3, :] * xa + cw[3:4, :] * _shift_rows(xa, before, after, 1) + cb)


def _rg_gates(ua_h, gw_ref, gb_ref, c_h, direction, head):
    r = _sigmoid(_bdot(ua_h, gw_ref[2 * direction, head]) + gb_ref[2 * direction, head:head + 1, :])
    i = _sigmoid(_bdot(ua_h, gw_ref[2 * direction + 1, head]) + gb_ref[2 * direction + 1, head:head + 1, :])
    log_a = -c_h * r
    a = jnp.exp(log_a)
    beta_sq = -jnp.tanh(log_a) * (1.0 + a * a)
    inv_beta = lax.rsqrt(jnp.maximum(beta_sq, SMALLEST_NORMAL))
    return r, i, a, beta_sq * inv_beta, inv_beta


def even_gates_fwd(proj, conv_w, conv_b, gate_w, gate_b, lam, exchange=None):
    s = proj.shape[0]
    ts = min(2 * ROW_TILE, s)
    n_tiles = s // ts

    def body(xa_ref, xb_ref, xn_ref, cw_ref, cb_ref, gw_ref, gb_ref, lam_ref, o_ref, hf_ref, carry):
        @pl.when(pl.program_id(0) == 0)
        def _():
            carry[...] = jnp.zeros_like(carry)

        xa, before, after = _halo_load(xa_ref, xb_ref, xn_ref, n_tiles)
        ua = _rg_conv(xa, before, after, cw_ref[...], cb_ref[...])
        c = RG_C * _softplus(-lam_ref[...])
        ua_bf16 = ua.astype(BF16)
        for direction in range(2):
            for head in range(RG_HEADS):
                lanes = slice(head * RG_HEAD_DIM, (head + 1) * RG_HEAD_DIM)
                ua_h = ua[:, lanes]
                _, i, a, beta, _ = _rg_gates(ua_bf16[:, lanes], gw_ref, gb_ref, c[direction:direction + 1, lanes], direction, head)
                o_ref[2 * direction, :, lanes] = a
                o_ref[2 * direction + 1, :, lanes] = beta * (i * ua_h)
        _scan_tile(o_ref.at[0], o_ref.at[1], hf_ref, carry, False, False)

    return _call(
        body, name="even_gates_fwd", grid=(n_tiles,),
        in_specs=_halo_specs(ts, s, D_MODEL, 0) + [_full(conv_w.shape), _full(conv_b.shape), _full(gate_w.shape),
                                                   _full(gate_b.shape), _full(lam.shape)],
        out_specs=[pl.BlockSpec((4, ts, D_MODEL), lambda i: (0, i, 0)), pl.BlockSpec((ts, D_MODEL), lambda i: (i, 0))],
        out_shape=[jax.ShapeDtypeStruct((4, s, D_MODEL), F32), jax.ShapeDtypeStruct((s, D_MODEL), F32)],
        scratch_shapes=[pltpu.VMEM((SUBLANES, D_MODEL), F32)],
        args=[proj, proj, proj, conv_w, conv_b, gate_w, gate_b, lam], exchange=exchange)


def _scan_tile(a_ref, b_ref, h_ref, carry, reverse, b_times_a):
    ts, c = h_ref.shape
    n_blocks = ts // SUBLANES
    row = lax.broadcasted_iota(jnp.int32, (SUBLANES, c), 0)

    def block(j, h_in):
        r0 = pl.multiple_of((n_blocks - 1 - j if reverse else j) * SUBLANES, SUBLANES)
        a = a_ref[pl.ds(r0, SUBLANES), :]
        b = b_ref[pl.ds(r0, SUBLANES), :]
        if b_times_a:
            b = a * b
        for step in (1, 2, 4):
            shift = SUBLANES - step if reverse else step
            valid = row < SUBLANES - step if reverse else row >= step
            b = jnp.where(valid, a * pltpu.roll(b, shift, 0) + b, b)
            a = jnp.where(valid, a * pltpu.roll(a, shift, 0), a)
        h = a * h_in + b
        h_ref[pl.ds(r0, SUBLANES), :] = h
        return h[0:1, :] if reverse else h[SUBLANES - 1:SUBLANES, :]

    carry[0:1, :] = lax.fori_loop(0, n_blocks, block, carry[0:1, :])


def linear_scan(a_arr, a_idx, b_arr, b_idx, reverse, b_times_a, name, exchange=None):
    _, s, c = a_arr.shape
    ts = min(MM_TILE, s)
    n_tiles = s // ts

    def tile_of(i):
        return n_tiles - 1 - i if reverse else i

    def body(a_ref, b_ref, h_ref, carry):
        @pl.when(pl.program_id(0) == 0)
        def _():
            carry[...] = jnp.zeros_like(carry)

        _scan_tile(a_ref, b_ref, h_ref, carry, reverse, b_times_a)

    return _call(
        body, name=name, grid=(n_tiles,),
        in_specs=[pl.BlockSpec((None, ts, c), lambda i: (a_idx, tile_of(i), 0)),
                  pl.BlockSpec((None, ts, c), lambda i: (b_idx, tile_of(i), 0))],
        out_specs=pl.BlockSpec((ts, c), lambda i: (tile_of(i), 0)),
        out_shape=jax.ShapeDtypeStruct((s, c), F32),
        scratch_shapes=[pltpu.VMEM((SUBLANES, c), F32)], args=[a_arr, b_arr], exchange=exchange)


def _sc_conv(p, before, after, w):
    return w[0:1, :] * _shift_rows(p, before, after, -1) + w[1:2, :] * p + w[2:3, :] * _shift_rows(p, before, after, 1)


def even_mix_fwd(ab, hf, proj, sc_w, w_out, xres, g_post, exchange=None):
    s = proj.shape[0]
    ts = min(ROW_TILE, s)
    n_tiles = s // ts

    def tile(i):
        return n_tiles - 1 - i

    row = pl.BlockSpec((ts, D_MODEL), lambda i: (tile(i), 0))

    def col(c):
        return pl.BlockSpec((ts, D_MODEL), lambda i: (tile(i), c))

    def body(a_ref, b_ref, hf_ref, za_ref, xb_ref, xbb_ref, xbn_ref, gb_ref, gc_ref, gcb_ref, gcn_ref, zb_ref, w_ref,
             wo_ref, x_ref, g_ref, u_ref, hb_ref, y_ref, out_ref, carry):
        @pl.when(pl.program_id(0) == 0)
        def _():
            carry[...] = jnp.zeros_like(carry)

        _scan_tile(a_ref, b_ref, hb_ref, carry, True, False)
        xb, xb_before, xb_after = _halo_load(xb_ref, xbb_ref, xbn_ref, n_tiles, tile)
        gc, gc_before, gc_after = _halo_load(gc_ref, gcb_ref, gcn_ref, n_tiles, tile)
        silu_za, _ = _silu_and_grad(za_ref[...])
        silu_zb, _ = _silu_and_grad(zb_ref[...])
        u_ref[:, :D_MODEL] = ((hf_ref[...] + hb_ref[...]) * silu_za).astype(BF16)
        cv = _sc_conv(gc * xb, gc_before * xb_before, gc_after * xb_after, w_ref[...])
        u_ref[:, D_MODEL:] = (gb_ref[...] * cv * silu_zb).astype(BF16)
        y = jnp.dot(u_ref[...], wo_ref[...], preferred_element_type=F32)
        y_ref[...] = y
        out_ref[...] = x_ref[...] + _rms(y, g_ref[...])

    return _call(
        body, name="even_mix_fwd", grid=(n_tiles,),
        in_specs=[pl.BlockSpec((None, ts, D_MODEL), lambda i: (2, tile(i), 0)), pl.BlockSpec((None, ts, D_MODEL), lambda i: (3, tile(i), 0)),
                  row, col(1)] + _halo_specs(ts, s, D_MODEL, 2, tile) + [col(3)] + _halo_specs(ts, s, D_MODEL, 4, tile)
        + [col(5), _full(sc_w.shape), _full(w_out.shape), row, _full(g_post.shape)],
        out_specs=[pl.BlockSpec((ts, 2 * D_MODEL), lambda i: (tile(i), 0)), row, row, row],
        out_shape=[jax.ShapeDtypeStruct((s, 2 * D_MODEL), BF16)] + [jax.ShapeDtypeStruct((s, D_MODEL), F32)] * 3,
        scratch_shapes=[pltpu.VMEM((SUBLANES, D_MODEL), F32)],
        args=[ab, ab, hf, proj, proj, proj, proj, proj, proj, proj, proj, proj, sc_w, w_out, xres, g_post], exchange=exchange)


def even_mix_bwd(du, hf, hb, proj, sc_w, ab, exchange=None):
    s = proj.shape[0]
    ts = min(ROW_TILE, s)
    n_tiles = s // ts
    row = pl.BlockSpec((ts, D_MODEL), lambda i: (i, 0))

    def body(dya_ref, dyb_ref, dybb_ref, dybn_ref, hf_ref, hb_ref, za_ref, xb_ref, xbb_ref, xbn_ref,
             gb_ref, gbb_ref, gbn_ref, gc_ref, gcb_ref, gcn_ref, zb_ref, zbb_ref, zbn_ref, w_ref, a_ref,
             dh_ref, dp_ref, dw_ref, adj_ref, carry):
        @pl.when(pl.program_id(0) == 0)
        def _():
            carry[...] = jnp.zeros_like(carry)

        dyb, dyb_before, dyb_after = _halo_load(dyb_ref, dybb_ref, dybn_ref, n_tiles)
        xb, xb_before, xb_after = _halo_load(xb_ref, xbb_ref, xbn_ref, n_tiles)
        gb, gb_before, gb_after = _halo_load(gb_ref, gbb_ref, gbn_ref, n_tiles)
        gc, gc_before, gc_after = _halo_load(gc_ref, gcb_ref, gcn_ref, n_tiles)
        zb, zb_before, zb_after = _halo_load(zb_ref, zbb_ref, zbn_ref, n_tiles)
        w = w_ref[...]
        dya, za = dya_ref[...], za_ref[...]
        silu_za, dsilu_za = _silu_and_grad(za)
        dh_ref[...] = dya * silu_za
        _scan_tile(a_ref, dh_ref, adj_ref, carry, False, True)
        dp_ref[:, 0:D_MODEL] = (dya * (hf_ref[...] + hb_ref[...]) * dsilu_za).astype(BF16)

        silu_zb, dsilu_zb = _silu_and_grad(zb)
        p, p_before, p_after = gc * xb, gc_before * xb_before, gc_after * xb_after
        cv = _sc_conv(p, p_before, p_after, w)
        dcv = dyb * gb * silu_zb
        dcv_before = dyb_before * gb_before * _silu_and_grad(zb_before)[0]
        dcv_after = dyb_after * gb_after * _silu_and_grad(zb_after)[0]
        dpp = (w[0:1, :] * _shift_rows(dcv, dcv_before, dcv_after, 1) + w[1:2, :] * dcv
               + w[2:3, :] * _shift_rows(dcv, dcv_before, dcv_after, -1))
        dp_ref[:, D_MODEL:2 * D_MODEL] = (dpp * gc).astype(BF16)
        dp_ref[:, 2 * D_MODEL:3 * D_MODEL] = (dyb * cv * silu_zb).astype(BF16)
        dp_ref[:, 3 * D_MODEL:4 * D_MODEL] = (dpp * xb).astype(BF16)
        dp_ref[:, 4 * D_MODEL:5 * D_MODEL] = (dyb * gb * cv * dsilu_zb).astype(BF16)

        @pl.when(pl.program_id(0) == 0)
        def _():
            dw_ref[...] = jnp.zeros_like(dw_ref)

        dw_ref[0:1, :] += jnp.sum(dcv * _shift_rows(p, p_before, p_after, -1), axis=0, keepdims=True)
        dw_ref[1:2, :] += jnp.sum(dcv * p, axis=0, keepdims=True)
        dw_ref[2:3, :] += jnp.sum(dcv * _shift_rows(p, p_before, p_after, 1), axis=0, keepdims=True)

    return _call(
        body, name="even_mix_bwd", grid=(n_tiles,),
        in_specs=[row] + _halo_specs(ts, s, D_MODEL, 1) + [row, row, pl.BlockSpec((ts, D_MODEL), lambda i: (i, 1))]
        + _halo_specs(ts, s, D_MODEL, 2) + _halo_specs(ts, s, D_MODEL, 3) + _halo_specs(ts, s, D_MODEL, 4)
        + _halo_specs(ts, s, D_MODEL, 5) + [_full(sc_w.shape), pl.BlockSpec((None, ts, D_MODEL), lambda i: (2, i, 0))],
        out_specs=[row, pl.BlockSpec((ts, 5 * D_MODEL), lambda i: (i, 0)), _full(sc_w.shape), row],
        out_shape=[jax.ShapeDtypeStruct((s, D_MODEL), F32), jax.ShapeDtypeStruct((s, 5 * D_MODEL), BF16),
                   jax.ShapeDtypeStruct(sc_w.shape, F32), jax.ShapeDtypeStruct((s, D_MODEL), F32)],
        scratch_shapes=[pltpu.VMEM((SUBLANES, D_MODEL), F32)],
        args=[du, du, du, du, hf, hb, proj, *([proj] * 12), sc_w, ab], exchange=exchange)


def even_gates_bwd(proj, adj_f, adj_b, hf, hb, dh, conv_w, conv_b, gate_w, gate_b, lam, exchange=None):
    s = proj.shape[0]
    ts = min(2 * ROW_TILE, s)
    n_tiles = s // ts
    row = pl.BlockSpec((ts, D_MODEL), lambda i: (i, 0))

    def body(xa_ref, xab_ref, xan_ref, af_ref, afb_ref, afn_ref, ab_ref, abb_ref, abn_ref,
             hf_ref, hfb_ref, hfn_ref, hb_ref, hbb_ref, hbn_ref, dh_ref,
             cw_ref, cb_ref, gw_ref, gb_ref, lam_ref, dua_ref, dgw_ref, dgb_ref, dlam_ref):
        @pl.when(pl.program_id(0) == 0)
        def _():
            dgw_ref[...] = jnp.zeros_like(dgw_ref)
            dgb_ref[...] = jnp.zeros_like(dgb_ref)
            dlam_ref[...] = jnp.zeros_like(dlam_ref)

        xa, before, after = _halo_load(xa_ref, xab_ref, xan_ref, n_tiles)
        ua = _rg_conv(xa, before, after, cw_ref[...], cb_ref[...])
        lam_v = lam_ref[...]
        c = RG_C * _softplus(-lam_v)
        dc_dlam = -RG_C * _sigmoid(-lam_v)
        dh = dh_ref[...]
        adj = (_halo_load(af_ref, afb_ref, afn_ref, n_tiles), _halo_load(ab_ref, abb_ref, abn_ref, n_tiles))
        hs = (_halo_load(hf_ref, hfb_ref, hfn_ref, n_tiles), _halo_load(hb_ref, hbb_ref, hbn_ref, n_tiles))
        dua = jnp.zeros_like(ua)
        ua_bf16 = ua.astype(BF16)
        for direction in range(2):
            step = 1 if direction == 0 else -1
            g = dh + _shift_rows(*adj[direction], step)
            da_all = g * _shift_rows(*hs[direction], -step)
            dua_parts = []
            for head in range(RG_HEADS):
                lanes = slice(head * RG_HEAD_DIM, (head + 1) * RG_HEAD_DIM)
                ua_h = ua[:, lanes]
                c_h = c[direction:direction + 1, lanes]
                ua_hb = ua_bf16[:, lanes]
                r, i, a, beta, inv_beta = _rg_gates(ua_hb, gw_ref, gb_ref, c_h, direction, head)
                db_beta = g[:, lanes] * beta
                d_i = db_beta * ua_h
                dbeta = g[:, lanes] * (i * ua_h)
                dlog_a = (da_all[:, lanes] - dbeta * a * inv_beta) * a
                dpr = -c_h * dlog_a * r * (1.0 - r)
                dpi = d_i * i * (1.0 - i)
                dpr_b, dpi_b = dpr.astype(BF16), dpi.astype(BF16)
                dua_parts.append(db_beta * i + _bdot_nt(dpr_b, gw_ref[2 * direction, head])
                                 + _bdot_nt(dpi_b, gw_ref[2 * direction + 1, head]))
                dgw_ref[2 * direction, head] += _bdot_tn(ua_hb, dpr_b)
                dgw_ref[2 * direction + 1, head] += _bdot_tn(ua_hb, dpi_b)
                dgb_ref[2 * direction, head:head + 1, :] += jnp.sum(dpr, axis=0, keepdims=True)
                dgb_ref[2 * direction + 1, head:head + 1, :] += jnp.sum(dpi, axis=0, keepdims=True)
                dlam_ref[direction:direction + 1, lanes] += (
                    jnp.sum(-r * dlog_a, axis=0, keepdims=True) * dc_dlam[direction:direction + 1, lanes])
            dua = dua + jnp.concatenate(dua_parts, axis=1)
        dua_ref[...] = dua

    return _call(
        body, name="even_gates_bwd", grid=(n_tiles,),
        in_specs=_halo_specs(ts, s, D_MODEL, 0) * 5 + [row] + [_full(conv_w.shape), _full(conv_b.shape), _full(gate_w.shape),
                                                             _full(gate_b.shape), _full(lam.shape)],
        out_specs=[row, _full(gate_w.shape), _full(gate_b.shape), _full(lam.shape)],
        out_shape=[jax.ShapeDtypeStruct((s, D_MODEL), F32), jax.ShapeDtypeStruct(gate_w.shape, F32),
                   jax.ShapeDtypeStruct(gate_b.shape, F32), jax.ShapeDtypeStruct(lam.shape, F32)],
        args=[proj, proj, proj, adj_f, adj_f, adj_f, adj_b, adj_b, adj_b, hf, hf, hf, hb, hb, hb, dh, conv_w, conv_b, gate_w,
              gate_b, lam], exchange=exchange)


def rg_conv_bwd(dua, proj, conv_w, exchange=None):
    s = proj.shape[0]
    ts = min(2 * ROW_TILE, s)
    n_tiles = s // ts

    def body(du_ref, dub_ref, dun_ref, xa_ref, xab_ref, xan_ref, cw_ref, dp_ref, dw_ref, db_ref):
        @pl.when(pl.program_id(0) == 0)
        def _():
            dw_ref[...] = jnp.zeros_like(dw_ref)
            db_ref[...] = jnp.zeros_like(db_ref)

        dua, dua_before, dua_after = _halo_load(du_ref, dub_ref, dun_ref, n_tiles)
        xa, xa_before, xa_after = _halo_load(xa_ref, xab_ref, xan_ref, n_tiles)
        cw = cw_ref[...]
        dxa = (cw[0:1, :] * _shift_rows(dua, dua_before, dua_after, 2) + cw[1:2, :] * _shift_rows(dua, dua_before, dua_after, 1)
               + cw[2:3, :] * dua + cw[3:4, :] * _shift_rows(dua, dua_before, dua_after, -1))
        dp_ref[...] = dxa.astype(BF16)
        for tap, offset in enumerate((-2, -1, 0, 1)):
            shifted = xa if offset == 0 else _shift_rows(xa, xa_before, xa_after, offset)
            dw_ref[tap:tap + 1, :] += jnp.sum(dua * shifted, axis=0, keepdims=True)
        db_ref[...] += jnp.sum(dua, axis=0, keepdims=True)

    return _call(
        body, name="rg_conv_bwd", grid=(n_tiles,),
        in_specs=_halo_specs(ts, s, D_MODEL, 0) * 2 + [_full(conv_w.shape)],
        out_specs=[pl.BlockSpec((ts, D_MODEL), lambda i: (i, 0)), _full(conv_w.shape), _full((1, D_MODEL))],
        out_shape=[jax.ShapeDtypeStruct((s, D_MODEL), BF16), jax.ShapeDtypeStruct(conv_w.shape, F32),
                   jax.ShapeDtypeStruct((1, D_MODEL), F32)],
        args=[dua, dua, dua, proj, proj, proj, conv_w], exchange=exchange)


def _split3(x):
    x1 = x.astype(BF16)
    rest = x - x1.astype(F32)
    x2 = rest.astype(BF16)
    return x1, x2, (rest - x2.astype(F32)).astype(BF16)


def _chunk_sum_matrix(t, reverse, transpose):
    i = lax.broadcasted_iota(jnp.int32, (t, t), 0)
    j = lax.broadcasted_iota(jnp.int32, (t, t), 1)
    if transpose:
        i, j = j, i
    same = (i // GLA_CHUNK) == (j // GLA_CHUNK)
    return jnp.where(same & ((j >= i) if reverse else (j <= i)), 1.0, 0.0).astype(BF16)


def _exact_dot(m, x):
    return sum(jnp.dot(m, part, preferred_element_type=F32) for part in _split3(x))


def _chunk_mask(t, reverse):
    i = lax.broadcasted_iota(jnp.int32, (t, t), 0)
    j = lax.broadcasted_iota(jnp.int32, (t, t), 1)
    return ((i // GLA_CHUNK) == (j // GLA_CHUNK)) & ((j >= i) if reverse else (j <= i))


def _chunk_rows(c):
    return slice(c * GLA_CHUNK, (c + 1) * GLA_CHUNK)


def _gla_gate(lr, wg, bg):
    z = _bdot(lr, wg) + bg
    log_alpha = (jnp.minimum(z, 0.0) - jnp.log(1.0 + jnp.exp(-jnp.abs(z)))) * (1.0 / GLA_NORMALIZER)
    return z, log_alpha


def _gla_tile_terms(q, k, bcum, reverse):
    n_chunks = q.shape[0] // GLA_CHUNK
    totals = []
    for c in range(n_chunks):
        edge = c * GLA_CHUNK if reverse else (c + 1) * GLA_CHUNK - 1
        totals.append(bcum[edge:edge + 1, :])
    btot = jnp.concatenate([jnp.broadcast_to(total, (GLA_CHUNK, total.shape[1])) for total in totals], axis=0)
    e_pos, e_neg, e_st = jnp.exp(bcum), jnp.exp(-bcum), jnp.exp(btot - bcum)
    return q * (GLA_DK ** -0.5) * e_pos, k * e_neg, k * e_st, e_pos, e_neg, e_st, [jnp.exp(total) for total in totals]


def _gla_specs(t, n_tiles, reverse_order):
    def tile(i):
        return n_tiles - 1 - i if reverse_order else i

    return tile, [
        pl.BlockSpec((t, GLA_KEY), lambda i: (tile(i), 0)),
        pl.BlockSpec((t, GLA_KEY), lambda i: (tile(i), 1)),
        pl.BlockSpec((t, D_MODEL), lambda i: (tile(i), 1)),
        pl.BlockSpec((t, LANES), lambda i: (tile(i), (ODD_IN_PAD - LANES) // LANES)),
    ]


def gla_fwd(proj, wg, bg, reverse, o_other=None, gnorm=None, post=None):
    s = proj.shape[0]
    t = min(GLA_TILE, s)
    n_tiles = s // t
    n_chunks = t // GLA_CHUNK
    final = o_other is not None
    tile, specs = _gla_specs(t, n_tiles, reverse)

    def body(*refs):
        if final:
            (q_ref, k_ref, v_ref, lr_ref, wg_ref, bg_ref, oo_ref, r_ref, gn_ref, wo_ref, x_ref, gp_ref, t_ref,
             osum_ref, u_ref, st_ref, y_ref, dout_ref, loss_ref, state) = refs
        else:
            q_ref, k_ref, v_ref, lr_ref, wg_ref, bg_ref, o_ref, st_ref, state = refs
            osum_ref = o_ref

        @pl.when(pl.program_id(0) == 0)
        def _():
            state[...] = jnp.zeros_like(state)

        _, log_alpha = _gla_gate(lr_ref[...], wg_ref[...], bg_ref[...])
        bcum = _exact_dot(_chunk_sum_matrix(t, reverse, False), log_alpha)
        q, k, v = q_ref[...], k_ref[...], v_ref[...]
        q_in, k_in, k_st, _, _, _, decays = _gla_tile_terms(q, k, bcum, reverse)
        mask = _chunk_mask(t, reverse)
        order = list(range(n_chunks))[::-1] if reverse else list(range(n_chunks))
        intra, increments = [], []
        for head in range(GLA_HEADS):
            kl = slice(head * GLA_DK, (head + 1) * GLA_DK)
            vl = slice(head * GLA_DV, (head + 1) * GLA_DV)
            scores = jnp.where(mask, _bdot_nt(q_in[:, kl], k_in[:, kl]), 0.0)
            intra.append(_bdot(scores, v[:, vl]))
            increments.append([_bdot_tn(v[_chunk_rows(c), vl], k_st[_chunk_rows(c), kl]) for c in range(n_chunks)])
        for head in range(GLA_HEADS):
            kl = slice(head * GLA_DK, (head + 1) * GLA_DK)
            vl = slice(head * GLA_DV, (head + 1) * GLA_DV)
            running = state[head]
            before = [None] * n_chunks
            for c in order:
                before[c] = running
                st_ref[c, head] = running
                running = running * decays[c][:, kl] + increments[head][c]
            state[head] = running
            inter = [_bdot_nt(q_in[_chunk_rows(c), kl], before[c]) for c in range(n_chunks)]
            osum_ref[:, vl] = intra[head] + jnp.concatenate(inter, axis=0)
        if final:
            osum = osum_ref[...] + oo_ref[...]
            osum_ref[...] = osum
            silu_r, _ = _silu_and_grad(r_ref[...])
            gn = gn_ref[...]
            for head in range(GLA_HEADS):
                vl = slice(head * GLA_DV, (head + 1) * GLA_DV)
                u_ref[:, vl] = (_rms(osum[:, vl], gn[:, vl]) * silu_r[:, vl]).astype(BF16)

            @pl.when(pl.program_id(0) == 0)
            def _():
                loss_ref[...] = jnp.zeros_like(loss_ref)

            y = jnp.dot(u_ref[...], wo_ref[...], preferred_element_type=F32)
            y_ref[...] = y
            diff = x_ref[...] + _rms(y, gp_ref[...]) - t_ref[...]
            dout_ref[...] = diff * (1.0 / D_MODEL)
            loss_ref[...] += 0.5 * jnp.sum(jnp.mean(diff * diff, axis=-1, keepdims=True))

    row = pl.BlockSpec((t, D_MODEL), lambda i: (tile(i), 0))
    st_spec = pl.BlockSpec((n_chunks, GLA_HEADS, GLA_DV, GLA_DK), lambda i: (tile(i), 0, 0, 0))
    st_shape = jax.ShapeDtypeStruct((s // GLA_CHUNK, GLA_HEADS, GLA_DV, GLA_DK), F32)
    in_specs = specs + [_full(wg.shape), _full(bg.shape)]
    args = [proj, proj, proj, proj, wg, bg]
    if final:
        w_out, xres, g_post, target = post
        in_specs += [row, pl.BlockSpec((t, D_MODEL), lambda i: (tile(i), 2)), _full(gnorm.shape), _full(w_out.shape), row,
                     _full(g_post.shape), row]
        args += [o_other, proj, gnorm, w_out, xres, g_post, target]
        out_specs = [row, row, st_spec, row, row, _full((SUBLANES, LANES))]
        out_shape = [jax.ShapeDtypeStruct((s, D_MODEL), F32), jax.ShapeDtypeStruct((s, D_MODEL), BF16), st_shape,
                     jax.ShapeDtypeStruct((s, D_MODEL), F32), jax.ShapeDtypeStruct((s, D_MODEL), F32),
                     jax.ShapeDtypeStruct((SUBLANES, LANES), F32)]
    else:
        out_specs = [row, st_spec]
        out_shape = [jax.ShapeDtypeStruct((s, D_MODEL), F32), st_shape]
    return pl.pallas_call(
        body, name="gla_fwd_rev" if reverse else "gla_fwd", grid=(n_tiles,), in_specs=in_specs, out_specs=out_specs,
        out_shape=out_shape, scratch_shapes=[pltpu.VMEM((GLA_HEADS, GLA_DV, GLA_DK), F32)], compiler_params=_params(1),
    )(*args)


def gla_bwd(proj, wg, bg, do, states, reverse, first=None):
    s = proj.shape[0]
    t = min(GLA_TILE, s)
    n_tiles = s // t
    n_chunks = t // GLA_CHUNK
    final = first is not None
    tile, specs = _gla_specs(t, n_tiles, not reverse)

    def body(*refs):
        if final:
            (q_ref, k_ref, v_ref, lr_ref, wg_ref, bg_ref, do_ref, st_ref, dqkv1_ref, dlr1_ref, dr_ref,
             dp_ref, dwg_ref, dbg_ref, dstate, dqkv, dbc, dbt) = refs
        else:
            (q_ref, k_ref, v_ref, lr_ref, wg_ref, bg_ref, do_ref, st_ref,
             dqkv, dlr_ref, dwg_ref, dbg_ref, dstate, dbc, dbt) = refs

        @pl.when(pl.program_id(0) == 0)
        def _():
            dstate[...] = jnp.zeros_like(dstate)
            dwg_ref[...] = jnp.zeros_like(dwg_ref)
            dbg_ref[...] = jnp.zeros_like(dbg_ref)

        lr, wg_v = lr_ref[...], wg_ref[...]
        z, log_alpha = _gla_gate(lr, wg_v, bg_ref[...])
        bcum = _exact_dot(_chunk_sum_matrix(t, reverse, False), log_alpha)
        q, k, v, do_v = q_ref[...], k_ref[...], v_ref[...], do_ref[...]
        q_in, k_in, k_st, e_pos, e_neg, e_st, decays = _gla_tile_terms(q, k, bcum, reverse)
        mask = _chunk_mask(t, reverse)
        order = list(range(n_chunks)) if reverse else list(range(n_chunks))[::-1]
        q_b, k_b, ks_b, v_b, do_b = (a.astype(BF16) for a in (q_in, k_in, k_st, v, do_v))
        dq_intra, dk_intra, dv_intra, increments = [], [], [], []
        for head in range(GLA_HEADS):
            kl = slice(head * GLA_DK, (head + 1) * GLA_DK)
            vl = slice(head * GLA_DV, (head + 1) * GLA_DV)
            scores = jnp.where(mask, _bdot_nt(q_b[:, kl], k_b[:, kl]), 0.0).astype(BF16)
            dscores = jnp.where(mask, _bdot_nt(do_b[:, vl], v_b[:, vl]), 0.0).astype(BF16)
            dv_intra.append(_bdot_tn(scores, do_b[:, vl]))
            dq_intra.append(_bdot(dscores, k_b[:, kl]))
            dk_intra.append(_bdot_tn(dscores, q_b[:, kl]))
            increments.append([_bdot_tn(do_b[_chunk_rows(c), vl], q_b[_chunk_rows(c), kl]) for c in range(n_chunks)])
        after_all, ddecay_all = [], []
        for head in range(GLA_HEADS):
            kl = slice(head * GLA_DK, (head + 1) * GLA_DK)
            running = dstate[head]
            after, ddecay = [None] * n_chunks, [None] * n_chunks
            for c in order:
                after[c] = running
                ddecay[c] = jnp.sum(running * st_ref[c, head], axis=0, keepdims=True)
                running = running * decays[c][:, kl] + increments[head][c]
            dstate[head] = running
            after_all.append(after)
            ddecay_all.append(ddecay)
        for head in range(GLA_HEADS):
            kl = slice(head * GLA_DK, (head + 1) * GLA_DK)
            vl = slice(head * GLA_DV, (head + 1) * GLA_DV)
            after, ddecay = after_all[head], ddecay_all[head]
            dq_inter = jnp.concatenate([_bdot(do_b[_chunk_rows(c), vl], st_ref[c, head]) for c in range(n_chunks)], axis=0)
            dv_inter = jnp.concatenate([_bdot_nt(ks_b[_chunk_rows(c), kl], after[c]) for c in range(n_chunks)], axis=0)
            dk_st = jnp.concatenate([_bdot(v_b[_chunk_rows(c), vl], after[c]) for c in range(n_chunks)], axis=0)
            dq_in = dq_intra[head] + dq_inter
            ks_h = k_st[:, kl]
            dqkv[:, 2 * GLA_KEY + head * GLA_DV:2 * GLA_KEY + (head + 1) * GLA_DV] = dv_intra[head] + dv_inter
            dqkv[:, kl] = dq_in * (GLA_DK ** -0.5) * e_pos[:, kl]
            dqkv[:, GLA_KEY + head * GLA_DK:GLA_KEY + (head + 1) * GLA_DK] = dk_intra[head] * e_neg[:, kl] + dk_st * e_st[:, kl]
            dbc[:, kl] = dq_in * q_in[:, kl] - dk_intra[head] * k_in[:, kl] - dk_st * ks_h
            weighted = dk_st * ks_h
            for c in range(n_chunks):
                dbtot = jnp.sum(weighted[_chunk_rows(c)], axis=0, keepdims=True) + ddecay[c] * decays[c][:, kl]
                dbt[_chunk_rows(c), kl] = jnp.broadcast_to(dbtot, (GLA_CHUNK, GLA_DK))
        dlog_alpha = _exact_dot(_chunk_sum_matrix(t, reverse, True), dbc[...]) + dbt[...]
        dz = dlog_alpha * _sigmoid(-z) * (1.0 / GLA_NORMALIZER)
        dlr = _bdot_nt(dz, wg_v)
        dwg_ref[...] += _bdot_tn(lr, dz)
        dbg_ref[...] += jnp.sum(dz, axis=0, keepdims=True)
        if final:
            dp_ref[:, :2 * D_MODEL] = (dqkv[...] + dqkv1_ref[...]).astype(BF16)
            dp_ref[:, 2 * D_MODEL:3 * D_MODEL] = dr_ref[...]
            dp_ref[:, 3 * D_MODEL:] = (dlr + dlr1_ref[...]).astype(BF16)
        else:
            dlr_ref[...] = dlr

    row = pl.BlockSpec((t, D_MODEL), lambda i: (tile(i), 0))
    wide = pl.BlockSpec((t, 2 * D_MODEL), lambda i: (tile(i), 0))
    narrow = pl.BlockSpec((t, LANES), lambda i: (tile(i), 0))
    st_spec = pl.BlockSpec((n_chunks, GLA_HEADS, GLA_DV, GLA_DK), lambda i: (tile(i), 0, 0, 0))
    in_specs = specs + [_full(wg.shape), _full(bg.shape), row, st_spec]
    args = [proj, proj, proj, proj, wg, bg, do, states]
    acc_specs = [_full(wg.shape), _full(bg.shape)]
    acc_shapes = [jax.ShapeDtypeStruct(wg.shape, F32), jax.ShapeDtypeStruct(bg.shape, F32)]
    scratch = [pltpu.VMEM((GLA_HEADS, GLA_DV, GLA_DK), F32)]
    work = [pltpu.VMEM((t, GLA_KEY), F32), pltpu.VMEM((t, GLA_KEY), F32)]
    if final:
        in_specs += [wide, narrow, row]
        args += list(first)
        out_specs = [pl.BlockSpec((t, ODD_IN_PAD), lambda i: (tile(i), 0))] + acc_specs
        out_shape = [jax.ShapeDtypeStruct((s, ODD_IN_PAD), BF16)] + acc_shapes
        scratch += [pltpu.VMEM((t, 2 * D_MODEL), F32)] + work
    else:
        out_specs = [wide, narrow] + acc_specs
        out_shape = [jax.ShapeDtypeStruct((s, 2 * D_MODEL), F32), jax.ShapeDtypeStruct((s, LANES), F32)] + acc_shapes
        scratch += work
    return pl.pallas_call(
        body, name="gla_bwd_rev" if reverse else "gla_bwd", grid=(n_tiles,), in_specs=in_specs, out_specs=out_specs,
        out_shape=out_shape, scratch_shapes=scratch, compiler_params=_params(1),
    )(*args)


def column_blocks(a, width):
    r, c = a.shape
    window = -(-(width + LANES) // LANES) * LANES
    padded = -(-width // LANES) * LANES
    assert window <= c

    def body(a_ref, o_ref):
        row = lax.broadcasted_iota(jnp.int32, (window, padded), 0)
        col = lax.broadcasted_iota(jnp.int32, (window, padded), 1)
        for j in range(N_DEV):
            start = min(j * width // LANES * LANES, c - window)
            pick = jnp.where((row == col + (j * width - start)) & (col < width), 1.0, 0.0).astype(BF16)
            picked = jnp.dot(a_ref[:, start:start + window], pick, preferred_element_type=F32)
            o_ref[j] = picked[:, :width].astype(o_ref.dtype)

    return pl.pallas_call(
        body, name="column_blocks", grid=(1,), in_specs=[_full((r, c))], out_specs=_full((N_DEV, r, width)),
        out_shape=jax.ShapeDtypeStruct((N_DEV, r, width), a.dtype), compiler_params=_params(1),
    )(a)


def columns_from_blocks(parts, total):
    width = parts[0].shape[2]
    rows = [p.shape[1] for p in parts]
    window = -(-(width + LANES) // LANES) * LANES

    def body(*refs):
        part_refs, o_ref, acc = refs[:len(parts)], refs[len(parts)], refs[len(parts) + 1]
        acc[...] = jnp.zeros_like(acc)
        row = lax.broadcasted_iota(jnp.int32, (width, window), 0)
        col = lax.broadcasted_iota(jnp.int32, (width, window), 1)
        for j in range(N_DEV):
            start = min(j * width // LANES * LANES, total - window)
            place = jnp.where(col == row + (j * width - start), 1.0, 0.0).astype(BF16)
            at = 0
            for part_ref, r in zip(part_refs, rows):
                acc[at:at + r, start:start + window] += jnp.dot(part_ref[j], place, preferred_element_type=F32)
                at += r
        o_ref[...] = acc[...].astype(o_ref.dtype)

    return pl.pallas_call(
        body, name="columns_from_blocks", grid=(1,), in_specs=[_full(p.shape) for p in parts], out_specs=_full((sum(rows), total)),
        out_shape=jax.ShapeDtypeStruct((sum(rows), total), parts[0].dtype), scratch_shapes=[pltpu.VMEM((sum(rows), total), F32)],
        compiler_params=_params(1),
    )(*parts)


def pair_sum(grad, from_sibling):
    n_chips, r, w = from_sibling.shape

    def body(even_ref, odd_ref, sib_ref, o_ref):
        mine = jnp.where(lax.axis_index("c") == 1, odd_ref[...], even_ref[...])
        o_ref[...] = (mine.astype(F32) + sib_ref[...].astype(F32)).astype(o_ref.dtype)

    return pl.pallas_call(
        body, name="pair_sum", grid=(n_chips,),
        in_specs=[pl.BlockSpec((r, w), lambda k: (0, 2 * k)), pl.BlockSpec((r, w), lambda k: (0, 2 * k + 1)),
                  pl.BlockSpec((None, r, w), lambda k: (k, 0, 0))],
        out_specs=pl.BlockSpec((None, r, w), lambda k: (k, 0, 0)),
        out_shape=jax.ShapeDtypeStruct(from_sibling.shape, from_sibling.dtype), compiler_params=_params(1),
    )(grad, grad, from_sibling)


def _adamw_update(g, w, m, v):
    new_m = ADAM_B1 * m + (1.0 - ADAM_B1) * g
    new_v = ADAM_B2 * v + (1.0 - ADAM_B2) * (g * g)
    m_hat = new_m / (1.0 - ADAM_B1 ** ADAM_STEP)
    v_hat = new_v / (1.0 - ADAM_B2 ** ADAM_STEP)
    return -ADAM_LR * (m_hat / (jnp.sqrt(v_hat) + ADAM_EPS) + ADAM_WD * w), new_m, new_v


def sum_parts(parts, name):
    _, r, c = parts.shape

    def body(p_ref, o_ref):
        total = p_ref[0].astype(F32)
        for j in range(1, N_DEV):
            total = total + p_ref[j].astype(F32)
        o_ref[...] = total

    return pl.pallas_call(body, name=name, in_specs=[_full(parts.shape)], out_specs=_full((r, c)), grid=(1,),
                          out_shape=jax.ShapeDtypeStruct((r, c), F32), compiler_params=_params(1))(parts)


def adamw(parts, w, m, v, name, exchange=None):
    n, r, c = parts.shape
    tr = r
    while tr * c * 4 > ADAMW_BLOCK_BYTES and tr % (2 * SUBLANES) == 0:
        tr //= 2

    def body(p_ref, w_ref, m_ref, v_ref, g_ref, d_ref, nm_ref, nv_ref):
        g = p_ref[0].astype(F32)
        for j in range(1, n):
            g = g + p_ref[j].astype(F32)
        g_ref[...] = g
        d_ref[...], nm_ref[...], nv_ref[...] = _adamw_update(g, w_ref[...], m_ref[...], v_ref[...])

    row = pl.BlockSpec((tr, c), lambda i: (i, 0))
    return _call(
        body, name=name, grid=(r // tr,),
        in_specs=[pl.BlockSpec((n, tr, c), lambda i: (0, i, 0)), row, row, row], out_specs=[row] * 4,
        out_shape=[jax.ShapeDtypeStruct((r, c), F32)] * 4, args=[parts, w, m, v], exchange=exchange)


def adamw_transposed(parts, w_t, m_t, v_t, name, exchange=None):
    n, r, c = parts.shape
    tr = min(MM_TILE, r)

    def body(p_ref, w_ref, m_ref, v_ref, g_ref, d_ref, nm_ref, nv_ref):
        eye = (lax.broadcasted_iota(jnp.int32, (tr, tr), 0) == lax.broadcasted_iota(jnp.int32, (tr, tr), 1)).astype(BF16)
        g = _bdot_tn(p_ref[0], eye)
        for j in range(1, n):
            g = g + _bdot_tn(p_ref[j], eye)
        g_ref[:, 0, :] = g
        d_ref[:, 0, :], nm_ref[:, 0, :], nv_ref[:, 0, :] = _adamw_update(g, w_ref[:, 0, :], m_ref[:, 0, :], v_ref[:, 0, :])

    col = pl.BlockSpec((c, 1, tr), lambda i: (0, 0, i))
    return _call(
        body, name=name, grid=(r // tr,), in_specs=[pl.BlockSpec((n, tr, c), lambda i: (0, i, 0)), col, col, col], out_specs=[col] * 4,
        out_shape=[jax.ShapeDtypeStruct((c, 1, r), F32)] * 4, args=[parts, w_t, m_t, v_t], exchange=exchange)


def _small_views(shape):
    if len(shape) == 2:
        return [((slice(None), slice(None)), (slice(None), slice(None)))]
    if len(shape) == 3:
        return [((slice(None), slice(None)), (0,))]
    rows = shape[2]
    return [((slice(k * rows, (k + 1) * rows), slice(None)), (0, k)) for k in range(shape[1])]


def adamw_small(landings, w, m, v):
    names = list(landings)
    n = len(names)
    shapes = [w[name].shape for name in names]

    def body(*refs):
        land, ws, ms, vs = refs[:n], refs[n:2 * n], refs[2 * n:3 * n], refs[3 * n:4 * n]
        outs = [refs[(4 + k) * n:(5 + k) * n] for k in range(4)]
        for k in range(n):
            total = land[k][0]
            for j in range(1, N_DEV):
                total = total + land[k][j]
            for rows, at in _small_views(shapes[k]):
                g = total[rows]
                outs[0][k][at] = g
                outs[1][k][at], outs[2][k][at], outs[3][k][at] = _adamw_update(g, ws[k][at], ms[k][at], vs[k][at])

    blocks = [_full(sh) for sh in shapes]
    outs = pl.pallas_call(
        body, name="adamw_small", grid=(1,),
        in_specs=[_full(landings[name].shape) for name in names] + blocks * 3, out_specs=blocks * 4,
        out_shape=[jax.ShapeDtypeStruct(sh, F32) for sh in shapes] * 4, compiler_params=_params(1),
    )(*[landings[name] for name in names], *[src[name] for src in (w, m, v) for name in names])
    return [dict(zip(names, outs[k * n:(k + 1) * n])) for k in range(4)]


def adamw_replicated(land_vec, land_gate_b, land_loss, names, w, m, v, gate_b):
    n = len(names)

    def body(*refs):
        vec_ref, gb_ref, loss_ref = refs[:3]
        ws, ms, vs = refs[3:3 + n], refs[3 + n:3 + 2 * n], refs[3 + 2 * n:3 + 3 * n]
        gw_ref, gm_ref, gv_ref = refs[3 + 3 * n:6 + 3 * n]
        outs = refs[6 + 3 * n:]
        vec, gb, loss = vec_ref[0], gb_ref[0], loss_ref[0]
        for j in range(1, N_DEV):
            vec, gb, loss = vec + vec_ref[j], gb + gb_ref[j], loss + loss_ref[j]
        for k in range(n):
            g = vec[k:k + 1, :]
            outs[k][...] = g
            outs[n + k][...], outs[2 * n + k][...], outs[3 * n + k][...] = _adamw_update(g, ws[k][...], ms[k][...], vs[k][...])
        outs[4 * n][...] = gb
        outs[4 * n + 1][...], outs[4 * n + 2][...], outs[4 * n + 3][...] = _adamw_update(gb, gw_ref[...], gm_ref[...], gv_ref[...])
        outs[4 * n + 4][...] = loss

    vec_block, gb_block = _full((1, D_MODEL)), _full(gate_b[0].shape)
    outs = pl.pallas_call(
        body, name="adamw_replicated", grid=(1,),
        in_specs=[_full(land_vec.shape), _full(land_gate_b.shape), _full(land_loss.shape)] + [vec_block] * (3 * n) + [gb_block] * 3,
        out_specs=[vec_block] * (4 * n) + [gb_block] * 4 + [_full(land_loss.shape[1:])],
        out_shape=[jax.ShapeDtypeStruct((1, D_MODEL), F32)] * (4 * n) + [jax.ShapeDtypeStruct(gate_b[0].shape, F32)] * 4
        + [jax.ShapeDtypeStruct(land_loss.shape[1:], F32)],
        compiler_params=_params(1),
    )(land_vec, land_gate_b, land_loss, *[src[name] for src in (w, m, v) for name in names], *gate_b)
    results = {name: [outs[k * n + i] for k in range(4)] for i, name in enumerate(names)}
    return results, outs[4 * n:4 * n + 4], outs[4 * n + 4]


SMALL_SHARDED = ("rg_conv_w", "rg_lambda", "sc_conv_w", "odd_norm_pre", "odd_norm_post", "gla_b_gate", "gla_norm_g", "gla_w_gate_lr")
SMALL_ROWS = {"rg_conv_w": (0, 4), "rg_lambda": (4, 2), "sc_conv_w": (6, 3), "odd_norm_pre": (9, 1), "odd_norm_post": (10, 1),
              "gla_b_gate": (11, 2), "gla_norm_g": (13, 1), "gla_w_gate_lr": (16, 32)}


def _pack_small(shards):
    pieces, at = [], 0
    for name in SMALL_SHARDED:
        start, rows = SMALL_ROWS[name]
        if start > at:
            pieces.append(jnp.zeros((start - at, LANES), F32))
        a = shards[name].reshape(rows, -1)
        pieces.append(jnp.pad(a, ((0, 0), (0, LANES - a.shape[1]))))
        at = start + rows
    return jnp.concatenate(pieces, axis=0)


def _unpack_gathered(g):
    def cols(name, width):
        start, rows = SMALL_ROWS[name]
        return jnp.transpose(g[:, start:start + rows, :width], (1, 0, 2)).reshape(rows, N_DEV * width)

    w_lr = cols("gla_w_gate_lr", GLA_KEY // N_DEV).reshape(2, GLA_RANK, GLA_KEY)
    return dict(rg_conv_w=cols("rg_conv_w", LANES), rg_lambda=cols("rg_lambda", LANES), sc_conv_w=cols("sc_conv_w", LANES),
                odd_norm_pre=cols("odd_norm_pre", LANES), odd_norm_post=cols("odd_norm_post", LANES),
                gla_b_gate=cols("gla_b_gate", GLA_KEY // N_DEV), gla_norm_g=cols("gla_norm_g", GLA_DV // N_DEV), gla_w_gate_lr=w_lr)


def _blocks_along_columns(a, rows):
    return jnp.transpose(a.reshape(rows, N_DEV, -1), (1, 0, 2))


def kernel(x, even_norm_pre, even_norm_post, even_w_in, rg_conv_w, rg_conv_b, rg_gate_w, rg_gate_b, rg_lambda, sc_conv_w, even_w_out, odd_norm_pre, odd_norm_post, odd_w_in, gla_w_gate_lr, gla_b_gate, gla_norm_g, odd_w_out, loss_target, m_even_norm_pre, m_even_norm_post, m_even_w_in, m_rg_conv_w, m_rg_conv_b, m_rg_gate_w, m_rg_gate_b, m_rg_lambda, m_sc_conv_w, m_even_w_out, m_odd_norm_pre, m_odd_norm_post, m_odd_w_in, m_gla_w_gate_lr, m_gla_b_gate, m_gla_norm_g, m_odd_w_out, v_even_norm_pre, v_even_norm_post, v_even_w_in, v_rg_conv_w, v_rg_conv_b, v_rg_gate_w, v_rg_gate_b, v_rg_lambda, v_sc_conv_w, v_even_w_out, v_odd_norm_pre, v_odd_norm_post, v_odd_w_in, v_gla_w_gate_lr, v_gla_b_gate, v_gla_norm_g, v_odd_w_out):
    weights = dict(even_norm_pre=even_norm_pre, even_norm_post=even_norm_post, even_w_in=even_w_in, rg_conv_w=rg_conv_w,
                   rg_conv_b=rg_conv_b, rg_gate_w=rg_gate_w, rg_gate_b=rg_gate_b, rg_lambda=rg_lambda, sc_conv_w=sc_conv_w,
                   even_w_out=even_w_out, odd_norm_pre=odd_norm_pre, odd_norm_post=odd_norm_post, odd_w_in=odd_w_in,
                   gla_w_gate_lr=gla_w_gate_lr, gla_b_gate=gla_b_gate, gla_norm_g=gla_norm_g, odd_w_out=odd_w_out)
    m_in = dict(even_norm_pre=m_even_norm_pre, even_norm_post=m_even_norm_post, even_w_in=m_even_w_in, rg_conv_w=m_rg_conv_w,
                rg_conv_b=m_rg_conv_b, rg_gate_w=m_rg_gate_w, rg_gate_b=m_rg_gate_b, rg_lambda=m_rg_lambda, sc_conv_w=m_sc_conv_w,
                even_w_out=m_even_w_out, odd_norm_pre=m_odd_norm_pre, odd_norm_post=m_odd_norm_post, odd_w_in=m_odd_w_in,
                gla_w_gate_lr=m_gla_w_gate_lr, gla_b_gate=m_gla_b_gate, gla_norm_g=m_gla_norm_g, odd_w_out=m_odd_w_out)
    v_in = dict(even_norm_pre=v_even_norm_pre, even_norm_post=v_even_norm_post, even_w_in=v_even_w_in, rg_conv_w=v_rg_conv_w,
                rg_conv_b=v_rg_conv_b, rg_gate_w=v_rg_gate_w, rg_gate_b=v_rg_gate_b, rg_lambda=v_rg_lambda, sc_conv_w=v_sc_conv_w,
                even_w_out=v_even_w_out, odd_norm_pre=v_odd_norm_pre, odd_norm_post=v_odd_norm_post, odd_w_in=v_odd_w_in,
                gla_w_gate_lr=v_gla_w_gate_lr, gla_b_gate=v_gla_b_gate, gla_norm_g=v_gla_norm_g, odd_w_out=v_odd_w_out)
    names = list(weights)
    shapes = {n: weights[n].shape for n in names}
    xs = x[0]
    tgt = loss_target[0]

    proj_e, h_e, w_in_e, small_all = gather_matmul(xs, even_norm_pre, even_w_in[0].astype(BF16),
                                                   _pack_small({n: weights[n][0] for n in SMALL_SHARDED}), 2 * MM_TILE)
    small = _unpack_gathered(small_all)
    gate_w = rg_gate_w[0].reshape(4, RG_HEADS, RG_HEAD_DIM, RG_HEAD_DIM).astype(BF16)
    gate_b = rg_gate_b[0].reshape(4, RG_HEADS, RG_HEAD_DIM)
    conv_b = rg_conv_b
    wg_pad = [jnp.pad(small["gla_w_gate_lr"][d], ((GLA_RANK * d, LANES - GLA_RANK * (d + 1)), (0, 0))).astype(BF16) for d in range(2)]
    bg = [small["gla_b_gate"][d:d + 1] for d in range(2)]
    gnorm = jnp.tile(small["gla_norm_g"], (1, GLA_HEADS))

    half = D_MODEL // 2
    behind_gates = Exchange()
    behind_gates.gather(even_w_out[0].astype(BF16), via_sibling=True)
    behind_gates.gather(odd_w_in[0, :half].astype(BF16), via_sibling=True)
    (ab, hf), (w_out_e, w_in_o_top) = even_gates_fwd(proj_e, small["rg_conv_w"], conv_b, gate_w, gate_b, small["rg_lambda"],
                                                     exchange=behind_gates)
    w_out_e = w_out_e.reshape(2 * D_MODEL, D_MODEL)
    behind_mix_fwd = Exchange()
    behind_mix_fwd.gather(odd_w_in[0, half:].astype(BF16), via_sibling=True)
    behind_mix_fwd.gather(odd_w_out[0].astype(BF16), via_sibling=True)
    (u_e, hb, y_e, x1), (w_in_o_bottom, w_out_o) = even_mix_fwd(ab, hf, proj_e, small["sc_conv_w"], w_out_e, xs, even_norm_post,
                                                                exchange=behind_mix_fwd)
    w_out_o = w_out_o.reshape(D_MODEL, D_MODEL)
    w_in_o = columns_from_blocks([w_in_o_top, w_in_o_bottom], ODD_IN_PAD)

    proj_o, h_o = rms_matmul(x1, small["odd_norm_pre"], w_in_o, MM_TILE, ODD_IN_PAD, "odd_in")
    o_f, st_f = gla_fwd(proj_o, wg_pad[0], bg[0], False)
    osum, u_o, st_b, y_o, dout, loss_part = gla_fwd(proj_o, wg_pad[1], bg[1], True, o_other=o_f, gnorm=gnorm,
                                                    post=(w_out_o, x1, small["odd_norm_post"], tgt))

    do, dr, dy_o, d_odd_norm_post, d_gnorm = normbwd_matmul_nt(y_o, small["odd_norm_post"], dout, w_out_o, D_MODEL, "odd_out_bwd",
                                                               gla=(proj_o, osum, gnorm))
    d_w_out_o = matmul_tn(u_o, dy_o, D_MODEL, D_MODEL, 4 * MM_TILE, BF16, "odd_w_out_grad")
    dqkv_f, dlr_f, dwg_f, dbg_f = gla_bwd(proj_o, wg_pad[0], bg[0], do, st_f, False)
    dproj_o, dwg_b, dbg_b = gla_bwd(proj_o, wg_pad[1], bg[1], do, st_b, True, first=(dqkv_f, dlr_f, dr))
    dx1, d_odd_norm_pre = matmul_nt_normbwd(dproj_o, w_in_o, x1, small["odd_norm_pre"], dout, MM_TILE, ODD_IN_PAD, "odd_in_bwd")
    d_w_in_o = matmul_tn(h_o, dproj_o, D_MODEL, ODD_IN_PAD // 5, 8 * MM_TILE, BF16, "odd_w_in_grad")

    landed = {}
    behind_out = Exchange()
    behind_out.scatter(d_w_out_o.reshape(N_DEV, D_MODEL // N_DEV, D_MODEL))
    behind_out.scatter(d_odd_norm_pre, columns=True)
    behind_out.scatter(d_odd_norm_post, columns=True)
    behind_out.scatter(_blocks_along_columns(jnp.concatenate([dbg_f, dbg_b], axis=0), 2))
    behind_out.scatter(_blocks_along_columns(d_gnorm, 1))
    behind_out.scatter(_blocks_along_columns(jnp.concatenate([dwg_f[:GLA_RANK], dwg_b[GLA_RANK:2 * GLA_RANK]], axis=0), 2 * GLA_RANK))
    (du_e, dy_e, d_even_norm_post), got = normbwd_matmul_nt(y_e, even_norm_post, dx1, w_out_e, 2 * D_MODEL, "even_out_bwd",
                                                           exchange=behind_out)
    p_w_out_o = got[0]
    for n, part in zip(("odd_norm_pre", "odd_norm_post", "gla_b_gate", "gla_norm_g", "gla_w_gate_lr"), got[1:]):
        landed[n] = part
    d_w_out_e = matmul_tn(u_e, dy_e, D_MODEL, D_MODEL, 4 * MM_TILE, BF16, "even_w_out_grad")
    behind_mix = Exchange()
    behind_mix.scatter(d_w_out_e.reshape(N_DEV, 2 * D_MODEL // N_DEV, D_MODEL))
    (dh, drest, d_sc_w, adj_b), (p_w_out_e,) = even_mix_bwd(du_e, hf, hb, proj_e, small["sc_conv_w"], ab, exchange=behind_mix)
    adj_f = linear_scan(ab, 0, dh.reshape(1, *dh.shape), 0, True, True, "scan_fwd_adjoint")
    behind_gates_bwd = Exchange()
    behind_gates_bwd.scatter(column_blocks(d_w_in_o, ODD_SHARD))
    behind_gates_bwd.scatter(d_sc_w, columns=True)
    (dua, d_gate_w, d_gate_b, d_lam), (p_w_in_o, landed["sc_conv_w"]) = even_gates_bwd(
        proj_e, adj_f, adj_b, hf, hb, dh, small["rg_conv_w"], conv_b, gate_w, gate_b, small["rg_lambda"], exchange=behind_gates_bwd)
    gate_w_rows = 4 * RG_HEADS * RG_HEAD_DIM
    behind_conv = Exchange()
    behind_conv.scatter(d_gate_w.reshape(N_DEV, gate_w_rows // N_DEV, RG_HEAD_DIM))
    behind_conv.scatter(d_lam, columns=True)
    (dxa, d_conv_w, d_conv_b), (p_gate_w, landed["rg_lambda"]) = rg_conv_bwd(dua, proj_e, small["rg_conv_w"], exchange=behind_conv)
    behind_w_grad = Exchange()
    behind_w_grad.gather(sum_parts(p_gate_w, "sum_gate_w"))
    d_w_in_e, (g_gate_w_all,) = matmul_tn(h_e, drest, D_MODEL, D_MODEL, 4 * MM_TILE, BF16, "even_w_in_grad",
                                          exchange=behind_w_grad, b_first=dxa)
    to_sibling = Exchange()
    to_sibling.to_sibling(d_w_in_e)
    to_sibling.scatter(d_conv_w, columns=True)
    from_sibling, landed["rg_conv_w"] = run_exchange(to_sibling, "scatter_to_sibling")
    behind_in_bwd = Exchange()
    behind_in_bwd.among_chips(pair_sum(d_w_in_e, from_sibling))
    (grad_x, d_even_norm_pre), (p_w_in_e,) = matmul_nt_normbwd(
        drest, w_in_e, xs, even_norm_pre, dx1, 2 * MM_TILE, D_MODEL, "even_in_bwd", exchange=behind_in_bwd, first=dxa)
    last = Exchange()
    replicated_vecs = ("even_norm_pre", "even_norm_post", "rg_conv_b")
    last.gather(jnp.concatenate([d_even_norm_pre, d_even_norm_post, d_conv_b], axis=0))
    last.gather(d_gate_b.reshape(4 * RG_HEADS, RG_HEAD_DIM))
    last.gather(loss_part)

    results = {}

    def update(name, parts_, shape2d, exchange=None):
        outs = adamw(parts_, weights[name][0].reshape(shape2d), m_in[name][0].reshape(shape2d), v_in[name][0].reshape(shape2d),
                     "adamw_" + name, exchange=exchange)
        if exchange is not None:
            outs, gathered = outs
        results[name] = [o.reshape(shapes[name]) for o in outs]
        return gathered if exchange is not None else None

    update("even_w_in", p_w_in_e, (D_MODEL, EVEN_SHARD))
    update("even_w_out", p_w_out_e, (2 * D_MODEL // N_DEV, D_MODEL))
    odd_w_in_out, (land_vec, land_gate_b, land_loss) = adamw_transposed(
        p_w_in_o, *[jnp.transpose(src["odd_w_in"], (2, 0, 1)) for src in (weights, m_in, v_in)], "adamw_odd_w_in", exchange=last)
    results["odd_w_in"] = [jnp.transpose(o, (1, 2, 0)) for o in odd_w_in_out]
    update("odd_w_out", p_w_out_o, (D_MODEL // N_DEV, D_MODEL))
    update("rg_gate_w", g_gate_w_all.reshape(1, gate_w_rows, RG_HEAD_DIM), (gate_w_rows, RG_HEAD_DIM))
    small_out = adamw_small({n: landed[n] for n in SMALL_SHARDED}, weights, m_in, v_in)
    for n in SMALL_SHARDED:
        results[n] = [o[n] for o in small_out]
    gate_b_shape = (4 * RG_HEADS, RG_HEAD_DIM)
    rep_out, gate_b_out, loss_all = adamw_replicated(land_vec, land_gate_b, land_loss, replicated_vecs, weights, m_in, v_in,
                                                     [src["rg_gate_b"].reshape(gate_b_shape) for src in (weights, m_in, v_in)])
    results.update(rep_out)
    results["rg_gate_b"] = [o.reshape(shapes["rg_gate_b"]) for o in gate_b_out]

    return (loss_all[0, 0], grad_x.reshape(x.shape), *[results[n][0] for n in names], *[results[n][1] for n in names],
            *[results[n][2] for n in names], *[results[n][3] for n in names])
```

```python
import functools

import jax
import jax.numpy as jnp
from jax import lax
from jax.experimental import pallas as pl
from jax.experimental.pallas import tpu as pltpu

F32 = jnp.float32
BF16 = jnp.bfloat16

N_DEV = 8
D_MODEL = 1024
NORM_EPS = 1e-6
RG_HEADS = 8
RG_HEAD_DIM = 128
RG_C = 8.0
GLA_HEADS = 4
GLA_DK = 128
GLA_DV = 256
GLA_KEY = 512
GLA_RANK = 16
GLA_NORMALIZER = 16.0
GLA_CHUNK = 64
EVEN_IN = 6144
ODD_IN = 3104
ODD_IN_PAD = 3200
ODD_SHARD = ODD_IN // N_DEV
EVEN_SHARD = EVEN_IN // N_DEV
ADAM_LR = 0.001
ADAM_B1 = 0.9
ADAM_B2 = 0.999
ADAM_EPS = 1e-08
ADAM_WD = 0.01
ADAM_STEP = 10

SMALLEST_NORMAL = 1.1754944e-38
SUBLANES = 8
LANES = 128
VMEM_LIMIT_BYTES = 48 * 2 ** 20
ROW_TILE = 256
GLA_TILE = 256
MM_TILE = 512
ADAMW_BLOCK_BYTES = 2 ** 20
PACK_ROWS = 48
MESH_ID = pl.DeviceIdType.MESH


def _params(n_grid):
    return pltpu.CompilerParams(dimension_semantics=("arbitrary",) * n_grid, vmem_limit_bytes=VMEM_LIMIT_BYTES)


def _bdot(a, b):
    return jnp.dot(a.astype(BF16), b.astype(BF16), preferred_element_type=F32)


def _bdot_nt(a, b):
    return lax.dot_general(a.astype(BF16), b.astype(BF16), (((1,), (1,)), ((), ())), preferred_element_type=F32)


def _bdot_tn(a, b):
    return lax.dot_general(a.astype(BF16), b.astype(BF16), (((0,), (0,)), ((), ())), preferred_element_type=F32)


def _rstd(x):
    return lax.rsqrt(jnp.mean(x * x, axis=-1, keepdims=True) + NORM_EPS)


def _rms(x, g):
    return x * _rstd(x) * g


def _rms_bwd(x, g, dy):
    xh = x * _rstd(x)
    dyg = dy * g
    dx = _rstd(x) * (dyg - xh * jnp.mean(dyg * xh, axis=-1, keepdims=True))
    return dx, jnp.sum(dy * xh, axis=0, keepdims=True)


def _sigmoid(z):
    return 0.5 * jnp.tanh(0.5 * z) + 0.5


def _silu_and_grad(z):
    s = _sigmoid(z)
    return z * s, s * (1.0 + z * (1.0 - s))


def _softplus(z):
    return jnp.maximum(z, 0.0) + jnp.log(1.0 + jnp.exp(-jnp.abs(z)))


def _shift_rows(cur, before, after, d):
    ts = cur.shape[0]
    row = lax.broadcasted_iota(jnp.int32, (SUBLANES, cur.shape[1]), 0)
    out = pltpu.roll(cur, (-d) % ts, 0)
    if d < 0:
        edge = jnp.where(row < -d, pltpu.roll(before, (-d) % SUBLANES, 0), out[:SUBLANES])
        return jnp.concatenate([edge, out[SUBLANES:]], axis=0)
    edge = jnp.where(row >= SUBLANES - d, pltpu.roll(after, (-d) % SUBLANES, 0), out[ts - SUBLANES:])
    return jnp.concatenate([out[:ts - SUBLANES], edge], axis=0)


def _halo_specs(ts, s, width, col, tile=lambda i: i):
    per = ts // SUBLANES
    last = s // SUBLANES - 1
    return [
        pl.BlockSpec((ts, width), lambda i: (tile(i), col)),
        pl.BlockSpec((SUBLANES, width), lambda i: (jnp.maximum(tile(i) * per - 1, 0), col)),
        pl.BlockSpec((SUBLANES, width), lambda i: (jnp.minimum((tile(i) + 1) * per, last), col)),
    ]


def _halo_load(cur_ref, before_ref, after_ref, n_tiles, tile=lambda i: i):
    i = tile(pl.program_id(0))
    before = jnp.where(i > 0, before_ref[...], 0.0)
    after = jnp.where(i < n_tiles - 1, after_ref[...], 0.0)
    return cur_ref[...], before, after


def _full(shape):
    return pl.BlockSpec(shape, lambda *_: (0,) * len(shape))


def _peer(x, y, c, mask):
    px, py, pc = x ^ (mask >> 2), y ^ ((mask >> 1) & 1), c ^ (mask & 1)
    return (px, py, pc), 4 * px + 2 * py + pc


class Exchange:
    SIBLING = 1
    OTHER_CHIPS = (2, 4, 6)

    def __init__(self):
        self.args, self.out_shape, self._kinds = [], [], []

    def gather(self, block, columns=False, via_sibling=False):
        shape = (block.shape[0], N_DEV * block.shape[1]) if columns else (N_DEV,) + block.shape
        return self._add(block, shape, ("gather", columns, via_sibling))

    def scatter(self, stack, columns=False):
        shape = (N_DEV, stack.shape[0], stack.shape[1] // N_DEV) if columns else stack.shape
        return self._add(stack, shape, ("scatter", columns, False))

    def _add(self, arg, shape, kind):
        self.args.append(arg)
        self.out_shape.append(jax.ShapeDtypeStruct(shape, arg.dtype))
        self._kinds.append(kind)
        return len(self.args) - 1

    def semaphores(self):
        n = len(self.args)
        return [pltpu.SemaphoreType.DMA((n, N_DEV - 1)), pltpu.SemaphoreType.DMA((n, N_DEV - 1)), pltpu.SemaphoreType.DMA((n,))]

    def to_sibling(self, array):
        shape = (N_DEV // 2, array.shape[0], array.shape[1] // N_DEV)
        return self._add(array, shape, ("to_sibling", True, False))

    def among_chips(self, stack):
        return self._add(stack, stack.shape, ("among_chips", False, False))

    def _copies(self, position, in_refs, out_refs):
        x, y, c, me = position
        for arr, ((kind, columns, via_sibling), src, out) in enumerate(zip(self._kinds, in_refs, out_refs)):
            if kind == "to_sibling":
                width = src.shape[-1] // N_DEV
                for k in range(N_DEV // 2):
                    block = src.at[:, pl.ds(pl.multiple_of((2 * k + 1 - c) * width, LANES), width)]
                    yield arr, k + 1, block, out.at[k], out.at[k], False, self.SIBLING
                continue
            for mask in range(N_DEV):
                _, peer_id = _peer(x, y, c, mask)
                relayed = via_sibling and mask not in (0, self.SIBLING) + self.OTHER_CHIPS
                if kind == "among_chips":
                    if mask in (0,) + self.OTHER_CHIPS:
                        yield arr, mask, src.at[peer_id // 2], out.at[me // 2], out.at[peer_id // 2], False, mask
                elif kind == "gather":
                    if columns:
                        width = src.shape[-1]
                        yield (arr, mask, src, out.at[:, pl.ds(pl.multiple_of(me * width, LANES), width)],
                               out.at[:, pl.ds(pl.multiple_of(peer_id * width, LANES), width)], relayed, mask)
                    else:
                        yield arr, mask, src, out.at[me], out.at[peer_id], relayed, mask
                else:
                    if columns:
                        width = src.shape[-1] // N_DEV
                        block = src.at[:, pl.ds(pl.multiple_of(peer_id * width, LANES), width)]
                    else:
                        block = src.at[peer_id]
                    yield arr, mask, block, out.at[me], out.at[peer_id], False, mask

    def _remote(self, position, sems, arr, slot, to_mask, src, dst):
        x, y, c, _ = position
        return pltpu.make_async_remote_copy(src_ref=src, dst_ref=dst, send_sem=sems[0].at[arr, slot - 1], recv_sem=sems[1].at[arr, slot - 1],
                                            device_id=_peer(x, y, c, to_mask)[0], device_id_type=MESH_ID)

    def start(self, position, in_refs, out_refs, sems):
        for arr, slot, src, dst, _, relayed, to_mask in self._copies(position, in_refs, out_refs):
            if slot == 0:
                pltpu.make_async_copy(src, dst, sems[2].at[arr]).start()
            elif not relayed:
                self._remote(position, sems, arr, slot, to_mask, src, dst).start()

    def wait(self, position, in_refs, out_refs, sems):
        copies = list(self._copies(position, in_refs, out_refs))
        landings = {(arr, slot): landing for arr, slot, _, _, landing, _, _ in copies}
        passed_on = set()
        for arr, mask, src, _, landing, relayed, _ in copies:
            if relayed:
                held = landings[arr, mask ^ self.SIBLING]
                self._remote(position, sems, arr, mask ^ self.SIBLING, mask ^ self.SIBLING, src, held).wait_recv()
                self._remote(position, sems, arr, mask, self.SIBLING, held, held).start()
                passed_on.add((arr, mask ^ self.SIBLING))
        for arr, slot, src, dst, landing, relayed, to_mask in copies:
            if slot == 0:
                pltpu.make_async_copy(src, dst, sems[2].at[arr]).wait()
                continue
            if (arr, slot) not in passed_on:
                self._remote(position, sems, arr, slot, to_mask, src, landing).wait_recv()
            if relayed:
                held = landings[arr, slot ^ self.SIBLING]
                self._remote(position, sems, arr, slot, self.SIBLING, held, held).wait_send()
            else:
                self._remote(position, sems, arr, slot, to_mask, src, dst).wait_send()


def _call(body, *, name, grid, in_specs, out_specs, out_shape, args, scratch_shapes=(), exchange=None):
    single = not isinstance(out_shape, (list, tuple))
    if single:
        out_specs, out_shape = [out_specs], [out_shape]
    params = _params(len(grid))
    if exchange is None:
        outs = pl.pallas_call(body, name=name, grid=grid, in_specs=in_specs, out_specs=out_specs, out_shape=out_shape,
                              scratch_shapes=list(scratch_shapes), compiler_params=params)(*args)
        return outs[0] if single else outs
    counts = (len(args), len(exchange.args), len(out_shape), len(exchange.out_shape), len(scratch_shapes), 3)

    def wrapped(*refs):
        groups, at = [], 0
        for n in counts:
            groups.append(refs[at:at + n])
            at += n
        main_in, ex_in, main_out, ex_out, main_scratch, sems = groups
        x, y, c = lax.axis_index("x"), lax.axis_index("y"), lax.axis_index("c")
        position = (x, y, c, 4 * x + 2 * y + c)
        ids = [pl.program_id(a) for a in range(len(grid))]
        first = functools.reduce(jnp.logical_and, [i == 0 for i in ids])
        last = functools.reduce(jnp.logical_and, [i == g - 1 for i, g in zip(ids, grid)])

        @pl.when(first)
        def _():
            exchange.start(position, ex_in, ex_out, sems)

        body(*main_in, *main_out, *main_scratch)

        @pl.when(last)
        def _():
            exchange.wait(position, ex_in, ex_out, sems)

    hbm = pl.BlockSpec(memory_space=pl.ANY)
    outs = pl.pallas_call(
        wrapped, name=name, grid=grid, in_specs=list(in_specs) + [hbm] * counts[1], out_specs=list(out_specs) + [hbm] * counts[3],
        out_shape=list(out_shape) + exchange.out_shape, scratch_shapes=list(scratch_shapes) + exchange.semaphores(),
        compiler_params=params)(*args, *exchange.args)
    main = outs[:counts[2]]
    return (main[0] if single else main), outs[counts[2]:]


def run_exchange(exchange, name):
    return _call(lambda: None, name=name, grid=(1,), in_specs=[], out_specs=[], out_shape=[], args=[], exchange=exchange)[1]


def gather_matmul(x, g, w_block, small_block, tm):
    s, d = x.shape
    width = w_block.shape[1]
    tm = min(tm, s)
    n_i = s // tm
    sibling = Exchange.SIBLING
    y_nbr, x_nbr, diagonal = Exchange.OTHER_CHIPS

    def links(core):
        return (y_nbr, x_nbr) if core == 1 else (x_nbr, y_nbr)

    def block_order(core):
        first, second = links(core)
        return [0, sibling, first, second | sibling, second, first | sibling, diagonal, diagonal | sibling]

    def body(order_ref, x_ref, g_ref, wb_ref, sb_ref, proj_ref, h_ref, w_ref, small_ref, h_all, w_buf, send, recv, local, load_sem):
        j, i = pl.program_id(0), pl.program_id(1)
        xx, yy, cc = lax.axis_index("x"), lax.axis_index("y"), lax.axis_index("c")
        me = 4 * xx + 2 * yy + cc

        def block_of(dev):
            return w_ref.at[:, pl.ds(pl.multiple_of(dev * width, LANES), width)]

        def half_of(dev, part):
            return w_ref.at[pl.ds(part * (d // 2), d // 2), pl.ds(pl.multiple_of(dev * width, LANES), width)]

        def remote(arr, slot, to_mask, src, dst):
            return pltpu.make_async_remote_copy(src_ref=src, dst_ref=dst, send_sem=send.at[arr, slot - 1], recv_sem=recv.at[arr, slot - 1],
                                                device_id=_peer(xx, yy, cc, to_mask)[0], device_id_type=MESH_ID)

        def mine_to(mask):
            return remote(0, mask, mask, wb_ref, block_of(me))

        def arrival(mask):
            return remote(0, mask, mask, wb_ref, block_of(me ^ mask))

        def to_sibling(mask):
            return remote(0, mask | sibling, sibling, block_of(me ^ mask), block_of(me ^ mask))

        def relay(of):
            along_x = of == y_nbr
            part = 0 if along_x else 1
            return remote(0 if along_x else 2, diagonal, x_nbr if along_x else y_nbr, half_of(me ^ of, part), half_of(me ^ of, part))

        def diagonal_half(part):
            return remote(0 if part == 0 else 2, diagonal, x_nbr if part == 0 else y_nbr, wb_ref.at[pl.ds(0, d // 2), :],
                          half_of(me ^ diagonal, part))

        @pl.when((j == 0) & (i == 0))
        def _():
            pltpu.make_async_copy(wb_ref, block_of(me), local.at[0]).start()
            pltpu.make_async_copy(sb_ref, small_ref.at[me], local.at[1]).start()
            mine_to(sibling).start()
            for mask in range(1, N_DEV):
                remote(1, mask, mask, sb_ref, small_ref.at[me]).start()

        def load(step):
            return pltpu.make_async_copy(w_ref.at[:, pl.ds(pl.multiple_of(order_ref[step] * width, LANES), width)],
                                         w_buf.at[step % 2], load_sem.at[step % 2])

        for core in range(2):
            first, second = links(core)
            for step in range(N_DEV):
                at_step = (j == 0) & (i == 0) if step == 0 else (j == step - 1) & (i == n_i - 1)

                @pl.when(at_step & (cc == core))
                def _(step=step, first=first, second=second):
                    if step == 0:
                        mine_to(first).start()
                        pltpu.make_async_copy(wb_ref, block_of(me), local.at[0]).wait()
                    elif step == 1:
                        arrival(sibling).wait_recv()
                    elif step == 2:
                        arrival(first).wait_recv()
                        to_sibling(first).start()
                        mine_to(first).wait_send()
                        mine_to(second).start()
                        relay(first).start()
                    elif step == 3:
                        arrival(second | sibling).wait_recv()
                    elif step == 4:
                        arrival(second).wait_recv()
                        to_sibling(second).start()
                        relay(second).start()
                    elif step == 5:
                        arrival(first | sibling).wait_recv()
                    elif step == 6:
                        diagonal_half(0).wait_recv()
                        diagonal_half(1).wait_recv()
                        to_sibling(diagonal).start()
                    else:
                        arrival(diagonal | sibling).wait_recv()

        @pl.when((j == 0) & (i == 0))
        def _():
            load(0).start()

        @pl.when(i == 0)
        def _():
            load(j).wait()

        @pl.when((i == n_i - 1) & (j < N_DEV - 1))
        def _():
            load(j + 1).start()

        rows = pl.ds(pl.multiple_of(i * tm, tm), tm)

        @pl.when(j == 0)
        def _():
            h = _rms(x_ref[...], g_ref[...]).astype(BF16)
            h_all[rows, :] = h
            h_ref[...] = h

        proj_ref[...] = jnp.dot(h_all[rows, :], w_buf[j % 2], preferred_element_type=F32)

        @pl.when((j == N_DEV - 1) & (i == n_i - 1))
        def _():
            pltpu.make_async_copy(sb_ref, small_ref.at[me], local.at[1]).wait()
            for mask in range(1, N_DEV):
                remote(1, mask, mask, sb_ref, small_ref.at[me ^ mask]).wait_recv()
                remote(1, mask, mask, sb_ref, small_ref.at[me]).wait_send()
            mine_to(sibling).wait_send()
            for core in range(2):
                @pl.when(cc == core)
                def _(core=core):
                    mine_to(links(core)[1]).wait_send()
            for mask in Exchange.OTHER_CHIPS:
                to_sibling(mask).wait_send()
            relay(y_nbr).wait_send()
            relay(x_nbr).wait_send()

    def first_pass_row(j, i, order):
        return jnp.where(j == 0, i, n_i - 1), 0

    hbm = pl.BlockSpec(memory_space=pl.ANY)
    core = lax.axis_index("c")
    me = 4 * lax.axis_index("x") + 2 * lax.axis_index("y") + core
    order = (me ^ jnp.where(core == 1, jnp.array(block_order(1)), jnp.array(block_order(0)))).astype(jnp.int32)
    grid_spec = pltpu.PrefetchScalarGridSpec(
        num_scalar_prefetch=1, grid=(N_DEV, n_i),
        in_specs=[pl.BlockSpec((tm, d), first_pass_row), pl.BlockSpec((1, d), lambda j, i, order: (0, 0)), hbm, hbm],
        out_specs=[pl.BlockSpec((tm, width), lambda j, i, order: (i, order[j])), pl.BlockSpec((tm, d), first_pass_row), hbm, hbm],
        scratch_shapes=[pltpu.VMEM((s, d), BF16), pltpu.VMEM((2, d, width), BF16), pltpu.SemaphoreType.DMA((3, N_DEV - 1)),
                        pltpu.SemaphoreType.DMA((3, N_DEV - 1)), pltpu.SemaphoreType.DMA((2,)), pltpu.SemaphoreType.DMA((2,))])
    return pl.pallas_call(
        body, name="even_in", grid_spec=grid_spec,
        out_shape=[jax.ShapeDtypeStruct((s, N_DEV * width), F32), jax.ShapeDtypeStruct((s, d), BF16),
                   jax.ShapeDtypeStruct((d, N_DEV * width), w_block.dtype), jax.ShapeDtypeStruct((N_DEV,) + small_block.shape, small_block.dtype)],
        compiler_params=_params(2),
    )(order, x, g, w_block, small_block)


def rms_matmul(x, g, w, tm, tn, name, exchange=None):
    s, d = x.shape
    n = w.shape[1]
    tm = min(tm, s)

    def body(x_ref, g_ref, w_ref, o_ref, h_ref):
        @pl.when(pl.program_id(1) == 0)
        def _():
            h_ref[...] = _rms(x_ref[...], g_ref[...]).astype(BF16)

        o_ref[...] = jnp.dot(h_ref[...], w_ref[...], preferred_element_type=F32)

    return _call(
        body, name=name, grid=(s // tm, n // tn),
        in_specs=[pl.BlockSpec((tm, d), lambda i, j: (i, 0)), _full((1, d)), pl.BlockSpec((d, tn), lambda i, j: (0, j))],
        out_specs=[pl.BlockSpec((tm, tn), lambda i, j: (i, j)), pl.BlockSpec((tm, d), lambda i, j: (i, 0))],
        out_shape=[jax.ShapeDtypeStruct((s, n), F32), jax.ShapeDtypeStruct((s, d), BF16)],
        args=[x, g, w], exchange=exchange)


def _gla_out_bwd(du, r, osum, gn, do_ref, dr_ref, dgn_ref):
    silu_r, dsilu_r = _silu_and_grad(r)
    for head in range(GLA_HEADS):
        vl = slice(head * GLA_DV, (head + 1) * GLA_DV)
        o_h, g_h, du_h = osum[:, vl], gn[:, vl], du[:, vl]
        dr_ref[:, vl] = (du_h * _rms(o_h, g_h) * dsilu_r[:, vl]).astype(BF16)
        do_h, dg_h = _rms_bwd(o_h, g_h, du_h * silu_r[:, vl])
        do_ref[:, vl] = do_h
        dgn_ref[...] += dg_h


def normbwd_matmul_nt(y, g, dout, w, tn, name, exchange=None, gla=None):
    s, d = y.shape
    n = w.shape[0]
    tm = min(MM_TILE, s)

    def body(*refs):
        if gla is None:
            y_ref, g_ref, dout_ref, w_ref, du_ref, dy_ref, dg_ref = refs
        else:
            y_ref, g_ref, dout_ref, w_ref, r_ref, o_ref, gn_ref, do_ref, dr_ref, dy_ref, dg_ref, dgn_ref = refs
        i, j = pl.program_id(0), pl.program_id(1)

        @pl.when(j == 0)
        def _():
            dy, dg = _rms_bwd(y_ref[...], g_ref[...], dout_ref[...])
            dy_ref[...] = dy.astype(BF16)

            @pl.when(i == 0)
            def _():
                dg_ref[...] = jnp.zeros_like(dg_ref)
                if gla is not None:
                    dgn_ref[...] = jnp.zeros_like(dgn_ref)

            dg_ref[...] += dg

        du = lax.dot_general(dy_ref[...], w_ref[...], (((1,), (1,)), ((), ())), preferred_element_type=F32)
        if gla is None:
            du_ref[...] = du
        else:
            _gla_out_bwd(du, r_ref[...], o_ref[...], gn_ref[...], do_ref, dr_ref, dgn_ref)

    row = pl.BlockSpec((tm, d), lambda i, j: (i, 0))
    in_specs = [row, _full((1, d)), row, pl.BlockSpec((tn, d), lambda i, j: (j, 0))]
    args = [y, g, dout, w]
    tail_specs = [row, _full((1, d))]
    tail_shapes = [jax.ShapeDtypeStruct((s, d), BF16), jax.ShapeDtypeStruct((1, d), F32)]
    if gla is None:
        out_specs = [pl.BlockSpec((tm, tn), lambda i, j: (i, j))] + tail_specs
        out_shape = [jax.ShapeDtypeStruct((s, n), F32)] + tail_shapes
    else:
        proj, osum, gnorm = gla
        assert n == tn == D_MODEL
        in_specs += [pl.BlockSpec((tm, D_MODEL), lambda i, j: (i, 2)), row, _full(gnorm.shape)]
        args += [proj, osum, gnorm]
        out_specs = [row, row] + tail_specs + [_full((1, GLA_DV))]
        out_shape = [jax.ShapeDtypeStruct((s, D_MODEL), F32), jax.ShapeDtypeStruct((s, D_MODEL), BF16)] + tail_shapes + [
            jax.ShapeDtypeStruct((1, GLA_DV), F32)]
    return _call(body, name=name, grid=(s // tm, n // tn), in_specs=in_specs, out_specs=out_specs, out_shape=out_shape,
                 args=args, exchange=exchange)


def matmul_tn(a, b, tm, tn, ts, out_dtype, name, exchange=None, b_first=None):
    s, m = a.shape
    n = b.shape[1] + (0 if b_first is None else tn)
    ts = min(ts, s)
    n_k = s // ts
    dims = (((0,), (0,)), ((), ()))

    def body(*refs):
        if b_first is None:
            a_ref, b_ref, o_ref, acc = refs
        else:
            a_ref, first_ref, b_ref, o_ref, acc = refs
        j, k = pl.program_id(1), pl.program_id(2)

        @pl.when(k == 0)
        def _():
            acc[...] = jnp.zeros_like(acc)

        if b_first is None:
            acc[...] += lax.dot_general(a_ref[...], b_ref[...], dims, preferred_element_type=F32)
        else:
            @pl.when(j == 0)
            def _():
                acc[...] += lax.dot_general(a_ref[...], first_ref[...], dims, preferred_element_type=F32)

            @pl.when(j > 0)
            def _():
                acc[...] += lax.dot_general(a_ref[...], b_ref[...], dims, preferred_element_type=F32)

        @pl.when(k == n_k - 1)
        def _():
            o_ref[...] = acc[...].astype(out_dtype)

    if b_first is None:
        b_specs, b_args = [pl.BlockSpec((ts, tn), lambda i, j, k: (k, j))], [b]
    else:
        b_specs = [pl.BlockSpec((ts, tn), lambda i, j, k: (k, 0)), pl.BlockSpec((ts, tn), lambda i, j, k: (k, jnp.maximum(j - 1, 0)))]
        b_args = [b_first, b]
    return _call(
        body, name=name, grid=(m // tm, n // tn, n_k),
        in_specs=[pl.BlockSpec((ts, tm), lambda i, j, k: (k, i))] + b_specs,
        out_specs=pl.BlockSpec((tm, tn), lambda i, j, k: (i, j)),
        out_shape=jax.ShapeDtypeStruct((m, n), out_dtype),
        scratch_shapes=[pltpu.VMEM((tm, tn), F32)], args=[a] + b_args, exchange=exchange)


def matmul_nt_normbwd(dproj, w, x, g, dres, tm, tk, name, exchange=None, first=None):
    s, kt = dproj.shape
    kt += 0 if first is None else tk
    d = w.shape[0]
    tm = min(tm, s)
    n_k = kt // tk
    dims = (((1,), (1,)), ((), ()))

    def body(*refs):
        if first is None:
            a_ref, w_ref, x_ref, g_ref, r_ref, dx_ref, dg_ref, acc = refs
        else:
            first_ref, a_ref, w_ref, x_ref, g_ref, r_ref, dx_ref, dg_ref, acc = refs
        i, k = pl.program_id(0), pl.program_id(1)

        @pl.when(k == 0)
        def _():
            acc[...] = jnp.zeros_like(acc)

        if first is None:
            acc[...] += lax.dot_general(a_ref[...], w_ref[...], dims, preferred_element_type=F32)
        else:
            @pl.when(k == 0)
            def _():
                acc[...] += lax.dot_general(first_ref[...], w_ref[...], dims, preferred_element_type=F32)

            @pl.when(k > 0)
            def _():
                acc[...] += lax.dot_general(a_ref[...], w_ref[...], dims, preferred_element_type=F32)

        @pl.when(k == n_k - 1)
        def _():
            dx, dg = _rms_bwd(x_ref[...], g_ref[...], acc[...])
            dx_ref[...] = r_ref[...] + dx

            @pl.when(i == 0)
            def _():
                dg_ref[...] = jnp.zeros_like(dg_ref)

            dg_ref[...] += dg

    row = pl.BlockSpec((tm, d), lambda i, k: (i, 0))
    if first is None:
        a_specs, a_args = [pl.BlockSpec((tm, tk), lambda i, k: (i, k))], [dproj]
    else:
        a_specs = [pl.BlockSpec((tm, tk), lambda i, k: (i, 0)), pl.BlockSpec((tm, tk), lambda i, k: (i, jnp.maximum(k - 1, 0)))]
        a_args = [first, dproj]
    return _call(
        body, name=name, grid=(s // tm, n_k),
        in_specs=a_specs + [pl.BlockSpec((d, tk), lambda i, k: (0, k)), row, _full((1, d)), row],
        out_specs=[row, _full((1, d))],
        out_shape=[jax.ShapeDtypeStruct((s, d), F32), jax.ShapeDtypeStruct((1, d), F32)],
        scratch_shapes=[pltpu.VMEM((tm, d), F32)], args=a_args + [w, x, g, dres], exchange=exchange)


def _rg_conv(xa, before, after, cw, cb):
    return (cw[0:1, :] * _shift_rows(xa, before, after, -2) + cw[1:2, :] * _shift_rows(xa, before, after, -1)
            + cw[2:3, :] * xa + cw[3:4, :] * _shift_rows(xa, before, after, 1) + cb)


def _rg_gates(ua_h, gw_ref, gb_ref, c_h, direction, head):
    r = _sigmoid(_bdot(ua_h, gw_ref[2 * direction, head]) + gb_ref[2 * direction, head:head + 1, :])
    i = _sigmoid(_bdot(ua_h, gw_ref[2 * direction + 1, head]) + gb_ref[2 * direction + 1, head:head + 1, :])
    log_a = -c_h * r
    a = jnp.exp(log_a)
    beta_sq = -jnp.tanh(log_a) * (1.0 + a * a)
    inv_beta = lax.rsqrt(jnp.maximum(beta_sq, SMALLEST_NORMAL))
    return r, i, a, beta_sq * inv_beta, inv_beta


def even_gates_fwd(proj, conv_w, conv_b, gate_w, gate_b, lam, exchange=None):
    s = proj.shape[0]
    ts = min(2 * ROW_TILE, s)
    n_tiles = s // ts

    def body(xa_ref, xb_ref, xn_ref, cw_ref, cb_ref, gw_ref, gb_ref, lam_ref, o_ref, hf_ref, carry):
        @pl.when(pl.program_id(0) == 0)
        def _():
            carry[...] = jnp.zeros_like(carry)

        xa, before, after = _halo_load(xa_ref, xb_ref, xn_ref, n_tiles)
        ua = _rg_conv(xa, before, after, cw_ref[...], cb_ref[...])
        c = RG_C * _softplus(-lam_ref[...])
        ua_bf16 = ua.astype(BF16)
        for direction in range(2):
            for head in range(RG_HEADS):
                lanes = slice(head * RG_HEAD_DIM, (head + 1) * RG_HEAD_DIM)
                ua_h = ua[:, lanes]
                _, i, a, beta, _ = _rg_gates(ua_bf16[:, lanes], gw_ref, gb_ref, c[direction:direction + 1, lanes], direction, head)
                o_ref[2 * direction, :, lanes] = a
                o_ref[2 * direction + 1, :, lanes] = beta * (i * ua_h)
        _scan_tile(o_ref.at[0], o_ref.at[1], hf_ref, carry, False, False)

    return _call(
        body, name="even_gates_fwd", grid=(n_tiles,),
        in_specs=_halo_specs(ts, s, D_MODEL, 0) + [_full(conv_w.shape), _full(conv_b.shape), _full(gate_w.shape),
                                                   _full(gate_b.shape), _full(lam.shape)],
        out_specs=[pl.BlockSpec((4, ts, D_MODEL), lambda i: (0, i, 0)), pl.BlockSpec((ts, D_MODEL), lambda i: (i, 0))],
        out_shape=[jax.ShapeDtypeStruct((4, s, D_MODEL), F32), jax.ShapeDtypeStruct((s, D_MODEL), F32)],
        scratch_shapes=[pltpu.VMEM((SUBLANES, D_MODEL), F32)],
        args=[proj, proj, proj, conv_w, conv_b, gate_w, gate_b, lam], exchange=exchange)


def _scan_tile(a_ref, b_ref, h_ref, carry, reverse, b_times_a):
    ts, c = h_ref.shape
    n_blocks = ts // SUBLANES
    row = lax.broadcasted_iota(jnp.int32, (SUBLANES, c), 0)

    def block(j, h_in):
        r0 = pl.multiple_of((n_blocks - 1 - j if reverse else j) * SUBLANES, SUBLANES)
        a = a_ref[pl.ds(r0, SUBLANES), :]
        b = b_ref[pl.ds(r0, SUBLANES), :]
        if b_times_a:
            b = a * b
        for step in (1, 2, 4):
            shift = SUBLANES - step if reverse else step
            valid = row < SUBLANES - step if reverse else row >= step
            b = jnp.where(valid, a * pltpu.roll(b, shift, 0) + b, b)
            a = jnp.where(valid, a * pltpu.roll(a, shift, 0), a)
        h = a * h_in + b
        h_ref[pl.ds(r0, SUBLANES), :] = h
        return h[0:1, :] if reverse else h[SUBLANES - 1:SUBLANES, :]

    carry[0:1, :] = lax.fori_loop(0, n_blocks, block, carry[0:1, :])


def linear_scan(a_arr, a_idx, b_arr, b_idx, reverse, b_times_a, name, exchange=None):
    _, s, c = a_arr.shape
    ts = min(MM_TILE, s)
    n_tiles = s // ts

    def tile_of(i):
        return n_tiles - 1 - i if reverse else i

    def body(a_ref, b_ref, h_ref, carry):
        @pl.when(pl.program_id(0) == 0)
        def _():
            carry[...] = jnp.zeros_like(carry)

        _scan_tile(a_ref, b_ref, h_ref, carry, reverse, b_times_a)

    return _call(
        body, name=name, grid=(n_tiles,),
        in_specs=[pl.BlockSpec((None, ts, c), lambda i: (a_idx, tile_of(i), 0)),
                  pl.BlockSpec((None, ts, c), lambda i: (b_idx, tile_of(i), 0))],
        out_specs=pl.BlockSpec((ts, c), lambda i: (tile_of(i), 0)),
        out_shape=jax.ShapeDtypeStruct((s, c), F32),
        scratch_shapes=[pltpu.VMEM((SUBLANES, c), F32)], args=[a_arr, b_arr], exchange=exchange)


def _sc_conv(p, before, after, w):
    return w[0:1, :] * _shift_rows(p, before, after, -1) + w[1:2, :] * p + w[2:3, :] * _shift_rows(p, before, after, 1)


def even_mix_fwd(ab, hf, proj, sc_w, w_out, xres, g_post, exchange=None):
    s = proj.shape[0]
    ts = min(ROW_TILE, s)
    n_tiles = s // ts

    def tile(i):
        return n_tiles - 1 - i

    row = pl.BlockSpec((ts, D_MODEL), lambda i: (tile(i), 0))

    def col(c):
        return pl.BlockSpec((ts, D_MODEL), lambda i: (tile(i), c))

    def body(a_ref, b_ref, hf_ref, za_ref, xb_ref, xbb_ref, xbn_ref, gb_ref, gc_ref, gcb_ref, gcn_ref, zb_ref, w_ref,
             wo_ref, x_ref, g_ref, u_ref, hb_ref, y_ref, out_ref, carry):
        @pl.when(pl.program_id(0) == 0)
        def _():
            carry[...] = jnp.zeros_like(carry)

        _scan_tile(a_ref, b_ref, hb_ref, carry, True, False)
        xb, xb_before, xb_after = _halo_load(xb_ref, xbb_ref, xbn_ref, n_tiles, tile)
        gc, gc_before, gc_after = _halo_load(gc_ref, gcb_ref, gcn_ref, n_tiles, tile)
        silu_za, _ = _silu_and_grad(za_ref[...])
        silu_zb, _ = _silu_and_grad(zb_ref[...])
        u_ref[:, :D_MODEL] = ((hf_ref[...] + hb_ref[...]) * silu_za).astype(BF16)
        cv = _sc_conv(gc * xb, gc_before * xb_before, gc_after * xb_after, w_ref[...])
        u_ref[:, D_MODEL:] = (gb_ref[...] * cv * silu_zb).astype(BF16)
        y = jnp.dot(u_ref[...], wo_ref[...], preferred_element_type=F32)
        y_ref[...] = y
        out_ref[...] = x_ref[...] + _rms(y, g_ref[...])

    return _call(
        body, name="even_mix_fwd", grid=(n_tiles,),
        in_specs=[pl.BlockSpec((None, ts, D_MODEL), lambda i: (2, tile(i), 0)), pl.BlockSpec((None, ts, D_MODEL), lambda i: (3, tile(i), 0)),
                  row, col(1)] + _halo_specs(ts, s, D_MODEL, 2, tile) + [col(3)] + _halo_specs(ts, s, D_MODEL, 4, tile)
        + [col(5), _full(sc_w.shape), _full(w_out.shape), row, _full(g_post.shape)],
        out_specs=[pl.BlockSpec((ts, 2 * D_MODEL), lambda i: (tile(i), 0)), row, row, row],
        out_shape=[jax.ShapeDtypeStruct((s, 2 * D_MODEL), BF16)] + [jax.ShapeDtypeStruct((s, D_MODEL), F32)] * 3,
        scratch_shapes=[pltpu.VMEM((SUBLANES, D_MODEL), F32)],
        args=[ab, ab, hf, proj, proj, proj, proj, proj, proj, proj, proj, proj, sc_w, w_out, xres, g_post], exchange=exchange)


def even_mix_bwd(du, hf, hb, proj, sc_w, ab, exchange=None):
    s = proj.shape[0]
    ts = min(ROW_TILE, s)
    n_tiles = s // ts
    row = pl.BlockSpec((ts, D_MODEL), lambda i: (i, 0))

    def body(dya_ref, dyb_ref, dybb_ref, dybn_ref, hf_ref, hb_ref, za_ref, xb_ref, xbb_ref, xbn_ref,
             gb_ref, gbb_ref, gbn_ref, gc_ref, gcb_ref, gcn_ref, zb_ref, zbb_ref, zbn_ref, w_ref, a_ref,
             dh_ref, dp_ref, dw_ref, adj_ref, carry):
        @pl.when(pl.program_id(0) == 0)
        def _():
            carry[...] = jnp.zeros_like(carry)

        dyb, dyb_before, dyb_after = _halo_load(dyb_ref, dybb_ref, dybn_ref, n_tiles)
        xb, xb_before, xb_after = _halo_load(xb_ref, xbb_ref, xbn_ref, n_tiles)
        gb, gb_before, gb_after = _halo_load(gb_ref, gbb_ref, gbn_ref, n_tiles)
        gc, gc_before, gc_after = _halo_load(gc_ref, gcb_ref, gcn_ref, n_tiles)
        zb, zb_before, zb_after = _halo_load(zb_ref, zbb_ref, zbn_ref, n_tiles)
        w = w_ref[...]
        dya, za = dya_ref[...], za_ref[...]
        silu_za, dsilu_za = _silu_and_grad(za)
        dh_ref[...] = dya * silu_za
        _scan_tile(a_ref, dh_ref, adj_ref, carry, False, True)
        dp_ref[:, 0:D_MODEL] = (dya * (hf_ref[...] + hb_ref[...]) * dsilu_za).astype(BF16)

        silu_zb, dsilu_zb = _silu_and_grad(zb)
        p, p_before, p_after = gc * xb, gc_before * xb_before, gc_after * xb_after
        cv = _sc_conv(p, p_before, p_after, w)
        dcv = dyb * gb * silu_zb
        dcv_before = dyb_before * gb_before * _silu_and_grad(zb_before)[0]
        dcv_after = dyb_after * gb_after * _silu_and_grad(zb_after)[0]
        dpp = (w[0:1, :] * _shift_rows(dcv, dcv_before, dcv_after, 1) + w[1:2, :] * dcv
               + w[2:3, :] * _shift_rows(dcv, dcv_before, dcv_after, -1))
        dp_ref[:, D_MODEL:2 * D_MODEL] = (dpp * gc).astype(BF16)
        dp_ref[:, 2 * D_MODEL:3 * D_MODEL] = (dyb * cv * silu_zb).astype(BF16)
        dp_ref[:, 3 * D_MODEL:4 * D_MODEL] = (dpp * xb).astype(BF16)
        dp_ref[:, 4 * D_MODEL:5 * D_MODEL] = (dyb * gb * cv * dsilu_zb).astype(BF16)

        @pl.when(pl.program_id(0) == 0)
        def _():
            dw_ref[...] = jnp.zeros_like(dw_ref)

        dw_ref[0:1, :] += jnp.sum(dcv * _shift_rows(p, p_before, p_after, -1), axis=0, keepdims=True)
        dw_ref[1:2, :] += jnp.sum(dcv * p, axis=0, keepdims=True)
        dw_ref[2:3, :] += jnp.sum(dcv * _shift_rows(p, p_before, p_after, 1), axis=0, keepdims=True)

    return _call(
        body, name="even_mix_bwd", grid=(n_tiles,),
        in_specs=[row] + _halo_specs(ts, s, D_MODEL, 1) + [row, row, pl.BlockSpec((ts, D_MODEL), lambda i: (i, 1))]
        + _halo_specs(ts, s, D_MODEL, 2) + _halo_specs(ts, s, D_MODEL, 3) + _halo_specs(ts, s, D_MODEL, 4)
        + _halo_specs(ts, s, D_MODEL, 5) + [_full(sc_w.shape), pl.BlockSpec((None, ts, D_MODEL), lambda i: (2, i, 0))],
        out_specs=[row, pl.BlockSpec((ts, 5 * D_MODEL), lambda i: (i, 0)), _full(sc_w.shape), row],
        out_shape=[jax.ShapeDtypeStruct((s, D_MODEL), F32), jax.ShapeDtypeStruct((s, 5 * D_MODEL), BF16),
                   jax.ShapeDtypeStruct(sc_w.shape, F32), jax.ShapeDtypeStruct((s, D_MODEL), F32)],
        scratch_shapes=[pltpu.VMEM((SUBLANES, D_MODEL), F32)],
        args=[du, du, du, du, hf, hb, proj, *([proj] * 12), sc_w, ab], exchange=exchange)


def even_gates_bwd(proj, adj_f, adj_b, hf, hb, dh, conv_w, conv_b, gate_w, gate_b, lam, exchange=None):
    s = proj.shape[0]
    ts = min(2 * ROW_TILE, s)
    n_tiles = s // ts
    row = pl.BlockSpec((ts, D_MODEL), lambda i: (i, 0))

    def body(xa_ref, xab_ref, xan_ref, af_ref, afb_ref, afn_ref, ab_ref, abb_ref, abn_ref,
             hf_ref, hfb_ref, hfn_ref, hb_ref, hbb_ref, hbn_ref, dh_ref,
             cw_ref, cb_ref, gw_ref, gb_ref, lam_ref, dua_ref, dgw_ref, dgb_ref, dlam_ref):
        @pl.when(pl.program_id(0) == 0)
        def _():
            dgw_ref[...] = jnp.zeros_like(dgw_ref)
            dgb_ref[...] = jnp.zeros_like(dgb_ref)
            dlam_ref[...] = jnp.zeros_like(dlam_ref)

        xa, before, after = _halo_load(xa_ref, xab_ref, xan_ref, n_tiles)
        ua = _rg_conv(xa, before, after, cw_ref[...], cb_ref[...])
        lam_v = lam_ref[...]
        c = RG_C * _softplus(-lam_v)
        dc_dlam = -RG_C * _sigmoid(-lam_v)
        dh = dh_ref[...]
        adj = (_halo_load(af_ref, afb_ref, afn_ref, n_tiles), _halo_load(ab_ref, abb_ref, abn_ref, n_tiles))
        hs = (_halo_load(hf_ref, hfb_ref, hfn_ref, n_tiles), _halo_load(hb_ref, hbb_ref, hbn_ref, n_tiles))
        dua = jnp.zeros_like(ua)
        ua_bf16 = ua.astype(BF16)
        for direction in range(2):
            step = 1 if direction == 0 else -1
            g = dh + _shift_rows(*adj[direction], step)
            da_all = g * _shift_rows(*hs[direction], -step)
            dua_parts = []
            for head in range(RG_HEADS):
                lanes = slice(head * RG_HEAD_DIM, (head + 1) * RG_HEAD_DIM)
                ua_h = ua[:, lanes]
                c_h = c[direction:direction + 1, lanes]
                ua_hb = ua_bf16[:, lanes]
                r, i, a, beta, inv_beta = _rg_gates(ua_hb, gw_ref, gb_ref, c_h, direction, head)
                db_beta = g[:, lanes] * beta
                d_i = db_beta * ua_h
                dbeta = g[:, lanes] * (i * ua_h)
                dlog_a = (da_all[:, lanes] - dbeta * a * inv_beta) * a
                dpr = -c_h * dlog_a * r * (1.0 - r)
                dpi = d_i * i * (1.0 - i)
                dpr_b, dpi_b = dpr.astype(BF16), dpi.astype(BF16)
                dua_parts.append(db_beta * i + _bdot_nt(dpr_b, gw_ref[2 * direction, head])
                                 + _bdot_nt(dpi_b, gw_ref[2 * direction + 1, head]))
                dgw_ref[2 * direction, head] += _bdot_tn(ua_hb, dpr_b)
                dgw_ref[2 * direction + 1, head] += _bdot_tn(ua_hb, dpi_b)
                dgb_ref[2 * direction, head:head + 1, :] += jnp.sum(dpr, axis=0, keepdims=True)
                dgb_ref[2 * direction + 1, head:head + 1, :] += jnp.sum(dpi, axis=0, keepdims=True)
                dlam_ref[direction:direction + 1, lanes] += (
                    jnp.sum(-r * dlog_a, axis=0, keepdims=True) * dc_dlam[direction:direction + 1, lanes])
            dua = dua + jnp.concatenate(dua_parts, axis=1)
        dua_ref[...] = dua

    return _call(
        body, name="even_gates_bwd", grid=(n_tiles,),
        in_specs=_halo_specs(ts, s, D_MODEL, 0) * 5 + [row] + [_full(conv_w.shape), _full(conv_b.shape), _full(gate_w.shape),
                                                             _full(gate_b.shape), _full(lam.shape)],
        out_specs=[row, _full(gate_w.shape), _full(gate_b.shape), _full(lam.shape)],
        out_shape=[jax.ShapeDtypeStruct((s, D_MODEL), F32), jax.ShapeDtypeStruct(gate_w.shape, F32),
                   jax.ShapeDtypeStruct(gate_b.shape, F32), jax.ShapeDtypeStruct(lam.shape, F32)],
        args=[proj, proj, proj, adj_f, adj_f, adj_f, adj_b, adj_b, adj_b, hf, hf, hf, hb, hb, hb, dh, conv_w, conv_b, gate_w,
              gate_b, lam], exchange=exchange)


def rg_conv_bwd(dua, proj, conv_w, exchange=None):
    s = proj.shape[0]
    ts = min(2 * ROW_TILE, s)
    n_tiles = s // ts

    def body(du_ref, dub_ref, dun_ref, xa_ref, xab_ref, xan_ref, cw_ref, dp_ref, dw_ref, db_ref):
        @pl.when(pl.program_id(0) == 0)
        def _():
            dw_ref[...] = jnp.zeros_like(dw_ref)
            db_ref[...] = jnp.zeros_like(db_ref)

        dua, dua_before, dua_after = _halo_load(du_ref, dub_ref, dun_ref, n_tiles)
        xa, xa_before, xa_after = _halo_load(xa_ref, xab_ref, xan_ref, n_tiles)
        cw = cw_ref[...]
        dxa = (cw[0:1, :] * _shift_rows(dua, dua_before, dua_after, 2) + cw[1:2, :] * _shift_rows(dua, dua_before, dua_after, 1)
               + cw[2:3, :] * dua + cw[3:4, :] * _shift_rows(dua, dua_before, dua_after, -1))
        dp_ref[...] = dxa.astype(BF16)
        for tap, offset in enumerate((-2, -1, 0, 1)):
            shifted = xa if offset == 0 else _shift_rows(xa, xa_before, xa_after, offset)
            dw_ref[tap:tap + 1, :] += jnp.sum(dua * shifted, axis=0, keepdims=True)
        db_ref[...] += jnp.sum(dua, axis=0, keepdims=True)

    return _call(
        body, name="rg_conv_bwd", grid=(n_tiles,),
        in_specs=_halo_specs(ts, s, D_MODEL, 0) * 2 + [_full(conv_w.shape)],
        out_specs=[pl.BlockSpec((ts, D_MODEL), lambda i: (i, 0)), _full(conv_w.shape), _full((1, D_MODEL))],
        out_shape=[jax.ShapeDtypeStruct((s, D_MODEL), BF16), jax.ShapeDtypeStruct(conv_w.shape, F32),
                   jax.ShapeDtypeStruct((1, D_MODEL), F32)],
        args=[dua, dua, dua, proj, proj, proj, conv_w], exchange=exchange)


def _split3(x):
    x1 = x.astype(BF16)
    rest = x - x1.astype(F32)
    x2 = rest.astype(BF16)
    return x1, x2, (rest - x2.astype(F32)).astype(BF16)


def _chunk_sum_matrix(t, reverse, transpose):
    i = lax.broadcasted_iota(jnp.int32, (t, t), 0)
    j = lax.broadcasted_iota(jnp.int32, (t, t), 1)
    if transpose:
        i, j = j, i
    same = (i // GLA_CHUNK) == (j // GLA_CHUNK)
    return jnp.where(same & ((j >= i) if reverse else (j <= i)), 1.0, 0.0).astype(BF16)


def _exact_dot(m, x):
    return sum(jnp.dot(m, part, preferred_element_type=F32) for part in _split3(x))


def _chunk_mask(t, reverse):
    i = lax.broadcasted_iota(jnp.int32, (t, t), 0)
    j = lax.broadcasted_iota(jnp.int32, (t, t), 1)
    return ((i // GLA_CHUNK) == (j // GLA_CHUNK)) & ((j >= i) if reverse else (j <= i))


def _chunk_rows(c):
    return slice(c * GLA_CHUNK, (c + 1) * GLA_CHUNK)


def _gla_gate(lr, wg, bg):
    z = _bdot(lr, wg) + bg
    log_alpha = (jnp.minimum(z, 0.0) - jnp.log(1.0 + jnp.exp(-jnp.abs(z)))) * (1.0 / GLA_NORMALIZER)
    return z, log_alpha


def _gla_tile_terms(q, k, bcum, reverse):
    n_chunks = q.shape[0] // GLA_CHUNK
    totals = []
    for c in range(n_chunks):
        edge = c * GLA_CHUNK if reverse else (c + 1) * GLA_CHUNK - 1
        totals.append(bcum[edge:edge + 1, :])
    btot = jnp.concatenate([jnp.broadcast_to(total, (GLA_CHUNK, total.shape[1])) for total in totals], axis=0)
    e_pos, e_neg, e_st = jnp.exp(bcum), jnp.exp(-bcum), jnp.exp(btot - bcum)
    return q * (GLA_DK ** -0.5) * e_pos, k * e_neg, k * e_st, e_pos, e_neg, e_st, [jnp.exp(total) for total in totals]


def _gla_specs(t, n_tiles, reverse_order):
    def tile(i):
        return n_tiles - 1 - i if reverse_order else i

    return tile, [
        pl.BlockSpec((t, GLA_KEY), lambda i: (tile(i), 0)),
        pl.BlockSpec((t, GLA_KEY), lambda i: (tile(i), 1)),
        pl.BlockSpec((t, D_MODEL), lambda i: (tile(i), 1)),
        pl.BlockSpec((t, LANES), lambda i: (tile(i), (ODD_IN_PAD - LANES) // LANES)),
    ]


def gla_fwd(proj, wg, bg, reverse, o_other=None, gnorm=None, post=None):
    s = proj.shape[0]
    t = min(GLA_TILE, s)
    n_tiles = s // t
    n_chunks = t // GLA_CHUNK
    final = o_other is not None
    tile, specs = _gla_specs(t, n_tiles, reverse)

    def body(*refs):
        if final:
            (q_ref, k_ref, v_ref, lr_ref, wg_ref, bg_ref, oo_ref, r_ref, gn_ref, wo_ref, x_ref, gp_ref, t_ref,
             osum_ref, u_ref, st_ref, y_ref, dout_ref, loss_ref, state) = refs
        else:
            q_ref, k_ref, v_ref, lr_ref, wg_ref, bg_ref, o_ref, st_ref, state = refs
            osum_ref = o_ref

        @pl.when(pl.program_id(0) == 0)
        def _():
            state[...] = jnp.zeros_like(state)

        _, log_alpha = _gla_gate(lr_ref[...], wg_ref[...], bg_ref[...])
        bcum = _exact_dot(_chunk_sum_matrix(t, reverse, False), log_alpha)
        q, k, v = q_ref[...], k_ref[...], v_ref[...]
        q_in, k_in, k_st, _, _, _, decays = _gla_tile_terms(q, k, bcum, reverse)
        mask = _chunk_mask(t, reverse)
        order = list(range(n_chunks))[::-1] if reverse else list(range(n_chunks))
        intra, increments = [], []
        for head in range(GLA_HEADS):
            kl = slice(head * GLA_DK, (head + 1) * GLA_DK)
            vl = slice(head * GLA_DV, (head + 1) * GLA_DV)
            scores = jnp.where(mask, _bdot_nt(q_in[:, kl], k_in[:, kl]), 0.0)
            intra.append(_bdot(scores, v[:, vl]))
            increments.append([_bdot_tn(v[_chunk_rows(c), vl], k_st[_chunk_rows(c), kl]) for c in range(n_chunks)])
        for head in range(GLA_HEADS):
            kl = slice(head * GLA_DK, (head + 1) * GLA_DK)
            vl = slice(head * GLA_DV, (head + 1) * GLA_DV)
            running = state[head]
            before = [None] * n_chunks
            for c in order:
                before[c] = running
                st_ref[c, head] = running
                running = running * decays[c][:, kl] + increments[head][c]
            state[head] = running
            inter = [_bdot_nt(q_in[_chunk_rows(c), kl], before[c]) for c in range(n_chunks)]
            osum_ref[:, vl] = intra[head] + jnp.concatenate(inter, axis=0)
        if final:
            osum = osum_ref[...] + oo_ref[...]
            osum_ref[...] = osum
            silu_r, _ = _silu_and_grad(r_ref[...])
            gn = gn_ref[...]
            for head in range(GLA_HEADS):
                vl = slice(head * GLA_DV, (head + 1) * GLA_DV)
                u_ref[:, vl] = (_rms(osum[:, vl], gn[:, vl]) * silu_r[:, vl]).astype(BF16)

            @pl.when(pl.program_id(0) == 0)
            def _():
                loss_ref[...] = jnp.zeros_like(loss_ref)

            y = jnp.dot(u_ref[...], wo_ref[...], preferred_element_type=F32)
            y_ref[...] = y
            diff = x_ref[...] + _rms(y, gp_ref[...]) - t_ref[...]
            dout_ref[...] = diff * (1.0 / D_MODEL)
            loss_ref[...] += 0.5 * jnp.sum(jnp.mean(diff * diff, axis=-1, keepdims=True))

    row = pl.BlockSpec((t, D_MODEL), lambda i: (tile(i), 0))
    st_spec = pl.BlockSpec((n_chunks, GLA_HEADS, GLA_DV, GLA_DK), lambda i: (tile(i), 0, 0, 0))
    st_shape = jax.ShapeDtypeStruct((s // GLA_CHUNK, GLA_HEADS, GLA_DV, GLA_DK), F32)
    in_specs = specs + [_full(wg.shape), _full(bg.shape)]
    args = [proj, proj, proj, proj, wg, bg]
    if final:
        w_out, xres, g_post, target = post
        in_specs += [row, pl.BlockSpec((t, D_MODEL), lambda i: (tile(i), 2)), _full(gnorm.shape), _full(w_out.shape), row,
                     _full(g_post.shape), row]
        args += [o_other, proj, gnorm, w_out, xres, g_post, target]
        out_specs = [row, row, st_spec, row, row, _full((SUBLANES, LANES))]
        out_shape = [jax.ShapeDtypeStruct((s, D_MODEL), F32), jax.ShapeDtypeStruct((s, D_MODEL), BF16), st_shape,
                     jax.ShapeDtypeStruct((s, D_MODEL), F32), jax.ShapeDtypeStruct((s, D_MODEL), F32),
                     jax.ShapeDtypeStruct((SUBLANES, LANES), F32)]
    else:
        out_specs = [row, st_spec]
        out_shape = [jax.ShapeDtypeStruct((s, D_MODEL), F32), st_shape]
    return pl.pallas_call(
        body, name="gla_fwd_rev" if reverse else "gla_fwd", grid=(n_tiles,), in_specs=in_specs, out_specs=out_specs,
        out_shape=out_shape, scratch_shapes=[pltpu.VMEM((GLA_HEADS, GLA_DV, GLA_DK), F32)], compiler_params=_params(1),
    )(*args)


def gla_bwd(proj, wg, bg, do, states, reverse, first=None):
    s = proj.shape[0]
    t = min(GLA_TILE, s)
    n_tiles = s // t
    n_chunks = t // GLA_CHUNK
    final = first is not None
    tile, specs = _gla_specs(t, n_tiles, not reverse)

    def body(*refs):
        if final:
            (q_ref, k_ref, v_ref, lr_ref, wg_ref, bg_ref, do_ref, st_ref, dqkv1_ref, dlr1_ref, dr_ref,
             dp_ref, dwg_ref, dbg_ref, dstate, dqkv, dbc, dbt) = refs
        else:
            (q_ref, k_ref, v_ref, lr_ref, wg_ref, bg_ref, do_ref, st_ref,
             dqkv, dlr_ref, dwg_ref, dbg_ref, dstate, dbc, dbt) = refs

        @pl.when(pl.program_id(0) == 0)
        def _():
            dstate[...] = jnp.zeros_like(dstate)
            dwg_ref[...] = jnp.zeros_like(dwg_ref)
            dbg_ref[...] = jnp.zeros_like(dbg_ref)

        lr, wg_v = lr_ref[...], wg_ref[...]
        z, log_alpha = _gla_gate(lr, wg_v, bg_ref[...])
        bcum = _exact_dot(_chunk_sum_matrix(t, reverse, False), log_alpha)
        q, k, v, do_v = q_ref[...], k_ref[...], v_ref[...], do_ref[...]
        q_in, k_in, k_st, e_pos, e_neg, e_st, decays = _gla_tile_terms(q, k, bcum, reverse)
        mask = _chunk_mask(t, reverse)
        order = list(range(n_chunks)) if reverse else list(range(n_chunks))[::-1]
        q_b, k_b, ks_b, v_b, do_b = (a.astype(BF16) for a in (q_in, k_in, k_st, v, do_v))
        dq_intra, dk_intra, dv_intra, increments = [], [], [], []
        for head in range(GLA_HEADS):
            kl = slice(head * GLA_DK, (head + 1) * GLA_DK)
            vl = slice(head * GLA_DV, (head + 1) * GLA_DV)
            scores = jnp.where(mask, _bdot_nt(q_b[:, kl], k_b[:, kl]), 0.0).astype(BF16)
            dscores = jnp.where(mask, _bdot_nt(do_b[:, vl], v_b[:, vl]), 0.0).astype(BF16)
            dv_intra.append(_bdot_tn(scores, do_b[:, vl]))
            dq_intra.append(_bdot(dscores, k_b[:, kl]))
            dk_intra.append(_bdot_tn(dscores, q_b[:, kl]))
            increments.append([_bdot_tn(do_b[_chunk_rows(c), vl], q_b[_chunk_rows(c), kl]) for c in range(n_chunks)])
        after_all, ddecay_all = [], []
        for head in range(GLA_HEADS):
            kl = slice(head * GLA_DK, (head + 1) * GLA_DK)
            running = dstate[head]
            after, ddecay = [None] * n_chunks, [None] * n_chunks
            for c in order:
                after[c] = running
                ddecay[c] = jnp.sum(running * st_ref[c, head], axis=0, keepdims=True)
                running = running * decays[c][:, kl] + increments[head][c]
            dstate[head] = running
            after_all.append(after)
            ddecay_all.append(ddecay)
        for head in range(GLA_HEADS):
            kl = slice(head * GLA_DK, (head + 1) * GLA_DK)
            vl = slice(head * GLA_DV, (head + 1) * GLA_DV)
            after, ddecay = after_all[head], ddecay_all[head]
            dq_inter = jnp.concatenate([_bdot(do_b[_chunk_rows(c), vl], st_ref[c, head]) for c in range(n_chunks)], axis=0)
            dv_inter = jnp.concatenate([_bdot_nt(ks_b[_chunk_rows(c), kl], after[c]) for c in range(n_chunks)], axis=0)
            dk_st = jnp.concatenate([_bdot(v_b[_chunk_rows(c), vl], after[c]) for c in range(n_chunks)], axis=0)
            dq_in = dq_intra[head] + dq_inter
            ks_h = k_st[:, kl]
            dqkv[:, 2 * GLA_KEY + head * GLA_DV:2 * GLA_KEY + (head + 1) * GLA_DV] = dv_intra[head] + dv_inter
            dqkv[:, kl] = dq_in * (GLA_DK ** -0.5) * e_pos[:, kl]
            dqkv[:, GLA_KEY + head * GLA_DK:GLA_KEY + (head + 1) * GLA_DK] = dk_intra[head] * e_neg[:, kl] + dk_st * e_st[:, kl]
            dbc[:, kl] = dq_in * q_in[:, kl] - dk_intra[head] * k_in[:, kl] - dk_st * ks_h
            weighted = dk_st * ks_h
            for c in range(n_chunks):
                dbtot = jnp.sum(weighted[_chunk_rows(c)], axis=0, keepdims=True) + ddecay[c] * decays[c][:, kl]
                dbt[_chunk_rows(c), kl] = jnp.broadcast_to(dbtot, (GLA_CHUNK, GLA_DK))
        dlog_alpha = _exact_dot(_chunk_sum_matrix(t, reverse, True), dbc[...]) + dbt[...]
        dz = dlog_alpha * _sigmoid(-z) * (1.0 / GLA_NORMALIZER)
        dlr = _bdot_nt(dz, wg_v)
        dwg_ref[...] += _bdot_tn(lr, dz)
        dbg_ref[...] += jnp.sum(dz, axis=0, keepdims=True)
        if final:
            dp_ref[:, :2 * D_MODEL] = (dqkv[...] + dqkv1_ref[...]).astype(BF16)
            dp_ref[:, 2 * D_MODEL:3 * D_MODEL] = dr_ref[...]
            dp_ref[:, 3 * D_MODEL:] = (dlr + dlr1_ref[...]).astype(BF16)
        else:
            dlr_ref[...] = dlr

    row = pl.BlockSpec((t, D_MODEL), lambda i: (tile(i), 0))
    wide = pl.BlockSpec((t, 2 * D_MODEL), lambda i: (tile(i), 0))
    narrow = pl.BlockSpec((t, LANES), lambda i: (tile(i), 0))
    st_spec = pl.BlockSpec((n_chunks, GLA_HEADS, GLA_DV, GLA_DK), lambda i: (tile(i), 0, 0, 0))
    in_specs = specs + [_full(wg.shape), _full(bg.shape), row, st_spec]
    args = [proj, proj, proj, proj, wg, bg, do, states]
    acc_specs = [_full(wg.shape), _full(bg.shape)]
    acc_shapes = [jax.ShapeDtypeStruct(wg.shape, F32), jax.ShapeDtypeStruct(bg.shape, F32)]
    scratch = [pltpu.VMEM((GLA_HEADS, GLA_DV, GLA_DK), F32)]
    work = [pltpu.VMEM((t, GLA_KEY), F32), pltpu.VMEM((t, GLA_KEY), F32)]
    if final:
        in_specs += [wide, narrow, row]
        args += list(first)
        out_specs = [pl.BlockSpec((t, ODD_IN_PAD), lambda i: (tile(i), 0))] + acc_specs
        out_shape = [jax.ShapeDtypeStruct((s, ODD_IN_PAD), BF16)] + acc_shapes
        scratch += [pltpu.VMEM((t, 2 * D_MODEL), F32)] + work
    else:
        out_specs = [wide, narrow] + acc_specs
        out_shape = [jax.ShapeDtypeStruct((s, 2 * D_MODEL), F32), jax.ShapeDtypeStruct((s, LANES), F32)] + acc_shapes
        scratch += work
    return pl.pallas_call(
        body, name="gla_bwd_rev" if reverse else "gla_bwd", grid=(n_tiles,), in_specs=in_specs, out_specs=out_specs,
        out_shape=out_shape, scratch_shapes=scratch, compiler_params=_params(1),
    )(*args)


def column_blocks(a, width):
    r, c = a.shape
    window = -(-(width + LANES) // LANES) * LANES
    padded = -(-width // LANES) * LANES
    assert window <= c

    def body(a_ref, o_ref):
        row = lax.broadcasted_iota(jnp.int32, (window, padded), 0)
        col = lax.broadcasted_iota(jnp.int32, (window, padded), 1)
        for j in range(N_DEV):
            start = min(j * width // LANES * LANES, c - window)
            pick = jnp.where((row == col + (j * width - start)) & (col < width), 1.0, 0.0).astype(BF16)
            picked = jnp.dot(a_ref[:, start:start + window], pick, preferred_element_type=F32)
            o_ref[j] = picked[:, :width].astype(o_ref.dtype)

    return pl.pallas_call(
        body, name="column_blocks", grid=(1,), in_specs=[_full((r, c))], out_specs=_full((N_DEV, r, width)),
        out_shape=jax.ShapeDtypeStruct((N_DEV, r, width), a.dtype), compiler_params=_params(1),
    )(a)


def columns_from_blocks(parts, total):
    width = parts[0].shape[2]
    rows = [p.shape[1] for p in parts]
    window = -(-(width + LANES) // LANES) * LANES

    def body(*refs):
        part_refs, o_ref, acc = refs[:len(parts)], refs[len(parts)], refs[len(parts) + 1]
        acc[...] = jnp.zeros_like(acc)
        row = lax.broadcasted_iota(jnp.int32, (width, window), 0)
        col = lax.broadcasted_iota(jnp.int32, (width, window), 1)
        for j in range(N_DEV):
            start = min(j * width // LANES * LANES, total - window)
            place = jnp.where(col == row + (j * width - start), 1.0, 0.0).astype(BF16)
            at = 0
            for part_ref, r in zip(part_refs, rows):
                acc[at:at + r, start:start + window] += jnp.dot(part_ref[j], place, preferred_element_type=F32)
                at += r
        o_ref[...] = acc[...].astype(o_ref.dtype)

    return pl.pallas_call(
        body, name="columns_from_blocks", grid=(1,), in_specs=[_full(p.shape) for p in parts], out_specs=_full((sum(rows), total)),
        out_shape=jax.ShapeDtypeStruct((sum(rows), total), parts[0].dtype), scratch_shapes=[pltpu.VMEM((sum(rows), total), F32)],
        compiler_params=_params(1),
    )(*parts)


def pair_sum(grad, from_sibling):
    n_chips, r, w = from_sibling.shape

    def body(even_ref, odd_ref, sib_ref, o_ref):
        mine = jnp.where(lax.axis_index("c") == 1, odd_ref[...], even_ref[...])
        o_ref[...] = (mine.astype(F32) + sib_ref[...].astype(F32)).astype(o_ref.dtype)

    return pl.pallas_call(
        body, name="pair_sum", grid=(n_chips,),
        in_specs=[pl.BlockSpec((r, w), lambda k: (0, 2 * k)), pl.BlockSpec((r, w), lambda k: (0, 2 * k + 1)),
                  pl.BlockSpec((None, r, w), lambda k: (k, 0, 0))],
        out_specs=pl.BlockSpec((None, r, w), lambda k: (k, 0, 0)),
        out_shape=jax.ShapeDtypeStruct(from_sibling.shape, from_sibling.dtype), compiler_params=_params(1),
    )(grad, grad, from_sibling)


def _adamw_update(g, w, m, v):
    new_m = ADAM_B1 * m + (1.0 - ADAM_B1) * g
    new_v = ADAM_B2 * v + (1.0 - ADAM_B2) * (g * g)
    m_hat = new_m / (1.0 - ADAM_B1 ** ADAM_STEP)
    v_hat = new_v / (1.0 - ADAM_B2 ** ADAM_STEP)
    return -ADAM_LR * (m_hat / (jnp.sqrt(v_hat) + ADAM_EPS) + ADAM_WD * w), new_m, new_v


def sum_parts(parts, name):
    _, r, c = parts.shape

    def body(p_ref, o_ref):
        total = p_ref[0].astype(F32)
        for j in range(1, N_DEV):
            total = total + p_ref[j].astype(F32)
        o_ref[...] = total

    return pl.pallas_call(body, name=name, in_specs=[_full(parts.shape)], out_specs=_full((r, c)), grid=(1,),
                          out_shape=jax.ShapeDtypeStruct((r, c), F32), compiler_params=_params(1))(parts)


def adamw(parts, w, m, v, name, exchange=None):
    n, r, c = parts.shape
    tr = r
    while tr * c * 4 > ADAMW_BLOCK_BYTES and tr % (2 * SUBLANES) == 0:
        tr //= 2

    def body(p_ref, w_ref, m_ref, v_ref, g_ref, d_ref, nm_ref, nv_ref):
        g = p_ref[0].astype(F32)
        for j in range(1, n):
            g = g + p_ref[j].astype(F32)
        g_ref[...] = g
        d_ref[...], nm_ref[...], nv_ref[...] = _adamw_update(g, w_ref[...], m_ref[...], v_ref[...])

    row = pl.BlockSpec((tr, c), lambda i: (i, 0))
    return _call(
        body, name=name, grid=(r // tr,),
        in_specs=[pl.BlockSpec((n, tr, c), lambda i: (0, i, 0)), row, row, row], out_specs=[row] * 4,
        out_shape=[jax.ShapeDtypeStruct((r, c), F32)] * 4, args=[parts, w, m, v], exchange=exchange)


def adamw_transposed(parts, w_t, m_t, v_t, name, exchange=None):
    n, r, c = parts.shape
    tr = min(MM_TILE, r)

    def body(p_ref, w_ref, m_ref, v_ref, g_ref, d_ref, nm_ref, nv_ref):
        eye = (lax.broadcasted_iota(jnp.int32, (tr, tr), 0) == lax.broadcasted_iota(jnp.int32, (tr, tr), 1)).astype(BF16)
        g = _bdot_tn(p_ref[0], eye)
        for j in range(1, n):
            g = g + _bdot_tn(p_ref[j], eye)
        g_ref[:, 0, :] = g
        d_ref[:, 0, :], nm_ref[:, 0, :], nv_ref[:, 0, :] = _adamw_update(g, w_ref[:, 0, :], m_ref[:, 0, :], v_ref[:, 0, :])

    col = pl.BlockSpec((c, 1, tr), lambda i: (0, 0, i))
    return _call(
        body, name=name, grid=(r // tr,), in_specs=[pl.BlockSpec((n, tr, c), lambda i: (0, i, 0)), col, col, col], out_specs=[col] * 4,
        out_shape=[jax.ShapeDtypeStruct((c, 1, r), F32)] * 4, args=[parts, w_t, m_t, v_t], exchange=exchange)


def _small_views(shape):
    if len(shape) == 2:
        return [((slice(None), slice(None)), (slice(None), slice(None)))]
    if len(shape) == 3:
        return [((slice(None), slice(None)), (0,))]
    rows = shape[2]
    return [((slice(k * rows, (k + 1) * rows), slice(None)), (0, k)) for k in range(shape[1])]


def adamw_small(landings, w, m, v):
    names = list(landings)
    n = len(names)
    shapes = [w[name].shape for name in names]

    def body(*refs):
        land, ws, ms, vs = refs[:n], refs[n:2 * n], refs[2 * n:3 * n], refs[3 * n:4 * n]
        outs = [refs[(4 + k) * n:(5 + k) * n] for k in range(4)]
        for k in range(n):
            total = land[k][0]
            for j in range(1, N_DEV):
                total = total + land[k][j]
            for rows, at in _small_views(shapes[k]):
                g = total[rows]
                outs[0][k][at] = g
                outs[1][k][at], outs[2][k][at], outs[3][k][at] = _adamw_update(g, ws[k][at], ms[k][at], vs[k][at])

    blocks = [_full(sh) for sh in shapes]
    outs = pl.pallas_call(
        body, name="adamw_small", grid=(1,),
        in_specs=[_full(landings[name].shape) for name in names] + blocks * 3, out_specs=blocks * 4,
        out_shape=[jax.ShapeDtypeStruct(sh, F32) for sh in shapes] * 4, compiler_params=_params(1),
    )(*[landings[name] for name in names], *[src[name] for src in (w, m, v) for name in names])
    return [dict(zip(names, outs[k * n:(k + 1) * n])) for k in range(4)]


def adamw_replicated(land_vec, land_gate_b, land_loss, names, w, m, v, gate_b):
    n = len(names)

    def body(*refs):
        vec_ref, gb_ref, loss_ref = refs[:3]
        ws, ms, vs = refs[3:3 + n], refs[3 + n:3 + 2 * n], refs[3 + 2 * n:3 + 3 * n]
        gw_ref, gm_ref, gv_ref = refs[3 + 3 * n:6 + 3 * n]
        outs = refs[6 + 3 * n:]
        vec, gb, loss = vec_ref[0], gb_ref[0], loss_ref[0]
        for j in range(1, N_DEV):
            vec, gb, loss = vec + vec_ref[j], gb + gb_ref[j], loss + loss_ref[j]
        for k in range(n):
            g = vec[k:k + 1, :]
            outs[k][...] = g
            outs[n + k][...], outs[2 * n + k][...], outs[3 * n + k][...] = _adamw_update(g, ws[k][...], ms[k][...], vs[k][...])
        outs[4 * n][...] = gb
        outs[4 * n + 1][...], outs[4 * n + 2][...], outs[4 * n + 3][...] = _adamw_update(gb, gw_ref[...], gm_ref[...], gv_ref[...])
        outs[4 * n + 4][...] = loss

    vec_block, gb_block = _full((1, D_MODEL)), _full(gate_b[0].shape)
    outs = pl.pallas_call(
        body, name="adamw_replicated", grid=(1,),
        in_specs=[_full(land_vec.shape), _full(land_gate_b.shape), _full(land_loss.shape)] + [vec_block] * (3 * n) + [gb_block] * 3,
        out_specs=[vec_block] * (4 * n) + [gb_block] * 4 + [_full(land_loss.shape[1:])],
        out_shape=[jax.ShapeDtypeStruct((1, D_MODEL), F32)] * (4 * n) + [jax.ShapeDtypeStruct(gate_b[0].shape, F32)] * 4
        + [jax.ShapeDtypeStruct(land_loss.shape[1:], F32)],
        compiler_params=_params(1),
    )(land_vec, land_gate_b, land_loss, *[src[name] for src in (w, m, v) for name in names], *gate_b)
    results = {name: [outs[k * n + i] for k in range(4)] for i, name in enumerate(names)}
    return results, outs[4 * n:4 * n + 4], outs[4 * n + 4]


SMALL_SHARDED = ("rg_conv_w", "rg_lambda", "sc_conv_w", "odd_norm_pre", "odd_norm_post", "gla_b_gate", "gla_norm_g", "gla_w_gate_lr")
SMALL_ROWS = {"rg_conv_w": (0, 4), "rg_lambda": (4, 2), "sc_conv_w": (6, 3), "odd_norm_pre": (9, 1), "odd_norm_post": (10, 1),
              "gla_b_gate": (11, 2), "gla_norm_g": (13, 1), "gla_w_gate_lr": (16, 32)}


def _pack_small(shards):
    pieces, at = [], 0
    for name in SMALL_SHARDED:
        start, rows = SMALL_ROWS[name]
        if start > at:
            pieces.append(jnp.zeros((start - at, LANES), F32))
        a = shards[name].reshape(rows, -1)
        pieces.append(jnp.pad(a, ((0, 0), (0, LANES - a.shape[1]))))
        at = start + rows
    return jnp.concatenate(pieces, axis=0)


def _unpack_gathered(g):
    def cols(name, width):
        start, rows = SMALL_ROWS[name]
        return jnp.transpose(g[:, start:start + rows, :width], (1, 0, 2)).reshape(rows, N_DEV * width)

    w_lr = cols("gla_w_gate_lr", GLA_KEY // N_DEV).reshape(2, GLA_RANK, GLA_KEY)
    return dict(rg_conv_w=cols("rg_conv_w", LANES), rg_lambda=cols("rg_lambda", LANES), sc_conv_w=cols("sc_conv_w", LANES),
                odd_norm_pre=cols("odd_norm_pre", LANES), odd_norm_post=cols("odd_norm_post", LANES),
                gla_b_gate=cols("gla_b_gate", GLA_KEY // N_DEV), gla_norm_g=cols("gla_norm_g", GLA_DV // N_DEV), gla_w_gate_lr=w_lr)


def _blocks_along_columns(a, rows):
    return jnp.transpose(a.reshape(rows, N_DEV, -1), (1, 0, 2))


def kernel(x, even_norm_pre, even_norm_post, even_w_in, rg_conv_w, rg_conv_b, rg_gate_w, rg_gate_b, rg_lambda, sc_conv_w, even_w_out, odd_norm_pre, odd_norm_post, odd_w_in, gla_w_gate_lr, gla_b_gate, gla_norm_g, odd_w_out, loss_target, m_even_norm_pre, m_even_norm_post, m_even_w_in, m_rg_conv_w, m_rg_conv_b, m_rg_gate_w, m_rg_gate_b, m_rg_lambda, m_sc_conv_w, m_even_w_out, m_odd_norm_pre, m_odd_norm_post, m_odd_w_in, m_gla_w_gate_lr, m_gla_b_gate, m_gla_norm_g, m_odd_w_out, v_even_norm_pre, v_even_norm_post, v_even_w_in, v_rg_conv_w, v_rg_conv_b, v_rg_gate_w, v_rg_gate_b, v_rg_lambda, v_sc_conv_w, v_even_w_out, v_odd_norm_pre, v_odd_norm_post, v_odd_w_in, v_gla_w_gate_lr, v_gla_b_gate, v_gla_norm_g, v_odd_w_out):
    weights = dict(even_norm_pre=even_norm_pre, even_norm_post=even_norm_post, even_w_in=even_w_in, rg_conv_w=rg_conv_w,
                   rg_conv_b=rg_conv_b, rg_gate_w=rg_gate_w, rg_gate_b=rg_gate_b, rg_lambda=rg_lambda, sc_conv_w=sc_conv_w,
                   even_w_out=even_w_out, odd_norm_pre=odd_norm_pre, odd_norm_post=odd_norm_post, odd_w_in=odd_w_in,
                   gla_w_gate_lr=gla_w_gate_lr, gla_b_gate=gla_b_gate, gla_norm_g=gla_norm_g, odd_w_out=odd_w_out)
    m_in = dict(even_norm_pre=m_even_norm_pre, even_norm_post=m_even_norm_post, even_w_in=m_even_w_in, rg_conv_w=m_rg_conv_w,
                rg_conv_b=m_rg_conv_b, rg_gate_w=m_rg_gate_w, rg_gate_b=m_rg_gate_b, rg_lambda=m_rg_lambda, sc_conv_w=m_sc_conv_w,
                even_w_out=m_even_w_out, odd_norm_pre=m_odd_norm_pre, odd_norm_post=m_odd_norm_post, odd_w_in=m_odd_w_in,
                gla_w_gate_lr=m_gla_w_gate_lr, gla_b_gate=m_gla_b_gate, gla_norm_g=m_gla_norm_g, odd_w_out=m_odd_w_out)
    v_in = dict(even_norm_pre=v_even_norm_pre, even_norm_post=v_even_norm_post, even_w_in=v_even_w_in, rg_conv_w=v_rg_conv_w,
                rg_conv_b=v_rg_conv_b, rg_gate_w=v_rg_gate_w, rg_gate_b=v_rg_gate_b, rg_lambda=v_rg_lambda, sc_conv_w=v_sc_conv_w,
                even_w_out=v_even_w_out, odd_norm_pre=v_odd_norm_pre, odd_norm_post=v_odd_norm_post, odd_w_in=v_odd_w_in,
                gla_w_gate_lr=v_gla_w_gate_lr, gla_b_gate=v_gla_b_gate, gla_norm_g=v_gla_norm_g, odd_w_out=v_odd_w_out)
    names = list(weights)
    shapes = {n: weights[n].shape for n in names}
    xs = x[0]
    tgt = loss_target[0]

    proj_e, h_e, w_in_e, small_all = gather_matmul(xs, even_norm_pre, even_w_in[0].astype(BF16),
                                                   _pack_small({n: weights[n][0] for n in SMALL_SHARDED}), 2 * MM_TILE)
    small = _unpack_gathered(small_all)
    gate_w = rg_gate_w[0].reshape(4, RG_HEADS, RG_HEAD_DIM, RG_HEAD_DIM).astype(BF16)
    gate_b = rg_gate_b[0].reshape(4, RG_HEADS, RG_HEAD_DIM)
    conv_b = rg_conv_b
    wg_pad = [jnp.pad(small["gla_w_gate_lr"][d], ((GLA_RANK * d, LANES - GLA_RANK * (d + 1)), (0, 0))).astype(BF16) for d in range(2)]
    bg = [small["gla_b_gate"][d:d + 1] for d in range(2)]
    gnorm = jnp.tile(small["gla_norm_g"], (1, GLA_HEADS))

    half = D_MODEL // 2
    behind_gates = Exchange()
    behind_gates.gather(even_w_out[0].astype(BF16), via_sibling=True)
    behind_gates.gather(odd_w_in[0, :half].astype(BF16), via_sibling=True)
    (ab, hf), (w_out_e, w_in_o_top) = even_gates_fwd(proj_e, small["rg_conv_w"], conv_b, gate_w, gate_b, small["rg_lambda"],
                                                     exchange=behind_gates)
    w_out_e = w_out_e.reshape(2 * D_MODEL, D_MODEL)
    behind_mix_fwd = Exchange()
    behind_mix_fwd.gather(odd_w_in[0, half:].astype(BF16), via_sibling=True)
    behind_mix_fwd.gather(odd_w_out[0].astype(BF16), via_sibling=True)
    (u_e, hb, y_e, x1), (w_in_o_bottom, w_out_o) = even_mix_fwd(ab, hf, proj_e, small["sc_conv_w"], w_out_e, xs, even_norm_post,
                                                                exchange=behind_mix_fwd)
    w_out_o = w_out_o.reshape(D_MODEL, D_MODEL)
    w_in_o = columns_from_blocks([w_in_o_top, w_in_o_bottom], ODD_IN_PAD)

    proj_o, h_o = rms_matmul(x1, small["odd_norm_pre"], w_in_o, MM_TILE, ODD_IN_PAD, "odd_in")
    o_f, st_f = gla_fwd(proj_o, wg_pad[0], bg[0], False)
    osum, u_o, st_b, y_o, dout, loss_part = gla_fwd(proj_o, wg_pad[1], bg[1], True, o_other=o_f, gnorm=gnorm,
                                                    post=(w_out_o, x1, small["odd_norm_post"], tgt))

    do, dr, dy_o, d_odd_norm_post, d_gnorm = normbwd_matmul_nt(y_o, small["odd_norm_post"], dout, w_out_o, D_MODEL, "odd_out_bwd",
                                                               gla=(proj_o, osum, gnorm))
    d_w_out_o = matmul_tn(u_o, dy_o, D_MODEL, D_MODEL, 4 * MM_TILE, BF16, "odd_w_out_grad")
    dqkv_f, dlr_f, dwg_f, dbg_f = gla_bwd(proj_o, wg_pad[0], bg[0], do, st_f, False)
    dproj_o, dwg_b, dbg_b = gla_bwd(proj_o, wg_pad[1], bg[1], do, st_b, True, first=(dqkv_f, dlr_f, dr))
    dx1, d_odd_norm_pre = matmul_nt_normbwd(dproj_o, w_in_o, x1, small["odd_norm_pre"], dout, MM_TILE, ODD_IN_PAD, "odd_in_bwd")
    d_w_in_o = matmul_tn(h_o, dproj_o, D_MODEL, ODD_IN_PAD // 5, 8 * MM_TILE, BF16, "odd_w_in_grad")

    landed = {}
    behind_out = Exchange()
    behind_out.scatter(d_w_out_o.reshape(N_DEV, D_MODEL // N_DEV, D_MODEL))
    behind_out.scatter(d_odd_norm_pre, columns=True)
    behind_out.scatter(d_odd_norm_post, columns=True)
    behind_out.scatter(_blocks_along_columns(jnp.concatenate([dbg_f, dbg_b], axis=0), 2))
    behind_out.scatter(_blocks_along_columns(d_gnorm, 1))
    behind_out.scatter(_blocks_along_columns(jnp.concatenate([dwg_f[:GLA_RANK], dwg_b[GLA_RANK:2 * GLA_RANK]], axis=0), 2 * GLA_RANK))
    (du_e, dy_e, d_even_norm_post), got = normbwd_matmul_nt(y_e, even_norm_post, dx1, w_out_e, 2 * D_MODEL, "even_out_bwd",
                                                           exchange=behind_out)
    p_w_out_o = got[0]
    for n, part in zip(("odd_norm_pre", "odd_norm_post", "gla_b_gate", "gla_norm_g", "gla_w_gate_lr"), got[1:]):
        landed[n] = part
    d_w_out_e = matmul_tn(u_e, dy_e, D_MODEL, D_MODEL, 4 * MM_TILE, BF16, "even_w_out_grad")
    behind_mix = Exchange()
    behind_mix.scatter(d_w_out_e.reshape(N_DEV, 2 * D_MODEL // N_DEV, D_MODEL))
    (dh, drest, d_sc_w, adj_b), (p_w_out_e,) = even_mix_bwd(du_e, hf, hb, proj_e, small["sc_conv_w"], ab, exchange=behind_mix)
    adj_f = linear_scan(ab, 0, dh.reshape(1, *dh.shape), 0, True, True, "scan_fwd_adjoint")
    behind_gates_bwd = Exchange()
    behind_gates_bwd.scatter(column_blocks(d_w_in_o, ODD_SHARD))
    behind_gates_bwd.scatter(d_sc_w, columns=True)
    (dua, d_gate_w, d_gate_b, d_lam), (p_w_in_o, landed["sc_conv_w"]) = even_gates_bwd(
        proj_e, adj_f, adj_b, hf, hb, dh, small["rg_conv_w"], conv_b, gate_w, gate_b, small["rg_lambda"], exchange=behind_gates_bwd)
    gate_w_rows = 4 * RG_HEADS * RG_HEAD_DIM
    behind_conv = Exchange()
    behind_conv.scatter(d_gate_w.reshape(N_DEV, gate_w_rows // N_DEV, RG_HEAD_DIM))
    behind_conv.scatter(d_lam, columns=True)
    (dxa, d_conv_w, d_conv_b), (p_gate_w, landed["rg_lambda"]) = rg_conv_bwd(dua, proj_e, small["rg_conv_w"], exchange=behind_conv)
    behind_w_grad = Exchange()
    behind_w_grad.gather(sum_parts(p_gate_w, "sum_gate_w"))
    behind_w_grad.gather(jnp.concatenate([d_even_norm_post, d_conv_b], axis=0))
    behind_w_grad.gather(d_gate_b.reshape(4 * RG_HEADS, RG_HEAD_DIM))
    behind_w_grad.gather(loss_part)
    d_w_in_e, (g_gate_w_all, land_early, land_gate_b, land_loss) = matmul_tn(
        h_e, drest, D_MODEL, D_MODEL, 4 * MM_TILE, BF16, "even_w_in_grad", exchange=behind_w_grad, b_first=dxa)
    to_sibling = Exchange()
    to_sibling.to_sibling(d_w_in_e)
    to_sibling.scatter(d_conv_w, columns=True)
    from_sibling, landed["rg_conv_w"] = run_exchange(to_sibling, "scatter_to_sibling")
    behind_in_bwd = Exchange()
    behind_in_bwd.among_chips(pair_sum(d_w_in_e, from_sibling))
    (grad_x, d_even_norm_pre), (p_w_in_e,) = matmul_nt_normbwd(
        drest, w_in_e, xs, even_norm_pre, dx1, 2 * MM_TILE, D_MODEL, "even_in_bwd", exchange=behind_in_bwd, first=dxa)
    last = Exchange()
    replicated_vecs = ("even_norm_pre", "even_norm_post", "rg_conv_b")
    last.gather(d_even_norm_pre)

    results = {}

    def update(name, parts_, shape2d, exchange=None):
        outs = adamw(parts_, weights[name][0].reshape(shape2d), m_in[name][0].reshape(shape2d), v_in[name][0].reshape(shape2d),
                     "adamw_" + name, exchange=exchange)
        if exchange is not None:
            outs, gathered = outs
        results[name] = [o.reshape(shapes[name]) for o in outs]
        return gathered if exchange is not None else None

    update("even_w_in", p_w_in_e, (D_MODEL, EVEN_SHARD))
    update("even_w_out", p_w_out_e, (2 * D_MODEL // N_DEV, D_MODEL))
    odd_w_in_out, (land_pre,) = adamw_transposed(
        p_w_in_o, *[jnp.transpose(src["odd_w_in"], (2, 0, 1)) for src in (weights, m_in, v_in)], "adamw_odd_w_in", exchange=last)
    land_vec = jnp.concatenate([land_pre, land_early], axis=1)
    results["odd_w_in"] = [jnp.transpose(o, (1, 2, 0)) for o in odd_w_in_out]
    update("odd_w_out", p_w_out_o, (D_MODEL // N_DEV, D_MODEL))
    update("rg_gate_w", g_gate_w_all.reshape(1, gate_w_rows, RG_HEAD_DIM), (gate_w_rows, RG_HEAD_DIM))
    small_out = adamw_small({n: landed[n] for n in SMALL_SHARDED}, weights, m_in, v_in)
    for n in SMALL_SHARDED:
        results[n] = [o[n] for o in small_out]
    gate_b_shape = (4 * RG_HEADS, RG_HEAD_DIM)
    rep_out, gate_b_out, loss_all = adamw_replicated(land_vec, land_gate_b, land_loss, replicated_vecs, weights, m_in, v_in,
                                                     [src["rg_gate_b"].reshape(gate_b_shape) for src in (weights, m_in, v_in)])
    results.update(rep_out)
    results["rg_gate_b"] = [o.reshape(shapes["rg_gate_b"]) for o in gate_b_out]

    return (loss_all[0, 0], grad_x.reshape(x.shape), *[results[n][0] for n in names], *[results[n][1] for n in names],
            *[results[n][2] for n in names], *[results[n][3] for n in names])
```
